```python
import math
import jax, jax.numpy as jnp
from jax import lax
import numpy as np

D_MODEL = 1024
BATCH = 16
SEQ = 2048
DEPTH = 1

MIX_WIDTH = D_MODEL
ATT_WIDTH = MIX_WIDTH // 2
HG_WIDTH = MIX_WIDTH - ATT_WIDTH
ATT_HEAD_DIM = 64
ATT_Q_HEADS = ATT_WIDTH // ATT_HEAD_DIM
ATT_KV_HEADS = 2
ATT_KV_COLS = ATT_KV_HEADS * ATT_HEAD_DIM
WINDOW = 128
ROPE_DIM = ATT_HEAD_DIM // 4
ROPE_THETA = 500000.0
HG_HEAD_DIM = 128
HG_HEADS = HG_WIDTH // HG_HEAD_DIM
HG_CHUNK = 32
IN_COLS = ATT_WIDTH + 2 * ATT_KV_COLS + 4 * HG_WIDTH
_SPLITS = list(np.cumsum([ATT_WIDTH, ATT_KV_COLS, ATT_KV_COLS, HG_WIDTH, HG_WIDTH, HG_WIDTH])[:].tolist())
D_FF = 4 * D_MODEL
N_MOD = 6
EPS = 1e-6

kernel_name = "hybrid_swa_sink_hgrn2_adaln_layer"


def rmsnorm(x, w):
    xf = x.astype(jnp.float32)
    y = xf * lax.rsqrt(jnp.mean(xf * xf, axis=-1, keepdims=True) + EPS)
    return (y * w.astype(jnp.float32)).astype(x.dtype)


def partial_rope(x):
    T = x.shape[1]
    half = ROPE_DIM // 2
    inv_freq = ROPE_THETA ** (-jnp.arange(0, ROPE_DIM, 2, dtype=jnp.float32) / ROPE_DIM)
    ang = jnp.arange(T, dtype=jnp.float32)[:, None] * inv_freq[None, :]
    cos = jnp.cos(ang)[None, :, None, :].astype(x.dtype)
    sin = jnp.sin(ang)[None, :, None, :].astype(x.dtype)
    x1, x2, rest = x[..., :half], x[..., half:ROPE_DIM], x[..., ROPE_DIM:]
    return jnp.concatenate([x1 * cos - x2 * sin, x2 * cos + x1 * sin, rest], axis=-1)


def sliding_window_sink_attention(q, k, v, sinks):
    B, T, Hq, D = q.shape
    nb = T // WINDOW
    G = Hq // ATT_KV_HEADS
    qb = q.reshape(B, nb, WINDOW, ATT_KV_HEADS, G, D)

    def band(a):
        ab = a.reshape(B, nb, WINDOW, ATT_KV_HEADS, D)
        prev = jnp.pad(ab, ((0, 0), (1, 0), (0, 0), (0, 0), (0, 0)))[:, :-1]
        return jnp.concatenate([prev, ab], axis=2)

    kk, vv = band(k), band(v)
    s = jnp.einsum('bnqhgd,bnkhd->bnhgqk', qb, kk).astype(jnp.float32) * (D ** -0.5)
    blk = jnp.arange(nb)[:, None]
    q_pos = blk * WINDOW + jnp.arange(WINDOW)[None, :]
    k_pos = (blk - 1) * WINDOW + jnp.arange(2 * WINDOW)[None, :]
    diff = q_pos[:, :, None] - k_pos[:, None, :]
    mask = (diff >= 0) & (diff < WINDOW) & (k_pos[:, None, :] >= 0)
    s = jnp.where(mask[None, :, None, None], s, jnp.finfo(jnp.float32).min)
    sink = sinks.astype(jnp.float32).reshape(ATT_KV_HEADS, G)[None, None, :, :, None, None]
    m = jnp.maximum(jnp.max(s, axis=-1, keepdims=True), sink)
    p = jnp.exp(s - m)
    p = p / (jnp.sum(p, axis=-1, keepdims=True) + jnp.exp(sink - m))
    o = jnp.einsum('bnhgqk,bnkhd->bnqhgd', p.astype(v.dtype), vv)
    return o.reshape(B, T, Hq * D)


def hgrn2_chunkwise(q, k, v, log_f):
    B, T, H, Dk = q.shape
    Dv = v.shape[-1]
    nc = T // HG_CHUNK

    def to_chunks(a):
        return a.astype(jnp.float32).reshape(B, nc, HG_CHUNK, H, a.shape[-1]).transpose(1, 0, 3, 2, 4)

    qc, kc, vc, gc = to_chunks(q), to_chunks(k), to_chunks(v), to_chunks(log_f)
    bc = jnp.cumsum(gc, axis=3)
    tri = jnp.tril(jnp.ones((HG_CHUNK, HG_CHUNK), dtype=bool))

    def step(S, inp):
        q_, k_, v_, b_ = inp
        b_last = b_[:, :, -1:, :]
        q_dec = q_ * jnp.exp(b_)
        k_dec = k_ * jnp.exp(-b_)
        a = jnp.where(tri, jnp.einsum('bhtk,bhsk->bhts', q_dec, k_dec), 0.0)
        o = jnp.einsum('bhts,bhsv->bhtv', a, v_) + jnp.einsum('bhtk,bhkv->bhtv', q_dec, S)
        S = S * jnp.exp(b_last[:, :, 0, :])[..., None] + \
            jnp.einsum('bhsk,bhsv->bhkv', k_ * jnp.exp(b_last - b_), v_)
        return S, o

    S0 = jnp.zeros((B, H, Dk, Dv), jnp.float32)
    _, o = lax.scan(step, S0, (qc, kc, vc, bc))
    return o.transpose(1, 0, 3, 2, 4).reshape(B, T, H, Dv)


def _fwd_setup_inputs(seed: int = 0) -> dict:
    key = jax.random.key(seed)
    ks = jax.random.split(key, 17)
    f32 = jnp.float32

    def gain(k, shape):
        return (1.0 + 0.02 * jax.random.normal(k, shape)).astype(f32)

    return {
        "x": jax.random.normal(ks[0], (BATCH, SEQ, D_MODEL), f32),
        "c": jax.random.normal(ks[1], (BATCH, D_MODEL), f32),
        "w_ada": jax.random.normal(ks[2], (DEPTH, D_MODEL, N_MOD * D_MODEL), f32) * (0.5 * D_MODEL ** -0.5),
        "b_ada": jax.random.normal(ks[3], (DEPTH, N_MOD * D_MODEL), f32) * 0.02,
        "pre_w_mix": gain(ks[4], (DEPTH, D_MODEL)),
        "w_in": jax.random.normal(ks[5], (DEPTH, D_MODEL, IN_COLS), f32) * D_MODEL ** -0.5,
        "attn_sinks": jax.random.normal(ks[6], (DEPTH, ATT_Q_HEADS), f32) * 0.5,
        "attn_out_w": gain(ks[7], (DEPTH, ATT_WIDTH)),
        "lb_table": jax.random.normal(ks[8], (DEPTH + 1, HG_WIDTH), f32) * 0.1,
        "hg_norm_w": gain(ks[9], (DEPTH, HG_HEAD_DIM)),
        "w_out": jax.random.normal(ks[10], (DEPTH, MIX_WIDTH, D_MODEL), f32) * MIX_WIDTH ** -0.5,
        "post_w_mix": gain(ks[11], (DEPTH, D_MODEL)),
        "pre_w_mlp": gain(ks[12], (DEPTH, D_MODEL)),
        "w_up": jax.random.normal(ks[13], (DEPTH, D_MODEL, D_FF), f32) * D_MODEL ** -0.5,
        "w_down": jax.random.normal(ks[14], (DEPTH, D_FF, D_MODEL), f32) * D_FF ** -0.5,
        "post_w_mlp": gain(ks[15], (DEPTH, D_MODEL)),
    }


def _fwd_reference(x, c, w_ada, b_ada, pre_w_mix, w_in, attn_sinks, attn_out_w, lb_table,
              hg_norm_w, w_out, post_w_mix, pre_w_mlp, w_up, w_down, post_w_mlp):
    B, T, _ = x.shape
    lb_p = jax.nn.softmax(lb_table.astype(jnp.float32), axis=0)
    lower_bounds = jnp.cumsum(lb_p, axis=0) - lb_p[0:1]
    c_act = jax.nn.silu(c)

    for l in range(DEPTH):
        mod = c_act @ w_ada[l] + b_ada[l]
        sh1, sc1, g1, sh2, sc2, g2 = [m[:, None, :] for m in jnp.split(mod, N_MOD, axis=-1)]

        h = rmsnorm(x, pre_w_mix[l]) * (1.0 + sc1) + sh1
        proj = h @ w_in[l]
        aq, ak, av, hq, hf, hi, hg = jnp.split(proj, _SPLITS, axis=-1)

        aq = partial_rope(aq.reshape(B, T, ATT_Q_HEADS, ATT_HEAD_DIM))
        ak = partial_rope(ak.reshape(B, T, ATT_KV_HEADS, ATT_HEAD_DIM))
        av = av.reshape(B, T, ATT_KV_HEADS, ATT_HEAD_DIM)
        attn = sliding_window_sink_attention(aq, ak, av, attn_sinks[l])
        attn = rmsnorm(attn, attn_out_w[l])

        lb = lower_bounds[l + 1].reshape(HG_HEADS, HG_HEAD_DIM)
        f = lb + (1.0 - lb) * jax.nn.sigmoid(hf.reshape(B, T, HG_HEADS, HG_HEAD_DIM).astype(jnp.float32))
        hq4 = jax.nn.silu(hq.reshape(B, T, HG_HEADS, HG_HEAD_DIM))
        hv4 = hi.reshape(B, T, HG_HEADS, HG_HEAD_DIM)
        rec = hgrn2_chunkwise(hq4, 1.0 - f, hv4, jnp.log(f)).astype(x.dtype)
        rec = rmsnorm(rec, hg_norm_w[l]) * jax.nn.silu(hg.reshape(B, T, HG_HEADS, HG_HEAD_DIM))
        rec = rec.reshape(B, T, HG_WIDTH)

        mix = jnp.concatenate([attn, rec], axis=-1) @ w_out[l]
        x = x + g1 * rmsnorm(mix, post_w_mix[l])

        h = rmsnorm(x, pre_w_mlp[l]) * (1.0 + sc2) + sh2
        u = jnp.square(jax.nn.relu(h @ w_up[l]))
        x = x + g2 * rmsnorm(u @ w_down[l], post_w_mlp[l])
    return x


import jax as _jax
import jax.numpy as _jnp

TWIN_FORMAT = 'train_step'
FWD_PARAMS = ['x', 'c', 'w_ada', 'b_ada', 'pre_w_mix', 'w_in', 'attn_sinks', 'attn_out_w', 'lb_table', 'hg_norm_w', 'w_out', 'post_w_mix', 'pre_w_mlp', 'w_up', 'w_down', 'post_w_mlp']
TWIN_WEIGHTS = ['w_ada', 'b_ada', 'pre_w_mix', 'w_in', 'attn_sinks', 'attn_out_w', 'lb_table', 'hg_norm_w', 'w_out', 'post_w_mix', 'pre_w_mlp', 'w_up', 'w_down', 'post_w_mlp']
TWIN_DIFF_INPUT = 'x'
TWIN_INPUTS = ['x', 'c', 'w_ada', 'b_ada', 'pre_w_mix', 'w_in', 'attn_sinks', 'attn_out_w', 'lb_table', 'hg_norm_w', 'w_out', 'post_w_mix', 'pre_w_mlp', 'w_up', 'w_down', 'post_w_mlp', 'loss_target', 'm_w_ada', 'm_b_ada', 'm_pre_w_mix', 'm_w_in', 'm_attn_sinks', 'm_attn_out_w', 'm_lb_table', 'm_hg_norm_w', 'm_w_out', 'm_post_w_mix', 'm_pre_w_mlp', 'm_w_up', 'm_w_down', 'm_post_w_mlp', 'v_w_ada', 'v_b_ada', 'v_pre_w_mix', 'v_w_in', 'v_attn_sinks', 'v_attn_out_w', 'v_lb_table', 'v_hg_norm_w', 'v_w_out', 'v_post_w_mix', 'v_pre_w_mlp', 'v_w_up', 'v_w_down', 'v_post_w_mlp']
TWIN_OUTPUTS = ['loss', 'grad_x', 'grad_w_ada', 'grad_b_ada', 'grad_pre_w_mix', 'grad_w_in', 'grad_attn_sinks', 'grad_attn_out_w', 'grad_lb_table', 'grad_hg_norm_w', 'grad_w_out', 'grad_post_w_mix', 'grad_pre_w_mlp', 'grad_w_up', 'grad_w_down', 'grad_post_w_mlp', 'delta_w_ada', 'delta_b_ada', 'delta_pre_w_mix', 'delta_w_in', 'delta_attn_sinks', 'delta_attn_out_w', 'delta_lb_table', 'delta_hg_norm_w', 'delta_w_out', 'delta_post_w_mix', 'delta_pre_w_mlp', 'delta_w_up', 'delta_w_down', 'delta_post_w_mlp', 'new_m_w_ada', 'new_m_b_ada', 'new_m_pre_w_mix', 'new_m_w_in', 'new_m_attn_sinks', 'new_m_attn_out_w', 'new_m_lb_table', 'new_m_hg_norm_w', 'new_m_w_out', 'new_m_post_w_mix', 'new_m_pre_w_mlp', 'new_m_w_up', 'new_m_w_down', 'new_m_post_w_mlp', 'new_v_w_ada', 'new_v_b_ada', 'new_v_pre_w_mix', 'new_v_w_in', 'new_v_attn_sinks', 'new_v_attn_out_w', 'new_v_lb_table', 'new_v_hg_norm_w', 'new_v_w_out', 'new_v_post_w_mix', 'new_v_pre_w_mlp', 'new_v_w_up', 'new_v_w_down', 'new_v_post_w_mlp']
TWIN_LEAF_KINDS = {'loss': 'loss', 'grad_x': 'grad_x', 'grad_w_ada': 'grad_w', 'grad_b_ada': 'grad_w', 'grad_pre_w_mix': 'grad_w', 'grad_w_in': 'grad_w', 'grad_attn_sinks': 'grad_w', 'grad_attn_out_w': 'grad_w', 'grad_lb_table': 'grad_w', 'grad_hg_norm_w': 'grad_w', 'grad_w_out': 'grad_w', 'grad_post_w_mix': 'grad_w', 'grad_pre_w_mlp': 'grad_w', 'grad_w_up': 'grad_w', 'grad_w_down': 'grad_w', 'grad_post_w_mlp': 'grad_w', 'delta_w_ada': 'delta_w', 'delta_b_ada': 'delta_w', 'delta_pre_w_mix': 'delta_w', 'delta_w_in': 'delta_w', 'delta_attn_sinks': 'delta_w', 'delta_attn_out_w': 'delta_w', 'delta_lb_table': 'delta_w', 'delta_hg_norm_w': 'delta_w', 'delta_w_out': 'delta_w', 'delta_post_w_mix': 'delta_w', 'delta_pre_w_mlp': 'delta_w', 'delta_w_up': 'delta_w', 'delta_w_down': 'delta_w', 'delta_post_w_mlp': 'delta_w', 'new_m_w_ada': 'new_m', 'new_m_b_ada': 'new_m', 'new_m_pre_w_mix': 'new_m', 'new_m_w_in': 'new_m', 'new_m_attn_sinks': 'new_m', 'new_m_attn_out_w': 'new_m', 'new_m_lb_table': 'new_m', 'new_m_hg_norm_w': 'new_m', 'new_m_w_out': 'new_m', 'new_m_post_w_mix': 'new_m', 'new_m_pre_w_mlp': 'new_m', 'new_m_w_up': 'new_m', 'new_m_w_down': 'new_m', 'new_m_post_w_mlp': 'new_m', 'new_v_w_ada': 'new_v', 'new_v_b_ada': 'new_v', 'new_v_pre_w_mix': 'new_v', 'new_v_w_in': 'new_v', 'new_v_attn_sinks': 'new_v', 'new_v_attn_out_w': 'new_v', 'new_v_lb_table': 'new_v', 'new_v_hg_norm_w': 'new_v', 'new_v_w_out': 'new_v', 'new_v_post_w_mix': 'new_v', 'new_v_pre_w_mlp': 'new_v', 'new_v_w_up': 'new_v', 'new_v_w_down': 'new_v', 'new_v_post_w_mlp': 'new_v'}


def _forward(args):
    return _fwd_reference(*[args[k] for k in FWD_PARAMS])


def _output_shape():
    out = _jax.eval_shape(lambda: _forward(_fwd_setup_inputs(0)))
    return out.shape, out.dtype

N_MICROBATCH = 1
ADAM_LR = 0.001
ADAM_B1 = 0.9
ADAM_B2 = 0.999
ADAM_EPS = 1e-08
ADAM_WD = 0.01
ADAM_STEP = 10
PER_EXAMPLE_BATCH_AXIS = {'x': 0, 'c': 0, 'loss_target': 0}
SHARED_INPUTS = []
_WEIGHT_DTYPES = {'w_ada': _jnp.float32, 'b_ada': _jnp.float32, 'pre_w_mix': _jnp.float32, 'w_in': _jnp.float32, 'attn_sinks': _jnp.float32, 'attn_out_w': _jnp.float32, 'lb_table': _jnp.float32, 'hg_norm_w': _jnp.float32, 'w_out': _jnp.float32, 'post_w_mix': _jnp.float32, 'pre_w_mlp': _jnp.float32, 'w_up': _jnp.float32, 'w_down': _jnp.float32, 'post_w_mlp': _jnp.float32}
MOMENT_SCALE = {'w_ada': 2.430596e+00, 'b_ada': 4.353342e+00, 'pre_w_mix': 1.269794e-01, 'w_in': 8.008644e-01, 'attn_sinks': 4.809353e-02, 'attn_out_w': 1.779079e+00, 'lb_table': 7.703094e-03, 'hg_norm_w': 4.761111e-01, 'w_out': 1.362221e+00, 'post_w_mix': 4.610027e+00, 'pre_w_mlp': 2.270372e-01, 'w_up': 1.521867e-01, 'w_down': 6.479562e-01, 'post_w_mlp': 3.922261e+00}


def _to_microbatches(a, axis):
    t = _jnp.moveaxis(a, axis, 0)
    t = t.reshape((N_MICROBATCH, t.shape[0] // N_MICROBATCH) + t.shape[1:])
    return _jnp.moveaxis(t, 1, axis + 1)


def setup_inputs(seed: int = 0) -> dict:
    inp = _fwd_setup_inputs(seed)
    key = _jax.random.fold_in(_jax.random.key(seed), 7919)
    shape, _ = _output_shape()
    out = dict(inp)
    out["loss_target"] = _jax.random.normal(_jax.random.fold_in(key, 0), shape, _jnp.float32)
    for i, name in enumerate(TWIN_WEIGHTS):
        w = inp[name].astype(_jnp.float32)
        if MOMENT_SCALE is None:
            s = _jnp.sqrt(_jnp.mean(_jnp.square(w)) + 1e-30)
        else:
            s = MOMENT_SCALE[name]
        km, kv = _jax.random.split(_jax.random.fold_in(key, i + 1))
        out[name] = w
        out["m_" + name] = s * _jax.random.normal(km, w.shape, _jnp.float32)
        out["v_" + name] = (s * s) * _jax.random.uniform(kv, w.shape, _jnp.float32, 0.5, 1.5)
    if N_MICROBATCH > 1:
        for name, axis in PER_EXAMPLE_BATCH_AXIS.items():
            out[name] = _to_microbatches(out[name], axis)
    return {'x': out['x'], 'c': out['c'], 'w_ada': out['w_ada'], 'b_ada': out['b_ada'], 'pre_w_mix': out['pre_w_mix'], 'w_in': out['w_in'], 'attn_sinks': out['attn_sinks'], 'attn_out_w': out['attn_out_w'], 'lb_table': out['lb_table'], 'hg_norm_w': out['hg_norm_w'], 'w_out': out['w_out'], 'post_w_mix': out['post_w_mix'], 'pre_w_mlp': out['pre_w_mlp'], 'w_up': out['w_up'], 'w_down': out['w_down'], 'post_w_mlp': out['post_w_mlp'], 'loss_target': out['loss_target'], 'm_w_ada': out['m_w_ada'], 'm_b_ada': out['m_b_ada'], 'm_pre_w_mix': out['m_pre_w_mix'], 'm_w_in': out['m_w_in'], 'm_attn_sinks': out['m_attn_sinks'], 'm_attn_out_w': out['m_attn_out_w'], 'm_lb_table': out['m_lb_table'], 'm_hg_norm_w': out['m_hg_norm_w'], 'm_w_out': out['m_w_out'], 'm_post_w_mix': out['m_post_w_mix'], 'm_pre_w_mlp': out['m_pre_w_mlp'], 'm_w_up': out['m_w_up'], 'm_w_down': out['m_w_down'], 'm_post_w_mlp': out['m_post_w_mlp'], 'v_w_ada': out['v_w_ada'], 'v_b_ada': out['v_b_ada'], 'v_pre_w_mix': out['v_pre_w_mix'], 'v_w_in': out['v_w_in'], 'v_attn_sinks': out['v_attn_sinks'], 'v_attn_out_w': out['v_attn_out_w'], 'v_lb_table': out['v_lb_table'], 'v_hg_norm_w': out['v_hg_norm_w'], 'v_w_out': out['v_w_out'], 'v_post_w_mix': out['v_post_w_mix'], 'v_pre_w_mlp': out['v_pre_w_mlp'], 'v_w_up': out['v_w_up'], 'v_w_down': out['v_w_down'], 'v_post_w_mlp': out['v_post_w_mlp']}


def _loss(weights, diff, rest, loss_target):
    with _jax.named_scope("forward"):
        args = {**rest, TWIN_DIFF_INPUT: diff, **{k: w.astype(_WEIGHT_DTYPES[k]) for k, w in weights.items()}}
        y = _forward(args)
    with _jax.named_scope("loss_head"):
        err = _jnp.square(y.astype(_jnp.float32) - loss_target)
        return 0.5 * _jnp.sum(_jnp.mean(err, axis=-1)) if err.ndim else 0.5 * err


def _adamw(w, g, m, v):
    m = ADAM_B1 * m + (1.0 - ADAM_B1) * g
    v = ADAM_B2 * v + (1.0 - ADAM_B2) * _jnp.square(g)
    m_hat = m / (1.0 - ADAM_B1 ** ADAM_STEP)
    v_hat = v / (1.0 - ADAM_B2 ** ADAM_STEP)
    delta = -ADAM_LR * (m_hat / (_jnp.sqrt(v_hat) + ADAM_EPS) + ADAM_WD * w)
    return delta, m, v


def reference(x, c, w_ada, b_ada, pre_w_mix, w_in, attn_sinks, attn_out_w, lb_table, hg_norm_w, w_out, post_w_mix, pre_w_mlp, w_up, w_down, post_w_mlp, loss_target, m_w_ada, m_b_ada, m_pre_w_mix, m_w_in, m_attn_sinks, m_attn_out_w, m_lb_table, m_hg_norm_w, m_w_out, m_post_w_mix, m_pre_w_mlp, m_w_up, m_w_down, m_post_w_mlp, v_w_ada, v_b_ada, v_pre_w_mix, v_w_in, v_attn_sinks, v_attn_out_w, v_lb_table, v_hg_norm_w, v_w_out, v_post_w_mix, v_pre_w_mlp, v_w_up, v_w_down, v_post_w_mlp):
    given = dict(x=x, c=c, w_ada=w_ada, b_ada=b_ada, pre_w_mix=pre_w_mix, w_in=w_in, attn_sinks=attn_sinks, attn_out_w=attn_out_w, lb_table=lb_table, hg_norm_w=hg_norm_w, w_out=w_out, post_w_mix=post_w_mix, pre_w_mlp=pre_w_mlp, w_up=w_up, w_down=w_down, post_w_mlp=post_w_mlp, loss_target=loss_target, m_w_ada=m_w_ada, m_b_ada=m_b_ada, m_pre_w_mix=m_pre_w_mix, m_w_in=m_w_in, m_attn_sinks=m_attn_sinks, m_attn_out_w=m_attn_out_w, m_lb_table=m_lb_table, m_hg_norm_w=m_hg_norm_w, m_w_out=m_w_out, m_post_w_mix=m_post_w_mix, m_pre_w_mlp=m_pre_w_mlp, m_w_up=m_w_up, m_w_down=m_w_down, m_post_w_mlp=m_post_w_mlp, v_w_ada=v_w_ada, v_b_ada=v_b_ada, v_pre_w_mix=v_pre_w_mix, v_w_in=v_w_in, v_attn_sinks=v_attn_sinks, v_attn_out_w=v_attn_out_w, v_lb_table=v_lb_table, v_hg_norm_w=v_hg_norm_w, v_w_out=v_w_out, v_post_w_mix=v_post_w_mix, v_pre_w_mlp=v_pre_w_mlp, v_w_up=v_w_up, v_w_down=v_w_down, v_post_w_mlp=v_post_w_mlp)
    weights = {n: given[n] for n in TWIN_WEIGHTS}
    shared = {n: given[n] for n in SHARED_INPUTS}
    per_example = {n: given[n] for n in ['x', 'c']}
    grad_fn = _jax.value_and_grad(_loss, argnums=(0, 1))

    def one_microbatch(ex, loss_target):
        ex = dict(ex)
        diff = ex.pop(TWIN_DIFF_INPUT)
        return grad_fn(weights, diff, {**shared, **ex}, loss_target)

    if N_MICROBATCH == 1:
        loss, (grad_w, grad_x) = one_microbatch(per_example, given["loss_target"])
    else:
        def body(carry, xs):
            loss_sum, grad_sum = carry
            l_k, (gw_k, gx_k) = one_microbatch(xs[0], xs[1])
            with _jax.named_scope("update"):
                return (loss_sum + l_k, _jax.tree.map(_jnp.add, grad_sum, gw_k)), gx_k

        init = (_jnp.zeros((), _jnp.float32), _jax.tree.map(_jnp.zeros_like, weights))
        (loss, grad_w), grad_x = _jax.lax.scan(body, init, (per_example, given["loss_target"]))
    with _jax.named_scope("update"):
        delta_w, new_m, new_v = {}, {}, {}
        for n in TWIN_WEIGHTS:
            delta_w[n], new_m[n], new_v[n] = _adamw(weights[n], grad_w[n], given["m_" + n], given["v_" + n])
    return (loss, grad_x, *[grad_w[n] for n in TWIN_WEIGHTS], *[delta_w[n] for n in TWIN_WEIGHTS],
            *[new_m[n] for n in TWIN_WEIGHTS], *[new_v[n] for n in TWIN_WEIGHTS])
```

```python
import functools

import jax
import jax.numpy as jnp
from jax import lax
from jax.experimental import pallas as pl
from jax.experimental.pallas import tpu as pltpu

F32 = jnp.float32
BF16 = jnp.bfloat16
SDS = jax.ShapeDtypeStruct

D_MODEL = 1024
ATT_WIDTH = 512
ATT_HEAD_DIM = 64
ATT_KV_HEADS = 2
ATT_GROUP = 4
WINDOW = 128
ROPE_DIM = 16
ROPE_THETA = 500000.0
HG_WIDTH = 512
HG_HEAD_DIM = 128
HG_HEADS = 4
HG_CHUNK = 32
IN_COLS = 2816
D_FF = 4096
EPS = 1e-6
N_DEV = 8

ADAM_LR = 0.001
ADAM_B1 = 0.9
ADAM_B2 = 0.999
ADAM_EPS = 1e-08
ADAM_WD = 0.01
ADAM_STEP = 10

VMEM_LIMIT_BIG = 56 << 20
LANES = 128

MESH = pl.DeviceIdType.MESH
NT_DIMS = (((1,), (1,)), ((), ()))
TN_DIMS = (((0,), (0,)), ((), ()))


def _dot(a, b):
    return jnp.dot(a, b, preferred_element_type=F32)


def _dot_nt(a, b):
    return lax.dot_general(a, b, NT_DIMS, preferred_element_type=F32)


def _dot_tn(a, b):
    return lax.dot_general(a, b, TN_DIMS, preferred_element_type=F32)


def _bf(a):
    return a.astype(BF16)


def _sigmoid(a):
    return 1.0 / (1.0 + jnp.exp(-a))


def _mean_last(a):
    return jnp.mean(a, axis=-1, keepdims=True)


def _sum_rows(a):
    return jnp.sum(a, axis=0, keepdims=True)


def _tri_sum(tri_bf, a):
    a1 = _bf(a)
    r1 = a - a1.astype(F32)
    a2 = _bf(r1)
    a3 = _bf(r1 - a2.astype(F32))
    return _dot(tri_bf, a1) + _dot(tri_bf, a2) + _dot(tri_bf, a3)


def _params(sem=None, vmem=None):
    kw = {}
    if sem is not None:
        kw["dimension_semantics"] = sem
    if vmem is not None:
        kw["vmem_limit_bytes"] = vmem
    return pltpu.CompilerParams(**kw)


def _exchange(name, srcs, modes):
    n = len(srcs)
    out_shape = []
    for s, m in zip(srcs, modes):
        shp = (N_DEV,) + tuple(s.shape) if m == "gather" else tuple(s.shape)
        out_shape.append(SDS(shp, s.dtype))

    def body(*refs):
        src_refs = refs[:n]
        out_refs = refs[n:2 * n]
        send_sems, recv_sems, own_sems = refs[2 * n:]
        x, y, c = lax.axis_index("x"), lax.axis_index("y"), lax.axis_index("c")
        me = 4 * x + 2 * y + c
        own, sends, recvs = [], [], []
        for a in range(n):
            gather = modes[a] == "gather"
            mine = src_refs[a] if gather else src_refs[a].at[me]
            own.append(pltpu.make_async_copy(mine, out_refs[a].at[me], own_sems.at[a]))
            for k in range(1, N_DEV):
                px, py, pc = x ^ ((k >> 2) & 1), y ^ ((k >> 1) & 1), c ^ (k & 1)
                peer = 4 * px + 2 * py + pc
                src = src_refs[a] if gather else src_refs[a].at[peer]
                sends.append(pltpu.make_async_remote_copy(
                    src_ref=src, dst_ref=out_refs[a].at[me],
                    send_sem=send_sems.at[a, k - 1], recv_sem=recv_sems.at[a, k - 1],
                    device_id=(px, py, pc), device_id_type=MESH))
                recvs.append(pltpu.make_async_remote_copy(
                    src_ref=src, dst_ref=out_refs[a].at[peer],
                    send_sem=send_sems.at[a, k - 1], recv_sem=recv_sems.at[a, k - 1],
                    device_id=(px, py, pc), device_id_type=MESH))
        for cp in own + sends:
            cp.start()
        for cp in recvs:
            cp.wait_recv()
        for cp in sends:
            cp.wait_send()
        for cp in own:
            cp.wait()

    any_spec = pl.BlockSpec(memory_space=pl.ANY)
    return pl.pallas_call(
        body, name=name, out_shape=out_shape,
        in_specs=[any_spec] * n, out_specs=[any_spec] * n,
        scratch_shapes=[pltpu.SemaphoreType.DMA((n, N_DEV - 1)),
                        pltpu.SemaphoreType.DMA((n, N_DEV - 1)),
                        pltpu.SemaphoreType.DMA((n,))],
    )(*srcs)


def _ada_mod(c_all, w_ada, b_ada_mine):
    nb, cols = c_all.shape[0], w_ada.shape[1]

    def body(c_ref, w_ref, b_ref, o_ref):
        cv = c_ref[...]
        ca = cv * _sigmoid(cv)
        o_ref[...] = _dot(ca, w_ref[...]) + b_ref[...]

    return pl.pallas_call(body, name="ada_mod", out_shape=SDS((nb, cols), F32))(c_all, w_ada, b_ada_mine)


def _tile_rows(T):
    return min(256, T)


def _mod_spec(tps):
    return pl.BlockSpec((None, 8, D_MODEL), lambda i: (i // tps, 0, 0))


def _in_proj(x2, mod8, pre_w, w_in_bf, T):
    N = x2.shape[0]
    TM = _tile_rows(T)
    tps = T // TM

    def body(x_ref, mod_ref, pw_ref, w_ref, proj_ref, h1_ref):
        x = x_ref[...]
        r = lax.rsqrt(_mean_last(x * x) + EPS)
        h = (x * r * pw_ref[...]) * (1.0 + mod_ref[1:2, :]) + mod_ref[0:1, :]
        hb = _bf(h)
        h1_ref[...] = hb
        proj_ref[...] = _dot(hb, w_ref[...])

    return pl.pallas_call(
        body, name="in_proj", grid=(N // TM,),
        in_specs=[pl.BlockSpec((TM, D_MODEL), lambda i: (i, 0)), _mod_spec(tps),
                  pl.BlockSpec((1, D_MODEL), lambda i: (0, 0)),
                  pl.BlockSpec((D_MODEL, IN_COLS), lambda i: (0, 0))],
        out_specs=[pl.BlockSpec((TM, IN_COLS), lambda i: (i, 0)),
                   pl.BlockSpec((TM, D_MODEL), lambda i: (i, 0))],
        out_shape=[SDS((N, IN_COLS), F32), SDS((N, D_MODEL), BF16)],
        compiler_params=_params(("arbitrary",), VMEM_LIMIT_BIG),
    )(x2, mod8, pre_w, w_in_bf)


def _rope_tables(T):
    half = ROPE_DIM // 2
    inv_freq = ROPE_THETA ** (-jnp.arange(0, ROPE_DIM, 2, dtype=F32) / ROPE_DIM)
    ang = jnp.arange(T, dtype=F32)[:, None] * inv_freq[None, :]
    cos, sin = jnp.cos(ang), jnp.sin(ang)
    ones = jnp.ones((T, ATT_HEAD_DIM - ROPE_DIM), F32)
    zeros = jnp.zeros((T, ATT_HEAD_DIM - ROPE_DIM), F32)
    zh = jnp.zeros((T, half), F32)
    cos64 = jnp.concatenate([cos, cos, ones], axis=1)
    sin_left = jnp.concatenate([-sin, zh, zeros], axis=1)
    sin_right = jnp.concatenate([zh, sin, zeros], axis=1)
    rep = LANES // ATT_HEAD_DIM
    return jnp.tile(cos64, (1, rep)), jnp.tile(sin_left, (1, rep)), jnp.tile(sin_right, (1, rep))


def _rope(xc, cs, sl, sr):
    return xc * cs + pltpu.roll(xc, LANES - 8, 1) * sl + pltpu.roll(xc, 8, 1) * sr


def _rope_t(dy, cs, sl, sr):
    return dy * cs + pltpu.roll(dy * sl, 8, 1) + pltpu.roll(dy * sr, LANES - 8, 1)


def _band_mask(n):
    rows = ATT_GROUP * WINDOW
    i = lax.broadcasted_iota(jnp.int32, (rows, 2 * WINDOW), 0) & (WINDOW - 1)
    j = lax.broadcasted_iota(jnp.int32, (rows, 2 * WINDOW), 1)
    diff = i + WINDOW - j
    return (diff >= 0) & (diff < WINDOW) & ((j >= WINDOW) | (n > 0))


def _sink_col(sink_ref, hk):
    return jnp.concatenate(
        [jnp.full((WINDOW, 1), sink_ref[0, ATT_GROUP * hk + g], F32) for g in range(ATT_GROUP)], axis=0)


def _softmax_band(qs, kk, mask, sink):
    s = _dot_nt(qs, kk) * (ATT_HEAD_DIM ** -0.5)
    s = jnp.where(mask, s, jnp.finfo(F32).min)
    m = jnp.maximum(jnp.max(s, axis=-1, keepdims=True), sink)
    p = jnp.exp(s - m)
    es = jnp.exp(sink - m)
    inv = 1.0 / (jnp.sum(p, axis=-1, keepdims=True) + es)
    return p * inv, es * inv


def _stack_heads(parts, hk):
    hs = []
    for g in range(ATT_GROUP):
        h = ATT_GROUP * hk + g
        hs.append(parts[h // 2][:, (h % 2) * ATT_HEAD_DIM:(h % 2 + 1) * ATT_HEAD_DIM])
    return jnp.concatenate(hs, axis=0)


def _attn_fwd(proj3, tables, sinks, attn_w):
    B, T, _ = proj3.shape
    nb = T // WINDOW
    cos, sinl, sinr = tables

    def body(q_ref, k_ref, v_ref, cos_ref, sl_ref, sr_ref, sink_ref, aw_ref,
             o_ref, an_ref, qr_ref, kr_ref, kpad, vpad):
        kpad[0:WINDOW, :] = jnp.zeros((WINDOW, LANES), BF16)
        vpad[0:WINDOW, :] = jnp.zeros((WINDOW, LANES), BF16)

        def block(n, carry):
            r0 = pl.multiple_of(n * WINDOW, WINDOW)
            rows = pl.ds(r0, WINDOW)
            nxt = pl.ds(r0 + WINDOW, WINDOW)
            band = pl.ds(r0, 2 * WINDOW)
            cs, sl, sr = cos_ref[rows, :], sl_ref[rows, :], sr_ref[rows, :]
            kb = _bf(_rope(k_ref[rows, :], cs, sl, sr))
            kpad[nxt, :] = kb
            kr_ref[rows, :] = kb
            vpad[nxt, :] = _bf(v_ref[rows, :])
            qparts = []
            for j in range(ATT_WIDTH // LANES):
                qp = _bf(_rope(q_ref[rows, j * LANES:(j + 1) * LANES], cs, sl, sr))
                qr_ref[rows, j * LANES:(j + 1) * LANES] = qp
                qparts.append(qp)
            mask = _band_mask(n)
            for hk in range(ATT_KV_HEADS):
                lanes = slice(hk * ATT_HEAD_DIM, (hk + 1) * ATT_HEAD_DIM)
                qs = _stack_heads(qparts, hk)
                p, _ = _softmax_band(qs, kpad[band, lanes], mask, _sink_col(sink_ref, hk))
                o = _dot(_bf(p), vpad[band, lanes])
                for g in range(ATT_GROUP):
                    h = ATT_GROUP * hk + g
                    o_ref[rows, h * ATT_HEAD_DIM:(h + 1) * ATT_HEAD_DIM] = o[g * WINDOW:(g + 1) * WINDOW, :]
            ob = o_ref[rows, :]
            an_ref[rows, :] = _bf(ob * lax.rsqrt(_mean_last(ob * ob) + EPS) * aw_ref[...])
            return carry

        lax.fori_loop(0, nb, block, 0)

    seq = lambda w, j: pl.BlockSpec((None, T, w), lambda b: (b, 0, j))
    full = lambda r, w: pl.BlockSpec((r, w), lambda b: (0, 0))
    return pl.pallas_call(
        body, name="attn_fwd", grid=(B,),
        in_specs=[seq(ATT_WIDTH, 0), seq(LANES, 4), seq(LANES, 5),
                  full(T, LANES), full(T, LANES), full(T, LANES),
                  pl.BlockSpec(memory_space=pltpu.SMEM), full(1, ATT_WIDTH)],
        out_specs=[seq(ATT_WIDTH, 0), seq(ATT_WIDTH, 0), seq(ATT_WIDTH, 0), seq(LANES, 0)],
        out_shape=[SDS((B, T, ATT_WIDTH), F32), SDS((B, T, ATT_WIDTH), BF16),
                   SDS((B, T, ATT_WIDTH), BF16), SDS((B, T, LANES), BF16)],
        scratch_shapes=[pltpu.VMEM((T + WINDOW, LANES), BF16), pltpu.VMEM((T + WINDOW, LANES), BF16)],
        compiler_params=_params(("arbitrary",), VMEM_LIMIT_BIG),
    )(proj3, proj3, proj3, cos, sinl, sinr, sinks, attn_w)


def _hgrn_gates(hq, hf, lb, tri_lo):
    sq = _sigmoid(hq)
    q = hq * sq
    sg = _sigmoid(hf)
    f = lb + (1.0 - lb) * sg
    k = 1.0 - f
    b = _tri_sum(tri_lo, jnp.log(f))
    last = lax.broadcasted_iota(jnp.int32, b.shape, 0) == HG_CHUNK - 1
    bl = _sum_rows(jnp.where(last, b, 0.0))
    eb = jnp.exp(b)
    enb = jnp.exp(-b)
    e2 = jnp.exp(bl - b)
    return dict(sq=sq, q=q, sg=sg, f=f, k=k, eb=eb, enb=enb, e2=e2, ebl=jnp.exp(bl),
                qd=q * eb, kd=k * enb, k2=k * e2)


def _tri(lower):
    r = lax.broadcasted_iota(jnp.int32, (HG_CHUNK, HG_CHUNK), 0)
    c = lax.broadcasted_iota(jnp.int32, (HG_CHUNK, HG_CHUNK), 1)
    return (r >= c) if lower else (c >= r)


def _hgrn_specs(B, T):
    head = lambda base: pl.BlockSpec((None, T, LANES), lambda b, h: (b, 0, base + h))
    return head


def _hgrn_fwd(proj3, lb, hg_w):
    B, T, _ = proj3.shape
    nc = T // HG_CHUNK
    head = _hgrn_specs(B, T)

    def body(hq_ref, hf_ref, hi_ref, hg_ref, lb_ref, gw_ref, o_ref, rg_ref, sp_ref, st):
        st[...] = jnp.zeros((HG_HEAD_DIM, HG_HEAD_DIM), F32)
        tri = _tri(True)
        tri_lo = jnp.where(tri, 1.0, 0.0).astype(BF16)
        lbv = lb_ref[...]

        def chunk(c, carry):
            rows = pl.ds(pl.multiple_of(c * HG_CHUNK, HG_CHUNK), HG_CHUNK)
            gt = _hgrn_gates(hq_ref[rows, :], hf_ref[rows, :], lbv, tri_lo)
            v = _bf(hi_ref[rows, :])
            qd = _bf(gt["qd"])
            a = jnp.where(tri, _dot_nt(qd, _bf(gt["kd"])), 0.0)
            s_prev = st[...]
            sp_ref[c] = s_prev
            o = _dot(_bf(a), v) + _dot_nt(qd, _bf(s_prev))
            st[...] = s_prev * gt["ebl"] + _dot_tn(v, _bf(gt["k2"]))
            o_ref[rows, :] = o
            hg = hg_ref[rows, :]
            rn = o * lax.rsqrt(_mean_last(o * o) + EPS) * gw_ref[...]
            rg_ref[rows, :] = _bf(rn * (hg * _sigmoid(hg)))
            return carry

        lax.fori_loop(0, nc, chunk, 0)

    out_head = pl.BlockSpec((None, T, LANES), lambda b, h: (b, 0, h))
    return pl.pallas_call(
        body, name="hgrn_fwd", grid=(B, HG_HEADS),
        in_specs=[head(6), head(10), head(14), head(18),
                  pl.BlockSpec((1, LANES), lambda b, h: (0, h)),
                  pl.BlockSpec((1, LANES), lambda b, h: (0, 0))],
        out_specs=[out_head, out_head,
                   pl.BlockSpec((None, None, nc, HG_HEAD_DIM, HG_HEAD_DIM), lambda b, h: (b, h, 0, 0, 0))],
        out_shape=[SDS((B, T, HG_WIDTH), F32), SDS((B, T, HG_WIDTH), BF16),
                   SDS((B, HG_HEADS, nc, HG_HEAD_DIM, HG_HEAD_DIM), F32)],
        scratch_shapes=[pltpu.VMEM((HG_HEAD_DIM, HG_HEAD_DIM), F32)],
        compiler_params=_params(("arbitrary", "arbitrary"), VMEM_LIMIT_BIG),
    )(proj3, proj3, proj3, proj3, lb, hg_w)


def _mix_out(x2, attn_n, rec_g, mod8, post_w, w_out_bf, T):
    N = x2.shape[0]
    TM = _tile_rows(T)
    tps = T // TM

    def body(x_ref, an_ref, rg_ref, mod_ref, pw_ref, w_ref, mix_ref, x1_ref, cat_ref):
        cat = jnp.concatenate([an_ref[...], rg_ref[...]], axis=1)
        cat_ref[...] = cat
        mix = _dot(cat, w_ref[...])
        mix_ref[...] = mix
        r = lax.rsqrt(_mean_last(mix * mix) + EPS)
        x1_ref[...] = x_ref[...] + mod_ref[2:3, :] * (mix * r * pw_ref[...])

    row = lambda w: pl.BlockSpec((TM, w), lambda i: (i, 0))
    return pl.pallas_call(
        body, name="mix_out", grid=(N // TM,),
        in_specs=[row(D_MODEL), row(ATT_WIDTH), row(HG_WIDTH), _mod_spec(tps),
                  pl.BlockSpec((1, D_MODEL), lambda i: (0, 0)),
                  pl.BlockSpec((D_MODEL, D_MODEL), lambda i: (0, 0))],
        out_specs=[row(D_MODEL), row(D_MODEL), row(D_MODEL)],
        out_shape=[SDS((N, D_MODEL), F32), SDS((N, D_MODEL), F32), SDS((N, D_MODEL), BF16)],
        compiler_params=_params(("arbitrary",), VMEM_LIMIT_BIG),
    )(x2, attn_n, rec_g, mod8, post_w, w_out_bf)


def _load_weights_once(pairs, sem):
    @pl.when(pl.program_id(0) == 0)
    def _():
        cps = [pltpu.make_async_copy(src, dst, sem.at[i]) for i, (src, dst) in enumerate(pairs)]
        for cp in cps:
            cp.start()
        for cp in cps:
            cp.wait()


def _mlp_fwd(x1, mod8, pre_w, w_up_bf, w_down_bf, T):
    N = x1.shape[0]
    TM = _tile_rows(T)
    tps = T // TM

    def body(x_ref, mod_ref, pw_ref, wu_hbm, wd_hbm, up_ref, d_ref, h2_ref, wu, wd, sem):
        _load_weights_once([(wu_hbm, wu), (wd_hbm, wd)], sem)
        x = x_ref[...]
        r = lax.rsqrt(_mean_last(x * x) + EPS)
        h = (x * r * pw_ref[...]) * (1.0 + mod_ref[4:5, :]) + mod_ref[3:4, :]
        hb = _bf(h)
        h2_ref[...] = hb
        up = _dot(hb, wu[...])
        up_ref[...] = up
        ru = jnp.maximum(up, 0.0)
        d_ref[...] = _dot(_bf(ru * ru), wd[...])

    row = lambda w: pl.BlockSpec((TM, w), lambda i: (i, 0))
    return pl.pallas_call(
        body, name="mlp_fwd", grid=(N // TM,),
        in_specs=[row(D_MODEL), _mod_spec(tps), pl.BlockSpec((1, D_MODEL), lambda i: (0, 0)),
                  pl.BlockSpec(memory_space=pl.ANY), pl.BlockSpec(memory_space=pl.ANY)],
        out_specs=[row(D_FF), row(D_MODEL), row(D_MODEL)],
        out_shape=[SDS((N, D_FF), F32), SDS((N, D_MODEL), F32), SDS((N, D_MODEL), BF16)],
        scratch_shapes=[pltpu.VMEM((D_MODEL, D_FF), BF16), pltpu.VMEM((D_FF, D_MODEL), BF16),
                        pltpu.SemaphoreType.DMA((2,))],
        compiler_params=_params(("arbitrary",), VMEM_LIMIT_BIG),
    )(x1, mod8, pre_w, w_up_bf, w_down_bf)


def _acc_rows(acc_ref, first, rows):
    @pl.when(first)
    def _():
        acc_ref[...] = jnp.zeros(acc_ref.shape, F32)
    for i, r in enumerate(rows):
        acc_ref[i:i + 1, :] += r


def _mlp_bwd(x1, d, up, tgt, mod8, pre_w, post_w, w_down_t_bf, w_up_t_bf, T):
    N = x1.shape[0]
    TM = _tile_rows(T)
    tps = T // TM

    def body(x_ref, d_ref, up_ref, t_ref, mod_ref, pw_ref, qw_ref, wdt_hbm, wut_hbm,
             dx_ref, u_ref, dup_ref, dd_ref, acc_ref, wdt, wut, sem):
        _load_weights_once([(wdt_hbm, wdt), (wut_hbm, wut)], sem)
        sh2, sc2, g2 = mod_ref[3:4, :], mod_ref[4:5, :], mod_ref[5:6, :]
        x = x_ref[...]
        r1 = lax.rsqrt(_mean_last(x * x) + EPS)
        xh = x * r1
        n2 = xh * pw_ref[...]
        dv = d_ref[...]
        rd = lax.rsqrt(_mean_last(dv * dv) + EPS)
        dh = dv * rd
        rr = dh * qw_ref[...]
        e = x + g2 * rr - t_ref[...]
        loss = 0.5 * jnp.sum(_sum_rows(e * e), axis=1, keepdims=True) / D_MODEL
        dy = e * (1.0 / D_MODEL)
        dg2 = _sum_rows(dy * rr)
        drr = dy * g2
        dw_post = _sum_rows(drr * dh)
        ddh = drr * qw_ref[...]
        dd = _bf(rd * (ddh - dh * _mean_last(ddh * dh)))
        dd_ref[...] = dd
        ru = jnp.maximum(up_ref[...], 0.0)
        u_ref[...] = _bf(ru * ru)
        dup = _bf(_dot(dd, wdt[...]) * (2.0 * ru))
        dup_ref[...] = dup
        dh2 = _dot(dup, wut[...])
        dsh2 = _sum_rows(dh2)
        dsc2 = _sum_rows(dh2 * n2)
        dn2 = dh2 * (1.0 + sc2)
        dw_pre = _sum_rows(dn2 * xh)
        dxh = dn2 * pw_ref[...]
        dx_ref[...] = dy + r1 * (dxh - xh * _mean_last(dxh * xh))
        _acc_rows(acc_ref, pl.program_id(0) % tps == 0,
                  [dsh2, dsc2, dg2, dw_pre, dw_post, jnp.broadcast_to(loss, (1, D_MODEL))])

    row = lambda w: pl.BlockSpec((TM, w), lambda i: (i, 0))
    vec = pl.BlockSpec((1, D_MODEL), lambda i: (0, 0))
    B = N // T
    return pl.pallas_call(
        body, name="mlp_bwd", grid=(N // TM,),
        in_specs=[row(D_MODEL), row(D_MODEL), row(D_FF), row(D_MODEL), _mod_spec(tps), vec, vec,
                  pl.BlockSpec(memory_space=pl.ANY), pl.BlockSpec(memory_space=pl.ANY)],
        out_specs=[row(D_MODEL), row(D_FF), row(D_FF), row(D_MODEL), _mod_spec(tps)],
        out_shape=[SDS((N, D_MODEL), F32), SDS((N, D_FF), BF16), SDS((N, D_FF), BF16),
                   SDS((N, D_MODEL), BF16), SDS((B, 8, D_MODEL), F32)],
        scratch_shapes=[pltpu.VMEM((D_MODEL, D_FF), BF16), pltpu.VMEM((D_FF, D_MODEL), BF16),
                        pltpu.SemaphoreType.DMA((2,))],
        compiler_params=_params(("arbitrary",), VMEM_LIMIT_BIG),
    )(x1, d, up, tgt, mod8, pre_w, post_w, w_down_t_bf, w_up_t_bf)


def _mix_bwd(mix, dx1, mod8, post_w, w_out_t_bf, T):
    N = mix.shape[0]
    TM = _tile_rows(T)
    tps = T // TM

    def body(mix_ref, dx_ref, mod_ref, pw_ref, w_ref, dan_ref, drg_ref, dmix_ref, acc_ref):
        g1 = mod_ref[2:3, :]
        mix = mix_ref[...]
        dx1 = dx_ref[...]
        rm = lax.rsqrt(_mean_last(mix * mix) + EPS)
        mh = mix * rm
        dg1 = _sum_rows(dx1 * (mh * pw_ref[...]))
        dr = dx1 * g1
        dw_post = _sum_rows(dr * mh)
        dmh = dr * pw_ref[...]
        dmix = _bf(rm * (dmh - mh * _mean_last(dmh * mh)))
        dmix_ref[...] = dmix
        dcat = _dot(dmix, w_ref[...])
        dan_ref[...] = dcat[:, :ATT_WIDTH]
        drg_ref[...] = dcat[:, ATT_WIDTH:]
        _acc_rows(acc_ref, pl.program_id(0) % tps == 0, [dg1, dw_post])

    row = lambda w: pl.BlockSpec((TM, w), lambda i: (i, 0))
    B = N // T
    return pl.pallas_call(
        body, name="mix_bwd", grid=(N // TM,),
        in_specs=[row(D_MODEL), row(D_MODEL), _mod_spec(tps), pl.BlockSpec((1, D_MODEL), lambda i: (0, 0)),
                  pl.BlockSpec((D_MODEL, D_MODEL), lambda i: (0, 0))],
        out_specs=[row(ATT_WIDTH), row(HG_WIDTH), row(D_MODEL), _mod_spec(tps)],
        out_shape=[SDS((N, ATT_WIDTH), F32), SDS((N, HG_WIDTH), F32), SDS((N, D_MODEL), BF16),
                   SDS((B, 8, D_MODEL), F32)],
        compiler_params=_params(("arbitrary",), VMEM_LIMIT_BIG),
    )(mix, dx1, mod8, post_w, w_out_t_bf)


def _hgrn_bwd(proj3, lb, hg_w, o, s_prev, drg):
    B, T, _ = proj3.shape
    nc = T // HG_CHUNK
    head = _hgrn_specs(B, T)

    def body(hq_ref, hf_ref, hi_ref, hg_ref, lb_ref, gw_ref, o_ref, sp_ref, drg_ref,
             dhq_ref, dhf_ref, dhi_ref, dhg_ref, dlb_ref, dgw_ref, dst):
        dst[...] = jnp.zeros((HG_HEAD_DIM, HG_HEAD_DIM), F32)
        tri = _tri(True)
        tri_lo = jnp.where(tri, 1.0, 0.0).astype(BF16)
        tri_up = jnp.where(_tri(False), 1.0, 0.0).astype(BF16)
        last_row = lax.broadcasted_iota(jnp.int32, (HG_CHUNK, LANES), 0) == HG_CHUNK - 1
        lbv = lb_ref[...]
        gw = gw_ref[...]

        def chunk(i, carry):
            dlb, dgw = carry
            c = nc - 1 - i
            rows = pl.ds(pl.multiple_of(c * HG_CHUNK, HG_CHUNK), HG_CHUNK)
            hq = hq_ref[rows, :]
            gt = _hgrn_gates(hq, hf_ref[rows, :], lbv, tri_lo)
            sq, sg, qd, kd, k2 = gt["sq"], gt["sg"], gt["qd"], gt["kd"], gt["k2"]
            qdb, kdb, k2b = _bf(qd), _bf(kd), _bf(k2)
            v = _bf(hi_ref[rows, :])
            ov = o_ref[rows, :]
            hg = hg_ref[rows, :]
            shg = _sigmoid(hg)
            dr = drg_ref[rows, :]
            ro = lax.rsqrt(_mean_last(ov * ov) + EPS)
            oh = ov * ro
            dhg_ref[rows, :] = dr * (oh * gw) * (shg + hg * shg * (1.0 - shg))
            drn = dr * (hg * shg)
            dgw = dgw + _sum_rows(drn * oh)
            doh = drn * gw
            do = _bf(ro * (doh - oh * _mean_last(doh * oh)))
            sp = sp_ref[c]
            ds = dst[...]
            dsb = _bf(ds)
            a = jnp.where(tri, _dot_nt(qdb, kdb), 0.0)
            da = _bf(jnp.where(tri, _dot_nt(do, v), 0.0))
            dhi_ref[rows, :] = _dot_tn(_bf(a), do) + _dot_nt(k2b, dsb)
            dqd = _dot(da, kdb) + _dot(do, _bf(sp))
            dkd = _dot_tn(da, qdb)
            dk2 = _dot(v, dsb)
            dst[...] = ds * gt["ebl"] + _dot_tn(do, qdb)
            dbl = _sum_rows(ds * sp) * gt["ebl"] + _sum_rows(dk2 * k2)
            db = dqd * qd - dkd * kd - dk2 * k2
            db = db + jnp.where(last_row, dbl, 0.0)
            dq = dqd * gt["eb"]
            dk = dkd * gt["enb"] + dk2 * gt["e2"]
            df = _tri_sum(tri_up, db) / gt["f"] - dk
            dhf_ref[rows, :] = df * (1.0 - lbv) * sg * (1.0 - sg)
            dlb = dlb + _sum_rows(df * (1.0 - sg))
            dhq_ref[rows, :] = dq * (sq + hq * sq * (1.0 - sq))
            return dlb, dgw

        zero = jnp.zeros((1, LANES), F32)
        dlb, dgw = lax.fori_loop(0, nc, chunk, (zero, zero))
        dlb_ref[...] = jnp.broadcast_to(dlb, (8, LANES))
        dgw_ref[...] = jnp.broadcast_to(dgw, (8, LANES))

    out_head = pl.BlockSpec((None, T, LANES), lambda b, h: (b, 0, h))
    small = pl.BlockSpec((None, 8, LANES), lambda b, h: (b, 0, h))
    return pl.pallas_call(
        body, name="hgrn_bwd", grid=(B, HG_HEADS),
        in_specs=[head(6), head(10), head(14), head(18),
                  pl.BlockSpec((1, LANES), lambda b, h: (0, h)),
                  pl.BlockSpec((1, LANES), lambda b, h: (0, 0)),
                  out_head,
                  pl.BlockSpec((None, None, nc, HG_HEAD_DIM, HG_HEAD_DIM), lambda b, h: (b, h, 0, 0, 0)),
                  out_head],
        out_specs=[out_head, out_head, out_head, out_head, small, small],
        out_shape=[SDS((B, T, HG_WIDTH), F32)] * 4 + [SDS((B, 8, HG_WIDTH), F32)] * 2,
        scratch_shapes=[pltpu.VMEM((HG_HEAD_DIM, HG_HEAD_DIM), F32)],
        compiler_params=_params(("arbitrary", "arbitrary"), VMEM_LIMIT_BIG),
    )(proj3, proj3, proj3, proj3, lb, hg_w, o, s_prev, drg)


def _attn_bwd(qr, kr, proj3, attn_o, dan, tables, sinks, attn_w):
    B, T, _ = proj3.shape
    nb = T // WINDOW
    cos, sinl, sinr = tables
    QKV = ATT_WIDTH + 2 * LANES

    def body(qr_ref, kr_ref, v_ref, o_ref, dan_ref, cos_ref, sl_ref, sr_ref, sink_ref, aw_ref,
             dqkv_ref, dsink_ref, daw_ref, kpad, vpad, dkpad, dvpad, dqb, dsk):
        kpad[0:WINDOW, :] = jnp.zeros((WINDOW, LANES), BF16)
        vpad[0:WINDOW, :] = jnp.zeros((WINDOW, LANES), BF16)
        kpad[WINDOW:, :] = kr_ref[...]
        vpad[WINDOW:, :] = _bf(v_ref[...])
        dkpad[...] = jnp.zeros(dkpad.shape, F32)
        dvpad[...] = jnp.zeros(dvpad.shape, F32)
        dsk[...] = jnp.zeros(dsk.shape, F32)
        aw = aw_ref[...]

        def block(n, daw):
            r0 = pl.multiple_of(n * WINDOW, WINDOW)
            rows = pl.ds(r0, WINDOW)
            band = pl.ds(r0, 2 * WINDOW)
            ob = o_ref[rows, :]
            dn = dan_ref[rows, :]
            ro = lax.rsqrt(_mean_last(ob * ob) + EPS)
            oh = ob * ro
            daw = daw + _sum_rows(dn * oh)
            doh = dn * aw
            do = _bf(ro * (doh - oh * _mean_last(doh * oh)))
            doparts = [do[:, j * LANES:(j + 1) * LANES] for j in range(ATT_WIDTH // LANES)]
            qparts = [qr_ref[rows, j * LANES:(j + 1) * LANES] for j in range(ATT_WIDTH // LANES)]
            mask = _band_mask(n)
            for hk in range(ATT_KV_HEADS):
                lanes = slice(hk * ATT_HEAD_DIM, (hk + 1) * ATT_HEAD_DIM)
                qs = _stack_heads(qparts, hk)
                dos = _stack_heads(doparts, hk)
                kk, vv = kpad[band, lanes], vpad[band, lanes]
                p, psink = _softmax_band(qs, kk, mask, _sink_col(sink_ref, hk))
                dp = _dot_nt(dos, vv)
                delta = jnp.sum(p * dp, axis=-1, keepdims=True)
                ds = _bf(p * (dp - delta) * (ATT_HEAD_DIM ** -0.5))
                sk = psink * delta
                dqs = _dot(ds, kk)
                dkpad[band, lanes] += _dot_tn(ds, qs)
                dvpad[band, lanes] += _dot_tn(_bf(p), dos)
                for g in range(ATT_GROUP):
                    h = ATT_GROUP * hk + g
                    dqb[:, h * ATT_HEAD_DIM:(h + 1) * ATT_HEAD_DIM] = dqs[g * WINDOW:(g + 1) * WINDOW, :]
                    dsk[h:h + 1, :] += jnp.broadcast_to(-_sum_rows(sk[g * WINDOW:(g + 1) * WINDOW, :]), (1, LANES))
            cs, sl, sr = cos_ref[rows, :], sl_ref[rows, :], sr_ref[rows, :]
            for j in range(ATT_WIDTH // LANES):
                dqkv_ref[rows, j * LANES:(j + 1) * LANES] = _rope_t(dqb[:, j * LANES:(j + 1) * LANES], cs, sl, sr)
            return daw

        daw = lax.fori_loop(0, nb, block, jnp.zeros((1, ATT_WIDTH), F32))
        daw_ref[...] = jnp.broadcast_to(daw, (8, ATT_WIDTH))
        dsink_ref[...] = dsk[...]

        def finish(n, carry):
            r0 = pl.multiple_of(n * WINDOW, WINDOW)
            rows = pl.ds(r0, WINDOW)
            nxt = pl.ds(r0 + WINDOW, WINDOW)
            cs, sl, sr = cos_ref[rows, :], sl_ref[rows, :], sr_ref[rows, :]
            dqkv_ref[rows, ATT_WIDTH:ATT_WIDTH + LANES] = _rope_t(dkpad[nxt, :], cs, sl, sr)
            dqkv_ref[rows, ATT_WIDTH + LANES:QKV] = dvpad[nxt, :]
            return carry

        lax.fori_loop(0, nb, finish, 0)

    seq = lambda w, j: pl.BlockSpec((None, T, w), lambda b: (b, 0, j))
    full = lambda r, w: pl.BlockSpec((r, w), lambda b: (0, 0))
    return pl.pallas_call(
        body, name="attn_bwd", grid=(B,),
        in_specs=[seq(ATT_WIDTH, 0), seq(LANES, 0), seq(LANES, 5), seq(ATT_WIDTH, 0), seq(ATT_WIDTH, 0),
                  full(T, LANES), full(T, LANES), full(T, LANES),
                  pl.BlockSpec(memory_space=pltpu.SMEM), full(1, ATT_WIDTH)],
        out_specs=[seq(QKV, 0), pl.BlockSpec((None, 8, LANES), lambda b: (b, 0, 0)),
                   pl.BlockSpec((None, 8, ATT_WIDTH), lambda b: (b, 0, 0))],
        out_shape=[SDS((B, T, QKV), F32), SDS((B, 8, LANES), F32), SDS((B, 8, ATT_WIDTH), F32)],
        scratch_shapes=[pltpu.VMEM((T + WINDOW, LANES), BF16), pltpu.VMEM((T + WINDOW, LANES), BF16),
                        pltpu.VMEM((T + WINDOW, LANES), F32), pltpu.VMEM((T + WINDOW, LANES), F32),
                        pltpu.VMEM((WINDOW, ATT_WIDTH), F32), pltpu.VMEM((8, LANES), F32)],
        compiler_params=_params(("arbitrary",), VMEM_LIMIT_BIG),
    )(qr, kr, proj3, attn_o, dan, cos, sinl, sinr, sinks, attn_w)


def _in_bwd(x2, dx1, dqkv, dhq, dhf, dhi, dhg, mod8, pre_w, w_in_t_bf, T):
    N = x2.shape[0]
    TM = _tile_rows(T)
    tps = T // TM
    pieces = [(0, ATT_WIDTH + 2 * LANES), (768, HG_WIDTH), (1280, HG_WIDTH), (1792, HG_WIDTH), (2304, HG_WIDTH)]

    def body(x_ref, dx_ref, p0, p1, p2, p3, p4, mod_ref, pw_ref, w_ref, gx_ref, dproj_ref, acc_ref):
        sc1 = mod_ref[1:2, :]
        dh = jnp.zeros((TM, D_MODEL), F32)
        for ref, (off, width) in zip((p0, p1, p2, p3, p4), pieces):
            pb = _bf(ref[...])
            dproj_ref[:, off:off + width] = pb
            dh = dh + _dot(pb, w_ref[off:off + width, :])
        x = x_ref[...]
        r = lax.rsqrt(_mean_last(x * x) + EPS)
        xh = x * r
        n1 = xh * pw_ref[...]
        dsh1 = _sum_rows(dh)
        dsc1 = _sum_rows(dh * n1)
        dn1 = dh * (1.0 + sc1)
        dw_pre = _sum_rows(dn1 * xh)
        dxh = dn1 * pw_ref[...]
        gx_ref[...] = dx_ref[...] + r * (dxh - xh * _mean_last(dxh * xh))
        _acc_rows(acc_ref, pl.program_id(0) % tps == 0, [dsh1, dsc1, dw_pre])

    row = lambda w: pl.BlockSpec((TM, w), lambda i: (i, 0))
    B = N // T
    return pl.pallas_call(
        body, name="in_bwd", grid=(N // TM,),
        in_specs=[row(D_MODEL), row(D_MODEL), row(768), row(HG_WIDTH), row(HG_WIDTH), row(HG_WIDTH),
                  row(HG_WIDTH), _mod_spec(tps), pl.BlockSpec((1, D_MODEL), lambda i: (0, 0)),
                  pl.BlockSpec((IN_COLS, D_MODEL), lambda i: (0, 0))],
        out_specs=[row(D_MODEL), row(IN_COLS), _mod_spec(tps)],
        out_shape=[SDS((N, D_MODEL), F32), SDS((N, IN_COLS), BF16), SDS((B, 8, D_MODEL), F32)],
        compiler_params=_params(("arbitrary",), VMEM_LIMIT_BIG),
    )(x2, dx1, dqkv, dhq, dhf, dhi, dhg, mod8, pre_w, w_in_t_bf)


def _matmul_tn(name, a, b, tn, by_columns=False):
    K, M = a.shape
    Nc = b.shape[1]
    tm = min(512, M)
    tk = min(512, K)
    nk = K // tk

    def body(a_ref, b_ref, o_ref, acc):
        k = pl.program_id(2)

        @pl.when(k == 0)
        def _():
            acc[...] = jnp.zeros(acc.shape, F32)

        acc[...] += _dot_tn(a_ref[...], b_ref[...])

        @pl.when(k == nk - 1)
        def _():
            o_ref[...] = _bf(acc[...])

    if by_columns:
        out_shape = SDS((Nc // tn, M, tn), BF16)
        out_spec = pl.BlockSpec((None, tm, tn), lambda i, j, k: (j, i, 0))
    else:
        out_shape = SDS((M, Nc), BF16)
        out_spec = pl.BlockSpec((tm, tn), lambda i, j, k: (i, j))
    return pl.pallas_call(
        body, name=name, grid=(M // tm, Nc // tn, nk),
        in_specs=[pl.BlockSpec((tk, tm), lambda i, j, k: (k, i)),
                  pl.BlockSpec((tk, tn), lambda i, j, k: (k, j))],
        out_specs=out_spec, out_shape=out_shape,
        scratch_shapes=[pltpu.VMEM((tm, tn), F32)],
        compiler_params=_params(("arbitrary", "arbitrary", "arbitrary"), VMEM_LIMIT_BIG),
    )(a, b)


def _adamw_math(w, g, m, v):
    m2 = ADAM_B1 * m + (1.0 - ADAM_B1) * g
    v2 = ADAM_B2 * v + (1.0 - ADAM_B2) * (g * g)
    m_hat = m2 / (1.0 - ADAM_B1 ** ADAM_STEP)
    v_hat = v2 / (1.0 - ADAM_B2 ** ADAM_STEP)
    delta = -ADAM_LR * (m_hat / (jnp.sqrt(v_hat) + ADAM_EPS) + ADAM_WD * w)
    return delta, m2, v2


def _reduce_adamw(name, parts, w, m, v):
    r, c = w.shape
    tr = r if r <= 256 else 256

    def body(p_ref, w_ref, m_ref, v_ref, g_ref, d_ref, m2_ref, v2_ref):
        g = p_ref[0].astype(F32)
        for s in range(1, N_DEV):
            g = g + p_ref[s].astype(F32)
        g_ref[...] = g
        d_ref[...], m2_ref[...], v2_ref[...] = _adamw_math(w_ref[...], g, m_ref[...], v_ref[...])

    blk = pl.BlockSpec((tr, c), lambda i: (i, 0))
    return pl.pallas_call(
        body, name=name, grid=(r // tr,),
        in_specs=[pl.BlockSpec((N_DEV, tr, c), lambda i: (0, i, 0)), blk, blk, blk],
        out_specs=[blk] * 4, out_shape=[SDS((r, c), F32)] * 4,
        compiler_params=_params(("arbitrary",), VMEM_LIMIT_BIG),
    )(parts, w, m, v)


def _ada_grad_adamw(c_all, dmod_all, w, m, v):
    r, c = w.shape
    tr = 256
    nb = c_all.shape[0]

    def body(c_ref, dm_ref, w_ref, m_ref, v_ref, g_ref, d_ref, m2_ref, v2_ref):
        cv = c_ref[...]
        g = _dot_tn(cv * _sigmoid(cv), dm_ref[...])
        g_ref[...] = g
        d_ref[...], m2_ref[...], v2_ref[...] = _adamw_math(w_ref[...], g, m_ref[...], v_ref[...])

    blk = pl.BlockSpec((tr, c), lambda i: (i, 0))
    return pl.pallas_call(
        body, name="ada_grad_adamw", grid=(r // tr,),
        in_specs=[pl.BlockSpec((nb, tr), lambda i: (0, i)), pl.BlockSpec((nb, c), lambda i: (0, 0)),
                  blk, blk, blk],
        out_specs=[blk] * 4, out_shape=[SDS((r, c), F32)] * 4,
        compiler_params=_params(("arbitrary",)),
    )(c_all, dmod_all, w, m, v)


_SMALL = [("b_ada", 6144), ("pre_w_mix", 1024), ("attn_sinks", 128), ("attn_out_w", 512), ("lb_table", 1024),
          ("hg_norm_w", 128), ("post_w_mix", 1024), ("pre_w_mlp", 1024), ("post_w_mlp", 1024)]


def _pack_small(vals):
    out = []
    for name, width in _SMALL:
        f = vals[name].reshape(-1).astype(F32)
        out.append(jnp.pad(f, (0, width - f.shape[0])))
    return jnp.concatenate(out).reshape(1, -1)


def _unpack_small(vec, shapes):
    out, off = {}, 0
    for name, width in _SMALL:
        n = 1
        for s in shapes[name]:
            n *= s
        out[name] = vec[0, off:off + n].reshape(shapes[name])
        off += width
    return out


def _columns_to_full(g):
    return g.transpose(1, 0, 2).reshape(g.shape[1], -1)


def kernel(x, c, w_ada, b_ada, pre_w_mix, w_in, attn_sinks, attn_out_w, lb_table, hg_norm_w, w_out, post_w_mix, pre_w_mlp, w_up, w_down, post_w_mlp, loss_target, m_w_ada, m_b_ada, m_pre_w_mix, m_w_in, m_attn_sinks, m_attn_out_w, m_lb_table, m_hg_norm_w, m_w_out, m_post_w_mix, m_pre_w_mlp, m_w_up, m_w_down, m_post_w_mlp, v_w_ada, v_b_ada, v_pre_w_mix, v_w_in, v_attn_sinks, v_attn_out_w, v_lb_table, v_hg_norm_w, v_w_out, v_post_w_mix, v_pre_w_mlp, v_w_up, v_w_down, v_post_w_mlp):
    B, T, _ = x.shape
    N = B * T
    me = 4 * lax.axis_index("x") + 2 * lax.axis_index("y") + lax.axis_index("c")
    x2 = x.reshape(N, D_MODEL)
    tgt2 = loss_target.reshape(N, D_MODEL)

    w_in_g, w_out_g, w_up_g, w_down_g, c_g = _exchange(
        "gather_weights",
        [_bf(w_in[0]), _bf(w_out[0]), _bf(w_up[0]), _bf(w_down[0]), c],
        ["gather"] * 5)
    w_in_f = _columns_to_full(w_in_g)
    w_out_f = w_out_g.reshape(D_MODEL, D_MODEL)
    w_up_f = _columns_to_full(w_up_g)
    w_down_f = w_down_g.reshape(D_FF, D_MODEL)
    c_all = c_g.reshape(N_DEV * B, D_MODEL)

    ada_cols = w_ada.shape[2]
    b_mine = lax.dynamic_slice(b_ada, (0, me * ada_cols), (1, ada_cols))
    mod_cols = _ada_mod(c_all, w_ada[0], b_mine)
    (mod_g,) = _exchange("scatter_mod", [mod_cols.reshape(N_DEV, B, ada_cols)], ["a2a"])
    mod = mod_g.transpose(1, 0, 2).reshape(B, 6, D_MODEL)
    mod8 = jnp.pad(mod, ((0, 0), (0, 2), (0, 0)))

    lb_p = jax.nn.softmax(lb_table, axis=0)
    lb = lb_p[1:2]
    tables = _rope_tables(T)

    proj, h1 = _in_proj(x2, mod8, pre_w_mix, w_in_f, T)
    proj3 = proj.reshape(B, T, IN_COLS)
    attn_o, attn_n, qr, kr = _attn_fwd(proj3, tables, attn_sinks, attn_out_w)
    rec_o, rec_g, s_prev = _hgrn_fwd(proj3, lb, hg_norm_w)
    mix, x1, cat = _mix_out(x2, attn_n.reshape(N, ATT_WIDTH), rec_g.reshape(N, HG_WIDTH), mod8,
                            post_w_mix, w_out_f, T)
    up, d, h2 = _mlp_fwd(x1, mod8, pre_w_mlp, w_up_f, w_down_f, T)

    dx1, u, dup, dd, acc_mlp = _mlp_bwd(x1, d, up, tgt2, mod8, pre_w_mlp, post_w_mlp,
                                        w_down_f.T, w_up_f.T, T)
    dan, drg, dmix, acc_mix = _mix_bwd(mix, dx1, mod8, post_w_mix, w_out_f.T, T)
    dhq, dhf, dhi, dhg, dlb_p, dgw_p = _hgrn_bwd(proj3, lb, hg_norm_w, rec_o, s_prev,
                                                 drg.reshape(B, T, HG_WIDTH))
    dqkv, dsink_p, daw_p = _attn_bwd(qr, kr, proj3, attn_o, dan.reshape(B, T, ATT_WIDTH), tables,
                                     attn_sinks, attn_out_w)
    flat = lambda a: a.reshape(N, a.shape[-1])
    grad_x, dproj, acc_in = _in_bwd(x2, dx1, flat(dqkv), flat(dhq), flat(dhf), flat(dhi), flat(dhg),
                                    mod8, pre_w_mix, w_in_f.T, T)

    gw_in = _matmul_tn("grad_w_in", h1, dproj, IN_COLS // 2)
    in_cols = w_in.shape[2]
    gw_in = gw_in.reshape(D_MODEL, N_DEV, in_cols).transpose(1, 0, 2)
    gw_out = _matmul_tn("grad_w_out", cat, dmix, 512).reshape(N_DEV, D_MODEL // N_DEV, D_MODEL)
    gw_up = _matmul_tn("grad_w_up", h2, dup, D_FF // N_DEV, by_columns=True)
    gw_down = _matmul_tn("grad_w_down", u, dd, 512).reshape(N_DEV, D_FF // N_DEV, D_MODEL)

    dmod = jnp.concatenate([acc_in[:, 0:2], acc_mix[:, 0:1], acc_mlp[:, 0:3]], axis=1)
    dlb = dlb_p[:, 0].sum(0)
    dlb_table = jnp.stack([-dlb, dlb]) * (lb_p[0] * lb_p[1])[None, :]
    small = {
        "b_ada": dmod.sum(0),
        "pre_w_mix": acc_in[:, 2].sum(0),
        "attn_sinks": dsink_p[:, :, 0].sum(0),
        "attn_out_w": daw_p[:, 0].sum(0),
        "lb_table": dlb_table,
        "hg_norm_w": dgw_p[:, 0].reshape(B, HG_HEADS, LANES).sum((0, 1)),
        "post_w_mix": acc_mix[:, 1].sum(0),
        "pre_w_mlp": acc_mlp[:, 3].sum(0),
        "post_w_mlp": acc_mlp[:, 4].sum(0),
    }
    loss_part = acc_mlp[:, 5, 0].sum()
    dmod_blocks = dmod.reshape(B, N_DEV, ada_cols).transpose(1, 0, 2)

    r_in, r_out, r_up, r_down, r_dmod, r_small = _exchange(
        "reduce_grads", [gw_in, gw_out, gw_up, gw_down, dmod_blocks, _pack_small(small)],
        ["a2a", "a2a", "a2a", "a2a", "a2a", "gather"])

    res = {}
    res["w_in"] = _reduce_adamw("adamw_w_in", r_in, w_in[0], m_w_in[0], v_w_in[0])
    res["w_out"] = _reduce_adamw("adamw_w_out", r_out, w_out[0], m_w_out[0], v_w_out[0])
    res["w_up"] = _reduce_adamw("adamw_w_up", r_up, w_up[0], m_w_up[0], v_w_up[0])
    res["w_down"] = _reduce_adamw("adamw_w_down", r_down, w_down[0], m_w_down[0], v_w_down[0])
    res["w_ada"] = _ada_grad_adamw(c_all, r_dmod.reshape(N_DEV * B, ada_cols), w_ada[0], m_w_ada[0], v_w_ada[0])

    given = dict(b_ada=(b_ada, m_b_ada, v_b_ada), pre_w_mix=(pre_w_mix, m_pre_w_mix, v_pre_w_mix),
                 attn_sinks=(attn_sinks, m_attn_sinks, v_attn_sinks),
                 attn_out_w=(attn_out_w, m_attn_out_w, v_attn_out_w), lb_table=(lb_table, m_lb_table, v_lb_table),
                 hg_norm_w=(hg_norm_w, m_hg_norm_w, v_hg_norm_w), post_w_mix=(post_w_mix, m_post_w_mix, v_post_w_mix),
                 pre_w_mlp=(pre_w_mlp, m_pre_w_mlp, v_pre_w_mlp), post_w_mlp=(post_w_mlp, m_post_w_mlp, v_post_w_mlp))
    shapes = {k: t[0].shape for k, t in given.items()}
    packed = [_pack_small({k: t[i] for k, t in given.items()}) for i in range(3)]
    small_res = _reduce_adamw("adamw_small", r_small, *packed)
    small_out = [_unpack_small(a, shapes) for a in small_res]
    for k in given:
        res[k] = tuple(so[k] for so in small_out)

    loss = lax.psum(loss_part, ("x", "y", "c"))
    order = ["w_ada", "b_ada", "pre_w_mix", "w_in", "attn_sinks", "attn_out_w", "lb_table", "hg_norm_w", "w_out",
             "post_w_mix", "pre_w_mlp", "w_up", "w_down", "post_w_mlp"]
    big = {"w_ada", "w_in", "w_out", "w_up", "w_down"}
    outs = [loss, grad_x.reshape(B, T, D_MODEL)]
    for i in range(4):
        for k in order:
            a = res[k][i]
            outs.append(a[None] if k in big else a)
    return tuple(outs)
```

```python
import functools

import jax
import jax.numpy as jnp
from jax import lax
from jax.experimental import pallas as pl
from jax.experimental.pallas import tpu as pltpu

F32 = jnp.float32
BF16 = jnp.bfloat16
SDS = jax.ShapeDtypeStruct

D_MODEL = 1024
ATT_WIDTH = 512
ATT_HEAD_DIM = 64
ATT_KV_HEADS = 2
ATT_GROUP = 4
WINDOW = 128
ROPE_DIM = 16
ROPE_THETA = 500000.0
HG_WIDTH = 512
HG_HEAD_DIM = 128
HG_HEADS = 4
HG_CHUNK = 32
IN_COLS = 2816
D_FF = 4096
EPS = 1e-6
N_DEV = 8

ADAM_LR = 0.001
ADAM_B1 = 0.9
ADAM_B2 = 0.999
ADAM_EPS = 1e-08
ADAM_WD = 0.01
ADAM_STEP = 10

VMEM_LIMIT_BIG = 56 << 20
LANES = 128

MESH = pl.DeviceIdType.MESH
NT_DIMS = (((1,), (1,)), ((), ()))
TN_DIMS = (((0,), (0,)), ((), ()))


def _dot(a, b):
    return jnp.dot(a, b, preferred_element_type=F32)


def _dot_nt(a, b):
    return lax.dot_general(a, b, NT_DIMS, preferred_element_type=F32)


def _dot_tn(a, b):
    return lax.dot_general(a, b, TN_DIMS, preferred_element_type=F32)


def _bf(a):
    return a.astype(BF16)


def _sigmoid(a):
    return 1.0 / (1.0 + jnp.exp(-a))


def _mean_last(a):
    return jnp.mean(a, axis=-1, keepdims=True)


def _sum_rows(a):
    return jnp.sum(a, axis=0, keepdims=True)


def _tri_sum(tri_bf, a):
    a1 = _bf(a)
    r1 = a - a1.astype(F32)
    a2 = _bf(r1)
    a3 = _bf(r1 - a2.astype(F32))
    return _dot(tri_bf, a1) + _dot(tri_bf, a2) + _dot(tri_bf, a3)


def _params(sem=None, vmem=None):
    kw = {}
    if sem is not None:
        kw["dimension_semantics"] = sem
    if vmem is not None:
        kw["vmem_limit_bytes"] = vmem
    return pltpu.CompilerParams(**kw)


ANY_SPEC = pl.BlockSpec(memory_space=pl.ANY)


def _exchange_shapes(srcs, modes):
    out_shape = []
    for s, m in zip(srcs, modes):
        shp = (N_DEV,) + tuple(s.shape) if m == "gather" else tuple(s.shape)
        out_shape.append(SDS(shp, s.dtype))
    return out_shape


def _exchange_sems(n):
    return [pltpu.SemaphoreType.DMA((n, N_DEV - 1)), pltpu.SemaphoreType.DMA((n, N_DEV - 1)),
            pltpu.SemaphoreType.DMA((n,))]


def _exchange_copies(modes, src_refs, out_refs, send_sems, recv_sems, own_sems):
    x, y, c = lax.axis_index("x"), lax.axis_index("y"), lax.axis_index("c")
    me = 4 * x + 2 * y + c
    own, sends, recvs = [], [], []
    for a, mode in enumerate(modes):
        gather = mode == "gather"
        mine = src_refs[a] if gather else src_refs[a].at[me]
        own.append(pltpu.make_async_copy(mine, out_refs[a].at[me], own_sems.at[a]))
        for k in range(1, N_DEV):
            px, py, pc = x ^ ((k >> 2) & 1), y ^ ((k >> 1) & 1), c ^ (k & 1)
            peer = 4 * px + 2 * py + pc
            src = src_refs[a] if gather else src_refs[a].at[peer]
            sends.append(pltpu.make_async_remote_copy(
                src_ref=src, dst_ref=out_refs[a].at[me],
                send_sem=send_sems.at[a, k - 1], recv_sem=recv_sems.at[a, k - 1],
                device_id=(px, py, pc), device_id_type=MESH))
            recvs.append(pltpu.make_async_remote_copy(
                src_ref=src, dst_ref=out_refs[a].at[peer],
                send_sem=send_sems.at[a, k - 1], recv_sem=recv_sems.at[a, k - 1],
                device_id=(px, py, pc), device_id_type=MESH))
    return own, sends, recvs


def _exchange_start(copies):
    own, sends, _ = copies
    for cp in own + sends:
        cp.start()


def _exchange_wait(copies):
    own, sends, recvs = copies
    for cp in recvs:
        cp.wait_recv()
    for cp in sends:
        cp.wait_send()
    for cp in own:
        cp.wait()


def _exchange(name, srcs, modes):
    n = len(srcs)

    def body(*refs):
        copies = _exchange_copies(modes, refs[:n], refs[n:2 * n], *refs[2 * n:])
        _exchange_start(copies)
        _exchange_wait(copies)

    return pl.pallas_call(
        body, name=name, out_shape=_exchange_shapes(srcs, modes),
        in_specs=[ANY_SPEC] * n, out_specs=[ANY_SPEC] * n,
        scratch_shapes=_exchange_sems(n),
    )(*srcs)


def _ride_start(modes, first, src_refs, out_refs, sems):
    @pl.when(first)
    def _():
        _exchange_start(_exchange_copies(modes, src_refs, out_refs, *sems))


def _ride_wait(modes, last, src_refs, out_refs, sems):
    @pl.when(last)
    def _():
        _exchange_wait(_exchange_copies(modes, src_refs, out_refs, *sems))


def _ada_mod(c_all, w_ada, b_ada_mine):
    nb, cols = c_all.shape[0], w_ada.shape[1]

    def body(c_ref, w_ref, b_ref, o_ref):
        cv = c_ref[...]
        ca = cv * _sigmoid(cv)
        o_ref[...] = _dot(ca, w_ref[...]) + b_ref[...]

    return pl.pallas_call(body, name="ada_mod", out_shape=SDS((nb, cols), F32))(c_all, w_ada, b_ada_mine)


def _tile_rows(T):
    return min(256, T)


def _mod_spec(tps):
    return pl.BlockSpec((None, 8, D_MODEL), lambda i: (i // tps, 0, 0))


def _in_proj(x2, mod8, pre_w, w_in_bf, T):
    N = x2.shape[0]
    TM = _tile_rows(T)
    tps = T // TM

    def body(x_ref, mod_ref, pw_ref, w_ref, proj_ref, h1_ref):
        x = x_ref[...]
        r = lax.rsqrt(_mean_last(x * x) + EPS)
        h = (x * r * pw_ref[...]) * (1.0 + mod_ref[1:2, :]) + mod_ref[0:1, :]
        hb = _bf(h)
        h1_ref[...] = hb
        proj_ref[...] = _dot(hb, w_ref[...])

    return pl.pallas_call(
        body, name="in_proj", grid=(N // TM,),
        in_specs=[pl.BlockSpec((TM, D_MODEL), lambda i: (i, 0)), _mod_spec(tps),
                  pl.BlockSpec((1, D_MODEL), lambda i: (0, 0)),
                  pl.BlockSpec((D_MODEL, IN_COLS), lambda i: (0, 0))],
        out_specs=[pl.BlockSpec((TM, IN_COLS), lambda i: (i, 0)),
                   pl.BlockSpec((TM, D_MODEL), lambda i: (i, 0))],
        out_shape=[SDS((N, IN_COLS), F32), SDS((N, D_MODEL), BF16)],
        compiler_params=_params(("arbitrary",), VMEM_LIMIT_BIG),
    )(x2, mod8, pre_w, w_in_bf)


def _rope_tables(T):
    half = ROPE_DIM // 2
    inv_freq = ROPE_THETA ** (-jnp.arange(0, ROPE_DIM, 2, dtype=F32) / ROPE_DIM)
    ang = jnp.arange(T, dtype=F32)[:, None] * inv_freq[None, :]
    cos, sin = jnp.cos(ang), jnp.sin(ang)
    ones = jnp.ones((T, ATT_HEAD_DIM - ROPE_DIM), F32)
    zeros = jnp.zeros((T, ATT_HEAD_DIM - ROPE_DIM), F32)
    zh = jnp.zeros((T, half), F32)
    cos64 = jnp.concatenate([cos, cos, ones], axis=1)
    sin_left = jnp.concatenate([-sin, zh, zeros], axis=1)
    sin_right = jnp.concatenate([zh, sin, zeros], axis=1)
    rep = LANES // ATT_HEAD_DIM
    return jnp.tile(cos64, (1, rep)), jnp.tile(sin_left, (1, rep)), jnp.tile(sin_right, (1, rep))


def _rope(xc, cs, sl, sr):
    return xc * cs + pltpu.roll(xc, LANES - 8, 1) * sl + pltpu.roll(xc, 8, 1) * sr


def _rope_t(dy, cs, sl, sr):
    return dy * cs + pltpu.roll(dy * sl, 8, 1) + pltpu.roll(dy * sr, LANES - 8, 1)


def _band_mask(n):
    rows = ATT_GROUP * WINDOW
    i = lax.broadcasted_iota(jnp.int32, (rows, 2 * WINDOW), 0) & (WINDOW - 1)
    j = lax.broadcasted_iota(jnp.int32, (rows, 2 * WINDOW), 1)
    diff = i + WINDOW - j
    return (diff >= 0) & (diff < WINDOW) & ((j >= WINDOW) | (n > 0))


def _sink_col(sink_ref, hk):
    return jnp.concatenate(
        [jnp.full((WINDOW, 1), sink_ref[0, ATT_GROUP * hk + g], F32) for g in range(ATT_GROUP)], axis=0)


def _softmax_band(qs, kk, mask, sink):
    s = _dot_nt(qs, kk) * (ATT_HEAD_DIM ** -0.5)
    s = jnp.where(mask, s, jnp.finfo(F32).min)
    m = jnp.maximum(jnp.max(s, axis=-1, keepdims=True), sink)
    p = jnp.exp(s - m)
    es = jnp.exp(sink - m)
    inv = 1.0 / (jnp.sum(p, axis=-1, keepdims=True) + es)
    return p * inv, es * inv


def _stack_heads(parts, hk):
    hs = []
    for g in range(ATT_GROUP):
        h = ATT_GROUP * hk + g
        hs.append(parts[h // 2][:, (h % 2) * ATT_HEAD_DIM:(h % 2 + 1) * ATT_HEAD_DIM])
    return jnp.concatenate(hs, axis=0)


def _attn_fwd(proj3, tables, sinks, attn_w):
    B, T, _ = proj3.shape
    nb = T // WINDOW
    cos, sinl, sinr = tables

    def body(q_ref, k_ref, v_ref, cos_ref, sl_ref, sr_ref, sink_ref, aw_ref,
             o_ref, an_ref, qr_ref, kr_ref, kpad, vpad):
        kpad[0:WINDOW, :] = jnp.zeros((WINDOW, LANES), BF16)
        vpad[0:WINDOW, :] = jnp.zeros((WINDOW, LANES), BF16)

        def block(n, carry):
            r0 = pl.multiple_of(n * WINDOW, WINDOW)
            rows = pl.ds(r0, WINDOW)
            nxt = pl.ds(r0 + WINDOW, WINDOW)
            band = pl.ds(r0, 2 * WINDOW)
            cs, sl, sr = cos_ref[rows, :], sl_ref[rows, :], sr_ref[rows, :]
            kb = _bf(_rope(k_ref[rows, :], cs, sl, sr))
            kpad[nxt, :] = kb
            kr_ref[rows, :] = kb
            vpad[nxt, :] = _bf(v_ref[rows, :])
            qparts = []
            for j in range(ATT_WIDTH // LANES):
                qp = _bf(_rope(q_ref[rows, j * LANES:(j + 1) * LANES], cs, sl, sr))
                qr_ref[rows, j * LANES:(j + 1) * LANES] = qp
                qparts.append(qp)
            mask = _band_mask(n)
            for hk in range(ATT_KV_HEADS):
                lanes = slice(hk * ATT_HEAD_DIM, (hk + 1) * ATT_HEAD_DIM)
                qs = _stack_heads(qparts, hk)
                p, _ = _softmax_band(qs, kpad[band, lanes], mask, _sink_col(sink_ref, hk))
                o = _dot(_bf(p), vpad[band, lanes])
                for g in range(ATT_GROUP):
                    h = ATT_GROUP * hk + g
                    o_ref[rows, h * ATT_HEAD_DIM:(h + 1) * ATT_HEAD_DIM] = o[g * WINDOW:(g + 1) * WINDOW, :]
            ob = o_ref[rows, :]
            an_ref[rows, :] = _bf(ob * lax.rsqrt(_mean_last(ob * ob) + EPS) * aw_ref[...])
            return carry

        lax.fori_loop(0, nb, block, 0)

    seq = lambda w, j: pl.BlockSpec((None, T, w), lambda b: (b, 0, j))
    full = lambda r, w: pl.BlockSpec((r, w), lambda b: (0, 0))
    return pl.pallas_call(
        body, name="attn_fwd", grid=(B,),
        in_specs=[seq(ATT_WIDTH, 0), seq(LANES, 4), seq(LANES, 5),
                  full(T, LANES), full(T, LANES), full(T, LANES),
                  pl.BlockSpec(memory_space=pltpu.SMEM), full(1, ATT_WIDTH)],
        out_specs=[seq(ATT_WIDTH, 0), seq(ATT_WIDTH, 0), seq(ATT_WIDTH, 0), seq(LANES, 0)],
        out_shape=[SDS((B, T, ATT_WIDTH), F32), SDS((B, T, ATT_WIDTH), BF16),
                   SDS((B, T, ATT_WIDTH), BF16), SDS((B, T, LANES), BF16)],
        scratch_shapes=[pltpu.VMEM((T + WINDOW, LANES), BF16), pltpu.VMEM((T + WINDOW, LANES), BF16)],
        compiler_params=_params(("arbitrary",), VMEM_LIMIT_BIG),
    )(proj3, proj3, proj3, cos, sinl, sinr, sinks, attn_w)


HG_GROUP = 8
HG_ROWS = HG_GROUP * HG_CHUNK


def _group_masks():
    r = lax.broadcasted_iota(jnp.int32, (HG_ROWS, HG_ROWS), 0)
    c = lax.broadcasted_iota(jnp.int32, (HG_ROWS, HG_ROWS), 1)
    same = (r // HG_CHUNK) == (c // HG_CHUNK)
    return same & (r >= c), same & (c >= r), same & (c > r)


def _ones_bf(mask):
    return jnp.where(mask, 1.0, 0.0).astype(BF16)


def _hgrn_gates(hq, hf, lb, sums_bf, ebl_scr):
    sq = _sigmoid(hq)
    q = hq * sq
    sg = _sigmoid(hf)
    f = lb + (1.0 - lb) * sg
    k = 1.0 - f
    cs = _tri_sum(sums_bf, jnp.log(f))
    b, rem = cs[:HG_ROWS], cs[HG_ROWS:]
    eb, enb, e2 = jnp.exp(b), jnp.exp(-b), jnp.exp(rem)
    ebl_scr[...] = eb * e2
    return dict(sq=sq, sg=sg, f=f, eb=eb, enb=enb, e2=e2, qd=q * eb, kd=k * enb, k2=k * e2)


def _hgrn_specs(B, T):
    head = lambda base: pl.BlockSpec((None, T, LANES), lambda b, h: (b, 0, base + h))
    return head


def _chunk_rows(c):
    return slice(c * HG_CHUNK, (c + 1) * HG_CHUNK)


def _hgrn_fwd(proj3, lb, hg_w, ride_srcs, ride_modes):
    B, T, _ = proj3.shape
    nc = T // HG_CHUNK
    ng = T // HG_ROWS
    nr = len(ride_srcs)
    head = _hgrn_specs(B, T)

    def body(*refs):
        hq_ref, hf_ref, hi_ref, hg_ref, lb_ref, gw_ref = refs[:6]
        ride_in = refs[6:6 + nr]
        o_ref, rg_ref, sp_ref = refs[6 + nr:9 + nr]
        ride_out = refs[9 + nr:9 + 2 * nr]
        st, ebl_scr = refs[9 + 2 * nr:11 + 2 * nr]
        sems = refs[11 + 2 * nr:]
        first = (pl.program_id(0) == 0) & (pl.program_id(1) == 0)
        last = (pl.program_id(0) == B - 1) & (pl.program_id(1) == HG_HEADS - 1)
        _ride_start(ride_modes, first, ride_in, ride_out, sems)

        st[...] = jnp.zeros((HG_HEAD_DIM, HG_HEAD_DIM), F32)
        lo, _, ups = _group_masks()
        sums_bf = jnp.concatenate([_ones_bf(lo), _ones_bf(ups)], axis=0)
        lbv = lb_ref[...]

        def group(gi, carry):
            rows = pl.ds(pl.multiple_of(gi * HG_ROWS, HG_ROWS), HG_ROWS)
            gt = _hgrn_gates(hq_ref[rows, :], hf_ref[rows, :], lbv, sums_bf, ebl_scr)
            v, qd, kd, k2 = _bf(hi_ref[rows, :]), _bf(gt["qd"]), _bf(gt["kd"]), _bf(gt["k2"])
            a = jnp.where(lo, _dot_nt(qd, kd), 0.0)
            o = _dot(_bf(a), v)
            s = st[...]
            inter = []
            for c in range(HG_GROUP):
                cr = _chunk_rows(c)
                sp_ref[gi * HG_GROUP + c] = s
                inter.append(_dot_nt(qd[cr], _bf(s)))
                s = s * ebl_scr[c * HG_CHUNK:c * HG_CHUNK + 1, :] + _dot_tn(v[cr], k2[cr])
            st[...] = s
            o = o + jnp.concatenate(inter, axis=0)
            o_ref[rows, :] = o
            hg = hg_ref[rows, :]
            rn = o * lax.rsqrt(_mean_last(o * o) + EPS) * gw_ref[...]
            rg_ref[rows, :] = _bf(rn * (hg * _sigmoid(hg)))
            return carry

        lax.fori_loop(0, ng, group, 0)
        _ride_wait(ride_modes, last, ride_in, ride_out, sems)

    out_head = pl.BlockSpec((None, T, LANES), lambda b, h: (b, 0, h))
    return pl.pallas_call(
        body, name="hgrn_fwd", grid=(B, HG_HEADS),
        in_specs=[head(6), head(10), head(14), head(18),
                  pl.BlockSpec((1, LANES), lambda b, h: (0, h)),
                  pl.BlockSpec((1, LANES), lambda b, h: (0, 0))] + [ANY_SPEC] * nr,
        out_specs=[out_head, out_head,
                   pl.BlockSpec((None, None, nc, HG_HEAD_DIM, HG_HEAD_DIM), lambda b, h: (b, h, 0, 0, 0))]
        + [ANY_SPEC] * nr,
        out_shape=[SDS((B, T, HG_WIDTH), F32), SDS((B, T, HG_WIDTH), BF16),
                   SDS((B, HG_HEADS, nc, HG_HEAD_DIM, HG_HEAD_DIM), F32)] + _exchange_shapes(ride_srcs, ride_modes),
        scratch_shapes=[pltpu.VMEM((HG_HEAD_DIM, HG_HEAD_DIM), F32), pltpu.VMEM((HG_ROWS, LANES), F32)]
        + _exchange_sems(nr),
        compiler_params=_params(("arbitrary", "arbitrary"), VMEM_LIMIT_BIG),
    )(proj3, proj3, proj3, proj3, lb, hg_w, *ride_srcs)


def _mix_out(x2, attn_n, rec_g, mod8, post_w, w_out_bf, T):
    N = x2.shape[0]
    TM = _tile_rows(T)
    tps = T // TM

    def body(x_ref, an_ref, rg_ref, mod_ref, pw_ref, w_ref, mix_ref, x1_ref, cat_ref):
        cat = jnp.concatenate([an_ref[...], rg_ref[...]], axis=1)
        cat_ref[...] = cat
        mix = _dot(cat, w_ref[...])
        mix_ref[...] = mix
        r = lax.rsqrt(_mean_last(mix * mix) + EPS)
        x1_ref[...] = x_ref[...] + mod_ref[2:3, :] * (mix * r * pw_ref[...])

    row = lambda w: pl.BlockSpec((TM, w), lambda i: (i, 0))
    return pl.pallas_call(
        body, name="mix_out", grid=(N // TM,),
        in_specs=[row(D_MODEL), row(ATT_WIDTH), row(HG_WIDTH), _mod_spec(tps),
                  pl.BlockSpec((1, D_MODEL), lambda i: (0, 0)),
                  pl.BlockSpec((D_MODEL, D_MODEL), lambda i: (0, 0))],
        out_specs=[row(D_MODEL), row(D_MODEL), row(D_MODEL)],
        out_shape=[SDS((N, D_MODEL), F32), SDS((N, D_MODEL), F32), SDS((N, D_MODEL), BF16)],
        compiler_params=_params(("arbitrary",), VMEM_LIMIT_BIG),
    )(x2, attn_n, rec_g, mod8, post_w, w_out_bf)


def _load_weights_once(pairs, sem):
    @pl.when(pl.program_id(0) == 0)
    def _():
        cps = [pltpu.make_async_copy(src, dst, sem.at[i]) for i, (src, dst) in enumerate(pairs)]
        for cp in cps:
            cp.start()
        for cp in cps:
            cp.wait()


def _mlp_fwd(x1, mod8, pre_w, w_up_bf, w_down_bf, T):
    N = x1.shape[0]
    TM = _tile_rows(T)
    tps = T // TM

    def body(x_ref, mod_ref, pw_ref, wu_hbm, wd_hbm, up_ref, d_ref, h2_ref, wu, wd, sem):
        _load_weights_once([(wu_hbm, wu), (wd_hbm, wd)], sem)
        x = x_ref[...]
        r = lax.rsqrt(_mean_last(x * x) + EPS)
        h = (x * r * pw_ref[...]) * (1.0 + mod_ref[4:5, :]) + mod_ref[3:4, :]
        hb = _bf(h)
        h2_ref[...] = hb
        up = _dot(hb, wu[...])
        up_ref[...] = up
        ru = jnp.maximum(up, 0.0)
        d_ref[...] = _dot(_bf(ru * ru), wd[...])

    row = lambda w: pl.BlockSpec((TM, w), lambda i: (i, 0))
    return pl.pallas_call(
        body, name="mlp_fwd", grid=(N // TM,),
        in_specs=[row(D_MODEL), _mod_spec(tps), pl.BlockSpec((1, D_MODEL), lambda i: (0, 0)),
                  pl.BlockSpec(memory_space=pl.ANY), pl.BlockSpec(memory_space=pl.ANY)],
        out_specs=[row(D_FF), row(D_MODEL), row(D_MODEL)],
        out_shape=[SDS((N, D_FF), F32), SDS((N, D_MODEL), F32), SDS((N, D_MODEL), BF16)],
        scratch_shapes=[pltpu.VMEM((D_MODEL, D_FF), BF16), pltpu.VMEM((D_FF, D_MODEL), BF16),
                        pltpu.SemaphoreType.DMA((2,))],
        compiler_params=_params(("arbitrary",), VMEM_LIMIT_BIG),
    )(x1, mod8, pre_w, w_up_bf, w_down_bf)


def _acc_rows(acc_ref, first, rows):
    @pl.when(first)
    def _():
        acc_ref[...] = jnp.zeros(acc_ref.shape, F32)
    for i, r in enumerate(rows):
        acc_ref[i:i + 1, :] += r


def _mlp_bwd(x1, d, up, tgt, mod8, pre_w, post_w, w_down_t_bf, w_up_t_bf, T):
    N = x1.shape[0]
    TM = _tile_rows(T)
    tps = T // TM

    def body(x_ref, d_ref, up_ref, t_ref, mod_ref, pw_ref, qw_ref, wdt_hbm, wut_hbm,
             dx_ref, u_ref, dup_ref, dd_ref, acc_ref, wdt, wut, sem):
        _load_weights_once([(wdt_hbm, wdt), (wut_hbm, wut)], sem)
        sh2, sc2, g2 = mod_ref[3:4, :], mod_ref[4:5, :], mod_ref[5:6, :]
        x = x_ref[...]
        r1 = lax.rsqrt(_mean_last(x * x) + EPS)
        xh = x * r1
        n2 = xh * pw_ref[...]
        dv = d_ref[...]
        rd = lax.rsqrt(_mean_last(dv * dv) + EPS)
        dh = dv * rd
        rr = dh * qw_ref[...]
        e = x + g2 * rr - t_ref[...]
        loss = 0.5 * jnp.sum(_sum_rows(e * e), axis=1, keepdims=True) / D_MODEL
        dy = e * (1.0 / D_MODEL)
        dg2 = _sum_rows(dy * rr)
        drr = dy * g2
        dw_post = _sum_rows(drr * dh)
        ddh = drr * qw_ref[...]
        dd = _bf(rd * (ddh - dh * _mean_last(ddh * dh)))
        dd_ref[...] = dd
        ru = jnp.maximum(up_ref[...], 0.0)
        u_ref[...] = _bf(ru * ru)
        dup = _bf(_dot(dd, wdt[...]) * (2.0 * ru))
        dup_ref[...] = dup
        dh2 = _dot(dup, wut[...])
        dsh2 = _sum_rows(dh2)
        dsc2 = _sum_rows(dh2 * n2)
        dn2 = dh2 * (1.0 + sc2)
        dw_pre = _sum_rows(dn2 * xh)
        dxh = dn2 * pw_ref[...]
        dx_ref[...] = dy + r1 * (dxh - xh * _mean_last(dxh * xh))
        _acc_rows(acc_ref, pl.program_id(0) % tps == 0,
                  [dsh2, dsc2, dg2, dw_pre, dw_post, jnp.broadcast_to(loss, (1, D_MODEL))])

    row = lambda w: pl.BlockSpec((TM, w), lambda i: (i, 0))
    vec = pl.BlockSpec((1, D_MODEL), lambda i: (0, 0))
    B = N // T
    return pl.pallas_call(
        body, name="mlp_bwd", grid=(N // TM,),
        in_specs=[row(D_MODEL), row(D_MODEL), row(D_FF), row(D_MODEL), _mod_spec(tps), vec, vec,
                  pl.BlockSpec(memory_space=pl.ANY), pl.BlockSpec(memory_space=pl.ANY)],
        out_specs=[row(D_MODEL), row(D_FF), row(D_FF), row(D_MODEL), _mod_spec(tps)],
        out_shape=[SDS((N, D_MODEL), F32), SDS((N, D_FF), BF16), SDS((N, D_FF), BF16),
                   SDS((N, D_MODEL), BF16), SDS((B, 8, D_MODEL), F32)],
        scratch_shapes=[pltpu.VMEM((D_MODEL, D_FF), BF16), pltpu.VMEM((D_FF, D_MODEL), BF16),
                        pltpu.SemaphoreType.DMA((2,))],
        compiler_params=_params(("arbitrary",), VMEM_LIMIT_BIG),
    )(x1, d, up, tgt, mod8, pre_w, post_w, w_down_t_bf, w_up_t_bf)


def _mix_bwd(mix, dx1, mod8, post_w, w_out_t_bf, T):
    N = mix.shape[0]
    TM = _tile_rows(T)
    tps = T // TM

    def body(mix_ref, dx_ref, mod_ref, pw_ref, w_ref, dan_ref, drg_ref, dmix_ref, acc_ref):
        g1 = mod_ref[2:3, :]
        mix = mix_ref[...]
        dx1 = dx_ref[...]
        rm = lax.rsqrt(_mean_last(mix * mix) + EPS)
        mh = mix * rm
        dg1 = _sum_rows(dx1 * (mh * pw_ref[...]))
        dr = dx1 * g1
        dw_post = _sum_rows(dr * mh)
        dmh = dr * pw_ref[...]
        dmix = _bf(rm * (dmh - mh * _mean_last(dmh * mh)))
        dmix_ref[...] = dmix
        dcat = _dot(dmix, w_ref[...])
        dan_ref[...] = dcat[:, :ATT_WIDTH]
        drg_ref[...] = dcat[:, ATT_WIDTH:]
        _acc_rows(acc_ref, pl.program_id(0) % tps == 0, [dg1, dw_post])

    row = lambda w: pl.BlockSpec((TM, w), lambda i: (i, 0))
    B = N // T
    return pl.pallas_call(
        body, name="mix_bwd", grid=(N // TM,),
        in_specs=[row(D_MODEL), row(D_MODEL), _mod_spec(tps), pl.BlockSpec((1, D_MODEL), lambda i: (0, 0)),
                  pl.BlockSpec((D_MODEL, D_MODEL), lambda i: (0, 0))],
        out_specs=[row(ATT_WIDTH), row(HG_WIDTH), row(D_MODEL), _mod_spec(tps)],
        out_shape=[SDS((N, ATT_WIDTH), F32), SDS((N, HG_WIDTH), F32), SDS((N, D_MODEL), BF16),
                   SDS((B, 8, D_MODEL), F32)],
        compiler_params=_params(("arbitrary",), VMEM_LIMIT_BIG),
    )(mix, dx1, mod8, post_w, w_out_t_bf)


def _hgrn_bwd(proj3, lb, hg_w, o, s_prev, drg, ride_srcs, ride_modes):
    B, T, _ = proj3.shape
    nc = T // HG_CHUNK
    ng = T // HG_ROWS
    nr = len(ride_srcs)
    head = _hgrn_specs(B, T)

    def body(*refs):
        hq_ref, hf_ref, hi_ref, hg_ref, lb_ref, gw_ref, o_ref, sp_ref, drg_ref = refs[:9]
        ride_in = refs[9:9 + nr]
        dhq_ref, dhf_ref, dhi_ref, dhg_ref, dlb_ref, dgw_ref = refs[9 + nr:15 + nr]
        ride_out = refs[15 + nr:15 + 2 * nr]
        dst, ebl_scr = refs[15 + 2 * nr:17 + 2 * nr]
        sems = refs[17 + 2 * nr:]
        first = (pl.program_id(0) == 0) & (pl.program_id(1) == 0)
        last = (pl.program_id(0) == B - 1) & (pl.program_id(1) == HG_HEADS - 1)
        _ride_start(ride_modes, first, ride_in, ride_out, sems)

        dst[...] = jnp.zeros((HG_HEAD_DIM, HG_HEAD_DIM), F32)
        lo, up, ups = _group_masks()
        sums_bf = jnp.concatenate([_ones_bf(lo), _ones_bf(ups)], axis=0)
        up_bf = _ones_bf(up)
        last_row = lax.broadcasted_iota(jnp.int32, (HG_CHUNK, LANES), 0) == HG_CHUNK - 1
        lbv = lb_ref[...]
        gw = gw_ref[...]

        def group(i, carry):
            dlb, dgw = carry
            gi = ng - 1 - i
            rows = pl.ds(pl.multiple_of(gi * HG_ROWS, HG_ROWS), HG_ROWS)
            hq = hq_ref[rows, :]
            gt = _hgrn_gates(hq, hf_ref[rows, :], lbv, sums_bf, ebl_scr)
            sq, sg, qdf, kdf, k2f = gt["sq"], gt["sg"], gt["qd"], gt["kd"], gt["k2"]
            v, qd, kd, k2 = _bf(hi_ref[rows, :]), _bf(qdf), _bf(kdf), _bf(k2f)
            ov = o_ref[rows, :]
            hg = hg_ref[rows, :]
            shg = _sigmoid(hg)
            dr = drg_ref[rows, :]
            ro = lax.rsqrt(_mean_last(ov * ov) + EPS)
            oh = ov * ro
            dhg_ref[rows, :] = dr * (oh * gw) * (shg + hg * shg * (1.0 - shg))
            drn = dr * (hg * shg)
            dgw = dgw + _sum_rows(drn * oh)
            doh = drn * gw
            do = _bf(ro * (doh - oh * _mean_last(doh * oh)))
            a = jnp.where(lo, _dot_nt(qd, kd), 0.0)
            da = _bf(jnp.where(lo, _dot_nt(do, v), 0.0))
            dv = _dot_tn(_bf(a), do)
            dqd = _dot(da, kd)
            dkd = _dot_tn(da, qd)
            ds = dst[...]
            dk2_l, dv_l, dqd_l, dbl_l = [None] * HG_GROUP, [None] * HG_GROUP, [None] * HG_GROUP, [None] * HG_GROUP
            for c in reversed(range(HG_GROUP)):
                cr = _chunk_rows(c)
                sp = sp_ref[gi * HG_GROUP + c]
                ebl = ebl_scr[c * HG_CHUNK:c * HG_CHUNK + 1, :]
                dsb = _bf(ds)
                dk2_c = _dot(v[cr], dsb)
                dk2_l[c] = dk2_c
                dv_l[c] = _dot_nt(k2[cr], dsb)
                dqd_l[c] = _dot(do[cr], _bf(sp))
                dbl = _sum_rows(ds * sp) * ebl + _sum_rows(dk2_c * k2f[cr])
                dbl_l[c] = jnp.where(last_row, dbl, 0.0)
                ds = ds * ebl + _dot_tn(do[cr], qd[cr])
            dst[...] = ds
            dk2 = jnp.concatenate(dk2_l, axis=0)
            dhi_ref[rows, :] = dv + jnp.concatenate(dv_l, axis=0)
            dqd = dqd + jnp.concatenate(dqd_l, axis=0)
            db = dqd * qdf - dkd * kdf - dk2 * k2f + jnp.concatenate(dbl_l, axis=0)
            dk = dkd * gt["enb"] + dk2 * gt["e2"]
            df = _tri_sum(up_bf, db) / gt["f"] - dk
            dhf_ref[rows, :] = df * (1.0 - lbv) * sg * (1.0 - sg)
            dlb = dlb + _sum_rows(df * (1.0 - sg))
            dhq_ref[rows, :] = (dqd * gt["eb"]) * (sq + hq * sq * (1.0 - sq))
            return dlb, dgw

        zero = jnp.zeros((1, LANES), F32)
        dlb, dgw = lax.fori_loop(0, ng, group, (zero, zero))
        dlb_ref[...] = jnp.broadcast_to(dlb, (8, LANES))
        dgw_ref[...] = jnp.broadcast_to(dgw, (8, LANES))
        _ride_wait(ride_modes, last, ride_in, ride_out, sems)

    out_head = pl.BlockSpec((None, T, LANES), lambda b, h: (b, 0, h))
    small = pl.BlockSpec((None, 8, LANES), lambda b, h: (b, 0, h))
    return pl.pallas_call(
        body, name="hgrn_bwd", grid=(B, HG_HEADS),
        in_specs=[head(6), head(10), head(14), head(18),
                  pl.BlockSpec((1, LANES), lambda b, h: (0, h)),
                  pl.BlockSpec((1, LANES), lambda b, h: (0, 0)),
                  out_head,
                  pl.BlockSpec((None, None, nc, HG_HEAD_DIM, HG_HEAD_DIM), lambda b, h: (b, h, 0, 0, 0)),
                  out_head] + [ANY_SPEC] * nr,
        out_specs=[out_head, out_head, out_head, out_head, small, small] + [ANY_SPEC] * nr,
        out_shape=[SDS((B, T, HG_WIDTH), F32)] * 4 + [SDS((B, 8, HG_WIDTH), F32)] * 2
        + _exchange_shapes(ride_srcs, ride_modes),
        scratch_shapes=[pltpu.VMEM((HG_HEAD_DIM, HG_HEAD_DIM), F32), pltpu.VMEM((HG_ROWS, LANES), F32)]
        + _exchange_sems(nr),
        compiler_params=_params(("arbitrary", "arbitrary"), VMEM_LIMIT_BIG),
    )(proj3, proj3, proj3, proj3, lb, hg_w, o, s_prev, drg, *ride_srcs)


def _attn_bwd(qr, kr, proj3, attn_o, dan, tables, sinks, attn_w):
    B, T, _ = proj3.shape
    nb = T // WINDOW
    cos, sinl, sinr = tables
    QKV = ATT_WIDTH + 2 * LANES

    def body(qr_ref, kr_ref, v_ref, o_ref, dan_ref, cos_ref, sl_ref, sr_ref, sink_ref, aw_ref,
             dqkv_ref, dsink_ref, daw_ref, kpad, vpad, dkpad, dvpad, dqb, dsk):
        kpad[0:WINDOW, :] = jnp.zeros((WINDOW, LANES), BF16)
        vpad[0:WINDOW, :] = jnp.zeros((WINDOW, LANES), BF16)
        kpad[WINDOW:, :] = kr_ref[...]
        vpad[WINDOW:, :] = _bf(v_ref[...])
        dkpad[...] = jnp.zeros(dkpad.shape, F32)
        dvpad[...] = jnp.zeros(dvpad.shape, F32)
        dsk[...] = jnp.zeros(dsk.shape, F32)
        aw = aw_ref[...]

        def block(n, daw):
            r0 = pl.multiple_of(n * WINDOW, WINDOW)
            rows = pl.ds(r0, WINDOW)
            band = pl.ds(r0, 2 * WINDOW)
            ob = o_ref[rows, :]
            dn = dan_ref[rows, :]
            ro = lax.rsqrt(_mean_last(ob * ob) + EPS)
            oh = ob * ro
            daw = daw + _sum_rows(dn * oh)
            doh = dn * aw
            do = _bf(ro * (doh - oh * _mean_last(doh * oh)))
            doparts = [do[:, j * LANES:(j + 1) * LANES] for j in range(ATT_WIDTH // LANES)]
            qparts = [qr_ref[rows, j * LANES:(j + 1) * LANES] for j in range(ATT_WIDTH // LANES)]
            mask = _band_mask(n)
            for hk in range(ATT_KV_HEADS):
                lanes = slice(hk * ATT_HEAD_DIM, (hk + 1) * ATT_HEAD_DIM)
                qs = _stack_heads(qparts, hk)
                dos = _stack_heads(doparts, hk)
                kk, vv = kpad[band, lanes], vpad[band, lanes]
                p, psink = _softmax_band(qs, kk, mask, _sink_col(sink_ref, hk))
                dp = _dot_nt(dos, vv)
                delta = jnp.sum(p * dp, axis=-1, keepdims=True)
                ds = _bf(p * (dp - delta) * (ATT_HEAD_DIM ** -0.5))
                sk = psink * delta
                dqs = _dot(ds, kk)
                dkpad[band, lanes] += _dot_tn(ds, qs)
                dvpad[band, lanes] += _dot_tn(_bf(p), dos)
                for g in range(ATT_GROUP):
                    h = ATT_GROUP * hk + g
                    dqb[:, h * ATT_HEAD_DIM:(h + 1) * ATT_HEAD_DIM] = dqs[g * WINDOW:(g + 1) * WINDOW, :]
                    dsk[h:h + 1, :] += jnp.broadcast_to(-_sum_rows(sk[g * WINDOW:(g + 1) * WINDOW, :]), (1, LANES))
            cs, sl, sr = cos_ref[rows, :], sl_ref[rows, :], sr_ref[rows, :]
            for j in range(ATT_WIDTH // LANES):
                dqkv_ref[rows, j * LANES:(j + 1) * LANES] = _rope_t(dqb[:, j * LANES:(j + 1) * LANES], cs, sl, sr)
            return daw

        daw = lax.fori_loop(0, nb, block, jnp.zeros((1, ATT_WIDTH), F32))
        daw_ref[...] = jnp.broadcast_to(daw, (8, ATT_WIDTH))
        dsink_ref[...] = dsk[...]

        def finish(n, carry):
            r0 = pl.multiple_of(n * WINDOW, WINDOW)
            rows = pl.ds(r0, WINDOW)
            nxt = pl.ds(r0 + WINDOW, WINDOW)
            cs, sl, sr = cos_ref[rows, :], sl_ref[rows, :], sr_ref[rows, :]
            dqkv_ref[rows, ATT_WIDTH:ATT_WIDTH + LANES] = _rope_t(dkpad[nxt, :], cs, sl, sr)
            dqkv_ref[rows, ATT_WIDTH + LANES:QKV] = dvpad[nxt, :]
            return carry

        lax.fori_loop(0, nb, finish, 0)

    seq = lambda w, j: pl.BlockSpec((None, T, w), lambda b: (b, 0, j))
    full = lambda r, w: pl.BlockSpec((r, w), lambda b: (0, 0))
    return pl.pallas_call(
        body, name="attn_bwd", grid=(B,),
        in_specs=[seq(ATT_WIDTH, 0), seq(LANES, 0), seq(LANES, 5), seq(ATT_WIDTH, 0), seq(ATT_WIDTH, 0),
                  full(T, LANES), full(T, LANES), full(T, LANES),
                  pl.BlockSpec(memory_space=pltpu.SMEM), full(1, ATT_WIDTH)],
        out_specs=[seq(QKV, 0), pl.BlockSpec((None, 8, LANES), lambda b: (b, 0, 0)),
                   pl.BlockSpec((None, 8, ATT_WIDTH), lambda b: (b, 0, 0))],
        out_shape=[SDS((B, T, QKV), F32), SDS((B, 8, LANES), F32), SDS((B, 8, ATT_WIDTH), F32)],
        scratch_shapes=[pltpu.VMEM((T + WINDOW, LANES), BF16), pltpu.VMEM((T + WINDOW, LANES), BF16),
                        pltpu.VMEM((T + WINDOW, LANES), F32), pltpu.VMEM((T + WINDOW, LANES), F32),
                        pltpu.VMEM((WINDOW, ATT_WIDTH), F32), pltpu.VMEM((8, LANES), F32)],
        compiler_params=_params(("arbitrary",), VMEM_LIMIT_BIG),
    )(qr, kr, proj3, attn_o, dan, cos, sinl, sinr, sinks, attn_w)


def _in_bwd(x2, dx1, dqkv, dhq, dhf, dhi, dhg, mod8, pre_w, w_in_t_bf, T):
    N = x2.shape[0]
    TM = _tile_rows(T)
    tps = T // TM
    pieces = [(0, ATT_WIDTH + 2 * LANES), (768, HG_WIDTH), (1280, HG_WIDTH), (1792, HG_WIDTH), (2304, HG_WIDTH)]

    def body(x_ref, dx_ref, p0, p1, p2, p3, p4, mod_ref, pw_ref, w_ref, gx_ref, dproj_ref, acc_ref):
        sc1 = mod_ref[1:2, :]
        dh = jnp.zeros((TM, D_MODEL), F32)
        for ref, (off, width) in zip((p0, p1, p2, p3, p4), pieces):
            pb = _bf(ref[...])
            dproj_ref[:, off:off + width] = pb
            dh = dh + _dot(pb, w_ref[off:off + width, :])
        x = x_ref[...]
        r = lax.rsqrt(_mean_last(x * x) + EPS)
        xh = x * r
        n1 = xh * pw_ref[...]
        dsh1 = _sum_rows(dh)
        dsc1 = _sum_rows(dh * n1)
        dn1 = dh * (1.0 + sc1)
        dw_pre = _sum_rows(dn1 * xh)
        dxh = dn1 * pw_ref[...]
        gx_ref[...] = dx_ref[...] + r * (dxh - xh * _mean_last(dxh * xh))
        _acc_rows(acc_ref, pl.program_id(0) % tps == 0, [dsh1, dsc1, dw_pre])

    row = lambda w: pl.BlockSpec((TM, w), lambda i: (i, 0))
    B = N // T
    return pl.pallas_call(
        body, name="in_bwd", grid=(N // TM,),
        in_specs=[row(D_MODEL), row(D_MODEL), row(768), row(HG_WIDTH), row(HG_WIDTH), row(HG_WIDTH),
                  row(HG_WIDTH), _mod_spec(tps), pl.BlockSpec((1, D_MODEL), lambda i: (0, 0)),
                  pl.BlockSpec((IN_COLS, D_MODEL), lambda i: (0, 0))],
        out_specs=[row(D_MODEL), row(IN_COLS), _mod_spec(tps)],
        out_shape=[SDS((N, D_MODEL), F32), SDS((N, IN_COLS), BF16), SDS((B, 8, D_MODEL), F32)],
        compiler_params=_params(("arbitrary",), VMEM_LIMIT_BIG),
    )(x2, dx1, dqkv, dhq, dhf, dhi, dhg, mod8, pre_w, w_in_t_bf)


def _matmul_tn(name, a, b, tn, by_columns=False):
    K, M = a.shape
    Nc = b.shape[1]
    tm = min(512, M)

    def body(a_ref, b_ref, o_ref):
        o_ref[...] = _bf(_dot_tn(a_ref[...], b_ref[...]))

    if by_columns:
        out_shape = SDS((Nc // tn, M, tn), BF16)
        out_spec = pl.BlockSpec((None, tm, tn), lambda i, j: (j, i, 0))
    else:
        out_shape = SDS((M, Nc), BF16)
        out_spec = pl.BlockSpec((tm, tn), lambda i, j: (i, j))
    return pl.pallas_call(
        body, name=name, grid=(M // tm, Nc // tn),
        in_specs=[pl.BlockSpec((K, tm), lambda i, j: (0, i)),
                  pl.BlockSpec((K, tn), lambda i, j: (0, j))],
        out_specs=out_spec, out_shape=out_shape,
        compiler_params=_params(("arbitrary", "arbitrary"), VMEM_LIMIT_BIG),
    )(a, b)


def _adamw_math(w, g, m, v):
    m2 = ADAM_B1 * m + (1.0 - ADAM_B1) * g
    v2 = ADAM_B2 * v + (1.0 - ADAM_B2) * (g * g)
    m_hat = m2 / (1.0 - ADAM_B1 ** ADAM_STEP)
    v_hat = v2 / (1.0 - ADAM_B2 ** ADAM_STEP)
    delta = -ADAM_LR * (m_hat / (jnp.sqrt(v_hat) + ADAM_EPS) + ADAM_WD * w)
    return delta, m2, v2


def _reduce_adamw(name, parts, w, m, v):
    r, c = w.shape
    tr = r if r <= 256 else 256

    def body(p_ref, w_ref, m_ref, v_ref, g_ref, d_ref, m2_ref, v2_ref):
        g = p_ref[0].astype(F32)
        for s in range(1, N_DEV):
            g = g + p_ref[s].astype(F32)
        g_ref[...] = g
        d_ref[...], m2_ref[...], v2_ref[...] = _adamw_math(w_ref[...], g, m_ref[...], v_ref[...])

    blk = pl.BlockSpec((tr, c), lambda i: (i, 0))
    return pl.pallas_call(
        body, name=name, grid=(r // tr,),
        in_specs=[pl.BlockSpec((N_DEV, tr, c), lambda i: (0, i, 0)), blk, blk, blk],
        out_specs=[blk] * 4, out_shape=[SDS((r, c), F32)] * 4,
        compiler_params=_params(("arbitrary",), VMEM_LIMIT_BIG),
    )(parts, w, m, v)


def _ada_grad_adamw(c_all, dmod_all, w, m, v):
    r, c = w.shape
    tr = 256
    nb = c_all.shape[0]

    def body(c_ref, dm_ref, w_ref, m_ref, v_ref, g_ref, d_ref, m2_ref, v2_ref):
        cv = c_ref[...]
        g = _dot_tn(cv * _sigmoid(cv), dm_ref[...])
        g_ref[...] = g
        d_ref[...], m2_ref[...], v2_ref[...] = _adamw_math(w_ref[...], g, m_ref[...], v_ref[...])

    blk = pl.BlockSpec((tr, c), lambda i: (i, 0))
    return pl.pallas_call(
        body, name="ada_grad_adamw", grid=(r // tr,),
        in_specs=[pl.BlockSpec((nb, tr), lambda i: (0, i)), pl.BlockSpec((nb, c), lambda i: (0, 0)),
                  blk, blk, blk],
        out_specs=[blk] * 4, out_shape=[SDS((r, c), F32)] * 4,
        compiler_params=_params(("arbitrary",)),
    )(c_all, dmod_all, w, m, v)


_SMALL = [("b_ada", 6144), ("pre_w_mix", 1024), ("attn_sinks", 128), ("attn_out_w", 512), ("lb_table", 1024),
          ("hg_norm_w", 128), ("post_w_mix", 1024), ("pre_w_mlp", 1024), ("post_w_mlp", 1024)]


def _pack_small(vals):
    out = []
    for name, width in _SMALL:
        f = vals[name].reshape(-1).astype(F32)
        out.append(jnp.pad(f, (0, width - f.shape[0])))
    return jnp.concatenate(out).reshape(1, -1)


def _unpack_small(vec, shapes):
    out, off = {}, 0
    for name, width in _SMALL:
        n = 1
        for s in shapes[name]:
            n *= s
        out[name] = vec[0, off:off + n].reshape(shapes[name])
        off += width
    return out


def _columns_to_full(g):
    return g.transpose(1, 0, 2).reshape(g.shape[1], -1)


def kernel(x, c, w_ada, b_ada, pre_w_mix, w_in, attn_sinks, attn_out_w, lb_table, hg_norm_w, w_out, post_w_mix, pre_w_mlp, w_up, w_down, post_w_mlp, loss_target, m_w_ada, m_b_ada, m_pre_w_mix, m_w_in, m_attn_sinks, m_attn_out_w, m_lb_table, m_hg_norm_w, m_w_out, m_post_w_mix, m_pre_w_mlp, m_w_up, m_w_down, m_post_w_mlp, v_w_ada, v_b_ada, v_pre_w_mix, v_w_in, v_attn_sinks, v_attn_out_w, v_lb_table, v_hg_norm_w, v_w_out, v_post_w_mix, v_pre_w_mlp, v_w_up, v_w_down, v_post_w_mlp):
    B, T, _ = x.shape
    N = B * T
    me = 4 * lax.axis_index("x") + 2 * lax.axis_index("y") + lax.axis_index("c")
    x2 = x.reshape(N, D_MODEL)
    tgt2 = loss_target.reshape(N, D_MODEL)

    w_in_g, c_g = _exchange("gather_w_in", [_bf(w_in[0]), c], ["gather"] * 2)
    w_in_f = _columns_to_full(w_in_g)
    c_all = c_g.reshape(N_DEV * B, D_MODEL)

    ada_cols = w_ada.shape[2]
    b_mine = lax.dynamic_slice(b_ada, (0, me * ada_cols), (1, ada_cols))
    mod_cols = _ada_mod(c_all, w_ada[0], b_mine)
    (mod_g,) = _exchange("scatter_mod", [mod_cols.reshape(N_DEV, B, ada_cols)], ["a2a"])
    mod = mod_g.transpose(1, 0, 2).reshape(B, 6, D_MODEL)
    mod8 = jnp.pad(mod, ((0, 0), (0, 2), (0, 0)))

    lb_p = jax.nn.softmax(lb_table, axis=0)
    lb = lb_p[1:2]
    tables = _rope_tables(T)

    proj, h1 = _in_proj(x2, mod8, pre_w_mix, w_in_f, T)
    proj3 = proj.reshape(B, T, IN_COLS)
    attn_o, attn_n, qr, kr = _attn_fwd(proj3, tables, attn_sinks, attn_out_w)
    rec_o, rec_g, s_prev, w_out_g, w_up_g, w_down_g = _hgrn_fwd(
        proj3, lb, hg_norm_w, [_bf(w_out[0]), _bf(w_up[0]), _bf(w_down[0])], ["gather"] * 3)
    w_out_f = w_out_g.reshape(D_MODEL, D_MODEL)
    w_up_f = _columns_to_full(w_up_g)
    w_down_f = w_down_g.reshape(D_FF, D_MODEL)
    mix, x1, cat = _mix_out(x2, attn_n.reshape(N, ATT_WIDTH), rec_g.reshape(N, HG_WIDTH), mod8,
                            post_w_mix, w_out_f, T)
    up, d, h2 = _mlp_fwd(x1, mod8, pre_w_mlp, w_up_f, w_down_f, T)

    dx1, u, dup, dd, acc_mlp = _mlp_bwd(x1, d, up, tgt2, mod8, pre_w_mlp, post_w_mlp,
                                        w_down_f.T, w_up_f.T, T)
    dan, drg, dmix, acc_mix = _mix_bwd(mix, dx1, mod8, post_w_mix, w_out_f.T, T)
    gw_out = _matmul_tn("grad_w_out", cat, dmix, 512).reshape(N_DEV, D_MODEL // N_DEV, D_MODEL)
    gw_up = _matmul_tn("grad_w_up", h2, dup, D_FF // N_DEV, by_columns=True)
    gw_down = _matmul_tn("grad_w_down", u, dd, 512).reshape(N_DEV, D_FF // N_DEV, D_MODEL)
    dhq, dhf, dhi, dhg, dlb_p, dgw_p, r_out, r_up, r_down = _hgrn_bwd(
        proj3, lb, hg_norm_w, rec_o, s_prev, drg.reshape(B, T, HG_WIDTH),
        [gw_out, gw_up, gw_down], ["a2a"] * 3)
    dqkv, dsink_p, daw_p = _attn_bwd(qr, kr, proj3, attn_o, dan.reshape(B, T, ATT_WIDTH), tables,
                                     attn_sinks, attn_out_w)
    flat = lambda a: a.reshape(N, a.shape[-1])
    grad_x, dproj, acc_in = _in_bwd(x2, dx1, flat(dqkv), flat(dhq), flat(dhf), flat(dhi), flat(dhg),
                                    mod8, pre_w_mix, w_in_f.T, T)

    gw_in = _matmul_tn("grad_w_in", h1, dproj, IN_COLS // 2)
    in_cols = w_in.shape[2]
    gw_in = gw_in.reshape(D_MODEL, N_DEV, in_cols).transpose(1, 0, 2)

    dmod = jnp.concatenate([acc_in[:, 0:2], acc_mix[:, 0:1], acc_mlp[:, 0:3]], axis=1)
    dlb = dlb_p[:, 0].sum(0)
    dlb_table = jnp.stack([-dlb, dlb]) * (lb_p[0] * lb_p[1])[None, :]
    small = {
        "b_ada": dmod.sum(0),
        "pre_w_mix": acc_in[:, 2].sum(0),
        "attn_sinks": dsink_p[:, :, 0].sum(0),
        "attn_out_w": daw_p[:, 0].sum(0),
        "lb_table": dlb_table,
        "hg_norm_w": dgw_p[:, 0].reshape(B, HG_HEADS, LANES).sum((0, 1)),
        "post_w_mix": acc_mix[:, 1].sum(0),
        "pre_w_mlp": acc_mlp[:, 3].sum(0),
        "post_w_mlp": acc_mlp[:, 4].sum(0),
    }
    loss_part = acc_mlp[:, 5, 0].sum()
    dmod_blocks = dmod.reshape(B, N_DEV, ada_cols).transpose(1, 0, 2)

    r_in, r_dmod, r_small = _exchange(
        "reduce_grads", [gw_in, dmod_blocks, _pack_small(small)], ["a2a", "a2a", "gather"])

    res = {}
    res["w_in"] = _reduce_adamw("adamw_w_in", r_in, w_in[0], m_w_in[0], v_w_in[0])
    res["w_out"] = _reduce_adamw("adamw_w_out", r_out, w_out[0], m_w_out[0], v_w_out[0])
    res["w_up"] = _reduce_adamw("adamw_w_up", r_up, w_up[0], m_w_up[0], v_w_up[0])
    res["w_down"] = _reduce_adamw("adamw_w_down", r_down, w_down[0], m_w_down[0], v_w_down[0])
    res["w_ada"] = _ada_grad_adamw(c_all, r_dmod.reshape(N_DEV * B, ada_cols), w_ada[0], m_w_ada[0], v_w_ada[0])

    given = dict(b_ada=(b_ada, m_b_ada, v_b_ada), pre_w_mix=(pre_w_mix, m_pre_w_mix, v_pre_w_mix),
                 attn_sinks=(attn_sinks, m_attn_sinks, v_attn_sinks),
                 attn_out_w=(attn_out_w, m_attn_out_w, v_attn_out_w), lb_table=(lb_table, m_lb_table, v_lb_table),
                 hg_norm_w=(hg_norm_w, m_hg_norm_w, v_hg_norm_w), post_w_mix=(post_w_mix, m_post_w_mix, v_post_w_mix),
                 pre_w_mlp=(pre_w_mlp, m_pre_w_mlp, v_pre_w_mlp), post_w_mlp=(post_w_mlp, m_post_w_mlp, v_post_w_mlp))
    shapes = {k: t[0].shape for k, t in given.items()}
    packed = [_pack_small({k: t[i] for k, t in given.items()}) for i in range(3)]
    small_res = _reduce_adamw("adamw_small", r_small, *packed)
    small_out = [_unpack_small(a, shapes) for a in small_res]
    for k in given:
        res[k] = tuple(so[k] for so in small_out)

    loss = lax.psum(loss_part, ("x", "y", "c"))
    order = ["w_ada", "b_ada", "pre_w_mix", "w_in", "attn_sinks", "attn_out_w", "lb_table", "hg_norm_w", "w_out",
             "post_w_mix", "pre_w_mlp", "w_up", "w_down", "post_w_mlp"]
    big = {"w_ada", "w_in", "w_out", "w_up", "w_down"}
    outs = [loss, grad_x.reshape(B, T, D_MODEL)]
    for i in range(4):
        for k in order:
            a = res[k][i]
            outs.append(a[None] if k in big else a)
    return tuple(outs)
```

```python
import functools

import jax
import jax.numpy as jnp
from jax import lax
from jax.experimental import pallas as pl
from jax.experimental.pallas import tpu as pltpu

F32 = jnp.float32
BF16 = jnp.bfloat16
SDS = jax.ShapeDtypeStruct

D_MODEL = 1024
ATT_WIDTH = 512
ATT_HEAD_DIM = 64
ATT_KV_HEADS = 2
ATT_GROUP = 4
WINDOW = 128
ROPE_DIM = 16
ROPE_THETA = 500000.0
HG_WIDTH = 512
HG_HEAD_DIM = 128
HG_HEADS = 4
HG_CHUNK = 32
IN_COLS = 2816
D_FF = 4096
EPS = 1e-6
N_DEV = 8

ADAM_LR = 0.001
ADAM_B1 = 0.9
ADAM_B2 = 0.999
ADAM_EPS = 1e-08
ADAM_WD = 0.01
ADAM_STEP = 10

VMEM_LIMIT_BIG = 56 << 20
LANES = 128

MESH = pl.DeviceIdType.MESH
NT_DIMS = (((1,), (1,)), ((), ()))
TN_DIMS = (((0,), (0,)), ((), ()))


def _dot(a, b):
    return jnp.dot(a, b, preferred_element_type=F32)


def _dot_nt(a, b):
    return lax.dot_general(a, b, NT_DIMS, preferred_element_type=F32)


def _dot_tn(a, b):
    return lax.dot_general(a, b, TN_DIMS, preferred_element_type=F32)


def _bf(a):
    return a.astype(BF16)


def _sigmoid(a):
    return 1.0 / (1.0 + jnp.exp(-a))


def _mean_last(a):
    return jnp.mean(a, axis=-1, keepdims=True)


def _sum_rows(a):
    return jnp.sum(a, axis=0, keepdims=True)


def _tri_sum(tri_bf, a):
    a1 = _bf(a)
    r1 = a - a1.astype(F32)
    a2 = _bf(r1)
    a3 = _bf(r1 - a2.astype(F32))
    return _dot(tri_bf, a1) + _dot(tri_bf, a2) + _dot(tri_bf, a3)


def _params(sem=None, vmem=None):
    kw = {}
    if sem is not None:
        kw["dimension_semantics"] = sem
    if vmem is not None:
        kw["vmem_limit_bytes"] = vmem
    return pltpu.CompilerParams(**kw)


ANY_SPEC = pl.BlockSpec(memory_space=pl.ANY)


def _exchange_shapes(srcs, modes):
    out_shape = []
    for s, m in zip(srcs, modes):
        shp = (N_DEV,) + tuple(s.shape) if m == "gather" else tuple(s.shape)
        out_shape.append(SDS(shp, s.dtype))
    return out_shape


def _exchange_sems(n):
    return [pltpu.SemaphoreType.DMA((n, N_DEV - 1)), pltpu.SemaphoreType.DMA((n, N_DEV - 1)),
            pltpu.SemaphoreType.DMA((n,))]


def _exchange_copies(modes, src_refs, out_refs, send_sems, recv_sems, own_sems):
    x, y, c = lax.axis_index("x"), lax.axis_index("y"), lax.axis_index("c")
    me = 4 * x + 2 * y + c
    own, sends, recvs = [], [], []
    for a, mode in enumerate(modes):
        gather = mode == "gather"
        mine = src_refs[a] if gather else src_refs[a].at[me]
        own.append(pltpu.make_async_copy(mine, out_refs[a].at[me], own_sems.at[a]))
        for k in range(1, N_DEV):
            px, py, pc = x ^ ((k >> 2) & 1), y ^ ((k >> 1) & 1), c ^ (k & 1)
            peer = 4 * px + 2 * py + pc
            src = src_refs[a] if gather else src_refs[a].at[peer]
            sends.append(pltpu.make_async_remote_copy(
                src_ref=src, dst_ref=out_refs[a].at[me],
                send_sem=send_sems.at[a, k - 1], recv_sem=recv_sems.at[a, k - 1],
                device_id=(px, py, pc), device_id_type=MESH))
            recvs.append(pltpu.make_async_remote_copy(
                src_ref=src, dst_ref=out_refs[a].at[peer],
                send_sem=send_sems.at[a, k - 1], recv_sem=recv_sems.at[a, k - 1],
                device_id=(px, py, pc), device_id_type=MESH))
    return own, sends, recvs


def _exchange_start(copies):
    own, sends, _ = copies
    for cp in own + sends:
        cp.start()


def _exchange_wait(copies):
    own, sends, recvs = copies
    for cp in recvs:
        cp.wait_recv()
    for cp in sends:
        cp.wait_send()
    for cp in own:
        cp.wait()


def _exchange(name, srcs, modes):
    n = len(srcs)

    def body(*refs):
        copies = _exchange_copies(modes, refs[:n], refs[n:2 * n], *refs[2 * n:])
        _exchange_start(copies)
        _exchange_wait(copies)

    return pl.pallas_call(
        body, name=name, out_shape=_exchange_shapes(srcs, modes),
        in_specs=[ANY_SPEC] * n, out_specs=[ANY_SPEC] * n,
        scratch_shapes=_exchange_sems(n),
    )(*srcs)


def _ride_start(modes, first, src_refs, out_refs, sems):
    @pl.when(first)
    def _():
        _exchange_start(_exchange_copies(modes, src_refs, out_refs, *sems))


def _ride_wait(modes, last, src_refs, out_refs, sems):
    @pl.when(last)
    def _():
        _exchange_wait(_exchange_copies(modes, src_refs, out_refs, *sems))


def _ada_mod(c_all, w_ada, b_ada_mine):
    nb, cols = c_all.shape[0], w_ada.shape[1]

    def body(c_ref, w_ref, b_ref, o_ref):
        cv = c_ref[...]
        ca = cv * _sigmoid(cv)
        o_ref[...] = _dot(ca, w_ref[...]) + b_ref[...]

    return pl.pallas_call(body, name="ada_mod", out_shape=SDS((nb, cols), F32))(c_all, w_ada, b_ada_mine)


def _tile_rows(T):
    return min(256, T)


def _mod_spec(tps):
    return pl.BlockSpec((None, 8, D_MODEL), lambda i: (i // tps, 0, 0))


def _in_proj(x2, mod8, pre_w, w_in_bf, T):
    N = x2.shape[0]
    TM = _tile_rows(T)
    tps = T // TM

    def body(x_ref, mod_ref, pw_ref, w_ref, proj_ref, h1_ref):
        x = x_ref[...]
        r = lax.rsqrt(_mean_last(x * x) + EPS)
        h = (x * r * pw_ref[...]) * (1.0 + mod_ref[1:2, :]) + mod_ref[0:1, :]
        hb = _bf(h)
        h1_ref[...] = hb
        proj_ref[...] = _dot(hb, w_ref[...])

    return pl.pallas_call(
        body, name="in_proj", grid=(N // TM,),
        in_specs=[pl.BlockSpec((TM, D_MODEL), lambda i: (i, 0)), _mod_spec(tps),
                  pl.BlockSpec((1, D_MODEL), lambda i: (0, 0)),
                  pl.BlockSpec((D_MODEL, IN_COLS), lambda i: (0, 0))],
        out_specs=[pl.BlockSpec((TM, IN_COLS), lambda i: (i, 0)),
                   pl.BlockSpec((TM, D_MODEL), lambda i: (i, 0))],
        out_shape=[SDS((N, IN_COLS), F32), SDS((N, D_MODEL), BF16)],
        compiler_params=_params(("arbitrary",), VMEM_LIMIT_BIG),
    )(x2, mod8, pre_w, w_in_bf)


def _rope_tables(T):
    half = ROPE_DIM // 2
    inv_freq = ROPE_THETA ** (-jnp.arange(0, ROPE_DIM, 2, dtype=F32) / ROPE_DIM)
    ang = jnp.arange(T, dtype=F32)[:, None] * inv_freq[None, :]
    cos, sin = jnp.cos(ang), jnp.sin(ang)
    ones = jnp.ones((T, ATT_HEAD_DIM - ROPE_DIM), F32)
    zeros = jnp.zeros((T, ATT_HEAD_DIM - ROPE_DIM), F32)
    zh = jnp.zeros((T, half), F32)
    cos64 = jnp.concatenate([cos, cos, ones], axis=1)
    sin_left = jnp.concatenate([-sin, zh, zeros], axis=1)
    sin_right = jnp.concatenate([zh, sin, zeros], axis=1)
    rep = LANES // ATT_HEAD_DIM
    return jnp.tile(cos64, (1, rep)), jnp.tile(sin_left, (1, rep)), jnp.tile(sin_right, (1, rep))


def _rope(xc, cs, sl, sr):
    return xc * cs + pltpu.roll(xc, LANES - 8, 1) * sl + pltpu.roll(xc, 8, 1) * sr


def _rope_t(dy, cs, sl, sr):
    return dy * cs + pltpu.roll(dy * sl, 8, 1) + pltpu.roll(dy * sr, LANES - 8, 1)


ATT_SCALE = ATT_HEAD_DIM ** -0.5


def _band_masks():
    cols = ATT_GROUP * WINDOW
    j = lax.broadcasted_iota(jnp.int32, (2 * WINDOW, cols), 0)
    i = lax.broadcasted_iota(jnp.int32, (2 * WINDOW, cols), 1) & (WINDOW - 1)
    diff = i + WINDOW - j
    return (diff >= 0) & (diff < WINDOW), j >= WINDOW


def _sink_row(sink_ref, hk):
    return jnp.concatenate(
        [jnp.full((1, WINDOW), sink_ref[0, ATT_GROUP * hk + g], F32) for g in range(ATT_GROUP)], axis=1)


def _softmax_band(qs, kk, mask, sink):
    s = jnp.where(mask, _dot_nt(kk, qs), jnp.finfo(F32).min)
    m = jnp.maximum(jnp.max(s, axis=0, keepdims=True), sink)
    p = jnp.exp(s - m)
    es = jnp.exp(sink - m)
    inv = 1.0 / (jnp.sum(p, axis=0, keepdims=True) + es)
    return p, inv, es


def _stack_heads(parts, hk):
    hs = []
    for g in range(ATT_GROUP):
        h = ATT_GROUP * hk + g
        hs.append(parts[h // 2][:, (h % 2) * ATT_HEAD_DIM:(h % 2 + 1) * ATT_HEAD_DIM])
    return jnp.concatenate(hs, axis=0)


def _attn_fwd(proj3, tables, sinks, attn_w, ride_srcs, ride_modes):
    B, T, _ = proj3.shape
    nb = T // WINDOW
    nr = len(ride_srcs)
    cos, sinl, sinr = tables

    def body(*refs):
        q_ref, k_ref, v_ref, cos_ref, sl_ref, sr_ref, sink_ref, aw_ref = refs[:8]
        ride_in = refs[8:8 + nr]
        o_ref, an_ref, qr_ref, kr_ref = refs[8 + nr:12 + nr]
        ride_out = refs[12 + nr:12 + 2 * nr]
        kpad, vpad = refs[12 + 2 * nr:14 + 2 * nr]
        sems = refs[14 + 2 * nr:]
        _ride_start(ride_modes, pl.program_id(0) == 0, ride_in, ride_out, sems)

        kpad[0:WINDOW, :] = jnp.zeros((WINDOW, LANES), BF16)
        vpad[0:WINDOW, :] = jnp.zeros((WINDOW, LANES), BF16)
        window, current = _band_masks()

        def block(n, carry):
            r0 = pl.multiple_of(n * WINDOW, WINDOW)
            rows = pl.ds(r0, WINDOW)
            nxt = pl.ds(r0 + WINDOW, WINDOW)
            band = pl.ds(r0, 2 * WINDOW)
            cs, sl, sr = cos_ref[rows, :], sl_ref[rows, :], sr_ref[rows, :]
            kb = _bf(_rope(k_ref[rows, :], cs, sl, sr))
            kpad[nxt, :] = kb
            kr_ref[rows, :] = kb
            vpad[nxt, :] = _bf(v_ref[rows, :])
            qparts = []
            for j in range(ATT_WIDTH // LANES):
                qp = _bf(_rope(q_ref[rows, j * LANES:(j + 1) * LANES], cs, sl, sr) * ATT_SCALE)
                qr_ref[rows, j * LANES:(j + 1) * LANES] = qp
                qparts.append(qp)
            mask = window & (current | (n > 0))
            for hk in range(ATT_KV_HEADS):
                lanes = slice(hk * ATT_HEAD_DIM, (hk + 1) * ATT_HEAD_DIM)
                qs = _stack_heads(qparts, hk)
                p, inv, _ = _softmax_band(qs, kpad[band, lanes], mask, _sink_row(sink_ref, hk))
                ot = _dot_tn(vpad[band, lanes], _bf(p)) * inv
                for g in range(ATT_GROUP):
                    h = ATT_GROUP * hk + g
                    o_ref[rows, h * ATT_HEAD_DIM:(h + 1) * ATT_HEAD_DIM] = ot[:, g * WINDOW:(g + 1) * WINDOW].T
            ob = o_ref[rows, :]
            an_ref[rows, :] = _bf(ob * lax.rsqrt(_mean_last(ob * ob) + EPS) * aw_ref[...])
            return carry

        lax.fori_loop(0, nb, block, 0)
        _ride_wait(ride_modes, pl.program_id(0) == B - 1, ride_in, ride_out, sems)

    seq = lambda w, j: pl.BlockSpec((None, T, w), lambda b: (b, 0, j))
    full = lambda r, w: pl.BlockSpec((r, w), lambda b: (0, 0))
    return pl.pallas_call(
        body, name="attn_fwd", grid=(B,),
        in_specs=[seq(ATT_WIDTH, 0), seq(LANES, 4), seq(LANES, 5),
                  full(T, LANES), full(T, LANES), full(T, LANES),
                  pl.BlockSpec(memory_space=pltpu.SMEM), full(1, ATT_WIDTH)] + [ANY_SPEC] * nr,
        out_specs=[seq(ATT_WIDTH, 0), seq(ATT_WIDTH, 0), seq(ATT_WIDTH, 0), seq(LANES, 0)] + [ANY_SPEC] * nr,
        out_shape=[SDS((B, T, ATT_WIDTH), F32), SDS((B, T, ATT_WIDTH), BF16),
                   SDS((B, T, ATT_WIDTH), BF16), SDS((B, T, LANES), BF16)] + _exchange_shapes(ride_srcs, ride_modes),
        scratch_shapes=[pltpu.VMEM((T + WINDOW, LANES), BF16), pltpu.VMEM((T + WINDOW, LANES), BF16)]
        + _exchange_sems(nr),
        compiler_params=_params(("arbitrary",), VMEM_LIMIT_BIG),
    )(proj3, proj3, proj3, cos, sinl, sinr, sinks, attn_w, *ride_srcs)


HG_GROUP = 8
HG_ROWS = HG_GROUP * HG_CHUNK


def _group_masks():
    r = lax.broadcasted_iota(jnp.int32, (HG_ROWS, HG_ROWS), 0)
    c = lax.broadcasted_iota(jnp.int32, (HG_ROWS, HG_ROWS), 1)
    same = (r // HG_CHUNK) == (c // HG_CHUNK)
    return same & (r >= c), same & (c >= r), same & (c > r)


def _ones_bf(mask):
    return jnp.where(mask, 1.0, 0.0).astype(BF16)


def _hgrn_gates(hq, hf, lb, sums_bf, ebl_scr):
    sq = _sigmoid(hq)
    q = hq * sq
    sg = _sigmoid(hf)
    f = lb + (1.0 - lb) * sg
    k = 1.0 - f
    cs = _tri_sum(sums_bf, jnp.log(f))
    b, rem = cs[:HG_ROWS], cs[HG_ROWS:]
    eb, enb, e2 = jnp.exp(b), jnp.exp(-b), jnp.exp(rem)
    ebl_scr[...] = eb * e2
    return dict(sq=sq, sg=sg, f=f, eb=eb, enb=enb, e2=e2, qd=q * eb, kd=k * enb, k2=k * e2)


def _hgrn_specs(B, T):
    head = lambda base: pl.BlockSpec((None, T, LANES), lambda b, h: (b, 0, base + h))
    return head


def _chunk_rows(c):
    return slice(c * HG_CHUNK, (c + 1) * HG_CHUNK)


def _hgrn_fwd(proj3, lb, hg_w, ride_srcs, ride_modes):
    B, T, _ = proj3.shape
    nc = T // HG_CHUNK
    ng = T // HG_ROWS
    nr = len(ride_srcs)
    head = _hgrn_specs(B, T)

    def body(*refs):
        hq_ref, hf_ref, hi_ref, hg_ref, lb_ref, gw_ref = refs[:6]
        ride_in = refs[6:6 + nr]
        o_ref, rg_ref, sp_ref = refs[6 + nr:9 + nr]
        ride_out = refs[9 + nr:9 + 2 * nr]
        st, ebl_scr = refs[9 + 2 * nr:11 + 2 * nr]
        sems = refs[11 + 2 * nr:]
        first = (pl.program_id(0) == 0) & (pl.program_id(1) == 0)
        last = (pl.program_id(0) == B - 1) & (pl.program_id(1) == HG_HEADS - 1)
        _ride_start(ride_modes, first, ride_in, ride_out, sems)

        st[...] = jnp.zeros((HG_HEAD_DIM, HG_HEAD_DIM), F32)
        lo, _, ups = _group_masks()
        sums_bf = jnp.concatenate([_ones_bf(lo), _ones_bf(ups)], axis=0)
        lbv = lb_ref[...]

        def group(gi, carry):
            rows = pl.ds(pl.multiple_of(gi * HG_ROWS, HG_ROWS), HG_ROWS)
            gt = _hgrn_gates(hq_ref[rows, :], hf_ref[rows, :], lbv, sums_bf, ebl_scr)
            v, qd, kd, k2 = _bf(hi_ref[rows, :]), _bf(gt["qd"]), _bf(gt["kd"]), _bf(gt["k2"])
            a = jnp.where(lo, _dot_nt(qd, kd), 0.0)
            o = _dot(_bf(a), v)
            s = st[...]
            inter = []
            for c in range(HG_GROUP):
                cr = _chunk_rows(c)
                sp_ref[gi * HG_GROUP + c] = s
                inter.append(_dot_nt(qd[cr], _bf(s)))
                s = s * ebl_scr[c * HG_CHUNK:c * HG_CHUNK + 1, :] + _dot_tn(v[cr], k2[cr])
            st[...] = s
            o = o + jnp.concatenate(inter, axis=0)
            o_ref[rows, :] = o
            hg = hg_ref[rows, :]
            rn = o * lax.rsqrt(_mean_last(o * o) + EPS) * gw_ref[...]
            rg_ref[rows, :] = _bf(rn * (hg * _sigmoid(hg)))
            return carry

        lax.fori_loop(0, ng, group, 0)
        _ride_wait(ride_modes, last, ride_in, ride_out, sems)

    out_head = pl.BlockSpec((None, T, LANES), lambda b, h: (b, 0, h))
    return pl.pallas_call(
        body, name="hgrn_fwd", grid=(B, HG_HEADS),
        in_specs=[head(6), head(10), head(14), head(18),
                  pl.BlockSpec((1, LANES), lambda b, h: (0, h)),
                  pl.BlockSpec((1, LANES), lambda b, h: (0, 0))] + [ANY_SPEC] * nr,
        out_specs=[out_head, out_head,
                   pl.BlockSpec((None, None, nc, HG_HEAD_DIM, HG_HEAD_DIM), lambda b, h: (b, h, 0, 0, 0))]
        + [ANY_SPEC] * nr,
        out_shape=[SDS((B, T, HG_WIDTH), F32), SDS((B, T, HG_WIDTH), BF16),
                   SDS((B, HG_HEADS, nc, HG_HEAD_DIM, HG_HEAD_DIM), F32)] + _exchange_shapes(ride_srcs, ride_modes),
        scratch_shapes=[pltpu.VMEM((HG_HEAD_DIM, HG_HEAD_DIM), F32), pltpu.VMEM((HG_ROWS, LANES), F32)]
        + _exchange_sems(nr),
        compiler_params=_params(("arbitrary", "arbitrary"), VMEM_LIMIT_BIG),
    )(proj3, proj3, proj3, proj3, lb, hg_w, *ride_srcs)


def _mix_out(x2, attn_n, rec_g, mod8, post_w, w_out_bf, T):
    N = x2.shape[0]
    TM = _tile_rows(T)
    tps = T // TM

    def body(x_ref, an_ref, rg_ref, mod_ref, pw_ref, w_ref, mix_ref, x1_ref, cat_ref):
        cat = jnp.concatenate([an_ref[...], rg_ref[...]], axis=1)
        cat_ref[...] = cat
        mix = _dot(cat, w_ref[...])
        mix_ref[...] = mix
        r = lax.rsqrt(_mean_last(mix * mix) + EPS)
        x1_ref[...] = x_ref[...] + mod_ref[2:3, :] * (mix * r * pw_ref[...])

    row = lambda w: pl.BlockSpec((TM, w), lambda i: (i, 0))
    return pl.pallas_call(
        body, name="mix_out", grid=(N // TM,),
        in_specs=[row(D_MODEL), row(ATT_WIDTH), row(HG_WIDTH), _mod_spec(tps),
                  pl.BlockSpec((1, D_MODEL), lambda i: (0, 0)),
                  pl.BlockSpec((D_MODEL, D_MODEL), lambda i: (0, 0))],
        out_specs=[row(D_MODEL), row(D_MODEL), row(D_MODEL)],
        out_shape=[SDS((N, D_MODEL), F32), SDS((N, D_MODEL), F32), SDS((N, D_MODEL), BF16)],
        compiler_params=_params(("arbitrary",), VMEM_LIMIT_BIG),
    )(x2, attn_n, rec_g, mod8, post_w, w_out_bf)


def _load_weights_once(pairs, sem):
    @pl.when(pl.program_id(0) == 0)
    def _():
        cps = [pltpu.make_async_copy(src, dst, sem.at[i]) for i, (src, dst) in enumerate(pairs)]
        for cp in cps:
            cp.start()
        for cp in cps:
            cp.wait()


def _mlp_fwd(x1, mod8, pre_w, w_up_bf, w_down_bf, T):
    N = x1.shape[0]
    TM = _tile_rows(T)
    tps = T // TM

    def body(x_ref, mod_ref, pw_ref, wu_hbm, wd_hbm, up_ref, d_ref, h2_ref, wu, wd, sem):
        _load_weights_once([(wu_hbm, wu), (wd_hbm, wd)], sem)
        x = x_ref[...]
        r = lax.rsqrt(_mean_last(x * x) + EPS)
        h = (x * r * pw_ref[...]) * (1.0 + mod_ref[4:5, :]) + mod_ref[3:4, :]
        hb = _bf(h)
        h2_ref[...] = hb
        up = _dot(hb, wu[...])
        up_ref[...] = up
        ru = jnp.maximum(up, 0.0)
        d_ref[...] = _dot(_bf(ru * ru), wd[...])

    row = lambda w: pl.BlockSpec((TM, w), lambda i: (i, 0))
    return pl.pallas_call(
        body, name="mlp_fwd", grid=(N // TM,),
        in_specs=[row(D_MODEL), _mod_spec(tps), pl.BlockSpec((1, D_MODEL), lambda i: (0, 0)),
                  pl.BlockSpec(memory_space=pl.ANY), pl.BlockSpec(memory_space=pl.ANY)],
        out_specs=[row(D_FF), row(D_MODEL), row(D_MODEL)],
        out_shape=[SDS((N, D_FF), F32), SDS((N, D_MODEL), F32), SDS((N, D_MODEL), BF16)],
        scratch_shapes=[pltpu.VMEM((D_MODEL, D_FF), BF16), pltpu.VMEM((D_FF, D_MODEL), BF16),
                        pltpu.SemaphoreType.DMA((2,))],
        compiler_params=_params(("arbitrary",), VMEM_LIMIT_BIG),
    )(x1, mod8, pre_w, w_up_bf, w_down_bf)


def _acc_rows(acc_ref, first, rows):
    @pl.when(first)
    def _():
        acc_ref[...] = jnp.zeros(acc_ref.shape, F32)
    for i, r in enumerate(rows):
        acc_ref[i:i + 1, :] += r


def _mlp_bwd(x1, d, up, tgt, mod8, pre_w, post_w, w_down_bf, w_up_bf, T):
    N = x1.shape[0]
    TM = _tile_rows(T)
    tps = T // TM

    def body(x_ref, d_ref, up_ref, t_ref, mod_ref, pw_ref, qw_ref, wd_hbm, wu_hbm,
             dx_ref, u_ref, dup_ref, dd_ref, acc_ref, wd, wu, sem):
        _load_weights_once([(wd_hbm, wd), (wu_hbm, wu)], sem)
        sh2, sc2, g2 = mod_ref[3:4, :], mod_ref[4:5, :], mod_ref[5:6, :]
        x = x_ref[...]
        r1 = lax.rsqrt(_mean_last(x * x) + EPS)
        xh = x * r1
        n2 = xh * pw_ref[...]
        dv = d_ref[...]
        rd = lax.rsqrt(_mean_last(dv * dv) + EPS)
        dh = dv * rd
        rr = dh * qw_ref[...]
        e = x + g2 * rr - t_ref[...]
        loss = 0.5 * jnp.sum(_sum_rows(e * e), axis=1, keepdims=True) / D_MODEL
        dy = e * (1.0 / D_MODEL)
        dg2 = _sum_rows(dy * rr)
        drr = dy * g2
        dw_post = _sum_rows(drr * dh)
        ddh = drr * qw_ref[...]
        dd = _bf(rd * (ddh - dh * _mean_last(ddh * dh)))
        dd_ref[...] = dd
        ru = jnp.maximum(up_ref[...], 0.0)
        u_ref[...] = _bf(ru * ru)
        dup = _bf(_dot_nt(dd, wd[...]) * (2.0 * ru))
        dup_ref[...] = dup
        dh2 = _dot_nt(dup, wu[...])
        dsh2 = _sum_rows(dh2)
        dsc2 = _sum_rows(dh2 * n2)
        dn2 = dh2 * (1.0 + sc2)
        dw_pre = _sum_rows(dn2 * xh)
        dxh = dn2 * pw_ref[...]
        dx_ref[...] = dy + r1 * (dxh - xh * _mean_last(dxh * xh))
        _acc_rows(acc_ref, pl.program_id(0) % tps == 0,
                  [dsh2, dsc2, dg2, dw_pre, dw_post, jnp.broadcast_to(loss, (1, D_MODEL))])

    row = lambda w: pl.BlockSpec((TM, w), lambda i: (i, 0))
    vec = pl.BlockSpec((1, D_MODEL), lambda i: (0, 0))
    B = N // T
    return pl.pallas_call(
        body, name="mlp_bwd", grid=(N // TM,),
        in_specs=[row(D_MODEL), row(D_MODEL), row(D_FF), row(D_MODEL), _mod_spec(tps), vec, vec,
                  pl.BlockSpec(memory_space=pl.ANY), pl.BlockSpec(memory_space=pl.ANY)],
        out_specs=[row(D_MODEL), row(D_FF), row(D_FF), row(D_MODEL), _mod_spec(tps)],
        out_shape=[SDS((N, D_MODEL), F32), SDS((N, D_FF), BF16), SDS((N, D_FF), BF16),
                   SDS((N, D_MODEL), BF16), SDS((B, 8, D_MODEL), F32)],
        scratch_shapes=[pltpu.VMEM((D_FF, D_MODEL), BF16), pltpu.VMEM((D_MODEL, D_FF), BF16),
                        pltpu.SemaphoreType.DMA((2,))],
        compiler_params=_params(("arbitrary",), VMEM_LIMIT_BIG),
    )(x1, d, up, tgt, mod8, pre_w, post_w, w_down_bf, w_up_bf)


def _mix_bwd(mix, dx1, mod8, post_w, w_out_bf, T):
    N = mix.shape[0]
    TM = _tile_rows(T)
    tps = T // TM

    def body(mix_ref, dx_ref, mod_ref, pw_ref, w_ref, dan_ref, drg_ref, dmix_ref, acc_ref):
        g1 = mod_ref[2:3, :]
        mix = mix_ref[...]
        dx1 = dx_ref[...]
        rm = lax.rsqrt(_mean_last(mix * mix) + EPS)
        mh = mix * rm
        dg1 = _sum_rows(dx1 * (mh * pw_ref[...]))
        dr = dx1 * g1
        dw_post = _sum_rows(dr * mh)
        dmh = dr * pw_ref[...]
        dmix = _bf(rm * (dmh - mh * _mean_last(dmh * mh)))
        dmix_ref[...] = dmix
        dcat = _dot_nt(dmix, w_ref[...])
        dan_ref[...] = dcat[:, :ATT_WIDTH]
        drg_ref[...] = dcat[:, ATT_WIDTH:]
        _acc_rows(acc_ref, pl.program_id(0) % tps == 0, [dg1, dw_post])

    row = lambda w: pl.BlockSpec((TM, w), lambda i: (i, 0))
    B = N // T
    return pl.pallas_call(
        body, name="mix_bwd", grid=(N // TM,),
        in_specs=[row(D_MODEL), row(D_MODEL), _mod_spec(tps), pl.BlockSpec((1, D_MODEL), lambda i: (0, 0)),
                  pl.BlockSpec((D_MODEL, D_MODEL), lambda i: (0, 0))],
        out_specs=[row(ATT_WIDTH), row(HG_WIDTH), row(D_MODEL), _mod_spec(tps)],
        out_shape=[SDS((N, ATT_WIDTH), F32), SDS((N, HG_WIDTH), F32), SDS((N, D_MODEL), BF16),
                   SDS((B, 8, D_MODEL), F32)],
        compiler_params=_params(("arbitrary",), VMEM_LIMIT_BIG),
    )(mix, dx1, mod8, post_w, w_out_bf)


def _hgrn_bwd(proj3, lb, hg_w, o, s_prev, drg, ride_srcs, ride_modes):
    B, T, _ = proj3.shape
    nc = T // HG_CHUNK
    ng = T // HG_ROWS
    nr = len(ride_srcs)
    head = _hgrn_specs(B, T)

    def body(*refs):
        hq_ref, hf_ref, hi_ref, hg_ref, lb_ref, gw_ref, o_ref, sp_ref, drg_ref = refs[:9]
        ride_in = refs[9:9 + nr]
        dhq_ref, dhf_ref, dhi_ref, dhg_ref, dlb_ref, dgw_ref = refs[9 + nr:15 + nr]
        ride_out = refs[15 + nr:15 + 2 * nr]
        dst, ebl_scr = refs[15 + 2 * nr:17 + 2 * nr]
        sems = refs[17 + 2 * nr:]
        first = (pl.program_id(0) == 0) & (pl.program_id(1) == 0)
        last = (pl.program_id(0) == B - 1) & (pl.program_id(1) == HG_HEADS - 1)
        _ride_start(ride_modes, first, ride_in, ride_out, sems)

        dst[...] = jnp.zeros((HG_HEAD_DIM, HG_HEAD_DIM), F32)
        lo, up, ups = _group_masks()
        sums_bf = jnp.concatenate([_ones_bf(lo), _ones_bf(ups)], axis=0)
        up_bf = _ones_bf(up)
        last_row = lax.broadcasted_iota(jnp.int32, (HG_CHUNK, LANES), 0) == HG_CHUNK - 1
        lbv = lb_ref[...]
        gw = gw_ref[...]

        def group(i, carry):
            dlb, dgw = carry
            gi = ng - 1 - i
            rows = pl.ds(pl.multiple_of(gi * HG_ROWS, HG_ROWS), HG_ROWS)
            hq = hq_ref[rows, :]
            gt = _hgrn_gates(hq, hf_ref[rows, :], lbv, sums_bf, ebl_scr)
            sq, sg, qdf, kdf, k2f = gt["sq"], gt["sg"], gt["qd"], gt["kd"], gt["k2"]
            v, qd, kd, k2 = _bf(hi_ref[rows, :]), _bf(qdf), _bf(kdf), _bf(k2f)
            ov = o_ref[rows, :]
            hg = hg_ref[rows, :]
            shg = _sigmoid(hg)
            dr = drg_ref[rows, :]
            ro = lax.rsqrt(_mean_last(ov * ov) + EPS)
            oh = ov * ro
            dhg_ref[rows, :] = dr * (oh * gw) * (shg + hg * shg * (1.0 - shg))
            drn = dr * (hg * shg)
            dgw = dgw + _sum_rows(drn * oh)
            doh = drn * gw
            do = _bf(ro * (doh - oh * _mean_last(doh * oh)))
            a = jnp.where(lo, _dot_nt(qd, kd), 0.0)
            da = _bf(jnp.where(lo, _dot_nt(do, v), 0.0))
            dv = _dot_tn(_bf(a), do)
            dqd = _dot(da, kd)
            dkd = _dot_tn(da, qd)
            ds = dst[...]
            dk2_l, dv_l, dqd_l, dbl_l = [None] * HG_GROUP, [None] * HG_GROUP, [None] * HG_GROUP, [None] * HG_GROUP
            for c in reversed(range(HG_GROUP)):
                cr = _chunk_rows(c)
                sp = sp_ref[gi * HG_GROUP + c]
                ebl = ebl_scr[c * HG_CHUNK:c * HG_CHUNK + 1, :]
                dsb = _bf(ds)
                dk2_c = _dot(v[cr], dsb)
                dk2_l[c] = dk2_c
                dv_l[c] = _dot_nt(k2[cr], dsb)
                dqd_l[c] = _dot(do[cr], _bf(sp))
                dbl = _sum_rows(ds * sp) * ebl + _sum_rows(dk2_c * k2f[cr])
                dbl_l[c] = jnp.where(last_row, dbl, 0.0)
                ds = ds * ebl + _dot_tn(do[cr], qd[cr])
            dst[...] = ds
            dk2 = jnp.concatenate(dk2_l, axis=0)
            dhi_ref[rows, :] = dv + jnp.concatenate(dv_l, axis=0)
            dqd = dqd + jnp.concatenate(dqd_l, axis=0)
            db = dqd * qdf - dkd * kdf - dk2 * k2f + jnp.concatenate(dbl_l, axis=0)
            dk = dkd * gt["enb"] + dk2 * gt["e2"]
            df = _tri_sum(up_bf, db) / gt["f"] - dk
            dhf_ref[rows, :] = df * (1.0 - lbv) * sg * (1.0 - sg)
            dlb = dlb + _sum_rows(df * (1.0 - sg))
            dhq_ref[rows, :] = (dqd * gt["eb"]) * (sq + hq * sq * (1.0 - sq))
            return dlb, dgw

        zero = jnp.zeros((1, LANES), F32)
        dlb, dgw = lax.fori_loop(0, ng, group, (zero, zero))
        dlb_ref[...] = jnp.broadcast_to(dlb, (8, LANES))
        dgw_ref[...] = jnp.broadcast_to(dgw, (8, LANES))
        _ride_wait(ride_modes, last, ride_in, ride_out, sems)

    out_head = pl.BlockSpec((None, T, LANES), lambda b, h: (b, 0, h))
    small = pl.BlockSpec((None, 8, LANES), lambda b, h: (b, 0, h))
    return pl.pallas_call(
        body, name="hgrn_bwd", grid=(B, HG_HEADS),
        in_specs=[head(6), head(10), head(14), head(18),
                  pl.BlockSpec((1, LANES), lambda b, h: (0, h)),
                  pl.BlockSpec((1, LANES), lambda b, h: (0, 0)),
                  out_head,
                  pl.BlockSpec((None, None, nc, HG_HEAD_DIM, HG_HEAD_DIM), lambda b, h: (b, h, 0, 0, 0)),
                  out_head] + [ANY_SPEC] * nr,
        out_specs=[out_head, out_head, out_head, out_head, small, small] + [ANY_SPEC] * nr,
        out_shape=[SDS((B, T, HG_WIDTH), F32)] * 4 + [SDS((B, 8, HG_WIDTH), F32)] * 2
        + _exchange_shapes(ride_srcs, ride_modes),
        scratch_shapes=[pltpu.VMEM((HG_HEAD_DIM, HG_HEAD_DIM), F32), pltpu.VMEM((HG_ROWS, LANES), F32)]
        + _exchange_sems(nr),
        compiler_params=_params(("arbitrary", "arbitrary"), VMEM_LIMIT_BIG),
    )(proj3, proj3, proj3, proj3, lb, hg_w, o, s_prev, drg, *ride_srcs)


def _attn_bwd(qr, kr, proj3, attn_o, dan, tables, sinks, attn_w, ride_srcs, ride_modes):
    B, T, _ = proj3.shape
    nb = T // WINDOW
    nr = len(ride_srcs)
    cos, sinl, sinr = tables
    QKV = ATT_WIDTH + 2 * LANES

    def body(*refs):
        qr_ref, kr_ref, v_ref, o_ref, dan_ref, cos_ref, sl_ref, sr_ref, sink_ref, aw_ref = refs[:10]
        ride_in = refs[10:10 + nr]
        dqkv_ref, dsink_ref, daw_ref = refs[10 + nr:13 + nr]
        ride_out = refs[13 + nr:13 + 2 * nr]
        kpad, vpad, dkpad, dvpad, dqb, dsk = refs[13 + 2 * nr:19 + 2 * nr]
        sems = refs[19 + 2 * nr:]
        _ride_start(ride_modes, pl.program_id(0) == 0, ride_in, ride_out, sems)

        window, current = _band_masks()
        kpad[0:WINDOW, :] = jnp.zeros((WINDOW, LANES), BF16)
        vpad[0:WINDOW, :] = jnp.zeros((WINDOW, LANES), BF16)
        kpad[WINDOW:, :] = kr_ref[...]
        vpad[WINDOW:, :] = _bf(v_ref[...])
        dkpad[...] = jnp.zeros(dkpad.shape, F32)
        dvpad[...] = jnp.zeros(dvpad.shape, F32)
        dsk[...] = jnp.zeros(dsk.shape, F32)
        aw = aw_ref[...]

        def block(n, daw):
            r0 = pl.multiple_of(n * WINDOW, WINDOW)
            rows = pl.ds(r0, WINDOW)
            band = pl.ds(r0, 2 * WINDOW)
            ob = o_ref[rows, :]
            dn = dan_ref[rows, :]
            ro = lax.rsqrt(_mean_last(ob * ob) + EPS)
            oh = ob * ro
            daw = daw + _sum_rows(dn * oh)
            doh = dn * aw
            do = _bf(ro * (doh - oh * _mean_last(doh * oh)))
            doparts = [do[:, j * LANES:(j + 1) * LANES] for j in range(ATT_WIDTH // LANES)]
            qparts = [qr_ref[rows, j * LANES:(j + 1) * LANES] for j in range(ATT_WIDTH // LANES)]
            mask = window & (current | (n > 0))
            for hk in range(ATT_KV_HEADS):
                lanes = slice(hk * ATT_HEAD_DIM, (hk + 1) * ATT_HEAD_DIM)
                qs = _stack_heads(qparts, hk)
                dos = _stack_heads(doparts, hk)
                kk, vv = kpad[band, lanes], vpad[band, lanes]
                p, inv, es = _softmax_band(qs, kk, mask, _sink_row(sink_ref, hk))
                p = p * inv
                dp = _dot_nt(vv, dos)
                delta = jnp.sum(p * dp, axis=0, keepdims=True)
                ds = _bf(p * (dp - delta))
                sk = (es * inv) * delta
                dqt = _dot_tn(kk, ds) * ATT_SCALE
                dkpad[band, lanes] += _dot(ds, qs)
                dvpad[band, lanes] += _dot(_bf(p), dos)
                for g in range(ATT_GROUP):
                    h = ATT_GROUP * hk + g
                    cols = slice(g * WINDOW, (g + 1) * WINDOW)
                    dqb[:, h * ATT_HEAD_DIM:(h + 1) * ATT_HEAD_DIM] = dqt[:, cols].T
                    dsk[h:h + 1, :] += jnp.broadcast_to(-jnp.sum(sk[:, cols], axis=1, keepdims=True), (1, LANES))
            cs, sl, sr = cos_ref[rows, :], sl_ref[rows, :], sr_ref[rows, :]
            for j in range(ATT_WIDTH // LANES):
                dqkv_ref[rows, j * LANES:(j + 1) * LANES] = _rope_t(dqb[:, j * LANES:(j + 1) * LANES], cs, sl, sr)
            return daw

        daw = lax.fori_loop(0, nb, block, jnp.zeros((1, ATT_WIDTH), F32))
        daw_ref[...] = jnp.broadcast_to(daw, (8, ATT_WIDTH))
        dsink_ref[...] = dsk[...]

        def finish(n, carry):
            r0 = pl.multiple_of(n * WINDOW, WINDOW)
            rows = pl.ds(r0, WINDOW)
            nxt = pl.ds(r0 + WINDOW, WINDOW)
            cs, sl, sr = cos_ref[rows, :], sl_ref[rows, :], sr_ref[rows, :]
            dqkv_ref[rows, ATT_WIDTH:ATT_WIDTH + LANES] = _rope_t(dkpad[nxt, :], cs, sl, sr)
            dqkv_ref[rows, ATT_WIDTH + LANES:QKV] = dvpad[nxt, :]
            return carry

        lax.fori_loop(0, nb, finish, 0)
        _ride_wait(ride_modes, pl.program_id(0) == B - 1, ride_in, ride_out, sems)

    seq = lambda w, j: pl.BlockSpec((None, T, w), lambda b: (b, 0, j))
    full = lambda r, w: pl.BlockSpec((r, w), lambda b: (0, 0))
    return pl.pallas_call(
        body, name="attn_bwd", grid=(B,),
        in_specs=[seq(ATT_WIDTH, 0), seq(LANES, 0), seq(LANES, 5), seq(ATT_WIDTH, 0), seq(ATT_WIDTH, 0),
                  full(T, LANES), full(T, LANES), full(T, LANES),
                  pl.BlockSpec(memory_space=pltpu.SMEM), full(1, ATT_WIDTH)] + [ANY_SPEC] * nr,
        out_specs=[seq(QKV, 0), pl.BlockSpec((None, 8, LANES), lambda b: (b, 0, 0)),
                   pl.BlockSpec((None, 8, ATT_WIDTH), lambda b: (b, 0, 0))] + [ANY_SPEC] * nr,
        out_shape=[SDS((B, T, QKV), F32), SDS((B, 8, LANES), F32), SDS((B, 8, ATT_WIDTH), F32)]
        + _exchange_shapes(ride_srcs, ride_modes),
        scratch_shapes=[pltpu.VMEM((T + WINDOW, LANES), BF16), pltpu.VMEM((T + WINDOW, LANES), BF16),
                        pltpu.VMEM((T + WINDOW, LANES), F32), pltpu.VMEM((T + WINDOW, LANES), F32),
                        pltpu.VMEM((WINDOW, ATT_WIDTH), F32), pltpu.VMEM((8, LANES), F32)] + _exchange_sems(nr),
        compiler_params=_params(("arbitrary",), VMEM_LIMIT_BIG),
    )(qr, kr, proj3, attn_o, dan, cos, sinl, sinr, sinks, attn_w, *ride_srcs)


def _in_bwd(x2, dx1, dqkv, dhq, dhf, dhi, dhg, mod8, pre_w, w_in_bf, T):
    N = x2.shape[0]
    TM = _tile_rows(T)
    tps = T // TM
    pieces = [(0, ATT_WIDTH + 2 * LANES), (768, HG_WIDTH), (1280, HG_WIDTH), (1792, HG_WIDTH), (2304, HG_WIDTH)]

    def body(x_ref, dx_ref, p0, p1, p2, p3, p4, mod_ref, pw_ref, w_ref, gx_ref, dproj_ref, acc_ref):
        sc1 = mod_ref[1:2, :]
        dh = jnp.zeros((TM, D_MODEL), F32)
        for ref, (off, width) in zip((p0, p1, p2, p3, p4), pieces):
            pb = _bf(ref[...])
            dproj_ref[:, off:off + width] = pb
            dh = dh + _dot_nt(pb, w_ref[:, off:off + width])
        x = x_ref[...]
        r = lax.rsqrt(_mean_last(x * x) + EPS)
        xh = x * r
        n1 = xh * pw_ref[...]
        dsh1 = _sum_rows(dh)
        dsc1 = _sum_rows(dh * n1)
        dn1 = dh * (1.0 + sc1)
        dw_pre = _sum_rows(dn1 * xh)
        dxh = dn1 * pw_ref[...]
        gx_ref[...] = dx_ref[...] + r * (dxh - xh * _mean_last(dxh * xh))
        _acc_rows(acc_ref, pl.program_id(0) % tps == 0, [dsh1, dsc1, dw_pre])

    row = lambda w: pl.BlockSpec((TM, w), lambda i: (i, 0))
    B = N // T
    return pl.pallas_call(
        body, name="in_bwd", grid=(N // TM,),
        in_specs=[row(D_MODEL), row(D_MODEL), row(768), row(HG_WIDTH), row(HG_WIDTH), row(HG_WIDTH),
                  row(HG_WIDTH), _mod_spec(tps), pl.BlockSpec((1, D_MODEL), lambda i: (0, 0)),
                  pl.BlockSpec((D_MODEL, IN_COLS), lambda i: (0, 0))],
        out_specs=[row(D_MODEL), row(IN_COLS), _mod_spec(tps)],
        out_shape=[SDS((N, D_MODEL), F32), SDS((N, IN_COLS), BF16), SDS((B, 8, D_MODEL), F32)],
        compiler_params=_params(("arbitrary",), VMEM_LIMIT_BIG),
    )(x2, dx1, dqkv, dhq, dhf, dhi, dhg, mod8, pre_w, w_in_bf)


def _matmul_tn(name, a, b, tn, by_columns=False):
    K, M = a.shape
    Nc = b.shape[1]
    tm = min(512, M)

    def body(a_ref, b_ref, o_ref):
        o_ref[...] = _bf(_dot_tn(a_ref[...], b_ref[...]))

    if by_columns:
        out_shape = SDS((Nc // tn, M, tn), BF16)
        out_spec = pl.BlockSpec((None, tm, tn), lambda i, j: (j, i, 0))
    else:
        out_shape = SDS((M, Nc), BF16)
        out_spec = pl.BlockSpec((tm, tn), lambda i, j: (i, j))
    return pl.pallas_call(
        body, name=name, grid=(M // tm, Nc // tn),
        in_specs=[pl.BlockSpec((K, tm), lambda i, j: (0, i)),
                  pl.BlockSpec((K, tn), lambda i, j: (0, j))],
        out_specs=out_spec, out_shape=out_shape,
        compiler_params=_params(("arbitrary", "arbitrary"), VMEM_LIMIT_BIG),
    )(a, b)


def _adamw_math(w, g, m, v):
    m2 = ADAM_B1 * m + (1.0 - ADAM_B1) * g
    v2 = ADAM_B2 * v + (1.0 - ADAM_B2) * (g * g)
    m_hat = m2 / (1.0 - ADAM_B1 ** ADAM_STEP)
    v_hat = v2 / (1.0 - ADAM_B2 ** ADAM_STEP)
    delta = -ADAM_LR * (m_hat / (jnp.sqrt(v_hat) + ADAM_EPS) + ADAM_WD * w)
    return delta, m2, v2


def _reduce_adamw(name, parts, w, m, v):
    r, c = w.shape
    tr = r if r <= 256 else 256

    def body(p_ref, w_ref, m_ref, v_ref, g_ref, d_ref, m2_ref, v2_ref):
        g = p_ref[0].astype(F32)
        for s in range(1, N_DEV):
            g = g + p_ref[s].astype(F32)
        g_ref[...] = g
        d_ref[...], m2_ref[...], v2_ref[...] = _adamw_math(w_ref[...], g, m_ref[...], v_ref[...])

    blk = pl.BlockSpec((tr, c), lambda i: (i, 0))
    return pl.pallas_call(
        body, name=name, grid=(r // tr,),
        in_specs=[pl.BlockSpec((N_DEV, tr, c), lambda i: (0, i, 0)), blk, blk, blk],
        out_specs=[blk] * 4, out_shape=[SDS((r, c), F32)] * 4,
        compiler_params=_params(("arbitrary",), VMEM_LIMIT_BIG),
    )(parts, w, m, v)


def _ada_grad_adamw(c_all, dmod_all, w, m, v):
    r, c = w.shape
    tr = 256
    nb = c_all.shape[0]

    def body(c_ref, dm_ref, w_ref, m_ref, v_ref, g_ref, d_ref, m2_ref, v2_ref):
        cv = c_ref[...]
        g = _dot_tn(cv * _sigmoid(cv), dm_ref[...])
        g_ref[...] = g
        d_ref[...], m2_ref[...], v2_ref[...] = _adamw_math(w_ref[...], g, m_ref[...], v_ref[...])

    blk = pl.BlockSpec((tr, c), lambda i: (i, 0))
    return pl.pallas_call(
        body, name="ada_grad_adamw", grid=(r // tr,),
        in_specs=[pl.BlockSpec((nb, tr), lambda i: (0, i)), pl.BlockSpec((nb, c), lambda i: (0, 0)),
                  blk, blk, blk],
        out_specs=[blk] * 4, out_shape=[SDS((r, c), F32)] * 4,
        compiler_params=_params(("arbitrary",)),
    )(c_all, dmod_all, w, m, v)


_SMALL = [("b_ada", 6144), ("pre_w_mix", 1024), ("attn_sinks", 128), ("attn_out_w", 512), ("lb_table", 1024),
          ("hg_norm_w", 128), ("post_w_mix", 1024), ("pre_w_mlp", 1024), ("post_w_mlp", 1024)]


def _pack_small(vals):
    out = []
    for name, width in _SMALL:
        f = vals[name].reshape(-1).astype(F32)
        out.append(jnp.pad(f, (0, width - f.shape[0])))
    return jnp.concatenate(out).reshape(1, -1)


def _unpack_small(vec, shapes):
    out, off = {}, 0
    for name, width in _SMALL:
        n = 1
        for s in shapes[name]:
            n *= s
        out[name] = vec[0, off:off + n].reshape(shapes[name])
        off += width
    return out


def _columns_to_full(g):
    return g.transpose(1, 0, 2).reshape(g.shape[1], -1)


def kernel(x, c, w_ada, b_ada, pre_w_mix, w_in, attn_sinks, attn_out_w, lb_table, hg_norm_w, w_out, post_w_mix, pre_w_mlp, w_up, w_down, post_w_mlp, loss_target, m_w_ada, m_b_ada, m_pre_w_mix, m_w_in, m_attn_sinks, m_attn_out_w, m_lb_table, m_hg_norm_w, m_w_out, m_post_w_mix, m_pre_w_mlp, m_w_up, m_w_down, m_post_w_mlp, v_w_ada, v_b_ada, v_pre_w_mix, v_w_in, v_attn_sinks, v_attn_out_w, v_lb_table, v_hg_norm_w, v_w_out, v_post_w_mix, v_pre_w_mlp, v_w_up, v_w_down, v_post_w_mlp):
    B, T, _ = x.shape
    N = B * T
    me = 4 * lax.axis_index("x") + 2 * lax.axis_index("y") + lax.axis_index("c")
    x2 = x.reshape(N, D_MODEL)
    tgt2 = loss_target.reshape(N, D_MODEL)

    w_in_g, c_g = _exchange("gather_w_in", [_bf(w_in[0]), c], ["gather"] * 2)
    w_in_f = _columns_to_full(w_in_g)
    c_all = c_g.reshape(N_DEV * B, D_MODEL)

    ada_cols = w_ada.shape[2]
    b_mine = lax.dynamic_slice(b_ada, (0, me * ada_cols), (1, ada_cols))
    mod_cols = _ada_mod(c_all, w_ada[0], b_mine)
    (mod_g,) = _exchange("scatter_mod", [mod_cols.reshape(N_DEV, B, ada_cols)], ["a2a"])
    mod = mod_g.transpose(1, 0, 2).reshape(B, 6, D_MODEL)
    mod8 = jnp.pad(mod, ((0, 0), (0, 2), (0, 0)))

    lb_p = jax.nn.softmax(lb_table, axis=0)
    lb = lb_p[1:2]
    tables = _rope_tables(T)

    proj, h1 = _in_proj(x2, mod8, pre_w_mix, w_in_f, T)
    proj3 = proj.reshape(B, T, IN_COLS)
    rec_o, rec_g, s_prev, w_out_g, w_up_g = _hgrn_fwd(
        proj3, lb, hg_norm_w, [_bf(w_out[0]), _bf(w_up[0])], ["gather"] * 2)
    attn_o, attn_n, qr, kr, w_down_g = _attn_fwd(proj3, tables, attn_sinks, attn_out_w,
                                                 [_bf(w_down[0])], ["gather"])
    w_out_f = w_out_g.reshape(D_MODEL, D_MODEL)
    w_up_f = _columns_to_full(w_up_g)
    w_down_f = w_down_g.reshape(D_FF, D_MODEL)
    mix, x1, cat = _mix_out(x2, attn_n.reshape(N, ATT_WIDTH), rec_g.reshape(N, HG_WIDTH), mod8,
                            post_w_mix, w_out_f, T)
    up, d, h2 = _mlp_fwd(x1, mod8, pre_w_mlp, w_up_f, w_down_f, T)

    dx1, u, dup, dd, acc_mlp = _mlp_bwd(x1, d, up, tgt2, mod8, pre_w_mlp, post_w_mlp,
                                        w_down_f, w_up_f, T)
    dan, drg, dmix, acc_mix = _mix_bwd(mix, dx1, mod8, post_w_mix, w_out_f, T)
    gw_out = _matmul_tn("grad_w_out", cat, dmix, 512).reshape(N_DEV, D_MODEL // N_DEV, D_MODEL)
    gw_up = _matmul_tn("grad_w_up", h2, dup, D_FF // N_DEV, by_columns=True)
    gw_down = _matmul_tn("grad_w_down", u, dd, 512).reshape(N_DEV, D_FF // N_DEV, D_MODEL)
    dhq, dhf, dhi, dhg, dlb_p, dgw_p, r_down = _hgrn_bwd(
        proj3, lb, hg_norm_w, rec_o, s_prev, drg.reshape(B, T, HG_WIDTH), [gw_down], ["a2a"])
    dqkv, dsink_p, daw_p, r_up, r_out = _attn_bwd(qr, kr, proj3, attn_o, dan.reshape(B, T, ATT_WIDTH), tables,
                                                  attn_sinks, attn_out_w, [gw_up, gw_out], ["a2a"] * 2)
    flat = lambda a: a.reshape(N, a.shape[-1])
    grad_x, dproj, acc_in = _in_bwd(x2, dx1, flat(dqkv), flat(dhq), flat(dhf), flat(dhi), flat(dhg),
                                    mod8, pre_w_mix, w_in_f, T)

    gw_in = _matmul_tn("grad_w_in", h1, dproj, IN_COLS // 2)
    in_cols = w_in.shape[2]
    gw_in = gw_in.reshape(D_MODEL, N_DEV, in_cols).transpose(1, 0, 2)

    dmod = jnp.concatenate([acc_in[:, 0:2], acc_mix[:, 0:1], acc_mlp[:, 0:3]], axis=1)
    dlb = dlb_p[:, 0].sum(0)
    dlb_table = jnp.stack([-dlb, dlb]) * (lb_p[0] * lb_p[1])[None, :]
    small = {
        "b_ada": dmod.sum(0),
        "pre_w_mix": acc_in[:, 2].sum(0),
        "attn_sinks": dsink_p[:, :, 0].sum(0),
        "attn_out_w": daw_p[:, 0].sum(0),
        "lb_table": dlb_table,
        "hg_norm_w": dgw_p[:, 0].reshape(B, HG_HEADS, LANES).sum((0, 1)),
        "post_w_mix": acc_mix[:, 1].sum(0),
        "pre_w_mlp": acc_mlp[:, 3].sum(0),
        "post_w_mlp": acc_mlp[:, 4].sum(0),
    }
    loss_part = acc_mlp[:, 5, 0].sum()
    dmod_blocks = dmod.reshape(B, N_DEV, ada_cols).transpose(1, 0, 2)

    r_in, r_dmod, r_small = _exchange(
        "reduce_grads", [gw_in, dmod_blocks, _pack_small(small)], ["a2a", "a2a", "gather"])

    res = {}
    res["w_in"] = _reduce_adamw("adamw_w_in", r_in, w_in[0], m_w_in[0], v_w_in[0])
    res["w_out"] = _reduce_adamw("adamw_w_out", r_out, w_out[0], m_w_out[0], v_w_out[0])
    res["w_up"] = _reduce_adamw("adamw_w_up", r_up, w_up[0], m_w_up[0], v_w_up[0])
    res["w_down"] = _reduce_adamw("adamw_w_down", r_down, w_down[0], m_w_down[0], v_w_down[0])
    res["w_ada"] = _ada_grad_adamw(c_all, r_dmod.reshape(N_DEV * B, ada_cols), w_ada[0], m_w_ada[0], v_w_ada[0])

    given = dict(b_ada=(b_ada, m_b_ada, v_b_ada), pre_w_mix=(pre_w_mix, m_pre_w_mix, v_pre_w_mix),
                 attn_sinks=(attn_sinks, m_attn_sinks, v_attn_sinks),
                 attn_out_w=(attn_out_w, m_attn_out_w, v_attn_out_w), lb_table=(lb_table, m_lb_table, v_lb_table),
                 hg_norm_w=(hg_norm_w, m_hg_norm_w, v_hg_norm_w), post_w_mix=(post_w_mix, m_post_w_mix, v_post_w_mix),
                 pre_w_mlp=(pre_w_mlp, m_pre_w_mlp, v_pre_w_mlp), post_w_mlp=(post_w_mlp, m_post_w_mlp, v_post_w_mlp))
    shapes = {k: t[0].shape for k, t in given.items()}
    packed = [_pack_small({k: t[i] for k, t in given.items()}) for i in range(3)]
    small_res = _reduce_adamw("adamw_small", r_small, *packed)
    small_out = [_unpack_small(a, shapes) for a in small_res]
    for k in given:
        res[k] = tuple(so[k] for so in small_out)

    loss = lax.psum(loss_part, ("x", "y", "c"))
    order = ["w_ada", "b_ada", "pre_w_mix", "w_in", "attn_sinks", "attn_out_w", "lb_table", "hg_norm_w", "w_out",
             "post_w_mix", "pre_w_mlp", "w_up", "w_down", "post_w_mlp"]
    big = {"w_ada", "w_in", "w_out", "w_up", "w_down"}
    outs = [loss, grad_x.reshape(B, T, D_MODEL)]
    for i in range(4):
        for k in order:
            a = res[k][i]
            outs.append(a[None] if k in big else a)
    return tuple(outs)
```

```python
import functools

import jax
import jax.numpy as jnp
from jax import lax
from jax.experimental import pallas as pl
from jax.experimental.pallas import tpu as pltpu

F32 = jnp.float32
BF16 = jnp.bfloat16
SDS = jax.ShapeDtypeStruct

D_MODEL = 1024
ATT_WIDTH = 512
ATT_HEAD_DIM = 64
ATT_KV_HEADS = 2
ATT_GROUP = 4
WINDOW = 128
ROPE_DIM = 16
ROPE_THETA = 500000.0
HG_WIDTH = 512
HG_HEAD_DIM = 128
HG_HEADS = 4
HG_CHUNK = 32
IN_COLS = 2816
D_FF = 4096
EPS = 1e-6
N_DEV = 8

ADAM_LR = 0.001
ADAM_B1 = 0.9
ADAM_B2 = 0.999
ADAM_EPS = 1e-08
ADAM_WD = 0.01
ADAM_STEP = 10

VMEM_LIMIT_BIG = 56 << 20
LANES = 128

MESH = pl.DeviceIdType.MESH
NT_DIMS = (((1,), (1,)), ((), ()))
TN_DIMS = (((0,), (0,)), ((), ()))


def _dot(a, b):
    return jnp.dot(a, b, preferred_element_type=F32)


def _dot_nt(a, b):
    return lax.dot_general(a, b, NT_DIMS, preferred_element_type=F32)


def _dot_tn(a, b):
    return lax.dot_general(a, b, TN_DIMS, preferred_element_type=F32)


def _bf(a):
    return a.astype(BF16)


def _sigmoid(a):
    return 1.0 / (1.0 + jnp.exp(-a))


def _mean_last(a):
    return jnp.mean(a, axis=-1, keepdims=True)


def _sum_rows(a):
    return jnp.sum(a, axis=0, keepdims=True)


def _tri_sum(tri_bf, a):
    a1 = _bf(a)
    r1 = a - a1.astype(F32)
    a2 = _bf(r1)
    a3 = _bf(r1 - a2.astype(F32))
    return _dot(tri_bf, a1) + _dot(tri_bf, a2) + _dot(tri_bf, a3)


def _params(sem=None, vmem=None):
    kw = {}
    if sem is not None:
        kw["dimension_semantics"] = sem
    if vmem is not None:
        kw["vmem_limit_bytes"] = vmem
    return pltpu.CompilerParams(**kw)


ANY_SPEC = pl.BlockSpec(memory_space=pl.ANY)


def _exchange_shapes(srcs, modes):
    out_shape = []
    for s, m in zip(srcs, modes):
        shp = (N_DEV,) + tuple(s.shape) if m == "gather" else tuple(s.shape)
        out_shape.append(SDS(shp, s.dtype))
    return out_shape


def _exchange_sems(n):
    return [pltpu.SemaphoreType.DMA((n, N_DEV - 1)), pltpu.SemaphoreType.DMA((n, N_DEV - 1)),
            pltpu.SemaphoreType.DMA((n,))]


def _exchange_copies(modes, src_refs, out_refs, send_sems, recv_sems, own_sems):
    x, y, c = lax.axis_index("x"), lax.axis_index("y"), lax.axis_index("c")
    me = 4 * x + 2 * y + c
    own, sends, recvs = [], [], []
    for a, mode in enumerate(modes):
        gather = mode == "gather"
        mine = src_refs[a] if gather else src_refs[a].at[me]
        own.append(pltpu.make_async_copy(mine, out_refs[a].at[me], own_sems.at[a]))
        for k in range(1, N_DEV):
            px, py, pc = x ^ ((k >> 2) & 1), y ^ ((k >> 1) & 1), c ^ (k & 1)
            peer = 4 * px + 2 * py + pc
            src = src_refs[a] if gather else src_refs[a].at[peer]
            sends.append(pltpu.make_async_remote_copy(
                src_ref=src, dst_ref=out_refs[a].at[me],
                send_sem=send_sems.at[a, k - 1], recv_sem=recv_sems.at[a, k - 1],
                device_id=(px, py, pc), device_id_type=MESH))
            recvs.append(pltpu.make_async_remote_copy(
                src_ref=src, dst_ref=out_refs[a].at[peer],
                send_sem=send_sems.at[a, k - 1], recv_sem=recv_sems.at[a, k - 1],
                device_id=(px, py, pc), device_id_type=MESH))
    return own, sends, recvs


def _exchange_start(copies):
    own, sends, _ = copies
    for cp in own + sends:
        cp.start()


def _exchange_wait(copies):
    own, sends, recvs = copies
    for cp in recvs:
        cp.wait_recv()
    for cp in sends:
        cp.wait_send()
    for cp in own:
        cp.wait()


def _exchange(name, srcs, modes):
    n = len(srcs)

    def body(*refs):
        copies = _exchange_copies(modes, refs[:n], refs[n:2 * n], *refs[2 * n:])
        _exchange_start(copies)
        _exchange_wait(copies)

    return pl.pallas_call(
        body, name=name, out_shape=_exchange_shapes(srcs, modes),
        in_specs=[ANY_SPEC] * n, out_specs=[ANY_SPEC] * n,
        scratch_shapes=_exchange_sems(n),
    )(*srcs)


def _ride_start(modes, first, src_refs, out_refs, sems):
    @pl.when(first)
    def _():
        _exchange_start(_exchange_copies(modes, src_refs, out_refs, *sems))


def _ride_wait(modes, last, src_refs, out_refs, sems):
    @pl.when(last)
    def _():
        _exchange_wait(_exchange_copies(modes, src_refs, out_refs, *sems))


def _ada_mod(c_all, w_ada, b_ada_mine):
    nb, cols = c_all.shape[0], w_ada.shape[1]

    def body(c_ref, w_ref, b_ref, o_ref):
        cv = c_ref[...]
        ca = cv * _sigmoid(cv)
        o_ref[...] = _dot(ca, w_ref[...]) + b_ref[...]

    return pl.pallas_call(body, name="ada_mod", out_shape=SDS((nb, cols), F32))(c_all, w_ada, b_ada_mine)


def _tile_rows(T):
    return min(256, T)


def _mod_spec(tps):
    return pl.BlockSpec((None, 8, D_MODEL), lambda i: (i // tps, 0, 0))


def _in_proj(x2, mod8, pre_w, w_in_bf, T, ride_srcs, ride_modes):
    N = x2.shape[0]
    TM = _tile_rows(T)
    tps = T // TM
    nr = len(ride_srcs)

    def body(*refs):
        x_ref, mod_ref, pw_ref, w_ref = refs[:4]
        ride_in = refs[4:4 + nr]
        proj_ref, h1_ref = refs[4 + nr:6 + nr]
        ride_out = refs[6 + nr:6 + 2 * nr]
        sems = refs[6 + 2 * nr:]
        _ride_start(ride_modes, pl.program_id(0) == 0, ride_in, ride_out, sems)
        x = x_ref[...]
        r = lax.rsqrt(_mean_last(x * x) + EPS)
        h = (x * r * pw_ref[...]) * (1.0 + mod_ref[1:2, :]) + mod_ref[0:1, :]
        hb = _bf(h)
        h1_ref[...] = hb
        proj_ref[...] = _dot(hb, w_ref[...])
        _ride_wait(ride_modes, pl.program_id(0) == N // TM - 1, ride_in, ride_out, sems)

    return pl.pallas_call(
        body, name="in_proj", grid=(N // TM,),
        in_specs=[pl.BlockSpec((TM, D_MODEL), lambda i: (i, 0)), _mod_spec(tps),
                  pl.BlockSpec((1, D_MODEL), lambda i: (0, 0)),
                  pl.BlockSpec((D_MODEL, IN_COLS), lambda i: (0, 0))] + [ANY_SPEC] * nr,
        out_specs=[pl.BlockSpec((TM, IN_COLS), lambda i: (i, 0)),
                   pl.BlockSpec((TM, D_MODEL), lambda i: (i, 0))] + [ANY_SPEC] * nr,
        out_shape=[SDS((N, IN_COLS), F32), SDS((N, D_MODEL), BF16)] + _exchange_shapes(ride_srcs, ride_modes),
        scratch_shapes=_exchange_sems(nr),
        compiler_params=_params(("arbitrary",), VMEM_LIMIT_BIG),
    )(x2, mod8, pre_w, w_in_bf, *ride_srcs)


def _rope_tables(T):
    half = ROPE_DIM // 2
    inv_freq = ROPE_THETA ** (-jnp.arange(0, ROPE_DIM, 2, dtype=F32) / ROPE_DIM)
    ang = jnp.arange(T, dtype=F32)[:, None] * inv_freq[None, :]
    cos, sin = jnp.cos(ang), jnp.sin(ang)
    ones = jnp.ones((T, ATT_HEAD_DIM - ROPE_DIM), F32)
    zeros = jnp.zeros((T, ATT_HEAD_DIM - ROPE_DIM), F32)
    zh = jnp.zeros((T, half), F32)
    cos64 = jnp.concatenate([cos, cos, ones], axis=1)
    sin_left = jnp.concatenate([-sin, zh, zeros], axis=1)
    sin_right = jnp.concatenate([zh, sin, zeros], axis=1)
    rep = LANES // ATT_HEAD_DIM
    return jnp.tile(cos64, (1, rep)), jnp.tile(sin_left, (1, rep)), jnp.tile(sin_right, (1, rep))


def _rope(xc, cs, sl, sr):
    return xc * cs + pltpu.roll(xc, LANES - 8, 1) * sl + pltpu.roll(xc, 8, 1) * sr


def _rope_t(dy, cs, sl, sr):
    return dy * cs + pltpu.roll(dy * sl, 8, 1) + pltpu.roll(dy * sr, LANES - 8, 1)


ATT_SCALE = ATT_HEAD_DIM ** -0.5


def _band_masks():
    cols = ATT_GROUP * WINDOW
    j = lax.broadcasted_iota(jnp.int32, (2 * WINDOW, cols), 0)
    i = lax.broadcasted_iota(jnp.int32, (2 * WINDOW, cols), 1) & (WINDOW - 1)
    diff = i + WINDOW - j
    return (diff >= 0) & (diff < WINDOW), j >= WINDOW


def _sink_row(sink_ref, hk):
    return jnp.concatenate(
        [jnp.full((1, WINDOW), sink_ref[0, ATT_GROUP * hk + g], F32) for g in range(ATT_GROUP)], axis=1)


def _softmax_band(qs, kk, mask, sink):
    s = jnp.where(mask, _dot_nt(kk, qs), jnp.finfo(F32).min)
    m = jnp.maximum(jnp.max(s, axis=0, keepdims=True), sink)
    p = jnp.exp(s - m)
    es = jnp.exp(sink - m)
    inv = 1.0 / (jnp.sum(p, axis=0, keepdims=True) + es)
    return p, inv, es


def _stack_heads(parts, hk):
    hs = []
    for g in range(ATT_GROUP):
        h = ATT_GROUP * hk + g
        hs.append(parts[h // 2][:, (h % 2) * ATT_HEAD_DIM:(h % 2 + 1) * ATT_HEAD_DIM])
    return jnp.concatenate(hs, axis=0)


def _attn_fwd(proj3, tables, sinks, attn_w, ride_srcs, ride_modes):
    B, T, _ = proj3.shape
    nb = T // WINDOW
    nr = len(ride_srcs)
    cos, sinl, sinr = tables

    def body(*refs):
        q_ref, k_ref, v_ref, cos_ref, sl_ref, sr_ref, sink_ref, aw_ref = refs[:8]
        ride_in = refs[8:8 + nr]
        o_ref, an_ref, qr_ref, kr_ref = refs[8 + nr:12 + nr]
        ride_out = refs[12 + nr:12 + 2 * nr]
        kpad, vpad = refs[12 + 2 * nr:14 + 2 * nr]
        sems = refs[14 + 2 * nr:]
        _ride_start(ride_modes, pl.program_id(0) == 0, ride_in, ride_out, sems)

        kpad[0:WINDOW, :] = jnp.zeros((WINDOW, LANES), BF16)
        vpad[0:WINDOW, :] = jnp.zeros((WINDOW, LANES), BF16)
        window, current = _band_masks()

        def block(n, carry):
            r0 = pl.multiple_of(n * WINDOW, WINDOW)
            rows = pl.ds(r0, WINDOW)
            nxt = pl.ds(r0 + WINDOW, WINDOW)
            band = pl.ds(r0, 2 * WINDOW)
            cs, sl, sr = cos_ref[rows, :], sl_ref[rows, :], sr_ref[rows, :]
            kb = _bf(_rope(k_ref[rows, :], cs, sl, sr))
            kpad[nxt, :] = kb
            kr_ref[rows, :] = kb
            vpad[nxt, :] = _bf(v_ref[rows, :])
            qparts = []
            for j in range(ATT_WIDTH // LANES):
                qp = _bf(_rope(q_ref[rows, j * LANES:(j + 1) * LANES], cs, sl, sr) * ATT_SCALE)
                qr_ref[rows, j * LANES:(j + 1) * LANES] = qp
                qparts.append(qp)
            mask = window & (current | (n > 0))
            for hk in range(ATT_KV_HEADS):
                lanes = slice(hk * ATT_HEAD_DIM, (hk + 1) * ATT_HEAD_DIM)
                qs = _stack_heads(qparts, hk)
                p, inv, _ = _softmax_band(qs, kpad[band, lanes], mask, _sink_row(sink_ref, hk))
                ot = _dot_tn(vpad[band, lanes], _bf(p)) * inv
                for g in range(ATT_GROUP):
                    h = ATT_GROUP * hk + g
                    o_ref[rows, h * ATT_HEAD_DIM:(h + 1) * ATT_HEAD_DIM] = ot[:, g * WINDOW:(g + 1) * WINDOW].T
            ob = o_ref[rows, :]
            an_ref[rows, :] = _bf(ob * lax.rsqrt(_mean_last(ob * ob) + EPS) * aw_ref[...])
            return carry

        lax.fori_loop(0, nb, block, 0)
        _ride_wait(ride_modes, pl.program_id(0) == B - 1, ride_in, ride_out, sems)

    seq = lambda w, j: pl.BlockSpec((None, T, w), lambda b: (b, 0, j))
    full = lambda r, w: pl.BlockSpec((r, w), lambda b: (0, 0))
    return pl.pallas_call(
        body, name="attn_fwd", grid=(B,),
        in_specs=[seq(ATT_WIDTH, 0), seq(LANES, 4), seq(LANES, 5),
                  full(T, LANES), full(T, LANES), full(T, LANES),
                  pl.BlockSpec(memory_space=pltpu.SMEM), full(1, ATT_WIDTH)] + [ANY_SPEC] * nr,
        out_specs=[seq(ATT_WIDTH, 0), seq(ATT_WIDTH, 0), seq(ATT_WIDTH, 0), seq(LANES, 0)] + [ANY_SPEC] * nr,
        out_shape=[SDS((B, T, ATT_WIDTH), F32), SDS((B, T, ATT_WIDTH), BF16),
                   SDS((B, T, ATT_WIDTH), BF16), SDS((B, T, LANES), BF16)] + _exchange_shapes(ride_srcs, ride_modes),
        scratch_shapes=[pltpu.VMEM((T + WINDOW, LANES), BF16), pltpu.VMEM((T + WINDOW, LANES), BF16)]
        + _exchange_sems(nr),
        compiler_params=_params(("arbitrary",), VMEM_LIMIT_BIG),
    )(proj3, proj3, proj3, cos, sinl, sinr, sinks, attn_w, *ride_srcs)


HG_GROUP = 8
HG_ROWS = HG_GROUP * HG_CHUNK


def _group_masks():
    r = lax.broadcasted_iota(jnp.int32, (HG_ROWS, HG_ROWS), 0)
    c = lax.broadcasted_iota(jnp.int32, (HG_ROWS, HG_ROWS), 1)
    same = (r // HG_CHUNK) == (c // HG_CHUNK)
    return same & (r >= c), same & (c >= r), same & (c > r)


def _ones_bf(mask):
    return jnp.where(mask, 1.0, 0.0).astype(BF16)


def _hgrn_gates(hq, hf, lb, sums_bf, ebl_scr):
    sq = _sigmoid(hq)
    q = hq * sq
    sg = _sigmoid(hf)
    f = lb + (1.0 - lb) * sg
    k = 1.0 - f
    cs = _tri_sum(sums_bf, jnp.log(f))
    b, rem = cs[:HG_ROWS], cs[HG_ROWS:]
    eb, enb, e2 = jnp.exp(b), jnp.exp(-b), jnp.exp(rem)
    ebl_scr[...] = eb * e2
    return dict(sq=sq, sg=sg, f=f, eb=eb, enb=enb, e2=e2, qd=q * eb, kd=k * enb, k2=k * e2)


def _hgrn_specs(B, T):
    head = lambda base: pl.BlockSpec((None, T, LANES), lambda b, h: (b, 0, base + h))
    return head


def _chunk_rows(c):
    return slice(c * HG_CHUNK, (c + 1) * HG_CHUNK)


def _hgrn_fwd(proj3, lb, hg_w, ride_srcs, ride_modes):
    B, T, _ = proj3.shape
    nc = T // HG_CHUNK
    ng = T // HG_ROWS
    nr = len(ride_srcs)
    head = _hgrn_specs(B, T)

    def body(*refs):
        hq_ref, hf_ref, hi_ref, hg_ref, lb_ref, gw_ref = refs[:6]
        ride_in = refs[6:6 + nr]
        o_ref, rg_ref, sp_ref = refs[6 + nr:9 + nr]
        ride_out = refs[9 + nr:9 + 2 * nr]
        st, ebl_scr = refs[9 + 2 * nr:11 + 2 * nr]
        sems = refs[11 + 2 * nr:]
        first = (pl.program_id(0) == 0) & (pl.program_id(1) == 0)
        last = (pl.program_id(0) == B - 1) & (pl.program_id(1) == HG_HEADS - 1)
        _ride_start(ride_modes, first, ride_in, ride_out, sems)

        st[...] = jnp.zeros((HG_HEAD_DIM, HG_HEAD_DIM), F32)
        lo, _, ups = _group_masks()
        sums_bf = jnp.concatenate([_ones_bf(lo), _ones_bf(ups)], axis=0)
        lbv = lb_ref[...]

        def group(gi, carry):
            rows = pl.ds(pl.multiple_of(gi * HG_ROWS, HG_ROWS), HG_ROWS)
            gt = _hgrn_gates(hq_ref[rows, :], hf_ref[rows, :], lbv, sums_bf, ebl_scr)
            v, qd, kd, k2 = _bf(hi_ref[rows, :]), _bf(gt["qd"]), _bf(gt["kd"]), _bf(gt["k2"])
            a = jnp.where(lo, _dot_nt(qd, kd), 0.0)
            o = _dot(_bf(a), v)
            s = st[...]
            inter = []
            for c in range(HG_GROUP):
                cr = _chunk_rows(c)
                sp_ref[gi * HG_GROUP + c] = s
                inter.append(_dot_nt(qd[cr], _bf(s)))
                s = s * ebl_scr[c * HG_CHUNK:c * HG_CHUNK + 1, :] + _dot_tn(v[cr], k2[cr])
            st[...] = s
            o = o + jnp.concatenate(inter, axis=0)
            o_ref[rows, :] = o
            hg = hg_ref[rows, :]
            rn = o * lax.rsqrt(_mean_last(o * o) + EPS) * gw_ref[...]
            rg_ref[rows, :] = _bf(rn * (hg * _sigmoid(hg)))
            return carry

        lax.fori_loop(0, ng, group, 0)
        _ride_wait(ride_modes, last, ride_in, ride_out, sems)

    out_head = pl.BlockSpec((None, T, LANES), lambda b, h: (b, 0, h))
    return pl.pallas_call(
        body, name="hgrn_fwd", grid=(B, HG_HEADS),
        in_specs=[head(6), head(10), head(14), head(18),
                  pl.BlockSpec((1, LANES), lambda b, h: (0, h)),
                  pl.BlockSpec((1, LANES), lambda b, h: (0, 0))] + [ANY_SPEC] * nr,
        out_specs=[out_head, out_head,
                   pl.BlockSpec((None, None, nc, HG_HEAD_DIM, HG_HEAD_DIM), lambda b, h: (b, h, 0, 0, 0))]
        + [ANY_SPEC] * nr,
        out_shape=[SDS((B, T, HG_WIDTH), F32), SDS((B, T, HG_WIDTH), BF16),
                   SDS((B, HG_HEADS, nc, HG_HEAD_DIM, HG_HEAD_DIM), F32)] + _exchange_shapes(ride_srcs, ride_modes),
        scratch_shapes=[pltpu.VMEM((HG_HEAD_DIM, HG_HEAD_DIM), F32), pltpu.VMEM((HG_ROWS, LANES), F32)]
        + _exchange_sems(nr),
        compiler_params=_params(("arbitrary", "arbitrary"), VMEM_LIMIT_BIG),
    )(proj3, proj3, proj3, proj3, lb, hg_w, *ride_srcs)


def _mix_out(x2, attn_n, rec_g, mod8, post_w, w_out_bf, T, ride_srcs, ride_modes):
    N = x2.shape[0]
    TM = _tile_rows(T)
    tps = T // TM
    nr = len(ride_srcs)

    def body(*refs):
        x_ref, an_ref, rg_ref, mod_ref, pw_ref, w_ref = refs[:6]
        ride_in = refs[6:6 + nr]
        mix_ref, x1_ref, cat_ref = refs[6 + nr:9 + nr]
        ride_out = refs[9 + nr:9 + 2 * nr]
        sems = refs[9 + 2 * nr:]
        _ride_start(ride_modes, pl.program_id(0) == 0, ride_in, ride_out, sems)
        cat = jnp.concatenate([an_ref[...], rg_ref[...]], axis=1)
        cat_ref[...] = cat
        mix = _dot(cat, w_ref[...])
        mix_ref[...] = mix
        r = lax.rsqrt(_mean_last(mix * mix) + EPS)
        x1_ref[...] = x_ref[...] + mod_ref[2:3, :] * (mix * r * pw_ref[...])
        _ride_wait(ride_modes, pl.program_id(0) == N // TM - 1, ride_in, ride_out, sems)

    row = lambda w: pl.BlockSpec((TM, w), lambda i: (i, 0))
    return pl.pallas_call(
        body, name="mix_out", grid=(N // TM,),
        in_specs=[row(D_MODEL), row(ATT_WIDTH), row(HG_WIDTH), _mod_spec(tps),
                  pl.BlockSpec((1, D_MODEL), lambda i: (0, 0)),
                  pl.BlockSpec((D_MODEL, D_MODEL), lambda i: (0, 0))] + [ANY_SPEC] * nr,
        out_specs=[row(D_MODEL), row(D_MODEL), row(D_MODEL)] + [ANY_SPEC] * nr,
        out_shape=[SDS((N, D_MODEL), F32), SDS((N, D_MODEL), F32), SDS((N, D_MODEL), BF16)]
        + _exchange_shapes(ride_srcs, ride_modes),
        scratch_shapes=_exchange_sems(nr),
        compiler_params=_params(("arbitrary",), VMEM_LIMIT_BIG),
    )(x2, attn_n, rec_g, mod8, post_w, w_out_bf, *ride_srcs)


def _load_weights_once(pairs, sem):
    @pl.when(pl.program_id(0) == 0)
    def _():
        cps = [pltpu.make_async_copy(src, dst, sem.at[i]) for i, (src, dst) in enumerate(pairs)]
        for cp in cps:
            cp.start()
        for cp in cps:
            cp.wait()


def _mlp_fwd(x1, mod8, pre_w, w_up_bf, w_down_bf, T):
    N = x1.shape[0]
    TM = _tile_rows(T)
    tps = T // TM

    def body(x_ref, mod_ref, pw_ref, wu_hbm, wd_hbm, up_ref, d_ref, h2_ref, wu, wd, sem):
        _load_weights_once([(wu_hbm, wu), (wd_hbm, wd)], sem)
        x = x_ref[...]
        r = lax.rsqrt(_mean_last(x * x) + EPS)
        h = (x * r * pw_ref[...]) * (1.0 + mod_ref[4:5, :]) + mod_ref[3:4, :]
        hb = _bf(h)
        h2_ref[...] = hb
        up = _dot(hb, wu[...])
        up_ref[...] = up
        ru = jnp.maximum(up, 0.0)
        d_ref[...] = _dot(_bf(ru * ru), wd[...])

    row = lambda w: pl.BlockSpec((TM, w), lambda i: (i, 0))
    return pl.pallas_call(
        body, name="mlp_fwd", grid=(N // TM,),
        in_specs=[row(D_MODEL), _mod_spec(tps), pl.BlockSpec((1, D_MODEL), lambda i: (0, 0)),
                  pl.BlockSpec(memory_space=pl.ANY), pl.BlockSpec(memory_space=pl.ANY)],
        out_specs=[row(D_FF), row(D_MODEL), row(D_MODEL)],
        out_shape=[SDS((N, D_FF), F32), SDS((N, D_MODEL), F32), SDS((N, D_MODEL), BF16)],
        scratch_shapes=[pltpu.VMEM((D_MODEL, D_FF), BF16), pltpu.VMEM((D_FF, D_MODEL), BF16),
                        pltpu.SemaphoreType.DMA((2,))],
        compiler_params=_params(("arbitrary",), VMEM_LIMIT_BIG),
    )(x1, mod8, pre_w, w_up_bf, w_down_bf)


def _acc_rows(acc_ref, first, rows):
    @pl.when(first)
    def _():
        acc_ref[...] = jnp.zeros(acc_ref.shape, F32)
    for i, r in enumerate(rows):
        acc_ref[i:i + 1, :] += r


def _mlp_bwd(x1, d, up, tgt, mod8, pre_w, post_w, w_down_bf, w_up_bf, T):
    N = x1.shape[0]
    TM = _tile_rows(T)
    tps = T // TM

    def body(x_ref, d_ref, up_ref, t_ref, mod_ref, pw_ref, qw_ref, wd_hbm, wu_hbm,
             dx_ref, u_ref, dup_ref, dd_ref, acc_ref, wd, wu, sem):
        _load_weights_once([(wd_hbm, wd), (wu_hbm, wu)], sem)
        sh2, sc2, g2 = mod_ref[3:4, :], mod_ref[4:5, :], mod_ref[5:6, :]
        x = x_ref[...]
        r1 = lax.rsqrt(_mean_last(x * x) + EPS)
        xh = x * r1
        n2 = xh * pw_ref[...]
        dv = d_ref[...]
        rd = lax.rsqrt(_mean_last(dv * dv) + EPS)
        dh = dv * rd
        rr = dh * qw_ref[...]
        e = x + g2 * rr - t_ref[...]
        loss = 0.5 * jnp.sum(_sum_rows(e * e), axis=1, keepdims=True) / D_MODEL
        dy = e * (1.0 / D_MODEL)
        dg2 = _sum_rows(dy * rr)
        drr = dy * g2
        dw_post = _sum_rows(drr * dh)
        ddh = drr * qw_ref[...]
        dd = _bf(rd * (ddh - dh * _mean_last(ddh * dh)))
        dd_ref[...] = dd
        ru = jnp.maximum(up_ref[...], 0.0)
        u_ref[...] = _bf(ru * ru)
        dup = _bf(_dot_nt(dd, wd[...]) * (2.0 * ru))
        dup_ref[...] = dup
        dh2 = _dot_nt(dup, wu[...])
        dsh2 = _sum_rows(dh2)
        dsc2 = _sum_rows(dh2 * n2)
        dn2 = dh2 * (1.0 + sc2)
        dw_pre = _sum_rows(dn2 * xh)
        dxh = dn2 * pw_ref[...]
        dx_ref[...] = dy + r1 * (dxh - xh * _mean_last(dxh * xh))
        _acc_rows(acc_ref, pl.program_id(0) % tps == 0,
                  [dsh2, dsc2, dg2, dw_pre, dw_post, jnp.broadcast_to(loss, (1, D_MODEL))])

    row = lambda w: pl.BlockSpec((TM, w), lambda i: (i, 0))
    vec = pl.BlockSpec((1, D_MODEL), lambda i: (0, 0))
    B = N // T
    return pl.pallas_call(
        body, name="mlp_bwd", grid=(N // TM,),
        in_specs=[row(D_MODEL), row(D_MODEL), row(D_FF), row(D_MODEL), _mod_spec(tps), vec, vec,
                  pl.BlockSpec(memory_space=pl.ANY), pl.BlockSpec(memory_space=pl.ANY)],
        out_specs=[row(D_MODEL), row(D_FF), row(D_FF), row(D_MODEL), _mod_spec(tps)],
        out_shape=[SDS((N, D_MODEL), F32), SDS((N, D_FF), BF16), SDS((N, D_FF), BF16),
                   SDS((N, D_MODEL), BF16), SDS((B, 8, D_MODEL), F32)],
        scratch_shapes=[pltpu.VMEM((D_FF, D_MODEL), BF16), pltpu.VMEM((D_MODEL, D_FF), BF16),
                        pltpu.SemaphoreType.DMA((2,))],
        compiler_params=_params(("arbitrary",), VMEM_LIMIT_BIG),
    )(x1, d, up, tgt, mod8, pre_w, post_w, w_down_bf, w_up_bf)


def _mix_bwd(mix, dx1, mod8, post_w, w_out_bf, T):
    N = mix.shape[0]
    TM = _tile_rows(T)
    tps = T // TM

    def body(mix_ref, dx_ref, mod_ref, pw_ref, w_ref, dan_ref, drg_ref, dmix_ref, acc_ref):
        g1 = mod_ref[2:3, :]
        mix = mix_ref[...]
        dx1 = dx_ref[...]
        rm = lax.rsqrt(_mean_last(mix * mix) + EPS)
        mh = mix * rm
        dg1 = _sum_rows(dx1 * (mh * pw_ref[...]))
        dr = dx1 * g1
        dw_post = _sum_rows(dr * mh)
        dmh = dr * pw_ref[...]
        dmix = _bf(rm * (dmh - mh * _mean_last(dmh * mh)))
        dmix_ref[...] = dmix
        dcat = _dot_nt(dmix, w_ref[...])
        dan_ref[...] = dcat[:, :ATT_WIDTH]
        drg_ref[...] = dcat[:, ATT_WIDTH:]
        _acc_rows(acc_ref, pl.program_id(0) % tps == 0, [dg1, dw_post])

    row = lambda w: pl.BlockSpec((TM, w), lambda i: (i, 0))
    B = N // T
    return pl.pallas_call(
        body, name="mix_bwd", grid=(N // TM,),
        in_specs=[row(D_MODEL), row(D_MODEL), _mod_spec(tps), pl.BlockSpec((1, D_MODEL), lambda i: (0, 0)),
                  pl.BlockSpec((D_MODEL, D_MODEL), lambda i: (0, 0))],
        out_specs=[row(ATT_WIDTH), row(HG_WIDTH), row(D_MODEL), _mod_spec(tps)],
        out_shape=[SDS((N, ATT_WIDTH), F32), SDS((N, HG_WIDTH), F32), SDS((N, D_MODEL), BF16),
                   SDS((B, 8, D_MODEL), F32)],
        compiler_params=_params(("arbitrary",), VMEM_LIMIT_BIG),
    )(mix, dx1, mod8, post_w, w_out_bf)


def _hgrn_bwd(proj3, lb, hg_w, o, s_prev, drg, ride_srcs, ride_modes):
    B, T, _ = proj3.shape
    nc = T // HG_CHUNK
    ng = T // HG_ROWS
    nr = len(ride_srcs)
    head = _hgrn_specs(B, T)

    def body(*refs):
        hq_ref, hf_ref, hi_ref, hg_ref, lb_ref, gw_ref, o_ref, sp_ref, drg_ref = refs[:9]
        ride_in = refs[9:9 + nr]
        dhq_ref, dhf_ref, dhi_ref, dhg_ref, dlb_ref, dgw_ref = refs[9 + nr:15 + nr]
        ride_out = refs[15 + nr:15 + 2 * nr]
        dst, ebl_scr = refs[15 + 2 * nr:17 + 2 * nr]
        sems = refs[17 + 2 * nr:]
        first = (pl.program_id(0) == 0) & (pl.program_id(1) == 0)
        last = (pl.program_id(0) == B - 1) & (pl.program_id(1) == HG_HEADS - 1)
        _ride_start(ride_modes, first, ride_in, ride_out, sems)

        dst[...] = jnp.zeros((HG_HEAD_DIM, HG_HEAD_DIM), F32)
        lo, up, ups = _group_masks()
        sums_bf = jnp.concatenate([_ones_bf(lo), _ones_bf(ups)], axis=0)
        up_bf = _ones_bf(up)
        last_row = lax.broadcasted_iota(jnp.int32, (HG_CHUNK, LANES), 0) == HG_CHUNK - 1
        lbv = lb_ref[...]
        gw = gw_ref[...]

        def group(i, carry):
            dlb, dgw = carry
            gi = ng - 1 - i
            rows = pl.ds(pl.multiple_of(gi * HG_ROWS, HG_ROWS), HG_ROWS)
            hq = hq_ref[rows, :]
            gt = _hgrn_gates(hq, hf_ref[rows, :], lbv, sums_bf, ebl_scr)
            sq, sg, qdf, kdf, k2f = gt["sq"], gt["sg"], gt["qd"], gt["kd"], gt["k2"]
            v, qd, kd, k2 = _bf(hi_ref[rows, :]), _bf(qdf), _bf(kdf), _bf(k2f)
            ov = o_ref[rows, :]
            hg = hg_ref[rows, :]
            shg = _sigmoid(hg)
            dr = drg_ref[rows, :]
            ro = lax.rsqrt(_mean_last(ov * ov) + EPS)
            oh = ov * ro
            dhg_ref[rows, :] = dr * (oh * gw) * (shg + hg * shg * (1.0 - shg))
            drn = dr * (hg * shg)
            dgw = dgw + _sum_rows(drn * oh)
            doh = drn * gw
            do = _bf(ro * (doh - oh * _mean_last(doh * oh)))
            a = jnp.where(lo, _dot_nt(qd, kd), 0.0)
            da = _bf(jnp.where(lo, _dot_nt(do, v), 0.0))
            dv = _dot_tn(_bf(a), do)
            dqd = _dot(da, kd)
            dkd = _dot_tn(da, qd)
            ds = dst[...]
            dk2_l, dv_l, dqd_l, dbl_l = [None] * HG_GROUP, [None] * HG_GROUP, [None] * HG_GROUP, [None] * HG_GROUP
            for c in reversed(range(HG_GROUP)):
                cr = _chunk_rows(c)
                sp = sp_ref[gi * HG_GROUP + c]
                ebl = ebl_scr[c * HG_CHUNK:c * HG_CHUNK + 1, :]
                dsb = _bf(ds)
                dk2_c = _dot(v[cr], dsb)
                dk2_l[c] = dk2_c
                dv_l[c] = _dot_nt(k2[cr], dsb)
                dqd_l[c] = _dot(do[cr], _bf(sp))
                dbl = _sum_rows(ds * sp) * ebl + _sum_rows(dk2_c * k2f[cr])
                dbl_l[c] = jnp.where(last_row, dbl, 0.0)
                ds = ds * ebl + _dot_tn(do[cr], qd[cr])
            dst[...] = ds
            dk2 = jnp.concatenate(dk2_l, axis=0)
            dhi_ref[rows, :] = dv + jnp.concatenate(dv_l, axis=0)
            dqd = dqd + jnp.concatenate(dqd_l, axis=0)
            db = dqd * qdf - dkd * kdf - dk2 * k2f + jnp.concatenate(dbl_l, axis=0)
            dk = dkd * gt["enb"] + dk2 * gt["e2"]
            df = _tri_sum(up_bf, db) / gt["f"] - dk
            dhf_ref[rows, :] = df * (1.0 - lbv) * sg * (1.0 - sg)
            dlb = dlb + _sum_rows(df * (1.0 - sg))
            dhq_ref[rows, :] = (dqd * gt["eb"]) * (sq + hq * sq * (1.0 - sq))
            return dlb, dgw

        zero = jnp.zeros((1, LANES), F32)
        dlb, dgw = lax.fori_loop(0, ng, group, (zero, zero))
        dlb_ref[...] = jnp.broadcast_to(dlb, (8, LANES))
        dgw_ref[...] = jnp.broadcast_to(dgw, (8, LANES))
        _ride_wait(ride_modes, last, ride_in, ride_out, sems)

    out_head = pl.BlockSpec((None, T, LANES), lambda b, h: (b, 0, h))
    small = pl.BlockSpec((None, 8, LANES), lambda b, h: (b, 0, h))
    return pl.pallas_call(
        body, name="hgrn_bwd", grid=(B, HG_HEADS),
        in_specs=[head(6), head(10), head(14), head(18),
                  pl.BlockSpec((1, LANES), lambda b, h: (0, h)),
                  pl.BlockSpec((1, LANES), lambda b, h: (0, 0)),
                  out_head,
                  pl.BlockSpec((None, None, nc, HG_HEAD_DIM, HG_HEAD_DIM), lambda b, h: (b, h, 0, 0, 0)),
                  out_head] + [ANY_SPEC] * nr,
        out_specs=[out_head, out_head, out_head, out_head, small, small] + [ANY_SPEC] * nr,
        out_shape=[SDS((B, T, HG_WIDTH), F32)] * 4 + [SDS((B, 8, HG_WIDTH), F32)] * 2
        + _exchange_shapes(ride_srcs, ride_modes),
        scratch_shapes=[pltpu.VMEM((HG_HEAD_DIM, HG_HEAD_DIM), F32), pltpu.VMEM((HG_ROWS, LANES), F32)]
        + _exchange_sems(nr),
        compiler_params=_params(("arbitrary", "arbitrary"), VMEM_LIMIT_BIG),
    )(proj3, proj3, proj3, proj3, lb, hg_w, o, s_prev, drg, *ride_srcs)


def _attn_bwd(qr, kr, proj3, attn_o, dan, tables, sinks, attn_w, ride_srcs, ride_modes):
    B, T, _ = proj3.shape
    nb = T // WINDOW
    nr = len(ride_srcs)
    cos, sinl, sinr = tables
    QKV = ATT_WIDTH + 2 * LANES

    def body(*refs):
        qr_ref, kr_ref, v_ref, o_ref, dan_ref, cos_ref, sl_ref, sr_ref, sink_ref, aw_ref = refs[:10]
        ride_in = refs[10:10 + nr]
        dqkv_ref, dsink_ref, daw_ref = refs[10 + nr:13 + nr]
        ride_out = refs[13 + nr:13 + 2 * nr]
        kpad, vpad, dkpad, dvpad, dqb, dsk = refs[13 + 2 * nr:19 + 2 * nr]
        sems = refs[19 + 2 * nr:]
        _ride_start(ride_modes, pl.program_id(0) == 0, ride_in, ride_out, sems)

        window, current = _band_masks()
        kpad[0:WINDOW, :] = jnp.zeros((WINDOW, LANES), BF16)
        vpad[0:WINDOW, :] = jnp.zeros((WINDOW, LANES), BF16)
        kpad[WINDOW:, :] = kr_ref[...]
        vpad[WINDOW:, :] = _bf(v_ref[...])
        dkpad[...] = jnp.zeros(dkpad.shape, F32)
        dvpad[...] = jnp.zeros(dvpad.shape, F32)
        dsk[...] = jnp.zeros(dsk.shape, F32)
        aw = aw_ref[...]

        def block(n, daw):
            r0 = pl.multiple_of(n * WINDOW, WINDOW)
            rows = pl.ds(r0, WINDOW)
            band = pl.ds(r0, 2 * WINDOW)
            ob = o_ref[rows, :]
            dn = dan_ref[rows, :]
            ro = lax.rsqrt(_mean_last(ob * ob) + EPS)
            oh = ob * ro
            daw = daw + _sum_rows(dn * oh)
            doh = dn * aw
            do = _bf(ro * (doh - oh * _mean_last(doh * oh)))
            doparts = [do[:, j * LANES:(j + 1) * LANES] for j in range(ATT_WIDTH // LANES)]
            qparts = [qr_ref[rows, j * LANES:(j + 1) * LANES] for j in range(ATT_WIDTH // LANES)]
            mask = window & (current | (n > 0))
            for hk in range(ATT_KV_HEADS):
                lanes = slice(hk * ATT_HEAD_DIM, (hk + 1) * ATT_HEAD_DIM)
                qs = _stack_heads(qparts, hk)
                dos = _stack_heads(doparts, hk)
                kk, vv = kpad[band, lanes], vpad[band, lanes]
                p, inv, es = _softmax_band(qs, kk, mask, _sink_row(sink_ref, hk))
                p = p * inv
                dp = _dot_nt(vv, dos)
                delta = jnp.sum(p * dp, axis=0, keepdims=True)
                ds = _bf(p * (dp - delta))
                sk = (es * inv) * delta
                dqt = _dot_tn(kk, ds) * ATT_SCALE
                dkpad[band, lanes] += _dot(ds, qs)
                dvpad[band, lanes] += _dot(_bf(p), dos)
                for g in range(ATT_GROUP):
                    h = ATT_GROUP * hk + g
                    cols = slice(g * WINDOW, (g + 1) * WINDOW)
                    dqb[:, h * ATT_HEAD_DIM:(h + 1) * ATT_HEAD_DIM] = dqt[:, cols].T
                    dsk[h:h + 1, :] += jnp.broadcast_to(-jnp.sum(sk[:, cols], axis=1, keepdims=True), (1, LANES))
            cs, sl, sr = cos_ref[rows, :], sl_ref[rows, :], sr_ref[rows, :]
            for j in range(ATT_WIDTH // LANES):
                dqkv_ref[rows, j * LANES:(j + 1) * LANES] = _rope_t(dqb[:, j * LANES:(j + 1) * LANES], cs, sl, sr)
            return daw

        daw = lax.fori_loop(0, nb, block, jnp.zeros((1, ATT_WIDTH), F32))
        daw_ref[...] = jnp.broadcast_to(daw, (8, ATT_WIDTH))
        dsink_ref[...] = dsk[...]

        def finish(n, carry):
            r0 = pl.multiple_of(n * WINDOW, WINDOW)
            rows = pl.ds(r0, WINDOW)
            nxt = pl.ds(r0 + WINDOW, WINDOW)
            cs, sl, sr = cos_ref[rows, :], sl_ref[rows, :], sr_ref[rows, :]
            dqkv_ref[rows, ATT_WIDTH:ATT_WIDTH + LANES] = _rope_t(dkpad[nxt, :], cs, sl, sr)
            dqkv_ref[rows, ATT_WIDTH + LANES:QKV] = dvpad[nxt, :]
            return carry

        lax.fori_loop(0, nb, finish, 0)
        _ride_wait(ride_modes, pl.program_id(0) == B - 1, ride_in, ride_out, sems)

    seq = lambda w, j: pl.BlockSpec((None, T, w), lambda b: (b, 0, j))
    full = lambda r, w: pl.BlockSpec((r, w), lambda b: (0, 0))
    return pl.pallas_call(
        body, name="attn_bwd", grid=(B,),
        in_specs=[seq(ATT_WIDTH, 0), seq(LANES, 0), seq(LANES, 5), seq(ATT_WIDTH, 0), seq(ATT_WIDTH, 0),
                  full(T, LANES), full(T, LANES), full(T, LANES),
                  pl.BlockSpec(memory_space=pltpu.SMEM), full(1, ATT_WIDTH)] + [ANY_SPEC] * nr,
        out_specs=[seq(QKV, 0), pl.BlockSpec((None, 8, LANES), lambda b: (b, 0, 0)),
                   pl.BlockSpec((None, 8, ATT_WIDTH), lambda b: (b, 0, 0))] + [ANY_SPEC] * nr,
        out_shape=[SDS((B, T, QKV), F32), SDS((B, 8, LANES), F32), SDS((B, 8, ATT_WIDTH), F32)]
        + _exchange_shapes(ride_srcs, ride_modes),
        scratch_shapes=[pltpu.VMEM((T + WINDOW, LANES), BF16), pltpu.VMEM((T + WINDOW, LANES), BF16),
                        pltpu.VMEM((T + WINDOW, LANES), F32), pltpu.VMEM((T + WINDOW, LANES), F32),
                        pltpu.VMEM((WINDOW, ATT_WIDTH), F32), pltpu.VMEM((8, LANES), F32)] + _exchange_sems(nr),
        compiler_params=_params(("arbitrary",), VMEM_LIMIT_BIG),
    )(qr, kr, proj3, attn_o, dan, cos, sinl, sinr, sinks, attn_w, *ride_srcs)


def _in_bwd(x2, dx1, dqkv, dhq, dhf, dhi, dhg, mod8, pre_w, w_in_bf, T):
    N = x2.shape[0]
    TM = _tile_rows(T)
    tps = T // TM
    pieces = [(0, ATT_WIDTH + 2 * LANES), (768, HG_WIDTH), (1280, HG_WIDTH), (1792, HG_WIDTH), (2304, HG_WIDTH)]

    def body(x_ref, dx_ref, p0, p1, p2, p3, p4, mod_ref, pw_ref, w_ref, gx_ref, dproj_ref, acc_ref):
        sc1 = mod_ref[1:2, :]
        dh = jnp.zeros((TM, D_MODEL), F32)
        for ref, (off, width) in zip((p0, p1, p2, p3, p4), pieces):
            pb = _bf(ref[...])
            dproj_ref[:, off:off + width] = pb
            dh = dh + _dot_nt(pb, w_ref[:, off:off + width])
        x = x_ref[...]
        r = lax.rsqrt(_mean_last(x * x) + EPS)
        xh = x * r
        n1 = xh * pw_ref[...]
        dsh1 = _sum_rows(dh)
        dsc1 = _sum_rows(dh * n1)
        dn1 = dh * (1.0 + sc1)
        dw_pre = _sum_rows(dn1 * xh)
        dxh = dn1 * pw_ref[...]
        gx_ref[...] = dx_ref[...] + r * (dxh - xh * _mean_last(dxh * xh))
        _acc_rows(acc_ref, pl.program_id(0) % tps == 0, [dsh1, dsc1, dw_pre])

    row = lambda w: pl.BlockSpec((TM, w), lambda i: (i, 0))
    B = N // T
    return pl.pallas_call(
        body, name="in_bwd", grid=(N // TM,),
        in_specs=[row(D_MODEL), row(D_MODEL), row(768), row(HG_WIDTH), row(HG_WIDTH), row(HG_WIDTH),
                  row(HG_WIDTH), _mod_spec(tps), pl.BlockSpec((1, D_MODEL), lambda i: (0, 0)),
                  pl.BlockSpec((D_MODEL, IN_COLS), lambda i: (0, 0))],
        out_specs=[row(D_MODEL), row(IN_COLS), _mod_spec(tps)],
        out_shape=[SDS((N, D_MODEL), F32), SDS((N, IN_COLS), BF16), SDS((B, 8, D_MODEL), F32)],
        compiler_params=_params(("arbitrary",), VMEM_LIMIT_BIG),
    )(x2, dx1, dqkv, dhq, dhf, dhi, dhg, mod8, pre_w, w_in_bf)


def _matmul_tn(name, a, b, tn, by_columns=False):
    K, M = a.shape
    Nc = b.shape[1]
    tm = min(512, M)

    def body(a_ref, b_ref, o_ref):
        o_ref[...] = _bf(_dot_tn(a_ref[...], b_ref[...]))

    if by_columns:
        out_shape = SDS((Nc // tn, M, tn), BF16)
        out_spec = pl.BlockSpec((None, tm, tn), lambda i, j: (j, i, 0))
    else:
        out_shape = SDS((M, Nc), BF16)
        out_spec = pl.BlockSpec((tm, tn), lambda i, j: (i, j))
    return pl.pallas_call(
        body, name=name, grid=(M // tm, Nc // tn),
        in_specs=[pl.BlockSpec((K, tm), lambda i, j: (0, i)),
                  pl.BlockSpec((K, tn), lambda i, j: (0, j))],
        out_specs=out_spec, out_shape=out_shape,
        compiler_params=_params(("arbitrary", "arbitrary"), VMEM_LIMIT_BIG),
    )(a, b)


def _adamw_math(w, g, m, v):
    m2 = ADAM_B1 * m + (1.0 - ADAM_B1) * g
    v2 = ADAM_B2 * v + (1.0 - ADAM_B2) * (g * g)
    m_hat = m2 / (1.0 - ADAM_B1 ** ADAM_STEP)
    v_hat = v2 / (1.0 - ADAM_B2 ** ADAM_STEP)
    delta = -ADAM_LR * (m_hat / (jnp.sqrt(v_hat) + ADAM_EPS) + ADAM_WD * w)
    return delta, m2, v2


def _reduce_adamw(name, parts, w, m, v):
    r, c = w.shape
    tr = r if r <= 256 else 256

    def body(p_ref, w_ref, m_ref, v_ref, g_ref, d_ref, m2_ref, v2_ref):
        g = p_ref[0].astype(F32)
        for s in range(1, N_DEV):
            g = g + p_ref[s].astype(F32)
        g_ref[...] = g
        d_ref[...], m2_ref[...], v2_ref[...] = _adamw_math(w_ref[...], g, m_ref[...], v_ref[...])

    blk = pl.BlockSpec((tr, c), lambda i: (i, 0))
    return pl.pallas_call(
        body, name=name, grid=(r // tr,),
        in_specs=[pl.BlockSpec((N_DEV, tr, c), lambda i: (0, i, 0)), blk, blk, blk],
        out_specs=[blk] * 4, out_shape=[SDS((r, c), F32)] * 4,
        compiler_params=_params(("arbitrary",), VMEM_LIMIT_BIG),
    )(parts, w, m, v)


def _ada_grad_adamw(c_all, dmod_all, w, m, v):
    r, c = w.shape
    tr = 256
    nb = c_all.shape[0]

    def body(c_ref, dm_ref, w_ref, m_ref, v_ref, g_ref, d_ref, m2_ref, v2_ref):
        cv = c_ref[...]
        g = _dot_tn(cv * _sigmoid(cv), dm_ref[...])
        g_ref[...] = g
        d_ref[...], m2_ref[...], v2_ref[...] = _adamw_math(w_ref[...], g, m_ref[...], v_ref[...])

    blk = pl.BlockSpec((tr, c), lambda i: (i, 0))
    return pl.pallas_call(
        body, name="ada_grad_adamw", grid=(r // tr,),
        in_specs=[pl.BlockSpec((nb, tr), lambda i: (0, i)), pl.BlockSpec((nb, c), lambda i: (0, 0)),
                  blk, blk, blk],
        out_specs=[blk] * 4, out_shape=[SDS((r, c), F32)] * 4,
        compiler_params=_params(("arbitrary",)),
    )(c_all, dmod_all, w, m, v)


_SMALL = [("b_ada", 6144), ("pre_w_mix", 1024), ("attn_sinks", 128), ("attn_out_w", 512), ("lb_table", 1024),
          ("hg_norm_w", 128), ("post_w_mix", 1024), ("pre_w_mlp", 1024), ("post_w_mlp", 1024)]


def _pack_small(vals):
    out = []
    for name, width in _SMALL:
        f = vals[name].reshape(-1).astype(F32)
        out.append(jnp.pad(f, (0, width - f.shape[0])))
    return jnp.concatenate(out).reshape(1, -1)


def _adamw_small(parts, given):
    names = [n for n, _ in _SMALL]
    flat_in = [a for n in names for a in given[n]]

    def body(*refs):
        p_ref = refs[0]
        in_refs = refs[1:1 + 3 * len(names)]
        out_refs = refs[1 + 3 * len(names):]
        g = p_ref[0]
        for s in range(1, N_DEV):
            g = g + p_ref[s]
        off = 0
        for i, (name, width) in enumerate(_SMALL):
            w_ref, m_ref, v_ref = in_refs[3 * i:3 * i + 3]
            rows, cols = w_ref.shape
            for r in range(rows):
                gr = g[:, off + r * cols:off + (r + 1) * cols]
                res = (gr,) + _adamw_math(w_ref[r:r + 1, :], gr, m_ref[r:r + 1, :], v_ref[r:r + 1, :])
                for o_ref, val in zip(out_refs[4 * i:4 * i + 4], res):
                    o_ref[r:r + 1, :] = val
            off += width

    out_shape = [SDS(given[n][0].shape, F32) for n in names for _ in range(4)]
    outs = pl.pallas_call(body, name="adamw_small", out_shape=out_shape)(parts, *flat_in)
    return {n: tuple(outs[4 * i:4 * i + 4]) for i, n in enumerate(names)}


def _columns_to_full(g):
    return g.transpose(1, 0, 2).reshape(g.shape[1], -1)


def kernel(x, c, w_ada, b_ada, pre_w_mix, w_in, attn_sinks, attn_out_w, lb_table, hg_norm_w, w_out, post_w_mix, pre_w_mlp, w_up, w_down, post_w_mlp, loss_target, m_w_ada, m_b_ada, m_pre_w_mix, m_w_in, m_attn_sinks, m_attn_out_w, m_lb_table, m_hg_norm_w, m_w_out, m_post_w_mix, m_pre_w_mlp, m_w_up, m_w_down, m_post_w_mlp, v_w_ada, v_b_ada, v_pre_w_mix, v_w_in, v_attn_sinks, v_attn_out_w, v_lb_table, v_hg_norm_w, v_w_out, v_post_w_mix, v_pre_w_mlp, v_w_up, v_w_down, v_post_w_mlp):
    B, T, _ = x.shape
    N = B * T
    me = 4 * lax.axis_index("x") + 2 * lax.axis_index("y") + lax.axis_index("c")
    x2 = x.reshape(N, D_MODEL)
    tgt2 = loss_target.reshape(N, D_MODEL)

    w_in_g, c_g = _exchange("gather_w_in", [_bf(w_in[0]), c], ["gather"] * 2)
    w_in_f = _columns_to_full(w_in_g)
    c_all = c_g.reshape(N_DEV * B, D_MODEL)

    ada_cols = w_ada.shape[2]
    b_mine = lax.dynamic_slice(b_ada, (0, me * ada_cols), (1, ada_cols))
    mod_cols = _ada_mod(c_all, w_ada[0], b_mine)
    (mod_g,) = _exchange("scatter_mod", [mod_cols.reshape(N_DEV, B, ada_cols)], ["a2a"])
    mod = mod_g.transpose(1, 0, 2).reshape(B, 6, D_MODEL)
    mod8 = jnp.pad(mod, ((0, 0), (0, 2), (0, 0)))

    lb_p = jax.nn.softmax(lb_table, axis=0)
    lb = lb_p[1:2]
    tables = _rope_tables(T)

    w_down_b = _bf(w_down[0])
    half = w_down_b.shape[0] // 2
    proj, h1, w_out_g = _in_proj(x2, mod8, pre_w_mix, w_in_f, T, [_bf(w_out[0])], ["gather"])
    proj3 = proj.reshape(B, T, IN_COLS)
    rec_o, rec_g, s_prev, w_up_g = _hgrn_fwd(proj3, lb, hg_norm_w, [_bf(w_up[0])], ["gather"])
    attn_o, attn_n, qr, kr, w_down_g0 = _attn_fwd(proj3, tables, attn_sinks, attn_out_w,
                                                  [w_down_b[:half]], ["gather"])
    w_out_f = w_out_g.reshape(D_MODEL, D_MODEL)
    mix, x1, cat, w_down_g1 = _mix_out(x2, attn_n.reshape(N, ATT_WIDTH), rec_g.reshape(N, HG_WIDTH), mod8,
                                       post_w_mix, w_out_f, T, [w_down_b[half:]], ["gather"])
    w_up_f = _columns_to_full(w_up_g)
    w_down_f = jnp.concatenate([w_down_g0, w_down_g1], axis=1).reshape(D_FF, D_MODEL)
    up, d, h2 = _mlp_fwd(x1, mod8, pre_w_mlp, w_up_f, w_down_f, T)

    dx1, u, dup, dd, acc_mlp = _mlp_bwd(x1, d, up, tgt2, mod8, pre_w_mlp, post_w_mlp,
                                        w_down_f, w_up_f, T)
    dan, drg, dmix, acc_mix = _mix_bwd(mix, dx1, mod8, post_w_mix, w_out_f, T)
    gw_out = _matmul_tn("grad_w_out", cat, dmix, 512).reshape(N_DEV, D_MODEL // N_DEV, D_MODEL)
    gw_up = _matmul_tn("grad_w_up", h2, dup, D_FF // N_DEV, by_columns=True)
    gw_down = _matmul_tn("grad_w_down", u, dd, 512).reshape(N_DEV, D_FF // N_DEV, D_MODEL)
    dhq, dhf, dhi, dhg, dlb_p, dgw_p, r_down, r_up = _hgrn_bwd(
        proj3, lb, hg_norm_w, rec_o, s_prev, drg.reshape(B, T, HG_WIDTH), [gw_down, gw_up], ["a2a"] * 2)
    dqkv, dsink_p, daw_p, r_out = _attn_bwd(qr, kr, proj3, attn_o, dan.reshape(B, T, ATT_WIDTH), tables,
                                            attn_sinks, attn_out_w, [gw_out], ["a2a"])
    flat = lambda a: a.reshape(N, a.shape[-1])
    grad_x, dproj, acc_in = _in_bwd(x2, dx1, flat(dqkv), flat(dhq), flat(dhf), flat(dhi), flat(dhg),
                                    mod8, pre_w_mix, w_in_f, T)

    gw_in = _matmul_tn("grad_w_in", h1, dproj, IN_COLS // 2)
    in_cols = w_in.shape[2]
    gw_in = gw_in.reshape(D_MODEL, N_DEV, in_cols).transpose(1, 0, 2)

    dmod = jnp.concatenate([acc_in[:, 0:2], acc_mix[:, 0:1], acc_mlp[:, 0:3]], axis=1)
    dlb = dlb_p[:, 0].sum(0)
    dlb_table = jnp.stack([-dlb, dlb]) * (lb_p[0] * lb_p[1])[None, :]
    small = {
        "b_ada": dmod.sum(0),
        "pre_w_mix": acc_in[:, 2].sum(0),
        "attn_sinks": dsink_p[:, :, 0].sum(0),
        "attn_out_w": daw_p[:, 0].sum(0),
        "lb_table": dlb_table,
        "hg_norm_w": dgw_p[:, 0].reshape(B, HG_HEADS, LANES).sum((0, 1)),
        "post_w_mix": acc_mix[:, 1].sum(0),
        "pre_w_mlp": acc_mlp[:, 3].sum(0),
        "post_w_mlp": acc_mlp[:, 4].sum(0),
    }
    loss_part = acc_mlp[:, 5, 0].sum()
    dmod_blocks = dmod.reshape(B, N_DEV, ada_cols).transpose(1, 0, 2)

    r_in, r_dmod, r_small = _exchange(
        "reduce_grads", [gw_in, dmod_blocks, _pack_small(small)], ["a2a", "a2a", "gather"])

    res = {}
    res["w_in"] = _reduce_adamw("adamw_w_in", r_in, w_in[0], m_w_in[0], v_w_in[0])
    res["w_out"] = _reduce_adamw("adamw_w_out", r_out, w_out[0], m_w_out[0], v_w_out[0])
    res["w_up"] = _reduce_adamw("adamw_w_up", r_up, w_up[0], m_w_up[0], v_w_up[0])
    res["w_down"] = _reduce_adamw("adamw_w_down", r_down, w_down[0], m_w_down[0], v_w_down[0])
    res["w_ada"] = _ada_grad_adamw(c_all, r_dmod.reshape(N_DEV * B, ada_cols), w_ada[0], m_w_ada[0], v_w_ada[0])

    given = dict(b_ada=(b_ada, m_b_ada, v_b_ada), pre_w_mix=(pre_w_mix, m_pre_w_mix, v_pre_w_mix),
                 attn_sinks=(attn_sinks, m_attn_sinks, v_attn_sinks),
                 attn_out_w=(attn_out_w, m_attn_out_w, v_attn_out_w), lb_table=(lb_table, m_lb_table, v_lb_table),
                 hg_norm_w=(hg_norm_w, m_hg_norm_w, v_hg_norm_w), post_w_mix=(post_w_mix, m_post_w_mix, v_post_w_mix),
                 pre_w_mlp=(pre_w_mlp, m_pre_w_mlp, v_pre_w_mlp), post_w_mlp=(post_w_mlp, m_post_w_mlp, v_post_w_mlp))
    res.update(_adamw_small(r_small, given))

    loss = lax.psum(loss_part, ("x", "y", "c"))
    order = ["w_ada", "b_ada", "pre_w_mix", "w_in", "attn_sinks", "attn_out_w", "lb_table", "hg_norm_w", "w_out",
             "post_w_mix", "pre_w_mlp", "w_up", "w_down", "post_w_mlp"]
    big = {"w_ada", "w_in", "w_out", "w_up", "w_down"}
    outs = [loss, grad_x.reshape(B, T, D_MODEL)]
    for i in range(4):
        for k in order:
            a = res[k][i]
            outs.append(a[None] if k in big else a)
    return tuple(outs)
```

```python
import functools

import jax
import jax.numpy as jnp
from jax import lax
from jax.experimental import pallas as pl
from jax.experimental.pallas import tpu as pltpu

F32 = jnp.float32
BF16 = jnp.bfloat16
SDS = jax.ShapeDtypeStruct

D_MODEL = 1024
ATT_WIDTH = 512
ATT_HEAD_DIM = 64
ATT_KV_HEADS = 2
ATT_GROUP = 4
WINDOW = 128
ROPE_DIM = 16
ROPE_THETA = 500000.0
HG_WIDTH = 512
HG_HEAD_DIM = 128
HG_HEADS = 4
HG_CHUNK = 32
IN_COLS = 2816
D_FF = 4096
EPS = 1e-6
N_DEV = 8

ADAM_LR = 0.001
ADAM_B1 = 0.9
ADAM_B2 = 0.999
ADAM_EPS = 1e-08
ADAM_WD = 0.01
ADAM_STEP = 10

VMEM_LIMIT_BIG = 56 << 20
LANES = 128

MESH = pl.DeviceIdType.MESH
NT_DIMS = (((1,), (1,)), ((), ()))
TN_DIMS = (((0,), (0,)), ((), ()))


def _dot(a, b):
    return jnp.dot(a, b, preferred_element_type=F32)


def _dot_nt(a, b):
    return lax.dot_general(a, b, NT_DIMS, preferred_element_type=F32)


def _dot_tn(a, b):
    return lax.dot_general(a, b, TN_DIMS, preferred_element_type=F32)


def _bf(a):
    return a.astype(BF16)


def _sigmoid(a):
    return 1.0 / (1.0 + jnp.exp(-a))


def _mean_last(a):
    return jnp.mean(a, axis=-1, keepdims=True)


def _sum_rows(a):
    return jnp.sum(a, axis=0, keepdims=True)


def _tri_sum(tri_bf, a):
    a1 = _bf(a)
    r1 = a - a1.astype(F32)
    a2 = _bf(r1)
    a3 = _bf(r1 - a2.astype(F32))
    return _dot(tri_bf, a1) + _dot(tri_bf, a2) + _dot(tri_bf, a3)


def _params(sem=None, vmem=None):
    kw = {}
    if sem is not None:
        kw["dimension_semantics"] = sem
    if vmem is not None:
        kw["vmem_limit_bytes"] = vmem
    return pltpu.CompilerParams(**kw)


ANY_SPEC = pl.BlockSpec(memory_space=pl.ANY)


def _exchange_shapes(srcs, modes):
    out_shape = []
    for s, m in zip(srcs, modes):
        shp = (N_DEV,) + tuple(s.shape) if m == "gather" else tuple(s.shape)
        out_shape.append(SDS(shp, s.dtype))
    return out_shape


def _exchange_sems(n):
    if n == 0:
        return []
    return [pltpu.SemaphoreType.DMA((n, N_DEV - 1)), pltpu.SemaphoreType.DMA((n, N_DEV - 1)),
            pltpu.SemaphoreType.DMA((n,))]


SIBLING = 1
OTHER_CHIPS = (2, 4, 6)


def _related(k):
    x, y, c = lax.axis_index("x"), lax.axis_index("y"), lax.axis_index("c")
    px, py, pc = x ^ ((k >> 2) & 1), y ^ ((k >> 1) & 1), c ^ (k & 1)
    return (px, py, pc), 4 * px + 2 * py + pc


def _exchange_phases(modes, src_refs, out_refs, send_sems, recv_sems, own_sems):
    _, me = _related(0)
    sib_dev, sib = _related(SIBLING)
    start, middle, end = [], [], []

    def remote(a, i, src, dst, dev):
        return pltpu.make_async_remote_copy(src_ref=src, dst_ref=dst, send_sem=send_sems.at[a, i],
                                            recv_sem=recv_sems.at[a, i], device_id=dev, device_id_type=MESH)

    for a, mode in enumerate(modes):
        out = out_refs[a]
        if mode == "gather":
            src = src_refs[a]
            own = pltpu.make_async_copy(src, out.at[me], own_sems.at[a])
            to_sib = remote(a, 0, src, out.at[me], sib_dev)
            start += [own.start, to_sib.start]
            end += [remote(a, 0, src, out.at[sib], sib_dev).wait_recv, to_sib.wait_send, own.wait]
            for j, k in enumerate(OTHER_CHIPS, start=1):
                dev, peer = _related(k)
                _, peer_sib = _related(k ^ SIBLING)
                send = remote(a, j, src, out.at[me], dev)
                passed = remote(a, 3 + j, out.at[peer], out.at[peer], sib_dev)
                start.append(send.start)
                middle += [remote(a, j, src, out.at[peer], dev).wait_recv, passed.start]
                end += [remote(a, 3 + j, out.at[peer_sib], out.at[peer_sib], sib_dev).wait_recv,
                        send.wait_send, passed.wait_send]
        else:
            own = pltpu.make_async_copy(src_refs[a].at[me], out.at[me], own_sems.at[a])
            start.append(own.start)
            end.append(own.wait)
            for k in range(1, N_DEV):
                dev, peer = _related(k)
                send = remote(a, k - 1, src_refs[a].at[peer], out.at[me], dev)
                start.append(send.start)
                end += [remote(a, k - 1, src_refs[a].at[peer], out.at[peer], dev).wait_recv, send.wait_send]
    return start, middle, end


def _run(actions):
    for act in actions:
        act()


def _exchange(name, srcs, modes):
    n = len(srcs)

    def body(*refs):
        start, middle, end = _exchange_phases(modes, refs[:n], refs[n:2 * n], *refs[2 * n:])
        _run(start)
        _run(middle)
        _run(end)

    return pl.pallas_call(
        body, name=name, out_shape=_exchange_shapes(srcs, modes),
        in_specs=[ANY_SPEC] * n, out_specs=[ANY_SPEC] * n,
        scratch_shapes=_exchange_sems(n),
    )(*srcs)


def _ride_start(modes, step, steps, src_refs, out_refs, sems):
    if not modes:
        return
    middle_step = (3 * steps) // 4

    @pl.when(step == 0)
    def _():
        _run(_exchange_phases(modes, src_refs, out_refs, *sems)[0])

    if "gather" in modes:
        @pl.when(step == middle_step)
        def _():
            _run(_exchange_phases(modes, src_refs, out_refs, *sems)[1])


def _ride_wait(modes, step, steps, src_refs, out_refs, sems):
    if not modes:
        return

    @pl.when(step == steps - 1)
    def _():
        _run(_exchange_phases(modes, src_refs, out_refs, *sems)[2])


def _ada_mod(c_all, w_ada, b_ada_mine):
    nb, cols = c_all.shape[0], w_ada.shape[1]

    def body(c_ref, w_ref, b_ref, o_ref):
        cv = c_ref[...]
        ca = cv * _sigmoid(cv)
        o_ref[...] = _dot(ca, w_ref[...]) + b_ref[...]

    return pl.pallas_call(body, name="ada_mod", out_shape=SDS((nb, cols), F32))(c_all, w_ada, b_ada_mine)


def _tile_rows(T):
    return min(256, T)


def _mod_spec(tps):
    return pl.BlockSpec((None, 8, D_MODEL), lambda i: (i // tps, 0, 0))


def _in_proj(x2, mod8, pre_w, w_in_bf, T, ride_srcs, ride_modes):
    N = x2.shape[0]
    TM = _tile_rows(T)
    tps = T // TM
    nr = len(ride_srcs)

    def body(*refs):
        x_ref, mod_ref, pw_ref, w_ref = refs[:4]
        ride_in = refs[4:4 + nr]
        proj_ref, h1_ref = refs[4 + nr:6 + nr]
        ride_out = refs[6 + nr:6 + 2 * nr]
        sems = refs[6 + 2 * nr:]
        _ride_start(ride_modes, pl.program_id(0), N // TM, ride_in, ride_out, sems)
        x = x_ref[...]
        r = lax.rsqrt(_mean_last(x * x) + EPS)
        h = (x * r * pw_ref[...]) * (1.0 + mod_ref[1:2, :]) + mod_ref[0:1, :]
        hb = _bf(h)
        h1_ref[...] = hb
        proj_ref[...] = _dot(hb, w_ref[...])
        _ride_wait(ride_modes, pl.program_id(0), N // TM, ride_in, ride_out, sems)

    return pl.pallas_call(
        body, name="in_proj", grid=(N // TM,),
        in_specs=[pl.BlockSpec((TM, D_MODEL), lambda i: (i, 0)), _mod_spec(tps),
                  pl.BlockSpec((1, D_MODEL), lambda i: (0, 0)),
                  pl.BlockSpec((D_MODEL, IN_COLS), lambda i: (0, 0))] + [ANY_SPEC] * nr,
        out_specs=[pl.BlockSpec((TM, IN_COLS), lambda i: (i, 0)),
                   pl.BlockSpec((TM, D_MODEL), lambda i: (i, 0))] + [ANY_SPEC] * nr,
        out_shape=[SDS((N, IN_COLS), F32), SDS((N, D_MODEL), BF16)] + _exchange_shapes(ride_srcs, ride_modes),
        scratch_shapes=_exchange_sems(nr),
        compiler_params=_params(("arbitrary",), VMEM_LIMIT_BIG),
    )(x2, mod8, pre_w, w_in_bf, *ride_srcs)


def _rope_tables(T):
    half = ROPE_DIM // 2
    inv_freq = ROPE_THETA ** (-jnp.arange(0, ROPE_DIM, 2, dtype=F32) / ROPE_DIM)
    ang = jnp.arange(T, dtype=F32)[:, None] * inv_freq[None, :]
    cos, sin = jnp.cos(ang), jnp.sin(ang)
    ones = jnp.ones((T, ATT_HEAD_DIM - ROPE_DIM), F32)
    zeros = jnp.zeros((T, ATT_HEAD_DIM - ROPE_DIM), F32)
    zh = jnp.zeros((T, half), F32)
    cos64 = jnp.concatenate([cos, cos, ones], axis=1)
    sin_left = jnp.concatenate([-sin, zh, zeros], axis=1)
    sin_right = jnp.concatenate([zh, sin, zeros], axis=1)
    rep = LANES // ATT_HEAD_DIM
    return jnp.tile(cos64, (1, rep)), jnp.tile(sin_left, (1, rep)), jnp.tile(sin_right, (1, rep))


def _rope(xc, cs, sl, sr):
    return xc * cs + pltpu.roll(xc, LANES - 8, 1) * sl + pltpu.roll(xc, 8, 1) * sr


def _rope_t(dy, cs, sl, sr):
    return dy * cs + pltpu.roll(dy * sl, 8, 1) + pltpu.roll(dy * sr, LANES - 8, 1)


ATT_SCALE = ATT_HEAD_DIM ** -0.5


def _band_masks():
    cols = ATT_GROUP * WINDOW
    j = lax.broadcasted_iota(jnp.int32, (2 * WINDOW, cols), 0)
    i = lax.broadcasted_iota(jnp.int32, (2 * WINDOW, cols), 1) & (WINDOW - 1)
    diff = i + WINDOW - j
    return (diff >= 0) & (diff < WINDOW), j >= WINDOW


def _sink_row(sink_ref, hk):
    return jnp.concatenate(
        [jnp.full((1, WINDOW), sink_ref[0, ATT_GROUP * hk + g], F32) for g in range(ATT_GROUP)], axis=1)


def _softmax_band(qs, kk, mask, sink):
    s = jnp.where(mask, _dot_nt(kk, qs), jnp.finfo(F32).min)
    m = jnp.maximum(jnp.max(s, axis=0, keepdims=True), sink)
    p = jnp.exp(s - m)
    es = jnp.exp(sink - m)
    inv = 1.0 / (jnp.sum(p, axis=0, keepdims=True) + es)
    return p, inv, es


def _stack_heads(parts, hk):
    hs = []
    for g in range(ATT_GROUP):
        h = ATT_GROUP * hk + g
        hs.append(parts[h // 2][:, (h % 2) * ATT_HEAD_DIM:(h % 2 + 1) * ATT_HEAD_DIM])
    return jnp.concatenate(hs, axis=0)


def _attn_fwd(proj3, tables, sinks, attn_w, ride_srcs, ride_modes):
    B, T, _ = proj3.shape
    nb = T // WINDOW
    nr = len(ride_srcs)
    cos, sinl, sinr = tables

    def body(*refs):
        q_ref, k_ref, v_ref, cos_ref, sl_ref, sr_ref, sink_ref, aw_ref = refs[:8]
        ride_in = refs[8:8 + nr]
        o_ref, an_ref, qr_ref, kr_ref = refs[8 + nr:12 + nr]
        ride_out = refs[12 + nr:12 + 2 * nr]
        kpad, vpad = refs[12 + 2 * nr:14 + 2 * nr]
        sems = refs[14 + 2 * nr:]
        _ride_start(ride_modes, pl.program_id(0), B, ride_in, ride_out, sems)

        kpad[0:WINDOW, :] = jnp.zeros((WINDOW, LANES), BF16)
        vpad[0:WINDOW, :] = jnp.zeros((WINDOW, LANES), BF16)
        window, current = _band_masks()

        def block(n, carry):
            r0 = pl.multiple_of(n * WINDOW, WINDOW)
            rows = pl.ds(r0, WINDOW)
            nxt = pl.ds(r0 + WINDOW, WINDOW)
            band = pl.ds(r0, 2 * WINDOW)
            cs, sl, sr = cos_ref[rows, :], sl_ref[rows, :], sr_ref[rows, :]
            kb = _bf(_rope(k_ref[rows, :], cs, sl, sr))
            kpad[nxt, :] = kb
            kr_ref[rows, :] = kb
            vpad[nxt, :] = _bf(v_ref[rows, :])
            qparts = []
            for j in range(ATT_WIDTH // LANES):
                qp = _bf(_rope(q_ref[rows, j * LANES:(j + 1) * LANES], cs, sl, sr) * ATT_SCALE)
                qr_ref[rows, j * LANES:(j + 1) * LANES] = qp
                qparts.append(qp)
            mask = window & (current | (n > 0))
            for hk in range(ATT_KV_HEADS):
                lanes = slice(hk * ATT_HEAD_DIM, (hk + 1) * ATT_HEAD_DIM)
                qs = _stack_heads(qparts, hk)
                p, inv, _ = _softmax_band(qs, kpad[band, lanes], mask, _sink_row(sink_ref, hk))
                ot = _dot_tn(vpad[band, lanes], _bf(p)) * inv
                for g in range(ATT_GROUP):
                    h = ATT_GROUP * hk + g
                    o_ref[rows, h * ATT_HEAD_DIM:(h + 1) * ATT_HEAD_DIM] = ot[:, g * WINDOW:(g + 1) * WINDOW].T
            ob = o_ref[rows, :]
            an_ref[rows, :] = _bf(ob * lax.rsqrt(_mean_last(ob * ob) + EPS) * aw_ref[...])
            return carry

        lax.fori_loop(0, nb, block, 0)
        _ride_wait(ride_modes, pl.program_id(0), B, ride_in, ride_out, sems)

    seq = lambda w, j: pl.BlockSpec((None, T, w), lambda b: (b, 0, j))
    full = lambda r, w: pl.BlockSpec((r, w), lambda b: (0, 0))
    return pl.pallas_call(
        body, name="attn_fwd", grid=(B,),
        in_specs=[seq(ATT_WIDTH, 0), seq(LANES, 4), seq(LANES, 5),
                  full(T, LANES), full(T, LANES), full(T, LANES),
                  pl.BlockSpec(memory_space=pltpu.SMEM), full(1, ATT_WIDTH)] + [ANY_SPEC] * nr,
        out_specs=[seq(ATT_WIDTH, 0), seq(ATT_WIDTH, 0), seq(ATT_WIDTH, 0), seq(LANES, 0)] + [ANY_SPEC] * nr,
        out_shape=[SDS((B, T, ATT_WIDTH), F32), SDS((B, T, ATT_WIDTH), BF16),
                   SDS((B, T, ATT_WIDTH), BF16), SDS((B, T, LANES), BF16)] + _exchange_shapes(ride_srcs, ride_modes),
        scratch_shapes=[pltpu.VMEM((T + WINDOW, LANES), BF16), pltpu.VMEM((T + WINDOW, LANES), BF16)]
        + _exchange_sems(nr),
        compiler_params=_params(("arbitrary",), VMEM_LIMIT_BIG),
    )(proj3, proj3, proj3, cos, sinl, sinr, sinks, attn_w, *ride_srcs)


HG_GROUP = 8
HG_ROWS = HG_GROUP * HG_CHUNK


HG_STACK = HG_GROUP * HG_HEAD_DIM


def _group_masks():
    r = lax.broadcasted_iota(jnp.int32, (HG_ROWS, HG_ROWS), 0)
    c = lax.broadcasted_iota(jnp.int32, (HG_ROWS, HG_ROWS), 1)
    same = (r // HG_CHUNK) == (c // HG_CHUNK)
    return same & (r >= c), same & (c >= r), same & (c > r), same


def _row_chunk():
    return lax.broadcasted_iota(jnp.int32, (HG_ROWS, HG_HEAD_DIM), 0) // HG_CHUNK


def _spread(a, row_chunk):
    return jnp.concatenate([jnp.where(row_chunk == c, a, jnp.zeros_like(a)) for c in range(HG_GROUP)], axis=1)


def _pick(r, row_chunk):
    out = jnp.where(row_chunk == 0, r[:, :HG_HEAD_DIM], 0.0)
    for c in range(1, HG_GROUP):
        out = out + jnp.where(row_chunk == c, r[:, c * HG_HEAD_DIM:(c + 1) * HG_HEAD_DIM], 0.0)
    return out


def _lane_block(a, c):
    return a[:, c * HG_HEAD_DIM:(c + 1) * HG_HEAD_DIM]


def _ones_bf(mask):
    return jnp.where(mask, 1.0, 0.0).astype(BF16)


def _hgrn_gates(hq, hf, lb, sums_bf, ebl_scr):
    sq = _sigmoid(hq)
    q = hq * sq
    sg = _sigmoid(hf)
    f = lb + (1.0 - lb) * sg
    k = 1.0 - f
    cs = _tri_sum(sums_bf, jnp.log(f))
    b, rem = cs[:HG_ROWS], cs[HG_ROWS:]
    eb, enb, e2 = jnp.exp(b), jnp.exp(-b), jnp.exp(rem)
    ebl_scr[...] = eb * e2
    return dict(sq=sq, sg=sg, f=f, eb=eb, enb=enb, e2=e2, qd=q * eb, kd=k * enb, k2=k * e2)


def _hgrn_specs(B, T):
    head = lambda base: pl.BlockSpec((None, T, LANES), lambda b, h: (b, 0, base + h))
    return head


def _chunk_rows(c):
    return slice(c * HG_CHUNK, (c + 1) * HG_CHUNK)


def _hgrn_fwd(proj3, lb, hg_w, ride_srcs, ride_modes):
    B, T, _ = proj3.shape
    nc = T // HG_CHUNK
    ng = T // HG_ROWS
    nr = len(ride_srcs)
    head = _hgrn_specs(B, T)

    def body(*refs):
        hq_ref, hf_ref, hi_ref, hg_ref, lb_ref, gw_ref = refs[:6]
        ride_in = refs[6:6 + nr]
        o_ref, rg_ref, sp_ref = refs[6 + nr:9 + nr]
        ride_out = refs[9 + nr:9 + 2 * nr]
        st, ebl_scr = refs[9 + 2 * nr:11 + 2 * nr]
        sems = refs[11 + 2 * nr:]
        step = pl.program_id(0) * HG_HEADS + pl.program_id(1)
        _ride_start(ride_modes, step, B * HG_HEADS, ride_in, ride_out, sems)

        st[...] = jnp.zeros((HG_HEAD_DIM, HG_HEAD_DIM), F32)
        lo, _, ups, _ = _group_masks()
        sums_bf = jnp.concatenate([_ones_bf(lo), _ones_bf(ups)], axis=0)
        row_chunk = _row_chunk()
        lbv = lb_ref[...]

        def group(gi, carry):
            rows = pl.ds(pl.multiple_of(gi * HG_ROWS, HG_ROWS), HG_ROWS)
            gt = _hgrn_gates(hq_ref[rows, :], hf_ref[rows, :], lbv, sums_bf, ebl_scr)
            v, qd, kd, k2 = _bf(hi_ref[rows, :]), _bf(gt["qd"]), _bf(gt["kd"]), _bf(gt["k2"])
            a = jnp.where(lo, _dot_nt(qd, kd), 0.0)
            kv = _dot_tn(v, _bf(_spread(gt["k2"], row_chunk)))
            s = st[...]
            before = []
            for c in range(HG_GROUP):
                before.append(s)
                s = s * ebl_scr[c * HG_CHUNK:c * HG_CHUNK + 1, :] + _lane_block(kv, c)
            st[...] = s
            sp = jnp.concatenate(before, axis=1)
            sp_ref[gi] = sp
            o = _dot(_bf(a), v) + _dot_nt(_bf(_spread(gt["qd"], row_chunk)), _bf(sp))
            o_ref[rows, :] = o
            hg = hg_ref[rows, :]
            rn = o * lax.rsqrt(_mean_last(o * o) + EPS) * gw_ref[...]
            rg_ref[rows, :] = _bf(rn * (hg * _sigmoid(hg)))
            return carry

        lax.fori_loop(0, ng, group, 0)
        _ride_wait(ride_modes, step, B * HG_HEADS, ride_in, ride_out, sems)

    out_head = pl.BlockSpec((None, T, LANES), lambda b, h: (b, 0, h))
    return pl.pallas_call(
        body, name="hgrn_fwd", grid=(B, HG_HEADS),
        in_specs=[head(6), head(10), head(14), head(18),
                  pl.BlockSpec((1, LANES), lambda b, h: (0, h)),
                  pl.BlockSpec((1, LANES), lambda b, h: (0, 0))] + [ANY_SPEC] * nr,
        out_specs=[out_head, out_head,
                   pl.BlockSpec((None, None, ng, HG_HEAD_DIM, HG_STACK), lambda b, h: (b, h, 0, 0, 0))]
        + [ANY_SPEC] * nr,
        out_shape=[SDS((B, T, HG_WIDTH), F32), SDS((B, T, HG_WIDTH), BF16),
                   SDS((B, HG_HEADS, ng, HG_HEAD_DIM, HG_STACK), F32)] + _exchange_shapes(ride_srcs, ride_modes),
        scratch_shapes=[pltpu.VMEM((HG_HEAD_DIM, HG_HEAD_DIM), F32), pltpu.VMEM((HG_ROWS, LANES), F32)]
        + _exchange_sems(nr),
        compiler_params=_params(("arbitrary", "arbitrary"), VMEM_LIMIT_BIG),
    )(proj3, proj3, proj3, proj3, lb, hg_w, *ride_srcs)


def _mix_out(x2, attn_n, rec_g, mod8, post_w, w_out_bf, T, ride_srcs, ride_modes):
    N = x2.shape[0]
    TM = _tile_rows(T)
    tps = T // TM
    nr = len(ride_srcs)

    def body(*refs):
        x_ref, an_ref, rg_ref, mod_ref, pw_ref, w_ref = refs[:6]
        ride_in = refs[6:6 + nr]
        mix_ref, x1_ref, cat_ref = refs[6 + nr:9 + nr]
        ride_out = refs[9 + nr:9 + 2 * nr]
        sems = refs[9 + 2 * nr:]
        _ride_start(ride_modes, pl.program_id(0), N // TM, ride_in, ride_out, sems)
        cat = jnp.concatenate([an_ref[...], rg_ref[...]], axis=1)
        cat_ref[...] = cat
        mix = _dot(cat, w_ref[...])
        mix_ref[...] = mix
        r = lax.rsqrt(_mean_last(mix * mix) + EPS)
        x1_ref[...] = x_ref[...] + mod_ref[2:3, :] * (mix * r * pw_ref[...])
        _ride_wait(ride_modes, pl.program_id(0), N // TM, ride_in, ride_out, sems)

    row = lambda w: pl.BlockSpec((TM, w), lambda i: (i, 0))
    return pl.pallas_call(
        body, name="mix_out", grid=(N // TM,),
        in_specs=[row(D_MODEL), row(ATT_WIDTH), row(HG_WIDTH), _mod_spec(tps),
                  pl.BlockSpec((1, D_MODEL), lambda i: (0, 0)),
                  pl.BlockSpec((D_MODEL, D_MODEL), lambda i: (0, 0))] + [ANY_SPEC] * nr,
        out_specs=[row(D_MODEL), row(D_MODEL), row(D_MODEL)] + [ANY_SPEC] * nr,
        out_shape=[SDS((N, D_MODEL), F32), SDS((N, D_MODEL), F32), SDS((N, D_MODEL), BF16)]
        + _exchange_shapes(ride_srcs, ride_modes),
        scratch_shapes=_exchange_sems(nr),
        compiler_params=_params(("arbitrary",), VMEM_LIMIT_BIG),
    )(x2, attn_n, rec_g, mod8, post_w, w_out_bf, *ride_srcs)


def _load_weights_once(pairs, sem):
    @pl.when(pl.program_id(0) == 0)
    def _():
        cps = [pltpu.make_async_copy(src, dst, sem.at[i]) for i, (src, dst) in enumerate(pairs)]
        for cp in cps:
            cp.start()
        for cp in cps:
            cp.wait()


def _mlp_fwd(x1, mod8, pre_w, w_up_bf, w_down_bf, T):
    N = x1.shape[0]
    TM = _tile_rows(T)
    tps = T // TM

    def body(x_ref, mod_ref, pw_ref, wu_hbm, wd_hbm, up_ref, d_ref, h2_ref, wu, wd, sem):
        _load_weights_once([(wu_hbm, wu), (wd_hbm, wd)], sem)
        x = x_ref[...]
        r = lax.rsqrt(_mean_last(x * x) + EPS)
        h = (x * r * pw_ref[...]) * (1.0 + mod_ref[4:5, :]) + mod_ref[3:4, :]
        hb = _bf(h)
        h2_ref[...] = hb
        up = _dot(hb, wu[...])
        up_ref[...] = up
        ru = jnp.maximum(up, 0.0)
        d_ref[...] = _dot(_bf(ru * ru), wd[...])

    row = lambda w: pl.BlockSpec((TM, w), lambda i: (i, 0))
    return pl.pallas_call(
        body, name="mlp_fwd", grid=(N // TM,),
        in_specs=[row(D_MODEL), _mod_spec(tps), pl.BlockSpec((1, D_MODEL), lambda i: (0, 0)),
                  pl.BlockSpec(memory_space=pl.ANY), pl.BlockSpec(memory_space=pl.ANY)],
        out_specs=[row(D_FF), row(D_MODEL), row(D_MODEL)],
        out_shape=[SDS((N, D_FF), F32), SDS((N, D_MODEL), F32), SDS((N, D_MODEL), BF16)],
        scratch_shapes=[pltpu.VMEM((D_MODEL, D_FF), BF16), pltpu.VMEM((D_FF, D_MODEL), BF16),
                        pltpu.SemaphoreType.DMA((2,))],
        compiler_params=_params(("arbitrary",), VMEM_LIMIT_BIG),
    )(x1, mod8, pre_w, w_up_bf, w_down_bf)


def _acc_rows(acc_ref, first, rows):
    @pl.when(first)
    def _():
        acc_ref[...] = jnp.zeros(acc_ref.shape, F32)
    for i, r in enumerate(rows):
        acc_ref[i:i + 1, :] += r


def _mlp_bwd(x1, d, up, tgt, mod8, pre_w, post_w, w_down_bf, w_up_bf, T):
    N = x1.shape[0]
    TM = _tile_rows(T)
    tps = T // TM

    def body(x_ref, d_ref, up_ref, t_ref, mod_ref, pw_ref, qw_ref, wd_hbm, wu_hbm,
             dx_ref, u_ref, dup_ref, dd_ref, acc_ref, wd, wu, sem):
        _load_weights_once([(wd_hbm, wd), (wu_hbm, wu)], sem)
        sh2, sc2, g2 = mod_ref[3:4, :], mod_ref[4:5, :], mod_ref[5:6, :]
        x = x_ref[...]
        r1 = lax.rsqrt(_mean_last(x * x) + EPS)
        xh = x * r1
        n2 = xh * pw_ref[...]
        dv = d_ref[...]
        rd = lax.rsqrt(_mean_last(dv * dv) + EPS)
        dh = dv * rd
        rr = dh * qw_ref[...]
        e = x + g2 * rr - t_ref[...]
        loss = 0.5 * jnp.sum(_sum_rows(e * e), axis=1, keepdims=True) / D_MODEL
        dy = e * (1.0 / D_MODEL)
        dg2 = _sum_rows(dy * rr)
        drr = dy * g2
        dw_post = _sum_rows(drr * dh)
        ddh = drr * qw_ref[...]
        dd = _bf(rd * (ddh - dh * _mean_last(ddh * dh)))
        dd_ref[...] = dd
        ru = jnp.maximum(up_ref[...], 0.0)
        u_ref[...] = _bf(ru * ru)
        dup = _bf(_dot_nt(dd, wd[...]) * (2.0 * ru))
        dup_ref[...] = dup
        dh2 = _dot_nt(dup, wu[...])
        dsh2 = _sum_rows(dh2)
        dsc2 = _sum_rows(dh2 * n2)
        dn2 = dh2 * (1.0 + sc2)
        dw_pre = _sum_rows(dn2 * xh)
        dxh = dn2 * pw_ref[...]
        dx_ref[...] = dy + r1 * (dxh - xh * _mean_last(dxh * xh))
        _acc_rows(acc_ref, pl.program_id(0) % tps == 0,
                  [dsh2, dsc2, dg2, dw_pre, dw_post, jnp.broadcast_to(loss, (1, D_MODEL))])

    row = lambda w: pl.BlockSpec((TM, w), lambda i: (i, 0))
    vec = pl.BlockSpec((1, D_MODEL), lambda i: (0, 0))
    B = N // T
    return pl.pallas_call(
        body, name="mlp_bwd", grid=(N // TM,),
        in_specs=[row(D_MODEL), row(D_MODEL), row(D_FF), row(D_MODEL), _mod_spec(tps), vec, vec,
                  pl.BlockSpec(memory_space=pl.ANY), pl.BlockSpec(memory_space=pl.ANY)],
        out_specs=[row(D_MODEL), row(D_FF), row(D_FF), row(D_MODEL), _mod_spec(tps)],
        out_shape=[SDS((N, D_MODEL), F32), SDS((N, D_FF), BF16), SDS((N, D_FF), BF16),
                   SDS((N, D_MODEL), BF16), SDS((B, 8, D_MODEL), F32)],
        scratch_shapes=[pltpu.VMEM((D_FF, D_MODEL), BF16), pltpu.VMEM((D_MODEL, D_FF), BF16),
                        pltpu.SemaphoreType.DMA((2,))],
        compiler_params=_params(("arbitrary",), VMEM_LIMIT_BIG),
    )(x1, d, up, tgt, mod8, pre_w, post_w, w_down_bf, w_up_bf)


def _mix_bwd(mix, dx1, mod8, post_w, w_out_bf, T):
    N = mix.shape[0]
    TM = _tile_rows(T)
    tps = T // TM

    def body(mix_ref, dx_ref, mod_ref, pw_ref, w_ref, dan_ref, drg_ref, dmix_ref, acc_ref):
        g1 = mod_ref[2:3, :]
        mix = mix_ref[...]
        dx1 = dx_ref[...]
        rm = lax.rsqrt(_mean_last(mix * mix) + EPS)
        mh = mix * rm
        dg1 = _sum_rows(dx1 * (mh * pw_ref[...]))
        dr = dx1 * g1
        dw_post = _sum_rows(dr * mh)
        dmh = dr * pw_ref[...]
        dmix = _bf(rm * (dmh - mh * _mean_last(dmh * mh)))
        dmix_ref[...] = dmix
        dcat = _dot_nt(dmix, w_ref[...])
        dan_ref[...] = dcat[:, :ATT_WIDTH]
        drg_ref[...] = dcat[:, ATT_WIDTH:]
        _acc_rows(acc_ref, pl.program_id(0) % tps == 0, [dg1, dw_post])

    row = lambda w: pl.BlockSpec((TM, w), lambda i: (i, 0))
    B = N // T
    return pl.pallas_call(
        body, name="mix_bwd", grid=(N // TM,),
        in_specs=[row(D_MODEL), row(D_MODEL), _mod_spec(tps), pl.BlockSpec((1, D_MODEL), lambda i: (0, 0)),
                  pl.BlockSpec((D_MODEL, D_MODEL), lambda i: (0, 0))],
        out_specs=[row(ATT_WIDTH), row(HG_WIDTH), row(D_MODEL), _mod_spec(tps)],
        out_shape=[SDS((N, ATT_WIDTH), F32), SDS((N, HG_WIDTH), F32), SDS((N, D_MODEL), BF16),
                   SDS((B, 8, D_MODEL), F32)],
        compiler_params=_params(("arbitrary",), VMEM_LIMIT_BIG),
    )(mix, dx1, mod8, post_w, w_out_bf)


def _hgrn_bwd(proj3, lb, hg_w, o, s_prev, drg, ride_srcs, ride_modes):
    B, T, _ = proj3.shape
    nc = T // HG_CHUNK
    ng = T // HG_ROWS
    nr = len(ride_srcs)
    head = _hgrn_specs(B, T)

    def body(*refs):
        hq_ref, hf_ref, hi_ref, hg_ref, lb_ref, gw_ref, o_ref, sp_ref, drg_ref = refs[:9]
        ride_in = refs[9:9 + nr]
        dhq_ref, dhf_ref, dhi_ref, dhg_ref, dlb_ref, dgw_ref = refs[9 + nr:15 + nr]
        ride_out = refs[15 + nr:15 + 2 * nr]
        dst, ebl_scr = refs[15 + 2 * nr:17 + 2 * nr]
        sems = refs[17 + 2 * nr:]
        step = pl.program_id(0) * HG_HEADS + pl.program_id(1)
        _ride_start(ride_modes, step, B * HG_HEADS, ride_in, ride_out, sems)

        dst[...] = jnp.zeros((HG_HEAD_DIM, HG_HEAD_DIM), F32)
        lo, up, ups, same = _group_masks()
        sums_bf = jnp.concatenate([_ones_bf(lo), _ones_bf(ups)], axis=0)
        back_bf = jnp.concatenate([_ones_bf(up), _ones_bf(same)], axis=1)
        row_chunk = _row_chunk()
        lbv = lb_ref[...]
        gw = gw_ref[...]

        def group(i, carry):
            dlb, dgw = carry
            gi = ng - 1 - i
            rows = pl.ds(pl.multiple_of(gi * HG_ROWS, HG_ROWS), HG_ROWS)
            hq = hq_ref[rows, :]
            gt = _hgrn_gates(hq, hf_ref[rows, :], lbv, sums_bf, ebl_scr)
            sq, sg, qdf, kdf, k2f = gt["sq"], gt["sg"], gt["qd"], gt["kd"], gt["k2"]
            v, qd, kd, k2 = _bf(hi_ref[rows, :]), _bf(qdf), _bf(kdf), _bf(k2f)
            ov = o_ref[rows, :]
            hg = hg_ref[rows, :]
            shg = _sigmoid(hg)
            dr = drg_ref[rows, :]
            ro = lax.rsqrt(_mean_last(ov * ov) + EPS)
            oh = ov * ro
            dhg_ref[rows, :] = dr * (oh * gw) * (shg + hg * shg * (1.0 - shg))
            drn = dr * (hg * shg)
            dgw = dgw + _sum_rows(drn * oh)
            doh = drn * gw
            do = _bf(ro * (doh - oh * _mean_last(doh * oh)))
            a = jnp.where(lo, _dot_nt(qd, kd), 0.0)
            da = _bf(jnp.where(lo, _dot_nt(do, v), 0.0))
            dv = _dot_tn(_bf(a), do)
            dqd = _dot(da, kd)
            dkd = _dot_tn(da, qd)
            sp = sp_ref[gi]
            incr = _dot_tn(do, _bf(_spread(qdf, row_chunk)))
            ds = dst[...]
            after = [None] * HG_GROUP
            for c in reversed(range(HG_GROUP)):
                after[c] = ds
                ds = ds * ebl_scr[c * HG_CHUNK:c * HG_CHUNK + 1, :] + _lane_block(incr, c)
            dst[...] = ds
            dss = jnp.concatenate(after, axis=1)
            dssb = _bf(dss)
            dk2 = _pick(_dot(v, dssb), row_chunk)
            dhi_ref[rows, :] = dv + _dot_nt(_bf(_spread(k2f, row_chunk)), dssb)
            dqd = dqd + _pick(_dot(do, _bf(sp)), row_chunk)
            debl = _sum_rows(dss * sp)
            dbl = jnp.concatenate(
                [jnp.broadcast_to(_lane_block(debl, c) * ebl_scr[c * HG_CHUNK:c * HG_CHUNK + 1, :],
                                  (HG_CHUNK, HG_HEAD_DIM)) for c in range(HG_GROUP)], axis=0)
            k2g = dk2 * k2f
            db = dqd * qdf - dkd * kdf - k2g
            dk = dkd * gt["enb"] + dk2 * gt["e2"]
            dg = _tri_sum(back_bf, jnp.concatenate([db, k2g], axis=0)) + dbl
            df = dg / gt["f"] - dk
            dhf_ref[rows, :] = df * (1.0 - lbv) * sg * (1.0 - sg)
            dlb = dlb + _sum_rows(df * (1.0 - sg))
            dhq_ref[rows, :] = (dqd * gt["eb"]) * (sq + hq * sq * (1.0 - sq))
            return dlb, dgw

        zero = jnp.zeros((1, LANES), F32)
        dlb, dgw = lax.fori_loop(0, ng, group, (zero, zero))
        dlb_ref[...] = jnp.broadcast_to(dlb, (8, LANES))
        dgw_ref[...] = jnp.broadcast_to(dgw, (8, LANES))
        _ride_wait(ride_modes, step, B * HG_HEADS, ride_in, ride_out, sems)

    out_head = pl.BlockSpec((None, T, LANES), lambda b, h: (b, 0, h))
    small = pl.BlockSpec((None, 8, LANES), lambda b, h: (b, 0, h))
    return pl.pallas_call(
        body, name="hgrn_bwd", grid=(B, HG_HEADS),
        in_specs=[head(6), head(10), head(14), head(18),
                  pl.BlockSpec((1, LANES), lambda b, h: (0, h)),
                  pl.BlockSpec((1, LANES), lambda b, h: (0, 0)),
                  out_head,
                  pl.BlockSpec((None, None, ng, HG_HEAD_DIM, HG_STACK), lambda b, h: (b, h, 0, 0, 0)),
                  out_head] + [ANY_SPEC] * nr,
        out_specs=[out_head, out_head, out_head, out_head, small, small] + [ANY_SPEC] * nr,
        out_shape=[SDS((B, T, HG_WIDTH), F32)] * 4 + [SDS((B, 8, HG_WIDTH), F32)] * 2
        + _exchange_shapes(ride_srcs, ride_modes),
        scratch_shapes=[pltpu.VMEM((HG_HEAD_DIM, HG_HEAD_DIM), F32), pltpu.VMEM((HG_ROWS, LANES), F32)]
        + _exchange_sems(nr),
        compiler_params=_params(("arbitrary", "arbitrary"), VMEM_LIMIT_BIG),
    )(proj3, proj3, proj3, proj3, lb, hg_w, o, s_prev, drg, *ride_srcs)


def _attn_bwd(qr, kr, proj3, attn_o, dan, tables, sinks, attn_w, ride_srcs, ride_modes):
    B, T, _ = proj3.shape
    nb = T // WINDOW
    nr = len(ride_srcs)
    cos, sinl, sinr = tables
    QKV = ATT_WIDTH + 2 * LANES

    def body(*refs):
        qr_ref, kr_ref, v_ref, o_ref, dan_ref, cos_ref, sl_ref, sr_ref, sink_ref, aw_ref = refs[:10]
        ride_in = refs[10:10 + nr]
        dqkv_ref, dsink_ref, daw_ref = refs[10 + nr:13 + nr]
        ride_out = refs[13 + nr:13 + 2 * nr]
        kpad, vpad, dkpad, dvpad, dqb, dsk = refs[13 + 2 * nr:19 + 2 * nr]
        sems = refs[19 + 2 * nr:]
        _ride_start(ride_modes, pl.program_id(0), B, ride_in, ride_out, sems)

        window, current = _band_masks()
        kpad[0:WINDOW, :] = jnp.zeros((WINDOW, LANES), BF16)
        vpad[0:WINDOW, :] = jnp.zeros((WINDOW, LANES), BF16)
        kpad[WINDOW:, :] = kr_ref[...]
        vpad[WINDOW:, :] = _bf(v_ref[...])
        dkpad[...] = jnp.zeros(dkpad.shape, F32)
        dvpad[...] = jnp.zeros(dvpad.shape, F32)
        dsk[...] = jnp.zeros(dsk.shape, F32)
        aw = aw_ref[...]

        def block(n, daw):
            r0 = pl.multiple_of(n * WINDOW, WINDOW)
            rows = pl.ds(r0, WINDOW)
            band = pl.ds(r0, 2 * WINDOW)
            ob = o_ref[rows, :]
            dn = dan_ref[rows, :]
            ro = lax.rsqrt(_mean_last(ob * ob) + EPS)
            oh = ob * ro
            daw = daw + _sum_rows(dn * oh)
            doh = dn * aw
            do = _bf(ro * (doh - oh * _mean_last(doh * oh)))
            doparts = [do[:, j * LANES:(j + 1) * LANES] for j in range(ATT_WIDTH // LANES)]
            qparts = [qr_ref[rows, j * LANES:(j + 1) * LANES] for j in range(ATT_WIDTH // LANES)]
            mask = window & (current | (n > 0))
            for hk in range(ATT_KV_HEADS):
                lanes = slice(hk * ATT_HEAD_DIM, (hk + 1) * ATT_HEAD_DIM)
                qs = _stack_heads(qparts, hk)
                dos = _stack_heads(doparts, hk)
                kk, vv = kpad[band, lanes], vpad[band, lanes]
                p, inv, es = _softmax_band(qs, kk, mask, _sink_row(sink_ref, hk))
                p = p * inv
                dp = _dot_nt(vv, dos)
                delta = jnp.sum(p * dp, axis=0, keepdims=True)
                ds = _bf(p * (dp - delta))
                sk = (es * inv) * delta
                dqt = _dot_tn(kk, ds) * ATT_SCALE
                dkpad[band, lanes] += _dot(ds, qs)
                dvpad[band, lanes] += _dot(_bf(p), dos)
                for g in range(ATT_GROUP):
                    h = ATT_GROUP * hk + g
                    cols = slice(g * WINDOW, (g + 1) * WINDOW)
                    dqb[:, h * ATT_HEAD_DIM:(h + 1) * ATT_HEAD_DIM] = dqt[:, cols].T
                    dsk[h:h + 1, :] += jnp.broadcast_to(-jnp.sum(sk[:, cols], axis=1, keepdims=True), (1, LANES))
            cs, sl, sr = cos_ref[rows, :], sl_ref[rows, :], sr_ref[rows, :]
            for j in range(ATT_WIDTH // LANES):
                dqkv_ref[rows, j * LANES:(j + 1) * LANES] = _rope_t(dqb[:, j * LANES:(j + 1) * LANES], cs, sl, sr)
            return daw

        daw = lax.fori_loop(0, nb, block, jnp.zeros((1, ATT_WIDTH), F32))
        daw_ref[...] = jnp.broadcast_to(daw, (8, ATT_WIDTH))
        dsink_ref[...] = dsk[...]

        def finish(n, carry):
            r0 = pl.multiple_of(n * WINDOW, WINDOW)
            rows = pl.ds(r0, WINDOW)
            nxt = pl.ds(r0 + WINDOW, WINDOW)
            cs, sl, sr = cos_ref[rows, :], sl_ref[rows, :], sr_ref[rows, :]
            dqkv_ref[rows, ATT_WIDTH:ATT_WIDTH + LANES] = _rope_t(dkpad[nxt, :], cs, sl, sr)
            dqkv_ref[rows, ATT_WIDTH + LANES:QKV] = dvpad[nxt, :]
            return carry

        lax.fori_loop(0, nb, finish, 0)
        _ride_wait(ride_modes, pl.program_id(0), B, ride_in, ride_out, sems)

    seq = lambda w, j: pl.BlockSpec((None, T, w), lambda b: (b, 0, j))
    full = lambda r, w: pl.BlockSpec((r, w), lambda b: (0, 0))
    return pl.pallas_call(
        body, name="attn_bwd", grid=(B,),
        in_specs=[seq(ATT_WIDTH, 0), seq(LANES, 0), seq(LANES, 5), seq(ATT_WIDTH, 0), seq(ATT_WIDTH, 0),
                  full(T, LANES), full(T, LANES), full(T, LANES),
                  pl.BlockSpec(memory_space=pltpu.SMEM), full(1, ATT_WIDTH)] + [ANY_SPEC] * nr,
        out_specs=[seq(QKV, 0), pl.BlockSpec((None, 8, LANES), lambda b: (b, 0, 0)),
                   pl.BlockSpec((None, 8, ATT_WIDTH), lambda b: (b, 0, 0))] + [ANY_SPEC] * nr,
        out_shape=[SDS((B, T, QKV), F32), SDS((B, 8, LANES), F32), SDS((B, 8, ATT_WIDTH), F32)]
        + _exchange_shapes(ride_srcs, ride_modes),
        scratch_shapes=[pltpu.VMEM((T + WINDOW, LANES), BF16), pltpu.VMEM((T + WINDOW, LANES), BF16),
                        pltpu.VMEM((T + WINDOW, LANES), F32), pltpu.VMEM((T + WINDOW, LANES), F32),
                        pltpu.VMEM((WINDOW, ATT_WIDTH), F32), pltpu.VMEM((8, LANES), F32)] + _exchange_sems(nr),
        compiler_params=_params(("arbitrary",), VMEM_LIMIT_BIG),
    )(qr, kr, proj3, attn_o, dan, cos, sinl, sinr, sinks, attn_w, *ride_srcs)


def _in_bwd(x2, dx1, dqkv, dhq, dhf, dhi, dhg, mod8, pre_w, w_in_bf, T):
    N = x2.shape[0]
    TM = _tile_rows(T)
    tps = T // TM
    pieces = [(0, ATT_WIDTH + 2 * LANES), (768, HG_WIDTH), (1280, HG_WIDTH), (1792, HG_WIDTH), (2304, HG_WIDTH)]

    def body(x_ref, dx_ref, p0, p1, p2, p3, p4, mod_ref, pw_ref, w_ref, gx_ref, dproj_ref, acc_ref):
        sc1 = mod_ref[1:2, :]
        dh = jnp.zeros((TM, D_MODEL), F32)
        for ref, (off, width) in zip((p0, p1, p2, p3, p4), pieces):
            pb = _bf(ref[...])
            dproj_ref[:, off:off + width] = pb
            dh = dh + _dot_nt(pb, w_ref[:, off:off + width])
        x = x_ref[...]
        r = lax.rsqrt(_mean_last(x * x) + EPS)
        xh = x * r
        n1 = xh * pw_ref[...]
        dsh1 = _sum_rows(dh)
        dsc1 = _sum_rows(dh * n1)
        dn1 = dh * (1.0 + sc1)
        dw_pre = _sum_rows(dn1 * xh)
        dxh = dn1 * pw_ref[...]
        gx_ref[...] = dx_ref[...] + r * (dxh - xh * _mean_last(dxh * xh))
        _acc_rows(acc_ref, pl.program_id(0) % tps == 0, [dsh1, dsc1, dw_pre])

    row = lambda w: pl.BlockSpec((TM, w), lambda i: (i, 0))
    B = N // T
    return pl.pallas_call(
        body, name="in_bwd", grid=(N // TM,),
        in_specs=[row(D_MODEL), row(D_MODEL), row(768), row(HG_WIDTH), row(HG_WIDTH), row(HG_WIDTH),
                  row(HG_WIDTH), _mod_spec(tps), pl.BlockSpec((1, D_MODEL), lambda i: (0, 0)),
                  pl.BlockSpec((D_MODEL, IN_COLS), lambda i: (0, 0))],
        out_specs=[row(D_MODEL), row(IN_COLS), _mod_spec(tps)],
        out_shape=[SDS((N, D_MODEL), F32), SDS((N, IN_COLS), BF16), SDS((B, 8, D_MODEL), F32)],
        compiler_params=_params(("arbitrary",), VMEM_LIMIT_BIG),
    )(x2, dx1, dqkv, dhq, dhf, dhi, dhg, mod8, pre_w, w_in_bf)


def _matmul_tn(name, a, b, tn, by_columns=False):
    K, M = a.shape
    Nc = b.shape[1]
    tm = min(512, M)

    def body(a_ref, b_ref, o_ref):
        o_ref[...] = _bf(_dot_tn(a_ref[...], b_ref[...]))

    if by_columns:
        out_shape = SDS((Nc // tn, M, tn), BF16)
        out_spec = pl.BlockSpec((None, tm, tn), lambda i, j: (j, i, 0))
    else:
        out_shape = SDS((M, Nc), BF16)
        out_spec = pl.BlockSpec((tm, tn), lambda i, j: (i, j))
    return pl.pallas_call(
        body, name=name, grid=(M // tm, Nc // tn),
        in_specs=[pl.BlockSpec((K, tm), lambda i, j: (0, i)),
                  pl.BlockSpec((K, tn), lambda i, j: (0, j))],
        out_specs=out_spec, out_shape=out_shape,
        compiler_params=_params(("arbitrary", "arbitrary"), VMEM_LIMIT_BIG),
    )(a, b)


def _adamw_math(w, g, m, v):
    m2 = ADAM_B1 * m + (1.0 - ADAM_B1) * g
    v2 = ADAM_B2 * v + (1.0 - ADAM_B2) * (g * g)
    m_hat = m2 / (1.0 - ADAM_B1 ** ADAM_STEP)
    v_hat = v2 / (1.0 - ADAM_B2 ** ADAM_STEP)
    delta = -ADAM_LR * (m_hat / (jnp.sqrt(v_hat) + ADAM_EPS) + ADAM_WD * w)
    return delta, m2, v2


def _reduce_adamw(name, parts, w, m, v):
    r, c = w.shape
    tr = r if r <= 256 else 256

    def body(p_ref, w_ref, m_ref, v_ref, g_ref, d_ref, m2_ref, v2_ref):
        g = p_ref[0].astype(F32)
        for s in range(1, N_DEV):
            g = g + p_ref[s].astype(F32)
        g_ref[...] = g
        d_ref[...], m2_ref[...], v2_ref[...] = _adamw_math(w_ref[...], g, m_ref[...], v_ref[...])

    blk = pl.BlockSpec((tr, c), lambda i: (i, 0))
    return pl.pallas_call(
        body, name=name, grid=(r // tr,),
        in_specs=[pl.BlockSpec((N_DEV, tr, c), lambda i: (0, i, 0)), blk, blk, blk],
        out_specs=[blk] * 4, out_shape=[SDS((r, c), F32)] * 4,
        compiler_params=_params(("arbitrary",), VMEM_LIMIT_BIG),
    )(parts, w, m, v)


def _ada_grad_adamw(c_all, dmod_all, w, m, v):
    r, c = w.shape
    tr = 256
    nb = c_all.shape[0]

    def body(c_ref, dm_ref, w_ref, m_ref, v_ref, g_ref, d_ref, m2_ref, v2_ref):
        cv = c_ref[...]
        g = _dot_tn(cv * _sigmoid(cv), dm_ref[...])
        g_ref[...] = g
        d_ref[...], m2_ref[...], v2_ref[...] = _adamw_math(w_ref[...], g, m_ref[...], v_ref[...])

    blk = pl.BlockSpec((tr, c), lambda i: (i, 0))
    return pl.pallas_call(
        body, name="ada_grad_adamw", grid=(r // tr,),
        in_specs=[pl.BlockSpec((nb, tr), lambda i: (0, i)), pl.BlockSpec((nb, c), lambda i: (0, 0)),
                  blk, blk, blk],
        out_specs=[blk] * 4, out_shape=[SDS((r, c), F32)] * 4,
        compiler_params=_params(("arbitrary",)),
    )(c_all, dmod_all, w, m, v)


_SMALL = [("b_ada", 6144), ("pre_w_mix", 1024), ("attn_sinks", 128), ("attn_out_w", 512), ("lb_table", 1024),
          ("hg_norm_w", 128), ("post_w_mix", 1024), ("pre_w_mlp", 1024), ("post_w_mlp", 1024)]


def _pack_small(vals):
    out = []
    for name, width in _SMALL:
        f = vals[name].reshape(-1).astype(F32)
        out.append(jnp.pad(f, (0, width - f.shape[0])))
    return jnp.concatenate(out).reshape(1, -1)


def _adamw_small(parts, given):
    names = [n for n, _ in _SMALL]
    flat_in = [a for n in names for a in given[n]]

    def body(*refs):
        p_ref = refs[0]
        in_refs = refs[1:1 + 3 * len(names)]
        out_refs = refs[1 + 3 * len(names):]
        g = p_ref[0]
        for s in range(1, N_DEV):
            g = g + p_ref[s]
        off = 0
        for i, (name, width) in enumerate(_SMALL):
            w_ref, m_ref, v_ref = in_refs[3 * i:3 * i + 3]
            rows, cols = w_ref.shape
            for r in range(rows):
                gr = g[:, off + r * cols:off + (r + 1) * cols]
                res = (gr,) + _adamw_math(w_ref[r:r + 1, :], gr, m_ref[r:r + 1, :], v_ref[r:r + 1, :])
                for o_ref, val in zip(out_refs[4 * i:4 * i + 4], res):
                    o_ref[r:r + 1, :] = val
            off += width

    out_shape = [SDS(given[n][0].shape, F32) for n in names for _ in range(4)]
    outs = pl.pallas_call(body, name="adamw_small", out_shape=out_shape)(parts, *flat_in)
    return {n: tuple(outs[4 * i:4 * i + 4]) for i, n in enumerate(names)}


def _columns_to_full(g):
    return g.transpose(1, 0, 2).reshape(g.shape[1], -1)


def kernel(x, c, w_ada, b_ada, pre_w_mix, w_in, attn_sinks, attn_out_w, lb_table, hg_norm_w, w_out, post_w_mix, pre_w_mlp, w_up, w_down, post_w_mlp, loss_target, m_w_ada, m_b_ada, m_pre_w_mix, m_w_in, m_attn_sinks, m_attn_out_w, m_lb_table, m_hg_norm_w, m_w_out, m_post_w_mix, m_pre_w_mlp, m_w_up, m_w_down, m_post_w_mlp, v_w_ada, v_b_ada, v_pre_w_mix, v_w_in, v_attn_sinks, v_attn_out_w, v_lb_table, v_hg_norm_w, v_w_out, v_post_w_mix, v_pre_w_mlp, v_w_up, v_w_down, v_post_w_mlp):
    B, T, _ = x.shape
    N = B * T
    me = 4 * lax.axis_index("x") + 2 * lax.axis_index("y") + lax.axis_index("c")
    x2 = x.reshape(N, D_MODEL)
    tgt2 = loss_target.reshape(N, D_MODEL)

    w_in_g, c_g = _exchange("gather_w_in", [_bf(w_in[0]), c], ["gather"] * 2)
    w_in_f = _columns_to_full(w_in_g)
    c_all = c_g.reshape(N_DEV * B, D_MODEL)

    ada_cols = w_ada.shape[2]
    b_mine = lax.dynamic_slice(b_ada, (0, me * ada_cols), (1, ada_cols))
    mod_cols = _ada_mod(c_all, w_ada[0], b_mine)
    (mod_g,) = _exchange("scatter_mod", [mod_cols.reshape(N_DEV, B, ada_cols)], ["a2a"])
    mod = mod_g.transpose(1, 0, 2).reshape(B, 6, D_MODEL)
    mod8 = jnp.pad(mod, ((0, 0), (0, 2), (0, 0)))

    lb_p = jax.nn.softmax(lb_table, axis=0)
    lb = lb_p[1:2]
    tables = _rope_tables(T)

    proj, h1, w_out_g = _in_proj(x2, mod8, pre_w_mix, w_in_f, T, [_bf(w_out[0])], ["gather"])
    proj3 = proj.reshape(B, T, IN_COLS)
    rec_o, rec_g, s_prev, w_up_g, w_down_g = _hgrn_fwd(proj3, lb, hg_norm_w, [_bf(w_up[0]), _bf(w_down[0])],
                                                       ["gather"] * 2)
    attn_o, attn_n, qr, kr = _attn_fwd(proj3, tables, attn_sinks, attn_out_w, [], [])
    w_out_f = w_out_g.reshape(D_MODEL, D_MODEL)
    mix, x1, cat = _mix_out(x2, attn_n.reshape(N, ATT_WIDTH), rec_g.reshape(N, HG_WIDTH), mod8,
                            post_w_mix, w_out_f, T, [], [])
    w_up_f = _columns_to_full(w_up_g)
    w_down_f = w_down_g.reshape(D_FF, D_MODEL)
    up, d, h2 = _mlp_fwd(x1, mod8, pre_w_mlp, w_up_f, w_down_f, T)

    dx1, u, dup, dd, acc_mlp = _mlp_bwd(x1, d, up, tgt2, mod8, pre_w_mlp, post_w_mlp,
                                        w_down_f, w_up_f, T)
    dan, drg, dmix, acc_mix = _mix_bwd(mix, dx1, mod8, post_w_mix, w_out_f, T)
    gw_out = _matmul_tn("grad_w_out", cat, dmix, 512).reshape(N_DEV, D_MODEL // N_DEV, D_MODEL)
    gw_up = _matmul_tn("grad_w_up", h2, dup, D_FF // N_DEV, by_columns=True)
    gw_down = _matmul_tn("grad_w_down", u, dd, 512).reshape(N_DEV, D_FF // N_DEV, D_MODEL)
    dhq, dhf, dhi, dhg, dlb_p, dgw_p, r_down, r_up = _hgrn_bwd(
        proj3, lb, hg_norm_w, rec_o, s_prev, drg.reshape(B, T, HG_WIDTH), [gw_down, gw_up], ["a2a"] * 2)
    dqkv, dsink_p, daw_p, r_out = _attn_bwd(qr, kr, proj3, attn_o, dan.reshape(B, T, ATT_WIDTH), tables,
                                            attn_sinks, attn_out_w, [gw_out], ["a2a"])
    flat = lambda a: a.reshape(N, a.shape[-1])
    grad_x, dproj, acc_in = _in_bwd(x2, dx1, flat(dqkv), flat(dhq), flat(dhf), flat(dhi), flat(dhg),
                                    mod8, pre_w_mix, w_in_f, T)

    gw_in = _matmul_tn("grad_w_in", h1, dproj, IN_COLS // 2)
    in_cols = w_in.shape[2]
    gw_in = gw_in.reshape(D_MODEL, N_DEV, in_cols).transpose(1, 0, 2)

    dmod = jnp.concatenate([acc_in[:, 0:2], acc_mix[:, 0:1], acc_mlp[:, 0:3]], axis=1)
    dlb = dlb_p[:, 0].sum(0)
    dlb_table = jnp.stack([-dlb, dlb]) * (lb_p[0] * lb_p[1])[None, :]
    small = {
        "b_ada": dmod.sum(0),
        "pre_w_mix": acc_in[:, 2].sum(0),
        "attn_sinks": dsink_p[:, :, 0].sum(0),
        "attn_out_w": daw_p[:, 0].sum(0),
        "lb_table": dlb_table,
        "hg_norm_w": dgw_p[:, 0].reshape(B, HG_HEADS, LANES).sum((0, 1)),
        "post_w_mix": acc_mix[:, 1].sum(0),
        "pre_w_mlp": acc_mlp[:, 3].sum(0),
        "post_w_mlp": acc_mlp[:, 4].sum(0),
    }
    loss_part = acc_mlp[:, 5, 0].sum()
    dmod_blocks = dmod.reshape(B, N_DEV, ada_cols).transpose(1, 0, 2)

    r_in, r_dmod, r_small = _exchange(
        "reduce_grads", [gw_in, dmod_blocks, _pack_small(small)], ["a2a", "a2a", "gather"])

    res = {}
    res["w_in"] = _reduce_adamw("adamw_w_in", r_in, w_in[0], m_w_in[0], v_w_in[0])
    res["w_out"] = _reduce_adamw("adamw_w_out", r_out, w_out[0], m_w_out[0], v_w_out[0])
    res["w_up"] = _reduce_adamw("adamw_w_up", r_up, w_up[0], m_w_up[0], v_w_up[0])
    res["w_down"] = _reduce_adamw("adamw_w_down", r_down, w_down[0], m_w_down[0], v_w_down[0])
    res["w_ada"] = _ada_grad_adamw(c_all, r_dmod.reshape(N_DEV * B, ada_cols), w_ada[0], m_w_ada[0], v_w_ada[0])

    given = dict(b_ada=(b_ada, m_b_ada, v_b_ada), pre_w_mix=(pre_w_mix, m_pre_w_mix, v_pre_w_mix),
                 attn_sinks=(attn_sinks, m_attn_sinks, v_attn_sinks),
                 attn_out_w=(attn_out_w, m_attn_out_w, v_attn_out_w), lb_table=(lb_table, m_lb_table, v_lb_table),
                 hg_norm_w=(hg_norm_w, m_hg_norm_w, v_hg_norm_w), post_w_mix=(post_w_mix, m_post_w_mix, v_post_w_mix),
                 pre_w_mlp=(pre_w_mlp, m_pre_w_mlp, v_pre_w_mlp), post_w_mlp=(post_w_mlp, m_post_w_mlp, v_post_w_mlp))
    res.update(_adamw_small(r_small, given))

    loss = lax.psum(loss_part, ("x", "y", "c"))
    order = ["w_ada", "b_ada", "pre_w_mix", "w_in", "attn_sinks", "attn_out_w", "lb_table", "hg_norm_w", "w_out",
             "post_w_mix", "pre_w_mlp", "w_up", "w_down", "post_w_mlp"]
    big = {"w_ada", "w_in", "w_out", "w_up", "w_down"}
    outs = [loss, grad_x.reshape(B, T, D_MODEL)]
    for i in range(4):
        for k in order:
            a = res[k][i]
            outs.append(a[None] if k in big else a)
    return tuple(outs)
```

```python
import functools

import jax
import jax.numpy as jnp
from jax import lax
from jax.experimental import pallas as pl
from jax.experimental.pallas import tpu as pltpu

F32 = jnp.float32
BF16 = jnp.bfloat16
SDS = jax.ShapeDtypeStruct

D_MODEL = 1024
ATT_WIDTH = 512
ATT_HEAD_DIM = 64
ATT_KV_HEADS = 2
ATT_GROUP = 4
WINDOW = 128
ROPE_DIM = 16
ROPE_THETA = 500000.0
HG_WIDTH = 512
HG_HEAD_DIM = 128
HG_HEADS = 4
HG_CHUNK = 32
IN_COLS = 2816
D_FF = 4096
EPS = 1e-6
N_DEV = 8

ADAM_LR = 0.001
ADAM_B1 = 0.9
ADAM_B2 = 0.999
ADAM_EPS = 1e-08
ADAM_WD = 0.01
ADAM_STEP = 10

VMEM_LIMIT_BIG = 56 << 20
LANES = 128

MESH = pl.DeviceIdType.MESH
NT_DIMS = (((1,), (1,)), ((), ()))
TN_DIMS = (((0,), (0,)), ((), ()))


def _dot(a, b):
    return jnp.dot(a, b, preferred_element_type=F32)


def _dot_nt(a, b):
    return lax.dot_general(a, b, NT_DIMS, preferred_element_type=F32)


def _dot_tn(a, b):
    return lax.dot_general(a, b, TN_DIMS, preferred_element_type=F32)


def _bf(a):
    return a.astype(BF16)


def _sigmoid(a):
    return 1.0 / (1.0 + jnp.exp(-a))


def _mean_last(a):
    return jnp.mean(a, axis=-1, keepdims=True)


def _sum_rows(a):
    return jnp.sum(a, axis=0, keepdims=True)


def _tri_sum(tri_bf, a):
    a1 = _bf(a)
    r1 = a - a1.astype(F32)
    a2 = _bf(r1)
    a3 = _bf(r1 - a2.astype(F32))
    return _dot(tri_bf, a1) + _dot(tri_bf, a2) + _dot(tri_bf, a3)


def _params(sem=None, vmem=None):
    kw = {}
    if sem is not None:
        kw["dimension_semantics"] = sem
    if vmem is not None:
        kw["vmem_limit_bytes"] = vmem
    return pltpu.CompilerParams(**kw)


ANY_SPEC = pl.BlockSpec(memory_space=pl.ANY)


def _exchange_shapes(srcs, modes):
    out_shape = []
    for s, m in zip(srcs, modes):
        shp = (N_DEV,) + tuple(s.shape) if m == "gather" else tuple(s.shape)
        out_shape.append(SDS(shp, s.dtype))
    return out_shape


def _exchange_sems(n):
    if n == 0:
        return []
    return [pltpu.SemaphoreType.DMA((n, N_DEV - 1)), pltpu.SemaphoreType.DMA((n, N_DEV - 1)),
            pltpu.SemaphoreType.DMA((n,))]


SIBLING = 1
OTHER_CHIPS = (2, 4, 6)


def _related(k):
    x, y, c = lax.axis_index("x"), lax.axis_index("y"), lax.axis_index("c")
    px, py, pc = x ^ ((k >> 2) & 1), y ^ ((k >> 1) & 1), c ^ (k & 1)
    return (px, py, pc), 4 * px + 2 * py + pc


def _exchange_phases(modes, src_refs, out_refs, send_sems, recv_sems, own_sems):
    _, me = _related(0)
    sib_dev, sib = _related(SIBLING)
    start, middle, end = [], [], []

    def remote(a, i, src, dst, dev):
        return pltpu.make_async_remote_copy(src_ref=src, dst_ref=dst, send_sem=send_sems.at[a, i],
                                            recv_sem=recv_sems.at[a, i], device_id=dev, device_id_type=MESH)

    for a, mode in enumerate(modes):
        out = out_refs[a]
        if mode == "gather":
            src = src_refs[a]
            own = pltpu.make_async_copy(src, out.at[me], own_sems.at[a])
            to_sib = remote(a, 0, src, out.at[me], sib_dev)
            start += [own.start, to_sib.start]
            end += [remote(a, 0, src, out.at[sib], sib_dev).wait_recv, to_sib.wait_send, own.wait]
            for j, k in enumerate(OTHER_CHIPS, start=1):
                dev, peer = _related(k)
                _, peer_sib = _related(k ^ SIBLING)
                send = remote(a, j, src, out.at[me], dev)
                passed = remote(a, 3 + j, out.at[peer], out.at[peer], sib_dev)
                start.append(send.start)
                middle += [remote(a, j, src, out.at[peer], dev).wait_recv, passed.start]
                end += [remote(a, 3 + j, out.at[peer_sib], out.at[peer_sib], sib_dev).wait_recv,
                        send.wait_send, passed.wait_send]
        elif mode == "pair":
            core = lax.axis_index("c")
            own = pltpu.make_async_copy(src_refs[a].at[core], out.at[0], own_sems.at[a])
            send = remote(a, 0, src_refs[a].at[1 - core], out.at[1], sib_dev)
            start += [own.start, send.start]
            end += [remote(a, 0, src_refs[a].at[1 - core], out.at[1], sib_dev).wait_recv, send.wait_send, own.wait]
        elif mode == "chips":
            chip = me // 2
            own = pltpu.make_async_copy(src_refs[a].at[chip], out.at[chip], own_sems.at[a])
            start.append(own.start)
            end.append(own.wait)
            for j, k in enumerate(OTHER_CHIPS, start=1):
                dev, peer = _related(k)
                send = remote(a, j, src_refs[a].at[peer // 2], out.at[chip], dev)
                start.append(send.start)
                end += [remote(a, j, src_refs[a].at[peer // 2], out.at[peer // 2], dev).wait_recv, send.wait_send]
        else:
            own = pltpu.make_async_copy(src_refs[a].at[me], out.at[me], own_sems.at[a])
            start.append(own.start)
            end.append(own.wait)
            for k in range(1, N_DEV):
                dev, peer = _related(k)
                send = remote(a, k - 1, src_refs[a].at[peer], out.at[me], dev)
                start.append(send.start)
                end += [remote(a, k - 1, src_refs[a].at[peer], out.at[peer], dev).wait_recv, send.wait_send]
    return start, middle, end


def _run(actions):
    for act in actions:
        act()


def _exchange(name, srcs, modes):
    n = len(srcs)

    def body(*refs):
        start, middle, end = _exchange_phases(modes, refs[:n], refs[n:2 * n], *refs[2 * n:])
        _run(start)
        _run(middle)
        _run(end)

    return pl.pallas_call(
        body, name=name, out_shape=_exchange_shapes(srcs, modes),
        in_specs=[ANY_SPEC] * n, out_specs=[ANY_SPEC] * n,
        scratch_shapes=_exchange_sems(n),
    )(*srcs)


def _ride_start(modes, step, steps, src_refs, out_refs, sems):
    if not modes:
        return
    middle_step = steps - 1

    @pl.when(step == 0)
    def _():
        _run(_exchange_phases(modes, src_refs, out_refs, *sems)[0])

    if "gather" in modes:
        @pl.when(step == middle_step)
        def _():
            _run(_exchange_phases(modes, src_refs, out_refs, *sems)[1])


def _ride_wait(modes, step, steps, src_refs, out_refs, sems):
    if not modes:
        return

    @pl.when(step == steps - 1)
    def _():
        _run(_exchange_phases(modes, src_refs, out_refs, *sems)[2])


def _ada_mod(c_all, w_ada, b_ada_mine):
    nb, cols = c_all.shape[0], w_ada.shape[1]

    def body(c_ref, w_ref, b_ref, o_ref):
        cv = c_ref[...]
        ca = cv * _sigmoid(cv)
        o_ref[...] = _dot(ca, w_ref[...]) + b_ref[...]

    return pl.pallas_call(body, name="ada_mod", out_shape=SDS((nb, cols), F32))(c_all, w_ada, b_ada_mine)


def _tile_rows(T):
    return min(256, T)


def _mod_spec(tps):
    return pl.BlockSpec((None, 8, D_MODEL), lambda i: (i // tps, 0, 0))


def _in_proj(x2, mod8, pre_w, w_in_bf, T, ride_srcs, ride_modes):
    N = x2.shape[0]
    TM = _tile_rows(T)
    tps = T // TM
    nr = len(ride_srcs)

    def body(*refs):
        x_ref, mod_ref, pw_ref, w_ref = refs[:4]
        ride_in = refs[4:4 + nr]
        proj_ref, h1_ref = refs[4 + nr:6 + nr]
        ride_out = refs[6 + nr:6 + 2 * nr]
        sems = refs[6 + 2 * nr:]
        _ride_start(ride_modes, pl.program_id(0), N // TM, ride_in, ride_out, sems)
        x = x_ref[...]
        r = lax.rsqrt(_mean_last(x * x) + EPS)
        h = (x * r * pw_ref[...]) * (1.0 + mod_ref[1:2, :]) + mod_ref[0:1, :]
        hb = _bf(h)
        h1_ref[...] = hb
        proj_ref[...] = _dot(hb, w_ref[...])
        _ride_wait(ride_modes, pl.program_id(0), N // TM, ride_in, ride_out, sems)

    return pl.pallas_call(
        body, name="in_proj", grid=(N // TM,),
        in_specs=[pl.BlockSpec((TM, D_MODEL), lambda i: (i, 0)), _mod_spec(tps),
                  pl.BlockSpec((1, D_MODEL), lambda i: (0, 0)),
                  pl.BlockSpec((D_MODEL, IN_COLS), lambda i: (0, 0))] + [ANY_SPEC] * nr,
        out_specs=[pl.BlockSpec((TM, IN_COLS), lambda i: (i, 0)),
                   pl.BlockSpec((TM, D_MODEL), lambda i: (i, 0))] + [ANY_SPEC] * nr,
        out_shape=[SDS((N, IN_COLS), F32), SDS((N, D_MODEL), BF16)] + _exchange_shapes(ride_srcs, ride_modes),
        scratch_shapes=_exchange_sems(nr),
        compiler_params=_params(("arbitrary",), VMEM_LIMIT_BIG),
    )(x2, mod8, pre_w, w_in_bf, *ride_srcs)


def _rope_tables(T):
    half = ROPE_DIM // 2
    inv_freq = ROPE_THETA ** (-jnp.arange(0, ROPE_DIM, 2, dtype=F32) / ROPE_DIM)
    ang = jnp.arange(T, dtype=F32)[:, None] * inv_freq[None, :]
    cos, sin = jnp.cos(ang), jnp.sin(ang)
    ones = jnp.ones((T, ATT_HEAD_DIM - ROPE_DIM), F32)
    zeros = jnp.zeros((T, ATT_HEAD_DIM - ROPE_DIM), F32)
    zh = jnp.zeros((T, half), F32)
    cos64 = jnp.concatenate([cos, cos, ones], axis=1)
    sin_left = jnp.concatenate([-sin, zh, zeros], axis=1)
    sin_right = jnp.concatenate([zh, sin, zeros], axis=1)
    rep = LANES // ATT_HEAD_DIM
    return jnp.tile(cos64, (1, rep)), jnp.tile(sin_left, (1, rep)), jnp.tile(sin_right, (1, rep))


def _rope(xc, cs, sl, sr):
    return xc * cs + pltpu.roll(xc, LANES - 8, 1) * sl + pltpu.roll(xc, 8, 1) * sr


def _rope_t(dy, cs, sl, sr):
    return dy * cs + pltpu.roll(dy * sl, 8, 1) + pltpu.roll(dy * sr, LANES - 8, 1)


ATT_SCALE = ATT_HEAD_DIM ** -0.5


def _band_masks():
    cols = ATT_GROUP * WINDOW
    j = lax.broadcasted_iota(jnp.int32, (2 * WINDOW, cols), 0)
    i = lax.broadcasted_iota(jnp.int32, (2 * WINDOW, cols), 1) & (WINDOW - 1)
    diff = i + WINDOW - j
    return (diff >= 0) & (diff < WINDOW), j >= WINDOW


def _sink_row(sink_ref, hk):
    return jnp.concatenate(
        [jnp.full((1, WINDOW), sink_ref[0, ATT_GROUP * hk + g], F32) for g in range(ATT_GROUP)], axis=1)


def _softmax_band(qs, kk, mask, sink):
    s = jnp.where(mask, _dot_nt(kk, qs), jnp.finfo(F32).min)
    m = jnp.maximum(jnp.max(s, axis=0, keepdims=True), sink)
    p = jnp.exp(s - m)
    es = jnp.exp(sink - m)
    inv = 1.0 / (jnp.sum(p, axis=0, keepdims=True) + es)
    return p, inv, es


def _stack_heads(parts, hk):
    hs = []
    for g in range(ATT_GROUP):
        h = ATT_GROUP * hk + g
        hs.append(parts[h // 2][:, (h % 2) * ATT_HEAD_DIM:(h % 2 + 1) * ATT_HEAD_DIM])
    return jnp.concatenate(hs, axis=0)


def _attn_fwd(proj3, tables, sinks, attn_w, ride_srcs, ride_modes):
    B, T, _ = proj3.shape
    nb = T // WINDOW
    nr = len(ride_srcs)
    cos, sinl, sinr = tables

    def body(*refs):
        q_ref, k_ref, v_ref, cos_ref, sl_ref, sr_ref, sink_ref, aw_ref = refs[:8]
        ride_in = refs[8:8 + nr]
        o_ref, an_ref, qr_ref, kr_ref = refs[8 + nr:12 + nr]
        ride_out = refs[12 + nr:12 + 2 * nr]
        kpad, vpad = refs[12 + 2 * nr:14 + 2 * nr]
        sems = refs[14 + 2 * nr:]
        _ride_start(ride_modes, pl.program_id(0), B, ride_in, ride_out, sems)

        kpad[0:WINDOW, :] = jnp.zeros((WINDOW, LANES), BF16)
        vpad[0:WINDOW, :] = jnp.zeros((WINDOW, LANES), BF16)
        window, current = _band_masks()

        def block(n, carry):
            r0 = pl.multiple_of(n * WINDOW, WINDOW)
            rows = pl.ds(r0, WINDOW)
            nxt = pl.ds(r0 + WINDOW, WINDOW)
            band = pl.ds(r0, 2 * WINDOW)
            cs, sl, sr = cos_ref[rows, :], sl_ref[rows, :], sr_ref[rows, :]
            kb = _bf(_rope(k_ref[rows, :], cs, sl, sr))
            kpad[nxt, :] = kb
            kr_ref[rows, :] = kb
            vpad[nxt, :] = _bf(v_ref[rows, :])
            qparts = []
            for j in range(ATT_WIDTH // LANES):
                qp = _bf(_rope(q_ref[rows, j * LANES:(j + 1) * LANES], cs, sl, sr) * ATT_SCALE)
                qr_ref[rows, j * LANES:(j + 1) * LANES] = qp
                qparts.append(qp)
            mask = window & (current | (n > 0))
            for hk in range(ATT_KV_HEADS):
                lanes = slice(hk * ATT_HEAD_DIM, (hk + 1) * ATT_HEAD_DIM)
                qs = _stack_heads(qparts, hk)
                p, inv, _ = _softmax_band(qs, kpad[band, lanes], mask, _sink_row(sink_ref, hk))
                ot = _dot_tn(vpad[band, lanes], _bf(p)) * inv
                for g in range(ATT_GROUP):
                    h = ATT_GROUP * hk + g
                    o_ref[rows, h * ATT_HEAD_DIM:(h + 1) * ATT_HEAD_DIM] = ot[:, g * WINDOW:(g + 1) * WINDOW].T
            ob = o_ref[rows, :]
            an_ref[rows, :] = _bf(ob * lax.rsqrt(_mean_last(ob * ob) + EPS) * aw_ref[...])
            return carry

        lax.fori_loop(0, nb, block, 0)
        _ride_wait(ride_modes, pl.program_id(0), B, ride_in, ride_out, sems)

    seq = lambda w, j: pl.BlockSpec((None, T, w), lambda b: (b, 0, j))
    full = lambda r, w: pl.BlockSpec((r, w), lambda b: (0, 0))
    return pl.pallas_call(
        body, name="attn_fwd", grid=(B,),
        in_specs=[seq(ATT_WIDTH, 0), seq(LANES, 4), seq(LANES, 5),
                  full(T, LANES), full(T, LANES), full(T, LANES),
                  pl.BlockSpec(memory_space=pltpu.SMEM), full(1, ATT_WIDTH)] + [ANY_SPEC] * nr,
        out_specs=[seq(ATT_WIDTH, 0), seq(ATT_WIDTH, 0), seq(ATT_WIDTH, 0), seq(LANES, 0)] + [ANY_SPEC] * nr,
        out_shape=[SDS((B, T, ATT_WIDTH), F32), SDS((B, T, ATT_WIDTH), BF16),
                   SDS((B, T, ATT_WIDTH), BF16), SDS((B, T, LANES), BF16)] + _exchange_shapes(ride_srcs, ride_modes),
        scratch_shapes=[pltpu.VMEM((T + WINDOW, LANES), BF16), pltpu.VMEM((T + WINDOW, LANES), BF16)]
        + _exchange_sems(nr),
        compiler_params=_params(("arbitrary",), VMEM_LIMIT_BIG),
    )(proj3, proj3, proj3, cos, sinl, sinr, sinks, attn_w, *ride_srcs)


HG_GROUP = 8
HG_ROWS = HG_GROUP * HG_CHUNK


HG_STACK = HG_GROUP * HG_HEAD_DIM


def _group_masks():
    r = lax.broadcasted_iota(jnp.int32, (HG_ROWS, HG_ROWS), 0)
    c = lax.broadcasted_iota(jnp.int32, (HG_ROWS, HG_ROWS), 1)
    same = (r // HG_CHUNK) == (c // HG_CHUNK)
    return same & (r >= c), same & (c >= r), same & (c > r), same


def _row_chunk():
    return lax.broadcasted_iota(jnp.int32, (HG_ROWS, HG_HEAD_DIM), 0) // HG_CHUNK


def _spread(a, row_chunk):
    return jnp.concatenate([jnp.where(row_chunk == c, a, jnp.zeros_like(a)) for c in range(HG_GROUP)], axis=1)


def _pick(r, row_chunk):
    out = jnp.where(row_chunk == 0, r[:, :HG_HEAD_DIM], 0.0)
    for c in range(1, HG_GROUP):
        out = out + jnp.where(row_chunk == c, r[:, c * HG_HEAD_DIM:(c + 1) * HG_HEAD_DIM], 0.0)
    return out


def _lane_block(a, c):
    return a[:, c * HG_HEAD_DIM:(c + 1) * HG_HEAD_DIM]


def _ones_bf(mask):
    return jnp.where(mask, 1.0, 0.0).astype(BF16)


def _hgrn_gates(hq, hf, lb, sums_bf, ebl_scr):
    sq = _sigmoid(hq)
    q = hq * sq
    sg = _sigmoid(hf)
    f = lb + (1.0 - lb) * sg
    k = 1.0 - f
    cs = _tri_sum(sums_bf, jnp.log(f))
    b, rem = cs[:HG_ROWS], cs[HG_ROWS:]
    eb, enb, e2 = jnp.exp(b), jnp.exp(-b), jnp.exp(rem)
    ebl_scr[...] = eb * e2
    return dict(sq=sq, sg=sg, f=f, eb=eb, enb=enb, e2=e2, qd=q * eb, kd=k * enb, k2=k * e2)


def _hgrn_specs(B, T):
    head = lambda base: pl.BlockSpec((None, T, LANES), lambda b, h: (b, 0, base + h))
    return head


def _chunk_rows(c):
    return slice(c * HG_CHUNK, (c + 1) * HG_CHUNK)


def _hgrn_fwd(proj3, lb, hg_w, ride_srcs, ride_modes):
    B, T, _ = proj3.shape
    nc = T // HG_CHUNK
    ng = T // HG_ROWS
    nr = len(ride_srcs)
    head = _hgrn_specs(B, T)

    def body(*refs):
        hq_ref, hf_ref, hi_ref, hg_ref, lb_ref, gw_ref = refs[:6]
        ride_in = refs[6:6 + nr]
        o_ref, rg_ref, sp_ref = refs[6 + nr:9 + nr]
        ride_out = refs[9 + nr:9 + 2 * nr]
        st, ebl_scr = refs[9 + 2 * nr:11 + 2 * nr]
        sems = refs[11 + 2 * nr:]
        step = pl.program_id(0) * HG_HEADS + pl.program_id(1)
        _ride_start(ride_modes, step, B * HG_HEADS, ride_in, ride_out, sems)

        st[...] = jnp.zeros((HG_HEAD_DIM, HG_HEAD_DIM), F32)
        lo, _, ups, _ = _group_masks()
        sums_bf = jnp.concatenate([_ones_bf(lo), _ones_bf(ups)], axis=0)
        row_chunk = _row_chunk()
        lbv = lb_ref[...]

        def group(gi, carry):
            rows = pl.ds(pl.multiple_of(gi * HG_ROWS, HG_ROWS), HG_ROWS)
            gt = _hgrn_gates(hq_ref[rows, :], hf_ref[rows, :], lbv, sums_bf, ebl_scr)
            v, qd, kd, k2 = _bf(hi_ref[rows, :]), _bf(gt["qd"]), _bf(gt["kd"]), _bf(gt["k2"])
            a = jnp.where(lo, _dot_nt(qd, kd), 0.0)
            kv = _dot_tn(v, _bf(_spread(gt["k2"], row_chunk)))
            s = st[...]
            before = []
            for c in range(HG_GROUP):
                before.append(s)
                s = s * ebl_scr[c * HG_CHUNK:c * HG_CHUNK + 1, :] + _lane_block(kv, c)
            st[...] = s
            sp = jnp.concatenate(before, axis=1)
            sp_ref[gi] = sp
            o = _dot(_bf(a), v) + _dot_nt(_bf(_spread(gt["qd"], row_chunk)), _bf(sp))
            o_ref[rows, :] = o
            hg = hg_ref[rows, :]
            rn = o * lax.rsqrt(_mean_last(o * o) + EPS) * gw_ref[...]
            rg_ref[rows, :] = _bf(rn * (hg * _sigmoid(hg)))
            return carry

        lax.fori_loop(0, ng, group, 0)
        _ride_wait(ride_modes, step, B * HG_HEADS, ride_in, ride_out, sems)

    out_head = pl.BlockSpec((None, T, LANES), lambda b, h: (b, 0, h))
    return pl.pallas_call(
        body, name="hgrn_fwd", grid=(B, HG_HEADS),
        in_specs=[head(6), head(10), head(14), head(18),
                  pl.BlockSpec((1, LANES), lambda b, h: (0, h)),
                  pl.BlockSpec((1, LANES), lambda b, h: (0, 0))] + [ANY_SPEC] * nr,
        out_specs=[out_head, out_head,
                   pl.BlockSpec((None, None, ng, HG_HEAD_DIM, HG_STACK), lambda b, h: (b, h, 0, 0, 0))]
        + [ANY_SPEC] * nr,
        out_shape=[SDS((B, T, HG_WIDTH), F32), SDS((B, T, HG_WIDTH), BF16),
                   SDS((B, HG_HEADS, ng, HG_HEAD_DIM, HG_STACK), F32)] + _exchange_shapes(ride_srcs, ride_modes),
        scratch_shapes=[pltpu.VMEM((HG_HEAD_DIM, HG_HEAD_DIM), F32), pltpu.VMEM((HG_ROWS, LANES), F32)]
        + _exchange_sems(nr),
        compiler_params=_params(("arbitrary", "arbitrary"), VMEM_LIMIT_BIG),
    )(proj3, proj3, proj3, proj3, lb, hg_w, *ride_srcs)


def _mix_out(x2, attn_n, rec_g, mod8, post_w, w_out_bf, T, ride_srcs, ride_modes):
    N = x2.shape[0]
    TM = _tile_rows(T)
    tps = T // TM
    nr = len(ride_srcs)

    def body(*refs):
        x_ref, an_ref, rg_ref, mod_ref, pw_ref, w_ref = refs[:6]
        ride_in = refs[6:6 + nr]
        mix_ref, x1_ref, cat_ref = refs[6 + nr:9 + nr]
        ride_out = refs[9 + nr:9 + 2 * nr]
        sems = refs[9 + 2 * nr:]
        _ride_start(ride_modes, pl.program_id(0), N // TM, ride_in, ride_out, sems)
        cat = jnp.concatenate([an_ref[...], rg_ref[...]], axis=1)
        cat_ref[...] = cat
        mix = _dot(cat, w_ref[...])
        mix_ref[...] = mix
        r = lax.rsqrt(_mean_last(mix * mix) + EPS)
        x1_ref[...] = x_ref[...] + mod_ref[2:3, :] * (mix * r * pw_ref[...])
        _ride_wait(ride_modes, pl.program_id(0), N // TM, ride_in, ride_out, sems)

    row = lambda w: pl.BlockSpec((TM, w), lambda i: (i, 0))
    return pl.pallas_call(
        body, name="mix_out", grid=(N // TM,),
        in_specs=[row(D_MODEL), row(ATT_WIDTH), row(HG_WIDTH), _mod_spec(tps),
                  pl.BlockSpec((1, D_MODEL), lambda i: (0, 0)),
                  pl.BlockSpec((D_MODEL, D_MODEL), lambda i: (0, 0))] + [ANY_SPEC] * nr,
        out_specs=[row(D_MODEL), row(D_MODEL), row(D_MODEL)] + [ANY_SPEC] * nr,
        out_shape=[SDS((N, D_MODEL), F32), SDS((N, D_MODEL), F32), SDS((N, D_MODEL), BF16)]
        + _exchange_shapes(ride_srcs, ride_modes),
        scratch_shapes=_exchange_sems(nr),
        compiler_params=_params(("arbitrary",), VMEM_LIMIT_BIG),
    )(x2, attn_n, rec_g, mod8, post_w, w_out_bf, *ride_srcs)


def _load_weights_once(pairs, sem):
    @pl.when(pl.program_id(0) == 0)
    def _():
        cps = [pltpu.make_async_copy(src, dst, sem.at[i]) for i, (src, dst) in enumerate(pairs)]
        for cp in cps:
            cp.start()
        for cp in cps:
            cp.wait()


def _mlp_fwd(x1, mod8, pre_w, w_up_bf, w_down_bf, T):
    N = x1.shape[0]
    TM = _tile_rows(T)
    tps = T // TM

    def body(x_ref, mod_ref, pw_ref, wu_hbm, wd_hbm, up_ref, d_ref, h2_ref, wu, wd, sem):
        _load_weights_once([(wu_hbm, wu), (wd_hbm, wd)], sem)
        x = x_ref[...]
        r = lax.rsqrt(_mean_last(x * x) + EPS)
        h = (x * r * pw_ref[...]) * (1.0 + mod_ref[4:5, :]) + mod_ref[3:4, :]
        hb = _bf(h)
        h2_ref[...] = hb
        up = _dot(hb, wu[...])
        up_ref[...] = up
        ru = jnp.maximum(up, 0.0)
        d_ref[...] = _dot(_bf(ru * ru), wd[...])

    row = lambda w: pl.BlockSpec((TM, w), lambda i: (i, 0))
    return pl.pallas_call(
        body, name="mlp_fwd", grid=(N // TM,),
        in_specs=[row(D_MODEL), _mod_spec(tps), pl.BlockSpec((1, D_MODEL), lambda i: (0, 0)),
                  pl.BlockSpec(memory_space=pl.ANY), pl.BlockSpec(memory_space=pl.ANY)],
        out_specs=[row(D_FF), row(D_MODEL), row(D_MODEL)],
        out_shape=[SDS((N, D_FF), F32), SDS((N, D_MODEL), F32), SDS((N, D_MODEL), BF16)],
        scratch_shapes=[pltpu.VMEM((D_MODEL, D_FF), BF16), pltpu.VMEM((D_FF, D_MODEL), BF16),
                        pltpu.SemaphoreType.DMA((2,))],
        compiler_params=_params(("arbitrary",), VMEM_LIMIT_BIG),
    )(x1, mod8, pre_w, w_up_bf, w_down_bf)


def _acc_rows(acc_ref, first, rows):
    @pl.when(first)
    def _():
        acc_ref[...] = jnp.zeros(acc_ref.shape, F32)
    for i, r in enumerate(rows):
        acc_ref[i:i + 1, :] += r


def _mlp_bwd(x1, d, up, tgt, mod8, pre_w, post_w, w_down_bf, w_up_bf, T):
    N = x1.shape[0]
    TM = _tile_rows(T)
    tps = T // TM

    def body(x_ref, d_ref, up_ref, t_ref, mod_ref, pw_ref, qw_ref, wd_hbm, wu_hbm,
             dx_ref, u_ref, dup_ref, dd_ref, acc_ref, wd, wu, sem):
        _load_weights_once([(wd_hbm, wd), (wu_hbm, wu)], sem)
        sh2, sc2, g2 = mod_ref[3:4, :], mod_ref[4:5, :], mod_ref[5:6, :]
        x = x_ref[...]
        r1 = lax.rsqrt(_mean_last(x * x) + EPS)
        xh = x * r1
        n2 = xh * pw_ref[...]
        dv = d_ref[...]
        rd = lax.rsqrt(_mean_last(dv * dv) + EPS)
        dh = dv * rd
        rr = dh * qw_ref[...]
        e = x + g2 * rr - t_ref[...]
        loss = 0.5 * jnp.sum(_sum_rows(e * e), axis=1, keepdims=True) / D_MODEL
        dy = e * (1.0 / D_MODEL)
        dg2 = _sum_rows(dy * rr)
        drr = dy * g2
        dw_post = _sum_rows(drr * dh)
        ddh = drr * qw_ref[...]
        dd = _bf(rd * (ddh - dh * _mean_last(ddh * dh)))
        dd_ref[...] = dd
        ru = jnp.maximum(up_ref[...], 0.0)
        u_ref[...] = _bf(ru * ru)
        dup = _bf(_dot_nt(dd, wd[...]) * (2.0 * ru))
        dup_ref[...] = dup
        dh2 = _dot_nt(dup, wu[...])
        dsh2 = _sum_rows(dh2)
        dsc2 = _sum_rows(dh2 * n2)
        dn2 = dh2 * (1.0 + sc2)
        dw_pre = _sum_rows(dn2 * xh)
        dxh = dn2 * pw_ref[...]
        dx_ref[...] = dy + r1 * (dxh - xh * _mean_last(dxh * xh))
        _acc_rows(acc_ref, pl.program_id(0) % tps == 0,
                  [dsh2, dsc2, dg2, dw_pre, dw_post, jnp.broadcast_to(loss, (1, D_MODEL))])

    row = lambda w: pl.BlockSpec((TM, w), lambda i: (i, 0))
    vec = pl.BlockSpec((1, D_MODEL), lambda i: (0, 0))
    B = N // T
    return pl.pallas_call(
        body, name="mlp_bwd", grid=(N // TM,),
        in_specs=[row(D_MODEL), row(D_MODEL), row(D_FF), row(D_MODEL), _mod_spec(tps), vec, vec,
                  pl.BlockSpec(memory_space=pl.ANY), pl.BlockSpec(memory_space=pl.ANY)],
        out_specs=[row(D_MODEL), row(D_FF), row(D_FF), row(D_MODEL), _mod_spec(tps)],
        out_shape=[SDS((N, D_MODEL), F32), SDS((N, D_FF), BF16), SDS((N, D_FF), BF16),
                   SDS((N, D_MODEL), BF16), SDS((B, 8, D_MODEL), F32)],
        scratch_shapes=[pltpu.VMEM((D_FF, D_MODEL), BF16), pltpu.VMEM((D_MODEL, D_FF), BF16),
                        pltpu.SemaphoreType.DMA((2,))],
        compiler_params=_params(("arbitrary",), VMEM_LIMIT_BIG),
    )(x1, d, up, tgt, mod8, pre_w, post_w, w_down_bf, w_up_bf)


def _mix_bwd(mix, dx1, mod8, post_w, w_out_bf, T, ride_srcs, ride_modes):
    N = mix.shape[0]
    TM = _tile_rows(T)
    tps = T // TM
    nr = len(ride_srcs)

    def body(*refs):
        mix_ref, dx_ref, mod_ref, pw_ref, w_ref = refs[:5]
        ride_in = refs[5:5 + nr]
        dan_ref, drg_ref, dmix_ref, acc_ref = refs[5 + nr:9 + nr]
        ride_out = refs[9 + nr:9 + 2 * nr]
        sems = refs[9 + 2 * nr:]
        _ride_start(ride_modes, pl.program_id(0), N // TM, ride_in, ride_out, sems)
        g1 = mod_ref[2:3, :]
        mix = mix_ref[...]
        dx1 = dx_ref[...]
        rm = lax.rsqrt(_mean_last(mix * mix) + EPS)
        mh = mix * rm
        dg1 = _sum_rows(dx1 * (mh * pw_ref[...]))
        dr = dx1 * g1
        dw_post = _sum_rows(dr * mh)
        dmh = dr * pw_ref[...]
        dmix = _bf(rm * (dmh - mh * _mean_last(dmh * mh)))
        dmix_ref[...] = dmix
        dcat = _dot_nt(dmix, w_ref[...])
        dan_ref[...] = dcat[:, :ATT_WIDTH]
        drg_ref[...] = dcat[:, ATT_WIDTH:]
        _acc_rows(acc_ref, pl.program_id(0) % tps == 0, [dg1, dw_post])
        _ride_wait(ride_modes, pl.program_id(0), N // TM, ride_in, ride_out, sems)

    row = lambda w: pl.BlockSpec((TM, w), lambda i: (i, 0))
    B = N // T
    return pl.pallas_call(
        body, name="mix_bwd", grid=(N // TM,),
        in_specs=[row(D_MODEL), row(D_MODEL), _mod_spec(tps), pl.BlockSpec((1, D_MODEL), lambda i: (0, 0)),
                  pl.BlockSpec((D_MODEL, D_MODEL), lambda i: (0, 0))] + [ANY_SPEC] * nr,
        out_specs=[row(ATT_WIDTH), row(HG_WIDTH), row(D_MODEL), _mod_spec(tps)] + [ANY_SPEC] * nr,
        out_shape=[SDS((N, ATT_WIDTH), F32), SDS((N, HG_WIDTH), F32), SDS((N, D_MODEL), BF16),
                   SDS((B, 8, D_MODEL), F32)] + _exchange_shapes(ride_srcs, ride_modes),
        scratch_shapes=_exchange_sems(nr),
        compiler_params=_params(("arbitrary",), VMEM_LIMIT_BIG),
    )(mix, dx1, mod8, post_w, w_out_bf, *ride_srcs)


def _hgrn_bwd(proj3, lb, hg_w, o, s_prev, drg, ride_srcs, ride_modes):
    B, T, _ = proj3.shape
    nc = T // HG_CHUNK
    ng = T // HG_ROWS
    nr = len(ride_srcs)
    head = _hgrn_specs(B, T)

    def body(*refs):
        hq_ref, hf_ref, hi_ref, hg_ref, lb_ref, gw_ref, o_ref, sp_ref, drg_ref = refs[:9]
        ride_in = refs[9:9 + nr]
        dhq_ref, dhf_ref, dhi_ref, dhg_ref, dlb_ref, dgw_ref = refs[9 + nr:15 + nr]
        ride_out = refs[15 + nr:15 + 2 * nr]
        dst, ebl_scr = refs[15 + 2 * nr:17 + 2 * nr]
        sems = refs[17 + 2 * nr:]
        step = pl.program_id(0) * HG_HEADS + pl.program_id(1)
        _ride_start(ride_modes, step, B * HG_HEADS, ride_in, ride_out, sems)

        dst[...] = jnp.zeros((HG_HEAD_DIM, HG_HEAD_DIM), F32)
        lo, up, ups, same = _group_masks()
        sums_bf = jnp.concatenate([_ones_bf(lo), _ones_bf(ups)], axis=0)
        back_bf = jnp.concatenate([_ones_bf(up), _ones_bf(same)], axis=1)
        row_chunk = _row_chunk()
        lbv = lb_ref[...]
        gw = gw_ref[...]

        def group(i, carry):
            dlb, dgw = carry
            gi = ng - 1 - i
            rows = pl.ds(pl.multiple_of(gi * HG_ROWS, HG_ROWS), HG_ROWS)
            hq = hq_ref[rows, :]
            gt = _hgrn_gates(hq, hf_ref[rows, :], lbv, sums_bf, ebl_scr)
            sq, sg, qdf, kdf, k2f = gt["sq"], gt["sg"], gt["qd"], gt["kd"], gt["k2"]
            v, qd, kd, k2 = _bf(hi_ref[rows, :]), _bf(qdf), _bf(kdf), _bf(k2f)
            ov = o_ref[rows, :]
            hg = hg_ref[rows, :]
            shg = _sigmoid(hg)
            dr = drg_ref[rows, :]
            ro = lax.rsqrt(_mean_last(ov * ov) + EPS)
            oh = ov * ro
            dhg_ref[rows, :] = dr * (oh * gw) * (shg + hg * shg * (1.0 - shg))
            drn = dr * (hg * shg)
            dgw = dgw + _sum_rows(drn * oh)
            doh = drn * gw
            do = _bf(ro * (doh - oh * _mean_last(doh * oh)))
            a = jnp.where(lo, _dot_nt(qd, kd), 0.0)
            da = _bf(jnp.where(lo, _dot_nt(do, v), 0.0))
            dv = _dot_tn(_bf(a), do)
            dqd = _dot(da, kd)
            dkd = _dot_tn(da, qd)
            sp = sp_ref[gi]
            incr = _dot_tn(do, _bf(_spread(qdf, row_chunk)))
            ds = dst[...]
            after = [None] * HG_GROUP
            for c in reversed(range(HG_GROUP)):
                after[c] = ds
                ds = ds * ebl_scr[c * HG_CHUNK:c * HG_CHUNK + 1, :] + _lane_block(incr, c)
            dst[...] = ds
            dss = jnp.concatenate(after, axis=1)
            dssb = _bf(dss)
            dk2 = _pick(_dot(v, dssb), row_chunk)
            dhi_ref[rows, :] = dv + _dot_nt(_bf(_spread(k2f, row_chunk)), dssb)
            dqd = dqd + _pick(_dot(do, _bf(sp)), row_chunk)
            debl = _sum_rows(dss * sp)
            dbl = jnp.concatenate(
                [jnp.broadcast_to(_lane_block(debl, c) * ebl_scr[c * HG_CHUNK:c * HG_CHUNK + 1, :],
                                  (HG_CHUNK, HG_HEAD_DIM)) for c in range(HG_GROUP)], axis=0)
            k2g = dk2 * k2f
            db = dqd * qdf - dkd * kdf - k2g
            dk = dkd * gt["enb"] + dk2 * gt["e2"]
            dg = _tri_sum(back_bf, jnp.concatenate([db, k2g], axis=0)) + dbl
            df = dg / gt["f"] - dk
            dhf_ref[rows, :] = df * (1.0 - lbv) * sg * (1.0 - sg)
            dlb = dlb + _sum_rows(df * (1.0 - sg))
            dhq_ref[rows, :] = (dqd * gt["eb"]) * (sq + hq * sq * (1.0 - sq))
            return dlb, dgw

        zero = jnp.zeros((1, LANES), F32)
        dlb, dgw = lax.fori_loop(0, ng, group, (zero, zero))
        dlb_ref[...] = jnp.broadcast_to(dlb, (8, LANES))
        dgw_ref[...] = jnp.broadcast_to(dgw, (8, LANES))
        _ride_wait(ride_modes, step, B * HG_HEADS, ride_in, ride_out, sems)

    out_head = pl.BlockSpec((None, T, LANES), lambda b, h: (b, 0, h))
    small = pl.BlockSpec((None, 8, LANES), lambda b, h: (b, 0, h))
    return pl.pallas_call(
        body, name="hgrn_bwd", grid=(B, HG_HEADS),
        in_specs=[head(6), head(10), head(14), head(18),
                  pl.BlockSpec((1, LANES), lambda b, h: (0, h)),
                  pl.BlockSpec((1, LANES), lambda b, h: (0, 0)),
                  out_head,
                  pl.BlockSpec((None, None, ng, HG_HEAD_DIM, HG_STACK), lambda b, h: (b, h, 0, 0, 0)),
                  out_head] + [ANY_SPEC] * nr,
        out_specs=[out_head, out_head, out_head, out_head, small, small] + [ANY_SPEC] * nr,
        out_shape=[SDS((B, T, HG_WIDTH), F32)] * 4 + [SDS((B, 8, HG_WIDTH), F32)] * 2
        + _exchange_shapes(ride_srcs, ride_modes),
        scratch_shapes=[pltpu.VMEM((HG_HEAD_DIM, HG_HEAD_DIM), F32), pltpu.VMEM((HG_ROWS, LANES), F32)]
        + _exchange_sems(nr),
        compiler_params=_params(("arbitrary", "arbitrary"), VMEM_LIMIT_BIG),
    )(proj3, proj3, proj3, proj3, lb, hg_w, o, s_prev, drg, *ride_srcs)


def _attn_bwd(qr, kr, proj3, attn_o, dan, tables, sinks, attn_w, ride_srcs, ride_modes):
    B, T, _ = proj3.shape
    nb = T // WINDOW
    nr = len(ride_srcs)
    cos, sinl, sinr = tables
    QKV = ATT_WIDTH + 2 * LANES

    def body(*refs):
        qr_ref, kr_ref, v_ref, o_ref, dan_ref, cos_ref, sl_ref, sr_ref, sink_ref, aw_ref = refs[:10]
        ride_in = refs[10:10 + nr]
        dqkv_ref, dsink_ref, daw_ref = refs[10 + nr:13 + nr]
        ride_out = refs[13 + nr:13 + 2 * nr]
        kpad, vpad, dkpad, dvpad, dqb, dsk = refs[13 + 2 * nr:19 + 2 * nr]
        sems = refs[19 + 2 * nr:]
        _ride_start(ride_modes, pl.program_id(0), B, ride_in, ride_out, sems)

        window, current = _band_masks()
        kpad[0:WINDOW, :] = jnp.zeros((WINDOW, LANES), BF16)
        vpad[0:WINDOW, :] = jnp.zeros((WINDOW, LANES), BF16)
        kpad[WINDOW:, :] = kr_ref[...]
        vpad[WINDOW:, :] = _bf(v_ref[...])
        dkpad[...] = jnp.zeros(dkpad.shape, F32)
        dvpad[...] = jnp.zeros(dvpad.shape, F32)
        dsk[...] = jnp.zeros(dsk.shape, F32)
        aw = aw_ref[...]

        def block(n, daw):
            r0 = pl.multiple_of(n * WINDOW, WINDOW)
            rows = pl.ds(r0, WINDOW)
            band = pl.ds(r0, 2 * WINDOW)
            ob = o_ref[rows, :]
            dn = dan_ref[rows, :]
            ro = lax.rsqrt(_mean_last(ob * ob) + EPS)
            oh = ob * ro
            daw = daw + _sum_rows(dn * oh)
            doh = dn * aw
            do = _bf(ro * (doh - oh * _mean_last(doh * oh)))
            doparts = [do[:, j * LANES:(j + 1) * LANES] for j in range(ATT_WIDTH // LANES)]
            qparts = [qr_ref[rows, j * LANES:(j + 1) * LANES] for j in range(ATT_WIDTH // LANES)]
            mask = window & (current | (n > 0))
            for hk in range(ATT_KV_HEADS):
                lanes = slice(hk * ATT_HEAD_DIM, (hk + 1) * ATT_HEAD_DIM)
                qs = _stack_heads(qparts, hk)
                dos = _stack_heads(doparts, hk)
                kk, vv = kpad[band, lanes], vpad[band, lanes]
                p, inv, es = _softmax_band(qs, kk, mask, _sink_row(sink_ref, hk))
                p = p * inv
                dp = _dot_nt(vv, dos)
                delta = jnp.sum(p * dp, axis=0, keepdims=True)
                ds = _bf(p * (dp - delta))
                sk = (es * inv) * delta
                dqt = _dot_tn(kk, ds) * ATT_SCALE
                dkpad[band, lanes] += _dot(ds, qs)
                dvpad[band, lanes] += _dot(_bf(p), dos)
                for g in range(ATT_GROUP):
                    h = ATT_GROUP * hk + g
                    cols = slice(g * WINDOW, (g + 1) * WINDOW)
                    dqb[:, h * ATT_HEAD_DIM:(h + 1) * ATT_HEAD_DIM] = dqt[:, cols].T
                    dsk[h:h + 1, :] += jnp.broadcast_to(-jnp.sum(sk[:, cols], axis=1, keepdims=True), (1, LANES))
            cs, sl, sr = cos_ref[rows, :], sl_ref[rows, :], sr_ref[rows, :]
            for j in range(ATT_WIDTH // LANES):
                dqkv_ref[rows, j * LANES:(j + 1) * LANES] = _rope_t(dqb[:, j * LANES:(j + 1) * LANES], cs, sl, sr)
            return daw

        daw = lax.fori_loop(0, nb, block, jnp.zeros((1, ATT_WIDTH), F32))
        daw_ref[...] = jnp.broadcast_to(daw, (8, ATT_WIDTH))
        dsink_ref[...] = dsk[...]

        def finish(n, carry):
            r0 = pl.multiple_of(n * WINDOW, WINDOW)
            rows = pl.ds(r0, WINDOW)
            nxt = pl.ds(r0 + WINDOW, WINDOW)
            cs, sl, sr = cos_ref[rows, :], sl_ref[rows, :], sr_ref[rows, :]
            dqkv_ref[rows, ATT_WIDTH:ATT_WIDTH + LANES] = _rope_t(dkpad[nxt, :], cs, sl, sr)
            dqkv_ref[rows, ATT_WIDTH + LANES:QKV] = dvpad[nxt, :]
            return carry

        lax.fori_loop(0, nb, finish, 0)
        _ride_wait(ride_modes, pl.program_id(0), B, ride_in, ride_out, sems)

    seq = lambda w, j: pl.BlockSpec((None, T, w), lambda b: (b, 0, j))
    full = lambda r, w: pl.BlockSpec((r, w), lambda b: (0, 0))
    return pl.pallas_call(
        body, name="attn_bwd", grid=(B,),
        in_specs=[seq(ATT_WIDTH, 0), seq(LANES, 0), seq(LANES, 5), seq(ATT_WIDTH, 0), seq(ATT_WIDTH, 0),
                  full(T, LANES), full(T, LANES), full(T, LANES),
                  pl.BlockSpec(memory_space=pltpu.SMEM), full(1, ATT_WIDTH)] + [ANY_SPEC] * nr,
        out_specs=[seq(QKV, 0), pl.BlockSpec((None, 8, LANES), lambda b: (b, 0, 0)),
                   pl.BlockSpec((None, 8, ATT_WIDTH), lambda b: (b, 0, 0))] + [ANY_SPEC] * nr,
        out_shape=[SDS((B, T, QKV), F32), SDS((B, 8, LANES), F32), SDS((B, 8, ATT_WIDTH), F32)]
        + _exchange_shapes(ride_srcs, ride_modes),
        scratch_shapes=[pltpu.VMEM((T + WINDOW, LANES), BF16), pltpu.VMEM((T + WINDOW, LANES), BF16),
                        pltpu.VMEM((T + WINDOW, LANES), F32), pltpu.VMEM((T + WINDOW, LANES), F32),
                        pltpu.VMEM((WINDOW, ATT_WIDTH), F32), pltpu.VMEM((8, LANES), F32)] + _exchange_sems(nr),
        compiler_params=_params(("arbitrary",), VMEM_LIMIT_BIG),
    )(qr, kr, proj3, attn_o, dan, cos, sinl, sinr, sinks, attn_w, *ride_srcs)


def _in_bwd(x2, dx1, dqkv, dhq, dhf, dhi, dhg, mod8, pre_w, w_in_bf, T):
    N = x2.shape[0]
    TM = _tile_rows(T)
    tps = T // TM
    pieces = [(0, ATT_WIDTH + 2 * LANES), (768, HG_WIDTH), (1280, HG_WIDTH), (1792, HG_WIDTH), (2304, HG_WIDTH)]

    def body(x_ref, dx_ref, p0, p1, p2, p3, p4, mod_ref, pw_ref, w_ref, gx_ref, dproj_ref, acc_ref):
        sc1 = mod_ref[1:2, :]
        dh = jnp.zeros((TM, D_MODEL), F32)
        for ref, (off, width) in zip((p0, p1, p2, p3, p4), pieces):
            pb = _bf(ref[...])
            dproj_ref[:, off:off + width] = pb
            dh = dh + _dot_nt(pb, w_ref[:, off:off + width])
        x = x_ref[...]
        r = lax.rsqrt(_mean_last(x * x) + EPS)
        xh = x * r
        n1 = xh * pw_ref[...]
        dsh1 = _sum_rows(dh)
        dsc1 = _sum_rows(dh * n1)
        dn1 = dh * (1.0 + sc1)
        dw_pre = _sum_rows(dn1 * xh)
        dxh = dn1 * pw_ref[...]
        gx_ref[...] = dx_ref[...] + r * (dxh - xh * _mean_last(dxh * xh))
        _acc_rows(acc_ref, pl.program_id(0) % tps == 0, [dsh1, dsc1, dw_pre])

    row = lambda w: pl.BlockSpec((TM, w), lambda i: (i, 0))
    B = N // T
    return pl.pallas_call(
        body, name="in_bwd", grid=(N // TM,),
        in_specs=[row(D_MODEL), row(D_MODEL), row(768), row(HG_WIDTH), row(HG_WIDTH), row(HG_WIDTH),
                  row(HG_WIDTH), _mod_spec(tps), pl.BlockSpec((1, D_MODEL), lambda i: (0, 0)),
                  pl.BlockSpec((D_MODEL, IN_COLS), lambda i: (0, 0))],
        out_specs=[row(D_MODEL), row(IN_COLS), _mod_spec(tps)],
        out_shape=[SDS((N, D_MODEL), F32), SDS((N, IN_COLS), BF16), SDS((B, 8, D_MODEL), F32)],
        compiler_params=_params(("arbitrary",), VMEM_LIMIT_BIG),
    )(x2, dx1, dqkv, dhq, dhf, dhi, dhg, mod8, pre_w, w_in_bf)


def _matmul_tn(name, a, b, tn, by_owner=None):
    K, M = a.shape
    Nc = b.shape[1]
    tm = min(512, M)

    def body(a_ref, b_ref, o_ref):
        o_ref[...] = _bf(_dot_tn(a_ref[...], b_ref[...]))

    if by_owner == "cols":
        assert tn * N_DEV == Nc
        out_shape = SDS((2, N_DEV // 2, M, tn), BF16)
        out_spec = pl.BlockSpec((None, None, tm, tn), lambda i, j: (j % 2, j // 2, i, 0))
    elif by_owner == "rows":
        assert tm * N_DEV == M
        out_shape = SDS((2, N_DEV // 2, tm, Nc), BF16)
        out_spec = pl.BlockSpec((None, None, tm, tn), lambda i, j: (i % 2, i // 2, 0, j))
    else:
        out_shape = SDS((M, Nc), BF16)
        out_spec = pl.BlockSpec((tm, tn), lambda i, j: (i, j))
    return pl.pallas_call(
        body, name=name, grid=(M // tm, Nc // tn),
        in_specs=[pl.BlockSpec((K, tm), lambda i, j: (0, i)),
                  pl.BlockSpec((K, tn), lambda i, j: (0, j))],
        out_specs=out_spec, out_shape=out_shape,
        compiler_params=_params(("arbitrary", "arbitrary"), VMEM_LIMIT_BIG),
    )(a, b)


def _adamw_math(w, g, m, v):
    m2 = ADAM_B1 * m + (1.0 - ADAM_B1) * g
    v2 = ADAM_B2 * v + (1.0 - ADAM_B2) * (g * g)
    m_hat = m2 / (1.0 - ADAM_B1 ** ADAM_STEP)
    v_hat = v2 / (1.0 - ADAM_B2 ** ADAM_STEP)
    delta = -ADAM_LR * (m_hat / (jnp.sqrt(v_hat) + ADAM_EPS) + ADAM_WD * w)
    return delta, m2, v2


def _pair_add(name, q):
    _, chips, r, c = q.shape
    tr = min(256, r)

    def body(q_ref, o_ref):
        o_ref[...] = _bf(q_ref[0].astype(F32) + q_ref[1].astype(F32))

    return pl.pallas_call(
        body, name=name, grid=(chips, r // tr),
        in_specs=[pl.BlockSpec((2, None, tr, c), lambda s, i: (0, s, i, 0))],
        out_specs=pl.BlockSpec((None, tr, c), lambda s, i: (s, i, 0)),
        out_shape=SDS((chips, r, c), BF16),
        compiler_params=_params(("arbitrary", "arbitrary")),
    )(q)


def _reduce_adamw(name, parts, w, m, v):
    r, c = w.shape
    tr = r if r <= 256 else 256
    slots = parts.shape[0]

    def body(p_ref, w_ref, m_ref, v_ref, g_ref, d_ref, m2_ref, v2_ref):
        g = p_ref[0].astype(F32)
        for s in range(1, slots):
            g = g + p_ref[s].astype(F32)
        g_ref[...] = g
        d_ref[...], m2_ref[...], v2_ref[...] = _adamw_math(w_ref[...], g, m_ref[...], v_ref[...])

    blk = pl.BlockSpec((tr, c), lambda i: (i, 0))
    return pl.pallas_call(
        body, name=name, grid=(r // tr,),
        in_specs=[pl.BlockSpec((slots, tr, c), lambda i: (0, i, 0)), blk, blk, blk],
        out_specs=[blk] * 4, out_shape=[SDS((r, c), F32)] * 4,
        compiler_params=_params(("arbitrary",), VMEM_LIMIT_BIG),
    )(parts, w, m, v)


def _ada_grad_adamw(c_all, dmod_all, w, m, v):
    r, c = w.shape
    tr = 256
    nb = c_all.shape[0]

    def body(c_ref, dm_ref, w_ref, m_ref, v_ref, g_ref, d_ref, m2_ref, v2_ref):
        cv = c_ref[...]
        g = _dot_tn(cv * _sigmoid(cv), dm_ref[...])
        g_ref[...] = g
        d_ref[...], m2_ref[...], v2_ref[...] = _adamw_math(w_ref[...], g, m_ref[...], v_ref[...])

    blk = pl.BlockSpec((tr, c), lambda i: (i, 0))
    return pl.pallas_call(
        body, name="ada_grad_adamw", grid=(r // tr,),
        in_specs=[pl.BlockSpec((nb, tr), lambda i: (0, i)), pl.BlockSpec((nb, c), lambda i: (0, 0)),
                  blk, blk, blk],
        out_specs=[blk] * 4, out_shape=[SDS((r, c), F32)] * 4,
        compiler_params=_params(("arbitrary",)),
    )(c_all, dmod_all, w, m, v)


_SMALL = [("b_ada", 6144), ("pre_w_mix", 1024), ("attn_sinks", 128), ("attn_out_w", 512), ("lb_table", 1024),
          ("hg_norm_w", 128), ("post_w_mix", 1024), ("pre_w_mlp", 1024), ("post_w_mlp", 1024)]


def _pack_small(vals):
    out = []
    for name, width in _SMALL:
        f = vals[name].reshape(-1).astype(F32)
        out.append(jnp.pad(f, (0, width - f.shape[0])))
    return jnp.concatenate(out).reshape(1, -1)


def _adamw_small(parts, given):
    names = [n for n, _ in _SMALL]
    flat_in = [a for n in names for a in given[n]]

    def body(*refs):
        p_ref = refs[0]
        in_refs = refs[1:1 + 3 * len(names)]
        out_refs = refs[1 + 3 * len(names):]
        g = p_ref[0]
        for s in range(1, N_DEV):
            g = g + p_ref[s]
        off = 0
        for i, (name, width) in enumerate(_SMALL):
            w_ref, m_ref, v_ref = in_refs[3 * i:3 * i + 3]
            rows, cols = w_ref.shape
            for r in range(rows):
                gr = g[:, off + r * cols:off + (r + 1) * cols]
                res = (gr,) + _adamw_math(w_ref[r:r + 1, :], gr, m_ref[r:r + 1, :], v_ref[r:r + 1, :])
                for o_ref, val in zip(out_refs[4 * i:4 * i + 4], res):
                    o_ref[r:r + 1, :] = val
            off += width

    out_shape = [SDS(given[n][0].shape, F32) for n in names for _ in range(4)]
    outs = pl.pallas_call(body, name="adamw_small", out_shape=out_shape)(parts, *flat_in)
    return {n: tuple(outs[4 * i:4 * i + 4]) for i, n in enumerate(names)}


def _columns_to_full(g):
    return g.transpose(1, 0, 2).reshape(g.shape[1], -1)


def kernel(x, c, w_ada, b_ada, pre_w_mix, w_in, attn_sinks, attn_out_w, lb_table, hg_norm_w, w_out, post_w_mix, pre_w_mlp, w_up, w_down, post_w_mlp, loss_target, m_w_ada, m_b_ada, m_pre_w_mix, m_w_in, m_attn_sinks, m_attn_out_w, m_lb_table, m_hg_norm_w, m_w_out, m_post_w_mix, m_pre_w_mlp, m_w_up, m_w_down, m_post_w_mlp, v_w_ada, v_b_ada, v_pre_w_mix, v_w_in, v_attn_sinks, v_attn_out_w, v_lb_table, v_hg_norm_w, v_w_out, v_post_w_mix, v_pre_w_mlp, v_w_up, v_w_down, v_post_w_mlp):
    B, T, _ = x.shape
    N = B * T
    me = 4 * lax.axis_index("x") + 2 * lax.axis_index("y") + lax.axis_index("c")
    x2 = x.reshape(N, D_MODEL)
    tgt2 = loss_target.reshape(N, D_MODEL)

    w_in_g, c_g = _exchange("gather_w_in", [_bf(w_in[0]), c], ["gather"] * 2)
    w_in_f = _columns_to_full(w_in_g)
    c_all = c_g.reshape(N_DEV * B, D_MODEL)

    ada_cols = w_ada.shape[2]
    b_mine = lax.dynamic_slice(b_ada, (0, me * ada_cols), (1, ada_cols))
    mod_cols = _ada_mod(c_all, w_ada[0], b_mine)
    (mod_g,) = _exchange("scatter_mod", [mod_cols.reshape(N_DEV, B, ada_cols)], ["a2a"])
    mod = mod_g.transpose(1, 0, 2).reshape(B, 6, D_MODEL)
    mod8 = jnp.pad(mod, ((0, 0), (0, 2), (0, 0)))

    lb_p = jax.nn.softmax(lb_table, axis=0)
    lb = lb_p[1:2]
    tables = _rope_tables(T)

    proj, h1, w_out_g = _in_proj(x2, mod8, pre_w_mix, w_in_f, T, [_bf(w_out[0])], ["gather"])
    proj3 = proj.reshape(B, T, IN_COLS)
    rec_o, rec_g, s_prev, w_up_g, w_down_g = _hgrn_fwd(proj3, lb, hg_norm_w, [_bf(w_up[0]), _bf(w_down[0])],
                                                       ["gather"] * 2)
    attn_o, attn_n, qr, kr = _attn_fwd(proj3, tables, attn_sinks, attn_out_w, [], [])
    w_out_f = w_out_g.reshape(D_MODEL, D_MODEL)
    mix, x1, cat = _mix_out(x2, attn_n.reshape(N, ATT_WIDTH), rec_g.reshape(N, HG_WIDTH), mod8,
                            post_w_mix, w_out_f, T, [], [])
    w_up_f = _columns_to_full(w_up_g)
    w_down_f = w_down_g.reshape(D_FF, D_MODEL)
    up, d, h2 = _mlp_fwd(x1, mod8, pre_w_mlp, w_up_f, w_down_f, T)

    dx1, u, dup, dd, acc_mlp = _mlp_bwd(x1, d, up, tgt2, mod8, pre_w_mlp, post_w_mlp,
                                        w_down_f, w_up_f, T)
    gw_up = _matmul_tn("grad_w_up", h2, dup, D_FF // N_DEV, by_owner="cols")
    gw_down = _matmul_tn("grad_w_down", u, dd, 512, by_owner="rows")
    dan, drg, dmix, acc_mix, q_down, q_up = _mix_bwd(mix, dx1, mod8, post_w_mix, w_out_f, T,
                                                     [gw_down, gw_up], ["pair"] * 2)
    p_down, p_up = _pair_add("pair_add_w_down", q_down), _pair_add("pair_add_w_up", q_up)
    gw_out = _matmul_tn("grad_w_out", cat, dmix, 512).reshape(N_DEV, D_MODEL // N_DEV, D_MODEL)
    dhq, dhf, dhi, dhg, dlb_p, dgw_p, r_down, r_up = _hgrn_bwd(
        proj3, lb, hg_norm_w, rec_o, s_prev, drg.reshape(B, T, HG_WIDTH), [p_down, p_up], ["chips"] * 2)
    dqkv, dsink_p, daw_p, r_out = _attn_bwd(qr, kr, proj3, attn_o, dan.reshape(B, T, ATT_WIDTH), tables,
                                            attn_sinks, attn_out_w, [gw_out], ["a2a"])
    flat = lambda a: a.reshape(N, a.shape[-1])
    grad_x, dproj, acc_in = _in_bwd(x2, dx1, flat(dqkv), flat(dhq), flat(dhf), flat(dhi), flat(dhg),
                                    mod8, pre_w_mix, w_in_f, T)

    gw_in = _matmul_tn("grad_w_in", h1, dproj, IN_COLS // 2)
    in_cols = w_in.shape[2]
    gw_in = gw_in.reshape(D_MODEL, N_DEV // 2, 2, in_cols).transpose(2, 1, 0, 3)
    (q_in,) = _exchange("pair_w_in", [gw_in], ["pair"])
    p_in = _pair_add("pair_add_w_in", q_in)

    dmod = jnp.concatenate([acc_in[:, 0:2], acc_mix[:, 0:1], acc_mlp[:, 0:3]], axis=1)
    dlb = dlb_p[:, 0].sum(0)
    dlb_table = jnp.stack([-dlb, dlb]) * (lb_p[0] * lb_p[1])[None, :]
    small = {
        "b_ada": dmod.sum(0),
        "pre_w_mix": acc_in[:, 2].sum(0),
        "attn_sinks": dsink_p[:, :, 0].sum(0),
        "attn_out_w": daw_p[:, 0].sum(0),
        "lb_table": dlb_table,
        "hg_norm_w": dgw_p[:, 0].reshape(B, HG_HEADS, LANES).sum((0, 1)),
        "post_w_mix": acc_mix[:, 1].sum(0),
        "pre_w_mlp": acc_mlp[:, 3].sum(0),
        "post_w_mlp": acc_mlp[:, 4].sum(0),
    }
    loss_part = acc_mlp[:, 5, 0].sum()
    dmod_blocks = dmod.reshape(B, N_DEV, ada_cols).transpose(1, 0, 2)

    r_in, r_dmod, r_small = _exchange(
        "reduce_grads", [p_in, dmod_blocks, _pack_small(small)], ["chips", "a2a", "gather"])

    res = {}
    res["w_in"] = _reduce_adamw("adamw_w_in", r_in, w_in[0], m_w_in[0], v_w_in[0])
    res["w_out"] = _reduce_adamw("adamw_w_out", r_out, w_out[0], m_w_out[0], v_w_out[0])
    res["w_up"] = _reduce_adamw("adamw_w_up", r_up, w_up[0], m_w_up[0], v_w_up[0])
    res["w_down"] = _reduce_adamw("adamw_w_down", r_down, w_down[0], m_w_down[0], v_w_down[0])
    res["w_ada"] = _ada_grad_adamw(c_all, r_dmod.reshape(N_DEV * B, ada_cols), w_ada[0], m_w_ada[0], v_w_ada[0])

    given = dict(b_ada=(b_ada, m_b_ada, v_b_ada), pre_w_mix=(pre_w_mix, m_pre_w_mix, v_pre_w_mix),
                 attn_sinks=(attn_sinks, m_attn_sinks, v_attn_sinks),
                 attn_out_w=(attn_out_w, m_attn_out_w, v_attn_out_w), lb_table=(lb_table, m_lb_table, v_lb_table),
                 hg_norm_w=(hg_norm_w, m_hg_norm_w, v_hg_norm_w), post_w_mix=(post_w_mix, m_post_w_mix, v_post_w_mix),
                 pre_w_mlp=(pre_w_mlp, m_pre_w_mlp, v_pre_w_mlp), post_w_mlp=(post_w_mlp, m_post_w_mlp, v_post_w_mlp))
    res.update(_adamw_small(r_small, given))

    loss = lax.psum(loss_part, ("x", "y", "c"))
    order = ["w_ada", "b_ada", "pre_w_mix", "w_in", "attn_sinks", "attn_out_w", "lb_table", "hg_norm_w", "w_out",
             "post_w_mix", "pre_w_mlp", "w_up", "w_down", "post_w_mlp"]
    big = {"w_ada", "w_in", "w_out", "w_up", "w_down"}
    outs = [loss, grad_x.reshape(B, T, D_MODEL)]
    for i in range(4):
        for k in order:
            a = res[k][i]
            outs.append(a[None] if k in big else a)
    return tuple(outs)
```

```python
import functools

import jax
import jax.numpy as jnp
from jax import lax
from jax.experimental import pallas as pl
from jax.experimental.pallas import tpu as pltpu

F32 = jnp.float32
BF16 = jnp.bfloat16
SDS = jax.ShapeDtypeStruct

D_MODEL = 1024
ATT_WIDTH = 512
ATT_HEAD_DIM = 64
ATT_KV_HEADS = 2
ATT_GROUP = 4
WINDOW = 128
ROPE_DIM = 16
ROPE_THETA = 500000.0
HG_WIDTH = 512
HG_HEAD_DIM = 128
HG_HEADS = 4
HG_CHUNK = 32
IN_COLS = 2816
D_FF = 4096
EPS = 1e-6
N_DEV = 8

ADAM_LR = 0.001
ADAM_B1 = 0.9
ADAM_B2 = 0.999
ADAM_EPS = 1e-08
ADAM_WD = 0.01
ADAM_STEP = 10

VMEM_LIMIT_BIG = 56 << 20
LANES = 128

MESH = pl.DeviceIdType.MESH
NT_DIMS = (((1,), (1,)), ((), ()))
TN_DIMS = (((0,), (0,)), ((), ()))


def _dot(a, b):
    return jnp.dot(a, b, preferred_element_type=F32)


def _dot_nt(a, b):
    return lax.dot_general(a, b, NT_DIMS, preferred_element_type=F32)


def _dot_tn(a, b):
    return lax.dot_general(a, b, TN_DIMS, preferred_element_type=F32)


def _bf(a):
    return a.astype(BF16)


def _sigmoid(a):
    return 1.0 / (1.0 + jnp.exp(-a))


def _mean_last(a):
    return jnp.mean(a, axis=-1, keepdims=True)


def _sum_rows(a):
    return jnp.sum(a, axis=0, keepdims=True)


def _tri_sum(tri_bf, a):
    a1 = _bf(a)
    r1 = a - a1.astype(F32)
    a2 = _bf(r1)
    a3 = _bf(r1 - a2.astype(F32))
    return _dot(tri_bf, a1) + _dot(tri_bf, a2) + _dot(tri_bf, a3)


def _params(sem=None, vmem=None):
    kw = {}
    if sem is not None:
        kw["dimension_semantics"] = sem
    if vmem is not None:
        kw["vmem_limit_bytes"] = vmem
    return pltpu.CompilerParams(**kw)


ANY_SPEC = pl.BlockSpec(memory_space=pl.ANY)


def _exchange_shapes(srcs, modes):
    out_shape = []
    for s, m in zip(srcs, modes):
        shp = {"gather": (N_DEV,) + tuple(s.shape), "pair": tuple(s.shape[1:])}.get(m, tuple(s.shape))
        out_shape.append(SDS(shp, s.dtype))
    return out_shape


def _exchange_sems(n):
    if n == 0:
        return []
    return [pltpu.SemaphoreType.DMA((n, N_DEV - 1)), pltpu.SemaphoreType.DMA((n, N_DEV - 1)),
            pltpu.SemaphoreType.DMA((n,))]


SIBLING = 1
OTHER_CHIPS = (2, 4, 6)


def _related(k):
    x, y, c = lax.axis_index("x"), lax.axis_index("y"), lax.axis_index("c")
    px, py, pc = x ^ ((k >> 2) & 1), y ^ ((k >> 1) & 1), c ^ (k & 1)
    return (px, py, pc), 4 * px + 2 * py + pc


def _exchange_phases(modes, src_refs, out_refs, send_sems, recv_sems, own_sems):
    _, me = _related(0)
    sib_dev, sib = _related(SIBLING)
    start, middle, end = [], [], []

    def remote(a, i, src, dst, dev):
        return pltpu.make_async_remote_copy(src_ref=src, dst_ref=dst, send_sem=send_sems.at[a, i],
                                            recv_sem=recv_sems.at[a, i], device_id=dev, device_id_type=MESH)

    for a, mode in enumerate(modes):
        out = out_refs[a]
        if mode == "gather":
            src = src_refs[a]
            own = pltpu.make_async_copy(src, out.at[me], own_sems.at[a])
            to_sib = remote(a, 0, src, out.at[me], sib_dev)
            start += [own.start, to_sib.start]
            end += [remote(a, 0, src, out.at[sib], sib_dev).wait_recv, to_sib.wait_send, own.wait]
            for j, k in enumerate(OTHER_CHIPS, start=1):
                dev, peer = _related(k)
                _, peer_sib = _related(k ^ SIBLING)
                send = remote(a, j, src, out.at[me], dev)
                passed = remote(a, 3 + j, out.at[peer], out.at[peer], sib_dev)
                start.append(send.start)
                middle += [remote(a, j, src, out.at[peer], dev).wait_recv, passed.start]
                end += [remote(a, 3 + j, out.at[peer_sib], out.at[peer_sib], sib_dev).wait_recv,
                        send.wait_send, passed.wait_send]
        elif mode == "pair":
            core = lax.axis_index("c")
            for s in range(N_DEV // 2):
                send = remote(a, s, src_refs[a].at[1 - core, s], out.at[s], sib_dev)
                start.append(send.start)
                end += [remote(a, s, src_refs[a].at[1 - core, s], out.at[s], sib_dev).wait_recv, send.wait_send]
        elif mode == "chips":
            chip = me // 2
            own = pltpu.make_async_copy(src_refs[a].at[chip], out.at[chip], own_sems.at[a])
            start.append(own.start)
            end.append(own.wait)
            for j, k in enumerate(OTHER_CHIPS, start=1):
                dev, peer = _related(k)
                send = remote(a, j, src_refs[a].at[peer // 2], out.at[chip], dev)
                start.append(send.start)
                end += [remote(a, j, src_refs[a].at[peer // 2], out.at[peer // 2], dev).wait_recv, send.wait_send]
        else:
            own = pltpu.make_async_copy(src_refs[a].at[me], out.at[me], own_sems.at[a])
            start.append(own.start)
            end.append(own.wait)
            for k in range(1, N_DEV):
                dev, peer = _related(k)
                send = remote(a, k - 1, src_refs[a].at[peer], out.at[me], dev)
                start.append(send.start)
                end += [remote(a, k - 1, src_refs[a].at[peer], out.at[peer], dev).wait_recv, send.wait_send]
    return start, middle, end


def _run(actions):
    for act in actions:
        act()


def _exchange(name, srcs, modes):
    n = len(srcs)

    def body(*refs):
        start, middle, end = _exchange_phases(modes, refs[:n], refs[n:2 * n], *refs[2 * n:])
        _run(start)
        _run(middle)
        _run(end)

    return pl.pallas_call(
        body, name=name, out_shape=_exchange_shapes(srcs, modes),
        in_specs=[ANY_SPEC] * n, out_specs=[ANY_SPEC] * n,
        scratch_shapes=_exchange_sems(n),
    )(*srcs)


def _ride_start(modes, step, steps, src_refs, out_refs, sems):
    if not modes:
        return
    middle_step = steps - 1

    @pl.when(step == 0)
    def _():
        _run(_exchange_phases(modes, src_refs, out_refs, *sems)[0])

    if "gather" in modes:
        @pl.when(step == middle_step)
        def _():
            _run(_exchange_phases(modes, src_refs, out_refs, *sems)[1])


def _ride_wait(modes, step, steps, src_refs, out_refs, sems):
    if not modes:
        return

    @pl.when(step == steps - 1)
    def _():
        _run(_exchange_phases(modes, src_refs, out_refs, *sems)[2])


def _ada_mod(c_all, w_ada, b_ada_mine):
    nb, cols = c_all.shape[0], w_ada.shape[1]

    def body(c_ref, w_ref, b_ref, o_ref):
        cv = c_ref[...]
        ca = cv * _sigmoid(cv)
        o_ref[...] = _dot(ca, w_ref[...]) + b_ref[...]

    return pl.pallas_call(body, name="ada_mod", out_shape=SDS((nb, cols), F32))(c_all, w_ada, b_ada_mine)


def _tile_rows(T):
    return min(256, T)


def _mod_spec(tps):
    return pl.BlockSpec((None, 8, D_MODEL), lambda i: (i // tps, 0, 0))


def _in_proj(x2, mod8, pre_w, w_in_bf, T, ride_srcs, ride_modes):
    N = x2.shape[0]
    TM = _tile_rows(T)
    tps = T // TM
    nr = len(ride_srcs)

    def body(*refs):
        x_ref, mod_ref, pw_ref, w_ref = refs[:4]
        ride_in = refs[4:4 + nr]
        proj_ref, h1_ref = refs[4 + nr:6 + nr]
        ride_out = refs[6 + nr:6 + 2 * nr]
        sems = refs[6 + 2 * nr:]
        _ride_start(ride_modes, pl.program_id(0), N // TM, ride_in, ride_out, sems)
        x = x_ref[...]
        r = lax.rsqrt(_mean_last(x * x) + EPS)
        h = (x * r * pw_ref[...]) * (1.0 + mod_ref[1:2, :]) + mod_ref[0:1, :]
        hb = _bf(h)
        h1_ref[...] = hb
        proj_ref[...] = _dot(hb, w_ref[...])
        _ride_wait(ride_modes, pl.program_id(0), N // TM, ride_in, ride_out, sems)

    return pl.pallas_call(
        body, name="in_proj", grid=(N // TM,),
        in_specs=[pl.BlockSpec((TM, D_MODEL), lambda i: (i, 0)), _mod_spec(tps),
                  pl.BlockSpec((1, D_MODEL), lambda i: (0, 0)),
                  pl.BlockSpec((D_MODEL, IN_COLS), lambda i: (0, 0))] + [ANY_SPEC] * nr,
        out_specs=[pl.BlockSpec((TM, IN_COLS), lambda i: (i, 0)),
                   pl.BlockSpec((TM, D_MODEL), lambda i: (i, 0))] + [ANY_SPEC] * nr,
        out_shape=[SDS((N, IN_COLS), F32), SDS((N, D_MODEL), BF16)] + _exchange_shapes(ride_srcs, ride_modes),
        scratch_shapes=_exchange_sems(nr),
        compiler_params=_params(("arbitrary",), VMEM_LIMIT_BIG),
    )(x2, mod8, pre_w, w_in_bf, *ride_srcs)


def _rope_tables(T):
    half = ROPE_DIM // 2
    inv_freq = ROPE_THETA ** (-jnp.arange(0, ROPE_DIM, 2, dtype=F32) / ROPE_DIM)
    ang = jnp.arange(T, dtype=F32)[:, None] * inv_freq[None, :]
    cos, sin = jnp.cos(ang), jnp.sin(ang)
    ones = jnp.ones((T, ATT_HEAD_DIM - ROPE_DIM), F32)
    zeros = jnp.zeros((T, ATT_HEAD_DIM - ROPE_DIM), F32)
    zh = jnp.zeros((T, half), F32)
    cos64 = jnp.concatenate([cos, cos, ones], axis=1)
    sin_left = jnp.concatenate([-sin, zh, zeros], axis=1)
    sin_right = jnp.concatenate([zh, sin, zeros], axis=1)
    rep = LANES // ATT_HEAD_DIM
    return jnp.tile(cos64, (1, rep)), jnp.tile(sin_left, (1, rep)), jnp.tile(sin_right, (1, rep))


def _rope(xc, cs, sl, sr):
    return xc * cs + pltpu.roll(xc, LANES - 8, 1) * sl + pltpu.roll(xc, 8, 1) * sr


def _rope_t(dy, cs, sl, sr):
    return dy * cs + pltpu.roll(dy * sl, 8, 1) + pltpu.roll(dy * sr, LANES - 8, 1)


ATT_SCALE = ATT_HEAD_DIM ** -0.5


def _band_masks():
    cols = ATT_GROUP * WINDOW
    j = lax.broadcasted_iota(jnp.int32, (2 * WINDOW, cols), 0)
    i = lax.broadcasted_iota(jnp.int32, (2 * WINDOW, cols), 1) & (WINDOW - 1)
    diff = i + WINDOW - j
    return (diff >= 0) & (diff < WINDOW), j >= WINDOW


def _sink_row(sink_ref, hk):
    return jnp.concatenate(
        [jnp.full((1, WINDOW), sink_ref[0, ATT_GROUP * hk + g], F32) for g in range(ATT_GROUP)], axis=1)


def _softmax_band(qs, kk, mask, sink):
    s = jnp.where(mask, _dot_nt(kk, qs), jnp.finfo(F32).min)
    m = jnp.maximum(jnp.max(s, axis=0, keepdims=True), sink)
    p = jnp.exp(s - m)
    es = jnp.exp(sink - m)
    inv = 1.0 / (jnp.sum(p, axis=0, keepdims=True) + es)
    return p, inv, es


def _stack_heads(parts, hk):
    hs = []
    for g in range(ATT_GROUP):
        h = ATT_GROUP * hk + g
        hs.append(parts[h // 2][:, (h % 2) * ATT_HEAD_DIM:(h % 2 + 1) * ATT_HEAD_DIM])
    return jnp.concatenate(hs, axis=0)


def _attn_fwd(proj3, tables, sinks, attn_w, ride_srcs, ride_modes):
    B, T, _ = proj3.shape
    nb = T // WINDOW
    nr = len(ride_srcs)
    cos, sinl, sinr = tables

    def body(*refs):
        q_ref, k_ref, v_ref, cos_ref, sl_ref, sr_ref, sink_ref, aw_ref = refs[:8]
        ride_in = refs[8:8 + nr]
        o_ref, an_ref, qr_ref, kr_ref = refs[8 + nr:12 + nr]
        ride_out = refs[12 + nr:12 + 2 * nr]
        kpad, vpad = refs[12 + 2 * nr:14 + 2 * nr]
        sems = refs[14 + 2 * nr:]
        _ride_start(ride_modes, pl.program_id(0), B, ride_in, ride_out, sems)

        kpad[0:WINDOW, :] = jnp.zeros((WINDOW, LANES), BF16)
        vpad[0:WINDOW, :] = jnp.zeros((WINDOW, LANES), BF16)
        window, current = _band_masks()

        def block(n, carry):
            r0 = pl.multiple_of(n * WINDOW, WINDOW)
            rows = pl.ds(r0, WINDOW)
            nxt = pl.ds(r0 + WINDOW, WINDOW)
            band = pl.ds(r0, 2 * WINDOW)
            cs, sl, sr = cos_ref[rows, :], sl_ref[rows, :], sr_ref[rows, :]
            kb = _bf(_rope(k_ref[rows, :], cs, sl, sr))
            kpad[nxt, :] = kb
            kr_ref[rows, :] = kb
            vpad[nxt, :] = _bf(v_ref[rows, :])
            qparts = []
            for j in range(ATT_WIDTH // LANES):
                qp = _bf(_rope(q_ref[rows, j * LANES:(j + 1) * LANES], cs, sl, sr) * ATT_SCALE)
                qr_ref[rows, j * LANES:(j + 1) * LANES] = qp
                qparts.append(qp)
            mask = window & (current | (n > 0))
            for hk in range(ATT_KV_HEADS):
                lanes = slice(hk * ATT_HEAD_DIM, (hk + 1) * ATT_HEAD_DIM)
                qs = _stack_heads(qparts, hk)
                p, inv, _ = _softmax_band(qs, kpad[band, lanes], mask, _sink_row(sink_ref, hk))
                ot = _dot_tn(vpad[band, lanes], _bf(p)) * inv
                for g in range(ATT_GROUP):
                    h = ATT_GROUP * hk + g
                    o_ref[rows, h * ATT_HEAD_DIM:(h + 1) * ATT_HEAD_DIM] = ot[:, g * WINDOW:(g + 1) * WINDOW].T
            ob = o_ref[rows, :]
            an_ref[rows, :] = _bf(ob * lax.rsqrt(_mean_last(ob * ob) + EPS) * aw_ref[...])
            return carry

        lax.fori_loop(0, nb, block, 0)
        _ride_wait(ride_modes, pl.program_id(0), B, ride_in, ride_out, sems)

    seq = lambda w, j: pl.BlockSpec((None, T, w), lambda b: (b, 0, j))
    full = lambda r, w: pl.BlockSpec((r, w), lambda b: (0, 0))
    return pl.pallas_call(
        body, name="attn_fwd", grid=(B,),
        in_specs=[seq(ATT_WIDTH, 0), seq(LANES, 4), seq(LANES, 5),
                  full(T, LANES), full(T, LANES), full(T, LANES),
                  pl.BlockSpec(memory_space=pltpu.SMEM), full(1, ATT_WIDTH)] + [ANY_SPEC] * nr,
        out_specs=[seq(ATT_WIDTH, 0), seq(ATT_WIDTH, 0), seq(ATT_WIDTH, 0), seq(LANES, 0)] + [ANY_SPEC] * nr,
        out_shape=[SDS((B, T, ATT_WIDTH), F32), SDS((B, T, ATT_WIDTH), BF16),
                   SDS((B, T, ATT_WIDTH), BF16), SDS((B, T, LANES), BF16)] + _exchange_shapes(ride_srcs, ride_modes),
        scratch_shapes=[pltpu.VMEM((T + WINDOW, LANES), BF16), pltpu.VMEM((T + WINDOW, LANES), BF16)]
        + _exchange_sems(nr),
        compiler_params=_params(("arbitrary",), VMEM_LIMIT_BIG),
    )(proj3, proj3, proj3, cos, sinl, sinr, sinks, attn_w, *ride_srcs)


HG_GROUP = 8
HG_ROWS = HG_GROUP * HG_CHUNK


HG_STACK = HG_GROUP * HG_HEAD_DIM


def _group_masks():
    r = lax.broadcasted_iota(jnp.int32, (HG_ROWS, HG_ROWS), 0)
    c = lax.broadcasted_iota(jnp.int32, (HG_ROWS, HG_ROWS), 1)
    same = (r // HG_CHUNK) == (c // HG_CHUNK)
    return same & (r >= c), same & (c >= r), same & (c > r), same


def _row_chunk():
    return lax.broadcasted_iota(jnp.int32, (HG_ROWS, HG_HEAD_DIM), 0) // HG_CHUNK


def _spread(a, row_chunk):
    return jnp.concatenate([jnp.where(row_chunk == c, a, jnp.zeros_like(a)) for c in range(HG_GROUP)], axis=1)


def _pick(r, row_chunk):
    out = jnp.where(row_chunk == 0, r[:, :HG_HEAD_DIM], 0.0)
    for c in range(1, HG_GROUP):
        out = out + jnp.where(row_chunk == c, r[:, c * HG_HEAD_DIM:(c + 1) * HG_HEAD_DIM], 0.0)
    return out


def _lane_block(a, c):
    return a[:, c * HG_HEAD_DIM:(c + 1) * HG_HEAD_DIM]


def _ones_bf(mask):
    return jnp.where(mask, 1.0, 0.0).astype(BF16)


def _hgrn_gates(hq, hf, lb, sums_bf, ebl_scr):
    sq = _sigmoid(hq)
    q = hq * sq
    sg = _sigmoid(hf)
    f = lb + (1.0 - lb) * sg
    k = 1.0 - f
    cs = _tri_sum(sums_bf, jnp.log(f))
    b, rem = cs[:HG_ROWS], cs[HG_ROWS:]
    eb, enb, e2 = jnp.exp(b), jnp.exp(-b), jnp.exp(rem)
    ebl_scr[...] = eb * e2
    return dict(sq=sq, sg=sg, f=f, eb=eb, enb=enb, e2=e2, qd=q * eb, kd=k * enb, k2=k * e2)


def _hgrn_specs(B, T):
    head = lambda base: pl.BlockSpec((None, T, LANES), lambda b, h: (b, 0, base + h))
    return head


def _chunk_rows(c):
    return slice(c * HG_CHUNK, (c + 1) * HG_CHUNK)


def _hgrn_fwd(proj3, lb, hg_w, ride_srcs, ride_modes):
    B, T, _ = proj3.shape
    nc = T // HG_CHUNK
    ng = T // HG_ROWS
    nr = len(ride_srcs)
    head = _hgrn_specs(B, T)

    def body(*refs):
        hq_ref, hf_ref, hi_ref, hg_ref, lb_ref, gw_ref = refs[:6]
        ride_in = refs[6:6 + nr]
        o_ref, rg_ref, sp_ref = refs[6 + nr:9 + nr]
        ride_out = refs[9 + nr:9 + 2 * nr]
        st, ebl_scr = refs[9 + 2 * nr:11 + 2 * nr]
        sems = refs[11 + 2 * nr:]
        step = pl.program_id(0) * HG_HEADS + pl.program_id(1)
        _ride_start(ride_modes, step, B * HG_HEADS, ride_in, ride_out, sems)

        st[...] = jnp.zeros((HG_HEAD_DIM, HG_HEAD_DIM), F32)
        lo, _, ups, _ = _group_masks()
        sums_bf = jnp.concatenate([_ones_bf(lo), _ones_bf(ups)], axis=0)
        row_chunk = _row_chunk()
        lbv = lb_ref[...]

        def group(gi, carry):
            rows = pl.ds(pl.multiple_of(gi * HG_ROWS, HG_ROWS), HG_ROWS)
            gt = _hgrn_gates(hq_ref[rows, :], hf_ref[rows, :], lbv, sums_bf, ebl_scr)
            v, qd, kd, k2 = _bf(hi_ref[rows, :]), _bf(gt["qd"]), _bf(gt["kd"]), _bf(gt["k2"])
            a = jnp.where(lo, _dot_nt(qd, kd), 0.0)
            kv = _dot_tn(v, _bf(_spread(gt["k2"], row_chunk)))
            s = st[...]
            before = []
            for c in range(HG_GROUP):
                before.append(s)
                s = s * ebl_scr[c * HG_CHUNK:c * HG_CHUNK + 1, :] + _lane_block(kv, c)
            st[...] = s
            sp = jnp.concatenate(before, axis=1)
            sp_ref[gi] = sp
            o = _dot(_bf(a), v) + _dot_nt(_bf(_spread(gt["qd"], row_chunk)), _bf(sp))
            o_ref[rows, :] = o
            hg = hg_ref[rows, :]
            rn = o * lax.rsqrt(_mean_last(o * o) + EPS) * gw_ref[...]
            rg_ref[rows, :] = _bf(rn * (hg * _sigmoid(hg)))
            return carry

        lax.fori_loop(0, ng, group, 0)
        _ride_wait(ride_modes, step, B * HG_HEADS, ride_in, ride_out, sems)

    out_head = pl.BlockSpec((None, T, LANES), lambda b, h: (b, 0, h))
    return pl.pallas_call(
        body, name="hgrn_fwd", grid=(B, HG_HEADS),
        in_specs=[head(6), head(10), head(14), head(18),
                  pl.BlockSpec((1, LANES), lambda b, h: (0, h)),
                  pl.BlockSpec((1, LANES), lambda b, h: (0, 0))] + [ANY_SPEC] * nr,
        out_specs=[out_head, out_head,
                   pl.BlockSpec((None, None, ng, HG_HEAD_DIM, HG_STACK), lambda b, h: (b, h, 0, 0, 0))]
        + [ANY_SPEC] * nr,
        out_shape=[SDS((B, T, HG_WIDTH), F32), SDS((B, T, HG_WIDTH), BF16),
                   SDS((B, HG_HEADS, ng, HG_HEAD_DIM, HG_STACK), F32)] + _exchange_shapes(ride_srcs, ride_modes),
        scratch_shapes=[pltpu.VMEM((HG_HEAD_DIM, HG_HEAD_DIM), F32), pltpu.VMEM((HG_ROWS, LANES), F32)]
        + _exchange_sems(nr),
        compiler_params=_params(("arbitrary", "arbitrary"), VMEM_LIMIT_BIG),
    )(proj3, proj3, proj3, proj3, lb, hg_w, *ride_srcs)


def _mix_out(x2, attn_n, rec_g, mod8, post_w, w_out_bf, T, ride_srcs, ride_modes):
    N = x2.shape[0]
    TM = _tile_rows(T)
    tps = T // TM
    nr = len(ride_srcs)

    def body(*refs):
        x_ref, an_ref, rg_ref, mod_ref, pw_ref, w_ref = refs[:6]
        ride_in = refs[6:6 + nr]
        mix_ref, x1_ref, cat_ref = refs[6 + nr:9 + nr]
        ride_out = refs[9 + nr:9 + 2 * nr]
        sems = refs[9 + 2 * nr:]
        _ride_start(ride_modes, pl.program_id(0), N // TM, ride_in, ride_out, sems)
        cat = jnp.concatenate([an_ref[...], rg_ref[...]], axis=1)
        cat_ref[...] = cat
        mix = _dot(cat, w_ref[...])
        mix_ref[...] = mix
        r = lax.rsqrt(_mean_last(mix * mix) + EPS)
        x1_ref[...] = x_ref[...] + mod_ref[2:3, :] * (mix * r * pw_ref[...])
        _ride_wait(ride_modes, pl.program_id(0), N // TM, ride_in, ride_out, sems)

    row = lambda w: pl.BlockSpec((TM, w), lambda i: (i, 0))
    return pl.pallas_call(
        body, name="mix_out", grid=(N // TM,),
        in_specs=[row(D_MODEL), row(ATT_WIDTH), row(HG_WIDTH), _mod_spec(tps),
                  pl.BlockSpec((1, D_MODEL), lambda i: (0, 0)),
                  pl.BlockSpec((D_MODEL, D_MODEL), lambda i: (0, 0))] + [ANY_SPEC] * nr,
        out_specs=[row(D_MODEL), row(D_MODEL), row(D_MODEL)] + [ANY_SPEC] * nr,
        out_shape=[SDS((N, D_MODEL), F32), SDS((N, D_MODEL), F32), SDS((N, D_MODEL), BF16)]
        + _exchange_shapes(ride_srcs, ride_modes),
        scratch_shapes=_exchange_sems(nr),
        compiler_params=_params(("arbitrary",), VMEM_LIMIT_BIG),
    )(x2, attn_n, rec_g, mod8, post_w, w_out_bf, *ride_srcs)


def _load_weights_once(pairs, sem):
    @pl.when(pl.program_id(0) == 0)
    def _():
        cps = [pltpu.make_async_copy(src, dst, sem.at[i]) for i, (src, dst) in enumerate(pairs)]
        for cp in cps:
            cp.start()
        for cp in cps:
            cp.wait()


def _mlp_fwd(x1, mod8, pre_w, w_up_bf, w_down_bf, T):
    N = x1.shape[0]
    TM = _tile_rows(T)
    tps = T // TM

    def body(x_ref, mod_ref, pw_ref, wu_hbm, wd_hbm, up_ref, d_ref, h2_ref, wu, wd, sem):
        _load_weights_once([(wu_hbm, wu), (wd_hbm, wd)], sem)
        x = x_ref[...]
        r = lax.rsqrt(_mean_last(x * x) + EPS)
        h = (x * r * pw_ref[...]) * (1.0 + mod_ref[4:5, :]) + mod_ref[3:4, :]
        hb = _bf(h)
        h2_ref[...] = hb
        up = _dot(hb, wu[...])
        up_ref[...] = up
        ru = jnp.maximum(up, 0.0)
        d_ref[...] = _dot(_bf(ru * ru), wd[...])

    row = lambda w: pl.BlockSpec((TM, w), lambda i: (i, 0))
    return pl.pallas_call(
        body, name="mlp_fwd", grid=(N // TM,),
        in_specs=[row(D_MODEL), _mod_spec(tps), pl.BlockSpec((1, D_MODEL), lambda i: (0, 0)),
                  pl.BlockSpec(memory_space=pl.ANY), pl.BlockSpec(memory_space=pl.ANY)],
        out_specs=[row(D_FF), row(D_MODEL), row(D_MODEL)],
        out_shape=[SDS((N, D_FF), F32), SDS((N, D_MODEL), F32), SDS((N, D_MODEL), BF16)],
        scratch_shapes=[pltpu.VMEM((D_MODEL, D_FF), BF16), pltpu.VMEM((D_FF, D_MODEL), BF16),
                        pltpu.SemaphoreType.DMA((2,))],
        compiler_params=_params(("arbitrary",), VMEM_LIMIT_BIG),
    )(x1, mod8, pre_w, w_up_bf, w_down_bf)


def _acc_rows(acc_ref, first, rows):
    @pl.when(first)
    def _():
        acc_ref[...] = jnp.zeros(acc_ref.shape, F32)
    for i, r in enumerate(rows):
        acc_ref[i:i + 1, :] += r


def _mlp_bwd(x1, d, up, tgt, mod8, pre_w, post_w, w_down_bf, w_up_bf, T):
    N = x1.shape[0]
    TM = _tile_rows(T)
    tps = T // TM

    def body(x_ref, d_ref, up_ref, t_ref, mod_ref, pw_ref, qw_ref, wd_hbm, wu_hbm,
             dx_ref, u_ref, dup_ref, dd_ref, acc_ref, wd, wu, sem):
        _load_weights_once([(wd_hbm, wd), (wu_hbm, wu)], sem)
        sh2, sc2, g2 = mod_ref[3:4, :], mod_ref[4:5, :], mod_ref[5:6, :]
        x = x_ref[...]
        r1 = lax.rsqrt(_mean_last(x * x) + EPS)
        xh = x * r1
        n2 = xh * pw_ref[...]
        dv = d_ref[...]
        rd = lax.rsqrt(_mean_last(dv * dv) + EPS)
        dh = dv * rd
        rr = dh * qw_ref[...]
        e = x + g2 * rr - t_ref[...]
        loss = 0.5 * jnp.sum(_sum_rows(e * e), axis=1, keepdims=True) / D_MODEL
        dy = e * (1.0 / D_MODEL)
        dg2 = _sum_rows(dy * rr)
        drr = dy * g2
        dw_post = _sum_rows(drr * dh)
        ddh = drr * qw_ref[...]
        dd = _bf(rd * (ddh - dh * _mean_last(ddh * dh)))
        dd_ref[...] = dd
        ru = jnp.maximum(up_ref[...], 0.0)
        u_ref[...] = _bf(ru * ru)
        dup = _bf(_dot_nt(dd, wd[...]) * (2.0 * ru))
        dup_ref[...] = dup
        dh2 = _dot_nt(dup, wu[...])
        dsh2 = _sum_rows(dh2)
        dsc2 = _sum_rows(dh2 * n2)
        dn2 = dh2 * (1.0 + sc2)
        dw_pre = _sum_rows(dn2 * xh)
        dxh = dn2 * pw_ref[...]
        dx_ref[...] = dy + r1 * (dxh - xh * _mean_last(dxh * xh))
        _acc_rows(acc_ref, pl.program_id(0) % tps == 0,
                  [dsh2, dsc2, dg2, dw_pre, dw_post, jnp.broadcast_to(loss, (1, D_MODEL))])

    row = lambda w: pl.BlockSpec((TM, w), lambda i: (i, 0))
    vec = pl.BlockSpec((1, D_MODEL), lambda i: (0, 0))
    B = N // T
    return pl.pallas_call(
        body, name="mlp_bwd", grid=(N // TM,),
        in_specs=[row(D_MODEL), row(D_MODEL), row(D_FF), row(D_MODEL), _mod_spec(tps), vec, vec,
                  pl.BlockSpec(memory_space=pl.ANY), pl.BlockSpec(memory_space=pl.ANY)],
        out_specs=[row(D_MODEL), row(D_FF), row(D_FF), row(D_MODEL), _mod_spec(tps)],
        out_shape=[SDS((N, D_MODEL), F32), SDS((N, D_FF), BF16), SDS((N, D_FF), BF16),
                   SDS((N, D_MODEL), BF16), SDS((B, 8, D_MODEL), F32)],
        scratch_shapes=[pltpu.VMEM((D_FF, D_MODEL), BF16), pltpu.VMEM((D_MODEL, D_FF), BF16),
                        pltpu.SemaphoreType.DMA((2,))],
        compiler_params=_params(("arbitrary",), VMEM_LIMIT_BIG),
    )(x1, d, up, tgt, mod8, pre_w, post_w, w_down_bf, w_up_bf)


def _mix_bwd(mix, dx1, mod8, post_w, w_out_bf, T, ride_srcs, ride_modes):
    N = mix.shape[0]
    TM = _tile_rows(T)
    tps = T // TM
    nr = len(ride_srcs)

    def body(*refs):
        mix_ref, dx_ref, mod_ref, pw_ref, w_ref = refs[:5]
        ride_in = refs[5:5 + nr]
        dan_ref, drg_ref, dmix_ref, acc_ref = refs[5 + nr:9 + nr]
        ride_out = refs[9 + nr:9 + 2 * nr]
        sems = refs[9 + 2 * nr:]
        _ride_start(ride_modes, pl.program_id(0), N // TM, ride_in, ride_out, sems)
        g1 = mod_ref[2:3, :]
        mix = mix_ref[...]
        dx1 = dx_ref[...]
        rm = lax.rsqrt(_mean_last(mix * mix) + EPS)
        mh = mix * rm
        dg1 = _sum_rows(dx1 * (mh * pw_ref[...]))
        dr = dx1 * g1
        dw_post = _sum_rows(dr * mh)
        dmh = dr * pw_ref[...]
        dmix = _bf(rm * (dmh - mh * _mean_last(dmh * mh)))
        dmix_ref[...] = dmix
        dcat = _dot_nt(dmix, w_ref[...])
        dan_ref[...] = dcat[:, :ATT_WIDTH]
        drg_ref[...] = dcat[:, ATT_WIDTH:]
        _acc_rows(acc_ref, pl.program_id(0) % tps == 0, [dg1, dw_post])
        _ride_wait(ride_modes, pl.program_id(0), N // TM, ride_in, ride_out, sems)

    row = lambda w: pl.BlockSpec((TM, w), lambda i: (i, 0))
    B = N // T
    return pl.pallas_call(
        body, name="mix_bwd", grid=(N // TM,),
        in_specs=[row(D_MODEL), row(D_MODEL), _mod_spec(tps), pl.BlockSpec((1, D_MODEL), lambda i: (0, 0)),
                  pl.BlockSpec((D_MODEL, D_MODEL), lambda i: (0, 0))] + [ANY_SPEC] * nr,
        out_specs=[row(ATT_WIDTH), row(HG_WIDTH), row(D_MODEL), _mod_spec(tps)] + [ANY_SPEC] * nr,
        out_shape=[SDS((N, ATT_WIDTH), F32), SDS((N, HG_WIDTH), F32), SDS((N, D_MODEL), BF16),
                   SDS((B, 8, D_MODEL), F32)] + _exchange_shapes(ride_srcs, ride_modes),
        scratch_shapes=_exchange_sems(nr),
        compiler_params=_params(("arbitrary",), VMEM_LIMIT_BIG),
    )(mix, dx1, mod8, post_w, w_out_bf, *ride_srcs)


def _hgrn_bwd(proj3, lb, hg_w, o, s_prev, drg, ride_srcs, ride_modes):
    B, T, _ = proj3.shape
    nc = T // HG_CHUNK
    ng = T // HG_ROWS
    nr = len(ride_srcs)
    head = _hgrn_specs(B, T)

    def body(*refs):
        hq_ref, hf_ref, hi_ref, hg_ref, lb_ref, gw_ref, o_ref, sp_ref, drg_ref = refs[:9]
        ride_in = refs[9:9 + nr]
        dhq_ref, dhf_ref, dhi_ref, dhg_ref, dlb_ref, dgw_ref = refs[9 + nr:15 + nr]
        ride_out = refs[15 + nr:15 + 2 * nr]
        dst, ebl_scr = refs[15 + 2 * nr:17 + 2 * nr]
        sems = refs[17 + 2 * nr:]
        step = pl.program_id(0) * HG_HEADS + pl.program_id(1)
        _ride_start(ride_modes, step, B * HG_HEADS, ride_in, ride_out, sems)

        dst[...] = jnp.zeros((HG_HEAD_DIM, HG_HEAD_DIM), F32)
        lo, up, ups, same = _group_masks()
        sums_bf = jnp.concatenate([_ones_bf(lo), _ones_bf(ups)], axis=0)
        back_bf = jnp.concatenate([_ones_bf(up), _ones_bf(same)], axis=1)
        row_chunk = _row_chunk()
        lbv = lb_ref[...]
        gw = gw_ref[...]

        def group(i, carry):
            dlb, dgw = carry
            gi = ng - 1 - i
            rows = pl.ds(pl.multiple_of(gi * HG_ROWS, HG_ROWS), HG_ROWS)
            hq = hq_ref[rows, :]
            gt = _hgrn_gates(hq, hf_ref[rows, :], lbv, sums_bf, ebl_scr)
            sq, sg, qdf, kdf, k2f = gt["sq"], gt["sg"], gt["qd"], gt["kd"], gt["k2"]
            v, qd, kd, k2 = _bf(hi_ref[rows, :]), _bf(qdf), _bf(kdf), _bf(k2f)
            ov = o_ref[rows, :]
            hg = hg_ref[rows, :]
            shg = _sigmoid(hg)
            dr = drg_ref[rows, :]
            ro = lax.rsqrt(_mean_last(ov * ov) + EPS)
            oh = ov * ro
            dhg_ref[rows, :] = dr * (oh * gw) * (shg + hg * shg * (1.0 - shg))
            drn = dr * (hg * shg)
            dgw = dgw + _sum_rows(drn * oh)
            doh = drn * gw
            do = _bf(ro * (doh - oh * _mean_last(doh * oh)))
            a = jnp.where(lo, _dot_nt(qd, kd), 0.0)
            da = _bf(jnp.where(lo, _dot_nt(do, v), 0.0))
            dv = _dot_tn(_bf(a), do)
            dqd = _dot(da, kd)
            dkd = _dot_tn(da, qd)
            sp = sp_ref[gi]
            incr = _dot_tn(do, _bf(_spread(qdf, row_chunk)))
            ds = dst[...]
            after = [None] * HG_GROUP
            for c in reversed(range(HG_GROUP)):
                after[c] = ds
                ds = ds * ebl_scr[c * HG_CHUNK:c * HG_CHUNK + 1, :] + _lane_block(incr, c)
            dst[...] = ds
            dss = jnp.concatenate(after, axis=1)
            dssb = _bf(dss)
            dk2 = _pick(_dot(v, dssb), row_chunk)
            dhi_ref[rows, :] = dv + _dot_nt(_bf(_spread(k2f, row_chunk)), dssb)
            dqd = dqd + _pick(_dot(do, _bf(sp)), row_chunk)
            debl = _sum_rows(dss * sp)
            dbl = jnp.concatenate(
                [jnp.broadcast_to(_lane_block(debl, c) * ebl_scr[c * HG_CHUNK:c * HG_CHUNK + 1, :],
                                  (HG_CHUNK, HG_HEAD_DIM)) for c in range(HG_GROUP)], axis=0)
            k2g = dk2 * k2f
            db = dqd * qdf - dkd * kdf - k2g
            dk = dkd * gt["enb"] + dk2 * gt["e2"]
            dg = _tri_sum(back_bf, jnp.concatenate([db, k2g], axis=0)) + dbl
            df = dg / gt["f"] - dk
            dhf_ref[rows, :] = df * (1.0 - lbv) * sg * (1.0 - sg)
            dlb = dlb + _sum_rows(df * (1.0 - sg))
            dhq_ref[rows, :] = (dqd * gt["eb"]) * (sq + hq * sq * (1.0 - sq))
            return dlb, dgw

        zero = jnp.zeros((1, LANES), F32)
        dlb, dgw = lax.fori_loop(0, ng, group, (zero, zero))
        dlb_ref[...] = jnp.broadcast_to(dlb, (8, LANES))
        dgw_ref[...] = jnp.broadcast_to(dgw, (8, LANES))
        _ride_wait(ride_modes, step, B * HG_HEADS, ride_in, ride_out, sems)

    out_head = pl.BlockSpec((None, T, LANES), lambda b, h: (b, 0, h))
    small = pl.BlockSpec((None, 8, LANES), lambda b, h: (b, 0, h))
    return pl.pallas_call(
        body, name="hgrn_bwd", grid=(B, HG_HEADS),
        in_specs=[head(6), head(10), head(14), head(18),
                  pl.BlockSpec((1, LANES), lambda b, h: (0, h)),
                  pl.BlockSpec((1, LANES), lambda b, h: (0, 0)),
                  out_head,
                  pl.BlockSpec((None, None, ng, HG_HEAD_DIM, HG_STACK), lambda b, h: (b, h, 0, 0, 0)),
                  out_head] + [ANY_SPEC] * nr,
        out_specs=[out_head, out_head, out_head, out_head, small, small] + [ANY_SPEC] * nr,
        out_shape=[SDS((B, T, HG_WIDTH), F32)] * 4 + [SDS((B, 8, HG_WIDTH), F32)] * 2
        + _exchange_shapes(ride_srcs, ride_modes),
        scratch_shapes=[pltpu.VMEM((HG_HEAD_DIM, HG_HEAD_DIM), F32), pltpu.VMEM((HG_ROWS, LANES), F32)]
        + _exchange_sems(nr),
        compiler_params=_params(("arbitrary", "arbitrary"), VMEM_LIMIT_BIG),
    )(proj3, proj3, proj3, proj3, lb, hg_w, o, s_prev, drg, *ride_srcs)


def _attn_bwd(qr, kr, proj3, attn_o, dan, tables, sinks, attn_w, ride_srcs, ride_modes):
    B, T, _ = proj3.shape
    nb = T // WINDOW
    nr = len(ride_srcs)
    cos, sinl, sinr = tables
    QKV = ATT_WIDTH + 2 * LANES

    def body(*refs):
        qr_ref, kr_ref, v_ref, o_ref, dan_ref, cos_ref, sl_ref, sr_ref, sink_ref, aw_ref = refs[:10]
        ride_in = refs[10:10 + nr]
        dqkv_ref, dsink_ref, daw_ref = refs[10 + nr:13 + nr]
        ride_out = refs[13 + nr:13 + 2 * nr]
        kpad, vpad, dkpad, dvpad, dqb, dsk = refs[13 + 2 * nr:19 + 2 * nr]
        sems = refs[19 + 2 * nr:]
        _ride_start(ride_modes, pl.program_id(0), B, ride_in, ride_out, sems)

        window, current = _band_masks()
        kpad[0:WINDOW, :] = jnp.zeros((WINDOW, LANES), BF16)
        vpad[0:WINDOW, :] = jnp.zeros((WINDOW, LANES), BF16)
        kpad[WINDOW:, :] = kr_ref[...]
        vpad[WINDOW:, :] = _bf(v_ref[...])
        dkpad[...] = jnp.zeros(dkpad.shape, F32)
        dvpad[...] = jnp.zeros(dvpad.shape, F32)
        dsk[...] = jnp.zeros(dsk.shape, F32)
        aw = aw_ref[...]

        def block(n, daw):
            r0 = pl.multiple_of(n * WINDOW, WINDOW)
            rows = pl.ds(r0, WINDOW)
            band = pl.ds(r0, 2 * WINDOW)
            ob = o_ref[rows, :]
            dn = dan_ref[rows, :]
            ro = lax.rsqrt(_mean_last(ob * ob) + EPS)
            oh = ob * ro
            daw = daw + _sum_rows(dn * oh)
            doh = dn * aw
            do = _bf(ro * (doh - oh * _mean_last(doh * oh)))
            doparts = [do[:, j * LANES:(j + 1) * LANES] for j in range(ATT_WIDTH // LANES)]
            qparts = [qr_ref[rows, j * LANES:(j + 1) * LANES] for j in range(ATT_WIDTH // LANES)]
            mask = window & (current | (n > 0))
            for hk in range(ATT_KV_HEADS):
                lanes = slice(hk * ATT_HEAD_DIM, (hk + 1) * ATT_HEAD_DIM)
                qs = _stack_heads(qparts, hk)
                dos = _stack_heads(doparts, hk)
                kk, vv = kpad[band, lanes], vpad[band, lanes]
                p, inv, es = _softmax_band(qs, kk, mask, _sink_row(sink_ref, hk))
                p = p * inv
                dp = _dot_nt(vv, dos)
                delta = jnp.sum(p * dp, axis=0, keepdims=True)
                ds = _bf(p * (dp - delta))
                sk = (es * inv) * delta
                dqt = _dot_tn(kk, ds) * ATT_SCALE
                dkpad[band, lanes] += _dot(ds, qs)
                dvpad[band, lanes] += _dot(_bf(p), dos)
                for g in range(ATT_GROUP):
                    h = ATT_GROUP * hk + g
                    cols = slice(g * WINDOW, (g + 1) * WINDOW)
                    dqb[:, h * ATT_HEAD_DIM:(h + 1) * ATT_HEAD_DIM] = dqt[:, cols].T
                    dsk[h:h + 1, :] += jnp.broadcast_to(-jnp.sum(sk[:, cols], axis=1, keepdims=True), (1, LANES))
            cs, sl, sr = cos_ref[rows, :], sl_ref[rows, :], sr_ref[rows, :]
            for j in range(ATT_WIDTH // LANES):
                dqkv_ref[rows, j * LANES:(j + 1) * LANES] = _rope_t(dqb[:, j * LANES:(j + 1) * LANES], cs, sl, sr)
            return daw

        daw = lax.fori_loop(0, nb, block, jnp.zeros((1, ATT_WIDTH), F32))
        daw_ref[...] = jnp.broadcast_to(daw, (8, ATT_WIDTH))
        dsink_ref[...] = dsk[...]

        def finish(n, carry):
            r0 = pl.multiple_of(n * WINDOW, WINDOW)
            rows = pl.ds(r0, WINDOW)
            nxt = pl.ds(r0 + WINDOW, WINDOW)
            cs, sl, sr = cos_ref[rows, :], sl_ref[rows, :], sr_ref[rows, :]
            dqkv_ref[rows, ATT_WIDTH:ATT_WIDTH + LANES] = _rope_t(dkpad[nxt, :], cs, sl, sr)
            dqkv_ref[rows, ATT_WIDTH + LANES:QKV] = dvpad[nxt, :]
            return carry

        lax.fori_loop(0, nb, finish, 0)
        _ride_wait(ride_modes, pl.program_id(0), B, ride_in, ride_out, sems)

    seq = lambda w, j: pl.BlockSpec((None, T, w), lambda b: (b, 0, j))
    full = lambda r, w: pl.BlockSpec((r, w), lambda b: (0, 0))
    return pl.pallas_call(
        body, name="attn_bwd", grid=(B,),
        in_specs=[seq(ATT_WIDTH, 0), seq(LANES, 0), seq(LANES, 5), seq(ATT_WIDTH, 0), seq(ATT_WIDTH, 0),
                  full(T, LANES), full(T, LANES), full(T, LANES),
                  pl.BlockSpec(memory_space=pltpu.SMEM), full(1, ATT_WIDTH)] + [ANY_SPEC] * nr,
        out_specs=[seq(QKV, 0), pl.BlockSpec((None, 8, LANES), lambda b: (b, 0, 0)),
                   pl.BlockSpec((None, 8, ATT_WIDTH), lambda b: (b, 0, 0))] + [ANY_SPEC] * nr,
        out_shape=[SDS((B, T, QKV), F32), SDS((B, 8, LANES), F32), SDS((B, 8, ATT_WIDTH), F32)]
        + _exchange_shapes(ride_srcs, ride_modes),
        scratch_shapes=[pltpu.VMEM((T + WINDOW, LANES), BF16), pltpu.VMEM((T + WINDOW, LANES), BF16),
                        pltpu.VMEM((T + WINDOW, LANES), F32), pltpu.VMEM((T + WINDOW, LANES), F32),
                        pltpu.VMEM((WINDOW, ATT_WIDTH), F32), pltpu.VMEM((8, LANES), F32)] + _exchange_sems(nr),
        compiler_params=_params(("arbitrary",), VMEM_LIMIT_BIG),
    )(qr, kr, proj3, attn_o, dan, cos, sinl, sinr, sinks, attn_w, *ride_srcs)


def _in_bwd(x2, dx1, dqkv, dhq, dhf, dhi, dhg, mod8, pre_w, w_in_bf, T):
    N = x2.shape[0]
    TM = _tile_rows(T)
    tps = T // TM
    pieces = [(0, ATT_WIDTH + 2 * LANES), (768, HG_WIDTH), (1280, HG_WIDTH), (1792, HG_WIDTH), (2304, HG_WIDTH)]

    def body(x_ref, dx_ref, p0, p1, p2, p3, p4, mod_ref, pw_ref, w_ref, gx_ref, dproj_ref, acc_ref):
        sc1 = mod_ref[1:2, :]
        dh = jnp.zeros((TM, D_MODEL), F32)
        for ref, (off, width) in zip((p0, p1, p2, p3, p4), pieces):
            pb = _bf(ref[...])
            dproj_ref[:, off:off + width] = pb
            dh = dh + _dot_nt(pb, w_ref[:, off:off + width])
        x = x_ref[...]
        r = lax.rsqrt(_mean_last(x * x) + EPS)
        xh = x * r
        n1 = xh * pw_ref[...]
        dsh1 = _sum_rows(dh)
        dsc1 = _sum_rows(dh * n1)
        dn1 = dh * (1.0 + sc1)
        dw_pre = _sum_rows(dn1 * xh)
        dxh = dn1 * pw_ref[...]
        gx_ref[...] = dx_ref[...] + r * (dxh - xh * _mean_last(dxh * xh))
        _acc_rows(acc_ref, pl.program_id(0) % tps == 0, [dsh1, dsc1, dw_pre])

    row = lambda w: pl.BlockSpec((TM, w), lambda i: (i, 0))
    B = N // T
    return pl.pallas_call(
        body, name="in_bwd", grid=(N // TM,),
        in_specs=[row(D_MODEL), row(D_MODEL), row(768), row(HG_WIDTH), row(HG_WIDTH), row(HG_WIDTH),
                  row(HG_WIDTH), _mod_spec(tps), pl.BlockSpec((1, D_MODEL), lambda i: (0, 0)),
                  pl.BlockSpec((D_MODEL, IN_COLS), lambda i: (0, 0))],
        out_specs=[row(D_MODEL), row(IN_COLS), _mod_spec(tps)],
        out_shape=[SDS((N, D_MODEL), F32), SDS((N, IN_COLS), BF16), SDS((B, 8, D_MODEL), F32)],
        compiler_params=_params(("arbitrary",), VMEM_LIMIT_BIG),
    )(x2, dx1, dqkv, dhq, dhf, dhi, dhg, mod8, pre_w, w_in_bf)


def _matmul_tn(name, a, b, tn, by_owner=None):
    K, M = a.shape
    Nc = b.shape[1]
    tm = min(512, M)

    def body(a_ref, b_ref, o_ref):
        o_ref[...] = _bf(_dot_tn(a_ref[...], b_ref[...]))

    if by_owner == "cols":
        assert tn * N_DEV == Nc
        out_shape = SDS((2, N_DEV // 2, M, tn), BF16)
        out_spec = pl.BlockSpec((None, None, tm, tn), lambda i, j: (j % 2, j // 2, i, 0))
    elif by_owner == "rows":
        assert tm * N_DEV == M
        out_shape = SDS((2, N_DEV // 2, tm, Nc), BF16)
        out_spec = pl.BlockSpec((None, None, tm, tn), lambda i, j: (i % 2, i // 2, 0, j))
    else:
        out_shape = SDS((M, Nc), BF16)
        out_spec = pl.BlockSpec((tm, tn), lambda i, j: (i, j))
    return pl.pallas_call(
        body, name=name, grid=(M // tm, Nc // tn),
        in_specs=[pl.BlockSpec((K, tm), lambda i, j: (0, i)),
                  pl.BlockSpec((K, tn), lambda i, j: (0, j))],
        out_specs=out_spec, out_shape=out_shape,
        compiler_params=_params(("arbitrary", "arbitrary"), VMEM_LIMIT_BIG),
    )(a, b)


def _adamw_math(w, g, m, v):
    m2 = ADAM_B1 * m + (1.0 - ADAM_B1) * g
    v2 = ADAM_B2 * v + (1.0 - ADAM_B2) * (g * g)
    m_hat = m2 / (1.0 - ADAM_B1 ** ADAM_STEP)
    v_hat = v2 / (1.0 - ADAM_B2 ** ADAM_STEP)
    delta = -ADAM_LR * (m_hat / (jnp.sqrt(v_hat) + ADAM_EPS) + ADAM_WD * w)
    return delta, m2, v2


def _pair_add(name, gw, theirs):
    _, chips, r, c = gw.shape
    tr = r
    core = lax.axis_index("c").astype(jnp.int32).reshape(1)

    def body(core_ref, mine_ref, theirs_ref, o_ref):
        o_ref[...] = _bf(mine_ref[...].astype(F32) + theirs_ref[...].astype(F32))

    block = pl.BlockSpec((None, tr, c), lambda s, i, core_ref: (s, i, 0))
    grid_spec = pltpu.PrefetchScalarGridSpec(
        num_scalar_prefetch=1, grid=(chips, r // tr),
        in_specs=[pl.BlockSpec((None, None, tr, c), lambda s, i, core_ref: (core_ref[0], s, i, 0)), block],
        out_specs=block)
    return pl.pallas_call(
        body, name=name, grid_spec=grid_spec, out_shape=SDS((chips, r, c), BF16),
        compiler_params=_params(("arbitrary", "arbitrary")),
    )(core, gw, theirs)


def _reduce_adamw(name, parts, w, m, v):
    r, c = w.shape
    tr = r if r <= 256 else 256
    slots = parts.shape[0]

    def body(p_ref, w_ref, m_ref, v_ref, g_ref, d_ref, m2_ref, v2_ref):
        g = p_ref[0].astype(F32)
        for s in range(1, slots):
            g = g + p_ref[s].astype(F32)
        g_ref[...] = g
        d_ref[...], m2_ref[...], v2_ref[...] = _adamw_math(w_ref[...], g, m_ref[...], v_ref[...])

    blk = pl.BlockSpec((tr, c), lambda i: (i, 0))
    return pl.pallas_call(
        body, name=name, grid=(r // tr,),
        in_specs=[pl.BlockSpec((slots, tr, c), lambda i: (0, i, 0)), blk, blk, blk],
        out_specs=[blk] * 4, out_shape=[SDS((r, c), F32)] * 4,
        compiler_params=_params(("arbitrary",), VMEM_LIMIT_BIG),
    )(parts, w, m, v)


def _ada_grad_adamw(c_all, dmod_all, w, m, v):
    r, c = w.shape
    tr = 256
    nb = c_all.shape[0]

    def body(c_ref, dm_ref, w_ref, m_ref, v_ref, g_ref, d_ref, m2_ref, v2_ref):
        cv = c_ref[...]
        g = _dot_tn(cv * _sigmoid(cv), dm_ref[...])
        g_ref[...] = g
        d_ref[...], m2_ref[...], v2_ref[...] = _adamw_math(w_ref[...], g, m_ref[...], v_ref[...])

    blk = pl.BlockSpec((tr, c), lambda i: (i, 0))
    return pl.pallas_call(
        body, name="ada_grad_adamw", grid=(r // tr,),
        in_specs=[pl.BlockSpec((nb, tr), lambda i: (0, i)), pl.BlockSpec((nb, c), lambda i: (0, 0)),
                  blk, blk, blk],
        out_specs=[blk] * 4, out_shape=[SDS((r, c), F32)] * 4,
        compiler_params=_params(("arbitrary",)),
    )(c_all, dmod_all, w, m, v)


_SMALL = [("b_ada", 6144), ("pre_w_mix", 1024), ("attn_sinks", 128), ("attn_out_w", 512), ("lb_table", 1024),
          ("hg_norm_w", 128), ("post_w_mix", 1024), ("pre_w_mlp", 1024), ("post_w_mlp", 1024)]


def _pack_small(vals):
    out = []
    for name, width in _SMALL:
        f = vals[name].reshape(-1).astype(F32)
        out.append(jnp.pad(f, (0, width - f.shape[0])))
    return jnp.concatenate(out).reshape(1, -1)


def _adamw_small(parts, given):
    names = [n for n, _ in _SMALL]
    flat_in = [a for n in names for a in given[n]]

    def body(*refs):
        p_ref = refs[0]
        in_refs = refs[1:1 + 3 * len(names)]
        out_refs = refs[1 + 3 * len(names):]
        g = p_ref[0]
        for s in range(1, N_DEV):
            g = g + p_ref[s]
        off = 0
        for i, (name, width) in enumerate(_SMALL):
            w_ref, m_ref, v_ref = in_refs[3 * i:3 * i + 3]
            rows, cols = w_ref.shape
            for r in range(rows):
                gr = g[:, off + r * cols:off + (r + 1) * cols]
                res = (gr,) + _adamw_math(w_ref[r:r + 1, :], gr, m_ref[r:r + 1, :], v_ref[r:r + 1, :])
                for o_ref, val in zip(out_refs[4 * i:4 * i + 4], res):
                    o_ref[r:r + 1, :] = val
            off += width

    out_shape = [SDS(given[n][0].shape, F32) for n in names for _ in range(4)]
    outs = pl.pallas_call(body, name="adamw_small", out_shape=out_shape)(parts, *flat_in)
    return {n: tuple(outs[4 * i:4 * i + 4]) for i, n in enumerate(names)}


def _columns_to_full(g):
    return g.transpose(1, 0, 2).reshape(g.shape[1], -1)


def kernel(x, c, w_ada, b_ada, pre_w_mix, w_in, attn_sinks, attn_out_w, lb_table, hg_norm_w, w_out, post_w_mix, pre_w_mlp, w_up, w_down, post_w_mlp, loss_target, m_w_ada, m_b_ada, m_pre_w_mix, m_w_in, m_attn_sinks, m_attn_out_w, m_lb_table, m_hg_norm_w, m_w_out, m_post_w_mix, m_pre_w_mlp, m_w_up, m_w_down, m_post_w_mlp, v_w_ada, v_b_ada, v_pre_w_mix, v_w_in, v_attn_sinks, v_attn_out_w, v_lb_table, v_hg_norm_w, v_w_out, v_post_w_mix, v_pre_w_mlp, v_w_up, v_w_down, v_post_w_mlp):
    B, T, _ = x.shape
    N = B * T
    me = 4 * lax.axis_index("x") + 2 * lax.axis_index("y") + lax.axis_index("c")
    x2 = x.reshape(N, D_MODEL)
    tgt2 = loss_target.reshape(N, D_MODEL)

    w_in_g, c_g = _exchange("gather_w_in", [_bf(w_in[0]), c], ["gather"] * 2)
    w_in_f = _columns_to_full(w_in_g)
    c_all = c_g.reshape(N_DEV * B, D_MODEL)

    ada_cols = w_ada.shape[2]
    b_mine = lax.dynamic_slice(b_ada, (0, me * ada_cols), (1, ada_cols))
    mod_cols = _ada_mod(c_all, w_ada[0], b_mine)
    (mod_g,) = _exchange("scatter_mod", [mod_cols.reshape(N_DEV, B, ada_cols)], ["a2a"])
    mod = mod_g.transpose(1, 0, 2).reshape(B, 6, D_MODEL)
    mod8 = jnp.pad(mod, ((0, 0), (0, 2), (0, 0)))

    lb_p = jax.nn.softmax(lb_table, axis=0)
    lb = lb_p[1:2]
    tables = _rope_tables(T)

    proj, h1, w_out_g = _in_proj(x2, mod8, pre_w_mix, w_in_f, T, [_bf(w_out[0])], ["gather"])
    proj3 = proj.reshape(B, T, IN_COLS)
    rec_o, rec_g, s_prev, w_up_g, w_down_g = _hgrn_fwd(proj3, lb, hg_norm_w, [_bf(w_up[0]), _bf(w_down[0])],
                                                       ["gather"] * 2)
    attn_o, attn_n, qr, kr = _attn_fwd(proj3, tables, attn_sinks, attn_out_w, [], [])
    w_out_f = w_out_g.reshape(D_MODEL, D_MODEL)
    mix, x1, cat = _mix_out(x2, attn_n.reshape(N, ATT_WIDTH), rec_g.reshape(N, HG_WIDTH), mod8,
                            post_w_mix, w_out_f, T, [], [])
    w_up_f = _columns_to_full(w_up_g)
    w_down_f = w_down_g.reshape(D_FF, D_MODEL)
    up, d, h2 = _mlp_fwd(x1, mod8, pre_w_mlp, w_up_f, w_down_f, T)

    dx1, u, dup, dd, acc_mlp = _mlp_bwd(x1, d, up, tgt2, mod8, pre_w_mlp, post_w_mlp,
                                        w_down_f, w_up_f, T)
    gw_up = _matmul_tn("grad_w_up", h2, dup, D_FF // N_DEV, by_owner="cols")
    gw_down = _matmul_tn("grad_w_down", u, dd, 512, by_owner="rows")
    dan, drg, dmix, acc_mix, q_down, q_up = _mix_bwd(mix, dx1, mod8, post_w_mix, w_out_f, T,
                                                     [gw_down, gw_up], ["pair"] * 2)
    p_down, p_up = _pair_add("pair_add_w_down", gw_down, q_down), _pair_add("pair_add_w_up", gw_up, q_up)
    gw_out = _matmul_tn("grad_w_out", cat, dmix, 512).reshape(N_DEV, D_MODEL // N_DEV, D_MODEL)
    dhq, dhf, dhi, dhg, dlb_p, dgw_p, r_down, r_up = _hgrn_bwd(
        proj3, lb, hg_norm_w, rec_o, s_prev, drg.reshape(B, T, HG_WIDTH), [p_down, p_up], ["chips"] * 2)
    dqkv, dsink_p, daw_p, r_out = _attn_bwd(qr, kr, proj3, attn_o, dan.reshape(B, T, ATT_WIDTH), tables,
                                            attn_sinks, attn_out_w, [gw_out], ["a2a"])
    flat = lambda a: a.reshape(N, a.shape[-1])
    grad_x, dproj, acc_in = _in_bwd(x2, dx1, flat(dqkv), flat(dhq), flat(dhf), flat(dhi), flat(dhg),
                                    mod8, pre_w_mix, w_in_f, T)

    gw_in = _matmul_tn("grad_w_in", h1, dproj, IN_COLS // 2)
    in_cols = w_in.shape[2]
    gw_in = gw_in.reshape(D_MODEL, N_DEV // 2, 2, in_cols).transpose(2, 1, 0, 3)
    (q_in,) = _exchange("pair_w_in", [gw_in], ["pair"])
    p_in = _pair_add("pair_add_w_in", gw_in, q_in)

    dmod = jnp.concatenate([acc_in[:, 0:2], acc_mix[:, 0:1], acc_mlp[:, 0:3]], axis=1)
    dlb = dlb_p[:, 0].sum(0)
    dlb_table = jnp.stack([-dlb, dlb]) * (lb_p[0] * lb_p[1])[None, :]
    small = {
        "b_ada": dmod.sum(0),
        "pre_w_mix": acc_in[:, 2].sum(0),
        "attn_sinks": dsink_p[:, :, 0].sum(0),
        "attn_out_w": daw_p[:, 0].sum(0),
        "lb_table": dlb_table,
        "hg_norm_w": dgw_p[:, 0].reshape(B, HG_HEADS, LANES).sum((0, 1)),
        "post_w_mix": acc_mix[:, 1].sum(0),
        "pre_w_mlp": acc_mlp[:, 3].sum(0),
        "post_w_mlp": acc_mlp[:, 4].sum(0),
    }
    loss_part = acc_mlp[:, 5, 0].sum()
    dmod_blocks = dmod.reshape(B, N_DEV, ada_cols).transpose(1, 0, 2)

    r_in, r_dmod, r_small = _exchange(
        "reduce_grads", [p_in, dmod_blocks, _pack_small(small)], ["chips", "a2a", "gather"])

    res = {}
    res["w_in"] = _reduce_adamw("adamw_w_in", r_in, w_in[0], m_w_in[0], v_w_in[0])
    res["w_out"] = _reduce_adamw("adamw_w_out", r_out, w_out[0], m_w_out[0], v_w_out[0])
    res["w_up"] = _reduce_adamw("adamw_w_up", r_up, w_up[0], m_w_up[0], v_w_up[0])
    res["w_down"] = _reduce_adamw("adamw_w_down", r_down, w_down[0], m_w_down[0], v_w_down[0])
    res["w_ada"] = _ada_grad_adamw(c_all, r_dmod.reshape(N_DEV * B, ada_cols), w_ada[0], m_w_ada[0], v_w_ada[0])

    given = dict(b_ada=(b_ada, m_b_ada, v_b_ada), pre_w_mix=(pre_w_mix, m_pre_w_mix, v_pre_w_mix),
                 attn_sinks=(attn_sinks, m_attn_sinks, v_attn_sinks),
                 attn_out_w=(attn_out_w, m_attn_out_w, v_attn_out_w), lb_table=(lb_table, m_lb_table, v_lb_table),
                 hg_norm_w=(hg_norm_w, m_hg_norm_w, v_hg_norm_w), post_w_mix=(post_w_mix, m_post_w_mix, v_post_w_mix),
                 pre_w_mlp=(pre_w_mlp, m_pre_w_mlp, v_pre_w_mlp), post_w_mlp=(post_w_mlp, m_post_w_mlp, v_post_w_mlp))
    res.update(_adamw_small(r_small, given))

    loss = lax.psum(loss_part, ("x", "y", "c"))
    order = ["w_ada", "b_ada", "pre_w_mix", "w_in", "attn_sinks", "attn_out_w", "lb_table", "hg_norm_w", "w_out",
             "post_w_mix", "pre_w_mlp", "w_up", "w_down", "post_w_mlp"]
    big = {"w_ada", "w_in", "w_out", "w_up", "w_down"}
    outs = [loss, grad_x.reshape(B, T, D_MODEL)]
    for i in range(4):
        for k in order:
            a = res[k][i]
            outs.append(a[None] if k in big else a)
    return tuple(outs)
```

```python
import functools

import jax
import jax.numpy as jnp
from jax import lax
from jax.experimental import pallas as pl
from jax.experimental.pallas import tpu as pltpu

F32 = jnp.float32
BF16 = jnp.bfloat16
SDS = jax.ShapeDtypeStruct

D_MODEL = 1024
ATT_WIDTH = 512
ATT_HEAD_DIM = 64
ATT_KV_HEADS = 2
ATT_GROUP = 4
WINDOW = 128
ROPE_DIM = 16
ROPE_THETA = 500000.0
HG_WIDTH = 512
HG_HEAD_DIM = 128
HG_HEADS = 4
HG_CHUNK = 32
IN_COLS = 2816
D_FF = 4096
EPS = 1e-6
N_DEV = 8

ADAM_LR = 0.001
ADAM_B1 = 0.9
ADAM_B2 = 0.999
ADAM_EPS = 1e-08
ADAM_WD = 0.01
ADAM_STEP = 10

VMEM_LIMIT_BIG = 56 << 20
LANES = 128

MESH = pl.DeviceIdType.MESH
NT_DIMS = (((1,), (1,)), ((), ()))
TN_DIMS = (((0,), (0,)), ((), ()))


def _dot(a, b):
    return jnp.dot(a, b, preferred_element_type=F32)


def _dot_nt(a, b):
    return lax.dot_general(a, b, NT_DIMS, preferred_element_type=F32)


def _dot_tn(a, b):
    return lax.dot_general(a, b, TN_DIMS, preferred_element_type=F32)


def _bf(a):
    return a.astype(BF16)


def _sigmoid(a):
    return 1.0 / (1.0 + jnp.exp(-a))


def _mean_last(a):
    return jnp.mean(a, axis=-1, keepdims=True)


def _sum_rows(a):
    return jnp.sum(a, axis=0, keepdims=True)


def _tri_sum(tri_bf, a):
    a1 = _bf(a)
    r1 = a - a1.astype(F32)
    a2 = _bf(r1)
    a3 = _bf(r1 - a2.astype(F32))
    return _dot(tri_bf, a1) + _dot(tri_bf, a2) + _dot(tri_bf, a3)


def _params(sem=None, vmem=None):
    kw = {}
    if sem is not None:
        kw["dimension_semantics"] = sem
    if vmem is not None:
        kw["vmem_limit_bytes"] = vmem
    return pltpu.CompilerParams(**kw)


ANY_SPEC = pl.BlockSpec(memory_space=pl.ANY)


def _exchange_shapes(srcs, modes):
    out_shape = []
    for s, m in zip(srcs, modes):
        shp = {"gather": (N_DEV,) + tuple(s.shape), "pair": tuple(s.shape[1:])}.get(m, tuple(s.shape))
        out_shape.append(SDS(shp, s.dtype))
    return out_shape


def _exchange_sems(n):
    if n == 0:
        return []
    return [pltpu.SemaphoreType.DMA((n, N_DEV - 1)), pltpu.SemaphoreType.DMA((n, N_DEV - 1)),
            pltpu.SemaphoreType.DMA((n,))]


SIBLING = 1
OTHER_CHIPS = (2, 4, 6)


def _related(k):
    x, y, c = lax.axis_index("x"), lax.axis_index("y"), lax.axis_index("c")
    px, py, pc = x ^ ((k >> 2) & 1), y ^ ((k >> 1) & 1), c ^ (k & 1)
    return (px, py, pc), 4 * px + 2 * py + pc


def _exchange_phases(modes, src_refs, out_refs, send_sems, recv_sems, own_sems):
    _, me = _related(0)
    sib_dev, sib = _related(SIBLING)
    start, middle, end = [], [], []

    def remote(a, i, src, dst, dev):
        return pltpu.make_async_remote_copy(src_ref=src, dst_ref=dst, send_sem=send_sems.at[a, i],
                                            recv_sem=recv_sems.at[a, i], device_id=dev, device_id_type=MESH)

    for a, mode in enumerate(modes):
        out = out_refs[a]
        if mode == "gather":
            src = src_refs[a]
            own = pltpu.make_async_copy(src, out.at[me], own_sems.at[a])
            to_sib = remote(a, 0, src, out.at[me], sib_dev)
            start += [own.start, to_sib.start]
            end += [remote(a, 0, src, out.at[sib], sib_dev).wait_recv, to_sib.wait_send, own.wait]
            for j, k in enumerate(OTHER_CHIPS, start=1):
                dev, peer = _related(k)
                _, peer_sib = _related(k ^ SIBLING)
                send = remote(a, j, src, out.at[me], dev)
                passed = remote(a, 3 + j, out.at[peer], out.at[peer], sib_dev)
                start.append(send.start)
                middle += [remote(a, j, src, out.at[peer], dev).wait_recv, passed.start]
                end += [remote(a, 3 + j, out.at[peer_sib], out.at[peer_sib], sib_dev).wait_recv,
                        send.wait_send, passed.wait_send]
        elif mode == "pair":
            core = lax.axis_index("c")
            for s in range(N_DEV // 2):
                send = remote(a, s, src_refs[a].at[1 - core, s], out.at[s], sib_dev)
                start.append(send.start)
                end += [remote(a, s, src_refs[a].at[1 - core, s], out.at[s], sib_dev).wait_recv, send.wait_send]
        elif mode == "chips":
            chip = me // 2
            own = pltpu.make_async_copy(src_refs[a].at[chip], out.at[chip], own_sems.at[a])
            start.append(own.start)
            end.append(own.wait)
            for j, k in enumerate(OTHER_CHIPS, start=1):
                dev, peer = _related(k)
                send = remote(a, j, src_refs[a].at[peer // 2], out.at[chip], dev)
                start.append(send.start)
                end += [remote(a, j, src_refs[a].at[peer // 2], out.at[peer // 2], dev).wait_recv, send.wait_send]
        else:
            own = pltpu.make_async_copy(src_refs[a].at[me], out.at[me], own_sems.at[a])
            start.append(own.start)
            end.append(own.wait)
            for k in range(1, N_DEV):
                dev, peer = _related(k)
                send = remote(a, k - 1, src_refs[a].at[peer], out.at[me], dev)
                start.append(send.start)
                end += [remote(a, k - 1, src_refs[a].at[peer], out.at[peer], dev).wait_recv, send.wait_send]
    return start, middle, end


def _run(actions):
    for act in actions:
        act()


def _exchange(name, srcs, modes):
    n = len(srcs)

    def body(*refs):
        start, middle, end = _exchange_phases(modes, refs[:n], refs[n:2 * n], *refs[2 * n:])
        _run(start)
        _run(middle)
        _run(end)

    return pl.pallas_call(
        body, name=name, out_shape=_exchange_shapes(srcs, modes),
        in_specs=[ANY_SPEC] * n, out_specs=[ANY_SPEC] * n,
        scratch_shapes=_exchange_sems(n),
    )(*srcs)


def _ride_start(modes, step, steps, src_refs, out_refs, sems):
    if not modes:
        return
    middle_step = steps - 1

    @pl.when(step == 0)
    def _():
        _run(_exchange_phases(modes, src_refs, out_refs, *sems)[0])

    if "gather" in modes:
        @pl.when(step == middle_step)
        def _():
            _run(_exchange_phases(modes, src_refs, out_refs, *sems)[1])


def _ride_wait(modes, step, steps, src_refs, out_refs, sems):
    if not modes:
        return

    @pl.when(step == steps - 1)
    def _():
        _run(_exchange_phases(modes, src_refs, out_refs, *sems)[2])


def _ada_mod(c_all, w_ada, b_ada_mine):
    nb, cols = c_all.shape[0], w_ada.shape[1]

    def body(c_ref, w_ref, b_ref, o_ref):
        cv = c_ref[...]
        ca = cv * _sigmoid(cv)
        o_ref[...] = _dot(ca, w_ref[...]) + b_ref[...]

    return pl.pallas_call(body, name="ada_mod", out_shape=SDS((nb, cols), F32))(c_all, w_ada, b_ada_mine)


def _tile_rows(T):
    return min(256, T)


def _mod_spec(tps):
    return pl.BlockSpec((None, 8, D_MODEL), lambda i: (i // tps, 0, 0))


def _in_proj(x2, mod8, pre_w, w_in_bf, T, ride_srcs, ride_modes):
    N = x2.shape[0]
    TM = _tile_rows(T)
    tps = T // TM
    nr = len(ride_srcs)

    def body(*refs):
        x_ref, mod_ref, pw_ref, w_ref = refs[:4]
        ride_in = refs[4:4 + nr]
        proj_ref, h1_ref = refs[4 + nr:6 + nr]
        ride_out = refs[6 + nr:6 + 2 * nr]
        sems = refs[6 + 2 * nr:]
        _ride_start(ride_modes, pl.program_id(0), N // TM, ride_in, ride_out, sems)
        x = x_ref[...]
        r = lax.rsqrt(_mean_last(x * x) + EPS)
        h = (x * r * pw_ref[...]) * (1.0 + mod_ref[1:2, :]) + mod_ref[0:1, :]
        hb = _bf(h)
        h1_ref[...] = hb
        proj_ref[...] = _dot(hb, w_ref[...])
        _ride_wait(ride_modes, pl.program_id(0), N // TM, ride_in, ride_out, sems)

    return pl.pallas_call(
        body, name="in_proj", grid=(N // TM,),
        in_specs=[pl.BlockSpec((TM, D_MODEL), lambda i: (i, 0)), _mod_spec(tps),
                  pl.BlockSpec((1, D_MODEL), lambda i: (0, 0)),
                  pl.BlockSpec((D_MODEL, IN_COLS), lambda i: (0, 0))] + [ANY_SPEC] * nr,
        out_specs=[pl.BlockSpec((TM, IN_COLS), lambda i: (i, 0)),
                   pl.BlockSpec((TM, D_MODEL), lambda i: (i, 0))] + [ANY_SPEC] * nr,
        out_shape=[SDS((N, IN_COLS), F32), SDS((N, D_MODEL), BF16)] + _exchange_shapes(ride_srcs, ride_modes),
        scratch_shapes=_exchange_sems(nr),
        compiler_params=_params(("arbitrary",), VMEM_LIMIT_BIG),
    )(x2, mod8, pre_w, w_in_bf, *ride_srcs)


def _rope_tables(T):
    half = ROPE_DIM // 2
    inv_freq = ROPE_THETA ** (-jnp.arange(0, ROPE_DIM, 2, dtype=F32) / ROPE_DIM)
    ang = jnp.arange(T, dtype=F32)[:, None] * inv_freq[None, :]
    cos, sin = jnp.cos(ang), jnp.sin(ang)
    ones = jnp.ones((T, ATT_HEAD_DIM - ROPE_DIM), F32)
    zeros = jnp.zeros((T, ATT_HEAD_DIM - ROPE_DIM), F32)
    zh = jnp.zeros((T, half), F32)
    cos64 = jnp.concatenate([cos, cos, ones], axis=1)
    sin_left = jnp.concatenate([-sin, zh, zeros], axis=1)
    sin_right = jnp.concatenate([zh, sin, zeros], axis=1)
    rep = LANES // ATT_HEAD_DIM
    return jnp.tile(cos64, (1, rep)), jnp.tile(sin_left, (1, rep)), jnp.tile(sin_right, (1, rep))


def _rope(xc, cs, sl, sr):
    return xc * cs + pltpu.roll(xc, LANES - 8, 1) * sl + pltpu.roll(xc, 8, 1) * sr


def _rope_t(dy, cs, sl, sr):
    return dy * cs + pltpu.roll(dy * sl, 8, 1) + pltpu.roll(dy * sr, LANES - 8, 1)


ATT_SCALE = ATT_HEAD_DIM ** -0.5
ATT_SPLITS = 4


def _band_masks():
    cols = ATT_GROUP * WINDOW
    j = lax.broadcasted_iota(jnp.int32, (2 * WINDOW, cols), 0)
    i = lax.broadcasted_iota(jnp.int32, (2 * WINDOW, cols), 1) & (WINDOW - 1)
    diff = i + WINDOW - j
    return (diff >= 0) & (diff < WINDOW), j >= WINDOW


def _sink_row(sink_ref, hk):
    return jnp.concatenate(
        [jnp.full((1, WINDOW), sink_ref[0, ATT_GROUP * hk + g], F32) for g in range(ATT_GROUP)], axis=1)


def _softmax_band(qs, kk, mask, sink):
    s = jnp.where(mask, _dot_nt(kk, qs), jnp.finfo(F32).min)
    m = jnp.maximum(jnp.max(s, axis=0, keepdims=True), sink)
    p = jnp.exp(s - m)
    es = jnp.exp(sink - m)
    inv = 1.0 / (jnp.sum(p, axis=0, keepdims=True) + es)
    return p, inv, es


def _stack_heads(parts, hk):
    hs = []
    for g in range(ATT_GROUP):
        h = ATT_GROUP * hk + g
        hs.append(parts[h // 2][:, (h % 2) * ATT_HEAD_DIM:(h % 2 + 1) * ATT_HEAD_DIM])
    return jnp.concatenate(hs, axis=0)


def _attn_fwd(proj3, tables, sinks, attn_w, ride_srcs, ride_modes):
    B, T, _ = proj3.shape
    nb = T // WINDOW
    splits = min(ATT_SPLITS, nb)
    per = nb // splits
    nr = len(ride_srcs)
    cos, sinl, sinr = tables

    def body(*refs):
        q_ref, k_ref, v_ref, cos_ref, sl_ref, sr_ref, sink_ref, aw_ref = refs[:8]
        ride_in = refs[8:8 + nr]
        o_ref, an_ref, qr_ref, kr_ref = refs[8 + nr:12 + nr]
        ride_out = refs[12 + nr:12 + 2 * nr]
        kpad, vpad = refs[12 + 2 * nr:14 + 2 * nr]
        sems = refs[14 + 2 * nr:]
        part = pl.program_id(1)
        step = pl.program_id(0) * splits + part
        _ride_start(ride_modes, step, B * splits, ride_in, ride_out, sems)

        @pl.when(part == 0)
        def _():
            kpad[0:WINDOW, :] = jnp.zeros((WINDOW, LANES), BF16)
            vpad[0:WINDOW, :] = jnp.zeros((WINDOW, LANES), BF16)

        window, current = _band_masks()

        def block(n, carry):
            r0 = pl.multiple_of(n * WINDOW, WINDOW)
            rows = pl.ds(r0, WINDOW)
            nxt = pl.ds(r0 + WINDOW, WINDOW)
            band = pl.ds(r0, 2 * WINDOW)
            cs, sl, sr = cos_ref[rows, :], sl_ref[rows, :], sr_ref[rows, :]
            kb = _bf(_rope(k_ref[rows, :], cs, sl, sr))
            kpad[nxt, :] = kb
            kr_ref[rows, :] = kb
            vpad[nxt, :] = _bf(v_ref[rows, :])
            qparts = []
            for j in range(ATT_WIDTH // LANES):
                qp = _bf(_rope(q_ref[rows, j * LANES:(j + 1) * LANES], cs, sl, sr) * ATT_SCALE)
                qr_ref[rows, j * LANES:(j + 1) * LANES] = qp
                qparts.append(qp)
            mask = window & (current | (n > 0))
            for hk in range(ATT_KV_HEADS):
                lanes = slice(hk * ATT_HEAD_DIM, (hk + 1) * ATT_HEAD_DIM)
                qs = _stack_heads(qparts, hk)
                p, inv, _ = _softmax_band(qs, kpad[band, lanes], mask, _sink_row(sink_ref, hk))
                ot = _dot_tn(vpad[band, lanes], _bf(p)) * inv
                for g in range(ATT_GROUP):
                    h = ATT_GROUP * hk + g
                    o_ref[rows, h * ATT_HEAD_DIM:(h + 1) * ATT_HEAD_DIM] = ot[:, g * WINDOW:(g + 1) * WINDOW].T
            ob = o_ref[rows, :]
            an_ref[rows, :] = _bf(ob * lax.rsqrt(_mean_last(ob * ob) + EPS) * aw_ref[...])
            return carry

        lax.fori_loop(part * per, (part + 1) * per, block, 0)
        _ride_wait(ride_modes, step, B * splits, ride_in, ride_out, sems)

    seq = lambda w, j: pl.BlockSpec((None, T, w), lambda b, s: (b, 0, j))
    full = lambda r, w: pl.BlockSpec((r, w), lambda b, s: (0, 0))
    return pl.pallas_call(
        body, name="attn_fwd", grid=(B, splits),
        in_specs=[seq(ATT_WIDTH, 0), seq(LANES, 4), seq(LANES, 5),
                  full(T, LANES), full(T, LANES), full(T, LANES),
                  pl.BlockSpec(memory_space=pltpu.SMEM), full(1, ATT_WIDTH)] + [ANY_SPEC] * nr,
        out_specs=[seq(ATT_WIDTH, 0), seq(ATT_WIDTH, 0), seq(ATT_WIDTH, 0), seq(LANES, 0)] + [ANY_SPEC] * nr,
        out_shape=[SDS((B, T, ATT_WIDTH), F32), SDS((B, T, ATT_WIDTH), BF16),
                   SDS((B, T, ATT_WIDTH), BF16), SDS((B, T, LANES), BF16)] + _exchange_shapes(ride_srcs, ride_modes),
        scratch_shapes=[pltpu.VMEM((T + WINDOW, LANES), BF16), pltpu.VMEM((T + WINDOW, LANES), BF16)]
        + _exchange_sems(nr),
        compiler_params=_params(("arbitrary", "arbitrary"), VMEM_LIMIT_BIG),
    )(proj3, proj3, proj3, cos, sinl, sinr, sinks, attn_w, *ride_srcs)


HG_GROUP = 8
HG_ROWS = HG_GROUP * HG_CHUNK


HG_STACK = HG_GROUP * HG_HEAD_DIM


def _group_masks():
    r = lax.broadcasted_iota(jnp.int32, (HG_ROWS, HG_ROWS), 0)
    c = lax.broadcasted_iota(jnp.int32, (HG_ROWS, HG_ROWS), 1)
    same = (r // HG_CHUNK) == (c // HG_CHUNK)
    return same & (r >= c), same & (c >= r)


def _row_chunk():
    return lax.broadcasted_iota(jnp.int32, (HG_ROWS, HG_HEAD_DIM), 0) // HG_CHUNK


def _spread(a, row_chunk):
    return jnp.concatenate([jnp.where(row_chunk == c, a, jnp.zeros_like(a)) for c in range(HG_GROUP)], axis=1)


def _pick(r, row_chunk):
    out = jnp.where(row_chunk == 0, r[:, :HG_HEAD_DIM], 0.0)
    for c in range(1, HG_GROUP):
        out = out + jnp.where(row_chunk == c, r[:, c * HG_HEAD_DIM:(c + 1) * HG_HEAD_DIM], 0.0)
    return out


def _lane_block(a, c):
    return a[:, c * HG_HEAD_DIM:(c + 1) * HG_HEAD_DIM]


def _ones_bf(mask):
    return jnp.where(mask, 1.0, 0.0).astype(BF16)


def _chunk_bcast(rows_1x128):
    return jnp.concatenate([jnp.broadcast_to(r, (HG_CHUNK, HG_HEAD_DIM)) for r in rows_1x128], axis=0)


def _hgrn_gates(hq, hf, lb, lower_bf):
    sq = _sigmoid(hq)
    q = hq * sq
    sg = _sigmoid(hf)
    f = lb + (1.0 - lb) * sg
    k = 1.0 - f
    logf = jnp.log(f)
    b = _tri_sum(lower_bf, logf)
    bl = [_sum_rows(logf[_chunk_rows(c), :]) for c in range(HG_GROUP)]
    eb, enb, e2 = jnp.exp(b), jnp.exp(-b), jnp.exp(_chunk_bcast(bl) - b)
    ebl = [jnp.exp(r) for r in bl]
    return dict(sq=sq, sg=sg, f=f, eb=eb, enb=enb, e2=e2, ebl=ebl, qd=q * eb, kd=k * enb, k2=k * e2)


def _hgrn_specs(B, T):
    head = lambda base: pl.BlockSpec((None, T, LANES), lambda b, h: (b, 0, base + h))
    return head


def _chunk_rows(c):
    return slice(c * HG_CHUNK, (c + 1) * HG_CHUNK)


def _loop_groups(ng, group, init):
    if ng % 2:
        return lax.fori_loop(0, ng, group, init)
    return lax.fori_loop(0, ng // 2, lambda i, carry: group(2 * i + 1, group(2 * i, carry)), init)


def _hgrn_fwd(proj3, lb, hg_w, ride_srcs, ride_modes):
    B, T, _ = proj3.shape
    nc = T // HG_CHUNK
    ng = T // HG_ROWS
    nr = len(ride_srcs)
    head = _hgrn_specs(B, T)

    def body(*refs):
        hq_ref, hf_ref, hi_ref, hg_ref, lb_ref, gw_ref = refs[:6]
        ride_in = refs[6:6 + nr]
        o_ref, rg_ref, sp_ref = refs[6 + nr:9 + nr]
        ride_out = refs[9 + nr:9 + 2 * nr]
        sems = refs[9 + 2 * nr:]
        step = pl.program_id(0) * HG_HEADS + pl.program_id(1)
        _ride_start(ride_modes, step, B * HG_HEADS, ride_in, ride_out, sems)

        lo, _ = _group_masks()
        lower_bf = _ones_bf(lo)
        row_chunk = _row_chunk()
        lbv = lb_ref[...]

        def group(gi, s):
            rows = pl.ds(pl.multiple_of(gi * HG_ROWS, HG_ROWS), HG_ROWS)
            gt = _hgrn_gates(hq_ref[rows, :], hf_ref[rows, :], lbv, lower_bf)
            v, qd, kd = _bf(hi_ref[rows, :]), _bf(gt["qd"]), _bf(gt["kd"])
            a = jnp.where(lo, _dot_nt(qd, kd), 0.0)
            kv = _dot_tn(v, _bf(_spread(gt["k2"], row_chunk)))
            before = []
            for c in range(HG_GROUP):
                before.append(s)
                s = s * gt["ebl"][c] + _lane_block(kv, c)
            sp = jnp.concatenate(before, axis=1)
            sp_ref[gi] = sp
            o = _dot(_bf(a), v) + _dot_nt(_bf(_spread(gt["qd"], row_chunk)), _bf(sp))
            o_ref[rows, :] = o
            hg = hg_ref[rows, :]
            rn = o * lax.rsqrt(_mean_last(o * o) + EPS) * gw_ref[...]
            rg_ref[rows, :] = _bf(rn * (hg * _sigmoid(hg)))
            return s

        _loop_groups(ng, group, jnp.zeros((HG_HEAD_DIM, HG_HEAD_DIM), F32))
        _ride_wait(ride_modes, step, B * HG_HEADS, ride_in, ride_out, sems)

    out_head = pl.BlockSpec((None, T, LANES), lambda b, h: (b, 0, h))
    return pl.pallas_call(
        body, name="hgrn_fwd", grid=(B, HG_HEADS),
        in_specs=[head(6), head(10), head(14), head(18),
                  pl.BlockSpec((1, LANES), lambda b, h: (0, h)),
                  pl.BlockSpec((1, LANES), lambda b, h: (0, 0))] + [ANY_SPEC] * nr,
        out_specs=[out_head, out_head,
                   pl.BlockSpec((None, None, ng, HG_HEAD_DIM, HG_STACK), lambda b, h: (b, h, 0, 0, 0))]
        + [ANY_SPEC] * nr,
        out_shape=[SDS((B, T, HG_WIDTH), F32), SDS((B, T, HG_WIDTH), BF16),
                   SDS((B, HG_HEADS, ng, HG_HEAD_DIM, HG_STACK), F32)] + _exchange_shapes(ride_srcs, ride_modes),
        scratch_shapes=_exchange_sems(nr),
        compiler_params=_params(("arbitrary", "arbitrary"), VMEM_LIMIT_BIG),
    )(proj3, proj3, proj3, proj3, lb, hg_w, *ride_srcs)


def _mix_out(x2, attn_n, rec_g, mod8, post_w, w_out_bf, T, ride_srcs, ride_modes):
    N = x2.shape[0]
    TM = _tile_rows(T)
    tps = T // TM
    nr = len(ride_srcs)

    def body(*refs):
        x_ref, an_ref, rg_ref, mod_ref, pw_ref, w_ref = refs[:6]
        ride_in = refs[6:6 + nr]
        mix_ref, x1_ref, cat_ref = refs[6 + nr:9 + nr]
        ride_out = refs[9 + nr:9 + 2 * nr]
        sems = refs[9 + 2 * nr:]
        _ride_start(ride_modes, pl.program_id(0), N // TM, ride_in, ride_out, sems)
        cat = jnp.concatenate([an_ref[...], rg_ref[...]], axis=1)
        cat_ref[...] = cat
        mix = _dot(cat, w_ref[...])
        mix_ref[...] = mix
        r = lax.rsqrt(_mean_last(mix * mix) + EPS)
        x1_ref[...] = x_ref[...] + mod_ref[2:3, :] * (mix * r * pw_ref[...])
        _ride_wait(ride_modes, pl.program_id(0), N // TM, ride_in, ride_out, sems)

    row = lambda w: pl.BlockSpec((TM, w), lambda i: (i, 0))
    return pl.pallas_call(
        body, name="mix_out", grid=(N // TM,),
        in_specs=[row(D_MODEL), row(ATT_WIDTH), row(HG_WIDTH), _mod_spec(tps),
                  pl.BlockSpec((1, D_MODEL), lambda i: (0, 0)),
                  pl.BlockSpec((D_MODEL, D_MODEL), lambda i: (0, 0))] + [ANY_SPEC] * nr,
        out_specs=[row(D_MODEL), row(D_MODEL), row(D_MODEL)] + [ANY_SPEC] * nr,
        out_shape=[SDS((N, D_MODEL), F32), SDS((N, D_MODEL), F32), SDS((N, D_MODEL), BF16)]
        + _exchange_shapes(ride_srcs, ride_modes),
        scratch_shapes=_exchange_sems(nr),
        compiler_params=_params(("arbitrary",), VMEM_LIMIT_BIG),
    )(x2, attn_n, rec_g, mod8, post_w, w_out_bf, *ride_srcs)


def _load_weights_once(pairs, sem):
    @pl.when(pl.program_id(0) == 0)
    def _():
        cps = [pltpu.make_async_copy(src, dst, sem.at[i]) for i, (src, dst) in enumerate(pairs)]
        for cp in cps:
            cp.start()
        for cp in cps:
            cp.wait()


def _mlp_fwd(x1, mod8, pre_w, w_up_bf, w_down_bf, T):
    N = x1.shape[0]
    TM = _tile_rows(T)
    tps = T // TM

    def body(x_ref, mod_ref, pw_ref, wu_hbm, wd_hbm, up_ref, d_ref, h2_ref, wu, wd, sem):
        _load_weights_once([(wu_hbm, wu), (wd_hbm, wd)], sem)
        x = x_ref[...]
        r = lax.rsqrt(_mean_last(x * x) + EPS)
        h = (x * r * pw_ref[...]) * (1.0 + mod_ref[4:5, :]) + mod_ref[3:4, :]
        hb = _bf(h)
        h2_ref[...] = hb
        up = _dot(hb, wu[...])
        up_ref[...] = up
        ru = jnp.maximum(up, 0.0)
        d_ref[...] = _dot(_bf(ru * ru), wd[...])

    row = lambda w: pl.BlockSpec((TM, w), lambda i: (i, 0))
    return pl.pallas_call(
        body, name="mlp_fwd", grid=(N // TM,),
        in_specs=[row(D_MODEL), _mod_spec(tps), pl.BlockSpec((1, D_MODEL), lambda i: (0, 0)),
                  pl.BlockSpec(memory_space=pl.ANY), pl.BlockSpec(memory_space=pl.ANY)],
        out_specs=[row(D_FF), row(D_MODEL), row(D_MODEL)],
        out_shape=[SDS((N, D_FF), F32), SDS((N, D_MODEL), F32), SDS((N, D_MODEL), BF16)],
        scratch_shapes=[pltpu.VMEM((D_MODEL, D_FF), BF16), pltpu.VMEM((D_FF, D_MODEL), BF16),
                        pltpu.SemaphoreType.DMA((2,))],
        compiler_params=_params(("arbitrary",), VMEM_LIMIT_BIG),
    )(x1, mod8, pre_w, w_up_bf, w_down_bf)


def _acc_rows(acc_ref, first, rows):
    @pl.when(first)
    def _():
        acc_ref[...] = jnp.zeros(acc_ref.shape, F32)
    for i, r in enumerate(rows):
        acc_ref[i:i + 1, :] += r


def _mlp_bwd(x1, d, up, tgt, mod8, pre_w, post_w, w_down_bf, w_up_bf, T):
    N = x1.shape[0]
    TM = _tile_rows(T)
    tps = T // TM

    def body(x_ref, d_ref, up_ref, t_ref, mod_ref, pw_ref, qw_ref, wd_hbm, wu_hbm,
             dx_ref, u_ref, dup_ref, dd_ref, acc_ref, wd, wu, sem):
        _load_weights_once([(wd_hbm, wd), (wu_hbm, wu)], sem)
        sh2, sc2, g2 = mod_ref[3:4, :], mod_ref[4:5, :], mod_ref[5:6, :]
        x = x_ref[...]
        r1 = lax.rsqrt(_mean_last(x * x) + EPS)
        xh = x * r1
        n2 = xh * pw_ref[...]
        dv = d_ref[...]
        rd = lax.rsqrt(_mean_last(dv * dv) + EPS)
        dh = dv * rd
        rr = dh * qw_ref[...]
        e = x + g2 * rr - t_ref[...]
        loss = 0.5 * jnp.sum(_sum_rows(e * e), axis=1, keepdims=True) / D_MODEL
        dy = e * (1.0 / D_MODEL)
        dg2 = _sum_rows(dy * rr)
        drr = dy * g2
        dw_post = _sum_rows(drr * dh)
        ddh = drr * qw_ref[...]
        dd = _bf(rd * (ddh - dh * _mean_last(ddh * dh)))
        dd_ref[...] = dd
        ru = jnp.maximum(up_ref[...], 0.0)
        u_ref[...] = _bf(ru * ru)
        dup = _bf(_dot_nt(dd, wd[...]) * (2.0 * ru))
        dup_ref[...] = dup
        dh2 = _dot_nt(dup, wu[...])
        dsh2 = _sum_rows(dh2)
        dsc2 = _sum_rows(dh2 * n2)
        dn2 = dh2 * (1.0 + sc2)
        dw_pre = _sum_rows(dn2 * xh)
        dxh = dn2 * pw_ref[...]
        dx_ref[...] = dy + r1 * (dxh - xh * _mean_last(dxh * xh))
        _acc_rows(acc_ref, pl.program_id(0) % tps == 0,
                  [dsh2, dsc2, dg2, dw_pre, dw_post, jnp.broadcast_to(loss, (1, D_MODEL))])

    row = lambda w: pl.BlockSpec((TM, w), lambda i: (i, 0))
    vec = pl.BlockSpec((1, D_MODEL), lambda i: (0, 0))
    B = N // T
    return pl.pallas_call(
        body, name="mlp_bwd", grid=(N // TM,),
        in_specs=[row(D_MODEL), row(D_MODEL), row(D_FF), row(D_MODEL), _mod_spec(tps), vec, vec,
                  pl.BlockSpec(memory_space=pl.ANY), pl.BlockSpec(memory_space=pl.ANY)],
        out_specs=[row(D_MODEL), row(D_FF), row(D_FF), row(D_MODEL), _mod_spec(tps)],
        out_shape=[SDS((N, D_MODEL), F32), SDS((N, D_FF), BF16), SDS((N, D_FF), BF16),
                   SDS((N, D_MODEL), BF16), SDS((B, 8, D_MODEL), F32)],
        scratch_shapes=[pltpu.VMEM((D_FF, D_MODEL), BF16), pltpu.VMEM((D_MODEL, D_FF), BF16),
                        pltpu.SemaphoreType.DMA((2,))],
        compiler_params=_params(("arbitrary",), VMEM_LIMIT_BIG),
    )(x1, d, up, tgt, mod8, pre_w, post_w, w_down_bf, w_up_bf)


def _mix_bwd(mix, dx1, mod8, post_w, w_out_bf, T, ride_srcs, ride_modes):
    N = mix.shape[0]
    TM = _tile_rows(T)
    tps = T // TM
    nr = len(ride_srcs)

    def body(*refs):
        mix_ref, dx_ref, mod_ref, pw_ref, w_ref = refs[:5]
        ride_in = refs[5:5 + nr]
        dan_ref, drg_ref, dmix_ref, acc_ref = refs[5 + nr:9 + nr]
        ride_out = refs[9 + nr:9 + 2 * nr]
        sems = refs[9 + 2 * nr:]
        _ride_start(ride_modes, pl.program_id(0), N // TM, ride_in, ride_out, sems)
        g1 = mod_ref[2:3, :]
        mix = mix_ref[...]
        dx1 = dx_ref[...]
        rm = lax.rsqrt(_mean_last(mix * mix) + EPS)
        mh = mix * rm
        dg1 = _sum_rows(dx1 * (mh * pw_ref[...]))
        dr = dx1 * g1
        dw_post = _sum_rows(dr * mh)
        dmh = dr * pw_ref[...]
        dmix = _bf(rm * (dmh - mh * _mean_last(dmh * mh)))
        dmix_ref[...] = dmix
        dcat = _dot_nt(dmix, w_ref[...])
        dan_ref[...] = dcat[:, :ATT_WIDTH]
        drg_ref[...] = dcat[:, ATT_WIDTH:]
        _acc_rows(acc_ref, pl.program_id(0) % tps == 0, [dg1, dw_post])
        _ride_wait(ride_modes, pl.program_id(0), N // TM, ride_in, ride_out, sems)

    row = lambda w: pl.BlockSpec((TM, w), lambda i: (i, 0))
    B = N // T
    return pl.pallas_call(
        body, name="mix_bwd", grid=(N // TM,),
        in_specs=[row(D_MODEL), row(D_MODEL), _mod_spec(tps), pl.BlockSpec((1, D_MODEL), lambda i: (0, 0)),
                  pl.BlockSpec((D_MODEL, D_MODEL), lambda i: (0, 0))] + [ANY_SPEC] * nr,
        out_specs=[row(ATT_WIDTH), row(HG_WIDTH), row(D_MODEL), _mod_spec(tps)] + [ANY_SPEC] * nr,
        out_shape=[SDS((N, ATT_WIDTH), F32), SDS((N, HG_WIDTH), F32), SDS((N, D_MODEL), BF16),
                   SDS((B, 8, D_MODEL), F32)] + _exchange_shapes(ride_srcs, ride_modes),
        scratch_shapes=_exchange_sems(nr),
        compiler_params=_params(("arbitrary",), VMEM_LIMIT_BIG),
    )(mix, dx1, mod8, post_w, w_out_bf, *ride_srcs)


def _hgrn_bwd(proj3, lb, hg_w, o, s_prev, drg, ride_srcs, ride_modes):
    B, T, _ = proj3.shape
    nc = T // HG_CHUNK
    ng = T // HG_ROWS
    nr = len(ride_srcs)
    head = _hgrn_specs(B, T)

    def body(*refs):
        hq_ref, hf_ref, hi_ref, hg_ref, lb_ref, gw_ref, o_ref, sp_ref, drg_ref = refs[:9]
        ride_in = refs[9:9 + nr]
        dhq_ref, dhf_ref, dhi_ref, dhg_ref, dlb_ref, dgw_ref = refs[9 + nr:15 + nr]
        ride_out = refs[15 + nr:15 + 2 * nr]
        sems = refs[15 + 2 * nr:]
        step = pl.program_id(0) * HG_HEADS + pl.program_id(1)
        _ride_start(ride_modes, step, B * HG_HEADS, ride_in, ride_out, sems)

        lo, up = _group_masks()
        lower_bf, upper_bf = _ones_bf(lo), _ones_bf(up)
        row_chunk = _row_chunk()
        lbv = lb_ref[...]
        gw = gw_ref[...]

        def group(i, carry):
            dlb, dgw, ds = carry
            gi = ng - 1 - i
            rows = pl.ds(pl.multiple_of(gi * HG_ROWS, HG_ROWS), HG_ROWS)
            hq = hq_ref[rows, :]
            gt = _hgrn_gates(hq, hf_ref[rows, :], lbv, lower_bf)
            sq, sg, qdf, kdf, k2f, ebl = gt["sq"], gt["sg"], gt["qd"], gt["kd"], gt["k2"], gt["ebl"]
            v, qd, kd = _bf(hi_ref[rows, :]), _bf(qdf), _bf(kdf)
            ov = o_ref[rows, :]
            hg = hg_ref[rows, :]
            shg = _sigmoid(hg)
            dr = drg_ref[rows, :]
            ro = lax.rsqrt(_mean_last(ov * ov) + EPS)
            oh = ov * ro
            dhg_ref[rows, :] = dr * (oh * gw) * (shg + hg * shg * (1.0 - shg))
            drn = dr * (hg * shg)
            dgw = dgw + _sum_rows(drn * oh)
            doh = drn * gw
            do = _bf(ro * (doh - oh * _mean_last(doh * oh)))
            a = jnp.where(lo, _dot_nt(qd, kd), 0.0)
            da = _bf(jnp.where(lo, _dot_nt(do, v), 0.0))
            dv = _dot_tn(_bf(a), do)
            dqd = _dot(da, kd)
            dkd = _dot_tn(da, qd)
            sp = sp_ref[gi]
            incr = _dot_tn(do, _bf(_spread(qdf, row_chunk)))
            after = [None] * HG_GROUP
            for c in reversed(range(HG_GROUP)):
                after[c] = ds
                ds = ds * ebl[c] + _lane_block(incr, c)
            dss = jnp.concatenate(after, axis=1)
            dssb = _bf(dss)
            dk2 = _pick(_dot(v, dssb), row_chunk)
            dhi_ref[rows, :] = dv + _dot_nt(_bf(_spread(k2f, row_chunk)), dssb)
            dqd = dqd + _pick(_dot(do, _bf(sp)), row_chunk)
            debl = _sum_rows(dss * sp)
            k2g = dk2 * k2f
            db = dqd * qdf - dkd * kdf - k2g
            dk = dkd * gt["enb"] + dk2 * gt["e2"]
            dbl = _chunk_bcast([_lane_block(debl, c) * ebl[c] + _sum_rows(k2g[_chunk_rows(c), :])
                                for c in range(HG_GROUP)])
            dg = _tri_sum(upper_bf, db) + dbl
            df = dg / gt["f"] - dk
            dhf_ref[rows, :] = df * (1.0 - lbv) * sg * (1.0 - sg)
            dlb = dlb + _sum_rows(df * (1.0 - sg))
            dhq_ref[rows, :] = (dqd * gt["eb"]) * (sq + hq * sq * (1.0 - sq))
            return dlb, dgw, ds

        zero = jnp.zeros((1, LANES), F32)
        dlb, dgw, _ = _loop_groups(ng, group, (zero, zero, jnp.zeros((HG_HEAD_DIM, HG_HEAD_DIM), F32)))
        dlb_ref[...] = jnp.broadcast_to(dlb, (8, LANES))
        dgw_ref[...] = jnp.broadcast_to(dgw, (8, LANES))
        _ride_wait(ride_modes, step, B * HG_HEADS, ride_in, ride_out, sems)

    out_head = pl.BlockSpec((None, T, LANES), lambda b, h: (b, 0, h))
    small = pl.BlockSpec((None, 8, LANES), lambda b, h: (b, 0, h))
    return pl.pallas_call(
        body, name="hgrn_bwd", grid=(B, HG_HEADS),
        in_specs=[head(6), head(10), head(14), head(18),
                  pl.BlockSpec((1, LANES), lambda b, h: (0, h)),
                  pl.BlockSpec((1, LANES), lambda b, h: (0, 0)),
                  out_head,
                  pl.BlockSpec((None, None, ng, HG_HEAD_DIM, HG_STACK), lambda b, h: (b, h, 0, 0, 0)),
                  out_head] + [ANY_SPEC] * nr,
        out_specs=[out_head, out_head, out_head, out_head, small, small] + [ANY_SPEC] * nr,
        out_shape=[SDS((B, T, HG_WIDTH), F32)] * 4 + [SDS((B, 8, HG_WIDTH), F32)] * 2
        + _exchange_shapes(ride_srcs, ride_modes),
        scratch_shapes=_exchange_sems(nr),
        compiler_params=_params(("arbitrary", "arbitrary"), VMEM_LIMIT_BIG),
    )(proj3, proj3, proj3, proj3, lb, hg_w, o, s_prev, drg, *ride_srcs)


def _attn_bwd(qr, kr, proj3, attn_o, dan, tables, sinks, attn_w, ride_srcs, ride_modes):
    B, T, _ = proj3.shape
    nb = T // WINDOW
    splits = min(ATT_SPLITS, nb)
    per = nb // splits
    nr = len(ride_srcs)
    cos, sinl, sinr = tables
    QKV = ATT_WIDTH + 2 * LANES

    def body(*refs):
        qr_ref, kr_ref, v_ref, o_ref, dan_ref, cos_ref, sl_ref, sr_ref, sink_ref, aw_ref = refs[:10]
        ride_in = refs[10:10 + nr]
        dqkv_ref, dsink_ref, daw_ref = refs[10 + nr:13 + nr]
        ride_out = refs[13 + nr:13 + 2 * nr]
        kpad, vpad, dkpad, dvpad, dqb, dsk = refs[13 + 2 * nr:19 + 2 * nr]
        sems = refs[19 + 2 * nr:]
        part = pl.program_id(1)
        step = pl.program_id(0) * splits + part
        _ride_start(ride_modes, step, B * splits, ride_in, ride_out, sems)

        @pl.when(part == 0)
        def _():
            kpad[0:WINDOW, :] = jnp.zeros((WINDOW, LANES), BF16)
            vpad[0:WINDOW, :] = jnp.zeros((WINDOW, LANES), BF16)
            kpad[WINDOW:, :] = kr_ref[...]
            vpad[WINDOW:, :] = _bf(v_ref[...])
            dkpad[...] = jnp.zeros(dkpad.shape, F32)
            dvpad[...] = jnp.zeros(dvpad.shape, F32)
            dsk[...] = jnp.zeros(dsk.shape, F32)
            daw_ref[...] = jnp.zeros(daw_ref.shape, F32)

        window, current = _band_masks()
        aw = aw_ref[...]

        def block(n, daw):
            r0 = pl.multiple_of(n * WINDOW, WINDOW)
            rows = pl.ds(r0, WINDOW)
            band = pl.ds(r0, 2 * WINDOW)
            ob = o_ref[rows, :]
            dn = dan_ref[rows, :]
            ro = lax.rsqrt(_mean_last(ob * ob) + EPS)
            oh = ob * ro
            daw = daw + _sum_rows(dn * oh)
            doh = dn * aw
            do = _bf(ro * (doh - oh * _mean_last(doh * oh)))
            doparts = [do[:, j * LANES:(j + 1) * LANES] for j in range(ATT_WIDTH // LANES)]
            qparts = [qr_ref[rows, j * LANES:(j + 1) * LANES] for j in range(ATT_WIDTH // LANES)]
            mask = window & (current | (n > 0))
            for hk in range(ATT_KV_HEADS):
                lanes = slice(hk * ATT_HEAD_DIM, (hk + 1) * ATT_HEAD_DIM)
                qs = _stack_heads(qparts, hk)
                dos = _stack_heads(doparts, hk)
                kk, vv = kpad[band, lanes], vpad[band, lanes]
                p, inv, es = _softmax_band(qs, kk, mask, _sink_row(sink_ref, hk))
                p = p * inv
                dp = _dot_nt(vv, dos)
                delta = jnp.sum(p * dp, axis=0, keepdims=True)
                ds = _bf(p * (dp - delta))
                sk = (es * inv) * delta
                dqt = _dot_tn(kk, ds) * ATT_SCALE
                dkpad[band, lanes] += _dot(ds, qs)
                dvpad[band, lanes] += _dot(_bf(p), dos)
                for g in range(ATT_GROUP):
                    h = ATT_GROUP * hk + g
                    cols = slice(g * WINDOW, (g + 1) * WINDOW)
                    dqb[:, h * ATT_HEAD_DIM:(h + 1) * ATT_HEAD_DIM] = dqt[:, cols].T
                    dsk[h:h + 1, :] += jnp.broadcast_to(-jnp.sum(sk[:, cols], axis=1, keepdims=True), (1, LANES))
            cs, sl, sr = cos_ref[rows, :], sl_ref[rows, :], sr_ref[rows, :]
            for j in range(ATT_WIDTH // LANES):
                dqkv_ref[rows, j * LANES:(j + 1) * LANES] = _rope_t(dqb[:, j * LANES:(j + 1) * LANES], cs, sl, sr)
            return daw

        daw = lax.fori_loop(part * per, (part + 1) * per, block, jnp.zeros((1, ATT_WIDTH), F32))
        daw_ref[...] += jnp.broadcast_to(daw, (8, ATT_WIDTH))
        dsink_ref[...] = dsk[...]

        def finish(n, carry):
            r0 = pl.multiple_of(n * WINDOW, WINDOW)
            rows = pl.ds(r0, WINDOW)
            nxt = pl.ds(r0 + WINDOW, WINDOW)
            cs, sl, sr = cos_ref[rows, :], sl_ref[rows, :], sr_ref[rows, :]
            dqkv_ref[rows, ATT_WIDTH:ATT_WIDTH + LANES] = _rope_t(dkpad[nxt, :], cs, sl, sr)
            dqkv_ref[rows, ATT_WIDTH + LANES:QKV] = dvpad[nxt, :]
            return carry

        @pl.when(part == splits - 1)
        def _():
            lax.fori_loop(0, nb, finish, 0)

        _ride_wait(ride_modes, step, B * splits, ride_in, ride_out, sems)

    seq = lambda w, j: pl.BlockSpec((None, T, w), lambda b, s: (b, 0, j))
    full = lambda r, w: pl.BlockSpec((r, w), lambda b, s: (0, 0))
    return pl.pallas_call(
        body, name="attn_bwd", grid=(B, splits),
        in_specs=[seq(ATT_WIDTH, 0), seq(LANES, 0), seq(LANES, 5), seq(ATT_WIDTH, 0), seq(ATT_WIDTH, 0),
                  full(T, LANES), full(T, LANES), full(T, LANES),
                  pl.BlockSpec(memory_space=pltpu.SMEM), full(1, ATT_WIDTH)] + [ANY_SPEC] * nr,
        out_specs=[seq(QKV, 0), pl.BlockSpec((None, 8, LANES), lambda b, s: (b, 0, 0)),
                   pl.BlockSpec((None, 8, ATT_WIDTH), lambda b, s: (b, 0, 0))] + [ANY_SPEC] * nr,
        out_shape=[SDS((B, T, QKV), F32), SDS((B, 8, LANES), F32), SDS((B, 8, ATT_WIDTH), F32)]
        + _exchange_shapes(ride_srcs, ride_modes),
        scratch_shapes=[pltpu.VMEM((T + WINDOW, LANES), BF16), pltpu.VMEM((T + WINDOW, LANES), BF16),
                        pltpu.VMEM((T + WINDOW, LANES), F32), pltpu.VMEM((T + WINDOW, LANES), F32),
                        pltpu.VMEM((WINDOW, ATT_WIDTH), F32), pltpu.VMEM((8, LANES), F32)] + _exchange_sems(nr),
        compiler_params=_params(("arbitrary", "arbitrary"), VMEM_LIMIT_BIG),
    )(qr, kr, proj3, attn_o, dan, cos, sinl, sinr, sinks, attn_w, *ride_srcs)


def _in_bwd(x2, dx1, dqkv, dhq, dhf, dhi, dhg, mod8, pre_w, w_in_bf, T):
    N = x2.shape[0]
    TM = _tile_rows(T)
    tps = T // TM
    pieces = [(0, ATT_WIDTH + 2 * LANES), (768, HG_WIDTH), (1280, HG_WIDTH), (1792, HG_WIDTH), (2304, HG_WIDTH)]

    def body(x_ref, dx_ref, p0, p1, p2, p3, p4, mod_ref, pw_ref, w_ref, gx_ref, dproj_ref, acc_ref):
        sc1 = mod_ref[1:2, :]
        dh = jnp.zeros((TM, D_MODEL), F32)
        for ref, (off, width) in zip((p0, p1, p2, p3, p4), pieces):
            pb = _bf(ref[...])
            dproj_ref[:, off:off + width] = pb
            dh = dh + _dot_nt(pb, w_ref[:, off:off + width])
        x = x_ref[...]
        r = lax.rsqrt(_mean_last(x * x) + EPS)
        xh = x * r
        n1 = xh * pw_ref[...]
        dsh1 = _sum_rows(dh)
        dsc1 = _sum_rows(dh * n1)
        dn1 = dh * (1.0 + sc1)
        dw_pre = _sum_rows(dn1 * xh)
        dxh = dn1 * pw_ref[...]
        gx_ref[...] = dx_ref[...] + r * (dxh - xh * _mean_last(dxh * xh))
        _acc_rows(acc_ref, pl.program_id(0) % tps == 0, [dsh1, dsc1, dw_pre])

    row = lambda w: pl.BlockSpec((TM, w), lambda i: (i, 0))
    B = N // T
    return pl.pallas_call(
        body, name="in_bwd", grid=(N // TM,),
        in_specs=[row(D_MODEL), row(D_MODEL), row(768), row(HG_WIDTH), row(HG_WIDTH), row(HG_WIDTH),
                  row(HG_WIDTH), _mod_spec(tps), pl.BlockSpec((1, D_MODEL), lambda i: (0, 0)),
                  pl.BlockSpec((D_MODEL, IN_COLS), lambda i: (0, 0))],
        out_specs=[row(D_MODEL), row(IN_COLS), _mod_spec(tps)],
        out_shape=[SDS((N, D_MODEL), F32), SDS((N, IN_COLS), BF16), SDS((B, 8, D_MODEL), F32)],
        compiler_params=_params(("arbitrary",), VMEM_LIMIT_BIG),
    )(x2, dx1, dqkv, dhq, dhf, dhi, dhg, mod8, pre_w, w_in_bf)


def _matmul_tn(name, a, b, tn, by_owner=None):
    K, M = a.shape
    Nc = b.shape[1]
    tm = min(512, M)

    def body(a_ref, b_ref, o_ref):
        o_ref[...] = _bf(_dot_tn(a_ref[...], b_ref[...]))

    if by_owner == "cols":
        assert tn * N_DEV == Nc
        out_shape = SDS((2, N_DEV // 2, M, tn), BF16)
        out_spec = pl.BlockSpec((None, None, tm, tn), lambda i, j: (j % 2, j // 2, i, 0))
    elif by_owner == "rows":
        assert tm * N_DEV == M
        out_shape = SDS((2, N_DEV // 2, tm, Nc), BF16)
        out_spec = pl.BlockSpec((None, None, tm, tn), lambda i, j: (i % 2, i // 2, 0, j))
    else:
        out_shape = SDS((M, Nc), BF16)
        out_spec = pl.BlockSpec((tm, tn), lambda i, j: (i, j))
    return pl.pallas_call(
        body, name=name, grid=(M // tm, Nc // tn),
        in_specs=[pl.BlockSpec((K, tm), lambda i, j: (0, i)),
                  pl.BlockSpec((K, tn), lambda i, j: (0, j))],
        out_specs=out_spec, out_shape=out_shape,
        compiler_params=_params(("arbitrary", "arbitrary"), VMEM_LIMIT_BIG),
    )(a, b)


def _adamw_math(w, g, m, v):
    m2 = ADAM_B1 * m + (1.0 - ADAM_B1) * g
    v2 = ADAM_B2 * v + (1.0 - ADAM_B2) * (g * g)
    m_hat = m2 / (1.0 - ADAM_B1 ** ADAM_STEP)
    v_hat = v2 / (1.0 - ADAM_B2 ** ADAM_STEP)
    delta = -ADAM_LR * (m_hat / (jnp.sqrt(v_hat) + ADAM_EPS) + ADAM_WD * w)
    return delta, m2, v2


def _pair_add(name, gw, theirs):
    _, chips, r, c = gw.shape
    tr = r
    core = lax.axis_index("c").astype(jnp.int32).reshape(1)

    def body(core_ref, mine_ref, theirs_ref, o_ref):
        o_ref[...] = _bf(mine_ref[...].astype(F32) + theirs_ref[...].astype(F32))

    block = pl.BlockSpec((None, tr, c), lambda s, i, core_ref: (s, i, 0))
    grid_spec = pltpu.PrefetchScalarGridSpec(
        num_scalar_prefetch=1, grid=(chips, r // tr),
        in_specs=[pl.BlockSpec((None, None, tr, c), lambda s, i, core_ref: (core_ref[0], s, i, 0)), block],
        out_specs=block)
    return pl.pallas_call(
        body, name=name, grid_spec=grid_spec, out_shape=SDS((chips, r, c), BF16),
        compiler_params=_params(("arbitrary", "arbitrary")),
    )(core, gw, theirs)


def _reduce_adamw(name, parts, w, m, v):
    r, c = w.shape
    tr = r if r <= 256 else 256
    slots = parts.shape[0]

    def body(p_ref, w_ref, m_ref, v_ref, g_ref, d_ref, m2_ref, v2_ref):
        g = p_ref[0].astype(F32)
        for s in range(1, slots):
            g = g + p_ref[s].astype(F32)
        g_ref[...] = g
        d_ref[...], m2_ref[...], v2_ref[...] = _adamw_math(w_ref[...], g, m_ref[...], v_ref[...])

    blk = pl.BlockSpec((tr, c), lambda i: (i, 0))
    return pl.pallas_call(
        body, name=name, grid=(r // tr,),
        in_specs=[pl.BlockSpec((slots, tr, c), lambda i: (0, i, 0)), blk, blk, blk],
        out_specs=[blk] * 4, out_shape=[SDS((r, c), F32)] * 4,
        compiler_params=_params(("arbitrary",), VMEM_LIMIT_BIG),
    )(parts, w, m, v)


def _ada_grad_adamw(c_all, dmod_all, w, m, v):
    r, c = w.shape
    tr = 256
    nb = c_all.shape[0]

    def body(c_ref, dm_ref, w_ref, m_ref, v_ref, g_ref, d_ref, m2_ref, v2_ref):
        cv = c_ref[...]
        g = _dot_tn(cv * _sigmoid(cv), dm_ref[...])
        g_ref[...] = g
        d_ref[...], m2_ref[...], v2_ref[...] = _adamw_math(w_ref[...], g, m_ref[...], v_ref[...])

    blk = pl.BlockSpec((tr, c), lambda i: (i, 0))
    return pl.pallas_call(
        body, name="ada_grad_adamw", grid=(r // tr,),
        in_specs=[pl.BlockSpec((nb, tr), lambda i: (0, i)), pl.BlockSpec((nb, c), lambda i: (0, 0)),
                  blk, blk, blk],
        out_specs=[blk] * 4, out_shape=[SDS((r, c), F32)] * 4,
        compiler_params=_params(("arbitrary",)),
    )(c_all, dmod_all, w, m, v)


_SMALL = [("b_ada", 6144), ("pre_w_mix", 1024), ("attn_sinks", 128), ("attn_out_w", 512), ("lb_table", 1024),
          ("hg_norm_w", 128), ("post_w_mix", 1024), ("pre_w_mlp", 1024), ("post_w_mlp", 1024)]


def _pack_small(vals, loss_part):
    out = []
    for name, width in _SMALL:
        f = vals[name].reshape(-1).astype(F32)
        out.append(jnp.pad(f, (0, width - f.shape[0])))
    out.append(jnp.broadcast_to(loss_part, (LANES,)))
    return jnp.concatenate(out).reshape(1, -1)


def _adamw_small(parts, given):
    names = [n for n, _ in _SMALL]
    flat_in = [a for n in names for a in given[n]]

    def body(*refs):
        p_ref = refs[0]
        in_refs = refs[1:1 + 3 * len(names)]
        out_refs = refs[1 + 3 * len(names):-1]
        loss_ref = refs[-1]
        g = p_ref[0]
        for s in range(1, N_DEV):
            g = g + p_ref[s]
        off = 0
        for i, (name, width) in enumerate(_SMALL):
            w_ref, m_ref, v_ref = in_refs[3 * i:3 * i + 3]
            rows, cols = w_ref.shape
            for r in range(rows):
                gr = g[:, off + r * cols:off + (r + 1) * cols]
                res = (gr,) + _adamw_math(w_ref[r:r + 1, :], gr, m_ref[r:r + 1, :], v_ref[r:r + 1, :])
                for o_ref, val in zip(out_refs[4 * i:4 * i + 4], res):
                    o_ref[r:r + 1, :] = val
            off += width
        loss_ref[...] = g[:, off:off + LANES]

    out_shape = [SDS(given[n][0].shape, F32) for n in names for _ in range(4)] + [SDS((1, LANES), F32)]
    outs = pl.pallas_call(body, name="adamw_small", out_shape=out_shape)(parts, *flat_in)
    return {n: tuple(outs[4 * i:4 * i + 4]) for i, n in enumerate(names)}, outs[-1][0, 0]


def _columns_to_full(g):
    return g.transpose(1, 0, 2).reshape(g.shape[1], -1)


def kernel(x, c, w_ada, b_ada, pre_w_mix, w_in, attn_sinks, attn_out_w, lb_table, hg_norm_w, w_out, post_w_mix, pre_w_mlp, w_up, w_down, post_w_mlp, loss_target, m_w_ada, m_b_ada, m_pre_w_mix, m_w_in, m_attn_sinks, m_attn_out_w, m_lb_table, m_hg_norm_w, m_w_out, m_post_w_mix, m_pre_w_mlp, m_w_up, m_w_down, m_post_w_mlp, v_w_ada, v_b_ada, v_pre_w_mix, v_w_in, v_attn_sinks, v_attn_out_w, v_lb_table, v_hg_norm_w, v_w_out, v_post_w_mix, v_pre_w_mlp, v_w_up, v_w_down, v_post_w_mlp):
    B, T, _ = x.shape
    N = B * T
    me = 4 * lax.axis_index("x") + 2 * lax.axis_index("y") + lax.axis_index("c")
    x2 = x.reshape(N, D_MODEL)
    tgt2 = loss_target.reshape(N, D_MODEL)

    w_in_g, c_g = _exchange("gather_w_in", [_bf(w_in[0]), c], ["gather"] * 2)
    w_in_f = _columns_to_full(w_in_g)
    c_all = c_g.reshape(N_DEV * B, D_MODEL)

    ada_cols = w_ada.shape[2]
    b_mine = lax.dynamic_slice(b_ada, (0, me * ada_cols), (1, ada_cols))
    mod_cols = _ada_mod(c_all, w_ada[0], b_mine)
    (mod_g,) = _exchange("scatter_mod", [mod_cols.reshape(N_DEV, B, ada_cols)], ["a2a"])
    mod = mod_g.transpose(1, 0, 2).reshape(B, 6, D_MODEL)
    mod8 = jnp.pad(mod, ((0, 0), (0, 2), (0, 0)))

    lb_p = jax.nn.softmax(lb_table, axis=0)
    lb = lb_p[1:2]
    tables = _rope_tables(T)

    proj, h1, w_out_g = _in_proj(x2, mod8, pre_w_mix, w_in_f, T, [_bf(w_out[0])], ["gather"])
    proj3 = proj.reshape(B, T, IN_COLS)
    rec_o, rec_g, s_prev, w_down_g = _hgrn_fwd(proj3, lb, hg_norm_w, [_bf(w_down[0])], ["gather"])
    attn_o, attn_n, qr, kr, w_up_g = _attn_fwd(proj3, tables, attn_sinks, attn_out_w, [_bf(w_up[0])], ["gather"])
    w_out_f = w_out_g.reshape(D_MODEL, D_MODEL)
    mix, x1, cat = _mix_out(x2, attn_n.reshape(N, ATT_WIDTH), rec_g.reshape(N, HG_WIDTH), mod8,
                            post_w_mix, w_out_f, T, [], [])
    w_up_f = _columns_to_full(w_up_g)
    w_down_f = w_down_g.reshape(D_FF, D_MODEL)
    up, d, h2 = _mlp_fwd(x1, mod8, pre_w_mlp, w_up_f, w_down_f, T)

    dx1, u, dup, dd, acc_mlp = _mlp_bwd(x1, d, up, tgt2, mod8, pre_w_mlp, post_w_mlp,
                                        w_down_f, w_up_f, T)
    gw_up = _matmul_tn("grad_w_up", h2, dup, D_FF // N_DEV, by_owner="cols")
    gw_down = _matmul_tn("grad_w_down", u, dd, 512, by_owner="rows")
    dan, drg, dmix, acc_mix, q_down, q_up = _mix_bwd(mix, dx1, mod8, post_w_mix, w_out_f, T,
                                                     [gw_down, gw_up], ["pair"] * 2)
    p_down, p_up = _pair_add("pair_add_w_down", gw_down, q_down), _pair_add("pair_add_w_up", gw_up, q_up)
    gw_out = _matmul_tn("grad_w_out", cat, dmix, 512).reshape(N_DEV, D_MODEL // N_DEV, D_MODEL)
    dhq, dhf, dhi, dhg, dlb_p, dgw_p, r_down, r_up = _hgrn_bwd(
        proj3, lb, hg_norm_w, rec_o, s_prev, drg.reshape(B, T, HG_WIDTH), [p_down, p_up], ["chips"] * 2)
    dqkv, dsink_p, daw_p, r_out = _attn_bwd(qr, kr, proj3, attn_o, dan.reshape(B, T, ATT_WIDTH), tables,
                                            attn_sinks, attn_out_w, [gw_out], ["a2a"])
    flat = lambda a: a.reshape(N, a.shape[-1])
    grad_x, dproj, acc_in = _in_bwd(x2, dx1, flat(dqkv), flat(dhq), flat(dhf), flat(dhi), flat(dhg),
                                    mod8, pre_w_mix, w_in_f, T)

    gw_in = _matmul_tn("grad_w_in", h1, dproj, IN_COLS // 2)
    in_cols = w_in.shape[2]
    gw_in = gw_in.reshape(D_MODEL, N_DEV // 2, 2, in_cols).transpose(2, 1, 0, 3)
    (q_in,) = _exchange("pair_w_in", [gw_in], ["pair"])
    p_in = _pair_add("pair_add_w_in", gw_in, q_in)

    dmod = jnp.concatenate([acc_in[:, 0:2], acc_mix[:, 0:1], acc_mlp[:, 0:3]], axis=1)
    dlb = dlb_p[:, 0].sum(0)
    dlb_table = jnp.stack([-dlb, dlb]) * (lb_p[0] * lb_p[1])[None, :]
    small = {
        "b_ada": dmod.sum(0),
        "pre_w_mix": acc_in[:, 2].sum(0),
        "attn_sinks": dsink_p[:, :, 0].sum(0),
        "attn_out_w": daw_p[:, 0].sum(0),
        "lb_table": dlb_table,
        "hg_norm_w": dgw_p[:, 0].reshape(B, HG_HEADS, LANES).sum((0, 1)),
        "post_w_mix": acc_mix[:, 1].sum(0),
        "pre_w_mlp": acc_mlp[:, 3].sum(0),
        "post_w_mlp": acc_mlp[:, 4].sum(0),
    }
    loss_part = acc_mlp[:, 5, 0].sum()
    dmod_blocks = dmod.reshape(B, N_DEV, ada_cols).transpose(1, 0, 2)

    r_in, r_dmod, r_small = _exchange(
        "reduce_grads", [p_in, dmod_blocks, _pack_small(small, loss_part)], ["chips", "a2a", "gather"])

    res = {}
    res["w_in"] = _reduce_adamw("adamw_w_in", r_in, w_in[0], m_w_in[0], v_w_in[0])
    res["w_out"] = _reduce_adamw("adamw_w_out", r_out, w_out[0], m_w_out[0], v_w_out[0])
    res["w_up"] = _reduce_adamw("adamw_w_up", r_up, w_up[0], m_w_up[0], v_w_up[0])
    res["w_down"] = _reduce_adamw("adamw_w_down", r_down, w_down[0], m_w_down[0], v_w_down[0])
    res["w_ada"] = _ada_grad_adamw(c_all, r_dmod.reshape(N_DEV * B, ada_cols), w_ada[0], m_w_ada[0], v_w_ada[0])

    given = dict(b_ada=(b_ada, m_b_ada, v_b_ada), pre_w_mix=(pre_w_mix, m_pre_w_mix, v_pre_w_mix),
                 attn_sinks=(attn_sinks, m_attn_sinks, v_attn_sinks),
                 attn_out_w=(attn_out_w, m_attn_out_w, v_attn_out_w), lb_table=(lb_table, m_lb_table, v_lb_table),
                 hg_norm_w=(hg_norm_w, m_hg_norm_w, v_hg_norm_w), post_w_mix=(post_w_mix, m_post_w_mix, v_post_w_mix),
                 pre_w_mlp=(pre_w_mlp, m_pre_w_mlp, v_pre_w_mlp), post_w_mlp=(post_w_mlp, m_post_w_mlp, v_post_w_mlp))
    small_res, loss = _adamw_small(r_small, given)
    res.update(small_res)

    order = ["w_ada", "b_ada", "pre_w_mix", "w_in", "attn_sinks", "attn_out_w", "lb_table", "hg_norm_w", "w_out",
             "post_w_mix", "pre_w_mlp", "w_up", "w_down", "post_w_mlp"]
    big = {"w_ada", "w_in", "w_out", "w_up", "w_down"}
    outs = [loss, grad_x.reshape(B, T, D_MODEL)]
    for i in range(4):
        for k in order:
            a = res[k][i]
            outs.append(a[None] if k in big else a)
    return tuple(outs)
```

```python
import functools

import jax
import jax.numpy as jnp
from jax import lax
from jax.experimental import pallas as pl
from jax.experimental.pallas import tpu as pltpu

F32 = jnp.float32
BF16 = jnp.bfloat16
SDS = jax.ShapeDtypeStruct

D_MODEL = 1024
ATT_WIDTH = 512
ATT_HEAD_DIM = 64
ATT_KV_HEADS = 2
ATT_GROUP = 4
WINDOW = 128
ROPE_DIM = 16
ROPE_THETA = 500000.0
HG_WIDTH = 512
HG_HEAD_DIM = 128
HG_HEADS = 4
HG_CHUNK = 32
IN_COLS = 2816
D_FF = 4096
EPS = 1e-6
N_DEV = 8

ADAM_LR = 0.001
ADAM_B1 = 0.9
ADAM_B2 = 0.999
ADAM_EPS = 1e-08
ADAM_WD = 0.01
ADAM_STEP = 10

VMEM_LIMIT_BIG = 56 << 20
LANES = 128

MESH = pl.DeviceIdType.MESH
NT_DIMS = (((1,), (1,)), ((), ()))
TN_DIMS = (((0,), (0,)), ((), ()))


def _dot(a, b):
    return jnp.dot(a, b, preferred_element_type=F32)


def _dot_nt(a, b):
    return lax.dot_general(a, b, NT_DIMS, preferred_element_type=F32)


def _dot_tn(a, b):
    return lax.dot_general(a, b, TN_DIMS, preferred_element_type=F32)


def _bf(a):
    return a.astype(BF16)


def _sigmoid(a):
    return 1.0 / (1.0 + jnp.exp(-a))


def _mean_last(a):
    return jnp.mean(a, axis=-1, keepdims=True)


def _sum_rows(a):
    return jnp.sum(a, axis=0, keepdims=True)


def _tri_sum(tri_bf, a):
    a1 = _bf(a)
    r1 = a - a1.astype(F32)
    a2 = _bf(r1)
    a3 = _bf(r1 - a2.astype(F32))
    return _dot(tri_bf, a1) + _dot(tri_bf, a2) + _dot(tri_bf, a3)


def _loop_pairs(first, count, body, init):
    if count % 2:
        return lax.fori_loop(first, first + count, body, init)
    return lax.fori_loop(0, count // 2, lambda i, c: body(first + 2 * i + 1, body(first + 2 * i, c)), init)


def _params(sem=None, vmem=None):
    kw = {}
    if sem is not None:
        kw["dimension_semantics"] = sem
    if vmem is not None:
        kw["vmem_limit_bytes"] = vmem
    return pltpu.CompilerParams(**kw)


ANY_SPEC = pl.BlockSpec(memory_space=pl.ANY)


def _exchange_shapes(srcs, modes):
    out_shape = []
    for s, m in zip(srcs, modes):
        shp = {"gather": (N_DEV,) + tuple(s.shape), "pair": tuple(s.shape[1:])}.get(m, tuple(s.shape))
        out_shape.append(SDS(shp, s.dtype))
    return out_shape


def _exchange_sems(n):
    if n == 0:
        return []
    return [pltpu.SemaphoreType.DMA((n, N_DEV - 1)), pltpu.SemaphoreType.DMA((n, N_DEV - 1)),
            pltpu.SemaphoreType.DMA((n,))]


SIBLING = 1
OTHER_CHIPS = (2, 4, 6)


def _related(k):
    x, y, c = lax.axis_index("x"), lax.axis_index("y"), lax.axis_index("c")
    px, py, pc = x ^ ((k >> 2) & 1), y ^ ((k >> 1) & 1), c ^ (k & 1)
    return (px, py, pc), 4 * px + 2 * py + pc


def _exchange_phases(modes, src_refs, out_refs, send_sems, recv_sems, own_sems):
    _, me = _related(0)
    sib_dev, sib = _related(SIBLING)
    start, middle, end = [], [], []

    def remote(a, i, src, dst, dev):
        return pltpu.make_async_remote_copy(src_ref=src, dst_ref=dst, send_sem=send_sems.at[a, i],
                                            recv_sem=recv_sems.at[a, i], device_id=dev, device_id_type=MESH)

    for a, mode in enumerate(modes):
        out = out_refs[a]
        if mode == "gather":
            src = src_refs[a]
            own = pltpu.make_async_copy(src, out.at[me], own_sems.at[a])
            to_sib = remote(a, 0, src, out.at[me], sib_dev)
            start += [own.start, to_sib.start]
            end += [remote(a, 0, src, out.at[sib], sib_dev).wait_recv, to_sib.wait_send, own.wait]
            for j, k in enumerate(OTHER_CHIPS, start=1):
                dev, peer = _related(k)
                _, peer_sib = _related(k ^ SIBLING)
                send = remote(a, j, src, out.at[me], dev)
                passed = remote(a, 3 + j, out.at[peer], out.at[peer], sib_dev)
                start.append(send.start)
                middle += [remote(a, j, src, out.at[peer], dev).wait_recv, passed.start]
                end += [remote(a, 3 + j, out.at[peer_sib], out.at[peer_sib], sib_dev).wait_recv,
                        send.wait_send, passed.wait_send]
        elif mode == "pair":
            core = lax.axis_index("c")
            for s in range(N_DEV // 2):
                send = remote(a, s, src_refs[a].at[1 - core, s], out.at[s], sib_dev)
                start.append(send.start)
                end += [remote(a, s, src_refs[a].at[1 - core, s], out.at[s], sib_dev).wait_recv, send.wait_send]
        elif mode == "chips":
            chip = me // 2
            own = pltpu.make_async_copy(src_refs[a].at[chip], out.at[chip], own_sems.at[a])
            start.append(own.start)
            end.append(own.wait)
            for j, k in enumerate(OTHER_CHIPS, start=1):
                dev, peer = _related(k)
                send = remote(a, j, src_refs[a].at[peer // 2], out.at[chip], dev)
                start.append(send.start)
                end += [remote(a, j, src_refs[a].at[peer // 2], out.at[peer // 2], dev).wait_recv, send.wait_send]
        else:
            own = pltpu.make_async_copy(src_refs[a].at[me], out.at[me], own_sems.at[a])
            start.append(own.start)
            end.append(own.wait)
            for k in range(1, N_DEV):
                dev, peer = _related(k)
                send = remote(a, k - 1, src_refs[a].at[peer], out.at[me], dev)
                start.append(send.start)
                end += [remote(a, k - 1, src_refs[a].at[peer], out.at[peer], dev).wait_recv, send.wait_send]
    return start, middle, end


def _run(actions):
    for act in actions:
        act()


def _exchange(name, srcs, modes):
    n = len(srcs)

    def body(*refs):
        start, middle, end = _exchange_phases(modes, refs[:n], refs[n:2 * n], *refs[2 * n:])
        _run(start)
        _run(middle)
        _run(end)

    return pl.pallas_call(
        body, name=name, out_shape=_exchange_shapes(srcs, modes),
        in_specs=[ANY_SPEC] * n, out_specs=[ANY_SPEC] * n,
        scratch_shapes=_exchange_sems(n),
    )(*srcs)


def _ride_start(modes, step, steps, src_refs, out_refs, sems):
    if not modes:
        return
    middle_step = steps - 1

    @pl.when(step == 0)
    def _():
        _run(_exchange_phases(modes, src_refs, out_refs, *sems)[0])

    if "gather" in modes:
        @pl.when(step == middle_step)
        def _():
            _run(_exchange_phases(modes, src_refs, out_refs, *sems)[1])


def _ride_wait(modes, step, steps, src_refs, out_refs, sems):
    if not modes:
        return

    @pl.when(step == steps - 1)
    def _():
        _run(_exchange_phases(modes, src_refs, out_refs, *sems)[2])


def _ada_mod(c_all, w_ada, b_ada_mine):
    nb, cols = c_all.shape[0], w_ada.shape[1]

    def body(c_ref, w_ref, b_ref, o_ref):
        cv = c_ref[...]
        ca = cv * _sigmoid(cv)
        o_ref[...] = _dot(ca, w_ref[...]) + b_ref[...]

    return pl.pallas_call(body, name="ada_mod", out_shape=SDS((nb, cols), F32))(c_all, w_ada, b_ada_mine)


def _tile_rows(T):
    return min(256, T)


def _mod_spec(tps):
    return pl.BlockSpec((None, 8, D_MODEL), lambda i: (i // tps, 0, 0))


def _in_proj(x2, mod8, pre_w, w_in_bf, T, ride_srcs, ride_modes):
    N = x2.shape[0]
    TM = _tile_rows(T)
    tps = T // TM
    nr = len(ride_srcs)

    def body(*refs):
        x_ref, mod_ref, pw_ref, w_ref = refs[:4]
        ride_in = refs[4:4 + nr]
        proj_ref, h1_ref = refs[4 + nr:6 + nr]
        ride_out = refs[6 + nr:6 + 2 * nr]
        sems = refs[6 + 2 * nr:]
        _ride_start(ride_modes, pl.program_id(0), N // TM, ride_in, ride_out, sems)
        x = x_ref[...]
        r = lax.rsqrt(_mean_last(x * x) + EPS)
        h = (x * r * pw_ref[...]) * (1.0 + mod_ref[1:2, :]) + mod_ref[0:1, :]
        hb = _bf(h)
        h1_ref[...] = hb
        proj_ref[...] = _dot(hb, w_ref[...])
        _ride_wait(ride_modes, pl.program_id(0), N // TM, ride_in, ride_out, sems)

    return pl.pallas_call(
        body, name="in_proj", grid=(N // TM,),
        in_specs=[pl.BlockSpec((TM, D_MODEL), lambda i: (i, 0)), _mod_spec(tps),
                  pl.BlockSpec((1, D_MODEL), lambda i: (0, 0)),
                  pl.BlockSpec((D_MODEL, IN_COLS), lambda i: (0, 0))] + [ANY_SPEC] * nr,
        out_specs=[pl.BlockSpec((TM, IN_COLS), lambda i: (i, 0)),
                   pl.BlockSpec((TM, D_MODEL), lambda i: (i, 0))] + [ANY_SPEC] * nr,
        out_shape=[SDS((N, IN_COLS), F32), SDS((N, D_MODEL), BF16)] + _exchange_shapes(ride_srcs, ride_modes),
        scratch_shapes=_exchange_sems(nr),
        compiler_params=_params(("arbitrary",), VMEM_LIMIT_BIG),
    )(x2, mod8, pre_w, w_in_bf, *ride_srcs)


def _rope_tables(T):
    half = ROPE_DIM // 2
    inv_freq = ROPE_THETA ** (-jnp.arange(0, ROPE_DIM, 2, dtype=F32) / ROPE_DIM)
    ang = jnp.arange(T, dtype=F32)[:, None] * inv_freq[None, :]
    cos, sin = jnp.cos(ang), jnp.sin(ang)
    ones = jnp.ones((T, ATT_HEAD_DIM - ROPE_DIM), F32)
    zeros = jnp.zeros((T, ATT_HEAD_DIM - ROPE_DIM), F32)
    zh = jnp.zeros((T, half), F32)
    cos64 = jnp.concatenate([cos, cos, ones], axis=1)
    sin_left = jnp.concatenate([-sin, zh, zeros], axis=1)
    sin_right = jnp.concatenate([zh, sin, zeros], axis=1)
    rep = LANES // ATT_HEAD_DIM
    return jnp.tile(cos64, (1, rep)), jnp.tile(sin_left, (1, rep)), jnp.tile(sin_right, (1, rep))


def _rope(xc, cs, sl, sr):
    return xc * cs + pltpu.roll(xc, LANES - 8, 1) * sl + pltpu.roll(xc, 8, 1) * sr


def _rope_t(dy, cs, sl, sr):
    return dy * cs + pltpu.roll(dy * sl, 8, 1) + pltpu.roll(dy * sr, LANES - 8, 1)


ATT_SCALE = ATT_HEAD_DIM ** -0.5
ATT_SPLITS = 4


def _band_masks():
    cols = ATT_GROUP * WINDOW
    j = lax.broadcasted_iota(jnp.int32, (2 * WINDOW, cols), 0)
    i = lax.broadcasted_iota(jnp.int32, (2 * WINDOW, cols), 1) & (WINDOW - 1)
    diff = i + WINDOW - j
    return (diff >= 0) & (diff < WINDOW), j >= WINDOW


def _sink_row(sink_ref, hk):
    return jnp.concatenate(
        [jnp.full((1, WINDOW), sink_ref[0, ATT_GROUP * hk + g], F32) for g in range(ATT_GROUP)], axis=1)


def _softmax_band(qs, kk, mask, sink):
    s = jnp.where(mask, _dot_nt(kk, qs), jnp.finfo(F32).min)
    m = jnp.maximum(jnp.max(s, axis=0, keepdims=True), sink)
    p = jnp.exp(s - m)
    es = jnp.exp(sink - m)
    inv = 1.0 / (jnp.sum(p, axis=0, keepdims=True) + es)
    return p, inv, es


def _stack_heads(parts, hk):
    hs = []
    for g in range(ATT_GROUP):
        h = ATT_GROUP * hk + g
        hs.append(parts[h // 2][:, (h % 2) * ATT_HEAD_DIM:(h % 2 + 1) * ATT_HEAD_DIM])
    return jnp.concatenate(hs, axis=0)


def _attn_fwd(proj3, tables, sinks, attn_w, ride_srcs, ride_modes):
    B, T, _ = proj3.shape
    nb = T // WINDOW
    splits = min(ATT_SPLITS, nb)
    per = nb // splits
    nr = len(ride_srcs)
    cos, sinl, sinr = tables

    def body(*refs):
        q_ref, k_ref, v_ref, cos_ref, sl_ref, sr_ref, sink_ref, aw_ref = refs[:8]
        ride_in = refs[8:8 + nr]
        o_ref, an_ref, qr_ref, kr_ref = refs[8 + nr:12 + nr]
        ride_out = refs[12 + nr:12 + 2 * nr]
        kpad, vpad = refs[12 + 2 * nr:14 + 2 * nr]
        sems = refs[14 + 2 * nr:]
        part = pl.program_id(1)
        step = pl.program_id(0) * splits + part
        _ride_start(ride_modes, step, B * splits, ride_in, ride_out, sems)

        @pl.when(part == 0)
        def _():
            kpad[0:WINDOW, :] = jnp.zeros((WINDOW, LANES), BF16)
            vpad[0:WINDOW, :] = jnp.zeros((WINDOW, LANES), BF16)

        window, current = _band_masks()

        def block(n, carry):
            r0 = pl.multiple_of(n * WINDOW, WINDOW)
            rows = pl.ds(r0, WINDOW)
            nxt = pl.ds(r0 + WINDOW, WINDOW)
            band = pl.ds(r0, 2 * WINDOW)
            cs, sl, sr = cos_ref[rows, :], sl_ref[rows, :], sr_ref[rows, :]
            kb = _bf(_rope(k_ref[rows, :], cs, sl, sr))
            kpad[nxt, :] = kb
            kr_ref[rows, :] = kb
            vpad[nxt, :] = _bf(v_ref[rows, :])
            qparts = []
            for j in range(ATT_WIDTH // LANES):
                qp = _bf(_rope(q_ref[rows, j * LANES:(j + 1) * LANES], cs, sl, sr) * ATT_SCALE)
                qr_ref[rows, j * LANES:(j + 1) * LANES] = qp
                qparts.append(qp)
            mask = window & (current | (n > 0))
            for hk in range(ATT_KV_HEADS):
                lanes = slice(hk * ATT_HEAD_DIM, (hk + 1) * ATT_HEAD_DIM)
                qs = _stack_heads(qparts, hk)
                p, inv, _ = _softmax_band(qs, kpad[band, lanes], mask, _sink_row(sink_ref, hk))
                ot = _dot_tn(vpad[band, lanes], _bf(p)) * inv
                for g in range(ATT_GROUP):
                    h = ATT_GROUP * hk + g
                    o_ref[rows, h * ATT_HEAD_DIM:(h + 1) * ATT_HEAD_DIM] = ot[:, g * WINDOW:(g + 1) * WINDOW].T
            ob = o_ref[rows, :]
            an_ref[rows, :] = _bf(ob * lax.rsqrt(_mean_last(ob * ob) + EPS) * aw_ref[...])
            return carry

        _loop_pairs(part * per, per, block, 0)
        _ride_wait(ride_modes, step, B * splits, ride_in, ride_out, sems)

    seq = lambda w, j: pl.BlockSpec((None, T, w), lambda b, s: (b, 0, j))
    full = lambda r, w: pl.BlockSpec((r, w), lambda b, s: (0, 0))
    return pl.pallas_call(
        body, name="attn_fwd", grid=(B, splits),
        in_specs=[seq(ATT_WIDTH, 0), seq(LANES, 4), seq(LANES, 5),
                  full(T, LANES), full(T, LANES), full(T, LANES),
                  pl.BlockSpec(memory_space=pltpu.SMEM), full(1, ATT_WIDTH)] + [ANY_SPEC] * nr,
        out_specs=[seq(ATT_WIDTH, 0), seq(ATT_WIDTH, 0), seq(ATT_WIDTH, 0), seq(LANES, 0)] + [ANY_SPEC] * nr,
        out_shape=[SDS((B, T, ATT_WIDTH), F32), SDS((B, T, ATT_WIDTH), BF16),
                   SDS((B, T, ATT_WIDTH), BF16), SDS((B, T, LANES), BF16)] + _exchange_shapes(ride_srcs, ride_modes),
        scratch_shapes=[pltpu.VMEM((T + WINDOW, LANES), BF16), pltpu.VMEM((T + WINDOW, LANES), BF16)]
        + _exchange_sems(nr),
        compiler_params=_params(("arbitrary", "arbitrary"), VMEM_LIMIT_BIG),
    )(proj3, proj3, proj3, cos, sinl, sinr, sinks, attn_w, *ride_srcs)


HG_GROUP = 8
HG_ROWS = HG_GROUP * HG_CHUNK


HG_STACK = HG_GROUP * HG_HEAD_DIM


def _group_masks():
    r = lax.broadcasted_iota(jnp.int32, (HG_ROWS, HG_ROWS), 0)
    c = lax.broadcasted_iota(jnp.int32, (HG_ROWS, HG_ROWS), 1)
    same = (r // HG_CHUNK) == (c // HG_CHUNK)
    return same & (r >= c), same & (c >= r)


def _row_chunk():
    return lax.broadcasted_iota(jnp.int32, (HG_ROWS, HG_HEAD_DIM), 0) // HG_CHUNK


def _spread(a, row_chunk):
    return jnp.concatenate([jnp.where(row_chunk == c, a, jnp.zeros_like(a)) for c in range(HG_GROUP)], axis=1)


def _pick(r, row_chunk):
    out = jnp.where(row_chunk == 0, r[:, :HG_HEAD_DIM], 0.0)
    for c in range(1, HG_GROUP):
        out = out + jnp.where(row_chunk == c, r[:, c * HG_HEAD_DIM:(c + 1) * HG_HEAD_DIM], 0.0)
    return out


def _lane_block(a, c):
    return a[:, c * HG_HEAD_DIM:(c + 1) * HG_HEAD_DIM]


def _ones_bf(mask):
    return jnp.where(mask, 1.0, 0.0).astype(BF16)


def _chunk_bcast(rows_1x128):
    return jnp.concatenate([jnp.broadcast_to(r, (HG_CHUNK, HG_HEAD_DIM)) for r in rows_1x128], axis=0)


def _hgrn_gates(hq, hf, lb, lower_bf):
    sq = _sigmoid(hq)
    q = hq * sq
    sg = _sigmoid(hf)
    f = lb + (1.0 - lb) * sg
    k = 1.0 - f
    logf = jnp.log(f)
    b = _tri_sum(lower_bf, logf)
    bl = [_sum_rows(logf[_chunk_rows(c), :]) for c in range(HG_GROUP)]
    eb, enb, e2 = jnp.exp(b), jnp.exp(-b), jnp.exp(_chunk_bcast(bl) - b)
    ebl = [jnp.exp(r) for r in bl]
    return dict(sq=sq, sg=sg, f=f, eb=eb, enb=enb, e2=e2, ebl=ebl, qd=q * eb, kd=k * enb, k2=k * e2)


def _hgrn_specs(B, T):
    head = lambda base: pl.BlockSpec((None, T, LANES), lambda b, h: (b, 0, base + h))
    return head


def _chunk_rows(c):
    return slice(c * HG_CHUNK, (c + 1) * HG_CHUNK)


def _loop_groups(ng, group, init):
    return _loop_pairs(0, ng, group, init)


def _hgrn_fwd(proj3, lb, hg_w, ride_srcs, ride_modes):
    B, T, _ = proj3.shape
    nc = T // HG_CHUNK
    ng = T // HG_ROWS
    nr = len(ride_srcs)
    head = _hgrn_specs(B, T)

    def body(*refs):
        hq_ref, hf_ref, hi_ref, hg_ref, lb_ref, gw_ref = refs[:6]
        ride_in = refs[6:6 + nr]
        o_ref, rg_ref, sp_ref = refs[6 + nr:9 + nr]
        ride_out = refs[9 + nr:9 + 2 * nr]
        sems = refs[9 + 2 * nr:]
        step = pl.program_id(0) * HG_HEADS + pl.program_id(1)
        _ride_start(ride_modes, step, B * HG_HEADS, ride_in, ride_out, sems)

        lo, _ = _group_masks()
        lower_bf = _ones_bf(lo)
        row_chunk = _row_chunk()
        lbv = lb_ref[...]

        def group(gi, s):
            rows = pl.ds(pl.multiple_of(gi * HG_ROWS, HG_ROWS), HG_ROWS)
            gt = _hgrn_gates(hq_ref[rows, :], hf_ref[rows, :], lbv, lower_bf)
            v, qd, kd = _bf(hi_ref[rows, :]), _bf(gt["qd"]), _bf(gt["kd"])
            a = jnp.where(lo, _dot_nt(qd, kd), 0.0)
            kv = _dot_tn(v, _bf(_spread(gt["k2"], row_chunk)))
            before = []
            for c in range(HG_GROUP):
                before.append(s)
                s = s * gt["ebl"][c] + _lane_block(kv, c)
            sp = jnp.concatenate(before, axis=1)
            sp_ref[gi] = sp
            o = _dot(_bf(a), v) + _dot_nt(_bf(_spread(gt["qd"], row_chunk)), _bf(sp))
            o_ref[rows, :] = o
            hg = hg_ref[rows, :]
            rn = o * lax.rsqrt(_mean_last(o * o) + EPS) * gw_ref[...]
            rg_ref[rows, :] = _bf(rn * (hg * _sigmoid(hg)))
            return s

        _loop_groups(ng, group, jnp.zeros((HG_HEAD_DIM, HG_HEAD_DIM), F32))
        _ride_wait(ride_modes, step, B * HG_HEADS, ride_in, ride_out, sems)

    out_head = pl.BlockSpec((None, T, LANES), lambda b, h: (b, 0, h))
    return pl.pallas_call(
        body, name="hgrn_fwd", grid=(B, HG_HEADS),
        in_specs=[head(6), head(10), head(14), head(18),
                  pl.BlockSpec((1, LANES), lambda b, h: (0, h)),
                  pl.BlockSpec((1, LANES), lambda b, h: (0, 0))] + [ANY_SPEC] * nr,
        out_specs=[out_head, out_head,
                   pl.BlockSpec((None, None, ng, HG_HEAD_DIM, HG_STACK), lambda b, h: (b, h, 0, 0, 0))]
        + [ANY_SPEC] * nr,
        out_shape=[SDS((B, T, HG_WIDTH), F32), SDS((B, T, HG_WIDTH), BF16),
                   SDS((B, HG_HEADS, ng, HG_HEAD_DIM, HG_STACK), F32)] + _exchange_shapes(ride_srcs, ride_modes),
        scratch_shapes=_exchange_sems(nr),
        compiler_params=_params(("arbitrary", "arbitrary"), VMEM_LIMIT_BIG),
    )(proj3, proj3, proj3, proj3, lb, hg_w, *ride_srcs)


def _mix_out(x2, attn_n, rec_g, mod8, post_w, w_out_bf, T, ride_srcs, ride_modes):
    N = x2.shape[0]
    TM = _tile_rows(T)
    tps = T // TM
    nr = len(ride_srcs)

    def body(*refs):
        x_ref, an_ref, rg_ref, mod_ref, pw_ref, w_ref = refs[:6]
        ride_in = refs[6:6 + nr]
        mix_ref, x1_ref, cat_ref = refs[6 + nr:9 + nr]
        ride_out = refs[9 + nr:9 + 2 * nr]
        sems = refs[9 + 2 * nr:]
        _ride_start(ride_modes, pl.program_id(0), N // TM, ride_in, ride_out, sems)
        cat = jnp.concatenate([an_ref[...], rg_ref[...]], axis=1)
        cat_ref[...] = cat
        mix = _dot(cat, w_ref[...])
        mix_ref[...] = mix
        r = lax.rsqrt(_mean_last(mix * mix) + EPS)
        x1_ref[...] = x_ref[...] + mod_ref[2:3, :] * (mix * r * pw_ref[...])
        _ride_wait(ride_modes, pl.program_id(0), N // TM, ride_in, ride_out, sems)

    row = lambda w: pl.BlockSpec((TM, w), lambda i: (i, 0))
    return pl.pallas_call(
        body, name="mix_out", grid=(N // TM,),
        in_specs=[row(D_MODEL), row(ATT_WIDTH), row(HG_WIDTH), _mod_spec(tps),
                  pl.BlockSpec((1, D_MODEL), lambda i: (0, 0)),
                  pl.BlockSpec((D_MODEL, D_MODEL), lambda i: (0, 0))] + [ANY_SPEC] * nr,
        out_specs=[row(D_MODEL), row(D_MODEL), row(D_MODEL)] + [ANY_SPEC] * nr,
        out_shape=[SDS((N, D_MODEL), F32), SDS((N, D_MODEL), F32), SDS((N, D_MODEL), BF16)]
        + _exchange_shapes(ride_srcs, ride_modes),
        scratch_shapes=_exchange_sems(nr),
        compiler_params=_params(("arbitrary",), VMEM_LIMIT_BIG),
    )(x2, attn_n, rec_g, mod8, post_w, w_out_bf, *ride_srcs)


def _load_weights_once(pairs, sem):
    @pl.when(pl.program_id(0) == 0)
    def _():
        cps = [pltpu.make_async_copy(src, dst, sem.at[i]) for i, (src, dst) in enumerate(pairs)]
        for cp in cps:
            cp.start()
        for cp in cps:
            cp.wait()


MLP_HALF = D_MODEL // 2
MLP_PIECES = 2 * N_DEV + 2


def _mlp_weight_pieces(wu_a, wu_b, wd_a, wd_b, wu, wd):
    cols = D_FF // N_DEV
    pairs = []
    for h, half in enumerate((wu_a, wu_b)):
        for j in range(N_DEV):
            pairs.append((half.at[j], wu.at[pl.ds(h * MLP_HALF, MLP_HALF), pl.ds(j * cols, cols)]))
    for h, half in enumerate((wd_a, wd_b)):
        pairs.append((half, wd.at[:, pl.ds(h * MLP_HALF, MLP_HALF)]))
    return pairs


def _mlp_fwd(x1, mod8, pre_w, w_up_halves, w_down_halves, T):
    N = x1.shape[0]
    TM = _tile_rows(T)
    tps = T // TM

    def body(x_ref, mod_ref, pw_ref, wua, wub, wda, wdb, up_ref, d_ref, h2_ref, wu, wd, sem):
        _load_weights_once(_mlp_weight_pieces(wua, wub, wda, wdb, wu, wd), sem)
        x = x_ref[...]
        r = lax.rsqrt(_mean_last(x * x) + EPS)
        h = (x * r * pw_ref[...]) * (1.0 + mod_ref[4:5, :]) + mod_ref[3:4, :]
        hb = _bf(h)
        h2_ref[...] = hb
        up = _dot(hb, wu[...])
        up_ref[...] = up
        ru = jnp.maximum(up, 0.0)
        d_ref[...] = _dot(_bf(ru * ru), wd[...])

    row = lambda w: pl.BlockSpec((TM, w), lambda i: (i, 0))
    return pl.pallas_call(
        body, name="mlp_fwd", grid=(N // TM,),
        in_specs=[row(D_MODEL), _mod_spec(tps), pl.BlockSpec((1, D_MODEL), lambda i: (0, 0))] + [ANY_SPEC] * 4,
        out_specs=[row(D_FF), row(D_MODEL), row(D_MODEL)],
        out_shape=[SDS((N, D_FF), F32), SDS((N, D_MODEL), F32), SDS((N, D_MODEL), BF16)],
        scratch_shapes=[pltpu.VMEM((D_MODEL, D_FF), BF16), pltpu.VMEM((D_FF, D_MODEL), BF16),
                        pltpu.SemaphoreType.DMA((MLP_PIECES,))],
        compiler_params=_params(("arbitrary",), VMEM_LIMIT_BIG),
    )(x1, mod8, pre_w, *w_up_halves, *w_down_halves)


def _acc_rows(acc_ref, first, rows):
    @pl.when(first)
    def _():
        acc_ref[...] = jnp.zeros(acc_ref.shape, F32)
    for i, r in enumerate(rows):
        acc_ref[i:i + 1, :] += r


def _mlp_bwd(x1, d, up, tgt, mod8, pre_w, post_w, w_up_halves, w_down_halves, T):
    N = x1.shape[0]
    TM = _tile_rows(T)
    tps = T // TM

    def body(x_ref, d_ref, up_ref, t_ref, mod_ref, pw_ref, qw_ref, wua, wub, wda, wdb,
             dx_ref, u_ref, dup_ref, dd_ref, acc_ref, wd, wu, sem):
        _load_weights_once(_mlp_weight_pieces(wua, wub, wda, wdb, wu, wd), sem)
        sh2, sc2, g2 = mod_ref[3:4, :], mod_ref[4:5, :], mod_ref[5:6, :]
        x = x_ref[...]
        r1 = lax.rsqrt(_mean_last(x * x) + EPS)
        xh = x * r1
        n2 = xh * pw_ref[...]
        dv = d_ref[...]
        rd = lax.rsqrt(_mean_last(dv * dv) + EPS)
        dh = dv * rd
        rr = dh * qw_ref[...]
        e = x + g2 * rr - t_ref[...]
        loss = 0.5 * jnp.sum(_sum_rows(e * e), axis=1, keepdims=True) / D_MODEL
        dy = e * (1.0 / D_MODEL)
        dg2 = _sum_rows(dy * rr)
        drr = dy * g2
        dw_post = _sum_rows(drr * dh)
        ddh = drr * qw_ref[...]
        dd = _bf(rd * (ddh - dh * _mean_last(ddh * dh)))
        dd_ref[...] = dd
        ru = jnp.maximum(up_ref[...], 0.0)
        u_ref[...] = _bf(ru * ru)
        dup = _bf(_dot_nt(dd, wd[...]) * (2.0 * ru))
        dup_ref[...] = dup
        dh2 = _dot_nt(dup, wu[...])
        dsh2 = _sum_rows(dh2)
        dsc2 = _sum_rows(dh2 * n2)
        dn2 = dh2 * (1.0 + sc2)
        dw_pre = _sum_rows(dn2 * xh)
        dxh = dn2 * pw_ref[...]
        dx_ref[...] = dy + r1 * (dxh - xh * _mean_last(dxh * xh))
        _acc_rows(acc_ref, pl.program_id(0) % tps == 0,
                  [dsh2, dsc2, dg2, dw_pre, dw_post, jnp.broadcast_to(loss, (1, D_MODEL))])

    row = lambda w: pl.BlockSpec((TM, w), lambda i: (i, 0))
    vec = pl.BlockSpec((1, D_MODEL), lambda i: (0, 0))
    B = N // T
    return pl.pallas_call(
        body, name="mlp_bwd", grid=(N // TM,),
        in_specs=[row(D_MODEL), row(D_MODEL), row(D_FF), row(D_MODEL), _mod_spec(tps), vec, vec] + [ANY_SPEC] * 4,
        out_specs=[row(D_MODEL), row(D_FF), row(D_FF), row(D_MODEL), _mod_spec(tps)],
        out_shape=[SDS((N, D_MODEL), F32), SDS((N, D_FF), BF16), SDS((N, D_FF), BF16),
                   SDS((N, D_MODEL), BF16), SDS((B, 8, D_MODEL), F32)],
        scratch_shapes=[pltpu.VMEM((D_FF, D_MODEL), BF16), pltpu.VMEM((D_MODEL, D_FF), BF16),
                        pltpu.SemaphoreType.DMA((MLP_PIECES,))],
        compiler_params=_params(("arbitrary",), VMEM_LIMIT_BIG),
    )(x1, d, up, tgt, mod8, pre_w, post_w, *w_up_halves, *w_down_halves)


def _mix_bwd(mix, dx1, mod8, post_w, w_out_bf, T, ride_srcs, ride_modes):
    N = mix.shape[0]
    TM = _tile_rows(T)
    tps = T // TM
    nr = len(ride_srcs)

    def body(*refs):
        mix_ref, dx_ref, mod_ref, pw_ref, w_ref = refs[:5]
        ride_in = refs[5:5 + nr]
        dan_ref, drg_ref, dmix_ref, acc_ref = refs[5 + nr:9 + nr]
        ride_out = refs[9 + nr:9 + 2 * nr]
        sems = refs[9 + 2 * nr:]
        _ride_start(ride_modes, pl.program_id(0), N // TM, ride_in, ride_out, sems)
        g1 = mod_ref[2:3, :]
        mix = mix_ref[...]
        dx1 = dx_ref[...]
        rm = lax.rsqrt(_mean_last(mix * mix) + EPS)
        mh = mix * rm
        dg1 = _sum_rows(dx1 * (mh * pw_ref[...]))
        dr = dx1 * g1
        dw_post = _sum_rows(dr * mh)
        dmh = dr * pw_ref[...]
        dmix = _bf(rm * (dmh - mh * _mean_last(dmh * mh)))
        dmix_ref[...] = dmix
        dcat = _dot_nt(dmix, w_ref[...])
        dan_ref[...] = dcat[:, :ATT_WIDTH]
        drg_ref[...] = dcat[:, ATT_WIDTH:]
        _acc_rows(acc_ref, pl.program_id(0) % tps == 0, [dg1, dw_post])
        _ride_wait(ride_modes, pl.program_id(0), N // TM, ride_in, ride_out, sems)

    row = lambda w: pl.BlockSpec((TM, w), lambda i: (i, 0))
    B = N // T
    return pl.pallas_call(
        body, name="mix_bwd", grid=(N // TM,),
        in_specs=[row(D_MODEL), row(D_MODEL), _mod_spec(tps), pl.BlockSpec((1, D_MODEL), lambda i: (0, 0)),
                  pl.BlockSpec((D_MODEL, D_MODEL), lambda i: (0, 0))] + [ANY_SPEC] * nr,
        out_specs=[row(ATT_WIDTH), row(HG_WIDTH), row(D_MODEL), _mod_spec(tps)] + [ANY_SPEC] * nr,
        out_shape=[SDS((N, ATT_WIDTH), F32), SDS((N, HG_WIDTH), F32), SDS((N, D_MODEL), BF16),
                   SDS((B, 8, D_MODEL), F32)] + _exchange_shapes(ride_srcs, ride_modes),
        scratch_shapes=_exchange_sems(nr),
        compiler_params=_params(("arbitrary",), VMEM_LIMIT_BIG),
    )(mix, dx1, mod8, post_w, w_out_bf, *ride_srcs)


def _hgrn_bwd(proj3, lb, hg_w, o, s_prev, drg, ride_srcs, ride_modes):
    B, T, _ = proj3.shape
    nc = T // HG_CHUNK
    ng = T // HG_ROWS
    nr = len(ride_srcs)
    head = _hgrn_specs(B, T)

    def body(*refs):
        hq_ref, hf_ref, hi_ref, hg_ref, lb_ref, gw_ref, o_ref, sp_ref, drg_ref = refs[:9]
        ride_in = refs[9:9 + nr]
        dhq_ref, dhf_ref, dhi_ref, dhg_ref, dlb_ref, dgw_ref = refs[9 + nr:15 + nr]
        ride_out = refs[15 + nr:15 + 2 * nr]
        sems = refs[15 + 2 * nr:]
        step = pl.program_id(0) * HG_HEADS + pl.program_id(1)
        _ride_start(ride_modes, step, B * HG_HEADS, ride_in, ride_out, sems)

        lo, up = _group_masks()
        lower_bf, upper_bf = _ones_bf(lo), _ones_bf(up)
        row_chunk = _row_chunk()
        lbv = lb_ref[...]
        gw = gw_ref[...]

        def group(i, carry):
            dlb, dgw, ds = carry
            gi = ng - 1 - i
            rows = pl.ds(pl.multiple_of(gi * HG_ROWS, HG_ROWS), HG_ROWS)
            hq = hq_ref[rows, :]
            gt = _hgrn_gates(hq, hf_ref[rows, :], lbv, lower_bf)
            sq, sg, qdf, kdf, k2f, ebl = gt["sq"], gt["sg"], gt["qd"], gt["kd"], gt["k2"], gt["ebl"]
            v, qd, kd = _bf(hi_ref[rows, :]), _bf(qdf), _bf(kdf)
            ov = o_ref[rows, :]
            hg = hg_ref[rows, :]
            shg = _sigmoid(hg)
            dr = drg_ref[rows, :]
            ro = lax.rsqrt(_mean_last(ov * ov) + EPS)
            oh = ov * ro
            dhg_ref[rows, :] = dr * (oh * gw) * (shg + hg * shg * (1.0 - shg))
            drn = dr * (hg * shg)
            dgw = dgw + _sum_rows(drn * oh)
            doh = drn * gw
            do = _bf(ro * (doh - oh * _mean_last(doh * oh)))
            a = jnp.where(lo, _dot_nt(qd, kd), 0.0)
            da = _bf(jnp.where(lo, _dot_nt(do, v), 0.0))
            dv = _dot_tn(_bf(a), do)
            dqd = _dot(da, kd)
            dkd = _dot_tn(da, qd)
            sp = sp_ref[gi]
            incr = _dot_tn(do, _bf(_spread(qdf, row_chunk)))
            after = [None] * HG_GROUP
            for c in reversed(range(HG_GROUP)):
                after[c] = ds
                ds = ds * ebl[c] + _lane_block(incr, c)
            dss = jnp.concatenate(after, axis=1)
            dssb = _bf(dss)
            dk2 = _pick(_dot(v, dssb), row_chunk)
            dhi_ref[rows, :] = dv + _dot_nt(_bf(_spread(k2f, row_chunk)), dssb)
            dqd = dqd + _pick(_dot(do, _bf(sp)), row_chunk)
            debl = _sum_rows(dss * sp)
            k2g = dk2 * k2f
            db = dqd * qdf - dkd * kdf - k2g
            dk = dkd * gt["enb"] + dk2 * gt["e2"]
            dbl = _chunk_bcast([_lane_block(debl, c) * ebl[c] + _sum_rows(k2g[_chunk_rows(c), :])
                                for c in range(HG_GROUP)])
            dg = _tri_sum(upper_bf, db) + dbl
            df = dg / gt["f"] - dk
            dhf_ref[rows, :] = df * (1.0 - lbv) * sg * (1.0 - sg)
            dlb = dlb + _sum_rows(df * (1.0 - sg))
            dhq_ref[rows, :] = (dqd * gt["eb"]) * (sq + hq * sq * (1.0 - sq))
            return dlb, dgw, ds

        zero = jnp.zeros((1, LANES), F32)
        dlb, dgw, _ = _loop_groups(ng, group, (zero, zero, jnp.zeros((HG_HEAD_DIM, HG_HEAD_DIM), F32)))
        dlb_ref[...] = jnp.broadcast_to(dlb, (8, LANES))
        dgw_ref[...] = jnp.broadcast_to(dgw, (8, LANES))
        _ride_wait(ride_modes, step, B * HG_HEADS, ride_in, ride_out, sems)

    out_head = pl.BlockSpec((None, T, LANES), lambda b, h: (b, 0, h))
    small = pl.BlockSpec((None, 8, LANES), lambda b, h: (b, 0, h))
    return pl.pallas_call(
        body, name="hgrn_bwd", grid=(B, HG_HEADS),
        in_specs=[head(6), head(10), head(14), head(18),
                  pl.BlockSpec((1, LANES), lambda b, h: (0, h)),
                  pl.BlockSpec((1, LANES), lambda b, h: (0, 0)),
                  out_head,
                  pl.BlockSpec((None, None, ng, HG_HEAD_DIM, HG_STACK), lambda b, h: (b, h, 0, 0, 0)),
                  out_head] + [ANY_SPEC] * nr,
        out_specs=[out_head, out_head, out_head, out_head, small, small] + [ANY_SPEC] * nr,
        out_shape=[SDS((B, T, HG_WIDTH), F32)] * 4 + [SDS((B, 8, HG_WIDTH), F32)] * 2
        + _exchange_shapes(ride_srcs, ride_modes),
        scratch_shapes=_exchange_sems(nr),
        compiler_params=_params(("arbitrary", "arbitrary"), VMEM_LIMIT_BIG),
    )(proj3, proj3, proj3, proj3, lb, hg_w, o, s_prev, drg, *ride_srcs)


def _attn_bwd(qr, kr, proj3, attn_o, dan, tables, sinks, attn_w, ride_srcs, ride_modes):
    B, T, _ = proj3.shape
    nb = T // WINDOW
    splits = min(ATT_SPLITS, nb)
    per = nb // splits
    nr = len(ride_srcs)
    cos, sinl, sinr = tables
    QKV = ATT_WIDTH + 2 * LANES

    def body(*refs):
        qr_ref, kr_ref, v_ref, o_ref, dan_ref, cos_ref, sl_ref, sr_ref, sink_ref, aw_ref = refs[:10]
        ride_in = refs[10:10 + nr]
        dqkv_ref, dsink_ref, daw_ref = refs[10 + nr:13 + nr]
        ride_out = refs[13 + nr:13 + 2 * nr]
        kpad, vpad, dkpad, dvpad, dqb, dsk = refs[13 + 2 * nr:19 + 2 * nr]
        sems = refs[19 + 2 * nr:]
        part = pl.program_id(1)
        step = pl.program_id(0) * splits + part
        _ride_start(ride_modes, step, B * splits, ride_in, ride_out, sems)

        @pl.when(part == 0)
        def _():
            kpad[0:WINDOW, :] = jnp.zeros((WINDOW, LANES), BF16)
            vpad[0:WINDOW, :] = jnp.zeros((WINDOW, LANES), BF16)
            kpad[WINDOW:, :] = kr_ref[...]
            vpad[WINDOW:, :] = _bf(v_ref[...])
            dkpad[...] = jnp.zeros(dkpad.shape, F32)
            dvpad[...] = jnp.zeros(dvpad.shape, F32)
            dsk[...] = jnp.zeros(dsk.shape, F32)
            daw_ref[...] = jnp.zeros(daw_ref.shape, F32)

        window, current = _band_masks()
        aw = aw_ref[...]

        def block(n, daw):
            r0 = pl.multiple_of(n * WINDOW, WINDOW)
            rows = pl.ds(r0, WINDOW)
            band = pl.ds(r0, 2 * WINDOW)
            ob = o_ref[rows, :]
            dn = dan_ref[rows, :]
            ro = lax.rsqrt(_mean_last(ob * ob) + EPS)
            oh = ob * ro
            daw = daw + _sum_rows(dn * oh)
            doh = dn * aw
            do = _bf(ro * (doh - oh * _mean_last(doh * oh)))
            doparts = [do[:, j * LANES:(j + 1) * LANES] for j in range(ATT_WIDTH // LANES)]
            qparts = [qr_ref[rows, j * LANES:(j + 1) * LANES] for j in range(ATT_WIDTH // LANES)]
            mask = window & (current | (n > 0))
            for hk in range(ATT_KV_HEADS):
                lanes = slice(hk * ATT_HEAD_DIM, (hk + 1) * ATT_HEAD_DIM)
                qs = _stack_heads(qparts, hk)
                dos = _stack_heads(doparts, hk)
                kk, vv = kpad[band, lanes], vpad[band, lanes]
                p, inv, es = _softmax_band(qs, kk, mask, _sink_row(sink_ref, hk))
                p = p * inv
                dp = _dot_nt(vv, dos)
                delta = jnp.sum(p * dp, axis=0, keepdims=True)
                ds = _bf(p * (dp - delta))
                sk = (es * inv) * delta
                dqt = _dot_tn(kk, ds) * ATT_SCALE
                dkpad[band, lanes] += _dot(ds, qs)
                dvpad[band, lanes] += _dot(_bf(p), dos)
                for g in range(ATT_GROUP):
                    h = ATT_GROUP * hk + g
                    cols = slice(g * WINDOW, (g + 1) * WINDOW)
                    dqb[:, h * ATT_HEAD_DIM:(h + 1) * ATT_HEAD_DIM] = dqt[:, cols].T
                    dsk[h:h + 1, :] += jnp.broadcast_to(-jnp.sum(sk[:, cols], axis=1, keepdims=True), (1, LANES))
            cs, sl, sr = cos_ref[rows, :], sl_ref[rows, :], sr_ref[rows, :]
            for j in range(ATT_WIDTH // LANES):
                dqkv_ref[rows, j * LANES:(j + 1) * LANES] = _rope_t(dqb[:, j * LANES:(j + 1) * LANES], cs, sl, sr)
            return daw

        daw = _loop_pairs(part * per, per, block, jnp.zeros((1, ATT_WIDTH), F32))
        daw_ref[...] += jnp.broadcast_to(daw, (8, ATT_WIDTH))
        dsink_ref[...] = dsk[...]

        def finish(n, carry):
            r0 = pl.multiple_of(n * WINDOW, WINDOW)
            rows = pl.ds(r0, WINDOW)
            nxt = pl.ds(r0 + WINDOW, WINDOW)
            cs, sl, sr = cos_ref[rows, :], sl_ref[rows, :], sr_ref[rows, :]
            dqkv_ref[rows, ATT_WIDTH:ATT_WIDTH + LANES] = _rope_t(dkpad[nxt, :], cs, sl, sr)
            dqkv_ref[rows, ATT_WIDTH + LANES:QKV] = dvpad[nxt, :]
            return carry

        @pl.when(part == splits - 1)
        def _():
            lax.fori_loop(0, nb, finish, 0)

        _ride_wait(ride_modes, step, B * splits, ride_in, ride_out, sems)

    seq = lambda w, j: pl.BlockSpec((None, T, w), lambda b, s: (b, 0, j))
    full = lambda r, w: pl.BlockSpec((r, w), lambda b, s: (0, 0))
    return pl.pallas_call(
        body, name="attn_bwd", grid=(B, splits),
        in_specs=[seq(ATT_WIDTH, 0), seq(LANES, 0), seq(LANES, 5), seq(ATT_WIDTH, 0), seq(ATT_WIDTH, 0),
                  full(T, LANES), full(T, LANES), full(T, LANES),
                  pl.BlockSpec(memory_space=pltpu.SMEM), full(1, ATT_WIDTH)] + [ANY_SPEC] * nr,
        out_specs=[seq(QKV, 0), pl.BlockSpec((None, 8, LANES), lambda b, s: (b, 0, 0)),
                   pl.BlockSpec((None, 8, ATT_WIDTH), lambda b, s: (b, 0, 0))] + [ANY_SPEC] * nr,
        out_shape=[SDS((B, T, QKV), F32), SDS((B, 8, LANES), F32), SDS((B, 8, ATT_WIDTH), F32)]
        + _exchange_shapes(ride_srcs, ride_modes),
        scratch_shapes=[pltpu.VMEM((T + WINDOW, LANES), BF16), pltpu.VMEM((T + WINDOW, LANES), BF16),
                        pltpu.VMEM((T + WINDOW, LANES), F32), pltpu.VMEM((T + WINDOW, LANES), F32),
                        pltpu.VMEM((WINDOW, ATT_WIDTH), F32), pltpu.VMEM((8, LANES), F32)] + _exchange_sems(nr),
        compiler_params=_params(("arbitrary", "arbitrary"), VMEM_LIMIT_BIG),
    )(qr, kr, proj3, attn_o, dan, cos, sinl, sinr, sinks, attn_w, *ride_srcs)


def _in_bwd(x2, dx1, dqkv, dhq, dhf, dhi, dhg, mod8, pre_w, w_in_bf, T, ride_srcs, ride_modes):
    N = x2.shape[0]
    TM = _tile_rows(T)
    tps = T // TM
    nr = len(ride_srcs)
    pieces = [(0, ATT_WIDTH + 2 * LANES), (768, HG_WIDTH), (1280, HG_WIDTH), (1792, HG_WIDTH), (2304, HG_WIDTH)]

    def body(*refs):
        x_ref, dx_ref, p0, p1, p2, p3, p4, mod_ref, pw_ref, w_ref = refs[:10]
        ride_in = refs[10:10 + nr]
        gx_ref, dproj_ref, acc_ref = refs[10 + nr:13 + nr]
        ride_out = refs[13 + nr:13 + 2 * nr]
        sems = refs[13 + 2 * nr:]
        _ride_start(ride_modes, pl.program_id(0), N // TM, ride_in, ride_out, sems)
        sc1 = mod_ref[1:2, :]
        dh = jnp.zeros((TM, D_MODEL), F32)
        for ref, (off, width) in zip((p0, p1, p2, p3, p4), pieces):
            pb = _bf(ref[...])
            dproj_ref[:, off:off + width] = pb
            dh = dh + _dot_nt(pb, w_ref[:, off:off + width])
        x = x_ref[...]
        r = lax.rsqrt(_mean_last(x * x) + EPS)
        xh = x * r
        n1 = xh * pw_ref[...]
        dsh1 = _sum_rows(dh)
        dsc1 = _sum_rows(dh * n1)
        dn1 = dh * (1.0 + sc1)
        dw_pre = _sum_rows(dn1 * xh)
        dxh = dn1 * pw_ref[...]
        gx_ref[...] = dx_ref[...] + r * (dxh - xh * _mean_last(dxh * xh))
        _acc_rows(acc_ref, pl.program_id(0) % tps == 0, [dsh1, dsc1, dw_pre])
        _ride_wait(ride_modes, pl.program_id(0), N // TM, ride_in, ride_out, sems)

    row = lambda w: pl.BlockSpec((TM, w), lambda i: (i, 0))
    B = N // T
    return pl.pallas_call(
        body, name="in_bwd", grid=(N // TM,),
        in_specs=[row(D_MODEL), row(D_MODEL), row(768), row(HG_WIDTH), row(HG_WIDTH), row(HG_WIDTH),
                  row(HG_WIDTH), _mod_spec(tps), pl.BlockSpec((1, D_MODEL), lambda i: (0, 0)),
                  pl.BlockSpec((D_MODEL, IN_COLS), lambda i: (0, 0))] + [ANY_SPEC] * nr,
        out_specs=[row(D_MODEL), row(IN_COLS), _mod_spec(tps)] + [ANY_SPEC] * nr,
        out_shape=[SDS((N, D_MODEL), F32), SDS((N, IN_COLS), BF16), SDS((B, 8, D_MODEL), F32)]
        + _exchange_shapes(ride_srcs, ride_modes),
        scratch_shapes=_exchange_sems(nr),
        compiler_params=_params(("arbitrary",), VMEM_LIMIT_BIG),
    )(x2, dx1, dqkv, dhq, dhf, dhi, dhg, mod8, pre_w, w_in_bf, *ride_srcs)


def _matmul_tn(name, a, b, tn, by_owner=None):
    K, M = a.shape
    Nc = b.shape[1]
    tm = min(512, M)

    def body(a_ref, b_ref, o_ref):
        o_ref[...] = _bf(_dot_tn(a_ref[...], b_ref[...]))

    if by_owner == "cols":
        assert tn * N_DEV == Nc
        out_shape = SDS((2, N_DEV // 2, M, tn), BF16)
        out_spec = pl.BlockSpec((None, None, tm, tn), lambda i, j: (j % 2, j // 2, i, 0))
    elif by_owner == "rows":
        assert tm * N_DEV == M
        out_shape = SDS((2, N_DEV // 2, tm, Nc), BF16)
        out_spec = pl.BlockSpec((None, None, tm, tn), lambda i, j: (i % 2, i // 2, 0, j))
    else:
        out_shape = SDS((M, Nc), BF16)
        out_spec = pl.BlockSpec((tm, tn), lambda i, j: (i, j))
    return pl.pallas_call(
        body, name=name, grid=(M // tm, Nc // tn),
        in_specs=[pl.BlockSpec((K, tm), lambda i, j: (0, i)),
                  pl.BlockSpec((K, tn), lambda i, j: (0, j))],
        out_specs=out_spec, out_shape=out_shape,
        compiler_params=_params(("arbitrary", "arbitrary"), VMEM_LIMIT_BIG),
    )(a, b)


def _adamw_math(w, g, m, v):
    m2 = ADAM_B1 * m + (1.0 - ADAM_B1) * g
    v2 = ADAM_B2 * v + (1.0 - ADAM_B2) * (g * g)
    m_hat = m2 / (1.0 - ADAM_B1 ** ADAM_STEP)
    v_hat = v2 / (1.0 - ADAM_B2 ** ADAM_STEP)
    delta = -ADAM_LR * (m_hat / (jnp.sqrt(v_hat) + ADAM_EPS) + ADAM_WD * w)
    return delta, m2, v2


def _pair_add(name, gw, theirs):
    _, chips, r, c = gw.shape
    tr = r
    core = lax.axis_index("c").astype(jnp.int32).reshape(1)

    def body(core_ref, mine_ref, theirs_ref, o_ref):
        o_ref[...] = _bf(mine_ref[...].astype(F32) + theirs_ref[...].astype(F32))

    block = pl.BlockSpec((None, tr, c), lambda s, i, core_ref: (s, i, 0))
    grid_spec = pltpu.PrefetchScalarGridSpec(
        num_scalar_prefetch=1, grid=(chips, r // tr),
        in_specs=[pl.BlockSpec((None, None, tr, c), lambda s, i, core_ref: (core_ref[0], s, i, 0)), block],
        out_specs=block)
    return pl.pallas_call(
        body, name=name, grid_spec=grid_spec, out_shape=SDS((chips, r, c), BF16),
        compiler_params=_params(("arbitrary", "arbitrary")),
    )(core, gw, theirs)


def _reduce_adamw(name, parts, w, m, v):
    r, c = w.shape
    tr = r if r <= 256 else 256
    slots = parts.shape[0]

    def body(p_ref, w_ref, m_ref, v_ref, g_ref, d_ref, m2_ref, v2_ref):
        g = p_ref[0].astype(F32)
        for s in range(1, slots):
            g = g + p_ref[s].astype(F32)
        g_ref[...] = g
        d_ref[...], m2_ref[...], v2_ref[...] = _adamw_math(w_ref[...], g, m_ref[...], v_ref[...])

    blk = pl.BlockSpec((tr, c), lambda i: (i, 0))
    return pl.pallas_call(
        body, name=name, grid=(r // tr,),
        in_specs=[pl.BlockSpec((slots, tr, c), lambda i: (0, i, 0)), blk, blk, blk],
        out_specs=[blk] * 4, out_shape=[SDS((r, c), F32)] * 4,
        compiler_params=_params(("arbitrary",), VMEM_LIMIT_BIG),
    )(parts, w, m, v)


def _ada_grad_adamw(c_all, dmod_all, w, m, v):
    r, c = w.shape
    tr = 256
    nb = c_all.shape[0]

    def body(c_ref, dm_ref, w_ref, m_ref, v_ref, g_ref, d_ref, m2_ref, v2_ref):
        cv = c_ref[...]
        g = _dot_tn(cv * _sigmoid(cv), dm_ref[...])
        g_ref[...] = g
        d_ref[...], m2_ref[...], v2_ref[...] = _adamw_math(w_ref[...], g, m_ref[...], v_ref[...])

    blk = pl.BlockSpec((tr, c), lambda i: (i, 0))
    return pl.pallas_call(
        body, name="ada_grad_adamw", grid=(r // tr,),
        in_specs=[pl.BlockSpec((nb, tr), lambda i: (0, i)), pl.BlockSpec((nb, c), lambda i: (0, 0)),
                  blk, blk, blk],
        out_specs=[blk] * 4, out_shape=[SDS((r, c), F32)] * 4,
        compiler_params=_params(("arbitrary",)),
    )(c_all, dmod_all, w, m, v)


_SMALL = [("b_ada", 6144), ("pre_w_mix", 1024), ("attn_sinks", 128), ("attn_out_w", 512), ("lb_table", 1024),
          ("hg_norm_w", 128), ("post_w_mix", 1024), ("pre_w_mlp", 1024), ("post_w_mlp", 1024)]


def _pack_small(vals, loss_part):
    out = []
    for name, width in _SMALL:
        f = vals[name].reshape(-1).astype(F32)
        out.append(jnp.pad(f, (0, width - f.shape[0])))
    out.append(jnp.broadcast_to(loss_part, (LANES,)))
    return jnp.concatenate(out).reshape(1, -1)


def _adamw_small(parts, given):
    names = [n for n, _ in _SMALL]
    flat_in = [a for n in names for a in given[n]]

    def body(*refs):
        p_ref = refs[0]
        in_refs = refs[1:1 + 3 * len(names)]
        out_refs = refs[1 + 3 * len(names):-1]
        loss_ref = refs[-1]
        g = p_ref[0]
        for s in range(1, N_DEV):
            g = g + p_ref[s]
        off = 0
        for i, (name, width) in enumerate(_SMALL):
            w_ref, m_ref, v_ref = in_refs[3 * i:3 * i + 3]
            rows, cols = w_ref.shape
            for r in range(rows):
                gr = g[:, off + r * cols:off + (r + 1) * cols]
                res = (gr,) + _adamw_math(w_ref[r:r + 1, :], gr, m_ref[r:r + 1, :], v_ref[r:r + 1, :])
                for o_ref, val in zip(out_refs[4 * i:4 * i + 4], res):
                    o_ref[r:r + 1, :] = val
            off += width
        loss_ref[...] = g[:, off:off + LANES]

    out_shape = [SDS(given[n][0].shape, F32) for n in names for _ in range(4)] + [SDS((1, LANES), F32)]
    outs = pl.pallas_call(body, name="adamw_small", out_shape=out_shape)(parts, *flat_in)
    return {n: tuple(outs[4 * i:4 * i + 4]) for i, n in enumerate(names)}, outs[-1][0, 0]


def _columns_to_full(g):
    return g.transpose(1, 0, 2).reshape(g.shape[1], -1)


def kernel(x, c, w_ada, b_ada, pre_w_mix, w_in, attn_sinks, attn_out_w, lb_table, hg_norm_w, w_out, post_w_mix, pre_w_mlp, w_up, w_down, post_w_mlp, loss_target, m_w_ada, m_b_ada, m_pre_w_mix, m_w_in, m_attn_sinks, m_attn_out_w, m_lb_table, m_hg_norm_w, m_w_out, m_post_w_mix, m_pre_w_mlp, m_w_up, m_w_down, m_post_w_mlp, v_w_ada, v_b_ada, v_pre_w_mix, v_w_in, v_attn_sinks, v_attn_out_w, v_lb_table, v_hg_norm_w, v_w_out, v_post_w_mix, v_pre_w_mlp, v_w_up, v_w_down, v_post_w_mlp):
    B, T, _ = x.shape
    N = B * T
    me = 4 * lax.axis_index("x") + 2 * lax.axis_index("y") + lax.axis_index("c")
    x2 = x.reshape(N, D_MODEL)
    tgt2 = loss_target.reshape(N, D_MODEL)

    w_in_g, c_g = _exchange("gather_w_in", [_bf(w_in[0]), c], ["gather"] * 2)
    w_in_f = _columns_to_full(w_in_g)
    c_all = c_g.reshape(N_DEV * B, D_MODEL)

    ada_cols = w_ada.shape[2]
    b_mine = lax.dynamic_slice(b_ada, (0, me * ada_cols), (1, ada_cols))
    mod_cols = _ada_mod(c_all, w_ada[0], b_mine)
    (mod_g,) = _exchange("scatter_mod", [mod_cols.reshape(N_DEV, B, ada_cols)], ["a2a"])
    mod = mod_g.transpose(1, 0, 2).reshape(B, 6, D_MODEL)
    mod8 = jnp.pad(mod, ((0, 0), (0, 2), (0, 0)))

    lb_p = jax.nn.softmax(lb_table, axis=0)
    lb = lb_p[1:2]
    tables = _rope_tables(T)

    w_up_b, w_down_b = _bf(w_up[0]), _bf(w_down[0])
    proj, h1, w_out_g, w_up_g0 = _in_proj(x2, mod8, pre_w_mix, w_in_f, T,
                                          [_bf(w_out[0]), w_up_b[:MLP_HALF]], ["gather"] * 2)
    proj3 = proj.reshape(B, T, IN_COLS)
    rec_o, rec_g, s_prev, w_up_g1, w_down_g0 = _hgrn_fwd(proj3, lb, hg_norm_w,
                                                         [w_up_b[MLP_HALF:], w_down_b[:, :MLP_HALF]], ["gather"] * 2)
    attn_o, attn_n, qr, kr, w_down_g1 = _attn_fwd(proj3, tables, attn_sinks, attn_out_w,
                                                  [w_down_b[:, MLP_HALF:]], ["gather"])
    w_out_f = w_out_g.reshape(D_MODEL, D_MODEL)
    mix, x1, cat = _mix_out(x2, attn_n.reshape(N, ATT_WIDTH), rec_g.reshape(N, HG_WIDTH), mod8,
                            post_w_mix, w_out_f, T, [], [])
    w_up_halves = [w_up_g0, w_up_g1]
    w_down_halves = [w_down_g0.reshape(D_FF, MLP_HALF), w_down_g1.reshape(D_FF, MLP_HALF)]
    up, d, h2 = _mlp_fwd(x1, mod8, pre_w_mlp, w_up_halves, w_down_halves, T)

    dx1, u, dup, dd, acc_mlp = _mlp_bwd(x1, d, up, tgt2, mod8, pre_w_mlp, post_w_mlp,
                                        w_up_halves, w_down_halves, T)
    gw_up = _matmul_tn("grad_w_up", h2, dup, D_FF // N_DEV, by_owner="cols")
    gw_down = _matmul_tn("grad_w_down", u, dd, 512, by_owner="rows")
    dan, drg, dmix, acc_mix, q_down, q_up = _mix_bwd(mix, dx1, mod8, post_w_mix, w_out_f, T,
                                                     [gw_down, gw_up], ["pair"] * 2)
    p_down, p_up = _pair_add("pair_add_w_down", gw_down, q_down), _pair_add("pair_add_w_up", gw_up, q_up)
    gw_out = _matmul_tn("grad_w_out", cat, dmix, 512).reshape(N_DEV, D_MODEL // N_DEV, D_MODEL)
    dhq, dhf, dhi, dhg, dlb_p, dgw_p, r_down = _hgrn_bwd(
        proj3, lb, hg_norm_w, rec_o, s_prev, drg.reshape(B, T, HG_WIDTH), [p_down], ["chips"])
    dqkv, dsink_p, daw_p, r_up = _attn_bwd(qr, kr, proj3, attn_o, dan.reshape(B, T, ATT_WIDTH), tables,
                                           attn_sinks, attn_out_w, [p_up], ["chips"])
    flat = lambda a: a.reshape(N, a.shape[-1])
    grad_x, dproj, acc_in, r_out = _in_bwd(x2, dx1, flat(dqkv), flat(dhq), flat(dhf), flat(dhi), flat(dhg),
                                           mod8, pre_w_mix, w_in_f, T, [gw_out], ["a2a"])

    gw_in = _matmul_tn("grad_w_in", h1, dproj, IN_COLS // 2)
    in_cols = w_in.shape[2]
    gw_in = gw_in.reshape(D_MODEL, N_DEV // 2, 2, in_cols).transpose(2, 1, 0, 3)
    (q_in,) = _exchange("pair_w_in", [gw_in], ["pair"])
    p_in = _pair_add("pair_add_w_in", gw_in, q_in)

    dmod = jnp.concatenate([acc_in[:, 0:2], acc_mix[:, 0:1], acc_mlp[:, 0:3]], axis=1)
    dlb = dlb_p[:, 0].sum(0)
    dlb_table = jnp.stack([-dlb, dlb]) * (lb_p[0] * lb_p[1])[None, :]
    small = {
        "b_ada": dmod.sum(0),
        "pre_w_mix": acc_in[:, 2].sum(0),
        "attn_sinks": dsink_p[:, :, 0].sum(0),
        "attn_out_w": daw_p[:, 0].sum(0),
        "lb_table": dlb_table,
        "hg_norm_w": dgw_p[:, 0].reshape(B, HG_HEADS, LANES).sum((0, 1)),
        "post_w_mix": acc_mix[:, 1].sum(0),
        "pre_w_mlp": acc_mlp[:, 3].sum(0),
        "post_w_mlp": acc_mlp[:, 4].sum(0),
    }
    loss_part = acc_mlp[:, 5, 0].sum()
    dmod_blocks = dmod.reshape(B, N_DEV, ada_cols).transpose(1, 0, 2)

    r_in, r_dmod, r_small = _exchange(
        "reduce_grads", [p_in, dmod_blocks, _pack_small(small, loss_part)], ["chips", "a2a", "gather"])

    res = {}
    res["w_in"] = _reduce_adamw("adamw_w_in", r_in, w_in[0], m_w_in[0], v_w_in[0])
    res["w_out"] = _reduce_adamw("adamw_w_out", r_out, w_out[0], m_w_out[0], v_w_out[0])
    res["w_up"] = _reduce_adamw("adamw_w_up", r_up, w_up[0], m_w_up[0], v_w_up[0])
    res["w_down"] = _reduce_adamw("adamw_w_down", r_down, w_down[0], m_w_down[0], v_w_down[0])
    res["w_ada"] = _ada_grad_adamw(c_all, r_dmod.reshape(N_DEV * B, ada_cols), w_ada[0], m_w_ada[0], v_w_ada[0])

    given = dict(b_ada=(b_ada, m_b_ada, v_b_ada), pre_w_mix=(pre_w_mix, m_pre_w_mix, v_pre_w_mix),
                 attn_sinks=(attn_sinks, m_attn_sinks, v_attn_sinks),
                 attn_out_w=(attn_out_w, m_attn_out_w, v_attn_out_w), lb_table=(lb_table, m_lb_table, v_lb_table),
                 hg_norm_w=(hg_norm_w, m_hg_norm_w, v_hg_norm_w), post_w_mix=(post_w_mix, m_post_w_mix, v_post_w_mix),
                 pre_w_mlp=(pre_w_mlp, m_pre_w_mlp, v_pre_w_mlp), post_w_mlp=(post_w_mlp, m_post_w_mlp, v_post_w_mlp))
    small_res, loss = _adamw_small(r_small, given)
    res.update(small_res)

    order = ["w_ada", "b_ada", "pre_w_mix", "w_in", "attn_sinks", "attn_out_w", "lb_table", "hg_norm_w", "w_out",
             "post_w_mix", "pre_w_mlp", "w_up", "w_down", "post_w_mlp"]
    big = {"w_ada", "w_in", "w_out", "w_up", "w_down"}
    outs = [loss, grad_x.reshape(B, T, D_MODEL)]
    for i in range(4):
        for k in order:
            a = res[k][i]
            outs.append(a[None] if k in big else a)
    return tuple(outs)
```

```python
import functools

import jax
import jax.numpy as jnp
from jax import lax
from jax.experimental import pallas as pl
from jax.experimental.pallas import tpu as pltpu

F32 = jnp.float32
BF16 = jnp.bfloat16
SDS = jax.ShapeDtypeStruct

D_MODEL = 1024
ATT_WIDTH = 512
ATT_HEAD_DIM = 64
ATT_KV_HEADS = 2
ATT_GROUP = 4
WINDOW = 128
ROPE_DIM = 16
ROPE_THETA = 500000.0
HG_WIDTH = 512
HG_HEAD_DIM = 128
HG_HEADS = 4
HG_CHUNK = 32
IN_COLS = 2816
D_FF = 4096
EPS = 1e-6
N_DEV = 8

ADAM_LR = 0.001
ADAM_B1 = 0.9
ADAM_B2 = 0.999
ADAM_EPS = 1e-08
ADAM_WD = 0.01
ADAM_STEP = 10

VMEM_LIMIT_BIG = 56 << 20
LANES = 128

MESH = pl.DeviceIdType.MESH
NT_DIMS = (((1,), (1,)), ((), ()))
TN_DIMS = (((0,), (0,)), ((), ()))


def _dot(a, b):
    return jnp.dot(a, b, preferred_element_type=F32)


def _dot_nt(a, b):
    return lax.dot_general(a, b, NT_DIMS, preferred_element_type=F32)


def _dot_tn(a, b):
    return lax.dot_general(a, b, TN_DIMS, preferred_element_type=F32)


def _bf(a):
    return a.astype(BF16)


def _sigmoid(a):
    return 1.0 / (1.0 + jnp.exp(-a))


def _mean_last(a):
    return jnp.mean(a, axis=-1, keepdims=True)


def _sum_rows(a):
    return jnp.sum(a, axis=0, keepdims=True)


def _tri_sum(tri_bf, a):
    a1 = _bf(a)
    r1 = a - a1.astype(F32)
    a2 = _bf(r1)
    a3 = _bf(r1 - a2.astype(F32))
    return _dot(tri_bf, a1) + _dot(tri_bf, a2) + _dot(tri_bf, a3)


def _loop_pairs(first, count, body, init):
    if count % 2:
        return lax.fori_loop(first, first + count, body, init)
    return lax.fori_loop(0, count // 2, lambda i, c: body(first + 2 * i + 1, body(first + 2 * i, c)), init)


def _params(sem=None, vmem=None):
    kw = {}
    if sem is not None:
        kw["dimension_semantics"] = sem
    if vmem is not None:
        kw["vmem_limit_bytes"] = vmem
    return pltpu.CompilerParams(**kw)


ANY_SPEC = pl.BlockSpec(memory_space=pl.ANY)


def _exchange_shapes(srcs, modes):
    out_shape = []
    for s, m in zip(srcs, modes):
        shp = {"gather": (N_DEV,) + tuple(s.shape), "pair": (s.shape[0],) + tuple(s.shape[2:])}.get(m, tuple(s.shape))
        out_shape.append(SDS(shp, s.dtype))
    return out_shape


def _exchange_sems(n):
    if n == 0:
        return []
    return [pltpu.SemaphoreType.DMA((n, N_DEV - 1)), pltpu.SemaphoreType.DMA((n, N_DEV - 1)),
            pltpu.SemaphoreType.DMA((n,))]


SIBLING = 1
OTHER_CHIPS = (2, 4, 6)


def _related(k):
    x, y, c = lax.axis_index("x"), lax.axis_index("y"), lax.axis_index("c")
    px, py, pc = x ^ ((k >> 2) & 1), y ^ ((k >> 1) & 1), c ^ (k & 1)
    return (px, py, pc), 4 * px + 2 * py + pc


def _exchange_phases(modes, src_refs, out_refs, send_sems, recv_sems, own_sems):
    _, me = _related(0)
    sib_dev, sib = _related(SIBLING)
    start, middle, end = [], [], []

    def remote(a, i, src, dst, dev):
        return pltpu.make_async_remote_copy(src_ref=src, dst_ref=dst, send_sem=send_sems.at[a, i],
                                            recv_sem=recv_sems.at[a, i], device_id=dev, device_id_type=MESH)

    for a, mode in enumerate(modes):
        out = out_refs[a]
        if mode == "gather":
            src = src_refs[a]
            own = pltpu.make_async_copy(src, out.at[me], own_sems.at[a])
            to_sib = remote(a, 0, src, out.at[me], sib_dev)
            start += [own.start, to_sib.start]
            end += [remote(a, 0, src, out.at[sib], sib_dev).wait_recv, to_sib.wait_send, own.wait]
            for j, k in enumerate(OTHER_CHIPS, start=1):
                dev, peer = _related(k)
                _, peer_sib = _related(k ^ SIBLING)
                send = remote(a, j, src, out.at[me], dev)
                passed = remote(a, 3 + j, out.at[peer], out.at[peer], sib_dev)
                start.append(send.start)
                middle += [remote(a, j, src, out.at[peer], dev).wait_recv, passed.start]
                end += [remote(a, 3 + j, out.at[peer_sib], out.at[peer_sib], sib_dev).wait_recv,
                        send.wait_send, passed.wait_send]
        elif mode == "pair":
            core = lax.axis_index("c")
            for s in range(N_DEV // 2):
                send = remote(a, s, src_refs[a].at[s, 1 - core], out.at[s], sib_dev)
                start.append(send.start)
                end += [remote(a, s, src_refs[a].at[s, 1 - core], out.at[s], sib_dev).wait_recv, send.wait_send]
        elif mode == "chips":
            chip = me // 2
            own = pltpu.make_async_copy(src_refs[a].at[chip], out.at[chip], own_sems.at[a])
            start.append(own.start)
            end.append(own.wait)
            for j, k in enumerate(OTHER_CHIPS, start=1):
                dev, peer = _related(k)
                send = remote(a, j, src_refs[a].at[peer // 2], out.at[chip], dev)
                start.append(send.start)
                end += [remote(a, j, src_refs[a].at[peer // 2], out.at[peer // 2], dev).wait_recv, send.wait_send]
        else:
            own = pltpu.make_async_copy(src_refs[a].at[me], out.at[me], own_sems.at[a])
            start.append(own.start)
            end.append(own.wait)
            for k in range(1, N_DEV):
                dev, peer = _related(k)
                send = remote(a, k - 1, src_refs[a].at[peer], out.at[me], dev)
                start.append(send.start)
                end += [remote(a, k - 1, src_refs[a].at[peer], out.at[peer], dev).wait_recv, send.wait_send]
    return start, middle, end


def _run(actions):
    for act in actions:
        act()


def _exchange(name, srcs, modes):
    n = len(srcs)

    def body(*refs):
        start, middle, end = _exchange_phases(modes, refs[:n], refs[n:2 * n], *refs[2 * n:])
        _run(start)
        _run(middle)
        _run(end)

    return pl.pallas_call(
        body, name=name, out_shape=_exchange_shapes(srcs, modes),
        in_specs=[ANY_SPEC] * n, out_specs=[ANY_SPEC] * n,
        scratch_shapes=_exchange_sems(n),
    )(*srcs)


def _ride_start(modes, step, steps, src_refs, out_refs, sems):
    if not modes:
        return
    middle_step = steps - 1

    @pl.when(step == 0)
    def _():
        _run(_exchange_phases(modes, src_refs, out_refs, *sems)[0])

    if "gather" in modes:
        @pl.when(step == middle_step)
        def _():
            _run(_exchange_phases(modes, src_refs, out_refs, *sems)[1])


def _ride_wait(modes, step, steps, src_refs, out_refs, sems):
    if not modes:
        return

    @pl.when(step == steps - 1)
    def _():
        _run(_exchange_phases(modes, src_refs, out_refs, *sems)[2])


def _ada_mod(c_all, w_ada, b_ada_mine):
    nb, cols = c_all.shape[0], w_ada.shape[1]

    def body(c_ref, w_ref, b_ref, o_ref):
        cv = c_ref[...]
        ca = cv * _sigmoid(cv)
        o_ref[...] = _dot(ca, w_ref[...]) + b_ref[...]

    return pl.pallas_call(body, name="ada_mod", out_shape=SDS((nb, cols), F32))(c_all, w_ada, b_ada_mine)


def _tile_rows(T):
    return min(256, T)


def _mod_spec(tps):
    return pl.BlockSpec((None, 8, D_MODEL), lambda i: (i // tps, 0, 0))


def _in_proj(x2, mod8, pre_w, w_in_bf, T, ride_srcs, ride_modes):
    N = x2.shape[0]
    TM = _tile_rows(T)
    tps = T // TM
    nr = len(ride_srcs)

    def body(*refs):
        x_ref, mod_ref, pw_ref, w_ref = refs[:4]
        ride_in = refs[4:4 + nr]
        proj_ref, h1_ref = refs[4 + nr:6 + nr]
        ride_out = refs[6 + nr:6 + 2 * nr]
        sems = refs[6 + 2 * nr:]
        _ride_start(ride_modes, pl.program_id(0), N // TM, ride_in, ride_out, sems)
        x = x_ref[...]
        r = lax.rsqrt(_mean_last(x * x) + EPS)
        h = (x * r * pw_ref[...]) * (1.0 + mod_ref[1:2, :]) + mod_ref[0:1, :]
        hb = _bf(h)
        h1_ref[...] = hb
        proj_ref[...] = _dot_nt(hb, w_ref[...])
        _ride_wait(ride_modes, pl.program_id(0), N // TM, ride_in, ride_out, sems)

    return pl.pallas_call(
        body, name="in_proj", grid=(N // TM,),
        in_specs=[pl.BlockSpec((TM, D_MODEL), lambda i: (i, 0)), _mod_spec(tps),
                  pl.BlockSpec((1, D_MODEL), lambda i: (0, 0)),
                  pl.BlockSpec((IN_COLS, D_MODEL), lambda i: (0, 0))] + [ANY_SPEC] * nr,
        out_specs=[pl.BlockSpec((TM, IN_COLS), lambda i: (i, 0)),
                   pl.BlockSpec((TM, D_MODEL), lambda i: (i, 0))] + [ANY_SPEC] * nr,
        out_shape=[SDS((N, IN_COLS), F32), SDS((N, D_MODEL), BF16)] + _exchange_shapes(ride_srcs, ride_modes),
        scratch_shapes=_exchange_sems(nr),
        compiler_params=_params(("arbitrary",), VMEM_LIMIT_BIG),
    )(x2, mod8, pre_w, w_in_bf, *ride_srcs)


def _rope_tables(T):
    half = ROPE_DIM // 2
    inv_freq = ROPE_THETA ** (-jnp.arange(0, ROPE_DIM, 2, dtype=F32) / ROPE_DIM)
    ang = jnp.arange(T, dtype=F32)[:, None] * inv_freq[None, :]
    cos, sin = jnp.cos(ang), jnp.sin(ang)
    ones = jnp.ones((T, ATT_HEAD_DIM - ROPE_DIM), F32)
    zeros = jnp.zeros((T, ATT_HEAD_DIM - ROPE_DIM), F32)
    zh = jnp.zeros((T, half), F32)
    cos64 = jnp.concatenate([cos, cos, ones], axis=1)
    sin_left = jnp.concatenate([-sin, zh, zeros], axis=1)
    sin_right = jnp.concatenate([zh, sin, zeros], axis=1)
    rep = LANES // ATT_HEAD_DIM
    return jnp.tile(cos64, (1, rep)), jnp.tile(sin_left, (1, rep)), jnp.tile(sin_right, (1, rep))


def _rope(xc, cs, sl, sr):
    return xc * cs + pltpu.roll(xc, LANES - 8, 1) * sl + pltpu.roll(xc, 8, 1) * sr


def _rope_t(dy, cs, sl, sr):
    return dy * cs + pltpu.roll(dy * sl, 8, 1) + pltpu.roll(dy * sr, LANES - 8, 1)


ATT_SCALE = ATT_HEAD_DIM ** -0.5
ATT_SPLITS = 4


def _band_masks():
    cols = ATT_GROUP * WINDOW
    j = lax.broadcasted_iota(jnp.int32, (2 * WINDOW, cols), 0)
    i = lax.broadcasted_iota(jnp.int32, (2 * WINDOW, cols), 1) & (WINDOW - 1)
    diff = i + WINDOW - j
    return (diff >= 0) & (diff < WINDOW), j >= WINDOW


def _sink_row(sink_ref, hk):
    return jnp.concatenate(
        [jnp.full((1, WINDOW), sink_ref[0, ATT_GROUP * hk + g], F32) for g in range(ATT_GROUP)], axis=1)


def _softmax_band(qs, kk, mask, sink):
    s = jnp.where(mask, _dot_nt(kk, qs), jnp.finfo(F32).min)
    m = jnp.maximum(jnp.max(s, axis=0, keepdims=True), sink)
    p = jnp.exp(s - m)
    es = jnp.exp(sink - m)
    inv = 1.0 / (jnp.sum(p, axis=0, keepdims=True) + es)
    return p, inv, es


def _stack_heads(parts, hk):
    hs = []
    for g in range(ATT_GROUP):
        h = ATT_GROUP * hk + g
        hs.append(parts[h // 2][:, (h % 2) * ATT_HEAD_DIM:(h % 2 + 1) * ATT_HEAD_DIM])
    return jnp.concatenate(hs, axis=0)


def _attn_fwd(proj3, tables, sinks, attn_w, ride_srcs, ride_modes):
    B, T, _ = proj3.shape
    nb = T // WINDOW
    splits = min(ATT_SPLITS, nb)
    per = nb // splits
    nr = len(ride_srcs)
    cos, sinl, sinr = tables

    def body(*refs):
        q_ref, k_ref, v_ref, cos_ref, sl_ref, sr_ref, sink_ref, aw_ref = refs[:8]
        ride_in = refs[8:8 + nr]
        o_ref, an_ref, qr_ref, kr_ref = refs[8 + nr:12 + nr]
        ride_out = refs[12 + nr:12 + 2 * nr]
        kpad, vpad = refs[12 + 2 * nr:14 + 2 * nr]
        sems = refs[14 + 2 * nr:]
        part = pl.program_id(1)
        step = pl.program_id(0) * splits + part
        _ride_start(ride_modes, step, B * splits, ride_in, ride_out, sems)

        @pl.when(part == 0)
        def _():
            kpad[0:WINDOW, :] = jnp.zeros((WINDOW, LANES), BF16)
            vpad[0:WINDOW, :] = jnp.zeros((WINDOW, LANES), BF16)

        window, current = _band_masks()

        def block(n, carry):
            r0 = pl.multiple_of(n * WINDOW, WINDOW)
            rows = pl.ds(r0, WINDOW)
            nxt = pl.ds(r0 + WINDOW, WINDOW)
            band = pl.ds(r0, 2 * WINDOW)
            cs, sl, sr = cos_ref[rows, :], sl_ref[rows, :], sr_ref[rows, :]
            kb = _bf(_rope(k_ref[rows, :], cs, sl, sr))
            kpad[nxt, :] = kb
            kr_ref[rows, :] = kb
            vpad[nxt, :] = _bf(v_ref[rows, :])
            qparts = []
            for j in range(ATT_WIDTH // LANES):
                qp = _bf(_rope(q_ref[rows, j * LANES:(j + 1) * LANES], cs, sl, sr) * ATT_SCALE)
                qr_ref[rows, j * LANES:(j + 1) * LANES] = qp
                qparts.append(qp)
            mask = window & (current | (n > 0))
            for hk in range(ATT_KV_HEADS):
                lanes = slice(hk * ATT_HEAD_DIM, (hk + 1) * ATT_HEAD_DIM)
                qs = _stack_heads(qparts, hk)
                p, inv, _ = _softmax_band(qs, kpad[band, lanes], mask, _sink_row(sink_ref, hk))
                ot = _dot_tn(vpad[band, lanes], _bf(p)) * inv
                for g in range(ATT_GROUP):
                    h = ATT_GROUP * hk + g
                    o_ref[rows, h * ATT_HEAD_DIM:(h + 1) * ATT_HEAD_DIM] = ot[:, g * WINDOW:(g + 1) * WINDOW].T
            ob = o_ref[rows, :]
            an_ref[rows, :] = _bf(ob * lax.rsqrt(_mean_last(ob * ob) + EPS) * aw_ref[...])
            return carry

        _loop_pairs(part * per, per, block, 0)
        _ride_wait(ride_modes, step, B * splits, ride_in, ride_out, sems)

    seq = lambda w, j: pl.BlockSpec((None, T, w), lambda b, s: (b, 0, j))
    full = lambda r, w: pl.BlockSpec((r, w), lambda b, s: (0, 0))
    return pl.pallas_call(
        body, name="attn_fwd", grid=(B, splits),
        in_specs=[seq(ATT_WIDTH, 0), seq(LANES, 4), seq(LANES, 5),
                  full(T, LANES), full(T, LANES), full(T, LANES),
                  pl.BlockSpec(memory_space=pltpu.SMEM), full(1, ATT_WIDTH)] + [ANY_SPEC] * nr,
        out_specs=[seq(ATT_WIDTH, 0), seq(ATT_WIDTH, 0), seq(ATT_WIDTH, 0), seq(LANES, 0)] + [ANY_SPEC] * nr,
        out_shape=[SDS((B, T, ATT_WIDTH), F32), SDS((B, T, ATT_WIDTH), BF16),
                   SDS((B, T, ATT_WIDTH), BF16), SDS((B, T, LANES), BF16)] + _exchange_shapes(ride_srcs, ride_modes),
        scratch_shapes=[pltpu.VMEM((T + WINDOW, LANES), BF16), pltpu.VMEM((T + WINDOW, LANES), BF16)]
        + _exchange_sems(nr),
        compiler_params=_params(("arbitrary", "arbitrary"), VMEM_LIMIT_BIG),
    )(proj3, proj3, proj3, cos, sinl, sinr, sinks, attn_w, *ride_srcs)


HG_GROUP = 8
HG_ROWS = HG_GROUP * HG_CHUNK


HG_STACK = HG_GROUP * HG_HEAD_DIM


def _group_masks():
    r = lax.broadcasted_iota(jnp.int32, (HG_ROWS, HG_ROWS), 0)
    c = lax.broadcasted_iota(jnp.int32, (HG_ROWS, HG_ROWS), 1)
    same = (r // HG_CHUNK) == (c // HG_CHUNK)
    return same & (r >= c), same & (c >= r)


def _row_chunk():
    return lax.broadcasted_iota(jnp.int32, (HG_ROWS, HG_HEAD_DIM), 0) // HG_CHUNK


def _spread(a, row_chunk):
    return jnp.concatenate([jnp.where(row_chunk == c, a, jnp.zeros_like(a)) for c in range(HG_GROUP)], axis=1)


def _pick(r, row_chunk):
    out = jnp.where(row_chunk == 0, r[:, :HG_HEAD_DIM], 0.0)
    for c in range(1, HG_GROUP):
        out = out + jnp.where(row_chunk == c, r[:, c * HG_HEAD_DIM:(c + 1) * HG_HEAD_DIM], 0.0)
    return out


def _lane_block(a, c):
    return a[:, c * HG_HEAD_DIM:(c + 1) * HG_HEAD_DIM]


def _ones_bf(mask):
    return jnp.where(mask, 1.0, 0.0).astype(BF16)


def _chunk_bcast(rows_1x128):
    return jnp.concatenate([jnp.broadcast_to(r, (HG_CHUNK, HG_HEAD_DIM)) for r in rows_1x128], axis=0)


def _hgrn_gates(hq, hf, lb, lower_bf):
    sq = _sigmoid(hq)
    q = hq * sq
    sg = _sigmoid(hf)
    f = lb + (1.0 - lb) * sg
    k = 1.0 - f
    logf = jnp.log(f)
    b = _tri_sum(lower_bf, logf)
    bl = [_sum_rows(logf[_chunk_rows(c), :]) for c in range(HG_GROUP)]
    eb, enb, e2 = jnp.exp(b), jnp.exp(-b), jnp.exp(_chunk_bcast(bl) - b)
    ebl = [jnp.exp(r) for r in bl]
    return dict(sq=sq, sg=sg, f=f, eb=eb, enb=enb, e2=e2, ebl=ebl, qd=q * eb, kd=k * enb, k2=k * e2)


def _hgrn_specs(B, T):
    head = lambda base: pl.BlockSpec((None, T, LANES), lambda b, h: (b, 0, base + h))
    return head


def _chunk_rows(c):
    return slice(c * HG_CHUNK, (c + 1) * HG_CHUNK)


def _loop_groups(ng, group, init):
    return _loop_pairs(0, ng, group, init)


def _hgrn_fwd(proj3, lb, hg_w, ride_srcs, ride_modes):
    B, T, _ = proj3.shape
    nc = T // HG_CHUNK
    ng = T // HG_ROWS
    nr = len(ride_srcs)
    head = _hgrn_specs(B, T)

    def body(*refs):
        hq_ref, hf_ref, hi_ref, hg_ref, lb_ref, gw_ref = refs[:6]
        ride_in = refs[6:6 + nr]
        o_ref, rg_ref, sp_ref = refs[6 + nr:9 + nr]
        ride_out = refs[9 + nr:9 + 2 * nr]
        sems = refs[9 + 2 * nr:]
        step = pl.program_id(0) * HG_HEADS + pl.program_id(1)
        _ride_start(ride_modes, step, B * HG_HEADS, ride_in, ride_out, sems)

        lo, _ = _group_masks()
        lower_bf = _ones_bf(lo)
        row_chunk = _row_chunk()
        lbv = lb_ref[...]

        def group(gi, s):
            rows = pl.ds(pl.multiple_of(gi * HG_ROWS, HG_ROWS), HG_ROWS)
            gt = _hgrn_gates(hq_ref[rows, :], hf_ref[rows, :], lbv, lower_bf)
            v, qd, kd = _bf(hi_ref[rows, :]), _bf(gt["qd"]), _bf(gt["kd"])
            a = jnp.where(lo, _dot_nt(qd, kd), 0.0)
            kv = _dot_tn(v, _bf(_spread(gt["k2"], row_chunk)))
            before = []
            for c in range(HG_GROUP):
                before.append(s)
                s = s * gt["ebl"][c] + _lane_block(kv, c)
            sp = jnp.concatenate(before, axis=1)
            sp_ref[gi] = sp
            o = _dot(_bf(a), v) + _dot_nt(_bf(_spread(gt["qd"], row_chunk)), _bf(sp))
            o_ref[rows, :] = o
            hg = hg_ref[rows, :]
            rn = o * lax.rsqrt(_mean_last(o * o) + EPS) * gw_ref[...]
            rg_ref[rows, :] = _bf(rn * (hg * _sigmoid(hg)))
            return s

        _loop_groups(ng, group, jnp.zeros((HG_HEAD_DIM, HG_HEAD_DIM), F32))
        _ride_wait(ride_modes, step, B * HG_HEADS, ride_in, ride_out, sems)

    out_head = pl.BlockSpec((None, T, LANES), lambda b, h: (b, 0, h))
    return pl.pallas_call(
        body, name="hgrn_fwd", grid=(B, HG_HEADS),
        in_specs=[head(6), head(10), head(14), head(18),
                  pl.BlockSpec((1, LANES), lambda b, h: (0, h)),
                  pl.BlockSpec((1, LANES), lambda b, h: (0, 0))] + [ANY_SPEC] * nr,
        out_specs=[out_head, out_head,
                   pl.BlockSpec((None, None, ng, HG_HEAD_DIM, HG_STACK), lambda b, h: (b, h, 0, 0, 0))]
        + [ANY_SPEC] * nr,
        out_shape=[SDS((B, T, HG_WIDTH), F32), SDS((B, T, HG_WIDTH), BF16),
                   SDS((B, HG_HEADS, ng, HG_HEAD_DIM, HG_STACK), F32)] + _exchange_shapes(ride_srcs, ride_modes),
        scratch_shapes=_exchange_sems(nr),
        compiler_params=_params(("arbitrary", "arbitrary"), VMEM_LIMIT_BIG),
    )(proj3, proj3, proj3, proj3, lb, hg_w, *ride_srcs)


def _mix_out(x2, attn_n, rec_g, mod8, post_w, w_out_bf, T, ride_srcs, ride_modes):
    N = x2.shape[0]
    TM = _tile_rows(T)
    tps = T // TM
    nr = len(ride_srcs)

    def body(*refs):
        x_ref, an_ref, rg_ref, mod_ref, pw_ref, w_ref = refs[:6]
        ride_in = refs[6:6 + nr]
        mix_ref, x1_ref, cat_ref = refs[6 + nr:9 + nr]
        ride_out = refs[9 + nr:9 + 2 * nr]
        sems = refs[9 + 2 * nr:]
        _ride_start(ride_modes, pl.program_id(0), N // TM, ride_in, ride_out, sems)
        cat = jnp.concatenate([an_ref[...], rg_ref[...]], axis=1)
        cat_ref[...] = cat
        mix = _dot(cat, w_ref[...])
        mix_ref[...] = mix
        r = lax.rsqrt(_mean_last(mix * mix) + EPS)
        x1_ref[...] = x_ref[...] + mod_ref[2:3, :] * (mix * r * pw_ref[...])
        _ride_wait(ride_modes, pl.program_id(0), N // TM, ride_in, ride_out, sems)

    row = lambda w: pl.BlockSpec((TM, w), lambda i: (i, 0))
    return pl.pallas_call(
        body, name="mix_out", grid=(N // TM,),
        in_specs=[row(D_MODEL), row(ATT_WIDTH), row(HG_WIDTH), _mod_spec(tps),
                  pl.BlockSpec((1, D_MODEL), lambda i: (0, 0)),
                  pl.BlockSpec((D_MODEL, D_MODEL), lambda i: (0, 0))] + [ANY_SPEC] * nr,
        out_specs=[row(D_MODEL), row(D_MODEL), row(D_MODEL)] + [ANY_SPEC] * nr,
        out_shape=[SDS((N, D_MODEL), F32), SDS((N, D_MODEL), F32), SDS((N, D_MODEL), BF16)]
        + _exchange_shapes(ride_srcs, ride_modes),
        scratch_shapes=_exchange_sems(nr),
        compiler_params=_params(("arbitrary",), VMEM_LIMIT_BIG),
    )(x2, attn_n, rec_g, mod8, post_w, w_out_bf, *ride_srcs)


def _load_weights_once(pairs, sem):
    @pl.when(pl.program_id(0) == 0)
    def _():
        cps = [pltpu.make_async_copy(src, dst, sem.at[i]) for i, (src, dst) in enumerate(pairs)]
        for cp in cps:
            cp.start()
        for cp in cps:
            cp.wait()


MLP_HALF = D_MODEL // 2
MLP_PIECES = 2 * N_DEV + 2


def _mlp_weight_pieces(wu_a, wu_b, wd_a, wd_b, wu, wd):
    cols = D_FF // N_DEV
    pairs = []
    for h, half in enumerate((wu_a, wu_b)):
        for j in range(N_DEV):
            pairs.append((half.at[j], wu.at[pl.ds(h * MLP_HALF, MLP_HALF), pl.ds(j * cols, cols)]))
    for h, half in enumerate((wd_a, wd_b)):
        pairs.append((half, wd.at[:, pl.ds(h * MLP_HALF, MLP_HALF)]))
    return pairs


def _mlp_fwd(x1, mod8, pre_w, w_up_halves, w_down_halves, T):
    N = x1.shape[0]
    TM = _tile_rows(T)
    tps = T // TM

    def body(x_ref, mod_ref, pw_ref, wua, wub, wda, wdb, up_ref, d_ref, h2_ref, wu, wd, sem):
        _load_weights_once(_mlp_weight_pieces(wua, wub, wda, wdb, wu, wd), sem)
        x = x_ref[...]
        r = lax.rsqrt(_mean_last(x * x) + EPS)
        h = (x * r * pw_ref[...]) * (1.0 + mod_ref[4:5, :]) + mod_ref[3:4, :]
        hb = _bf(h)
        h2_ref[...] = hb
        up = _dot(hb, wu[...])
        up_ref[...] = up
        ru = jnp.maximum(up, 0.0)
        d_ref[...] = _dot(_bf(ru * ru), wd[...])

    row = lambda w: pl.BlockSpec((TM, w), lambda i: (i, 0))
    return pl.pallas_call(
        body, name="mlp_fwd", grid=(N // TM,),
        in_specs=[row(D_MODEL), _mod_spec(tps), pl.BlockSpec((1, D_MODEL), lambda i: (0, 0))] + [ANY_SPEC] * 4,
        out_specs=[row(D_FF), row(D_MODEL), row(D_MODEL)],
        out_shape=[SDS((N, D_FF), F32), SDS((N, D_MODEL), F32), SDS((N, D_MODEL), BF16)],
        scratch_shapes=[pltpu.VMEM((D_MODEL, D_FF), BF16), pltpu.VMEM((D_FF, D_MODEL), BF16),
                        pltpu.SemaphoreType.DMA((MLP_PIECES,))],
        compiler_params=_params(("arbitrary",), VMEM_LIMIT_BIG),
    )(x1, mod8, pre_w, *w_up_halves, *w_down_halves)


def _acc_rows(acc_ref, first, rows):
    @pl.when(first)
    def _():
        acc_ref[...] = jnp.zeros(acc_ref.shape, F32)
    for i, r in enumerate(rows):
        acc_ref[i:i + 1, :] += r


def _mlp_bwd(x1, d, up, tgt, mod8, pre_w, post_w, w_up_halves, w_down_halves, T):
    N = x1.shape[0]
    TM = _tile_rows(T)
    tps = T // TM

    def body(x_ref, d_ref, up_ref, t_ref, mod_ref, pw_ref, qw_ref, wua, wub, wda, wdb,
             dx_ref, u_ref, dup_ref, dd_ref, acc_ref, wd, wu, sem):
        _load_weights_once(_mlp_weight_pieces(wua, wub, wda, wdb, wu, wd), sem)
        sh2, sc2, g2 = mod_ref[3:4, :], mod_ref[4:5, :], mod_ref[5:6, :]
        x = x_ref[...]
        r1 = lax.rsqrt(_mean_last(x * x) + EPS)
        xh = x * r1
        n2 = xh * pw_ref[...]
        dv = d_ref[...]
        rd = lax.rsqrt(_mean_last(dv * dv) + EPS)
        dh = dv * rd
        rr = dh * qw_ref[...]
        e = x + g2 * rr - t_ref[...]
        loss = 0.5 * jnp.sum(_sum_rows(e * e), axis=1, keepdims=True) / D_MODEL
        dy = e * (1.0 / D_MODEL)
        dg2 = _sum_rows(dy * rr)
        drr = dy * g2
        dw_post = _sum_rows(drr * dh)
        ddh = drr * qw_ref[...]
        dd = _bf(rd * (ddh - dh * _mean_last(ddh * dh)))
        dd_ref[...] = dd
        ru = jnp.maximum(up_ref[...], 0.0)
        u_ref[...] = _bf(ru * ru)
        dup = _bf(_dot_nt(dd, wd[...]) * (2.0 * ru))
        dup_ref[...] = dup
        dh2 = _dot_nt(dup, wu[...])
        dsh2 = _sum_rows(dh2)
        dsc2 = _sum_rows(dh2 * n2)
        dn2 = dh2 * (1.0 + sc2)
        dw_pre = _sum_rows(dn2 * xh)
        dxh = dn2 * pw_ref[...]
        dx_ref[...] = dy + r1 * (dxh - xh * _mean_last(dxh * xh))
        _acc_rows(acc_ref, pl.program_id(0) % tps == 0,
                  [dsh2, dsc2, dg2, dw_pre, dw_post, jnp.broadcast_to(loss, (1, D_MODEL))])

    row = lambda w: pl.BlockSpec((TM, w), lambda i: (i, 0))
    vec = pl.BlockSpec((1, D_MODEL), lambda i: (0, 0))
    B = N // T
    return pl.pallas_call(
        body, name="mlp_bwd", grid=(N // TM,),
        in_specs=[row(D_MODEL), row(D_MODEL), row(D_FF), row(D_MODEL), _mod_spec(tps), vec, vec] + [ANY_SPEC] * 4,
        out_specs=[row(D_MODEL), row(D_FF), row(D_FF), row(D_MODEL), _mod_spec(tps)],
        out_shape=[SDS((N, D_MODEL), F32), SDS((N, D_FF), BF16), SDS((N, D_FF), BF16),
                   SDS((N, D_MODEL), BF16), SDS((B, 8, D_MODEL), F32)],
        scratch_shapes=[pltpu.VMEM((D_FF, D_MODEL), BF16), pltpu.VMEM((D_MODEL, D_FF), BF16),
                        pltpu.SemaphoreType.DMA((MLP_PIECES,))],
        compiler_params=_params(("arbitrary",), VMEM_LIMIT_BIG),
    )(x1, d, up, tgt, mod8, pre_w, post_w, *w_up_halves, *w_down_halves)


def _mix_bwd(mix, dx1, mod8, post_w, w_out_bf, T, ride_srcs, ride_modes):
    N = mix.shape[0]
    TM = _tile_rows(T)
    tps = T // TM
    nr = len(ride_srcs)

    def body(*refs):
        mix_ref, dx_ref, mod_ref, pw_ref, w_ref = refs[:5]
        ride_in = refs[5:5 + nr]
        dan_ref, drg_ref, dmix_ref, acc_ref = refs[5 + nr:9 + nr]
        ride_out = refs[9 + nr:9 + 2 * nr]
        sems = refs[9 + 2 * nr:]
        _ride_start(ride_modes, pl.program_id(0), N // TM, ride_in, ride_out, sems)
        g1 = mod_ref[2:3, :]
        mix = mix_ref[...]
        dx1 = dx_ref[...]
        rm = lax.rsqrt(_mean_last(mix * mix) + EPS)
        mh = mix * rm
        dg1 = _sum_rows(dx1 * (mh * pw_ref[...]))
        dr = dx1 * g1
        dw_post = _sum_rows(dr * mh)
        dmh = dr * pw_ref[...]
        dmix = _bf(rm * (dmh - mh * _mean_last(dmh * mh)))
        dmix_ref[...] = dmix
        dcat = _dot_nt(dmix, w_ref[...])
        dan_ref[...] = dcat[:, :ATT_WIDTH]
        drg_ref[...] = dcat[:, ATT_WIDTH:]
        _acc_rows(acc_ref, pl.program_id(0) % tps == 0, [dg1, dw_post])
        _ride_wait(ride_modes, pl.program_id(0), N // TM, ride_in, ride_out, sems)

    row = lambda w: pl.BlockSpec((TM, w), lambda i: (i, 0))
    B = N // T
    return pl.pallas_call(
        body, name="mix_bwd", grid=(N // TM,),
        in_specs=[row(D_MODEL), row(D_MODEL), _mod_spec(tps), pl.BlockSpec((1, D_MODEL), lambda i: (0, 0)),
                  pl.BlockSpec((D_MODEL, D_MODEL), lambda i: (0, 0))] + [ANY_SPEC] * nr,
        out_specs=[row(ATT_WIDTH), row(HG_WIDTH), row(D_MODEL), _mod_spec(tps)] + [ANY_SPEC] * nr,
        out_shape=[SDS((N, ATT_WIDTH), F32), SDS((N, HG_WIDTH), F32), SDS((N, D_MODEL), BF16),
                   SDS((B, 8, D_MODEL), F32)] + _exchange_shapes(ride_srcs, ride_modes),
        scratch_shapes=_exchange_sems(nr),
        compiler_params=_params(("arbitrary",), VMEM_LIMIT_BIG),
    )(mix, dx1, mod8, post_w, w_out_bf, *ride_srcs)


def _hgrn_bwd(proj3, lb, hg_w, o, s_prev, drg, ride_srcs, ride_modes):
    B, T, _ = proj3.shape
    nc = T // HG_CHUNK
    ng = T // HG_ROWS
    nr = len(ride_srcs)
    head = _hgrn_specs(B, T)

    def body(*refs):
        hq_ref, hf_ref, hi_ref, hg_ref, lb_ref, gw_ref, o_ref, sp_ref, drg_ref = refs[:9]
        ride_in = refs[9:9 + nr]
        dhq_ref, dhf_ref, dhi_ref, dhg_ref, dlb_ref, dgw_ref = refs[9 + nr:15 + nr]
        ride_out = refs[15 + nr:15 + 2 * nr]
        sems = refs[15 + 2 * nr:]
        step = pl.program_id(0) * HG_HEADS + pl.program_id(1)
        _ride_start(ride_modes, step, B * HG_HEADS, ride_in, ride_out, sems)

        lo, up = _group_masks()
        lower_bf, upper_bf = _ones_bf(lo), _ones_bf(up)
        row_chunk = _row_chunk()
        lbv = lb_ref[...]
        gw = gw_ref[...]

        def group(i, carry):
            dlb, dgw, ds = carry
            gi = ng - 1 - i
            rows = pl.ds(pl.multiple_of(gi * HG_ROWS, HG_ROWS), HG_ROWS)
            hq = hq_ref[rows, :]
            gt = _hgrn_gates(hq, hf_ref[rows, :], lbv, lower_bf)
            sq, sg, qdf, kdf, k2f, ebl = gt["sq"], gt["sg"], gt["qd"], gt["kd"], gt["k2"], gt["ebl"]
            v, qd, kd = _bf(hi_ref[rows, :]), _bf(qdf), _bf(kdf)
            ov = o_ref[rows, :]
            hg = hg_ref[rows, :]
            shg = _sigmoid(hg)
            dr = drg_ref[rows, :]
            ro = lax.rsqrt(_mean_last(ov * ov) + EPS)
            oh = ov * ro
            dhg_ref[rows, :] = dr * (oh * gw) * (shg + hg * shg * (1.0 - shg))
            drn = dr * (hg * shg)
            dgw = dgw + _sum_rows(drn * oh)
            doh = drn * gw
            do = _bf(ro * (doh - oh * _mean_last(doh * oh)))
            a = jnp.where(lo, _dot_nt(qd, kd), 0.0)
            da = _bf(jnp.where(lo, _dot_nt(do, v), 0.0))
            dv = _dot_tn(_bf(a), do)
            dqd = _dot(da, kd)
            dkd = _dot_tn(da, qd)
            sp = sp_ref[gi]
            incr = _dot_tn(do, _bf(_spread(qdf, row_chunk)))
            after = [None] * HG_GROUP
            for c in reversed(range(HG_GROUP)):
                after[c] = ds
                ds = ds * ebl[c] + _lane_block(incr, c)
            dss = jnp.concatenate(after, axis=1)
            dssb = _bf(dss)
            dk2 = _pick(_dot(v, dssb), row_chunk)
            dhi_ref[rows, :] = dv + _dot_nt(_bf(_spread(k2f, row_chunk)), dssb)
            dqd = dqd + _pick(_dot(do, _bf(sp)), row_chunk)
            debl = _sum_rows(dss * sp)
            k2g = dk2 * k2f
            db = dqd * qdf - dkd * kdf - k2g
            dk = dkd * gt["enb"] + dk2 * gt["e2"]
            dbl = _chunk_bcast([_lane_block(debl, c) * ebl[c] + _sum_rows(k2g[_chunk_rows(c), :])
                                for c in range(HG_GROUP)])
            dg = _tri_sum(upper_bf, db) + dbl
            df = dg / gt["f"] - dk
            dhf_ref[rows, :] = df * (1.0 - lbv) * sg * (1.0 - sg)
            dlb = dlb + _sum_rows(df * (1.0 - sg))
            dhq_ref[rows, :] = (dqd * gt["eb"]) * (sq + hq * sq * (1.0 - sq))
            return dlb, dgw, ds

        zero = jnp.zeros((1, LANES), F32)
        dlb, dgw, _ = _loop_groups(ng, group, (zero, zero, jnp.zeros((HG_HEAD_DIM, HG_HEAD_DIM), F32)))
        dlb_ref[...] = jnp.broadcast_to(dlb, (8, LANES))
        dgw_ref[...] = jnp.broadcast_to(dgw, (8, LANES))
        _ride_wait(ride_modes, step, B * HG_HEADS, ride_in, ride_out, sems)

    out_head = pl.BlockSpec((None, T, LANES), lambda b, h: (b, 0, h))
    small = pl.BlockSpec((None, 8, LANES), lambda b, h: (b, 0, h))
    return pl.pallas_call(
        body, name="hgrn_bwd", grid=(B, HG_HEADS),
        in_specs=[head(6), head(10), head(14), head(18),
                  pl.BlockSpec((1, LANES), lambda b, h: (0, h)),
                  pl.BlockSpec((1, LANES), lambda b, h: (0, 0)),
                  out_head,
                  pl.BlockSpec((None, None, ng, HG_HEAD_DIM, HG_STACK), lambda b, h: (b, h, 0, 0, 0)),
                  out_head] + [ANY_SPEC] * nr,
        out_specs=[out_head, out_head, out_head, out_head, small, small] + [ANY_SPEC] * nr,
        out_shape=[SDS((B, T, HG_WIDTH), F32)] * 4 + [SDS((B, 8, HG_WIDTH), F32)] * 2
        + _exchange_shapes(ride_srcs, ride_modes),
        scratch_shapes=_exchange_sems(nr),
        compiler_params=_params(("arbitrary", "arbitrary"), VMEM_LIMIT_BIG),
    )(proj3, proj3, proj3, proj3, lb, hg_w, o, s_prev, drg, *ride_srcs)


def _attn_bwd(qr, kr, proj3, attn_o, dan, tables, sinks, attn_w, ride_srcs, ride_modes):
    B, T, _ = proj3.shape
    nb = T // WINDOW
    splits = min(ATT_SPLITS, nb)
    per = nb // splits
    nr = len(ride_srcs)
    cos, sinl, sinr = tables
    QKV = ATT_WIDTH + 2 * LANES

    def body(*refs):
        qr_ref, kr_ref, v_ref, o_ref, dan_ref, cos_ref, sl_ref, sr_ref, sink_ref, aw_ref = refs[:10]
        ride_in = refs[10:10 + nr]
        dqkv_ref, dsink_ref, daw_ref = refs[10 + nr:13 + nr]
        ride_out = refs[13 + nr:13 + 2 * nr]
        kpad, vpad, dkpad, dvpad, dqb, dsk = refs[13 + 2 * nr:19 + 2 * nr]
        sems = refs[19 + 2 * nr:]
        part = pl.program_id(1)
        step = pl.program_id(0) * splits + part
        _ride_start(ride_modes, step, B * splits, ride_in, ride_out, sems)

        @pl.when(part == 0)
        def _():
            kpad[0:WINDOW, :] = jnp.zeros((WINDOW, LANES), BF16)
            vpad[0:WINDOW, :] = jnp.zeros((WINDOW, LANES), BF16)
            kpad[WINDOW:, :] = kr_ref[...]
            vpad[WINDOW:, :] = _bf(v_ref[...])
            dkpad[...] = jnp.zeros(dkpad.shape, F32)
            dvpad[...] = jnp.zeros(dvpad.shape, F32)
            dsk[...] = jnp.zeros(dsk.shape, F32)
            daw_ref[...] = jnp.zeros(daw_ref.shape, F32)

        window, current = _band_masks()
        aw = aw_ref[...]

        def block(n, daw):
            r0 = pl.multiple_of(n * WINDOW, WINDOW)
            rows = pl.ds(r0, WINDOW)
            band = pl.ds(r0, 2 * WINDOW)
            ob = o_ref[rows, :]
            dn = dan_ref[rows, :]
            ro = lax.rsqrt(_mean_last(ob * ob) + EPS)
            oh = ob * ro
            daw = daw + _sum_rows(dn * oh)
            doh = dn * aw
            do = _bf(ro * (doh - oh * _mean_last(doh * oh)))
            doparts = [do[:, j * LANES:(j + 1) * LANES] for j in range(ATT_WIDTH // LANES)]
            qparts = [qr_ref[rows, j * LANES:(j + 1) * LANES] for j in range(ATT_WIDTH // LANES)]
            mask = window & (current | (n > 0))
            for hk in range(ATT_KV_HEADS):
                lanes = slice(hk * ATT_HEAD_DIM, (hk + 1) * ATT_HEAD_DIM)
                qs = _stack_heads(qparts, hk)
                dos = _stack_heads(doparts, hk)
                kk, vv = kpad[band, lanes], vpad[band, lanes]
                p, inv, es = _softmax_band(qs, kk, mask, _sink_row(sink_ref, hk))
                p = p * inv
                dp = _dot_nt(vv, dos)
                delta = jnp.sum(p * dp, axis=0, keepdims=True)
                ds = _bf(p * (dp - delta))
                sk = (es * inv) * delta
                dqt = _dot_tn(kk, ds) * ATT_SCALE
                dkpad[band, lanes] += _dot(ds, qs)
                dvpad[band, lanes] += _dot(_bf(p), dos)
                for g in range(ATT_GROUP):
                    h = ATT_GROUP * hk + g
                    cols = slice(g * WINDOW, (g + 1) * WINDOW)
                    dqb[:, h * ATT_HEAD_DIM:(h + 1) * ATT_HEAD_DIM] = dqt[:, cols].T
                    dsk[h:h + 1, :] += jnp.broadcast_to(-jnp.sum(sk[:, cols], axis=1, keepdims=True), (1, LANES))
            cs, sl, sr = cos_ref[rows, :], sl_ref[rows, :], sr_ref[rows, :]
            for j in range(ATT_WIDTH // LANES):
                dqkv_ref[rows, j * LANES:(j + 1) * LANES] = _rope_t(dqb[:, j * LANES:(j + 1) * LANES], cs, sl, sr)
            return daw

        daw = _loop_pairs(part * per, per, block, jnp.zeros((1, ATT_WIDTH), F32))
        daw_ref[...] += jnp.broadcast_to(daw, (8, ATT_WIDTH))
        dsink_ref[...] = dsk[...]

        def finish(n, carry):
            r0 = pl.multiple_of(n * WINDOW, WINDOW)
            rows = pl.ds(r0, WINDOW)
            nxt = pl.ds(r0 + WINDOW, WINDOW)
            cs, sl, sr = cos_ref[rows, :], sl_ref[rows, :], sr_ref[rows, :]
            dqkv_ref[rows, ATT_WIDTH:ATT_WIDTH + LANES] = _rope_t(dkpad[nxt, :], cs, sl, sr)
            dqkv_ref[rows, ATT_WIDTH + LANES:QKV] = dvpad[nxt, :]
            return carry

        @pl.when(part == splits - 1)
        def _():
            lax.fori_loop(0, nb, finish, 0)

        _ride_wait(ride_modes, step, B * splits, ride_in, ride_out, sems)

    seq = lambda w, j: pl.BlockSpec((None, T, w), lambda b, s: (b, 0, j))
    full = lambda r, w: pl.BlockSpec((r, w), lambda b, s: (0, 0))
    return pl.pallas_call(
        body, name="attn_bwd", grid=(B, splits),
        in_specs=[seq(ATT_WIDTH, 0), seq(LANES, 0), seq(LANES, 5), seq(ATT_WIDTH, 0), seq(ATT_WIDTH, 0),
                  full(T, LANES), full(T, LANES), full(T, LANES),
                  pl.BlockSpec(memory_space=pltpu.SMEM), full(1, ATT_WIDTH)] + [ANY_SPEC] * nr,
        out_specs=[seq(QKV, 0), pl.BlockSpec((None, 8, LANES), lambda b, s: (b, 0, 0)),
                   pl.BlockSpec((None, 8, ATT_WIDTH), lambda b, s: (b, 0, 0))] + [ANY_SPEC] * nr,
        out_shape=[SDS((B, T, QKV), F32), SDS((B, 8, LANES), F32), SDS((B, 8, ATT_WIDTH), F32)]
        + _exchange_shapes(ride_srcs, ride_modes),
        scratch_shapes=[pltpu.VMEM((T + WINDOW, LANES), BF16), pltpu.VMEM((T + WINDOW, LANES), BF16),
                        pltpu.VMEM((T + WINDOW, LANES), F32), pltpu.VMEM((T + WINDOW, LANES), F32),
                        pltpu.VMEM((WINDOW, ATT_WIDTH), F32), pltpu.VMEM((8, LANES), F32)] + _exchange_sems(nr),
        compiler_params=_params(("arbitrary", "arbitrary"), VMEM_LIMIT_BIG),
    )(qr, kr, proj3, attn_o, dan, cos, sinl, sinr, sinks, attn_w, *ride_srcs)


def _in_bwd(x2, dx1, dqkv, dhq, dhf, dhi, dhg, mod8, pre_w, w_in_bf, T, ride_srcs, ride_modes):
    N = x2.shape[0]
    TM = _tile_rows(T)
    tps = T // TM
    nr = len(ride_srcs)
    pieces = [(0, ATT_WIDTH + 2 * LANES), (768, HG_WIDTH), (1280, HG_WIDTH), (1792, HG_WIDTH), (2304, HG_WIDTH)]

    def body(*refs):
        x_ref, dx_ref, p0, p1, p2, p3, p4, mod_ref, pw_ref, w_ref = refs[:10]
        ride_in = refs[10:10 + nr]
        gx_ref, dproj_ref, acc_ref = refs[10 + nr:13 + nr]
        ride_out = refs[13 + nr:13 + 2 * nr]
        sems = refs[13 + 2 * nr:]
        _ride_start(ride_modes, pl.program_id(0), N // TM, ride_in, ride_out, sems)
        sc1 = mod_ref[1:2, :]
        dh = jnp.zeros((TM, D_MODEL), F32)
        for ref, (off, width) in zip((p0, p1, p2, p3, p4), pieces):
            pb = _bf(ref[...])
            dproj_ref[:, off:off + width] = pb
            dh = dh + _dot(pb, w_ref[off:off + width, :])
        x = x_ref[...]
        r = lax.rsqrt(_mean_last(x * x) + EPS)
        xh = x * r
        n1 = xh * pw_ref[...]
        dsh1 = _sum_rows(dh)
        dsc1 = _sum_rows(dh * n1)
        dn1 = dh * (1.0 + sc1)
        dw_pre = _sum_rows(dn1 * xh)
        dxh = dn1 * pw_ref[...]
        gx_ref[...] = dx_ref[...] + r * (dxh - xh * _mean_last(dxh * xh))
        _acc_rows(acc_ref, pl.program_id(0) % tps == 0, [dsh1, dsc1, dw_pre])
        _ride_wait(ride_modes, pl.program_id(0), N // TM, ride_in, ride_out, sems)

    row = lambda w: pl.BlockSpec((TM, w), lambda i: (i, 0))
    B = N // T
    return pl.pallas_call(
        body, name="in_bwd", grid=(N // TM,),
        in_specs=[row(D_MODEL), row(D_MODEL), row(768), row(HG_WIDTH), row(HG_WIDTH), row(HG_WIDTH),
                  row(HG_WIDTH), _mod_spec(tps), pl.BlockSpec((1, D_MODEL), lambda i: (0, 0)),
                  pl.BlockSpec((IN_COLS, D_MODEL), lambda i: (0, 0))] + [ANY_SPEC] * nr,
        out_specs=[row(D_MODEL), row(IN_COLS), _mod_spec(tps)] + [ANY_SPEC] * nr,
        out_shape=[SDS((N, D_MODEL), F32), SDS((N, IN_COLS), BF16), SDS((B, 8, D_MODEL), F32)]
        + _exchange_shapes(ride_srcs, ride_modes),
        scratch_shapes=_exchange_sems(nr),
        compiler_params=_params(("arbitrary",), VMEM_LIMIT_BIG),
    )(x2, dx1, dqkv, dhq, dhf, dhi, dhg, mod8, pre_w, w_in_bf, *ride_srcs)


def _matmul_tn(name, a, b, tn, tm=512, by_owner_cols=False):
    K, M = a.shape
    Nc = b.shape[1]
    tm = min(tm, M)

    def body(a_ref, b_ref, o_ref):
        o_ref[...] = _bf(_dot_tn(a_ref[...], b_ref[...]))

    if by_owner_cols:
        assert tn * N_DEV == Nc
        out_shape = SDS((N_DEV, M, tn), BF16)
        out_spec = pl.BlockSpec((None, tm, tn), lambda i, j: (j, i, 0))
    else:
        out_shape = SDS((M, Nc), BF16)
        out_spec = pl.BlockSpec((tm, tn), lambda i, j: (i, j))
    return pl.pallas_call(
        body, name=name, grid=(M // tm, Nc // tn),
        in_specs=[pl.BlockSpec((K, tm), lambda i, j: (0, i)),
                  pl.BlockSpec((K, tn), lambda i, j: (0, j))],
        out_specs=out_spec, out_shape=out_shape,
        compiler_params=_params(("arbitrary", "arbitrary"), VMEM_LIMIT_BIG),
    )(a, b)


def _adamw_math(w, g, m, v):
    m2 = ADAM_B1 * m + (1.0 - ADAM_B1) * g
    v2 = ADAM_B2 * v + (1.0 - ADAM_B2) * (g * g)
    m_hat = m2 / (1.0 - ADAM_B1 ** ADAM_STEP)
    v_hat = v2 / (1.0 - ADAM_B2 ** ADAM_STEP)
    delta = -ADAM_LR * (m_hat / (jnp.sqrt(v_hat) + ADAM_EPS) + ADAM_WD * w)
    return delta, m2, v2


def _pair_add(name, gw, theirs):
    chips, _, r, c = gw.shape
    tr = r
    core = lax.axis_index("c").astype(jnp.int32).reshape(1)

    def body(core_ref, mine_ref, theirs_ref, o_ref):
        o_ref[...] = _bf(mine_ref[...].astype(F32) + theirs_ref[...].astype(F32))

    block = pl.BlockSpec((None, tr, c), lambda s, i, core_ref: (s, i, 0))
    grid_spec = pltpu.PrefetchScalarGridSpec(
        num_scalar_prefetch=1, grid=(chips, r // tr),
        in_specs=[pl.BlockSpec((None, None, tr, c), lambda s, i, core_ref: (s, core_ref[0], i, 0)), block],
        out_specs=block)
    return pl.pallas_call(
        body, name=name, grid_spec=grid_spec, out_shape=SDS((chips, r, c), BF16),
        compiler_params=_params(("arbitrary", "arbitrary")),
    )(core, gw, theirs)


def _reduce_adamw(name, parts, w, m, v):
    r, c = w.shape
    tr = r if r % 256 else 256
    slots = parts.shape[0]

    def body(p_ref, w_ref, m_ref, v_ref, g_ref, d_ref, m2_ref, v2_ref):
        g = p_ref[0].astype(F32)
        for s in range(1, slots):
            g = g + p_ref[s].astype(F32)
        g_ref[...] = g
        d_ref[...], m2_ref[...], v2_ref[...] = _adamw_math(w_ref[...], g, m_ref[...], v_ref[...])

    blk = pl.BlockSpec((tr, c), lambda i: (i, 0))
    return pl.pallas_call(
        body, name=name, grid=(r // tr,),
        in_specs=[pl.BlockSpec((slots, tr, c), lambda i: (0, i, 0)), blk, blk, blk],
        out_specs=[blk] * 4, out_shape=[SDS((r, c), F32)] * 4,
        compiler_params=_params(("arbitrary",), VMEM_LIMIT_BIG),
    )(parts, w, m, v)


def _ada_grad_adamw(c_all, dmod_all, w, m, v):
    r, c = w.shape
    tr = 256
    nb = c_all.shape[0]

    def body(c_ref, dm_ref, w_ref, m_ref, v_ref, g_ref, d_ref, m2_ref, v2_ref):
        cv = c_ref[...]
        g = _dot_tn(cv * _sigmoid(cv), dm_ref[...])
        g_ref[...] = g
        d_ref[...], m2_ref[...], v2_ref[...] = _adamw_math(w_ref[...], g, m_ref[...], v_ref[...])

    blk = pl.BlockSpec((tr, c), lambda i: (i, 0))
    return pl.pallas_call(
        body, name="ada_grad_adamw", grid=(r // tr,),
        in_specs=[pl.BlockSpec((nb, tr), lambda i: (0, i)), pl.BlockSpec((nb, c), lambda i: (0, 0)),
                  blk, blk, blk],
        out_specs=[blk] * 4, out_shape=[SDS((r, c), F32)] * 4,
        compiler_params=_params(("arbitrary",)),
    )(c_all, dmod_all, w, m, v)


_SMALL = [("b_ada", 6144), ("pre_w_mix", 1024), ("attn_sinks", 128), ("attn_out_w", 512), ("lb_table", 1024),
          ("hg_norm_w", 128), ("post_w_mix", 1024), ("pre_w_mlp", 1024), ("post_w_mlp", 1024)]


def _pack_small(vals, loss_part):
    out = []
    for name, width in _SMALL:
        f = vals[name].reshape(-1).astype(F32)
        out.append(jnp.pad(f, (0, width - f.shape[0])))
    out.append(jnp.broadcast_to(loss_part, (LANES,)))
    return jnp.concatenate(out).reshape(1, -1)


def _adamw_small(parts, given):
    names = [n for n, _ in _SMALL]
    flat_in = [a for n in names for a in given[n]]

    def body(*refs):
        p_ref = refs[0]
        in_refs = refs[1:1 + 3 * len(names)]
        out_refs = refs[1 + 3 * len(names):-1]
        loss_ref = refs[-1]
        g = p_ref[0]
        for s in range(1, N_DEV):
            g = g + p_ref[s]
        off = 0
        for i, (name, width) in enumerate(_SMALL):
            w_ref, m_ref, v_ref = in_refs[3 * i:3 * i + 3]
            rows, cols = w_ref.shape
            for r in range(rows):
                gr = g[:, off + r * cols:off + (r + 1) * cols]
                res = (gr,) + _adamw_math(w_ref[r:r + 1, :], gr, m_ref[r:r + 1, :], v_ref[r:r + 1, :])
                for o_ref, val in zip(out_refs[4 * i:4 * i + 4], res):
                    o_ref[r:r + 1, :] = val
            off += width
        loss_ref[...] = g[:, off:off + LANES]

    out_shape = [SDS(given[n][0].shape, F32) for n in names for _ in range(4)] + [SDS((1, LANES), F32)]
    outs = pl.pallas_call(body, name="adamw_small", out_shape=out_shape)(parts, *flat_in)
    return {n: tuple(outs[4 * i:4 * i + 4]) for i, n in enumerate(names)}, outs[-1][0, 0]


def kernel(x, c, w_ada, b_ada, pre_w_mix, w_in, attn_sinks, attn_out_w, lb_table, hg_norm_w, w_out, post_w_mix, pre_w_mlp, w_up, w_down, post_w_mlp, loss_target, m_w_ada, m_b_ada, m_pre_w_mix, m_w_in, m_attn_sinks, m_attn_out_w, m_lb_table, m_hg_norm_w, m_w_out, m_post_w_mix, m_pre_w_mlp, m_w_up, m_w_down, m_post_w_mlp, v_w_ada, v_b_ada, v_pre_w_mix, v_w_in, v_attn_sinks, v_attn_out_w, v_lb_table, v_hg_norm_w, v_w_out, v_post_w_mix, v_pre_w_mlp, v_w_up, v_w_down, v_post_w_mlp):
    B, T, _ = x.shape
    N = B * T
    me = 4 * lax.axis_index("x") + 2 * lax.axis_index("y") + lax.axis_index("c")
    x2 = x.reshape(N, D_MODEL)
    tgt2 = loss_target.reshape(N, D_MODEL)

    w_in_t, m_w_in_t, v_w_in_t = w_in[0].T, m_w_in[0].T, v_w_in[0].T
    w_in_g, c_g = _exchange("gather_w_in", [_bf(w_in_t), c], ["gather"] * 2)
    w_in_f = w_in_g.reshape(IN_COLS, D_MODEL)
    c_all = c_g.reshape(N_DEV * B, D_MODEL)

    ada_cols = w_ada.shape[2]
    b_mine = lax.dynamic_slice(b_ada, (0, me * ada_cols), (1, ada_cols))
    mod_cols = _ada_mod(c_all, w_ada[0], b_mine)
    (mod_g,) = _exchange("scatter_mod", [mod_cols.reshape(N_DEV, B, ada_cols)], ["a2a"])
    mod = mod_g.transpose(1, 0, 2).reshape(B, 6, D_MODEL)
    mod8 = jnp.pad(mod, ((0, 0), (0, 2), (0, 0)))

    lb_p = jax.nn.softmax(lb_table, axis=0)
    lb = lb_p[1:2]
    tables = _rope_tables(T)

    w_up_b, w_down_b = _bf(w_up[0]), _bf(w_down[0])
    proj, h1, w_out_g, w_up_g0 = _in_proj(x2, mod8, pre_w_mix, w_in_f, T,
                                          [_bf(w_out[0]), w_up_b[:MLP_HALF]], ["gather"] * 2)
    proj3 = proj.reshape(B, T, IN_COLS)
    rec_o, rec_g, s_prev, w_up_g1, w_down_g0 = _hgrn_fwd(proj3, lb, hg_norm_w,
                                                         [w_up_b[MLP_HALF:], w_down_b[:, :MLP_HALF]], ["gather"] * 2)
    attn_o, attn_n, qr, kr, w_down_g1 = _attn_fwd(proj3, tables, attn_sinks, attn_out_w,
                                                  [w_down_b[:, MLP_HALF:]], ["gather"])
    w_out_f = w_out_g.reshape(D_MODEL, D_MODEL)
    mix, x1, cat = _mix_out(x2, attn_n.reshape(N, ATT_WIDTH), rec_g.reshape(N, HG_WIDTH), mod8,
                            post_w_mix, w_out_f, T, [], [])
    w_up_halves = [w_up_g0, w_up_g1]
    w_down_halves = [w_down_g0.reshape(D_FF, MLP_HALF), w_down_g1.reshape(D_FF, MLP_HALF)]
    up, d, h2 = _mlp_fwd(x1, mod8, pre_w_mlp, w_up_halves, w_down_halves, T)

    dx1, u, dup, dd, acc_mlp = _mlp_bwd(x1, d, up, tgt2, mod8, pre_w_mlp, post_w_mlp,
                                        w_up_halves, w_down_halves, T)
    chips = N_DEV // 2
    by_chip = lambda a: a.reshape((chips, 2, a.shape[0] // N_DEV) + a.shape[1:])
    gw_up = _matmul_tn("grad_w_up", h2, dup, D_FF // N_DEV, by_owner_cols=True)
    gw_up = gw_up.reshape(chips, 2, D_MODEL, D_FF // N_DEV)
    gw_down = by_chip(_matmul_tn("grad_w_down", u, dd, 512))
    dan, drg, dmix, acc_mix, q_down, q_up = _mix_bwd(mix, dx1, mod8, post_w_mix, w_out_f, T,
                                                     [gw_down, gw_up], ["pair"] * 2)
    p_down, p_up = _pair_add("pair_add_w_down", gw_down, q_down), _pair_add("pair_add_w_up", gw_up, q_up)
    gw_out = _matmul_tn("grad_w_out", cat, dmix, 512).reshape(N_DEV, D_MODEL // N_DEV, D_MODEL)
    dhq, dhf, dhi, dhg, dlb_p, dgw_p, r_down, r_out = _hgrn_bwd(
        proj3, lb, hg_norm_w, rec_o, s_prev, drg.reshape(B, T, HG_WIDTH), [p_down, gw_out], ["chips", "a2a"])
    dqkv, dsink_p, daw_p, r_up = _attn_bwd(qr, kr, proj3, attn_o, dan.reshape(B, T, ATT_WIDTH), tables,
                                           attn_sinks, attn_out_w, [p_up], ["chips"])
    flat = lambda a: a.reshape(N, a.shape[-1])
    grad_x, dproj, acc_in = _in_bwd(x2, dx1, flat(dqkv), flat(dhq), flat(dhf), flat(dhi), flat(dhg),
                                    mod8, pre_w_mix, w_in_f, T, [], [])

    gw_in = by_chip(_matmul_tn("grad_w_in", dproj, h1, 512, tm=IN_COLS // 2))
    (q_in,) = _exchange("pair_w_in", [gw_in], ["pair"])
    p_in = _pair_add("pair_add_w_in", gw_in, q_in)

    dmod = jnp.concatenate([acc_in[:, 0:2], acc_mix[:, 0:1], acc_mlp[:, 0:3]], axis=1)
    dlb = dlb_p[:, 0].sum(0)
    dlb_table = jnp.stack([-dlb, dlb]) * (lb_p[0] * lb_p[1])[None, :]
    small = {
        "b_ada": dmod.sum(0),
        "pre_w_mix": acc_in[:, 2].sum(0),
        "attn_sinks": dsink_p[:, :, 0].sum(0),
        "attn_out_w": daw_p[:, 0].sum(0),
        "lb_table": dlb_table,
        "hg_norm_w": dgw_p[:, 0].reshape(B, HG_HEADS, LANES).sum((0, 1)),
        "post_w_mix": acc_mix[:, 1].sum(0),
        "pre_w_mlp": acc_mlp[:, 3].sum(0),
        "post_w_mlp": acc_mlp[:, 4].sum(0),
    }
    loss_part = acc_mlp[:, 5, 0].sum()
    dmod_blocks = dmod.reshape(B, N_DEV, ada_cols).transpose(1, 0, 2)

    r_in, r_dmod, r_small = _exchange(
        "reduce_grads", [p_in, dmod_blocks, _pack_small(small, loss_part)], ["chips", "a2a", "gather"])

    res = {}
    res["w_in"] = tuple(a.T for a in _reduce_adamw("adamw_w_in", r_in, w_in_t, m_w_in_t, v_w_in_t))
    res["w_out"] = _reduce_adamw("adamw_w_out", r_out, w_out[0], m_w_out[0], v_w_out[0])
    res["w_up"] = _reduce_adamw("adamw_w_up", r_up, w_up[0], m_w_up[0], v_w_up[0])
    res["w_down"] = _reduce_adamw("adamw_w_down", r_down, w_down[0], m_w_down[0], v_w_down[0])
    res["w_ada"] = _ada_grad_adamw(c_all, r_dmod.reshape(N_DEV * B, ada_cols), w_ada[0], m_w_ada[0], v_w_ada[0])

    given = dict(b_ada=(b_ada, m_b_ada, v_b_ada), pre_w_mix=(pre_w_mix, m_pre_w_mix, v_pre_w_mix),
                 attn_sinks=(attn_sinks, m_attn_sinks, v_attn_sinks),
                 attn_out_w=(attn_out_w, m_attn_out_w, v_attn_out_w), lb_table=(lb_table, m_lb_table, v_lb_table),
                 hg_norm_w=(hg_norm_w, m_hg_norm_w, v_hg_norm_w), post_w_mix=(post_w_mix, m_post_w_mix, v_post_w_mix),
                 pre_w_mlp=(pre_w_mlp, m_pre_w_mlp, v_pre_w_mlp), post_w_mlp=(post_w_mlp, m_post_w_mlp, v_post_w_mlp))
    small_res, loss = _adamw_small(r_small, given)
    res.update(small_res)

    order = ["w_ada", "b_ada", "pre_w_mix", "w_in", "attn_sinks", "attn_out_w", "lb_table", "hg_norm_w", "w_out",
             "post_w_mix", "pre_w_mlp", "w_up", "w_down", "post_w_mlp"]
    big = {"w_ada", "w_in", "w_out", "w_up", "w_down"}
    outs = [loss, grad_x.reshape(B, T, D_MODEL)]
    for i in range(4):
        for k in order:
            a = res[k][i]
            outs.append(a[None] if k in big else a)
    return tuple(outs)
```

```python
import functools

import jax
import jax.numpy as jnp
from jax import lax
from jax.experimental import pallas as pl
from jax.experimental.pallas import tpu as pltpu

F32 = jnp.float32
BF16 = jnp.bfloat16
SDS = jax.ShapeDtypeStruct

D_MODEL = 1024
ATT_WIDTH = 512
ATT_HEAD_DIM = 64
ATT_KV_HEADS = 2
ATT_GROUP = 4
WINDOW = 128
ROPE_DIM = 16
ROPE_THETA = 500000.0
HG_WIDTH = 512
HG_HEAD_DIM = 128
HG_HEADS = 4
HG_CHUNK = 32
IN_COLS = 2816
D_FF = 4096
EPS = 1e-6
N_DEV = 8

ADAM_LR = 0.001
ADAM_B1 = 0.9
ADAM_B2 = 0.999
ADAM_EPS = 1e-08
ADAM_WD = 0.01
ADAM_STEP = 10

VMEM_LIMIT_BIG = 56 << 20
LANES = 128

MESH = pl.DeviceIdType.MESH
NT_DIMS = (((1,), (1,)), ((), ()))
TN_DIMS = (((0,), (0,)), ((), ()))


def _dot(a, b):
    return jnp.dot(a, b, preferred_element_type=F32)


def _dot_nt(a, b):
    return lax.dot_general(a, b, NT_DIMS, preferred_element_type=F32)


def _dot_tn(a, b):
    return lax.dot_general(a, b, TN_DIMS, preferred_element_type=F32)


def _bf(a):
    return a.astype(BF16)


def _sigmoid(a):
    return 1.0 / (1.0 + jnp.exp(-a))


def _mean_last(a):
    return jnp.mean(a, axis=-1, keepdims=True)


def _sum_rows(a):
    return jnp.sum(a, axis=0, keepdims=True)


def _tri_sum(tri_bf, a, terms=3):
    a1 = _bf(a)
    r1 = a - a1.astype(F32)
    a2 = _bf(r1)
    out = _dot(tri_bf, a1) + _dot(tri_bf, a2)
    if terms == 3:
        out = out + _dot(tri_bf, _bf(r1 - a2.astype(F32)))
    return out


def _loop_pairs(first, count, body, init):
    if count % 2:
        return lax.fori_loop(first, first + count, body, init)
    return lax.fori_loop(0, count // 2, lambda i, c: body(first + 2 * i + 1, body(first + 2 * i, c)), init)


def _params(sem=None, vmem=None):
    kw = {}
    if sem is not None:
        kw["dimension_semantics"] = sem
    if vmem is not None:
        kw["vmem_limit_bytes"] = vmem
    return pltpu.CompilerParams(**kw)


ANY_SPEC = pl.BlockSpec(memory_space=pl.ANY)


def _exchange_shapes(srcs, modes):
    out_shape = []
    for s, m in zip(srcs, modes):
        shp = {"gather": (N_DEV,) + tuple(s.shape), "pair": (s.shape[0],) + tuple(s.shape[2:])}.get(m, tuple(s.shape))
        out_shape.append(SDS(shp, s.dtype))
    return out_shape


def _exchange_sems(n):
    if n == 0:
        return []
    return [pltpu.SemaphoreType.DMA((n, N_DEV - 1)), pltpu.SemaphoreType.DMA((n, N_DEV - 1)),
            pltpu.SemaphoreType.DMA((n,))]


SIBLING = 1
OTHER_CHIPS = (2, 4, 6)


def _related(k):
    x, y, c = lax.axis_index("x"), lax.axis_index("y"), lax.axis_index("c")
    px, py, pc = x ^ ((k >> 2) & 1), y ^ ((k >> 1) & 1), c ^ (k & 1)
    return (px, py, pc), 4 * px + 2 * py + pc


def _exchange_phases(modes, src_refs, out_refs, send_sems, recv_sems, own_sems):
    _, me = _related(0)
    sib_dev, sib = _related(SIBLING)
    start, middle, end = [], [], []

    def remote(a, i, src, dst, dev):
        return pltpu.make_async_remote_copy(src_ref=src, dst_ref=dst, send_sem=send_sems.at[a, i],
                                            recv_sem=recv_sems.at[a, i], device_id=dev, device_id_type=MESH)

    for a, mode in enumerate(modes):
        out = out_refs[a]
        if mode == "gather":
            src = src_refs[a]
            own = pltpu.make_async_copy(src, out.at[me], own_sems.at[a])
            to_sib = remote(a, 0, src, out.at[me], sib_dev)
            start += [own.start, to_sib.start]
            end += [remote(a, 0, src, out.at[sib], sib_dev).wait_recv, to_sib.wait_send, own.wait]
            for j, k in enumerate(OTHER_CHIPS, start=1):
                dev, peer = _related(k)
                _, peer_sib = _related(k ^ SIBLING)
                send = remote(a, j, src, out.at[me], dev)
                passed = remote(a, 3 + j, out.at[peer], out.at[peer], sib_dev)
                start.append(send.start)
                middle += [remote(a, j, src, out.at[peer], dev).wait_recv, passed.start]
                end += [remote(a, 3 + j, out.at[peer_sib], out.at[peer_sib], sib_dev).wait_recv,
                        send.wait_send, passed.wait_send]
        elif mode == "pair":
            core = lax.axis_index("c")
            for s in range(N_DEV // 2):
                send = remote(a, s, src_refs[a].at[s, 1 - core], out.at[s], sib_dev)
                start.append(send.start)
                end += [remote(a, s, src_refs[a].at[s, 1 - core], out.at[s], sib_dev).wait_recv, send.wait_send]
        elif mode == "chips":
            chip = me // 2
            own = pltpu.make_async_copy(src_refs[a].at[chip], out.at[chip], own_sems.at[a])
            start.append(own.start)
            end.append(own.wait)
            for j, k in enumerate(OTHER_CHIPS, start=1):
                dev, peer = _related(k)
                send = remote(a, j, src_refs[a].at[peer // 2], out.at[chip], dev)
                start.append(send.start)
                end += [remote(a, j, src_refs[a].at[peer // 2], out.at[peer // 2], dev).wait_recv, send.wait_send]
        else:
            own = pltpu.make_async_copy(src_refs[a].at[me], out.at[me], own_sems.at[a])
            start.append(own.start)
            end.append(own.wait)
            for k in range(1, N_DEV):
                dev, peer = _related(k)
                send = remote(a, k - 1, src_refs[a].at[peer], out.at[me], dev)
                start.append(send.start)
                end += [remote(a, k - 1, src_refs[a].at[peer], out.at[peer], dev).wait_recv, send.wait_send]
    return start, middle, end


def _run(actions):
    for act in actions:
        act()


def _exchange(name, srcs, modes):
    n = len(srcs)

    def body(*refs):
        start, middle, end = _exchange_phases(modes, refs[:n], refs[n:2 * n], *refs[2 * n:])
        _run(start)
        _run(middle)
        _run(end)

    return pl.pallas_call(
        body, name=name, out_shape=_exchange_shapes(srcs, modes),
        in_specs=[ANY_SPEC] * n, out_specs=[ANY_SPEC] * n,
        scratch_shapes=_exchange_sems(n),
    )(*srcs)


def _ride_start(modes, step, steps, src_refs, out_refs, sems):
    if not modes:
        return
    middle_step = steps - 1

    @pl.when(step == 0)
    def _():
        _run(_exchange_phases(modes, src_refs, out_refs, *sems)[0])

    if "gather" in modes:
        @pl.when(step == middle_step)
        def _():
            _run(_exchange_phases(modes, src_refs, out_refs, *sems)[1])


def _ride_wait(modes, step, steps, src_refs, out_refs, sems):
    if not modes:
        return

    @pl.when(step == steps - 1)
    def _():
        _run(_exchange_phases(modes, src_refs, out_refs, *sems)[2])


def _ada_mod(c_all, w_ada, b_ada_mine):
    nb, cols = c_all.shape[0], w_ada.shape[1]

    def body(c_ref, w_ref, b_ref, o_ref):
        cv = c_ref[...]
        ca = cv * _sigmoid(cv)
        o_ref[...] = _dot(ca, w_ref[...]) + b_ref[...]

    return pl.pallas_call(body, name="ada_mod", out_shape=SDS((nb, cols), F32))(c_all, w_ada, b_ada_mine)


def _tile_rows(T):
    return min(256, T)


def _mod_spec(tps):
    return pl.BlockSpec((None, 8, D_MODEL), lambda i: (i // tps, 0, 0))


def _in_proj(x2, mod8, pre_w, w_in_bf, T, ride_srcs, ride_modes):
    N = x2.shape[0]
    TM = _tile_rows(T)
    tps = T // TM
    nr = len(ride_srcs)

    def body(*refs):
        x_ref, mod_ref, pw_ref, w_ref = refs[:4]
        ride_in = refs[4:4 + nr]
        proj_ref, h1_ref = refs[4 + nr:6 + nr]
        ride_out = refs[6 + nr:6 + 2 * nr]
        sems = refs[6 + 2 * nr:]
        _ride_start(ride_modes, pl.program_id(0), N // TM, ride_in, ride_out, sems)
        x = x_ref[...]
        r = lax.rsqrt(_mean_last(x * x) + EPS)
        h = (x * r * pw_ref[...]) * (1.0 + mod_ref[1:2, :]) + mod_ref[0:1, :]
        hb = _bf(h)
        h1_ref[...] = hb
        proj_ref[...] = _dot_nt(hb, w_ref[...])
        _ride_wait(ride_modes, pl.program_id(0), N // TM, ride_in, ride_out, sems)

    return pl.pallas_call(
        body, name="in_proj", grid=(N // TM,),
        in_specs=[pl.BlockSpec((TM, D_MODEL), lambda i: (i, 0)), _mod_spec(tps),
                  pl.BlockSpec((1, D_MODEL), lambda i: (0, 0)),
                  pl.BlockSpec((IN_COLS, D_MODEL), lambda i: (0, 0))] + [ANY_SPEC] * nr,
        out_specs=[pl.BlockSpec((TM, IN_COLS), lambda i: (i, 0)),
                   pl.BlockSpec((TM, D_MODEL), lambda i: (i, 0))] + [ANY_SPEC] * nr,
        out_shape=[SDS((N, IN_COLS), F32), SDS((N, D_MODEL), BF16)] + _exchange_shapes(ride_srcs, ride_modes),
        scratch_shapes=_exchange_sems(nr),
        compiler_params=_params(("arbitrary",), VMEM_LIMIT_BIG),
    )(x2, mod8, pre_w, w_in_bf, *ride_srcs)


def _rope_tables(T):
    half = ROPE_DIM // 2
    inv_freq = ROPE_THETA ** (-jnp.arange(0, ROPE_DIM, 2, dtype=F32) / ROPE_DIM)
    ang = jnp.arange(T, dtype=F32)[:, None] * inv_freq[None, :]
    cos, sin = jnp.cos(ang), jnp.sin(ang)
    ones = jnp.ones((T, ATT_HEAD_DIM - ROPE_DIM), F32)
    zeros = jnp.zeros((T, ATT_HEAD_DIM - ROPE_DIM), F32)
    zh = jnp.zeros((T, half), F32)
    cos64 = jnp.concatenate([cos, cos, ones], axis=1)
    sin_left = jnp.concatenate([-sin, zh, zeros], axis=1)
    sin_right = jnp.concatenate([zh, sin, zeros], axis=1)
    rep = LANES // ATT_HEAD_DIM
    return jnp.tile(cos64, (1, rep)), jnp.tile(sin_left, (1, rep)), jnp.tile(sin_right, (1, rep))


def _rope(xc, cs, sl, sr):
    return xc * cs + pltpu.roll(xc, LANES - 8, 1) * sl + pltpu.roll(xc, 8, 1) * sr


def _rope_t(dy, cs, sl, sr):
    return dy * cs + pltpu.roll(dy * sl, 8, 1) + pltpu.roll(dy * sr, LANES - 8, 1)


ATT_SCALE = ATT_HEAD_DIM ** -0.5
ATT_SPLITS = 4


def _band_masks():
    cols = ATT_GROUP * WINDOW
    j = lax.broadcasted_iota(jnp.int32, (2 * WINDOW, cols), 0)
    i = lax.broadcasted_iota(jnp.int32, (2 * WINDOW, cols), 1) & (WINDOW - 1)
    diff = i + WINDOW - j
    return (diff >= 0) & (diff < WINDOW), j >= WINDOW


def _sink_row(sink_ref, hk):
    return jnp.concatenate(
        [jnp.full((1, WINDOW), sink_ref[0, ATT_GROUP * hk + g], F32) for g in range(ATT_GROUP)], axis=1)


def _softmax_band(qs, kk, mask, sink):
    s = jnp.where(mask, _dot_nt(kk, qs), jnp.finfo(F32).min)
    m = jnp.maximum(jnp.max(s, axis=0, keepdims=True), sink)
    p = jnp.exp(s - m)
    es = jnp.exp(sink - m)
    inv = 1.0 / (jnp.sum(p, axis=0, keepdims=True) + es)
    return p, inv, es


def _stack_heads(parts, hk):
    hs = []
    for g in range(ATT_GROUP):
        h = ATT_GROUP * hk + g
        hs.append(parts[h // 2][:, (h % 2) * ATT_HEAD_DIM:(h % 2 + 1) * ATT_HEAD_DIM])
    return jnp.concatenate(hs, axis=0)


def _attn_fwd(proj3, tables, sinks, attn_w, ride_srcs, ride_modes):
    B, T, _ = proj3.shape
    nb = T // WINDOW
    splits = min(ATT_SPLITS, nb)
    per = nb // splits
    nr = len(ride_srcs)
    cos, sinl, sinr = tables

    def body(*refs):
        q_ref, k_ref, v_ref, cos_ref, sl_ref, sr_ref, sink_ref, aw_ref = refs[:8]
        ride_in = refs[8:8 + nr]
        o_ref, an_ref, qr_ref, kr_ref = refs[8 + nr:12 + nr]
        ride_out = refs[12 + nr:12 + 2 * nr]
        kpad, vpad = refs[12 + 2 * nr:14 + 2 * nr]
        sems = refs[14 + 2 * nr:]
        part = pl.program_id(1)
        step = pl.program_id(0) * splits + part
        _ride_start(ride_modes, step, B * splits, ride_in, ride_out, sems)

        @pl.when(part == 0)
        def _():
            kpad[0:WINDOW, :] = jnp.zeros((WINDOW, LANES), BF16)
            vpad[0:WINDOW, :] = jnp.zeros((WINDOW, LANES), BF16)

        window, current = _band_masks()

        def block(n, carry):
            r0 = pl.multiple_of(n * WINDOW, WINDOW)
            rows = pl.ds(r0, WINDOW)
            nxt = pl.ds(r0 + WINDOW, WINDOW)
            band = pl.ds(r0, 2 * WINDOW)
            cs, sl, sr = cos_ref[rows, :], sl_ref[rows, :], sr_ref[rows, :]
            kb = _bf(_rope(k_ref[rows, :], cs, sl, sr))
            kpad[nxt, :] = kb
            kr_ref[rows, :] = kb
            vpad[nxt, :] = _bf(v_ref[rows, :])
            qparts = []
            for j in range(ATT_WIDTH // LANES):
                qp = _bf(_rope(q_ref[rows, j * LANES:(j + 1) * LANES], cs, sl, sr) * ATT_SCALE)
                qr_ref[rows, j * LANES:(j + 1) * LANES] = qp
                qparts.append(qp)
            mask = window & (current | (n > 0))
            for hk in range(ATT_KV_HEADS):
                lanes = slice(hk * ATT_HEAD_DIM, (hk + 1) * ATT_HEAD_DIM)
                qs = _stack_heads(qparts, hk)
                p, inv, _ = _softmax_band(qs, kpad[band, lanes], mask, _sink_row(sink_ref, hk))
                ot = _dot_tn(vpad[band, lanes], _bf(p)) * inv
                for g in range(ATT_GROUP):
                    h = ATT_GROUP * hk + g
                    o_ref[rows, h * ATT_HEAD_DIM:(h + 1) * ATT_HEAD_DIM] = ot[:, g * WINDOW:(g + 1) * WINDOW].T
            ob = o_ref[rows, :]
            an_ref[rows, :] = _bf(ob * lax.rsqrt(_mean_last(ob * ob) + EPS) * aw_ref[...])
            return carry

        _loop_pairs(part * per, per, block, 0)
        _ride_wait(ride_modes, step, B * splits, ride_in, ride_out, sems)

    seq = lambda w, j: pl.BlockSpec((None, T, w), lambda b, s: (b, 0, j))
    full = lambda r, w: pl.BlockSpec((r, w), lambda b, s: (0, 0))
    return pl.pallas_call(
        body, name="attn_fwd", grid=(B, splits),
        in_specs=[seq(ATT_WIDTH, 0), seq(LANES, 4), seq(LANES, 5),
                  full(T, LANES), full(T, LANES), full(T, LANES),
                  pl.BlockSpec(memory_space=pltpu.SMEM), full(1, ATT_WIDTH)] + [ANY_SPEC] * nr,
        out_specs=[seq(ATT_WIDTH, 0), seq(ATT_WIDTH, 0), seq(ATT_WIDTH, 0), seq(LANES, 0)] + [ANY_SPEC] * nr,
        out_shape=[SDS((B, T, ATT_WIDTH), F32), SDS((B, T, ATT_WIDTH), BF16),
                   SDS((B, T, ATT_WIDTH), BF16), SDS((B, T, LANES), BF16)] + _exchange_shapes(ride_srcs, ride_modes),
        scratch_shapes=[pltpu.VMEM((T + WINDOW, LANES), BF16), pltpu.VMEM((T + WINDOW, LANES), BF16)]
        + _exchange_sems(nr),
        compiler_params=_params(("arbitrary", "arbitrary"), VMEM_LIMIT_BIG),
    )(proj3, proj3, proj3, cos, sinl, sinr, sinks, attn_w, *ride_srcs)


HG_GROUP = 8
HG_ROWS = HG_GROUP * HG_CHUNK


HG_STACK = HG_GROUP * HG_HEAD_DIM


def _group_masks():
    r = lax.broadcasted_iota(jnp.int32, (HG_ROWS, HG_ROWS), 0)
    c = lax.broadcasted_iota(jnp.int32, (HG_ROWS, HG_ROWS), 1)
    same = (r // HG_CHUNK) == (c // HG_CHUNK)
    return same & (r >= c), same & (c >= r)


def _row_chunk():
    return lax.broadcasted_iota(jnp.int32, (HG_ROWS, HG_HEAD_DIM), 0) // HG_CHUNK


def _spread(a, row_chunk):
    return jnp.concatenate([jnp.where(row_chunk == c, a, jnp.zeros_like(a)) for c in range(HG_GROUP)], axis=1)


def _pick(r, row_chunk):
    out = jnp.where(row_chunk == 0, r[:, :HG_HEAD_DIM], 0.0)
    for c in range(1, HG_GROUP):
        out = out + jnp.where(row_chunk == c, r[:, c * HG_HEAD_DIM:(c + 1) * HG_HEAD_DIM], 0.0)
    return out


def _lane_block(a, c):
    return a[:, c * HG_HEAD_DIM:(c + 1) * HG_HEAD_DIM]


def _ones_bf(mask):
    return jnp.where(mask, 1.0, 0.0).astype(BF16)


def _chunk_bcast(rows_1x128):
    return jnp.concatenate([jnp.broadcast_to(r, (HG_CHUNK, HG_HEAD_DIM)) for r in rows_1x128], axis=0)


def _hgrn_gates(hq, hf, lb, lower_bf):
    sq = _sigmoid(hq)
    q = hq * sq
    sg = _sigmoid(hf)
    f = lb + (1.0 - lb) * sg
    k = 1.0 - f
    logf = jnp.log(f)
    b = _tri_sum(lower_bf, logf)
    bl = [_sum_rows(logf[_chunk_rows(c), :]) for c in range(HG_GROUP)]
    eb, enb, e2 = jnp.exp(b), jnp.exp(-b), jnp.exp(_chunk_bcast(bl) - b)
    ebl = [jnp.exp(r) for r in bl]
    return dict(sq=sq, sg=sg, f=f, eb=eb, enb=enb, e2=e2, ebl=ebl, qd=q * eb, kd=k * enb, k2=k * e2)


def _hgrn_specs(B, T):
    head = lambda base: pl.BlockSpec((None, T, LANES), lambda b, h: (b, 0, base + h))
    return head


def _chunk_rows(c):
    return slice(c * HG_CHUNK, (c + 1) * HG_CHUNK)


def _loop_groups(ng, group, init):
    return _loop_pairs(0, ng, group, init)


def _hgrn_fwd(proj3, lb, hg_w, ride_srcs, ride_modes):
    B, T, _ = proj3.shape
    nc = T // HG_CHUNK
    ng = T // HG_ROWS
    nr = len(ride_srcs)
    head = _hgrn_specs(B, T)

    def body(*refs):
        hq_ref, hf_ref, hi_ref, hg_ref, lb_ref, gw_ref = refs[:6]
        ride_in = refs[6:6 + nr]
        o_ref, rg_ref, sp_ref = refs[6 + nr:9 + nr]
        ride_out = refs[9 + nr:9 + 2 * nr]
        sems = refs[9 + 2 * nr:]
        step = pl.program_id(0) * HG_HEADS + pl.program_id(1)
        _ride_start(ride_modes, step, B * HG_HEADS, ride_in, ride_out, sems)

        lo, _ = _group_masks()
        lower_bf = _ones_bf(lo)
        row_chunk = _row_chunk()
        lbv = lb_ref[...]

        def group(gi, s):
            rows = pl.ds(pl.multiple_of(gi * HG_ROWS, HG_ROWS), HG_ROWS)
            gt = _hgrn_gates(hq_ref[rows, :], hf_ref[rows, :], lbv, lower_bf)
            v, qd, kd = _bf(hi_ref[rows, :]), _bf(gt["qd"]), _bf(gt["kd"])
            a = jnp.where(lo, _dot_nt(qd, kd), 0.0)
            kv = _dot_tn(v, _bf(_spread(gt["k2"], row_chunk)))
            before = []
            for c in range(HG_GROUP):
                before.append(s)
                s = s * gt["ebl"][c] + _lane_block(kv, c)
            sp = jnp.concatenate(before, axis=1)
            sp_ref[gi] = sp
            o = _dot(_bf(a), v) + _dot_nt(_bf(_spread(gt["qd"], row_chunk)), _bf(sp))
            o_ref[rows, :] = o
            hg = hg_ref[rows, :]
            rn = o * lax.rsqrt(_mean_last(o * o) + EPS) * gw_ref[...]
            rg_ref[rows, :] = _bf(rn * (hg * _sigmoid(hg)))
            return s

        _loop_groups(ng, group, jnp.zeros((HG_HEAD_DIM, HG_HEAD_DIM), F32))
        _ride_wait(ride_modes, step, B * HG_HEADS, ride_in, ride_out, sems)

    out_head = pl.BlockSpec((None, T, LANES), lambda b, h: (b, 0, h))
    return pl.pallas_call(
        body, name="hgrn_fwd", grid=(B, HG_HEADS),
        in_specs=[head(6), head(10), head(14), head(18),
                  pl.BlockSpec((1, LANES), lambda b, h: (0, h)),
                  pl.BlockSpec((1, LANES), lambda b, h: (0, 0))] + [ANY_SPEC] * nr,
        out_specs=[out_head, out_head,
                   pl.BlockSpec((None, None, ng, HG_HEAD_DIM, HG_STACK), lambda b, h: (b, h, 0, 0, 0))]
        + [ANY_SPEC] * nr,
        out_shape=[SDS((B, T, HG_WIDTH), F32), SDS((B, T, HG_WIDTH), BF16),
                   SDS((B, HG_HEADS, ng, HG_HEAD_DIM, HG_STACK), F32)] + _exchange_shapes(ride_srcs, ride_modes),
        scratch_shapes=_exchange_sems(nr),
        compiler_params=_params(("arbitrary", "arbitrary"), VMEM_LIMIT_BIG),
    )(proj3, proj3, proj3, proj3, lb, hg_w, *ride_srcs)


def _mix_out(x2, attn_n, rec_g, mod8, post_w, w_out_bf, T, ride_srcs, ride_modes):
    N = x2.shape[0]
    TM = _tile_rows(T)
    tps = T // TM
    nr = len(ride_srcs)

    def body(*refs):
        x_ref, an_ref, rg_ref, mod_ref, pw_ref, w_ref = refs[:6]
        ride_in = refs[6:6 + nr]
        mix_ref, x1_ref, cat_ref = refs[6 + nr:9 + nr]
        ride_out = refs[9 + nr:9 + 2 * nr]
        sems = refs[9 + 2 * nr:]
        _ride_start(ride_modes, pl.program_id(0), N // TM, ride_in, ride_out, sems)
        cat = jnp.concatenate([an_ref[...], rg_ref[...]], axis=1)
        cat_ref[...] = cat
        mix = _dot(cat, w_ref[...])
        mix_ref[...] = mix
        r = lax.rsqrt(_mean_last(mix * mix) + EPS)
        x1_ref[...] = x_ref[...] + mod_ref[2:3, :] * (mix * r * pw_ref[...])
        _ride_wait(ride_modes, pl.program_id(0), N // TM, ride_in, ride_out, sems)

    row = lambda w: pl.BlockSpec((TM, w), lambda i: (i, 0))
    return pl.pallas_call(
        body, name="mix_out", grid=(N // TM,),
        in_specs=[row(D_MODEL), row(ATT_WIDTH), row(HG_WIDTH), _mod_spec(tps),
                  pl.BlockSpec((1, D_MODEL), lambda i: (0, 0)),
                  pl.BlockSpec((D_MODEL, D_MODEL), lambda i: (0, 0))] + [ANY_SPEC] * nr,
        out_specs=[row(D_MODEL), row(D_MODEL), row(D_MODEL)] + [ANY_SPEC] * nr,
        out_shape=[SDS((N, D_MODEL), F32), SDS((N, D_MODEL), F32), SDS((N, D_MODEL), BF16)]
        + _exchange_shapes(ride_srcs, ride_modes),
        scratch_shapes=_exchange_sems(nr),
        compiler_params=_params(("arbitrary",), VMEM_LIMIT_BIG),
    )(x2, attn_n, rec_g, mod8, post_w, w_out_bf, *ride_srcs)


def _load_weights_once(pairs, sem):
    @pl.when(pl.program_id(0) == 0)
    def _():
        cps = [pltpu.make_async_copy(src, dst, sem.at[i]) for i, (src, dst) in enumerate(pairs)]
        for cp in cps:
            cp.start()
        for cp in cps:
            cp.wait()


MLP_HALF = D_MODEL // 2
MLP_PIECES = 2 * N_DEV + 2


def _mlp_weight_pieces(wu_a, wu_b, wd_a, wd_b, wu, wd):
    cols = D_FF // N_DEV
    pairs = []
    for h, half in enumerate((wu_a, wu_b)):
        for j in range(N_DEV):
            pairs.append((half.at[j], wu.at[pl.ds(h * MLP_HALF, MLP_HALF), pl.ds(j * cols, cols)]))
    for h, half in enumerate((wd_a, wd_b)):
        pairs.append((half, wd.at[:, pl.ds(h * MLP_HALF, MLP_HALF)]))
    return pairs


def _mlp_fwd(x1, mod8, pre_w, w_up_halves, w_down_halves, T):
    N = x1.shape[0]
    TM = _tile_rows(T)
    tps = T // TM

    def body(x_ref, mod_ref, pw_ref, wua, wub, wda, wdb, up_ref, d_ref, h2_ref, wu, wd, sem):
        _load_weights_once(_mlp_weight_pieces(wua, wub, wda, wdb, wu, wd), sem)
        x = x_ref[...]
        r = lax.rsqrt(_mean_last(x * x) + EPS)
        h = (x * r * pw_ref[...]) * (1.0 + mod_ref[4:5, :]) + mod_ref[3:4, :]
        hb = _bf(h)
        h2_ref[...] = hb
        up = _dot(hb, wu[...])
        up_ref[...] = up
        ru = jnp.maximum(up, 0.0)
        d_ref[...] = _dot(_bf(ru * ru), wd[...])

    row = lambda w: pl.BlockSpec((TM, w), lambda i: (i, 0))
    return pl.pallas_call(
        body, name="mlp_fwd", grid=(N // TM,),
        in_specs=[row(D_MODEL), _mod_spec(tps), pl.BlockSpec((1, D_MODEL), lambda i: (0, 0))] + [ANY_SPEC] * 4,
        out_specs=[row(D_FF), row(D_MODEL), row(D_MODEL)],
        out_shape=[SDS((N, D_FF), F32), SDS((N, D_MODEL), F32), SDS((N, D_MODEL), BF16)],
        scratch_shapes=[pltpu.VMEM((D_MODEL, D_FF), BF16), pltpu.VMEM((D_FF, D_MODEL), BF16),
                        pltpu.SemaphoreType.DMA((MLP_PIECES,))],
        compiler_params=_params(("arbitrary",), VMEM_LIMIT_BIG),
    )(x1, mod8, pre_w, *w_up_halves, *w_down_halves)


def _acc_rows(acc_ref, first, rows):
    @pl.when(first)
    def _():
        acc_ref[...] = jnp.zeros(acc_ref.shape, F32)
    for i, r in enumerate(rows):
        acc_ref[i:i + 1, :] += r


def _mlp_bwd(x1, d, up, tgt, mod8, pre_w, post_w, w_up_halves, w_down_halves, T):
    N = x1.shape[0]
    TM = _tile_rows(T)
    tps = T // TM

    def body(x_ref, d_ref, up_ref, t_ref, mod_ref, pw_ref, qw_ref, wua, wub, wda, wdb,
             dx_ref, u_ref, dup_ref, dd_ref, acc_ref, wd, wu, sem):
        _load_weights_once(_mlp_weight_pieces(wua, wub, wda, wdb, wu, wd), sem)
        sh2, sc2, g2 = mod_ref[3:4, :], mod_ref[4:5, :], mod_ref[5:6, :]
        x = x_ref[...]
        r1 = lax.rsqrt(_mean_last(x * x) + EPS)
        xh = x * r1
        n2 = xh * pw_ref[...]
        dv = d_ref[...]
        rd = lax.rsqrt(_mean_last(dv * dv) + EPS)
        dh = dv * rd
        rr = dh * qw_ref[...]
        e = x + g2 * rr - t_ref[...]
        loss = 0.5 * jnp.sum(_sum_rows(e * e), axis=1, keepdims=True) / D_MODEL
        dy = e * (1.0 / D_MODEL)
        dg2 = _sum_rows(dy * rr)
        drr = dy * g2
        dw_post = _sum_rows(drr * dh)
        ddh = drr * qw_ref[...]
        dd = _bf(rd * (ddh - dh * _mean_last(ddh * dh)))
        dd_ref[...] = dd
        ru = jnp.maximum(up_ref[...], 0.0)
        u_ref[...] = _bf(ru * ru)
        dup = _bf(_dot_nt(dd, wd[...]) * (2.0 * ru))
        dup_ref[...] = dup
        dh2 = _dot_nt(dup, wu[...])
        dsh2 = _sum_rows(dh2)
        dsc2 = _sum_rows(dh2 * n2)
        dn2 = dh2 * (1.0 + sc2)
        dw_pre = _sum_rows(dn2 * xh)
        dxh = dn2 * pw_ref[...]
        dx_ref[...] = dy + r1 * (dxh - xh * _mean_last(dxh * xh))
        _acc_rows(acc_ref, pl.program_id(0) % tps == 0,
                  [dsh2, dsc2, dg2, dw_pre, dw_post, jnp.broadcast_to(loss, (1, D_MODEL))])

    row = lambda w: pl.BlockSpec((TM, w), lambda i: (i, 0))
    vec = pl.BlockSpec((1, D_MODEL), lambda i: (0, 0))
    B = N // T
    return pl.pallas_call(
        body, name="mlp_bwd", grid=(N // TM,),
        in_specs=[row(D_MODEL), row(D_MODEL), row(D_FF), row(D_MODEL), _mod_spec(tps), vec, vec] + [ANY_SPEC] * 4,
        out_specs=[row(D_MODEL), row(D_FF), row(D_FF), row(D_MODEL), _mod_spec(tps)],
        out_shape=[SDS((N, D_MODEL), F32), SDS((N, D_FF), BF16), SDS((N, D_FF), BF16),
                   SDS((N, D_MODEL), BF16), SDS((B, 8, D_MODEL), F32)],
        scratch_shapes=[pltpu.VMEM((D_FF, D_MODEL), BF16), pltpu.VMEM((D_MODEL, D_FF), BF16),
                        pltpu.SemaphoreType.DMA((MLP_PIECES,))],
        compiler_params=_params(("arbitrary",), VMEM_LIMIT_BIG),
    )(x1, d, up, tgt, mod8, pre_w, post_w, *w_up_halves, *w_down_halves)


def _mix_bwd(mix, dx1, mod8, post_w, w_out_bf, T, ride_srcs, ride_modes):
    N = mix.shape[0]
    TM = _tile_rows(T)
    tps = T // TM
    nr = len(ride_srcs)

    def body(*refs):
        mix_ref, dx_ref, mod_ref, pw_ref, w_ref = refs[:5]
        ride_in = refs[5:5 + nr]
        dan_ref, drg_ref, dmix_ref, acc_ref = refs[5 + nr:9 + nr]
        ride_out = refs[9 + nr:9 + 2 * nr]
        sems = refs[9 + 2 * nr:]
        _ride_start(ride_modes, pl.program_id(0), N // TM, ride_in, ride_out, sems)
        g1 = mod_ref[2:3, :]
        mix = mix_ref[...]
        dx1 = dx_ref[...]
        rm = lax.rsqrt(_mean_last(mix * mix) + EPS)
        mh = mix * rm
        dg1 = _sum_rows(dx1 * (mh * pw_ref[...]))
        dr = dx1 * g1
        dw_post = _sum_rows(dr * mh)
        dmh = dr * pw_ref[...]
        dmix = _bf(rm * (dmh - mh * _mean_last(dmh * mh)))
        dmix_ref[...] = dmix
        dcat = _dot_nt(dmix, w_ref[...])
        dan_ref[...] = dcat[:, :ATT_WIDTH]
        drg_ref[...] = dcat[:, ATT_WIDTH:]
        _acc_rows(acc_ref, pl.program_id(0) % tps == 0, [dg1, dw_post])
        _ride_wait(ride_modes, pl.program_id(0), N // TM, ride_in, ride_out, sems)

    row = lambda w: pl.BlockSpec((TM, w), lambda i: (i, 0))
    B = N // T
    return pl.pallas_call(
        body, name="mix_bwd", grid=(N // TM,),
        in_specs=[row(D_MODEL), row(D_MODEL), _mod_spec(tps), pl.BlockSpec((1, D_MODEL), lambda i: (0, 0)),
                  pl.BlockSpec((D_MODEL, D_MODEL), lambda i: (0, 0))] + [ANY_SPEC] * nr,
        out_specs=[row(ATT_WIDTH), row(HG_WIDTH), row(D_MODEL), _mod_spec(tps)] + [ANY_SPEC] * nr,
        out_shape=[SDS((N, ATT_WIDTH), F32), SDS((N, HG_WIDTH), F32), SDS((N, D_MODEL), BF16),
                   SDS((B, 8, D_MODEL), F32)] + _exchange_shapes(ride_srcs, ride_modes),
        scratch_shapes=_exchange_sems(nr),
        compiler_params=_params(("arbitrary",), VMEM_LIMIT_BIG),
    )(mix, dx1, mod8, post_w, w_out_bf, *ride_srcs)


def _hgrn_bwd(proj3, lb, hg_w, o, s_prev, drg, ride_srcs, ride_modes):
    B, T, _ = proj3.shape
    nc = T // HG_CHUNK
    ng = T // HG_ROWS
    nr = len(ride_srcs)
    head = _hgrn_specs(B, T)

    def body(*refs):
        hq_ref, hf_ref, hi_ref, hg_ref, lb_ref, gw_ref, o_ref, sp_ref, drg_ref = refs[:9]
        ride_in = refs[9:9 + nr]
        dhq_ref, dhf_ref, dhi_ref, dhg_ref, dlb_ref, dgw_ref = refs[9 + nr:15 + nr]
        ride_out = refs[15 + nr:15 + 2 * nr]
        sems = refs[15 + 2 * nr:]
        step = pl.program_id(0) * HG_HEADS + pl.program_id(1)
        _ride_start(ride_modes, step, B * HG_HEADS, ride_in, ride_out, sems)

        lo, up = _group_masks()
        lower_bf, upper_bf = _ones_bf(lo), _ones_bf(up)
        row_chunk = _row_chunk()
        lbv = lb_ref[...]
        gw = gw_ref[...]

        def group(i, carry):
            dlb, dgw, ds = carry
            gi = ng - 1 - i
            rows = pl.ds(pl.multiple_of(gi * HG_ROWS, HG_ROWS), HG_ROWS)
            hq = hq_ref[rows, :]
            gt = _hgrn_gates(hq, hf_ref[rows, :], lbv, lower_bf)
            sq, sg, qdf, kdf, k2f, ebl = gt["sq"], gt["sg"], gt["qd"], gt["kd"], gt["k2"], gt["ebl"]
            v, qd, kd = _bf(hi_ref[rows, :]), _bf(qdf), _bf(kdf)
            ov = o_ref[rows, :]
            hg = hg_ref[rows, :]
            shg = _sigmoid(hg)
            dr = drg_ref[rows, :]
            ro = lax.rsqrt(_mean_last(ov * ov) + EPS)
            oh = ov * ro
            dhg_ref[rows, :] = _bf(dr * (oh * gw) * (shg + hg * shg * (1.0 - shg)))
            drn = dr * (hg * shg)
            dgw = dgw + _sum_rows(drn * oh)
            doh = drn * gw
            do = _bf(ro * (doh - oh * _mean_last(doh * oh)))
            a = jnp.where(lo, _dot_nt(qd, kd), 0.0)
            da = _bf(jnp.where(lo, _dot_nt(do, v), 0.0))
            dv = _dot_tn(_bf(a), do)
            dqd = _dot(da, kd)
            dkd = _dot_tn(da, qd)
            sp = sp_ref[gi]
            incr = _dot_tn(do, _bf(_spread(qdf, row_chunk)))
            after = [None] * HG_GROUP
            for c in reversed(range(HG_GROUP)):
                after[c] = ds
                ds = ds * ebl[c] + _lane_block(incr, c)
            dss = jnp.concatenate(after, axis=1)
            dssb = _bf(dss)
            dk2 = _pick(_dot(v, dssb), row_chunk)
            dhi_ref[rows, :] = _bf(dv + _dot_nt(_bf(_spread(k2f, row_chunk)), dssb))
            dqd = dqd + _pick(_dot(do, _bf(sp)), row_chunk)
            debl = _sum_rows(dss * sp)
            k2g = dk2 * k2f
            db = dqd * qdf - dkd * kdf - k2g
            dk = dkd * gt["enb"] + dk2 * gt["e2"]
            dbl = _chunk_bcast([_lane_block(debl, c) * ebl[c] + _sum_rows(k2g[_chunk_rows(c), :])
                                for c in range(HG_GROUP)])
            dg = _tri_sum(upper_bf, db, terms=2) + dbl
            df = dg / gt["f"] - dk
            dhf_ref[rows, :] = _bf(df * (1.0 - lbv) * sg * (1.0 - sg))
            dlb = dlb + _sum_rows(df * (1.0 - sg))
            dhq_ref[rows, :] = _bf((dqd * gt["eb"]) * (sq + hq * sq * (1.0 - sq)))
            return dlb, dgw, ds

        zero = jnp.zeros((1, LANES), F32)
        dlb, dgw, _ = _loop_groups(ng, group, (zero, zero, jnp.zeros((HG_HEAD_DIM, HG_HEAD_DIM), F32)))
        dlb_ref[...] = jnp.broadcast_to(dlb, (8, LANES))
        dgw_ref[...] = jnp.broadcast_to(dgw, (8, LANES))
        _ride_wait(ride_modes, step, B * HG_HEADS, ride_in, ride_out, sems)

    out_head = pl.BlockSpec((None, T, LANES), lambda b, h: (b, 0, h))
    small = pl.BlockSpec((None, 8, LANES), lambda b, h: (b, 0, h))
    return pl.pallas_call(
        body, name="hgrn_bwd", grid=(B, HG_HEADS),
        in_specs=[head(6), head(10), head(14), head(18),
                  pl.BlockSpec((1, LANES), lambda b, h: (0, h)),
                  pl.BlockSpec((1, LANES), lambda b, h: (0, 0)),
                  out_head,
                  pl.BlockSpec((None, None, ng, HG_HEAD_DIM, HG_STACK), lambda b, h: (b, h, 0, 0, 0)),
                  out_head] + [ANY_SPEC] * nr,
        out_specs=[out_head, out_head, out_head, out_head, small, small] + [ANY_SPEC] * nr,
        out_shape=[SDS((B, T, HG_WIDTH), BF16)] * 4 + [SDS((B, 8, HG_WIDTH), F32)] * 2
        + _exchange_shapes(ride_srcs, ride_modes),
        scratch_shapes=_exchange_sems(nr),
        compiler_params=_params(("arbitrary", "arbitrary"), VMEM_LIMIT_BIG),
    )(proj3, proj3, proj3, proj3, lb, hg_w, o, s_prev, drg, *ride_srcs)


def _attn_bwd(qr, kr, proj3, attn_o, dan, tables, sinks, attn_w, ride_srcs, ride_modes):
    B, T, _ = proj3.shape
    nb = T // WINDOW
    splits = min(ATT_SPLITS, nb)
    per = nb // splits
    nr = len(ride_srcs)
    cos, sinl, sinr = tables
    QKV = ATT_WIDTH + 2 * LANES

    def body(*refs):
        qr_ref, kr_ref, v_ref, o_ref, dan_ref, cos_ref, sl_ref, sr_ref, sink_ref, aw_ref = refs[:10]
        ride_in = refs[10:10 + nr]
        dqkv_ref, dsink_ref, daw_ref = refs[10 + nr:13 + nr]
        ride_out = refs[13 + nr:13 + 2 * nr]
        kpad, vpad, dkpad, dvpad, dqb, dsk = refs[13 + 2 * nr:19 + 2 * nr]
        sems = refs[19 + 2 * nr:]
        part = pl.program_id(1)
        step = pl.program_id(0) * splits + part
        _ride_start(ride_modes, step, B * splits, ride_in, ride_out, sems)

        @pl.when(part == 0)
        def _():
            kpad[0:WINDOW, :] = jnp.zeros((WINDOW, LANES), BF16)
            vpad[0:WINDOW, :] = jnp.zeros((WINDOW, LANES), BF16)
            kpad[WINDOW:, :] = kr_ref[...]
            vpad[WINDOW:, :] = _bf(v_ref[...])
            dkpad[...] = jnp.zeros(dkpad.shape, F32)
            dvpad[...] = jnp.zeros(dvpad.shape, F32)
            dsk[...] = jnp.zeros(dsk.shape, F32)
            daw_ref[...] = jnp.zeros(daw_ref.shape, F32)

        window, current = _band_masks()
        aw = aw_ref[...]

        def block(n, daw):
            r0 = pl.multiple_of(n * WINDOW, WINDOW)
            rows = pl.ds(r0, WINDOW)
            band = pl.ds(r0, 2 * WINDOW)
            ob = o_ref[rows, :]
            dn = dan_ref[rows, :]
            ro = lax.rsqrt(_mean_last(ob * ob) + EPS)
            oh = ob * ro
            daw = daw + _sum_rows(dn * oh)
            doh = dn * aw
            do = _bf(ro * (doh - oh * _mean_last(doh * oh)))
            doparts = [do[:, j * LANES:(j + 1) * LANES] for j in range(ATT_WIDTH // LANES)]
            qparts = [qr_ref[rows, j * LANES:(j + 1) * LANES] for j in range(ATT_WIDTH // LANES)]
            mask = window & (current | (n > 0))
            for hk in range(ATT_KV_HEADS):
                lanes = slice(hk * ATT_HEAD_DIM, (hk + 1) * ATT_HEAD_DIM)
                qs = _stack_heads(qparts, hk)
                dos = _stack_heads(doparts, hk)
                kk, vv = kpad[band, lanes], vpad[band, lanes]
                p, inv, es = _softmax_band(qs, kk, mask, _sink_row(sink_ref, hk))
                p = p * inv
                dp = _dot_nt(vv, dos)
                delta = jnp.sum(p * dp, axis=0, keepdims=True)
                ds = _bf(p * (dp - delta))
                sk = (es * inv) * delta
                dqt = _dot_tn(kk, ds) * ATT_SCALE
                dkpad[band, lanes] += _dot(ds, qs)
                dvpad[band, lanes] += _dot(_bf(p), dos)
                for g in range(ATT_GROUP):
                    h = ATT_GROUP * hk + g
                    cols = slice(g * WINDOW, (g + 1) * WINDOW)
                    dqb[:, h * ATT_HEAD_DIM:(h + 1) * ATT_HEAD_DIM] = dqt[:, cols].T
                    dsk[h:h + 1, :] += jnp.broadcast_to(-jnp.sum(sk[:, cols], axis=1, keepdims=True), (1, LANES))
            cs, sl, sr = cos_ref[rows, :], sl_ref[rows, :], sr_ref[rows, :]
            for j in range(ATT_WIDTH // LANES):
                dqkv_ref[rows, j * LANES:(j + 1) * LANES] = _bf(_rope_t(dqb[:, j * LANES:(j + 1) * LANES], cs, sl, sr))
            return daw

        daw = _loop_pairs(part * per, per, block, jnp.zeros((1, ATT_WIDTH), F32))
        daw_ref[...] += jnp.broadcast_to(daw, (8, ATT_WIDTH))
        dsink_ref[...] = dsk[...]

        def finish(n, carry):
            r0 = pl.multiple_of(n * WINDOW, WINDOW)
            rows = pl.ds(r0, WINDOW)
            nxt = pl.ds(r0 + WINDOW, WINDOW)
            cs, sl, sr = cos_ref[rows, :], sl_ref[rows, :], sr_ref[rows, :]
            dqkv_ref[rows, ATT_WIDTH:ATT_WIDTH + LANES] = _bf(_rope_t(dkpad[nxt, :], cs, sl, sr))
            dqkv_ref[rows, ATT_WIDTH + LANES:QKV] = _bf(dvpad[nxt, :])
            return carry

        @pl.when(part == splits - 1)
        def _():
            lax.fori_loop(0, nb, finish, 0)

        _ride_wait(ride_modes, step, B * splits, ride_in, ride_out, sems)

    seq = lambda w, j: pl.BlockSpec((None, T, w), lambda b, s: (b, 0, j))
    full = lambda r, w: pl.BlockSpec((r, w), lambda b, s: (0, 0))
    return pl.pallas_call(
        body, name="attn_bwd", grid=(B, splits),
        in_specs=[seq(ATT_WIDTH, 0), seq(LANES, 0), seq(LANES, 5), seq(ATT_WIDTH, 0), seq(ATT_WIDTH, 0),
                  full(T, LANES), full(T, LANES), full(T, LANES),
                  pl.BlockSpec(memory_space=pltpu.SMEM), full(1, ATT_WIDTH)] + [ANY_SPEC] * nr,
        out_specs=[seq(QKV, 0), pl.BlockSpec((None, 8, LANES), lambda b, s: (b, 0, 0)),
                   pl.BlockSpec((None, 8, ATT_WIDTH), lambda b, s: (b, 0, 0))] + [ANY_SPEC] * nr,
        out_shape=[SDS((B, T, QKV), BF16), SDS((B, 8, LANES), F32), SDS((B, 8, ATT_WIDTH), F32)]
        + _exchange_shapes(ride_srcs, ride_modes),
        scratch_shapes=[pltpu.VMEM((T + WINDOW, LANES), BF16), pltpu.VMEM((T + WINDOW, LANES), BF16),
                        pltpu.VMEM((T + WINDOW, LANES), F32), pltpu.VMEM((T + WINDOW, LANES), F32),
                        pltpu.VMEM((WINDOW, ATT_WIDTH), F32), pltpu.VMEM((8, LANES), F32)] + _exchange_sems(nr),
        compiler_params=_params(("arbitrary", "arbitrary"), VMEM_LIMIT_BIG),
    )(qr, kr, proj3, attn_o, dan, cos, sinl, sinr, sinks, attn_w, *ride_srcs)


def _in_bwd(x2, dx1, dqkv, dhq, dhf, dhi, dhg, mod8, pre_w, w_in_bf, T, ride_srcs, ride_modes):
    N = x2.shape[0]
    TM = _tile_rows(T)
    tps = T // TM
    nr = len(ride_srcs)
    pieces = [(0, ATT_WIDTH + 2 * LANES), (768, HG_WIDTH), (1280, HG_WIDTH), (1792, HG_WIDTH), (2304, HG_WIDTH)]

    def body(*refs):
        x_ref, dx_ref, p0, p1, p2, p3, p4, mod_ref, pw_ref, w_ref = refs[:10]
        ride_in = refs[10:10 + nr]
        gx_ref, dproj_ref, acc_ref = refs[10 + nr:13 + nr]
        ride_out = refs[13 + nr:13 + 2 * nr]
        sems = refs[13 + 2 * nr:]
        _ride_start(ride_modes, pl.program_id(0), N // TM, ride_in, ride_out, sems)
        sc1 = mod_ref[1:2, :]
        dh = jnp.zeros((TM, D_MODEL), F32)
        for ref, (off, width) in zip((p0, p1, p2, p3, p4), pieces):
            pb = ref[...]
            dproj_ref[:, off:off + width] = pb
            dh = dh + _dot(pb, w_ref[off:off + width, :])
        x = x_ref[...]
        r = lax.rsqrt(_mean_last(x * x) + EPS)
        xh = x * r
        n1 = xh * pw_ref[...]
        dsh1 = _sum_rows(dh)
        dsc1 = _sum_rows(dh * n1)
        dn1 = dh * (1.0 + sc1)
        dw_pre = _sum_rows(dn1 * xh)
        dxh = dn1 * pw_ref[...]
        gx_ref[...] = dx_ref[...] + r * (dxh - xh * _mean_last(dxh * xh))
        _acc_rows(acc_ref, pl.program_id(0) % tps == 0, [dsh1, dsc1, dw_pre])
        _ride_wait(ride_modes, pl.program_id(0), N // TM, ride_in, ride_out, sems)

    row = lambda w: pl.BlockSpec((TM, w), lambda i: (i, 0))
    B = N // T
    return pl.pallas_call(
        body, name="in_bwd", grid=(N // TM,),
        in_specs=[row(D_MODEL), row(D_MODEL), row(768), row(HG_WIDTH), row(HG_WIDTH), row(HG_WIDTH),
                  row(HG_WIDTH), _mod_spec(tps), pl.BlockSpec((1, D_MODEL), lambda i: (0, 0)),
                  pl.BlockSpec((IN_COLS, D_MODEL), lambda i: (0, 0))] + [ANY_SPEC] * nr,
        out_specs=[row(D_MODEL), row(IN_COLS), _mod_spec(tps)] + [ANY_SPEC] * nr,
        out_shape=[SDS((N, D_MODEL), F32), SDS((N, IN_COLS), BF16), SDS((B, 8, D_MODEL), F32)]
        + _exchange_shapes(ride_srcs, ride_modes),
        scratch_shapes=_exchange_sems(nr),
        compiler_params=_params(("arbitrary",), VMEM_LIMIT_BIG),
    )(x2, dx1, dqkv, dhq, dhf, dhi, dhg, mod8, pre_w, w_in_bf, *ride_srcs)


def _matmul_tn(name, a, b, tn, tm=512, by_owner_cols=False):
    K, M = a.shape
    Nc = b.shape[1]
    tm = min(tm, M)

    def body(a_ref, b_ref, o_ref):
        o_ref[...] = _bf(_dot_tn(a_ref[...], b_ref[...]))

    if by_owner_cols:
        assert tn * N_DEV == Nc
        out_shape = SDS((N_DEV, M, tn), BF16)
        out_spec = pl.BlockSpec((None, tm, tn), lambda i, j: (j, i, 0))
    else:
        out_shape = SDS((M, Nc), BF16)
        out_spec = pl.BlockSpec((tm, tn), lambda i, j: (i, j))
    return pl.pallas_call(
        body, name=name, grid=(M // tm, Nc // tn),
        in_specs=[pl.BlockSpec((K, tm), lambda i, j: (0, i)),
                  pl.BlockSpec((K, tn), lambda i, j: (0, j))],
        out_specs=out_spec, out_shape=out_shape,
        compiler_params=_params(("arbitrary", "arbitrary"), VMEM_LIMIT_BIG),
    )(a, b)


def _adamw_math(w, g, m, v):
    m2 = ADAM_B1 * m + (1.0 - ADAM_B1) * g
    v2 = ADAM_B2 * v + (1.0 - ADAM_B2) * (g * g)
    m_hat = m2 / (1.0 - ADAM_B1 ** ADAM_STEP)
    v_hat = v2 / (1.0 - ADAM_B2 ** ADAM_STEP)
    delta = -ADAM_LR * (m_hat / (jnp.sqrt(v_hat) + ADAM_EPS) + ADAM_WD * w)
    return delta, m2, v2


def _pair_add(name, gw, theirs):
    chips, _, r, c = gw.shape
    tr = r
    core = lax.axis_index("c").astype(jnp.int32).reshape(1)

    def body(core_ref, mine_ref, theirs_ref, o_ref):
        o_ref[...] = _bf(mine_ref[...].astype(F32) + theirs_ref[...].astype(F32))

    block = pl.BlockSpec((None, tr, c), lambda s, i, core_ref: (s, i, 0))
    grid_spec = pltpu.PrefetchScalarGridSpec(
        num_scalar_prefetch=1, grid=(chips, r // tr),
        in_specs=[pl.BlockSpec((None, None, tr, c), lambda s, i, core_ref: (s, core_ref[0], i, 0)), block],
        out_specs=block)
    return pl.pallas_call(
        body, name=name, grid_spec=grid_spec, out_shape=SDS((chips, r, c), BF16),
        compiler_params=_params(("arbitrary", "arbitrary")),
    )(core, gw, theirs)


def _reduce_adamw(name, parts, w, m, v):
    r, c = w.shape
    tr = r if r % 256 else 256
    slots = parts.shape[0]

    def body(p_ref, w_ref, m_ref, v_ref, g_ref, d_ref, m2_ref, v2_ref):
        g = p_ref[0].astype(F32)
        for s in range(1, slots):
            g = g + p_ref[s].astype(F32)
        g_ref[...] = g
        d_ref[...], m2_ref[...], v2_ref[...] = _adamw_math(w_ref[...], g, m_ref[...], v_ref[...])

    blk = pl.BlockSpec((tr, c), lambda i: (i, 0))
    return pl.pallas_call(
        body, name=name, grid=(r // tr,),
        in_specs=[pl.BlockSpec((slots, tr, c), lambda i: (0, i, 0)), blk, blk, blk],
        out_specs=[blk] * 4, out_shape=[SDS((r, c), F32)] * 4,
        compiler_params=_params(("arbitrary",), VMEM_LIMIT_BIG),
    )(parts, w, m, v)


def _ada_grad_adamw(c_all, dmod_all, w, m, v):
    r, c = w.shape
    tr = 256
    nb = c_all.shape[0]

    def body(c_ref, dm_ref, w_ref, m_ref, v_ref, g_ref, d_ref, m2_ref, v2_ref):
        cv = c_ref[...]
        g = _dot_tn(cv * _sigmoid(cv), dm_ref[...])
        g_ref[...] = g
        d_ref[...], m2_ref[...], v2_ref[...] = _adamw_math(w_ref[...], g, m_ref[...], v_ref[...])

    blk = pl.BlockSpec((tr, c), lambda i: (i, 0))
    return pl.pallas_call(
        body, name="ada_grad_adamw", grid=(r // tr,),
        in_specs=[pl.BlockSpec((nb, tr), lambda i: (0, i)), pl.BlockSpec((nb, c), lambda i: (0, 0)),
                  blk, blk, blk],
        out_specs=[blk] * 4, out_shape=[SDS((r, c), F32)] * 4,
        compiler_params=_params(("arbitrary",)),
    )(c_all, dmod_all, w, m, v)


_SMALL = [("b_ada", 6144), ("pre_w_mix", 1024), ("attn_sinks", 128), ("attn_out_w", 512), ("lb_table", 1024),
          ("hg_norm_w", 128), ("post_w_mix", 1024), ("pre_w_mlp", 1024), ("post_w_mlp", 1024)]


def _pack_small(vals, loss_part):
    out = []
    for name, width in _SMALL:
        f = vals[name].reshape(-1).astype(F32)
        out.append(jnp.pad(f, (0, width - f.shape[0])))
    out.append(jnp.broadcast_to(loss_part, (LANES,)))
    return jnp.concatenate(out).reshape(1, -1)


def _adamw_small(parts, given):
    names = [n for n, _ in _SMALL]
    flat_in = [a for n in names for a in given[n]]

    def body(*refs):
        p_ref = refs[0]
        in_refs = refs[1:1 + 3 * len(names)]
        out_refs = refs[1 + 3 * len(names):-1]
        loss_ref = refs[-1]
        g = p_ref[0]
        for s in range(1, N_DEV):
            g = g + p_ref[s]
        off = 0
        for i, (name, width) in enumerate(_SMALL):
            w_ref, m_ref, v_ref = in_refs[3 * i:3 * i + 3]
            rows, cols = w_ref.shape
            for r in range(rows):
                gr = g[:, off + r * cols:off + (r + 1) * cols]
                res = (gr,) + _adamw_math(w_ref[r:r + 1, :], gr, m_ref[r:r + 1, :], v_ref[r:r + 1, :])
                for o_ref, val in zip(out_refs[4 * i:4 * i + 4], res):
                    o_ref[r:r + 1, :] = val
            off += width
        loss_ref[...] = g[:, off:off + LANES]

    out_shape = [SDS(given[n][0].shape, F32) for n in names for _ in range(4)] + [SDS((1, LANES), F32)]
    outs = pl.pallas_call(body, name="adamw_small", out_shape=out_shape)(parts, *flat_in)
    return {n: tuple(outs[4 * i:4 * i + 4]) for i, n in enumerate(names)}, outs[-1][0, 0]


def kernel(x, c, w_ada, b_ada, pre_w_mix, w_in, attn_sinks, attn_out_w, lb_table, hg_norm_w, w_out, post_w_mix, pre_w_mlp, w_up, w_down, post_w_mlp, loss_target, m_w_ada, m_b_ada, m_pre_w_mix, m_w_in, m_attn_sinks, m_attn_out_w, m_lb_table, m_hg_norm_w, m_w_out, m_post_w_mix, m_pre_w_mlp, m_w_up, m_w_down, m_post_w_mlp, v_w_ada, v_b_ada, v_pre_w_mix, v_w_in, v_attn_sinks, v_attn_out_w, v_lb_table, v_hg_norm_w, v_w_out, v_post_w_mix, v_pre_w_mlp, v_w_up, v_w_down, v_post_w_mlp):
    B, T, _ = x.shape
    N = B * T
    me = 4 * lax.axis_index("x") + 2 * lax.axis_index("y") + lax.axis_index("c")
    x2 = x.reshape(N, D_MODEL)
    tgt2 = loss_target.reshape(N, D_MODEL)

    w_in_t, m_w_in_t, v_w_in_t = w_in[0].T, m_w_in[0].T, v_w_in[0].T
    w_in_g, c_g = _exchange("gather_w_in", [_bf(w_in_t), c], ["gather"] * 2)
    w_in_f = w_in_g.reshape(IN_COLS, D_MODEL)
    c_all = c_g.reshape(N_DEV * B, D_MODEL)

    ada_cols = w_ada.shape[2]
    b_mine = lax.dynamic_slice(b_ada, (0, me * ada_cols), (1, ada_cols))
    mod_cols = _ada_mod(c_all, w_ada[0], b_mine)
    (mod_g,) = _exchange("scatter_mod", [mod_cols.reshape(N_DEV, B, ada_cols)], ["a2a"])
    mod = mod_g.transpose(1, 0, 2).reshape(B, 6, D_MODEL)
    mod8 = jnp.pad(mod, ((0, 0), (0, 2), (0, 0)))

    lb_p = jax.nn.softmax(lb_table, axis=0)
    lb = lb_p[1:2]
    tables = _rope_tables(T)

    w_up_b, w_down_b = _bf(w_up[0]), _bf(w_down[0])
    proj, h1, w_out_g, w_up_g0 = _in_proj(x2, mod8, pre_w_mix, w_in_f, T,
                                          [_bf(w_out[0]), w_up_b[:MLP_HALF]], ["gather"] * 2)
    proj3 = proj.reshape(B, T, IN_COLS)
    rec_o, rec_g, s_prev, w_up_g1, w_down_g0 = _hgrn_fwd(proj3, lb, hg_norm_w,
                                                         [w_up_b[MLP_HALF:], w_down_b[:, :MLP_HALF]], ["gather"] * 2)
    attn_o, attn_n, qr, kr, w_down_g1 = _attn_fwd(proj3, tables, attn_sinks, attn_out_w,
                                                  [w_down_b[:, MLP_HALF:]], ["gather"])
    w_out_f = w_out_g.reshape(D_MODEL, D_MODEL)
    mix, x1, cat = _mix_out(x2, attn_n.reshape(N, ATT_WIDTH), rec_g.reshape(N, HG_WIDTH), mod8,
                            post_w_mix, w_out_f, T, [], [])
    w_up_halves = [w_up_g0, w_up_g1]
    w_down_halves = [w_down_g0.reshape(D_FF, MLP_HALF), w_down_g1.reshape(D_FF, MLP_HALF)]
    up, d, h2 = _mlp_fwd(x1, mod8, pre_w_mlp, w_up_halves, w_down_halves, T)

    dx1, u, dup, dd, acc_mlp = _mlp_bwd(x1, d, up, tgt2, mod8, pre_w_mlp, post_w_mlp,
                                        w_up_halves, w_down_halves, T)
    chips = N_DEV // 2
    by_chip = lambda a: a.reshape((chips, 2, a.shape[0] // N_DEV) + a.shape[1:])
    gw_up = _matmul_tn("grad_w_up", h2, dup, D_FF // N_DEV, by_owner_cols=True)
    gw_up = gw_up.reshape(chips, 2, D_MODEL, D_FF // N_DEV)
    gw_down = by_chip(_matmul_tn("grad_w_down", u, dd, 512))
    dan, drg, dmix, acc_mix, q_down, q_up = _mix_bwd(mix, dx1, mod8, post_w_mix, w_out_f, T,
                                                     [gw_down, gw_up], ["pair"] * 2)
    p_down, p_up = _pair_add("pair_add_w_down", gw_down, q_down), _pair_add("pair_add_w_up", gw_up, q_up)
    gw_out = _matmul_tn("grad_w_out", cat, dmix, 512).reshape(N_DEV, D_MODEL // N_DEV, D_MODEL)
    dhq, dhf, dhi, dhg, dlb_p, dgw_p, r_down, r_up = _hgrn_bwd(
        proj3, lb, hg_norm_w, rec_o, s_prev, drg.reshape(B, T, HG_WIDTH), [p_down, p_up], ["chips"] * 2)
    dqkv, dsink_p, daw_p, r_out = _attn_bwd(qr, kr, proj3, attn_o, dan.reshape(B, T, ATT_WIDTH), tables,
                                            attn_sinks, attn_out_w, [gw_out], ["a2a"])
    flat = lambda a: a.reshape(N, a.shape[-1])
    grad_x, dproj, acc_in = _in_bwd(x2, dx1, flat(dqkv), flat(dhq), flat(dhf), flat(dhi), flat(dhg),
                                    mod8, pre_w_mix, w_in_f, T, [], [])

    gw_in = by_chip(_matmul_tn("grad_w_in", dproj, h1, 512, tm=IN_COLS // 2))
    (q_in,) = _exchange("pair_w_in", [gw_in], ["pair"])
    p_in = _pair_add("pair_add_w_in", gw_in, q_in)

    dmod = jnp.concatenate([acc_in[:, 0:2], acc_mix[:, 0:1], acc_mlp[:, 0:3]], axis=1)
    dlb = dlb_p[:, 0].sum(0)
    dlb_table = jnp.stack([-dlb, dlb]) * (lb_p[0] * lb_p[1])[None, :]
    small = {
        "b_ada": dmod.sum(0),
        "pre_w_mix": acc_in[:, 2].sum(0),
        "attn_sinks": dsink_p[:, :, 0].sum(0),
        "attn_out_w": daw_p[:, 0].sum(0),
        "lb_table": dlb_table,
        "hg_norm_w": dgw_p[:, 0].reshape(B, HG_HEADS, LANES).sum((0, 1)),
        "post_w_mix": acc_mix[:, 1].sum(0),
        "pre_w_mlp": acc_mlp[:, 3].sum(0),
        "post_w_mlp": acc_mlp[:, 4].sum(0),
    }
    loss_part = acc_mlp[:, 5, 0].sum()
    dmod_blocks = dmod.reshape(B, N_DEV, ada_cols).transpose(1, 0, 2)

    r_in, r_dmod, r_small = _exchange(
        "reduce_grads", [p_in, dmod_blocks, _pack_small(small, loss_part)], ["chips", "a2a", "gather"])

    res = {}
    res["w_in"] = tuple(a.T for a in _reduce_adamw("adamw_w_in", r_in, w_in_t, m_w_in_t, v_w_in_t))
    res["w_out"] = _reduce_adamw("adamw_w_out", r_out, w_out[0], m_w_out[0], v_w_out[0])
    res["w_up"] = _reduce_adamw("adamw_w_up", r_up, w_up[0], m_w_up[0], v_w_up[0])
    res["w_down"] = _reduce_adamw("adamw_w_down", r_down, w_down[0], m_w_down[0], v_w_down[0])
    res["w_ada"] = _ada_grad_adamw(c_all, r_dmod.reshape(N_DEV * B, ada_cols), w_ada[0], m_w_ada[0], v_w_ada[0])

    given = dict(b_ada=(b_ada, m_b_ada, v_b_ada), pre_w_mix=(pre_w_mix, m_pre_w_mix, v_pre_w_mix),
                 attn_sinks=(attn_sinks, m_attn_sinks, v_attn_sinks),
                 attn_out_w=(attn_out_w, m_attn_out_w, v_attn_out_w), lb_table=(lb_table, m_lb_table, v_lb_table),
                 hg_norm_w=(hg_norm_w, m_hg_norm_w, v_hg_norm_w), post_w_mix=(post_w_mix, m_post_w_mix, v_post_w_mix),
                 pre_w_mlp=(pre_w_mlp, m_pre_w_mlp, v_pre_w_mlp), post_w_mlp=(post_w_mlp, m_post_w_mlp, v_post_w_mlp))
    small_res, loss = _adamw_small(r_small, given)
    res.update(small_res)

    order = ["w_ada", "b_ada", "pre_w_mix", "w_in", "attn_sinks", "attn_out_w", "lb_table", "hg_norm_w", "w_out",
             "post_w_mix", "pre_w_mlp", "w_up", "w_down", "post_w_mlp"]
    big = {"w_ada", "w_in", "w_out", "w_up", "w_down"}
    outs = [loss, grad_x.reshape(B, T, D_MODEL)]
    for i in range(4):
        for k in order:
            a = res[k][i]
            outs.append(a[None] if k in big else a)
    return tuple(outs)
```

```python
import functools

import jax
import jax.numpy as jnp
from jax import lax
from jax.experimental import pallas as pl
from jax.experimental.pallas import tpu as pltpu

F32 = jnp.float32
BF16 = jnp.bfloat16
SDS = jax.ShapeDtypeStruct

D_MODEL = 1024
ATT_WIDTH = 512
ATT_HEAD_DIM = 64
ATT_KV_HEADS = 2
ATT_GROUP = 4
WINDOW = 128
ROPE_DIM = 16
ROPE_THETA = 500000.0
HG_WIDTH = 512
HG_HEAD_DIM = 128
HG_HEADS = 4
HG_CHUNK = 32
IN_COLS = 2816
D_FF = 4096
EPS = 1e-6
N_DEV = 8

ADAM_LR = 0.001
ADAM_B1 = 0.9
ADAM_B2 = 0.999
ADAM_EPS = 1e-08
ADAM_WD = 0.01
ADAM_STEP = 10

VMEM_LIMIT_BIG = 56 << 20
LANES = 128

MESH = pl.DeviceIdType.MESH
NT_DIMS = (((1,), (1,)), ((), ()))
TN_DIMS = (((0,), (0,)), ((), ()))


def _dot(a, b):
    return jnp.dot(a, b, preferred_element_type=F32)


def _dot_nt(a, b):
    return lax.dot_general(a, b, NT_DIMS, preferred_element_type=F32)


def _dot_tn(a, b):
    return lax.dot_general(a, b, TN_DIMS, preferred_element_type=F32)


def _bf(a):
    return a.astype(BF16)


def _sigmoid(a):
    return 1.0 / (1.0 + jnp.exp(-a))


def _mean_last(a):
    return jnp.mean(a, axis=-1, keepdims=True)


def _sum_rows(a):
    return jnp.sum(a, axis=0, keepdims=True)


def _tri_sum(tri_bf, a, terms=3):
    a1 = _bf(a)
    r1 = a - a1.astype(F32)
    a2 = _bf(r1)
    out = _dot(tri_bf, a1) + _dot(tri_bf, a2)
    if terms == 3:
        out = out + _dot(tri_bf, _bf(r1 - a2.astype(F32)))
    return out


def _loop_pairs(first, count, body, init):
    if count % 2:
        return lax.fori_loop(first, first + count, body, init)
    return lax.fori_loop(0, count // 2, lambda i, c: body(first + 2 * i + 1, body(first + 2 * i, c)), init)


def _params(sem=None, vmem=None):
    kw = {}
    if sem is not None:
        kw["dimension_semantics"] = sem
    if vmem is not None:
        kw["vmem_limit_bytes"] = vmem
    return pltpu.CompilerParams(**kw)


ANY_SPEC = pl.BlockSpec(memory_space=pl.ANY)


def _exchange_shapes(srcs, modes):
    out_shape = []
    for s, m in zip(srcs, modes):
        shp = {"gather": (N_DEV,) + tuple(s.shape), "pair": (s.shape[0],) + tuple(s.shape[2:])}.get(m, tuple(s.shape))
        out_shape.append(SDS(shp, s.dtype))
    return out_shape


def _exchange_sems(n):
    if n == 0:
        return []
    return [pltpu.SemaphoreType.DMA((n, N_DEV - 1)), pltpu.SemaphoreType.DMA((n, N_DEV - 1)),
            pltpu.SemaphoreType.DMA((n,))]


SIBLING = 1
OTHER_CHIPS = (2, 4, 6)


def _related(k):
    x, y, c = lax.axis_index("x"), lax.axis_index("y"), lax.axis_index("c")
    px, py, pc = x ^ ((k >> 2) & 1), y ^ ((k >> 1) & 1), c ^ (k & 1)
    return (px, py, pc), 4 * px + 2 * py + pc


def _exchange_phases(modes, src_refs, out_refs, send_sems, recv_sems, own_sems):
    _, me = _related(0)
    sib_dev, sib = _related(SIBLING)
    start, middle, end = [], [], []

    def remote(a, i, src, dst, dev):
        return pltpu.make_async_remote_copy(src_ref=src, dst_ref=dst, send_sem=send_sems.at[a, i],
                                            recv_sem=recv_sems.at[a, i], device_id=dev, device_id_type=MESH)

    for a, mode in enumerate(modes):
        out = out_refs[a]
        if mode == "gather":
            src = src_refs[a]
            own = pltpu.make_async_copy(src, out.at[me], own_sems.at[a])
            to_sib = remote(a, 0, src, out.at[me], sib_dev)
            start += [own.start, to_sib.start]
            end += [remote(a, 0, src, out.at[sib], sib_dev).wait_recv, to_sib.wait_send, own.wait]
            for j, k in enumerate(OTHER_CHIPS, start=1):
                dev, peer = _related(k)
                _, peer_sib = _related(k ^ SIBLING)
                send = remote(a, j, src, out.at[me], dev)
                passed = remote(a, 3 + j, out.at[peer], out.at[peer], sib_dev)
                start.append(send.start)
                middle += [remote(a, j, src, out.at[peer], dev).wait_recv, passed.start]
                end += [remote(a, 3 + j, out.at[peer_sib], out.at[peer_sib], sib_dev).wait_recv,
                        send.wait_send, passed.wait_send]
        elif mode == "pair":
            core = lax.axis_index("c")
            for s in range(N_DEV // 2):
                send = remote(a, s, src_refs[a].at[s, 1 - core], out.at[s], sib_dev)
                start.append(send.start)
                end += [remote(a, s, src_refs[a].at[s, 1 - core], out.at[s], sib_dev).wait_recv, send.wait_send]
        elif mode == "chips":
            chip = me // 2
            own = pltpu.make_async_copy(src_refs[a].at[chip], out.at[chip], own_sems.at[a])
            start.append(own.start)
            end.append(own.wait)
            for j, k in enumerate(OTHER_CHIPS, start=1):
                dev, peer = _related(k)
                send = remote(a, j, src_refs[a].at[peer // 2], out.at[chip], dev)
                start.append(send.start)
                end += [remote(a, j, src_refs[a].at[peer // 2], out.at[peer // 2], dev).wait_recv, send.wait_send]
        else:
            own = pltpu.make_async_copy(src_refs[a].at[me], out.at[me], own_sems.at[a])
            start.append(own.start)
            end.append(own.wait)
            for k in range(1, N_DEV):
                dev, peer = _related(k)
                send = remote(a, k - 1, src_refs[a].at[peer], out.at[me], dev)
                start.append(send.start)
                end += [remote(a, k - 1, src_refs[a].at[peer], out.at[peer], dev).wait_recv, send.wait_send]
    return start, middle, end


def _run(actions):
    for act in actions:
        act()


def _exchange(name, srcs, modes):
    n = len(srcs)

    def body(*refs):
        start, middle, end = _exchange_phases(modes, refs[:n], refs[n:2 * n], *refs[2 * n:])
        _run(start)
        _run(middle)
        _run(end)

    return pl.pallas_call(
        body, name=name, out_shape=_exchange_shapes(srcs, modes),
        in_specs=[ANY_SPEC] * n, out_specs=[ANY_SPEC] * n,
        scratch_shapes=_exchange_sems(n),
    )(*srcs)


def _ride_start(modes, step, steps, src_refs, out_refs, sems):
    if not modes:
        return
    middle_step = steps - 1

    @pl.when(step == 0)
    def _():
        _run(_exchange_phases(modes, src_refs, out_refs, *sems)[0])

    if "gather" in modes:
        @pl.when(step == middle_step)
        def _():
            _run(_exchange_phases(modes, src_refs, out_refs, *sems)[1])


def _ride_wait(modes, step, steps, src_refs, out_refs, sems):
    if not modes:
        return

    @pl.when(step == steps - 1)
    def _():
        _run(_exchange_phases(modes, src_refs, out_refs, *sems)[2])


def _ada_mod(c_all, w_ada, b_ada_mine):
    nb, cols = c_all.shape[0], w_ada.shape[1]

    def body(c_ref, w_ref, b_ref, o_ref):
        cv = c_ref[...]
        ca = cv * _sigmoid(cv)
        o_ref[...] = _dot(ca, w_ref[...]) + b_ref[...]

    return pl.pallas_call(body, name="ada_mod", out_shape=SDS((nb, cols), F32))(c_all, w_ada, b_ada_mine)


def _tile_rows(T, big=False):
    return min(512 if big else 256, T)


def _mod_spec(tps):
    return pl.BlockSpec((None, 8, D_MODEL), lambda i: (i // tps, 0, 0))


def _in_proj(x2, mod8, pre_w, w_in_bf, T, ride_srcs, ride_modes):
    N = x2.shape[0]
    TM = _tile_rows(T, big=True)
    tps = T // TM
    nr = len(ride_srcs)

    def body(*refs):
        x_ref, mod_ref, pw_ref, w_ref = refs[:4]
        ride_in = refs[4:4 + nr]
        proj_ref, h1_ref = refs[4 + nr:6 + nr]
        ride_out = refs[6 + nr:6 + 2 * nr]
        sems = refs[6 + 2 * nr:]
        _ride_start(ride_modes, pl.program_id(0), N // TM, ride_in, ride_out, sems)
        x = x_ref[...]
        r = lax.rsqrt(_mean_last(x * x) + EPS)
        h = (x * r * pw_ref[...]) * (1.0 + mod_ref[1:2, :]) + mod_ref[0:1, :]
        hb = _bf(h)
        h1_ref[...] = hb
        proj_ref[...] = _dot_nt(hb, w_ref[...])
        _ride_wait(ride_modes, pl.program_id(0), N // TM, ride_in, ride_out, sems)

    return pl.pallas_call(
        body, name="in_proj", grid=(N // TM,),
        in_specs=[pl.BlockSpec((TM, D_MODEL), lambda i: (i, 0)), _mod_spec(tps),
                  pl.BlockSpec((1, D_MODEL), lambda i: (0, 0)),
                  pl.BlockSpec((IN_COLS, D_MODEL), lambda i: (0, 0))] + [ANY_SPEC] * nr,
        out_specs=[pl.BlockSpec((TM, IN_COLS), lambda i: (i, 0)),
                   pl.BlockSpec((TM, D_MODEL), lambda i: (i, 0))] + [ANY_SPEC] * nr,
        out_shape=[SDS((N, IN_COLS), F32), SDS((N, D_MODEL), BF16)] + _exchange_shapes(ride_srcs, ride_modes),
        scratch_shapes=_exchange_sems(nr),
        compiler_params=_params(("arbitrary",), VMEM_LIMIT_BIG),
    )(x2, mod8, pre_w, w_in_bf, *ride_srcs)


def _rope_tables(T):
    half = ROPE_DIM // 2
    inv_freq = ROPE_THETA ** (-jnp.arange(0, ROPE_DIM, 2, dtype=F32) / ROPE_DIM)
    ang = jnp.arange(T, dtype=F32)[:, None] * inv_freq[None, :]
    cos, sin = jnp.cos(ang), jnp.sin(ang)
    ones = jnp.ones((T, ATT_HEAD_DIM - ROPE_DIM), F32)
    zeros = jnp.zeros((T, ATT_HEAD_DIM - ROPE_DIM), F32)
    zh = jnp.zeros((T, half), F32)
    cos64 = jnp.concatenate([cos, cos, ones], axis=1)
    sin_left = jnp.concatenate([-sin, zh, zeros], axis=1)
    sin_right = jnp.concatenate([zh, sin, zeros], axis=1)
    rep = LANES // ATT_HEAD_DIM
    return jnp.tile(cos64, (1, rep)), jnp.tile(sin_left, (1, rep)), jnp.tile(sin_right, (1, rep))


def _rope(xc, cs, sl, sr):
    return xc * cs + pltpu.roll(xc, LANES - 8, 1) * sl + pltpu.roll(xc, 8, 1) * sr


def _rope_t(dy, cs, sl, sr):
    return dy * cs + pltpu.roll(dy * sl, 8, 1) + pltpu.roll(dy * sr, LANES - 8, 1)


ATT_SCALE = ATT_HEAD_DIM ** -0.5
ATT_SPLITS = 4


def _band_masks():
    cols = ATT_GROUP * WINDOW
    j = lax.broadcasted_iota(jnp.int32, (2 * WINDOW, cols), 0)
    i = lax.broadcasted_iota(jnp.int32, (2 * WINDOW, cols), 1) & (WINDOW - 1)
    diff = i + WINDOW - j
    return (diff >= 0) & (diff < WINDOW), j >= WINDOW


def _sink_row(sink_ref, hk):
    return jnp.concatenate(
        [jnp.full((1, WINDOW), sink_ref[0, ATT_GROUP * hk + g], F32) for g in range(ATT_GROUP)], axis=1)


def _softmax_band(qs, kk, mask, sink):
    s = jnp.where(mask, _dot_nt(kk, qs), jnp.finfo(F32).min)
    m = jnp.maximum(jnp.max(s, axis=0, keepdims=True), sink)
    p = jnp.exp(s - m)
    es = jnp.exp(sink - m)
    inv = 1.0 / (jnp.sum(p, axis=0, keepdims=True) + es)
    return p, inv, es


def _stack_heads(parts, hk):
    hs = []
    for g in range(ATT_GROUP):
        h = ATT_GROUP * hk + g
        hs.append(parts[h // 2][:, (h % 2) * ATT_HEAD_DIM:(h % 2 + 1) * ATT_HEAD_DIM])
    return jnp.concatenate(hs, axis=0)


def _attn_fwd(proj3, tables, sinks, attn_w, ride_srcs, ride_modes):
    B, T, _ = proj3.shape
    nb = T // WINDOW
    splits = min(ATT_SPLITS, nb)
    per = nb // splits
    nr = len(ride_srcs)
    cos, sinl, sinr = tables

    def body(*refs):
        q_ref, k_ref, v_ref, cos_ref, sl_ref, sr_ref, sink_ref, aw_ref = refs[:8]
        ride_in = refs[8:8 + nr]
        o_ref, an_ref, qr_ref, kr_ref = refs[8 + nr:12 + nr]
        ride_out = refs[12 + nr:12 + 2 * nr]
        kpad, vpad = refs[12 + 2 * nr:14 + 2 * nr]
        sems = refs[14 + 2 * nr:]
        part = pl.program_id(1)
        step = pl.program_id(0) * splits + part
        _ride_start(ride_modes, step, B * splits, ride_in, ride_out, sems)

        @pl.when(part == 0)
        def _():
            kpad[0:WINDOW, :] = jnp.zeros((WINDOW, LANES), BF16)
            vpad[0:WINDOW, :] = jnp.zeros((WINDOW, LANES), BF16)

        window, current = _band_masks()

        def block(n, carry):
            r0 = pl.multiple_of(n * WINDOW, WINDOW)
            rows = pl.ds(r0, WINDOW)
            nxt = pl.ds(r0 + WINDOW, WINDOW)
            band = pl.ds(r0, 2 * WINDOW)
            cs, sl, sr = cos_ref[rows, :], sl_ref[rows, :], sr_ref[rows, :]
            kb = _bf(_rope(k_ref[rows, :], cs, sl, sr))
            kpad[nxt, :] = kb
            kr_ref[rows, :] = kb
            vpad[nxt, :] = _bf(v_ref[rows, :])
            qparts = []
            for j in range(ATT_WIDTH // LANES):
                qp = _bf(_rope(q_ref[rows, j * LANES:(j + 1) * LANES], cs, sl, sr) * ATT_SCALE)
                qr_ref[rows, j * LANES:(j + 1) * LANES] = qp
                qparts.append(qp)
            mask = window & (current | (n > 0))
            for hk in range(ATT_KV_HEADS):
                lanes = slice(hk * ATT_HEAD_DIM, (hk + 1) * ATT_HEAD_DIM)
                qs = _stack_heads(qparts, hk)
                p, inv, _ = _softmax_band(qs, kpad[band, lanes], mask, _sink_row(sink_ref, hk))
                ot = _dot_tn(vpad[band, lanes], _bf(p)) * inv
                for g in range(ATT_GROUP):
                    h = ATT_GROUP * hk + g
                    o_ref[rows, h * ATT_HEAD_DIM:(h + 1) * ATT_HEAD_DIM] = ot[:, g * WINDOW:(g + 1) * WINDOW].T
            ob = o_ref[rows, :]
            an_ref[rows, :] = _bf(ob * lax.rsqrt(_mean_last(ob * ob) + EPS) * aw_ref[...])
            return carry

        _loop_pairs(part * per, per, block, 0)
        _ride_wait(ride_modes, step, B * splits, ride_in, ride_out, sems)

    seq = lambda w, j: pl.BlockSpec((None, T, w), lambda b, s: (b, 0, j))
    full = lambda r, w: pl.BlockSpec((r, w), lambda b, s: (0, 0))
    return pl.pallas_call(
        body, name="attn_fwd", grid=(B, splits),
        in_specs=[seq(ATT_WIDTH, 0), seq(LANES, 4), seq(LANES, 5),
                  full(T, LANES), full(T, LANES), full(T, LANES),
                  pl.BlockSpec(memory_space=pltpu.SMEM), full(1, ATT_WIDTH)] + [ANY_SPEC] * nr,
        out_specs=[seq(ATT_WIDTH, 0), seq(ATT_WIDTH, 0), seq(ATT_WIDTH, 0), seq(LANES, 0)] + [ANY_SPEC] * nr,
        out_shape=[SDS((B, T, ATT_WIDTH), F32), SDS((B, T, ATT_WIDTH), BF16),
                   SDS((B, T, ATT_WIDTH), BF16), SDS((B, T, LANES), BF16)] + _exchange_shapes(ride_srcs, ride_modes),
        scratch_shapes=[pltpu.VMEM((T + WINDOW, LANES), BF16), pltpu.VMEM((T + WINDOW, LANES), BF16)]
        + _exchange_sems(nr),
        compiler_params=_params(("arbitrary", "arbitrary"), VMEM_LIMIT_BIG),
    )(proj3, proj3, proj3, cos, sinl, sinr, sinks, attn_w, *ride_srcs)


HG_GROUP = 8
HG_ROWS = HG_GROUP * HG_CHUNK


HG_STACK = HG_GROUP * HG_HEAD_DIM


def _group_masks():
    r = lax.broadcasted_iota(jnp.int32, (HG_ROWS, HG_ROWS), 0)
    c = lax.broadcasted_iota(jnp.int32, (HG_ROWS, HG_ROWS), 1)
    same = (r // HG_CHUNK) == (c // HG_CHUNK)
    return same & (r >= c), same & (c >= r)


def _row_chunk():
    return lax.broadcasted_iota(jnp.int32, (HG_ROWS, HG_HEAD_DIM), 0) // HG_CHUNK


def _spread(a, row_chunk):
    return jnp.concatenate([jnp.where(row_chunk == c, a, jnp.zeros_like(a)) for c in range(HG_GROUP)], axis=1)


def _pick(r, row_chunk):
    out = jnp.where(row_chunk == 0, r[:, :HG_HEAD_DIM], 0.0)
    for c in range(1, HG_GROUP):
        out = out + jnp.where(row_chunk == c, r[:, c * HG_HEAD_DIM:(c + 1) * HG_HEAD_DIM], 0.0)
    return out


def _lane_block(a, c):
    return a[:, c * HG_HEAD_DIM:(c + 1) * HG_HEAD_DIM]


def _ones_bf(mask):
    return jnp.where(mask, 1.0, 0.0).astype(BF16)


def _chunk_bcast(rows_1x128):
    return jnp.concatenate([jnp.broadcast_to(r, (HG_CHUNK, HG_HEAD_DIM)) for r in rows_1x128], axis=0)


def _hgrn_gates(hq, hf, lb, lower_bf):
    sq = _sigmoid(hq)
    q = hq * sq
    sg = _sigmoid(hf)
    f = lb + (1.0 - lb) * sg
    k = 1.0 - f
    logf = jnp.log(f)
    b = _tri_sum(lower_bf, logf)
    bl = [_sum_rows(logf[_chunk_rows(c), :]) for c in range(HG_GROUP)]
    eb, enb, e2 = jnp.exp(b), jnp.exp(-b), jnp.exp(_chunk_bcast(bl) - b)
    ebl = [jnp.exp(r) for r in bl]
    return dict(sq=sq, sg=sg, f=f, eb=eb, enb=enb, e2=e2, ebl=ebl, qd=q * eb, kd=k * enb, k2=k * e2)


def _hgrn_specs(B, T):
    head = lambda base: pl.BlockSpec((None, T, LANES), lambda b, h: (b, 0, base + h))
    return head


def _chunk_rows(c):
    return slice(c * HG_CHUNK, (c + 1) * HG_CHUNK)


def _loop_groups(ng, group, init):
    return _loop_pairs(0, ng, group, init)


def _hgrn_fwd(proj3, lb, hg_w, ride_srcs, ride_modes):
    B, T, _ = proj3.shape
    nc = T // HG_CHUNK
    ng = T // HG_ROWS
    nr = len(ride_srcs)
    head = _hgrn_specs(B, T)

    def body(*refs):
        hq_ref, hf_ref, hi_ref, hg_ref, lb_ref, gw_ref = refs[:6]
        ride_in = refs[6:6 + nr]
        o_ref, rg_ref, sp_ref = refs[6 + nr:9 + nr]
        ride_out = refs[9 + nr:9 + 2 * nr]
        sems = refs[9 + 2 * nr:]
        step = pl.program_id(0) * HG_HEADS + pl.program_id(1)
        _ride_start(ride_modes, step, B * HG_HEADS, ride_in, ride_out, sems)

        lo, _ = _group_masks()
        lower_bf = _ones_bf(lo)
        row_chunk = _row_chunk()
        lbv = lb_ref[...]

        def group(gi, s):
            rows = pl.ds(pl.multiple_of(gi * HG_ROWS, HG_ROWS), HG_ROWS)
            gt = _hgrn_gates(hq_ref[rows, :], hf_ref[rows, :], lbv, lower_bf)
            v, qd, kd = _bf(hi_ref[rows, :]), _bf(gt["qd"]), _bf(gt["kd"])
            a = jnp.where(lo, _dot_nt(qd, kd), 0.0)
            kv = _dot_tn(v, _bf(_spread(gt["k2"], row_chunk)))
            before = []
            for c in range(HG_GROUP):
                before.append(s)
                s = s * gt["ebl"][c] + _lane_block(kv, c)
            sp = jnp.concatenate(before, axis=1)
            sp_ref[gi] = sp
            o = _dot(_bf(a), v) + _dot_nt(_bf(_spread(gt["qd"], row_chunk)), _bf(sp))
            o_ref[rows, :] = o
            hg = hg_ref[rows, :]
            rn = o * lax.rsqrt(_mean_last(o * o) + EPS) * gw_ref[...]
            rg_ref[rows, :] = _bf(rn * (hg * _sigmoid(hg)))
            return s

        _loop_groups(ng, group, jnp.zeros((HG_HEAD_DIM, HG_HEAD_DIM), F32))
        _ride_wait(ride_modes, step, B * HG_HEADS, ride_in, ride_out, sems)

    out_head = pl.BlockSpec((None, T, LANES), lambda b, h: (b, 0, h))
    return pl.pallas_call(
        body, name="hgrn_fwd", grid=(B, HG_HEADS),
        in_specs=[head(6), head(10), head(14), head(18),
                  pl.BlockSpec((1, LANES), lambda b, h: (0, h)),
                  pl.BlockSpec((1, LANES), lambda b, h: (0, 0))] + [ANY_SPEC] * nr,
        out_specs=[out_head, out_head,
                   pl.BlockSpec((None, None, ng, HG_HEAD_DIM, HG_STACK), lambda b, h: (b, h, 0, 0, 0))]
        + [ANY_SPEC] * nr,
        out_shape=[SDS((B, T, HG_WIDTH), F32), SDS((B, T, HG_WIDTH), BF16),
                   SDS((B, HG_HEADS, ng, HG_HEAD_DIM, HG_STACK), F32)] + _exchange_shapes(ride_srcs, ride_modes),
        scratch_shapes=_exchange_sems(nr),
        compiler_params=_params(("arbitrary", "arbitrary"), VMEM_LIMIT_BIG),
    )(proj3, proj3, proj3, proj3, lb, hg_w, *ride_srcs)


def _mix_out(x2, attn_n, rec_g, mod8, post_w, w_out_bf, T, ride_srcs, ride_modes):
    N = x2.shape[0]
    TM = _tile_rows(T, big=True)
    tps = T // TM
    nr = len(ride_srcs)

    def body(*refs):
        x_ref, an_ref, rg_ref, mod_ref, pw_ref, w_ref = refs[:6]
        ride_in = refs[6:6 + nr]
        mix_ref, x1_ref, cat_ref = refs[6 + nr:9 + nr]
        ride_out = refs[9 + nr:9 + 2 * nr]
        sems = refs[9 + 2 * nr:]
        _ride_start(ride_modes, pl.program_id(0), N // TM, ride_in, ride_out, sems)
        cat = jnp.concatenate([an_ref[...], rg_ref[...]], axis=1)
        cat_ref[...] = cat
        mix = _dot(cat, w_ref[...])
        mix_ref[...] = mix
        r = lax.rsqrt(_mean_last(mix * mix) + EPS)
        x1_ref[...] = x_ref[...] + mod_ref[2:3, :] * (mix * r * pw_ref[...])
        _ride_wait(ride_modes, pl.program_id(0), N // TM, ride_in, ride_out, sems)

    row = lambda w: pl.BlockSpec((TM, w), lambda i: (i, 0))
    return pl.pallas_call(
        body, name="mix_out", grid=(N // TM,),
        in_specs=[row(D_MODEL), row(ATT_WIDTH), row(HG_WIDTH), _mod_spec(tps),
                  pl.BlockSpec((1, D_MODEL), lambda i: (0, 0)),
                  pl.BlockSpec((D_MODEL, D_MODEL), lambda i: (0, 0))] + [ANY_SPEC] * nr,
        out_specs=[row(D_MODEL), row(D_MODEL), row(D_MODEL)] + [ANY_SPEC] * nr,
        out_shape=[SDS((N, D_MODEL), F32), SDS((N, D_MODEL), F32), SDS((N, D_MODEL), BF16)]
        + _exchange_shapes(ride_srcs, ride_modes),
        scratch_shapes=_exchange_sems(nr),
        compiler_params=_params(("arbitrary",), VMEM_LIMIT_BIG),
    )(x2, attn_n, rec_g, mod8, post_w, w_out_bf, *ride_srcs)


def _load_weights_once(pairs, sem):
    @pl.when(pl.program_id(0) == 0)
    def _():
        cps = [pltpu.make_async_copy(src, dst, sem.at[i]) for i, (src, dst) in enumerate(pairs)]
        for cp in cps:
            cp.start()
        for cp in cps:
            cp.wait()


MLP_HALF = D_MODEL // 2
MLP_PIECES = 2 * N_DEV + 2


def _mlp_weight_pieces(wu_a, wu_b, wd_a, wd_b, wu, wd):
    cols = D_FF // N_DEV
    pairs = []
    for h, half in enumerate((wu_a, wu_b)):
        for j in range(N_DEV):
            pairs.append((half.at[j], wu.at[pl.ds(h * MLP_HALF, MLP_HALF), pl.ds(j * cols, cols)]))
    for h, half in enumerate((wd_a, wd_b)):
        pairs.append((half, wd.at[:, pl.ds(h * MLP_HALF, MLP_HALF)]))
    return pairs


def _mlp_fwd(x1, mod8, pre_w, w_up_halves, w_down_halves, T):
    N = x1.shape[0]
    TM = _tile_rows(T)
    tps = T // TM

    def body(x_ref, mod_ref, pw_ref, wua, wub, wda, wdb, up_ref, d_ref, h2_ref, wu, wd, sem):
        _load_weights_once(_mlp_weight_pieces(wua, wub, wda, wdb, wu, wd), sem)
        x = x_ref[...]
        r = lax.rsqrt(_mean_last(x * x) + EPS)
        h = (x * r * pw_ref[...]) * (1.0 + mod_ref[4:5, :]) + mod_ref[3:4, :]
        hb = _bf(h)
        h2_ref[...] = hb
        up = _dot(hb, wu[...])
        up_ref[...] = up
        ru = jnp.maximum(up, 0.0)
        d_ref[...] = _dot(_bf(ru * ru), wd[...])

    row = lambda w: pl.BlockSpec((TM, w), lambda i: (i, 0))
    return pl.pallas_call(
        body, name="mlp_fwd", grid=(N // TM,),
        in_specs=[row(D_MODEL), _mod_spec(tps), pl.BlockSpec((1, D_MODEL), lambda i: (0, 0))] + [ANY_SPEC] * 4,
        out_specs=[row(D_FF), row(D_MODEL), row(D_MODEL)],
        out_shape=[SDS((N, D_FF), F32), SDS((N, D_MODEL), F32), SDS((N, D_MODEL), BF16)],
        scratch_shapes=[pltpu.VMEM((D_MODEL, D_FF), BF16), pltpu.VMEM((D_FF, D_MODEL), BF16),
                        pltpu.SemaphoreType.DMA((MLP_PIECES,))],
        compiler_params=_params(("arbitrary",), VMEM_LIMIT_BIG),
    )(x1, mod8, pre_w, *w_up_halves, *w_down_halves)


def _acc_rows(acc_ref, first, rows):
    @pl.when(first)
    def _():
        acc_ref[...] = jnp.zeros(acc_ref.shape, F32)
    for i, r in enumerate(rows):
        acc_ref[i:i + 1, :] += r


def _mlp_bwd(x1, d, up, tgt, mod8, pre_w, post_w, w_up_halves, w_down_halves, T):
    N = x1.shape[0]
    TM = _tile_rows(T)
    tps = T // TM

    def body(x_ref, d_ref, up_ref, t_ref, mod_ref, pw_ref, qw_ref, wua, wub, wda, wdb,
             dx_ref, u_ref, dup_ref, dd_ref, acc_ref, wd, wu, sem):
        _load_weights_once(_mlp_weight_pieces(wua, wub, wda, wdb, wu, wd), sem)
        sh2, sc2, g2 = mod_ref[3:4, :], mod_ref[4:5, :], mod_ref[5:6, :]
        x = x_ref[...]
        r1 = lax.rsqrt(_mean_last(x * x) + EPS)
        xh = x * r1
        n2 = xh * pw_ref[...]
        dv = d_ref[...]
        rd = lax.rsqrt(_mean_last(dv * dv) + EPS)
        dh = dv * rd
        rr = dh * qw_ref[...]
        e = x + g2 * rr - t_ref[...]
        loss = 0.5 * jnp.sum(_sum_rows(e * e), axis=1, keepdims=True) / D_MODEL
        dy = e * (1.0 / D_MODEL)
        dg2 = _sum_rows(dy * rr)
        drr = dy * g2
        dw_post = _sum_rows(drr * dh)
        ddh = drr * qw_ref[...]
        dd = _bf(rd * (ddh - dh * _mean_last(ddh * dh)))
        dd_ref[...] = dd
        ru = jnp.maximum(up_ref[...], 0.0)
        u_ref[...] = _bf(ru * ru)
        dup = _bf(_dot_nt(dd, wd[...]) * (2.0 * ru))
        dup_ref[...] = dup
        dh2 = _dot_nt(dup, wu[...])
        dsh2 = _sum_rows(dh2)
        dsc2 = _sum_rows(dh2 * n2)
        dn2 = dh2 * (1.0 + sc2)
        dw_pre = _sum_rows(dn2 * xh)
        dxh = dn2 * pw_ref[...]
        dx_ref[...] = dy + r1 * (dxh - xh * _mean_last(dxh * xh))
        _acc_rows(acc_ref, pl.program_id(0) % tps == 0,
                  [dsh2, dsc2, dg2, dw_pre, dw_post, jnp.broadcast_to(loss, (1, D_MODEL))])

    row = lambda w: pl.BlockSpec((TM, w), lambda i: (i, 0))
    vec = pl.BlockSpec((1, D_MODEL), lambda i: (0, 0))
    B = N // T
    return pl.pallas_call(
        body, name="mlp_bwd", grid=(N // TM,),
        in_specs=[row(D_MODEL), row(D_MODEL), row(D_FF), row(D_MODEL), _mod_spec(tps), vec, vec] + [ANY_SPEC] * 4,
        out_specs=[row(D_MODEL), row(D_FF), row(D_FF), row(D_MODEL), _mod_spec(tps)],
        out_shape=[SDS((N, D_MODEL), F32), SDS((N, D_FF), BF16), SDS((N, D_FF), BF16),
                   SDS((N, D_MODEL), BF16), SDS((B, 8, D_MODEL), F32)],
        scratch_shapes=[pltpu.VMEM((D_FF, D_MODEL), BF16), pltpu.VMEM((D_MODEL, D_FF), BF16),
                        pltpu.SemaphoreType.DMA((MLP_PIECES,))],
        compiler_params=_params(("arbitrary",), VMEM_LIMIT_BIG),
    )(x1, d, up, tgt, mod8, pre_w, post_w, *w_up_halves, *w_down_halves)


def _mix_bwd(mix, dx1, mod8, post_w, w_out_bf, T, ride_srcs, ride_modes):
    N = mix.shape[0]
    TM = _tile_rows(T, big=True)
    tps = T // TM
    nr = len(ride_srcs)

    def body(*refs):
        mix_ref, dx_ref, mod_ref, pw_ref, w_ref = refs[:5]
        ride_in = refs[5:5 + nr]
        dan_ref, drg_ref, dmix_ref, acc_ref = refs[5 + nr:9 + nr]
        ride_out = refs[9 + nr:9 + 2 * nr]
        sems = refs[9 + 2 * nr:]
        _ride_start(ride_modes, pl.program_id(0), N // TM, ride_in, ride_out, sems)
        g1 = mod_ref[2:3, :]
        mix = mix_ref[...]
        dx1 = dx_ref[...]
        rm = lax.rsqrt(_mean_last(mix * mix) + EPS)
        mh = mix * rm
        dg1 = _sum_rows(dx1 * (mh * pw_ref[...]))
        dr = dx1 * g1
        dw_post = _sum_rows(dr * mh)
        dmh = dr * pw_ref[...]
        dmix = _bf(rm * (dmh - mh * _mean_last(dmh * mh)))
        dmix_ref[...] = dmix
        dcat = _dot_nt(dmix, w_ref[...])
        dan_ref[...] = dcat[:, :ATT_WIDTH]
        drg_ref[...] = dcat[:, ATT_WIDTH:]
        _acc_rows(acc_ref, pl.program_id(0) % tps == 0, [dg1, dw_post])
        _ride_wait(ride_modes, pl.program_id(0), N // TM, ride_in, ride_out, sems)

    row = lambda w: pl.BlockSpec((TM, w), lambda i: (i, 0))
    B = N // T
    return pl.pallas_call(
        body, name="mix_bwd", grid=(N // TM,),
        in_specs=[row(D_MODEL), row(D_MODEL), _mod_spec(tps), pl.BlockSpec((1, D_MODEL), lambda i: (0, 0)),
                  pl.BlockSpec((D_MODEL, D_MODEL), lambda i: (0, 0))] + [ANY_SPEC] * nr,
        out_specs=[row(ATT_WIDTH), row(HG_WIDTH), row(D_MODEL), _mod_spec(tps)] + [ANY_SPEC] * nr,
        out_shape=[SDS((N, ATT_WIDTH), F32), SDS((N, HG_WIDTH), F32), SDS((N, D_MODEL), BF16),
                   SDS((B, 8, D_MODEL), F32)] + _exchange_shapes(ride_srcs, ride_modes),
        scratch_shapes=_exchange_sems(nr),
        compiler_params=_params(("arbitrary",), VMEM_LIMIT_BIG),
    )(mix, dx1, mod8, post_w, w_out_bf, *ride_srcs)


def _hgrn_bwd(proj3, lb, hg_w, o, s_prev, drg, ride_srcs, ride_modes):
    B, T, _ = proj3.shape
    nc = T // HG_CHUNK
    ng = T // HG_ROWS
    nr = len(ride_srcs)
    head = _hgrn_specs(B, T)

    def body(*refs):
        hq_ref, hf_ref, hi_ref, hg_ref, lb_ref, gw_ref, o_ref, sp_ref, drg_ref = refs[:9]
        ride_in = refs[9:9 + nr]
        dhq_ref, dhf_ref, dhi_ref, dhg_ref, dlb_ref, dgw_ref = refs[9 + nr:15 + nr]
        ride_out = refs[15 + nr:15 + 2 * nr]
        sems = refs[15 + 2 * nr:]
        step = pl.program_id(0) * HG_HEADS + pl.program_id(1)
        _ride_start(ride_modes, step, B * HG_HEADS, ride_in, ride_out, sems)

        lo, up = _group_masks()
        lower_bf, upper_bf = _ones_bf(lo), _ones_bf(up)
        row_chunk = _row_chunk()
        lbv = lb_ref[...]
        gw = gw_ref[...]

        def group(i, carry):
            dlb, dgw, ds = carry
            gi = ng - 1 - i
            rows = pl.ds(pl.multiple_of(gi * HG_ROWS, HG_ROWS), HG_ROWS)
            hq = hq_ref[rows, :]
            gt = _hgrn_gates(hq, hf_ref[rows, :], lbv, lower_bf)
            sq, sg, qdf, kdf, k2f, ebl = gt["sq"], gt["sg"], gt["qd"], gt["kd"], gt["k2"], gt["ebl"]
            v, qd, kd = _bf(hi_ref[rows, :]), _bf(qdf), _bf(kdf)
            ov = o_ref[rows, :]
            hg = hg_ref[rows, :]
            shg = _sigmoid(hg)
            dr = drg_ref[rows, :]
            ro = lax.rsqrt(_mean_last(ov * ov) + EPS)
            oh = ov * ro
            dhg_ref[rows, :] = _bf(dr * (oh * gw) * (shg + hg * shg * (1.0 - shg)))
            drn = dr * (hg * shg)
            dgw = dgw + _sum_rows(drn * oh)
            doh = drn * gw
            do = _bf(ro * (doh - oh * _mean_last(doh * oh)))
            a = jnp.where(lo, _dot_nt(qd, kd), 0.0)
            da = _bf(jnp.where(lo, _dot_nt(do, v), 0.0))
            dv = _dot_tn(_bf(a), do)
            dqd = _dot(da, kd)
            dkd = _dot_tn(da, qd)
            sp = sp_ref[gi]
            incr = _dot_tn(do, _bf(_spread(qdf, row_chunk)))
            after = [None] * HG_GROUP
            for c in reversed(range(HG_GROUP)):
                after[c] = ds
                ds = ds * ebl[c] + _lane_block(incr, c)
            dss = jnp.concatenate(after, axis=1)
            dssb = _bf(dss)
            dk2 = _pick(_dot(v, dssb), row_chunk)
            dhi_ref[rows, :] = _bf(dv + _dot_nt(_bf(_spread(k2f, row_chunk)), dssb))
            dqd = dqd + _pick(_dot(do, _bf(sp)), row_chunk)
            debl = _sum_rows(dss * sp)
            k2g = dk2 * k2f
            db = dqd * qdf - dkd * kdf - k2g
            dk = dkd * gt["enb"] + dk2 * gt["e2"]
            dbl = _chunk_bcast([_lane_block(debl, c) * ebl[c] + _sum_rows(k2g[_chunk_rows(c), :])
                                for c in range(HG_GROUP)])
            dg = _tri_sum(upper_bf, db, terms=2) + dbl
            df = dg / gt["f"] - dk
            dhf_ref[rows, :] = _bf(df * (1.0 - lbv) * sg * (1.0 - sg))
            dlb = dlb + _sum_rows(df * (1.0 - sg))
            dhq_ref[rows, :] = _bf((dqd * gt["eb"]) * (sq + hq * sq * (1.0 - sq)))
            return dlb, dgw, ds

        zero = jnp.zeros((1, LANES), F32)
        dlb, dgw, _ = _loop_groups(ng, group, (zero, zero, jnp.zeros((HG_HEAD_DIM, HG_HEAD_DIM), F32)))
        dlb_ref[...] = jnp.broadcast_to(dlb, (8, LANES))
        dgw_ref[...] = jnp.broadcast_to(dgw, (8, LANES))
        _ride_wait(ride_modes, step, B * HG_HEADS, ride_in, ride_out, sems)

    out_head = pl.BlockSpec((None, T, LANES), lambda b, h: (b, 0, h))
    small = pl.BlockSpec((None, 8, LANES), lambda b, h: (b, 0, h))
    return pl.pallas_call(
        body, name="hgrn_bwd", grid=(B, HG_HEADS),
        in_specs=[head(6), head(10), head(14), head(18),
                  pl.BlockSpec((1, LANES), lambda b, h: (0, h)),
                  pl.BlockSpec((1, LANES), lambda b, h: (0, 0)),
                  out_head,
                  pl.BlockSpec((None, None, ng, HG_HEAD_DIM, HG_STACK), lambda b, h: (b, h, 0, 0, 0)),
                  out_head] + [ANY_SPEC] * nr,
        out_specs=[out_head, out_head, out_head, out_head, small, small] + [ANY_SPEC] * nr,
        out_shape=[SDS((B, T, HG_WIDTH), BF16)] * 4 + [SDS((B, 8, HG_WIDTH), F32)] * 2
        + _exchange_shapes(ride_srcs, ride_modes),
        scratch_shapes=_exchange_sems(nr),
        compiler_params=_params(("arbitrary", "arbitrary"), VMEM_LIMIT_BIG),
    )(proj3, proj3, proj3, proj3, lb, hg_w, o, s_prev, drg, *ride_srcs)


def _attn_bwd(qr, kr, proj3, attn_o, dan, tables, sinks, attn_w, ride_srcs, ride_modes):
    B, T, _ = proj3.shape
    nb = T // WINDOW
    splits = min(ATT_SPLITS, nb)
    per = nb // splits
    nr = len(ride_srcs)
    cos, sinl, sinr = tables
    QKV = ATT_WIDTH + 2 * LANES

    def body(*refs):
        qr_ref, kr_ref, v_ref, o_ref, dan_ref, cos_ref, sl_ref, sr_ref, sink_ref, aw_ref = refs[:10]
        ride_in = refs[10:10 + nr]
        dqkv_ref, dsink_ref, daw_ref = refs[10 + nr:13 + nr]
        ride_out = refs[13 + nr:13 + 2 * nr]
        kpad, vpad, dkpad, dvpad, dqb, dsk = refs[13 + 2 * nr:19 + 2 * nr]
        sems = refs[19 + 2 * nr:]
        part = pl.program_id(1)
        step = pl.program_id(0) * splits + part
        _ride_start(ride_modes, step, B * splits, ride_in, ride_out, sems)

        @pl.when(part == 0)
        def _():
            kpad[0:WINDOW, :] = jnp.zeros((WINDOW, LANES), BF16)
            vpad[0:WINDOW, :] = jnp.zeros((WINDOW, LANES), BF16)
            kpad[WINDOW:, :] = kr_ref[...]
            vpad[WINDOW:, :] = _bf(v_ref[...])
            dkpad[...] = jnp.zeros(dkpad.shape, F32)
            dvpad[...] = jnp.zeros(dvpad.shape, F32)
            dsk[...] = jnp.zeros(dsk.shape, F32)
            daw_ref[...] = jnp.zeros(daw_ref.shape, F32)

        window, current = _band_masks()
        aw = aw_ref[...]

        def block(n, daw):
            r0 = pl.multiple_of(n * WINDOW, WINDOW)
            rows = pl.ds(r0, WINDOW)
            band = pl.ds(r0, 2 * WINDOW)
            ob = o_ref[rows, :]
            dn = dan_ref[rows, :]
            ro = lax.rsqrt(_mean_last(ob * ob) + EPS)
            oh = ob * ro
            daw = daw + _sum_rows(dn * oh)
            doh = dn * aw
            do = _bf(ro * (doh - oh * _mean_last(doh * oh)))
            doparts = [do[:, j * LANES:(j + 1) * LANES] for j in range(ATT_WIDTH // LANES)]
            qparts = [qr_ref[rows, j * LANES:(j + 1) * LANES] for j in range(ATT_WIDTH // LANES)]
            mask = window & (current | (n > 0))
            for hk in range(ATT_KV_HEADS):
                lanes = slice(hk * ATT_HEAD_DIM, (hk + 1) * ATT_HEAD_DIM)
                qs = _stack_heads(qparts, hk)
                dos = _stack_heads(doparts, hk)
                kk, vv = kpad[band, lanes], vpad[band, lanes]
                p, inv, es = _softmax_band(qs, kk, mask, _sink_row(sink_ref, hk))
                p = p * inv
                dp = _dot_nt(vv, dos)
                delta = jnp.sum(p * dp, axis=0, keepdims=True)
                ds = _bf(p * (dp - delta))
                sk = (es * inv) * delta
                dqt = _dot_tn(kk, ds) * ATT_SCALE
                dkpad[band, lanes] += _dot(ds, qs)
                dvpad[band, lanes] += _dot(_bf(p), dos)
                for g in range(ATT_GROUP):
                    h = ATT_GROUP * hk + g
                    cols = slice(g * WINDOW, (g + 1) * WINDOW)
                    dqb[:, h * ATT_HEAD_DIM:(h + 1) * ATT_HEAD_DIM] = dqt[:, cols].T
                    dsk[h:h + 1, :] += jnp.broadcast_to(-jnp.sum(sk[:, cols], axis=1, keepdims=True), (1, LANES))
            cs, sl, sr = cos_ref[rows, :], sl_ref[rows, :], sr_ref[rows, :]
            for j in range(ATT_WIDTH // LANES):
                dqkv_ref[rows, j * LANES:(j + 1) * LANES] = _bf(_rope_t(dqb[:, j * LANES:(j + 1) * LANES], cs, sl, sr))
            return daw

        daw = _loop_pairs(part * per, per, block, jnp.zeros((1, ATT_WIDTH), F32))
        daw_ref[...] += jnp.broadcast_to(daw, (8, ATT_WIDTH))
        dsink_ref[...] = dsk[...]

        def finish(n, carry):
            r0 = pl.multiple_of(n * WINDOW, WINDOW)
            rows = pl.ds(r0, WINDOW)
            nxt = pl.ds(r0 + WINDOW, WINDOW)
            cs, sl, sr = cos_ref[rows, :], sl_ref[rows, :], sr_ref[rows, :]
            dqkv_ref[rows, ATT_WIDTH:ATT_WIDTH + LANES] = _bf(_rope_t(dkpad[nxt, :], cs, sl, sr))
            dqkv_ref[rows, ATT_WIDTH + LANES:QKV] = _bf(dvpad[nxt, :])
            return carry

        @pl.when(part == splits - 1)
        def _():
            lax.fori_loop(0, nb, finish, 0)

        _ride_wait(ride_modes, step, B * splits, ride_in, ride_out, sems)

    seq = lambda w, j: pl.BlockSpec((None, T, w), lambda b, s: (b, 0, j))
    full = lambda r, w: pl.BlockSpec((r, w), lambda b, s: (0, 0))
    return pl.pallas_call(
        body, name="attn_bwd", grid=(B, splits),
        in_specs=[seq(ATT_WIDTH, 0), seq(LANES, 0), seq(LANES, 5), seq(ATT_WIDTH, 0), seq(ATT_WIDTH, 0),
                  full(T, LANES), full(T, LANES), full(T, LANES),
                  pl.BlockSpec(memory_space=pltpu.SMEM), full(1, ATT_WIDTH)] + [ANY_SPEC] * nr,
        out_specs=[seq(QKV, 0), pl.BlockSpec((None, 8, LANES), lambda b, s: (b, 0, 0)),
                   pl.BlockSpec((None, 8, ATT_WIDTH), lambda b, s: (b, 0, 0))] + [ANY_SPEC] * nr,
        out_shape=[SDS((B, T, QKV), BF16), SDS((B, 8, LANES), F32), SDS((B, 8, ATT_WIDTH), F32)]
        + _exchange_shapes(ride_srcs, ride_modes),
        scratch_shapes=[pltpu.VMEM((T + WINDOW, LANES), BF16), pltpu.VMEM((T + WINDOW, LANES), BF16),
                        pltpu.VMEM((T + WINDOW, LANES), F32), pltpu.VMEM((T + WINDOW, LANES), F32),
                        pltpu.VMEM((WINDOW, ATT_WIDTH), F32), pltpu.VMEM((8, LANES), F32)] + _exchange_sems(nr),
        compiler_params=_params(("arbitrary", "arbitrary"), VMEM_LIMIT_BIG),
    )(qr, kr, proj3, attn_o, dan, cos, sinl, sinr, sinks, attn_w, *ride_srcs)


def _in_bwd(x2, dx1, dqkv, dhq, dhf, dhi, dhg, mod8, pre_w, w_in_bf, T, ride_srcs, ride_modes):
    N = x2.shape[0]
    TM = _tile_rows(T, big=True)
    tps = T // TM
    nr = len(ride_srcs)
    pieces = [(0, ATT_WIDTH + 2 * LANES), (768, HG_WIDTH), (1280, HG_WIDTH), (1792, HG_WIDTH), (2304, HG_WIDTH)]

    def body(*refs):
        x_ref, dx_ref, p0, p1, p2, p3, p4, mod_ref, pw_ref, w_ref = refs[:10]
        ride_in = refs[10:10 + nr]
        gx_ref, dproj_ref, acc_ref = refs[10 + nr:13 + nr]
        ride_out = refs[13 + nr:13 + 2 * nr]
        sems = refs[13 + 2 * nr:]
        _ride_start(ride_modes, pl.program_id(0), N // TM, ride_in, ride_out, sems)
        sc1 = mod_ref[1:2, :]
        dh = jnp.zeros((TM, D_MODEL), F32)
        for ref, (off, width) in zip((p0, p1, p2, p3, p4), pieces):
            pb = ref[...]
            dproj_ref[:, off:off + width] = pb
            dh = dh + _dot(pb, w_ref[off:off + width, :])
        x = x_ref[...]
        r = lax.rsqrt(_mean_last(x * x) + EPS)
        xh = x * r
        n1 = xh * pw_ref[...]
        dsh1 = _sum_rows(dh)
        dsc1 = _sum_rows(dh * n1)
        dn1 = dh * (1.0 + sc1)
        dw_pre = _sum_rows(dn1 * xh)
        dxh = dn1 * pw_ref[...]
        gx_ref[...] = dx_ref[...] + r * (dxh - xh * _mean_last(dxh * xh))
        _acc_rows(acc_ref, pl.program_id(0) % tps == 0, [dsh1, dsc1, dw_pre])
        _ride_wait(ride_modes, pl.program_id(0), N // TM, ride_in, ride_out, sems)

    row = lambda w: pl.BlockSpec((TM, w), lambda i: (i, 0))
    B = N // T
    return pl.pallas_call(
        body, name="in_bwd", grid=(N // TM,),
        in_specs=[row(D_MODEL), row(D_MODEL), row(768), row(HG_WIDTH), row(HG_WIDTH), row(HG_WIDTH),
                  row(HG_WIDTH), _mod_spec(tps), pl.BlockSpec((1, D_MODEL), lambda i: (0, 0)),
                  pl.BlockSpec((IN_COLS, D_MODEL), lambda i: (0, 0))] + [ANY_SPEC] * nr,
        out_specs=[row(D_MODEL), row(IN_COLS), _mod_spec(tps)] + [ANY_SPEC] * nr,
        out_shape=[SDS((N, D_MODEL), F32), SDS((N, IN_COLS), BF16), SDS((B, 8, D_MODEL), F32)]
        + _exchange_shapes(ride_srcs, ride_modes),
        scratch_shapes=_exchange_sems(nr),
        compiler_params=_params(("arbitrary",), VMEM_LIMIT_BIG),
    )(x2, dx1, dqkv, dhq, dhf, dhi, dhg, mod8, pre_w, w_in_bf, *ride_srcs)


def _matmul_tn(name, a, b, tn, tm=512, by_owner_cols=False):
    K, M = a.shape
    Nc = b.shape[1]
    tm = min(tm, M)

    def body(a_ref, b_ref, o_ref):
        o_ref[...] = _bf(_dot_tn(a_ref[...], b_ref[...]))

    if by_owner_cols:
        assert tn * N_DEV == Nc
        out_shape = SDS((N_DEV, M, tn), BF16)
        out_spec = pl.BlockSpec((None, tm, tn), lambda i, j: (j, i, 0))
    else:
        out_shape = SDS((M, Nc), BF16)
        out_spec = pl.BlockSpec((tm, tn), lambda i, j: (i, j))
    return pl.pallas_call(
        body, name=name, grid=(M // tm, Nc // tn),
        in_specs=[pl.BlockSpec((K, tm), lambda i, j: (0, i)),
                  pl.BlockSpec((K, tn), lambda i, j: (0, j))],
        out_specs=out_spec, out_shape=out_shape,
        compiler_params=_params(("arbitrary", "arbitrary"), VMEM_LIMIT_BIG),
    )(a, b)


def _adamw_math(w, g, m, v):
    m2 = ADAM_B1 * m + (1.0 - ADAM_B1) * g
    v2 = ADAM_B2 * v + (1.0 - ADAM_B2) * (g * g)
    m_hat = m2 / (1.0 - ADAM_B1 ** ADAM_STEP)
    v_hat = v2 / (1.0 - ADAM_B2 ** ADAM_STEP)
    delta = -ADAM_LR * (m_hat / (jnp.sqrt(v_hat) + ADAM_EPS) + ADAM_WD * w)
    return delta, m2, v2


def _pair_add(name, gw, theirs):
    chips, _, r, c = gw.shape
    tr = r
    core = lax.axis_index("c").astype(jnp.int32).reshape(1)

    def body(core_ref, mine_ref, theirs_ref, o_ref):
        o_ref[...] = _bf(mine_ref[...].astype(F32) + theirs_ref[...].astype(F32))

    block = pl.BlockSpec((None, tr, c), lambda s, i, core_ref: (s, i, 0))
    grid_spec = pltpu.PrefetchScalarGridSpec(
        num_scalar_prefetch=1, grid=(chips, r // tr),
        in_specs=[pl.BlockSpec((None, None, tr, c), lambda s, i, core_ref: (s, core_ref[0], i, 0)), block],
        out_specs=block)
    return pl.pallas_call(
        body, name=name, grid_spec=grid_spec, out_shape=SDS((chips, r, c), BF16),
        compiler_params=_params(("arbitrary", "arbitrary")),
    )(core, gw, theirs)


def _reduce_adamw(name, parts, w, m, v):
    r, c = w.shape
    tr = r if r % 256 else 256
    slots = parts.shape[0]

    def body(p_ref, w_ref, m_ref, v_ref, g_ref, d_ref, m2_ref, v2_ref):
        g = p_ref[0].astype(F32)
        for s in range(1, slots):
            g = g + p_ref[s].astype(F32)
        g_ref[...] = g
        d_ref[...], m2_ref[...], v2_ref[...] = _adamw_math(w_ref[...], g, m_ref[...], v_ref[...])

    blk = pl.BlockSpec((tr, c), lambda i: (i, 0))
    return pl.pallas_call(
        body, name=name, grid=(r // tr,),
        in_specs=[pl.BlockSpec((slots, tr, c), lambda i: (0, i, 0)), blk, blk, blk],
        out_specs=[blk] * 4, out_shape=[SDS((r, c), F32)] * 4,
        compiler_params=_params(("arbitrary",), VMEM_LIMIT_BIG),
    )(parts, w, m, v)


def _ada_grad_adamw(c_all, dmod_all, w, m, v):
    r, c = w.shape
    tr = 256
    nb = c_all.shape[0]

    def body(c_ref, dm_ref, w_ref, m_ref, v_ref, g_ref, d_ref, m2_ref, v2_ref):
        cv = c_ref[...]
        g = _dot_tn(cv * _sigmoid(cv), dm_ref[...])
        g_ref[...] = g
        d_ref[...], m2_ref[...], v2_ref[...] = _adamw_math(w_ref[...], g, m_ref[...], v_ref[...])

    blk = pl.BlockSpec((tr, c), lambda i: (i, 0))
    return pl.pallas_call(
        body, name="ada_grad_adamw", grid=(r // tr,),
        in_specs=[pl.BlockSpec((nb, tr), lambda i: (0, i)), pl.BlockSpec((nb, c), lambda i: (0, 0)),
                  blk, blk, blk],
        out_specs=[blk] * 4, out_shape=[SDS((r, c), F32)] * 4,
        compiler_params=_params(("arbitrary",)),
    )(c_all, dmod_all, w, m, v)


_SMALL = [("b_ada", 6144), ("pre_w_mix", 1024), ("attn_sinks", 128), ("attn_out_w", 512), ("lb_table", 1024),
          ("hg_norm_w", 128), ("post_w_mix", 1024), ("pre_w_mlp", 1024), ("post_w_mlp", 1024)]


def _pack_small(vals, loss_part):
    out = []
    for name, width in _SMALL:
        f = vals[name].reshape(-1).astype(F32)
        out.append(jnp.pad(f, (0, width - f.shape[0])))
    out.append(jnp.broadcast_to(loss_part, (LANES,)))
    return jnp.concatenate(out).reshape(1, -1)


def _adamw_small(parts, given):
    names = [n for n, _ in _SMALL]
    flat_in = [a for n in names for a in given[n]]

    def body(*refs):
        p_ref = refs[0]
        in_refs = refs[1:1 + 3 * len(names)]
        out_refs = refs[1 + 3 * len(names):-1]
        loss_ref = refs[-1]
        g = p_ref[0]
        for s in range(1, N_DEV):
            g = g + p_ref[s]
        off = 0
        for i, (name, width) in enumerate(_SMALL):
            w_ref, m_ref, v_ref = in_refs[3 * i:3 * i + 3]
            rows, cols = w_ref.shape
            for r in range(rows):
                gr = g[:, off + r * cols:off + (r + 1) * cols]
                res = (gr,) + _adamw_math(w_ref[r:r + 1, :], gr, m_ref[r:r + 1, :], v_ref[r:r + 1, :])
                for o_ref, val in zip(out_refs[4 * i:4 * i + 4], res):
                    o_ref[r:r + 1, :] = val
            off += width
        loss_ref[...] = g[:, off:off + LANES]

    out_shape = [SDS(given[n][0].shape, F32) for n in names for _ in range(4)] + [SDS((1, LANES), F32)]
    outs = pl.pallas_call(body, name="adamw_small", out_shape=out_shape)(parts, *flat_in)
    return {n: tuple(outs[4 * i:4 * i + 4]) for i, n in enumerate(names)}, outs[-1][0, 0]


def kernel(x, c, w_ada, b_ada, pre_w_mix, w_in, attn_sinks, attn_out_w, lb_table, hg_norm_w, w_out, post_w_mix, pre_w_mlp, w_up, w_down, post_w_mlp, loss_target, m_w_ada, m_b_ada, m_pre_w_mix, m_w_in, m_attn_sinks, m_attn_out_w, m_lb_table, m_hg_norm_w, m_w_out, m_post_w_mix, m_pre_w_mlp, m_w_up, m_w_down, m_post_w_mlp, v_w_ada, v_b_ada, v_pre_w_mix, v_w_in, v_attn_sinks, v_attn_out_w, v_lb_table, v_hg_norm_w, v_w_out, v_post_w_mix, v_pre_w_mlp, v_w_up, v_w_down, v_post_w_mlp):
    B, T, _ = x.shape
    N = B * T
    me = 4 * lax.axis_index("x") + 2 * lax.axis_index("y") + lax.axis_index("c")
    x2 = x.reshape(N, D_MODEL)
    tgt2 = loss_target.reshape(N, D_MODEL)

    w_in_t, m_w_in_t, v_w_in_t = w_in[0].T, m_w_in[0].T, v_w_in[0].T
    w_in_g, c_g = _exchange("gather_w_in", [_bf(w_in_t), c], ["gather"] * 2)
    w_in_f = w_in_g.reshape(IN_COLS, D_MODEL)
    c_all = c_g.reshape(N_DEV * B, D_MODEL)

    ada_cols = w_ada.shape[2]
    b_mine = lax.dynamic_slice(b_ada, (0, me * ada_cols), (1, ada_cols))
    mod_cols = _ada_mod(c_all, w_ada[0], b_mine)
    (mod_g,) = _exchange("scatter_mod", [mod_cols.reshape(N_DEV, B, ada_cols)], ["a2a"])
    mod = mod_g.transpose(1, 0, 2).reshape(B, 6, D_MODEL)
    mod8 = jnp.pad(mod, ((0, 0), (0, 2), (0, 0)))

    lb_p = jax.nn.softmax(lb_table, axis=0)
    lb = lb_p[1:2]
    tables = _rope_tables(T)

    w_up_b, w_down_b = _bf(w_up[0]), _bf(w_down[0])
    proj, h1, w_out_g, w_up_g0 = _in_proj(x2, mod8, pre_w_mix, w_in_f, T,
                                          [_bf(w_out[0]), w_up_b[:MLP_HALF]], ["gather"] * 2)
    proj3 = proj.reshape(B, T, IN_COLS)
    rec_o, rec_g, s_prev, w_up_g1 = _hgrn_fwd(proj3, lb, hg_norm_w, [w_up_b[MLP_HALF:]], ["gather"])
    attn_o, attn_n, qr, kr, w_down_g0 = _attn_fwd(proj3, tables, attn_sinks, attn_out_w,
                                                  [w_down_b[:, :MLP_HALF]], ["gather"])
    w_out_f = w_out_g.reshape(D_MODEL, D_MODEL)
    mix, x1, cat, w_down_g1 = _mix_out(x2, attn_n.reshape(N, ATT_WIDTH), rec_g.reshape(N, HG_WIDTH), mod8,
                                       post_w_mix, w_out_f, T, [w_down_b[:, MLP_HALF:]], ["gather"])
    w_up_halves = [w_up_g0, w_up_g1]
    w_down_halves = [w_down_g0.reshape(D_FF, MLP_HALF), w_down_g1.reshape(D_FF, MLP_HALF)]
    up, d, h2 = _mlp_fwd(x1, mod8, pre_w_mlp, w_up_halves, w_down_halves, T)

    dx1, u, dup, dd, acc_mlp = _mlp_bwd(x1, d, up, tgt2, mod8, pre_w_mlp, post_w_mlp,
                                        w_up_halves, w_down_halves, T)
    chips = N_DEV // 2
    by_chip = lambda a: a.reshape((chips, 2, a.shape[0] // N_DEV) + a.shape[1:])
    gw_up = _matmul_tn("grad_w_up", h2, dup, D_FF // N_DEV, by_owner_cols=True)
    gw_up = gw_up.reshape(chips, 2, D_MODEL, D_FF // N_DEV)
    gw_down = by_chip(_matmul_tn("grad_w_down", u, dd, 512))
    dan, drg, dmix, acc_mix, q_down, q_up = _mix_bwd(mix, dx1, mod8, post_w_mix, w_out_f, T,
                                                     [gw_down, gw_up], ["pair"] * 2)
    p_down, p_up = _pair_add("pair_add_w_down", gw_down, q_down), _pair_add("pair_add_w_up", gw_up, q_up)
    gw_out = _matmul_tn("grad_w_out", cat, dmix, 512).reshape(N_DEV, D_MODEL // N_DEV, D_MODEL)
    dhq, dhf, dhi, dhg, dlb_p, dgw_p, r_down, r_up = _hgrn_bwd(
        proj3, lb, hg_norm_w, rec_o, s_prev, drg.reshape(B, T, HG_WIDTH), [p_down, p_up], ["chips"] * 2)
    dqkv, dsink_p, daw_p, r_out = _attn_bwd(qr, kr, proj3, attn_o, dan.reshape(B, T, ATT_WIDTH), tables,
                                            attn_sinks, attn_out_w, [gw_out], ["a2a"])
    flat = lambda a: a.reshape(N, a.shape[-1])
    grad_x, dproj, acc_in = _in_bwd(x2, dx1, flat(dqkv), flat(dhq), flat(dhf), flat(dhi), flat(dhg),
                                    mod8, pre_w_mix, w_in_f, T, [], [])

    gw_in = by_chip(_matmul_tn("grad_w_in", dproj, h1, 512, tm=IN_COLS // 2))
    (q_in,) = _exchange("pair_w_in", [gw_in], ["pair"])
    p_in = _pair_add("pair_add_w_in", gw_in, q_in)

    dmod = jnp.concatenate([acc_in[:, 0:2], acc_mix[:, 0:1], acc_mlp[:, 0:3]], axis=1)
    dlb = dlb_p[:, 0].sum(0)
    dlb_table = jnp.stack([-dlb, dlb]) * (lb_p[0] * lb_p[1])[None, :]
    small = {
        "b_ada": dmod.sum(0),
        "pre_w_mix": acc_in[:, 2].sum(0),
        "attn_sinks": dsink_p[:, :, 0].sum(0),
        "attn_out_w": daw_p[:, 0].sum(0),
        "lb_table": dlb_table,
        "hg_norm_w": dgw_p[:, 0].reshape(B, HG_HEADS, LANES).sum((0, 1)),
        "post_w_mix": acc_mix[:, 1].sum(0),
        "pre_w_mlp": acc_mlp[:, 3].sum(0),
        "post_w_mlp": acc_mlp[:, 4].sum(0),
    }
    loss_part = acc_mlp[:, 5, 0].sum()
    dmod_blocks = dmod.reshape(B, N_DEV, ada_cols).transpose(1, 0, 2)

    r_in, r_dmod, r_small = _exchange(
        "reduce_grads", [p_in, dmod_blocks, _pack_small(small, loss_part)], ["chips", "a2a", "gather"])

    res = {}
    res["w_in"] = tuple(a.T for a in _reduce_adamw("adamw_w_in", r_in, w_in_t, m_w_in_t, v_w_in_t))
    res["w_out"] = _reduce_adamw("adamw_w_out", r_out, w_out[0], m_w_out[0], v_w_out[0])
    res["w_up"] = _reduce_adamw("adamw_w_up", r_up, w_up[0], m_w_up[0], v_w_up[0])
    res["w_down"] = _reduce_adamw("adamw_w_down", r_down, w_down[0], m_w_down[0], v_w_down[0])
    res["w_ada"] = _ada_grad_adamw(c_all, r_dmod.reshape(N_DEV * B, ada_cols), w_ada[0], m_w_ada[0], v_w_ada[0])

    given = dict(b_ada=(b_ada, m_b_ada, v_b_ada), pre_w_mix=(pre_w_mix, m_pre_w_mix, v_pre_w_mix),
                 attn_sinks=(attn_sinks, m_attn_sinks, v_attn_sinks),
                 attn_out_w=(attn_out_w, m_attn_out_w, v_attn_out_w), lb_table=(lb_table, m_lb_table, v_lb_table),
                 hg_norm_w=(hg_norm_w, m_hg_norm_w, v_hg_norm_w), post_w_mix=(post_w_mix, m_post_w_mix, v_post_w_mix),
                 pre_w_mlp=(pre_w_mlp, m_pre_w_mlp, v_pre_w_mlp), post_w_mlp=(post_w_mlp, m_post_w_mlp, v_post_w_mlp))
    small_res, loss = _adamw_small(r_small, given)
    res.update(small_res)

    order = ["w_ada", "b_ada", "pre_w_mix", "w_in", "attn_sinks", "attn_out_w", "lb_table", "hg_norm_w", "w_out",
             "post_w_mix", "pre_w_mlp", "w_up", "w_down", "post_w_mlp"]
    big = {"w_ada", "w_in", "w_out", "w_up", "w_down"}
    outs = [loss, grad_x.reshape(B, T, D_MODEL)]
    for i in range(4):
        for k in order:
            a = res[k][i]
            outs.append(a[None] if k in big else a)
    return tuple(outs)
```

```python
import functools

import jax
import jax.numpy as jnp
from jax import lax
from jax.experimental import pallas as pl
from jax.experimental.pallas import tpu as pltpu

F32 = jnp.float32
BF16 = jnp.bfloat16
SDS = jax.ShapeDtypeStruct

D_MODEL = 1024
ATT_WIDTH = 512
ATT_HEAD_DIM = 64
ATT_KV_HEADS = 2
ATT_GROUP = 4
WINDOW = 128
ROPE_DIM = 16
ROPE_THETA = 500000.0
HG_WIDTH = 512
HG_HEAD_DIM = 128
HG_HEADS = 4
HG_CHUNK = 32
IN_COLS = 2816
ATT_COLS = 768
D_FF = 4096
EPS = 1e-6
N_DEV = 8

ADAM_LR = 0.001
ADAM_B1 = 0.9
ADAM_B2 = 0.999
ADAM_EPS = 1e-08
ADAM_WD = 0.01
ADAM_STEP = 10

VMEM_LIMIT_BIG = 56 << 20
LANES = 128

MESH = pl.DeviceIdType.MESH
NT_DIMS = (((1,), (1,)), ((), ()))
TN_DIMS = (((0,), (0,)), ((), ()))


def _dot(a, b):
    return jnp.dot(a, b, preferred_element_type=F32)


def _dot_nt(a, b):
    return lax.dot_general(a, b, NT_DIMS, preferred_element_type=F32)


def _dot_tn(a, b):
    return lax.dot_general(a, b, TN_DIMS, preferred_element_type=F32)


def _bf(a):
    return a.astype(BF16)


def _sigmoid(a):
    return 1.0 / (1.0 + jnp.exp(-a))


def _mean_last(a):
    return jnp.mean(a, axis=-1, keepdims=True)


def _sum_rows(a):
    return jnp.sum(a, axis=0, keepdims=True)


def _tri_sum(tri_bf, a, terms=3):
    a1 = _bf(a)
    r1 = a - a1.astype(F32)
    a2 = _bf(r1)
    out = _dot(tri_bf, a1) + _dot(tri_bf, a2)
    if terms == 3:
        out = out + _dot(tri_bf, _bf(r1 - a2.astype(F32)))
    return out


def _loop_pairs(first, count, body, init):
    if count % 2:
        return lax.fori_loop(first, first + count, body, init)
    return lax.fori_loop(0, count // 2, lambda i, c: body(first + 2 * i + 1, body(first + 2 * i, c)), init)


def _params(sem=None, vmem=None):
    kw = {}
    if sem is not None:
        kw["dimension_semantics"] = sem
    if vmem is not None:
        kw["vmem_limit_bytes"] = vmem
    return pltpu.CompilerParams(**kw)


ANY_SPEC = pl.BlockSpec(memory_space=pl.ANY)


def _exchange_shapes(srcs, modes):
    out_shape = []
    for s, m in zip(srcs, modes):
        shp = {"gather": (N_DEV,) + tuple(s.shape), "pair": (s.shape[0],) + tuple(s.shape[2:])}.get(m, tuple(s.shape))
        out_shape.append(SDS(shp, s.dtype))
    return out_shape


def _exchange_sems(n):
    if n == 0:
        return []
    return [pltpu.SemaphoreType.DMA((n, N_DEV - 1)), pltpu.SemaphoreType.DMA((n, N_DEV - 1)),
            pltpu.SemaphoreType.DMA((n,))]


SIBLING = 1
OTHER_CHIPS = (2, 4, 6)


def _related(k):
    x, y, c = lax.axis_index("x"), lax.axis_index("y"), lax.axis_index("c")
    px, py, pc = x ^ ((k >> 2) & 1), y ^ ((k >> 1) & 1), c ^ (k & 1)
    return (px, py, pc), 4 * px + 2 * py + pc


def _exchange_phases(modes, src_refs, out_refs, send_sems, recv_sems, own_sems):
    _, me = _related(0)
    sib_dev, sib = _related(SIBLING)
    start, middle, end = [], [], []

    def remote(a, i, src, dst, dev):
        return pltpu.make_async_remote_copy(src_ref=src, dst_ref=dst, send_sem=send_sems.at[a, i],
                                            recv_sem=recv_sems.at[a, i], device_id=dev, device_id_type=MESH)

    for a, mode in enumerate(modes):
        out = out_refs[a]
        if mode == "gather":
            src = src_refs[a]
            own = pltpu.make_async_copy(src, out.at[me], own_sems.at[a])
            to_sib = remote(a, 0, src, out.at[me], sib_dev)
            start += [own.start, to_sib.start]
            end += [remote(a, 0, src, out.at[sib], sib_dev).wait_recv, to_sib.wait_send, own.wait]
            for j, k in enumerate(OTHER_CHIPS, start=1):
                dev, peer = _related(k)
                _, peer_sib = _related(k ^ SIBLING)
                send = remote(a, j, src, out.at[me], dev)
                passed = remote(a, 3 + j, out.at[peer], out.at[peer], sib_dev)
                start.append(send.start)
                middle += [remote(a, j, src, out.at[peer], dev).wait_recv, passed.start]
                end += [remote(a, 3 + j, out.at[peer_sib], out.at[peer_sib], sib_dev).wait_recv,
                        send.wait_send, passed.wait_send]
        elif mode == "pair":
            core = lax.axis_index("c")
            for s in range(N_DEV // 2):
                send = remote(a, s, src_refs[a].at[s, 1 - core], out.at[s], sib_dev)
                start.append(send.start)
                end += [remote(a, s, src_refs[a].at[s, 1 - core], out.at[s], sib_dev).wait_recv, send.wait_send]
        elif mode == "chips":
            chip = me // 2
            own = pltpu.make_async_copy(src_refs[a].at[chip], out.at[chip], own_sems.at[a])
            start.append(own.start)
            end.append(own.wait)
            for j, k in enumerate(OTHER_CHIPS, start=1):
                dev, peer = _related(k)
                send = remote(a, j, src_refs[a].at[peer // 2], out.at[chip], dev)
                start.append(send.start)
                end += [remote(a, j, src_refs[a].at[peer // 2], out.at[peer // 2], dev).wait_recv, send.wait_send]
        else:
            own = pltpu.make_async_copy(src_refs[a].at[me], out.at[me], own_sems.at[a])
            start.append(own.start)
            end.append(own.wait)
            for k in range(1, N_DEV):
                dev, peer = _related(k)
                send = remote(a, k - 1, src_refs[a].at[peer], out.at[me], dev)
                start.append(send.start)
                end += [remote(a, k - 1, src_refs[a].at[peer], out.at[peer], dev).wait_recv, send.wait_send]
    return start, middle, end


def _run(actions):
    for act in actions:
        act()


def _exchange(name, srcs, modes):
    n = len(srcs)

    def body(*refs):
        start, middle, end = _exchange_phases(modes, refs[:n], refs[n:2 * n], *refs[2 * n:])
        _run(start)
        _run(middle)
        _run(end)

    return pl.pallas_call(
        body, name=name, out_shape=_exchange_shapes(srcs, modes),
        in_specs=[ANY_SPEC] * n, out_specs=[ANY_SPEC] * n,
        scratch_shapes=_exchange_sems(n),
    )(*srcs)


def _ride_start(modes, step, steps, src_refs, out_refs, sems):
    if not modes:
        return
    middle_step = steps - 1

    @pl.when(step == 0)
    def _():
        _run(_exchange_phases(modes, src_refs, out_refs, *sems)[0])

    if "gather" in modes:
        @pl.when(step == middle_step)
        def _():
            _run(_exchange_phases(modes, src_refs, out_refs, *sems)[1])


def _ride_wait(modes, step, steps, src_refs, out_refs, sems):
    if not modes:
        return

    @pl.when(step == steps - 1)
    def _():
        _run(_exchange_phases(modes, src_refs, out_refs, *sems)[2])


def _ada_mod(c_all, w_ada, b_ada_mine):
    nb, cols = c_all.shape[0], w_ada.shape[1]

    def body(c_ref, w_ref, b_ref, o_ref):
        cv = c_ref[...]
        ca = cv * _sigmoid(cv)
        o_ref[...] = _dot(ca, w_ref[...]) + b_ref[...]

    return pl.pallas_call(body, name="ada_mod", out_shape=SDS((nb, cols), F32))(c_all, w_ada, b_ada_mine)


def _tile_rows(T, big=False):
    return min(512 if big else 256, T)


def _mod_spec(tps):
    return pl.BlockSpec((None, 8, D_MODEL), lambda i: (i // tps, 0, 0))


def _in_proj(x2, mod8, pre_w, w_in_bf, T, ride_srcs, ride_modes):
    N = x2.shape[0]
    TM = _tile_rows(T, big=True)
    tps = T // TM
    nr = len(ride_srcs)

    def body(*refs):
        x_ref, mod_ref, pw_ref, w_ref = refs[:4]
        ride_in = refs[4:4 + nr]
        pa_ref, ph_ref, h1_ref = refs[4 + nr:7 + nr]
        ride_out = refs[7 + nr:7 + 2 * nr]
        sems = refs[7 + 2 * nr:]
        _ride_start(ride_modes, pl.program_id(0), N // TM, ride_in, ride_out, sems)
        x = x_ref[...]
        r = lax.rsqrt(_mean_last(x * x) + EPS)
        h = (x * r * pw_ref[...]) * (1.0 + mod_ref[1:2, :]) + mod_ref[0:1, :]
        hb = _bf(h)
        h1_ref[...] = hb
        pa_ref[...] = _dot_nt(hb, w_ref[:ATT_COLS, :])
        ph_ref[...] = _dot_nt(hb, w_ref[ATT_COLS:, :])
        _ride_wait(ride_modes, pl.program_id(0), N // TM, ride_in, ride_out, sems)

    return pl.pallas_call(
        body, name="in_proj", grid=(N // TM,),
        in_specs=[pl.BlockSpec((TM, D_MODEL), lambda i: (i, 0)), _mod_spec(tps),
                  pl.BlockSpec((1, D_MODEL), lambda i: (0, 0)),
                  pl.BlockSpec((IN_COLS, D_MODEL), lambda i: (0, 0))] + [ANY_SPEC] * nr,
        out_specs=[pl.BlockSpec((TM, ATT_COLS), lambda i: (i, 0)),
                   pl.BlockSpec((TM, IN_COLS - ATT_COLS), lambda i: (i, 0)),
                   pl.BlockSpec((TM, D_MODEL), lambda i: (i, 0))] + [ANY_SPEC] * nr,
        out_shape=[SDS((N, ATT_COLS), F32), SDS((N, IN_COLS - ATT_COLS), F32), SDS((N, D_MODEL), BF16)]
        + _exchange_shapes(ride_srcs, ride_modes),
        scratch_shapes=_exchange_sems(nr),
        compiler_params=_params(("arbitrary",), VMEM_LIMIT_BIG),
    )(x2, mod8, pre_w, w_in_bf, *ride_srcs)


def _rope_tables(T):
    half = ROPE_DIM // 2
    inv_freq = ROPE_THETA ** (-jnp.arange(0, ROPE_DIM, 2, dtype=F32) / ROPE_DIM)
    ang = jnp.arange(T, dtype=F32)[:, None] * inv_freq[None, :]
    cos, sin = jnp.cos(ang), jnp.sin(ang)
    ones = jnp.ones((T, ATT_HEAD_DIM - ROPE_DIM), F32)
    zeros = jnp.zeros((T, ATT_HEAD_DIM - ROPE_DIM), F32)
    zh = jnp.zeros((T, half), F32)
    cos64 = jnp.concatenate([cos, cos, ones], axis=1)
    sin_left = jnp.concatenate([-sin, zh, zeros], axis=1)
    sin_right = jnp.concatenate([zh, sin, zeros], axis=1)
    rep = LANES // ATT_HEAD_DIM
    return jnp.tile(cos64, (1, rep)), jnp.tile(sin_left, (1, rep)), jnp.tile(sin_right, (1, rep))


def _rope(xc, cs, sl, sr):
    return xc * cs + pltpu.roll(xc, LANES - 8, 1) * sl + pltpu.roll(xc, 8, 1) * sr


def _rope_t(dy, cs, sl, sr):
    return dy * cs + pltpu.roll(dy * sl, 8, 1) + pltpu.roll(dy * sr, LANES - 8, 1)


ATT_SCALE = ATT_HEAD_DIM ** -0.5
ATT_SPLITS = 4


def _band_masks():
    cols = ATT_GROUP * WINDOW
    j = lax.broadcasted_iota(jnp.int32, (2 * WINDOW, cols), 0)
    i = lax.broadcasted_iota(jnp.int32, (2 * WINDOW, cols), 1) & (WINDOW - 1)
    diff = i + WINDOW - j
    return (diff >= 0) & (diff < WINDOW), j >= WINDOW


def _sink_row(sink_ref, hk):
    return jnp.concatenate(
        [jnp.full((1, WINDOW), sink_ref[0, ATT_GROUP * hk + g], F32) for g in range(ATT_GROUP)], axis=1)


def _softmax_band(qs, kk, mask, sink):
    s = jnp.where(mask, _dot_nt(kk, qs), jnp.finfo(F32).min)
    m = jnp.maximum(jnp.max(s, axis=0, keepdims=True), sink)
    p = jnp.exp(s - m)
    es = jnp.exp(sink - m)
    inv = 1.0 / (jnp.sum(p, axis=0, keepdims=True) + es)
    return p, inv, es


def _stack_heads(parts, hk):
    hs = []
    for g in range(ATT_GROUP):
        h = ATT_GROUP * hk + g
        hs.append(parts[h // 2][:, (h % 2) * ATT_HEAD_DIM:(h % 2 + 1) * ATT_HEAD_DIM])
    return jnp.concatenate(hs, axis=0)


def _attn_fwd(proj3, tables, sinks, attn_w, ride_srcs, ride_modes):
    B, T, _ = proj3.shape
    nb = T // WINDOW
    splits = min(ATT_SPLITS, nb)
    per = nb // splits
    nr = len(ride_srcs)
    cos, sinl, sinr = tables

    def body(*refs):
        q_ref, k_ref, v_ref, cos_ref, sl_ref, sr_ref, sink_ref, aw_ref = refs[:8]
        ride_in = refs[8:8 + nr]
        o_ref, an_ref, qr_ref, kr_ref = refs[8 + nr:12 + nr]
        ride_out = refs[12 + nr:12 + 2 * nr]
        kpad, vpad = refs[12 + 2 * nr:14 + 2 * nr]
        sems = refs[14 + 2 * nr:]
        part = pl.program_id(1)
        step = pl.program_id(0) * splits + part
        _ride_start(ride_modes, step, B * splits, ride_in, ride_out, sems)

        @pl.when(part == 0)
        def _():
            kpad[0:WINDOW, :] = jnp.zeros((WINDOW, LANES), BF16)
            vpad[0:WINDOW, :] = jnp.zeros((WINDOW, LANES), BF16)

        window, current = _band_masks()

        def block(n, carry):
            r0 = pl.multiple_of(n * WINDOW, WINDOW)
            rows = pl.ds(r0, WINDOW)
            nxt = pl.ds(r0 + WINDOW, WINDOW)
            band = pl.ds(r0, 2 * WINDOW)
            cs, sl, sr = cos_ref[rows, :], sl_ref[rows, :], sr_ref[rows, :]
            kb = _bf(_rope(k_ref[rows, :], cs, sl, sr))
            kpad[nxt, :] = kb
            kr_ref[rows, :] = kb
            vpad[nxt, :] = _bf(v_ref[rows, :])
            qparts = []
            for j in range(ATT_WIDTH // LANES):
                qp = _bf(_rope(q_ref[rows, j * LANES:(j + 1) * LANES], cs, sl, sr) * ATT_SCALE)
                qr_ref[rows, j * LANES:(j + 1) * LANES] = qp
                qparts.append(qp)
            mask = window & (current | (n > 0))
            for hk in range(ATT_KV_HEADS):
                lanes = slice(hk * ATT_HEAD_DIM, (hk + 1) * ATT_HEAD_DIM)
                qs = _stack_heads(qparts, hk)
                p, inv, _ = _softmax_band(qs, kpad[band, lanes], mask, _sink_row(sink_ref, hk))
                ot = _dot_tn(vpad[band, lanes], _bf(p)) * inv
                for g in range(ATT_GROUP):
                    h = ATT_GROUP * hk + g
                    o_ref[rows, h * ATT_HEAD_DIM:(h + 1) * ATT_HEAD_DIM] = ot[:, g * WINDOW:(g + 1) * WINDOW].T
            ob = o_ref[rows, :]
            an_ref[rows, :] = _bf(ob * lax.rsqrt(_mean_last(ob * ob) + EPS) * aw_ref[...])
            return carry

        _loop_pairs(part * per, per, block, 0)
        _ride_wait(ride_modes, step, B * splits, ride_in, ride_out, sems)

    seq = lambda w, j: pl.BlockSpec((None, T, w), lambda b, s: (b, 0, j))
    full = lambda r, w: pl.BlockSpec((r, w), lambda b, s: (0, 0))
    return pl.pallas_call(
        body, name="attn_fwd", grid=(B, splits),
        in_specs=[seq(ATT_WIDTH, 0), seq(LANES, 4), seq(LANES, 5),
                  full(T, LANES), full(T, LANES), full(T, LANES),
                  pl.BlockSpec(memory_space=pltpu.SMEM), full(1, ATT_WIDTH)] + [ANY_SPEC] * nr,
        out_specs=[seq(ATT_WIDTH, 0), seq(ATT_WIDTH, 0), seq(ATT_WIDTH, 0), seq(LANES, 0)] + [ANY_SPEC] * nr,
        out_shape=[SDS((B, T, ATT_WIDTH), F32), SDS((B, T, ATT_WIDTH), BF16),
                   SDS((B, T, ATT_WIDTH), BF16), SDS((B, T, LANES), BF16)] + _exchange_shapes(ride_srcs, ride_modes),
        scratch_shapes=[pltpu.VMEM((T + WINDOW, LANES), BF16), pltpu.VMEM((T + WINDOW, LANES), BF16)]
        + _exchange_sems(nr),
        compiler_params=_params(("arbitrary", "arbitrary"), VMEM_LIMIT_BIG),
    )(proj3, proj3, proj3, cos, sinl, sinr, sinks, attn_w, *ride_srcs)


HG_GROUP = 8
HG_ROWS = HG_GROUP * HG_CHUNK


HG_STACK = HG_GROUP * HG_HEAD_DIM


def _group_masks():
    r = lax.broadcasted_iota(jnp.int32, (HG_ROWS, HG_ROWS), 0)
    c = lax.broadcasted_iota(jnp.int32, (HG_ROWS, HG_ROWS), 1)
    same = (r // HG_CHUNK) == (c // HG_CHUNK)
    return same & (r >= c), same & (c >= r)


def _row_chunk():
    return lax.broadcasted_iota(jnp.int32, (HG_ROWS, HG_HEAD_DIM), 0) // HG_CHUNK


def _spread(a, row_chunk):
    return jnp.concatenate([jnp.where(row_chunk == c, a, jnp.zeros_like(a)) for c in range(HG_GROUP)], axis=1)


def _pick(r, row_chunk):
    out = jnp.where(row_chunk == 0, r[:, :HG_HEAD_DIM], 0.0)
    for c in range(1, HG_GROUP):
        out = out + jnp.where(row_chunk == c, r[:, c * HG_HEAD_DIM:(c + 1) * HG_HEAD_DIM], 0.0)
    return out


def _lane_block(a, c):
    return a[:, c * HG_HEAD_DIM:(c + 1) * HG_HEAD_DIM]


def _ones_bf(mask):
    return jnp.where(mask, 1.0, 0.0).astype(BF16)


def _chunk_bcast(rows_1x128):
    return jnp.concatenate([jnp.broadcast_to(r, (HG_CHUNK, HG_HEAD_DIM)) for r in rows_1x128], axis=0)


def _hgrn_gates(hq, hf, lb, lower_bf):
    sq = _sigmoid(hq)
    q = hq * sq
    sg = _sigmoid(hf)
    f = lb + (1.0 - lb) * sg
    k = 1.0 - f
    logf = jnp.log(f)
    b = _tri_sum(lower_bf, logf)
    bl = [_sum_rows(logf[_chunk_rows(c), :]) for c in range(HG_GROUP)]
    eb, enb, e2 = jnp.exp(b), jnp.exp(-b), jnp.exp(_chunk_bcast(bl) - b)
    ebl = [jnp.exp(r) for r in bl]
    return dict(sq=sq, sg=sg, f=f, eb=eb, enb=enb, e2=e2, ebl=ebl, qd=q * eb, kd=k * enb, k2=k * e2)


def _chunk_rows(c):
    return slice(c * HG_CHUNK, (c + 1) * HG_CHUNK)


def _head_lanes(h):
    return slice(h * HG_HEAD_DIM, (h + 1) * HG_HEAD_DIM)


def _hgrn_fwd(proj_h, lb, hg_w, ride_srcs, ride_modes):
    B, T, _ = proj_h.shape
    ng = T // HG_ROWS
    nr = len(ride_srcs)

    def body(*refs):
        hq_ref, hf_ref, hi_ref, hg_ref, lb_ref, gw_ref = refs[:6]
        ride_in = refs[6:6 + nr]
        o_ref, rg_ref, sp_ref = refs[6 + nr:9 + nr]
        ride_out = refs[9 + nr:9 + 2 * nr]
        st = refs[9 + 2 * nr]
        sems = refs[10 + 2 * nr:]
        gi = pl.program_id(1)
        step = pl.program_id(0) * ng + gi
        _ride_start(ride_modes, step, B * ng, ride_in, ride_out, sems)

        @pl.when(gi == 0)
        def _():
            st[...] = jnp.zeros(st.shape, F32)

        lo, _ = _group_masks()
        lower_bf = _ones_bf(lo)
        row_chunk = _row_chunk()
        for h in range(HG_HEADS):
            lanes = _head_lanes(h)
            gt = _hgrn_gates(hq_ref[:, lanes], hf_ref[:, lanes], lb_ref[:, lanes], lower_bf)
            v, qd, kd = _bf(hi_ref[:, lanes]), _bf(gt["qd"]), _bf(gt["kd"])
            a = jnp.where(lo, _dot_nt(qd, kd), 0.0)
            kv = _dot_tn(v, _bf(_spread(gt["k2"], row_chunk)))
            s = st[h]
            before = []
            for c in range(HG_GROUP):
                before.append(s)
                s = s * gt["ebl"][c] + _lane_block(kv, c)
            st[h] = s
            sp = jnp.concatenate(before, axis=1)
            sp_ref[h] = sp
            o = _dot(_bf(a), v) + _dot_nt(_bf(_spread(gt["qd"], row_chunk)), _bf(sp))
            o_ref[:, lanes] = o
            hg = hg_ref[:, lanes]
            rn = o * lax.rsqrt(_mean_last(o * o) + EPS) * gw_ref[...]
            rg_ref[:, lanes] = _bf(rn * (hg * _sigmoid(hg)))
        _ride_wait(ride_modes, step, B * ng, ride_in, ride_out, sems)

    part = lambda j: pl.BlockSpec((None, HG_ROWS, HG_WIDTH), lambda b, g: (b, g, j))
    return pl.pallas_call(
        body, name="hgrn_fwd", grid=(B, ng),
        in_specs=[part(0), part(1), part(2), part(3),
                  pl.BlockSpec((1, HG_WIDTH), lambda b, g: (0, 0)),
                  pl.BlockSpec((1, LANES), lambda b, g: (0, 0))] + [ANY_SPEC] * nr,
        out_specs=[part(0), part(0),
                   pl.BlockSpec((None, HG_HEADS, None, HG_HEAD_DIM, HG_STACK), lambda b, g: (b, 0, g, 0, 0))]
        + [ANY_SPEC] * nr,
        out_shape=[SDS((B, T, HG_WIDTH), F32), SDS((B, T, HG_WIDTH), BF16),
                   SDS((B, HG_HEADS, ng, HG_HEAD_DIM, HG_STACK), F32)] + _exchange_shapes(ride_srcs, ride_modes),
        scratch_shapes=[pltpu.VMEM((HG_HEADS, HG_HEAD_DIM, HG_HEAD_DIM), F32)] + _exchange_sems(nr),
        compiler_params=_params(("arbitrary", "arbitrary"), VMEM_LIMIT_BIG),
    )(proj_h, proj_h, proj_h, proj_h, lb, hg_w, *ride_srcs)


def _mix_out(x2, attn_n, rec_g, mod8, post_w, w_out_bf, T, ride_srcs, ride_modes):
    N = x2.shape[0]
    TM = _tile_rows(T, big=True)
    tps = T // TM
    nr = len(ride_srcs)

    def body(*refs):
        x_ref, an_ref, rg_ref, mod_ref, pw_ref, w_ref = refs[:6]
        ride_in = refs[6:6 + nr]
        mix_ref, x1_ref, cat_ref = refs[6 + nr:9 + nr]
        ride_out = refs[9 + nr:9 + 2 * nr]
        sems = refs[9 + 2 * nr:]
        _ride_start(ride_modes, pl.program_id(0), N // TM, ride_in, ride_out, sems)
        cat = jnp.concatenate([an_ref[...], rg_ref[...]], axis=1)
        cat_ref[...] = cat
        mix = _dot(cat, w_ref[...])
        mix_ref[...] = mix
        r = lax.rsqrt(_mean_last(mix * mix) + EPS)
        x1_ref[...] = x_ref[...] + mod_ref[2:3, :] * (mix * r * pw_ref[...])
        _ride_wait(ride_modes, pl.program_id(0), N // TM, ride_in, ride_out, sems)

    row = lambda w: pl.BlockSpec((TM, w), lambda i: (i, 0))
    return pl.pallas_call(
        body, name="mix_out", grid=(N // TM,),
        in_specs=[row(D_MODEL), row(ATT_WIDTH), row(HG_WIDTH), _mod_spec(tps),
                  pl.BlockSpec((1, D_MODEL), lambda i: (0, 0)),
                  pl.BlockSpec((D_MODEL, D_MODEL), lambda i: (0, 0))] + [ANY_SPEC] * nr,
        out_specs=[row(D_MODEL), row(D_MODEL), row(D_MODEL)] + [ANY_SPEC] * nr,
        out_shape=[SDS((N, D_MODEL), F32), SDS((N, D_MODEL), F32), SDS((N, D_MODEL), BF16)]
        + _exchange_shapes(ride_srcs, ride_modes),
        scratch_shapes=_exchange_sems(nr),
        compiler_params=_params(("arbitrary",), VMEM_LIMIT_BIG),
    )(x2, attn_n, rec_g, mod8, post_w, w_out_bf, *ride_srcs)


def _load_weights_once(pairs, sem):
    @pl.when(pl.program_id(0) == 0)
    def _():
        cps = [pltpu.make_async_copy(src, dst, sem.at[i]) for i, (src, dst) in enumerate(pairs)]
        for cp in cps:
            cp.start()
        for cp in cps:
            cp.wait()


MLP_HALF = D_MODEL // 2
MLP_PIECES = 2 * N_DEV + 2


def _mlp_weight_pieces(wu_a, wu_b, wd_a, wd_b, wu, wd):
    cols = D_FF // N_DEV
    pairs = []
    for h, half in enumerate((wu_a, wu_b)):
        for j in range(N_DEV):
            pairs.append((half.at[j], wu.at[pl.ds(h * MLP_HALF, MLP_HALF), pl.ds(j * cols, cols)]))
    for h, half in enumerate((wd_a, wd_b)):
        pairs.append((half, wd.at[:, pl.ds(h * MLP_HALF, MLP_HALF)]))
    return pairs


def _mlp_fwd(x1, mod8, pre_w, w_up_halves, w_down_halves, T):
    N = x1.shape[0]
    TM = _tile_rows(T)
    tps = T // TM

    def body(x_ref, mod_ref, pw_ref, wua, wub, wda, wdb, up_ref, d_ref, h2_ref, wu, wd, sem):
        _load_weights_once(_mlp_weight_pieces(wua, wub, wda, wdb, wu, wd), sem)
        x = x_ref[...]
        r = lax.rsqrt(_mean_last(x * x) + EPS)
        h = (x * r * pw_ref[...]) * (1.0 + mod_ref[4:5, :]) + mod_ref[3:4, :]
        hb = _bf(h)
        h2_ref[...] = hb
        up = _dot(hb, wu[...])
        up_ref[...] = up
        ru = jnp.maximum(up, 0.0)
        d_ref[...] = _dot(_bf(ru * ru), wd[...])

    row = lambda w: pl.BlockSpec((TM, w), lambda i: (i, 0))
    return pl.pallas_call(
        body, name="mlp_fwd", grid=(N // TM,),
        in_specs=[row(D_MODEL), _mod_spec(tps), pl.BlockSpec((1, D_MODEL), lambda i: (0, 0))] + [ANY_SPEC] * 4,
        out_specs=[row(D_FF), row(D_MODEL), row(D_MODEL)],
        out_shape=[SDS((N, D_FF), F32), SDS((N, D_MODEL), F32), SDS((N, D_MODEL), BF16)],
        scratch_shapes=[pltpu.VMEM((D_MODEL, D_FF), BF16), pltpu.VMEM((D_FF, D_MODEL), BF16),
                        pltpu.SemaphoreType.DMA((MLP_PIECES,))],
        compiler_params=_params(("arbitrary",), VMEM_LIMIT_BIG),
    )(x1, mod8, pre_w, *w_up_halves, *w_down_halves)


def _acc_rows(acc_ref, first, rows):
    @pl.when(first)
    def _():
        acc_ref[...] = jnp.zeros(acc_ref.shape, F32)
    for i, r in enumerate(rows):
        acc_ref[i:i + 1, :] += r


def _mlp_bwd(x1, d, up, tgt, mod8, pre_w, post_w, w_up_halves, w_down_halves, T):
    N = x1.shape[0]
    TM = _tile_rows(T)
    tps = T // TM

    def body(x_ref, d_ref, up_ref, t_ref, mod_ref, pw_ref, qw_ref, wua, wub, wda, wdb,
             dx_ref, u_ref, dup_ref, dd_ref, acc_ref, wd, wu, sem):
        _load_weights_once(_mlp_weight_pieces(wua, wub, wda, wdb, wu, wd), sem)
        sh2, sc2, g2 = mod_ref[3:4, :], mod_ref[4:5, :], mod_ref[5:6, :]
        x = x_ref[...]
        r1 = lax.rsqrt(_mean_last(x * x) + EPS)
        xh = x * r1
        n2 = xh * pw_ref[...]
        dv = d_ref[...]
        rd = lax.rsqrt(_mean_last(dv * dv) + EPS)
        dh = dv * rd
        rr = dh * qw_ref[...]
        e = x + g2 * rr - t_ref[...]
        loss = 0.5 * jnp.sum(_sum_rows(e * e), axis=1, keepdims=True) / D_MODEL
        dy = e * (1.0 / D_MODEL)
        dg2 = _sum_rows(dy * rr)
        drr = dy * g2
        dw_post = _sum_rows(drr * dh)
        ddh = drr * qw_ref[...]
        dd = _bf(rd * (ddh - dh * _mean_last(ddh * dh)))
        dd_ref[...] = dd
        ru = jnp.maximum(up_ref[...], 0.0)
        u_ref[...] = _bf(ru * ru)
        dup = _bf(_dot_nt(dd, wd[...]) * (2.0 * ru))
        dup_ref[...] = dup
        dh2 = _dot_nt(dup, wu[...])
        dsh2 = _sum_rows(dh2)
        dsc2 = _sum_rows(dh2 * n2)
        dn2 = dh2 * (1.0 + sc2)
        dw_pre = _sum_rows(dn2 * xh)
        dxh = dn2 * pw_ref[...]
        dx_ref[...] = dy + r1 * (dxh - xh * _mean_last(dxh * xh))
        _acc_rows(acc_ref, pl.program_id(0) % tps == 0,
                  [dsh2, dsc2, dg2, dw_pre, dw_post, jnp.broadcast_to(loss, (1, D_MODEL))])

    row = lambda w: pl.BlockSpec((TM, w), lambda i: (i, 0))
    vec = pl.BlockSpec((1, D_MODEL), lambda i: (0, 0))
    B = N // T
    return pl.pallas_call(
        body, name="mlp_bwd", grid=(N // TM,),
        in_specs=[row(D_MODEL), row(D_MODEL), row(D_FF), row(D_MODEL), _mod_spec(tps), vec, vec] + [ANY_SPEC] * 4,
        out_specs=[row(D_MODEL), row(D_FF), row(D_FF), row(D_MODEL), _mod_spec(tps)],
        out_shape=[SDS((N, D_MODEL), F32), SDS((N, D_FF), BF16), SDS((N, D_FF), BF16),
                   SDS((N, D_MODEL), BF16), SDS((B, 8, D_MODEL), F32)],
        scratch_shapes=[pltpu.VMEM((D_FF, D_MODEL), BF16), pltpu.VMEM((D_MODEL, D_FF), BF16),
                        pltpu.SemaphoreType.DMA((MLP_PIECES,))],
        compiler_params=_params(("arbitrary",), VMEM_LIMIT_BIG),
    )(x1, d, up, tgt, mod8, pre_w, post_w, *w_up_halves, *w_down_halves)


def _mix_bwd(mix, dx1, mod8, post_w, w_out_bf, T, ride_srcs, ride_modes):
    N = mix.shape[0]
    TM = _tile_rows(T, big=True)
    tps = T // TM
    nr = len(ride_srcs)

    def body(*refs):
        mix_ref, dx_ref, mod_ref, pw_ref, w_ref = refs[:5]
        ride_in = refs[5:5 + nr]
        dan_ref, drg_ref, dmix_ref, acc_ref = refs[5 + nr:9 + nr]
        ride_out = refs[9 + nr:9 + 2 * nr]
        sems = refs[9 + 2 * nr:]
        _ride_start(ride_modes, pl.program_id(0), N // TM, ride_in, ride_out, sems)
        g1 = mod_ref[2:3, :]
        mix = mix_ref[...]
        dx1 = dx_ref[...]
        rm = lax.rsqrt(_mean_last(mix * mix) + EPS)
        mh = mix * rm
        dg1 = _sum_rows(dx1 * (mh * pw_ref[...]))
        dr = dx1 * g1
        dw_post = _sum_rows(dr * mh)
        dmh = dr * pw_ref[...]
        dmix = _bf(rm * (dmh - mh * _mean_last(dmh * mh)))
        dmix_ref[...] = dmix
        dcat = _dot_nt(dmix, w_ref[...])
        dan_ref[...] = dcat[:, :ATT_WIDTH]
        drg_ref[...] = dcat[:, ATT_WIDTH:]
        _acc_rows(acc_ref, pl.program_id(0) % tps == 0, [dg1, dw_post])
        _ride_wait(ride_modes, pl.program_id(0), N // TM, ride_in, ride_out, sems)

    row = lambda w: pl.BlockSpec((TM, w), lambda i: (i, 0))
    B = N // T
    return pl.pallas_call(
        body, name="mix_bwd", grid=(N // TM,),
        in_specs=[row(D_MODEL), row(D_MODEL), _mod_spec(tps), pl.BlockSpec((1, D_MODEL), lambda i: (0, 0)),
                  pl.BlockSpec((D_MODEL, D_MODEL), lambda i: (0, 0))] + [ANY_SPEC] * nr,
        out_specs=[row(ATT_WIDTH), row(HG_WIDTH), row(D_MODEL), _mod_spec(tps)] + [ANY_SPEC] * nr,
        out_shape=[SDS((N, ATT_WIDTH), F32), SDS((N, HG_WIDTH), F32), SDS((N, D_MODEL), BF16),
                   SDS((B, 8, D_MODEL), F32)] + _exchange_shapes(ride_srcs, ride_modes),
        scratch_shapes=_exchange_sems(nr),
        compiler_params=_params(("arbitrary",), VMEM_LIMIT_BIG),
    )(mix, dx1, mod8, post_w, w_out_bf, *ride_srcs)


def _hgrn_bwd(proj_h, lb, hg_w, o, s_prev, drg, ride_srcs, ride_modes):
    B, T, _ = proj_h.shape
    ng = T // HG_ROWS
    nr = len(ride_srcs)

    def body(*refs):
        hq_ref, hf_ref, hi_ref, hg_ref, lb_ref, gw_ref, o_ref, sp_ref, drg_ref = refs[:9]
        ride_in = refs[9:9 + nr]
        dhq_ref, dhf_ref, dhi_ref, dhg_ref, dlb_ref, dgw_ref = refs[9 + nr:15 + nr]
        ride_out = refs[15 + nr:15 + 2 * nr]
        dst = refs[15 + 2 * nr]
        sems = refs[16 + 2 * nr:]
        step = pl.program_id(0) * ng + pl.program_id(1)
        _ride_start(ride_modes, step, B * ng, ride_in, ride_out, sems)

        @pl.when(pl.program_id(1) == 0)
        def _():
            dst[...] = jnp.zeros(dst.shape, F32)
            dlb_ref[...] = jnp.zeros(dlb_ref.shape, F32)
            dgw_ref[...] = jnp.zeros(dgw_ref.shape, F32)

        lo, up = _group_masks()
        lower_bf, upper_bf = _ones_bf(lo), _ones_bf(up)
        row_chunk = _row_chunk()
        gw = gw_ref[...]

        for h in range(HG_HEADS):
            lanes = _head_lanes(h)
            lbv = lb_ref[:, lanes]
            hq = hq_ref[:, lanes]
            gt = _hgrn_gates(hq, hf_ref[:, lanes], lbv, lower_bf)
            sq, sg, qdf, kdf, k2f, ebl = gt["sq"], gt["sg"], gt["qd"], gt["kd"], gt["k2"], gt["ebl"]
            v, qd, kd = _bf(hi_ref[:, lanes]), _bf(qdf), _bf(kdf)
            ov = o_ref[:, lanes]
            hg = hg_ref[:, lanes]
            shg = _sigmoid(hg)
            dr = drg_ref[:, lanes]
            ro = lax.rsqrt(_mean_last(ov * ov) + EPS)
            oh = ov * ro
            dhg_ref[:, lanes] = _bf(dr * (oh * gw) * (shg + hg * shg * (1.0 - shg)))
            drn = dr * (hg * shg)
            dgw_ref[...] += jnp.broadcast_to(_sum_rows(drn * oh), (8, LANES))
            doh = drn * gw
            do = _bf(ro * (doh - oh * _mean_last(doh * oh)))
            a = jnp.where(lo, _dot_nt(qd, kd), 0.0)
            da = _bf(jnp.where(lo, _dot_nt(do, v), 0.0))
            dv = _dot_tn(_bf(a), do)
            dqd = _dot(da, kd)
            dkd = _dot_tn(da, qd)
            sp = sp_ref[h]
            incr = _dot_tn(do, _bf(_spread(qdf, row_chunk)))
            ds = dst[h]
            after = [None] * HG_GROUP
            for c in reversed(range(HG_GROUP)):
                after[c] = ds
                ds = ds * ebl[c] + _lane_block(incr, c)
            dst[h] = ds
            dss = jnp.concatenate(after, axis=1)
            dssb = _bf(dss)
            dk2 = _pick(_dot(v, dssb), row_chunk)
            dhi_ref[:, lanes] = _bf(dv + _dot_nt(_bf(_spread(k2f, row_chunk)), dssb))
            dqd = dqd + _pick(_dot(do, _bf(sp)), row_chunk)
            debl = _sum_rows(dss * sp)
            k2g = dk2 * k2f
            db = dqd * qdf - dkd * kdf - k2g
            dk = dkd * gt["enb"] + dk2 * gt["e2"]
            dbl = _chunk_bcast([_lane_block(debl, c) * ebl[c] + _sum_rows(k2g[_chunk_rows(c), :])
                                for c in range(HG_GROUP)])
            dg = _tri_sum(upper_bf, db, terms=2) + dbl
            df = dg / gt["f"] - dk
            dhf_ref[:, lanes] = _bf(df * (1.0 - lbv) * sg * (1.0 - sg))
            dlb_ref[:, lanes] += jnp.broadcast_to(_sum_rows(df * (1.0 - sg)), (8, LANES))
            dhq_ref[:, lanes] = _bf((dqd * gt["eb"]) * (sq + hq * sq * (1.0 - sq)))
        _ride_wait(ride_modes, step, B * ng, ride_in, ride_out, sems)

    part = lambda j: pl.BlockSpec((None, HG_ROWS, HG_WIDTH), lambda b, g: (b, ng - 1 - g, j))
    return pl.pallas_call(
        body, name="hgrn_bwd", grid=(B, ng),
        in_specs=[part(0), part(1), part(2), part(3),
                  pl.BlockSpec((1, HG_WIDTH), lambda b, g: (0, 0)),
                  pl.BlockSpec((1, LANES), lambda b, g: (0, 0)),
                  part(0),
                  pl.BlockSpec((None, HG_HEADS, None, HG_HEAD_DIM, HG_STACK), lambda b, g: (b, 0, ng - 1 - g, 0, 0)),
                  part(0)] + [ANY_SPEC] * nr,
        out_specs=[part(0), part(0), part(0), part(0),
                   pl.BlockSpec((None, 8, HG_WIDTH), lambda b, g: (b, 0, 0)),
                   pl.BlockSpec((None, 8, LANES), lambda b, g: (b, 0, 0))] + [ANY_SPEC] * nr,
        out_shape=[SDS((B, T, HG_WIDTH), BF16)] * 4 + [SDS((B, 8, HG_WIDTH), F32), SDS((B, 8, LANES), F32)]
        + _exchange_shapes(ride_srcs, ride_modes),
        scratch_shapes=[pltpu.VMEM((HG_HEADS, HG_HEAD_DIM, HG_HEAD_DIM), F32)] + _exchange_sems(nr),
        compiler_params=_params(("arbitrary", "arbitrary"), VMEM_LIMIT_BIG),
    )(proj_h, proj_h, proj_h, proj_h, lb, hg_w, o, s_prev, drg, *ride_srcs)


def _attn_bwd(qr, kr, proj3, attn_o, dan, tables, sinks, attn_w, ride_srcs, ride_modes):
    B, T, _ = proj3.shape
    nb = T // WINDOW
    splits = min(ATT_SPLITS, nb)
    per = nb // splits
    nr = len(ride_srcs)
    cos, sinl, sinr = tables
    QKV = ATT_WIDTH + 2 * LANES

    def body(*refs):
        qr_ref, kr_ref, v_ref, o_ref, dan_ref, cos_ref, sl_ref, sr_ref, sink_ref, aw_ref = refs[:10]
        ride_in = refs[10:10 + nr]
        dqkv_ref, dsink_ref, daw_ref = refs[10 + nr:13 + nr]
        ride_out = refs[13 + nr:13 + 2 * nr]
        kpad, vpad, dkpad, dvpad, dqb, dsk = refs[13 + 2 * nr:19 + 2 * nr]
        sems = refs[19 + 2 * nr:]
        part = pl.program_id(1)
        step = pl.program_id(0) * splits + part
        _ride_start(ride_modes, step, B * splits, ride_in, ride_out, sems)

        @pl.when(part == 0)
        def _():
            kpad[0:WINDOW, :] = jnp.zeros((WINDOW, LANES), BF16)
            vpad[0:WINDOW, :] = jnp.zeros((WINDOW, LANES), BF16)
            kpad[WINDOW:, :] = kr_ref[...]
            vpad[WINDOW:, :] = _bf(v_ref[...])
            dkpad[...] = jnp.zeros(dkpad.shape, F32)
            dvpad[...] = jnp.zeros(dvpad.shape, F32)
            dsk[...] = jnp.zeros(dsk.shape, F32)
            daw_ref[...] = jnp.zeros(daw_ref.shape, F32)

        window, current = _band_masks()
        aw = aw_ref[...]

        def block(n, daw):
            r0 = pl.multiple_of(n * WINDOW, WINDOW)
            rows = pl.ds(r0, WINDOW)
            band = pl.ds(r0, 2 * WINDOW)
            ob = o_ref[rows, :]
            dn = dan_ref[rows, :]
            ro = lax.rsqrt(_mean_last(ob * ob) + EPS)
            oh = ob * ro
            daw = daw + _sum_rows(dn * oh)
            doh = dn * aw
            do = _bf(ro * (doh - oh * _mean_last(doh * oh)))
            doparts = [do[:, j * LANES:(j + 1) * LANES] for j in range(ATT_WIDTH // LANES)]
            qparts = [qr_ref[rows, j * LANES:(j + 1) * LANES] for j in range(ATT_WIDTH // LANES)]
            mask = window & (current | (n > 0))
            for hk in range(ATT_KV_HEADS):
                lanes = slice(hk * ATT_HEAD_DIM, (hk + 1) * ATT_HEAD_DIM)
                qs = _stack_heads(qparts, hk)
                dos = _stack_heads(doparts, hk)
                kk, vv = kpad[band, lanes], vpad[band, lanes]
                p, inv, es = _softmax_band(qs, kk, mask, _sink_row(sink_ref, hk))
                p = p * inv
                dp = _dot_nt(vv, dos)
                delta = jnp.sum(p * dp, axis=0, keepdims=True)
                ds = _bf(p * (dp - delta))
                sk = (es * inv) * delta
                dqt = _dot_tn(kk, ds) * ATT_SCALE
                dkpad[band, lanes] += _dot(ds, qs)
                dvpad[band, lanes] += _dot(_bf(p), dos)
                for g in range(ATT_GROUP):
                    h = ATT_GROUP * hk + g
                    cols = slice(g * WINDOW, (g + 1) * WINDOW)
                    dqb[:, h * ATT_HEAD_DIM:(h + 1) * ATT_HEAD_DIM] = dqt[:, cols].T
                    dsk[h:h + 1, :] += jnp.broadcast_to(-jnp.sum(sk[:, cols], axis=1, keepdims=True), (1, LANES))
            cs, sl, sr = cos_ref[rows, :], sl_ref[rows, :], sr_ref[rows, :]
            for j in range(ATT_WIDTH // LANES):
                dqkv_ref[rows, j * LANES:(j + 1) * LANES] = _bf(_rope_t(dqb[:, j * LANES:(j + 1) * LANES], cs, sl, sr))
            return daw

        daw = _loop_pairs(part * per, per, block, jnp.zeros((1, ATT_WIDTH), F32))
        daw_ref[...] += jnp.broadcast_to(daw, (8, ATT_WIDTH))
        dsink_ref[...] = dsk[...]

        def finish(n, carry):
            r0 = pl.multiple_of(n * WINDOW, WINDOW)
            rows = pl.ds(r0, WINDOW)
            nxt = pl.ds(r0 + WINDOW, WINDOW)
            cs, sl, sr = cos_ref[rows, :], sl_ref[rows, :], sr_ref[rows, :]
            dqkv_ref[rows, ATT_WIDTH:ATT_WIDTH + LANES] = _bf(_rope_t(dkpad[nxt, :], cs, sl, sr))
            dqkv_ref[rows, ATT_WIDTH + LANES:QKV] = _bf(dvpad[nxt, :])
            return carry

        @pl.when(part == splits - 1)
        def _():
            lax.fori_loop(0, nb, finish, 0)

        _ride_wait(ride_modes, step, B * splits, ride_in, ride_out, sems)

    seq = lambda w, j: pl.BlockSpec((None, T, w), lambda b, s: (b, 0, j))
    full = lambda r, w: pl.BlockSpec((r, w), lambda b, s: (0, 0))
    return pl.pallas_call(
        body, name="attn_bwd", grid=(B, splits),
        in_specs=[seq(ATT_WIDTH, 0), seq(LANES, 0), seq(LANES, 5), seq(ATT_WIDTH, 0), seq(ATT_WIDTH, 0),
                  full(T, LANES), full(T, LANES), full(T, LANES),
                  pl.BlockSpec(memory_space=pltpu.SMEM), full(1, ATT_WIDTH)] + [ANY_SPEC] * nr,
        out_specs=[seq(QKV, 0), pl.BlockSpec((None, 8, LANES), lambda b, s: (b, 0, 0)),
                   pl.BlockSpec((None, 8, ATT_WIDTH), lambda b, s: (b, 0, 0))] + [ANY_SPEC] * nr,
        out_shape=[SDS((B, T, QKV), BF16), SDS((B, 8, LANES), F32), SDS((B, 8, ATT_WIDTH), F32)]
        + _exchange_shapes(ride_srcs, ride_modes),
        scratch_shapes=[pltpu.VMEM((T + WINDOW, LANES), BF16), pltpu.VMEM((T + WINDOW, LANES), BF16),
                        pltpu.VMEM((T + WINDOW, LANES), F32), pltpu.VMEM((T + WINDOW, LANES), F32),
                        pltpu.VMEM((WINDOW, ATT_WIDTH), F32), pltpu.VMEM((8, LANES), F32)] + _exchange_sems(nr),
        compiler_params=_params(("arbitrary", "arbitrary"), VMEM_LIMIT_BIG),
    )(qr, kr, proj3, attn_o, dan, cos, sinl, sinr, sinks, attn_w, *ride_srcs)


def _in_bwd(x2, dx1, dqkv, dhq, dhf, dhi, dhg, mod8, pre_w, w_in_bf, T, ride_srcs, ride_modes):
    N = x2.shape[0]
    TM = _tile_rows(T, big=True)
    tps = T // TM
    nr = len(ride_srcs)
    pieces = [(0, ATT_WIDTH + 2 * LANES), (768, HG_WIDTH), (1280, HG_WIDTH), (1792, HG_WIDTH), (2304, HG_WIDTH)]

    def body(*refs):
        x_ref, dx_ref, p0, p1, p2, p3, p4, mod_ref, pw_ref, w_ref = refs[:10]
        ride_in = refs[10:10 + nr]
        gx_ref, dproj_ref, acc_ref = refs[10 + nr:13 + nr]
        ride_out = refs[13 + nr:13 + 2 * nr]
        sems = refs[13 + 2 * nr:]
        _ride_start(ride_modes, pl.program_id(0), N // TM, ride_in, ride_out, sems)
        sc1 = mod_ref[1:2, :]
        dh = jnp.zeros((TM, D_MODEL), F32)
        for ref, (off, width) in zip((p0, p1, p2, p3, p4), pieces):
            pb = ref[...]
            dproj_ref[:, off:off + width] = pb
            dh = dh + _dot(pb, w_ref[off:off + width, :])
        x = x_ref[...]
        r = lax.rsqrt(_mean_last(x * x) + EPS)
        xh = x * r
        n1 = xh * pw_ref[...]
        dsh1 = _sum_rows(dh)
        dsc1 = _sum_rows(dh * n1)
        dn1 = dh * (1.0 + sc1)
        dw_pre = _sum_rows(dn1 * xh)
        dxh = dn1 * pw_ref[...]
        gx_ref[...] = dx_ref[...] + r * (dxh - xh * _mean_last(dxh * xh))
        _acc_rows(acc_ref, pl.program_id(0) % tps == 0, [dsh1, dsc1, dw_pre])
        _ride_wait(ride_modes, pl.program_id(0), N // TM, ride_in, ride_out, sems)

    row = lambda w: pl.BlockSpec((TM, w), lambda i: (i, 0))
    B = N // T
    return pl.pallas_call(
        body, name="in_bwd", grid=(N // TM,),
        in_specs=[row(D_MODEL), row(D_MODEL), row(768), row(HG_WIDTH), row(HG_WIDTH), row(HG_WIDTH),
                  row(HG_WIDTH), _mod_spec(tps), pl.BlockSpec((1, D_MODEL), lambda i: (0, 0)),
                  pl.BlockSpec((IN_COLS, D_MODEL), lambda i: (0, 0))] + [ANY_SPEC] * nr,
        out_specs=[row(D_MODEL), row(IN_COLS), _mod_spec(tps)] + [ANY_SPEC] * nr,
        out_shape=[SDS((N, D_MODEL), F32), SDS((N, IN_COLS), BF16), SDS((B, 8, D_MODEL), F32)]
        + _exchange_shapes(ride_srcs, ride_modes),
        scratch_shapes=_exchange_sems(nr),
        compiler_params=_params(("arbitrary",), VMEM_LIMIT_BIG),
    )(x2, dx1, dqkv, dhq, dhf, dhi, dhg, mod8, pre_w, w_in_bf, *ride_srcs)


def _matmul_tn(name, a, b, tn, tm=512, by_owner_cols=False):
    K, M = a.shape
    Nc = b.shape[1]
    tm = min(tm, M)

    def body(a_ref, b_ref, o_ref):
        o_ref[...] = _bf(_dot_tn(a_ref[...], b_ref[...]))

    if by_owner_cols:
        assert tn * N_DEV == Nc
        out_shape = SDS((N_DEV, M, tn), BF16)
        out_spec = pl.BlockSpec((None, tm, tn), lambda i, j: (j, i, 0))
    else:
        out_shape = SDS((M, Nc), BF16)
        out_spec = pl.BlockSpec((tm, tn), lambda i, j: (i, j))
    return pl.pallas_call(
        body, name=name, grid=(M // tm, Nc // tn),
        in_specs=[pl.BlockSpec((K, tm), lambda i, j: (0, i)),
                  pl.BlockSpec((K, tn), lambda i, j: (0, j))],
        out_specs=out_spec, out_shape=out_shape,
        compiler_params=_params(("arbitrary", "arbitrary"), VMEM_LIMIT_BIG),
    )(a, b)


def _adamw_math(w, g, m, v):
    m2 = ADAM_B1 * m + (1.0 - ADAM_B1) * g
    v2 = ADAM_B2 * v + (1.0 - ADAM_B2) * (g * g)
    m_hat = m2 / (1.0 - ADAM_B1 ** ADAM_STEP)
    v_hat = v2 / (1.0 - ADAM_B2 ** ADAM_STEP)
    delta = -ADAM_LR * (m_hat / (jnp.sqrt(v_hat) + ADAM_EPS) + ADAM_WD * w)
    return delta, m2, v2


def _pair_add(name, gw, theirs):
    chips, _, r, c = gw.shape
    tr = r
    core = lax.axis_index("c").astype(jnp.int32).reshape(1)

    def body(core_ref, mine_ref, theirs_ref, o_ref):
        o_ref[...] = _bf(mine_ref[...].astype(F32) + theirs_ref[...].astype(F32))

    block = pl.BlockSpec((None, tr, c), lambda s, i, core_ref: (s, i, 0))
    grid_spec = pltpu.PrefetchScalarGridSpec(
        num_scalar_prefetch=1, grid=(chips, r // tr),
        in_specs=[pl.BlockSpec((None, None, tr, c), lambda s, i, core_ref: (s, core_ref[0], i, 0)), block],
        out_specs=block)
    return pl.pallas_call(
        body, name=name, grid_spec=grid_spec, out_shape=SDS((chips, r, c), BF16),
        compiler_params=_params(("arbitrary", "arbitrary")),
    )(core, gw, theirs)


def _reduce_adamw(name, parts, w, m, v):
    r, c = w.shape
    tr = r if r % 256 else 256
    slots = parts.shape[0]

    def body(p_ref, w_ref, m_ref, v_ref, g_ref, d_ref, m2_ref, v2_ref):
        g = p_ref[0].astype(F32)
        for s in range(1, slots):
            g = g + p_ref[s].astype(F32)
        g_ref[...] = g
        d_ref[...], m2_ref[...], v2_ref[...] = _adamw_math(w_ref[...], g, m_ref[...], v_ref[...])

    blk = pl.BlockSpec((tr, c), lambda i: (i, 0))
    return pl.pallas_call(
        body, name=name, grid=(r // tr,),
        in_specs=[pl.BlockSpec((slots, tr, c), lambda i: (0, i, 0)), blk, blk, blk],
        out_specs=[blk] * 4, out_shape=[SDS((r, c), F32)] * 4,
        compiler_params=_params(("arbitrary",), VMEM_LIMIT_BIG),
    )(parts, w, m, v)


def _ada_grad_adamw(c_all, dmod_all, w, m, v):
    r, c = w.shape
    tr = 256
    nb = c_all.shape[0]

    def body(c_ref, dm_ref, w_ref, m_ref, v_ref, g_ref, d_ref, m2_ref, v2_ref):
        cv = c_ref[...]
        g = _dot_tn(cv * _sigmoid(cv), dm_ref[...])
        g_ref[...] = g
        d_ref[...], m2_ref[...], v2_ref[...] = _adamw_math(w_ref[...], g, m_ref[...], v_ref[...])

    blk = pl.BlockSpec((tr, c), lambda i: (i, 0))
    return pl.pallas_call(
        body, name="ada_grad_adamw", grid=(r // tr,),
        in_specs=[pl.BlockSpec((nb, tr), lambda i: (0, i)), pl.BlockSpec((nb, c), lambda i: (0, 0)),
                  blk, blk, blk],
        out_specs=[blk] * 4, out_shape=[SDS((r, c), F32)] * 4,
        compiler_params=_params(("arbitrary",)),
    )(c_all, dmod_all, w, m, v)


_SMALL = [("b_ada", 6144), ("pre_w_mix", 1024), ("attn_sinks", 128), ("attn_out_w", 512), ("lb_table", 1024),
          ("hg_norm_w", 128), ("post_w_mix", 1024), ("pre_w_mlp", 1024), ("post_w_mlp", 1024)]


def _pack_small(vals, loss_part):
    out = []
    for name, width in _SMALL:
        f = vals[name].reshape(-1).astype(F32)
        out.append(jnp.pad(f, (0, width - f.shape[0])))
    out.append(jnp.broadcast_to(loss_part, (LANES,)))
    return jnp.concatenate(out).reshape(1, -1)


def _adamw_small(parts, given):
    names = [n for n, _ in _SMALL]
    flat_in = [a for n in names for a in given[n]]

    def body(*refs):
        p_ref = refs[0]
        in_refs = refs[1:1 + 3 * len(names)]
        out_refs = refs[1 + 3 * len(names):-1]
        loss_ref = refs[-1]
        g = p_ref[0]
        for s in range(1, N_DEV):
            g = g + p_ref[s]
        off = 0
        for i, (name, width) in enumerate(_SMALL):
            w_ref, m_ref, v_ref = in_refs[3 * i:3 * i + 3]
            rows, cols = w_ref.shape
            for r in range(rows):
                gr = g[:, off + r * cols:off + (r + 1) * cols]
                res = (gr,) + _adamw_math(w_ref[r:r + 1, :], gr, m_ref[r:r + 1, :], v_ref[r:r + 1, :])
                for o_ref, val in zip(out_refs[4 * i:4 * i + 4], res):
                    o_ref[r:r + 1, :] = val
            off += width
        loss_ref[...] = g[:, off:off + LANES]

    out_shape = [SDS(given[n][0].shape, F32) for n in names for _ in range(4)] + [SDS((1, LANES), F32)]
    outs = pl.pallas_call(body, name="adamw_small", out_shape=out_shape)(parts, *flat_in)
    return {n: tuple(outs[4 * i:4 * i + 4]) for i, n in enumerate(names)}, outs[-1][0, 0]


def kernel(x, c, w_ada, b_ada, pre_w_mix, w_in, attn_sinks, attn_out_w, lb_table, hg_norm_w, w_out, post_w_mix, pre_w_mlp, w_up, w_down, post_w_mlp, loss_target, m_w_ada, m_b_ada, m_pre_w_mix, m_w_in, m_attn_sinks, m_attn_out_w, m_lb_table, m_hg_norm_w, m_w_out, m_post_w_mix, m_pre_w_mlp, m_w_up, m_w_down, m_post_w_mlp, v_w_ada, v_b_ada, v_pre_w_mix, v_w_in, v_attn_sinks, v_attn_out_w, v_lb_table, v_hg_norm_w, v_w_out, v_post_w_mix, v_pre_w_mlp, v_w_up, v_w_down, v_post_w_mlp):
    B, T, _ = x.shape
    N = B * T
    me = 4 * lax.axis_index("x") + 2 * lax.axis_index("y") + lax.axis_index("c")
    x2 = x.reshape(N, D_MODEL)
    tgt2 = loss_target.reshape(N, D_MODEL)

    w_in_t, m_w_in_t, v_w_in_t = w_in[0].T, m_w_in[0].T, v_w_in[0].T
    w_in_g, c_g = _exchange("gather_w_in", [_bf(w_in_t), c], ["gather"] * 2)
    w_in_f = w_in_g.reshape(IN_COLS, D_MODEL)
    c_all = c_g.reshape(N_DEV * B, D_MODEL)

    ada_cols = w_ada.shape[2]
    b_mine = lax.dynamic_slice(b_ada, (0, me * ada_cols), (1, ada_cols))
    mod_cols = _ada_mod(c_all, w_ada[0], b_mine)
    (mod_g,) = _exchange("scatter_mod", [mod_cols.reshape(N_DEV, B, ada_cols)], ["a2a"])
    mod = mod_g.transpose(1, 0, 2).reshape(B, 6, D_MODEL)
    mod8 = jnp.pad(mod, ((0, 0), (0, 2), (0, 0)))

    lb_p = jax.nn.softmax(lb_table, axis=0)
    lb = lb_p[1:2]
    tables = _rope_tables(T)

    w_up_b, w_down_b = _bf(w_up[0]), _bf(w_down[0])
    proj_a, proj_h, h1, w_out_g, w_up_g0 = _in_proj(x2, mod8, pre_w_mix, w_in_f, T,
                                                    [_bf(w_out[0]), w_up_b[:MLP_HALF]], ["gather"] * 2)
    proj3 = proj_a.reshape(B, T, ATT_COLS)
    proj_h = proj_h.reshape(B, T, IN_COLS - ATT_COLS)
    rec_o, rec_g, s_prev, w_up_g1 = _hgrn_fwd(proj_h, lb, hg_norm_w, [w_up_b[MLP_HALF:]], ["gather"])
    attn_o, attn_n, qr, kr, w_down_g0 = _attn_fwd(proj3, tables, attn_sinks, attn_out_w,
                                                  [w_down_b[:, :MLP_HALF]], ["gather"])
    w_out_f = w_out_g.reshape(D_MODEL, D_MODEL)
    mix, x1, cat, w_down_g1 = _mix_out(x2, attn_n.reshape(N, ATT_WIDTH), rec_g.reshape(N, HG_WIDTH), mod8,
                                       post_w_mix, w_out_f, T, [w_down_b[:, MLP_HALF:]], ["gather"])
    w_up_halves = [w_up_g0, w_up_g1]
    w_down_halves = [w_down_g0.reshape(D_FF, MLP_HALF), w_down_g1.reshape(D_FF, MLP_HALF)]
    up, d, h2 = _mlp_fwd(x1, mod8, pre_w_mlp, w_up_halves, w_down_halves, T)

    dx1, u, dup, dd, acc_mlp = _mlp_bwd(x1, d, up, tgt2, mod8, pre_w_mlp, post_w_mlp,
                                        w_up_halves, w_down_halves, T)
    chips = N_DEV // 2
    by_chip = lambda a: a.reshape((chips, 2, a.shape[0] // N_DEV) + a.shape[1:])
    gw_up = _matmul_tn("grad_w_up", h2, dup, D_FF // N_DEV, by_owner_cols=True)
    gw_up = gw_up.reshape(chips, 2, D_MODEL, D_FF // N_DEV)
    gw_down = by_chip(_matmul_tn("grad_w_down", u, dd, 512))
    dan, drg, dmix, acc_mix, q_down, q_up = _mix_bwd(mix, dx1, mod8, post_w_mix, w_out_f, T,
                                                     [gw_down, gw_up], ["pair"] * 2)
    p_down, p_up = _pair_add("pair_add_w_down", gw_down, q_down), _pair_add("pair_add_w_up", gw_up, q_up)
    gw_out = _matmul_tn("grad_w_out", cat, dmix, 512).reshape(N_DEV, D_MODEL // N_DEV, D_MODEL)
    dhq, dhf, dhi, dhg, dlb_p, dgw_p, r_down, r_up = _hgrn_bwd(
        proj_h, lb, hg_norm_w, rec_o, s_prev, drg.reshape(B, T, HG_WIDTH), [p_down, p_up], ["chips"] * 2)
    dqkv, dsink_p, daw_p, r_out = _attn_bwd(qr, kr, proj3, attn_o, dan.reshape(B, T, ATT_WIDTH), tables,
                                            attn_sinks, attn_out_w, [gw_out], ["a2a"])
    flat = lambda a: a.reshape(N, a.shape[-1])
    grad_x, dproj, acc_in = _in_bwd(x2, dx1, flat(dqkv), flat(dhq), flat(dhf), flat(dhi), flat(dhg),
                                    mod8, pre_w_mix, w_in_f, T, [], [])

    gw_in = by_chip(_matmul_tn("grad_w_in", dproj, h1, 512, tm=IN_COLS // 2))
    (q_in,) = _exchange("pair_w_in", [gw_in], ["pair"])
    p_in = _pair_add("pair_add_w_in", gw_in, q_in)

    dmod = jnp.concatenate([acc_in[:, 0:2], acc_mix[:, 0:1], acc_mlp[:, 0:3]], axis=1)
    dlb = dlb_p[:, 0].sum(0)
    dlb_table = jnp.stack([-dlb, dlb]) * (lb_p[0] * lb_p[1])[None, :]
    small = {
        "b_ada": dmod.sum(0),
        "pre_w_mix": acc_in[:, 2].sum(0),
        "attn_sinks": dsink_p[:, :, 0].sum(0),
        "attn_out_w": daw_p[:, 0].sum(0),
        "lb_table": dlb_table,
        "hg_norm_w": dgw_p[:, 0].sum(0),
        "post_w_mix": acc_mix[:, 1].sum(0),
        "pre_w_mlp": acc_mlp[:, 3].sum(0),
        "post_w_mlp": acc_mlp[:, 4].sum(0),
    }
    loss_part = acc_mlp[:, 5, 0].sum()
    dmod_blocks = dmod.reshape(B, N_DEV, ada_cols).transpose(1, 0, 2)

    r_in, r_dmod, r_small = _exchange(
        "reduce_grads", [p_in, dmod_blocks, _pack_small(small, loss_part)], ["chips", "a2a", "gather"])

    res = {}
    res["w_in"] = tuple(a.T for a in _reduce_adamw("adamw_w_in", r_in, w_in_t, m_w_in_t, v_w_in_t))
    res["w_out"] = _reduce_adamw("adamw_w_out", r_out, w_out[0], m_w_out[0], v_w_out[0])
    res["w_up"] = _reduce_adamw("adamw_w_up", r_up, w_up[0], m_w_up[0], v_w_up[0])
    res["w_down"] = _reduce_adamw("adamw_w_down", r_down, w_down[0], m_w_down[0], v_w_down[0])
    res["w_ada"] = _ada_grad_adamw(c_all, r_dmod.reshape(N_DEV * B, ada_cols), w_ada[0], m_w_ada[0], v_w_ada[0])

    given = dict(b_ada=(b_ada, m_b_ada, v_b_ada), pre_w_mix=(pre_w_mix, m_pre_w_mix, v_pre_w_mix),
                 attn_sinks=(attn_sinks, m_attn_sinks, v_attn_sinks),
                 attn_out_w=(attn_out_w, m_attn_out_w, v_attn_out_w), lb_table=(lb_table, m_lb_table, v_lb_table),
                 hg_norm_w=(hg_norm_w, m_hg_norm_w, v_hg_norm_w), post_w_mix=(post_w_mix, m_post_w_mix, v_post_w_mix),
                 pre_w_mlp=(pre_w_mlp, m_pre_w_mlp, v_pre_w_mlp), post_w_mlp=(post_w_mlp, m_post_w_mlp, v_post_w_mlp))
    small_res, loss = _adamw_small(r_small, given)
    res.update(small_res)

    order = ["w_ada", "b_ada", "pre_w_mix", "w_in", "attn_sinks", "attn_out_w", "lb_table", "hg_norm_w", "w_out",
             "post_w_mix", "pre_w_mlp", "w_up", "w_down", "post_w_mlp"]
    big = {"w_ada", "w_in", "w_out", "w_up", "w_down"}
    outs = [loss, grad_x.reshape(B, T, D_MODEL)]
    for i in range(4):
        for k in order:
            a = res[k][i]
            outs.append(a[None] if k in big else a)
    return tuple(outs)
```

```python
import jax
import jax.numpy as jnp
from jax import lax
from jax.experimental import pallas as pl
from jax.experimental.pallas import tpu as pltpu

F32 = jnp.float32
BF16 = jnp.bfloat16
SDS = jax.ShapeDtypeStruct

D_MODEL = 1024
ATT_WIDTH = 512
ATT_HEAD_DIM = 64
ATT_KV_HEADS = 2
ATT_GROUP = 4
WINDOW = 128
ROPE_DIM = 16
ROPE_THETA = 500000.0
HG_WIDTH = 512
HG_HEAD_DIM = 128
HG_HEADS = 4
HG_CHUNK = 32
IN_COLS = 2816
ATT_COLS = 768
D_FF = 4096
EPS = 1e-6
N_DEV = 8

ADAM_LR = 0.001
ADAM_B1 = 0.9
ADAM_B2 = 0.999
ADAM_EPS = 1e-08
ADAM_WD = 0.01
ADAM_STEP = 10

VMEM_LIMIT_BIG = 56 << 20
LANES = 128

MESH = pl.DeviceIdType.MESH
NT_DIMS = (((1,), (1,)), ((), ()))
TN_DIMS = (((0,), (0,)), ((), ()))


def _dot(a, b):
    return jnp.dot(a, b, preferred_element_type=F32)


def _dot_nt(a, b):
    return lax.dot_general(a, b, NT_DIMS, preferred_element_type=F32)


def _dot_tn(a, b):
    return lax.dot_general(a, b, TN_DIMS, preferred_element_type=F32)


def _bf(a):
    return a.astype(BF16)


def _sigmoid(a):
    return 1.0 / (1.0 + jnp.exp(-a))


def _mean_last(a):
    return jnp.mean(a, axis=-1, keepdims=True)


def _sum_rows(a):
    return jnp.sum(a, axis=0, keepdims=True)


def _tri_sum(tri_bf, a, terms=3):
    a1 = _bf(a)
    r1 = a - a1.astype(F32)
    a2 = _bf(r1)
    out = _dot(tri_bf, a1) + _dot(tri_bf, a2)
    if terms == 3:
        out = out + _dot(tri_bf, _bf(r1 - a2.astype(F32)))
    return out


def _loop_pairs(first, count, body, init):
    if count % 2:
        return lax.fori_loop(first, first + count, body, init)
    return lax.fori_loop(0, count // 2, lambda i, c: body(first + 2 * i + 1, body(first + 2 * i, c)), init)


def _params(sem=None, vmem=None):
    kw = {}
    if sem is not None:
        kw["dimension_semantics"] = sem
    if vmem is not None:
        kw["vmem_limit_bytes"] = vmem
    return pltpu.CompilerParams(**kw)


ANY_SPEC = pl.BlockSpec(memory_space=pl.ANY)


def _exchange_shapes(srcs, modes):
    out_shape = []
    for s, m in zip(srcs, modes):
        shp = {"gather": (N_DEV,) + tuple(s.shape), "pair": (s.shape[0],) + tuple(s.shape[2:])}.get(m, tuple(s.shape))
        out_shape.append(SDS(shp, s.dtype))
    return out_shape


def _exchange_sems(n):
    if n == 0:
        return []
    return [pltpu.SemaphoreType.DMA((n, N_DEV - 1)), pltpu.SemaphoreType.DMA((n, N_DEV - 1)),
            pltpu.SemaphoreType.DMA((n,))]


SIBLING = 1
OTHER_CHIPS = (2, 4, 6)


def _related(k):
    x, y, c = lax.axis_index("x"), lax.axis_index("y"), lax.axis_index("c")
    px, py, pc = x ^ ((k >> 2) & 1), y ^ ((k >> 1) & 1), c ^ (k & 1)
    return (px, py, pc), 4 * px + 2 * py + pc


def _exchange_phases(modes, src_refs, out_refs, send_sems, recv_sems, own_sems):
    _, me = _related(0)
    sib_dev, sib = _related(SIBLING)
    start, middle, end = [], [], []

    def remote(a, i, src, dst, dev):
        return pltpu.make_async_remote_copy(src_ref=src, dst_ref=dst, send_sem=send_sems.at[a, i],
                                            recv_sem=recv_sems.at[a, i], device_id=dev, device_id_type=MESH)

    for a, mode in enumerate(modes):
        out = out_refs[a]
        if mode == "gather":
            src = src_refs[a]
            own = pltpu.make_async_copy(src, out.at[me], own_sems.at[a])
            to_sib = remote(a, 0, src, out.at[me], sib_dev)
            start += [own.start, to_sib.start]
            end += [remote(a, 0, src, out.at[sib], sib_dev).wait_recv, to_sib.wait_send, own.wait]
            for j, k in enumerate(OTHER_CHIPS, start=1):
                dev, peer = _related(k)
                _, peer_sib = _related(k ^ SIBLING)
                send = remote(a, j, src, out.at[me], dev)
                passed = remote(a, 3 + j, out.at[peer], out.at[peer], sib_dev)
                start.append(send.start)
                middle += [remote(a, j, src, out.at[peer], dev).wait_recv, passed.start]
                end += [remote(a, 3 + j, out.at[peer_sib], out.at[peer_sib], sib_dev).wait_recv,
                        send.wait_send, passed.wait_send]
        elif mode == "pair":
            core = lax.axis_index("c")
            for s in range(N_DEV // 2):
                send = remote(a, s, src_refs[a].at[s, 1 - core], out.at[s], sib_dev)
                start.append(send.start)
                end += [remote(a, s, src_refs[a].at[s, 1 - core], out.at[s], sib_dev).wait_recv, send.wait_send]
        elif mode == "chips":
            chip = me // 2
            own = pltpu.make_async_copy(src_refs[a].at[chip], out.at[chip], own_sems.at[a])
            start.append(own.start)
            end.append(own.wait)
            for j, k in enumerate(OTHER_CHIPS, start=1):
                dev, peer = _related(k)
                send = remote(a, j, src_refs[a].at[peer // 2], out.at[chip], dev)
                start.append(send.start)
                end += [remote(a, j, src_refs[a].at[peer // 2], out.at[peer // 2], dev).wait_recv, send.wait_send]
        else:
            own = pltpu.make_async_copy(src_refs[a].at[me], out.at[me], own_sems.at[a])
            start.append(own.start)
            end.append(own.wait)
            for k in range(1, N_DEV):
                dev, peer = _related(k)
                send = remote(a, k - 1, src_refs[a].at[peer], out.at[me], dev)
                start.append(send.start)
                end += [remote(a, k - 1, src_refs[a].at[peer], out.at[peer], dev).wait_recv, send.wait_send]
    return start, middle, end


def _run(actions):
    for act in actions:
        act()


def _exchange(name, srcs, modes):
    n = len(srcs)

    def body(*refs):
        start, middle, end = _exchange_phases(modes, refs[:n], refs[n:2 * n], *refs[2 * n:])
        _run(start)
        _run(middle)
        _run(end)

    return pl.pallas_call(
        body, name=name, out_shape=_exchange_shapes(srcs, modes),
        in_specs=[ANY_SPEC] * n, out_specs=[ANY_SPEC] * n,
        scratch_shapes=_exchange_sems(n),
    )(*srcs)


def _ride_start(modes, step, steps, src_refs, out_refs, sems):
    if not modes:
        return
    middle_step = steps - 1

    @pl.when(step == 0)
    def _():
        _run(_exchange_phases(modes, src_refs, out_refs, *sems)[0])

    if "gather" in modes:
        @pl.when(step == middle_step)
        def _():
            _run(_exchange_phases(modes, src_refs, out_refs, *sems)[1])


def _ride_wait(modes, step, steps, src_refs, out_refs, sems):
    if not modes:
        return

    @pl.when(step == steps - 1)
    def _():
        _run(_exchange_phases(modes, src_refs, out_refs, *sems)[2])


def _ada_mod(c_all, w_ada, b_ada_mine):
    nb, cols = c_all.shape[0], w_ada.shape[1]

    def body(c_ref, w_ref, b_ref, o_ref):
        cv = c_ref[...]
        ca = cv * _sigmoid(cv)
        o_ref[...] = _dot(ca, w_ref[...]) + b_ref[...]

    return pl.pallas_call(body, name="ada_mod", out_shape=SDS((nb, cols), F32))(c_all, w_ada, b_ada_mine)


def _tile_rows(T, big=False):
    return min(512 if big else 256, T)


def _mod_spec(tps):
    return pl.BlockSpec((None, 8, D_MODEL), lambda i: (i // tps, 0, 0))


def _in_proj(x2, mod8, pre_w, w_in_bf, T, ride_srcs, ride_modes):
    N = x2.shape[0]
    TM = _tile_rows(T, big=True)
    tps = T // TM
    nr = len(ride_srcs)

    def body(*refs):
        x_ref, mod_ref, pw_ref, w_ref = refs[:4]
        ride_in = refs[4:4 + nr]
        pa_ref, ph_ref, h1_ref = refs[4 + nr:7 + nr]
        ride_out = refs[7 + nr:7 + 2 * nr]
        sems = refs[7 + 2 * nr:]
        _ride_start(ride_modes, pl.program_id(0), N // TM, ride_in, ride_out, sems)
        x = x_ref[...]
        r = lax.rsqrt(_mean_last(x * x) + EPS)
        h = (x * r * pw_ref[...]) * (1.0 + mod_ref[1:2, :]) + mod_ref[0:1, :]
        hb = _bf(h)
        h1_ref[...] = hb
        pa_ref[...] = _dot_nt(hb, w_ref[:ATT_COLS, :])
        ph_ref[...] = _dot_nt(hb, w_ref[ATT_COLS:, :])
        _ride_wait(ride_modes, pl.program_id(0), N // TM, ride_in, ride_out, sems)

    return pl.pallas_call(
        body, name="in_proj", grid=(N // TM,),
        in_specs=[pl.BlockSpec((TM, D_MODEL), lambda i: (i, 0)), _mod_spec(tps),
                  pl.BlockSpec((1, D_MODEL), lambda i: (0, 0)),
                  pl.BlockSpec((IN_COLS, D_MODEL), lambda i: (0, 0))] + [ANY_SPEC] * nr,
        out_specs=[pl.BlockSpec((TM, ATT_COLS), lambda i: (i, 0)),
                   pl.BlockSpec((TM, IN_COLS - ATT_COLS), lambda i: (i, 0)),
                   pl.BlockSpec((TM, D_MODEL), lambda i: (i, 0))] + [ANY_SPEC] * nr,
        out_shape=[SDS((N, ATT_COLS), F32), SDS((N, IN_COLS - ATT_COLS), F32), SDS((N, D_MODEL), BF16)]
        + _exchange_shapes(ride_srcs, ride_modes),
        scratch_shapes=_exchange_sems(nr),
        compiler_params=_params(("arbitrary",), VMEM_LIMIT_BIG),
    )(x2, mod8, pre_w, w_in_bf, *ride_srcs)


def _rope_tables(T):
    half = ROPE_DIM // 2
    inv_freq = ROPE_THETA ** (-jnp.arange(0, ROPE_DIM, 2, dtype=F32) / ROPE_DIM)
    ang = jnp.arange(T, dtype=F32)[:, None] * inv_freq[None, :]
    cos, sin = jnp.cos(ang), jnp.sin(ang)
    ones = jnp.ones((T, ATT_HEAD_DIM - ROPE_DIM), F32)
    zeros = jnp.zeros((T, ATT_HEAD_DIM - ROPE_DIM), F32)
    zh = jnp.zeros((T, half), F32)
    cos64 = jnp.concatenate([cos, cos, ones], axis=1)
    sin_left = jnp.concatenate([-sin, zh, zeros], axis=1)
    sin_right = jnp.concatenate([zh, sin, zeros], axis=1)
    rep = LANES // ATT_HEAD_DIM
    return jnp.tile(cos64, (1, rep)), jnp.tile(sin_left, (1, rep)), jnp.tile(sin_right, (1, rep))


def _rope(xc, cs, sl, sr):
    return xc * cs + pltpu.roll(xc, LANES - 8, 1) * sl + pltpu.roll(xc, 8, 1) * sr


def _rope_t(dy, cs, sl, sr):
    return dy * cs + pltpu.roll(dy * sl, 8, 1) + pltpu.roll(dy * sr, LANES - 8, 1)


ATT_SCALE = ATT_HEAD_DIM ** -0.5
ATT_SPLITS = 4


def _band_masks():
    cols = ATT_GROUP * WINDOW
    j = lax.broadcasted_iota(jnp.int32, (2 * WINDOW, cols), 0)
    i = lax.broadcasted_iota(jnp.int32, (2 * WINDOW, cols), 1) & (WINDOW - 1)
    diff = i + WINDOW - j
    return (diff >= 0) & (diff < WINDOW), j >= WINDOW


def _sink_row(sink_ref, hk):
    return jnp.concatenate(
        [jnp.full((1, WINDOW), sink_ref[0, ATT_GROUP * hk + g], F32) for g in range(ATT_GROUP)], axis=1)


def _softmax_band(qs, kk, mask, sink):
    s = jnp.where(mask, _dot_nt(kk, qs), jnp.finfo(F32).min)
    m = jnp.maximum(jnp.max(s, axis=0, keepdims=True), sink)
    p = jnp.exp(s - m)
    es = jnp.exp(sink - m)
    inv = 1.0 / (jnp.sum(p, axis=0, keepdims=True) + es)
    return p, inv, es


def _stack_heads(parts, hk):
    hs = []
    for g in range(ATT_GROUP):
        h = ATT_GROUP * hk + g
        hs.append(parts[h // 2][:, (h % 2) * ATT_HEAD_DIM:(h % 2 + 1) * ATT_HEAD_DIM])
    return jnp.concatenate(hs, axis=0)


def _attn_fwd(proj3, tables, sinks, attn_w, ride_srcs, ride_modes):
    B, T, _ = proj3.shape
    nb = T // WINDOW
    splits = min(ATT_SPLITS, nb)
    per = nb // splits
    nr = len(ride_srcs)
    cos, sinl, sinr = tables

    def body(*refs):
        q_ref, k_ref, v_ref, cos_ref, sl_ref, sr_ref, sink_ref, aw_ref = refs[:8]
        ride_in = refs[8:8 + nr]
        o_ref, an_ref, qr_ref, kr_ref = refs[8 + nr:12 + nr]
        ride_out = refs[12 + nr:12 + 2 * nr]
        kpad, vpad = refs[12 + 2 * nr:14 + 2 * nr]
        sems = refs[14 + 2 * nr:]
        part = pl.program_id(1)
        step = pl.program_id(0) * splits + part
        _ride_start(ride_modes, step, B * splits, ride_in, ride_out, sems)

        @pl.when(part == 0)
        def _():
            kpad[0:WINDOW, :] = jnp.zeros((WINDOW, LANES), BF16)
            vpad[0:WINDOW, :] = jnp.zeros((WINDOW, LANES), BF16)

        window, current = _band_masks()

        def block(n, carry):
            r0 = pl.multiple_of(n * WINDOW, WINDOW)
            rows = pl.ds(r0, WINDOW)
            nxt = pl.ds(r0 + WINDOW, WINDOW)
            band = pl.ds(r0, 2 * WINDOW)
            cs, sl, sr = cos_ref[rows, :], sl_ref[rows, :], sr_ref[rows, :]
            kb = _bf(_rope(k_ref[rows, :], cs, sl, sr))
            kpad[nxt, :] = kb
            kr_ref[rows, :] = kb
            vpad[nxt, :] = _bf(v_ref[rows, :])
            qparts = []
            for j in range(ATT_WIDTH // LANES):
                qp = _bf(_rope(q_ref[rows, j * LANES:(j + 1) * LANES], cs, sl, sr) * ATT_SCALE)
                qr_ref[rows, j * LANES:(j + 1) * LANES] = qp
                qparts.append(qp)
            mask = window & (current | (n > 0))
            for hk in range(ATT_KV_HEADS):
                lanes = slice(hk * ATT_HEAD_DIM, (hk + 1) * ATT_HEAD_DIM)
                qs = _stack_heads(qparts, hk)
                p, inv, _ = _softmax_band(qs, kpad[band, lanes], mask, _sink_row(sink_ref, hk))
                ot = _dot_tn(vpad[band, lanes], _bf(p)) * inv
                for g in range(ATT_GROUP):
                    h = ATT_GROUP * hk + g
                    o_ref[rows, h * ATT_HEAD_DIM:(h + 1) * ATT_HEAD_DIM] = ot[:, g * WINDOW:(g + 1) * WINDOW].T
            ob = o_ref[rows, :]
            an_ref[rows, :] = _bf(ob * lax.rsqrt(_mean_last(ob * ob) + EPS) * aw_ref[...])
            return carry

        _loop_pairs(part * per, per, block, 0)
        _ride_wait(ride_modes, step, B * splits, ride_in, ride_out, sems)

    seq = lambda w, j: pl.BlockSpec((None, T, w), lambda b, s: (b, 0, j))
    full = lambda r, w: pl.BlockSpec((r, w), lambda b, s: (0, 0))
    return pl.pallas_call(
        body, name="attn_fwd", grid=(B, splits),
        in_specs=[seq(ATT_WIDTH, 0), seq(LANES, 4), seq(LANES, 5),
                  full(T, LANES), full(T, LANES), full(T, LANES),
                  pl.BlockSpec(memory_space=pltpu.SMEM), full(1, ATT_WIDTH)] + [ANY_SPEC] * nr,
        out_specs=[seq(ATT_WIDTH, 0), seq(ATT_WIDTH, 0), seq(ATT_WIDTH, 0), seq(LANES, 0)] + [ANY_SPEC] * nr,
        out_shape=[SDS((B, T, ATT_WIDTH), F32), SDS((B, T, ATT_WIDTH), BF16),
                   SDS((B, T, ATT_WIDTH), BF16), SDS((B, T, LANES), BF16)] + _exchange_shapes(ride_srcs, ride_modes),
        scratch_shapes=[pltpu.VMEM((T + WINDOW, LANES), BF16), pltpu.VMEM((T + WINDOW, LANES), BF16)]
        + _exchange_sems(nr),
        compiler_params=_params(("arbitrary", "arbitrary"), VMEM_LIMIT_BIG),
    )(proj3, proj3, proj3, cos, sinl, sinr, sinks, attn_w, *ride_srcs)


HG_GROUP = 8
HG_ROWS = HG_GROUP * HG_CHUNK


HG_STACK = HG_GROUP * HG_HEAD_DIM


def _group_masks():
    r = lax.broadcasted_iota(jnp.int32, (HG_ROWS, HG_ROWS), 0)
    c = lax.broadcasted_iota(jnp.int32, (HG_ROWS, HG_ROWS), 1)
    same = (r // HG_CHUNK) == (c // HG_CHUNK)
    return same & (r >= c), same & (c >= r)


def _row_chunk():
    return lax.broadcasted_iota(jnp.int32, (HG_ROWS, HG_HEAD_DIM), 0) // HG_CHUNK


def _spread(a, row_chunk):
    return jnp.concatenate([jnp.where(row_chunk == c, a, jnp.zeros_like(a)) for c in range(HG_GROUP)], axis=1)


def _pick(r, row_chunk):
    out = jnp.where(row_chunk == 0, r[:, :HG_HEAD_DIM], 0.0)
    for c in range(1, HG_GROUP):
        out = out + jnp.where(row_chunk == c, r[:, c * HG_HEAD_DIM:(c + 1) * HG_HEAD_DIM], 0.0)
    return out


def _lane_block(a, c):
    return a[:, c * HG_HEAD_DIM:(c + 1) * HG_HEAD_DIM]


def _ones_bf(mask):
    return jnp.where(mask, 1.0, 0.0).astype(BF16)


def _chunk_bcast(rows_1x128):
    return jnp.concatenate([jnp.broadcast_to(r, (HG_CHUNK, HG_HEAD_DIM)) for r in rows_1x128], axis=0)


def _hgrn_gates(hq, hf, lb, lower_bf):
    sq = _sigmoid(hq)
    q = hq * sq
    sg = _sigmoid(hf)
    f = lb + (1.0 - lb) * sg
    k = 1.0 - f
    logf = jnp.log(f)
    b = _tri_sum(lower_bf, logf)
    bl = [_sum_rows(logf[_chunk_rows(c), :]) for c in range(HG_GROUP)]
    eb, enb, e2 = jnp.exp(b), jnp.exp(-b), jnp.exp(_chunk_bcast(bl) - b)
    ebl = [jnp.exp(r) for r in bl]
    return dict(sq=sq, sg=sg, f=f, eb=eb, enb=enb, e2=e2, ebl=ebl, qd=q * eb, kd=k * enb, k2=k * e2)


def _chunk_rows(c):
    return slice(c * HG_CHUNK, (c + 1) * HG_CHUNK)


def _head_lanes(h):
    return slice(h * HG_HEAD_DIM, (h + 1) * HG_HEAD_DIM)


def _hgrn_fwd(proj_h, lb, hg_w, ride_srcs, ride_modes):
    B, T, _ = proj_h.shape
    ng = T // HG_ROWS
    nr = len(ride_srcs)

    def body(*refs):
        hq_ref, hf_ref, hi_ref, hg_ref, lb_ref, gw_ref = refs[:6]
        ride_in = refs[6:6 + nr]
        o_ref, rg_ref, sp_ref = refs[6 + nr:9 + nr]
        ride_out = refs[9 + nr:9 + 2 * nr]
        st = refs[9 + 2 * nr]
        sems = refs[10 + 2 * nr:]
        gi = pl.program_id(1)
        step = pl.program_id(0) * ng + gi
        _ride_start(ride_modes, step, B * ng, ride_in, ride_out, sems)

        @pl.when(gi == 0)
        def _():
            st[...] = jnp.zeros(st.shape, F32)

        lo, _ = _group_masks()
        lower_bf = _ones_bf(lo)
        row_chunk = _row_chunk()
        for h in range(HG_HEADS):
            lanes = _head_lanes(h)
            gt = _hgrn_gates(hq_ref[:, lanes], hf_ref[:, lanes], lb_ref[:, lanes], lower_bf)
            v, qd, kd = _bf(hi_ref[:, lanes]), _bf(gt["qd"]), _bf(gt["kd"])
            a = jnp.where(lo, _dot_nt(qd, kd), 0.0)
            kv = _dot_tn(v, _bf(_spread(gt["k2"], row_chunk)))
            s = st[h]
            before = []
            for c in range(HG_GROUP):
                before.append(s)
                s = s * gt["ebl"][c] + _lane_block(kv, c)
            st[h] = s
            sp = jnp.concatenate(before, axis=1)
            sp_ref[h] = sp
            o = _dot(_bf(a), v) + _dot_nt(_bf(_spread(gt["qd"], row_chunk)), _bf(sp))
            o_ref[:, lanes] = o
            hg = hg_ref[:, lanes]
            rn = o * lax.rsqrt(_mean_last(o * o) + EPS) * gw_ref[...]
            rg_ref[:, lanes] = _bf(rn * (hg * _sigmoid(hg)))
        _ride_wait(ride_modes, step, B * ng, ride_in, ride_out, sems)

    part = lambda j: pl.BlockSpec((None, HG_ROWS, HG_WIDTH), lambda b, g: (b, g, j))
    return pl.pallas_call(
        body, name="hgrn_fwd", grid=(B, ng),
        in_specs=[part(0), part(1), part(2), part(3),
                  pl.BlockSpec((1, HG_WIDTH), lambda b, g: (0, 0)),
                  pl.BlockSpec((1, LANES), lambda b, g: (0, 0))] + [ANY_SPEC] * nr,
        out_specs=[part(0), part(0),
                   pl.BlockSpec((None, HG_HEADS, None, HG_HEAD_DIM, HG_STACK), lambda b, g: (b, 0, g, 0, 0))]
        + [ANY_SPEC] * nr,
        out_shape=[SDS((B, T, HG_WIDTH), F32), SDS((B, T, HG_WIDTH), BF16),
                   SDS((B, HG_HEADS, ng, HG_HEAD_DIM, HG_STACK), F32)] + _exchange_shapes(ride_srcs, ride_modes),
        scratch_shapes=[pltpu.VMEM((HG_HEADS, HG_HEAD_DIM, HG_HEAD_DIM), F32)] + _exchange_sems(nr),
        compiler_params=_params(("arbitrary", "arbitrary"), VMEM_LIMIT_BIG),
    )(proj_h, proj_h, proj_h, proj_h, lb, hg_w, *ride_srcs)


def _mix_out(x2, attn_n, rec_g, mod8, post_w, w_out_bf, T, ride_srcs, ride_modes):
    N = x2.shape[0]
    TM = _tile_rows(T, big=True)
    tps = T // TM
    nr = len(ride_srcs)

    def body(*refs):
        x_ref, an_ref, rg_ref, mod_ref, pw_ref, w_ref = refs[:6]
        ride_in = refs[6:6 + nr]
        mix_ref, x1_ref, cat_ref = refs[6 + nr:9 + nr]
        ride_out = refs[9 + nr:9 + 2 * nr]
        sems = refs[9 + 2 * nr:]
        _ride_start(ride_modes, pl.program_id(0), N // TM, ride_in, ride_out, sems)
        cat = jnp.concatenate([an_ref[...], rg_ref[...]], axis=1)
        cat_ref[...] = cat
        mix = _dot(cat, w_ref[...])
        mix_ref[...] = mix
        r = lax.rsqrt(_mean_last(mix * mix) + EPS)
        x1_ref[...] = x_ref[...] + mod_ref[2:3, :] * (mix * r * pw_ref[...])
        _ride_wait(ride_modes, pl.program_id(0), N // TM, ride_in, ride_out, sems)

    row = lambda w: pl.BlockSpec((TM, w), lambda i: (i, 0))
    return pl.pallas_call(
        body, name="mix_out", grid=(N // TM,),
        in_specs=[row(D_MODEL), row(ATT_WIDTH), row(HG_WIDTH), _mod_spec(tps),
                  pl.BlockSpec((1, D_MODEL), lambda i: (0, 0)),
                  pl.BlockSpec((D_MODEL, D_MODEL), lambda i: (0, 0))] + [ANY_SPEC] * nr,
        out_specs=[row(D_MODEL), row(D_MODEL), row(D_MODEL)] + [ANY_SPEC] * nr,
        out_shape=[SDS((N, D_MODEL), F32), SDS((N, D_MODEL), F32), SDS((N, D_MODEL), BF16)]
        + _exchange_shapes(ride_srcs, ride_modes),
        scratch_shapes=_exchange_sems(nr),
        compiler_params=_params(("arbitrary",), VMEM_LIMIT_BIG),
    )(x2, attn_n, rec_g, mod8, post_w, w_out_bf, *ride_srcs)


def _load_weights_once(pairs, sem):
    @pl.when(pl.program_id(0) == 0)
    def _():
        cps = [pltpu.make_async_copy(src, dst, sem.at[i]) for i, (src, dst) in enumerate(pairs)]
        for cp in cps:
            cp.start()
        for cp in cps:
            cp.wait()


MLP_HALF = D_MODEL // 2
MLP_PIECES = 2 * N_DEV + 2


def _mlp_weight_pieces(wu_a, wu_b, wd_a, wd_b, wu, wd):
    cols = D_FF // N_DEV
    pairs = []
    for h, half in enumerate((wu_a, wu_b)):
        for j in range(N_DEV):
            pairs.append((half.at[j], wu.at[pl.ds(h * MLP_HALF, MLP_HALF), pl.ds(j * cols, cols)]))
    for h, half in enumerate((wd_a, wd_b)):
        pairs.append((half, wd.at[:, pl.ds(h * MLP_HALF, MLP_HALF)]))
    return pairs


def _mlp_fwd(x1, mod8, pre_w, w_up_halves, w_down_halves, T):
    N = x1.shape[0]
    TM = _tile_rows(T)
    tps = T // TM

    def body(x_ref, mod_ref, pw_ref, wua, wub, wda, wdb, up_ref, u_ref, d_ref, h2_ref, wu, wd, sem):
        _load_weights_once(_mlp_weight_pieces(wua, wub, wda, wdb, wu, wd), sem)
        x = x_ref[...]
        r = lax.rsqrt(_mean_last(x * x) + EPS)
        h = (x * r * pw_ref[...]) * (1.0 + mod_ref[4:5, :]) + mod_ref[3:4, :]
        hb = _bf(h)
        h2_ref[...] = hb
        up = _dot(hb, wu[...])
        up_ref[...] = up
        ru = jnp.maximum(up, 0.0)
        u = _bf(ru * ru)
        u_ref[...] = u
        d_ref[...] = _dot(u, wd[...])

    row = lambda w: pl.BlockSpec((TM, w), lambda i: (i, 0))
    return pl.pallas_call(
        body, name="mlp_fwd", grid=(N // TM,),
        in_specs=[row(D_MODEL), _mod_spec(tps), pl.BlockSpec((1, D_MODEL), lambda i: (0, 0))] + [ANY_SPEC] * 4,
        out_specs=[row(D_FF), row(D_FF), row(D_MODEL), row(D_MODEL)],
        out_shape=[SDS((N, D_FF), F32), SDS((N, D_FF), BF16), SDS((N, D_MODEL), F32), SDS((N, D_MODEL), BF16)],
        scratch_shapes=[pltpu.VMEM((D_MODEL, D_FF), BF16), pltpu.VMEM((D_FF, D_MODEL), BF16),
                        pltpu.SemaphoreType.DMA((MLP_PIECES,))],
        compiler_params=_params(("arbitrary",), VMEM_LIMIT_BIG),
    )(x1, mod8, pre_w, *w_up_halves, *w_down_halves)


def _acc_rows(acc_ref, first, rows):
    @pl.when(first)
    def _():
        acc_ref[...] = jnp.zeros(acc_ref.shape, F32)
    for i, r in enumerate(rows):
        acc_ref[i:i + 1, :] += r


def _mlp_bwd(x1, d, up, tgt, mod8, pre_w, post_w, w_up_halves, w_down_halves, T):
    N = x1.shape[0]
    TM = _tile_rows(T)
    tps = T // TM

    def body(x_ref, d_ref, up_ref, t_ref, mod_ref, pw_ref, qw_ref, wua, wub, wda, wdb,
             dx_ref, dup_ref, dd_ref, acc_ref, wd, wu, sem):
        _load_weights_once(_mlp_weight_pieces(wua, wub, wda, wdb, wu, wd), sem)
        sh2, sc2, g2 = mod_ref[3:4, :], mod_ref[4:5, :], mod_ref[5:6, :]
        x = x_ref[...]
        r1 = lax.rsqrt(_mean_last(x * x) + EPS)
        xh = x * r1
        n2 = xh * pw_ref[...]
        dv = d_ref[...]
        rd = lax.rsqrt(_mean_last(dv * dv) + EPS)
        dh = dv * rd
        rr = dh * qw_ref[...]
        e = x + g2 * rr - t_ref[...]
        loss = 0.5 * jnp.sum(_sum_rows(e * e), axis=1, keepdims=True) / D_MODEL
        dy = e * (1.0 / D_MODEL)
        dg2 = _sum_rows(dy * rr)
        drr = dy * g2
        dw_post = _sum_rows(drr * dh)
        ddh = drr * qw_ref[...]
        dd = _bf(rd * (ddh - dh * _mean_last(ddh * dh)))
        dd_ref[...] = dd
        ru = jnp.maximum(up_ref[...], 0.0)
        dup = _bf(_dot_nt(dd, wd[...]) * (2.0 * ru))
        dup_ref[...] = dup
        dh2 = _dot_nt(dup, wu[...])
        dsh2 = _sum_rows(dh2)
        dsc2 = _sum_rows(dh2 * n2)
        dn2 = dh2 * (1.0 + sc2)
        dw_pre = _sum_rows(dn2 * xh)
        dxh = dn2 * pw_ref[...]
        dx_ref[...] = dy + r1 * (dxh - xh * _mean_last(dxh * xh))
        _acc_rows(acc_ref, pl.program_id(0) % tps == 0,
                  [dsh2, dsc2, dg2, dw_pre, dw_post, jnp.broadcast_to(loss, (1, D_MODEL))])

    row = lambda w: pl.BlockSpec((TM, w), lambda i: (i, 0))
    vec = pl.BlockSpec((1, D_MODEL), lambda i: (0, 0))
    B = N // T
    return pl.pallas_call(
        body, name="mlp_bwd", grid=(N // TM,),
        in_specs=[row(D_MODEL), row(D_MODEL), row(D_FF), row(D_MODEL), _mod_spec(tps), vec, vec] + [ANY_SPEC] * 4,
        out_specs=[row(D_MODEL), row(D_FF), row(D_MODEL), _mod_spec(tps)],
        out_shape=[SDS((N, D_MODEL), F32), SDS((N, D_FF), BF16), SDS((N, D_MODEL), BF16),
                   SDS((B, 8, D_MODEL), F32)],
        scratch_shapes=[pltpu.VMEM((D_FF, D_MODEL), BF16), pltpu.VMEM((D_MODEL, D_FF), BF16),
                        pltpu.SemaphoreType.DMA((MLP_PIECES,))],
        compiler_params=_params(("arbitrary",), VMEM_LIMIT_BIG),
    )(x1, d, up, tgt, mod8, pre_w, post_w, *w_up_halves, *w_down_halves)


def _mix_bwd(mix, dx1, mod8, post_w, w_out_bf, T, ride_srcs, ride_modes):
    N = mix.shape[0]
    TM = _tile_rows(T, big=True)
    tps = T // TM
    nr = len(ride_srcs)

    def body(*refs):
        mix_ref, dx_ref, mod_ref, pw_ref, w_ref = refs[:5]
        ride_in = refs[5:5 + nr]
        dan_ref, drg_ref, dmix_ref, acc_ref = refs[5 + nr:9 + nr]
        ride_out = refs[9 + nr:9 + 2 * nr]
        sems = refs[9 + 2 * nr:]
        _ride_start(ride_modes, pl.program_id(0), N // TM, ride_in, ride_out, sems)
        g1 = mod_ref[2:3, :]
        mix = mix_ref[...]
        dx1 = dx_ref[...]
        rm = lax.rsqrt(_mean_last(mix * mix) + EPS)
        mh = mix * rm
        dg1 = _sum_rows(dx1 * (mh * pw_ref[...]))
        dr = dx1 * g1
        dw_post = _sum_rows(dr * mh)
        dmh = dr * pw_ref[...]
        dmix = _bf(rm * (dmh - mh * _mean_last(dmh * mh)))
        dmix_ref[...] = dmix
        dcat = _dot_nt(dmix, w_ref[...])
        dan_ref[...] = dcat[:, :ATT_WIDTH]
        drg_ref[...] = dcat[:, ATT_WIDTH:]
        _acc_rows(acc_ref, pl.program_id(0) % tps == 0, [dg1, dw_post])
        _ride_wait(ride_modes, pl.program_id(0), N // TM, ride_in, ride_out, sems)

    row = lambda w: pl.BlockSpec((TM, w), lambda i: (i, 0))
    B = N // T
    return pl.pallas_call(
        body, name="mix_bwd", grid=(N // TM,),
        in_specs=[row(D_MODEL), row(D_MODEL), _mod_spec(tps), pl.BlockSpec((1, D_MODEL), lambda i: (0, 0)),
                  pl.BlockSpec((D_MODEL, D_MODEL), lambda i: (0, 0))] + [ANY_SPEC] * nr,
        out_specs=[row(ATT_WIDTH), row(HG_WIDTH), row(D_MODEL), _mod_spec(tps)] + [ANY_SPEC] * nr,
        out_shape=[SDS((N, ATT_WIDTH), F32), SDS((N, HG_WIDTH), F32), SDS((N, D_MODEL), BF16),
                   SDS((B, 8, D_MODEL), F32)] + _exchange_shapes(ride_srcs, ride_modes),
        scratch_shapes=_exchange_sems(nr),
        compiler_params=_params(("arbitrary",), VMEM_LIMIT_BIG),
    )(mix, dx1, mod8, post_w, w_out_bf, *ride_srcs)


def _hgrn_bwd(proj_h, lb, hg_w, o, s_prev, drg, ride_srcs, ride_modes):
    B, T, _ = proj_h.shape
    ng = T // HG_ROWS
    nr = len(ride_srcs)

    def body(*refs):
        hq_ref, hf_ref, hi_ref, hg_ref, lb_ref, gw_ref, o_ref, sp_ref, drg_ref = refs[:9]
        ride_in = refs[9:9 + nr]
        dhq_ref, dhf_ref, dhi_ref, dhg_ref, dlb_ref, dgw_ref = refs[9 + nr:15 + nr]
        ride_out = refs[15 + nr:15 + 2 * nr]
        dst = refs[15 + 2 * nr]
        sems = refs[16 + 2 * nr:]
        step = pl.program_id(0) * ng + pl.program_id(1)
        _ride_start(ride_modes, step, B * ng, ride_in, ride_out, sems)

        @pl.when(pl.program_id(1) == 0)
        def _():
            dst[...] = jnp.zeros(dst.shape, F32)
            dlb_ref[...] = jnp.zeros(dlb_ref.shape, F32)
            dgw_ref[...] = jnp.zeros(dgw_ref.shape, F32)

        lo, up = _group_masks()
        lower_bf, upper_bf = _ones_bf(lo), _ones_bf(up)
        row_chunk = _row_chunk()
        gw = gw_ref[...]

        for h in range(HG_HEADS):
            lanes = _head_lanes(h)
            lbv = lb_ref[:, lanes]
            hq = hq_ref[:, lanes]
            gt = _hgrn_gates(hq, hf_ref[:, lanes], lbv, lower_bf)
            sq, sg, qdf, kdf, k2f, ebl = gt["sq"], gt["sg"], gt["qd"], gt["kd"], gt["k2"], gt["ebl"]
            v, qd, kd = _bf(hi_ref[:, lanes]), _bf(qdf), _bf(kdf)
            ov = o_ref[:, lanes]
            hg = hg_ref[:, lanes]
            shg = _sigmoid(hg)
            dr = drg_ref[:, lanes]
            ro = lax.rsqrt(_mean_last(ov * ov) + EPS)
            oh = ov * ro
            dhg_ref[:, lanes] = _bf(dr * (oh * gw) * (shg + hg * shg * (1.0 - shg)))
            drn = dr * (hg * shg)
            dgw_ref[...] += jnp.broadcast_to(_sum_rows(drn * oh), (8, LANES))
            doh = drn * gw
            do = _bf(ro * (doh - oh * _mean_last(doh * oh)))
            a = jnp.where(lo, _dot_nt(qd, kd), 0.0)
            da = _bf(jnp.where(lo, _dot_nt(do, v), 0.0))
            dv = _dot_tn(_bf(a), do)
            dqd = _dot(da, kd)
            dkd = _dot_tn(da, qd)
            sp = sp_ref[h]
            incr = _dot_tn(do, _bf(_spread(qdf, row_chunk)))
            ds = dst[h]
            after = [None] * HG_GROUP
            for c in reversed(range(HG_GROUP)):
                after[c] = ds
                ds = ds * ebl[c] + _lane_block(incr, c)
            dst[h] = ds
            dss = jnp.concatenate(after, axis=1)
            dssb = _bf(dss)
            dk2 = _pick(_dot(v, dssb), row_chunk)
            dhi_ref[:, lanes] = _bf(dv + _dot_nt(_bf(_spread(k2f, row_chunk)), dssb))
            dqd = dqd + _pick(_dot(do, _bf(sp)), row_chunk)
            debl = _sum_rows(dss * sp)
            k2g = dk2 * k2f
            db = dqd * qdf - dkd * kdf - k2g
            dk = dkd * gt["enb"] + dk2 * gt["e2"]
            dbl = _chunk_bcast([_lane_block(debl, c) * ebl[c] + _sum_rows(k2g[_chunk_rows(c), :])
                                for c in range(HG_GROUP)])
            dg = _tri_sum(upper_bf, db, terms=2) + dbl
            df = dg / gt["f"] - dk
            dhf_ref[:, lanes] = _bf(df * (1.0 - lbv) * sg * (1.0 - sg))
            dlb_ref[:, lanes] += jnp.broadcast_to(_sum_rows(df * (1.0 - sg)), (8, LANES))
            dhq_ref[:, lanes] = _bf((dqd * gt["eb"]) * (sq + hq * sq * (1.0 - sq)))
        _ride_wait(ride_modes, step, B * ng, ride_in, ride_out, sems)

    part = lambda j: pl.BlockSpec((None, HG_ROWS, HG_WIDTH), lambda b, g: (b, ng - 1 - g, j))
    return pl.pallas_call(
        body, name="hgrn_bwd", grid=(B, ng),
        in_specs=[part(0), part(1), part(2), part(3),
                  pl.BlockSpec((1, HG_WIDTH), lambda b, g: (0, 0)),
                  pl.BlockSpec((1, LANES), lambda b, g: (0, 0)),
                  part(0),
                  pl.BlockSpec((None, HG_HEADS, None, HG_HEAD_DIM, HG_STACK), lambda b, g: (b, 0, ng - 1 - g, 0, 0)),
                  part(0)] + [ANY_SPEC] * nr,
        out_specs=[part(0), part(0), part(0), part(0),
                   pl.BlockSpec((None, 8, HG_WIDTH), lambda b, g: (b, 0, 0)),
                   pl.BlockSpec((None, 8, LANES), lambda b, g: (b, 0, 0))] + [ANY_SPEC] * nr,
        out_shape=[SDS((B, T, HG_WIDTH), BF16)] * 4 + [SDS((B, 8, HG_WIDTH), F32), SDS((B, 8, LANES), F32)]
        + _exchange_shapes(ride_srcs, ride_modes),
        scratch_shapes=[pltpu.VMEM((HG_HEADS, HG_HEAD_DIM, HG_HEAD_DIM), F32)] + _exchange_sems(nr),
        compiler_params=_params(("arbitrary", "arbitrary"), VMEM_LIMIT_BIG),
    )(proj_h, proj_h, proj_h, proj_h, lb, hg_w, o, s_prev, drg, *ride_srcs)


def _attn_bwd(qr, kr, proj3, attn_o, dan, tables, sinks, attn_w, ride_srcs, ride_modes):
    B, T, _ = proj3.shape
    nb = T // WINDOW
    splits = min(ATT_SPLITS, nb)
    per = nb // splits
    nr = len(ride_srcs)
    cos, sinl, sinr = tables
    QKV = ATT_WIDTH + 2 * LANES

    def body(*refs):
        qr_ref, kr_ref, v_ref, o_ref, dan_ref, cos_ref, sl_ref, sr_ref, sink_ref, aw_ref = refs[:10]
        ride_in = refs[10:10 + nr]
        dqkv_ref, dsink_ref, daw_ref = refs[10 + nr:13 + nr]
        ride_out = refs[13 + nr:13 + 2 * nr]
        kpad, vpad, dkpad, dvpad, dqb, dsk = refs[13 + 2 * nr:19 + 2 * nr]
        sems = refs[19 + 2 * nr:]
        part = pl.program_id(1)
        step = pl.program_id(0) * splits + part
        _ride_start(ride_modes, step, B * splits, ride_in, ride_out, sems)

        @pl.when(part == 0)
        def _():
            kpad[0:WINDOW, :] = jnp.zeros((WINDOW, LANES), BF16)
            vpad[0:WINDOW, :] = jnp.zeros((WINDOW, LANES), BF16)
            kpad[WINDOW:, :] = kr_ref[...]
            vpad[WINDOW:, :] = _bf(v_ref[...])
            dkpad[...] = jnp.zeros(dkpad.shape, F32)
            dvpad[...] = jnp.zeros(dvpad.shape, F32)
            dsk[...] = jnp.zeros(dsk.shape, F32)
            daw_ref[...] = jnp.zeros(daw_ref.shape, F32)

        window, current = _band_masks()
        aw = aw_ref[...]

        def block(n, daw):
            r0 = pl.multiple_of(n * WINDOW, WINDOW)
            rows = pl.ds(r0, WINDOW)
            band = pl.ds(r0, 2 * WINDOW)
            ob = o_ref[rows, :]
            dn = dan_ref[rows, :]
            ro = lax.rsqrt(_mean_last(ob * ob) + EPS)
            oh = ob * ro
            daw = daw + _sum_rows(dn * oh)
            doh = dn * aw
            do = _bf(ro * (doh - oh * _mean_last(doh * oh)))
            doparts = [do[:, j * LANES:(j + 1) * LANES] for j in range(ATT_WIDTH // LANES)]
            qparts = [qr_ref[rows, j * LANES:(j + 1) * LANES] for j in range(ATT_WIDTH // LANES)]
            mask = window & (current | (n > 0))
            for hk in range(ATT_KV_HEADS):
                lanes = slice(hk * ATT_HEAD_DIM, (hk + 1) * ATT_HEAD_DIM)
                qs = _stack_heads(qparts, hk)
                dos = _stack_heads(doparts, hk)
                kk, vv = kpad[band, lanes], vpad[band, lanes]
                p, inv, es = _softmax_band(qs, kk, mask, _sink_row(sink_ref, hk))
                p = p * inv
                dp = _dot_nt(vv, dos)
                delta = jnp.sum(p * dp, axis=0, keepdims=True)
                ds = _bf(p * (dp - delta))
                sk = (es * inv) * delta
                dqt = _dot_tn(kk, ds) * ATT_SCALE
                dkpad[band, lanes] += _dot(ds, qs)
                dvpad[band, lanes] += _dot(_bf(p), dos)
                for g in range(ATT_GROUP):
                    h = ATT_GROUP * hk + g
                    cols = slice(g * WINDOW, (g + 1) * WINDOW)
                    dqb[:, h * ATT_HEAD_DIM:(h + 1) * ATT_HEAD_DIM] = dqt[:, cols].T
                    dsk[h:h + 1, :] += jnp.broadcast_to(-jnp.sum(sk[:, cols], axis=1, keepdims=True), (1, LANES))
            cs, sl, sr = cos_ref[rows, :], sl_ref[rows, :], sr_ref[rows, :]
            for j in range(ATT_WIDTH // LANES):
                dqkv_ref[rows, j * LANES:(j + 1) * LANES] = _bf(_rope_t(dqb[:, j * LANES:(j + 1) * LANES], cs, sl, sr))
            return daw

        daw = _loop_pairs(part * per, per, block, jnp.zeros((1, ATT_WIDTH), F32))
        daw_ref[...] += jnp.broadcast_to(daw, (8, ATT_WIDTH))
        dsink_ref[...] = dsk[...]

        def finish(n, carry):
            r0 = pl.multiple_of(n * WINDOW, WINDOW)
            rows = pl.ds(r0, WINDOW)
            nxt = pl.ds(r0 + WINDOW, WINDOW)
            cs, sl, sr = cos_ref[rows, :], sl_ref[rows, :], sr_ref[rows, :]
            dqkv_ref[rows, ATT_WIDTH:ATT_WIDTH + LANES] = _bf(_rope_t(dkpad[nxt, :], cs, sl, sr))
            dqkv_ref[rows, ATT_WIDTH + LANES:QKV] = _bf(dvpad[nxt, :])
            return carry

        @pl.when(part == splits - 1)
        def _():
            lax.fori_loop(0, nb, finish, 0)

        _ride_wait(ride_modes, step, B * splits, ride_in, ride_out, sems)

    seq = lambda w, j: pl.BlockSpec((None, T, w), lambda b, s: (b, 0, j))
    full = lambda r, w: pl.BlockSpec((r, w), lambda b, s: (0, 0))
    return pl.pallas_call(
        body, name="attn_bwd", grid=(B, splits),
        in_specs=[seq(ATT_WIDTH, 0), seq(LANES, 0), seq(LANES, 5), seq(ATT_WIDTH, 0), seq(ATT_WIDTH, 0),
                  full(T, LANES), full(T, LANES), full(T, LANES),
                  pl.BlockSpec(memory_space=pltpu.SMEM), full(1, ATT_WIDTH)] + [ANY_SPEC] * nr,
        out_specs=[seq(QKV, 0), pl.BlockSpec((None, 8, LANES), lambda b, s: (b, 0, 0)),
                   pl.BlockSpec((None, 8, ATT_WIDTH), lambda b, s: (b, 0, 0))] + [ANY_SPEC] * nr,
        out_shape=[SDS((B, T, QKV), BF16), SDS((B, 8, LANES), F32), SDS((B, 8, ATT_WIDTH), F32)]
        + _exchange_shapes(ride_srcs, ride_modes),
        scratch_shapes=[pltpu.VMEM((T + WINDOW, LANES), BF16), pltpu.VMEM((T + WINDOW, LANES), BF16),
                        pltpu.VMEM((T + WINDOW, LANES), F32), pltpu.VMEM((T + WINDOW, LANES), F32),
                        pltpu.VMEM((WINDOW, ATT_WIDTH), F32), pltpu.VMEM((8, LANES), F32)] + _exchange_sems(nr),
        compiler_params=_params(("arbitrary", "arbitrary"), VMEM_LIMIT_BIG),
    )(qr, kr, proj3, attn_o, dan, cos, sinl, sinr, sinks, attn_w, *ride_srcs)


def _in_bwd(x2, dx1, dqkv, dhq, dhf, dhi, dhg, mod8, pre_w, w_in_bf, T, ride_srcs, ride_modes):
    N = x2.shape[0]
    TM = _tile_rows(T, big=True)
    tps = T // TM
    nr = len(ride_srcs)
    pieces = [(0, ATT_WIDTH + 2 * LANES), (768, HG_WIDTH), (1280, HG_WIDTH), (1792, HG_WIDTH), (2304, HG_WIDTH)]

    def body(*refs):
        x_ref, dx_ref, p0, p1, p2, p3, p4, mod_ref, pw_ref, w_ref = refs[:10]
        ride_in = refs[10:10 + nr]
        gx_ref, dproj_ref, acc_ref = refs[10 + nr:13 + nr]
        ride_out = refs[13 + nr:13 + 2 * nr]
        sems = refs[13 + 2 * nr:]
        _ride_start(ride_modes, pl.program_id(0), N // TM, ride_in, ride_out, sems)
        sc1 = mod_ref[1:2, :]
        dh = jnp.zeros((TM, D_MODEL), F32)
        for ref, (off, width) in zip((p0, p1, p2, p3, p4), pieces):
            pb = ref[...]
            dproj_ref[:, off:off + width] = pb
            dh = dh + _dot(pb, w_ref[off:off + width, :])
        x = x_ref[...]
        r = lax.rsqrt(_mean_last(x * x) + EPS)
        xh = x * r
        n1 = xh * pw_ref[...]
        dsh1 = _sum_rows(dh)
        dsc1 = _sum_rows(dh * n1)
        dn1 = dh * (1.0 + sc1)
        dw_pre = _sum_rows(dn1 * xh)
        dxh = dn1 * pw_ref[...]
        gx_ref[...] = dx_ref[...] + r * (dxh - xh * _mean_last(dxh * xh))
        _acc_rows(acc_ref, pl.program_id(0) % tps == 0, [dsh1, dsc1, dw_pre])
        _ride_wait(ride_modes, pl.program_id(0), N // TM, ride_in, ride_out, sems)

    row = lambda w: pl.BlockSpec((TM, w), lambda i: (i, 0))
    B = N // T
    return pl.pallas_call(
        body, name="in_bwd", grid=(N // TM,),
        in_specs=[row(D_MODEL), row(D_MODEL), row(768), row(HG_WIDTH), row(HG_WIDTH), row(HG_WIDTH),
                  row(HG_WIDTH), _mod_spec(tps), pl.BlockSpec((1, D_MODEL), lambda i: (0, 0)),
                  pl.BlockSpec((IN_COLS, D_MODEL), lambda i: (0, 0))] + [ANY_SPEC] * nr,
        out_specs=[row(D_MODEL), row(IN_COLS), _mod_spec(tps)] + [ANY_SPEC] * nr,
        out_shape=[SDS((N, D_MODEL), F32), SDS((N, IN_COLS), BF16), SDS((B, 8, D_MODEL), F32)]
        + _exchange_shapes(ride_srcs, ride_modes),
        scratch_shapes=_exchange_sems(nr),
        compiler_params=_params(("arbitrary",), VMEM_LIMIT_BIG),
    )(x2, dx1, dqkv, dhq, dhf, dhi, dhg, mod8, pre_w, w_in_bf, *ride_srcs)


def _matmul_tn(name, a, b, tn, tm=512, by_owner_cols=False):
    K, M = a.shape
    Nc = b.shape[1]
    tm = min(tm, M)

    def body(a_ref, b_ref, o_ref):
        o_ref[...] = _bf(_dot_tn(a_ref[...], b_ref[...]))

    if by_owner_cols:
        assert tn * N_DEV == Nc
        out_shape = SDS((N_DEV, M, tn), BF16)
        out_spec = pl.BlockSpec((None, tm, tn), lambda i, j: (j, i, 0))
    else:
        out_shape = SDS((M, Nc), BF16)
        out_spec = pl.BlockSpec((tm, tn), lambda i, j: (i, j))
    return pl.pallas_call(
        body, name=name, grid=(M // tm, Nc // tn),
        in_specs=[pl.BlockSpec((K, tm), lambda i, j: (0, i)),
                  pl.BlockSpec((K, tn), lambda i, j: (0, j))],
        out_specs=out_spec, out_shape=out_shape,
        compiler_params=_params(("arbitrary", "arbitrary"), VMEM_LIMIT_BIG),
    )(a, b)


def _adamw_math(w, g, m, v):
    m2 = ADAM_B1 * m + (1.0 - ADAM_B1) * g
    v2 = ADAM_B2 * v + (1.0 - ADAM_B2) * (g * g)
    m_hat = m2 / (1.0 - ADAM_B1 ** ADAM_STEP)
    v_hat = v2 / (1.0 - ADAM_B2 ** ADAM_STEP)
    delta = -ADAM_LR * (m_hat / (jnp.sqrt(v_hat) + ADAM_EPS) + ADAM_WD * w)
    return delta, m2, v2


def _pair_add(name, gw, theirs):
    chips, _, r, c = gw.shape
    tr = r
    core = lax.axis_index("c").astype(jnp.int32).reshape(1)

    def body(core_ref, mine_ref, theirs_ref, o_ref):
        o_ref[...] = _bf(mine_ref[...].astype(F32) + theirs_ref[...].astype(F32))

    block = pl.BlockSpec((None, tr, c), lambda s, i, core_ref: (s, i, 0))
    grid_spec = pltpu.PrefetchScalarGridSpec(
        num_scalar_prefetch=1, grid=(chips, r // tr),
        in_specs=[pl.BlockSpec((None, None, tr, c), lambda s, i, core_ref: (s, core_ref[0], i, 0)), block],
        out_specs=block)
    return pl.pallas_call(
        body, name=name, grid_spec=grid_spec, out_shape=SDS((chips, r, c), BF16),
        compiler_params=_params(("arbitrary", "arbitrary")),
    )(core, gw, theirs)


def _reduce_adamw(name, parts, w, m, v):
    r, c = w.shape
    tr = r if r % 256 else 256
    slots = parts.shape[0]

    def body(p_ref, w_ref, m_ref, v_ref, g_ref, d_ref, m2_ref, v2_ref):
        g = p_ref[0].astype(F32)
        for s in range(1, slots):
            g = g + p_ref[s].astype(F32)
        g_ref[...] = g
        d_ref[...], m2_ref[...], v2_ref[...] = _adamw_math(w_ref[...], g, m_ref[...], v_ref[...])

    blk = pl.BlockSpec((tr, c), lambda i: (i, 0))
    return pl.pallas_call(
        body, name=name, grid=(r // tr,),
        in_specs=[pl.BlockSpec((slots, tr, c), lambda i: (0, i, 0)), blk, blk, blk],
        out_specs=[blk] * 4, out_shape=[SDS((r, c), F32)] * 4,
        compiler_params=_params(("arbitrary",), VMEM_LIMIT_BIG),
    )(parts, w, m, v)


def _ada_grad_adamw(c_all, dmod_all, w, m, v):
    r, c = w.shape
    tr = 256
    nb = c_all.shape[0]

    def body(c_ref, dm_ref, w_ref, m_ref, v_ref, g_ref, d_ref, m2_ref, v2_ref):
        cv = c_ref[...]
        g = _dot_tn(cv * _sigmoid(cv), dm_ref[...])
        g_ref[...] = g
        d_ref[...], m2_ref[...], v2_ref[...] = _adamw_math(w_ref[...], g, m_ref[...], v_ref[...])

    blk = pl.BlockSpec((tr, c), lambda i: (i, 0))
    return pl.pallas_call(
        body, name="ada_grad_adamw", grid=(r // tr,),
        in_specs=[pl.BlockSpec((nb, tr), lambda i: (0, i)), pl.BlockSpec((nb, c), lambda i: (0, 0)),
                  blk, blk, blk],
        out_specs=[blk] * 4, out_shape=[SDS((r, c), F32)] * 4,
        compiler_params=_params(("arbitrary",)),
    )(c_all, dmod_all, w, m, v)


_SMALL = [("b_ada", 6144), ("pre_w_mix", 1024), ("attn_sinks", 128), ("attn_out_w", 512), ("lb_table", 1024),
          ("hg_norm_w", 128), ("post_w_mix", 1024), ("pre_w_mlp", 1024), ("post_w_mlp", 1024)]


def _pack_small(vals, loss_part):
    out = []
    for name, width in _SMALL:
        f = vals[name].reshape(-1).astype(F32)
        out.append(jnp.pad(f, (0, width - f.shape[0])))
    out.append(jnp.broadcast_to(loss_part, (LANES,)))
    return jnp.concatenate(out).reshape(1, -1)


def _adamw_small(parts, given):
    names = [n for n, _ in _SMALL]
    flat_in = [a for n in names for a in given[n]]

    def body(*refs):
        p_ref = refs[0]
        in_refs = refs[1:1 + 3 * len(names)]
        out_refs = refs[1 + 3 * len(names):-1]
        loss_ref = refs[-1]
        g = p_ref[0]
        for s in range(1, N_DEV):
            g = g + p_ref[s]
        off = 0
        for i, (name, width) in enumerate(_SMALL):
            w_ref, m_ref, v_ref = in_refs[3 * i:3 * i + 3]
            rows, cols = w_ref.shape
            for r in range(rows):
                gr = g[:, off + r * cols:off + (r + 1) * cols]
                res = (gr,) + _adamw_math(w_ref[r:r + 1, :], gr, m_ref[r:r + 1, :], v_ref[r:r + 1, :])
                for o_ref, val in zip(out_refs[4 * i:4 * i + 4], res):
                    o_ref[r:r + 1, :] = val
            off += width
        loss_ref[...] = g[:, off:off + LANES]

    out_shape = [SDS(given[n][0].shape, F32) for n in names for _ in range(4)] + [SDS((1, LANES), F32)]
    outs = pl.pallas_call(body, name="adamw_small", out_shape=out_shape)(parts, *flat_in)
    return {n: tuple(outs[4 * i:4 * i + 4]) for i, n in enumerate(names)}, outs[-1][0, 0]


def kernel(x, c, w_ada, b_ada, pre_w_mix, w_in, attn_sinks, attn_out_w, lb_table, hg_norm_w, w_out, post_w_mix, pre_w_mlp, w_up, w_down, post_w_mlp, loss_target, m_w_ada, m_b_ada, m_pre_w_mix, m_w_in, m_attn_sinks, m_attn_out_w, m_lb_table, m_hg_norm_w, m_w_out, m_post_w_mix, m_pre_w_mlp, m_w_up, m_w_down, m_post_w_mlp, v_w_ada, v_b_ada, v_pre_w_mix, v_w_in, v_attn_sinks, v_attn_out_w, v_lb_table, v_hg_norm_w, v_w_out, v_post_w_mix, v_pre_w_mlp, v_w_up, v_w_down, v_post_w_mlp):
    B, T, _ = x.shape
    N = B * T
    me = 4 * lax.axis_index("x") + 2 * lax.axis_index("y") + lax.axis_index("c")
    x2 = x.reshape(N, D_MODEL)
    tgt2 = loss_target.reshape(N, D_MODEL)

    w_in_t, m_w_in_t, v_w_in_t = w_in[0].T, m_w_in[0].T, v_w_in[0].T
    w_in_g, c_g = _exchange("gather_w_in", [_bf(w_in_t), c], ["gather"] * 2)
    w_in_f = w_in_g.reshape(IN_COLS, D_MODEL)
    c_all = c_g.reshape(N_DEV * B, D_MODEL)

    ada_cols = w_ada.shape[2]
    b_mine = lax.dynamic_slice(b_ada, (0, me * ada_cols), (1, ada_cols))
    mod_cols = _ada_mod(c_all, w_ada[0], b_mine)
    (mod_g,) = _exchange("scatter_mod", [mod_cols.reshape(N_DEV, B, ada_cols)], ["a2a"])
    mod = mod_g.transpose(1, 0, 2).reshape(B, 6, D_MODEL)
    mod8 = jnp.pad(mod, ((0, 0), (0, 2), (0, 0)))

    lb_p = jax.nn.softmax(lb_table, axis=0)
    lb = lb_p[1:2]
    tables = _rope_tables(T)

    w_up_b, w_down_b = _bf(w_up[0]), _bf(w_down[0])
    proj_a, proj_h, h1, w_out_g = _in_proj(x2, mod8, pre_w_mix, w_in_f, T, [_bf(w_out[0])], ["gather"])
    proj3 = proj_a.reshape(B, T, ATT_COLS)
    proj_h = proj_h.reshape(B, T, IN_COLS - ATT_COLS)
    rec_o, rec_g, s_prev, w_up_g0, w_up_g1 = _hgrn_fwd(proj_h, lb, hg_norm_w,
                                                       [w_up_b[:MLP_HALF], w_up_b[MLP_HALF:]], ["gather"] * 2)
    attn_o, attn_n, qr, kr, w_down_g0 = _attn_fwd(proj3, tables, attn_sinks, attn_out_w,
                                                  [w_down_b[:, :MLP_HALF]], ["gather"])
    w_out_f = w_out_g.reshape(D_MODEL, D_MODEL)
    mix, x1, cat, w_down_g1 = _mix_out(x2, attn_n.reshape(N, ATT_WIDTH), rec_g.reshape(N, HG_WIDTH), mod8,
                                       post_w_mix, w_out_f, T, [w_down_b[:, MLP_HALF:]], ["gather"])
    w_up_halves = [w_up_g0, w_up_g1]
    w_down_halves = [w_down_g0.reshape(D_FF, MLP_HALF), w_down_g1.reshape(D_FF, MLP_HALF)]
    up, u, d, h2 = _mlp_fwd(x1, mod8, pre_w_mlp, w_up_halves, w_down_halves, T)

    dx1, dup, dd, acc_mlp = _mlp_bwd(x1, d, up, tgt2, mod8, pre_w_mlp, post_w_mlp, w_up_halves, w_down_halves, T)
    chips = N_DEV // 2
    by_chip = lambda a: a.reshape((chips, 2, a.shape[0] // N_DEV) + a.shape[1:])
    gw_up = _matmul_tn("grad_w_up", h2, dup, D_FF // N_DEV, by_owner_cols=True)
    gw_up = gw_up.reshape(chips, 2, D_MODEL, D_FF // N_DEV)
    gw_down = by_chip(_matmul_tn("grad_w_down", u, dd, 512))
    dan, drg, dmix, acc_mix, q_down, q_up = _mix_bwd(mix, dx1, mod8, post_w_mix, w_out_f, T,
                                                     [gw_down, gw_up], ["pair"] * 2)
    p_down, p_up = _pair_add("pair_add_w_down", gw_down, q_down), _pair_add("pair_add_w_up", gw_up, q_up)
    gw_out = _matmul_tn("grad_w_out", cat, dmix, 512).reshape(N_DEV, D_MODEL // N_DEV, D_MODEL)
    dhq, dhf, dhi, dhg, dlb_p, dgw_p, r_down, r_up = _hgrn_bwd(
        proj_h, lb, hg_norm_w, rec_o, s_prev, drg.reshape(B, T, HG_WIDTH), [p_down, p_up], ["chips"] * 2)
    dqkv, dsink_p, daw_p, r_out = _attn_bwd(qr, kr, proj3, attn_o, dan.reshape(B, T, ATT_WIDTH), tables,
                                            attn_sinks, attn_out_w, [gw_out], ["a2a"])
    flat = lambda a: a.reshape(N, a.shape[-1])
    grad_x, dproj, acc_in = _in_bwd(x2, dx1, flat(dqkv), flat(dhq), flat(dhf), flat(dhi), flat(dhg),
                                    mod8, pre_w_mix, w_in_f, T, [], [])

    gw_in = by_chip(_matmul_tn("grad_w_in", dproj, h1, 512, tm=IN_COLS // 2))
    (q_in,) = _exchange("pair_w_in", [gw_in], ["pair"])
    p_in = _pair_add("pair_add_w_in", gw_in, q_in)

    dmod = jnp.concatenate([acc_in[:, 0:2], acc_mix[:, 0:1], acc_mlp[:, 0:3]], axis=1)
    dlb = dlb_p[:, 0].sum(0)
    dlb_table = jnp.stack([-dlb, dlb]) * (lb_p[0] * lb_p[1])[None, :]
    small = {
        "b_ada": dmod.sum(0),
        "pre_w_mix": acc_in[:, 2].sum(0),
        "attn_sinks": dsink_p[:, :, 0].sum(0),
        "attn_out_w": daw_p[:, 0].sum(0),
        "lb_table": dlb_table,
        "hg_norm_w": dgw_p[:, 0].sum(0),
        "post_w_mix": acc_mix[:, 1].sum(0),
        "pre_w_mlp": acc_mlp[:, 3].sum(0),
        "post_w_mlp": acc_mlp[:, 4].sum(0),
    }
    loss_part = acc_mlp[:, 5, 0].sum()
    dmod_blocks = dmod.reshape(B, N_DEV, ada_cols).transpose(1, 0, 2)

    r_in, r_dmod, r_small = _exchange(
        "reduce_grads", [p_in, dmod_blocks, _pack_small(small, loss_part)], ["chips", "a2a", "gather"])

    res = {}
    res["w_in"] = tuple(a.T for a in _reduce_adamw("adamw_w_in", r_in, w_in_t, m_w_in_t, v_w_in_t))
    res["w_out"] = _reduce_adamw("adamw_w_out", r_out, w_out[0], m_w_out[0], v_w_out[0])
    res["w_up"] = _reduce_adamw("adamw_w_up", r_up, w_up[0], m_w_up[0], v_w_up[0])
    res["w_down"] = _reduce_adamw("adamw_w_down", r_down, w_down[0], m_w_down[0], v_w_down[0])
    res["w_ada"] = _ada_grad_adamw(c_all, r_dmod.reshape(N_DEV * B, ada_cols), w_ada[0], m_w_ada[0], v_w_ada[0])

    given = dict(b_ada=(b_ada, m_b_ada, v_b_ada), pre_w_mix=(pre_w_mix, m_pre_w_mix, v_pre_w_mix),
                 attn_sinks=(attn_sinks, m_attn_sinks, v_attn_sinks),
                 attn_out_w=(attn_out_w, m_attn_out_w, v_attn_out_w), lb_table=(lb_table, m_lb_table, v_lb_table),
                 hg_norm_w=(hg_norm_w, m_hg_norm_w, v_hg_norm_w), post_w_mix=(post_w_mix, m_post_w_mix, v_post_w_mix),
                 pre_w_mlp=(pre_w_mlp, m_pre_w_mlp, v_pre_w_mlp), post_w_mlp=(post_w_mlp, m_post_w_mlp, v_post_w_mlp))
    small_res, loss = _adamw_small(r_small, given)
    res.update(small_res)

    order = ["w_ada", "b_ada", "pre_w_mix", "w_in", "attn_sinks", "attn_out_w", "lb_table", "hg_norm_w", "w_out",
             "post_w_mix", "pre_w_mlp", "w_up", "w_down", "post_w_mlp"]
    big = {"w_ada", "w_in", "w_out", "w_up", "w_down"}
    outs = [loss, grad_x.reshape(B, T, D_MODEL)]
    for i in range(4):
        for k in order:
            a = res[k][i]
            outs.append(a[None] if k in big else a)
    return tuple(outs)
```

```python
import jax
import jax.numpy as jnp
from jax import lax
from jax.experimental import pallas as pl
from jax.experimental.pallas import tpu as pltpu

F32 = jnp.float32
BF16 = jnp.bfloat16
SDS = jax.ShapeDtypeStruct

D_MODEL = 1024
ATT_WIDTH = 512
ATT_HEAD_DIM = 64
ATT_KV_HEADS = 2
ATT_GROUP = 4
WINDOW = 128
ROPE_DIM = 16
ROPE_THETA = 500000.0
HG_WIDTH = 512
HG_HEAD_DIM = 128
HG_HEADS = 4
HG_CHUNK = 32
IN_COLS = 2816
ATT_COLS = 768
D_FF = 4096
EPS = 1e-6
N_DEV = 8

ADAM_LR = 0.001
ADAM_B1 = 0.9
ADAM_B2 = 0.999
ADAM_EPS = 1e-08
ADAM_WD = 0.01
ADAM_STEP = 10

VMEM_LIMIT_BIG = 56 << 20
LANES = 128

MESH = pl.DeviceIdType.MESH
NT_DIMS = (((1,), (1,)), ((), ()))
TN_DIMS = (((0,), (0,)), ((), ()))


def _dot(a, b):
    return jnp.dot(a, b, preferred_element_type=F32)


def _dot_nt(a, b):
    return lax.dot_general(a, b, NT_DIMS, preferred_element_type=F32)


def _dot_tn(a, b):
    return lax.dot_general(a, b, TN_DIMS, preferred_element_type=F32)


def _bf(a):
    return a.astype(BF16)


def _sigmoid(a):
    return 1.0 / (1.0 + jnp.exp(-a))


def _mean_last(a):
    return jnp.mean(a, axis=-1, keepdims=True)


def _sum_rows(a):
    return jnp.sum(a, axis=0, keepdims=True)


def _tri_sum(tri_bf, a, terms=3):
    a1 = _bf(a)
    r1 = a - a1.astype(F32)
    a2 = _bf(r1)
    out = _dot(tri_bf, a1) + _dot(tri_bf, a2)
    if terms == 3:
        out = out + _dot(tri_bf, _bf(r1 - a2.astype(F32)))
    return out


def _loop_pairs(first, count, body, init, per_trip=2):
    if count % per_trip:
        return lax.fori_loop(first, first + count, body, init)

    def trip(i, c):
        for k in range(per_trip):
            c = body(first + per_trip * i + k, c)
        return c

    return lax.fori_loop(0, count // per_trip, trip, init)


def _params(sem=None, vmem=None):
    kw = {}
    if sem is not None:
        kw["dimension_semantics"] = sem
    if vmem is not None:
        kw["vmem_limit_bytes"] = vmem
    return pltpu.CompilerParams(**kw)


ANY_SPEC = pl.BlockSpec(memory_space=pl.ANY)


def _exchange_shapes(srcs, modes):
    out_shape = []
    for s, m in zip(srcs, modes):
        shp = {"gather": (N_DEV,) + tuple(s.shape), "pair": (s.shape[0],) + tuple(s.shape[2:])}.get(m, tuple(s.shape))
        out_shape.append(SDS(shp, s.dtype))
    return out_shape


def _exchange_sems(n):
    if n == 0:
        return []
    return [pltpu.SemaphoreType.DMA((n, N_DEV - 1)), pltpu.SemaphoreType.DMA((n, N_DEV - 1)),
            pltpu.SemaphoreType.DMA((n,))]


SIBLING = 1
OTHER_CHIPS = (2, 4, 6)


def _related(k):
    x, y, c = lax.axis_index("x"), lax.axis_index("y"), lax.axis_index("c")
    px, py, pc = x ^ ((k >> 2) & 1), y ^ ((k >> 1) & 1), c ^ (k & 1)
    return (px, py, pc), 4 * px + 2 * py + pc


def _exchange_phases(modes, src_refs, out_refs, send_sems, recv_sems, own_sems):
    _, me = _related(0)
    sib_dev, sib = _related(SIBLING)
    start, middle, end = [], [], []

    def remote(a, i, src, dst, dev):
        return pltpu.make_async_remote_copy(src_ref=src, dst_ref=dst, send_sem=send_sems.at[a, i],
                                            recv_sem=recv_sems.at[a, i], device_id=dev, device_id_type=MESH)

    for a, mode in enumerate(modes):
        out = out_refs[a]
        if mode == "gather":
            src = src_refs[a]
            own = pltpu.make_async_copy(src, out.at[me], own_sems.at[a])
            to_sib = remote(a, 0, src, out.at[me], sib_dev)
            start += [own.start, to_sib.start]
            end += [remote(a, 0, src, out.at[sib], sib_dev).wait_recv, to_sib.wait_send, own.wait]
            for j, k in enumerate(OTHER_CHIPS, start=1):
                dev, peer = _related(k)
                _, peer_sib = _related(k ^ SIBLING)
                send = remote(a, j, src, out.at[me], dev)
                passed = remote(a, 3 + j, out.at[peer], out.at[peer], sib_dev)
                start.append(send.start)
                middle += [remote(a, j, src, out.at[peer], dev).wait_recv, passed.start]
                end += [remote(a, 3 + j, out.at[peer_sib], out.at[peer_sib], sib_dev).wait_recv,
                        send.wait_send, passed.wait_send]
        elif mode == "pair":
            core = lax.axis_index("c")
            for s in range(N_DEV // 2):
                send = remote(a, s, src_refs[a].at[s, 1 - core], out.at[s], sib_dev)
                start.append(send.start)
                end += [remote(a, s, src_refs[a].at[s, 1 - core], out.at[s], sib_dev).wait_recv, send.wait_send]
        elif mode == "chips":
            chip = me // 2
            own = pltpu.make_async_copy(src_refs[a].at[chip], out.at[chip], own_sems.at[a])
            start.append(own.start)
            end.append(own.wait)
            for j, k in enumerate(OTHER_CHIPS, start=1):
                dev, peer = _related(k)
                send = remote(a, j, src_refs[a].at[peer // 2], out.at[chip], dev)
                start.append(send.start)
                end += [remote(a, j, src_refs[a].at[peer // 2], out.at[peer // 2], dev).wait_recv, send.wait_send]
        else:
            own = pltpu.make_async_copy(src_refs[a].at[me], out.at[me], own_sems.at[a])
            start.append(own.start)
            end.append(own.wait)
            for k in range(1, N_DEV):
                dev, peer = _related(k)
                send = remote(a, k - 1, src_refs[a].at[peer], out.at[me], dev)
                start.append(send.start)
                end += [remote(a, k - 1, src_refs[a].at[peer], out.at[peer], dev).wait_recv, send.wait_send]
    return start, middle, end


def _run(actions):
    for act in actions:
        act()


def _exchange(name, srcs, modes):
    n = len(srcs)

    def body(*refs):
        start, middle, end = _exchange_phases(modes, refs[:n], refs[n:2 * n], *refs[2 * n:])
        _run(start)
        _run(middle)
        _run(end)

    return pl.pallas_call(
        body, name=name, out_shape=_exchange_shapes(srcs, modes),
        in_specs=[ANY_SPEC] * n, out_specs=[ANY_SPEC] * n,
        scratch_shapes=_exchange_sems(n),
    )(*srcs)


def _ride_start(modes, step, steps, src_refs, out_refs, sems):
    if not modes:
        return
    middle_step = steps - 1

    @pl.when(step == 0)
    def _():
        _run(_exchange_phases(modes, src_refs, out_refs, *sems)[0])

    if "gather" in modes:
        @pl.when(step == middle_step)
        def _():
            _run(_exchange_phases(modes, src_refs, out_refs, *sems)[1])


def _ride_wait(modes, step, steps, src_refs, out_refs, sems):
    if not modes:
        return

    @pl.when(step == steps - 1)
    def _():
        _run(_exchange_phases(modes, src_refs, out_refs, *sems)[2])


def _ada_mod(c_all, w_ada, b_ada_mine):
    nb, cols = c_all.shape[0], w_ada.shape[1]

    def body(c_ref, w_ref, b_ref, o_ref):
        cv = c_ref[...]
        ca = cv * _sigmoid(cv)
        o_ref[...] = _dot(ca, w_ref[...]) + b_ref[...]

    return pl.pallas_call(body, name="ada_mod", out_shape=SDS((nb, cols), F32))(c_all, w_ada, b_ada_mine)


def _tile_rows(T, big=False):
    return min(512 if big else 256, T)


def _mod_spec(tps):
    return pl.BlockSpec((None, 8, D_MODEL), lambda i: (i // tps, 0, 0))


def _in_proj(x2, mod8, pre_w, w_in_bf, T, ride_srcs, ride_modes):
    N = x2.shape[0]
    TM = _tile_rows(T, big=True)
    tps = T // TM
    nr = len(ride_srcs)

    def body(*refs):
        x_ref, mod_ref, pw_ref, w_ref = refs[:4]
        ride_in = refs[4:4 + nr]
        pa_ref, ph_ref, h1_ref = refs[4 + nr:7 + nr]
        ride_out = refs[7 + nr:7 + 2 * nr]
        sems = refs[7 + 2 * nr:]
        _ride_start(ride_modes, pl.program_id(0), N // TM, ride_in, ride_out, sems)
        x = x_ref[...]
        r = lax.rsqrt(_mean_last(x * x) + EPS)
        h = (x * r * pw_ref[...]) * (1.0 + mod_ref[1:2, :]) + mod_ref[0:1, :]
        hb = _bf(h)
        h1_ref[...] = hb
        pa_ref[...] = _dot_nt(hb, w_ref[:ATT_COLS, :])
        ph_ref[...] = _dot_nt(hb, w_ref[ATT_COLS:, :])
        _ride_wait(ride_modes, pl.program_id(0), N // TM, ride_in, ride_out, sems)

    return pl.pallas_call(
        body, name="in_proj", grid=(N // TM,),
        in_specs=[pl.BlockSpec((TM, D_MODEL), lambda i: (i, 0)), _mod_spec(tps),
                  pl.BlockSpec((1, D_MODEL), lambda i: (0, 0)),
                  pl.BlockSpec((IN_COLS, D_MODEL), lambda i: (0, 0))] + [ANY_SPEC] * nr,
        out_specs=[pl.BlockSpec((TM, ATT_COLS), lambda i: (i, 0)),
                   pl.BlockSpec((TM, IN_COLS - ATT_COLS), lambda i: (i, 0)),
                   pl.BlockSpec((TM, D_MODEL), lambda i: (i, 0))] + [ANY_SPEC] * nr,
        out_shape=[SDS((N, ATT_COLS), F32), SDS((N, IN_COLS - ATT_COLS), F32), SDS((N, D_MODEL), BF16)]
        + _exchange_shapes(ride_srcs, ride_modes),
        scratch_shapes=_exchange_sems(nr),
        compiler_params=_params(("arbitrary",), VMEM_LIMIT_BIG),
    )(x2, mod8, pre_w, w_in_bf, *ride_srcs)


def _rope_tables(T):
    half = ROPE_DIM // 2
    inv_freq = ROPE_THETA ** (-jnp.arange(0, ROPE_DIM, 2, dtype=F32) / ROPE_DIM)
    ang = jnp.arange(T, dtype=F32)[:, None] * inv_freq[None, :]
    cos, sin = jnp.cos(ang), jnp.sin(ang)
    ones = jnp.ones((T, ATT_HEAD_DIM - ROPE_DIM), F32)
    zeros = jnp.zeros((T, ATT_HEAD_DIM - ROPE_DIM), F32)
    zh = jnp.zeros((T, half), F32)
    cos64 = jnp.concatenate([cos, cos, ones], axis=1)
    sin_left = jnp.concatenate([-sin, zh, zeros], axis=1)
    sin_right = jnp.concatenate([zh, sin, zeros], axis=1)
    rep = LANES // ATT_HEAD_DIM
    return jnp.tile(cos64, (1, rep)), jnp.tile(sin_left, (1, rep)), jnp.tile(sin_right, (1, rep))


def _rope(xc, cs, sl, sr):
    return xc * cs + pltpu.roll(xc, LANES - 8, 1) * sl + pltpu.roll(xc, 8, 1) * sr


def _rope_t(dy, cs, sl, sr):
    return dy * cs + pltpu.roll(dy * sl, 8, 1) + pltpu.roll(dy * sr, LANES - 8, 1)


ATT_SCALE = ATT_HEAD_DIM ** -0.5
ATT_SPLITS = 4


def _lower_mask():
    j = lax.broadcasted_iota(jnp.int32, (WINDOW, ATT_GROUP * WINDOW), 0)
    i = lax.broadcasted_iota(jnp.int32, (WINDOW, ATT_GROUP * WINDOW), 1) & (WINDOW - 1)
    return j <= i


def _sink_row(sink_ref, hk):
    return jnp.concatenate(
        [jnp.full((1, WINDOW), sink_ref[0, ATT_GROUP * hk + g], F32) for g in range(ATT_GROUP)], axis=1)


def _softmax_window(qs, k_cur, k_prev, lower, has_prev, sink):
    s_prev = jnp.where(has_prev, _dot_nt(k_prev, qs), jnp.finfo(F32).min)
    s = jnp.where(lower, _dot_nt(k_cur, qs), s_prev)
    m = jnp.maximum(jnp.max(s, axis=0, keepdims=True), sink)
    p = jnp.exp(s - m)
    es = jnp.exp(sink - m)
    inv = 1.0 / (jnp.sum(p, axis=0, keepdims=True) + es)
    return p, inv, es


def _stack_heads(parts, hk):
    hs = []
    for g in range(ATT_GROUP):
        h = ATT_GROUP * hk + g
        hs.append(parts[h // 2][:, (h % 2) * ATT_HEAD_DIM:(h % 2 + 1) * ATT_HEAD_DIM])
    return jnp.concatenate(hs, axis=0)


def _attn_fwd(proj3, tables, sinks, attn_w, ride_srcs, ride_modes):
    B, T, _ = proj3.shape
    nb = T // WINDOW
    splits = min(ATT_SPLITS, nb)
    per = nb // splits
    nr = len(ride_srcs)
    cos, sinl, sinr = tables

    def body(*refs):
        q_ref, k_ref, v_ref, cos_ref, sl_ref, sr_ref, sink_ref, aw_ref = refs[:8]
        ride_in = refs[8:8 + nr]
        o_ref, an_ref, qr_ref, kr_ref = refs[8 + nr:12 + nr]
        ride_out = refs[12 + nr:12 + 2 * nr]
        kpad, vpad = refs[12 + 2 * nr:14 + 2 * nr]
        sems = refs[14 + 2 * nr:]
        part = pl.program_id(1)
        step = pl.program_id(0) * splits + part
        _ride_start(ride_modes, step, B * splits, ride_in, ride_out, sems)

        @pl.when(part == 0)
        def _():
            kpad[0:WINDOW, :] = jnp.zeros((WINDOW, LANES), BF16)
            vpad[0:WINDOW, :] = jnp.zeros((WINDOW, LANES), BF16)

        lower = _lower_mask()

        def block(n, carry):
            r0 = pl.multiple_of(n * WINDOW, WINDOW)
            rows = pl.ds(r0, WINDOW)
            nxt = pl.ds(r0 + WINDOW, WINDOW)
            cs, sl, sr = cos_ref[rows, :], sl_ref[rows, :], sr_ref[rows, :]
            kb = _bf(_rope(k_ref[rows, :], cs, sl, sr))
            vb = _bf(v_ref[rows, :])
            kpad[nxt, :] = kb
            kr_ref[rows, :] = kb
            vpad[nxt, :] = vb
            qparts = []
            for j in range(ATT_WIDTH // LANES):
                qp = _bf(_rope(q_ref[rows, j * LANES:(j + 1) * LANES], cs, sl, sr) * ATT_SCALE)
                qr_ref[rows, j * LANES:(j + 1) * LANES] = qp
                qparts.append(qp)
            for hk in range(ATT_KV_HEADS):
                lanes = slice(hk * ATT_HEAD_DIM, (hk + 1) * ATT_HEAD_DIM)
                qs = _stack_heads(qparts, hk)
                p, inv, _ = _softmax_window(qs, kb[:, lanes], kpad[rows, lanes], lower, n > 0,
                                            _sink_row(sink_ref, hk))
                p_cur = jnp.where(lower, p, 0.0)
                ot = (_dot_tn(vb[:, lanes], _bf(p_cur)) + _dot_tn(vpad[rows, lanes], _bf(p - p_cur))) * inv
                for g in range(ATT_GROUP):
                    h = ATT_GROUP * hk + g
                    o_ref[rows, h * ATT_HEAD_DIM:(h + 1) * ATT_HEAD_DIM] = ot[:, g * WINDOW:(g + 1) * WINDOW].T
            ob = o_ref[rows, :]
            an_ref[rows, :] = _bf(ob * lax.rsqrt(_mean_last(ob * ob) + EPS) * aw_ref[...])
            return carry

        _loop_pairs(part * per, per, block, 0)
        _ride_wait(ride_modes, step, B * splits, ride_in, ride_out, sems)

    seq = lambda w, j: pl.BlockSpec((None, T, w), lambda b, s: (b, 0, j))
    full = lambda r, w: pl.BlockSpec((r, w), lambda b, s: (0, 0))
    return pl.pallas_call(
        body, name="attn_fwd", grid=(B, splits),
        in_specs=[seq(ATT_WIDTH, 0), seq(LANES, 4), seq(LANES, 5),
                  full(T, LANES), full(T, LANES), full(T, LANES),
                  pl.BlockSpec(memory_space=pltpu.SMEM), full(1, ATT_WIDTH)] + [ANY_SPEC] * nr,
        out_specs=[seq(ATT_WIDTH, 0), seq(ATT_WIDTH, 0), seq(ATT_WIDTH, 0), seq(LANES, 0)] + [ANY_SPEC] * nr,
        out_shape=[SDS((B, T, ATT_WIDTH), F32), SDS((B, T, ATT_WIDTH), BF16),
                   SDS((B, T, ATT_WIDTH), BF16), SDS((B, T, LANES), BF16)] + _exchange_shapes(ride_srcs, ride_modes),
        scratch_shapes=[pltpu.VMEM((T + WINDOW, LANES), BF16), pltpu.VMEM((T + WINDOW, LANES), BF16)]
        + _exchange_sems(nr),
        compiler_params=_params(("arbitrary", "arbitrary"), VMEM_LIMIT_BIG),
    )(proj3, proj3, proj3, cos, sinl, sinr, sinks, attn_w, *ride_srcs)


HG_GROUP = 8
HG_ROWS = HG_GROUP * HG_CHUNK


HG_STACK = HG_GROUP * HG_HEAD_DIM


def _group_masks():
    r = lax.broadcasted_iota(jnp.int32, (HG_ROWS, HG_ROWS), 0)
    c = lax.broadcasted_iota(jnp.int32, (HG_ROWS, HG_ROWS), 1)
    same = (r // HG_CHUNK) == (c // HG_CHUNK)
    return same & (r >= c), same & (c >= r)


def _row_chunk():
    return lax.broadcasted_iota(jnp.int32, (HG_ROWS, HG_HEAD_DIM), 0) // HG_CHUNK


def _spread(a, row_chunk):
    return jnp.concatenate([jnp.where(row_chunk == c, a, jnp.zeros_like(a)) for c in range(HG_GROUP)], axis=1)


def _pick(r, row_chunk):
    out = jnp.where(row_chunk == 0, r[:, :HG_HEAD_DIM], 0.0)
    for c in range(1, HG_GROUP):
        out = out + jnp.where(row_chunk == c, r[:, c * HG_HEAD_DIM:(c + 1) * HG_HEAD_DIM], 0.0)
    return out


def _lane_block(a, c):
    return a[:, c * HG_HEAD_DIM:(c + 1) * HG_HEAD_DIM]


def _ones_bf(mask):
    return jnp.where(mask, 1.0, 0.0).astype(BF16)


def _chunk_bcast(rows_1x128):
    return jnp.concatenate([jnp.broadcast_to(r, (HG_CHUNK, HG_HEAD_DIM)) for r in rows_1x128], axis=0)


def _hgrn_gates(hq, hf, lb, lower_bf):
    sq = _sigmoid(hq)
    q = hq * sq
    sg = _sigmoid(hf)
    f = lb + (1.0 - lb) * sg
    k = 1.0 - f
    logf = jnp.log(f)
    b = _tri_sum(lower_bf, logf)
    bl = [_sum_rows(logf[_chunk_rows(c), :]) for c in range(HG_GROUP)]
    eb, enb, e2 = jnp.exp(b), jnp.exp(-b), jnp.exp(_chunk_bcast(bl) - b)
    ebl = [jnp.exp(r) for r in bl]
    return dict(sq=sq, sg=sg, f=f, eb=eb, enb=enb, e2=e2, ebl=ebl, qd=q * eb, kd=k * enb, k2=k * e2)


def _chunk_rows(c):
    return slice(c * HG_CHUNK, (c + 1) * HG_CHUNK)


def _head_lanes(h):
    return slice(h * HG_HEAD_DIM, (h + 1) * HG_HEAD_DIM)


def _hgrn_fwd(proj_h, lb, hg_w, ride_srcs, ride_modes):
    B, T, _ = proj_h.shape
    ng = T // HG_ROWS
    nr = len(ride_srcs)

    def body(*refs):
        hq_ref, hf_ref, hi_ref, hg_ref, lb_ref, gw_ref = refs[:6]
        ride_in = refs[6:6 + nr]
        o_ref, rg_ref, sp_ref = refs[6 + nr:9 + nr]
        ride_out = refs[9 + nr:9 + 2 * nr]
        st = refs[9 + 2 * nr]
        sems = refs[10 + 2 * nr:]
        gi = pl.program_id(1)
        step = pl.program_id(0) * ng + gi
        _ride_start(ride_modes, step, B * ng, ride_in, ride_out, sems)

        @pl.when(gi == 0)
        def _():
            st[...] = jnp.zeros(st.shape, F32)

        lo, _ = _group_masks()
        lower_bf = _ones_bf(lo)
        row_chunk = _row_chunk()
        for h in range(HG_HEADS):
            lanes = _head_lanes(h)
            gt = _hgrn_gates(hq_ref[:, lanes], hf_ref[:, lanes], lb_ref[:, lanes], lower_bf)
            v, qd, kd = _bf(hi_ref[:, lanes]), _bf(gt["qd"]), _bf(gt["kd"])
            a = jnp.where(lo, _dot_nt(qd, kd), 0.0)
            kv = _dot_tn(v, _bf(_spread(gt["k2"], row_chunk)))
            s = st[h]
            before = []
            for c in range(HG_GROUP):
                before.append(s)
                s = s * gt["ebl"][c] + _lane_block(kv, c)
            st[h] = s
            sp = jnp.concatenate(before, axis=1)
            sp_ref[h] = sp
            o = _dot(_bf(a), v) + _dot_nt(_bf(_spread(gt["qd"], row_chunk)), _bf(sp))
            o_ref[:, lanes] = o
            hg = hg_ref[:, lanes]
            rn = o * lax.rsqrt(_mean_last(o * o) + EPS) * gw_ref[...]
            rg_ref[:, lanes] = _bf(rn * (hg * _sigmoid(hg)))
        _ride_wait(ride_modes, step, B * ng, ride_in, ride_out, sems)

    part = lambda j: pl.BlockSpec((None, HG_ROWS, HG_WIDTH), lambda b, g: (b, g, j))
    return pl.pallas_call(
        body, name="hgrn_fwd", grid=(B, ng),
        in_specs=[part(0), part(1), part(2), part(3),
                  pl.BlockSpec((1, HG_WIDTH), lambda b, g: (0, 0)),
                  pl.BlockSpec((1, LANES), lambda b, g: (0, 0))] + [ANY_SPEC] * nr,
        out_specs=[part(0), part(0),
                   pl.BlockSpec((None, HG_HEADS, None, HG_HEAD_DIM, HG_STACK), lambda b, g: (b, 0, g, 0, 0))]
        + [ANY_SPEC] * nr,
        out_shape=[SDS((B, T, HG_WIDTH), F32), SDS((B, T, HG_WIDTH), BF16),
                   SDS((B, HG_HEADS, ng, HG_HEAD_DIM, HG_STACK), F32)] + _exchange_shapes(ride_srcs, ride_modes),
        scratch_shapes=[pltpu.VMEM((HG_HEADS, HG_HEAD_DIM, HG_HEAD_DIM), F32)] + _exchange_sems(nr),
        compiler_params=_params(("arbitrary", "arbitrary"), VMEM_LIMIT_BIG),
    )(proj_h, proj_h, proj_h, proj_h, lb, hg_w, *ride_srcs)


def _mix_out(x2, attn_n, rec_g, mod8, post_w, w_out_bf, T, ride_srcs, ride_modes):
    N = x2.shape[0]
    TM = _tile_rows(T, big=True)
    tps = T // TM
    nr = len(ride_srcs)

    def body(*refs):
        x_ref, an_ref, rg_ref, mod_ref, pw_ref, w_ref = refs[:6]
        ride_in = refs[6:6 + nr]
        mix_ref, x1_ref, cat_ref = refs[6 + nr:9 + nr]
        ride_out = refs[9 + nr:9 + 2 * nr]
        sems = refs[9 + 2 * nr:]
        _ride_start(ride_modes, pl.program_id(0), N // TM, ride_in, ride_out, sems)
        cat = jnp.concatenate([an_ref[...], rg_ref[...]], axis=1)
        cat_ref[...] = cat
        mix = _dot(cat, w_ref[...])
        mix_ref[...] = mix
        r = lax.rsqrt(_mean_last(mix * mix) + EPS)
        x1_ref[...] = x_ref[...] + mod_ref[2:3, :] * (mix * r * pw_ref[...])
        _ride_wait(ride_modes, pl.program_id(0), N // TM, ride_in, ride_out, sems)

    row = lambda w: pl.BlockSpec((TM, w), lambda i: (i, 0))
    return pl.pallas_call(
        body, name="mix_out", grid=(N // TM,),
        in_specs=[row(D_MODEL), row(ATT_WIDTH), row(HG_WIDTH), _mod_spec(tps),
                  pl.BlockSpec((1, D_MODEL), lambda i: (0, 0)),
                  pl.BlockSpec((D_MODEL, D_MODEL), lambda i: (0, 0))] + [ANY_SPEC] * nr,
        out_specs=[row(D_MODEL), row(D_MODEL), row(D_MODEL)] + [ANY_SPEC] * nr,
        out_shape=[SDS((N, D_MODEL), F32), SDS((N, D_MODEL), F32), SDS((N, D_MODEL), BF16)]
        + _exchange_shapes(ride_srcs, ride_modes),
        scratch_shapes=_exchange_sems(nr),
        compiler_params=_params(("arbitrary",), VMEM_LIMIT_BIG),
    )(x2, attn_n, rec_g, mod8, post_w, w_out_bf, *ride_srcs)


def _load_weights_once(pairs, sem):
    @pl.when(pl.program_id(0) == 0)
    def _():
        cps = [pltpu.make_async_copy(src, dst, sem.at[i]) for i, (src, dst) in enumerate(pairs)]
        for cp in cps:
            cp.start()
        for cp in cps:
            cp.wait()


MLP_HALF = D_MODEL // 2
MLP_PIECES = 2 * N_DEV + 2


def _mlp_weight_pieces(wu_a, wu_b, wd_a, wd_b, wu, wd):
    cols = D_FF // N_DEV
    pairs = []
    for h, half in enumerate((wu_a, wu_b)):
        for j in range(N_DEV):
            pairs.append((half.at[j], wu.at[pl.ds(h * MLP_HALF, MLP_HALF), pl.ds(j * cols, cols)]))
    for h, half in enumerate((wd_a, wd_b)):
        pairs.append((half, wd.at[:, pl.ds(h * MLP_HALF, MLP_HALF)]))
    return pairs


def _mlp_fwd(x1, mod8, pre_w, w_up_halves, w_down_halves, T):
    N = x1.shape[0]
    TM = _tile_rows(T)
    tps = T // TM

    def body(x_ref, mod_ref, pw_ref, wua, wub, wda, wdb, up_ref, u_ref, d_ref, h2_ref, wu, wd, sem):
        _load_weights_once(_mlp_weight_pieces(wua, wub, wda, wdb, wu, wd), sem)
        x = x_ref[...]
        r = lax.rsqrt(_mean_last(x * x) + EPS)
        h = (x * r * pw_ref[...]) * (1.0 + mod_ref[4:5, :]) + mod_ref[3:4, :]
        hb = _bf(h)
        h2_ref[...] = hb
        up = _dot(hb, wu[...])
        up_ref[...] = up
        ru = jnp.maximum(up, 0.0)
        u = _bf(ru * ru)
        u_ref[...] = u
        d_ref[...] = _dot(u, wd[...])

    row = lambda w: pl.BlockSpec((TM, w), lambda i: (i, 0))
    return pl.pallas_call(
        body, name="mlp_fwd", grid=(N // TM,),
        in_specs=[row(D_MODEL), _mod_spec(tps), pl.BlockSpec((1, D_MODEL), lambda i: (0, 0))] + [ANY_SPEC] * 4,
        out_specs=[row(D_FF), row(D_FF), row(D_MODEL), row(D_MODEL)],
        out_shape=[SDS((N, D_FF), F32), SDS((N, D_FF), BF16), SDS((N, D_MODEL), F32), SDS((N, D_MODEL), BF16)],
        scratch_shapes=[pltpu.VMEM((D_MODEL, D_FF), BF16), pltpu.VMEM((D_FF, D_MODEL), BF16),
                        pltpu.SemaphoreType.DMA((MLP_PIECES,))],
        compiler_params=_params(("arbitrary",), VMEM_LIMIT_BIG),
    )(x1, mod8, pre_w, *w_up_halves, *w_down_halves)


def _acc_rows(acc_ref, first, rows):
    @pl.when(first)
    def _():
        acc_ref[...] = jnp.zeros(acc_ref.shape, F32)
    for i, r in enumerate(rows):
        acc_ref[i:i + 1, :] += r


def _mlp_bwd(x1, d, up, tgt, mod8, pre_w, post_w, w_up_halves, w_down_halves, T):
    N = x1.shape[0]
    TM = _tile_rows(T)
    tps = T // TM

    def body(x_ref, d_ref, up_ref, t_ref, mod_ref, pw_ref, qw_ref, wua, wub, wda, wdb,
             dx_ref, dup_ref, dd_ref, acc_ref, wd, wu, sem):
        _load_weights_once(_mlp_weight_pieces(wua, wub, wda, wdb, wu, wd), sem)
        sh2, sc2, g2 = mod_ref[3:4, :], mod_ref[4:5, :], mod_ref[5:6, :]
        x = x_ref[...]
        r1 = lax.rsqrt(_mean_last(x * x) + EPS)
        xh = x * r1
        n2 = xh * pw_ref[...]
        dv = d_ref[...]
        rd = lax.rsqrt(_mean_last(dv * dv) + EPS)
        dh = dv * rd
        rr = dh * qw_ref[...]
        e = x + g2 * rr - t_ref[...]
        loss = 0.5 * jnp.sum(_sum_rows(e * e), axis=1, keepdims=True) / D_MODEL
        dy = e * (1.0 / D_MODEL)
        dg2 = _sum_rows(dy * rr)
        drr = dy * g2
        dw_post = _sum_rows(drr * dh)
        ddh = drr * qw_ref[...]
        dd = _bf(rd * (ddh - dh * _mean_last(ddh * dh)))
        dd_ref[...] = dd
        ru = jnp.maximum(up_ref[...], 0.0)
        dup = _bf(_dot_nt(dd, wd[...]) * (2.0 * ru))
        dup_ref[...] = dup
        dh2 = _dot_nt(dup, wu[...])
        dsh2 = _sum_rows(dh2)
        dsc2 = _sum_rows(dh2 * n2)
        dn2 = dh2 * (1.0 + sc2)
        dw_pre = _sum_rows(dn2 * xh)
        dxh = dn2 * pw_ref[...]
        dx_ref[...] = dy + r1 * (dxh - xh * _mean_last(dxh * xh))
        _acc_rows(acc_ref, pl.program_id(0) % tps == 0,
                  [dsh2, dsc2, dg2, dw_pre, dw_post, jnp.broadcast_to(loss, (1, D_MODEL))])

    row = lambda w: pl.BlockSpec((TM, w), lambda i: (i, 0))
    vec = pl.BlockSpec((1, D_MODEL), lambda i: (0, 0))
    B = N // T
    return pl.pallas_call(
        body, name="mlp_bwd", grid=(N // TM,),
        in_specs=[row(D_MODEL), row(D_MODEL), row(D_FF), row(D_MODEL), _mod_spec(tps), vec, vec] + [ANY_SPEC] * 4,
        out_specs=[row(D_MODEL), row(D_FF), row(D_MODEL), _mod_spec(tps)],
        out_shape=[SDS((N, D_MODEL), F32), SDS((N, D_FF), BF16), SDS((N, D_MODEL), BF16),
                   SDS((B, 8, D_MODEL), F32)],
        scratch_shapes=[pltpu.VMEM((D_FF, D_MODEL), BF16), pltpu.VMEM((D_MODEL, D_FF), BF16),
                        pltpu.SemaphoreType.DMA((MLP_PIECES,))],
        compiler_params=_params(("arbitrary",), VMEM_LIMIT_BIG),
    )(x1, d, up, tgt, mod8, pre_w, post_w, *w_up_halves, *w_down_halves)


def _mix_bwd(mix, dx1, mod8, post_w, w_out_bf, T, ride_srcs, ride_modes):
    N = mix.shape[0]
    TM = _tile_rows(T, big=True)
    tps = T // TM
    nr = len(ride_srcs)

    def body(*refs):
        mix_ref, dx_ref, mod_ref, pw_ref, w_ref = refs[:5]
        ride_in = refs[5:5 + nr]
        dan_ref, drg_ref, dmix_ref, acc_ref = refs[5 + nr:9 + nr]
        ride_out = refs[9 + nr:9 + 2 * nr]
        sems = refs[9 + 2 * nr:]
        _ride_start(ride_modes, pl.program_id(0), N // TM, ride_in, ride_out, sems)
        g1 = mod_ref[2:3, :]
        mix = mix_ref[...]
        dx1 = dx_ref[...]
        rm = lax.rsqrt(_mean_last(mix * mix) + EPS)
        mh = mix * rm
        dg1 = _sum_rows(dx1 * (mh * pw_ref[...]))
        dr = dx1 * g1
        dw_post = _sum_rows(dr * mh)
        dmh = dr * pw_ref[...]
        dmix = _bf(rm * (dmh - mh * _mean_last(dmh * mh)))
        dmix_ref[...] = dmix
        dcat = _dot_nt(dmix, w_ref[...])
        dan_ref[...] = dcat[:, :ATT_WIDTH]
        drg_ref[...] = dcat[:, ATT_WIDTH:]
        _acc_rows(acc_ref, pl.program_id(0) % tps == 0, [dg1, dw_post])
        _ride_wait(ride_modes, pl.program_id(0), N // TM, ride_in, ride_out, sems)

    row = lambda w: pl.BlockSpec((TM, w), lambda i: (i, 0))
    B = N // T
    return pl.pallas_call(
        body, name="mix_bwd", grid=(N // TM,),
        in_specs=[row(D_MODEL), row(D_MODEL), _mod_spec(tps), pl.BlockSpec((1, D_MODEL), lambda i: (0, 0)),
                  pl.BlockSpec((D_MODEL, D_MODEL), lambda i: (0, 0))] + [ANY_SPEC] * nr,
        out_specs=[row(ATT_WIDTH), row(HG_WIDTH), row(D_MODEL), _mod_spec(tps)] + [ANY_SPEC] * nr,
        out_shape=[SDS((N, ATT_WIDTH), F32), SDS((N, HG_WIDTH), F32), SDS((N, D_MODEL), BF16),
                   SDS((B, 8, D_MODEL), F32)] + _exchange_shapes(ride_srcs, ride_modes),
        scratch_shapes=_exchange_sems(nr),
        compiler_params=_params(("arbitrary",), VMEM_LIMIT_BIG),
    )(mix, dx1, mod8, post_w, w_out_bf, *ride_srcs)


def _hgrn_bwd(proj_h, lb, hg_w, o, s_prev, drg, ride_srcs, ride_modes):
    B, T, _ = proj_h.shape
    ng = T // HG_ROWS
    nr = len(ride_srcs)

    def body(*refs):
        hq_ref, hf_ref, hi_ref, hg_ref, lb_ref, gw_ref, o_ref, sp_ref, drg_ref = refs[:9]
        ride_in = refs[9:9 + nr]
        dhq_ref, dhf_ref, dhi_ref, dhg_ref, dlb_ref, dgw_ref = refs[9 + nr:15 + nr]
        ride_out = refs[15 + nr:15 + 2 * nr]
        dst = refs[15 + 2 * nr]
        sems = refs[16 + 2 * nr:]
        step = pl.program_id(0) * ng + pl.program_id(1)
        _ride_start(ride_modes, step, B * ng, ride_in, ride_out, sems)

        @pl.when(pl.program_id(1) == 0)
        def _():
            dst[...] = jnp.zeros(dst.shape, F32)
            dlb_ref[...] = jnp.zeros(dlb_ref.shape, F32)
            dgw_ref[...] = jnp.zeros(dgw_ref.shape, F32)

        lo, up = _group_masks()
        lower_bf, upper_bf = _ones_bf(lo), _ones_bf(up)
        row_chunk = _row_chunk()
        gw = gw_ref[...]

        for h in range(HG_HEADS):
            lanes = _head_lanes(h)
            lbv = lb_ref[:, lanes]
            hq = hq_ref[:, lanes]
            gt = _hgrn_gates(hq, hf_ref[:, lanes], lbv, lower_bf)
            sq, sg, qdf, kdf, k2f, ebl = gt["sq"], gt["sg"], gt["qd"], gt["kd"], gt["k2"], gt["ebl"]
            v, qd, kd = _bf(hi_ref[:, lanes]), _bf(qdf), _bf(kdf)
            ov = o_ref[:, lanes]
            hg = hg_ref[:, lanes]
            shg = _sigmoid(hg)
            dr = drg_ref[:, lanes]
            ro = lax.rsqrt(_mean_last(ov * ov) + EPS)
            oh = ov * ro
            dhg_ref[:, lanes] = _bf(dr * (oh * gw) * (shg + hg * shg * (1.0 - shg)))
            drn = dr * (hg * shg)
            dgw_ref[...] += jnp.broadcast_to(_sum_rows(drn * oh), (8, LANES))
            doh = drn * gw
            do = _bf(ro * (doh - oh * _mean_last(doh * oh)))
            a = jnp.where(lo, _dot_nt(qd, kd), 0.0)
            da = _bf(jnp.where(lo, _dot_nt(do, v), 0.0))
            dv = _dot_tn(_bf(a), do)
            dqd = _dot(da, kd)
            dkd = _dot_tn(da, qd)
            sp = sp_ref[h]
            incr = _dot_tn(do, _bf(_spread(qdf, row_chunk)))
            ds = dst[h]
            after = [None] * HG_GROUP
            for c in reversed(range(HG_GROUP)):
                after[c] = ds
                ds = ds * ebl[c] + _lane_block(incr, c)
            dst[h] = ds
            dss = jnp.concatenate(after, axis=1)
            dssb = _bf(dss)
            dk2 = _pick(_dot(v, dssb), row_chunk)
            dhi_ref[:, lanes] = _bf(dv + _dot_nt(_bf(_spread(k2f, row_chunk)), dssb))
            dqd = dqd + _pick(_dot(do, _bf(sp)), row_chunk)
            debl = _sum_rows(dss * sp)
            k2g = dk2 * k2f
            db = dqd * qdf - dkd * kdf - k2g
            dk = dkd * gt["enb"] + dk2 * gt["e2"]
            dbl = _chunk_bcast([_lane_block(debl, c) * ebl[c] + _sum_rows(k2g[_chunk_rows(c), :])
                                for c in range(HG_GROUP)])
            dg = _tri_sum(upper_bf, db, terms=2) + dbl
            df = dg / gt["f"] - dk
            dhf_ref[:, lanes] = _bf(df * (1.0 - lbv) * sg * (1.0 - sg))
            dlb_ref[:, lanes] += jnp.broadcast_to(_sum_rows(df * (1.0 - sg)), (8, LANES))
            dhq_ref[:, lanes] = _bf((dqd * gt["eb"]) * (sq + hq * sq * (1.0 - sq)))
        _ride_wait(ride_modes, step, B * ng, ride_in, ride_out, sems)

    part = lambda j: pl.BlockSpec((None, HG_ROWS, HG_WIDTH), lambda b, g: (b, ng - 1 - g, j))
    return pl.pallas_call(
        body, name="hgrn_bwd", grid=(B, ng),
        in_specs=[part(0), part(1), part(2), part(3),
                  pl.BlockSpec((1, HG_WIDTH), lambda b, g: (0, 0)),
                  pl.BlockSpec((1, LANES), lambda b, g: (0, 0)),
                  part(0),
                  pl.BlockSpec((None, HG_HEADS, None, HG_HEAD_DIM, HG_STACK), lambda b, g: (b, 0, ng - 1 - g, 0, 0)),
                  part(0)] + [ANY_SPEC] * nr,
        out_specs=[part(0), part(0), part(0), part(0),
                   pl.BlockSpec((None, 8, HG_WIDTH), lambda b, g: (b, 0, 0)),
                   pl.BlockSpec((None, 8, LANES), lambda b, g: (b, 0, 0))] + [ANY_SPEC] * nr,
        out_shape=[SDS((B, T, HG_WIDTH), BF16)] * 4 + [SDS((B, 8, HG_WIDTH), F32), SDS((B, 8, LANES), F32)]
        + _exchange_shapes(ride_srcs, ride_modes),
        scratch_shapes=[pltpu.VMEM((HG_HEADS, HG_HEAD_DIM, HG_HEAD_DIM), F32)] + _exchange_sems(nr),
        compiler_params=_params(("arbitrary", "arbitrary"), VMEM_LIMIT_BIG),
    )(proj_h, proj_h, proj_h, proj_h, lb, hg_w, o, s_prev, drg, *ride_srcs)


def _attn_bwd(qr, kr, proj3, attn_o, dan, tables, sinks, attn_w, ride_srcs, ride_modes):
    B, T, _ = proj3.shape
    nb = T // WINDOW
    splits = min(ATT_SPLITS, nb)
    per = nb // splits
    nr = len(ride_srcs)
    cos, sinl, sinr = tables
    QKV = ATT_WIDTH + 2 * LANES

    def body(*refs):
        qr_ref, kr_ref, v_ref, o_ref, dan_ref, cos_ref, sl_ref, sr_ref, sink_ref, aw_ref = refs[:10]
        ride_in = refs[10:10 + nr]
        dqkv_ref, dsink_ref, daw_ref = refs[10 + nr:13 + nr]
        ride_out = refs[13 + nr:13 + 2 * nr]
        kpad, vpad, dkpad, dvpad, dqb, dsk = refs[13 + 2 * nr:19 + 2 * nr]
        sems = refs[19 + 2 * nr:]
        part = pl.program_id(1)
        step = pl.program_id(0) * splits + part
        _ride_start(ride_modes, step, B * splits, ride_in, ride_out, sems)

        @pl.when(part == 0)
        def _():
            kpad[0:WINDOW, :] = jnp.zeros((WINDOW, LANES), BF16)
            vpad[0:WINDOW, :] = jnp.zeros((WINDOW, LANES), BF16)
            kpad[WINDOW:, :] = kr_ref[...]
            vpad[WINDOW:, :] = _bf(v_ref[...])
            dkpad[...] = jnp.zeros(dkpad.shape, F32)
            dvpad[...] = jnp.zeros(dvpad.shape, F32)
            dsk[...] = jnp.zeros(dsk.shape, F32)
            daw_ref[...] = jnp.zeros(daw_ref.shape, F32)

        lower = _lower_mask()
        aw = aw_ref[...]

        def block(n, daw):
            r0 = pl.multiple_of(n * WINDOW, WINDOW)
            rows = pl.ds(r0, WINDOW)
            nxt = pl.ds(r0 + WINDOW, WINDOW)
            ob = o_ref[rows, :]
            dn = dan_ref[rows, :]
            ro = lax.rsqrt(_mean_last(ob * ob) + EPS)
            oh = ob * ro
            daw = daw + _sum_rows(dn * oh)
            doh = dn * aw
            do = _bf(ro * (doh - oh * _mean_last(doh * oh)))
            doparts = [do[:, j * LANES:(j + 1) * LANES] for j in range(ATT_WIDTH // LANES)]
            qparts = [qr_ref[rows, j * LANES:(j + 1) * LANES] for j in range(ATT_WIDTH // LANES)]
            for hk in range(ATT_KV_HEADS):
                lanes = slice(hk * ATT_HEAD_DIM, (hk + 1) * ATT_HEAD_DIM)
                qs = _stack_heads(qparts, hk)
                dos = _stack_heads(doparts, hk)
                k_cur, k_prev = kpad[nxt, lanes], kpad[rows, lanes]
                v_cur, v_prev = vpad[nxt, lanes], vpad[rows, lanes]
                p, inv, es = _softmax_window(qs, k_cur, k_prev, lower, n > 0, _sink_row(sink_ref, hk))
                p = p * inv
                dp = jnp.where(lower, _dot_nt(v_cur, dos), _dot_nt(v_prev, dos))
                delta = jnp.sum(p * dp, axis=0, keepdims=True)
                ds = p * (dp - delta)
                sk = (es * inv) * delta
                ds_cur = jnp.where(lower, ds, 0.0)
                p_cur = jnp.where(lower, p, 0.0)
                ds_cur, ds_prev = _bf(ds_cur), _bf(ds - ds_cur)
                p_cur, p_prev = _bf(p_cur), _bf(p - p_cur)
                dqt = (_dot_tn(k_cur, ds_cur) + _dot_tn(k_prev, ds_prev)) * ATT_SCALE
                dkpad[nxt, lanes] += _dot(ds_cur, qs)
                dkpad[rows, lanes] += _dot(ds_prev, qs)
                dvpad[nxt, lanes] += _dot(p_cur, dos)
                dvpad[rows, lanes] += _dot(p_prev, dos)
                for g in range(ATT_GROUP):
                    h = ATT_GROUP * hk + g
                    cols = slice(g * WINDOW, (g + 1) * WINDOW)
                    dqb[:, h * ATT_HEAD_DIM:(h + 1) * ATT_HEAD_DIM] = dqt[:, cols].T
                    dsk[h:h + 1, :] += jnp.broadcast_to(-jnp.sum(sk[:, cols], axis=1, keepdims=True), (1, LANES))
            cs, sl, sr = cos_ref[rows, :], sl_ref[rows, :], sr_ref[rows, :]
            for j in range(ATT_WIDTH // LANES):
                dqkv_ref[rows, j * LANES:(j + 1) * LANES] = _bf(_rope_t(dqb[:, j * LANES:(j + 1) * LANES], cs, sl, sr))
            return daw

        daw = _loop_pairs(part * per, per, block, jnp.zeros((1, ATT_WIDTH), F32))
        daw_ref[...] += jnp.broadcast_to(daw, (8, ATT_WIDTH))
        dsink_ref[...] = dsk[...]

        def finish(n, carry):
            r0 = pl.multiple_of(n * WINDOW, WINDOW)
            rows = pl.ds(r0, WINDOW)
            nxt = pl.ds(r0 + WINDOW, WINDOW)
            cs, sl, sr = cos_ref[rows, :], sl_ref[rows, :], sr_ref[rows, :]
            dqkv_ref[rows, ATT_WIDTH:ATT_WIDTH + LANES] = _bf(_rope_t(dkpad[nxt, :], cs, sl, sr))
            dqkv_ref[rows, ATT_WIDTH + LANES:QKV] = _bf(dvpad[nxt, :])
            return carry

        @pl.when(part == splits - 1)
        def _():
            lax.fori_loop(0, nb, finish, 0)

        _ride_wait(ride_modes, step, B * splits, ride_in, ride_out, sems)

    seq = lambda w, j: pl.BlockSpec((None, T, w), lambda b, s: (b, 0, j))
    full = lambda r, w: pl.BlockSpec((r, w), lambda b, s: (0, 0))
    return pl.pallas_call(
        body, name="attn_bwd", grid=(B, splits),
        in_specs=[seq(ATT_WIDTH, 0), seq(LANES, 0), seq(LANES, 5), seq(ATT_WIDTH, 0), seq(ATT_WIDTH, 0),
                  full(T, LANES), full(T, LANES), full(T, LANES),
                  pl.BlockSpec(memory_space=pltpu.SMEM), full(1, ATT_WIDTH)] + [ANY_SPEC] * nr,
        out_specs=[seq(QKV, 0), pl.BlockSpec((None, 8, LANES), lambda b, s: (b, 0, 0)),
                   pl.BlockSpec((None, 8, ATT_WIDTH), lambda b, s: (b, 0, 0))] + [ANY_SPEC] * nr,
        out_shape=[SDS((B, T, QKV), BF16), SDS((B, 8, LANES), F32), SDS((B, 8, ATT_WIDTH), F32)]
        + _exchange_shapes(ride_srcs, ride_modes),
        scratch_shapes=[pltpu.VMEM((T + WINDOW, LANES), BF16), pltpu.VMEM((T + WINDOW, LANES), BF16),
                        pltpu.VMEM((T + WINDOW, LANES), F32), pltpu.VMEM((T + WINDOW, LANES), F32),
                        pltpu.VMEM((WINDOW, ATT_WIDTH), F32), pltpu.VMEM((8, LANES), F32)] + _exchange_sems(nr),
        compiler_params=_params(("arbitrary", "arbitrary"), VMEM_LIMIT_BIG),
    )(qr, kr, proj3, attn_o, dan, cos, sinl, sinr, sinks, attn_w, *ride_srcs)


def _in_bwd(x2, dx1, dqkv, dhq, dhf, dhi, dhg, mod8, pre_w, w_in_bf, T, ride_srcs, ride_modes):
    N = x2.shape[0]
    TM = _tile_rows(T, big=True)
    tps = T // TM
    nr = len(ride_srcs)
    pieces = [(0, ATT_WIDTH + 2 * LANES), (768, HG_WIDTH), (1280, HG_WIDTH), (1792, HG_WIDTH), (2304, HG_WIDTH)]

    def body(*refs):
        x_ref, dx_ref, p0, p1, p2, p3, p4, mod_ref, pw_ref, w_ref = refs[:10]
        ride_in = refs[10:10 + nr]
        gx_ref, dproj_ref, acc_ref = refs[10 + nr:13 + nr]
        ride_out = refs[13 + nr:13 + 2 * nr]
        sems = refs[13 + 2 * nr:]
        _ride_start(ride_modes, pl.program_id(0), N // TM, ride_in, ride_out, sems)
        sc1 = mod_ref[1:2, :]
        dh = jnp.zeros((TM, D_MODEL), F32)
        for ref, (off, width) in zip((p0, p1, p2, p3, p4), pieces):
            pb = ref[...]
            dproj_ref[:, off:off + width] = pb
            dh = dh + _dot(pb, w_ref[off:off + width, :])
        x = x_ref[...]
        r = lax.rsqrt(_mean_last(x * x) + EPS)
        xh = x * r
        n1 = xh * pw_ref[...]
        dsh1 = _sum_rows(dh)
        dsc1 = _sum_rows(dh * n1)
        dn1 = dh * (1.0 + sc1)
        dw_pre = _sum_rows(dn1 * xh)
        dxh = dn1 * pw_ref[...]
        gx_ref[...] = dx_ref[...] + r * (dxh - xh * _mean_last(dxh * xh))
        _acc_rows(acc_ref, pl.program_id(0) % tps == 0, [dsh1, dsc1, dw_pre])
        _ride_wait(ride_modes, pl.program_id(0), N // TM, ride_in, ride_out, sems)

    row = lambda w: pl.BlockSpec((TM, w), lambda i: (i, 0))
    B = N // T
    return pl.pallas_call(
        body, name="in_bwd", grid=(N // TM,),
        in_specs=[row(D_MODEL), row(D_MODEL), row(768), row(HG_WIDTH), row(HG_WIDTH), row(HG_WIDTH),
                  row(HG_WIDTH), _mod_spec(tps), pl.BlockSpec((1, D_MODEL), lambda i: (0, 0)),
                  pl.BlockSpec((IN_COLS, D_MODEL), lambda i: (0, 0))] + [ANY_SPEC] * nr,
        out_specs=[row(D_MODEL), row(IN_COLS), _mod_spec(tps)] + [ANY_SPEC] * nr,
        out_shape=[SDS((N, D_MODEL), F32), SDS((N, IN_COLS), BF16), SDS((B, 8, D_MODEL), F32)]
        + _exchange_shapes(ride_srcs, ride_modes),
        scratch_shapes=_exchange_sems(nr),
        compiler_params=_params(("arbitrary",), VMEM_LIMIT_BIG),
    )(x2, dx1, dqkv, dhq, dhf, dhi, dhg, mod8, pre_w, w_in_bf, *ride_srcs)


def _matmul_tn(name, a, b, tn, tm=512, by_owner_cols=False):
    K, M = a.shape
    Nc = b.shape[1]
    tm = min(tm, M)

    def body(a_ref, b_ref, o_ref):
        o_ref[...] = _bf(_dot_tn(a_ref[...], b_ref[...]))

    if by_owner_cols:
        assert tn * N_DEV == Nc
        out_shape = SDS((N_DEV, M, tn), BF16)
        out_spec = pl.BlockSpec((None, tm, tn), lambda i, j: (j, i, 0))
    else:
        out_shape = SDS((M, Nc), BF16)
        out_spec = pl.BlockSpec((tm, tn), lambda i, j: (i, j))
    return pl.pallas_call(
        body, name=name, grid=(M // tm, Nc // tn),
        in_specs=[pl.BlockSpec((K, tm), lambda i, j: (0, i)),
                  pl.BlockSpec((K, tn), lambda i, j: (0, j))],
        out_specs=out_spec, out_shape=out_shape,
        compiler_params=_params(("arbitrary", "arbitrary"), VMEM_LIMIT_BIG),
    )(a, b)


def _adamw_math(w, g, m, v):
    m2 = ADAM_B1 * m + (1.0 - ADAM_B1) * g
    v2 = ADAM_B2 * v + (1.0 - ADAM_B2) * (g * g)
    m_hat = m2 / (1.0 - ADAM_B1 ** ADAM_STEP)
    v_hat = v2 / (1.0 - ADAM_B2 ** ADAM_STEP)
    delta = -ADAM_LR * (m_hat / (jnp.sqrt(v_hat) + ADAM_EPS) + ADAM_WD * w)
    return delta, m2, v2


def _pair_add(name, gw, theirs):
    chips, _, r, c = gw.shape
    tr = r
    core = lax.axis_index("c").astype(jnp.int32).reshape(1)

    def body(core_ref, mine_ref, theirs_ref, o_ref):
        o_ref[...] = _bf(mine_ref[...].astype(F32) + theirs_ref[...].astype(F32))

    block = pl.BlockSpec((None, tr, c), lambda s, i, core_ref: (s, i, 0))
    grid_spec = pltpu.PrefetchScalarGridSpec(
        num_scalar_prefetch=1, grid=(chips, r // tr),
        in_specs=[pl.BlockSpec((None, None, tr, c), lambda s, i, core_ref: (s, core_ref[0], i, 0)), block],
        out_specs=block)
    return pl.pallas_call(
        body, name=name, grid_spec=grid_spec, out_shape=SDS((chips, r, c), BF16),
        compiler_params=_params(("arbitrary", "arbitrary")),
    )(core, gw, theirs)


def _reduce_adamw(name, parts, w, m, v):
    r, c = w.shape
    tr = r if r % 256 else 256
    slots = parts.shape[0]

    def body(p_ref, w_ref, m_ref, v_ref, g_ref, d_ref, m2_ref, v2_ref):
        g = p_ref[0].astype(F32)
        for s in range(1, slots):
            g = g + p_ref[s].astype(F32)
        g_ref[...] = g
        d_ref[...], m2_ref[...], v2_ref[...] = _adamw_math(w_ref[...], g, m_ref[...], v_ref[...])

    blk = pl.BlockSpec((tr, c), lambda i: (i, 0))
    return pl.pallas_call(
        body, name=name, grid=(r // tr,),
        in_specs=[pl.BlockSpec((slots, tr, c), lambda i: (0, i, 0)), blk, blk, blk],
        out_specs=[blk] * 4, out_shape=[SDS((r, c), F32)] * 4,
        compiler_params=_params(("arbitrary",), VMEM_LIMIT_BIG),
    )(parts, w, m, v)


def _ada_grad_adamw(c_all, dmod_all, w, m, v):
    r, c = w.shape
    tr = 256
    nb = c_all.shape[0]

    def body(c_ref, dm_ref, w_ref, m_ref, v_ref, g_ref, d_ref, m2_ref, v2_ref):
        cv = c_ref[...]
        g = _dot_tn(cv * _sigmoid(cv), dm_ref[...])
        g_ref[...] = g
        d_ref[...], m2_ref[...], v2_ref[...] = _adamw_math(w_ref[...], g, m_ref[...], v_ref[...])

    blk = pl.BlockSpec((tr, c), lambda i: (i, 0))
    return pl.pallas_call(
        body, name="ada_grad_adamw", grid=(r // tr,),
        in_specs=[pl.BlockSpec((nb, tr), lambda i: (0, i)), pl.BlockSpec((nb, c), lambda i: (0, 0)),
                  blk, blk, blk],
        out_specs=[blk] * 4, out_shape=[SDS((r, c), F32)] * 4,
        compiler_params=_params(("arbitrary",)),
    )(c_all, dmod_all, w, m, v)


_SMALL = [("b_ada", 6144), ("pre_w_mix", 1024), ("attn_sinks", 128), ("attn_out_w", 512), ("lb_table", 1024),
          ("hg_norm_w", 128), ("post_w_mix", 1024), ("pre_w_mlp", 1024), ("post_w_mlp", 1024)]


def _pack_small(vals, loss_part):
    out = []
    for name, width in _SMALL:
        f = vals[name].reshape(-1).astype(F32)
        out.append(jnp.pad(f, (0, width - f.shape[0])))
    out.append(jnp.broadcast_to(loss_part, (LANES,)))
    return jnp.concatenate(out).reshape(1, -1)


def _adamw_small(parts, given):
    names = [n for n, _ in _SMALL]
    flat_in = [a for n in names for a in given[n]]

    def body(*refs):
        p_ref = refs[0]
        in_refs = refs[1:1 + 3 * len(names)]
        out_refs = refs[1 + 3 * len(names):-1]
        loss_ref = refs[-1]
        g = p_ref[0]
        for s in range(1, N_DEV):
            g = g + p_ref[s]
        off = 0
        for i, (name, width) in enumerate(_SMALL):
            w_ref, m_ref, v_ref = in_refs[3 * i:3 * i + 3]
            rows, cols = w_ref.shape
            for r in range(rows):
                gr = g[:, off + r * cols:off + (r + 1) * cols]
                res = (gr,) + _adamw_math(w_ref[r:r + 1, :], gr, m_ref[r:r + 1, :], v_ref[r:r + 1, :])
                for o_ref, val in zip(out_refs[4 * i:4 * i + 4], res):
                    o_ref[r:r + 1, :] = val
            off += width
        loss_ref[...] = g[:, off:off + LANES]

    out_shape = [SDS(given[n][0].shape, F32) for n in names for _ in range(4)] + [SDS((1, LANES), F32)]
    outs = pl.pallas_call(body, name="adamw_small", out_shape=out_shape)(parts, *flat_in)
    return {n: tuple(outs[4 * i:4 * i + 4]) for i, n in enumerate(names)}, outs[-1][0, 0]


def kernel(x, c, w_ada, b_ada, pre_w_mix, w_in, attn_sinks, attn_out_w, lb_table, hg_norm_w, w_out, post_w_mix, pre_w_mlp, w_up, w_down, post_w_mlp, loss_target, m_w_ada, m_b_ada, m_pre_w_mix, m_w_in, m_attn_sinks, m_attn_out_w, m_lb_table, m_hg_norm_w, m_w_out, m_post_w_mix, m_pre_w_mlp, m_w_up, m_w_down, m_post_w_mlp, v_w_ada, v_b_ada, v_pre_w_mix, v_w_in, v_attn_sinks, v_attn_out_w, v_lb_table, v_hg_norm_w, v_w_out, v_post_w_mix, v_pre_w_mlp, v_w_up, v_w_down, v_post_w_mlp):
    B, T, _ = x.shape
    N = B * T
    me = 4 * lax.axis_index("x") + 2 * lax.axis_index("y") + lax.axis_index("c")
    x2 = x.reshape(N, D_MODEL)
    tgt2 = loss_target.reshape(N, D_MODEL)

    w_in_t, m_w_in_t, v_w_in_t = w_in[0].T, m_w_in[0].T, v_w_in[0].T
    w_in_g, c_g = _exchange("gather_w_in", [_bf(w_in_t), c], ["gather"] * 2)
    w_in_f = w_in_g.reshape(IN_COLS, D_MODEL)
    c_all = c_g.reshape(N_DEV * B, D_MODEL)

    ada_cols = w_ada.shape[2]
    b_mine = lax.dynamic_slice(b_ada, (0, me * ada_cols), (1, ada_cols))
    mod_cols = _ada_mod(c_all, w_ada[0], b_mine)
    (mod_g,) = _exchange("scatter_mod", [mod_cols.reshape(N_DEV, B, ada_cols)], ["a2a"])
    mod = mod_g.transpose(1, 0, 2).reshape(B, 6, D_MODEL)
    mod8 = jnp.pad(mod, ((0, 0), (0, 2), (0, 0)))

    lb_p = jax.nn.softmax(lb_table, axis=0)
    lb = lb_p[1:2]
    tables = _rope_tables(T)

    w_up_b, w_down_b = _bf(w_up[0]), _bf(w_down[0])
    proj_a, proj_h, h1, w_out_g = _in_proj(x2, mod8, pre_w_mix, w_in_f, T, [_bf(w_out[0])], ["gather"])
    proj3 = proj_a.reshape(B, T, ATT_COLS)
    proj_h = proj_h.reshape(B, T, IN_COLS - ATT_COLS)
    rec_o, rec_g, s_prev, w_up_g0, w_up_g1 = _hgrn_fwd(proj_h, lb, hg_norm_w,
                                                       [w_up_b[:MLP_HALF], w_up_b[MLP_HALF:]], ["gather"] * 2)
    attn_o, attn_n, qr, kr, w_down_g0 = _attn_fwd(proj3, tables, attn_sinks, attn_out_w,
                                                  [w_down_b[:, :MLP_HALF]], ["gather"])
    w_out_f = w_out_g.reshape(D_MODEL, D_MODEL)
    mix, x1, cat, w_down_g1 = _mix_out(x2, attn_n.reshape(N, ATT_WIDTH), rec_g.reshape(N, HG_WIDTH), mod8,
                                       post_w_mix, w_out_f, T, [w_down_b[:, MLP_HALF:]], ["gather"])
    w_up_halves = [w_up_g0, w_up_g1]
    w_down_halves = [w_down_g0.reshape(D_FF, MLP_HALF), w_down_g1.reshape(D_FF, MLP_HALF)]
    up, u, d, h2 = _mlp_fwd(x1, mod8, pre_w_mlp, w_up_halves, w_down_halves, T)

    dx1, dup, dd, acc_mlp = _mlp_bwd(x1, d, up, tgt2, mod8, pre_w_mlp, post_w_mlp, w_up_halves, w_down_halves, T)
    chips = N_DEV // 2
    by_chip = lambda a: a.reshape((chips, 2, a.shape[0] // N_DEV) + a.shape[1:])
    gw_up = _matmul_tn("grad_w_up", h2, dup, D_FF // N_DEV, by_owner_cols=True)
    gw_up = gw_up.reshape(chips, 2, D_MODEL, D_FF // N_DEV)
    gw_down = by_chip(_matmul_tn("grad_w_down", u, dd, 512))
    dan, drg, dmix, acc_mix, q_down, q_up = _mix_bwd(mix, dx1, mod8, post_w_mix, w_out_f, T,
                                                     [gw_down, gw_up], ["pair"] * 2)
    p_down, p_up = _pair_add("pair_add_w_down", gw_down, q_down), _pair_add("pair_add_w_up", gw_up, q_up)
    gw_out = _matmul_tn("grad_w_out", cat, dmix, 512).reshape(N_DEV, D_MODEL // N_DEV, D_MODEL)
    dhq, dhf, dhi, dhg, dlb_p, dgw_p, r_down, r_up = _hgrn_bwd(
        proj_h, lb, hg_norm_w, rec_o, s_prev, drg.reshape(B, T, HG_WIDTH), [p_down, p_up], ["chips"] * 2)
    dqkv, dsink_p, daw_p, r_out = _attn_bwd(qr, kr, proj3, attn_o, dan.reshape(B, T, ATT_WIDTH), tables,
                                            attn_sinks, attn_out_w, [gw_out], ["a2a"])
    flat = lambda a: a.reshape(N, a.shape[-1])
    grad_x, dproj, acc_in = _in_bwd(x2, dx1, flat(dqkv), flat(dhq), flat(dhf), flat(dhi), flat(dhg),
                                    mod8, pre_w_mix, w_in_f, T, [], [])

    gw_in = by_chip(_matmul_tn("grad_w_in", dproj, h1, 512, tm=IN_COLS // 2))
    (q_in,) = _exchange("pair_w_in", [gw_in], ["pair"])
    p_in = _pair_add("pair_add_w_in", gw_in, q_in)

    dmod = jnp.concatenate([acc_in[:, 0:2], acc_mix[:, 0:1], acc_mlp[:, 0:3]], axis=1)
    dlb = dlb_p[:, 0].sum(0)
    dlb_table = jnp.stack([-dlb, dlb]) * (lb_p[0] * lb_p[1])[None, :]
    small = {
        "b_ada": dmod.sum(0),
        "pre_w_mix": acc_in[:, 2].sum(0),
        "attn_sinks": dsink_p[:, :, 0].sum(0),
        "attn_out_w": daw_p[:, 0].sum(0),
        "lb_table": dlb_table,
        "hg_norm_w": dgw_p[:, 0].sum(0),
        "post_w_mix": acc_mix[:, 1].sum(0),
        "pre_w_mlp": acc_mlp[:, 3].sum(0),
        "post_w_mlp": acc_mlp[:, 4].sum(0),
    }
    loss_part = acc_mlp[:, 5, 0].sum()
    dmod_blocks = dmod.reshape(B, N_DEV, ada_cols).transpose(1, 0, 2)

    r_in, r_dmod, r_small = _exchange(
        "reduce_grads", [p_in, dmod_blocks, _pack_small(small, loss_part)], ["chips", "a2a", "gather"])

    res = {}
    res["w_in"] = tuple(a.T for a in _reduce_adamw("adamw_w_in", r_in, w_in_t, m_w_in_t, v_w_in_t))
    res["w_out"] = _reduce_adamw("adamw_w_out", r_out, w_out[0], m_w_out[0], v_w_out[0])
    res["w_up"] = _reduce_adamw("adamw_w_up", r_up, w_up[0], m_w_up[0], v_w_up[0])
    res["w_down"] = _reduce_adamw("adamw_w_down", r_down, w_down[0], m_w_down[0], v_w_down[0])
    res["w_ada"] = _ada_grad_adamw(c_all, r_dmod.reshape(N_DEV * B, ada_cols), w_ada[0], m_w_ada[0], v_w_ada[0])

    given = dict(b_ada=(b_ada, m_b_ada, v_b_ada), pre_w_mix=(pre_w_mix, m_pre_w_mix, v_pre_w_mix),
                 attn_sinks=(attn_sinks, m_attn_sinks, v_attn_sinks),
                 attn_out_w=(attn_out_w, m_attn_out_w, v_attn_out_w), lb_table=(lb_table, m_lb_table, v_lb_table),
                 hg_norm_w=(hg_norm_w, m_hg_norm_w, v_hg_norm_w), post_w_mix=(post_w_mix, m_post_w_mix, v_post_w_mix),
                 pre_w_mlp=(pre_w_mlp, m_pre_w_mlp, v_pre_w_mlp), post_w_mlp=(post_w_mlp, m_post_w_mlp, v_post_w_mlp))
    small_res, loss = _adamw_small(r_small, given)
    res.update(small_res)

    order = ["w_ada", "b_ada", "pre_w_mix", "w_in", "attn_sinks", "attn_out_w", "lb_table", "hg_norm_w", "w_out",
             "post_w_mix", "pre_w_mlp", "w_up", "w_down", "post_w_mlp"]
    big = {"w_ada", "w_in", "w_out", "w_up", "w_down"}
    outs = [loss, grad_x.reshape(B, T, D_MODEL)]
    for i in range(4):
        for k in order:
            a = res[k][i]
            outs.append(a[None] if k in big else a)
    return tuple(outs)
```

```python
import jax
import jax.numpy as jnp
from jax import lax
from jax.experimental import pallas as pl
from jax.experimental.pallas import tpu as pltpu

F32 = jnp.float32
BF16 = jnp.bfloat16
SDS = jax.ShapeDtypeStruct

D_MODEL = 1024
ATT_WIDTH = 512
ATT_HEAD_DIM = 64
ATT_KV_HEADS = 2
ATT_GROUP = 4
WINDOW = 128
ROPE_DIM = 16
ROPE_THETA = 500000.0
HG_WIDTH = 512
HG_HEAD_DIM = 128
HG_HEADS = 4
HG_CHUNK = 32
IN_COLS = 2816
ATT_COLS = 768
D_FF = 4096
EPS = 1e-6
N_DEV = 8

ADAM_LR = 0.001
ADAM_B1 = 0.9
ADAM_B2 = 0.999
ADAM_EPS = 1e-08
ADAM_WD = 0.01
ADAM_STEP = 10

VMEM_LIMIT_BIG = 56 << 20
LANES = 128

MESH = pl.DeviceIdType.MESH
NT_DIMS = (((1,), (1,)), ((), ()))
TN_DIMS = (((0,), (0,)), ((), ()))


def _dot(a, b):
    return jnp.dot(a, b, preferred_element_type=F32)


def _dot_nt(a, b):
    return lax.dot_general(a, b, NT_DIMS, preferred_element_type=F32)


def _dot_tn(a, b):
    return lax.dot_general(a, b, TN_DIMS, preferred_element_type=F32)


def _bf(a):
    return a.astype(BF16)


def _sigmoid(a):
    return 1.0 / (1.0 + jnp.exp(-a))


def _mean_last(a):
    return jnp.mean(a, axis=-1, keepdims=True)


def _sum_rows(a):
    return jnp.sum(a, axis=0, keepdims=True)


def _tri_sum(tri_bf, a, terms=3):
    a1 = _bf(a)
    r1 = a - a1.astype(F32)
    a2 = _bf(r1)
    out = _dot(tri_bf, a1) + _dot(tri_bf, a2)
    if terms == 3:
        out = out + _dot(tri_bf, _bf(r1 - a2.astype(F32)))
    return out


def _loop_pairs(first, count, body, init, per_trip=2):
    if count % per_trip:
        return lax.fori_loop(first, first + count, body, init)

    def trip(i, c):
        for k in range(per_trip):
            c = body(first + per_trip * i + k, c)
        return c

    return lax.fori_loop(0, count // per_trip, trip, init)


def _params(sem=None, vmem=None):
    kw = {}
    if sem is not None:
        kw["dimension_semantics"] = sem
    if vmem is not None:
        kw["vmem_limit_bytes"] = vmem
    return pltpu.CompilerParams(**kw)


ANY_SPEC = pl.BlockSpec(memory_space=pl.ANY)


def _exchange_shapes(srcs, modes):
    out_shape = []
    for s, m in zip(srcs, modes):
        shp = {"gather": (N_DEV,) + tuple(s.shape), "pair": (s.shape[0],) + tuple(s.shape[2:])}.get(m, tuple(s.shape))
        out_shape.append(SDS(shp, s.dtype))
    return out_shape


def _exchange_sems(n):
    if n == 0:
        return []
    return [pltpu.SemaphoreType.DMA((n, N_DEV - 1)), pltpu.SemaphoreType.DMA((n, N_DEV - 1)),
            pltpu.SemaphoreType.DMA((n,))]


SIBLING = 1
OTHER_CHIPS = (2, 4, 6)


def _related(k):
    x, y, c = lax.axis_index("x"), lax.axis_index("y"), lax.axis_index("c")
    px, py, pc = x ^ ((k >> 2) & 1), y ^ ((k >> 1) & 1), c ^ (k & 1)
    return (px, py, pc), 4 * px + 2 * py + pc


def _exchange_phases(modes, src_refs, out_refs, send_sems, recv_sems, own_sems):
    _, me = _related(0)
    sib_dev, sib = _related(SIBLING)
    start, middle, end = [], [], []

    def remote(a, i, src, dst, dev):
        return pltpu.make_async_remote_copy(src_ref=src, dst_ref=dst, send_sem=send_sems.at[a, i],
                                            recv_sem=recv_sems.at[a, i], device_id=dev, device_id_type=MESH)

    for a, mode in enumerate(modes):
        out = out_refs[a]
        if mode == "gather":
            src = src_refs[a]
            own = pltpu.make_async_copy(src, out.at[me], own_sems.at[a])
            to_sib = remote(a, 0, src, out.at[me], sib_dev)
            start += [own.start, to_sib.start]
            end += [remote(a, 0, src, out.at[sib], sib_dev).wait_recv, to_sib.wait_send, own.wait]
            for j, k in enumerate(OTHER_CHIPS, start=1):
                dev, peer = _related(k)
                _, peer_sib = _related(k ^ SIBLING)
                send = remote(a, j, src, out.at[me], dev)
                passed = remote(a, 3 + j, out.at[peer], out.at[peer], sib_dev)
                start.append(send.start)
                middle += [remote(a, j, src, out.at[peer], dev).wait_recv, passed.start]
                end += [remote(a, 3 + j, out.at[peer_sib], out.at[peer_sib], sib_dev).wait_recv,
                        send.wait_send, passed.wait_send]
        elif mode == "pair":
            core = lax.axis_index("c")
            for s in range(N_DEV // 2):
                send = remote(a, s, src_refs[a].at[s, 1 - core], out.at[s], sib_dev)
                start.append(send.start)
                end += [remote(a, s, src_refs[a].at[s, 1 - core], out.at[s], sib_dev).wait_recv, send.wait_send]
        elif mode == "chips":
            chip = me // 2
            own = pltpu.make_async_copy(src_refs[a].at[chip], out.at[chip], own_sems.at[a])
            start.append(own.start)
            end.append(own.wait)
            for j, k in enumerate(OTHER_CHIPS, start=1):
                dev, peer = _related(k)
                send = remote(a, j, src_refs[a].at[peer // 2], out.at[chip], dev)
                start.append(send.start)
                end += [remote(a, j, src_refs[a].at[peer // 2], out.at[peer // 2], dev).wait_recv, send.wait_send]
        else:
            own = pltpu.make_async_copy(src_refs[a].at[me], out.at[me], own_sems.at[a])
            start.append(own.start)
            end.append(own.wait)
            for k in range(1, N_DEV):
                dev, peer = _related(k)
                send = remote(a, k - 1, src_refs[a].at[peer], out.at[me], dev)
                start.append(send.start)
                end += [remote(a, k - 1, src_refs[a].at[peer], out.at[peer], dev).wait_recv, send.wait_send]
    return start, middle, end


def _run(actions):
    for act in actions:
        act()


def _exchange(name, srcs, modes):
    n = len(srcs)

    def body(*refs):
        start, middle, end = _exchange_phases(modes, refs[:n], refs[n:2 * n], *refs[2 * n:])
        _run(start)
        _run(middle)
        _run(end)

    return pl.pallas_call(
        body, name=name, out_shape=_exchange_shapes(srcs, modes),
        in_specs=[ANY_SPEC] * n, out_specs=[ANY_SPEC] * n,
        scratch_shapes=_exchange_sems(n),
    )(*srcs)


def _ride_start(modes, step, steps, src_refs, out_refs, sems):
    if not modes:
        return
    middle_step = steps - 1

    @pl.when(step == 0)
    def _():
        _run(_exchange_phases(modes, src_refs, out_refs, *sems)[0])

    if "gather" in modes:
        @pl.when(step == middle_step)
        def _():
            _run(_exchange_phases(modes, src_refs, out_refs, *sems)[1])


def _ride_wait(modes, step, steps, src_refs, out_refs, sems):
    if not modes:
        return

    @pl.when(step == steps - 1)
    def _():
        _run(_exchange_phases(modes, src_refs, out_refs, *sems)[2])


def _ada_mod(c_all, w_ada, b_ada_mine):
    nb, cols = c_all.shape[0], w_ada.shape[1]

    def body(c_ref, w_ref, b_ref, o_ref):
        cv = c_ref[...]
        ca = cv * _sigmoid(cv)
        o_ref[...] = _dot(ca, w_ref[...]) + b_ref[...]

    return pl.pallas_call(body, name="ada_mod", out_shape=SDS((nb, cols), F32))(c_all, w_ada, b_ada_mine)


def _tile_rows(T, big=False):
    return min(512 if big else 256, T)


def _mod_spec(tps):
    return pl.BlockSpec((None, 8, D_MODEL), lambda i: (i // tps, 0, 0))


def _in_proj(x2, mod8, pre_w, w_in_bf, T, ride_srcs, ride_modes):
    N = x2.shape[0]
    TM = _tile_rows(T, big=True)
    tps = T // TM
    nr = len(ride_srcs)

    def body(*refs):
        x_ref, mod_ref, pw_ref, w_ref = refs[:4]
        ride_in = refs[4:4 + nr]
        pa_ref, ph_ref, h1_ref = refs[4 + nr:7 + nr]
        ride_out = refs[7 + nr:7 + 2 * nr]
        sems = refs[7 + 2 * nr:]
        _ride_start(ride_modes, pl.program_id(0), N // TM, ride_in, ride_out, sems)
        x = x_ref[...]
        r = lax.rsqrt(_mean_last(x * x) + EPS)
        h = (x * r * pw_ref[...]) * (1.0 + mod_ref[1:2, :]) + mod_ref[0:1, :]
        hb = _bf(h)
        h1_ref[...] = hb
        pa_ref[...] = _dot_nt(hb, w_ref[:ATT_COLS, :])
        ph_ref[...] = _dot_nt(hb, w_ref[ATT_COLS:, :])
        _ride_wait(ride_modes, pl.program_id(0), N // TM, ride_in, ride_out, sems)

    return pl.pallas_call(
        body, name="in_proj", grid=(N // TM,),
        in_specs=[pl.BlockSpec((TM, D_MODEL), lambda i: (i, 0)), _mod_spec(tps),
                  pl.BlockSpec((1, D_MODEL), lambda i: (0, 0)),
                  pl.BlockSpec((IN_COLS, D_MODEL), lambda i: (0, 0))] + [ANY_SPEC] * nr,
        out_specs=[pl.BlockSpec((TM, ATT_COLS), lambda i: (i, 0)),
                   pl.BlockSpec((TM, IN_COLS - ATT_COLS), lambda i: (i, 0)),
                   pl.BlockSpec((TM, D_MODEL), lambda i: (i, 0))] + [ANY_SPEC] * nr,
        out_shape=[SDS((N, ATT_COLS), F32), SDS((N, IN_COLS - ATT_COLS), F32), SDS((N, D_MODEL), BF16)]
        + _exchange_shapes(ride_srcs, ride_modes),
        scratch_shapes=_exchange_sems(nr),
        compiler_params=_params(("arbitrary",), VMEM_LIMIT_BIG),
    )(x2, mod8, pre_w, w_in_bf, *ride_srcs)


def _rope_tables(T):
    half = ROPE_DIM // 2
    inv_freq = ROPE_THETA ** (-jnp.arange(0, ROPE_DIM, 2, dtype=F32) / ROPE_DIM)
    ang = jnp.arange(T, dtype=F32)[:, None] * inv_freq[None, :]
    cos, sin = jnp.cos(ang), jnp.sin(ang)
    ones = jnp.ones((T, ATT_HEAD_DIM - ROPE_DIM), F32)
    zeros = jnp.zeros((T, ATT_HEAD_DIM - ROPE_DIM), F32)
    zh = jnp.zeros((T, half), F32)
    cos64 = jnp.concatenate([cos, cos, ones], axis=1)
    sin_left = jnp.concatenate([-sin, zh, zeros], axis=1)
    sin_right = jnp.concatenate([zh, sin, zeros], axis=1)
    rep = LANES // ATT_HEAD_DIM
    return jnp.tile(cos64, (1, rep)), jnp.tile(sin_left, (1, rep)), jnp.tile(sin_right, (1, rep))


def _rope(xc, cs, sl, sr):
    return xc * cs + pltpu.roll(xc, LANES - 8, 1) * sl + pltpu.roll(xc, 8, 1) * sr


def _rope_t(dy, cs, sl, sr):
    return dy * cs + pltpu.roll(dy * sl, 8, 1) + pltpu.roll(dy * sr, LANES - 8, 1)


ATT_SCALE = ATT_HEAD_DIM ** -0.5
ATT_SPLITS = 4


def _lower_mask():
    j = lax.broadcasted_iota(jnp.int32, (WINDOW, ATT_GROUP * WINDOW), 0)
    i = lax.broadcasted_iota(jnp.int32, (WINDOW, ATT_GROUP * WINDOW), 1) & (WINDOW - 1)
    return j <= i


def _sink_row(sink_ref, hk):
    return jnp.concatenate(
        [jnp.full((1, WINDOW), sink_ref[0, ATT_GROUP * hk + g], F32) for g in range(ATT_GROUP)], axis=1)


def _softmax_window(qs, k_cur, k_prev, lower, has_prev, sink):
    s_prev = jnp.where(has_prev, _dot_nt(k_prev, qs), jnp.finfo(F32).min)
    s = jnp.where(lower, _dot_nt(k_cur, qs), s_prev)
    m = jnp.maximum(jnp.max(s, axis=0, keepdims=True), sink)
    p = jnp.exp(s - m)
    es = jnp.exp(sink - m)
    inv = 1.0 / (jnp.sum(p, axis=0, keepdims=True) + es)
    return p, inv, es


def _stack_heads(parts, hk):
    hs = []
    for g in range(ATT_GROUP):
        h = ATT_GROUP * hk + g
        hs.append(parts[h // 2][:, (h % 2) * ATT_HEAD_DIM:(h % 2 + 1) * ATT_HEAD_DIM])
    return jnp.concatenate(hs, axis=0)


def _attn_fwd(proj3, tables, sinks, attn_w, ride_srcs, ride_modes):
    B, T, _ = proj3.shape
    nb = T // WINDOW
    splits = min(ATT_SPLITS, nb)
    per = nb // splits
    nr = len(ride_srcs)
    cos, sinl, sinr = tables

    def body(*refs):
        q_ref, k_ref, v_ref, cos_ref, sl_ref, sr_ref, sink_ref, aw_ref = refs[:8]
        ride_in = refs[8:8 + nr]
        o_ref, an_ref, qr_ref, kr_ref = refs[8 + nr:12 + nr]
        ride_out = refs[12 + nr:12 + 2 * nr]
        kpad, vpad = refs[12 + 2 * nr:14 + 2 * nr]
        sems = refs[14 + 2 * nr:]
        part = pl.program_id(1)
        step = pl.program_id(0) * splits + part
        _ride_start(ride_modes, step, B * splits, ride_in, ride_out, sems)

        @pl.when(part == 0)
        def _():
            kpad[0:WINDOW, :] = jnp.zeros((WINDOW, LANES), BF16)
            vpad[0:WINDOW, :] = jnp.zeros((WINDOW, LANES), BF16)

        lower = _lower_mask()

        def block(n, carry):
            r0 = pl.multiple_of(n * WINDOW, WINDOW)
            rows = pl.ds(r0, WINDOW)
            nxt = pl.ds(r0 + WINDOW, WINDOW)
            cs, sl, sr = cos_ref[rows, :], sl_ref[rows, :], sr_ref[rows, :]
            kb = _bf(_rope(k_ref[rows, :], cs, sl, sr))
            vb = _bf(v_ref[rows, :])
            kpad[nxt, :] = kb
            kr_ref[rows, :] = kb
            vpad[nxt, :] = vb
            qparts = []
            for j in range(ATT_WIDTH // LANES):
                qp = _bf(_rope(q_ref[rows, j * LANES:(j + 1) * LANES], cs, sl, sr) * ATT_SCALE)
                qr_ref[rows, j * LANES:(j + 1) * LANES] = qp
                qparts.append(qp)
            for hk in range(ATT_KV_HEADS):
                lanes = slice(hk * ATT_HEAD_DIM, (hk + 1) * ATT_HEAD_DIM)
                qs = _stack_heads(qparts, hk)
                p, inv, _ = _softmax_window(qs, kb[:, lanes], kpad[rows, lanes], lower, n > 0,
                                            _sink_row(sink_ref, hk))
                p_cur = jnp.where(lower, p, 0.0)
                ot = (_dot_tn(vb[:, lanes], _bf(p_cur)) + _dot_tn(vpad[rows, lanes], _bf(p - p_cur))) * inv
                for g in range(ATT_GROUP):
                    h = ATT_GROUP * hk + g
                    o_ref[rows, h * ATT_HEAD_DIM:(h + 1) * ATT_HEAD_DIM] = ot[:, g * WINDOW:(g + 1) * WINDOW].T
            ob = o_ref[rows, :]
            an_ref[rows, :] = _bf(ob * lax.rsqrt(_mean_last(ob * ob) + EPS) * aw_ref[...])
            return carry

        _loop_pairs(part * per, per, block, 0)
        _ride_wait(ride_modes, step, B * splits, ride_in, ride_out, sems)

    seq = lambda w, j: pl.BlockSpec((None, T, w), lambda b, s: (b, 0, j))
    full = lambda r, w: pl.BlockSpec((r, w), lambda b, s: (0, 0))
    return pl.pallas_call(
        body, name="attn_fwd", grid=(B, splits),
        in_specs=[seq(ATT_WIDTH, 0), seq(LANES, 4), seq(LANES, 5),
                  full(T, LANES), full(T, LANES), full(T, LANES),
                  pl.BlockSpec(memory_space=pltpu.SMEM), full(1, ATT_WIDTH)] + [ANY_SPEC] * nr,
        out_specs=[seq(ATT_WIDTH, 0), seq(ATT_WIDTH, 0), seq(ATT_WIDTH, 0), seq(LANES, 0)] + [ANY_SPEC] * nr,
        out_shape=[SDS((B, T, ATT_WIDTH), F32), SDS((B, T, ATT_WIDTH), BF16),
                   SDS((B, T, ATT_WIDTH), BF16), SDS((B, T, LANES), BF16)] + _exchange_shapes(ride_srcs, ride_modes),
        scratch_shapes=[pltpu.VMEM((T + WINDOW, LANES), BF16), pltpu.VMEM((T + WINDOW, LANES), BF16)]
        + _exchange_sems(nr),
        compiler_params=_params(("arbitrary", "arbitrary"), VMEM_LIMIT_BIG),
    )(proj3, proj3, proj3, cos, sinl, sinr, sinks, attn_w, *ride_srcs)


HG_GROUP = 8
HG_ROWS = HG_GROUP * HG_CHUNK


HG_STACK = HG_GROUP * HG_HEAD_DIM


def _group_masks():
    r = lax.broadcasted_iota(jnp.int32, (HG_ROWS, HG_ROWS), 0)
    c = lax.broadcasted_iota(jnp.int32, (HG_ROWS, HG_ROWS), 1)
    same = (r // HG_CHUNK) == (c // HG_CHUNK)
    return same & (r >= c), same & (c >= r)


def _row_chunk():
    return lax.broadcasted_iota(jnp.int32, (HG_ROWS, HG_HEAD_DIM), 0) // HG_CHUNK


def _spread(a, row_chunk):
    return jnp.concatenate([jnp.where(row_chunk == c, a, jnp.zeros_like(a)) for c in range(HG_GROUP)], axis=1)


def _pick(r, row_chunk):
    out = jnp.where(row_chunk == 0, r[:, :HG_HEAD_DIM], 0.0)
    for c in range(1, HG_GROUP):
        out = out + jnp.where(row_chunk == c, r[:, c * HG_HEAD_DIM:(c + 1) * HG_HEAD_DIM], 0.0)
    return out


def _lane_block(a, c):
    return a[:, c * HG_HEAD_DIM:(c + 1) * HG_HEAD_DIM]


def _ones_bf(mask):
    return jnp.where(mask, 1.0, 0.0).astype(BF16)


def _chunk_bcast(rows_1x128):
    return jnp.concatenate([jnp.broadcast_to(r, (HG_CHUNK, HG_HEAD_DIM)) for r in rows_1x128], axis=0)


def _hgrn_gates(hq, hf, lb, lower_bf):
    sq = _sigmoid(hq)
    q = hq * sq
    sg = _sigmoid(hf)
    f = lb + (1.0 - lb) * sg
    k = 1.0 - f
    logf = jnp.log(f)
    b = _tri_sum(lower_bf, logf)
    bl = [_sum_rows(logf[_chunk_rows(c), :]) for c in range(HG_GROUP)]
    eb, enb, e2 = jnp.exp(b), jnp.exp(-b), jnp.exp(_chunk_bcast(bl) - b)
    ebl = [jnp.exp(r) for r in bl]
    return dict(sq=sq, sg=sg, f=f, eb=eb, enb=enb, e2=e2, ebl=ebl, qd=q * eb, kd=k * enb, k2=k * e2)


def _chunk_rows(c):
    return slice(c * HG_CHUNK, (c + 1) * HG_CHUNK)


def _head_lanes(h):
    return slice(h * HG_HEAD_DIM, (h + 1) * HG_HEAD_DIM)


def _hgrn_fwd(proj_h, lb, hg_w, ride_srcs, ride_modes):
    B, T, _ = proj_h.shape
    ng = T // HG_ROWS
    nr = len(ride_srcs)

    def body(*refs):
        hq_ref, hf_ref, hi_ref, hg_ref, lb_ref, gw_ref = refs[:6]
        ride_in = refs[6:6 + nr]
        o_ref, rg_ref, sp_ref = refs[6 + nr:9 + nr]
        ride_out = refs[9 + nr:9 + 2 * nr]
        st = refs[9 + 2 * nr]
        sems = refs[10 + 2 * nr:]
        gi = pl.program_id(1)
        step = pl.program_id(0) * ng + gi
        _ride_start(ride_modes, step, B * ng, ride_in, ride_out, sems)

        @pl.when(gi == 0)
        def _():
            st[...] = jnp.zeros(st.shape, F32)

        lo, _ = _group_masks()
        lower_bf = _ones_bf(lo)
        row_chunk = _row_chunk()
        for h in range(HG_HEADS):
            lanes = _head_lanes(h)
            gt = _hgrn_gates(hq_ref[:, lanes], hf_ref[:, lanes], lb_ref[:, lanes], lower_bf)
            v, qd, kd = _bf(hi_ref[:, lanes]), _bf(gt["qd"]), _bf(gt["kd"])
            a = jnp.where(lo, _dot_nt(qd, kd), 0.0)
            kv = _dot_tn(v, _bf(_spread(gt["k2"], row_chunk)))
            s = st[h]
            before = []
            for c in range(HG_GROUP):
                before.append(s)
                s = s * gt["ebl"][c] + _lane_block(kv, c)
            st[h] = s
            sp = jnp.concatenate(before, axis=1)
            sp_ref[h] = sp
            o = _dot(_bf(a), v) + _dot_nt(_bf(_spread(gt["qd"], row_chunk)), _bf(sp))
            o_ref[:, lanes] = o
            hg = hg_ref[:, lanes]
            rn = o * lax.rsqrt(_mean_last(o * o) + EPS) * gw_ref[...]
            rg_ref[:, lanes] = _bf(rn * (hg * _sigmoid(hg)))
        _ride_wait(ride_modes, step, B * ng, ride_in, ride_out, sems)

    part = lambda j: pl.BlockSpec((None, HG_ROWS, HG_WIDTH), lambda b, g: (b, g, j))
    return pl.pallas_call(
        body, name="hgrn_fwd", grid=(B, ng),
        in_specs=[part(0), part(1), part(2), part(3),
                  pl.BlockSpec((1, HG_WIDTH), lambda b, g: (0, 0)),
                  pl.BlockSpec((1, LANES), lambda b, g: (0, 0))] + [ANY_SPEC] * nr,
        out_specs=[part(0), part(0),
                   pl.BlockSpec((None, HG_HEADS, None, HG_HEAD_DIM, HG_STACK), lambda b, g: (b, 0, g, 0, 0))]
        + [ANY_SPEC] * nr,
        out_shape=[SDS((B, T, HG_WIDTH), F32), SDS((B, T, HG_WIDTH), BF16),
                   SDS((B, HG_HEADS, ng, HG_HEAD_DIM, HG_STACK), F32)] + _exchange_shapes(ride_srcs, ride_modes),
        scratch_shapes=[pltpu.VMEM((HG_HEADS, HG_HEAD_DIM, HG_HEAD_DIM), F32)] + _exchange_sems(nr),
        compiler_params=_params(("arbitrary", "arbitrary"), VMEM_LIMIT_BIG),
    )(proj_h, proj_h, proj_h, proj_h, lb, hg_w, *ride_srcs)


def _mix_out(x2, attn_n, rec_g, mod8, post_w, w_out_bf, T, ride_srcs, ride_modes):
    N = x2.shape[0]
    TM = _tile_rows(T, big=True)
    tps = T // TM
    nr = len(ride_srcs)

    def body(*refs):
        x_ref, an_ref, rg_ref, mod_ref, pw_ref, w_ref = refs[:6]
        ride_in = refs[6:6 + nr]
        mix_ref, x1_ref, cat_ref = refs[6 + nr:9 + nr]
        ride_out = refs[9 + nr:9 + 2 * nr]
        sems = refs[9 + 2 * nr:]
        _ride_start(ride_modes, pl.program_id(0), N // TM, ride_in, ride_out, sems)
        cat = jnp.concatenate([an_ref[...], rg_ref[...]], axis=1)
        cat_ref[...] = cat
        mix = _dot(cat, w_ref[...])
        mix_ref[...] = mix
        r = lax.rsqrt(_mean_last(mix * mix) + EPS)
        x1_ref[...] = x_ref[...] + mod_ref[2:3, :] * (mix * r * pw_ref[...])
        _ride_wait(ride_modes, pl.program_id(0), N // TM, ride_in, ride_out, sems)

    row = lambda w: pl.BlockSpec((TM, w), lambda i: (i, 0))
    return pl.pallas_call(
        body, name="mix_out", grid=(N // TM,),
        in_specs=[row(D_MODEL), row(ATT_WIDTH), row(HG_WIDTH), _mod_spec(tps),
                  pl.BlockSpec((1, D_MODEL), lambda i: (0, 0)),
                  pl.BlockSpec((D_MODEL, D_MODEL), lambda i: (0, 0))] + [ANY_SPEC] * nr,
        out_specs=[row(D_MODEL), row(D_MODEL), row(D_MODEL)] + [ANY_SPEC] * nr,
        out_shape=[SDS((N, D_MODEL), F32), SDS((N, D_MODEL), F32), SDS((N, D_MODEL), BF16)]
        + _exchange_shapes(ride_srcs, ride_modes),
        scratch_shapes=_exchange_sems(nr),
        compiler_params=_params(("arbitrary",), VMEM_LIMIT_BIG),
    )(x2, attn_n, rec_g, mod8, post_w, w_out_bf, *ride_srcs)


def _load_weights_once(pairs, sem):
    @pl.when(pl.program_id(0) == 0)
    def _():
        cps = [pltpu.make_async_copy(src, dst, sem.at[i]) for i, (src, dst) in enumerate(pairs)]
        for cp in cps:
            cp.start()
        for cp in cps:
            cp.wait()


MLP_HALF = D_MODEL // 2
MLP_PIECES = 2 * N_DEV + 2


def _mlp_weight_pieces(wu_a, wu_b, wd_a, wd_b, wu, wd):
    cols = D_FF // N_DEV
    pairs = []
    for h, half in enumerate((wu_a, wu_b)):
        for j in range(N_DEV):
            pairs.append((half.at[j], wu.at[pl.ds(h * MLP_HALF, MLP_HALF), pl.ds(j * cols, cols)]))
    for h, half in enumerate((wd_a, wd_b)):
        pairs.append((half, wd.at[:, pl.ds(h * MLP_HALF, MLP_HALF)]))
    return pairs


def _mlp_fwd(x1, mod8, pre_w, w_up_halves, w_down_halves, T):
    N = x1.shape[0]
    TM = _tile_rows(T)
    tps = T // TM

    def body(x_ref, mod_ref, pw_ref, wua, wub, wda, wdb, up_ref, u_ref, d_ref, h2_ref, wu, wd, sem):
        _load_weights_once(_mlp_weight_pieces(wua, wub, wda, wdb, wu, wd), sem)
        x = x_ref[...]
        r = lax.rsqrt(_mean_last(x * x) + EPS)
        h = (x * r * pw_ref[...]) * (1.0 + mod_ref[4:5, :]) + mod_ref[3:4, :]
        hb = _bf(h)
        h2_ref[...] = hb
        up = _dot(hb, wu[...])
        up_ref[...] = up
        ru = jnp.maximum(up, 0.0)
        u = _bf(ru * ru)
        u_ref[...] = u
        d_ref[...] = _dot(u, wd[...])

    row = lambda w: pl.BlockSpec((TM, w), lambda i: (i, 0))
    return pl.pallas_call(
        body, name="mlp_fwd", grid=(N // TM,),
        in_specs=[row(D_MODEL), _mod_spec(tps), pl.BlockSpec((1, D_MODEL), lambda i: (0, 0))] + [ANY_SPEC] * 4,
        out_specs=[row(D_FF), row(D_FF), row(D_MODEL), row(D_MODEL)],
        out_shape=[SDS((N, D_FF), F32), SDS((N, D_FF), BF16), SDS((N, D_MODEL), F32), SDS((N, D_MODEL), BF16)],
        scratch_shapes=[pltpu.VMEM((D_MODEL, D_FF), BF16), pltpu.VMEM((D_FF, D_MODEL), BF16),
                        pltpu.SemaphoreType.DMA((MLP_PIECES,))],
        compiler_params=_params(("arbitrary",), VMEM_LIMIT_BIG),
    )(x1, mod8, pre_w, *w_up_halves, *w_down_halves)


def _acc_rows(acc_ref, first, rows):
    @pl.when(first)
    def _():
        acc_ref[...] = jnp.zeros(acc_ref.shape, F32)
    for i, r in enumerate(rows):
        acc_ref[i:i + 1, :] += r


def _mlp_bwd(x1, d, up, tgt, mod8, pre_w, post_w, w_up_halves, w_down_halves, T):
    N = x1.shape[0]
    TM = _tile_rows(T)
    tps = T // TM

    def body(x_ref, d_ref, up_ref, t_ref, mod_ref, pw_ref, qw_ref, wua, wub, wda, wdb,
             dx_ref, dup_ref, dd_ref, acc_ref, wd, wu, sem):
        _load_weights_once(_mlp_weight_pieces(wua, wub, wda, wdb, wu, wd), sem)
        sh2, sc2, g2 = mod_ref[3:4, :], mod_ref[4:5, :], mod_ref[5:6, :]
        x = x_ref[...]
        r1 = lax.rsqrt(_mean_last(x * x) + EPS)
        xh = x * r1
        n2 = xh * pw_ref[...]
        dv = d_ref[...]
        rd = lax.rsqrt(_mean_last(dv * dv) + EPS)
        dh = dv * rd
        rr = dh * qw_ref[...]
        e = x + g2 * rr - t_ref[...]
        loss = 0.5 * jnp.sum(_sum_rows(e * e), axis=1, keepdims=True) / D_MODEL
        dy = e * (1.0 / D_MODEL)
        dg2 = _sum_rows(dy * rr)
        drr = dy * g2
        dw_post = _sum_rows(drr * dh)
        ddh = drr * qw_ref[...]
        dd = _bf(rd * (ddh - dh * _mean_last(ddh * dh)))
        dd_ref[...] = dd
        ru = jnp.maximum(up_ref[...], 0.0)
        dup = _bf(_dot_nt(dd, wd[...]) * (2.0 * ru))
        dup_ref[...] = dup
        dh2 = _dot_nt(dup, wu[...])
        dsh2 = _sum_rows(dh2)
        dsc2 = _sum_rows(dh2 * n2)
        dn2 = dh2 * (1.0 + sc2)
        dw_pre = _sum_rows(dn2 * xh)
        dxh = dn2 * pw_ref[...]
        dx_ref[...] = dy + r1 * (dxh - xh * _mean_last(dxh * xh))
        _acc_rows(acc_ref, pl.program_id(0) % tps == 0,
                  [dsh2, dsc2, dg2, dw_pre, dw_post, jnp.broadcast_to(loss, (1, D_MODEL))])

    row = lambda w: pl.BlockSpec((TM, w), lambda i: (i, 0))
    vec = pl.BlockSpec((1, D_MODEL), lambda i: (0, 0))
    B = N // T
    return pl.pallas_call(
        body, name="mlp_bwd", grid=(N // TM,),
        in_specs=[row(D_MODEL), row(D_MODEL), row(D_FF), row(D_MODEL), _mod_spec(tps), vec, vec] + [ANY_SPEC] * 4,
        out_specs=[row(D_MODEL), row(D_FF), row(D_MODEL), _mod_spec(tps)],
        out_shape=[SDS((N, D_MODEL), F32), SDS((N, D_FF), BF16), SDS((N, D_MODEL), BF16),
                   SDS((B, 8, D_MODEL), F32)],
        scratch_shapes=[pltpu.VMEM((D_FF, D_MODEL), BF16), pltpu.VMEM((D_MODEL, D_FF), BF16),
                        pltpu.SemaphoreType.DMA((MLP_PIECES,))],
        compiler_params=_params(("arbitrary",), VMEM_LIMIT_BIG),
    )(x1, d, up, tgt, mod8, pre_w, post_w, *w_up_halves, *w_down_halves)


def _mix_bwd(mix, dx1, mod8, post_w, w_out_bf, T, ride_srcs, ride_modes):
    N = mix.shape[0]
    TM = _tile_rows(T, big=True)
    tps = T // TM
    nr = len(ride_srcs)

    def body(*refs):
        mix_ref, dx_ref, mod_ref, pw_ref, w_ref = refs[:5]
        ride_in = refs[5:5 + nr]
        dan_ref, drg_ref, dmix_ref, acc_ref = refs[5 + nr:9 + nr]
        ride_out = refs[9 + nr:9 + 2 * nr]
        sems = refs[9 + 2 * nr:]
        _ride_start(ride_modes, pl.program_id(0), N // TM, ride_in, ride_out, sems)
        g1 = mod_ref[2:3, :]
        mix = mix_ref[...]
        dx1 = dx_ref[...]
        rm = lax.rsqrt(_mean_last(mix * mix) + EPS)
        mh = mix * rm
        dg1 = _sum_rows(dx1 * (mh * pw_ref[...]))
        dr = dx1 * g1
        dw_post = _sum_rows(dr * mh)
        dmh = dr * pw_ref[...]
        dmix = _bf(rm * (dmh - mh * _mean_last(dmh * mh)))
        dmix_ref[...] = dmix
        dcat = _dot_nt(dmix, w_ref[...])
        dan_ref[...] = dcat[:, :ATT_WIDTH]
        drg_ref[...] = dcat[:, ATT_WIDTH:]
        _acc_rows(acc_ref, pl.program_id(0) % tps == 0, [dg1, dw_post])
        _ride_wait(ride_modes, pl.program_id(0), N // TM, ride_in, ride_out, sems)

    row = lambda w: pl.BlockSpec((TM, w), lambda i: (i, 0))
    B = N // T
    return pl.pallas_call(
        body, name="mix_bwd", grid=(N // TM,),
        in_specs=[row(D_MODEL), row(D_MODEL), _mod_spec(tps), pl.BlockSpec((1, D_MODEL), lambda i: (0, 0)),
                  pl.BlockSpec((D_MODEL, D_MODEL), lambda i: (0, 0))] + [ANY_SPEC] * nr,
        out_specs=[row(ATT_WIDTH), row(HG_WIDTH), row(D_MODEL), _mod_spec(tps)] + [ANY_SPEC] * nr,
        out_shape=[SDS((N, ATT_WIDTH), F32), SDS((N, HG_WIDTH), F32), SDS((N, D_MODEL), BF16),
                   SDS((B, 8, D_MODEL), F32)] + _exchange_shapes(ride_srcs, ride_modes),
        scratch_shapes=_exchange_sems(nr),
        compiler_params=_params(("arbitrary",), VMEM_LIMIT_BIG),
    )(mix, dx1, mod8, post_w, w_out_bf, *ride_srcs)


def _side_tiles(M, Nc, steps):
    tn = min(512, Nc)
    nj = Nc // tn
    tm = M // (steps // nj)
    assert tm % LANES == 0 and tm * (steps // nj) == M and tn * nj == Nc
    return tm, tn, nj


def _hgrn_bwd(proj_h, lb, hg_w, o, s_prev, drg, side, ride_srcs, ride_modes):
    B, T, _ = proj_h.shape
    ng = T // HG_ROWS
    nr = len(ride_srcs)
    steps = B * ng
    tiles = [_side_tiles(a.shape[1], b.shape[1], steps) for a, b in side]

    def body(*refs):
        hq_ref, hf_ref, hi_ref, hg_ref, lb_ref, gw_ref, o_ref, sp_ref, drg_ref = refs[:9]
        side_in = refs[9:13]
        ride_in = refs[13:13 + nr]
        dhq_ref, dhf_ref, dhi_ref, dhg_ref, dlb_ref, dgw_ref = refs[13 + nr:19 + nr]
        side_out = refs[19 + nr:21 + nr]
        ride_out = refs[21 + nr:21 + 2 * nr]
        dst = refs[21 + 2 * nr]
        sems = refs[22 + 2 * nr:]
        step = pl.program_id(0) * ng + pl.program_id(1)
        _ride_start(ride_modes, step, B * ng, ride_in, ride_out, sems)

        side_acc = [None, None]

        def side_chunk(job, k, chunks):
            a_ref, b_ref = side_in[2 * job], side_in[2 * job + 1]
            rows = a_ref.shape[0] // chunks
            part = _dot_tn(a_ref[k * rows:(k + 1) * rows, :], b_ref[k * rows:(k + 1) * rows, :])
            side_acc[job] = part if side_acc[job] is None else side_acc[job] + part

        @pl.when(pl.program_id(1) == 0)
        def _():
            dst[...] = jnp.zeros(dst.shape, F32)
            dlb_ref[...] = jnp.zeros(dlb_ref.shape, F32)
            dgw_ref[...] = jnp.zeros(dgw_ref.shape, F32)

        lo, up = _group_masks()
        lower_bf, upper_bf = _ones_bf(lo), _ones_bf(up)
        row_chunk = _row_chunk()
        gw = gw_ref[...]

        for h in range(HG_HEADS):
            lanes = _head_lanes(h)
            lbv = lb_ref[:, lanes]
            hq = hq_ref[:, lanes]
            side_chunk(0, 2 * h, 2 * HG_HEADS)
            gt = _hgrn_gates(hq, hf_ref[:, lanes], lbv, lower_bf)
            sq, sg, qdf, kdf, k2f, ebl = gt["sq"], gt["sg"], gt["qd"], gt["kd"], gt["k2"], gt["ebl"]
            v, qd, kd = _bf(hi_ref[:, lanes]), _bf(qdf), _bf(kdf)
            side_chunk(0, 2 * h + 1, 2 * HG_HEADS)
            ov = o_ref[:, lanes]
            hg = hg_ref[:, lanes]
            shg = _sigmoid(hg)
            dr = drg_ref[:, lanes]
            ro = lax.rsqrt(_mean_last(ov * ov) + EPS)
            oh = ov * ro
            dhg_ref[:, lanes] = _bf(dr * (oh * gw) * (shg + hg * shg * (1.0 - shg)))
            drn = dr * (hg * shg)
            dgw_ref[...] += jnp.broadcast_to(_sum_rows(drn * oh), (8, LANES))
            doh = drn * gw
            do = _bf(ro * (doh - oh * _mean_last(doh * oh)))
            a = jnp.where(lo, _dot_nt(qd, kd), 0.0)
            da = _bf(jnp.where(lo, _dot_nt(do, v), 0.0))
            dv = _dot_tn(_bf(a), do)
            dqd = _dot(da, kd)
            dkd = _dot_tn(da, qd)
            sp = sp_ref[h]
            incr = _dot_tn(do, _bf(_spread(qdf, row_chunk)))
            ds = dst[h]
            after = [None] * HG_GROUP
            for c in reversed(range(HG_GROUP)):
                after[c] = ds
                ds = ds * ebl[c] + _lane_block(incr, c)
            dst[h] = ds
            dss = jnp.concatenate(after, axis=1)
            dssb = _bf(dss)
            dk2 = _pick(_dot(v, dssb), row_chunk)
            dhi_ref[:, lanes] = _bf(dv + _dot_nt(_bf(_spread(k2f, row_chunk)), dssb))
            dqd = dqd + _pick(_dot(do, _bf(sp)), row_chunk)
            debl = _sum_rows(dss * sp)
            k2g = dk2 * k2f
            db = dqd * qdf - dkd * kdf - k2g
            dk = dkd * gt["enb"] + dk2 * gt["e2"]
            dbl = _chunk_bcast([_lane_block(debl, c) * ebl[c] + _sum_rows(k2g[_chunk_rows(c), :])
                                for c in range(HG_GROUP)])
            dg = _tri_sum(upper_bf, db, terms=2) + dbl
            side_chunk(1, h, HG_HEADS)
            df = dg / gt["f"] - dk
            dhf_ref[:, lanes] = _bf(df * (1.0 - lbv) * sg * (1.0 - sg))
            dlb_ref[:, lanes] += jnp.broadcast_to(_sum_rows(df * (1.0 - sg)), (8, LANES))
            dhq_ref[:, lanes] = _bf((dqd * gt["eb"]) * (sq + hq * sq * (1.0 - sq)))
        side_out[0][...] = _bf(side_acc[0])
        side_out[1][...] = _bf(side_acc[1])
        _ride_wait(ride_modes, step, B * ng, ride_in, ride_out, sems)

    part = lambda j: pl.BlockSpec((None, HG_ROWS, HG_WIDTH), lambda b, g: (b, ng - 1 - g, j))
    side_specs, side_outs, side_shapes = [], [], []
    for (a, bmat), (tm, tn, nj) in zip(side, tiles):
        K = a.shape[0]
        side_specs += [pl.BlockSpec((K, tm), lambda b, g, nj=nj: (0, (b * ng + g) // nj)),
                       pl.BlockSpec((K, tn), lambda b, g, nj=nj: (0, (b * ng + g) % nj))]
        side_outs.append(pl.BlockSpec((tm, tn), lambda b, g, nj=nj: ((b * ng + g) // nj, (b * ng + g) % nj)))
        side_shapes.append(SDS((a.shape[1], bmat.shape[1]), BF16))
    return pl.pallas_call(
        body, name="hgrn_bwd", grid=(B, ng),
        in_specs=[part(0), part(1), part(2), part(3),
                  pl.BlockSpec((1, HG_WIDTH), lambda b, g: (0, 0)),
                  pl.BlockSpec((1, LANES), lambda b, g: (0, 0)),
                  part(0),
                  pl.BlockSpec((None, HG_HEADS, None, HG_HEAD_DIM, HG_STACK), lambda b, g: (b, 0, ng - 1 - g, 0, 0)),
                  part(0)] + side_specs + [ANY_SPEC] * nr,
        out_specs=[part(0), part(0), part(0), part(0),
                   pl.BlockSpec((None, 8, HG_WIDTH), lambda b, g: (b, 0, 0)),
                   pl.BlockSpec((None, 8, LANES), lambda b, g: (b, 0, 0))] + side_outs + [ANY_SPEC] * nr,
        out_shape=[SDS((B, T, HG_WIDTH), BF16)] * 4 + [SDS((B, 8, HG_WIDTH), F32), SDS((B, 8, LANES), F32)]
        + side_shapes + _exchange_shapes(ride_srcs, ride_modes),
        scratch_shapes=[pltpu.VMEM((HG_HEADS, HG_HEAD_DIM, HG_HEAD_DIM), F32)] + _exchange_sems(nr),
        compiler_params=_params(("arbitrary", "arbitrary"), VMEM_LIMIT_BIG),
    )(proj_h, proj_h, proj_h, proj_h, lb, hg_w, o, s_prev, drg, side[0][0], side[0][1], side[1][0], side[1][1],
      *ride_srcs)


def _attn_bwd(qr, kr, proj3, attn_o, dan, tables, sinks, attn_w, ride_srcs, ride_modes):
    B, T, _ = proj3.shape
    nb = T // WINDOW
    splits = min(ATT_SPLITS, nb)
    per = nb // splits
    nr = len(ride_srcs)
    cos, sinl, sinr = tables
    QKV = ATT_WIDTH + 2 * LANES

    def body(*refs):
        qr_ref, kr_ref, v_ref, o_ref, dan_ref, cos_ref, sl_ref, sr_ref, sink_ref, aw_ref = refs[:10]
        ride_in = refs[10:10 + nr]
        dqkv_ref, dsink_ref, daw_ref = refs[10 + nr:13 + nr]
        ride_out = refs[13 + nr:13 + 2 * nr]
        kpad, vpad, dkpad, dvpad, dqb, dsk = refs[13 + 2 * nr:19 + 2 * nr]
        sems = refs[19 + 2 * nr:]
        part = pl.program_id(1)
        step = pl.program_id(0) * splits + part
        _ride_start(ride_modes, step, B * splits, ride_in, ride_out, sems)

        @pl.when(part == 0)
        def _():
            kpad[0:WINDOW, :] = jnp.zeros((WINDOW, LANES), BF16)
            vpad[0:WINDOW, :] = jnp.zeros((WINDOW, LANES), BF16)
            kpad[WINDOW:, :] = kr_ref[...]
            vpad[WINDOW:, :] = _bf(v_ref[...])
            dkpad[...] = jnp.zeros(dkpad.shape, F32)
            dvpad[...] = jnp.zeros(dvpad.shape, F32)
            dsk[...] = jnp.zeros(dsk.shape, F32)
            daw_ref[...] = jnp.zeros(daw_ref.shape, F32)

        lower = _lower_mask()
        aw = aw_ref[...]

        def block(n, daw):
            r0 = pl.multiple_of(n * WINDOW, WINDOW)
            rows = pl.ds(r0, WINDOW)
            nxt = pl.ds(r0 + WINDOW, WINDOW)
            ob = o_ref[rows, :]
            dn = dan_ref[rows, :]
            ro = lax.rsqrt(_mean_last(ob * ob) + EPS)
            oh = ob * ro
            daw = daw + _sum_rows(dn * oh)
            doh = dn * aw
            do = _bf(ro * (doh - oh * _mean_last(doh * oh)))
            doparts = [do[:, j * LANES:(j + 1) * LANES] for j in range(ATT_WIDTH // LANES)]
            qparts = [qr_ref[rows, j * LANES:(j + 1) * LANES] for j in range(ATT_WIDTH // LANES)]
            for hk in range(ATT_KV_HEADS):
                lanes = slice(hk * ATT_HEAD_DIM, (hk + 1) * ATT_HEAD_DIM)
                qs = _stack_heads(qparts, hk)
                dos = _stack_heads(doparts, hk)
                k_cur, k_prev = kpad[nxt, lanes], kpad[rows, lanes]
                v_cur, v_prev = vpad[nxt, lanes], vpad[rows, lanes]
                p, inv, es = _softmax_window(qs, k_cur, k_prev, lower, n > 0, _sink_row(sink_ref, hk))
                p = p * inv
                dp = jnp.where(lower, _dot_nt(v_cur, dos), _dot_nt(v_prev, dos))
                delta = jnp.sum(p * dp, axis=0, keepdims=True)
                ds = p * (dp - delta)
                sk = (es * inv) * delta
                ds_cur = jnp.where(lower, ds, 0.0)
                p_cur = jnp.where(lower, p, 0.0)
                ds_cur, ds_prev = _bf(ds_cur), _bf(ds - ds_cur)
                p_cur, p_prev = _bf(p_cur), _bf(p - p_cur)
                dqt = (_dot_tn(k_cur, ds_cur) + _dot_tn(k_prev, ds_prev)) * ATT_SCALE
                dkpad[nxt, lanes] += _dot(ds_cur, qs)
                dkpad[rows, lanes] += _dot(ds_prev, qs)
                dvpad[nxt, lanes] += _dot(p_cur, dos)
                dvpad[rows, lanes] += _dot(p_prev, dos)
                for g in range(ATT_GROUP):
                    h = ATT_GROUP * hk + g
                    cols = slice(g * WINDOW, (g + 1) * WINDOW)
                    dqb[:, h * ATT_HEAD_DIM:(h + 1) * ATT_HEAD_DIM] = dqt[:, cols].T
                    dsk[h:h + 1, :] += jnp.broadcast_to(-jnp.sum(sk[:, cols], axis=1, keepdims=True), (1, LANES))
            cs, sl, sr = cos_ref[rows, :], sl_ref[rows, :], sr_ref[rows, :]
            for j in range(ATT_WIDTH // LANES):
                dqkv_ref[rows, j * LANES:(j + 1) * LANES] = _bf(_rope_t(dqb[:, j * LANES:(j + 1) * LANES], cs, sl, sr))
            return daw

        daw = _loop_pairs(part * per, per, block, jnp.zeros((1, ATT_WIDTH), F32))
        daw_ref[...] += jnp.broadcast_to(daw, (8, ATT_WIDTH))
        dsink_ref[...] = dsk[...]

        def finish(n, carry):
            r0 = pl.multiple_of(n * WINDOW, WINDOW)
            rows = pl.ds(r0, WINDOW)
            nxt = pl.ds(r0 + WINDOW, WINDOW)
            cs, sl, sr = cos_ref[rows, :], sl_ref[rows, :], sr_ref[rows, :]
            dqkv_ref[rows, ATT_WIDTH:ATT_WIDTH + LANES] = _bf(_rope_t(dkpad[nxt, :], cs, sl, sr))
            dqkv_ref[rows, ATT_WIDTH + LANES:QKV] = _bf(dvpad[nxt, :])
            return carry

        @pl.when(part == splits - 1)
        def _():
            lax.fori_loop(0, nb, finish, 0)

        _ride_wait(ride_modes, step, B * splits, ride_in, ride_out, sems)

    seq = lambda w, j: pl.BlockSpec((None, T, w), lambda b, s: (b, 0, j))
    full = lambda r, w: pl.BlockSpec((r, w), lambda b, s: (0, 0))
    return pl.pallas_call(
        body, name="attn_bwd", grid=(B, splits),
        in_specs=[seq(ATT_WIDTH, 0), seq(LANES, 0), seq(LANES, 5), seq(ATT_WIDTH, 0), seq(ATT_WIDTH, 0),
                  full(T, LANES), full(T, LANES), full(T, LANES),
                  pl.BlockSpec(memory_space=pltpu.SMEM), full(1, ATT_WIDTH)] + [ANY_SPEC] * nr,
        out_specs=[seq(QKV, 0), pl.BlockSpec((None, 8, LANES), lambda b, s: (b, 0, 0)),
                   pl.BlockSpec((None, 8, ATT_WIDTH), lambda b, s: (b, 0, 0))] + [ANY_SPEC] * nr,
        out_shape=[SDS((B, T, QKV), BF16), SDS((B, 8, LANES), F32), SDS((B, 8, ATT_WIDTH), F32)]
        + _exchange_shapes(ride_srcs, ride_modes),
        scratch_shapes=[pltpu.VMEM((T + WINDOW, LANES), BF16), pltpu.VMEM((T + WINDOW, LANES), BF16),
                        pltpu.VMEM((T + WINDOW, LANES), F32), pltpu.VMEM((T + WINDOW, LANES), F32),
                        pltpu.VMEM((WINDOW, ATT_WIDTH), F32), pltpu.VMEM((8, LANES), F32)] + _exchange_sems(nr),
        compiler_params=_params(("arbitrary", "arbitrary"), VMEM_LIMIT_BIG),
    )(qr, kr, proj3, attn_o, dan, cos, sinl, sinr, sinks, attn_w, *ride_srcs)


def _in_bwd(x2, dx1, dqkv, dhq, dhf, dhi, dhg, mod8, pre_w, w_in_bf, T, ride_srcs, ride_modes):
    N = x2.shape[0]
    TM = _tile_rows(T, big=True)
    tps = T // TM
    nr = len(ride_srcs)
    pieces = [(0, ATT_WIDTH + 2 * LANES), (768, HG_WIDTH), (1280, HG_WIDTH), (1792, HG_WIDTH), (2304, HG_WIDTH)]

    def body(*refs):
        x_ref, dx_ref, p0, p1, p2, p3, p4, mod_ref, pw_ref, w_ref = refs[:10]
        ride_in = refs[10:10 + nr]
        gx_ref, dproj_ref, acc_ref = refs[10 + nr:13 + nr]
        ride_out = refs[13 + nr:13 + 2 * nr]
        sems = refs[13 + 2 * nr:]
        _ride_start(ride_modes, pl.program_id(0), N // TM, ride_in, ride_out, sems)
        sc1 = mod_ref[1:2, :]
        dh = jnp.zeros((TM, D_MODEL), F32)
        for ref, (off, width) in zip((p0, p1, p2, p3, p4), pieces):
            pb = ref[...]
            dproj_ref[:, off:off + width] = pb
            dh = dh + _dot(pb, w_ref[off:off + width, :])
        x = x_ref[...]
        r = lax.rsqrt(_mean_last(x * x) + EPS)
        xh = x * r
        n1 = xh * pw_ref[...]
        dsh1 = _sum_rows(dh)
        dsc1 = _sum_rows(dh * n1)
        dn1 = dh * (1.0 + sc1)
        dw_pre = _sum_rows(dn1 * xh)
        dxh = dn1 * pw_ref[...]
        gx_ref[...] = dx_ref[...] + r * (dxh - xh * _mean_last(dxh * xh))
        _acc_rows(acc_ref, pl.program_id(0) % tps == 0, [dsh1, dsc1, dw_pre])
        _ride_wait(ride_modes, pl.program_id(0), N // TM, ride_in, ride_out, sems)

    row = lambda w: pl.BlockSpec((TM, w), lambda i: (i, 0))
    B = N // T
    return pl.pallas_call(
        body, name="in_bwd", grid=(N // TM,),
        in_specs=[row(D_MODEL), row(D_MODEL), row(768), row(HG_WIDTH), row(HG_WIDTH), row(HG_WIDTH),
                  row(HG_WIDTH), _mod_spec(tps), pl.BlockSpec((1, D_MODEL), lambda i: (0, 0)),
                  pl.BlockSpec((IN_COLS, D_MODEL), lambda i: (0, 0))] + [ANY_SPEC] * nr,
        out_specs=[row(D_MODEL), row(IN_COLS), _mod_spec(tps)] + [ANY_SPEC] * nr,
        out_shape=[SDS((N, D_MODEL), F32), SDS((N, IN_COLS), BF16), SDS((B, 8, D_MODEL), F32)]
        + _exchange_shapes(ride_srcs, ride_modes),
        scratch_shapes=_exchange_sems(nr),
        compiler_params=_params(("arbitrary",), VMEM_LIMIT_BIG),
    )(x2, dx1, dqkv, dhq, dhf, dhi, dhg, mod8, pre_w, w_in_bf, *ride_srcs)


def _matmul_tn(name, a, b, tn, tm=512, by_owner_cols=False, ride_srcs=(), ride_modes=()):
    K, M = a.shape
    Nc = b.shape[1]
    tm = min(tm, M)
    nr = len(ride_srcs)
    ni, nj = M // tm, Nc // tn

    def body(*refs):
        a_ref, b_ref = refs[:2]
        ride_in = refs[2:2 + nr]
        o_ref = refs[2 + nr]
        ride_out = refs[3 + nr:3 + 2 * nr]
        sems = refs[3 + 2 * nr:]
        step = pl.program_id(0) * nj + pl.program_id(1)
        _ride_start(ride_modes, step, ni * nj, ride_in, ride_out, sems)
        o_ref[...] = _bf(_dot_tn(a_ref[...], b_ref[...]))
        _ride_wait(ride_modes, step, ni * nj, ride_in, ride_out, sems)

    if by_owner_cols:
        assert tn * N_DEV == Nc
        out_shape = SDS((N_DEV, M, tn), BF16)
        out_spec = pl.BlockSpec((None, tm, tn), lambda i, j: (j, i, 0))
    else:
        out_shape = SDS((M, Nc), BF16)
        out_spec = pl.BlockSpec((tm, tn), lambda i, j: (i, j))
    outs = pl.pallas_call(
        body, name=name, grid=(ni, nj),
        in_specs=[pl.BlockSpec((K, tm), lambda i, j: (0, i)),
                  pl.BlockSpec((K, tn), lambda i, j: (0, j))] + [ANY_SPEC] * nr,
        out_specs=[out_spec] + [ANY_SPEC] * nr,
        out_shape=[out_shape] + _exchange_shapes(ride_srcs, ride_modes),
        scratch_shapes=_exchange_sems(nr),
        compiler_params=_params(("arbitrary", "arbitrary"), VMEM_LIMIT_BIG),
    )(a, b, *ride_srcs)
    return outs if nr else outs[0]


def _adamw_math(w, g, m, v):
    m2 = ADAM_B1 * m + (1.0 - ADAM_B1) * g
    v2 = ADAM_B2 * v + (1.0 - ADAM_B2) * (g * g)
    m_hat = m2 / (1.0 - ADAM_B1 ** ADAM_STEP)
    v_hat = v2 / (1.0 - ADAM_B2 ** ADAM_STEP)
    delta = -ADAM_LR * (m_hat / (jnp.sqrt(v_hat) + ADAM_EPS) + ADAM_WD * w)
    return delta, m2, v2


def _pair_add(name, gw, theirs):
    chips, _, r, c = gw.shape
    tr = r
    core = lax.axis_index("c").astype(jnp.int32).reshape(1)

    def body(core_ref, mine_ref, theirs_ref, o_ref):
        o_ref[...] = _bf(mine_ref[...].astype(F32) + theirs_ref[...].astype(F32))

    block = pl.BlockSpec((None, tr, c), lambda s, i, core_ref: (s, i, 0))
    grid_spec = pltpu.PrefetchScalarGridSpec(
        num_scalar_prefetch=1, grid=(chips, r // tr),
        in_specs=[pl.BlockSpec((None, None, tr, c), lambda s, i, core_ref: (s, core_ref[0], i, 0)), block],
        out_specs=block)
    return pl.pallas_call(
        body, name=name, grid_spec=grid_spec, out_shape=SDS((chips, r, c), BF16),
        compiler_params=_params(("arbitrary", "arbitrary")),
    )(core, gw, theirs)


def _reduce_adamw(name, parts, w, m, v):
    r, c = w.shape
    tr = r if r % 256 else 256
    slots = parts.shape[0]

    def body(p_ref, w_ref, m_ref, v_ref, g_ref, d_ref, m2_ref, v2_ref):
        g = p_ref[0].astype(F32)
        for s in range(1, slots):
            g = g + p_ref[s].astype(F32)
        g_ref[...] = g
        d_ref[...], m2_ref[...], v2_ref[...] = _adamw_math(w_ref[...], g, m_ref[...], v_ref[...])

    blk = pl.BlockSpec((tr, c), lambda i: (i, 0))
    return pl.pallas_call(
        body, name=name, grid=(r // tr,),
        in_specs=[pl.BlockSpec((slots, tr, c), lambda i: (0, i, 0)), blk, blk, blk],
        out_specs=[blk] * 4, out_shape=[SDS((r, c), F32)] * 4,
        compiler_params=_params(("arbitrary",), VMEM_LIMIT_BIG),
    )(parts, w, m, v)


def _ada_grad_adamw(c_all, dmod_all, w, m, v):
    r, c = w.shape
    tr = 256
    nb = c_all.shape[0]

    def body(c_ref, dm_ref, w_ref, m_ref, v_ref, g_ref, d_ref, m2_ref, v2_ref):
        cv = c_ref[...]
        g = _dot_tn(cv * _sigmoid(cv), dm_ref[...])
        g_ref[...] = g
        d_ref[...], m2_ref[...], v2_ref[...] = _adamw_math(w_ref[...], g, m_ref[...], v_ref[...])

    blk = pl.BlockSpec((tr, c), lambda i: (i, 0))
    return pl.pallas_call(
        body, name="ada_grad_adamw", grid=(r // tr,),
        in_specs=[pl.BlockSpec((nb, tr), lambda i: (0, i)), pl.BlockSpec((nb, c), lambda i: (0, 0)),
                  blk, blk, blk],
        out_specs=[blk] * 4, out_shape=[SDS((r, c), F32)] * 4,
        compiler_params=_params(("arbitrary",)),
    )(c_all, dmod_all, w, m, v)


_SMALL = [("b_ada", 6144), ("pre_w_mix", 1024), ("attn_sinks", 128), ("attn_out_w", 512), ("lb_table", 1024),
          ("hg_norm_w", 128), ("post_w_mix", 1024), ("pre_w_mlp", 1024), ("post_w_mlp", 1024)]


def _pack_small(vals, loss_part):
    out = []
    for name, width in _SMALL:
        f = vals[name].reshape(-1).astype(F32)
        out.append(jnp.pad(f, (0, width - f.shape[0])))
    out.append(jnp.broadcast_to(loss_part, (LANES,)))
    return jnp.concatenate(out).reshape(1, -1)


def _adamw_small(parts, given):
    names = [n for n, _ in _SMALL]
    flat_in = [a for n in names for a in given[n]]

    def body(*refs):
        p_ref = refs[0]
        in_refs = refs[1:1 + 3 * len(names)]
        out_refs = refs[1 + 3 * len(names):-1]
        loss_ref = refs[-1]
        g = p_ref[0]
        for s in range(1, N_DEV):
            g = g + p_ref[s]
        off = 0
        for i, (name, width) in enumerate(_SMALL):
            w_ref, m_ref, v_ref = in_refs[3 * i:3 * i + 3]
            rows, cols = w_ref.shape
            for r in range(rows):
                gr = g[:, off + r * cols:off + (r + 1) * cols]
                res = (gr,) + _adamw_math(w_ref[r:r + 1, :], gr, m_ref[r:r + 1, :], v_ref[r:r + 1, :])
                for o_ref, val in zip(out_refs[4 * i:4 * i + 4], res):
                    o_ref[r:r + 1, :] = val
            off += width
        loss_ref[...] = g[:, off:off + LANES]

    out_shape = [SDS(given[n][0].shape, F32) for n in names for _ in range(4)] + [SDS((1, LANES), F32)]
    outs = pl.pallas_call(body, name="adamw_small", out_shape=out_shape)(parts, *flat_in)
    return {n: tuple(outs[4 * i:4 * i + 4]) for i, n in enumerate(names)}, outs[-1][0, 0]


def kernel(x, c, w_ada, b_ada, pre_w_mix, w_in, attn_sinks, attn_out_w, lb_table, hg_norm_w, w_out, post_w_mix, pre_w_mlp, w_up, w_down, post_w_mlp, loss_target, m_w_ada, m_b_ada, m_pre_w_mix, m_w_in, m_attn_sinks, m_attn_out_w, m_lb_table, m_hg_norm_w, m_w_out, m_post_w_mix, m_pre_w_mlp, m_w_up, m_w_down, m_post_w_mlp, v_w_ada, v_b_ada, v_pre_w_mix, v_w_in, v_attn_sinks, v_attn_out_w, v_lb_table, v_hg_norm_w, v_w_out, v_post_w_mix, v_pre_w_mlp, v_w_up, v_w_down, v_post_w_mlp):
    B, T, _ = x.shape
    N = B * T
    me = 4 * lax.axis_index("x") + 2 * lax.axis_index("y") + lax.axis_index("c")
    x2 = x.reshape(N, D_MODEL)
    tgt2 = loss_target.reshape(N, D_MODEL)

    w_in_t, m_w_in_t, v_w_in_t = w_in[0].T, m_w_in[0].T, v_w_in[0].T
    w_in_g, c_g = _exchange("gather_w_in", [_bf(w_in_t), c], ["gather"] * 2)
    w_in_f = w_in_g.reshape(IN_COLS, D_MODEL)
    c_all = c_g.reshape(N_DEV * B, D_MODEL)

    ada_cols = w_ada.shape[2]
    b_mine = lax.dynamic_slice(b_ada, (0, me * ada_cols), (1, ada_cols))
    mod_cols = _ada_mod(c_all, w_ada[0], b_mine)
    (mod_g,) = _exchange("scatter_mod", [mod_cols.reshape(N_DEV, B, ada_cols)], ["a2a"])
    mod = mod_g.transpose(1, 0, 2).reshape(B, 6, D_MODEL)
    mod8 = jnp.pad(mod, ((0, 0), (0, 2), (0, 0)))

    lb_p = jax.nn.softmax(lb_table, axis=0)
    lb = lb_p[1:2]
    tables = _rope_tables(T)

    w_up_b, w_down_b = _bf(w_up[0]), _bf(w_down[0])
    proj_a, proj_h, h1, w_out_g = _in_proj(x2, mod8, pre_w_mix, w_in_f, T, [_bf(w_out[0])], ["gather"])
    proj3 = proj_a.reshape(B, T, ATT_COLS)
    proj_h = proj_h.reshape(B, T, IN_COLS - ATT_COLS)
    rec_o, rec_g, s_prev, w_up_g0, w_up_g1 = _hgrn_fwd(proj_h, lb, hg_norm_w,
                                                       [w_up_b[:MLP_HALF], w_up_b[MLP_HALF:]], ["gather"] * 2)
    attn_o, attn_n, qr, kr, w_down_g0 = _attn_fwd(proj3, tables, attn_sinks, attn_out_w,
                                                  [w_down_b[:, :MLP_HALF]], ["gather"])
    w_out_f = w_out_g.reshape(D_MODEL, D_MODEL)
    mix, x1, cat, w_down_g1 = _mix_out(x2, attn_n.reshape(N, ATT_WIDTH), rec_g.reshape(N, HG_WIDTH), mod8,
                                       post_w_mix, w_out_f, T, [w_down_b[:, MLP_HALF:]], ["gather"])
    w_up_halves = [w_up_g0, w_up_g1]
    w_down_halves = [w_down_g0.reshape(D_FF, MLP_HALF), w_down_g1.reshape(D_FF, MLP_HALF)]
    up, u, d, h2 = _mlp_fwd(x1, mod8, pre_w_mlp, w_up_halves, w_down_halves, T)

    dx1, dup, dd, acc_mlp = _mlp_bwd(x1, d, up, tgt2, mod8, pre_w_mlp, post_w_mlp, w_up_halves, w_down_halves, T)
    chips = N_DEV // 2
    by_chip = lambda a: a.reshape((chips, 2, a.shape[0] // N_DEV) + a.shape[1:])
    gw_up = _matmul_tn("grad_w_up", h2, dup, D_FF // N_DEV, by_owner_cols=True)
    gw_up = gw_up.reshape(chips, 2, D_MODEL, D_FF // N_DEV)
    dan, drg, dmix, acc_mix, q_up = _mix_bwd(mix, dx1, mod8, post_w_mix, w_out_f, T, [gw_up], ["pair"])
    p_up = _pair_add("pair_add_w_up", gw_up, q_up)
    dhq, dhf, dhi, dhg, dlb_p, dgw_p, gw_down, gw_out, r_up = _hgrn_bwd(
        proj_h, lb, hg_norm_w, rec_o, s_prev, drg.reshape(B, T, HG_WIDTH), [(u, dd), (cat, dmix)],
        [p_up], ["chips"])
    gw_down = by_chip(gw_down)
    gw_out = gw_out.reshape(N_DEV, D_MODEL // N_DEV, D_MODEL)
    dqkv, dsink_p, daw_p, q_down, r_out = _attn_bwd(qr, kr, proj3, attn_o, dan.reshape(B, T, ATT_WIDTH), tables,
                                                    attn_sinks, attn_out_w, [gw_down, gw_out], ["pair", "a2a"])
    p_down = _pair_add("pair_add_w_down", gw_down, q_down)
    flat = lambda a: a.reshape(N, a.shape[-1])
    grad_x, dproj, acc_in = _in_bwd(x2, dx1, flat(dqkv), flat(dhq), flat(dhf), flat(dhi), flat(dhg),
                                    mod8, pre_w_mix, w_in_f, T, [], [])

    gw_in, r_down = _matmul_tn("grad_w_in", dproj, h1, 512, tm=IN_COLS // 2,
                               ride_srcs=[p_down], ride_modes=["chips"])
    gw_in = by_chip(gw_in)
    (q_in,) = _exchange("pair_w_in", [gw_in], ["pair"])
    p_in = _pair_add("pair_add_w_in", gw_in, q_in)

    dmod = jnp.concatenate([acc_in[:, 0:2], acc_mix[:, 0:1], acc_mlp[:, 0:3]], axis=1)
    dlb = dlb_p[:, 0].sum(0)
    dlb_table = jnp.stack([-dlb, dlb]) * (lb_p[0] * lb_p[1])[None, :]
    small = {
        "b_ada": dmod.sum(0),
        "pre_w_mix": acc_in[:, 2].sum(0),
        "attn_sinks": dsink_p[:, :, 0].sum(0),
        "attn_out_w": daw_p[:, 0].sum(0),
        "lb_table": dlb_table,
        "hg_norm_w": dgw_p[:, 0].sum(0),
        "post_w_mix": acc_mix[:, 1].sum(0),
        "pre_w_mlp": acc_mlp[:, 3].sum(0),
        "post_w_mlp": acc_mlp[:, 4].sum(0),
    }
    loss_part = acc_mlp[:, 5, 0].sum()
    dmod_blocks = dmod.reshape(B, N_DEV, ada_cols).transpose(1, 0, 2)

    r_in, r_dmod, r_small = _exchange(
        "reduce_grads", [p_in, dmod_blocks, _pack_small(small, loss_part)], ["chips", "a2a", "gather"])

    res = {}
    res["w_in"] = tuple(a.T for a in _reduce_adamw("adamw_w_in", r_in, w_in_t, m_w_in_t, v_w_in_t))
    res["w_out"] = _reduce_adamw("adamw_w_out", r_out, w_out[0], m_w_out[0], v_w_out[0])
    res["w_up"] = _reduce_adamw("adamw_w_up", r_up, w_up[0], m_w_up[0], v_w_up[0])
    res["w_down"] = _reduce_adamw("adamw_w_down", r_down, w_down[0], m_w_down[0], v_w_down[0])
    res["w_ada"] = _ada_grad_adamw(c_all, r_dmod.reshape(N_DEV * B, ada_cols), w_ada[0], m_w_ada[0], v_w_ada[0])

    given = dict(b_ada=(b_ada, m_b_ada, v_b_ada), pre_w_mix=(pre_w_mix, m_pre_w_mix, v_pre_w_mix),
                 attn_sinks=(attn_sinks, m_attn_sinks, v_attn_sinks),
                 attn_out_w=(attn_out_w, m_attn_out_w, v_attn_out_w), lb_table=(lb_table, m_lb_table, v_lb_table),
                 hg_norm_w=(hg_norm_w, m_hg_norm_w, v_hg_norm_w), post_w_mix=(post_w_mix, m_post_w_mix, v_post_w_mix),
                 pre_w_mlp=(pre_w_mlp, m_pre_w_mlp, v_pre_w_mlp), post_w_mlp=(post_w_mlp, m_post_w_mlp, v_post_w_mlp))
    small_res, loss = _adamw_small(r_small, given)
    res.update(small_res)

    order = ["w_ada", "b_ada", "pre_w_mix", "w_in", "attn_sinks", "attn_out_w", "lb_table", "hg_norm_w", "w_out",
             "post_w_mix", "pre_w_mlp", "w_up", "w_down", "post_w_mlp"]
    big = {"w_ada", "w_in", "w_out", "w_up", "w_down"}
    outs = [loss, grad_x.reshape(B, T, D_MODEL)]
    for i in range(4):
        for k in order:
            a = res[k][i]
            outs.append(a[None] if k in big else a)
    return tuple(outs)
```

```python
import jax
import jax.numpy as jnp
import numpy as np
from jax import lax
from jax.experimental import pallas as pl
from jax.experimental.pallas import tpu as pltpu

F32 = jnp.float32
BF16 = jnp.bfloat16
SDS = jax.ShapeDtypeStruct

D_MODEL = 1024
ATT_WIDTH = 512
ATT_HEAD_DIM = 64
ATT_KV_HEADS = 2
ATT_GROUP = 4
WINDOW = 128
ROPE_DIM = 16
ROPE_THETA = 500000.0
HG_WIDTH = 512
HG_HEAD_DIM = 128
HG_HEADS = 4
HG_CHUNK = 32
IN_COLS = 2816
ATT_COLS = 768
D_FF = 4096
EPS = 1e-6
N_DEV = 8

ADAM_LR = 0.001
ADAM_B1 = 0.9
ADAM_B2 = 0.999
ADAM_EPS = 1e-08
ADAM_WD = 0.01
ADAM_STEP = 10

VMEM_LIMIT_BIG = 56 << 20
LANES = 128

MESH = pl.DeviceIdType.MESH
NT_DIMS = (((1,), (1,)), ((), ()))
TN_DIMS = (((0,), (0,)), ((), ()))


def _dot(a, b):
    return jnp.dot(a, b, preferred_element_type=F32)


def _dot_nt(a, b):
    return lax.dot_general(a, b, NT_DIMS, preferred_element_type=F32)


def _dot_tn(a, b):
    return lax.dot_general(a, b, TN_DIMS, preferred_element_type=F32)


def _bf(a):
    return a.astype(BF16)


def _sigmoid(a):
    return 1.0 / (1.0 + jnp.exp(-a))


def _mean_last(a):
    return jnp.mean(a, axis=-1, keepdims=True)


def _sum_rows(a):
    return jnp.sum(a, axis=0, keepdims=True)


def _tri_sum(tri_bf, a, terms=3):
    a1 = _bf(a)
    r1 = a - a1.astype(F32)
    a2 = _bf(r1)
    out = _dot(tri_bf, a1) + _dot(tri_bf, a2)
    if terms == 3:
        out = out + _dot(tri_bf, _bf(r1 - a2.astype(F32)))
    return out


def _loop_pairs(first, count, body, init, per_trip=2):
    if count % per_trip:
        return lax.fori_loop(first, first + count, body, init)

    def trip(i, c):
        for k in range(per_trip):
            c = body(first + per_trip * i + k, c)
        return c

    return lax.fori_loop(0, count // per_trip, trip, init)


def _params(sem=None, vmem=None):
    kw = {}
    if sem is not None:
        kw["dimension_semantics"] = sem
    if vmem is not None:
        kw["vmem_limit_bytes"] = vmem
    return pltpu.CompilerParams(**kw)


ANY_SPEC = pl.BlockSpec(memory_space=pl.ANY)


def _exchange_shapes(srcs, modes):
    out_shape = []
    for s, m in zip(srcs, modes):
        shp = {"gather": (N_DEV,) + tuple(s.shape), "pair": (s.shape[0],) + tuple(s.shape[2:])}.get(m, tuple(s.shape))
        out_shape.append(SDS(shp, s.dtype))
    return out_shape


def _exchange_sems(n):
    if n == 0:
        return []
    return [pltpu.SemaphoreType.DMA((n, N_DEV - 1)), pltpu.SemaphoreType.DMA((n, N_DEV - 1)),
            pltpu.SemaphoreType.DMA((n,))]


SIBLING = 1
OTHER_CHIPS = (2, 4, 6)


def _related(k):
    x, y, c = lax.axis_index("x"), lax.axis_index("y"), lax.axis_index("c")
    px, py, pc = x ^ ((k >> 2) & 1), y ^ ((k >> 1) & 1), c ^ (k & 1)
    return (px, py, pc), 4 * px + 2 * py + pc


def _exchange_phases(modes, src_refs, out_refs, send_sems, recv_sems, own_sems):
    _, me = _related(0)
    sib_dev, sib = _related(SIBLING)
    start, middle, end = [], [], []

    def remote(a, i, src, dst, dev):
        return pltpu.make_async_remote_copy(src_ref=src, dst_ref=dst, send_sem=send_sems.at[a, i],
                                            recv_sem=recv_sems.at[a, i], device_id=dev, device_id_type=MESH)

    for a, mode in enumerate(modes):
        out = out_refs[a]
        if mode == "gather":
            src = src_refs[a]
            own = pltpu.make_async_copy(src, out.at[me], own_sems.at[a])
            to_sib = remote(a, 0, src, out.at[me], sib_dev)
            start += [own.start, to_sib.start]
            end += [remote(a, 0, src, out.at[sib], sib_dev).wait_recv, to_sib.wait_send, own.wait]
            for j, k in enumerate(OTHER_CHIPS, start=1):
                dev, peer = _related(k)
                _, peer_sib = _related(k ^ SIBLING)
                send = remote(a, j, src, out.at[me], dev)
                passed = remote(a, 3 + j, out.at[peer], out.at[peer], sib_dev)
                start.append(send.start)
                middle += [remote(a, j, src, out.at[peer], dev).wait_recv, passed.start]
                end += [remote(a, 3 + j, out.at[peer_sib], out.at[peer_sib], sib_dev).wait_recv,
                        send.wait_send, passed.wait_send]
        elif mode == "pair":
            core = lax.axis_index("c")
            for s in range(N_DEV // 2):
                send = remote(a, s, src_refs[a].at[s, 1 - core], out.at[s], sib_dev)
                start.append(send.start)
                end += [remote(a, s, src_refs[a].at[s, 1 - core], out.at[s], sib_dev).wait_recv, send.wait_send]
        elif mode == "chips":
            chip = me // 2
            own = pltpu.make_async_copy(src_refs[a].at[chip], out.at[chip], own_sems.at[a])
            start.append(own.start)
            end.append(own.wait)
            for j, k in enumerate(OTHER_CHIPS, start=1):
                dev, peer = _related(k)
                send = remote(a, j, src_refs[a].at[peer // 2], out.at[chip], dev)
                start.append(send.start)
                end += [remote(a, j, src_refs[a].at[peer // 2], out.at[peer // 2], dev).wait_recv, send.wait_send]
        else:
            own = pltpu.make_async_copy(src_refs[a].at[me], out.at[me], own_sems.at[a])
            start.append(own.start)
            end.append(own.wait)
            for k in range(1, N_DEV):
                dev, peer = _related(k)
                send = remote(a, k - 1, src_refs[a].at[peer], out.at[me], dev)
                start.append(send.start)
                end += [remote(a, k - 1, src_refs[a].at[peer], out.at[peer], dev).wait_recv, send.wait_send]
    return start, middle, end


def _run(actions):
    for act in actions:
        act()


def _exchange(name, srcs, modes):
    n = len(srcs)

    def body(*refs):
        start, middle, end = _exchange_phases(modes, refs[:n], refs[n:2 * n], *refs[2 * n:])
        _run(start)
        _run(middle)
        _run(end)

    return pl.pallas_call(
        body, name=name, out_shape=_exchange_shapes(srcs, modes),
        in_specs=[ANY_SPEC] * n, out_specs=[ANY_SPEC] * n,
        scratch_shapes=_exchange_sems(n),
    )(*srcs)


def _ride_start(modes, step, steps, src_refs, out_refs, sems):
    if not modes:
        return
    middle_step = steps - 1

    @pl.when(step == 0)
    def _():
        _run(_exchange_phases(modes, src_refs, out_refs, *sems)[0])

    if "gather" in modes:
        @pl.when(step == middle_step)
        def _():
            _run(_exchange_phases(modes, src_refs, out_refs, *sems)[1])


def _ride_wait(modes, step, steps, src_refs, out_refs, sems):
    if not modes:
        return

    @pl.when(step == steps - 1)
    def _():
        _run(_exchange_phases(modes, src_refs, out_refs, *sems)[2])


def _ada_mod(c_all, w_ada, b_ada_mine):
    nb, cols = c_all.shape[0], w_ada.shape[1]

    def body(c_ref, w_ref, b_ref, o_ref):
        cv = c_ref[...]
        ca = cv * _sigmoid(cv)
        o_ref[...] = _dot(ca, w_ref[...]) + b_ref[...]

    return pl.pallas_call(body, name="ada_mod", out_shape=SDS((nb, cols), F32))(c_all, w_ada, b_ada_mine)


def _tile_rows(T, big=False):
    return min(512 if big else 256, T)


def _mod_spec(tps):
    return pl.BlockSpec((None, 8, D_MODEL), lambda i: (i // tps, 0, 0))


def _in_proj(x2, mod8, pre_w, w_in_bf, T, ride_srcs, ride_modes):
    N = x2.shape[0]
    TM = _tile_rows(T, big=True)
    tps = T // TM
    nr = len(ride_srcs)

    def body(*refs):
        x_ref, mod_ref, pw_ref, w_ref = refs[:4]
        ride_in = refs[4:4 + nr]
        pa_ref, ph_ref, h1_ref = refs[4 + nr:7 + nr]
        ride_out = refs[7 + nr:7 + 2 * nr]
        sems = refs[7 + 2 * nr:]
        _ride_start(ride_modes, pl.program_id(0), N // TM, ride_in, ride_out, sems)
        x = x_ref[...]
        r = lax.rsqrt(_mean_last(x * x) + EPS)
        h = (x * r * pw_ref[...]) * (1.0 + mod_ref[1:2, :]) + mod_ref[0:1, :]
        hb = _bf(h)
        h1_ref[...] = hb
        pa_ref[...] = _dot_nt(hb, w_ref[:ATT_COLS, :])
        ph_ref[...] = _dot_nt(hb, w_ref[ATT_COLS:, :])
        _ride_wait(ride_modes, pl.program_id(0), N // TM, ride_in, ride_out, sems)

    return pl.pallas_call(
        body, name="in_proj", grid=(N // TM,),
        in_specs=[pl.BlockSpec((TM, D_MODEL), lambda i: (i, 0)), _mod_spec(tps),
                  pl.BlockSpec((1, D_MODEL), lambda i: (0, 0)),
                  pl.BlockSpec((IN_COLS, D_MODEL), lambda i: (0, 0))] + [ANY_SPEC] * nr,
        out_specs=[pl.BlockSpec((TM, ATT_COLS), lambda i: (i, 0)),
                   pl.BlockSpec((TM, IN_COLS - ATT_COLS), lambda i: (i, 0)),
                   pl.BlockSpec((TM, D_MODEL), lambda i: (i, 0))] + [ANY_SPEC] * nr,
        out_shape=[SDS((N, ATT_COLS), F32), SDS((N, IN_COLS - ATT_COLS), F32), SDS((N, D_MODEL), BF16)]
        + _exchange_shapes(ride_srcs, ride_modes),
        scratch_shapes=_exchange_sems(nr),
        compiler_params=_params(("arbitrary",), VMEM_LIMIT_BIG),
    )(x2, mod8, pre_w, w_in_bf, *ride_srcs)


def _rope_tables(T):
    half = ROPE_DIM // 2
    f32 = np.float32
    inv_freq = (f32(ROPE_THETA) ** (-np.arange(0, ROPE_DIM, 2, dtype=f32) / f32(ROPE_DIM))).astype(f32)
    ang = np.arange(T, dtype=f32)[:, None] * inv_freq[None, :]
    cos, sin = np.cos(ang).astype(f32), np.sin(ang).astype(f32)
    ones = np.ones((T, ATT_HEAD_DIM - ROPE_DIM), f32)
    zeros = np.zeros((T, ATT_HEAD_DIM - ROPE_DIM), f32)
    zh = np.zeros((T, half), f32)
    cos64 = np.concatenate([cos, cos, ones], axis=1)
    sin_left = np.concatenate([-sin, zh, zeros], axis=1)
    sin_right = np.concatenate([zh, sin, zeros], axis=1)
    rep = LANES // ATT_HEAD_DIM
    return tuple(jnp.asarray(np.tile(t, (1, rep))) for t in (cos64, sin_left, sin_right))


def _rope(xc, cs, sl, sr):
    return xc * cs + pltpu.roll(xc, LANES - 8, 1) * sl + pltpu.roll(xc, 8, 1) * sr


def _rope_t(dy, cs, sl, sr):
    return dy * cs + pltpu.roll(dy * sl, 8, 1) + pltpu.roll(dy * sr, LANES - 8, 1)


ATT_SCALE = ATT_HEAD_DIM ** -0.5
ATT_SPLITS = 4


def _lower_mask():
    j = lax.broadcasted_iota(jnp.int32, (WINDOW, ATT_GROUP * WINDOW), 0)
    i = lax.broadcasted_iota(jnp.int32, (WINDOW, ATT_GROUP * WINDOW), 1) & (WINDOW - 1)
    return j <= i


def _sink_row(sink_ref, hk):
    return jnp.concatenate(
        [jnp.full((1, WINDOW), sink_ref[0, ATT_GROUP * hk + g], F32) for g in range(ATT_GROUP)], axis=1)


def _softmax_window(qs, k_cur, k_prev, lower, has_prev, sink):
    s_prev = jnp.where(has_prev, _dot_nt(k_prev, qs), jnp.finfo(F32).min)
    s = jnp.where(lower, _dot_nt(k_cur, qs), s_prev)
    m = jnp.maximum(jnp.max(s, axis=0, keepdims=True), sink)
    p = jnp.exp(s - m)
    es = jnp.exp(sink - m)
    inv = 1.0 / (jnp.sum(p, axis=0, keepdims=True) + es)
    return p, inv, es


def _stack_heads(parts, hk):
    hs = []
    for g in range(ATT_GROUP):
        h = ATT_GROUP * hk + g
        hs.append(parts[h // 2][:, (h % 2) * ATT_HEAD_DIM:(h % 2 + 1) * ATT_HEAD_DIM])
    return jnp.concatenate(hs, axis=0)


def _attn_fwd(proj3, tables, sinks, attn_w, ride_srcs, ride_modes):
    B, T, _ = proj3.shape
    nb = T // WINDOW
    splits = min(ATT_SPLITS, nb)
    per = nb // splits
    nr = len(ride_srcs)
    cos, sinl, sinr = tables

    def body(*refs):
        q_ref, k_ref, v_ref, cos_ref, sl_ref, sr_ref, sink_ref, aw_ref = refs[:8]
        ride_in = refs[8:8 + nr]
        o_ref, an_ref, qr_ref, kr_ref = refs[8 + nr:12 + nr]
        ride_out = refs[12 + nr:12 + 2 * nr]
        kpad, vpad = refs[12 + 2 * nr:14 + 2 * nr]
        sems = refs[14 + 2 * nr:]
        part = pl.program_id(1)
        step = pl.program_id(0) * splits + part
        _ride_start(ride_modes, step, B * splits, ride_in, ride_out, sems)

        @pl.when(part == 0)
        def _():
            kpad[0:WINDOW, :] = jnp.zeros((WINDOW, LANES), BF16)
            vpad[0:WINDOW, :] = jnp.zeros((WINDOW, LANES), BF16)

        lower = _lower_mask()

        def block(n, carry):
            r0 = pl.multiple_of(n * WINDOW, WINDOW)
            rows = pl.ds(r0, WINDOW)
            nxt = pl.ds(r0 + WINDOW, WINDOW)
            cs, sl, sr = cos_ref[rows, :], sl_ref[rows, :], sr_ref[rows, :]
            kb = _bf(_rope(k_ref[rows, :], cs, sl, sr))
            vb = _bf(v_ref[rows, :])
            kpad[nxt, :] = kb
            kr_ref[rows, :] = kb
            vpad[nxt, :] = vb
            qparts = []
            for j in range(ATT_WIDTH // LANES):
                qp = _bf(_rope(q_ref[rows, j * LANES:(j + 1) * LANES], cs, sl, sr) * ATT_SCALE)
                qr_ref[rows, j * LANES:(j + 1) * LANES] = qp
                qparts.append(qp)
            for hk in range(ATT_KV_HEADS):
                lanes = slice(hk * ATT_HEAD_DIM, (hk + 1) * ATT_HEAD_DIM)
                qs = _stack_heads(qparts, hk)
                p, inv, _ = _softmax_window(qs, kb[:, lanes], kpad[rows, lanes], lower, n > 0,
                                            _sink_row(sink_ref, hk))
                p_cur = jnp.where(lower, p, 0.0)
                ot = (_dot_tn(vb[:, lanes], _bf(p_cur)) + _dot_tn(vpad[rows, lanes], _bf(p - p_cur))) * inv
                for g in range(ATT_GROUP):
                    h = ATT_GROUP * hk + g
                    o_ref[rows, h * ATT_HEAD_DIM:(h + 1) * ATT_HEAD_DIM] = ot[:, g * WINDOW:(g + 1) * WINDOW].T
            ob = o_ref[rows, :]
            an_ref[rows, :] = _bf(ob * lax.rsqrt(_mean_last(ob * ob) + EPS) * aw_ref[...])
            return carry

        _loop_pairs(part * per, per, block, 0)
        _ride_wait(ride_modes, step, B * splits, ride_in, ride_out, sems)

    seq = lambda w, j: pl.BlockSpec((None, T, w), lambda b, s: (b, 0, j))
    full = lambda r, w: pl.BlockSpec((r, w), lambda b, s: (0, 0))
    return pl.pallas_call(
        body, name="attn_fwd", grid=(B, splits),
        in_specs=[seq(ATT_WIDTH, 0), seq(LANES, 4), seq(LANES, 5),
                  full(T, LANES), full(T, LANES), full(T, LANES),
                  pl.BlockSpec(memory_space=pltpu.SMEM), full(1, ATT_WIDTH)] + [ANY_SPEC] * nr,
        out_specs=[seq(ATT_WIDTH, 0), seq(ATT_WIDTH, 0), seq(ATT_WIDTH, 0), seq(LANES, 0)] + [ANY_SPEC] * nr,
        out_shape=[SDS((B, T, ATT_WIDTH), F32), SDS((B, T, ATT_WIDTH), BF16),
                   SDS((B, T, ATT_WIDTH), BF16), SDS((B, T, LANES), BF16)] + _exchange_shapes(ride_srcs, ride_modes),
        scratch_shapes=[pltpu.VMEM((T + WINDOW, LANES), BF16), pltpu.VMEM((T + WINDOW, LANES), BF16)]
        + _exchange_sems(nr),
        compiler_params=_params(("arbitrary", "arbitrary"), VMEM_LIMIT_BIG),
    )(proj3, proj3, proj3, cos, sinl, sinr, sinks, attn_w, *ride_srcs)


HG_GROUP = 8
HG_ROWS = HG_GROUP * HG_CHUNK


HG_STACK = HG_GROUP * HG_HEAD_DIM


def _group_masks():
    r = lax.broadcasted_iota(jnp.int32, (HG_ROWS, HG_ROWS), 0)
    c = lax.broadcasted_iota(jnp.int32, (HG_ROWS, HG_ROWS), 1)
    same = (r // HG_CHUNK) == (c // HG_CHUNK)
    return same & (r >= c), same & (c >= r)


def _row_chunk():
    return lax.broadcasted_iota(jnp.int32, (HG_ROWS, HG_HEAD_DIM), 0) // HG_CHUNK


def _spread(a, row_chunk):
    return jnp.concatenate([jnp.where(row_chunk == c, a, jnp.zeros_like(a)) for c in range(HG_GROUP)], axis=1)


def _pick(r, row_chunk):
    out = jnp.where(row_chunk == 0, r[:, :HG_HEAD_DIM], 0.0)
    for c in range(1, HG_GROUP):
        out = out + jnp.where(row_chunk == c, r[:, c * HG_HEAD_DIM:(c + 1) * HG_HEAD_DIM], 0.0)
    return out


def _lane_block(a, c):
    return a[:, c * HG_HEAD_DIM:(c + 1) * HG_HEAD_DIM]


def _ones_bf(mask):
    return jnp.where(mask, 1.0, 0.0).astype(BF16)


def _chunk_bcast(rows_1x128):
    return jnp.concatenate([jnp.broadcast_to(r, (HG_CHUNK, HG_HEAD_DIM)) for r in rows_1x128], axis=0)


def _hgrn_gates(hq, hf, lb, lower_bf):
    sq = _sigmoid(hq)
    q = hq * sq
    sg = _sigmoid(hf)
    f = lb + (1.0 - lb) * sg
    k = 1.0 - f
    logf = jnp.log(f)
    b = _tri_sum(lower_bf, logf)
    bl = [_sum_rows(logf[_chunk_rows(c), :]) for c in range(HG_GROUP)]
    eb, enb, e2 = jnp.exp(b), jnp.exp(-b), jnp.exp(_chunk_bcast(bl) - b)
    ebl = [jnp.exp(r) for r in bl]
    return dict(sq=sq, sg=sg, f=f, eb=eb, enb=enb, e2=e2, ebl=ebl, qd=q * eb, kd=k * enb, k2=k * e2)


def _chunk_rows(c):
    return slice(c * HG_CHUNK, (c + 1) * HG_CHUNK)


def _head_lanes(h):
    return slice(h * HG_HEAD_DIM, (h + 1) * HG_HEAD_DIM)


def _hgrn_fwd(proj_h, lb, hg_w, ride_srcs, ride_modes):
    B, T, _ = proj_h.shape
    ng = T // HG_ROWS
    nr = len(ride_srcs)

    def body(*refs):
        hq_ref, hf_ref, hi_ref, hg_ref, lb_ref, gw_ref = refs[:6]
        ride_in = refs[6:6 + nr]
        o_ref, rg_ref, sp_ref = refs[6 + nr:9 + nr]
        ride_out = refs[9 + nr:9 + 2 * nr]
        st = refs[9 + 2 * nr]
        sems = refs[10 + 2 * nr:]
        gi = pl.program_id(1)
        step = pl.program_id(0) * ng + gi
        _ride_start(ride_modes, step, B * ng, ride_in, ride_out, sems)

        @pl.when(gi == 0)
        def _():
            st[...] = jnp.zeros(st.shape, F32)

        lo, _ = _group_masks()
        lower_bf = _ones_bf(lo)
        row_chunk = _row_chunk()
        for h in range(HG_HEADS):
            lanes = _head_lanes(h)
            gt = _hgrn_gates(hq_ref[:, lanes], hf_ref[:, lanes], lb_ref[:, lanes], lower_bf)
            v, qd, kd = _bf(hi_ref[:, lanes]), _bf(gt["qd"]), _bf(gt["kd"])
            a = jnp.where(lo, _dot_nt(qd, kd), 0.0)
            kv = _dot_tn(v, _bf(_spread(gt["k2"], row_chunk)))
            s = st[h]
            before = []
            for c in range(HG_GROUP):
                before.append(s)
                s = s * gt["ebl"][c] + _lane_block(kv, c)
            st[h] = s
            sp = jnp.concatenate(before, axis=1)
            sp_ref[h] = sp
            o = _dot(_bf(a), v) + _dot_nt(_bf(_spread(gt["qd"], row_chunk)), _bf(sp))
            o_ref[:, lanes] = o
            hg = hg_ref[:, lanes]
            rn = o * lax.rsqrt(_mean_last(o * o) + EPS) * gw_ref[...]
            rg_ref[:, lanes] = _bf(rn * (hg * _sigmoid(hg)))
        _ride_wait(ride_modes, step, B * ng, ride_in, ride_out, sems)

    part = lambda j: pl.BlockSpec((None, HG_ROWS, HG_WIDTH), lambda b, g: (b, g, j))
    return pl.pallas_call(
        body, name="hgrn_fwd", grid=(B, ng),
        in_specs=[part(0), part(1), part(2), part(3),
                  pl.BlockSpec((1, HG_WIDTH), lambda b, g: (0, 0)),
                  pl.BlockSpec((1, LANES), lambda b, g: (0, 0))] + [ANY_SPEC] * nr,
        out_specs=[part(0), part(0),
                   pl.BlockSpec((None, HG_HEADS, None, HG_HEAD_DIM, HG_STACK), lambda b, g: (b, 0, g, 0, 0))]
        + [ANY_SPEC] * nr,
        out_shape=[SDS((B, T, HG_WIDTH), F32), SDS((B, T, HG_WIDTH), BF16),
                   SDS((B, HG_HEADS, ng, HG_HEAD_DIM, HG_STACK), F32)] + _exchange_shapes(ride_srcs, ride_modes),
        scratch_shapes=[pltpu.VMEM((HG_HEADS, HG_HEAD_DIM, HG_HEAD_DIM), F32)] + _exchange_sems(nr),
        compiler_params=_params(("arbitrary", "arbitrary"), VMEM_LIMIT_BIG),
    )(proj_h, proj_h, proj_h, proj_h, lb, hg_w, *ride_srcs)


def _mix_out(x2, attn_n, rec_g, mod8, post_w, w_out_bf, T, ride_srcs, ride_modes):
    N = x2.shape[0]
    TM = _tile_rows(T, big=True)
    tps = T // TM
    nr = len(ride_srcs)

    def body(*refs):
        x_ref, an_ref, rg_ref, mod_ref, pw_ref, w_ref = refs[:6]
        ride_in = refs[6:6 + nr]
        mix_ref, x1_ref, cat_ref = refs[6 + nr:9 + nr]
        ride_out = refs[9 + nr:9 + 2 * nr]
        sems = refs[9 + 2 * nr:]
        _ride_start(ride_modes, pl.program_id(0), N // TM, ride_in, ride_out, sems)
        cat = jnp.concatenate([an_ref[...], rg_ref[...]], axis=1)
        cat_ref[...] = cat
        mix = _dot(cat, w_ref[...])
        mix_ref[...] = mix
        r = lax.rsqrt(_mean_last(mix * mix) + EPS)
        x1_ref[...] = x_ref[...] + mod_ref[2:3, :] * (mix * r * pw_ref[...])
        _ride_wait(ride_modes, pl.program_id(0), N // TM, ride_in, ride_out, sems)

    row = lambda w: pl.BlockSpec((TM, w), lambda i: (i, 0))
    return pl.pallas_call(
        body, name="mix_out", grid=(N // TM,),
        in_specs=[row(D_MODEL), row(ATT_WIDTH), row(HG_WIDTH), _mod_spec(tps),
                  pl.BlockSpec((1, D_MODEL), lambda i: (0, 0)),
                  pl.BlockSpec((D_MODEL, D_MODEL), lambda i: (0, 0))] + [ANY_SPEC] * nr,
        out_specs=[row(D_MODEL), row(D_MODEL), row(D_MODEL)] + [ANY_SPEC] * nr,
        out_shape=[SDS((N, D_MODEL), F32), SDS((N, D_MODEL), F32), SDS((N, D_MODEL), BF16)]
        + _exchange_shapes(ride_srcs, ride_modes),
        scratch_shapes=_exchange_sems(nr),
        compiler_params=_params(("arbitrary",), VMEM_LIMIT_BIG),
    )(x2, attn_n, rec_g, mod8, post_w, w_out_bf, *ride_srcs)


def _load_weights_once(pairs, sem):
    @pl.when(pl.program_id(0) == 0)
    def _():
        cps = [pltpu.make_async_copy(src, dst, sem.at[i]) for i, (src, dst) in enumerate(pairs)]
        for cp in cps:
            cp.start()
        for cp in cps:
            cp.wait()


MLP_HALF = D_MODEL // 2
MLP_PIECES = 2 * N_DEV + 2


def _mlp_weight_pieces(wu_a, wu_b, wd_a, wd_b, wu, wd):
    cols = D_FF // N_DEV
    pairs = []
    for h, half in enumerate((wu_a, wu_b)):
        for j in range(N_DEV):
            pairs.append((half.at[j], wu.at[pl.ds(h * MLP_HALF, MLP_HALF), pl.ds(j * cols, cols)]))
    for h, half in enumerate((wd_a, wd_b)):
        pairs.append((half, wd.at[:, pl.ds(h * MLP_HALF, MLP_HALF)]))
    return pairs


def _mlp_fwd(x1, mod8, pre_w, w_up_halves, w_down_halves, T):
    N = x1.shape[0]
    TM = _tile_rows(T)
    tps = T // TM

    def body(x_ref, mod_ref, pw_ref, wua, wub, wda, wdb, u_ref, d_ref, h2_ref, wu, wd, sem):
        _load_weights_once(_mlp_weight_pieces(wua, wub, wda, wdb, wu, wd), sem)
        x = x_ref[...]
        r = lax.rsqrt(_mean_last(x * x) + EPS)
        h = (x * r * pw_ref[...]) * (1.0 + mod_ref[4:5, :]) + mod_ref[3:4, :]
        hb = _bf(h)
        h2_ref[...] = hb
        ru = jnp.maximum(_dot(hb, wu[...]), 0.0)
        u = _bf(ru * ru)
        u_ref[...] = u
        d_ref[...] = _dot(u, wd[...])

    row = lambda w: pl.BlockSpec((TM, w), lambda i: (i, 0))
    return pl.pallas_call(
        body, name="mlp_fwd", grid=(N // TM,),
        in_specs=[row(D_MODEL), _mod_spec(tps), pl.BlockSpec((1, D_MODEL), lambda i: (0, 0))] + [ANY_SPEC] * 4,
        out_specs=[row(D_FF), row(D_MODEL), row(D_MODEL)],
        out_shape=[SDS((N, D_FF), BF16), SDS((N, D_MODEL), F32), SDS((N, D_MODEL), BF16)],
        scratch_shapes=[pltpu.VMEM((D_MODEL, D_FF), BF16), pltpu.VMEM((D_FF, D_MODEL), BF16),
                        pltpu.SemaphoreType.DMA((MLP_PIECES,))],
        compiler_params=_params(("arbitrary",), VMEM_LIMIT_BIG),
    )(x1, mod8, pre_w, *w_up_halves, *w_down_halves)


def _acc_rows(acc_ref, first, rows):
    @pl.when(first)
    def _():
        acc_ref[...] = jnp.zeros(acc_ref.shape, F32)
    for i, r in enumerate(rows):
        acc_ref[i:i + 1, :] += r


def _mlp_bwd(x1, d, u, tgt, mod8, pre_w, post_w, w_up_halves, w_down_halves, T):
    N = x1.shape[0]
    TM = _tile_rows(T)
    tps = T // TM

    def body(x_ref, d_ref, u_ref, t_ref, mod_ref, pw_ref, qw_ref, wua, wub, wda, wdb,
             dx_ref, dup_ref, dd_ref, acc_ref, wd, wu, sem):
        _load_weights_once(_mlp_weight_pieces(wua, wub, wda, wdb, wu, wd), sem)
        sh2, sc2, g2 = mod_ref[3:4, :], mod_ref[4:5, :], mod_ref[5:6, :]
        x = x_ref[...]
        r1 = lax.rsqrt(_mean_last(x * x) + EPS)
        xh = x * r1
        n2 = xh * pw_ref[...]
        dv = d_ref[...]
        rd = lax.rsqrt(_mean_last(dv * dv) + EPS)
        dh = dv * rd
        rr = dh * qw_ref[...]
        e = x + g2 * rr - t_ref[...]
        loss = 0.5 * jnp.sum(_sum_rows(e * e), axis=1, keepdims=True) / D_MODEL
        dy = e * (1.0 / D_MODEL)
        dg2 = _sum_rows(dy * rr)
        drr = dy * g2
        dw_post = _sum_rows(drr * dh)
        ddh = drr * qw_ref[...]
        dd = _bf(rd * (ddh - dh * _mean_last(ddh * dh)))
        dd_ref[...] = dd
        ru = jnp.sqrt(u_ref[...].astype(F32))
        dup = _bf(_dot_nt(dd, wd[...]) * (2.0 * ru))
        dup_ref[...] = dup
        dh2 = _dot_nt(dup, wu[...])
        dsh2 = _sum_rows(dh2)
        dsc2 = _sum_rows(dh2 * n2)
        dn2 = dh2 * (1.0 + sc2)
        dw_pre = _sum_rows(dn2 * xh)
        dxh = dn2 * pw_ref[...]
        dx_ref[...] = dy + r1 * (dxh - xh * _mean_last(dxh * xh))
        _acc_rows(acc_ref, pl.program_id(0) % tps == 0,
                  [dsh2, dsc2, dg2, dw_pre, dw_post, jnp.broadcast_to(loss, (1, D_MODEL))])

    row = lambda w: pl.BlockSpec((TM, w), lambda i: (i, 0))
    vec = pl.BlockSpec((1, D_MODEL), lambda i: (0, 0))
    B = N // T
    return pl.pallas_call(
        body, name="mlp_bwd", grid=(N // TM,),
        in_specs=[row(D_MODEL), row(D_MODEL), row(D_FF), row(D_MODEL), _mod_spec(tps), vec, vec] + [ANY_SPEC] * 4,
        out_specs=[row(D_MODEL), row(D_FF), row(D_MODEL), _mod_spec(tps)],
        out_shape=[SDS((N, D_MODEL), F32), SDS((N, D_FF), BF16), SDS((N, D_MODEL), BF16),
                   SDS((B, 8, D_MODEL), F32)],
        scratch_shapes=[pltpu.VMEM((D_FF, D_MODEL), BF16), pltpu.VMEM((D_MODEL, D_FF), BF16),
                        pltpu.SemaphoreType.DMA((MLP_PIECES,))],
        compiler_params=_params(("arbitrary",), VMEM_LIMIT_BIG),
    )(x1, d, u, tgt, mod8, pre_w, post_w, *w_up_halves, *w_down_halves)


def _mix_bwd(mix, dx1, mod8, post_w, w_out_bf, T, ride_srcs, ride_modes):
    N = mix.shape[0]
    TM = _tile_rows(T, big=True)
    tps = T // TM
    nr = len(ride_srcs)

    def body(*refs):
        mix_ref, dx_ref, mod_ref, pw_ref, w_ref = refs[:5]
        ride_in = refs[5:5 + nr]
        dan_ref, drg_ref, dmix_ref, acc_ref = refs[5 + nr:9 + nr]
        ride_out = refs[9 + nr:9 + 2 * nr]
        sems = refs[9 + 2 * nr:]
        _ride_start(ride_modes, pl.program_id(0), N // TM, ride_in, ride_out, sems)
        g1 = mod_ref[2:3, :]
        mix = mix_ref[...]
        dx1 = dx_ref[...]
        rm = lax.rsqrt(_mean_last(mix * mix) + EPS)
        mh = mix * rm
        dg1 = _sum_rows(dx1 * (mh * pw_ref[...]))
        dr = dx1 * g1
        dw_post = _sum_rows(dr * mh)
        dmh = dr * pw_ref[...]
        dmix = _bf(rm * (dmh - mh * _mean_last(dmh * mh)))
        dmix_ref[...] = dmix
        dcat = _dot_nt(dmix, w_ref[...])
        dan_ref[...] = dcat[:, :ATT_WIDTH]
        drg_ref[...] = dcat[:, ATT_WIDTH:]
        _acc_rows(acc_ref, pl.program_id(0) % tps == 0, [dg1, dw_post])
        _ride_wait(ride_modes, pl.program_id(0), N // TM, ride_in, ride_out, sems)

    row = lambda w: pl.BlockSpec((TM, w), lambda i: (i, 0))
    B = N // T
    return pl.pallas_call(
        body, name="mix_bwd", grid=(N // TM,),
        in_specs=[row(D_MODEL), row(D_MODEL), _mod_spec(tps), pl.BlockSpec((1, D_MODEL), lambda i: (0, 0)),
                  pl.BlockSpec((D_MODEL, D_MODEL), lambda i: (0, 0))] + [ANY_SPEC] * nr,
        out_specs=[row(ATT_WIDTH), row(HG_WIDTH), row(D_MODEL), _mod_spec(tps)] + [ANY_SPEC] * nr,
        out_shape=[SDS((N, ATT_WIDTH), F32), SDS((N, HG_WIDTH), F32), SDS((N, D_MODEL), BF16),
                   SDS((B, 8, D_MODEL), F32)] + _exchange_shapes(ride_srcs, ride_modes),
        scratch_shapes=_exchange_sems(nr),
        compiler_params=_params(("arbitrary",), VMEM_LIMIT_BIG),
    )(mix, dx1, mod8, post_w, w_out_bf, *ride_srcs)


def _hgrn_bwd(proj_h, lb, hg_w, o, s_prev, drg, ride_srcs, ride_modes):
    B, T, _ = proj_h.shape
    ng = T // HG_ROWS
    nr = len(ride_srcs)

    def body(*refs):
        hq_ref, hf_ref, hi_ref, hg_ref, lb_ref, gw_ref, o_ref, sp_ref, drg_ref = refs[:9]
        ride_in = refs[9:9 + nr]
        dhq_ref, dhf_ref, dhi_ref, dhg_ref, dlb_ref, dgw_ref = refs[9 + nr:15 + nr]
        ride_out = refs[15 + nr:15 + 2 * nr]
        dst = refs[15 + 2 * nr]
        sems = refs[16 + 2 * nr:]
        step = pl.program_id(0) * ng + pl.program_id(1)
        _ride_start(ride_modes, step, B * ng, ride_in, ride_out, sems)

        @pl.when(pl.program_id(1) == 0)
        def _():
            dst[...] = jnp.zeros(dst.shape, F32)
            dlb_ref[...] = jnp.zeros(dlb_ref.shape, F32)
            dgw_ref[...] = jnp.zeros(dgw_ref.shape, F32)

        lo, up = _group_masks()
        lower_bf, upper_bf = _ones_bf(lo), _ones_bf(up)
        row_chunk = _row_chunk()
        gw = gw_ref[...]

        for h in range(HG_HEADS):
            lanes = _head_lanes(h)
            lbv = lb_ref[:, lanes]
            hq = hq_ref[:, lanes]
            gt = _hgrn_gates(hq, hf_ref[:, lanes], lbv, lower_bf)
            sq, sg, qdf, kdf, k2f, ebl = gt["sq"], gt["sg"], gt["qd"], gt["kd"], gt["k2"], gt["ebl"]
            v, qd, kd = _bf(hi_ref[:, lanes]), _bf(qdf), _bf(kdf)
            ov = o_ref[:, lanes]
            hg = hg_ref[:, lanes]
            shg = _sigmoid(hg)
            dr = drg_ref[:, lanes]
            ro = lax.rsqrt(_mean_last(ov * ov) + EPS)
            oh = ov * ro
            dhg_ref[:, lanes] = _bf(dr * (oh * gw) * (shg + hg * shg * (1.0 - shg)))
            drn = dr * (hg * shg)
            dgw_ref[...] += jnp.broadcast_to(_sum_rows(drn * oh), (8, LANES))
            doh = drn * gw
            do = _bf(ro * (doh - oh * _mean_last(doh * oh)))
            a = jnp.where(lo, _dot_nt(qd, kd), 0.0)
            da = _bf(jnp.where(lo, _dot_nt(do, v), 0.0))
            dv = _dot_tn(_bf(a), do)
            dqd = _dot(da, kd)
            dkd = _dot_tn(da, qd)
            sp = sp_ref[h]
            incr = _dot_tn(do, _bf(_spread(qdf, row_chunk)))
            ds = dst[h]
            after = [None] * HG_GROUP
            for c in reversed(range(HG_GROUP)):
                after[c] = ds
                ds = ds * ebl[c] + _lane_block(incr, c)
            dst[h] = ds
            dss = jnp.concatenate(after, axis=1)
            dssb = _bf(dss)
            dk2 = _pick(_dot(v, dssb), row_chunk)
            dhi_ref[:, lanes] = _bf(dv + _dot_nt(_bf(_spread(k2f, row_chunk)), dssb))
            dqd = dqd + _pick(_dot(do, _bf(sp)), row_chunk)
            debl = _sum_rows(dss * sp)
            k2g = dk2 * k2f
            db = dqd * qdf - dkd * kdf - k2g
            dk = dkd * gt["enb"] + dk2 * gt["e2"]
            dbl = _chunk_bcast([_lane_block(debl, c) * ebl[c] + _sum_rows(k2g[_chunk_rows(c), :])
                                for c in range(HG_GROUP)])
            dg = _tri_sum(upper_bf, db, terms=2) + dbl
            df = dg / gt["f"] - dk
            dhf_ref[:, lanes] = _bf(df * (1.0 - lbv) * sg * (1.0 - sg))
            dlb_ref[:, lanes] += jnp.broadcast_to(_sum_rows(df * (1.0 - sg)), (8, LANES))
            dhq_ref[:, lanes] = _bf((dqd * gt["eb"]) * (sq + hq * sq * (1.0 - sq)))
        _ride_wait(ride_modes, step, B * ng, ride_in, ride_out, sems)

    part = lambda j: pl.BlockSpec((None, HG_ROWS, HG_WIDTH), lambda b, g: (b, ng - 1 - g, j))
    return pl.pallas_call(
        body, name="hgrn_bwd", grid=(B, ng),
        in_specs=[part(0), part(1), part(2), part(3),
                  pl.BlockSpec((1, HG_WIDTH), lambda b, g: (0, 0)),
                  pl.BlockSpec((1, LANES), lambda b, g: (0, 0)),
                  part(0),
                  pl.BlockSpec((None, HG_HEADS, None, HG_HEAD_DIM, HG_STACK), lambda b, g: (b, 0, ng - 1 - g, 0, 0)),
                  part(0)] + [ANY_SPEC] * nr,
        out_specs=[part(0), part(0), part(0), part(0),
                   pl.BlockSpec((None, 8, HG_WIDTH), lambda b, g: (b, 0, 0)),
                   pl.BlockSpec((None, 8, LANES), lambda b, g: (b, 0, 0))] + [ANY_SPEC] * nr,
        out_shape=[SDS((B, T, HG_WIDTH), BF16)] * 4 + [SDS((B, 8, HG_WIDTH), F32), SDS((B, 8, LANES), F32)]
        + _exchange_shapes(ride_srcs, ride_modes),
        scratch_shapes=[pltpu.VMEM((HG_HEADS, HG_HEAD_DIM, HG_HEAD_DIM), F32)] + _exchange_sems(nr),
        compiler_params=_params(("arbitrary", "arbitrary"), VMEM_LIMIT_BIG),
    )(proj_h, proj_h, proj_h, proj_h, lb, hg_w, o, s_prev, drg, *ride_srcs)


def _attn_bwd(qr, kr, proj3, attn_o, dan, tables, sinks, attn_w, ride_srcs, ride_modes):
    B, T, _ = proj3.shape
    nb = T // WINDOW
    splits = min(ATT_SPLITS, nb)
    per = nb // splits
    nr = len(ride_srcs)
    cos, sinl, sinr = tables
    QKV = ATT_WIDTH + 2 * LANES

    def body(*refs):
        qr_ref, kr_ref, v_ref, o_ref, dan_ref, cos_ref, sl_ref, sr_ref, sink_ref, aw_ref = refs[:10]
        ride_in = refs[10:10 + nr]
        dqkv_ref, dsink_ref, daw_ref = refs[10 + nr:13 + nr]
        ride_out = refs[13 + nr:13 + 2 * nr]
        kpad, vpad, dkpad, dvpad, dqb, dsk = refs[13 + 2 * nr:19 + 2 * nr]
        sems = refs[19 + 2 * nr:]
        part = pl.program_id(1)
        step = pl.program_id(0) * splits + part
        _ride_start(ride_modes, step, B * splits, ride_in, ride_out, sems)

        @pl.when(part == 0)
        def _():
            kpad[0:WINDOW, :] = jnp.zeros((WINDOW, LANES), BF16)
            vpad[0:WINDOW, :] = jnp.zeros((WINDOW, LANES), BF16)
            kpad[WINDOW:, :] = kr_ref[...]
            vpad[WINDOW:, :] = _bf(v_ref[...])
            dkpad[...] = jnp.zeros(dkpad.shape, F32)
            dvpad[...] = jnp.zeros(dvpad.shape, F32)
            dsk[...] = jnp.zeros(dsk.shape, F32)
            daw_ref[...] = jnp.zeros(daw_ref.shape, F32)

        lower = _lower_mask()
        aw = aw_ref[...]

        def block(n, daw):
            r0 = pl.multiple_of(n * WINDOW, WINDOW)
            rows = pl.ds(r0, WINDOW)
            nxt = pl.ds(r0 + WINDOW, WINDOW)
            ob = o_ref[rows, :]
            dn = dan_ref[rows, :]
            ro = lax.rsqrt(_mean_last(ob * ob) + EPS)
            oh = ob * ro
            daw = daw + _sum_rows(dn * oh)
            doh = dn * aw
            do = _bf(ro * (doh - oh * _mean_last(doh * oh)))
            doparts = [do[:, j * LANES:(j + 1) * LANES] for j in range(ATT_WIDTH // LANES)]
            qparts = [qr_ref[rows, j * LANES:(j + 1) * LANES] for j in range(ATT_WIDTH // LANES)]
            for hk in range(ATT_KV_HEADS):
                lanes = slice(hk * ATT_HEAD_DIM, (hk + 1) * ATT_HEAD_DIM)
                qs = _stack_heads(qparts, hk)
                dos = _stack_heads(doparts, hk)
                k_cur, k_prev = kpad[nxt, lanes], kpad[rows, lanes]
                v_cur, v_prev = vpad[nxt, lanes], vpad[rows, lanes]
                p, inv, es = _softmax_window(qs, k_cur, k_prev, lower, n > 0, _sink_row(sink_ref, hk))
                p = p * inv
                dp = jnp.where(lower, _dot_nt(v_cur, dos), _dot_nt(v_prev, dos))
                delta = jnp.sum(p * dp, axis=0, keepdims=True)
                ds = p * (dp - delta)
                sk = (es * inv) * delta
                ds_cur = jnp.where(lower, ds, 0.0)
                p_cur = jnp.where(lower, p, 0.0)
                ds_cur, ds_prev = _bf(ds_cur), _bf(ds - ds_cur)
                p_cur, p_prev = _bf(p_cur), _bf(p - p_cur)
                dqt = (_dot_tn(k_cur, ds_cur) + _dot_tn(k_prev, ds_prev)) * ATT_SCALE
                dkpad[nxt, lanes] += _dot(ds_cur, qs)
                dkpad[rows, lanes] += _dot(ds_prev, qs)
                dvpad[nxt, lanes] += _dot(p_cur, dos)
                dvpad[rows, lanes] += _dot(p_prev, dos)
                for g in range(ATT_GROUP):
                    h = ATT_GROUP * hk + g
                    cols = slice(g * WINDOW, (g + 1) * WINDOW)
                    dqb[:, h * ATT_HEAD_DIM:(h + 1) * ATT_HEAD_DIM] = dqt[:, cols].T
                    dsk[h:h + 1, :] += jnp.broadcast_to(-jnp.sum(sk[:, cols], axis=1, keepdims=True), (1, LANES))
            cs, sl, sr = cos_ref[rows, :], sl_ref[rows, :], sr_ref[rows, :]
            for j in range(ATT_WIDTH // LANES):
                dqkv_ref[rows, j * LANES:(j + 1) * LANES] = _bf(_rope_t(dqb[:, j * LANES:(j + 1) * LANES], cs, sl, sr))
            return daw

        daw = _loop_pairs(part * per, per, block, jnp.zeros((1, ATT_WIDTH), F32))
        daw_ref[...] += jnp.broadcast_to(daw, (8, ATT_WIDTH))
        dsink_ref[...] = dsk[...]

        def finish(n, carry):
            r0 = pl.multiple_of(n * WINDOW, WINDOW)
            rows = pl.ds(r0, WINDOW)
            nxt = pl.ds(r0 + WINDOW, WINDOW)
            cs, sl, sr = cos_ref[rows, :], sl_ref[rows, :], sr_ref[rows, :]
            dqkv_ref[rows, ATT_WIDTH:ATT_WIDTH + LANES] = _bf(_rope_t(dkpad[nxt, :], cs, sl, sr))
            dqkv_ref[rows, ATT_WIDTH + LANES:QKV] = _bf(dvpad[nxt, :])
            return carry

        @pl.when(part == splits - 1)
        def _():
            lax.fori_loop(0, nb, finish, 0)

        _ride_wait(ride_modes, step, B * splits, ride_in, ride_out, sems)

    seq = lambda w, j: pl.BlockSpec((None, T, w), lambda b, s: (b, 0, j))
    full = lambda r, w: pl.BlockSpec((r, w), lambda b, s: (0, 0))
    return pl.pallas_call(
        body, name="attn_bwd", grid=(B, splits),
        in_specs=[seq(ATT_WIDTH, 0), seq(LANES, 0), seq(LANES, 5), seq(ATT_WIDTH, 0), seq(ATT_WIDTH, 0),
                  full(T, LANES), full(T, LANES), full(T, LANES),
                  pl.BlockSpec(memory_space=pltpu.SMEM), full(1, ATT_WIDTH)] + [ANY_SPEC] * nr,
        out_specs=[seq(QKV, 0), pl.BlockSpec((None, 8, LANES), lambda b, s: (b, 0, 0)),
                   pl.BlockSpec((None, 8, ATT_WIDTH), lambda b, s: (b, 0, 0))] + [ANY_SPEC] * nr,
        out_shape=[SDS((B, T, QKV), BF16), SDS((B, 8, LANES), F32), SDS((B, 8, ATT_WIDTH), F32)]
        + _exchange_shapes(ride_srcs, ride_modes),
        scratch_shapes=[pltpu.VMEM((T + WINDOW, LANES), BF16), pltpu.VMEM((T + WINDOW, LANES), BF16),
                        pltpu.VMEM((T + WINDOW, LANES), F32), pltpu.VMEM((T + WINDOW, LANES), F32),
                        pltpu.VMEM((WINDOW, ATT_WIDTH), F32), pltpu.VMEM((8, LANES), F32)] + _exchange_sems(nr),
        compiler_params=_params(("arbitrary", "arbitrary"), VMEM_LIMIT_BIG),
    )(qr, kr, proj3, attn_o, dan, cos, sinl, sinr, sinks, attn_w, *ride_srcs)


def _in_bwd(x2, dx1, dqkv, dhq, dhf, dhi, dhg, mod8, pre_w, w_in_bf, T, ride_srcs, ride_modes):
    N = x2.shape[0]
    TM = _tile_rows(T, big=True)
    tps = T // TM
    nr = len(ride_srcs)
    pieces = [(0, ATT_WIDTH + 2 * LANES), (768, HG_WIDTH), (1280, HG_WIDTH), (1792, HG_WIDTH), (2304, HG_WIDTH)]

    def body(*refs):
        x_ref, dx_ref, p0, p1, p2, p3, p4, mod_ref, pw_ref, w_ref = refs[:10]
        ride_in = refs[10:10 + nr]
        gx_ref, dproj_ref, acc_ref = refs[10 + nr:13 + nr]
        ride_out = refs[13 + nr:13 + 2 * nr]
        sems = refs[13 + 2 * nr:]
        _ride_start(ride_modes, pl.program_id(0), N // TM, ride_in, ride_out, sems)
        sc1 = mod_ref[1:2, :]
        dh = jnp.zeros((TM, D_MODEL), F32)
        for ref, (off, width) in zip((p0, p1, p2, p3, p4), pieces):
            pb = ref[...]
            dproj_ref[:, off:off + width] = pb
            dh = dh + _dot(pb, w_ref[off:off + width, :])
        x = x_ref[...]
        r = lax.rsqrt(_mean_last(x * x) + EPS)
        xh = x * r
        n1 = xh * pw_ref[...]
        dsh1 = _sum_rows(dh)
        dsc1 = _sum_rows(dh * n1)
        dn1 = dh * (1.0 + sc1)
        dw_pre = _sum_rows(dn1 * xh)
        dxh = dn1 * pw_ref[...]
        gx_ref[...] = dx_ref[...] + r * (dxh - xh * _mean_last(dxh * xh))
        _acc_rows(acc_ref, pl.program_id(0) % tps == 0, [dsh1, dsc1, dw_pre])
        _ride_wait(ride_modes, pl.program_id(0), N // TM, ride_in, ride_out, sems)

    row = lambda w: pl.BlockSpec((TM, w), lambda i: (i, 0))
    B = N // T
    return pl.pallas_call(
        body, name="in_bwd", grid=(N // TM,),
        in_specs=[row(D_MODEL), row(D_MODEL), row(768), row(HG_WIDTH), row(HG_WIDTH), row(HG_WIDTH),
                  row(HG_WIDTH), _mod_spec(tps), pl.BlockSpec((1, D_MODEL), lambda i: (0, 0)),
                  pl.BlockSpec((IN_COLS, D_MODEL), lambda i: (0, 0))] + [ANY_SPEC] * nr,
        out_specs=[row(D_MODEL), row(IN_COLS), _mod_spec(tps)] + [ANY_SPEC] * nr,
        out_shape=[SDS((N, D_MODEL), F32), SDS((N, IN_COLS), BF16), SDS((B, 8, D_MODEL), F32)]
        + _exchange_shapes(ride_srcs, ride_modes),
        scratch_shapes=_exchange_sems(nr),
        compiler_params=_params(("arbitrary",), VMEM_LIMIT_BIG),
    )(x2, dx1, dqkv, dhq, dhf, dhi, dhg, mod8, pre_w, w_in_bf, *ride_srcs)


def _matmul_tn(name, a, b, tn, tm=512, by_owner_cols=False):
    K, M = a.shape
    Nc = b.shape[1]
    tm = min(tm, M)

    def body(a_ref, b_ref, o_ref):
        o_ref[...] = _bf(_dot_tn(a_ref[...], b_ref[...]))

    if by_owner_cols:
        assert tn * N_DEV == Nc
        out_shape = SDS((N_DEV, M, tn), BF16)
        out_spec = pl.BlockSpec((None, tm, tn), lambda i, j: (j, i, 0))
    else:
        out_shape = SDS((M, Nc), BF16)
        out_spec = pl.BlockSpec((tm, tn), lambda i, j: (i, j))
    return pl.pallas_call(
        body, name=name, grid=(M // tm, Nc // tn),
        in_specs=[pl.BlockSpec((K, tm), lambda i, j: (0, i)),
                  pl.BlockSpec((K, tn), lambda i, j: (0, j))],
        out_specs=out_spec, out_shape=out_shape,
        compiler_params=_params(("arbitrary", "arbitrary"), VMEM_LIMIT_BIG),
    )(a, b)


def _adamw_math(w, g, m, v):
    m2 = ADAM_B1 * m + (1.0 - ADAM_B1) * g
    v2 = ADAM_B2 * v + (1.0 - ADAM_B2) * (g * g)
    m_hat = m2 / (1.0 - ADAM_B1 ** ADAM_STEP)
    v_hat = v2 / (1.0 - ADAM_B2 ** ADAM_STEP)
    delta = -ADAM_LR * (m_hat / (jnp.sqrt(v_hat) + ADAM_EPS) + ADAM_WD * w)
    return delta, m2, v2


def _pair_add(name, gw, theirs):
    chips, _, r, c = gw.shape
    tr = r
    core = lax.axis_index("c").astype(jnp.int32).reshape(1)

    def body(core_ref, mine_ref, theirs_ref, o_ref):
        o_ref[...] = _bf(mine_ref[...].astype(F32) + theirs_ref[...].astype(F32))

    block = pl.BlockSpec((None, tr, c), lambda s, i, core_ref: (s, i, 0))
    grid_spec = pltpu.PrefetchScalarGridSpec(
        num_scalar_prefetch=1, grid=(chips, r // tr),
        in_specs=[pl.BlockSpec((None, None, tr, c), lambda s, i, core_ref: (s, core_ref[0], i, 0)), block],
        out_specs=block)
    return pl.pallas_call(
        body, name=name, grid_spec=grid_spec, out_shape=SDS((chips, r, c), BF16),
        compiler_params=_params(("arbitrary", "arbitrary")),
    )(core, gw, theirs)


def _reduce_adamw(name, parts, w, m, v):
    r, c = w.shape
    tr = r if r % 256 else 256
    slots = parts.shape[0]

    def body(p_ref, w_ref, m_ref, v_ref, g_ref, d_ref, m2_ref, v2_ref):
        g = p_ref[0].astype(F32)
        for s in range(1, slots):
            g = g + p_ref[s].astype(F32)
        g_ref[...] = g
        d_ref[...], m2_ref[...], v2_ref[...] = _adamw_math(w_ref[...], g, m_ref[...], v_ref[...])

    blk = pl.BlockSpec((tr, c), lambda i: (i, 0))
    return pl.pallas_call(
        body, name=name, grid=(r // tr,),
        in_specs=[pl.BlockSpec((slots, tr, c), lambda i: (0, i, 0)), blk, blk, blk],
        out_specs=[blk] * 4, out_shape=[SDS((r, c), F32)] * 4,
        compiler_params=_params(("arbitrary",), VMEM_LIMIT_BIG),
    )(parts, w, m, v)


def _ada_grad_adamw(c_all, dmod_all, w, m, v):
    r, c = w.shape
    tr = 256
    nb = c_all.shape[0]

    def body(c_ref, dm_ref, w_ref, m_ref, v_ref, g_ref, d_ref, m2_ref, v2_ref):
        cv = c_ref[...]
        g = _dot_tn(cv * _sigmoid(cv), dm_ref[...])
        g_ref[...] = g
        d_ref[...], m2_ref[...], v2_ref[...] = _adamw_math(w_ref[...], g, m_ref[...], v_ref[...])

    blk = pl.BlockSpec((tr, c), lambda i: (i, 0))
    return pl.pallas_call(
        body, name="ada_grad_adamw", grid=(r // tr,),
        in_specs=[pl.BlockSpec((nb, tr), lambda i: (0, i)), pl.BlockSpec((nb, c), lambda i: (0, 0)),
                  blk, blk, blk],
        out_specs=[blk] * 4, out_shape=[SDS((r, c), F32)] * 4,
        compiler_params=_params(("arbitrary",)),
    )(c_all, dmod_all, w, m, v)


_SMALL = [("b_ada", 6144), ("pre_w_mix", 1024), ("attn_sinks", 128), ("attn_out_w", 512), ("lb_table", 1024),
          ("hg_norm_w", 128), ("post_w_mix", 1024), ("pre_w_mlp", 1024), ("post_w_mlp", 1024)]


def _pack_small(vals, loss_part):
    out = []
    for name, width in _SMALL:
        f = vals[name].reshape(-1).astype(F32)
        out.append(jnp.pad(f, (0, width - f.shape[0])))
    out.append(jnp.broadcast_to(loss_part, (LANES,)))
    return jnp.concatenate(out).reshape(1, -1)


def _adamw_small(parts, given):
    names = [n for n, _ in _SMALL]
    flat_in = [a for n in names for a in given[n]]

    def body(*refs):
        p_ref = refs[0]
        in_refs = refs[1:1 + 3 * len(names)]
        out_refs = refs[1 + 3 * len(names):-1]
        loss_ref = refs[-1]
        g = p_ref[0]
        for s in range(1, N_DEV):
            g = g + p_ref[s]
        off = 0
        for i, (name, width) in enumerate(_SMALL):
            w_ref, m_ref, v_ref = in_refs[3 * i:3 * i + 3]
            rows, cols = w_ref.shape
            for r in range(rows):
                gr = g[:, off + r * cols:off + (r + 1) * cols]
                res = (gr,) + _adamw_math(w_ref[r:r + 1, :], gr, m_ref[r:r + 1, :], v_ref[r:r + 1, :])
                for o_ref, val in zip(out_refs[4 * i:4 * i + 4], res):
                    o_ref[r:r + 1, :] = val
            off += width
        loss_ref[...] = g[:, off:off + LANES]

    out_shape = [SDS(given[n][0].shape, F32) for n in names for _ in range(4)] + [SDS((1, LANES), F32)]
    outs = pl.pallas_call(body, name="adamw_small", out_shape=out_shape)(parts, *flat_in)
    return {n: tuple(outs[4 * i:4 * i + 4]) for i, n in enumerate(names)}, outs[-1][0, 0]


def kernel(x, c, w_ada, b_ada, pre_w_mix, w_in, attn_sinks, attn_out_w, lb_table, hg_norm_w, w_out, post_w_mix, pre_w_mlp, w_up, w_down, post_w_mlp, loss_target, m_w_ada, m_b_ada, m_pre_w_mix, m_w_in, m_attn_sinks, m_attn_out_w, m_lb_table, m_hg_norm_w, m_w_out, m_post_w_mix, m_pre_w_mlp, m_w_up, m_w_down, m_post_w_mlp, v_w_ada, v_b_ada, v_pre_w_mix, v_w_in, v_attn_sinks, v_attn_out_w, v_lb_table, v_hg_norm_w, v_w_out, v_post_w_mix, v_pre_w_mlp, v_w_up, v_w_down, v_post_w_mlp):
    B, T, _ = x.shape
    N = B * T
    me = 4 * lax.axis_index("x") + 2 * lax.axis_index("y") + lax.axis_index("c")
    x2 = x.reshape(N, D_MODEL)
    tgt2 = loss_target.reshape(N, D_MODEL)

    w_in_t, m_w_in_t, v_w_in_t = w_in[0].T, m_w_in[0].T, v_w_in[0].T
    w_in_g, c_g = _exchange("gather_w_in", [_bf(w_in_t), c], ["gather"] * 2)
    w_in_f = w_in_g.reshape(IN_COLS, D_MODEL)
    c_all = c_g.reshape(N_DEV * B, D_MODEL)

    ada_cols = w_ada.shape[2]
    b_mine = lax.dynamic_slice(b_ada, (0, me * ada_cols), (1, ada_cols))
    mod_cols = _ada_mod(c_all, w_ada[0], b_mine)
    (mod_g,) = _exchange("scatter_mod", [mod_cols.reshape(N_DEV, B, ada_cols)], ["a2a"])
    mod = mod_g.transpose(1, 0, 2).reshape(B, 6, D_MODEL)
    mod8 = jnp.pad(mod, ((0, 0), (0, 2), (0, 0)))

    lb_p = jax.nn.softmax(lb_table, axis=0)
    lb = lb_p[1:2]
    tables = _rope_tables(T)

    w_up_b, w_down_b = _bf(w_up[0]), _bf(w_down[0])
    proj_a, proj_h, h1, w_out_g = _in_proj(x2, mod8, pre_w_mix, w_in_f, T, [_bf(w_out[0])], ["gather"])
    proj3 = proj_a.reshape(B, T, ATT_COLS)
    proj_h = proj_h.reshape(B, T, IN_COLS - ATT_COLS)
    rec_o, rec_g, s_prev, w_up_g0, w_up_g1 = _hgrn_fwd(proj_h, lb, hg_norm_w,
                                                       [w_up_b[:MLP_HALF], w_up_b[MLP_HALF:]], ["gather"] * 2)
    attn_o, attn_n, qr, kr, w_down_g0 = _attn_fwd(proj3, tables, attn_sinks, attn_out_w,
                                                  [w_down_b[:, :MLP_HALF]], ["gather"])
    w_out_f = w_out_g.reshape(D_MODEL, D_MODEL)
    mix, x1, cat, w_down_g1 = _mix_out(x2, attn_n.reshape(N, ATT_WIDTH), rec_g.reshape(N, HG_WIDTH), mod8,
                                       post_w_mix, w_out_f, T, [w_down_b[:, MLP_HALF:]], ["gather"])
    w_up_halves = [w_up_g0, w_up_g1]
    w_down_halves = [w_down_g0.reshape(D_FF, MLP_HALF), w_down_g1.reshape(D_FF, MLP_HALF)]
    u, d, h2 = _mlp_fwd(x1, mod8, pre_w_mlp, w_up_halves, w_down_halves, T)

    dx1, dup, dd, acc_mlp = _mlp_bwd(x1, d, u, tgt2, mod8, pre_w_mlp, post_w_mlp, w_up_halves, w_down_halves, T)
    chips = N_DEV // 2
    by_chip = lambda a: a.reshape((chips, 2, a.shape[0] // N_DEV) + a.shape[1:])
    gw_up = _matmul_tn("grad_w_up", h2, dup, D_FF // N_DEV, by_owner_cols=True)
    gw_up = gw_up.reshape(chips, 2, D_MODEL, D_FF // N_DEV)
    gw_down = by_chip(_matmul_tn("grad_w_down", u, dd, 512))
    dan, drg, dmix, acc_mix, q_down, q_up = _mix_bwd(mix, dx1, mod8, post_w_mix, w_out_f, T,
                                                     [gw_down, gw_up], ["pair"] * 2)
    p_down, p_up = _pair_add("pair_add_w_down", gw_down, q_down), _pair_add("pair_add_w_up", gw_up, q_up)
    gw_out = _matmul_tn("grad_w_out", cat, dmix, 512).reshape(N_DEV, D_MODEL // N_DEV, D_MODEL)
    dhq, dhf, dhi, dhg, dlb_p, dgw_p, r_down, r_up = _hgrn_bwd(
        proj_h, lb, hg_norm_w, rec_o, s_prev, drg.reshape(B, T, HG_WIDTH), [p_down, p_up], ["chips"] * 2)
    dqkv, dsink_p, daw_p, r_out = _attn_bwd(qr, kr, proj3, attn_o, dan.reshape(B, T, ATT_WIDTH), tables,
                                            attn_sinks, attn_out_w, [gw_out], ["a2a"])
    flat = lambda a: a.reshape(N, a.shape[-1])
    grad_x, dproj, acc_in = _in_bwd(x2, dx1, flat(dqkv), flat(dhq), flat(dhf), flat(dhi), flat(dhg),
                                    mod8, pre_w_mix, w_in_f, T, [], [])

    gw_in = by_chip(_matmul_tn("grad_w_in", dproj, h1, 512, tm=IN_COLS // 2))
    (q_in,) = _exchange("pair_w_in", [gw_in], ["pair"])
    p_in = _pair_add("pair_add_w_in", gw_in, q_in)

    dmod = jnp.concatenate([acc_in[:, 0:2], acc_mix[:, 0:1], acc_mlp[:, 0:3]], axis=1)
    dlb = dlb_p[:, 0].sum(0)
    dlb_table = jnp.stack([-dlb, dlb]) * (lb_p[0] * lb_p[1])[None, :]
    small = {
        "b_ada": dmod.sum(0),
        "pre_w_mix": acc_in[:, 2].sum(0),
        "attn_sinks": dsink_p[:, :, 0].sum(0),
        "attn_out_w": daw_p[:, 0].sum(0),
        "lb_table": dlb_table,
        "hg_norm_w": dgw_p[:, 0].sum(0),
        "post_w_mix": acc_mix[:, 1].sum(0),
        "pre_w_mlp": acc_mlp[:, 3].sum(0),
        "post_w_mlp": acc_mlp[:, 4].sum(0),
    }
    loss_part = acc_mlp[:, 5, 0].sum()
    dmod_blocks = dmod.reshape(B, N_DEV, ada_cols).transpose(1, 0, 2)

    r_in, r_dmod, r_small = _exchange(
        "reduce_grads", [p_in, dmod_blocks, _pack_small(small, loss_part)], ["chips", "a2a", "gather"])

    res = {}
    res["w_in"] = tuple(a.T for a in _reduce_adamw("adamw_w_in", r_in, w_in_t, m_w_in_t, v_w_in_t))
    res["w_out"] = _reduce_adamw("adamw_w_out", r_out, w_out[0], m_w_out[0], v_w_out[0])
    res["w_up"] = _reduce_adamw("adamw_w_up", r_up, w_up[0], m_w_up[0], v_w_up[0])
    res["w_down"] = _reduce_adamw("adamw_w_down", r_down, w_down[0], m_w_down[0], v_w_down[0])
    res["w_ada"] = _ada_grad_adamw(c_all, r_dmod.reshape(N_DEV * B, ada_cols), w_ada[0], m_w_ada[0], v_w_ada[0])

    given = dict(b_ada=(b_ada, m_b_ada, v_b_ada), pre_w_mix=(pre_w_mix, m_pre_w_mix, v_pre_w_mix),
                 attn_sinks=(attn_sinks, m_attn_sinks, v_attn_sinks),
                 attn_out_w=(attn_out_w, m_attn_out_w, v_attn_out_w), lb_table=(lb_table, m_lb_table, v_lb_table),
                 hg_norm_w=(hg_norm_w, m_hg_norm_w, v_hg_norm_w), post_w_mix=(post_w_mix, m_post_w_mix, v_post_w_mix),
                 pre_w_mlp=(pre_w_mlp, m_pre_w_mlp, v_pre_w_mlp), post_w_mlp=(post_w_mlp, m_post_w_mlp, v_post_w_mlp))
    small_res, loss = _adamw_small(r_small, given)
    res.update(small_res)

    order = ["w_ada", "b_ada", "pre_w_mix", "w_in", "attn_sinks", "attn_out_w", "lb_table", "hg_norm_w", "w_out",
             "post_w_mix", "pre_w_mlp", "w_up", "w_down", "post_w_mlp"]
    big = {"w_ada", "w_in", "w_out", "w_up", "w_down"}
    outs = [loss, grad_x.reshape(B, T, D_MODEL)]
    for i in range(4):
        for k in order:
            a = res[k][i]
            outs.append(a[None] if k in big else a)
    return tuple(outs)
```

```python
import jax
import jax.numpy as jnp
import numpy as np
from jax import lax
from jax.experimental import pallas as pl
from jax.experimental.pallas import tpu as pltpu

F32 = jnp.float32
BF16 = jnp.bfloat16
SDS = jax.ShapeDtypeStruct

D_MODEL = 1024
ATT_WIDTH = 512
ATT_HEAD_DIM = 64
ATT_KV_HEADS = 2
ATT_GROUP = 4
WINDOW = 128
ROPE_DIM = 16
ROPE_THETA = 500000.0
HG_WIDTH = 512
HG_HEAD_DIM = 128
HG_HEADS = 4
HG_CHUNK = 32
IN_COLS = 2816
ATT_COLS = 768
D_FF = 4096
EPS = 1e-6
N_DEV = 8

ADAM_LR = 0.001
ADAM_B1 = 0.9
ADAM_B2 = 0.999
ADAM_EPS = 1e-08
ADAM_WD = 0.01
ADAM_STEP = 10

VMEM_LIMIT_BIG = 56 << 20
LANES = 128

MESH = pl.DeviceIdType.MESH
NT_DIMS = (((1,), (1,)), ((), ()))
TN_DIMS = (((0,), (0,)), ((), ()))


def _dot(a, b):
    return jnp.dot(a, b, preferred_element_type=F32)


def _dot_nt(a, b):
    return lax.dot_general(a, b, NT_DIMS, preferred_element_type=F32)


def _dot_tn(a, b):
    return lax.dot_general(a, b, TN_DIMS, preferred_element_type=F32)


def _bf(a):
    return a.astype(BF16)


def _sigmoid(a):
    return 1.0 / (1.0 + jnp.exp(-a))


def _mean_last(a):
    return jnp.mean(a, axis=-1, keepdims=True)


def _sum_rows(a):
    return jnp.sum(a, axis=0, keepdims=True)


def _tri_sum(tri_bf, a, terms=3):
    a1 = _bf(a)
    r1 = a - a1.astype(F32)
    a2 = _bf(r1)
    out = _dot(tri_bf, a1) + _dot(tri_bf, a2)
    if terms == 3:
        out = out + _dot(tri_bf, _bf(r1 - a2.astype(F32)))
    return out


def _loop_pairs(first, count, body, init, per_trip=2):
    if count % per_trip:
        return lax.fori_loop(first, first + count, body, init)

    def trip(i, c):
        for k in range(per_trip):
            c = body(first + per_trip * i + k, c)
        return c

    return lax.fori_loop(0, count // per_trip, trip, init)


def _params(sem=None, vmem=None):
    kw = {}
    if sem is not None:
        kw["dimension_semantics"] = sem
    if vmem is not None:
        kw["vmem_limit_bytes"] = vmem
    return pltpu.CompilerParams(**kw)


ANY_SPEC = pl.BlockSpec(memory_space=pl.ANY)


def _exchange_shapes(srcs, modes):
    out_shape = []
    for s, m in zip(srcs, modes):
        shp = {"gather": (N_DEV,) + tuple(s.shape), "pair": (s.shape[0],) + tuple(s.shape[2:])}.get(m, tuple(s.shape))
        out_shape.append(SDS(shp, s.dtype))
    return out_shape


def _exchange_sems(n):
    if n == 0:
        return []
    return [pltpu.SemaphoreType.DMA((n, N_DEV - 1)), pltpu.SemaphoreType.DMA((n, N_DEV - 1)),
            pltpu.SemaphoreType.DMA((n,))]


SIBLING = 1
OTHER_CHIPS = (2, 4, 6)


def _related(k):
    x, y, c = lax.axis_index("x"), lax.axis_index("y"), lax.axis_index("c")
    px, py, pc = x ^ ((k >> 2) & 1), y ^ ((k >> 1) & 1), c ^ (k & 1)
    return (px, py, pc), 4 * px + 2 * py + pc


def _exchange_phases(modes, src_refs, out_refs, send_sems, recv_sems, own_sems):
    _, me = _related(0)
    sib_dev, sib = _related(SIBLING)
    start, middle, end = [], [], []

    def remote(a, i, src, dst, dev):
        return pltpu.make_async_remote_copy(src_ref=src, dst_ref=dst, send_sem=send_sems.at[a, i],
                                            recv_sem=recv_sems.at[a, i], device_id=dev, device_id_type=MESH)

    for a, mode in enumerate(modes):
        out = out_refs[a]
        if mode == "gather":
            src = src_refs[a]
            own = pltpu.make_async_copy(src, out.at[me], own_sems.at[a])
            to_sib = remote(a, 0, src, out.at[me], sib_dev)
            start += [own.start, to_sib.start]
            end += [remote(a, 0, src, out.at[sib], sib_dev).wait_recv, to_sib.wait_send, own.wait]
            for j, k in enumerate(OTHER_CHIPS, start=1):
                dev, peer = _related(k)
                _, peer_sib = _related(k ^ SIBLING)
                send = remote(a, j, src, out.at[me], dev)
                passed = remote(a, 3 + j, out.at[peer], out.at[peer], sib_dev)
                start.append(send.start)
                middle += [remote(a, j, src, out.at[peer], dev).wait_recv, passed.start]
                end += [remote(a, 3 + j, out.at[peer_sib], out.at[peer_sib], sib_dev).wait_recv,
                        send.wait_send, passed.wait_send]
        elif mode == "pair":
            core = lax.axis_index("c")
            for s in range(N_DEV // 2):
                send = remote(a, s, src_refs[a].at[s, 1 - core], out.at[s], sib_dev)
                start.append(send.start)
                end += [remote(a, s, src_refs[a].at[s, 1 - core], out.at[s], sib_dev).wait_recv, send.wait_send]
        elif mode == "chips":
            chip = me // 2
            own = pltpu.make_async_copy(src_refs[a].at[chip], out.at[chip], own_sems.at[a])
            start.append(own.start)
            end.append(own.wait)
            for j, k in enumerate(OTHER_CHIPS, start=1):
                dev, peer = _related(k)
                send = remote(a, j, src_refs[a].at[peer // 2], out.at[chip], dev)
                start.append(send.start)
                end += [remote(a, j, src_refs[a].at[peer // 2], out.at[peer // 2], dev).wait_recv, send.wait_send]
        else:
            own = pltpu.make_async_copy(src_refs[a].at[me], out.at[me], own_sems.at[a])
            start.append(own.start)
            end.append(own.wait)
            for k in range(1, N_DEV):
                dev, peer = _related(k)
                send = remote(a, k - 1, src_refs[a].at[peer], out.at[me], dev)
                start.append(send.start)
                end += [remote(a, k - 1, src_refs[a].at[peer], out.at[peer], dev).wait_recv, send.wait_send]
    return start, middle, end


def _run(actions):
    for act in actions:
        act()


def _exchange(name, srcs, modes):
    n = len(srcs)

    def body(*refs):
        start, middle, end = _exchange_phases(modes, refs[:n], refs[n:2 * n], *refs[2 * n:])
        _run(start)
        _run(middle)
        _run(end)

    return pl.pallas_call(
        body, name=name, out_shape=_exchange_shapes(srcs, modes),
        in_specs=[ANY_SPEC] * n, out_specs=[ANY_SPEC] * n,
        scratch_shapes=_exchange_sems(n),
    )(*srcs)


def _ride_start(modes, step, steps, src_refs, out_refs, sems):
    if not modes:
        return
    middle_step = steps - 1

    @pl.when(step == 0)
    def _():
        _run(_exchange_phases(modes, src_refs, out_refs, *sems)[0])

    if "gather" in modes:
        @pl.when(step == middle_step)
        def _():
            _run(_exchange_phases(modes, src_refs, out_refs, *sems)[1])


def _ride_wait(modes, step, steps, src_refs, out_refs, sems):
    if not modes:
        return

    @pl.when(step == steps - 1)
    def _():
        _run(_exchange_phases(modes, src_refs, out_refs, *sems)[2])


def _ada_mod(c_all, w_ada, b_ada_mine):
    nb, cols = c_all.shape[0], w_ada.shape[1]

    def body(c_ref, w_ref, b_ref, o_ref):
        cv = c_ref[...]
        ca = cv * _sigmoid(cv)
        o_ref[...] = _dot(ca, w_ref[...]) + b_ref[...]

    return pl.pallas_call(body, name="ada_mod", out_shape=SDS((nb, cols), F32))(c_all, w_ada, b_ada_mine)


def _tile_rows(T, big=False):
    return min(512 if big else 256, T)


def _mod_spec(tps):
    return pl.BlockSpec((None, 8, D_MODEL), lambda i: (i // tps, 0, 0))


def _in_proj(x2, mod8, pre_w, w_in_bf, T, ride_srcs, ride_modes):
    N = x2.shape[0]
    TM = _tile_rows(T, big=True)
    tps = T // TM
    nr = len(ride_srcs)

    def body(*refs):
        x_ref, mod_ref, pw_ref, w_ref = refs[:4]
        ride_in = refs[4:4 + nr]
        pa_ref, ph_ref, h1_ref = refs[4 + nr:7 + nr]
        ride_out = refs[7 + nr:7 + 2 * nr]
        sems = refs[7 + 2 * nr:]
        _ride_start(ride_modes, pl.program_id(0), N // TM, ride_in, ride_out, sems)
        x = x_ref[...]
        r = lax.rsqrt(_mean_last(x * x) + EPS)
        h = (x * r * pw_ref[...]) * (1.0 + mod_ref[1:2, :]) + mod_ref[0:1, :]
        hb = _bf(h)
        h1_ref[...] = hb
        pa_ref[...] = _dot_nt(hb, w_ref[:ATT_COLS, :])
        ph_ref[...] = _dot_nt(hb, w_ref[ATT_COLS:, :])
        _ride_wait(ride_modes, pl.program_id(0), N // TM, ride_in, ride_out, sems)

    return pl.pallas_call(
        body, name="in_proj", grid=(N // TM,),
        in_specs=[pl.BlockSpec((TM, D_MODEL), lambda i: (i, 0)), _mod_spec(tps),
                  pl.BlockSpec((1, D_MODEL), lambda i: (0, 0)),
                  pl.BlockSpec((IN_COLS, D_MODEL), lambda i: (0, 0))] + [ANY_SPEC] * nr,
        out_specs=[pl.BlockSpec((TM, ATT_COLS), lambda i: (i, 0)),
                   pl.BlockSpec((TM, IN_COLS - ATT_COLS), lambda i: (i, 0)),
                   pl.BlockSpec((TM, D_MODEL), lambda i: (i, 0))] + [ANY_SPEC] * nr,
        out_shape=[SDS((N, ATT_COLS), F32), SDS((N, IN_COLS - ATT_COLS), F32), SDS((N, D_MODEL), BF16)]
        + _exchange_shapes(ride_srcs, ride_modes),
        scratch_shapes=_exchange_sems(nr),
        compiler_params=_params(("arbitrary",), VMEM_LIMIT_BIG),
    )(x2, mod8, pre_w, w_in_bf, *ride_srcs)


def _rope_tables(T):
    half = ROPE_DIM // 2
    f32 = np.float32
    inv_freq = (f32(ROPE_THETA) ** (-np.arange(0, ROPE_DIM, 2, dtype=f32) / f32(ROPE_DIM))).astype(f32)
    ang = np.arange(T, dtype=f32)[:, None] * inv_freq[None, :]
    cos, sin = np.cos(ang).astype(f32), np.sin(ang).astype(f32)
    ones = np.ones((T, ATT_HEAD_DIM - ROPE_DIM), f32)
    zeros = np.zeros((T, ATT_HEAD_DIM - ROPE_DIM), f32)
    zh = np.zeros((T, half), f32)
    cos64 = np.concatenate([cos, cos, ones], axis=1)
    sin_left = np.concatenate([-sin, zh, zeros], axis=1)
    sin_right = np.concatenate([zh, sin, zeros], axis=1)
    rep = LANES // ATT_HEAD_DIM
    return tuple(jnp.asarray(np.tile(t, (1, rep))) for t in (cos64, sin_left, sin_right))


def _rope(xc, cs, sl, sr):
    return xc * cs + pltpu.roll(xc, LANES - 8, 1) * sl + pltpu.roll(xc, 8, 1) * sr


def _rope_t(dy, cs, sl, sr):
    return dy * cs + pltpu.roll(dy * sl, 8, 1) + pltpu.roll(dy * sr, LANES - 8, 1)


ATT_SCALE = ATT_HEAD_DIM ** -0.5
ATT_SPLITS = 4


def _lower_mask():
    j = lax.broadcasted_iota(jnp.int32, (WINDOW, ATT_GROUP * WINDOW), 0)
    i = lax.broadcasted_iota(jnp.int32, (WINDOW, ATT_GROUP * WINDOW), 1) & (WINDOW - 1)
    return j <= i


def _sink_row(sink_ref, hk):
    return jnp.concatenate(
        [jnp.full((1, WINDOW), sink_ref[0, ATT_GROUP * hk + g], F32) for g in range(ATT_GROUP)], axis=1)


def _softmax_window(qs, k_cur, k_prev, lower, has_prev, sink):
    s_prev = jnp.where(has_prev, _dot_nt(k_prev, qs), jnp.finfo(F32).min)
    s = jnp.where(lower, _dot_nt(k_cur, qs), s_prev)
    m = jnp.maximum(jnp.max(s, axis=0, keepdims=True), sink)
    p = jnp.exp(s - m)
    es = jnp.exp(sink - m)
    inv = 1.0 / (jnp.sum(p, axis=0, keepdims=True) + es)
    return p, inv, es


def _stack_heads(parts, hk):
    hs = []
    for g in range(ATT_GROUP):
        h = ATT_GROUP * hk + g
        hs.append(parts[h // 2][:, (h % 2) * ATT_HEAD_DIM:(h % 2 + 1) * ATT_HEAD_DIM])
    return jnp.concatenate(hs, axis=0)


def _attn_fwd(proj3, tables, sinks, attn_w, ride_srcs, ride_modes):
    B, T, _ = proj3.shape
    nb = T // WINDOW
    splits = min(ATT_SPLITS, nb)
    per = nb // splits
    nr = len(ride_srcs)
    cos, sinl, sinr = tables

    def body(*refs):
        q_ref, k_ref, v_ref, cos_ref, sl_ref, sr_ref, sink_ref, aw_ref = refs[:8]
        ride_in = refs[8:8 + nr]
        o_ref, an_ref, qr_ref, kr_ref = refs[8 + nr:12 + nr]
        ride_out = refs[12 + nr:12 + 2 * nr]
        kpad, vpad = refs[12 + 2 * nr:14 + 2 * nr]
        sems = refs[14 + 2 * nr:]
        part = pl.program_id(1)
        step = pl.program_id(0) * splits + part
        _ride_start(ride_modes, step, B * splits, ride_in, ride_out, sems)

        @pl.when(part == 0)
        def _():
            kpad[0:WINDOW, :] = jnp.zeros((WINDOW, LANES), BF16)
            vpad[0:WINDOW, :] = jnp.zeros((WINDOW, LANES), BF16)

        lower = _lower_mask()

        def block(n, carry):
            r0 = pl.multiple_of(n * WINDOW, WINDOW)
            rows = pl.ds(r0, WINDOW)
            nxt = pl.ds(r0 + WINDOW, WINDOW)
            cs, sl, sr = cos_ref[rows, :], sl_ref[rows, :], sr_ref[rows, :]
            kb = _bf(_rope(k_ref[rows, :], cs, sl, sr))
            vb = _bf(v_ref[rows, :])
            kpad[nxt, :] = kb
            kr_ref[rows, :] = kb
            vpad[nxt, :] = vb
            qparts = []
            for j in range(ATT_WIDTH // LANES):
                qp = _bf(_rope(q_ref[rows, j * LANES:(j + 1) * LANES], cs, sl, sr) * ATT_SCALE)
                qr_ref[rows, j * LANES:(j + 1) * LANES] = qp
                qparts.append(qp)
            for hk in range(ATT_KV_HEADS):
                lanes = slice(hk * ATT_HEAD_DIM, (hk + 1) * ATT_HEAD_DIM)
                qs = _stack_heads(qparts, hk)
                p, inv, _ = _softmax_window(qs, kb[:, lanes], kpad[rows, lanes], lower, n > 0,
                                            _sink_row(sink_ref, hk))
                p_cur = jnp.where(lower, p, 0.0)
                ot = (_dot_tn(vb[:, lanes], _bf(p_cur)) + _dot_tn(vpad[rows, lanes], _bf(p - p_cur))) * inv
                for g in range(ATT_GROUP):
                    h = ATT_GROUP * hk + g
                    o_ref[rows, h * ATT_HEAD_DIM:(h + 1) * ATT_HEAD_DIM] = ot[:, g * WINDOW:(g + 1) * WINDOW].T
            ob = o_ref[rows, :]
            an_ref[rows, :] = _bf(ob * lax.rsqrt(_mean_last(ob * ob) + EPS) * aw_ref[...])
            return carry

        _loop_pairs(part * per, per, block, 0)
        _ride_wait(ride_modes, step, B * splits, ride_in, ride_out, sems)

    seq = lambda w, j: pl.BlockSpec((None, T, w), lambda b, s: (b, 0, j))
    full = lambda r, w: pl.BlockSpec((r, w), lambda b, s: (0, 0))
    return pl.pallas_call(
        body, name="attn_fwd", grid=(B, splits),
        in_specs=[seq(ATT_WIDTH, 0), seq(LANES, 4), seq(LANES, 5),
                  full(T, LANES), full(T, LANES), full(T, LANES),
                  pl.BlockSpec(memory_space=pltpu.SMEM), full(1, ATT_WIDTH)] + [ANY_SPEC] * nr,
        out_specs=[seq(ATT_WIDTH, 0), seq(ATT_WIDTH, 0), seq(ATT_WIDTH, 0), seq(LANES, 0)] + [ANY_SPEC] * nr,
        out_shape=[SDS((B, T, ATT_WIDTH), F32), SDS((B, T, ATT_WIDTH), BF16),
                   SDS((B, T, ATT_WIDTH), BF16), SDS((B, T, LANES), BF16)] + _exchange_shapes(ride_srcs, ride_modes),
        scratch_shapes=[pltpu.VMEM((T + WINDOW, LANES), BF16), pltpu.VMEM((T + WINDOW, LANES), BF16)]
        + _exchange_sems(nr),
        compiler_params=_params(("arbitrary", "arbitrary"), VMEM_LIMIT_BIG),
    )(proj3, proj3, proj3, cos, sinl, sinr, sinks, attn_w, *ride_srcs)


HG_GROUP = 8
HG_ROWS = HG_GROUP * HG_CHUNK


HG_STACK = HG_GROUP * HG_HEAD_DIM


def _group_masks():
    r = lax.broadcasted_iota(jnp.int32, (HG_ROWS, HG_ROWS), 0)
    c = lax.broadcasted_iota(jnp.int32, (HG_ROWS, HG_ROWS), 1)
    same = (r // HG_CHUNK) == (c // HG_CHUNK)
    return same & (r >= c), same & (c >= r)


def _row_chunk():
    return lax.broadcasted_iota(jnp.int32, (HG_ROWS, HG_HEAD_DIM), 0) // HG_CHUNK


def _spread(a, row_chunk):
    return jnp.concatenate([jnp.where(row_chunk == c, a, jnp.zeros_like(a)) for c in range(HG_GROUP)], axis=1)


def _pick(r, row_chunk):
    out = jnp.where(row_chunk == 0, r[:, :HG_HEAD_DIM], 0.0)
    for c in range(1, HG_GROUP):
        out = out + jnp.where(row_chunk == c, r[:, c * HG_HEAD_DIM:(c + 1) * HG_HEAD_DIM], 0.0)
    return out


def _lane_block(a, c):
    return a[:, c * HG_HEAD_DIM:(c + 1) * HG_HEAD_DIM]


def _ones_bf(mask):
    return jnp.where(mask, 1.0, 0.0).astype(BF16)


def _chunk_bcast(rows_1x128):
    return jnp.concatenate([jnp.broadcast_to(r, (HG_CHUNK, HG_HEAD_DIM)) for r in rows_1x128], axis=0)


def _hgrn_gates(hq, hf, lb, lower_bf):
    sq = _sigmoid(hq)
    q = hq * sq
    sg = _sigmoid(hf)
    f = lb + (1.0 - lb) * sg
    k = 1.0 - f
    logf = jnp.log(f)
    b = _tri_sum(lower_bf, logf)
    bl = [_sum_rows(logf[_chunk_rows(c), :]) for c in range(HG_GROUP)]
    eb, enb, e2 = jnp.exp(b), jnp.exp(-b), jnp.exp(_chunk_bcast(bl) - b)
    ebl = [jnp.exp(r) for r in bl]
    return dict(sq=sq, sg=sg, f=f, eb=eb, enb=enb, e2=e2, ebl=ebl, qd=q * eb, kd=k * enb, k2=k * e2)


def _chunk_rows(c):
    return slice(c * HG_CHUNK, (c + 1) * HG_CHUNK)


def _head_lanes(h):
    return slice(h * HG_HEAD_DIM, (h + 1) * HG_HEAD_DIM)


def _hgrn_fwd(proj_h, lb, hg_w, ride_srcs, ride_modes):
    B, T, _ = proj_h.shape
    ng = T // HG_ROWS
    nr = len(ride_srcs)

    def body(*refs):
        hq_ref, hf_ref, hi_ref, hg_ref, lb_ref, gw_ref = refs[:6]
        ride_in = refs[6:6 + nr]
        o_ref, rg_ref, sp_ref = refs[6 + nr:9 + nr]
        ride_out = refs[9 + nr:9 + 2 * nr]
        st = refs[9 + 2 * nr]
        sems = refs[10 + 2 * nr:]
        gi = pl.program_id(1)
        step = pl.program_id(0) * ng + gi
        _ride_start(ride_modes, step, B * ng, ride_in, ride_out, sems)

        @pl.when(gi == 0)
        def _():
            st[...] = jnp.zeros(st.shape, F32)

        lo, _ = _group_masks()
        lower_bf = _ones_bf(lo)
        row_chunk = _row_chunk()
        for h in range(HG_HEADS):
            lanes = _head_lanes(h)
            gt = _hgrn_gates(hq_ref[:, lanes], hf_ref[:, lanes], lb_ref[:, lanes], lower_bf)
            v, qd, kd = _bf(hi_ref[:, lanes]), _bf(gt["qd"]), _bf(gt["kd"])
            a = jnp.where(lo, _dot_nt(qd, kd), 0.0)
            kv = _dot_tn(v, _bf(_spread(gt["k2"], row_chunk)))
            s = st[h]
            before = []
            for c in range(HG_GROUP):
                before.append(s)
                s = s * gt["ebl"][c] + _lane_block(kv, c)
            st[h] = s
            sp = jnp.concatenate(before, axis=1)
            sp_ref[h] = sp
            o = _dot(_bf(a), v) + _dot_nt(_bf(_spread(gt["qd"], row_chunk)), _bf(sp))
            o_ref[:, lanes] = o
            hg = hg_ref[:, lanes]
            rn = o * lax.rsqrt(_mean_last(o * o) + EPS) * gw_ref[...]
            rg_ref[:, lanes] = _bf(rn * (hg * _sigmoid(hg)))
        _ride_wait(ride_modes, step, B * ng, ride_in, ride_out, sems)

    part = lambda j: pl.BlockSpec((None, HG_ROWS, HG_WIDTH), lambda b, g: (b, g, j))
    return pl.pallas_call(
        body, name="hgrn_fwd", grid=(B, ng),
        in_specs=[part(0), part(1), part(2), part(3),
                  pl.BlockSpec((1, HG_WIDTH), lambda b, g: (0, 0)),
                  pl.BlockSpec((1, LANES), lambda b, g: (0, 0))] + [ANY_SPEC] * nr,
        out_specs=[part(0), part(0),
                   pl.BlockSpec((None, HG_HEADS, None, HG_HEAD_DIM, HG_STACK), lambda b, g: (b, 0, g, 0, 0))]
        + [ANY_SPEC] * nr,
        out_shape=[SDS((B, T, HG_WIDTH), F32), SDS((B, T, HG_WIDTH), BF16),
                   SDS((B, HG_HEADS, ng, HG_HEAD_DIM, HG_STACK), F32)] + _exchange_shapes(ride_srcs, ride_modes),
        scratch_shapes=[pltpu.VMEM((HG_HEADS, HG_HEAD_DIM, HG_HEAD_DIM), F32)] + _exchange_sems(nr),
        compiler_params=_params(("arbitrary", "arbitrary"), VMEM_LIMIT_BIG),
    )(proj_h, proj_h, proj_h, proj_h, lb, hg_w, *ride_srcs)


def _mix_out(x2, attn_n, rec_g, mod8, post_w, w_out_bf, T, ride_srcs, ride_modes):
    N = x2.shape[0]
    TM = _tile_rows(T, big=True)
    tps = T // TM
    nr = len(ride_srcs)

    def body(*refs):
        x_ref, an_ref, rg_ref, mod_ref, pw_ref, w_ref = refs[:6]
        ride_in = refs[6:6 + nr]
        mix_ref, x1_ref, cat_ref = refs[6 + nr:9 + nr]
        ride_out = refs[9 + nr:9 + 2 * nr]
        sems = refs[9 + 2 * nr:]
        _ride_start(ride_modes, pl.program_id(0), N // TM, ride_in, ride_out, sems)
        cat = jnp.concatenate([an_ref[...], rg_ref[...]], axis=1)
        cat_ref[...] = cat
        mix = _dot(cat, w_ref[...])
        mix_ref[...] = mix
        r = lax.rsqrt(_mean_last(mix * mix) + EPS)
        x1_ref[...] = x_ref[...] + mod_ref[2:3, :] * (mix * r * pw_ref[...])
        _ride_wait(ride_modes, pl.program_id(0), N // TM, ride_in, ride_out, sems)

    row = lambda w: pl.BlockSpec((TM, w), lambda i: (i, 0))
    return pl.pallas_call(
        body, name="mix_out", grid=(N // TM,),
        in_specs=[row(D_MODEL), row(ATT_WIDTH), row(HG_WIDTH), _mod_spec(tps),
                  pl.BlockSpec((1, D_MODEL), lambda i: (0, 0)),
                  pl.BlockSpec((D_MODEL, D_MODEL), lambda i: (0, 0))] + [ANY_SPEC] * nr,
        out_specs=[row(D_MODEL), row(D_MODEL), row(D_MODEL)] + [ANY_SPEC] * nr,
        out_shape=[SDS((N, D_MODEL), F32), SDS((N, D_MODEL), F32), SDS((N, D_MODEL), BF16)]
        + _exchange_shapes(ride_srcs, ride_modes),
        scratch_shapes=_exchange_sems(nr),
        compiler_params=_params(("arbitrary",), VMEM_LIMIT_BIG),
    )(x2, attn_n, rec_g, mod8, post_w, w_out_bf, *ride_srcs)


def _load_weights_once(pairs, sem):
    @pl.when(pl.program_id(0) == 0)
    def _():
        cps = [pltpu.make_async_copy(src, dst, sem.at[i]) for i, (src, dst) in enumerate(pairs)]
        for cp in cps:
            cp.start()
        for cp in cps:
            cp.wait()


MLP_HALF = D_MODEL // 2
MLP_PIECES = 2 * N_DEV + 2


def _mlp_weight_pieces(wu_a, wu_b, wd_a, wd_b, wu, wd):
    cols = D_FF // N_DEV
    pairs = []
    for h, half in enumerate((wu_a, wu_b)):
        for j in range(N_DEV):
            pairs.append((half.at[j], wu.at[pl.ds(h * MLP_HALF, MLP_HALF), pl.ds(j * cols, cols)]))
    for h, half in enumerate((wd_a, wd_b)):
        pairs.append((half, wd.at[:, pl.ds(h * MLP_HALF, MLP_HALF)]))
    return pairs


def _mlp_fwd(x1, mod8, pre_w, w_up_halves, w_down_halves, T):
    N = x1.shape[0]
    TM = _tile_rows(T)
    tps = T // TM

    def body(x_ref, mod_ref, pw_ref, wua, wub, wda, wdb, up_ref, u_ref, d_ref, h2_ref, wu, wd, sem):
        _load_weights_once(_mlp_weight_pieces(wua, wub, wda, wdb, wu, wd), sem)
        x = x_ref[...]
        r = lax.rsqrt(_mean_last(x * x) + EPS)
        h = (x * r * pw_ref[...]) * (1.0 + mod_ref[4:5, :]) + mod_ref[3:4, :]
        hb = _bf(h)
        h2_ref[...] = hb
        up = _dot(hb, wu[...])
        up_ref[...] = up
        ru = jnp.maximum(up, 0.0)
        u = _bf(ru * ru)
        u_ref[...] = u
        d_ref[...] = _dot(u, wd[...])

    row = lambda w: pl.BlockSpec((TM, w), lambda i: (i, 0))
    return pl.pallas_call(
        body, name="mlp_fwd", grid=(N // TM,),
        in_specs=[row(D_MODEL), _mod_spec(tps), pl.BlockSpec((1, D_MODEL), lambda i: (0, 0))] + [ANY_SPEC] * 4,
        out_specs=[row(D_FF), row(D_FF), row(D_MODEL), row(D_MODEL)],
        out_shape=[SDS((N, D_FF), F32), SDS((N, D_FF), BF16), SDS((N, D_MODEL), F32), SDS((N, D_MODEL), BF16)],
        scratch_shapes=[pltpu.VMEM((D_MODEL, D_FF), BF16), pltpu.VMEM((D_FF, D_MODEL), BF16),
                        pltpu.SemaphoreType.DMA((MLP_PIECES,))],
        compiler_params=_params(("arbitrary",), VMEM_LIMIT_BIG),
    )(x1, mod8, pre_w, *w_up_halves, *w_down_halves)


def _acc_rows(acc_ref, first, rows):
    @pl.when(first)
    def _():
        acc_ref[...] = jnp.zeros(acc_ref.shape, F32)
    for i, r in enumerate(rows):
        acc_ref[i:i + 1, :] += r


def _mlp_bwd(x1, d, up, tgt, mod8, pre_w, post_w, w_up_halves, w_down_halves, T):
    N = x1.shape[0]
    TM = _tile_rows(T)
    tps = T // TM

    def body(x_ref, d_ref, up_ref, t_ref, mod_ref, pw_ref, qw_ref, wua, wub, wda, wdb,
             dx_ref, dup_ref, dd_ref, acc_ref, wd, wu, sem):
        _load_weights_once(_mlp_weight_pieces(wua, wub, wda, wdb, wu, wd), sem)
        sh2, sc2, g2 = mod_ref[3:4, :], mod_ref[4:5, :], mod_ref[5:6, :]
        x = x_ref[...]
        r1 = lax.rsqrt(_mean_last(x * x) + EPS)
        xh = x * r1
        n2 = xh * pw_ref[...]
        dv = d_ref[...]
        rd = lax.rsqrt(_mean_last(dv * dv) + EPS)
        dh = dv * rd
        rr = dh * qw_ref[...]
        e = x + g2 * rr - t_ref[...]
        loss = 0.5 * jnp.sum(_sum_rows(e * e), axis=1, keepdims=True) / D_MODEL
        dy = e * (1.0 / D_MODEL)
        dg2 = _sum_rows(dy * rr)
        drr = dy * g2
        dw_post = _sum_rows(drr * dh)
        ddh = drr * qw_ref[...]
        dd = _bf(rd * (ddh - dh * _mean_last(ddh * dh)))
        dd_ref[...] = dd
        ru = jnp.maximum(up_ref[...], 0.0)
        dup = _bf(_dot_nt(dd, wd[...]) * (2.0 * ru))
        dup_ref[...] = dup
        dh2 = _dot_nt(dup, wu[...])
        dsh2 = _sum_rows(dh2)
        dsc2 = _sum_rows(dh2 * n2)
        dn2 = dh2 * (1.0 + sc2)
        dw_pre = _sum_rows(dn2 * xh)
        dxh = dn2 * pw_ref[...]
        dx_ref[...] = dy + r1 * (dxh - xh * _mean_last(dxh * xh))
        _acc_rows(acc_ref, pl.program_id(0) % tps == 0,
                  [dsh2, dsc2, dg2, dw_pre, dw_post, jnp.broadcast_to(loss, (1, D_MODEL))])

    row = lambda w: pl.BlockSpec((TM, w), lambda i: (i, 0))
    vec = pl.BlockSpec((1, D_MODEL), lambda i: (0, 0))
    B = N // T
    return pl.pallas_call(
        body, name="mlp_bwd", grid=(N // TM,),
        in_specs=[row(D_MODEL), row(D_MODEL), row(D_FF), row(D_MODEL), _mod_spec(tps), vec, vec] + [ANY_SPEC] * 4,
        out_specs=[row(D_MODEL), row(D_FF), row(D_MODEL), _mod_spec(tps)],
        out_shape=[SDS((N, D_MODEL), F32), SDS((N, D_FF), BF16), SDS((N, D_MODEL), BF16),
                   SDS((B, 8, D_MODEL), F32)],
        scratch_shapes=[pltpu.VMEM((D_FF, D_MODEL), BF16), pltpu.VMEM((D_MODEL, D_FF), BF16),
                        pltpu.SemaphoreType.DMA((MLP_PIECES,))],
        compiler_params=_params(("arbitrary",), VMEM_LIMIT_BIG),
    )(x1, d, up, tgt, mod8, pre_w, post_w, *w_up_halves, *w_down_halves)


def _mix_bwd(mix, dx1, mod8, post_w, w_out_bf, T, ride_srcs, ride_modes):
    N = mix.shape[0]
    TM = _tile_rows(T, big=True)
    tps = T // TM
    nr = len(ride_srcs)

    def body(*refs):
        mix_ref, dx_ref, mod_ref, pw_ref, w_ref = refs[:5]
        ride_in = refs[5:5 + nr]
        dan_ref, drg_ref, dmix_ref, acc_ref = refs[5 + nr:9 + nr]
        ride_out = refs[9 + nr:9 + 2 * nr]
        sems = refs[9 + 2 * nr:]
        _ride_start(ride_modes, pl.program_id(0), N // TM, ride_in, ride_out, sems)
        g1 = mod_ref[2:3, :]
        mix = mix_ref[...]
        dx1 = dx_ref[...]
        rm = lax.rsqrt(_mean_last(mix * mix) + EPS)
        mh = mix * rm
        dg1 = _sum_rows(dx1 * (mh * pw_ref[...]))
        dr = dx1 * g1
        dw_post = _sum_rows(dr * mh)
        dmh = dr * pw_ref[...]
        dmix = _bf(rm * (dmh - mh * _mean_last(dmh * mh)))
        dmix_ref[...] = dmix
        dcat = _dot_nt(dmix, w_ref[...])
        dan_ref[...] = dcat[:, :ATT_WIDTH]
        drg_ref[...] = dcat[:, ATT_WIDTH:]
        _acc_rows(acc_ref, pl.program_id(0) % tps == 0, [dg1, dw_post])
        _ride_wait(ride_modes, pl.program_id(0), N // TM, ride_in, ride_out, sems)

    row = lambda w: pl.BlockSpec((TM, w), lambda i: (i, 0))
    B = N // T
    return pl.pallas_call(
        body, name="mix_bwd", grid=(N // TM,),
        in_specs=[row(D_MODEL), row(D_MODEL), _mod_spec(tps), pl.BlockSpec((1, D_MODEL), lambda i: (0, 0)),
                  pl.BlockSpec((D_MODEL, D_MODEL), lambda i: (0, 0))] + [ANY_SPEC] * nr,
        out_specs=[row(ATT_WIDTH), row(HG_WIDTH), row(D_MODEL), _mod_spec(tps)] + [ANY_SPEC] * nr,
        out_shape=[SDS((N, ATT_WIDTH), F32), SDS((N, HG_WIDTH), F32), SDS((N, D_MODEL), BF16),
                   SDS((B, 8, D_MODEL), F32)] + _exchange_shapes(ride_srcs, ride_modes),
        scratch_shapes=_exchange_sems(nr),
        compiler_params=_params(("arbitrary",), VMEM_LIMIT_BIG),
    )(mix, dx1, mod8, post_w, w_out_bf, *ride_srcs)


def _hgrn_bwd(proj_h, lb, hg_w, o, s_prev, drg, ride_srcs, ride_modes):
    B, T, _ = proj_h.shape
    ng = T // HG_ROWS
    nr = len(ride_srcs)

    def body(*refs):
        hq_ref, hf_ref, hi_ref, hg_ref, lb_ref, gw_ref, o_ref, sp_ref, drg_ref = refs[:9]
        ride_in = refs[9:9 + nr]
        dhq_ref, dhf_ref, dhi_ref, dhg_ref, dlb_ref, dgw_ref = refs[9 + nr:15 + nr]
        ride_out = refs[15 + nr:15 + 2 * nr]
        dst = refs[15 + 2 * nr]
        sems = refs[16 + 2 * nr:]
        step = pl.program_id(0) * ng + pl.program_id(1)
        _ride_start(ride_modes, step, B * ng, ride_in, ride_out, sems)

        @pl.when(pl.program_id(1) == 0)
        def _():
            dst[...] = jnp.zeros(dst.shape, F32)
            dlb_ref[...] = jnp.zeros(dlb_ref.shape, F32)
            dgw_ref[...] = jnp.zeros(dgw_ref.shape, F32)

        lo, up = _group_masks()
        lower_bf, upper_bf = _ones_bf(lo), _ones_bf(up)
        row_chunk = _row_chunk()
        gw = gw_ref[...]

        for h in range(HG_HEADS):
            lanes = _head_lanes(h)
            lbv = lb_ref[:, lanes]
            hq = hq_ref[:, lanes]
            gt = _hgrn_gates(hq, hf_ref[:, lanes], lbv, lower_bf)
            sq, sg, qdf, kdf, k2f, ebl = gt["sq"], gt["sg"], gt["qd"], gt["kd"], gt["k2"], gt["ebl"]
            v, qd, kd = _bf(hi_ref[:, lanes]), _bf(qdf), _bf(kdf)
            ov = o_ref[:, lanes]
            hg = hg_ref[:, lanes]
            shg = _sigmoid(hg)
            dr = drg_ref[:, lanes]
            ro = lax.rsqrt(_mean_last(ov * ov) + EPS)
            oh = ov * ro
            dhg_ref[:, lanes] = _bf(dr * (oh * gw) * (shg + hg * shg * (1.0 - shg)))
            drn = dr * (hg * shg)
            dgw_ref[...] += jnp.broadcast_to(_sum_rows(drn * oh), (8, LANES))
            doh = drn * gw
            do = _bf(ro * (doh - oh * _mean_last(doh * oh)))
            a = jnp.where(lo, _dot_nt(qd, kd), 0.0)
            da = _bf(jnp.where(lo, _dot_nt(do, v), 0.0))
            dv = _dot_tn(_bf(a), do)
            dqd = _dot(da, kd)
            dkd = _dot_tn(da, qd)
            sp = sp_ref[h]
            incr = _dot_tn(do, _bf(_spread(qdf, row_chunk)))
            ds = dst[h]
            after = [None] * HG_GROUP
            for c in reversed(range(HG_GROUP)):
                after[c] = ds
                ds = ds * ebl[c] + _lane_block(incr, c)
            dst[h] = ds
            dss = jnp.concatenate(after, axis=1)
            dssb = _bf(dss)
            dk2 = _pick(_dot(v, dssb), row_chunk)
            dhi_ref[:, lanes] = _bf(dv + _dot_nt(_bf(_spread(k2f, row_chunk)), dssb))
            dqd = dqd + _pick(_dot(do, _bf(sp)), row_chunk)
            debl = _sum_rows(dss * sp)
            k2g = dk2 * k2f
            db = dqd * qdf - dkd * kdf - k2g
            dk = dkd * gt["enb"] + dk2 * gt["e2"]
            dbl = _chunk_bcast([_lane_block(debl, c) * ebl[c] + _sum_rows(k2g[_chunk_rows(c), :])
                                for c in range(HG_GROUP)])
            dg = _tri_sum(upper_bf, db, terms=2) + dbl
            df = dg / gt["f"] - dk
            dhf_ref[:, lanes] = _bf(df * (1.0 - lbv) * sg * (1.0 - sg))
            dlb_ref[:, lanes] += jnp.broadcast_to(_sum_rows(df * (1.0 - sg)), (8, LANES))
            dhq_ref[:, lanes] = _bf((dqd * gt["eb"]) * (sq + hq * sq * (1.0 - sq)))
        _ride_wait(ride_modes, step, B * ng, ride_in, ride_out, sems)

    part = lambda j: pl.BlockSpec((None, HG_ROWS, HG_WIDTH), lambda b, g: (b, ng - 1 - g, j))
    return pl.pallas_call(
        body, name="hgrn_bwd", grid=(B, ng),
        in_specs=[part(0), part(1), part(2), part(3),
                  pl.BlockSpec((1, HG_WIDTH), lambda b, g: (0, 0)),
                  pl.BlockSpec((1, LANES), lambda b, g: (0, 0)),
                  part(0),
                  pl.BlockSpec((None, HG_HEADS, None, HG_HEAD_DIM, HG_STACK), lambda b, g: (b, 0, ng - 1 - g, 0, 0)),
                  part(0)] + [ANY_SPEC] * nr,
        out_specs=[part(0), part(0), part(0), part(0),
                   pl.BlockSpec((None, 8, HG_WIDTH), lambda b, g: (b, 0, 0)),
                   pl.BlockSpec((None, 8, LANES), lambda b, g: (b, 0, 0))] + [ANY_SPEC] * nr,
        out_shape=[SDS((B, T, HG_WIDTH), BF16)] * 4 + [SDS((B, 8, HG_WIDTH), F32), SDS((B, 8, LANES), F32)]
        + _exchange_shapes(ride_srcs, ride_modes),
        scratch_shapes=[pltpu.VMEM((HG_HEADS, HG_HEAD_DIM, HG_HEAD_DIM), F32)] + _exchange_sems(nr),
        compiler_params=_params(("arbitrary", "arbitrary"), VMEM_LIMIT_BIG),
    )(proj_h, proj_h, proj_h, proj_h, lb, hg_w, o, s_prev, drg, *ride_srcs)


def _attn_bwd(qr, kr, proj3, attn_o, dan, tables, sinks, attn_w, ride_srcs, ride_modes):
    B, T, _ = proj3.shape
    nb = T // WINDOW
    splits = min(ATT_SPLITS, nb)
    per = nb // splits
    nr = len(ride_srcs)
    cos, sinl, sinr = tables
    QKV = ATT_WIDTH + 2 * LANES

    def body(*refs):
        qr_ref, kr_ref, v_ref, o_ref, dan_ref, cos_ref, sl_ref, sr_ref, sink_ref, aw_ref = refs[:10]
        ride_in = refs[10:10 + nr]
        dqkv_ref, dsink_ref, daw_ref = refs[10 + nr:13 + nr]
        ride_out = refs[13 + nr:13 + 2 * nr]
        kpad, vpad, dkpad, dvpad, dqb, dsk = refs[13 + 2 * nr:19 + 2 * nr]
        sems = refs[19 + 2 * nr:]
        part = pl.program_id(1)
        step = pl.program_id(0) * splits + part
        _ride_start(ride_modes, step, B * splits, ride_in, ride_out, sems)

        @pl.when(part == 0)
        def _():
            kpad[0:WINDOW, :] = jnp.zeros((WINDOW, LANES), BF16)
            vpad[0:WINDOW, :] = jnp.zeros((WINDOW, LANES), BF16)
            kpad[WINDOW:, :] = kr_ref[...]
            vpad[WINDOW:, :] = _bf(v_ref[...])
            dkpad[...] = jnp.zeros(dkpad.shape, F32)
            dvpad[...] = jnp.zeros(dvpad.shape, F32)
            dsk[...] = jnp.zeros(dsk.shape, F32)
            daw_ref[...] = jnp.zeros(daw_ref.shape, F32)

        lower = _lower_mask()
        aw = aw_ref[...]

        def block(n, daw):
            r0 = pl.multiple_of(n * WINDOW, WINDOW)
            rows = pl.ds(r0, WINDOW)
            nxt = pl.ds(r0 + WINDOW, WINDOW)
            ob = o_ref[rows, :]
            dn = dan_ref[rows, :]
            ro = lax.rsqrt(_mean_last(ob * ob) + EPS)
            oh = ob * ro
            daw = daw + _sum_rows(dn * oh)
            doh = dn * aw
            do = _bf(ro * (doh - oh * _mean_last(doh * oh)))
            doparts = [do[:, j * LANES:(j + 1) * LANES] for j in range(ATT_WIDTH // LANES)]
            qparts = [qr_ref[rows, j * LANES:(j + 1) * LANES] for j in range(ATT_WIDTH // LANES)]
            for hk in range(ATT_KV_HEADS):
                lanes = slice(hk * ATT_HEAD_DIM, (hk + 1) * ATT_HEAD_DIM)
                qs = _stack_heads(qparts, hk)
                dos = _stack_heads(doparts, hk)
                k_cur, k_prev = kpad[nxt, lanes], kpad[rows, lanes]
                v_cur, v_prev = vpad[nxt, lanes], vpad[rows, lanes]
                p, inv, es = _softmax_window(qs, k_cur, k_prev, lower, n > 0, _sink_row(sink_ref, hk))
                p = p * inv
                dp = jnp.where(lower, _dot_nt(v_cur, dos), _dot_nt(v_prev, dos))
                delta = jnp.sum(p * dp, axis=0, keepdims=True)
                ds = p * (dp - delta)
                sk = (es * inv) * delta
                ds_cur = jnp.where(lower, ds, 0.0)
                p_cur = jnp.where(lower, p, 0.0)
                ds_cur, ds_prev = _bf(ds_cur), _bf(ds - ds_cur)
                p_cur, p_prev = _bf(p_cur), _bf(p - p_cur)
                dqt = (_dot_tn(k_cur, ds_cur) + _dot_tn(k_prev, ds_prev)) * ATT_SCALE
                dkpad[nxt, lanes] += _dot(ds_cur, qs)
                dkpad[rows, lanes] += _dot(ds_prev, qs)
                dvpad[nxt, lanes] += _dot(p_cur, dos)
                dvpad[rows, lanes] += _dot(p_prev, dos)
                for g in range(ATT_GROUP):
                    h = ATT_GROUP * hk + g
                    cols = slice(g * WINDOW, (g + 1) * WINDOW)
                    dqb[:, h * ATT_HEAD_DIM:(h + 1) * ATT_HEAD_DIM] = dqt[:, cols].T
                    head_lane = lax.broadcasted_iota(jnp.int32, dsk.shape, 1) == h
                    dsk[...] += jnp.where(head_lane, -jnp.sum(sk[:, cols], axis=1, keepdims=True), 0.0)
            cs, sl, sr = cos_ref[rows, :], sl_ref[rows, :], sr_ref[rows, :]
            for j in range(ATT_WIDTH // LANES):
                dqkv_ref[rows, j * LANES:(j + 1) * LANES] = _bf(_rope_t(dqb[:, j * LANES:(j + 1) * LANES], cs, sl, sr))
            return daw

        daw = _loop_pairs(part * per, per, block, jnp.zeros((1, ATT_WIDTH), F32))
        daw_ref[...] += jnp.broadcast_to(daw, (8, ATT_WIDTH))
        dsink_ref[...] = dsk[...]

        def finish(n, carry):
            r0 = pl.multiple_of(n * WINDOW, WINDOW)
            rows = pl.ds(r0, WINDOW)
            nxt = pl.ds(r0 + WINDOW, WINDOW)
            cs, sl, sr = cos_ref[rows, :], sl_ref[rows, :], sr_ref[rows, :]
            dqkv_ref[rows, ATT_WIDTH:ATT_WIDTH + LANES] = _bf(_rope_t(dkpad[nxt, :], cs, sl, sr))
            dqkv_ref[rows, ATT_WIDTH + LANES:QKV] = _bf(dvpad[nxt, :])
            return carry

        @pl.when(part == splits - 1)
        def _():
            lax.fori_loop(0, nb, finish, 0)

        _ride_wait(ride_modes, step, B * splits, ride_in, ride_out, sems)

    seq = lambda w, j: pl.BlockSpec((None, T, w), lambda b, s: (b, 0, j))
    full = lambda r, w: pl.BlockSpec((r, w), lambda b, s: (0, 0))
    return pl.pallas_call(
        body, name="attn_bwd", grid=(B, splits),
        in_specs=[seq(ATT_WIDTH, 0), seq(LANES, 0), seq(LANES, 5), seq(ATT_WIDTH, 0), seq(ATT_WIDTH, 0),
                  full(T, LANES), full(T, LANES), full(T, LANES),
                  pl.BlockSpec(memory_space=pltpu.SMEM), full(1, ATT_WIDTH)] + [ANY_SPEC] * nr,
        out_specs=[seq(QKV, 0), pl.BlockSpec((None, 8, LANES), lambda b, s: (b, 0, 0)),
                   pl.BlockSpec((None, 8, ATT_WIDTH), lambda b, s: (b, 0, 0))] + [ANY_SPEC] * nr,
        out_shape=[SDS((B, T, QKV), BF16), SDS((B, 8, LANES), F32), SDS((B, 8, ATT_WIDTH), F32)]
        + _exchange_shapes(ride_srcs, ride_modes),
        scratch_shapes=[pltpu.VMEM((T + WINDOW, LANES), BF16), pltpu.VMEM((T + WINDOW, LANES), BF16),
                        pltpu.VMEM((T + WINDOW, LANES), F32), pltpu.VMEM((T + WINDOW, LANES), F32),
                        pltpu.VMEM((WINDOW, ATT_WIDTH), F32), pltpu.VMEM((8, LANES), F32)] + _exchange_sems(nr),
        compiler_params=_params(("arbitrary", "arbitrary"), VMEM_LIMIT_BIG),
    )(qr, kr, proj3, attn_o, dan, cos, sinl, sinr, sinks, attn_w, *ride_srcs)


def _in_bwd(x2, dx1, dqkv, dhq, dhf, dhi, dhg, mod8, pre_w, w_in_bf, T, ride_srcs, ride_modes):
    N = x2.shape[0]
    TM = _tile_rows(T, big=True)
    tps = T // TM
    nr = len(ride_srcs)
    pieces = [(0, ATT_WIDTH + 2 * LANES), (768, HG_WIDTH), (1280, HG_WIDTH), (1792, HG_WIDTH), (2304, HG_WIDTH)]

    def body(*refs):
        x_ref, dx_ref, p0, p1, p2, p3, p4, mod_ref, pw_ref, w_ref = refs[:10]
        ride_in = refs[10:10 + nr]
        gx_ref, dproj_ref, acc_ref = refs[10 + nr:13 + nr]
        ride_out = refs[13 + nr:13 + 2 * nr]
        sems = refs[13 + 2 * nr:]
        _ride_start(ride_modes, pl.program_id(0), N // TM, ride_in, ride_out, sems)
        sc1 = mod_ref[1:2, :]
        dh = jnp.zeros((TM, D_MODEL), F32)
        for ref, (off, width) in zip((p0, p1, p2, p3, p4), pieces):
            pb = ref[...]
            dproj_ref[:, off:off + width] = pb
            dh = dh + _dot(pb, w_ref[off:off + width, :])
        x = x_ref[...]
        r = lax.rsqrt(_mean_last(x * x) + EPS)
        xh = x * r
        n1 = xh * pw_ref[...]
        dsh1 = _sum_rows(dh)
        dsc1 = _sum_rows(dh * n1)
        dn1 = dh * (1.0 + sc1)
        dw_pre = _sum_rows(dn1 * xh)
        dxh = dn1 * pw_ref[...]
        gx_ref[...] = dx_ref[...] + r * (dxh - xh * _mean_last(dxh * xh))
        _acc_rows(acc_ref, pl.program_id(0) % tps == 0, [dsh1, dsc1, dw_pre])
        _ride_wait(ride_modes, pl.program_id(0), N // TM, ride_in, ride_out, sems)

    row = lambda w: pl.BlockSpec((TM, w), lambda i: (i, 0))
    B = N // T
    return pl.pallas_call(
        body, name="in_bwd", grid=(N // TM,),
        in_specs=[row(D_MODEL), row(D_MODEL), row(768), row(HG_WIDTH), row(HG_WIDTH), row(HG_WIDTH),
                  row(HG_WIDTH), _mod_spec(tps), pl.BlockSpec((1, D_MODEL), lambda i: (0, 0)),
                  pl.BlockSpec((IN_COLS, D_MODEL), lambda i: (0, 0))] + [ANY_SPEC] * nr,
        out_specs=[row(D_MODEL), row(IN_COLS), _mod_spec(tps)] + [ANY_SPEC] * nr,
        out_shape=[SDS((N, D_MODEL), F32), SDS((N, IN_COLS), BF16), SDS((B, 8, D_MODEL), F32)]
        + _exchange_shapes(ride_srcs, ride_modes),
        scratch_shapes=_exchange_sems(nr),
        compiler_params=_params(("arbitrary",), VMEM_LIMIT_BIG),
    )(x2, dx1, dqkv, dhq, dhf, dhi, dhg, mod8, pre_w, w_in_bf, *ride_srcs)


def _matmul_tn(name, a, b, tn, tm=512, by_owner_cols=False):
    K, M = a.shape
    Nc = b.shape[1]
    tm = min(tm, M)

    def body(a_ref, b_ref, o_ref):
        o_ref[...] = _bf(_dot_tn(a_ref[...], b_ref[...]))

    if by_owner_cols:
        assert tn * N_DEV == Nc
        out_shape = SDS((N_DEV, M, tn), BF16)
        out_spec = pl.BlockSpec((None, tm, tn), lambda i, j: (j, i, 0))
    else:
        out_shape = SDS((M, Nc), BF16)
        out_spec = pl.BlockSpec((tm, tn), lambda i, j: (i, j))
    return pl.pallas_call(
        body, name=name, grid=(M // tm, Nc // tn),
        in_specs=[pl.BlockSpec((K, tm), lambda i, j: (0, i)),
                  pl.BlockSpec((K, tn), lambda i, j: (0, j))],
        out_specs=out_spec, out_shape=out_shape,
        compiler_params=_params(("arbitrary", "arbitrary"), VMEM_LIMIT_BIG),
    )(a, b)


def _adamw_math(w, g, m, v):
    m2 = ADAM_B1 * m + (1.0 - ADAM_B1) * g
    v2 = ADAM_B2 * v + (1.0 - ADAM_B2) * (g * g)
    m_hat = m2 / (1.0 - ADAM_B1 ** ADAM_STEP)
    v_hat = v2 / (1.0 - ADAM_B2 ** ADAM_STEP)
    delta = -ADAM_LR * (m_hat / (jnp.sqrt(v_hat) + ADAM_EPS) + ADAM_WD * w)
    return delta, m2, v2


def _pair_add(name, gw, theirs):
    chips, _, r, c = gw.shape
    tr = r
    core = lax.axis_index("c").astype(jnp.int32).reshape(1)

    def body(core_ref, mine_ref, theirs_ref, o_ref):
        o_ref[...] = _bf(mine_ref[...].astype(F32) + theirs_ref[...].astype(F32))

    block = pl.BlockSpec((None, tr, c), lambda s, i, core_ref: (s, i, 0))
    grid_spec = pltpu.PrefetchScalarGridSpec(
        num_scalar_prefetch=1, grid=(chips, r // tr),
        in_specs=[pl.BlockSpec((None, None, tr, c), lambda s, i, core_ref: (s, core_ref[0], i, 0)), block],
        out_specs=block)
    return pl.pallas_call(
        body, name=name, grid_spec=grid_spec, out_shape=SDS((chips, r, c), BF16),
        compiler_params=_params(("arbitrary", "arbitrary")),
    )(core, gw, theirs)


def _reduce_adamw(name, parts, w, m, v):
    r, c = w.shape
    tr = r if r % 256 else 256
    slots = parts.shape[0]

    def body(p_ref, w_ref, m_ref, v_ref, g_ref, d_ref, m2_ref, v2_ref):
        g = p_ref[0].astype(F32)
        for s in range(1, slots):
            g = g + p_ref[s].astype(F32)
        g_ref[...] = g
        d_ref[...], m2_ref[...], v2_ref[...] = _adamw_math(w_ref[...], g, m_ref[...], v_ref[...])

    blk = pl.BlockSpec((tr, c), lambda i: (i, 0))
    return pl.pallas_call(
        body, name=name, grid=(r // tr,),
        in_specs=[pl.BlockSpec((slots, tr, c), lambda i: (0, i, 0)), blk, blk, blk],
        out_specs=[blk] * 4, out_shape=[SDS((r, c), F32)] * 4,
        compiler_params=_params(("arbitrary",), VMEM_LIMIT_BIG),
    )(parts, w, m, v)


def _ada_grad_adamw(c_all, dmod_all, w, m, v):
    r, c = w.shape
    tr = 256
    nb = c_all.shape[0]

    def body(c_ref, dm_ref, w_ref, m_ref, v_ref, g_ref, d_ref, m2_ref, v2_ref):
        cv = c_ref[...]
        g = _dot_tn(cv * _sigmoid(cv), dm_ref[...])
        g_ref[...] = g
        d_ref[...], m2_ref[...], v2_ref[...] = _adamw_math(w_ref[...], g, m_ref[...], v_ref[...])

    blk = pl.BlockSpec((tr, c), lambda i: (i, 0))
    return pl.pallas_call(
        body, name="ada_grad_adamw", grid=(r // tr,),
        in_specs=[pl.BlockSpec((nb, tr), lambda i: (0, i)), pl.BlockSpec((nb, c), lambda i: (0, 0)),
                  blk, blk, blk],
        out_specs=[blk] * 4, out_shape=[SDS((r, c), F32)] * 4,
        compiler_params=_params(("arbitrary",)),
    )(c_all, dmod_all, w, m, v)


_SMALL = [("b_ada", 6144), ("pre_w_mix", 1024), ("attn_sinks", 128), ("attn_out_w", 512), ("lb_table", 1024),
          ("hg_norm_w", 128), ("post_w_mix", 1024), ("pre_w_mlp", 1024), ("post_w_mlp", 1024)]


def _pack_small(acc_in, acc_mix, acc_mlp, dsink, daw, dlb, dgw, lb_p, ada_cols):
    B = acc_in.shape[0]
    width = sum(w for _, w in _SMALL) + LANES

    def body(ain, amix, amlp, dsk_ref, daw_ref, dlb_ref, dgw_ref, lbp_ref, packed_ref, dmod_ref):
        def total(ref, r, w=None):
            out = ref[0, r:r + 1, :] if w is None else ref[0, r:r + 1, :w]
            for b in range(1, B):
                out = out + (ref[b, r:r + 1, :] if w is None else ref[b, r:r + 1, :w])
            return out

        d_b_ada = None
        for b in range(B):
            mods = [ain[b, 0:1, :], ain[b, 1:2, :], amix[b, 0:1, :], amlp[b, 0:1, :], amlp[b, 1:2, :], amlp[b, 2:3, :]]
            full = jnp.concatenate(mods, axis=1)
            for j in range(N_DEV):
                dmod_ref[j, b:b + 1, :] = full[:, j * ada_cols:(j + 1) * ada_cols]
            d_b_ada = full if d_b_ada is None else d_b_ada + full
        d_lb = total(dlb_ref, 0)
        pp = lbp_ref[0:1, :] * lbp_ref[1:2, :]
        pieces = [d_b_ada, total(ain, 2), total(dsk_ref, 0), total(daw_ref, 0), -d_lb * pp, d_lb * pp,
                  total(dgw_ref, 0), total(amix, 1), total(amlp, 3), total(amlp, 4), total(amlp, 5, LANES)]
        off = 0
        for piece in pieces:
            packed_ref[:, off:off + piece.shape[1]] = piece
            off += piece.shape[1]

    return pl.pallas_call(
        body, name="pack_small",
        out_shape=[SDS((1, width), F32), SDS((N_DEV, B, ada_cols), F32)],
    )(acc_in, acc_mix, acc_mlp, dsink, daw, dlb, dgw, lb_p)


def _adamw_small(parts, given):
    names = [n for n, _ in _SMALL]
    flat_in = [a for n in names for a in given[n]]

    def body(*refs):
        p_ref = refs[0]
        in_refs = refs[1:1 + 3 * len(names)]
        out_refs = refs[1 + 3 * len(names):-1]
        loss_ref = refs[-1]
        g = p_ref[0]
        for s in range(1, N_DEV):
            g = g + p_ref[s]
        off = 0
        for i, (name, width) in enumerate(_SMALL):
            w_ref, m_ref, v_ref = in_refs[3 * i:3 * i + 3]
            rows, cols = w_ref.shape
            for r in range(rows):
                gr = g[:, off + r * cols:off + (r + 1) * cols]
                res = (gr,) + _adamw_math(w_ref[r:r + 1, :], gr, m_ref[r:r + 1, :], v_ref[r:r + 1, :])
                for o_ref, val in zip(out_refs[4 * i:4 * i + 4], res):
                    o_ref[r:r + 1, :] = val
            off += width
        loss_ref[...] = g[:, off:off + LANES]

    out_shape = [SDS(given[n][0].shape, F32) for n in names for _ in range(4)] + [SDS((1, LANES), F32)]
    outs = pl.pallas_call(body, name="adamw_small", out_shape=out_shape)(parts, *flat_in)
    return {n: tuple(outs[4 * i:4 * i + 4]) for i, n in enumerate(names)}, outs[-1][0, 0]


def kernel(x, c, w_ada, b_ada, pre_w_mix, w_in, attn_sinks, attn_out_w, lb_table, hg_norm_w, w_out, post_w_mix, pre_w_mlp, w_up, w_down, post_w_mlp, loss_target, m_w_ada, m_b_ada, m_pre_w_mix, m_w_in, m_attn_sinks, m_attn_out_w, m_lb_table, m_hg_norm_w, m_w_out, m_post_w_mix, m_pre_w_mlp, m_w_up, m_w_down, m_post_w_mlp, v_w_ada, v_b_ada, v_pre_w_mix, v_w_in, v_attn_sinks, v_attn_out_w, v_lb_table, v_hg_norm_w, v_w_out, v_post_w_mix, v_pre_w_mlp, v_w_up, v_w_down, v_post_w_mlp):
    B, T, _ = x.shape
    N = B * T
    me = 4 * lax.axis_index("x") + 2 * lax.axis_index("y") + lax.axis_index("c")
    x2 = x.reshape(N, D_MODEL)
    tgt2 = loss_target.reshape(N, D_MODEL)

    w_in_t, m_w_in_t, v_w_in_t = w_in[0].T, m_w_in[0].T, v_w_in[0].T
    w_in_g, c_g = _exchange("gather_w_in", [_bf(w_in_t), c], ["gather"] * 2)
    w_in_f = w_in_g.reshape(IN_COLS, D_MODEL)
    c_all = c_g.reshape(N_DEV * B, D_MODEL)

    ada_cols = w_ada.shape[2]
    b_mine = lax.dynamic_slice(b_ada, (0, me * ada_cols), (1, ada_cols))
    mod_cols = _ada_mod(c_all, w_ada[0], b_mine)
    (mod_g,) = _exchange("scatter_mod", [mod_cols.reshape(N_DEV, B, ada_cols)], ["a2a"])
    mod = mod_g.transpose(1, 0, 2).reshape(B, 6, D_MODEL)
    mod8 = jnp.pad(mod, ((0, 0), (0, 2), (0, 0)))

    lb_p = jax.nn.softmax(lb_table, axis=0)
    lb = lb_p[1:2]
    tables = _rope_tables(T)

    w_up_b, w_down_b = _bf(w_up[0]), _bf(w_down[0])
    proj_a, proj_h, h1, w_out_g = _in_proj(x2, mod8, pre_w_mix, w_in_f, T, [_bf(w_out[0])], ["gather"])
    proj3 = proj_a.reshape(B, T, ATT_COLS)
    proj_h = proj_h.reshape(B, T, IN_COLS - ATT_COLS)
    rec_o, rec_g, s_prev, w_up_g0, w_up_g1 = _hgrn_fwd(proj_h, lb, hg_norm_w,
                                                       [w_up_b[:MLP_HALF], w_up_b[MLP_HALF:]], ["gather"] * 2)
    attn_o, attn_n, qr, kr, w_down_g0 = _attn_fwd(proj3, tables, attn_sinks, attn_out_w,
                                                  [w_down_b[:, :MLP_HALF]], ["gather"])
    w_out_f = w_out_g.reshape(D_MODEL, D_MODEL)
    mix, x1, cat, w_down_g1 = _mix_out(x2, attn_n.reshape(N, ATT_WIDTH), rec_g.reshape(N, HG_WIDTH), mod8,
                                       post_w_mix, w_out_f, T, [w_down_b[:, MLP_HALF:]], ["gather"])
    w_up_halves = [w_up_g0, w_up_g1]
    w_down_halves = [w_down_g0.reshape(D_FF, MLP_HALF), w_down_g1.reshape(D_FF, MLP_HALF)]
    up, u, d, h2 = _mlp_fwd(x1, mod8, pre_w_mlp, w_up_halves, w_down_halves, T)

    dx1, dup, dd, acc_mlp = _mlp_bwd(x1, d, up, tgt2, mod8, pre_w_mlp, post_w_mlp, w_up_halves, w_down_halves, T)
    chips = N_DEV // 2
    by_chip = lambda a: a.reshape((chips, 2, a.shape[0] // N_DEV) + a.shape[1:])
    gw_up = _matmul_tn("grad_w_up", h2, dup, D_FF // N_DEV, by_owner_cols=True)
    gw_up = gw_up.reshape(chips, 2, D_MODEL, D_FF // N_DEV)
    gw_down = by_chip(_matmul_tn("grad_w_down", u, dd, 512))
    dan, drg, dmix, acc_mix, q_down, q_up = _mix_bwd(mix, dx1, mod8, post_w_mix, w_out_f, T,
                                                     [gw_down, gw_up], ["pair"] * 2)
    p_down, p_up = _pair_add("pair_add_w_down", gw_down, q_down), _pair_add("pair_add_w_up", gw_up, q_up)
    gw_out = _matmul_tn("grad_w_out", cat, dmix, 512).reshape(N_DEV, D_MODEL // N_DEV, D_MODEL)
    dhq, dhf, dhi, dhg, dlb_p, dgw_p, r_down, r_up = _hgrn_bwd(
        proj_h, lb, hg_norm_w, rec_o, s_prev, drg.reshape(B, T, HG_WIDTH), [p_down, p_up], ["chips"] * 2)
    dqkv, dsink_p, daw_p, r_out = _attn_bwd(qr, kr, proj3, attn_o, dan.reshape(B, T, ATT_WIDTH), tables,
                                            attn_sinks, attn_out_w, [gw_out], ["a2a"])
    flat = lambda a: a.reshape(N, a.shape[-1])
    grad_x, dproj, acc_in = _in_bwd(x2, dx1, flat(dqkv), flat(dhq), flat(dhf), flat(dhi), flat(dhg),
                                    mod8, pre_w_mix, w_in_f, T, [], [])

    gw_in = by_chip(_matmul_tn("grad_w_in", dproj, h1, 512, tm=IN_COLS // 2))
    (q_in,) = _exchange("pair_w_in", [gw_in], ["pair"])
    p_in = _pair_add("pair_add_w_in", gw_in, q_in)

    packed, dmod_blocks = _pack_small(acc_in, acc_mix, acc_mlp, dsink_p, daw_p, dlb_p, dgw_p, lb_p, ada_cols)
    r_in, r_dmod, r_small = _exchange("reduce_grads", [p_in, dmod_blocks, packed], ["chips", "a2a", "gather"])

    res = {}
    res["w_in"] = tuple(a.T for a in _reduce_adamw("adamw_w_in", r_in, w_in_t, m_w_in_t, v_w_in_t))
    res["w_out"] = _reduce_adamw("adamw_w_out", r_out, w_out[0], m_w_out[0], v_w_out[0])
    res["w_up"] = _reduce_adamw("adamw_w_up", r_up, w_up[0], m_w_up[0], v_w_up[0])
    res["w_down"] = _reduce_adamw("adamw_w_down", r_down, w_down[0], m_w_down[0], v_w_down[0])
    res["w_ada"] = _ada_grad_adamw(c_all, r_dmod.reshape(N_DEV * B, ada_cols), w_ada[0], m_w_ada[0], v_w_ada[0])

    given = dict(b_ada=(b_ada, m_b_ada, v_b_ada), pre_w_mix=(pre_w_mix, m_pre_w_mix, v_pre_w_mix),
                 attn_sinks=(attn_sinks, m_attn_sinks, v_attn_sinks),
                 attn_out_w=(attn_out_w, m_attn_out_w, v_attn_out_w), lb_table=(lb_table, m_lb_table, v_lb_table),
                 hg_norm_w=(hg_norm_w, m_hg_norm_w, v_hg_norm_w), post_w_mix=(post_w_mix, m_post_w_mix, v_post_w_mix),
                 pre_w_mlp=(pre_w_mlp, m_pre_w_mlp, v_pre_w_mlp), post_w_mlp=(post_w_mlp, m_post_w_mlp, v_post_w_mlp))
    small_res, loss = _adamw_small(r_small, given)
    res.update(small_res)

    order = ["w_ada", "b_ada", "pre_w_mix", "w_in", "attn_sinks", "attn_out_w", "lb_table", "hg_norm_w", "w_out",
             "post_w_mix", "pre_w_mlp", "w_up", "w_down", "post_w_mlp"]
    big = {"w_ada", "w_in", "w_out", "w_up", "w_down"}
    outs = [loss, grad_x.reshape(B, T, D_MODEL)]
    for i in range(4):
        for k in order:
            a = res[k][i]
            outs.append(a[None] if k in big else a)
    return tuple(outs)
```

```python
import jax
import jax.numpy as jnp
import numpy as np
from jax import lax
from jax.experimental import pallas as pl
from jax.experimental.pallas import tpu as pltpu

F32 = jnp.float32
BF16 = jnp.bfloat16
SDS = jax.ShapeDtypeStruct

D_MODEL = 1024
ATT_WIDTH = 512
ATT_HEAD_DIM = 64
ATT_KV_HEADS = 2
ATT_GROUP = 4
WINDOW = 128
ROPE_DIM = 16
ROPE_THETA = 500000.0
HG_WIDTH = 512
HG_HEAD_DIM = 128
HG_HEADS = 4
HG_CHUNK = 32
IN_COLS = 2816
ATT_COLS = 768
D_FF = 4096
EPS = 1e-6
N_DEV = 8

ADAM_LR = 0.001
ADAM_B1 = 0.9
ADAM_B2 = 0.999
ADAM_EPS = 1e-08
ADAM_WD = 0.01
ADAM_STEP = 10

VMEM_LIMIT_BIG = 56 << 20
LANES = 128

MESH = pl.DeviceIdType.MESH
NT_DIMS = (((1,), (1,)), ((), ()))
TN_DIMS = (((0,), (0,)), ((), ()))


def _dot(a, b):
    return jnp.dot(a, b, preferred_element_type=F32)


def _dot_nt(a, b):
    return lax.dot_general(a, b, NT_DIMS, preferred_element_type=F32)


def _dot_tn(a, b):
    return lax.dot_general(a, b, TN_DIMS, preferred_element_type=F32)


def _bf(a):
    return a.astype(BF16)


def _sigmoid(a):
    return 1.0 / (1.0 + jnp.exp(-a))


def _mean_last(a):
    return jnp.mean(a, axis=-1, keepdims=True)


def _sum_rows(a):
    return jnp.sum(a, axis=0, keepdims=True)


def _loop_pairs(first, count, body, init, per_trip=2):
    if count % per_trip:
        return lax.fori_loop(first, first + count, body, init)

    def trip(i, c):
        for k in range(per_trip):
            c = body(first + per_trip * i + k, c)
        return c

    return lax.fori_loop(0, count // per_trip, trip, init)


def _params(sem=None, vmem=None):
    kw = {}
    if sem is not None:
        kw["dimension_semantics"] = sem
    if vmem is not None:
        kw["vmem_limit_bytes"] = vmem
    return pltpu.CompilerParams(**kw)


ANY_SPEC = pl.BlockSpec(memory_space=pl.ANY)


def _exchange_shapes(srcs, modes):
    out_shape = []
    for s, m in zip(srcs, modes):
        shp = {"gather": (N_DEV,) + tuple(s.shape), "pair": (s.shape[0],) + tuple(s.shape[2:])}.get(m, tuple(s.shape))
        out_shape.append(SDS(shp, s.dtype))
    return out_shape


def _exchange_sems(n):
    if n == 0:
        return []
    return [pltpu.SemaphoreType.DMA((n, N_DEV - 1)), pltpu.SemaphoreType.DMA((n, N_DEV - 1)),
            pltpu.SemaphoreType.DMA((n,))]


SIBLING = 1
OTHER_CHIPS = (2, 4, 6)


def _related(k):
    x, y, c = lax.axis_index("x"), lax.axis_index("y"), lax.axis_index("c")
    px, py, pc = x ^ ((k >> 2) & 1), y ^ ((k >> 1) & 1), c ^ (k & 1)
    return (px, py, pc), 4 * px + 2 * py + pc


def _exchange_phases(modes, src_refs, out_refs, send_sems, recv_sems, own_sems):
    _, me = _related(0)
    sib_dev, sib = _related(SIBLING)
    start, middle, end = [], [], []

    def remote(a, i, src, dst, dev):
        return pltpu.make_async_remote_copy(src_ref=src, dst_ref=dst, send_sem=send_sems.at[a, i],
                                            recv_sem=recv_sems.at[a, i], device_id=dev, device_id_type=MESH)

    for a, mode in enumerate(modes):
        out = out_refs[a]
        if mode == "gather":
            src = src_refs[a]
            own = pltpu.make_async_copy(src, out.at[me], own_sems.at[a])
            to_sib = remote(a, 0, src, out.at[me], sib_dev)
            start += [own.start, to_sib.start]
            end += [remote(a, 0, src, out.at[sib], sib_dev).wait_recv, to_sib.wait_send, own.wait]
            for j, k in enumerate(OTHER_CHIPS, start=1):
                dev, peer = _related(k)
                _, peer_sib = _related(k ^ SIBLING)
                send = remote(a, j, src, out.at[me], dev)
                passed = remote(a, 3 + j, out.at[peer], out.at[peer], sib_dev)
                start.append(send.start)
                middle += [remote(a, j, src, out.at[peer], dev).wait_recv, passed.start]
                end += [remote(a, 3 + j, out.at[peer_sib], out.at[peer_sib], sib_dev).wait_recv,
                        send.wait_send, passed.wait_send]
        elif mode == "pair":
            core = lax.axis_index("c")
            for s in range(N_DEV // 2):
                send = remote(a, s, src_refs[a].at[s, 1 - core], out.at[s], sib_dev)
                start.append(send.start)
                end += [remote(a, s, src_refs[a].at[s, 1 - core], out.at[s], sib_dev).wait_recv, send.wait_send]
        elif mode == "chips":
            chip = me // 2
            own = pltpu.make_async_copy(src_refs[a].at[chip], out.at[chip], own_sems.at[a])
            start.append(own.start)
            end.append(own.wait)
            for j, k in enumerate(OTHER_CHIPS, start=1):
                dev, peer = _related(k)
                send = remote(a, j, src_refs[a].at[peer // 2], out.at[chip], dev)
                start.append(send.start)
                end += [remote(a, j, src_refs[a].at[peer // 2], out.at[peer // 2], dev).wait_recv, send.wait_send]
        else:
            own = pltpu.make_async_copy(src_refs[a].at[me], out.at[me], own_sems.at[a])
            start.append(own.start)
            end.append(own.wait)
            for k in range(1, N_DEV):
                dev, peer = _related(k)
                send = remote(a, k - 1, src_refs[a].at[peer], out.at[me], dev)
                start.append(send.start)
                end += [remote(a, k - 1, src_refs[a].at[peer], out.at[peer], dev).wait_recv, send.wait_send]
    return start, middle, end


def _run(actions):
    for act in actions:
        act()


def _exchange(name, srcs, modes):
    n = len(srcs)

    def body(*refs):
        start, middle, end = _exchange_phases(modes, refs[:n], refs[n:2 * n], *refs[2 * n:])
        _run(start)
        _run(middle)
        _run(end)

    return pl.pallas_call(
        body, name=name, out_shape=_exchange_shapes(srcs, modes),
        in_specs=[ANY_SPEC] * n, out_specs=[ANY_SPEC] * n,
        scratch_shapes=_exchange_sems(n),
    )(*srcs)


def _ride_start(modes, step, steps, src_refs, out_refs, sems):
    if not modes:
        return
    middle_step = steps - 1

    @pl.when(step == 0)
    def _():
        _run(_exchange_phases(modes, src_refs, out_refs, *sems)[0])

    if "gather" in modes:
        @pl.when(step == middle_step)
        def _():
            _run(_exchange_phases(modes, src_refs, out_refs, *sems)[1])


def _ride_wait(modes, step, steps, src_refs, out_refs, sems):
    if not modes:
        return

    @pl.when(step == steps - 1)
    def _():
        _run(_exchange_phases(modes, src_refs, out_refs, *sems)[2])


def _ada_mod(c_all, w_ada, b_ada_mine):
    nb, cols = c_all.shape[0], w_ada.shape[1]

    def body(c_ref, w_ref, b_ref, o_ref):
        cv = c_ref[...]
        ca = cv * _sigmoid(cv)
        o_ref[...] = _dot(ca, w_ref[...]) + b_ref[...]

    return pl.pallas_call(body, name="ada_mod", out_shape=SDS((nb, cols), F32))(c_all, w_ada, b_ada_mine)


def _tile_rows(T, big=False):
    return min(512 if big else 256, T)


def _mod_spec(tps):
    return pl.BlockSpec((None, 8, D_MODEL), lambda i: (i // tps, 0, 0))


def _in_proj(x2, mod8, pre_w, w_in_bf, T, ride_srcs, ride_modes):
    N = x2.shape[0]
    TM = _tile_rows(T, big=True)
    tps = T // TM
    nr = len(ride_srcs)

    def body(*refs):
        x_ref, mod_ref, pw_ref, w_ref = refs[:4]
        ride_in = refs[4:4 + nr]
        pa_ref, ph_ref, h1_ref = refs[4 + nr:7 + nr]
        ride_out = refs[7 + nr:7 + 2 * nr]
        sems = refs[7 + 2 * nr:]
        _ride_start(ride_modes, pl.program_id(0), N // TM, ride_in, ride_out, sems)
        x = x_ref[...]
        r = lax.rsqrt(_mean_last(x * x) + EPS)
        h = (x * r * pw_ref[...]) * (1.0 + mod_ref[1:2, :]) + mod_ref[0:1, :]
        hb = _bf(h)
        h1_ref[...] = hb
        pa_ref[...] = _dot_nt(hb, w_ref[:ATT_COLS, :])
        ph_ref[...] = _dot_nt(hb, w_ref[ATT_COLS:, :])
        _ride_wait(ride_modes, pl.program_id(0), N // TM, ride_in, ride_out, sems)

    return pl.pallas_call(
        body, name="in_proj", grid=(N // TM,),
        in_specs=[pl.BlockSpec((TM, D_MODEL), lambda i: (i, 0)), _mod_spec(tps),
                  pl.BlockSpec((1, D_MODEL), lambda i: (0, 0)),
                  pl.BlockSpec((IN_COLS, D_MODEL), lambda i: (0, 0))] + [ANY_SPEC] * nr,
        out_specs=[pl.BlockSpec((TM, ATT_COLS), lambda i: (i, 0)),
                   pl.BlockSpec((TM, IN_COLS - ATT_COLS), lambda i: (i, 0)),
                   pl.BlockSpec((TM, D_MODEL), lambda i: (i, 0))] + [ANY_SPEC] * nr,
        out_shape=[SDS((N, ATT_COLS), F32), SDS((N, IN_COLS - ATT_COLS), F32), SDS((N, D_MODEL), BF16)]
        + _exchange_shapes(ride_srcs, ride_modes),
        scratch_shapes=_exchange_sems(nr),
        compiler_params=_params(("arbitrary",), VMEM_LIMIT_BIG),
    )(x2, mod8, pre_w, w_in_bf, *ride_srcs)


def _rope_tables(T):
    half = ROPE_DIM // 2
    f32 = np.float32
    inv_freq = (f32(ROPE_THETA) ** (-np.arange(0, ROPE_DIM, 2, dtype=f32) / f32(ROPE_DIM))).astype(f32)
    ang = np.arange(T, dtype=f32)[:, None] * inv_freq[None, :]
    cos, sin = np.cos(ang).astype(f32), np.sin(ang).astype(f32)
    ones = np.ones((T, ATT_HEAD_DIM - ROPE_DIM), f32)
    zeros = np.zeros((T, ATT_HEAD_DIM - ROPE_DIM), f32)
    zh = np.zeros((T, half), f32)
    cos64 = np.concatenate([cos, cos, ones], axis=1)
    sin_left = np.concatenate([-sin, zh, zeros], axis=1)
    sin_right = np.concatenate([zh, sin, zeros], axis=1)
    rep = LANES // ATT_HEAD_DIM
    return tuple(jnp.asarray(np.tile(t, (1, rep))) for t in (cos64, sin_left, sin_right))


def _rope(xc, cs, sl, sr):
    return xc * cs + pltpu.roll(xc, LANES - 8, 1) * sl + pltpu.roll(xc, 8, 1) * sr


def _rope_t(dy, cs, sl, sr):
    return dy * cs + pltpu.roll(dy * sl, 8, 1) + pltpu.roll(dy * sr, LANES - 8, 1)


ATT_SCALE = ATT_HEAD_DIM ** -0.5
ATT_SPLITS = 4


def _lower_mask():
    j = lax.broadcasted_iota(jnp.int32, (WINDOW, ATT_GROUP * WINDOW), 0)
    i = lax.broadcasted_iota(jnp.int32, (WINDOW, ATT_GROUP * WINDOW), 1) & (WINDOW - 1)
    return j <= i


def _sink_row(sink_ref, hk):
    return jnp.concatenate(
        [jnp.full((1, WINDOW), sink_ref[0, ATT_GROUP * hk + g], F32) for g in range(ATT_GROUP)], axis=1)


def _softmax_window(qs, k_cur, k_prev, lower, has_prev, sink):
    s_prev = jnp.where(has_prev, _dot_nt(k_prev, qs), jnp.finfo(F32).min)
    s = jnp.where(lower, _dot_nt(k_cur, qs), s_prev)
    m = jnp.maximum(jnp.max(s, axis=0, keepdims=True), sink)
    p = jnp.exp(s - m)
    es = jnp.exp(sink - m)
    inv = 1.0 / (jnp.sum(p, axis=0, keepdims=True) + es)
    return p, inv, es


def _stack_heads(parts, hk):
    hs = []
    for g in range(ATT_GROUP):
        h = ATT_GROUP * hk + g
        hs.append(parts[h // 2][:, (h % 2) * ATT_HEAD_DIM:(h % 2 + 1) * ATT_HEAD_DIM])
    return jnp.concatenate(hs, axis=0)


def _attn_fwd(proj3, tables, sinks, attn_w, ride_srcs, ride_modes):
    B, T, _ = proj3.shape
    nb = T // WINDOW
    splits = min(ATT_SPLITS, nb)
    per = nb // splits
    nr = len(ride_srcs)
    cos, sinl, sinr = tables

    def body(*refs):
        q_ref, k_ref, v_ref, cos_ref, sl_ref, sr_ref, sink_ref, aw_ref = refs[:8]
        ride_in = refs[8:8 + nr]
        o_ref, an_ref, qr_ref, kr_ref = refs[8 + nr:12 + nr]
        ride_out = refs[12 + nr:12 + 2 * nr]
        kpad, vpad = refs[12 + 2 * nr:14 + 2 * nr]
        sems = refs[14 + 2 * nr:]
        part = pl.program_id(1)
        step = pl.program_id(0) * splits + part
        _ride_start(ride_modes, step, B * splits, ride_in, ride_out, sems)

        @pl.when(part == 0)
        def _():
            kpad[0:WINDOW, :] = jnp.zeros((WINDOW, LANES), BF16)
            vpad[0:WINDOW, :] = jnp.zeros((WINDOW, LANES), BF16)

        lower = _lower_mask()

        def block(n, carry):
            r0 = pl.multiple_of(n * WINDOW, WINDOW)
            rows = pl.ds(r0, WINDOW)
            nxt = pl.ds(r0 + WINDOW, WINDOW)
            cs, sl, sr = cos_ref[rows, :], sl_ref[rows, :], sr_ref[rows, :]
            kb = _bf(_rope(k_ref[rows, :], cs, sl, sr))
            vb = _bf(v_ref[rows, :])
            kpad[nxt, :] = kb
            kr_ref[rows, :] = kb
            vpad[nxt, :] = vb
            qparts = []
            for j in range(ATT_WIDTH // LANES):
                qp = _bf(_rope(q_ref[rows, j * LANES:(j + 1) * LANES], cs, sl, sr) * ATT_SCALE)
                qr_ref[rows, j * LANES:(j + 1) * LANES] = qp
                qparts.append(qp)
            for hk in range(ATT_KV_HEADS):
                lanes = slice(hk * ATT_HEAD_DIM, (hk + 1) * ATT_HEAD_DIM)
                qs = _stack_heads(qparts, hk)
                p, inv, _ = _softmax_window(qs, kb[:, lanes], kpad[rows, lanes], lower, n > 0,
                                            _sink_row(sink_ref, hk))
                p_cur = jnp.where(lower, p, 0.0)
                ot = (_dot_tn(vb[:, lanes], _bf(p_cur)) + _dot_tn(vpad[rows, lanes], _bf(p - p_cur))) * inv
                for g in range(ATT_GROUP):
                    h = ATT_GROUP * hk + g
                    o_ref[rows, h * ATT_HEAD_DIM:(h + 1) * ATT_HEAD_DIM] = ot[:, g * WINDOW:(g + 1) * WINDOW].T
            ob = o_ref[rows, :]
            an_ref[rows, :] = _bf(ob * lax.rsqrt(_mean_last(ob * ob) + EPS) * aw_ref[...])
            return carry

        _loop_pairs(part * per, per, block, 0)
        _ride_wait(ride_modes, step, B * splits, ride_in, ride_out, sems)

    seq = lambda w, j: pl.BlockSpec((None, T, w), lambda b, s: (b, 0, j))
    full = lambda r, w: pl.BlockSpec((r, w), lambda b, s: (0, 0))
    return pl.pallas_call(
        body, name="attn_fwd", grid=(B, splits),
        in_specs=[seq(ATT_WIDTH, 0), seq(LANES, 4), seq(LANES, 5),
                  full(T, LANES), full(T, LANES), full(T, LANES),
                  pl.BlockSpec(memory_space=pltpu.SMEM), full(1, ATT_WIDTH)] + [ANY_SPEC] * nr,
        out_specs=[seq(ATT_WIDTH, 0), seq(ATT_WIDTH, 0), seq(ATT_WIDTH, 0), seq(LANES, 0)] + [ANY_SPEC] * nr,
        out_shape=[SDS((B, T, ATT_WIDTH), F32), SDS((B, T, ATT_WIDTH), BF16),
                   SDS((B, T, ATT_WIDTH), BF16), SDS((B, T, LANES), BF16)] + _exchange_shapes(ride_srcs, ride_modes),
        scratch_shapes=[pltpu.VMEM((T + WINDOW, LANES), BF16), pltpu.VMEM((T + WINDOW, LANES), BF16)]
        + _exchange_sems(nr),
        compiler_params=_params(("arbitrary", "arbitrary"), VMEM_LIMIT_BIG),
    )(proj3, proj3, proj3, cos, sinl, sinr, sinks, attn_w, *ride_srcs)


HG_GROUP = 8
HG_ROWS = HG_GROUP * HG_CHUNK


HG_STACK = HG_GROUP * HG_HEAD_DIM


def _group_mask():
    r = lax.broadcasted_iota(jnp.int32, (HG_ROWS, HG_ROWS), 0)
    c = lax.broadcasted_iota(jnp.int32, (HG_ROWS, HG_ROWS), 1)
    return ((r // HG_CHUNK) == (c // HG_CHUNK)) & (r >= c)


def _row_chunk():
    return lax.broadcasted_iota(jnp.int32, (HG_ROWS, HG_HEAD_DIM), 0) // HG_CHUNK


def _spread(a, row_chunk):
    return jnp.concatenate([jnp.where(row_chunk == c, a, jnp.zeros_like(a)) for c in range(HG_GROUP)], axis=1)


def _pick(r, row_chunk):
    out = jnp.where(row_chunk == 0, r[:, :HG_HEAD_DIM], 0.0)
    for c in range(1, HG_GROUP):
        out = out + jnp.where(row_chunk == c, r[:, c * HG_HEAD_DIM:(c + 1) * HG_HEAD_DIM], 0.0)
    return out


def _lane_block(a, c):
    return a[:, c * HG_HEAD_DIM:(c + 1) * HG_HEAD_DIM]


def _chunk_cumsum(a, reverse=False):
    n = a.shape[0]
    pos = lax.broadcasted_iota(jnp.int32, a.shape, 0) % HG_CHUNK
    shift = 1
    while shift < HG_CHUNK:
        if reverse:
            a = a + jnp.where(pos < HG_CHUNK - shift, pltpu.roll(a, n - shift, 0), 0.0)
        else:
            a = a + jnp.where(pos >= shift, pltpu.roll(a, shift, 0), 0.0)
        shift *= 2
    return a


def _chunk_bcast(rows_1x128):
    return jnp.concatenate([jnp.broadcast_to(r, (HG_CHUNK, HG_HEAD_DIM)) for r in rows_1x128], axis=0)


def _hgrn_gates(hq, hf, lb):
    sq = _sigmoid(hq)
    q = hq * sq
    sg = _sigmoid(hf)
    f = lb + (1.0 - lb) * sg
    k = 1.0 - f
    logf = jnp.log(f)
    b = _chunk_cumsum(logf)
    bl = [_sum_rows(logf[_chunk_rows(c), :]) for c in range(HG_GROUP)]
    eb, enb, e2 = jnp.exp(b), jnp.exp(-b), jnp.exp(_chunk_bcast(bl) - b)
    ebl = [jnp.exp(r) for r in bl]
    return dict(sq=sq, sg=sg, f=f, eb=eb, enb=enb, e2=e2, ebl=ebl, qd=q * eb, kd=k * enb, k2=k * e2)


def _chunk_rows(c):
    return slice(c * HG_CHUNK, (c + 1) * HG_CHUNK)


def _head_lanes(h):
    return slice(h * HG_HEAD_DIM, (h + 1) * HG_HEAD_DIM)


def _hgrn_fwd(proj_h, lb, hg_w, ride_srcs, ride_modes):
    B, T, _ = proj_h.shape
    ng = T // HG_ROWS
    nr = len(ride_srcs)

    def body(*refs):
        hq_ref, hf_ref, hi_ref, hg_ref, lb_ref, gw_ref = refs[:6]
        ride_in = refs[6:6 + nr]
        o_ref, rg_ref, sp_ref = refs[6 + nr:9 + nr]
        ride_out = refs[9 + nr:9 + 2 * nr]
        st = refs[9 + 2 * nr]
        sems = refs[10 + 2 * nr:]
        gi = pl.program_id(1)
        step = pl.program_id(0) * ng + gi
        _ride_start(ride_modes, step, B * ng, ride_in, ride_out, sems)

        @pl.when(gi == 0)
        def _():
            st[...] = jnp.zeros(st.shape, F32)

        lo = _group_mask()
        row_chunk = _row_chunk()
        for h in range(HG_HEADS):
            lanes = _head_lanes(h)
            gt = _hgrn_gates(hq_ref[:, lanes], hf_ref[:, lanes], lb_ref[:, lanes])
            v, qd, kd = _bf(hi_ref[:, lanes]), _bf(gt["qd"]), _bf(gt["kd"])
            a = jnp.where(lo, _dot_nt(qd, kd), 0.0)
            kv = _dot_tn(v, _bf(_spread(gt["k2"], row_chunk)))
            s = st[h]
            before = []
            for c in range(HG_GROUP):
                before.append(s)
                s = s * gt["ebl"][c] + _lane_block(kv, c)
            st[h] = s
            sp = jnp.concatenate(before, axis=1)
            sp_ref[h] = sp
            o = _dot(_bf(a), v) + _dot_nt(_bf(_spread(gt["qd"], row_chunk)), _bf(sp))
            o_ref[:, lanes] = o
            hg = hg_ref[:, lanes]
            rn = o * lax.rsqrt(_mean_last(o * o) + EPS) * gw_ref[...]
            rg_ref[:, lanes] = _bf(rn * (hg * _sigmoid(hg)))
        _ride_wait(ride_modes, step, B * ng, ride_in, ride_out, sems)

    part = lambda j: pl.BlockSpec((None, HG_ROWS, HG_WIDTH), lambda b, g: (b, g, j))
    return pl.pallas_call(
        body, name="hgrn_fwd", grid=(B, ng),
        in_specs=[part(0), part(1), part(2), part(3),
                  pl.BlockSpec((1, HG_WIDTH), lambda b, g: (0, 0)),
                  pl.BlockSpec((1, LANES), lambda b, g: (0, 0))] + [ANY_SPEC] * nr,
        out_specs=[part(0), part(0),
                   pl.BlockSpec((None, HG_HEADS, None, HG_HEAD_DIM, HG_STACK), lambda b, g: (b, 0, g, 0, 0))]
        + [ANY_SPEC] * nr,
        out_shape=[SDS((B, T, HG_WIDTH), F32), SDS((B, T, HG_WIDTH), BF16),
                   SDS((B, HG_HEADS, ng, HG_HEAD_DIM, HG_STACK), F32)] + _exchange_shapes(ride_srcs, ride_modes),
        scratch_shapes=[pltpu.VMEM((HG_HEADS, HG_HEAD_DIM, HG_HEAD_DIM), F32)] + _exchange_sems(nr),
        compiler_params=_params(("arbitrary", "arbitrary"), VMEM_LIMIT_BIG),
    )(proj_h, proj_h, proj_h, proj_h, lb, hg_w, *ride_srcs)


def _mix_out(x2, attn_n, rec_g, mod8, post_w, w_out_bf, T, ride_srcs, ride_modes):
    N = x2.shape[0]
    TM = _tile_rows(T, big=True)
    tps = T // TM
    nr = len(ride_srcs)

    def body(*refs):
        x_ref, an_ref, rg_ref, mod_ref, pw_ref, w_ref = refs[:6]
        ride_in = refs[6:6 + nr]
        mix_ref, x1_ref, cat_ref = refs[6 + nr:9 + nr]
        ride_out = refs[9 + nr:9 + 2 * nr]
        sems = refs[9 + 2 * nr:]
        _ride_start(ride_modes, pl.program_id(0), N // TM, ride_in, ride_out, sems)
        cat = jnp.concatenate([an_ref[...], rg_ref[...]], axis=1)
        cat_ref[...] = cat
        mix = _dot(cat, w_ref[...])
        mix_ref[...] = mix
        r = lax.rsqrt(_mean_last(mix * mix) + EPS)
        x1_ref[...] = x_ref[...] + mod_ref[2:3, :] * (mix * r * pw_ref[...])
        _ride_wait(ride_modes, pl.program_id(0), N // TM, ride_in, ride_out, sems)

    row = lambda w: pl.BlockSpec((TM, w), lambda i: (i, 0))
    return pl.pallas_call(
        body, name="mix_out", grid=(N // TM,),
        in_specs=[row(D_MODEL), row(ATT_WIDTH), row(HG_WIDTH), _mod_spec(tps),
                  pl.BlockSpec((1, D_MODEL), lambda i: (0, 0)),
                  pl.BlockSpec((D_MODEL, D_MODEL), lambda i: (0, 0))] + [ANY_SPEC] * nr,
        out_specs=[row(D_MODEL), row(D_MODEL), row(D_MODEL)] + [ANY_SPEC] * nr,
        out_shape=[SDS((N, D_MODEL), F32), SDS((N, D_MODEL), F32), SDS((N, D_MODEL), BF16)]
        + _exchange_shapes(ride_srcs, ride_modes),
        scratch_shapes=_exchange_sems(nr),
        compiler_params=_params(("arbitrary",), VMEM_LIMIT_BIG),
    )(x2, attn_n, rec_g, mod8, post_w, w_out_bf, *ride_srcs)


def _load_weights_once(pairs, sem):
    @pl.when(pl.program_id(0) == 0)
    def _():
        cps = [pltpu.make_async_copy(src, dst, sem.at[i]) for i, (src, dst) in enumerate(pairs)]
        for cp in cps:
            cp.start()
        for cp in cps:
            cp.wait()


MLP_HALF = D_MODEL // 2
MLP_PIECES = 2 * N_DEV + 2


def _mlp_weight_pieces(wu_a, wu_b, wd_a, wd_b, wu, wd):
    cols = D_FF // N_DEV
    pairs = []
    for h, half in enumerate((wu_a, wu_b)):
        for j in range(N_DEV):
            pairs.append((half.at[j], wu.at[pl.ds(h * MLP_HALF, MLP_HALF), pl.ds(j * cols, cols)]))
    for h, half in enumerate((wd_a, wd_b)):
        pairs.append((half, wd.at[:, pl.ds(h * MLP_HALF, MLP_HALF)]))
    return pairs


def _mlp_fwd(x1, mod8, pre_w, w_up_halves, w_down_halves, T):
    N = x1.shape[0]
    TM = _tile_rows(T)
    tps = T // TM

    def body(x_ref, mod_ref, pw_ref, wua, wub, wda, wdb, up_ref, u_ref, d_ref, h2_ref, wu, wd, sem):
        _load_weights_once(_mlp_weight_pieces(wua, wub, wda, wdb, wu, wd), sem)
        x = x_ref[...]
        r = lax.rsqrt(_mean_last(x * x) + EPS)
        h = (x * r * pw_ref[...]) * (1.0 + mod_ref[4:5, :]) + mod_ref[3:4, :]
        hb = _bf(h)
        h2_ref[...] = hb
        up = _dot(hb, wu[...])
        up_ref[...] = up
        ru = jnp.maximum(up, 0.0)
        u = _bf(ru * ru)
        u_ref[...] = u
        d_ref[...] = _dot(u, wd[...])

    row = lambda w: pl.BlockSpec((TM, w), lambda i: (i, 0))
    return pl.pallas_call(
        body, name="mlp_fwd", grid=(N // TM,),
        in_specs=[row(D_MODEL), _mod_spec(tps), pl.BlockSpec((1, D_MODEL), lambda i: (0, 0))] + [ANY_SPEC] * 4,
        out_specs=[row(D_FF), row(D_FF), row(D_MODEL), row(D_MODEL)],
        out_shape=[SDS((N, D_FF), F32), SDS((N, D_FF), BF16), SDS((N, D_MODEL), F32), SDS((N, D_MODEL), BF16)],
        scratch_shapes=[pltpu.VMEM((D_MODEL, D_FF), BF16), pltpu.VMEM((D_FF, D_MODEL), BF16),
                        pltpu.SemaphoreType.DMA((MLP_PIECES,))],
        compiler_params=_params(("arbitrary",), VMEM_LIMIT_BIG),
    )(x1, mod8, pre_w, *w_up_halves, *w_down_halves)


def _acc_rows(acc_ref, first, rows):
    @pl.when(first)
    def _():
        acc_ref[...] = jnp.zeros(acc_ref.shape, F32)
    for i, r in enumerate(rows):
        acc_ref[i:i + 1, :] += r


def _mlp_bwd(x1, d, up, tgt, mod8, pre_w, post_w, w_up_halves, w_down_halves, T):
    N = x1.shape[0]
    TM = _tile_rows(T)
    tps = T // TM

    def body(x_ref, d_ref, up_ref, t_ref, mod_ref, pw_ref, qw_ref, wua, wub, wda, wdb,
             dx_ref, dup_ref, dd_ref, acc_ref, wd, wu, sem):
        _load_weights_once(_mlp_weight_pieces(wua, wub, wda, wdb, wu, wd), sem)
        sh2, sc2, g2 = mod_ref[3:4, :], mod_ref[4:5, :], mod_ref[5:6, :]
        x = x_ref[...]
        r1 = lax.rsqrt(_mean_last(x * x) + EPS)
        xh = x * r1
        n2 = xh * pw_ref[...]
        dv = d_ref[...]
        rd = lax.rsqrt(_mean_last(dv * dv) + EPS)
        dh = dv * rd
        rr = dh * qw_ref[...]
        e = x + g2 * rr - t_ref[...]
        loss = 0.5 * jnp.sum(_sum_rows(e * e), axis=1, keepdims=True) / D_MODEL
        dy = e * (1.0 / D_MODEL)
        dg2 = _sum_rows(dy * rr)
        drr = dy * g2
        dw_post = _sum_rows(drr * dh)
        ddh = drr * qw_ref[...]
        dd = _bf(rd * (ddh - dh * _mean_last(ddh * dh)))
        dd_ref[...] = dd
        ru = jnp.maximum(up_ref[...], 0.0)
        dup = _bf(_dot_nt(dd, wd[...]) * (2.0 * ru))
        dup_ref[...] = dup
        dh2 = _dot_nt(dup, wu[...])
        dsh2 = _sum_rows(dh2)
        dsc2 = _sum_rows(dh2 * n2)
        dn2 = dh2 * (1.0 + sc2)
        dw_pre = _sum_rows(dn2 * xh)
        dxh = dn2 * pw_ref[...]
        dx_ref[...] = dy + r1 * (dxh - xh * _mean_last(dxh * xh))
        _acc_rows(acc_ref, pl.program_id(0) % tps == 0,
                  [dsh2, dsc2, dg2, dw_pre, dw_post, jnp.broadcast_to(loss, (1, D_MODEL))])

    row = lambda w: pl.BlockSpec((TM, w), lambda i: (i, 0))
    vec = pl.BlockSpec((1, D_MODEL), lambda i: (0, 0))
    B = N // T
    return pl.pallas_call(
        body, name="mlp_bwd", grid=(N // TM,),
        in_specs=[row(D_MODEL), row(D_MODEL), row(D_FF), row(D_MODEL), _mod_spec(tps), vec, vec] + [ANY_SPEC] * 4,
        out_specs=[row(D_MODEL), row(D_FF), row(D_MODEL), _mod_spec(tps)],
        out_shape=[SDS((N, D_MODEL), F32), SDS((N, D_FF), BF16), SDS((N, D_MODEL), BF16),
                   SDS((B, 8, D_MODEL), F32)],
        scratch_shapes=[pltpu.VMEM((D_FF, D_MODEL), BF16), pltpu.VMEM((D_MODEL, D_FF), BF16),
                        pltpu.SemaphoreType.DMA((MLP_PIECES,))],
        compiler_params=_params(("arbitrary",), VMEM_LIMIT_BIG),
    )(x1, d, up, tgt, mod8, pre_w, post_w, *w_up_halves, *w_down_halves)


def _mix_bwd(mix, dx1, mod8, post_w, w_out_bf, T, ride_srcs, ride_modes):
    N = mix.shape[0]
    TM = _tile_rows(T, big=True)
    tps = T // TM
    nr = len(ride_srcs)

    def body(*refs):
        mix_ref, dx_ref, mod_ref, pw_ref, w_ref = refs[:5]
        ride_in = refs[5:5 + nr]
        dan_ref, drg_ref, dmix_ref, acc_ref = refs[5 + nr:9 + nr]
        ride_out = refs[9 + nr:9 + 2 * nr]
        sems = refs[9 + 2 * nr:]
        _ride_start(ride_modes, pl.program_id(0), N // TM, ride_in, ride_out, sems)
        g1 = mod_ref[2:3, :]
        mix = mix_ref[...]
        dx1 = dx_ref[...]
        rm = lax.rsqrt(_mean_last(mix * mix) + EPS)
        mh = mix * rm
        dg1 = _sum_rows(dx1 * (mh * pw_ref[...]))
        dr = dx1 * g1
        dw_post = _sum_rows(dr * mh)
        dmh = dr * pw_ref[...]
        dmix = _bf(rm * (dmh - mh * _mean_last(dmh * mh)))
        dmix_ref[...] = dmix
        dcat = _dot_nt(dmix, w_ref[...])
        dan_ref[...] = dcat[:, :ATT_WIDTH]
        drg_ref[...] = dcat[:, ATT_WIDTH:]
        _acc_rows(acc_ref, pl.program_id(0) % tps == 0, [dg1, dw_post])
        _ride_wait(ride_modes, pl.program_id(0), N // TM, ride_in, ride_out, sems)

    row = lambda w: pl.BlockSpec((TM, w), lambda i: (i, 0))
    B = N // T
    return pl.pallas_call(
        body, name="mix_bwd", grid=(N // TM,),
        in_specs=[row(D_MODEL), row(D_MODEL), _mod_spec(tps), pl.BlockSpec((1, D_MODEL), lambda i: (0, 0)),
                  pl.BlockSpec((D_MODEL, D_MODEL), lambda i: (0, 0))] + [ANY_SPEC] * nr,
        out_specs=[row(ATT_WIDTH), row(HG_WIDTH), row(D_MODEL), _mod_spec(tps)] + [ANY_SPEC] * nr,
        out_shape=[SDS((N, ATT_WIDTH), F32), SDS((N, HG_WIDTH), F32), SDS((N, D_MODEL), BF16),
                   SDS((B, 8, D_MODEL), F32)] + _exchange_shapes(ride_srcs, ride_modes),
        scratch_shapes=_exchange_sems(nr),
        compiler_params=_params(("arbitrary",), VMEM_LIMIT_BIG),
    )(mix, dx1, mod8, post_w, w_out_bf, *ride_srcs)


def _hgrn_bwd(proj_h, lb, hg_w, o, s_prev, drg, ride_srcs, ride_modes):
    B, T, _ = proj_h.shape
    ng = T // HG_ROWS
    nr = len(ride_srcs)

    def body(*refs):
        hq_ref, hf_ref, hi_ref, hg_ref, lb_ref, gw_ref, o_ref, sp_ref, drg_ref = refs[:9]
        ride_in = refs[9:9 + nr]
        dhq_ref, dhf_ref, dhi_ref, dhg_ref, dlb_ref, dgw_ref = refs[9 + nr:15 + nr]
        ride_out = refs[15 + nr:15 + 2 * nr]
        dst = refs[15 + 2 * nr]
        sems = refs[16 + 2 * nr:]
        step = pl.program_id(0) * ng + pl.program_id(1)
        _ride_start(ride_modes, step, B * ng, ride_in, ride_out, sems)

        @pl.when(pl.program_id(1) == 0)
        def _():
            dst[...] = jnp.zeros(dst.shape, F32)
            dlb_ref[...] = jnp.zeros(dlb_ref.shape, F32)
            dgw_ref[...] = jnp.zeros(dgw_ref.shape, F32)

        lo = _group_mask()
        row_chunk = _row_chunk()
        gw = gw_ref[...]

        for h in range(HG_HEADS):
            lanes = _head_lanes(h)
            lbv = lb_ref[:, lanes]
            hq = hq_ref[:, lanes]
            gt = _hgrn_gates(hq, hf_ref[:, lanes], lbv)
            sq, sg, qdf, kdf, k2f, ebl = gt["sq"], gt["sg"], gt["qd"], gt["kd"], gt["k2"], gt["ebl"]
            v, qd, kd = _bf(hi_ref[:, lanes]), _bf(qdf), _bf(kdf)
            ov = o_ref[:, lanes]
            hg = hg_ref[:, lanes]
            shg = _sigmoid(hg)
            dr = drg_ref[:, lanes]
            ro = lax.rsqrt(_mean_last(ov * ov) + EPS)
            oh = ov * ro
            dhg_ref[:, lanes] = _bf(dr * (oh * gw) * (shg + hg * shg * (1.0 - shg)))
            drn = dr * (hg * shg)
            dgw_ref[...] += jnp.broadcast_to(_sum_rows(drn * oh), (8, LANES))
            doh = drn * gw
            do = _bf(ro * (doh - oh * _mean_last(doh * oh)))
            a = jnp.where(lo, _dot_nt(qd, kd), 0.0)
            da = _bf(jnp.where(lo, _dot_nt(do, v), 0.0))
            dv = _dot_tn(_bf(a), do)
            dqd = _dot(da, kd)
            dkd = _dot_tn(da, qd)
            sp = sp_ref[h]
            incr = _dot_tn(do, _bf(_spread(qdf, row_chunk)))
            ds = dst[h]
            after = [None] * HG_GROUP
            for c in reversed(range(HG_GROUP)):
                after[c] = ds
                ds = ds * ebl[c] + _lane_block(incr, c)
            dst[h] = ds
            dss = jnp.concatenate(after, axis=1)
            dssb = _bf(dss)
            dk2 = _pick(_dot(v, dssb), row_chunk)
            dhi_ref[:, lanes] = _bf(dv + _dot_nt(_bf(_spread(k2f, row_chunk)), dssb))
            dqd = dqd + _pick(_dot(do, _bf(sp)), row_chunk)
            debl = _sum_rows(dss * sp)
            k2g = dk2 * k2f
            db = dqd * qdf - dkd * kdf - k2g
            dk = dkd * gt["enb"] + dk2 * gt["e2"]
            dbl = _chunk_bcast([_lane_block(debl, c) * ebl[c] + _sum_rows(k2g[_chunk_rows(c), :])
                                for c in range(HG_GROUP)])
            dg = _chunk_cumsum(db, reverse=True) + dbl
            df = dg / gt["f"] - dk
            dhf_ref[:, lanes] = _bf(df * (1.0 - lbv) * sg * (1.0 - sg))
            dlb_ref[:, lanes] += jnp.broadcast_to(_sum_rows(df * (1.0 - sg)), (8, LANES))
            dhq_ref[:, lanes] = _bf((dqd * gt["eb"]) * (sq + hq * sq * (1.0 - sq)))
        _ride_wait(ride_modes, step, B * ng, ride_in, ride_out, sems)

    part = lambda j: pl.BlockSpec((None, HG_ROWS, HG_WIDTH), lambda b, g: (b, ng - 1 - g, j))
    return pl.pallas_call(
        body, name="hgrn_bwd", grid=(B, ng),
        in_specs=[part(0), part(1), part(2), part(3),
                  pl.BlockSpec((1, HG_WIDTH), lambda b, g: (0, 0)),
                  pl.BlockSpec((1, LANES), lambda b, g: (0, 0)),
                  part(0),
                  pl.BlockSpec((None, HG_HEADS, None, HG_HEAD_DIM, HG_STACK), lambda b, g: (b, 0, ng - 1 - g, 0, 0)),
                  part(0)] + [ANY_SPEC] * nr,
        out_specs=[part(0), part(0), part(0), part(0),
                   pl.BlockSpec((None, 8, HG_WIDTH), lambda b, g: (b, 0, 0)),
                   pl.BlockSpec((None, 8, LANES), lambda b, g: (b, 0, 0))] + [ANY_SPEC] * nr,
        out_shape=[SDS((B, T, HG_WIDTH), BF16)] * 4 + [SDS((B, 8, HG_WIDTH), F32), SDS((B, 8, LANES), F32)]
        + _exchange_shapes(ride_srcs, ride_modes),
        scratch_shapes=[pltpu.VMEM((HG_HEADS, HG_HEAD_DIM, HG_HEAD_DIM), F32)] + _exchange_sems(nr),
        compiler_params=_params(("arbitrary", "arbitrary"), VMEM_LIMIT_BIG),
    )(proj_h, proj_h, proj_h, proj_h, lb, hg_w, o, s_prev, drg, *ride_srcs)


def _attn_bwd(qr, kr, proj3, attn_o, dan, tables, sinks, attn_w, ride_srcs, ride_modes):
    B, T, _ = proj3.shape
    nb = T // WINDOW
    splits = min(ATT_SPLITS, nb)
    per = nb // splits
    nr = len(ride_srcs)
    cos, sinl, sinr = tables
    QKV = ATT_WIDTH + 2 * LANES

    def body(*refs):
        qr_ref, kr_ref, v_ref, o_ref, dan_ref, cos_ref, sl_ref, sr_ref, sink_ref, aw_ref = refs[:10]
        ride_in = refs[10:10 + nr]
        dqkv_ref, dsink_ref, daw_ref = refs[10 + nr:13 + nr]
        ride_out = refs[13 + nr:13 + 2 * nr]
        kpad, vpad, dkpad, dvpad, dqb, dsk = refs[13 + 2 * nr:19 + 2 * nr]
        sems = refs[19 + 2 * nr:]
        part = pl.program_id(1)
        step = pl.program_id(0) * splits + part
        _ride_start(ride_modes, step, B * splits, ride_in, ride_out, sems)

        @pl.when(part == 0)
        def _():
            kpad[0:WINDOW, :] = jnp.zeros((WINDOW, LANES), BF16)
            vpad[0:WINDOW, :] = jnp.zeros((WINDOW, LANES), BF16)
            kpad[WINDOW:, :] = kr_ref[...]
            vpad[WINDOW:, :] = _bf(v_ref[...])
            dkpad[...] = jnp.zeros(dkpad.shape, F32)
            dvpad[...] = jnp.zeros(dvpad.shape, F32)
            dsk[...] = jnp.zeros(dsk.shape, F32)
            daw_ref[...] = jnp.zeros(daw_ref.shape, F32)

        lower = _lower_mask()
        aw = aw_ref[...]

        def block(n, daw):
            r0 = pl.multiple_of(n * WINDOW, WINDOW)
            rows = pl.ds(r0, WINDOW)
            nxt = pl.ds(r0 + WINDOW, WINDOW)
            ob = o_ref[rows, :]
            dn = dan_ref[rows, :]
            ro = lax.rsqrt(_mean_last(ob * ob) + EPS)
            oh = ob * ro
            daw = daw + _sum_rows(dn * oh)
            doh = dn * aw
            do = _bf(ro * (doh - oh * _mean_last(doh * oh)))
            doparts = [do[:, j * LANES:(j + 1) * LANES] for j in range(ATT_WIDTH // LANES)]
            qparts = [qr_ref[rows, j * LANES:(j + 1) * LANES] for j in range(ATT_WIDTH // LANES)]
            for hk in range(ATT_KV_HEADS):
                lanes = slice(hk * ATT_HEAD_DIM, (hk + 1) * ATT_HEAD_DIM)
                qs = _stack_heads(qparts, hk)
                dos = _stack_heads(doparts, hk)
                k_cur, k_prev = kpad[nxt, lanes], kpad[rows, lanes]
                v_cur, v_prev = vpad[nxt, lanes], vpad[rows, lanes]
                p, inv, es = _softmax_window(qs, k_cur, k_prev, lower, n > 0, _sink_row(sink_ref, hk))
                p = p * inv
                dp = jnp.where(lower, _dot_nt(v_cur, dos), _dot_nt(v_prev, dos))
                delta = jnp.sum(p * dp, axis=0, keepdims=True)
                ds = p * (dp - delta)
                sk = (es * inv) * delta
                ds_cur = jnp.where(lower, ds, 0.0)
                p_cur = jnp.where(lower, p, 0.0)
                ds_cur, ds_prev = _bf(ds_cur), _bf(ds - ds_cur)
                p_cur, p_prev = _bf(p_cur), _bf(p - p_cur)
                dqt = (_dot_tn(k_cur, ds_cur) + _dot_tn(k_prev, ds_prev)) * ATT_SCALE
                dkpad[nxt, lanes] += _dot(ds_cur, qs)
                dkpad[rows, lanes] += _dot(ds_prev, qs)
                dvpad[nxt, lanes] += _dot(p_cur, dos)
                dvpad[rows, lanes] += _dot(p_prev, dos)
                for g in range(ATT_GROUP):
                    h = ATT_GROUP * hk + g
                    cols = slice(g * WINDOW, (g + 1) * WINDOW)
                    dqb[:, h * ATT_HEAD_DIM:(h + 1) * ATT_HEAD_DIM] = dqt[:, cols].T
                    head_lane = lax.broadcasted_iota(jnp.int32, dsk.shape, 1) == h
                    dsk[...] += jnp.where(head_lane, -jnp.sum(sk[:, cols], axis=1, keepdims=True), 0.0)
            cs, sl, sr = cos_ref[rows, :], sl_ref[rows, :], sr_ref[rows, :]
            for j in range(ATT_WIDTH // LANES):
                dqkv_ref[rows, j * LANES:(j + 1) * LANES] = _bf(_rope_t(dqb[:, j * LANES:(j + 1) * LANES], cs, sl, sr))
            return daw

        daw = _loop_pairs(part * per, per, block, jnp.zeros((1, ATT_WIDTH), F32))
        daw_ref[...] += jnp.broadcast_to(daw, (8, ATT_WIDTH))
        dsink_ref[...] = dsk[...]

        def finish(n, carry):
            r0 = pl.multiple_of(n * WINDOW, WINDOW)
            rows = pl.ds(r0, WINDOW)
            nxt = pl.ds(r0 + WINDOW, WINDOW)
            cs, sl, sr = cos_ref[rows, :], sl_ref[rows, :], sr_ref[rows, :]
            dqkv_ref[rows, ATT_WIDTH:ATT_WIDTH + LANES] = _bf(_rope_t(dkpad[nxt, :], cs, sl, sr))
            dqkv_ref[rows, ATT_WIDTH + LANES:QKV] = _bf(dvpad[nxt, :])
            return carry

        @pl.when(part == splits - 1)
        def _():
            lax.fori_loop(0, nb, finish, 0)

        _ride_wait(ride_modes, step, B * splits, ride_in, ride_out, sems)

    seq = lambda w, j: pl.BlockSpec((None, T, w), lambda b, s: (b, 0, j))
    full = lambda r, w: pl.BlockSpec((r, w), lambda b, s: (0, 0))
    return pl.pallas_call(
        body, name="attn_bwd", grid=(B, splits),
        in_specs=[seq(ATT_WIDTH, 0), seq(LANES, 0), seq(LANES, 5), seq(ATT_WIDTH, 0), seq(ATT_WIDTH, 0),
                  full(T, LANES), full(T, LANES), full(T, LANES),
                  pl.BlockSpec(memory_space=pltpu.SMEM), full(1, ATT_WIDTH)] + [ANY_SPEC] * nr,
        out_specs=[seq(QKV, 0), pl.BlockSpec((None, 8, LANES), lambda b, s: (b, 0, 0)),
                   pl.BlockSpec((None, 8, ATT_WIDTH), lambda b, s: (b, 0, 0))] + [ANY_SPEC] * nr,
        out_shape=[SDS((B, T, QKV), BF16), SDS((B, 8, LANES), F32), SDS((B, 8, ATT_WIDTH), F32)]
        + _exchange_shapes(ride_srcs, ride_modes),
        scratch_shapes=[pltpu.VMEM((T + WINDOW, LANES), BF16), pltpu.VMEM((T + WINDOW, LANES), BF16),
                        pltpu.VMEM((T + WINDOW, LANES), F32), pltpu.VMEM((T + WINDOW, LANES), F32),
                        pltpu.VMEM((WINDOW, ATT_WIDTH), F32), pltpu.VMEM((8, LANES), F32)] + _exchange_sems(nr),
        compiler_params=_params(("arbitrary", "arbitrary"), VMEM_LIMIT_BIG),
    )(qr, kr, proj3, attn_o, dan, cos, sinl, sinr, sinks, attn_w, *ride_srcs)


def _in_bwd(x2, dx1, dqkv, dhq, dhf, dhi, dhg, mod8, pre_w, w_in_bf, T, ride_srcs, ride_modes):
    N = x2.shape[0]
    TM = _tile_rows(T, big=True)
    tps = T // TM
    nr = len(ride_srcs)
    pieces = [(0, ATT_WIDTH + 2 * LANES), (768, HG_WIDTH), (1280, HG_WIDTH), (1792, HG_WIDTH), (2304, HG_WIDTH)]

    def body(*refs):
        x_ref, dx_ref, p0, p1, p2, p3, p4, mod_ref, pw_ref, w_ref = refs[:10]
        ride_in = refs[10:10 + nr]
        gx_ref, dproj_ref, acc_ref = refs[10 + nr:13 + nr]
        ride_out = refs[13 + nr:13 + 2 * nr]
        sems = refs[13 + 2 * nr:]
        _ride_start(ride_modes, pl.program_id(0), N // TM, ride_in, ride_out, sems)
        sc1 = mod_ref[1:2, :]
        dh = jnp.zeros((TM, D_MODEL), F32)
        for ref, (off, width) in zip((p0, p1, p2, p3, p4), pieces):
            pb = ref[...]
            dproj_ref[:, off:off + width] = pb
            dh = dh + _dot(pb, w_ref[off:off + width, :])
        x = x_ref[...]
        r = lax.rsqrt(_mean_last(x * x) + EPS)
        xh = x * r
        n1 = xh * pw_ref[...]
        dsh1 = _sum_rows(dh)
        dsc1 = _sum_rows(dh * n1)
        dn1 = dh * (1.0 + sc1)
        dw_pre = _sum_rows(dn1 * xh)
        dxh = dn1 * pw_ref[...]
        gx_ref[...] = dx_ref[...] + r * (dxh - xh * _mean_last(dxh * xh))
        _acc_rows(acc_ref, pl.program_id(0) % tps == 0, [dsh1, dsc1, dw_pre])
        _ride_wait(ride_modes, pl.program_id(0), N // TM, ride_in, ride_out, sems)

    row = lambda w: pl.BlockSpec((TM, w), lambda i: (i, 0))
    B = N // T
    return pl.pallas_call(
        body, name="in_bwd", grid=(N // TM,),
        in_specs=[row(D_MODEL), row(D_MODEL), row(768), row(HG_WIDTH), row(HG_WIDTH), row(HG_WIDTH),
                  row(HG_WIDTH), _mod_spec(tps), pl.BlockSpec((1, D_MODEL), lambda i: (0, 0)),
                  pl.BlockSpec((IN_COLS, D_MODEL), lambda i: (0, 0))] + [ANY_SPEC] * nr,
        out_specs=[row(D_MODEL), row(IN_COLS), _mod_spec(tps)] + [ANY_SPEC] * nr,
        out_shape=[SDS((N, D_MODEL), F32), SDS((N, IN_COLS), BF16), SDS((B, 8, D_MODEL), F32)]
        + _exchange_shapes(ride_srcs, ride_modes),
        scratch_shapes=_exchange_sems(nr),
        compiler_params=_params(("arbitrary",), VMEM_LIMIT_BIG),
    )(x2, dx1, dqkv, dhq, dhf, dhi, dhg, mod8, pre_w, w_in_bf, *ride_srcs)


def _matmul_tn(name, a, b, tn, tm=512, by_owner_cols=False):
    K, M = a.shape
    Nc = b.shape[1]
    tm = min(tm, M)

    def body(a_ref, b_ref, o_ref):
        o_ref[...] = _bf(_dot_tn(a_ref[...], b_ref[...]))

    if by_owner_cols:
        assert tn * N_DEV == Nc
        out_shape = SDS((N_DEV, M, tn), BF16)
        out_spec = pl.BlockSpec((None, tm, tn), lambda i, j: (j, i, 0))
    else:
        out_shape = SDS((M, Nc), BF16)
        out_spec = pl.BlockSpec((tm, tn), lambda i, j: (i, j))
    return pl.pallas_call(
        body, name=name, grid=(M // tm, Nc // tn),
        in_specs=[pl.BlockSpec((K, tm), lambda i, j: (0, i)),
                  pl.BlockSpec((K, tn), lambda i, j: (0, j))],
        out_specs=out_spec, out_shape=out_shape,
        compiler_params=_params(("arbitrary", "arbitrary"), VMEM_LIMIT_BIG),
    )(a, b)


def _adamw_math(w, g, m, v):
    m2 = ADAM_B1 * m + (1.0 - ADAM_B1) * g
    v2 = ADAM_B2 * v + (1.0 - ADAM_B2) * (g * g)
    m_hat = m2 / (1.0 - ADAM_B1 ** ADAM_STEP)
    v_hat = v2 / (1.0 - ADAM_B2 ** ADAM_STEP)
    delta = -ADAM_LR * (m_hat / (jnp.sqrt(v_hat) + ADAM_EPS) + ADAM_WD * w)
    return delta, m2, v2


def _pair_add(name, gw, theirs):
    chips, _, r, c = gw.shape
    tr = r
    core = lax.axis_index("c").astype(jnp.int32).reshape(1)

    def body(core_ref, mine_ref, theirs_ref, o_ref):
        o_ref[...] = _bf(mine_ref[...].astype(F32) + theirs_ref[...].astype(F32))

    block = pl.BlockSpec((None, tr, c), lambda s, i, core_ref: (s, i, 0))
    grid_spec = pltpu.PrefetchScalarGridSpec(
        num_scalar_prefetch=1, grid=(chips, r // tr),
        in_specs=[pl.BlockSpec((None, None, tr, c), lambda s, i, core_ref: (s, core_ref[0], i, 0)), block],
        out_specs=block)
    return pl.pallas_call(
        body, name=name, grid_spec=grid_spec, out_shape=SDS((chips, r, c), BF16),
        compiler_params=_params(("arbitrary", "arbitrary")),
    )(core, gw, theirs)


def _reduce_adamw(name, parts, w, m, v):
    r, c = w.shape
    tr = r if r % 256 else 256
    slots = parts.shape[0]

    def body(p_ref, w_ref, m_ref, v_ref, g_ref, d_ref, m2_ref, v2_ref):
        g = p_ref[0].astype(F32)
        for s in range(1, slots):
            g = g + p_ref[s].astype(F32)
        g_ref[...] = g
        d_ref[...], m2_ref[...], v2_ref[...] = _adamw_math(w_ref[...], g, m_ref[...], v_ref[...])

    blk = pl.BlockSpec((tr, c), lambda i: (i, 0))
    return pl.pallas_call(
        body, name=name, grid=(r // tr,),
        in_specs=[pl.BlockSpec((slots, tr, c), lambda i: (0, i, 0)), blk, blk, blk],
        out_specs=[blk] * 4, out_shape=[SDS((r, c), F32)] * 4,
        compiler_params=_params(("arbitrary",), VMEM_LIMIT_BIG),
    )(parts, w, m, v)


def _ada_grad_adamw(c_all, dmod_all, w, m, v):
    r, c = w.shape
    tr = 256
    nb = c_all.shape[0]

    def body(c_ref, dm_ref, w_ref, m_ref, v_ref, g_ref, d_ref, m2_ref, v2_ref):
        cv = c_ref[...]
        g = _dot_tn(cv * _sigmoid(cv), dm_ref[...])
        g_ref[...] = g
        d_ref[...], m2_ref[...], v2_ref[...] = _adamw_math(w_ref[...], g, m_ref[...], v_ref[...])

    blk = pl.BlockSpec((tr, c), lambda i: (i, 0))
    return pl.pallas_call(
        body, name="ada_grad_adamw", grid=(r // tr,),
        in_specs=[pl.BlockSpec((nb, tr), lambda i: (0, i)), pl.BlockSpec((nb, c), lambda i: (0, 0)),
                  blk, blk, blk],
        out_specs=[blk] * 4, out_shape=[SDS((r, c), F32)] * 4,
        compiler_params=_params(("arbitrary",)),
    )(c_all, dmod_all, w, m, v)


_SMALL = [("b_ada", 6144), ("pre_w_mix", 1024), ("attn_sinks", 128), ("attn_out_w", 512), ("lb_table", 1024),
          ("hg_norm_w", 128), ("post_w_mix", 1024), ("pre_w_mlp", 1024), ("post_w_mlp", 1024)]


def _pack_small(acc_in, acc_mix, acc_mlp, dsink, daw, dlb, dgw, lb_p, ada_cols):
    B = acc_in.shape[0]
    width = sum(w for _, w in _SMALL) + LANES

    def body(ain, amix, amlp, dsk_ref, daw_ref, dlb_ref, dgw_ref, lbp_ref, packed_ref, dmod_ref):
        def total(ref, r, w=None):
            out = ref[0, r:r + 1, :] if w is None else ref[0, r:r + 1, :w]
            for b in range(1, B):
                out = out + (ref[b, r:r + 1, :] if w is None else ref[b, r:r + 1, :w])
            return out

        d_b_ada = None
        for b in range(B):
            mods = [ain[b, 0:1, :], ain[b, 1:2, :], amix[b, 0:1, :], amlp[b, 0:1, :], amlp[b, 1:2, :], amlp[b, 2:3, :]]
            full = jnp.concatenate(mods, axis=1)
            for j in range(N_DEV):
                dmod_ref[j, b:b + 1, :] = full[:, j * ada_cols:(j + 1) * ada_cols]
            d_b_ada = full if d_b_ada is None else d_b_ada + full
        d_lb = total(dlb_ref, 0)
        pp = lbp_ref[0:1, :] * lbp_ref[1:2, :]
        pieces = [d_b_ada, total(ain, 2), total(dsk_ref, 0), total(daw_ref, 0), -d_lb * pp, d_lb * pp,
                  total(dgw_ref, 0), total(amix, 1), total(amlp, 3), total(amlp, 4), total(amlp, 5, LANES)]
        off = 0
        for piece in pieces:
            packed_ref[:, off:off + piece.shape[1]] = piece
            off += piece.shape[1]

    return pl.pallas_call(
        body, name="pack_small",
        out_shape=[SDS((1, width), F32), SDS((N_DEV, B, ada_cols), F32)],
    )(acc_in, acc_mix, acc_mlp, dsink, daw, dlb, dgw, lb_p)


def _adamw_small(parts, given):
    names = [n for n, _ in _SMALL]
    flat_in = [a for n in names for a in given[n]]

    def body(*refs):
        p_ref = refs[0]
        in_refs = refs[1:1 + 3 * len(names)]
        out_refs = refs[1 + 3 * len(names):-1]
        loss_ref = refs[-1]
        g = p_ref[0]
        for s in range(1, N_DEV):
            g = g + p_ref[s]
        off = 0
        for i, (name, width) in enumerate(_SMALL):
            w_ref, m_ref, v_ref = in_refs[3 * i:3 * i + 3]
            rows, cols = w_ref.shape
            for r in range(rows):
                gr = g[:, off + r * cols:off + (r + 1) * cols]
                res = (gr,) + _adamw_math(w_ref[r:r + 1, :], gr, m_ref[r:r + 1, :], v_ref[r:r + 1, :])
                for o_ref, val in zip(out_refs[4 * i:4 * i + 4], res):
                    o_ref[r:r + 1, :] = val
            off += width
        loss_ref[...] = g[:, off:off + LANES]

    out_shape = [SDS(given[n][0].shape, F32) for n in names for _ in range(4)] + [SDS((1, LANES), F32)]
    outs = pl.pallas_call(body, name="adamw_small", out_shape=out_shape)(parts, *flat_in)
    return {n: tuple(outs[4 * i:4 * i + 4]) for i, n in enumerate(names)}, outs[-1][0, 0]


def kernel(x, c, w_ada, b_ada, pre_w_mix, w_in, attn_sinks, attn_out_w, lb_table, hg_norm_w, w_out, post_w_mix, pre_w_mlp, w_up, w_down, post_w_mlp, loss_target, m_w_ada, m_b_ada, m_pre_w_mix, m_w_in, m_attn_sinks, m_attn_out_w, m_lb_table, m_hg_norm_w, m_w_out, m_post_w_mix, m_pre_w_mlp, m_w_up, m_w_down, m_post_w_mlp, v_w_ada, v_b_ada, v_pre_w_mix, v_w_in, v_attn_sinks, v_attn_out_w, v_lb_table, v_hg_norm_w, v_w_out, v_post_w_mix, v_pre_w_mlp, v_w_up, v_w_down, v_post_w_mlp):
    B, T, _ = x.shape
    N = B * T
    me = 4 * lax.axis_index("x") + 2 * lax.axis_index("y") + lax.axis_index("c")
    x2 = x.reshape(N, D_MODEL)
    tgt2 = loss_target.reshape(N, D_MODEL)

    w_in_t, m_w_in_t, v_w_in_t = w_in[0].T, m_w_in[0].T, v_w_in[0].T
    w_in_g, c_g = _exchange("gather_w_in", [_bf(w_in_t), c], ["gather"] * 2)
    w_in_f = w_in_g.reshape(IN_COLS, D_MODEL)
    c_all = c_g.reshape(N_DEV * B, D_MODEL)

    ada_cols = w_ada.shape[2]
    b_mine = lax.dynamic_slice(b_ada, (0, me * ada_cols), (1, ada_cols))
    mod_cols = _ada_mod(c_all, w_ada[0], b_mine)
    (mod_g,) = _exchange("scatter_mod", [mod_cols.reshape(N_DEV, B, ada_cols)], ["a2a"])
    mod = mod_g.transpose(1, 0, 2).reshape(B, 6, D_MODEL)
    mod8 = jnp.pad(mod, ((0, 0), (0, 2), (0, 0)))

    lb_p = jax.nn.softmax(lb_table, axis=0)
    lb = lb_p[1:2]
    tables = _rope_tables(T)

    w_up_b, w_down_b = _bf(w_up[0]), _bf(w_down[0])
    proj_a, proj_h, h1, w_out_g, w_up_g0 = _in_proj(x2, mod8, pre_w_mix, w_in_f, T,
                                                    [_bf(w_out[0]), w_up_b[:MLP_HALF]], ["gather"] * 2)
    proj3 = proj_a.reshape(B, T, ATT_COLS)
    proj_h = proj_h.reshape(B, T, IN_COLS - ATT_COLS)
    rec_o, rec_g, s_prev, w_up_g1 = _hgrn_fwd(proj_h, lb, hg_norm_w, [w_up_b[MLP_HALF:]], ["gather"])
    attn_o, attn_n, qr, kr, w_down_g0 = _attn_fwd(proj3, tables, attn_sinks, attn_out_w,
                                                  [w_down_b[:, :MLP_HALF]], ["gather"])
    w_out_f = w_out_g.reshape(D_MODEL, D_MODEL)
    mix, x1, cat, w_down_g1 = _mix_out(x2, attn_n.reshape(N, ATT_WIDTH), rec_g.reshape(N, HG_WIDTH), mod8,
                                       post_w_mix, w_out_f, T, [w_down_b[:, MLP_HALF:]], ["gather"])
    w_up_halves = [w_up_g0, w_up_g1]
    w_down_halves = [w_down_g0.reshape(D_FF, MLP_HALF), w_down_g1.reshape(D_FF, MLP_HALF)]
    up, u, d, h2 = _mlp_fwd(x1, mod8, pre_w_mlp, w_up_halves, w_down_halves, T)

    dx1, dup, dd, acc_mlp = _mlp_bwd(x1, d, up, tgt2, mod8, pre_w_mlp, post_w_mlp, w_up_halves, w_down_halves, T)
    chips = N_DEV // 2
    by_chip = lambda a: a.reshape((chips, 2, a.shape[0] // N_DEV) + a.shape[1:])
    gw_up = _matmul_tn("grad_w_up", h2, dup, D_FF // N_DEV, by_owner_cols=True)
    gw_up = gw_up.reshape(chips, 2, D_MODEL, D_FF // N_DEV)
    gw_down = by_chip(_matmul_tn("grad_w_down", u, dd, 512))
    dan, drg, dmix, acc_mix, q_down, q_up = _mix_bwd(mix, dx1, mod8, post_w_mix, w_out_f, T,
                                                     [gw_down, gw_up], ["pair"] * 2)
    p_down, p_up = _pair_add("pair_add_w_down", gw_down, q_down), _pair_add("pair_add_w_up", gw_up, q_up)
    gw_out = _matmul_tn("grad_w_out", cat, dmix, 512).reshape(N_DEV, D_MODEL // N_DEV, D_MODEL)
    dhq, dhf, dhi, dhg, dlb_p, dgw_p, r_down, r_up = _hgrn_bwd(
        proj_h, lb, hg_norm_w, rec_o, s_prev, drg.reshape(B, T, HG_WIDTH), [p_down, p_up], ["chips"] * 2)
    dqkv, dsink_p, daw_p, r_out = _attn_bwd(qr, kr, proj3, attn_o, dan.reshape(B, T, ATT_WIDTH), tables,
                                            attn_sinks, attn_out_w, [gw_out], ["a2a"])
    flat = lambda a: a.reshape(N, a.shape[-1])
    grad_x, dproj, acc_in = _in_bwd(x2, dx1, flat(dqkv), flat(dhq), flat(dhf), flat(dhi), flat(dhg),
                                    mod8, pre_w_mix, w_in_f, T, [], [])

    gw_in = by_chip(_matmul_tn("grad_w_in", dproj, h1, 512, tm=IN_COLS // 2))
    (q_in,) = _exchange("pair_w_in", [gw_in], ["pair"])
    p_in = _pair_add("pair_add_w_in", gw_in, q_in)

    packed, dmod_blocks = _pack_small(acc_in, acc_mix, acc_mlp, dsink_p, daw_p, dlb_p, dgw_p, lb_p, ada_cols)
    r_in, r_dmod, r_small = _exchange("reduce_grads", [p_in, dmod_blocks, packed], ["chips", "a2a", "gather"])

    res = {}
    res["w_in"] = tuple(a.T for a in _reduce_adamw("adamw_w_in", r_in, w_in_t, m_w_in_t, v_w_in_t))
    res["w_out"] = _reduce_adamw("adamw_w_out", r_out, w_out[0], m_w_out[0], v_w_out[0])
    res["w_up"] = _reduce_adamw("adamw_w_up", r_up, w_up[0], m_w_up[0], v_w_up[0])
    res["w_down"] = _reduce_adamw("adamw_w_down", r_down, w_down[0], m_w_down[0], v_w_down[0])
    res["w_ada"] = _ada_grad_adamw(c_all, r_dmod.reshape(N_DEV * B, ada_cols), w_ada[0], m_w_ada[0], v_w_ada[0])

    given = dict(b_ada=(b_ada, m_b_ada, v_b_ada), pre_w_mix=(pre_w_mix, m_pre_w_mix, v_pre_w_mix),
                 attn_sinks=(attn_sinks, m_attn_sinks, v_attn_sinks),
                 attn_out_w=(attn_out_w, m_attn_out_w, v_attn_out_w), lb_table=(lb_table, m_lb_table, v_lb_table),
                 hg_norm_w=(hg_norm_w, m_hg_norm_w, v_hg_norm_w), post_w_mix=(post_w_mix, m_post_w_mix, v_post_w_mix),
                 pre_w_mlp=(pre_w_mlp, m_pre_w_mlp, v_pre_w_mlp), post_w_mlp=(post_w_mlp, m_post_w_mlp, v_post_w_mlp))
    small_res, loss = _adamw_small(r_small, given)
    res.update(small_res)

    order = ["w_ada", "b_ada", "pre_w_mix", "w_in", "attn_sinks", "attn_out_w", "lb_table", "hg_norm_w", "w_out",
             "post_w_mix", "pre_w_mlp", "w_up", "w_down", "post_w_mlp"]
    big = {"w_ada", "w_in", "w_out", "w_up", "w_down"}
    outs = [loss, grad_x.reshape(B, T, D_MODEL)]
    for i in range(4):
        for k in order:
            a = res[k][i]
            outs.append(a[None] if k in big else a)
    return tuple(outs)
```

```python
import jax
import jax.numpy as jnp
import numpy as np
from jax import lax
from jax.experimental import pallas as pl
from jax.experimental.pallas import tpu as pltpu

F32 = jnp.float32
BF16 = jnp.bfloat16
SDS = jax.ShapeDtypeStruct

D_MODEL = 1024
ATT_WIDTH = 512
ATT_HEAD_DIM = 64
ATT_KV_HEADS = 2
ATT_GROUP = 4
WINDOW = 128
ROPE_DIM = 16
ROPE_THETA = 500000.0
HG_WIDTH = 512
HG_HEAD_DIM = 128
HG_HEADS = 4
HG_CHUNK = 32
IN_COLS = 2816
ATT_COLS = 768
D_FF = 4096
EPS = 1e-6
N_DEV = 8

ADAM_LR = 0.001
ADAM_B1 = 0.9
ADAM_B2 = 0.999
ADAM_EPS = 1e-08
ADAM_WD = 0.01
ADAM_STEP = 10

VMEM_LIMIT_BIG = 56 << 20
LANES = 128

MESH = pl.DeviceIdType.MESH
NT_DIMS = (((1,), (1,)), ((), ()))
TN_DIMS = (((0,), (0,)), ((), ()))


def _dot(a, b):
    return jnp.dot(a, b, preferred_element_type=F32)


def _dot_nt(a, b):
    return lax.dot_general(a, b, NT_DIMS, preferred_element_type=F32)


def _dot_tn(a, b):
    return lax.dot_general(a, b, TN_DIMS, preferred_element_type=F32)


def _bf(a):
    return a.astype(BF16)


def _sigmoid(a):
    return 0.5 * jnp.tanh(0.5 * a) + 0.5


def _mean_last(a):
    return jnp.mean(a, axis=-1, keepdims=True)


def _sum_rows(a):
    return jnp.sum(a, axis=0, keepdims=True)


def _loop_pairs(first, count, body, init, per_trip=2):
    if count % per_trip:
        return lax.fori_loop(first, first + count, body, init)

    def trip(i, c):
        for k in range(per_trip):
            c = body(first + per_trip * i + k, c)
        return c

    return lax.fori_loop(0, count // per_trip, trip, init)


def _params(sem=None, vmem=None):
    kw = {}
    if sem is not None:
        kw["dimension_semantics"] = sem
    if vmem is not None:
        kw["vmem_limit_bytes"] = vmem
    return pltpu.CompilerParams(**kw)


ANY_SPEC = pl.BlockSpec(memory_space=pl.ANY)


def _exchange_shapes(srcs, modes):
    out_shape = []
    for s, m in zip(srcs, modes):
        shp = {"gather": (N_DEV,) + tuple(s.shape), "pair": (s.shape[0],) + tuple(s.shape[2:])}.get(m, tuple(s.shape))
        out_shape.append(SDS(shp, s.dtype))
    return out_shape


def _exchange_sems(n):
    if n == 0:
        return []
    return [pltpu.SemaphoreType.DMA((n, N_DEV - 1)), pltpu.SemaphoreType.DMA((n, N_DEV - 1)),
            pltpu.SemaphoreType.DMA((n,))]


SIBLING = 1
OTHER_CHIPS = (2, 4, 6)


def _related(k):
    x, y, c = lax.axis_index("x"), lax.axis_index("y"), lax.axis_index("c")
    px, py, pc = x ^ ((k >> 2) & 1), y ^ ((k >> 1) & 1), c ^ (k & 1)
    return (px, py, pc), 4 * px + 2 * py + pc


def _exchange_phases(modes, src_refs, out_refs, send_sems, recv_sems, own_sems):
    _, me = _related(0)
    sib_dev, sib = _related(SIBLING)
    start, middle, end = [], [], []

    def remote(a, i, src, dst, dev):
        return pltpu.make_async_remote_copy(src_ref=src, dst_ref=dst, send_sem=send_sems.at[a, i],
                                            recv_sem=recv_sems.at[a, i], device_id=dev, device_id_type=MESH)

    for a, mode in enumerate(modes):
        out = out_refs[a]
        if mode == "gather":
            src = src_refs[a]
            own = pltpu.make_async_copy(src, out.at[me], own_sems.at[a])
            to_sib = remote(a, 0, src, out.at[me], sib_dev)
            start += [own.start, to_sib.start]
            end += [remote(a, 0, src, out.at[sib], sib_dev).wait_recv, to_sib.wait_send, own.wait]
            for j, k in enumerate(OTHER_CHIPS, start=1):
                dev, peer = _related(k)
                _, peer_sib = _related(k ^ SIBLING)
                send = remote(a, j, src, out.at[me], dev)
                passed = remote(a, 3 + j, out.at[peer], out.at[peer], sib_dev)
                start.append(send.start)
                middle += [remote(a, j, src, out.at[peer], dev).wait_recv, passed.start]
                end += [remote(a, 3 + j, out.at[peer_sib], out.at[peer_sib], sib_dev).wait_recv,
                        send.wait_send, passed.wait_send]
        elif mode == "pair":
            core = lax.axis_index("c")
            for s in range(N_DEV // 2):
                send = remote(a, s, src_refs[a].at[s, 1 - core], out.at[s], sib_dev)
                start.append(send.start)
                end += [remote(a, s, src_refs[a].at[s, 1 - core], out.at[s], sib_dev).wait_recv, send.wait_send]
        elif mode == "chips":
            chip = me // 2
            own = pltpu.make_async_copy(src_refs[a].at[chip], out.at[chip], own_sems.at[a])
            start.append(own.start)
            end.append(own.wait)
            for j, k in enumerate(OTHER_CHIPS, start=1):
                dev, peer = _related(k)
                send = remote(a, j, src_refs[a].at[peer // 2], out.at[chip], dev)
                start.append(send.start)
                end += [remote(a, j, src_refs[a].at[peer // 2], out.at[peer // 2], dev).wait_recv, send.wait_send]
        else:
            own = pltpu.make_async_copy(src_refs[a].at[me], out.at[me], own_sems.at[a])
            start.append(own.start)
            end.append(own.wait)
            for k in range(1, N_DEV):
                dev, peer = _related(k)
                send = remote(a, k - 1, src_refs[a].at[peer], out.at[me], dev)
                start.append(send.start)
                end += [remote(a, k - 1, src_refs[a].at[peer], out.at[peer], dev).wait_recv, send.wait_send]
    return start, middle, end


def _run(actions):
    for act in actions:
        act()


def _exchange(name, srcs, modes):
    n = len(srcs)

    def body(*refs):
        start, middle, end = _exchange_phases(modes, refs[:n], refs[n:2 * n], *refs[2 * n:])
        _run(start)
        _run(middle)
        _run(end)

    return pl.pallas_call(
        body, name=name, out_shape=_exchange_shapes(srcs, modes),
        in_specs=[ANY_SPEC] * n, out_specs=[ANY_SPEC] * n,
        scratch_shapes=_exchange_sems(n),
    )(*srcs)


def _ride_start(modes, step, steps, src_refs, out_refs, sems):
    if not modes:
        return
    middle_step = steps - 1

    @pl.when(step == 0)
    def _():
        _run(_exchange_phases(modes, src_refs, out_refs, *sems)[0])

    if "gather" in modes:
        @pl.when(step == middle_step)
        def _():
            _run(_exchange_phases(modes, src_refs, out_refs, *sems)[1])


def _ride_wait(modes, step, steps, src_refs, out_refs, sems):
    if not modes:
        return

    @pl.when(step == steps - 1)
    def _():
        _run(_exchange_phases(modes, src_refs, out_refs, *sems)[2])


def _ada_mod(c_all, w_ada, b_ada_mine):
    nb, cols = c_all.shape[0], w_ada.shape[1]

    def body(c_ref, w_ref, b_ref, o_ref):
        cv = c_ref[...]
        ca = cv * _sigmoid(cv)
        o_ref[...] = _dot(ca, w_ref[...]) + b_ref[...]

    return pl.pallas_call(body, name="ada_mod", out_shape=SDS((nb, cols), F32))(c_all, w_ada, b_ada_mine)


def _tile_rows(T, big=False):
    return min(512 if big else 256, T)


def _mod_spec(tps):
    return pl.BlockSpec((None, 8, D_MODEL), lambda i: (i // tps, 0, 0))


def _in_proj(x2, mod8, pre_w, w_in_bf, T, ride_srcs, ride_modes):
    N = x2.shape[0]
    TM = _tile_rows(T, big=True)
    tps = T // TM
    nr = len(ride_srcs)

    def body(*refs):
        x_ref, mod_ref, pw_ref, w_ref = refs[:4]
        ride_in = refs[4:4 + nr]
        pa_ref, ph_ref, h1_ref = refs[4 + nr:7 + nr]
        ride_out = refs[7 + nr:7 + 2 * nr]
        sems = refs[7 + 2 * nr:]
        _ride_start(ride_modes, pl.program_id(0), N // TM, ride_in, ride_out, sems)
        x = x_ref[...]
        r = lax.rsqrt(_mean_last(x * x) + EPS)
        h = (x * r * pw_ref[...]) * (1.0 + mod_ref[1:2, :]) + mod_ref[0:1, :]
        hb = _bf(h)
        h1_ref[...] = hb
        pa_ref[...] = _dot_nt(hb, w_ref[:ATT_COLS, :])
        ph_ref[...] = _dot_nt(hb, w_ref[ATT_COLS:, :])
        _ride_wait(ride_modes, pl.program_id(0), N // TM, ride_in, ride_out, sems)

    return pl.pallas_call(
        body, name="in_proj", grid=(N // TM,),
        in_specs=[pl.BlockSpec((TM, D_MODEL), lambda i: (i, 0)), _mod_spec(tps),
                  pl.BlockSpec((1, D_MODEL), lambda i: (0, 0)),
                  pl.BlockSpec((IN_COLS, D_MODEL), lambda i: (0, 0))] + [ANY_SPEC] * nr,
        out_specs=[pl.BlockSpec((TM, ATT_COLS), lambda i: (i, 0)),
                   pl.BlockSpec((TM, IN_COLS - ATT_COLS), lambda i: (i, 0)),
                   pl.BlockSpec((TM, D_MODEL), lambda i: (i, 0))] + [ANY_SPEC] * nr,
        out_shape=[SDS((N, ATT_COLS), F32), SDS((N, IN_COLS - ATT_COLS), F32), SDS((N, D_MODEL), BF16)]
        + _exchange_shapes(ride_srcs, ride_modes),
        scratch_shapes=_exchange_sems(nr),
        compiler_params=_params(("arbitrary",), VMEM_LIMIT_BIG),
    )(x2, mod8, pre_w, w_in_bf, *ride_srcs)


def _rope_tables(T):
    half = ROPE_DIM // 2
    f32 = np.float32
    inv_freq = (f32(ROPE_THETA) ** (-np.arange(0, ROPE_DIM, 2, dtype=f32) / f32(ROPE_DIM))).astype(f32)
    ang = np.arange(T, dtype=f32)[:, None] * inv_freq[None, :]
    cos, sin = np.cos(ang).astype(f32), np.sin(ang).astype(f32)
    ones = np.ones((T, ATT_HEAD_DIM - ROPE_DIM), f32)
    zeros = np.zeros((T, ATT_HEAD_DIM - ROPE_DIM), f32)
    zh = np.zeros((T, half), f32)
    cos64 = np.concatenate([cos, cos, ones], axis=1)
    sin_left = np.concatenate([-sin, zh, zeros], axis=1)
    sin_right = np.concatenate([zh, sin, zeros], axis=1)
    rep = LANES // ATT_HEAD_DIM
    return tuple(jnp.asarray(np.tile(t, (1, rep))) for t in (cos64, sin_left, sin_right))


def _rope(xc, cs, sl, sr):
    return xc * cs + pltpu.roll(xc, LANES - 8, 1) * sl + pltpu.roll(xc, 8, 1) * sr


def _rope_t(dy, cs, sl, sr):
    return dy * cs + pltpu.roll(dy * sl, 8, 1) + pltpu.roll(dy * sr, LANES - 8, 1)


ATT_SCALE = ATT_HEAD_DIM ** -0.5
ATT_SPLITS = 4


def _lower_mask():
    j = lax.broadcasted_iota(jnp.int32, (WINDOW, ATT_GROUP * WINDOW), 0)
    i = lax.broadcasted_iota(jnp.int32, (WINDOW, ATT_GROUP * WINDOW), 1) & (WINDOW - 1)
    return j <= i


def _sink_row(sink_ref, hk):
    return jnp.concatenate(
        [jnp.full((1, WINDOW), sink_ref[0, ATT_GROUP * hk + g], F32) for g in range(ATT_GROUP)], axis=1)


def _softmax_window(qs, k_cur, k_prev, lower, has_prev, sink):
    s_prev = jnp.where(has_prev, _dot_nt(k_prev, qs), jnp.finfo(F32).min)
    s = jnp.where(lower, _dot_nt(k_cur, qs), s_prev)
    m = jnp.maximum(jnp.max(s, axis=0, keepdims=True), sink)
    p = jnp.exp(s - m)
    es = jnp.exp(sink - m)
    inv = 1.0 / (jnp.sum(p, axis=0, keepdims=True) + es)
    return p, inv, es


def _stack_heads(parts, hk):
    hs = []
    for g in range(ATT_GROUP):
        h = ATT_GROUP * hk + g
        hs.append(parts[h // 2][:, (h % 2) * ATT_HEAD_DIM:(h % 2 + 1) * ATT_HEAD_DIM])
    return jnp.concatenate(hs, axis=0)


def _attn_fwd(proj3, tables, sinks, attn_w, ride_srcs, ride_modes):
    B, T, _ = proj3.shape
    nb = T // WINDOW
    splits = min(ATT_SPLITS, nb)
    per = nb // splits
    nr = len(ride_srcs)
    cos, sinl, sinr = tables

    def body(*refs):
        q_ref, k_ref, v_ref, cos_ref, sl_ref, sr_ref, sink_ref, aw_ref = refs[:8]
        ride_in = refs[8:8 + nr]
        o_ref, an_ref, qr_ref, kr_ref = refs[8 + nr:12 + nr]
        ride_out = refs[12 + nr:12 + 2 * nr]
        kpad, vpad = refs[12 + 2 * nr:14 + 2 * nr]
        sems = refs[14 + 2 * nr:]
        part = pl.program_id(1)
        step = pl.program_id(0) * splits + part
        _ride_start(ride_modes, step, B * splits, ride_in, ride_out, sems)

        @pl.when(part == 0)
        def _():
            kpad[0:WINDOW, :] = jnp.zeros((WINDOW, LANES), BF16)
            vpad[0:WINDOW, :] = jnp.zeros((WINDOW, LANES), BF16)

        lower = _lower_mask()

        def block(n, carry):
            r0 = pl.multiple_of(n * WINDOW, WINDOW)
            rows = pl.ds(r0, WINDOW)
            nxt = pl.ds(r0 + WINDOW, WINDOW)
            cs, sl, sr = cos_ref[rows, :], sl_ref[rows, :], sr_ref[rows, :]
            kb = _bf(_rope(k_ref[rows, :], cs, sl, sr))
            vb = _bf(v_ref[rows, :])
            kpad[nxt, :] = kb
            kr_ref[rows, :] = kb
            vpad[nxt, :] = vb
            qparts = []
            for j in range(ATT_WIDTH // LANES):
                qp = _bf(_rope(q_ref[rows, j * LANES:(j + 1) * LANES], cs, sl, sr) * ATT_SCALE)
                qr_ref[rows, j * LANES:(j + 1) * LANES] = qp
                qparts.append(qp)
            for hk in range(ATT_KV_HEADS):
                lanes = slice(hk * ATT_HEAD_DIM, (hk + 1) * ATT_HEAD_DIM)
                qs = _stack_heads(qparts, hk)
                p, inv, _ = _softmax_window(qs, kb[:, lanes], kpad[rows, lanes], lower, n > 0,
                                            _sink_row(sink_ref, hk))
                p_cur = jnp.where(lower, p, 0.0)
                ot = (_dot_tn(vb[:, lanes], _bf(p_cur)) + _dot_tn(vpad[rows, lanes], _bf(p - p_cur))) * inv
                for g in range(ATT_GROUP):
                    h = ATT_GROUP * hk + g
                    o_ref[rows, h * ATT_HEAD_DIM:(h + 1) * ATT_HEAD_DIM] = ot[:, g * WINDOW:(g + 1) * WINDOW].T
            ob = o_ref[rows, :]
            an_ref[rows, :] = _bf(ob * lax.rsqrt(_mean_last(ob * ob) + EPS) * aw_ref[...])
            return carry

        _loop_pairs(part * per, per, block, 0)
        _ride_wait(ride_modes, step, B * splits, ride_in, ride_out, sems)

    seq = lambda w, j: pl.BlockSpec((None, T, w), lambda b, s: (b, 0, j))
    full = lambda r, w: pl.BlockSpec((r, w), lambda b, s: (0, 0))
    return pl.pallas_call(
        body, name="attn_fwd", grid=(B, splits),
        in_specs=[seq(ATT_WIDTH, 0), seq(LANES, 4), seq(LANES, 5),
                  full(T, LANES), full(T, LANES), full(T, LANES),
                  pl.BlockSpec(memory_space=pltpu.SMEM), full(1, ATT_WIDTH)] + [ANY_SPEC] * nr,
        out_specs=[seq(ATT_WIDTH, 0), seq(ATT_WIDTH, 0), seq(ATT_WIDTH, 0), seq(LANES, 0)] + [ANY_SPEC] * nr,
        out_shape=[SDS((B, T, ATT_WIDTH), F32), SDS((B, T, ATT_WIDTH), BF16),
                   SDS((B, T, ATT_WIDTH), BF16), SDS((B, T, LANES), BF16)] + _exchange_shapes(ride_srcs, ride_modes),
        scratch_shapes=[pltpu.VMEM((T + WINDOW, LANES), BF16), pltpu.VMEM((T + WINDOW, LANES), BF16)]
        + _exchange_sems(nr),
        compiler_params=_params(("arbitrary", "arbitrary"), VMEM_LIMIT_BIG),
    )(proj3, proj3, proj3, cos, sinl, sinr, sinks, attn_w, *ride_srcs)


HG_GROUP = 8
HG_ROWS = HG_GROUP * HG_CHUNK


HG_STACK = HG_GROUP * HG_HEAD_DIM


def _group_mask():
    r = lax.broadcasted_iota(jnp.int32, (HG_ROWS, HG_ROWS), 0)
    c = lax.broadcasted_iota(jnp.int32, (HG_ROWS, HG_ROWS), 1)
    return ((r // HG_CHUNK) == (c // HG_CHUNK)) & (r >= c)


def _spread(a):
    blocks = []
    for c in range(HG_GROUP):
        above = jnp.zeros((c * HG_CHUNK, HG_HEAD_DIM), a.dtype)
        below = jnp.zeros(((HG_GROUP - 1 - c) * HG_CHUNK, HG_HEAD_DIM), a.dtype)
        blocks.append(jnp.concatenate([p for p in (above, a[_chunk_rows(c), :], below) if p.shape[0]], axis=0))
    return jnp.concatenate(blocks, axis=1)


def _pick(r):
    return jnp.concatenate([r[_chunk_rows(c), c * HG_HEAD_DIM:(c + 1) * HG_HEAD_DIM] for c in range(HG_GROUP)], axis=0)


def _lane_block(a, c):
    return a[:, c * HG_HEAD_DIM:(c + 1) * HG_HEAD_DIM]


def _chunk_cumsum(a, reverse=False):
    n = a.shape[0]
    pos = lax.broadcasted_iota(jnp.int32, a.shape, 0) % HG_CHUNK
    shift = 1
    while shift < HG_CHUNK:
        if reverse:
            a = a + jnp.where(pos < HG_CHUNK - shift, pltpu.roll(a, n - shift, 0), 0.0)
        else:
            a = a + jnp.where(pos >= shift, pltpu.roll(a, shift, 0), 0.0)
        shift *= 2
    return a


def _chunk_bcast(rows_1x128):
    return jnp.concatenate([jnp.broadcast_to(r, (HG_CHUNK, HG_HEAD_DIM)) for r in rows_1x128], axis=0)


def _hgrn_gates(hq, hf, lb):
    sq = _sigmoid(hq)
    q = hq * sq
    sg = _sigmoid(hf)
    f = lb + (1.0 - lb) * sg
    k = 1.0 - f
    logf = jnp.log(f)
    b = _chunk_cumsum(logf)
    bl = [_sum_rows(logf[_chunk_rows(c), :]) for c in range(HG_GROUP)]
    eb, enb, e2 = jnp.exp(b), jnp.exp(-b), jnp.exp(_chunk_bcast(bl) - b)
    ebl = [jnp.exp(r) for r in bl]
    return dict(sq=sq, sg=sg, f=f, eb=eb, enb=enb, e2=e2, ebl=ebl, qd=q * eb, kd=k * enb, k2=k * e2)


def _chunk_rows(c):
    return slice(c * HG_CHUNK, (c + 1) * HG_CHUNK)


def _head_lanes(h):
    return slice(h * HG_HEAD_DIM, (h + 1) * HG_HEAD_DIM)


def _hgrn_fwd(proj_h, lb, hg_w, ride_srcs, ride_modes):
    B, T, _ = proj_h.shape
    ng = T // HG_ROWS
    nr = len(ride_srcs)

    def body(*refs):
        hq_ref, hf_ref, hi_ref, hg_ref, lb_ref, gw_ref = refs[:6]
        ride_in = refs[6:6 + nr]
        o_ref, rg_ref, sp_ref = refs[6 + nr:9 + nr]
        ride_out = refs[9 + nr:9 + 2 * nr]
        st = refs[9 + 2 * nr]
        sems = refs[10 + 2 * nr:]
        gi = pl.program_id(1)
        step = pl.program_id(0) * ng + gi
        _ride_start(ride_modes, step, B * ng, ride_in, ride_out, sems)

        @pl.when(gi == 0)
        def _():
            st[...] = jnp.zeros(st.shape, F32)

        lo = _group_mask()
        for h in range(HG_HEADS):
            lanes = _head_lanes(h)
            gt = _hgrn_gates(hq_ref[:, lanes], hf_ref[:, lanes], lb_ref[:, lanes])
            v, qd, kd = _bf(hi_ref[:, lanes]), _bf(gt["qd"]), _bf(gt["kd"])
            a = jnp.where(lo, _dot_nt(qd, kd), 0.0)
            kv = _dot_tn(v, _spread(_bf(gt["k2"])))
            s = st[h]
            before = []
            for c in range(HG_GROUP):
                before.append(s)
                s = s * gt["ebl"][c] + _lane_block(kv, c)
            st[h] = s
            sp = jnp.concatenate(before, axis=1)
            sp_ref[h] = sp
            o = _dot(_bf(a), v) + _dot_nt(_spread(qd), _bf(sp))
            o_ref[:, lanes] = o
            hg = hg_ref[:, lanes]
            rn = o * lax.rsqrt(_mean_last(o * o) + EPS) * gw_ref[...]
            rg_ref[:, lanes] = _bf(rn * (hg * _sigmoid(hg)))
        _ride_wait(ride_modes, step, B * ng, ride_in, ride_out, sems)

    part = lambda j: pl.BlockSpec((None, HG_ROWS, HG_WIDTH), lambda b, g: (b, g, j))
    return pl.pallas_call(
        body, name="hgrn_fwd", grid=(B, ng),
        in_specs=[part(0), part(1), part(2), part(3),
                  pl.BlockSpec((1, HG_WIDTH), lambda b, g: (0, 0)),
                  pl.BlockSpec((1, LANES), lambda b, g: (0, 0))] + [ANY_SPEC] * nr,
        out_specs=[part(0), part(0),
                   pl.BlockSpec((None, HG_HEADS, None, HG_HEAD_DIM, HG_STACK), lambda b, g: (b, 0, g, 0, 0))]
        + [ANY_SPEC] * nr,
        out_shape=[SDS((B, T, HG_WIDTH), F32), SDS((B, T, HG_WIDTH), BF16),
                   SDS((B, HG_HEADS, ng, HG_HEAD_DIM, HG_STACK), F32)] + _exchange_shapes(ride_srcs, ride_modes),
        scratch_shapes=[pltpu.VMEM((HG_HEADS, HG_HEAD_DIM, HG_HEAD_DIM), F32)] + _exchange_sems(nr),
        compiler_params=_params(("arbitrary", "arbitrary"), VMEM_LIMIT_BIG),
    )(proj_h, proj_h, proj_h, proj_h, lb, hg_w, *ride_srcs)


def _mix_out(x2, attn_n, rec_g, mod8, post_w, w_out_bf, T, ride_srcs, ride_modes):
    N = x2.shape[0]
    TM = _tile_rows(T, big=True)
    tps = T // TM
    nr = len(ride_srcs)

    def body(*refs):
        x_ref, an_ref, rg_ref, mod_ref, pw_ref, w_ref = refs[:6]
        ride_in = refs[6:6 + nr]
        mix_ref, x1_ref, cat_ref = refs[6 + nr:9 + nr]
        ride_out = refs[9 + nr:9 + 2 * nr]
        sems = refs[9 + 2 * nr:]
        _ride_start(ride_modes, pl.program_id(0), N // TM, ride_in, ride_out, sems)
        cat = jnp.concatenate([an_ref[...], rg_ref[...]], axis=1)
        cat_ref[...] = cat
        mix = _dot(cat, w_ref[...])
        mix_ref[...] = mix
        r = lax.rsqrt(_mean_last(mix * mix) + EPS)
        x1_ref[...] = x_ref[...] + mod_ref[2:3, :] * (mix * r * pw_ref[...])
        _ride_wait(ride_modes, pl.program_id(0), N // TM, ride_in, ride_out, sems)

    row = lambda w: pl.BlockSpec((TM, w), lambda i: (i, 0))
    return pl.pallas_call(
        body, name="mix_out", grid=(N // TM,),
        in_specs=[row(D_MODEL), row(ATT_WIDTH), row(HG_WIDTH), _mod_spec(tps),
                  pl.BlockSpec((1, D_MODEL), lambda i: (0, 0)),
                  pl.BlockSpec((D_MODEL, D_MODEL), lambda i: (0, 0))] + [ANY_SPEC] * nr,
        out_specs=[row(D_MODEL), row(D_MODEL), row(D_MODEL)] + [ANY_SPEC] * nr,
        out_shape=[SDS((N, D_MODEL), F32), SDS((N, D_MODEL), F32), SDS((N, D_MODEL), BF16)]
        + _exchange_shapes(ride_srcs, ride_modes),
        scratch_shapes=_exchange_sems(nr),
        compiler_params=_params(("arbitrary",), VMEM_LIMIT_BIG),
    )(x2, attn_n, rec_g, mod8, post_w, w_out_bf, *ride_srcs)


def _load_weights_once(pairs, sem):
    @pl.when(pl.program_id(0) == 0)
    def _():
        cps = [pltpu.make_async_copy(src, dst, sem.at[i]) for i, (src, dst) in enumerate(pairs)]
        for cp in cps:
            cp.start()
        for cp in cps:
            cp.wait()


MLP_HALF = D_MODEL // 2
MLP_PIECES = 2 * N_DEV + 2


def _mlp_weight_pieces(wu_a, wu_b, wd_a, wd_b, wu, wd):
    cols = D_FF // N_DEV
    pairs = []
    for h, half in enumerate((wu_a, wu_b)):
        for j in range(N_DEV):
            pairs.append((half.at[j], wu.at[pl.ds(h * MLP_HALF, MLP_HALF), pl.ds(j * cols, cols)]))
    for h, half in enumerate((wd_a, wd_b)):
        pairs.append((half, wd.at[:, pl.ds(h * MLP_HALF, MLP_HALF)]))
    return pairs


def _mlp_fwd(x1, mod8, pre_w, w_up_halves, w_down_halves, T):
    N = x1.shape[0]
    TM = _tile_rows(T)
    tps = T // TM

    def body(x_ref, mod_ref, pw_ref, wua, wub, wda, wdb, up_ref, u_ref, d_ref, h2_ref, wu, wd, sem):
        _load_weights_once(_mlp_weight_pieces(wua, wub, wda, wdb, wu, wd), sem)
        x = x_ref[...]
        r = lax.rsqrt(_mean_last(x * x) + EPS)
        h = (x * r * pw_ref[...]) * (1.0 + mod_ref[4:5, :]) + mod_ref[3:4, :]
        hb = _bf(h)
        h2_ref[...] = hb
        up = _dot(hb, wu[...])
        up_ref[...] = up
        ru = jnp.maximum(up, 0.0)
        u = _bf(ru * ru)
        u_ref[...] = u
        d_ref[...] = _dot(u, wd[...])

    row = lambda w: pl.BlockSpec((TM, w), lambda i: (i, 0))
    return pl.pallas_call(
        body, name="mlp_fwd", grid=(N // TM,),
        in_specs=[row(D_MODEL), _mod_spec(tps), pl.BlockSpec((1, D_MODEL), lambda i: (0, 0))] + [ANY_SPEC] * 4,
        out_specs=[row(D_FF), row(D_FF), row(D_MODEL), row(D_MODEL)],
        out_shape=[SDS((N, D_FF), F32), SDS((N, D_FF), BF16), SDS((N, D_MODEL), F32), SDS((N, D_MODEL), BF16)],
        scratch_shapes=[pltpu.VMEM((D_MODEL, D_FF), BF16), pltpu.VMEM((D_FF, D_MODEL), BF16),
                        pltpu.SemaphoreType.DMA((MLP_PIECES,))],
        compiler_params=_params(("arbitrary",), VMEM_LIMIT_BIG),
    )(x1, mod8, pre_w, *w_up_halves, *w_down_halves)


def _acc_rows(acc_ref, first, rows):
    @pl.when(first)
    def _():
        acc_ref[...] = jnp.zeros(acc_ref.shape, F32)
    for i, r in enumerate(rows):
        acc_ref[i:i + 1, :] += r


def _mlp_bwd(x1, d, up, tgt, mod8, pre_w, post_w, w_up_halves, w_down_halves, T):
    N = x1.shape[0]
    TM = _tile_rows(T)
    tps = T // TM

    def body(x_ref, d_ref, up_ref, t_ref, mod_ref, pw_ref, qw_ref, wua, wub, wda, wdb,
             dx_ref, dup_ref, dd_ref, acc_ref, wd, wu, sem):
        _load_weights_once(_mlp_weight_pieces(wua, wub, wda, wdb, wu, wd), sem)
        sh2, sc2, g2 = mod_ref[3:4, :], mod_ref[4:5, :], mod_ref[5:6, :]
        x = x_ref[...]
        r1 = lax.rsqrt(_mean_last(x * x) + EPS)
        xh = x * r1
        n2 = xh * pw_ref[...]
        dv = d_ref[...]
        rd = lax.rsqrt(_mean_last(dv * dv) + EPS)
        dh = dv * rd
        rr = dh * qw_ref[...]
        e = x + g2 * rr - t_ref[...]
        loss = 0.5 * jnp.sum(_sum_rows(e * e), axis=1, keepdims=True) / D_MODEL
        dy = e * (1.0 / D_MODEL)
        dg2 = _sum_rows(dy * rr)
        drr = dy * g2
        dw_post = _sum_rows(drr * dh)
        ddh = drr * qw_ref[...]
        dd = _bf(rd * (ddh - dh * _mean_last(ddh * dh)))
        dd_ref[...] = dd
        ru = jnp.maximum(up_ref[...], 0.0)
        dup = _bf(_dot_nt(dd, wd[...]) * (2.0 * ru))
        dup_ref[...] = dup
        dh2 = _dot_nt(dup, wu[...])
        dsh2 = _sum_rows(dh2)
        dsc2 = _sum_rows(dh2 * n2)
        dn2 = dh2 * (1.0 + sc2)
        dw_pre = _sum_rows(dn2 * xh)
        dxh = dn2 * pw_ref[...]
        dx_ref[...] = dy + r1 * (dxh - xh * _mean_last(dxh * xh))
        _acc_rows(acc_ref, pl.program_id(0) % tps == 0,
                  [dsh2, dsc2, dg2, dw_pre, dw_post, jnp.broadcast_to(loss, (1, D_MODEL))])

    row = lambda w: pl.BlockSpec((TM, w), lambda i: (i, 0))
    vec = pl.BlockSpec((1, D_MODEL), lambda i: (0, 0))
    B = N // T
    return pl.pallas_call(
        body, name="mlp_bwd", grid=(N // TM,),
        in_specs=[row(D_MODEL), row(D_MODEL), row(D_FF), row(D_MODEL), _mod_spec(tps), vec, vec] + [ANY_SPEC] * 4,
        out_specs=[row(D_MODEL), row(D_FF), row(D_MODEL), _mod_spec(tps)],
        out_shape=[SDS((N, D_MODEL), F32), SDS((N, D_FF), BF16), SDS((N, D_MODEL), BF16),
                   SDS((B, 8, D_MODEL), F32)],
        scratch_shapes=[pltpu.VMEM((D_FF, D_MODEL), BF16), pltpu.VMEM((D_MODEL, D_FF), BF16),
                        pltpu.SemaphoreType.DMA((MLP_PIECES,))],
        compiler_params=_params(("arbitrary",), VMEM_LIMIT_BIG),
    )(x1, d, up, tgt, mod8, pre_w, post_w, *w_up_halves, *w_down_halves)


def _mix_bwd(mix, dx1, mod8, post_w, w_out_bf, T, ride_srcs, ride_modes):
    N = mix.shape[0]
    TM = _tile_rows(T, big=True)
    tps = T // TM
    nr = len(ride_srcs)

    def body(*refs):
        mix_ref, dx_ref, mod_ref, pw_ref, w_ref = refs[:5]
        ride_in = refs[5:5 + nr]
        dan_ref, drg_ref, dmix_ref, acc_ref = refs[5 + nr:9 + nr]
        ride_out = refs[9 + nr:9 + 2 * nr]
        sems = refs[9 + 2 * nr:]
        _ride_start(ride_modes, pl.program_id(0), N // TM, ride_in, ride_out, sems)
        g1 = mod_ref[2:3, :]
        mix = mix_ref[...]
        dx1 = dx_ref[...]
        rm = lax.rsqrt(_mean_last(mix * mix) + EPS)
        mh = mix * rm
        dg1 = _sum_rows(dx1 * (mh * pw_ref[...]))
        dr = dx1 * g1
        dw_post = _sum_rows(dr * mh)
        dmh = dr * pw_ref[...]
        dmix = _bf(rm * (dmh - mh * _mean_last(dmh * mh)))
        dmix_ref[...] = dmix
        dcat = _dot_nt(dmix, w_ref[...])
        dan_ref[...] = dcat[:, :ATT_WIDTH]
        drg_ref[...] = dcat[:, ATT_WIDTH:]
        _acc_rows(acc_ref, pl.program_id(0) % tps == 0, [dg1, dw_post])
        _ride_wait(ride_modes, pl.program_id(0), N // TM, ride_in, ride_out, sems)

    row = lambda w: pl.BlockSpec((TM, w), lambda i: (i, 0))
    B = N // T
    return pl.pallas_call(
        body, name="mix_bwd", grid=(N // TM,),
        in_specs=[row(D_MODEL), row(D_MODEL), _mod_spec(tps), pl.BlockSpec((1, D_MODEL), lambda i: (0, 0)),
                  pl.BlockSpec((D_MODEL, D_MODEL), lambda i: (0, 0))] + [ANY_SPEC] * nr,
        out_specs=[row(ATT_WIDTH), row(HG_WIDTH), row(D_MODEL), _mod_spec(tps)] + [ANY_SPEC] * nr,
        out_shape=[SDS((N, ATT_WIDTH), F32), SDS((N, HG_WIDTH), F32), SDS((N, D_MODEL), BF16),
                   SDS((B, 8, D_MODEL), F32)] + _exchange_shapes(ride_srcs, ride_modes),
        scratch_shapes=_exchange_sems(nr),
        compiler_params=_params(("arbitrary",), VMEM_LIMIT_BIG),
    )(mix, dx1, mod8, post_w, w_out_bf, *ride_srcs)


def _hgrn_bwd(proj_h, lb, hg_w, o, s_prev, drg, ride_srcs, ride_modes):
    B, T, _ = proj_h.shape
    ng = T // HG_ROWS
    nr = len(ride_srcs)

    def body(*refs):
        hq_ref, hf_ref, hi_ref, hg_ref, lb_ref, gw_ref, o_ref, sp_ref, drg_ref = refs[:9]
        ride_in = refs[9:9 + nr]
        dhq_ref, dhf_ref, dhi_ref, dhg_ref, dlb_ref, dgw_ref = refs[9 + nr:15 + nr]
        ride_out = refs[15 + nr:15 + 2 * nr]
        dst = refs[15 + 2 * nr]
        sems = refs[16 + 2 * nr:]
        step = pl.program_id(0) * ng + pl.program_id(1)
        _ride_start(ride_modes, step, B * ng, ride_in, ride_out, sems)

        @pl.when(pl.program_id(1) == 0)
        def _():
            dst[...] = jnp.zeros(dst.shape, F32)
            dlb_ref[...] = jnp.zeros(dlb_ref.shape, F32)
            dgw_ref[...] = jnp.zeros(dgw_ref.shape, F32)

        lo = _group_mask()
        gw = gw_ref[...]

        for h in range(HG_HEADS):
            lanes = _head_lanes(h)
            lbv = lb_ref[:, lanes]
            hq = hq_ref[:, lanes]
            gt = _hgrn_gates(hq, hf_ref[:, lanes], lbv)
            sq, sg, qdf, kdf, k2f, ebl = gt["sq"], gt["sg"], gt["qd"], gt["kd"], gt["k2"], gt["ebl"]
            v, qd, kd = _bf(hi_ref[:, lanes]), _bf(qdf), _bf(kdf)
            ov = o_ref[:, lanes]
            hg = hg_ref[:, lanes]
            shg = _sigmoid(hg)
            dr = drg_ref[:, lanes]
            ro = lax.rsqrt(_mean_last(ov * ov) + EPS)
            oh = ov * ro
            dhg_ref[:, lanes] = _bf(dr * (oh * gw) * (shg + hg * shg * (1.0 - shg)))
            drn = dr * (hg * shg)
            dgw_ref[...] += jnp.broadcast_to(_sum_rows(drn * oh), (8, LANES))
            doh = drn * gw
            do = _bf(ro * (doh - oh * _mean_last(doh * oh)))
            a = jnp.where(lo, _dot_nt(qd, kd), 0.0)
            da = _bf(jnp.where(lo, _dot_nt(do, v), 0.0))
            dv = _dot_tn(_bf(a), do)
            dqd = _dot(da, kd)
            dkd = _dot_tn(da, qd)
            sp = sp_ref[h]
            incr = _dot_tn(do, _spread(qd))
            ds = dst[h]
            after = [None] * HG_GROUP
            for c in reversed(range(HG_GROUP)):
                after[c] = ds
                ds = ds * ebl[c] + _lane_block(incr, c)
            dst[h] = ds
            dss = jnp.concatenate(after, axis=1)
            dssb = _bf(dss)
            dk2 = _pick(_dot(v, dssb))
            dhi_ref[:, lanes] = _bf(dv + _dot_nt(_spread(_bf(k2f)), dssb))
            dqd = dqd + _pick(_dot(do, _bf(sp)))
            debl = _sum_rows(dss * sp)
            k2g = dk2 * k2f
            db = dqd * qdf - dkd * kdf - k2g
            dk = dkd * gt["enb"] + dk2 * gt["e2"]
            dbl = _chunk_bcast([_lane_block(debl, c) * ebl[c] + _sum_rows(k2g[_chunk_rows(c), :])
                                for c in range(HG_GROUP)])
            dg = _chunk_cumsum(db, reverse=True) + dbl
            df = dg / gt["f"] - dk
            dhf_ref[:, lanes] = _bf(df * (1.0 - lbv) * sg * (1.0 - sg))
            dlb_ref[:, lanes] += jnp.broadcast_to(_sum_rows(df * (1.0 - sg)), (8, LANES))
            dhq_ref[:, lanes] = _bf((dqd * gt["eb"]) * (sq + hq * sq * (1.0 - sq)))
        _ride_wait(ride_modes, step, B * ng, ride_in, ride_out, sems)

    part = lambda j: pl.BlockSpec((None, HG_ROWS, HG_WIDTH), lambda b, g: (b, ng - 1 - g, j))
    return pl.pallas_call(
        body, name="hgrn_bwd", grid=(B, ng),
        in_specs=[part(0), part(1), part(2), part(3),
                  pl.BlockSpec((1, HG_WIDTH), lambda b, g: (0, 0)),
                  pl.BlockSpec((1, LANES), lambda b, g: (0, 0)),
                  part(0),
                  pl.BlockSpec((None, HG_HEADS, None, HG_HEAD_DIM, HG_STACK), lambda b, g: (b, 0, ng - 1 - g, 0, 0)),
                  part(0)] + [ANY_SPEC] * nr,
        out_specs=[part(0), part(0), part(0), part(0),
                   pl.BlockSpec((None, 8, HG_WIDTH), lambda b, g: (b, 0, 0)),
                   pl.BlockSpec((None, 8, LANES), lambda b, g: (b, 0, 0))] + [ANY_SPEC] * nr,
        out_shape=[SDS((B, T, HG_WIDTH), BF16)] * 4 + [SDS((B, 8, HG_WIDTH), F32), SDS((B, 8, LANES), F32)]
        + _exchange_shapes(ride_srcs, ride_modes),
        scratch_shapes=[pltpu.VMEM((HG_HEADS, HG_HEAD_DIM, HG_HEAD_DIM), F32)] + _exchange_sems(nr),
        compiler_params=_params(("arbitrary", "arbitrary"), VMEM_LIMIT_BIG),
    )(proj_h, proj_h, proj_h, proj_h, lb, hg_w, o, s_prev, drg, *ride_srcs)


def _attn_bwd(qr, kr, proj3, attn_o, dan, tables, sinks, attn_w, ride_srcs, ride_modes):
    B, T, _ = proj3.shape
    nb = T // WINDOW
    splits = min(ATT_SPLITS, nb)
    per = nb // splits
    nr = len(ride_srcs)
    cos, sinl, sinr = tables
    QKV = ATT_WIDTH + 2 * LANES

    def body(*refs):
        qr_ref, kr_ref, v_ref, o_ref, dan_ref, cos_ref, sl_ref, sr_ref, sink_ref, aw_ref = refs[:10]
        ride_in = refs[10:10 + nr]
        dqkv_ref, dsink_ref, daw_ref = refs[10 + nr:13 + nr]
        ride_out = refs[13 + nr:13 + 2 * nr]
        kpad, vpad, dkpad, dvpad, dqb, dsk = refs[13 + 2 * nr:19 + 2 * nr]
        sems = refs[19 + 2 * nr:]
        part = pl.program_id(1)
        step = pl.program_id(0) * splits + part
        _ride_start(ride_modes, step, B * splits, ride_in, ride_out, sems)

        @pl.when(part == 0)
        def _():
            kpad[0:WINDOW, :] = jnp.zeros((WINDOW, LANES), BF16)
            vpad[0:WINDOW, :] = jnp.zeros((WINDOW, LANES), BF16)
            kpad[WINDOW:, :] = kr_ref[...]
            vpad[WINDOW:, :] = _bf(v_ref[...])
            dkpad[...] = jnp.zeros(dkpad.shape, F32)
            dvpad[...] = jnp.zeros(dvpad.shape, F32)
            dsk[...] = jnp.zeros(dsk.shape, F32)
            daw_ref[...] = jnp.zeros(daw_ref.shape, F32)

        lower = _lower_mask()
        aw = aw_ref[...]

        def block(n, daw):
            r0 = pl.multiple_of(n * WINDOW, WINDOW)
            rows = pl.ds(r0, WINDOW)
            nxt = pl.ds(r0 + WINDOW, WINDOW)
            ob = o_ref[rows, :]
            dn = dan_ref[rows, :]
            ro = lax.rsqrt(_mean_last(ob * ob) + EPS)
            oh = ob * ro
            daw = daw + _sum_rows(dn * oh)
            doh = dn * aw
            do = _bf(ro * (doh - oh * _mean_last(doh * oh)))
            doparts = [do[:, j * LANES:(j + 1) * LANES] for j in range(ATT_WIDTH // LANES)]
            qparts = [qr_ref[rows, j * LANES:(j + 1) * LANES] for j in range(ATT_WIDTH // LANES)]
            for hk in range(ATT_KV_HEADS):
                lanes = slice(hk * ATT_HEAD_DIM, (hk + 1) * ATT_HEAD_DIM)
                qs = _stack_heads(qparts, hk)
                dos = _stack_heads(doparts, hk)
                k_cur, k_prev = kpad[nxt, lanes], kpad[rows, lanes]
                v_cur, v_prev = vpad[nxt, lanes], vpad[rows, lanes]
                p, inv, es = _softmax_window(qs, k_cur, k_prev, lower, n > 0, _sink_row(sink_ref, hk))
                p = p * inv
                dp = jnp.where(lower, _dot_nt(v_cur, dos), _dot_nt(v_prev, dos))
                delta = jnp.sum(p * dp, axis=0, keepdims=True)
                ds = p * (dp - delta)
                sk = (es * inv) * delta
                ds_cur = jnp.where(lower, ds, 0.0)
                p_cur = jnp.where(lower, p, 0.0)
                ds_cur, ds_prev = _bf(ds_cur), _bf(ds - ds_cur)
                p_cur, p_prev = _bf(p_cur), _bf(p - p_cur)
                dqt = (_dot_tn(k_cur, ds_cur) + _dot_tn(k_prev, ds_prev)) * ATT_SCALE
                dkpad[nxt, lanes] += _dot(ds_cur, qs)
                dkpad[rows, lanes] += _dot(ds_prev, qs)
                dvpad[nxt, lanes] += _dot(p_cur, dos)
                dvpad[rows, lanes] += _dot(p_prev, dos)
                for g in range(ATT_GROUP):
                    h = ATT_GROUP * hk + g
                    cols = slice(g * WINDOW, (g + 1) * WINDOW)
                    dqb[:, h * ATT_HEAD_DIM:(h + 1) * ATT_HEAD_DIM] = dqt[:, cols].T
                    head_lane = lax.broadcasted_iota(jnp.int32, dsk.shape, 1) == h
                    dsk[...] += jnp.where(head_lane, -jnp.sum(sk[:, cols], axis=1, keepdims=True), 0.0)
            cs, sl, sr = cos_ref[rows, :], sl_ref[rows, :], sr_ref[rows, :]
            for j in range(ATT_WIDTH // LANES):
                dqkv_ref[rows, j * LANES:(j + 1) * LANES] = _bf(_rope_t(dqb[:, j * LANES:(j + 1) * LANES], cs, sl, sr))
            return daw

        daw = _loop_pairs(part * per, per, block, jnp.zeros((1, ATT_WIDTH), F32))
        daw_ref[...] += jnp.broadcast_to(daw, (8, ATT_WIDTH))
        dsink_ref[...] = dsk[...]

        def finish(n, carry):
            r0 = pl.multiple_of(n * WINDOW, WINDOW)
            rows = pl.ds(r0, WINDOW)
            nxt = pl.ds(r0 + WINDOW, WINDOW)
            cs, sl, sr = cos_ref[rows, :], sl_ref[rows, :], sr_ref[rows, :]
            dqkv_ref[rows, ATT_WIDTH:ATT_WIDTH + LANES] = _bf(_rope_t(dkpad[nxt, :], cs, sl, sr))
            dqkv_ref[rows, ATT_WIDTH + LANES:QKV] = _bf(dvpad[nxt, :])
            return carry

        @pl.when(part == splits - 1)
        def _():
            lax.fori_loop(0, nb, finish, 0)

        _ride_wait(ride_modes, step, B * splits, ride_in, ride_out, sems)

    seq = lambda w, j: pl.BlockSpec((None, T, w), lambda b, s: (b, 0, j))
    full = lambda r, w: pl.BlockSpec((r, w), lambda b, s: (0, 0))
    return pl.pallas_call(
        body, name="attn_bwd", grid=(B, splits),
        in_specs=[seq(ATT_WIDTH, 0), seq(LANES, 0), seq(LANES, 5), seq(ATT_WIDTH, 0), seq(ATT_WIDTH, 0),
                  full(T, LANES), full(T, LANES), full(T, LANES),
                  pl.BlockSpec(memory_space=pltpu.SMEM), full(1, ATT_WIDTH)] + [ANY_SPEC] * nr,
        out_specs=[seq(QKV, 0), pl.BlockSpec((None, 8, LANES), lambda b, s: (b, 0, 0)),
                   pl.BlockSpec((None, 8, ATT_WIDTH), lambda b, s: (b, 0, 0))] + [ANY_SPEC] * nr,
        out_shape=[SDS((B, T, QKV), BF16), SDS((B, 8, LANES), F32), SDS((B, 8, ATT_WIDTH), F32)]
        + _exchange_shapes(ride_srcs, ride_modes),
        scratch_shapes=[pltpu.VMEM((T + WINDOW, LANES), BF16), pltpu.VMEM((T + WINDOW, LANES), BF16),
                        pltpu.VMEM((T + WINDOW, LANES), F32), pltpu.VMEM((T + WINDOW, LANES), F32),
                        pltpu.VMEM((WINDOW, ATT_WIDTH), F32), pltpu.VMEM((8, LANES), F32)] + _exchange_sems(nr),
        compiler_params=_params(("arbitrary", "arbitrary"), VMEM_LIMIT_BIG),
    )(qr, kr, proj3, attn_o, dan, cos, sinl, sinr, sinks, attn_w, *ride_srcs)


def _in_bwd(x2, dx1, dqkv, dhq, dhf, dhi, dhg, mod8, pre_w, w_in_bf, T, ride_srcs, ride_modes):
    N = x2.shape[0]
    TM = _tile_rows(T, big=True)
    tps = T // TM
    nr = len(ride_srcs)
    pieces = [(0, ATT_WIDTH + 2 * LANES), (768, HG_WIDTH), (1280, HG_WIDTH), (1792, HG_WIDTH), (2304, HG_WIDTH)]

    def body(*refs):
        x_ref, dx_ref, p0, p1, p2, p3, p4, mod_ref, pw_ref, w_ref = refs[:10]
        ride_in = refs[10:10 + nr]
        gx_ref, dproj_ref, acc_ref = refs[10 + nr:13 + nr]
        ride_out = refs[13 + nr:13 + 2 * nr]
        sems = refs[13 + 2 * nr:]
        _ride_start(ride_modes, pl.program_id(0), N // TM, ride_in, ride_out, sems)
        sc1 = mod_ref[1:2, :]
        dh = jnp.zeros((TM, D_MODEL), F32)
        for ref, (off, width) in zip((p0, p1, p2, p3, p4), pieces):
            pb = ref[...]
            dproj_ref[:, off:off + width] = pb
            dh = dh + _dot(pb, w_ref[off:off + width, :])
        x = x_ref[...]
        r = lax.rsqrt(_mean_last(x * x) + EPS)
        xh = x * r
        n1 = xh * pw_ref[...]
        dsh1 = _sum_rows(dh)
        dsc1 = _sum_rows(dh * n1)
        dn1 = dh * (1.0 + sc1)
        dw_pre = _sum_rows(dn1 * xh)
        dxh = dn1 * pw_ref[...]
        gx_ref[...] = dx_ref[...] + r * (dxh - xh * _mean_last(dxh * xh))
        _acc_rows(acc_ref, pl.program_id(0) % tps == 0, [dsh1, dsc1, dw_pre])
        _ride_wait(ride_modes, pl.program_id(0), N // TM, ride_in, ride_out, sems)

    row = lambda w: pl.BlockSpec((TM, w), lambda i: (i, 0))
    B = N // T
    return pl.pallas_call(
        body, name="in_bwd", grid=(N // TM,),
        in_specs=[row(D_MODEL), row(D_MODEL), row(768), row(HG_WIDTH), row(HG_WIDTH), row(HG_WIDTH),
                  row(HG_WIDTH), _mod_spec(tps), pl.BlockSpec((1, D_MODEL), lambda i: (0, 0)),
                  pl.BlockSpec((IN_COLS, D_MODEL), lambda i: (0, 0))] + [ANY_SPEC] * nr,
        out_specs=[row(D_MODEL), row(IN_COLS), _mod_spec(tps)] + [ANY_SPEC] * nr,
        out_shape=[SDS((N, D_MODEL), F32), SDS((N, IN_COLS), BF16), SDS((B, 8, D_MODEL), F32)]
        + _exchange_shapes(ride_srcs, ride_modes),
        scratch_shapes=_exchange_sems(nr),
        compiler_params=_params(("arbitrary",), VMEM_LIMIT_BIG),
    )(x2, dx1, dqkv, dhq, dhf, dhi, dhg, mod8, pre_w, w_in_bf, *ride_srcs)


def _matmul_tn(name, a, b, tn, tm=512, by_owner_cols=False):
    K, M = a.shape
    Nc = b.shape[1]
    tm = min(tm, M)

    def body(a_ref, b_ref, o_ref):
        o_ref[...] = _bf(_dot_tn(a_ref[...], b_ref[...]))

    if by_owner_cols:
        assert tn * N_DEV == Nc
        out_shape = SDS((N_DEV, M, tn), BF16)
        out_spec = pl.BlockSpec((None, tm, tn), lambda i, j: (j, i, 0))
    else:
        out_shape = SDS((M, Nc), BF16)
        out_spec = pl.BlockSpec((tm, tn), lambda i, j: (i, j))
    return pl.pallas_call(
        body, name=name, grid=(M // tm, Nc // tn),
        in_specs=[pl.BlockSpec((K, tm), lambda i, j: (0, i)),
                  pl.BlockSpec((K, tn), lambda i, j: (0, j))],
        out_specs=out_spec, out_shape=out_shape,
        compiler_params=_params(("arbitrary", "arbitrary"), VMEM_LIMIT_BIG),
    )(a, b)


def _adamw_math(w, g, m, v):
    m2 = ADAM_B1 * m + (1.0 - ADAM_B1) * g
    v2 = ADAM_B2 * v + (1.0 - ADAM_B2) * (g * g)
    m_hat = m2 / (1.0 - ADAM_B1 ** ADAM_STEP)
    v_hat = v2 / (1.0 - ADAM_B2 ** ADAM_STEP)
    delta = -ADAM_LR * (m_hat / (jnp.sqrt(v_hat) + ADAM_EPS) + ADAM_WD * w)
    return delta, m2, v2


def _pair_add(name, gw, theirs):
    chips, _, r, c = gw.shape
    tr = r
    core = lax.axis_index("c").astype(jnp.int32).reshape(1)

    def body(core_ref, mine_ref, theirs_ref, o_ref):
        o_ref[...] = _bf(mine_ref[...].astype(F32) + theirs_ref[...].astype(F32))

    block = pl.BlockSpec((None, tr, c), lambda s, i, core_ref: (s, i, 0))
    grid_spec = pltpu.PrefetchScalarGridSpec(
        num_scalar_prefetch=1, grid=(chips, r // tr),
        in_specs=[pl.BlockSpec((None, None, tr, c), lambda s, i, core_ref: (s, core_ref[0], i, 0)), block],
        out_specs=block)
    return pl.pallas_call(
        body, name=name, grid_spec=grid_spec, out_shape=SDS((chips, r, c), BF16),
        compiler_params=_params(("arbitrary", "arbitrary")),
    )(core, gw, theirs)


def _reduce_adamw(name, parts, w, m, v):
    r, c = w.shape
    tr = r if r % 256 else 256
    slots = parts.shape[0]

    def body(p_ref, w_ref, m_ref, v_ref, g_ref, d_ref, m2_ref, v2_ref):
        g = p_ref[0].astype(F32)
        for s in range(1, slots):
            g = g + p_ref[s].astype(F32)
        g_ref[...] = g
        d_ref[...], m2_ref[...], v2_ref[...] = _adamw_math(w_ref[...], g, m_ref[...], v_ref[...])

    blk = pl.BlockSpec((tr, c), lambda i: (i, 0))
    return pl.pallas_call(
        body, name=name, grid=(r // tr,),
        in_specs=[pl.BlockSpec((slots, tr, c), lambda i: (0, i, 0)), blk, blk, blk],
        out_specs=[blk] * 4, out_shape=[SDS((r, c), F32)] * 4,
        compiler_params=_params(("arbitrary",), VMEM_LIMIT_BIG),
    )(parts, w, m, v)


def _ada_grad_adamw(c_all, dmod_all, w, m, v):
    r, c = w.shape
    tr = 256
    nb = c_all.shape[0]

    def body(c_ref, dm_ref, w_ref, m_ref, v_ref, g_ref, d_ref, m2_ref, v2_ref):
        cv = c_ref[...]
        g = _dot_tn(cv * _sigmoid(cv), dm_ref[...])
        g_ref[...] = g
        d_ref[...], m2_ref[...], v2_ref[...] = _adamw_math(w_ref[...], g, m_ref[...], v_ref[...])

    blk = pl.BlockSpec((tr, c), lambda i: (i, 0))
    return pl.pallas_call(
        body, name="ada_grad_adamw", grid=(r // tr,),
        in_specs=[pl.BlockSpec((nb, tr), lambda i: (0, i)), pl.BlockSpec((nb, c), lambda i: (0, 0)),
                  blk, blk, blk],
        out_specs=[blk] * 4, out_shape=[SDS((r, c), F32)] * 4,
        compiler_params=_params(("arbitrary",)),
    )(c_all, dmod_all, w, m, v)


_SMALL = [("b_ada", 6144), ("pre_w_mix", 1024), ("attn_sinks", 128), ("attn_out_w", 512), ("lb_table", 1024),
          ("hg_norm_w", 128), ("post_w_mix", 1024), ("pre_w_mlp", 1024), ("post_w_mlp", 1024)]


def _pack_small(acc_in, acc_mix, acc_mlp, dsink, daw, dlb, dgw, lb_p, ada_cols):
    B = acc_in.shape[0]
    width = sum(w for _, w in _SMALL) + LANES

    def body(ain, amix, amlp, dsk_ref, daw_ref, dlb_ref, dgw_ref, lbp_ref, packed_ref, dmod_ref):
        def total(ref, r, w=None):
            out = ref[0, r:r + 1, :] if w is None else ref[0, r:r + 1, :w]
            for b in range(1, B):
                out = out + (ref[b, r:r + 1, :] if w is None else ref[b, r:r + 1, :w])
            return out

        d_b_ada = None
        for b in range(B):
            mods = [ain[b, 0:1, :], ain[b, 1:2, :], amix[b, 0:1, :], amlp[b, 0:1, :], amlp[b, 1:2, :], amlp[b, 2:3, :]]
            full = jnp.concatenate(mods, axis=1)
            for j in range(N_DEV):
                dmod_ref[j, b:b + 1, :] = full[:, j * ada_cols:(j + 1) * ada_cols]
            d_b_ada = full if d_b_ada is None else d_b_ada + full
        d_lb = total(dlb_ref, 0)
        pp = lbp_ref[0:1, :] * lbp_ref[1:2, :]
        pieces = [d_b_ada, total(ain, 2), total(dsk_ref, 0), total(daw_ref, 0), -d_lb * pp, d_lb * pp,
                  total(dgw_ref, 0), total(amix, 1), total(amlp, 3), total(amlp, 4), total(amlp, 5, LANES)]
        off = 0
        for piece in pieces:
            packed_ref[:, off:off + piece.shape[1]] = piece
            off += piece.shape[1]

    return pl.pallas_call(
        body, name="pack_small",
        out_shape=[SDS((1, width), F32), SDS((N_DEV, B, ada_cols), F32)],
    )(acc_in, acc_mix, acc_mlp, dsink, daw, dlb, dgw, lb_p)


def _adamw_small(parts, given):
    names = [n for n, _ in _SMALL]
    flat_in = [a for n in names for a in given[n]]

    def body(*refs):
        p_ref = refs[0]
        in_refs = refs[1:1 + 3 * len(names)]
        out_refs = refs[1 + 3 * len(names):-1]
        loss_ref = refs[-1]
        g = p_ref[0]
        for s in range(1, N_DEV):
            g = g + p_ref[s]
        off = 0
        for i, (name, width) in enumerate(_SMALL):
            w_ref, m_ref, v_ref = in_refs[3 * i:3 * i + 3]
            rows, cols = w_ref.shape
            for r in range(rows):
                gr = g[:, off + r * cols:off + (r + 1) * cols]
                res = (gr,) + _adamw_math(w_ref[r:r + 1, :], gr, m_ref[r:r + 1, :], v_ref[r:r + 1, :])
                for o_ref, val in zip(out_refs[4 * i:4 * i + 4], res):
                    o_ref[r:r + 1, :] = val
            off += width
        loss_ref[...] = g[:, off:off + LANES]

    out_shape = [SDS(given[n][0].shape, F32) for n in names for _ in range(4)] + [SDS((1, LANES), F32)]
    outs = pl.pallas_call(body, name="adamw_small", out_shape=out_shape)(parts, *flat_in)
    return {n: tuple(outs[4 * i:4 * i + 4]) for i, n in enumerate(names)}, outs[-1][0, 0]


def kernel(x, c, w_ada, b_ada, pre_w_mix, w_in, attn_sinks, attn_out_w, lb_table, hg_norm_w, w_out, post_w_mix, pre_w_mlp, w_up, w_down, post_w_mlp, loss_target, m_w_ada, m_b_ada, m_pre_w_mix, m_w_in, m_attn_sinks, m_attn_out_w, m_lb_table, m_hg_norm_w, m_w_out, m_post_w_mix, m_pre_w_mlp, m_w_up, m_w_down, m_post_w_mlp, v_w_ada, v_b_ada, v_pre_w_mix, v_w_in, v_attn_sinks, v_attn_out_w, v_lb_table, v_hg_norm_w, v_w_out, v_post_w_mix, v_pre_w_mlp, v_w_up, v_w_down, v_post_w_mlp):
    B, T, _ = x.shape
    N = B * T
    me = 4 * lax.axis_index("x") + 2 * lax.axis_index("y") + lax.axis_index("c")
    x2 = x.reshape(N, D_MODEL)
    tgt2 = loss_target.reshape(N, D_MODEL)

    w_in_t, m_w_in_t, v_w_in_t = w_in[0].T, m_w_in[0].T, v_w_in[0].T
    w_in_g, c_g = _exchange("gather_w_in", [_bf(w_in_t), c], ["gather"] * 2)
    w_in_f = w_in_g.reshape(IN_COLS, D_MODEL)
    c_all = c_g.reshape(N_DEV * B, D_MODEL)

    ada_cols = w_ada.shape[2]
    b_mine = lax.dynamic_slice(b_ada, (0, me * ada_cols), (1, ada_cols))
    mod_cols = _ada_mod(c_all, w_ada[0], b_mine)
    (mod_g,) = _exchange("scatter_mod", [mod_cols.reshape(N_DEV, B, ada_cols)], ["a2a"])
    mod = mod_g.transpose(1, 0, 2).reshape(B, 6, D_MODEL)
    mod8 = jnp.pad(mod, ((0, 0), (0, 2), (0, 0)))

    lb_p = jax.nn.softmax(lb_table, axis=0)
    lb = lb_p[1:2]
    tables = _rope_tables(T)

    w_up_b, w_down_b = _bf(w_up[0]), _bf(w_down[0])
    proj_a, proj_h, h1, w_out_g, w_up_g0 = _in_proj(x2, mod8, pre_w_mix, w_in_f, T,
                                                    [_bf(w_out[0]), w_up_b[:MLP_HALF]], ["gather"] * 2)
    proj3 = proj_a.reshape(B, T, ATT_COLS)
    proj_h = proj_h.reshape(B, T, IN_COLS - ATT_COLS)
    rec_o, rec_g, s_prev, w_up_g1 = _hgrn_fwd(proj_h, lb, hg_norm_w, [w_up_b[MLP_HALF:]], ["gather"])
    attn_o, attn_n, qr, kr, w_down_g0 = _attn_fwd(proj3, tables, attn_sinks, attn_out_w,
                                                  [w_down_b[:, :MLP_HALF]], ["gather"])
    w_out_f = w_out_g.reshape(D_MODEL, D_MODEL)
    mix, x1, cat, w_down_g1 = _mix_out(x2, attn_n.reshape(N, ATT_WIDTH), rec_g.reshape(N, HG_WIDTH), mod8,
                                       post_w_mix, w_out_f, T, [w_down_b[:, MLP_HALF:]], ["gather"])
    w_up_halves = [w_up_g0, w_up_g1]
    w_down_halves = [w_down_g0.reshape(D_FF, MLP_HALF), w_down_g1.reshape(D_FF, MLP_HALF)]
    up, u, d, h2 = _mlp_fwd(x1, mod8, pre_w_mlp, w_up_halves, w_down_halves, T)

    dx1, dup, dd, acc_mlp = _mlp_bwd(x1, d, up, tgt2, mod8, pre_w_mlp, post_w_mlp, w_up_halves, w_down_halves, T)
    chips = N_DEV // 2
    by_chip = lambda a: a.reshape((chips, 2, a.shape[0] // N_DEV) + a.shape[1:])
    gw_up = _matmul_tn("grad_w_up", h2, dup, D_FF // N_DEV, by_owner_cols=True)
    gw_up = gw_up.reshape(chips, 2, D_MODEL, D_FF // N_DEV)
    gw_down = by_chip(_matmul_tn("grad_w_down", u, dd, 512))
    dan, drg, dmix, acc_mix, q_down, q_up = _mix_bwd(mix, dx1, mod8, post_w_mix, w_out_f, T,
                                                     [gw_down, gw_up], ["pair"] * 2)
    p_down, p_up = _pair_add("pair_add_w_down", gw_down, q_down), _pair_add("pair_add_w_up", gw_up, q_up)
    gw_out = _matmul_tn("grad_w_out", cat, dmix, 512).reshape(N_DEV, D_MODEL // N_DEV, D_MODEL)
    dhq, dhf, dhi, dhg, dlb_p, dgw_p, r_down, r_up = _hgrn_bwd(
        proj_h, lb, hg_norm_w, rec_o, s_prev, drg.reshape(B, T, HG_WIDTH), [p_down, p_up], ["chips"] * 2)
    dqkv, dsink_p, daw_p, r_out = _attn_bwd(qr, kr, proj3, attn_o, dan.reshape(B, T, ATT_WIDTH), tables,
                                            attn_sinks, attn_out_w, [gw_out], ["a2a"])
    flat = lambda a: a.reshape(N, a.shape[-1])
    grad_x, dproj, acc_in = _in_bwd(x2, dx1, flat(dqkv), flat(dhq), flat(dhf), flat(dhi), flat(dhg),
                                    mod8, pre_w_mix, w_in_f, T, [], [])

    gw_in = by_chip(_matmul_tn("grad_w_in", dproj, h1, 512, tm=IN_COLS // 2))
    (q_in,) = _exchange("pair_w_in", [gw_in], ["pair"])
    p_in = _pair_add("pair_add_w_in", gw_in, q_in)

    packed, dmod_blocks = _pack_small(acc_in, acc_mix, acc_mlp, dsink_p, daw_p, dlb_p, dgw_p, lb_p, ada_cols)
    r_in, r_dmod, r_small = _exchange("reduce_grads", [p_in, dmod_blocks, packed], ["chips", "a2a", "gather"])

    res = {}
    res["w_in"] = tuple(a.T for a in _reduce_adamw("adamw_w_in", r_in, w_in_t, m_w_in_t, v_w_in_t))
    res["w_out"] = _reduce_adamw("adamw_w_out", r_out, w_out[0], m_w_out[0], v_w_out[0])
    res["w_up"] = _reduce_adamw("adamw_w_up", r_up, w_up[0], m_w_up[0], v_w_up[0])
    res["w_down"] = _reduce_adamw("adamw_w_down", r_down, w_down[0], m_w_down[0], v_w_down[0])
    res["w_ada"] = _ada_grad_adamw(c_all, r_dmod.reshape(N_DEV * B, ada_cols), w_ada[0], m_w_ada[0], v_w_ada[0])

    given = dict(b_ada=(b_ada, m_b_ada, v_b_ada), pre_w_mix=(pre_w_mix, m_pre_w_mix, v_pre_w_mix),
                 attn_sinks=(attn_sinks, m_attn_sinks, v_attn_sinks),
                 attn_out_w=(attn_out_w, m_attn_out_w, v_attn_out_w), lb_table=(lb_table, m_lb_table, v_lb_table),
                 hg_norm_w=(hg_norm_w, m_hg_norm_w, v_hg_norm_w), post_w_mix=(post_w_mix, m_post_w_mix, v_post_w_mix),
                 pre_w_mlp=(pre_w_mlp, m_pre_w_mlp, v_pre_w_mlp), post_w_mlp=(post_w_mlp, m_post_w_mlp, v_post_w_mlp))
    small_res, loss = _adamw_small(r_small, given)
    res.update(small_res)

    order = ["w_ada", "b_ada", "pre_w_mix", "w_in", "attn_sinks", "attn_out_w", "lb_table", "hg_norm_w", "w_out",
             "post_w_mix", "pre_w_mlp", "w_up", "w_down", "post_w_mlp"]
    big = {"w_ada", "w_in", "w_out", "w_up", "w_down"}
    outs = [loss, grad_x.reshape(B, T, D_MODEL)]
    for i in range(4):
        for k in order:
            a = res[k][i]
            outs.append(a[None] if k in big else a)
    return tuple(outs)
```

```python
import jax
import jax.numpy as jnp
import numpy as np
from jax import lax
from jax.experimental import pallas as pl
from jax.experimental.pallas import tpu as pltpu

F32 = jnp.float32
BF16 = jnp.bfloat16
SDS = jax.ShapeDtypeStruct

D_MODEL = 1024
ATT_WIDTH = 512
ATT_HEAD_DIM = 64
ATT_KV_HEADS = 2
ATT_GROUP = 4
WINDOW = 128
ROPE_DIM = 16
ROPE_THETA = 500000.0
HG_WIDTH = 512
HG_HEAD_DIM = 128
HG_HEADS = 4
HG_CHUNK = 32
IN_COLS = 2816
ATT_COLS = 768
D_FF = 4096
EPS = 1e-6
N_DEV = 8

ADAM_LR = 0.001
ADAM_B1 = 0.9
ADAM_B2 = 0.999
ADAM_EPS = 1e-08
ADAM_WD = 0.01
ADAM_STEP = 10

VMEM_LIMIT_BIG = 56 << 20
LANES = 128

MESH = pl.DeviceIdType.MESH
NT_DIMS = (((1,), (1,)), ((), ()))
TN_DIMS = (((0,), (0,)), ((), ()))


def _dot(a, b):
    return jnp.dot(a, b, preferred_element_type=F32)


def _dot_nt(a, b):
    return lax.dot_general(a, b, NT_DIMS, preferred_element_type=F32)


def _dot_tn(a, b):
    return lax.dot_general(a, b, TN_DIMS, preferred_element_type=F32)


def _bf(a):
    return a.astype(BF16)


def _sigmoid(a):
    return 0.5 * jnp.tanh(0.5 * a) + 0.5


def _mean_last(a):
    return jnp.mean(a, axis=-1, keepdims=True)


def _sum_rows(a):
    return jnp.sum(a, axis=0, keepdims=True)


def _loop_pairs(first, count, body, init, per_trip=2):
    if count % per_trip:
        return lax.fori_loop(first, first + count, body, init)

    def trip(i, c):
        for k in range(per_trip):
            c = body(first + per_trip * i + k, c)
        return c

    return lax.fori_loop(0, count // per_trip, trip, init)


def _params(sem=None, vmem=None):
    kw = {}
    if sem is not None:
        kw["dimension_semantics"] = sem
    if vmem is not None:
        kw["vmem_limit_bytes"] = vmem
    return pltpu.CompilerParams(**kw)


ANY_SPEC = pl.BlockSpec(memory_space=pl.ANY)


def _exchange_shapes(srcs, modes):
    out_shape = []
    for s, m in zip(srcs, modes):
        shp = {"gather": (N_DEV,) + tuple(s.shape), "pair": (s.shape[0],) + tuple(s.shape[2:])}.get(m, tuple(s.shape))
        out_shape.append(SDS(shp, s.dtype))
    return out_shape


def _exchange_sems(n):
    if n == 0:
        return []
    return [pltpu.SemaphoreType.DMA((n, N_DEV - 1)), pltpu.SemaphoreType.DMA((n, N_DEV - 1)),
            pltpu.SemaphoreType.DMA((n,))]


SIBLING = 1
OTHER_CHIPS = (2, 4, 6)


def _related(k):
    x, y, c = lax.axis_index("x"), lax.axis_index("y"), lax.axis_index("c")
    px, py, pc = x ^ ((k >> 2) & 1), y ^ ((k >> 1) & 1), c ^ (k & 1)
    return (px, py, pc), 4 * px + 2 * py + pc


def _exchange_phases(modes, src_refs, out_refs, send_sems, recv_sems, own_sems):
    _, me = _related(0)
    sib_dev, sib = _related(SIBLING)
    start, middle, end = [], [], []

    def remote(a, i, src, dst, dev):
        return pltpu.make_async_remote_copy(src_ref=src, dst_ref=dst, send_sem=send_sems.at[a, i],
                                            recv_sem=recv_sems.at[a, i], device_id=dev, device_id_type=MESH)

    for a, mode in enumerate(modes):
        out = out_refs[a]
        if mode == "gather":
            src = src_refs[a]
            own = pltpu.make_async_copy(src, out.at[me], own_sems.at[a])
            to_sib = remote(a, 0, src, out.at[me], sib_dev)
            start += [own.start, to_sib.start]
            end += [remote(a, 0, src, out.at[sib], sib_dev).wait_recv, to_sib.wait_send, own.wait]
            for j, k in enumerate(OTHER_CHIPS, start=1):
                dev, peer = _related(k)
                _, peer_sib = _related(k ^ SIBLING)
                send = remote(a, j, src, out.at[me], dev)
                passed = remote(a, 3 + j, out.at[peer], out.at[peer], sib_dev)
                start.append(send.start)
                middle += [remote(a, j, src, out.at[peer], dev).wait_recv, passed.start]
                end += [remote(a, 3 + j, out.at[peer_sib], out.at[peer_sib], sib_dev).wait_recv,
                        send.wait_send, passed.wait_send]
        elif mode == "pair":
            core = lax.axis_index("c")
            for s in range(N_DEV // 2):
                send = remote(a, s, src_refs[a].at[s, 1 - core], out.at[s], sib_dev)
                start.append(send.start)
                end += [remote(a, s, src_refs[a].at[s, 1 - core], out.at[s], sib_dev).wait_recv, send.wait_send]
        elif mode == "chips":
            chip = me // 2
            own = pltpu.make_async_copy(src_refs[a].at[chip], out.at[chip], own_sems.at[a])
            start.append(own.start)
            end.append(own.wait)
            for j, k in enumerate(OTHER_CHIPS, start=1):
                dev, peer = _related(k)
                send = remote(a, j, src_refs[a].at[peer // 2], out.at[chip], dev)
                start.append(send.start)
                end += [remote(a, j, src_refs[a].at[peer // 2], out.at[peer // 2], dev).wait_recv, send.wait_send]
        else:
            own = pltpu.make_async_copy(src_refs[a].at[me], out.at[me], own_sems.at[a])
            start.append(own.start)
            end.append(own.wait)
            for k in range(1, N_DEV):
                dev, peer = _related(k)
                send = remote(a, k - 1, src_refs[a].at[peer], out.at[me], dev)
                start.append(send.start)
                end += [remote(a, k - 1, src_refs[a].at[peer], out.at[peer], dev).wait_recv, send.wait_send]
    return start, middle, end


def _run(actions):
    for act in actions:
        act()


def _exchange(name, srcs, modes):
    n = len(srcs)

    def body(*refs):
        start, middle, end = _exchange_phases(modes, refs[:n], refs[n:2 * n], *refs[2 * n:])
        _run(start)
        _run(middle)
        _run(end)

    return pl.pallas_call(
        body, name=name, out_shape=_exchange_shapes(srcs, modes),
        in_specs=[ANY_SPEC] * n, out_specs=[ANY_SPEC] * n,
        scratch_shapes=_exchange_sems(n),
    )(*srcs)


def _ride_start(modes, step, steps, src_refs, out_refs, sems):
    if not modes:
        return
    middle_step = max(0, steps - 1 - steps // 8)

    @pl.when(step == 0)
    def _():
        _run(_exchange_phases(modes, src_refs, out_refs, *sems)[0])

    if "gather" in modes:
        @pl.when(step == middle_step)
        def _():
            _run(_exchange_phases(modes, src_refs, out_refs, *sems)[1])


def _ride_wait(modes, step, steps, src_refs, out_refs, sems):
    if not modes:
        return

    @pl.when(step == steps - 1)
    def _():
        _run(_exchange_phases(modes, src_refs, out_refs, *sems)[2])


def _ada_mod(c_all, w_ada, b_ada_mine):
    nb, cols = c_all.shape[0], w_ada.shape[1]

    def body(c_ref, w_ref, b_ref, o_ref):
        cv = c_ref[...]
        ca = cv * _sigmoid(cv)
        o_ref[...] = _dot(ca, w_ref[...]) + b_ref[...]

    return pl.pallas_call(body, name="ada_mod", out_shape=SDS((nb, cols), F32))(c_all, w_ada, b_ada_mine)


def _tile_rows(T, big=False):
    return min(512 if big else 256, T)


def _mod_spec(tps):
    return pl.BlockSpec((None, 8, D_MODEL), lambda i: (i // tps, 0, 0))


def _in_proj(x2, mod8, pre_w, w_in_bf, T, ride_srcs, ride_modes):
    N = x2.shape[0]
    TM = _tile_rows(T, big=True)
    tps = T // TM
    nr = len(ride_srcs)

    def body(*refs):
        x_ref, mod_ref, pw_ref, w_ref = refs[:4]
        ride_in = refs[4:4 + nr]
        pa_ref, ph_ref, h1_ref = refs[4 + nr:7 + nr]
        ride_out = refs[7 + nr:7 + 2 * nr]
        sems = refs[7 + 2 * nr:]
        _ride_start(ride_modes, pl.program_id(0), N // TM, ride_in, ride_out, sems)
        x = x_ref[...]
        r = lax.rsqrt(_mean_last(x * x) + EPS)
        h = (x * r * pw_ref[...]) * (1.0 + mod_ref[1:2, :]) + mod_ref[0:1, :]
        hb = _bf(h)
        h1_ref[...] = hb
        pa_ref[...] = _dot_nt(hb, w_ref[:ATT_COLS, :])
        ph_ref[...] = _dot_nt(hb, w_ref[ATT_COLS:, :])
        _ride_wait(ride_modes, pl.program_id(0), N // TM, ride_in, ride_out, sems)

    return pl.pallas_call(
        body, name="in_proj", grid=(N // TM,),
        in_specs=[pl.BlockSpec((TM, D_MODEL), lambda i: (i, 0)), _mod_spec(tps),
                  pl.BlockSpec((1, D_MODEL), lambda i: (0, 0)),
                  pl.BlockSpec((IN_COLS, D_MODEL), lambda i: (0, 0))] + [ANY_SPEC] * nr,
        out_specs=[pl.BlockSpec((TM, ATT_COLS), lambda i: (i, 0)),
                   pl.BlockSpec((TM, IN_COLS - ATT_COLS), lambda i: (i, 0)),
                   pl.BlockSpec((TM, D_MODEL), lambda i: (i, 0))] + [ANY_SPEC] * nr,
        out_shape=[SDS((N, ATT_COLS), F32), SDS((N, IN_COLS - ATT_COLS), F32), SDS((N, D_MODEL), BF16)]
        + _exchange_shapes(ride_srcs, ride_modes),
        scratch_shapes=_exchange_sems(nr),
        compiler_params=_params(("arbitrary",), VMEM_LIMIT_BIG),
    )(x2, mod8, pre_w, w_in_bf, *ride_srcs)


def _rope_tables(T):
    half = ROPE_DIM // 2
    f32 = np.float32
    inv_freq = (f32(ROPE_THETA) ** (-np.arange(0, ROPE_DIM, 2, dtype=f32) / f32(ROPE_DIM))).astype(f32)
    ang = np.arange(T, dtype=f32)[:, None] * inv_freq[None, :]
    cos, sin = np.cos(ang).astype(f32), np.sin(ang).astype(f32)
    ones = np.ones((T, ATT_HEAD_DIM - ROPE_DIM), f32)
    zeros = np.zeros((T, ATT_HEAD_DIM - ROPE_DIM), f32)
    zh = np.zeros((T, half), f32)
    cos64 = np.concatenate([cos, cos, ones], axis=1)
    sin_left = np.concatenate([-sin, zh, zeros], axis=1)
    sin_right = np.concatenate([zh, sin, zeros], axis=1)
    rep = LANES // ATT_HEAD_DIM
    return tuple(jnp.asarray(np.tile(t, (1, rep))) for t in (cos64, sin_left, sin_right))


def _rope(xc, cs, sl, sr):
    return xc * cs + pltpu.roll(xc, LANES - 8, 1) * sl + pltpu.roll(xc, 8, 1) * sr


def _rope_t(dy, cs, sl, sr):
    return dy * cs + pltpu.roll(dy * sl, 8, 1) + pltpu.roll(dy * sr, LANES - 8, 1)


ATT_SCALE = ATT_HEAD_DIM ** -0.5
ATT_SPLITS = 4


def _lower_mask():
    j = lax.broadcasted_iota(jnp.int32, (WINDOW, ATT_GROUP * WINDOW), 0)
    i = lax.broadcasted_iota(jnp.int32, (WINDOW, ATT_GROUP * WINDOW), 1) & (WINDOW - 1)
    return j <= i


def _sink_row(sink_ref, hk):
    return jnp.concatenate(
        [jnp.full((1, WINDOW), sink_ref[0, ATT_GROUP * hk + g], F32) for g in range(ATT_GROUP)], axis=1)


def _softmax_window(qs, k_cur, k_prev, lower, has_prev, sink):
    s_prev = jnp.where(has_prev, _dot_nt(k_prev, qs), jnp.finfo(F32).min)
    s = jnp.where(lower, _dot_nt(k_cur, qs), s_prev)
    m = jnp.maximum(jnp.max(s, axis=0, keepdims=True), sink)
    p = jnp.exp(s - m)
    es = jnp.exp(sink - m)
    inv = 1.0 / (jnp.sum(p, axis=0, keepdims=True) + es)
    return p, inv, es


def _stack_heads(parts, hk):
    hs = []
    for g in range(ATT_GROUP):
        h = ATT_GROUP * hk + g
        hs.append(parts[h // 2][:, (h % 2) * ATT_HEAD_DIM:(h % 2 + 1) * ATT_HEAD_DIM])
    return jnp.concatenate(hs, axis=0)


def _attn_fwd(proj3, tables, sinks, attn_w, ride_srcs, ride_modes):
    B, T, _ = proj3.shape
    nb = T // WINDOW
    splits = min(ATT_SPLITS, nb)
    per = nb // splits
    nr = len(ride_srcs)
    cos, sinl, sinr = tables

    def body(*refs):
        q_ref, k_ref, v_ref, cos_ref, sl_ref, sr_ref, sink_ref, aw_ref = refs[:8]
        ride_in = refs[8:8 + nr]
        o_ref, an_ref, qr_ref, kr_ref = refs[8 + nr:12 + nr]
        ride_out = refs[12 + nr:12 + 2 * nr]
        kpad, vpad = refs[12 + 2 * nr:14 + 2 * nr]
        sems = refs[14 + 2 * nr:]
        part = pl.program_id(1)
        step = pl.program_id(0) * splits + part
        _ride_start(ride_modes, step, B * splits, ride_in, ride_out, sems)

        @pl.when(part == 0)
        def _():
            kpad[0:WINDOW, :] = jnp.zeros((WINDOW, LANES), BF16)
            vpad[0:WINDOW, :] = jnp.zeros((WINDOW, LANES), BF16)

        lower = _lower_mask()

        def block(n, carry):
            r0 = pl.multiple_of(n * WINDOW, WINDOW)
            rows = pl.ds(r0, WINDOW)
            nxt = pl.ds(r0 + WINDOW, WINDOW)
            cs, sl, sr = cos_ref[rows, :], sl_ref[rows, :], sr_ref[rows, :]
            kb = _bf(_rope(k_ref[rows, :], cs, sl, sr))
            vb = _bf(v_ref[rows, :])
            kpad[nxt, :] = kb
            kr_ref[rows, :] = kb
            vpad[nxt, :] = vb
            qparts = []
            for j in range(ATT_WIDTH // LANES):
                qp = _bf(_rope(q_ref[rows, j * LANES:(j + 1) * LANES], cs, sl, sr) * ATT_SCALE)
                qr_ref[rows, j * LANES:(j + 1) * LANES] = qp
                qparts.append(qp)
            for hk in range(ATT_KV_HEADS):
                lanes = slice(hk * ATT_HEAD_DIM, (hk + 1) * ATT_HEAD_DIM)
                qs = _stack_heads(qparts, hk)
                p, inv, _ = _softmax_window(qs, kb[:, lanes], kpad[rows, lanes], lower, n > 0,
                                            _sink_row(sink_ref, hk))
                p_cur = jnp.where(lower, p, 0.0)
                ot = (_dot_tn(vb[:, lanes], _bf(p_cur)) + _dot_tn(vpad[rows, lanes], _bf(p - p_cur))) * inv
                for g in range(ATT_GROUP):
                    h = ATT_GROUP * hk + g
                    o_ref[rows, h * ATT_HEAD_DIM:(h + 1) * ATT_HEAD_DIM] = ot[:, g * WINDOW:(g + 1) * WINDOW].T
            ob = o_ref[rows, :]
            an_ref[rows, :] = _bf(ob * lax.rsqrt(_mean_last(ob * ob) + EPS) * aw_ref[...])
            return carry

        _loop_pairs(part * per, per, block, 0)
        _ride_wait(ride_modes, step, B * splits, ride_in, ride_out, sems)

    seq = lambda w, j: pl.BlockSpec((None, T, w), lambda b, s: (b, 0, j))
    full = lambda r, w: pl.BlockSpec((r, w), lambda b, s: (0, 0))
    return pl.pallas_call(
        body, name="attn_fwd", grid=(B, splits),
        in_specs=[seq(ATT_WIDTH, 0), seq(LANES, 4), seq(LANES, 5),
                  full(T, LANES), full(T, LANES), full(T, LANES),
                  pl.BlockSpec(memory_space=pltpu.SMEM), full(1, ATT_WIDTH)] + [ANY_SPEC] * nr,
        out_specs=[seq(ATT_WIDTH, 0), seq(ATT_WIDTH, 0), seq(ATT_WIDTH, 0), seq(LANES, 0)] + [ANY_SPEC] * nr,
        out_shape=[SDS((B, T, ATT_WIDTH), F32), SDS((B, T, ATT_WIDTH), BF16),
                   SDS((B, T, ATT_WIDTH), BF16), SDS((B, T, LANES), BF16)] + _exchange_shapes(ride_srcs, ride_modes),
        scratch_shapes=[pltpu.VMEM((T + WINDOW, LANES), BF16), pltpu.VMEM((T + WINDOW, LANES), BF16)]
        + _exchange_sems(nr),
        compiler_params=_params(("arbitrary", "arbitrary"), VMEM_LIMIT_BIG),
    )(proj3, proj3, proj3, cos, sinl, sinr, sinks, attn_w, *ride_srcs)


HG_GROUP = 8
HG_ROWS = HG_GROUP * HG_CHUNK


HG_STACK = HG_GROUP * HG_HEAD_DIM


def _group_mask():
    r = lax.broadcasted_iota(jnp.int32, (HG_ROWS, HG_ROWS), 0)
    c = lax.broadcasted_iota(jnp.int32, (HG_ROWS, HG_ROWS), 1)
    return ((r // HG_CHUNK) == (c // HG_CHUNK)) & (r >= c)


def _spread(a):
    blocks = []
    for c in range(HG_GROUP):
        above = jnp.zeros((c * HG_CHUNK, HG_HEAD_DIM), a.dtype)
        below = jnp.zeros(((HG_GROUP - 1 - c) * HG_CHUNK, HG_HEAD_DIM), a.dtype)
        blocks.append(jnp.concatenate([p for p in (above, a[_chunk_rows(c), :], below) if p.shape[0]], axis=0))
    return jnp.concatenate(blocks, axis=1)


def _pick(r):
    return jnp.concatenate([r[_chunk_rows(c), c * HG_HEAD_DIM:(c + 1) * HG_HEAD_DIM] for c in range(HG_GROUP)], axis=0)


def _lane_block(a, c):
    return a[:, c * HG_HEAD_DIM:(c + 1) * HG_HEAD_DIM]


def _chunk_cumsum(a, reverse=False):
    n = a.shape[0]
    pos = lax.broadcasted_iota(jnp.int32, a.shape, 0) % HG_CHUNK
    shift = 1
    while shift < HG_CHUNK:
        if reverse:
            a = a + jnp.where(pos < HG_CHUNK - shift, pltpu.roll(a, n - shift, 0), 0.0)
        else:
            a = a + jnp.where(pos >= shift, pltpu.roll(a, shift, 0), 0.0)
        shift *= 2
    return a


def _chunk_bcast(rows_1x128):
    return jnp.concatenate([jnp.broadcast_to(r, (HG_CHUNK, HG_HEAD_DIM)) for r in rows_1x128], axis=0)


def _hgrn_gates(hq, hf, lb):
    sq = _sigmoid(hq)
    q = hq * sq
    sg = _sigmoid(hf)
    f = lb + (1.0 - lb) * sg
    k = 1.0 - f
    logf = jnp.log(f)
    b = _chunk_cumsum(logf)
    bl = [_sum_rows(logf[_chunk_rows(c), :]) for c in range(HG_GROUP)]
    eb, enb, e2 = jnp.exp(b), jnp.exp(-b), jnp.exp(_chunk_bcast(bl) - b)
    ebl = [jnp.exp(r) for r in bl]
    return dict(sq=sq, sg=sg, f=f, eb=eb, enb=enb, e2=e2, ebl=ebl, qd=q * eb, kd=k * enb, k2=k * e2)


def _chunk_rows(c):
    return slice(c * HG_CHUNK, (c + 1) * HG_CHUNK)


def _head_lanes(h):
    return slice(h * HG_HEAD_DIM, (h + 1) * HG_HEAD_DIM)


def _hgrn_fwd(proj_h, lb, hg_w, ride_srcs, ride_modes):
    B, T, _ = proj_h.shape
    ng = T // HG_ROWS
    nr = len(ride_srcs)

    def body(*refs):
        hq_ref, hf_ref, hi_ref, hg_ref, lb_ref, gw_ref = refs[:6]
        ride_in = refs[6:6 + nr]
        o_ref, rg_ref, sp_ref = refs[6 + nr:9 + nr]
        ride_out = refs[9 + nr:9 + 2 * nr]
        st = refs[9 + 2 * nr]
        sems = refs[10 + 2 * nr:]
        gi = pl.program_id(1)
        step = pl.program_id(0) * ng + gi
        _ride_start(ride_modes, step, B * ng, ride_in, ride_out, sems)

        @pl.when(gi == 0)
        def _():
            st[...] = jnp.zeros(st.shape, F32)

        lo = _group_mask()
        for h in range(HG_HEADS):
            lanes = _head_lanes(h)
            gt = _hgrn_gates(hq_ref[:, lanes], hf_ref[:, lanes], lb_ref[:, lanes])
            v, qd, kd = _bf(hi_ref[:, lanes]), _bf(gt["qd"]), _bf(gt["kd"])
            a = jnp.where(lo, _dot_nt(qd, kd), 0.0)
            kv = _dot_tn(v, _spread(_bf(gt["k2"])))
            s = st[h]
            before = []
            for c in range(HG_GROUP):
                before.append(s)
                s = s * gt["ebl"][c] + _lane_block(kv, c)
            st[h] = s
            sp = jnp.concatenate(before, axis=1)
            sp_ref[h] = sp
            o = _dot(_bf(a), v) + _dot_nt(_spread(qd), _bf(sp))
            o_ref[:, lanes] = o
            hg = hg_ref[:, lanes]
            rn = o * lax.rsqrt(_mean_last(o * o) + EPS) * gw_ref[...]
            rg_ref[:, lanes] = _bf(rn * (hg * _sigmoid(hg)))
        _ride_wait(ride_modes, step, B * ng, ride_in, ride_out, sems)

    part = lambda j: pl.BlockSpec((None, HG_ROWS, HG_WIDTH), lambda b, g: (b, g, j))
    return pl.pallas_call(
        body, name="hgrn_fwd", grid=(B, ng),
        in_specs=[part(0), part(1), part(2), part(3),
                  pl.BlockSpec((1, HG_WIDTH), lambda b, g: (0, 0)),
                  pl.BlockSpec((1, LANES), lambda b, g: (0, 0))] + [ANY_SPEC] * nr,
        out_specs=[part(0), part(0),
                   pl.BlockSpec((None, HG_HEADS, None, HG_HEAD_DIM, HG_STACK), lambda b, g: (b, 0, g, 0, 0))]
        + [ANY_SPEC] * nr,
        out_shape=[SDS((B, T, HG_WIDTH), F32), SDS((B, T, HG_WIDTH), BF16),
                   SDS((B, HG_HEADS, ng, HG_HEAD_DIM, HG_STACK), F32)] + _exchange_shapes(ride_srcs, ride_modes),
        scratch_shapes=[pltpu.VMEM((HG_HEADS, HG_HEAD_DIM, HG_HEAD_DIM), F32)] + _exchange_sems(nr),
        compiler_params=_params(("arbitrary", "arbitrary"), VMEM_LIMIT_BIG),
    )(proj_h, proj_h, proj_h, proj_h, lb, hg_w, *ride_srcs)


def _mix_out(x2, attn_n, rec_g, mod8, post_w, w_out_bf, T, ride_srcs, ride_modes):
    N = x2.shape[0]
    TM = _tile_rows(T, big=True)
    tps = T // TM
    nr = len(ride_srcs)

    def body(*refs):
        x_ref, an_ref, rg_ref, mod_ref, pw_ref, w_ref = refs[:6]
        ride_in = refs[6:6 + nr]
        mix_ref, x1_ref, cat_ref = refs[6 + nr:9 + nr]
        ride_out = refs[9 + nr:9 + 2 * nr]
        sems = refs[9 + 2 * nr:]
        _ride_start(ride_modes, pl.program_id(0), N // TM, ride_in, ride_out, sems)
        cat = jnp.concatenate([an_ref[...], rg_ref[...]], axis=1)
        cat_ref[...] = cat
        mix = _dot(cat, w_ref[...])
        mix_ref[...] = mix
        r = lax.rsqrt(_mean_last(mix * mix) + EPS)
        x1_ref[...] = x_ref[...] + mod_ref[2:3, :] * (mix * r * pw_ref[...])
        _ride_wait(ride_modes, pl.program_id(0), N // TM, ride_in, ride_out, sems)

    row = lambda w: pl.BlockSpec((TM, w), lambda i: (i, 0))
    return pl.pallas_call(
        body, name="mix_out", grid=(N // TM,),
        in_specs=[row(D_MODEL), row(ATT_WIDTH), row(HG_WIDTH), _mod_spec(tps),
                  pl.BlockSpec((1, D_MODEL), lambda i: (0, 0)),
                  pl.BlockSpec((D_MODEL, D_MODEL), lambda i: (0, 0))] + [ANY_SPEC] * nr,
        out_specs=[row(D_MODEL), row(D_MODEL), row(D_MODEL)] + [ANY_SPEC] * nr,
        out_shape=[SDS((N, D_MODEL), F32), SDS((N, D_MODEL), F32), SDS((N, D_MODEL), BF16)]
        + _exchange_shapes(ride_srcs, ride_modes),
        scratch_shapes=_exchange_sems(nr),
        compiler_params=_params(("arbitrary",), VMEM_LIMIT_BIG),
    )(x2, attn_n, rec_g, mod8, post_w, w_out_bf, *ride_srcs)


def _load_weights_once(pairs, sem):
    @pl.when(pl.program_id(0) == 0)
    def _():
        cps = [pltpu.make_async_copy(src, dst, sem.at[i]) for i, (src, dst) in enumerate(pairs)]
        for cp in cps:
            cp.start()
        for cp in cps:
            cp.wait()


MLP_HALF = D_MODEL // 2
MLP_PIECES = 2 * N_DEV + 2


def _mlp_weight_pieces(wu_a, wu_b, wd_a, wd_b, wu, wd):
    cols = D_FF // N_DEV
    pairs = []
    for h, half in enumerate((wu_a, wu_b)):
        for j in range(N_DEV):
            pairs.append((half.at[j], wu.at[pl.ds(h * MLP_HALF, MLP_HALF), pl.ds(j * cols, cols)]))
    for h, half in enumerate((wd_a, wd_b)):
        pairs.append((half, wd.at[:, pl.ds(h * MLP_HALF, MLP_HALF)]))
    return pairs


def _mlp_fwd(x1, mod8, pre_w, w_up_halves, w_down_halves, T):
    N = x1.shape[0]
    TM = _tile_rows(T)
    tps = T // TM

    def body(x_ref, mod_ref, pw_ref, wua, wub, wda, wdb, up_ref, u_ref, d_ref, h2_ref, wu, wd, sem):
        _load_weights_once(_mlp_weight_pieces(wua, wub, wda, wdb, wu, wd), sem)
        x = x_ref[...]
        r = lax.rsqrt(_mean_last(x * x) + EPS)
        h = (x * r * pw_ref[...]) * (1.0 + mod_ref[4:5, :]) + mod_ref[3:4, :]
        hb = _bf(h)
        h2_ref[...] = hb
        up = _dot(hb, wu[...])
        up_ref[...] = up
        ru = jnp.maximum(up, 0.0)
        u = _bf(ru * ru)
        u_ref[...] = u
        d_ref[...] = _dot(u, wd[...])

    row = lambda w: pl.BlockSpec((TM, w), lambda i: (i, 0))
    return pl.pallas_call(
        body, name="mlp_fwd", grid=(N // TM,),
        in_specs=[row(D_MODEL), _mod_spec(tps), pl.BlockSpec((1, D_MODEL), lambda i: (0, 0))] + [ANY_SPEC] * 4,
        out_specs=[row(D_FF), row(D_FF), row(D_MODEL), row(D_MODEL)],
        out_shape=[SDS((N, D_FF), F32), SDS((N, D_FF), BF16), SDS((N, D_MODEL), F32), SDS((N, D_MODEL), BF16)],
        scratch_shapes=[pltpu.VMEM((D_MODEL, D_FF), BF16), pltpu.VMEM((D_FF, D_MODEL), BF16),
                        pltpu.SemaphoreType.DMA((MLP_PIECES,))],
        compiler_params=_params(("arbitrary",), VMEM_LIMIT_BIG),
    )(x1, mod8, pre_w, *w_up_halves, *w_down_halves)


def _acc_rows(acc_ref, first, rows):
    @pl.when(first)
    def _():
        acc_ref[...] = jnp.zeros(acc_ref.shape, F32)
    for i, r in enumerate(rows):
        acc_ref[i:i + 1, :] += r


def _mlp_bwd(x1, d, up, tgt, mod8, pre_w, post_w, w_up_halves, w_down_halves, T):
    N = x1.shape[0]
    TM = _tile_rows(T)
    tps = T // TM

    def body(x_ref, d_ref, up_ref, t_ref, mod_ref, pw_ref, qw_ref, wua, wub, wda, wdb,
             dx_ref, dup_ref, dd_ref, acc_ref, wd, wu, sem):
        _load_weights_once(_mlp_weight_pieces(wua, wub, wda, wdb, wu, wd), sem)
        sh2, sc2, g2 = mod_ref[3:4, :], mod_ref[4:5, :], mod_ref[5:6, :]
        x = x_ref[...]
        r1 = lax.rsqrt(_mean_last(x * x) + EPS)
        xh = x * r1
        n2 = xh * pw_ref[...]
        dv = d_ref[...]
        rd = lax.rsqrt(_mean_last(dv * dv) + EPS)
        dh = dv * rd
        rr = dh * qw_ref[...]
        e = x + g2 * rr - t_ref[...]
        loss = 0.5 * jnp.sum(_sum_rows(e * e), axis=1, keepdims=True) / D_MODEL
        dy = e * (1.0 / D_MODEL)
        dg2 = _sum_rows(dy * rr)
        drr = dy * g2
        dw_post = _sum_rows(drr * dh)
        ddh = drr * qw_ref[...]
        dd = _bf(rd * (ddh - dh * _mean_last(ddh * dh)))
        dd_ref[...] = dd
        ru = jnp.maximum(up_ref[...], 0.0)
        dup = _bf(_dot_nt(dd, wd[...]) * (2.0 * ru))
        dup_ref[...] = dup
        dh2 = _dot_nt(dup, wu[...])
        dsh2 = _sum_rows(dh2)
        dsc2 = _sum_rows(dh2 * n2)
        dn2 = dh2 * (1.0 + sc2)
        dw_pre = _sum_rows(dn2 * xh)
        dxh = dn2 * pw_ref[...]
        dx_ref[...] = dy + r1 * (dxh - xh * _mean_last(dxh * xh))
        _acc_rows(acc_ref, pl.program_id(0) % tps == 0,
                  [dsh2, dsc2, dg2, dw_pre, dw_post, jnp.broadcast_to(loss, (1, D_MODEL))])

    row = lambda w: pl.BlockSpec((TM, w), lambda i: (i, 0))
    vec = pl.BlockSpec((1, D_MODEL), lambda i: (0, 0))
    B = N // T
    return pl.pallas_call(
        body, name="mlp_bwd", grid=(N // TM,),
        in_specs=[row(D_MODEL), row(D_MODEL), row(D_FF), row(D_MODEL), _mod_spec(tps), vec, vec] + [ANY_SPEC] * 4,
        out_specs=[row(D_MODEL), row(D_FF), row(D_MODEL), _mod_spec(tps)],
        out_shape=[SDS((N, D_MODEL), F32), SDS((N, D_FF), BF16), SDS((N, D_MODEL), BF16),
                   SDS((B, 8, D_MODEL), F32)],
        scratch_shapes=[pltpu.VMEM((D_FF, D_MODEL), BF16), pltpu.VMEM((D_MODEL, D_FF), BF16),
                        pltpu.SemaphoreType.DMA((MLP_PIECES,))],
        compiler_params=_params(("arbitrary",), VMEM_LIMIT_BIG),
    )(x1, d, up, tgt, mod8, pre_w, post_w, *w_up_halves, *w_down_halves)


def _mix_bwd(mix, dx1, mod8, post_w, w_out_bf, T, ride_srcs, ride_modes):
    N = mix.shape[0]
    TM = _tile_rows(T, big=True)
    tps = T // TM
    nr = len(ride_srcs)

    def body(*refs):
        mix_ref, dx_ref, mod_ref, pw_ref, w_ref = refs[:5]
        ride_in = refs[5:5 + nr]
        dan_ref, drg_ref, dmix_ref, acc_ref = refs[5 + nr:9 + nr]
        ride_out = refs[9 + nr:9 + 2 * nr]
        sems = refs[9 + 2 * nr:]
        _ride_start(ride_modes, pl.program_id(0), N // TM, ride_in, ride_out, sems)
        g1 = mod_ref[2:3, :]
        mix = mix_ref[...]
        dx1 = dx_ref[...]
        rm = lax.rsqrt(_mean_last(mix * mix) + EPS)
        mh = mix * rm
        dg1 = _sum_rows(dx1 * (mh * pw_ref[...]))
        dr = dx1 * g1
        dw_post = _sum_rows(dr * mh)
        dmh = dr * pw_ref[...]
        dmix = _bf(rm * (dmh - mh * _mean_last(dmh * mh)))
        dmix_ref[...] = dmix
        dcat = _dot_nt(dmix, w_ref[...])
        dan_ref[...] = dcat[:, :ATT_WIDTH]
        drg_ref[...] = dcat[:, ATT_WIDTH:]
        _acc_rows(acc_ref, pl.program_id(0) % tps == 0, [dg1, dw_post])
        _ride_wait(ride_modes, pl.program_id(0), N // TM, ride_in, ride_out, sems)

    row = lambda w: pl.BlockSpec((TM, w), lambda i: (i, 0))
    B = N // T
    return pl.pallas_call(
        body, name="mix_bwd", grid=(N // TM,),
        in_specs=[row(D_MODEL), row(D_MODEL), _mod_spec(tps), pl.BlockSpec((1, D_MODEL), lambda i: (0, 0)),
                  pl.BlockSpec((D_MODEL, D_MODEL), lambda i: (0, 0))] + [ANY_SPEC] * nr,
        out_specs=[row(ATT_WIDTH), row(HG_WIDTH), row(D_MODEL), _mod_spec(tps)] + [ANY_SPEC] * nr,
        out_shape=[SDS((N, ATT_WIDTH), F32), SDS((N, HG_WIDTH), F32), SDS((N, D_MODEL), BF16),
                   SDS((B, 8, D_MODEL), F32)] + _exchange_shapes(ride_srcs, ride_modes),
        scratch_shapes=_exchange_sems(nr),
        compiler_params=_params(("arbitrary",), VMEM_LIMIT_BIG),
    )(mix, dx1, mod8, post_w, w_out_bf, *ride_srcs)


def _hgrn_bwd(proj_h, lb, hg_w, o, s_prev, drg, ride_srcs, ride_modes):
    B, T, _ = proj_h.shape
    ng = T // HG_ROWS
    nr = len(ride_srcs)

    def body(*refs):
        hq_ref, hf_ref, hi_ref, hg_ref, lb_ref, gw_ref, o_ref, sp_ref, drg_ref = refs[:9]
        ride_in = refs[9:9 + nr]
        dhq_ref, dhf_ref, dhi_ref, dhg_ref, dlb_ref, dgw_ref = refs[9 + nr:15 + nr]
        ride_out = refs[15 + nr:15 + 2 * nr]
        dst = refs[15 + 2 * nr]
        sems = refs[16 + 2 * nr:]
        step = pl.program_id(0) * ng + pl.program_id(1)
        _ride_start(ride_modes, step, B * ng, ride_in, ride_out, sems)

        @pl.when(pl.program_id(1) == 0)
        def _():
            dst[...] = jnp.zeros(dst.shape, F32)
            dlb_ref[...] = jnp.zeros(dlb_ref.shape, F32)
            dgw_ref[...] = jnp.zeros(dgw_ref.shape, F32)

        lo = _group_mask()
        gw = gw_ref[...]

        for h in range(HG_HEADS):
            lanes = _head_lanes(h)
            lbv = lb_ref[:, lanes]
            hq = hq_ref[:, lanes]
            gt = _hgrn_gates(hq, hf_ref[:, lanes], lbv)
            sq, sg, qdf, kdf, k2f, ebl = gt["sq"], gt["sg"], gt["qd"], gt["kd"], gt["k2"], gt["ebl"]
            v, qd, kd = _bf(hi_ref[:, lanes]), _bf(qdf), _bf(kdf)
            ov = o_ref[:, lanes]
            hg = hg_ref[:, lanes]
            shg = _sigmoid(hg)
            dr = drg_ref[:, lanes]
            ro = lax.rsqrt(_mean_last(ov * ov) + EPS)
            oh = ov * ro
            dhg_ref[:, lanes] = _bf(dr * (oh * gw) * (shg + hg * shg * (1.0 - shg)))
            drn = dr * (hg * shg)
            dgw_ref[...] += jnp.broadcast_to(_sum_rows(drn * oh), (8, LANES))
            doh = drn * gw
            do = _bf(ro * (doh - oh * _mean_last(doh * oh)))
            a = jnp.where(lo, _dot_nt(qd, kd), 0.0)
            da = _bf(jnp.where(lo, _dot_nt(do, v), 0.0))
            dv = _dot_tn(_bf(a), do)
            dqd = _dot(da, kd)
            dkd = _dot_tn(da, qd)
            sp = sp_ref[h]
            incr = _dot_tn(do, _spread(qd))
            ds = dst[h]
            after = [None] * HG_GROUP
            for c in reversed(range(HG_GROUP)):
                after[c] = ds
                ds = ds * ebl[c] + _lane_block(incr, c)
            dst[h] = ds
            dss = jnp.concatenate(after, axis=1)
            dssb = _bf(dss)
            dk2 = _pick(_dot(v, dssb))
            dhi_ref[:, lanes] = _bf(dv + _dot_nt(_spread(_bf(k2f)), dssb))
            dqd = dqd + _pick(_dot(do, _bf(sp)))
            debl = _sum_rows(dss * sp)
            k2g = dk2 * k2f
            db = dqd * qdf - dkd * kdf - k2g
            dk = dkd * gt["enb"] + dk2 * gt["e2"]
            dbl = _chunk_bcast([_lane_block(debl, c) * ebl[c] + _sum_rows(k2g[_chunk_rows(c), :])
                                for c in range(HG_GROUP)])
            dg = _chunk_cumsum(db, reverse=True) + dbl
            df = dg / gt["f"] - dk
            dhf_ref[:, lanes] = _bf(df * (1.0 - lbv) * sg * (1.0 - sg))
            dlb_ref[:, lanes] += jnp.broadcast_to(_sum_rows(df * (1.0 - sg)), (8, LANES))
            dhq_ref[:, lanes] = _bf((dqd * gt["eb"]) * (sq + hq * sq * (1.0 - sq)))
        _ride_wait(ride_modes, step, B * ng, ride_in, ride_out, sems)

    part = lambda j: pl.BlockSpec((None, HG_ROWS, HG_WIDTH), lambda b, g: (b, ng - 1 - g, j))
    return pl.pallas_call(
        body, name="hgrn_bwd", grid=(B, ng),
        in_specs=[part(0), part(1), part(2), part(3),
                  pl.BlockSpec((1, HG_WIDTH), lambda b, g: (0, 0)),
                  pl.BlockSpec((1, LANES), lambda b, g: (0, 0)),
                  part(0),
                  pl.BlockSpec((None, HG_HEADS, None, HG_HEAD_DIM, HG_STACK), lambda b, g: (b, 0, ng - 1 - g, 0, 0)),
                  part(0)] + [ANY_SPEC] * nr,
        out_specs=[part(0), part(0), part(0), part(0),
                   pl.BlockSpec((None, 8, HG_WIDTH), lambda b, g: (b, 0, 0)),
                   pl.BlockSpec((None, 8, LANES), lambda b, g: (b, 0, 0))] + [ANY_SPEC] * nr,
        out_shape=[SDS((B, T, HG_WIDTH), BF16)] * 4 + [SDS((B, 8, HG_WIDTH), F32), SDS((B, 8, LANES), F32)]
        + _exchange_shapes(ride_srcs, ride_modes),
        scratch_shapes=[pltpu.VMEM((HG_HEADS, HG_HEAD_DIM, HG_HEAD_DIM), F32)] + _exchange_sems(nr),
        compiler_params=_params(("arbitrary", "arbitrary"), VMEM_LIMIT_BIG),
    )(proj_h, proj_h, proj_h, proj_h, lb, hg_w, o, s_prev, drg, *ride_srcs)


def _attn_bwd(qr, kr, proj3, attn_o, dan, tables, sinks, attn_w, ride_srcs, ride_modes):
    B, T, _ = proj3.shape
    nb = T // WINDOW
    splits = min(ATT_SPLITS, nb)
    per = nb // splits
    nr = len(ride_srcs)
    cos, sinl, sinr = tables
    QKV = ATT_WIDTH + 2 * LANES

    def body(*refs):
        qr_ref, kr_ref, v_ref, o_ref, dan_ref, cos_ref, sl_ref, sr_ref, sink_ref, aw_ref = refs[:10]
        ride_in = refs[10:10 + nr]
        dqkv_ref, dsink_ref, daw_ref = refs[10 + nr:13 + nr]
        ride_out = refs[13 + nr:13 + 2 * nr]
        kpad, vpad, dkpad, dvpad, dqb, dsk = refs[13 + 2 * nr:19 + 2 * nr]
        sems = refs[19 + 2 * nr:]
        part = pl.program_id(1)
        step = pl.program_id(0) * splits + part
        _ride_start(ride_modes, step, B * splits, ride_in, ride_out, sems)

        @pl.when(part == 0)
        def _():
            kpad[0:WINDOW, :] = jnp.zeros((WINDOW, LANES), BF16)
            vpad[0:WINDOW, :] = jnp.zeros((WINDOW, LANES), BF16)
            kpad[WINDOW:, :] = kr_ref[...]
            vpad[WINDOW:, :] = _bf(v_ref[...])
            dkpad[...] = jnp.zeros(dkpad.shape, F32)
            dvpad[...] = jnp.zeros(dvpad.shape, F32)
            dsk[...] = jnp.zeros(dsk.shape, F32)
            daw_ref[...] = jnp.zeros(daw_ref.shape, F32)

        lower = _lower_mask()
        aw = aw_ref[...]

        def block(n, daw):
            r0 = pl.multiple_of(n * WINDOW, WINDOW)
            rows = pl.ds(r0, WINDOW)
            nxt = pl.ds(r0 + WINDOW, WINDOW)
            ob = o_ref[rows, :]
            dn = dan_ref[rows, :]
            ro = lax.rsqrt(_mean_last(ob * ob) + EPS)
            oh = ob * ro
            daw = daw + _sum_rows(dn * oh)
            doh = dn * aw
            do = _bf(ro * (doh - oh * _mean_last(doh * oh)))
            doparts = [do[:, j * LANES:(j + 1) * LANES] for j in range(ATT_WIDTH // LANES)]
            qparts = [qr_ref[rows, j * LANES:(j + 1) * LANES] for j in range(ATT_WIDTH // LANES)]
            for hk in range(ATT_KV_HEADS):
                lanes = slice(hk * ATT_HEAD_DIM, (hk + 1) * ATT_HEAD_DIM)
                qs = _stack_heads(qparts, hk)
                dos = _stack_heads(doparts, hk)
                k_cur, k_prev = kpad[nxt, lanes], kpad[rows, lanes]
                v_cur, v_prev = vpad[nxt, lanes], vpad[rows, lanes]
                p, inv, es = _softmax_window(qs, k_cur, k_prev, lower, n > 0, _sink_row(sink_ref, hk))
                p = p * inv
                dp = jnp.where(lower, _dot_nt(v_cur, dos), _dot_nt(v_prev, dos))
                delta = jnp.sum(p * dp, axis=0, keepdims=True)
                ds = p * (dp - delta)
                sk = (es * inv) * delta
                ds_cur = jnp.where(lower, ds, 0.0)
                p_cur = jnp.where(lower, p, 0.0)
                ds_cur, ds_prev = _bf(ds_cur), _bf(ds - ds_cur)
                p_cur, p_prev = _bf(p_cur), _bf(p - p_cur)
                dqt = (_dot_tn(k_cur, ds_cur) + _dot_tn(k_prev, ds_prev)) * ATT_SCALE
                dkpad[nxt, lanes] += _dot(ds_cur, qs)
                dkpad[rows, lanes] += _dot(ds_prev, qs)
                dvpad[nxt, lanes] += _dot(p_cur, dos)
                dvpad[rows, lanes] += _dot(p_prev, dos)
                for g in range(ATT_GROUP):
                    h = ATT_GROUP * hk + g
                    cols = slice(g * WINDOW, (g + 1) * WINDOW)
                    dqb[:, h * ATT_HEAD_DIM:(h + 1) * ATT_HEAD_DIM] = dqt[:, cols].T
                    head_lane = lax.broadcasted_iota(jnp.int32, dsk.shape, 1) == h
                    dsk[...] += jnp.where(head_lane, -jnp.sum(sk[:, cols], axis=1, keepdims=True), 0.0)
            cs, sl, sr = cos_ref[rows, :], sl_ref[rows, :], sr_ref[rows, :]
            for j in range(ATT_WIDTH // LANES):
                dqkv_ref[rows, j * LANES:(j + 1) * LANES] = _bf(_rope_t(dqb[:, j * LANES:(j + 1) * LANES], cs, sl, sr))
            return daw

        daw = _loop_pairs(part * per, per, block, jnp.zeros((1, ATT_WIDTH), F32))
        daw_ref[...] += jnp.broadcast_to(daw, (8, ATT_WIDTH))
        dsink_ref[...] = dsk[...]

        def finish(n, carry):
            r0 = pl.multiple_of(n * WINDOW, WINDOW)
            rows = pl.ds(r0, WINDOW)
            nxt = pl.ds(r0 + WINDOW, WINDOW)
            cs, sl, sr = cos_ref[rows, :], sl_ref[rows, :], sr_ref[rows, :]
            dqkv_ref[rows, ATT_WIDTH:ATT_WIDTH + LANES] = _bf(_rope_t(dkpad[nxt, :], cs, sl, sr))
            dqkv_ref[rows, ATT_WIDTH + LANES:QKV] = _bf(dvpad[nxt, :])
            return carry

        @pl.when(part == splits - 1)
        def _():
            lax.fori_loop(0, nb, finish, 0)

        _ride_wait(ride_modes, step, B * splits, ride_in, ride_out, sems)

    seq = lambda w, j: pl.BlockSpec((None, T, w), lambda b, s: (b, 0, j))
    full = lambda r, w: pl.BlockSpec((r, w), lambda b, s: (0, 0))
    return pl.pallas_call(
        body, name="attn_bwd", grid=(B, splits),
        in_specs=[seq(ATT_WIDTH, 0), seq(LANES, 0), seq(LANES, 5), seq(ATT_WIDTH, 0), seq(ATT_WIDTH, 0),
                  full(T, LANES), full(T, LANES), full(T, LANES),
                  pl.BlockSpec(memory_space=pltpu.SMEM), full(1, ATT_WIDTH)] + [ANY_SPEC] * nr,
        out_specs=[seq(QKV, 0), pl.BlockSpec((None, 8, LANES), lambda b, s: (b, 0, 0)),
                   pl.BlockSpec((None, 8, ATT_WIDTH), lambda b, s: (b, 0, 0))] + [ANY_SPEC] * nr,
        out_shape=[SDS((B, T, QKV), BF16), SDS((B, 8, LANES), F32), SDS((B, 8, ATT_WIDTH), F32)]
        + _exchange_shapes(ride_srcs, ride_modes),
        scratch_shapes=[pltpu.VMEM((T + WINDOW, LANES), BF16), pltpu.VMEM((T + WINDOW, LANES), BF16),
                        pltpu.VMEM((T + WINDOW, LANES), F32), pltpu.VMEM((T + WINDOW, LANES), F32),
                        pltpu.VMEM((WINDOW, ATT_WIDTH), F32), pltpu.VMEM((8, LANES), F32)] + _exchange_sems(nr),
        compiler_params=_params(("arbitrary", "arbitrary"), VMEM_LIMIT_BIG),
    )(qr, kr, proj3, attn_o, dan, cos, sinl, sinr, sinks, attn_w, *ride_srcs)


def _in_bwd(x2, dx1, dqkv, dhq, dhf, dhi, dhg, mod8, pre_w, w_in_bf, T, ride_srcs, ride_modes):
    N = x2.shape[0]
    TM = _tile_rows(T, big=True)
    tps = T // TM
    nr = len(ride_srcs)
    pieces = [(0, ATT_WIDTH + 2 * LANES), (768, HG_WIDTH), (1280, HG_WIDTH), (1792, HG_WIDTH), (2304, HG_WIDTH)]

    def body(*refs):
        x_ref, dx_ref, p0, p1, p2, p3, p4, mod_ref, pw_ref, w_ref = refs[:10]
        ride_in = refs[10:10 + nr]
        gx_ref, dproj_ref, acc_ref = refs[10 + nr:13 + nr]
        ride_out = refs[13 + nr:13 + 2 * nr]
        sems = refs[13 + 2 * nr:]
        _ride_start(ride_modes, pl.program_id(0), N // TM, ride_in, ride_out, sems)
        sc1 = mod_ref[1:2, :]
        dh = jnp.zeros((TM, D_MODEL), F32)
        for ref, (off, width) in zip((p0, p1, p2, p3, p4), pieces):
            pb = ref[...]
            dproj_ref[:, off:off + width] = pb
            dh = dh + _dot(pb, w_ref[off:off + width, :])
        x = x_ref[...]
        r = lax.rsqrt(_mean_last(x * x) + EPS)
        xh = x * r
        n1 = xh * pw_ref[...]
        dsh1 = _sum_rows(dh)
        dsc1 = _sum_rows(dh * n1)
        dn1 = dh * (1.0 + sc1)
        dw_pre = _sum_rows(dn1 * xh)
        dxh = dn1 * pw_ref[...]
        gx_ref[...] = dx_ref[...] + r * (dxh - xh * _mean_last(dxh * xh))
        _acc_rows(acc_ref, pl.program_id(0) % tps == 0, [dsh1, dsc1, dw_pre])
        _ride_wait(ride_modes, pl.program_id(0), N // TM, ride_in, ride_out, sems)

    row = lambda w: pl.BlockSpec((TM, w), lambda i: (i, 0))
    B = N // T
    return pl.pallas_call(
        body, name="in_bwd", grid=(N // TM,),
        in_specs=[row(D_MODEL), row(D_MODEL), row(768), row(HG_WIDTH), row(HG_WIDTH), row(HG_WIDTH),
                  row(HG_WIDTH), _mod_spec(tps), pl.BlockSpec((1, D_MODEL), lambda i: (0, 0)),
                  pl.BlockSpec((IN_COLS, D_MODEL), lambda i: (0, 0))] + [ANY_SPEC] * nr,
        out_specs=[row(D_MODEL), row(IN_COLS), _mod_spec(tps)] + [ANY_SPEC] * nr,
        out_shape=[SDS((N, D_MODEL), F32), SDS((N, IN_COLS), BF16), SDS((B, 8, D_MODEL), F32)]
        + _exchange_shapes(ride_srcs, ride_modes),
        scratch_shapes=_exchange_sems(nr),
        compiler_params=_params(("arbitrary",), VMEM_LIMIT_BIG),
    )(x2, dx1, dqkv, dhq, dhf, dhi, dhg, mod8, pre_w, w_in_bf, *ride_srcs)


def _matmul_tn(name, a, b, tn, tm=512, by_owner_cols=False):
    K, M = a.shape
    Nc = b.shape[1]
    tm = min(tm, M)

    def body(a_ref, b_ref, o_ref):
        o_ref[...] = _bf(_dot_tn(a_ref[...], b_ref[...]))

    if by_owner_cols:
        assert tn * N_DEV == Nc
        out_shape = SDS((N_DEV, M, tn), BF16)
        out_spec = pl.BlockSpec((None, tm, tn), lambda i, j: (j, i, 0))
    else:
        out_shape = SDS((M, Nc), BF16)
        out_spec = pl.BlockSpec((tm, tn), lambda i, j: (i, j))
    return pl.pallas_call(
        body, name=name, grid=(M // tm, Nc // tn),
        in_specs=[pl.BlockSpec((K, tm), lambda i, j: (0, i)),
                  pl.BlockSpec((K, tn), lambda i, j: (0, j))],
        out_specs=out_spec, out_shape=out_shape,
        compiler_params=_params(("arbitrary", "arbitrary"), VMEM_LIMIT_BIG),
    )(a, b)


def _adamw_math(w, g, m, v):
    m2 = ADAM_B1 * m + (1.0 - ADAM_B1) * g
    v2 = ADAM_B2 * v + (1.0 - ADAM_B2) * (g * g)
    m_hat = m2 / (1.0 - ADAM_B1 ** ADAM_STEP)
    v_hat = v2 / (1.0 - ADAM_B2 ** ADAM_STEP)
    delta = -ADAM_LR * (m_hat / (jnp.sqrt(v_hat) + ADAM_EPS) + ADAM_WD * w)
    return delta, m2, v2


def _pair_add(name, gw, theirs):
    chips, _, r, c = gw.shape
    tr = r
    core = lax.axis_index("c").astype(jnp.int32).reshape(1)

    def body(core_ref, mine_ref, theirs_ref, o_ref):
        o_ref[...] = _bf(mine_ref[...].astype(F32) + theirs_ref[...].astype(F32))

    block = pl.BlockSpec((None, tr, c), lambda s, i, core_ref: (s, i, 0))
    grid_spec = pltpu.PrefetchScalarGridSpec(
        num_scalar_prefetch=1, grid=(chips, r // tr),
        in_specs=[pl.BlockSpec((None, None, tr, c), lambda s, i, core_ref: (s, core_ref[0], i, 0)), block],
        out_specs=block)
    return pl.pallas_call(
        body, name=name, grid_spec=grid_spec, out_shape=SDS((chips, r, c), BF16),
        compiler_params=_params(("arbitrary", "arbitrary")),
    )(core, gw, theirs)


def _reduce_adamw(name, parts, w, m, v):
    r, c = w.shape
    tr = r if r % 256 else 256
    slots = parts.shape[0]

    def body(p_ref, w_ref, m_ref, v_ref, g_ref, d_ref, m2_ref, v2_ref):
        g = p_ref[0].astype(F32)
        for s in range(1, slots):
            g = g + p_ref[s].astype(F32)
        g_ref[...] = g
        d_ref[...], m2_ref[...], v2_ref[...] = _adamw_math(w_ref[...], g, m_ref[...], v_ref[...])

    blk = pl.BlockSpec((tr, c), lambda i: (i, 0))
    return pl.pallas_call(
        body, name=name, grid=(r // tr,),
        in_specs=[pl.BlockSpec((slots, tr, c), lambda i: (0, i, 0)), blk, blk, blk],
        out_specs=[blk] * 4, out_shape=[SDS((r, c), F32)] * 4,
        compiler_params=_params(("arbitrary",), VMEM_LIMIT_BIG),
    )(parts, w, m, v)


def _ada_grad_adamw(c_all, dmod_all, w, m, v):
    r, c = w.shape
    tr = 256
    nb = c_all.shape[0]

    def body(c_ref, dm_ref, w_ref, m_ref, v_ref, g_ref, d_ref, m2_ref, v2_ref):
        cv = c_ref[...]
        g = _dot_tn(cv * _sigmoid(cv), dm_ref[...])
        g_ref[...] = g
        d_ref[...], m2_ref[...], v2_ref[...] = _adamw_math(w_ref[...], g, m_ref[...], v_ref[...])

    blk = pl.BlockSpec((tr, c), lambda i: (i, 0))
    return pl.pallas_call(
        body, name="ada_grad_adamw", grid=(r // tr,),
        in_specs=[pl.BlockSpec((nb, tr), lambda i: (0, i)), pl.BlockSpec((nb, c), lambda i: (0, 0)),
                  blk, blk, blk],
        out_specs=[blk] * 4, out_shape=[SDS((r, c), F32)] * 4,
        compiler_params=_params(("arbitrary",)),
    )(c_all, dmod_all, w, m, v)


_SMALL = [("b_ada", 6144), ("pre_w_mix", 1024), ("attn_sinks", 128), ("attn_out_w", 512), ("lb_table", 1024),
          ("hg_norm_w", 128), ("post_w_mix", 1024), ("pre_w_mlp", 1024), ("post_w_mlp", 1024)]


def _pack_small(acc_in, acc_mix, acc_mlp, dsink, daw, dlb, dgw, lb_p, ada_cols):
    B = acc_in.shape[0]
    width = sum(w for _, w in _SMALL) + LANES

    def body(ain, amix, amlp, dsk_ref, daw_ref, dlb_ref, dgw_ref, lbp_ref, packed_ref, dmod_ref):
        def total(ref, r, w=None):
            out = ref[0, r:r + 1, :] if w is None else ref[0, r:r + 1, :w]
            for b in range(1, B):
                out = out + (ref[b, r:r + 1, :] if w is None else ref[b, r:r + 1, :w])
            return out

        d_b_ada = None
        for b in range(B):
            mods = [ain[b, 0:1, :], ain[b, 1:2, :], amix[b, 0:1, :], amlp[b, 0:1, :], amlp[b, 1:2, :], amlp[b, 2:3, :]]
            full = jnp.concatenate(mods, axis=1)
            for j in range(N_DEV):
                dmod_ref[j, b:b + 1, :] = full[:, j * ada_cols:(j + 1) * ada_cols]
            d_b_ada = full if d_b_ada is None else d_b_ada + full
        d_lb = total(dlb_ref, 0)
        pp = lbp_ref[0:1, :] * lbp_ref[1:2, :]
        pieces = [d_b_ada, total(ain, 2), total(dsk_ref, 0), total(daw_ref, 0), -d_lb * pp, d_lb * pp,
                  total(dgw_ref, 0), total(amix, 1), total(amlp, 3), total(amlp, 4), total(amlp, 5, LANES)]
        off = 0
        for piece in pieces:
            packed_ref[:, off:off + piece.shape[1]] = piece
            off += piece.shape[1]

    return pl.pallas_call(
        body, name="pack_small",
        out_shape=[SDS((1, width), F32), SDS((N_DEV, B, ada_cols), F32)],
    )(acc_in, acc_mix, acc_mlp, dsink, daw, dlb, dgw, lb_p)


def _adamw_small(parts, given):
    names = [n for n, _ in _SMALL]
    flat_in = [a for n in names for a in given[n]]

    def body(*refs):
        p_ref = refs[0]
        in_refs = refs[1:1 + 3 * len(names)]
        out_refs = refs[1 + 3 * len(names):-1]
        loss_ref = refs[-1]
        g = p_ref[0]
        for s in range(1, N_DEV):
            g = g + p_ref[s]
        off = 0
        for i, (name, width) in enumerate(_SMALL):
            w_ref, m_ref, v_ref = in_refs[3 * i:3 * i + 3]
            rows, cols = w_ref.shape
            for r in range(rows):
                gr = g[:, off + r * cols:off + (r + 1) * cols]
                res = (gr,) + _adamw_math(w_ref[r:r + 1, :], gr, m_ref[r:r + 1, :], v_ref[r:r + 1, :])
                for o_ref, val in zip(out_refs[4 * i:4 * i + 4], res):
                    o_ref[r:r + 1, :] = val
            off += width
        loss_ref[...] = g[:, off:off + LANES]

    out_shape = [SDS(given[n][0].shape, F32) for n in names for _ in range(4)] + [SDS((1, LANES), F32)]
    outs = pl.pallas_call(body, name="adamw_small", out_shape=out_shape)(parts, *flat_in)
    return {n: tuple(outs[4 * i:4 * i + 4]) for i, n in enumerate(names)}, outs[-1][0, 0]


def kernel(x, c, w_ada, b_ada, pre_w_mix, w_in, attn_sinks, attn_out_w, lb_table, hg_norm_w, w_out, post_w_mix, pre_w_mlp, w_up, w_down, post_w_mlp, loss_target, m_w_ada, m_b_ada, m_pre_w_mix, m_w_in, m_attn_sinks, m_attn_out_w, m_lb_table, m_hg_norm_w, m_w_out, m_post_w_mix, m_pre_w_mlp, m_w_up, m_w_down, m_post_w_mlp, v_w_ada, v_b_ada, v_pre_w_mix, v_w_in, v_attn_sinks, v_attn_out_w, v_lb_table, v_hg_norm_w, v_w_out, v_post_w_mix, v_pre_w_mlp, v_w_up, v_w_down, v_post_w_mlp):
    B, T, _ = x.shape
    N = B * T
    me = 4 * lax.axis_index("x") + 2 * lax.axis_index("y") + lax.axis_index("c")
    x2 = x.reshape(N, D_MODEL)
    tgt2 = loss_target.reshape(N, D_MODEL)

    w_in_t, m_w_in_t, v_w_in_t = w_in[0].T, m_w_in[0].T, v_w_in[0].T
    w_in_g, c_g = _exchange("gather_w_in", [_bf(w_in_t), c], ["gather"] * 2)
    w_in_f = w_in_g.reshape(IN_COLS, D_MODEL)
    c_all = c_g.reshape(N_DEV * B, D_MODEL)

    ada_cols = w_ada.shape[2]
    b_mine = lax.dynamic_slice(b_ada, (0, me * ada_cols), (1, ada_cols))
    mod_cols = _ada_mod(c_all, w_ada[0], b_mine)
    (mod_g,) = _exchange("scatter_mod", [mod_cols.reshape(N_DEV, B, ada_cols)], ["a2a"])
    mod = mod_g.transpose(1, 0, 2).reshape(B, 6, D_MODEL)
    mod8 = jnp.pad(mod, ((0, 0), (0, 2), (0, 0)))

    lb_p = jax.nn.softmax(lb_table, axis=0)
    lb = lb_p[1:2]
    tables = _rope_tables(T)

    w_up_b, w_down_b = _bf(w_up[0]), _bf(w_down[0])
    proj_a, proj_h, h1, w_out_g, w_up_g0 = _in_proj(x2, mod8, pre_w_mix, w_in_f, T,
                                                    [_bf(w_out[0]), w_up_b[:MLP_HALF]], ["gather"] * 2)
    proj3 = proj_a.reshape(B, T, ATT_COLS)
    proj_h = proj_h.reshape(B, T, IN_COLS - ATT_COLS)
    rec_o, rec_g, s_prev, w_up_g1 = _hgrn_fwd(proj_h, lb, hg_norm_w, [w_up_b[MLP_HALF:]], ["gather"])
    attn_o, attn_n, qr, kr, w_down_g0 = _attn_fwd(proj3, tables, attn_sinks, attn_out_w,
                                                  [w_down_b[:, :MLP_HALF]], ["gather"])
    w_out_f = w_out_g.reshape(D_MODEL, D_MODEL)
    mix, x1, cat, w_down_g1 = _mix_out(x2, attn_n.reshape(N, ATT_WIDTH), rec_g.reshape(N, HG_WIDTH), mod8,
                                       post_w_mix, w_out_f, T, [w_down_b[:, MLP_HALF:]], ["gather"])
    w_up_halves = [w_up_g0, w_up_g1]
    w_down_halves = [w_down_g0.reshape(D_FF, MLP_HALF), w_down_g1.reshape(D_FF, MLP_HALF)]
    up, u, d, h2 = _mlp_fwd(x1, mod8, pre_w_mlp, w_up_halves, w_down_halves, T)

    dx1, dup, dd, acc_mlp = _mlp_bwd(x1, d, up, tgt2, mod8, pre_w_mlp, post_w_mlp, w_up_halves, w_down_halves, T)
    chips = N_DEV // 2
    by_chip = lambda a: a.reshape((chips, 2, a.shape[0] // N_DEV) + a.shape[1:])
    gw_up = _matmul_tn("grad_w_up", h2, dup, D_FF // N_DEV, by_owner_cols=True)
    gw_up = gw_up.reshape(chips, 2, D_MODEL, D_FF // N_DEV)
    gw_down = by_chip(_matmul_tn("grad_w_down", u, dd, 512))
    dan, drg, dmix, acc_mix, q_down, q_up = _mix_bwd(mix, dx1, mod8, post_w_mix, w_out_f, T,
                                                     [gw_down, gw_up], ["pair"] * 2)
    p_down, p_up = _pair_add("pair_add_w_down", gw_down, q_down), _pair_add("pair_add_w_up", gw_up, q_up)
    gw_out = _matmul_tn("grad_w_out", cat, dmix, 512).reshape(N_DEV, D_MODEL // N_DEV, D_MODEL)
    dhq, dhf, dhi, dhg, dlb_p, dgw_p, r_down, r_up = _hgrn_bwd(
        proj_h, lb, hg_norm_w, rec_o, s_prev, drg.reshape(B, T, HG_WIDTH), [p_down, p_up], ["chips"] * 2)
    dqkv, dsink_p, daw_p, r_out = _attn_bwd(qr, kr, proj3, attn_o, dan.reshape(B, T, ATT_WIDTH), tables,
                                            attn_sinks, attn_out_w, [gw_out], ["a2a"])
    flat = lambda a: a.reshape(N, a.shape[-1])
    grad_x, dproj, acc_in = _in_bwd(x2, dx1, flat(dqkv), flat(dhq), flat(dhf), flat(dhi), flat(dhg),
                                    mod8, pre_w_mix, w_in_f, T, [], [])

    gw_in = by_chip(_matmul_tn("grad_w_in", dproj, h1, 512, tm=IN_COLS // 2))
    (q_in,) = _exchange("pair_w_in", [gw_in], ["pair"])
    p_in = _pair_add("pair_add_w_in", gw_in, q_in)

    packed, dmod_blocks = _pack_small(acc_in, acc_mix, acc_mlp, dsink_p, daw_p, dlb_p, dgw_p, lb_p, ada_cols)
    r_in, r_dmod, r_small = _exchange("reduce_grads", [p_in, dmod_blocks, packed], ["chips", "a2a", "gather"])

    res = {}
    res["w_in"] = tuple(a.T for a in _reduce_adamw("adamw_w_in", r_in, w_in_t, m_w_in_t, v_w_in_t))
    res["w_out"] = _reduce_adamw("adamw_w_out", r_out, w_out[0], m_w_out[0], v_w_out[0])
    res["w_up"] = _reduce_adamw("adamw_w_up", r_up, w_up[0], m_w_up[0], v_w_up[0])
    res["w_down"] = _reduce_adamw("adamw_w_down", r_down, w_down[0], m_w_down[0], v_w_down[0])
    res["w_ada"] = _ada_grad_adamw(c_all, r_dmod.reshape(N_DEV * B, ada_cols), w_ada[0], m_w_ada[0], v_w_ada[0])

    given = dict(b_ada=(b_ada, m_b_ada, v_b_ada), pre_w_mix=(pre_w_mix, m_pre_w_mix, v_pre_w_mix),
                 attn_sinks=(attn_sinks, m_attn_sinks, v_attn_sinks),
                 attn_out_w=(attn_out_w, m_attn_out_w, v_attn_out_w), lb_table=(lb_table, m_lb_table, v_lb_table),
                 hg_norm_w=(hg_norm_w, m_hg_norm_w, v_hg_norm_w), post_w_mix=(post_w_mix, m_post_w_mix, v_post_w_mix),
                 pre_w_mlp=(pre_w_mlp, m_pre_w_mlp, v_pre_w_mlp), post_w_mlp=(post_w_mlp, m_post_w_mlp, v_post_w_mlp))
    small_res, loss = _adamw_small(r_small, given)
    res.update(small_res)

    order = ["w_ada", "b_ada", "pre_w_mix", "w_in", "attn_sinks", "attn_out_w", "lb_table", "hg_norm_w", "w_out",
             "post_w_mix", "pre_w_mlp", "w_up", "w_down", "post_w_mlp"]
    big = {"w_ada", "w_in", "w_out", "w_up", "w_down"}
    outs = [loss, grad_x.reshape(B, T, D_MODEL)]
    for i in range(4):
        for k in order:
            a = res[k][i]
            outs.append(a[None] if k in big else a)
    return tuple(outs)
```

```python
import jax
import jax.numpy as jnp
import numpy as np
from jax import lax
from jax.experimental import pallas as pl
from jax.experimental.pallas import tpu as pltpu

F32 = jnp.float32
BF16 = jnp.bfloat16
SDS = jax.ShapeDtypeStruct

D_MODEL = 1024
ATT_WIDTH = 512
ATT_HEAD_DIM = 64
ATT_KV_HEADS = 2
ATT_GROUP = 4
WINDOW = 128
ROPE_DIM = 16
ROPE_THETA = 500000.0
HG_WIDTH = 512
HG_HEAD_DIM = 128
HG_HEADS = 4
HG_CHUNK = 32
IN_COLS = 2816
ATT_COLS = 768
D_FF = 4096
EPS = 1e-6
N_DEV = 8

ADAM_LR = 0.001
ADAM_B1 = 0.9
ADAM_B2 = 0.999
ADAM_EPS = 1e-08
ADAM_WD = 0.01
ADAM_STEP = 10

VMEM_LIMIT_BIG = 56 << 20
LANES = 128

MESH = pl.DeviceIdType.MESH
NT_DIMS = (((1,), (1,)), ((), ()))
TN_DIMS = (((0,), (0,)), ((), ()))


def _dot(a, b):
    return jnp.dot(a, b, preferred_element_type=F32)


def _dot_nt(a, b):
    return lax.dot_general(a, b, NT_DIMS, preferred_element_type=F32)


def _dot_tn(a, b):
    return lax.dot_general(a, b, TN_DIMS, preferred_element_type=F32)


def _bf(a):
    return a.astype(BF16)


def _sigmoid(a):
    return 0.5 * jnp.tanh(0.5 * a) + 0.5


def _mean_last(a):
    return jnp.mean(a, axis=-1, keepdims=True)


def _sum_rows(a):
    return jnp.sum(a, axis=0, keepdims=True)


def _loop_pairs(first, count, body, init, per_trip=2):
    if count % per_trip:
        return lax.fori_loop(first, first + count, body, init)

    def trip(i, c):
        for k in range(per_trip):
            c = body(first + per_trip * i + k, c)
        return c

    return lax.fori_loop(0, count // per_trip, trip, init)


def _params(sem=None, vmem=None):
    kw = {}
    if sem is not None:
        kw["dimension_semantics"] = sem
    if vmem is not None:
        kw["vmem_limit_bytes"] = vmem
    return pltpu.CompilerParams(**kw)


ANY_SPEC = pl.BlockSpec(memory_space=pl.ANY)


def _exchange_shapes(srcs, modes):
    out_shape = []
    for s, m in zip(srcs, modes):
        shp = {"gather": (N_DEV,) + tuple(s.shape), "pair": (s.shape[0],) + tuple(s.shape[2:])}.get(m, tuple(s.shape))
        out_shape.append(SDS(shp, s.dtype))
    return out_shape


def _exchange_sems(n):
    if n == 0:
        return []
    return [pltpu.SemaphoreType.DMA((n, N_DEV - 1)), pltpu.SemaphoreType.DMA((n, N_DEV - 1)),
            pltpu.SemaphoreType.DMA((n,))]


SIBLING = 1
OTHER_CHIPS = (2, 4, 6)


def _related(k):
    x, y, c = lax.axis_index("x"), lax.axis_index("y"), lax.axis_index("c")
    px, py, pc = x ^ ((k >> 2) & 1), y ^ ((k >> 1) & 1), c ^ (k & 1)
    return (px, py, pc), 4 * px + 2 * py + pc


def _exchange_phases(modes, src_refs, out_refs, send_sems, recv_sems, own_sems):
    _, me = _related(0)
    sib_dev, sib = _related(SIBLING)
    start, middle, end = [], [], []

    def remote(a, i, src, dst, dev):
        return pltpu.make_async_remote_copy(src_ref=src, dst_ref=dst, send_sem=send_sems.at[a, i],
                                            recv_sem=recv_sems.at[a, i], device_id=dev, device_id_type=MESH)

    for a, mode in enumerate(modes):
        out = out_refs[a]
        if mode == "gather":
            src = src_refs[a]
            own = pltpu.make_async_copy(src, out.at[me], own_sems.at[a])
            to_sib = remote(a, 0, src, out.at[me], sib_dev)
            start += [own.start, to_sib.start]
            end += [remote(a, 0, src, out.at[sib], sib_dev).wait_recv, to_sib.wait_send, own.wait]
            for j, k in enumerate(OTHER_CHIPS, start=1):
                dev, peer = _related(k)
                _, peer_sib = _related(k ^ SIBLING)
                send = remote(a, j, src, out.at[me], dev)
                passed = remote(a, 3 + j, out.at[peer], out.at[peer], sib_dev)
                start.append(send.start)
                middle += [remote(a, j, src, out.at[peer], dev).wait_recv, passed.start]
                end += [remote(a, 3 + j, out.at[peer_sib], out.at[peer_sib], sib_dev).wait_recv,
                        send.wait_send, passed.wait_send]
        elif mode == "pair":
            core = lax.axis_index("c")
            for s in range(N_DEV // 2):
                send = remote(a, s, src_refs[a].at[s, 1 - core], out.at[s], sib_dev)
                start.append(send.start)
                end += [remote(a, s, src_refs[a].at[s, 1 - core], out.at[s], sib_dev).wait_recv, send.wait_send]
        elif mode == "chips":
            chip = me // 2
            own = pltpu.make_async_copy(src_refs[a].at[chip], out.at[chip], own_sems.at[a])
            start.append(own.start)
            end.append(own.wait)
            for j, k in enumerate(OTHER_CHIPS, start=1):
                dev, peer = _related(k)
                send = remote(a, j, src_refs[a].at[peer // 2], out.at[chip], dev)
                start.append(send.start)
                end += [remote(a, j, src_refs[a].at[peer // 2], out.at[peer // 2], dev).wait_recv, send.wait_send]
        else:
            own = pltpu.make_async_copy(src_refs[a].at[me], out.at[me], own_sems.at[a])
            start.append(own.start)
            end.append(own.wait)
            for k in range(1, N_DEV):
                dev, peer = _related(k)
                send = remote(a, k - 1, src_refs[a].at[peer], out.at[me], dev)
                start.append(send.start)
                end += [remote(a, k - 1, src_refs[a].at[peer], out.at[peer], dev).wait_recv, send.wait_send]
    return start, middle, end


def _run(actions):
    for act in actions:
        act()


def _exchange(name, srcs, modes):
    n = len(srcs)

    def body(*refs):
        start, middle, end = _exchange_phases(modes, refs[:n], refs[n:2 * n], *refs[2 * n:])
        _run(start)
        _run(middle)
        _run(end)

    return pl.pallas_call(
        body, name=name, out_shape=_exchange_shapes(srcs, modes),
        in_specs=[ANY_SPEC] * n, out_specs=[ANY_SPEC] * n,
        scratch_shapes=_exchange_sems(n),
    )(*srcs)


def _ride_start(modes, step, steps, src_refs, out_refs, sems):
    if not modes:
        return
    middle_step = steps - 1

    @pl.when(step == 0)
    def _():
        _run(_exchange_phases(modes, src_refs, out_refs, *sems)[0])

    if "gather" in modes:
        @pl.when(step == middle_step)
        def _():
            _run(_exchange_phases(modes, src_refs, out_refs, *sems)[1])


def _ride_wait(modes, step, steps, src_refs, out_refs, sems):
    if not modes:
        return

    @pl.when(step == steps - 1)
    def _():
        _run(_exchange_phases(modes, src_refs, out_refs, *sems)[2])


def _ada_mod(c_all, w_ada, b_ada_mine):
    nb, cols = c_all.shape[0], w_ada.shape[1]

    def body(c_ref, w_ref, b_ref, o_ref):
        cv = c_ref[...]
        ca = cv * _sigmoid(cv)
        o_ref[...] = _dot(ca, w_ref[...]) + b_ref[...]

    return pl.pallas_call(body, name="ada_mod", out_shape=SDS((nb, cols), F32))(c_all, w_ada, b_ada_mine)


def _tile_rows(T, big=False):
    return min(512 if big else 256, T)


def _mod_spec(tps):
    return pl.BlockSpec((None, 8, D_MODEL), lambda i: (i // tps, 0, 0))


def _in_proj(x2, mod8, pre_w, w_in_bf, T, ride_srcs, ride_modes):
    N = x2.shape[0]
    TM = _tile_rows(T, big=True)
    tps = T // TM
    nr = len(ride_srcs)

    def body(*refs):
        x_ref, mod_ref, pw_ref, w_ref = refs[:4]
        ride_in = refs[4:4 + nr]
        pa_ref, ph_ref, h1_ref = refs[4 + nr:7 + nr]
        ride_out = refs[7 + nr:7 + 2 * nr]
        sems = refs[7 + 2 * nr:]
        _ride_start(ride_modes, pl.program_id(0), N // TM, ride_in, ride_out, sems)
        x = x_ref[...]
        r = lax.rsqrt(_mean_last(x * x) + EPS)
        h = (x * r * pw_ref[...]) * (1.0 + mod_ref[1:2, :]) + mod_ref[0:1, :]
        hb = _bf(h)
        h1_ref[...] = hb
        pa_ref[...] = _dot_nt(hb, w_ref[:ATT_COLS, :])
        ph_ref[...] = _dot_nt(hb, w_ref[ATT_COLS:, :])
        _ride_wait(ride_modes, pl.program_id(0), N // TM, ride_in, ride_out, sems)

    return pl.pallas_call(
        body, name="in_proj", grid=(N // TM,),
        in_specs=[pl.BlockSpec((TM, D_MODEL), lambda i: (i, 0)), _mod_spec(tps),
                  pl.BlockSpec((1, D_MODEL), lambda i: (0, 0)),
                  pl.BlockSpec((IN_COLS, D_MODEL), lambda i: (0, 0))] + [ANY_SPEC] * nr,
        out_specs=[pl.BlockSpec((TM, ATT_COLS), lambda i: (i, 0)),
                   pl.BlockSpec((TM, IN_COLS - ATT_COLS), lambda i: (i, 0)),
                   pl.BlockSpec((TM, D_MODEL), lambda i: (i, 0))] + [ANY_SPEC] * nr,
        out_shape=[SDS((N, ATT_COLS), F32), SDS((N, IN_COLS - ATT_COLS), F32), SDS((N, D_MODEL), BF16)]
        + _exchange_shapes(ride_srcs, ride_modes),
        scratch_shapes=_exchange_sems(nr),
        compiler_params=_params(("arbitrary",), VMEM_LIMIT_BIG),
    )(x2, mod8, pre_w, w_in_bf, *ride_srcs)


def _rope_tables(T):
    half = ROPE_DIM // 2
    f32 = np.float32
    inv_freq = (f32(ROPE_THETA) ** (-np.arange(0, ROPE_DIM, 2, dtype=f32) / f32(ROPE_DIM))).astype(f32)
    ang = np.arange(T, dtype=f32)[:, None] * inv_freq[None, :]
    cos, sin = np.cos(ang).astype(f32), np.sin(ang).astype(f32)
    ones = np.ones((T, ATT_HEAD_DIM - ROPE_DIM), f32)
    zeros = np.zeros((T, ATT_HEAD_DIM - ROPE_DIM), f32)
    zh = np.zeros((T, half), f32)
    cos64 = np.concatenate([cos, cos, ones], axis=1)
    sin_left = np.concatenate([-sin, zh, zeros], axis=1)
    sin_right = np.concatenate([zh, sin, zeros], axis=1)
    rep = LANES // ATT_HEAD_DIM
    return tuple(jnp.asarray(np.tile(t, (1, rep))) for t in (cos64, sin_left, sin_right))


def _rope(xc, cs, sl, sr):
    return xc * cs + pltpu.roll(xc, LANES - 8, 1) * sl + pltpu.roll(xc, 8, 1) * sr


def _rope_t(dy, cs, sl, sr):
    return dy * cs + pltpu.roll(dy * sl, 8, 1) + pltpu.roll(dy * sr, LANES - 8, 1)


ATT_SCALE = ATT_HEAD_DIM ** -0.5
ATT_SPLITS = 4


def _lower_mask():
    j = lax.broadcasted_iota(jnp.int32, (WINDOW, ATT_GROUP * WINDOW), 0)
    i = lax.broadcasted_iota(jnp.int32, (WINDOW, ATT_GROUP * WINDOW), 1) & (WINDOW - 1)
    return j <= i


def _sink_row(sink_ref, hk):
    return jnp.concatenate(
        [jnp.full((1, WINDOW), sink_ref[0, ATT_GROUP * hk + g], F32) for g in range(ATT_GROUP)], axis=1)


def _softmax_window(qs, k_cur, k_prev, lower, has_prev, sink):
    s_prev = jnp.where(has_prev, _dot_nt(k_prev, qs), jnp.finfo(F32).min)
    s = jnp.where(lower, _dot_nt(k_cur, qs), s_prev)
    m = jnp.maximum(jnp.max(s, axis=0, keepdims=True), sink)
    p = jnp.exp(s - m)
    es = jnp.exp(sink - m)
    inv = 1.0 / (jnp.sum(p, axis=0, keepdims=True) + es)
    return p, inv, es


def _stack_heads(parts, hk):
    hs = []
    for g in range(ATT_GROUP):
        h = ATT_GROUP * hk + g
        hs.append(parts[h // 2][:, (h % 2) * ATT_HEAD_DIM:(h % 2 + 1) * ATT_HEAD_DIM])
    return jnp.concatenate(hs, axis=0)


def _attn_fwd(proj3, tables, sinks, attn_w, ride_srcs, ride_modes):
    B, T, _ = proj3.shape
    nb = T // WINDOW
    splits = min(ATT_SPLITS, nb)
    per = nb // splits
    nr = len(ride_srcs)
    cos, sinl, sinr = tables

    def body(*refs):
        q_ref, k_ref, v_ref, cos_ref, sl_ref, sr_ref, sink_ref, aw_ref = refs[:8]
        ride_in = refs[8:8 + nr]
        o_ref, an_ref, qr_ref, kr_ref = refs[8 + nr:12 + nr]
        ride_out = refs[12 + nr:12 + 2 * nr]
        kpad, vpad = refs[12 + 2 * nr:14 + 2 * nr]
        sems = refs[14 + 2 * nr:]
        part = pl.program_id(1)
        step = pl.program_id(0) * splits + part
        _ride_start(ride_modes, step, B * splits, ride_in, ride_out, sems)

        @pl.when(part == 0)
        def _():
            kpad[0:WINDOW, :] = jnp.zeros((WINDOW, LANES), BF16)
            vpad[0:WINDOW, :] = jnp.zeros((WINDOW, LANES), BF16)

        lower = _lower_mask()

        def block(n, carry):
            r0 = pl.multiple_of(n * WINDOW, WINDOW)
            rows = pl.ds(r0, WINDOW)
            nxt = pl.ds(r0 + WINDOW, WINDOW)
            cs, sl, sr = cos_ref[rows, :], sl_ref[rows, :], sr_ref[rows, :]
            kb = _bf(_rope(k_ref[rows, :], cs, sl, sr))
            vb = _bf(v_ref[rows, :])
            kpad[nxt, :] = kb
            kr_ref[rows, :] = kb
            vpad[nxt, :] = vb
            qparts = []
            for j in range(ATT_WIDTH // LANES):
                qp = _bf(_rope(q_ref[rows, j * LANES:(j + 1) * LANES], cs, sl, sr) * ATT_SCALE)
                qr_ref[rows, j * LANES:(j + 1) * LANES] = qp
                qparts.append(qp)
            for hk in range(ATT_KV_HEADS):
                lanes = slice(hk * ATT_HEAD_DIM, (hk + 1) * ATT_HEAD_DIM)
                qs = _stack_heads(qparts, hk)
                p, inv, _ = _softmax_window(qs, kb[:, lanes], kpad[rows, lanes], lower, n > 0,
                                            _sink_row(sink_ref, hk))
                p_cur = jnp.where(lower, p, 0.0)
                ot = (_dot_tn(vb[:, lanes], _bf(p_cur)) + _dot_tn(vpad[rows, lanes], _bf(p - p_cur))) * inv
                for g in range(ATT_GROUP):
                    h = ATT_GROUP * hk + g
                    o_ref[rows, h * ATT_HEAD_DIM:(h + 1) * ATT_HEAD_DIM] = ot[:, g * WINDOW:(g + 1) * WINDOW].T
            ob = o_ref[rows, :]
            an_ref[rows, :] = _bf(ob * lax.rsqrt(_mean_last(ob * ob) + EPS) * aw_ref[...])
            return carry

        _loop_pairs(part * per, per, block, 0)
        _ride_wait(ride_modes, step, B * splits, ride_in, ride_out, sems)

    seq = lambda w, j: pl.BlockSpec((None, T, w), lambda b, s: (b, 0, j))
    full = lambda r, w: pl.BlockSpec((r, w), lambda b, s: (0, 0))
    return pl.pallas_call(
        body, name="attn_fwd", grid=(B, splits),
        in_specs=[seq(ATT_WIDTH, 0), seq(LANES, 4), seq(LANES, 5),
                  full(T, LANES), full(T, LANES), full(T, LANES),
                  pl.BlockSpec(memory_space=pltpu.SMEM), full(1, ATT_WIDTH)] + [ANY_SPEC] * nr,
        out_specs=[seq(ATT_WIDTH, 0), seq(ATT_WIDTH, 0), seq(ATT_WIDTH, 0), seq(LANES, 0)] + [ANY_SPEC] * nr,
        out_shape=[SDS((B, T, ATT_WIDTH), F32), SDS((B, T, ATT_WIDTH), BF16),
                   SDS((B, T, ATT_WIDTH), BF16), SDS((B, T, LANES), BF16)] + _exchange_shapes(ride_srcs, ride_modes),
        scratch_shapes=[pltpu.VMEM((T + WINDOW, LANES), BF16), pltpu.VMEM((T + WINDOW, LANES), BF16)]
        + _exchange_sems(nr),
        compiler_params=_params(("arbitrary", "arbitrary"), VMEM_LIMIT_BIG),
    )(proj3, proj3, proj3, cos, sinl, sinr, sinks, attn_w, *ride_srcs)


HG_GROUP = 8
HG_ROWS = HG_GROUP * HG_CHUNK


HG_STACK = HG_GROUP * HG_HEAD_DIM


def _group_mask():
    r = lax.broadcasted_iota(jnp.int32, (HG_ROWS, HG_ROWS), 0)
    c = lax.broadcasted_iota(jnp.int32, (HG_ROWS, HG_ROWS), 1)
    return ((r // HG_CHUNK) == (c // HG_CHUNK)) & (r >= c)


def _spread(a):
    blocks = []
    for c in range(HG_GROUP):
        above = jnp.zeros((c * HG_CHUNK, HG_HEAD_DIM), a.dtype)
        below = jnp.zeros(((HG_GROUP - 1 - c) * HG_CHUNK, HG_HEAD_DIM), a.dtype)
        blocks.append(jnp.concatenate([p for p in (above, a[_chunk_rows(c), :], below) if p.shape[0]], axis=0))
    return jnp.concatenate(blocks, axis=1)


def _pick(r):
    return jnp.concatenate([r[_chunk_rows(c), c * HG_HEAD_DIM:(c + 1) * HG_HEAD_DIM] for c in range(HG_GROUP)], axis=0)


def _lane_block(a, c):
    return a[:, c * HG_HEAD_DIM:(c + 1) * HG_HEAD_DIM]


def _chunk_cumsum(a, reverse=False):
    n = a.shape[0]
    pos = lax.broadcasted_iota(jnp.int32, a.shape, 0) % HG_CHUNK
    shift = 1
    while shift < HG_CHUNK:
        if reverse:
            a = a + jnp.where(pos < HG_CHUNK - shift, pltpu.roll(a, n - shift, 0), 0.0)
        else:
            a = a + jnp.where(pos >= shift, pltpu.roll(a, shift, 0), 0.0)
        shift *= 2
    return a


def _chunk_bcast(rows_1x128):
    return jnp.concatenate([jnp.broadcast_to(r, (HG_CHUNK, HG_HEAD_DIM)) for r in rows_1x128], axis=0)


def _hgrn_gates(hq, hf, lb):
    sq = _sigmoid(hq)
    q = hq * sq
    sg = _sigmoid(hf)
    f = lb + (1.0 - lb) * sg
    k = 1.0 - f
    logf = jnp.log(f)
    b = _chunk_cumsum(logf)
    bl = [_sum_rows(logf[_chunk_rows(c), :]) for c in range(HG_GROUP)]
    eb, enb, e2 = jnp.exp(b), jnp.exp(-b), jnp.exp(_chunk_bcast(bl) - b)
    ebl = [jnp.exp(r) for r in bl]
    return dict(sq=sq, sg=sg, f=f, eb=eb, enb=enb, e2=e2, ebl=ebl, qd=q * eb, kd=k * enb, k2=k * e2)


def _chunk_rows(c):
    return slice(c * HG_CHUNK, (c + 1) * HG_CHUNK)


def _head_lanes(h):
    return slice(h * HG_HEAD_DIM, (h + 1) * HG_HEAD_DIM)


def _hgrn_fwd(proj_h, lb, hg_w, ride_srcs, ride_modes):
    B, T, _ = proj_h.shape
    ng = T // HG_ROWS
    nr = len(ride_srcs)

    def body(*refs):
        hq_ref, hf_ref, hi_ref, hg_ref, lb_ref, gw_ref = refs[:6]
        ride_in = refs[6:6 + nr]
        o_ref, rg_ref, sp_ref = refs[6 + nr:9 + nr]
        ride_out = refs[9 + nr:9 + 2 * nr]
        st = refs[9 + 2 * nr]
        sems = refs[10 + 2 * nr:]
        gi = pl.program_id(1)
        step = pl.program_id(0) * ng + gi
        _ride_start(ride_modes, step, B * ng, ride_in, ride_out, sems)

        @pl.when(gi == 0)
        def _():
            st[...] = jnp.zeros(st.shape, F32)

        lo = _group_mask()
        for h in range(HG_HEADS):
            lanes = _head_lanes(h)
            gt = _hgrn_gates(hq_ref[:, lanes], hf_ref[:, lanes], lb_ref[:, lanes])
            v, qd, kd = _bf(hi_ref[:, lanes]), _bf(gt["qd"]), _bf(gt["kd"])
            a = jnp.where(lo, _dot_nt(qd, kd), 0.0)
            kv = _dot_tn(v, _spread(_bf(gt["k2"])))
            s = st[h]
            before = []
            for c in range(HG_GROUP):
                before.append(s)
                s = s * gt["ebl"][c] + _lane_block(kv, c)
            st[h] = s
            sp = jnp.concatenate(before, axis=1)
            sp_ref[h] = sp
            o = _dot(_bf(a), v) + _dot_nt(_spread(qd), _bf(sp))
            o_ref[:, lanes] = o
            hg = hg_ref[:, lanes]
            rn = o * lax.rsqrt(_mean_last(o * o) + EPS) * gw_ref[...]
            rg_ref[:, lanes] = _bf(rn * (hg * _sigmoid(hg)))
        _ride_wait(ride_modes, step, B * ng, ride_in, ride_out, sems)

    part = lambda j: pl.BlockSpec((None, HG_ROWS, HG_WIDTH), lambda b, g: (b, g, j))
    return pl.pallas_call(
        body, name="hgrn_fwd", grid=(B, ng),
        in_specs=[part(0), part(1), part(2), part(3),
                  pl.BlockSpec((1, HG_WIDTH), lambda b, g: (0, 0)),
                  pl.BlockSpec((1, LANES), lambda b, g: (0, 0))] + [ANY_SPEC] * nr,
        out_specs=[part(0), part(0),
                   pl.BlockSpec((None, HG_HEADS, None, HG_HEAD_DIM, HG_STACK), lambda b, g: (b, 0, g, 0, 0))]
        + [ANY_SPEC] * nr,
        out_shape=[SDS((B, T, HG_WIDTH), F32), SDS((B, T, HG_WIDTH), BF16),
                   SDS((B, HG_HEADS, ng, HG_HEAD_DIM, HG_STACK), F32)] + _exchange_shapes(ride_srcs, ride_modes),
        scratch_shapes=[pltpu.VMEM((HG_HEADS, HG_HEAD_DIM, HG_HEAD_DIM), F32)] + _exchange_sems(nr),
        compiler_params=_params(("arbitrary", "arbitrary"), VMEM_LIMIT_BIG),
    )(proj_h, proj_h, proj_h, proj_h, lb, hg_w, *ride_srcs)


def _mix_out(x2, attn_n, rec_g, mod8, post_w, w_out_bf, T, ride_srcs, ride_modes):
    N = x2.shape[0]
    TM = _tile_rows(T, big=True)
    tps = T // TM
    nr = len(ride_srcs)

    def body(*refs):
        x_ref, an_ref, rg_ref, mod_ref, pw_ref, w_ref = refs[:6]
        ride_in = refs[6:6 + nr]
        mix_ref, x1_ref, cat_ref = refs[6 + nr:9 + nr]
        ride_out = refs[9 + nr:9 + 2 * nr]
        sems = refs[9 + 2 * nr:]
        _ride_start(ride_modes, pl.program_id(0), N // TM, ride_in, ride_out, sems)
        cat = jnp.concatenate([an_ref[...], rg_ref[...]], axis=1)
        cat_ref[...] = cat
        mix = _dot(cat, w_ref[...])
        mix_ref[...] = mix
        r = lax.rsqrt(_mean_last(mix * mix) + EPS)
        x1_ref[...] = x_ref[...] + mod_ref[2:3, :] * (mix * r * pw_ref[...])
        _ride_wait(ride_modes, pl.program_id(0), N // TM, ride_in, ride_out, sems)

    row = lambda w: pl.BlockSpec((TM, w), lambda i: (i, 0))
    return pl.pallas_call(
        body, name="mix_out", grid=(N // TM,),
        in_specs=[row(D_MODEL), row(ATT_WIDTH), row(HG_WIDTH), _mod_spec(tps),
                  pl.BlockSpec((1, D_MODEL), lambda i: (0, 0)),
                  pl.BlockSpec((D_MODEL, D_MODEL), lambda i: (0, 0))] + [ANY_SPEC] * nr,
        out_specs=[row(D_MODEL), row(D_MODEL), row(D_MODEL)] + [ANY_SPEC] * nr,
        out_shape=[SDS((N, D_MODEL), F32), SDS((N, D_MODEL), F32), SDS((N, D_MODEL), BF16)]
        + _exchange_shapes(ride_srcs, ride_modes),
        scratch_shapes=_exchange_sems(nr),
        compiler_params=_params(("arbitrary",), VMEM_LIMIT_BIG),
    )(x2, attn_n, rec_g, mod8, post_w, w_out_bf, *ride_srcs)


def _load_weights_once(pairs, sem):
    @pl.when(pl.program_id(0) == 0)
    def _():
        cps = [pltpu.make_async_copy(src, dst, sem.at[i]) for i, (src, dst) in enumerate(pairs)]
        for cp in cps:
            cp.start()
        for cp in cps:
            cp.wait()


MLP_HALF = D_MODEL // 2
MLP_PIECES = 2 * N_DEV + 2


def _mlp_weight_pieces(wu_a, wu_b, wd_a, wd_b, wu, wd):
    cols = D_FF // N_DEV
    pairs = []
    for h, half in enumerate((wu_a, wu_b)):
        for j in range(N_DEV):
            pairs.append((half.at[j], wu.at[pl.ds(h * MLP_HALF, MLP_HALF), pl.ds(j * cols, cols)]))
    for h, half in enumerate((wd_a, wd_b)):
        pairs.append((half, wd.at[:, pl.ds(h * MLP_HALF, MLP_HALF)]))
    return pairs


def _mlp_fwd(x1, mod8, pre_w, w_up_halves, w_down_halves, T):
    N = x1.shape[0]
    TM = _tile_rows(T)
    tps = T // TM

    def body(x_ref, mod_ref, pw_ref, wua, wub, wda, wdb, up_ref, u_ref, d_ref, h2_ref, wu, wd, sem):
        _load_weights_once(_mlp_weight_pieces(wua, wub, wda, wdb, wu, wd), sem)
        x = x_ref[...]
        r = lax.rsqrt(_mean_last(x * x) + EPS)
        h = (x * r * pw_ref[...]) * (1.0 + mod_ref[4:5, :]) + mod_ref[3:4, :]
        hb = _bf(h)
        h2_ref[...] = hb
        up = _dot(hb, wu[...])
        up_ref[...] = up
        ru = jnp.maximum(up, 0.0)
        u = _bf(ru * ru)
        u_ref[...] = u
        d_ref[...] = _dot(u, wd[...])

    row = lambda w: pl.BlockSpec((TM, w), lambda i: (i, 0))
    return pl.pallas_call(
        body, name="mlp_fwd", grid=(N // TM,),
        in_specs=[row(D_MODEL), _mod_spec(tps), pl.BlockSpec((1, D_MODEL), lambda i: (0, 0))] + [ANY_SPEC] * 4,
        out_specs=[row(D_FF), row(D_FF), row(D_MODEL), row(D_MODEL)],
        out_shape=[SDS((N, D_FF), F32), SDS((N, D_FF), BF16), SDS((N, D_MODEL), F32), SDS((N, D_MODEL), BF16)],
        scratch_shapes=[pltpu.VMEM((D_MODEL, D_FF), BF16), pltpu.VMEM((D_FF, D_MODEL), BF16),
                        pltpu.SemaphoreType.DMA((MLP_PIECES,))],
        compiler_params=_params(("arbitrary",), VMEM_LIMIT_BIG),
    )(x1, mod8, pre_w, *w_up_halves, *w_down_halves)


def _acc_rows(acc_ref, first, rows):
    @pl.when(first)
    def _():
        acc_ref[...] = jnp.zeros(acc_ref.shape, F32)
    for i, r in enumerate(rows):
        acc_ref[i:i + 1, :] += r


def _mlp_bwd(x1, d, up, tgt, mod8, pre_w, post_w, w_up_halves, w_down_halves, T):
    N = x1.shape[0]
    TM = _tile_rows(T)
    tps = T // TM

    def body(x_ref, d_ref, up_ref, t_ref, mod_ref, pw_ref, qw_ref, wua, wub, wda, wdb,
             dx_ref, dup_ref, dd_ref, acc_ref, wd, wu, sem):
        _load_weights_once(_mlp_weight_pieces(wua, wub, wda, wdb, wu, wd), sem)
        sh2, sc2, g2 = mod_ref[3:4, :], mod_ref[4:5, :], mod_ref[5:6, :]
        x = x_ref[...]
        r1 = lax.rsqrt(_mean_last(x * x) + EPS)
        xh = x * r1
        n2 = xh * pw_ref[...]
        dv = d_ref[...]
        rd = lax.rsqrt(_mean_last(dv * dv) + EPS)
        dh = dv * rd
        rr = dh * qw_ref[...]
        e = x + g2 * rr - t_ref[...]
        loss = 0.5 * jnp.sum(_sum_rows(e * e), axis=1, keepdims=True) / D_MODEL
        dy = e * (1.0 / D_MODEL)
        dg2 = _sum_rows(dy * rr)
        drr = dy * g2
        dw_post = _sum_rows(drr * dh)
        ddh = drr * qw_ref[...]
        dd = _bf(rd * (ddh - dh * _mean_last(ddh * dh)))
        dd_ref[...] = dd
        ru = jnp.maximum(up_ref[...], 0.0)
        dup = _bf(_dot_nt(dd, wd[...]) * (2.0 * ru))
        dup_ref[...] = dup
        dh2 = _dot_nt(dup, wu[...])
        dsh2 = _sum_rows(dh2)
        dsc2 = _sum_rows(dh2 * n2)
        dn2 = dh2 * (1.0 + sc2)
        dw_pre = _sum_rows(dn2 * xh)
        dxh = dn2 * pw_ref[...]
        dx_ref[...] = dy + r1 * (dxh - xh * _mean_last(dxh * xh))
        _acc_rows(acc_ref, pl.program_id(0) % tps == 0,
                  [dsh2, dsc2, dg2, dw_pre, dw_post, jnp.broadcast_to(loss, (1, D_MODEL))])

    row = lambda w: pl.BlockSpec((TM, w), lambda i: (i, 0))
    vec = pl.BlockSpec((1, D_MODEL), lambda i: (0, 0))
    B = N // T
    return pl.pallas_call(
        body, name="mlp_bwd", grid=(N // TM,),
        in_specs=[row(D_MODEL), row(D_MODEL), row(D_FF), row(D_MODEL), _mod_spec(tps), vec, vec] + [ANY_SPEC] * 4,
        out_specs=[row(D_MODEL), row(D_FF), row(D_MODEL), _mod_spec(tps)],
        out_shape=[SDS((N, D_MODEL), F32), SDS((N, D_FF), BF16), SDS((N, D_MODEL), BF16),
                   SDS((B, 8, D_MODEL), F32)],
        scratch_shapes=[pltpu.VMEM((D_FF, D_MODEL), BF16), pltpu.VMEM((D_MODEL, D_FF), BF16),
                        pltpu.SemaphoreType.DMA((MLP_PIECES,))],
        compiler_params=_params(("arbitrary",), VMEM_LIMIT_BIG),
    )(x1, d, up, tgt, mod8, pre_w, post_w, *w_up_halves, *w_down_halves)


def _mix_bwd(mix, dx1, mod8, post_w, w_out_bf, T, ride_srcs, ride_modes):
    N = mix.shape[0]
    TM = _tile_rows(T, big=True)
    tps = T // TM
    nr = len(ride_srcs)

    def body(*refs):
        mix_ref, dx_ref, mod_ref, pw_ref, w_ref = refs[:5]
        ride_in = refs[5:5 + nr]
        dan_ref, drg_ref, dmix_ref, acc_ref = refs[5 + nr:9 + nr]
        ride_out = refs[9 + nr:9 + 2 * nr]
        sems = refs[9 + 2 * nr:]
        _ride_start(ride_modes, pl.program_id(0), N // TM, ride_in, ride_out, sems)
        g1 = mod_ref[2:3, :]
        mix = mix_ref[...]
        dx1 = dx_ref[...]
        rm = lax.rsqrt(_mean_last(mix * mix) + EPS)
        mh = mix * rm
        dg1 = _sum_rows(dx1 * (mh * pw_ref[...]))
        dr = dx1 * g1
        dw_post = _sum_rows(dr * mh)
        dmh = dr * pw_ref[...]
        dmix = _bf(rm * (dmh - mh * _mean_last(dmh * mh)))
        dmix_ref[...] = dmix
        dcat = _dot_nt(dmix, w_ref[...])
        dan_ref[...] = dcat[:, :ATT_WIDTH]
        drg_ref[...] = dcat[:, ATT_WIDTH:]
        _acc_rows(acc_ref, pl.program_id(0) % tps == 0, [dg1, dw_post])
        _ride_wait(ride_modes, pl.program_id(0), N // TM, ride_in, ride_out, sems)

    row = lambda w: pl.BlockSpec((TM, w), lambda i: (i, 0))
    B = N // T
    return pl.pallas_call(
        body, name="mix_bwd", grid=(N // TM,),
        in_specs=[row(D_MODEL), row(D_MODEL), _mod_spec(tps), pl.BlockSpec((1, D_MODEL), lambda i: (0, 0)),
                  pl.BlockSpec((D_MODEL, D_MODEL), lambda i: (0, 0))] + [ANY_SPEC] * nr,
        out_specs=[row(ATT_WIDTH), row(HG_WIDTH), row(D_MODEL), _mod_spec(tps)] + [ANY_SPEC] * nr,
        out_shape=[SDS((N, ATT_WIDTH), F32), SDS((N, HG_WIDTH), F32), SDS((N, D_MODEL), BF16),
                   SDS((B, 8, D_MODEL), F32)] + _exchange_shapes(ride_srcs, ride_modes),
        scratch_shapes=_exchange_sems(nr),
        compiler_params=_params(("arbitrary",), VMEM_LIMIT_BIG),
    )(mix, dx1, mod8, post_w, w_out_bf, *ride_srcs)


def _hgrn_bwd(proj_h, lb, hg_w, o, s_prev, drg, ride_srcs, ride_modes):
    B, T, _ = proj_h.shape
    ng = T // HG_ROWS
    nr = len(ride_srcs)

    def body(*refs):
        hq_ref, hf_ref, hi_ref, hg_ref, lb_ref, gw_ref, o_ref, sp_ref, drg_ref = refs[:9]
        ride_in = refs[9:9 + nr]
        dhq_ref, dhf_ref, dhi_ref, dhg_ref, dlb_ref, dgw_ref = refs[9 + nr:15 + nr]
        ride_out = refs[15 + nr:15 + 2 * nr]
        dst = refs[15 + 2 * nr]
        sems = refs[16 + 2 * nr:]
        step = pl.program_id(0) * ng + pl.program_id(1)
        _ride_start(ride_modes, step, B * ng, ride_in, ride_out, sems)

        @pl.when(pl.program_id(1) == 0)
        def _():
            dst[...] = jnp.zeros(dst.shape, F32)
            dlb_ref[...] = jnp.zeros(dlb_ref.shape, F32)
            dgw_ref[...] = jnp.zeros(dgw_ref.shape, F32)

        lo = _group_mask()
        gw = gw_ref[...]

        for h in range(HG_HEADS):
            lanes = _head_lanes(h)
            lbv = lb_ref[:, lanes]
            hq = hq_ref[:, lanes]
            gt = _hgrn_gates(hq, hf_ref[:, lanes], lbv)
            sq, sg, qdf, kdf, k2f, ebl = gt["sq"], gt["sg"], gt["qd"], gt["kd"], gt["k2"], gt["ebl"]
            v, qd, kd = _bf(hi_ref[:, lanes]), _bf(qdf), _bf(kdf)
            ov = o_ref[:, lanes]
            hg = hg_ref[:, lanes]
            shg = _sigmoid(hg)
            dr = drg_ref[:, lanes]
            ro = lax.rsqrt(_mean_last(ov * ov) + EPS)
            oh = ov * ro
            dhg_ref[:, lanes] = _bf(dr * (oh * gw) * (shg + hg * shg * (1.0 - shg)))
            drn = dr * (hg * shg)
            dgw_ref[...] += jnp.broadcast_to(_sum_rows(drn * oh), (8, LANES))
            doh = drn * gw
            do = _bf(ro * (doh - oh * _mean_last(doh * oh)))
            a = jnp.where(lo, _dot_nt(qd, kd), 0.0)
            da = _bf(jnp.where(lo, _dot_nt(do, v), 0.0))
            dv = _dot_tn(_bf(a), do)
            dqd = _dot(da, kd)
            dkd = _dot_tn(da, qd)
            sp = sp_ref[h]
            incr = _dot_tn(do, _spread(qd))
            ds = dst[h]
            after = [None] * HG_GROUP
            for c in reversed(range(HG_GROUP)):
                after[c] = ds
                ds = ds * ebl[c] + _lane_block(incr, c)
            dst[h] = ds
            dss = jnp.concatenate(after, axis=1)
            dssb = _bf(dss)
            dk2 = _pick(_dot(v, dssb))
            dhi_ref[:, lanes] = _bf(dv + _dot_nt(_spread(_bf(k2f)), dssb))
            dqd = dqd + _pick(_dot(do, _bf(sp)))
            debl = _sum_rows(dss * sp)
            k2g = dk2 * k2f
            db = dqd * qdf - dkd * kdf - k2g
            dk = dkd * gt["enb"] + dk2 * gt["e2"]
            dbl = _chunk_bcast([_lane_block(debl, c) * ebl[c] + _sum_rows(k2g[_chunk_rows(c), :])
                                for c in range(HG_GROUP)])
            dg = _chunk_cumsum(db, reverse=True) + dbl
            df = dg / gt["f"] - dk
            dhf_ref[:, lanes] = _bf(df * (1.0 - lbv) * sg * (1.0 - sg))
            dlb_ref[:, lanes] += jnp.broadcast_to(_sum_rows(df * (1.0 - sg)), (8, LANES))
            dhq_ref[:, lanes] = _bf((dqd * gt["eb"]) * (sq + hq * sq * (1.0 - sq)))
        _ride_wait(ride_modes, step, B * ng, ride_in, ride_out, sems)

    part = lambda j: pl.BlockSpec((None, HG_ROWS, HG_WIDTH), lambda b, g: (b, ng - 1 - g, j))
    return pl.pallas_call(
        body, name="hgrn_bwd", grid=(B, ng),
        in_specs=[part(0), part(1), part(2), part(3),
                  pl.BlockSpec((1, HG_WIDTH), lambda b, g: (0, 0)),
                  pl.BlockSpec((1, LANES), lambda b, g: (0, 0)),
                  part(0),
                  pl.BlockSpec((None, HG_HEADS, None, HG_HEAD_DIM, HG_STACK), lambda b, g: (b, 0, ng - 1 - g, 0, 0)),
                  part(0)] + [ANY_SPEC] * nr,
        out_specs=[part(0), part(0), part(0), part(0),
                   pl.BlockSpec((None, 8, HG_WIDTH), lambda b, g: (b, 0, 0)),
                   pl.BlockSpec((None, 8, LANES), lambda b, g: (b, 0, 0))] + [ANY_SPEC] * nr,
        out_shape=[SDS((B, T, HG_WIDTH), BF16)] * 4 + [SDS((B, 8, HG_WIDTH), F32), SDS((B, 8, LANES), F32)]
        + _exchange_shapes(ride_srcs, ride_modes),
        scratch_shapes=[pltpu.VMEM((HG_HEADS, HG_HEAD_DIM, HG_HEAD_DIM), F32)] + _exchange_sems(nr),
        compiler_params=_params(("arbitrary", "arbitrary"), VMEM_LIMIT_BIG),
    )(proj_h, proj_h, proj_h, proj_h, lb, hg_w, o, s_prev, drg, *ride_srcs)


def _attn_bwd(qr, kr, proj3, attn_o, dan, tables, sinks, attn_w, ride_srcs, ride_modes):
    B, T, _ = proj3.shape
    nb = T // WINDOW
    splits = min(ATT_SPLITS, nb)
    per = nb // splits
    nr = len(ride_srcs)
    cos, sinl, sinr = tables
    QKV = ATT_WIDTH + 2 * LANES

    def body(*refs):
        qr_ref, kr_ref, v_ref, o_ref, dan_ref, cos_ref, sl_ref, sr_ref, sink_ref, aw_ref = refs[:10]
        ride_in = refs[10:10 + nr]
        dqkv_ref, dsink_ref, daw_ref = refs[10 + nr:13 + nr]
        ride_out = refs[13 + nr:13 + 2 * nr]
        kpad, vpad, dkpad, dvpad, dqb, dsk = refs[13 + 2 * nr:19 + 2 * nr]
        sems = refs[19 + 2 * nr:]
        part = pl.program_id(1)
        step = pl.program_id(0) * splits + part
        _ride_start(ride_modes, step, B * splits, ride_in, ride_out, sems)

        @pl.when(part == 0)
        def _():
            kpad[0:WINDOW, :] = jnp.zeros((WINDOW, LANES), BF16)
            vpad[0:WINDOW, :] = jnp.zeros((WINDOW, LANES), BF16)
            kpad[WINDOW:, :] = kr_ref[...]
            vpad[WINDOW:, :] = _bf(v_ref[...])
            dkpad[...] = jnp.zeros(dkpad.shape, F32)
            dvpad[...] = jnp.zeros(dvpad.shape, F32)
            dsk[...] = jnp.zeros(dsk.shape, F32)
            daw_ref[...] = jnp.zeros(daw_ref.shape, F32)

        lower = _lower_mask()
        aw = aw_ref[...]

        def block(n, daw):
            r0 = pl.multiple_of(n * WINDOW, WINDOW)
            rows = pl.ds(r0, WINDOW)
            nxt = pl.ds(r0 + WINDOW, WINDOW)
            ob = o_ref[rows, :]
            dn = dan_ref[rows, :]
            ro = lax.rsqrt(_mean_last(ob * ob) + EPS)
            oh = ob * ro
            daw = daw + _sum_rows(dn * oh)
            doh = dn * aw
            do = _bf(ro * (doh - oh * _mean_last(doh * oh)))
            doparts = [do[:, j * LANES:(j + 1) * LANES] for j in range(ATT_WIDTH // LANES)]
            qparts = [qr_ref[rows, j * LANES:(j + 1) * LANES] for j in range(ATT_WIDTH // LANES)]
            for hk in range(ATT_KV_HEADS):
                lanes = slice(hk * ATT_HEAD_DIM, (hk + 1) * ATT_HEAD_DIM)
                qs = _stack_heads(qparts, hk)
                dos = _stack_heads(doparts, hk)
                k_cur, k_prev = kpad[nxt, lanes], kpad[rows, lanes]
                v_cur, v_prev = vpad[nxt, lanes], vpad[rows, lanes]
                p, inv, es = _softmax_window(qs, k_cur, k_prev, lower, n > 0, _sink_row(sink_ref, hk))
                p = p * inv
                dp = jnp.where(lower, _dot_nt(v_cur, dos), _dot_nt(v_prev, dos))
                delta = jnp.sum(p * dp, axis=0, keepdims=True)
                ds = p * (dp - delta)
                sk = (es * inv) * delta
                ds_cur = jnp.where(lower, ds, 0.0)
                p_cur = jnp.where(lower, p, 0.0)
                ds_cur, ds_prev = _bf(ds_cur), _bf(ds - ds_cur)
                p_cur, p_prev = _bf(p_cur), _bf(p - p_cur)
                dqt = (_dot_tn(k_cur, ds_cur) + _dot_tn(k_prev, ds_prev)) * ATT_SCALE
                dkpad[nxt, lanes] += _dot(ds_cur, qs)
                dkpad[rows, lanes] += _dot(ds_prev, qs)
                dvpad[nxt, lanes] += _dot(p_cur, dos)
                dvpad[rows, lanes] += _dot(p_prev, dos)
                for g in range(ATT_GROUP):
                    h = ATT_GROUP * hk + g
                    cols = slice(g * WINDOW, (g + 1) * WINDOW)
                    dqb[:, h * ATT_HEAD_DIM:(h + 1) * ATT_HEAD_DIM] = dqt[:, cols].T
                    head_lane = lax.broadcasted_iota(jnp.int32, dsk.shape, 1) == h
                    dsk[...] += jnp.where(head_lane, -jnp.sum(sk[:, cols], axis=1, keepdims=True), 0.0)
            cs, sl, sr = cos_ref[rows, :], sl_ref[rows, :], sr_ref[rows, :]
            for j in range(ATT_WIDTH // LANES):
                dqkv_ref[rows, j * LANES:(j + 1) * LANES] = _bf(_rope_t(dqb[:, j * LANES:(j + 1) * LANES], cs, sl, sr))
            return daw

        daw = _loop_pairs(part * per, per, block, jnp.zeros((1, ATT_WIDTH), F32))
        daw_ref[...] += jnp.broadcast_to(daw, (8, ATT_WIDTH))
        dsink_ref[...] = dsk[...]

        def finish(n, carry):
            r0 = pl.multiple_of(n * WINDOW, WINDOW)
            rows = pl.ds(r0, WINDOW)
            nxt = pl.ds(r0 + WINDOW, WINDOW)
            cs, sl, sr = cos_ref[rows, :], sl_ref[rows, :], sr_ref[rows, :]
            dqkv_ref[rows, ATT_WIDTH:ATT_WIDTH + LANES] = _bf(_rope_t(dkpad[nxt, :], cs, sl, sr))
            dqkv_ref[rows, ATT_WIDTH + LANES:QKV] = _bf(dvpad[nxt, :])
            return carry

        @pl.when(part == splits - 1)
        def _():
            lax.fori_loop(0, nb, finish, 0)

        _ride_wait(ride_modes, step, B * splits, ride_in, ride_out, sems)

    seq = lambda w, j: pl.BlockSpec((None, T, w), lambda b, s: (b, 0, j))
    full = lambda r, w: pl.BlockSpec((r, w), lambda b, s: (0, 0))
    return pl.pallas_call(
        body, name="attn_bwd", grid=(B, splits),
        in_specs=[seq(ATT_WIDTH, 0), seq(LANES, 0), seq(LANES, 5), seq(ATT_WIDTH, 0), seq(ATT_WIDTH, 0),
                  full(T, LANES), full(T, LANES), full(T, LANES),
                  pl.BlockSpec(memory_space=pltpu.SMEM), full(1, ATT_WIDTH)] + [ANY_SPEC] * nr,
        out_specs=[seq(QKV, 0), pl.BlockSpec((None, 8, LANES), lambda b, s: (b, 0, 0)),
                   pl.BlockSpec((None, 8, ATT_WIDTH), lambda b, s: (b, 0, 0))] + [ANY_SPEC] * nr,
        out_shape=[SDS((B, T, QKV), BF16), SDS((B, 8, LANES), F32), SDS((B, 8, ATT_WIDTH), F32)]
        + _exchange_shapes(ride_srcs, ride_modes),
        scratch_shapes=[pltpu.VMEM((T + WINDOW, LANES), BF16), pltpu.VMEM((T + WINDOW, LANES), BF16),
                        pltpu.VMEM((T + WINDOW, LANES), F32), pltpu.VMEM((T + WINDOW, LANES), F32),
                        pltpu.VMEM((WINDOW, ATT_WIDTH), F32), pltpu.VMEM((8, LANES), F32)] + _exchange_sems(nr),
        compiler_params=_params(("arbitrary", "arbitrary"), VMEM_LIMIT_BIG),
    )(qr, kr, proj3, attn_o, dan, cos, sinl, sinr, sinks, attn_w, *ride_srcs)


def _in_bwd(x2, dx1, dqkv, dhq, dhf, dhi, dhg, mod8, pre_w, w_in_bf, T, ride_srcs, ride_modes):
    N = x2.shape[0]
    TM = _tile_rows(T, big=True)
    tps = T // TM
    nr = len(ride_srcs)
    pieces = [(0, ATT_WIDTH + 2 * LANES), (768, HG_WIDTH), (1280, HG_WIDTH), (1792, HG_WIDTH), (2304, HG_WIDTH)]

    def body(*refs):
        x_ref, dx_ref, p0, p1, p2, p3, p4, mod_ref, pw_ref, w_ref = refs[:10]
        ride_in = refs[10:10 + nr]
        gx_ref, dproj_ref, acc_ref = refs[10 + nr:13 + nr]
        ride_out = refs[13 + nr:13 + 2 * nr]
        sems = refs[13 + 2 * nr:]
        _ride_start(ride_modes, pl.program_id(0), N // TM, ride_in, ride_out, sems)
        sc1 = mod_ref[1:2, :]
        dh = jnp.zeros((TM, D_MODEL), F32)
        for ref, (off, width) in zip((p0, p1, p2, p3, p4), pieces):
            pb = ref[...]
            dproj_ref[:, off:off + width] = pb
            dh = dh + _dot(pb, w_ref[off:off + width, :])
        x = x_ref[...]
        r = lax.rsqrt(_mean_last(x * x) + EPS)
        xh = x * r
        n1 = xh * pw_ref[...]
        dsh1 = _sum_rows(dh)
        dsc1 = _sum_rows(dh * n1)
        dn1 = dh * (1.0 + sc1)
        dw_pre = _sum_rows(dn1 * xh)
        dxh = dn1 * pw_ref[...]
        gx_ref[...] = dx_ref[...] + r * (dxh - xh * _mean_last(dxh * xh))
        _acc_rows(acc_ref, pl.program_id(0) % tps == 0, [dsh1, dsc1, dw_pre])
        _ride_wait(ride_modes, pl.program_id(0), N // TM, ride_in, ride_out, sems)

    row = lambda w: pl.BlockSpec((TM, w), lambda i: (i, 0))
    B = N // T
    return pl.pallas_call(
        body, name="in_bwd", grid=(N // TM,),
        in_specs=[row(D_MODEL), row(D_MODEL), row(768), row(HG_WIDTH), row(HG_WIDTH), row(HG_WIDTH),
                  row(HG_WIDTH), _mod_spec(tps), pl.BlockSpec((1, D_MODEL), lambda i: (0, 0)),
                  pl.BlockSpec((IN_COLS, D_MODEL), lambda i: (0, 0))] + [ANY_SPEC] * nr,
        out_specs=[row(D_MODEL), row(IN_COLS), _mod_spec(tps)] + [ANY_SPEC] * nr,
        out_shape=[SDS((N, D_MODEL), F32), SDS((N, IN_COLS), BF16), SDS((B, 8, D_MODEL), F32)]
        + _exchange_shapes(ride_srcs, ride_modes),
        scratch_shapes=_exchange_sems(nr),
        compiler_params=_params(("arbitrary",), VMEM_LIMIT_BIG),
    )(x2, dx1, dqkv, dhq, dhf, dhi, dhg, mod8, pre_w, w_in_bf, *ride_srcs)


def _matmul_tn(name, a, b, tn, tm=512, by_owner_cols=False):
    K, M = a.shape
    Nc = b.shape[1]
    tm = min(tm, M)

    def body(a_ref, b_ref, o_ref):
        o_ref[...] = _bf(_dot_tn(a_ref[...], b_ref[...]))

    if by_owner_cols:
        assert tn * N_DEV == Nc
        out_shape = SDS((N_DEV, M, tn), BF16)
        out_spec = pl.BlockSpec((None, tm, tn), lambda i, j: (j, i, 0))
    else:
        out_shape = SDS((M, Nc), BF16)
        out_spec = pl.BlockSpec((tm, tn), lambda i, j: (i, j))
    return pl.pallas_call(
        body, name=name, grid=(M // tm, Nc // tn),
        in_specs=[pl.BlockSpec((K, tm), lambda i, j: (0, i)),
                  pl.BlockSpec((K, tn), lambda i, j: (0, j))],
        out_specs=out_spec, out_shape=out_shape,
        compiler_params=_params(("arbitrary", "arbitrary"), VMEM_LIMIT_BIG),
    )(a, b)


def _adamw_math(w, g, m, v):
    m2 = ADAM_B1 * m + (1.0 - ADAM_B1) * g
    v2 = ADAM_B2 * v + (1.0 - ADAM_B2) * (g * g)
    m_hat = m2 / (1.0 - ADAM_B1 ** ADAM_STEP)
    v_hat = v2 / (1.0 - ADAM_B2 ** ADAM_STEP)
    delta = -ADAM_LR * (m_hat / (jnp.sqrt(v_hat) + ADAM_EPS) + ADAM_WD * w)
    return delta, m2, v2


def _pair_add(name, gw, theirs):
    chips, _, r, c = gw.shape
    tr = r
    core = lax.axis_index("c").astype(jnp.int32).reshape(1)

    def body(core_ref, mine_ref, theirs_ref, o_ref):
        o_ref[...] = _bf(mine_ref[...].astype(F32) + theirs_ref[...].astype(F32))

    block = pl.BlockSpec((None, tr, c), lambda s, i, core_ref: (s, i, 0))
    grid_spec = pltpu.PrefetchScalarGridSpec(
        num_scalar_prefetch=1, grid=(chips, r // tr),
        in_specs=[pl.BlockSpec((None, None, tr, c), lambda s, i, core_ref: (s, core_ref[0], i, 0)), block],
        out_specs=block)
    return pl.pallas_call(
        body, name=name, grid_spec=grid_spec, out_shape=SDS((chips, r, c), BF16),
        compiler_params=_params(("arbitrary", "arbitrary")),
    )(core, gw, theirs)


def _reduce_adamw(name, parts, w, m, v):
    r, c = w.shape
    tr = r if r % 256 else 256
    slots = parts.shape[0]

    def body(p_ref, w_ref, m_ref, v_ref, g_ref, d_ref, m2_ref, v2_ref):
        g = p_ref[0].astype(F32)
        for s in range(1, slots):
            g = g + p_ref[s].astype(F32)
        g_ref[...] = g
        d_ref[...], m2_ref[...], v2_ref[...] = _adamw_math(w_ref[...], g, m_ref[...], v_ref[...])

    blk = pl.BlockSpec((tr, c), lambda i: (i, 0))
    return pl.pallas_call(
        body, name=name, grid=(r // tr,),
        in_specs=[pl.BlockSpec((slots, tr, c), lambda i: (0, i, 0)), blk, blk, blk],
        out_specs=[blk] * 4, out_shape=[SDS((r, c), F32)] * 4,
        compiler_params=_params(("arbitrary",), VMEM_LIMIT_BIG),
    )(parts, w, m, v)


def _ada_grad_adamw(c_all, dmod_all, w, m, v):
    r, c = w.shape
    tr = 256
    nb = c_all.shape[0]

    def body(c_ref, dm_ref, w_ref, m_ref, v_ref, g_ref, d_ref, m2_ref, v2_ref):
        cv = c_ref[...]
        g = _dot_tn(cv * _sigmoid(cv), dm_ref[...])
        g_ref[...] = g
        d_ref[...], m2_ref[...], v2_ref[...] = _adamw_math(w_ref[...], g, m_ref[...], v_ref[...])

    blk = pl.BlockSpec((tr, c), lambda i: (i, 0))
    return pl.pallas_call(
        body, name="ada_grad_adamw", grid=(r // tr,),
        in_specs=[pl.BlockSpec((nb, tr), lambda i: (0, i)), pl.BlockSpec((nb, c), lambda i: (0, 0)),
                  blk, blk, blk],
        out_specs=[blk] * 4, out_shape=[SDS((r, c), F32)] * 4,
        compiler_params=_params(("arbitrary",)),
    )(c_all, dmod_all, w, m, v)


_SMALL = [("b_ada", 6144), ("pre_w_mix", 1024), ("attn_sinks", 128), ("attn_out_w", 512), ("lb_table", 1024),
          ("hg_norm_w", 128), ("post_w_mix", 1024), ("pre_w_mlp", 1024), ("post_w_mlp", 1024)]


def _pack_small(acc_in, acc_mix, acc_mlp, dsink, daw, dlb, dgw, lb_p, ada_cols):
    B = acc_in.shape[0]
    width = sum(w for _, w in _SMALL) + LANES

    def body(ain, amix, amlp, dsk_ref, daw_ref, dlb_ref, dgw_ref, lbp_ref, packed_ref, dmod_ref):
        def total(ref, r, w=None):
            out = ref[0, r:r + 1, :] if w is None else ref[0, r:r + 1, :w]
            for b in range(1, B):
                out = out + (ref[b, r:r + 1, :] if w is None else ref[b, r:r + 1, :w])
            return out

        d_b_ada = None
        for b in range(B):
            mods = [ain[b, 0:1, :], ain[b, 1:2, :], amix[b, 0:1, :], amlp[b, 0:1, :], amlp[b, 1:2, :], amlp[b, 2:3, :]]
            full = jnp.concatenate(mods, axis=1)
            for j in range(N_DEV):
                dmod_ref[j, b:b + 1, :] = full[:, j * ada_cols:(j + 1) * ada_cols]
            d_b_ada = full if d_b_ada is None else d_b_ada + full
        d_lb = total(dlb_ref, 0)
        pp = lbp_ref[0:1, :] * lbp_ref[1:2, :]
        pieces = [d_b_ada, total(ain, 2), total(dsk_ref, 0), total(daw_ref, 0), -d_lb * pp, d_lb * pp,
                  total(dgw_ref, 0), total(amix, 1), total(amlp, 3), total(amlp, 4), total(amlp, 5, LANES)]
        off = 0
        for piece in pieces:
            packed_ref[:, off:off + piece.shape[1]] = piece
            off += piece.shape[1]

    return pl.pallas_call(
        body, name="pack_small",
        out_shape=[SDS((1, width), F32), SDS((N_DEV, B, ada_cols), F32)],
    )(acc_in, acc_mix, acc_mlp, dsink, daw, dlb, dgw, lb_p)


def _adamw_small(parts, given):
    names = [n for n, _ in _SMALL]
    flat_in = [a for n in names for a in given[n]]

    def body(*refs):
        p_ref = refs[0]
        in_refs = refs[1:1 + 3 * len(names)]
        out_refs = refs[1 + 3 * len(names):-1]
        loss_ref = refs[-1]
        g = p_ref[0]
        for s in range(1, N_DEV):
            g = g + p_ref[s]
        off = 0
        for i, (name, width) in enumerate(_SMALL):
            w_ref, m_ref, v_ref = in_refs[3 * i:3 * i + 3]
            rows, cols = w_ref.shape
            for r in range(rows):
                gr = g[:, off + r * cols:off + (r + 1) * cols]
                res = (gr,) + _adamw_math(w_ref[r:r + 1, :], gr, m_ref[r:r + 1, :], v_ref[r:r + 1, :])
                for o_ref, val in zip(out_refs[4 * i:4 * i + 4], res):
                    o_ref[r:r + 1, :] = val
            off += width
        loss_ref[...] = g[:, off:off + LANES]

    out_shape = [SDS(given[n][0].shape, F32) for n in names for _ in range(4)] + [SDS((1, LANES), F32)]
    outs = pl.pallas_call(body, name="adamw_small", out_shape=out_shape)(parts, *flat_in)
    return {n: tuple(outs[4 * i:4 * i + 4]) for i, n in enumerate(names)}, outs[-1][0, 0]


def kernel(x, c, w_ada, b_ada, pre_w_mix, w_in, attn_sinks, attn_out_w, lb_table, hg_norm_w, w_out, post_w_mix, pre_w_mlp, w_up, w_down, post_w_mlp, loss_target, m_w_ada, m_b_ada, m_pre_w_mix, m_w_in, m_attn_sinks, m_attn_out_w, m_lb_table, m_hg_norm_w, m_w_out, m_post_w_mix, m_pre_w_mlp, m_w_up, m_w_down, m_post_w_mlp, v_w_ada, v_b_ada, v_pre_w_mix, v_w_in, v_attn_sinks, v_attn_out_w, v_lb_table, v_hg_norm_w, v_w_out, v_post_w_mix, v_pre_w_mlp, v_w_up, v_w_down, v_post_w_mlp):
    B, T, _ = x.shape
    N = B * T
    me = 4 * lax.axis_index("x") + 2 * lax.axis_index("y") + lax.axis_index("c")
    x2 = x.reshape(N, D_MODEL)
    tgt2 = loss_target.reshape(N, D_MODEL)

    w_in_t, m_w_in_t, v_w_in_t = w_in[0].T, m_w_in[0].T, v_w_in[0].T
    w_in_g, c_g = _exchange("gather_w_in", [_bf(w_in_t), c], ["gather"] * 2)
    w_in_f = w_in_g.reshape(IN_COLS, D_MODEL)
    c_all = c_g.reshape(N_DEV * B, D_MODEL)

    ada_cols = w_ada.shape[2]
    b_mine = lax.dynamic_slice(b_ada, (0, me * ada_cols), (1, ada_cols))
    mod_cols = _ada_mod(c_all, w_ada[0], b_mine)
    (mod_g,) = _exchange("scatter_mod", [mod_cols.reshape(N_DEV, B, ada_cols)], ["a2a"])
    mod = mod_g.transpose(1, 0, 2).reshape(B, 6, D_MODEL)
    mod8 = jnp.pad(mod, ((0, 0), (0, 2), (0, 0)))

    lb_p = jax.nn.softmax(lb_table, axis=0)
    lb = lb_p[1:2]
    tables = _rope_tables(T)

    w_up_b, w_down_b = _bf(w_up[0]), _bf(w_down[0])
    proj_a, proj_h, h1, w_out_g, w_up_g0 = _in_proj(x2, mod8, pre_w_mix, w_in_f, T,
                                                    [_bf(w_out[0]), w_up_b[:MLP_HALF]], ["gather"] * 2)
    proj3 = proj_a.reshape(B, T, ATT_COLS)
    proj_h = proj_h.reshape(B, T, IN_COLS - ATT_COLS)
    rec_o, rec_g, s_prev, w_up_g1 = _hgrn_fwd(proj_h, lb, hg_norm_w, [w_up_b[MLP_HALF:]], ["gather"])
    attn_o, attn_n, qr, kr, w_down_g0 = _attn_fwd(proj3, tables, attn_sinks, attn_out_w,
                                                  [w_down_b[:, :MLP_HALF]], ["gather"])
    w_out_f = w_out_g.reshape(D_MODEL, D_MODEL)
    mix, x1, cat, w_down_g1 = _mix_out(x2, attn_n.reshape(N, ATT_WIDTH), rec_g.reshape(N, HG_WIDTH), mod8,
                                       post_w_mix, w_out_f, T, [w_down_b[:, MLP_HALF:]], ["gather"])
    w_up_halves = [w_up_g0, w_up_g1]
    w_down_halves = [w_down_g0.reshape(D_FF, MLP_HALF), w_down_g1.reshape(D_FF, MLP_HALF)]
    up, u, d, h2 = _mlp_fwd(x1, mod8, pre_w_mlp, w_up_halves, w_down_halves, T)

    dx1, dup, dd, acc_mlp = _mlp_bwd(x1, d, up, tgt2, mod8, pre_w_mlp, post_w_mlp, w_up_halves, w_down_halves, T)
    chips = N_DEV // 2
    by_chip = lambda a: a.reshape((chips, 2, a.shape[0] // N_DEV) + a.shape[1:])
    gw_up = _matmul_tn("grad_w_up", h2, dup, D_FF // N_DEV, tm=D_MODEL, by_owner_cols=True)
    gw_up = gw_up.reshape(chips, 2, D_MODEL, D_FF // N_DEV)
    gw_down = by_chip(_matmul_tn("grad_w_down", u, dd, D_MODEL))
    dan, drg, dmix, acc_mix, q_down, q_up = _mix_bwd(mix, dx1, mod8, post_w_mix, w_out_f, T,
                                                     [gw_down, gw_up], ["pair"] * 2)
    p_down, p_up = _pair_add("pair_add_w_down", gw_down, q_down), _pair_add("pair_add_w_up", gw_up, q_up)
    gw_out = _matmul_tn("grad_w_out", cat, dmix, 512).reshape(N_DEV, D_MODEL // N_DEV, D_MODEL)
    dhq, dhf, dhi, dhg, dlb_p, dgw_p, r_down, r_up = _hgrn_bwd(
        proj_h, lb, hg_norm_w, rec_o, s_prev, drg.reshape(B, T, HG_WIDTH), [p_down, p_up], ["chips"] * 2)
    dqkv, dsink_p, daw_p, r_out = _attn_bwd(qr, kr, proj3, attn_o, dan.reshape(B, T, ATT_WIDTH), tables,
                                            attn_sinks, attn_out_w, [gw_out], ["a2a"])
    flat = lambda a: a.reshape(N, a.shape[-1])
    grad_x, dproj, acc_in = _in_bwd(x2, dx1, flat(dqkv), flat(dhq), flat(dhf), flat(dhi), flat(dhg),
                                    mod8, pre_w_mix, w_in_f, T, [], [])

    gw_in = by_chip(_matmul_tn("grad_w_in", dproj, h1, D_MODEL, tm=IN_COLS // 2))
    (q_in,) = _exchange("pair_w_in", [gw_in], ["pair"])
    p_in = _pair_add("pair_add_w_in", gw_in, q_in)

    packed, dmod_blocks = _pack_small(acc_in, acc_mix, acc_mlp, dsink_p, daw_p, dlb_p, dgw_p, lb_p, ada_cols)
    r_in, r_dmod, r_small = _exchange("reduce_grads", [p_in, dmod_blocks, packed], ["chips", "a2a", "gather"])

    res = {}
    res["w_in"] = tuple(a.T for a in _reduce_adamw("adamw_w_in", r_in, w_in_t, m_w_in_t, v_w_in_t))
    res["w_out"] = _reduce_adamw("adamw_w_out", r_out, w_out[0], m_w_out[0], v_w_out[0])
    res["w_up"] = _reduce_adamw("adamw_w_up", r_up, w_up[0], m_w_up[0], v_w_up[0])
    res["w_down"] = _reduce_adamw("adamw_w_down", r_down, w_down[0], m_w_down[0], v_w_down[0])
    res["w_ada"] = _ada_grad_adamw(c_all, r_dmod.reshape(N_DEV * B, ada_cols), w_ada[0], m_w_ada[0], v_w_ada[0])

    given = dict(b_ada=(b_ada, m_b_ada, v_b_ada), pre_w_mix=(pre_w_mix, m_pre_w_mix, v_pre_w_mix),
                 attn_sinks=(attn_sinks, m_attn_sinks, v_attn_sinks),
                 attn_out_w=(attn_out_w, m_attn_out_w, v_attn_out_w), lb_table=(lb_table, m_lb_table, v_lb_table),
                 hg_norm_w=(hg_norm_w, m_hg_norm_w, v_hg_norm_w), post_w_mix=(post_w_mix, m_post_w_mix, v_post_w_mix),
                 pre_w_mlp=(pre_w_mlp, m_pre_w_mlp, v_pre_w_mlp), post_w_mlp=(post_w_mlp, m_post_w_mlp, v_post_w_mlp))
    small_res, loss = _adamw_small(r_small, given)
    res.update(small_res)

    order = ["w_ada", "b_ada", "pre_w_mix", "w_in", "attn_sinks", "attn_out_w", "lb_table", "hg_norm_w", "w_out",
             "post_w_mix", "pre_w_mlp", "w_up", "w_down", "post_w_mlp"]
    big = {"w_ada", "w_in", "w_out", "w_up", "w_down"}
    outs = [loss, grad_x.reshape(B, T, D_MODEL)]
    for i in range(4):
        for k in order:
            a = res[k][i]
            outs.append(a[None] if k in big else a)
    return tuple(outs)
```

```python
import jax
import jax.numpy as jnp
import numpy as np
from jax import lax
from jax.experimental import pallas as pl
from jax.experimental.pallas import tpu as pltpu

F32 = jnp.float32
BF16 = jnp.bfloat16
SDS = jax.ShapeDtypeStruct

D_MODEL = 1024
ATT_WIDTH = 512
ATT_HEAD_DIM = 64
ATT_KV_HEADS = 2
ATT_GROUP = 4
WINDOW = 128
ROPE_DIM = 16
ROPE_THETA = 500000.0
HG_WIDTH = 512
HG_HEAD_DIM = 128
HG_HEADS = 4
HG_CHUNK = 32
IN_COLS = 2816
ATT_COLS = 768
D_FF = 4096
EPS = 1e-6
N_DEV = 8

ADAM_LR = 0.001
ADAM_B1 = 0.9
ADAM_B2 = 0.999
ADAM_EPS = 1e-08
ADAM_WD = 0.01
ADAM_STEP = 10

VMEM_LIMIT_BIG = 56 << 20
LANES = 128

MESH = pl.DeviceIdType.MESH
NT_DIMS = (((1,), (1,)), ((), ()))
TN_DIMS = (((0,), (0,)), ((), ()))


def _dot(a, b):
    return jnp.dot(a, b, preferred_element_type=F32)


def _dot_nt(a, b):
    return lax.dot_general(a, b, NT_DIMS, preferred_element_type=F32)


def _dot_tn(a, b):
    return lax.dot_general(a, b, TN_DIMS, preferred_element_type=F32)


def _bf(a):
    return a.astype(BF16)


def _sigmoid(a):
    return 0.5 * jnp.tanh(0.5 * a) + 0.5


def _mean_last(a):
    return jnp.mean(a, axis=-1, keepdims=True)


def _sum_rows(a):
    return jnp.sum(a, axis=0, keepdims=True)


def _loop_pairs(first, count, body, init, per_trip=2):
    if count % per_trip:
        return lax.fori_loop(first, first + count, body, init)

    def trip(i, c):
        for k in range(per_trip):
            c = body(first + per_trip * i + k, c)
        return c

    return lax.fori_loop(0, count // per_trip, trip, init)


def _params(sem=None, vmem=None):
    kw = {}
    if sem is not None:
        kw["dimension_semantics"] = sem
    if vmem is not None:
        kw["vmem_limit_bytes"] = vmem
    return pltpu.CompilerParams(**kw)


ANY_SPEC = pl.BlockSpec(memory_space=pl.ANY)


def _exchange_shapes(srcs, modes):
    out_shape = []
    for s, m in zip(srcs, modes):
        shp = {"gather": (N_DEV,) + tuple(s.shape), "pair": (s.shape[0],) + tuple(s.shape[2:])}.get(m, tuple(s.shape))
        out_shape.append(SDS(shp, s.dtype))
    return out_shape


def _exchange_sems(n):
    if n == 0:
        return []
    return [pltpu.SemaphoreType.DMA((n, N_DEV - 1)), pltpu.SemaphoreType.DMA((n, N_DEV - 1)),
            pltpu.SemaphoreType.DMA((n,))]


SIBLING = 1
OTHER_CHIPS = (2, 4, 6)


def _related(k):
    x, y, c = lax.axis_index("x"), lax.axis_index("y"), lax.axis_index("c")
    px, py, pc = x ^ ((k >> 2) & 1), y ^ ((k >> 1) & 1), c ^ (k & 1)
    return (px, py, pc), 4 * px + 2 * py + pc


def _exchange_phases(modes, src_refs, out_refs, send_sems, recv_sems, own_sems):
    _, me = _related(0)
    sib_dev, sib = _related(SIBLING)
    start, middle, end = [], [], []

    def remote(a, i, src, dst, dev):
        return pltpu.make_async_remote_copy(src_ref=src, dst_ref=dst, send_sem=send_sems.at[a, i],
                                            recv_sem=recv_sems.at[a, i], device_id=dev, device_id_type=MESH)

    for a, mode in enumerate(modes):
        out = out_refs[a]
        if mode == "gather":
            src = src_refs[a]
            own = pltpu.make_async_copy(src, out.at[me], own_sems.at[a])
            to_sib = remote(a, 0, src, out.at[me], sib_dev)
            start += [own.start, to_sib.start]
            end += [remote(a, 0, src, out.at[sib], sib_dev).wait_recv, to_sib.wait_send, own.wait]
            for j, k in enumerate(OTHER_CHIPS, start=1):
                dev, peer = _related(k)
                _, peer_sib = _related(k ^ SIBLING)
                send = remote(a, j, src, out.at[me], dev)
                passed = remote(a, 3 + j, out.at[peer], out.at[peer], sib_dev)
                start.append(send.start)
                middle += [remote(a, j, src, out.at[peer], dev).wait_recv, passed.start]
                end += [remote(a, 3 + j, out.at[peer_sib], out.at[peer_sib], sib_dev).wait_recv,
                        send.wait_send, passed.wait_send]
        elif mode == "pair":
            core = lax.axis_index("c")
            for s in range(N_DEV // 2):
                send = remote(a, s, src_refs[a].at[s, 1 - core], out.at[s], sib_dev)
                start.append(send.start)
                end += [remote(a, s, src_refs[a].at[s, 1 - core], out.at[s], sib_dev).wait_recv, send.wait_send]
        elif mode == "chips":
            chip = me // 2
            own = pltpu.make_async_copy(src_refs[a].at[chip], out.at[chip], own_sems.at[a])
            start.append(own.start)
            end.append(own.wait)
            for j, k in enumerate(OTHER_CHIPS, start=1):
                dev, peer = _related(k)
                send = remote(a, j, src_refs[a].at[peer // 2], out.at[chip], dev)
                start.append(send.start)
                end += [remote(a, j, src_refs[a].at[peer // 2], out.at[peer // 2], dev).wait_recv, send.wait_send]
        else:
            own = pltpu.make_async_copy(src_refs[a].at[me], out.at[me], own_sems.at[a])
            start.append(own.start)
            end.append(own.wait)
            for k in range(1, N_DEV):
                dev, peer = _related(k)
                send = remote(a, k - 1, src_refs[a].at[peer], out.at[me], dev)
                start.append(send.start)
                end += [remote(a, k - 1, src_refs[a].at[peer], out.at[peer], dev).wait_recv, send.wait_send]
    return start, middle, end


def _run(actions):
    for act in actions:
        act()


def _exchange(name, srcs, modes):
    n = len(srcs)

    def body(*refs):
        start, middle, end = _exchange_phases(modes, refs[:n], refs[n:2 * n], *refs[2 * n:])
        _run(start)
        _run(middle)
        _run(end)

    return pl.pallas_call(
        body, name=name, out_shape=_exchange_shapes(srcs, modes),
        in_specs=[ANY_SPEC] * n, out_specs=[ANY_SPEC] * n,
        scratch_shapes=_exchange_sems(n),
    )(*srcs)


def _ride_start(modes, step, steps, src_refs, out_refs, sems):
    if not modes:
        return
    middle_step = steps - 1

    @pl.when(step == 0)
    def _():
        _run(_exchange_phases(modes, src_refs, out_refs, *sems)[0])

    if "gather" in modes:
        @pl.when(step == middle_step)
        def _():
            _run(_exchange_phases(modes, src_refs, out_refs, *sems)[1])


def _ride_wait(modes, step, steps, src_refs, out_refs, sems):
    if not modes:
        return

    @pl.when(step == steps - 1)
    def _():
        _run(_exchange_phases(modes, src_refs, out_refs, *sems)[2])


def _ada_mod(c_all, w_ada, b_ada_mine):
    nb, cols = c_all.shape[0], w_ada.shape[1]

    def body(c_ref, w_ref, b_ref, o_ref):
        cv = c_ref[...]
        ca = cv * _sigmoid(cv)
        o_ref[...] = _dot(ca, w_ref[...]) + b_ref[...]

    return pl.pallas_call(body, name="ada_mod", out_shape=SDS((nb, cols), F32))(c_all, w_ada, b_ada_mine)


def _tile_rows(T, big=False):
    return min(512 if big else 256, T)


def _mod_spec(tps):
    return pl.BlockSpec((None, 8, D_MODEL), lambda i: (i // tps, 0, 0))


def _in_proj(x2, mod8, pre_w, w_in_bf, T, ride_srcs, ride_modes):
    N = x2.shape[0]
    TM = _tile_rows(T, big=True)
    tps = T // TM
    nr = len(ride_srcs)

    def body(*refs):
        x_ref, mod_ref, pw_ref, w_ref = refs[:4]
        ride_in = refs[4:4 + nr]
        pa_ref, ph_ref, h1_ref = refs[4 + nr:7 + nr]
        ride_out = refs[7 + nr:7 + 2 * nr]
        sems = refs[7 + 2 * nr:]
        _ride_start(ride_modes, pl.program_id(0), N // TM, ride_in, ride_out, sems)
        x = x_ref[...]
        r = lax.rsqrt(_mean_last(x * x) + EPS)
        h = (x * r * pw_ref[...]) * (1.0 + mod_ref[1:2, :]) + mod_ref[0:1, :]
        hb = _bf(h)
        h1_ref[...] = hb
        pa_ref[...] = _dot_nt(hb, w_ref[:ATT_COLS, :])
        ph_ref[...] = _dot_nt(hb, w_ref[ATT_COLS:, :])
        _ride_wait(ride_modes, pl.program_id(0), N // TM, ride_in, ride_out, sems)

    return pl.pallas_call(
        body, name="in_proj", grid=(N // TM,),
        in_specs=[pl.BlockSpec((TM, D_MODEL), lambda i: (i, 0)), _mod_spec(tps),
                  pl.BlockSpec((1, D_MODEL), lambda i: (0, 0)),
                  pl.BlockSpec((IN_COLS, D_MODEL), lambda i: (0, 0))] + [ANY_SPEC] * nr,
        out_specs=[pl.BlockSpec((TM, ATT_COLS), lambda i: (i, 0)),
                   pl.BlockSpec((TM, IN_COLS - ATT_COLS), lambda i: (i, 0)),
                   pl.BlockSpec((TM, D_MODEL), lambda i: (i, 0))] + [ANY_SPEC] * nr,
        out_shape=[SDS((N, ATT_COLS), F32), SDS((N, IN_COLS - ATT_COLS), F32), SDS((N, D_MODEL), BF16)]
        + _exchange_shapes(ride_srcs, ride_modes),
        scratch_shapes=_exchange_sems(nr),
        compiler_params=_params(("arbitrary",), VMEM_LIMIT_BIG),
    )(x2, mod8, pre_w, w_in_bf, *ride_srcs)


def _rope_tables(T):
    half = ROPE_DIM // 2
    f32 = np.float32
    inv_freq = (f32(ROPE_THETA) ** (-np.arange(0, ROPE_DIM, 2, dtype=f32) / f32(ROPE_DIM))).astype(f32)
    ang = np.arange(T, dtype=f32)[:, None] * inv_freq[None, :]
    cos, sin = np.cos(ang).astype(f32), np.sin(ang).astype(f32)
    ones = np.ones((T, ATT_HEAD_DIM - ROPE_DIM), f32)
    zeros = np.zeros((T, ATT_HEAD_DIM - ROPE_DIM), f32)
    zh = np.zeros((T, half), f32)
    cos64 = np.concatenate([cos, cos, ones], axis=1)
    sin_left = np.concatenate([-sin, zh, zeros], axis=1)
    sin_right = np.concatenate([zh, sin, zeros], axis=1)
    rep = LANES // ATT_HEAD_DIM
    return tuple(jnp.asarray(np.tile(t, (1, rep))) for t in (cos64, sin_left, sin_right))


def _rope(xc, cs, sl, sr):
    return xc * cs + pltpu.roll(xc, LANES - 8, 1) * sl + pltpu.roll(xc, 8, 1) * sr


def _rope_t(dy, cs, sl, sr):
    return dy * cs + pltpu.roll(dy * sl, 8, 1) + pltpu.roll(dy * sr, LANES - 8, 1)


ATT_SCALE = ATT_HEAD_DIM ** -0.5
ATT_SPLITS = 4


def _lower_mask():
    j = lax.broadcasted_iota(jnp.int32, (WINDOW, ATT_GROUP * WINDOW), 0)
    i = lax.broadcasted_iota(jnp.int32, (WINDOW, ATT_GROUP * WINDOW), 1) & (WINDOW - 1)
    return j <= i


def _sink_row(sink_ref, hk):
    return jnp.concatenate(
        [jnp.full((1, WINDOW), sink_ref[0, ATT_GROUP * hk + g], F32) for g in range(ATT_GROUP)], axis=1)


def _softmax_window(qs, k_cur, k_prev, lower, has_prev, sink):
    s_prev = jnp.where(has_prev, _dot_nt(k_prev, qs), jnp.finfo(F32).min)
    s = jnp.where(lower, _dot_nt(k_cur, qs), s_prev)
    m = jnp.maximum(jnp.max(s, axis=0, keepdims=True), sink)
    p = jnp.exp(s - m)
    es = jnp.exp(sink - m)
    inv = 1.0 / (jnp.sum(p, axis=0, keepdims=True) + es)
    return p, inv, es


def _stack_heads(parts, hk):
    hs = []
    for g in range(ATT_GROUP):
        h = ATT_GROUP * hk + g
        hs.append(parts[h // 2][:, (h % 2) * ATT_HEAD_DIM:(h % 2 + 1) * ATT_HEAD_DIM])
    return jnp.concatenate(hs, axis=0)


def _attn_fwd(proj3, tables, sinks, attn_w, ride_srcs, ride_modes):
    B, T, _ = proj3.shape
    nb = T // WINDOW
    splits = min(ATT_SPLITS, nb)
    per = nb // splits
    nr = len(ride_srcs)
    cos, sinl, sinr = tables

    def body(*refs):
        q_ref, k_ref, v_ref, cos_ref, sl_ref, sr_ref, sink_ref, aw_ref = refs[:8]
        ride_in = refs[8:8 + nr]
        o_ref, an_ref, qr_ref, kr_ref = refs[8 + nr:12 + nr]
        ride_out = refs[12 + nr:12 + 2 * nr]
        kpad, vpad = refs[12 + 2 * nr:14 + 2 * nr]
        sems = refs[14 + 2 * nr:]
        part = pl.program_id(1)
        step = pl.program_id(0) * splits + part
        _ride_start(ride_modes, step, B * splits, ride_in, ride_out, sems)

        @pl.when(part == 0)
        def _():
            kpad[0:WINDOW, :] = jnp.zeros((WINDOW, LANES), BF16)
            vpad[0:WINDOW, :] = jnp.zeros((WINDOW, LANES), BF16)

        lower = _lower_mask()

        def block(n, carry):
            r0 = pl.multiple_of(n * WINDOW, WINDOW)
            rows = pl.ds(r0, WINDOW)
            nxt = pl.ds(r0 + WINDOW, WINDOW)
            cs, sl, sr = cos_ref[rows, :], sl_ref[rows, :], sr_ref[rows, :]
            kb = _bf(_rope(k_ref[rows, :], cs, sl, sr))
            vb = _bf(v_ref[rows, :])
            kpad[nxt, :] = kb
            kr_ref[rows, :] = kb
            vpad[nxt, :] = vb
            qparts = []
            for j in range(ATT_WIDTH // LANES):
                qp = _bf(_rope(q_ref[rows, j * LANES:(j + 1) * LANES], cs, sl, sr) * ATT_SCALE)
                qr_ref[rows, j * LANES:(j + 1) * LANES] = qp
                qparts.append(qp)
            for hk in range(ATT_KV_HEADS):
                lanes = slice(hk * ATT_HEAD_DIM, (hk + 1) * ATT_HEAD_DIM)
                qs = _stack_heads(qparts, hk)
                p, inv, _ = _softmax_window(qs, kb[:, lanes], kpad[rows, lanes], lower, n > 0,
                                            _sink_row(sink_ref, hk))
                p_cur = jnp.where(lower, p, 0.0)
                ot = (_dot_tn(vb[:, lanes], _bf(p_cur)) + _dot_tn(vpad[rows, lanes], _bf(p - p_cur))) * inv
                for g in range(ATT_GROUP):
                    h = ATT_GROUP * hk + g
                    o_ref[rows, h * ATT_HEAD_DIM:(h + 1) * ATT_HEAD_DIM] = ot[:, g * WINDOW:(g + 1) * WINDOW].T
            ob = o_ref[rows, :]
            an_ref[rows, :] = _bf(ob * lax.rsqrt(_mean_last(ob * ob) + EPS) * aw_ref[...])
            return carry

        _loop_pairs(part * per, per, block, 0)
        _ride_wait(ride_modes, step, B * splits, ride_in, ride_out, sems)

    seq = lambda w, j: pl.BlockSpec((None, T, w), lambda b, s: (b, 0, j))
    full = lambda r, w: pl.BlockSpec((r, w), lambda b, s: (0, 0))
    return pl.pallas_call(
        body, name="attn_fwd", grid=(B, splits),
        in_specs=[seq(ATT_WIDTH, 0), seq(LANES, 4), seq(LANES, 5),
                  full(T, LANES), full(T, LANES), full(T, LANES),
                  pl.BlockSpec(memory_space=pltpu.SMEM), full(1, ATT_WIDTH)] + [ANY_SPEC] * nr,
        out_specs=[seq(ATT_WIDTH, 0), seq(ATT_WIDTH, 0), seq(ATT_WIDTH, 0), seq(LANES, 0)] + [ANY_SPEC] * nr,
        out_shape=[SDS((B, T, ATT_WIDTH), F32), SDS((B, T, ATT_WIDTH), BF16),
                   SDS((B, T, ATT_WIDTH), BF16), SDS((B, T, LANES), BF16)] + _exchange_shapes(ride_srcs, ride_modes),
        scratch_shapes=[pltpu.VMEM((T + WINDOW, LANES), BF16), pltpu.VMEM((T + WINDOW, LANES), BF16)]
        + _exchange_sems(nr),
        compiler_params=_params(("arbitrary", "arbitrary"), VMEM_LIMIT_BIG),
    )(proj3, proj3, proj3, cos, sinl, sinr, sinks, attn_w, *ride_srcs)


HG_GROUP = 8
HG_ROWS = HG_GROUP * HG_CHUNK


HG_STACK = HG_GROUP * HG_HEAD_DIM


def _group_mask():
    r = lax.broadcasted_iota(jnp.int32, (HG_ROWS, HG_ROWS), 0)
    c = lax.broadcasted_iota(jnp.int32, (HG_ROWS, HG_ROWS), 1)
    return ((r // HG_CHUNK) == (c // HG_CHUNK)) & (r >= c)


def _spread(a):
    blocks = []
    for c in range(HG_GROUP):
        above = jnp.zeros((c * HG_CHUNK, HG_HEAD_DIM), a.dtype)
        below = jnp.zeros(((HG_GROUP - 1 - c) * HG_CHUNK, HG_HEAD_DIM), a.dtype)
        blocks.append(jnp.concatenate([p for p in (above, a[_chunk_rows(c), :], below) if p.shape[0]], axis=0))
    return jnp.concatenate(blocks, axis=1)


def _pick(r):
    return jnp.concatenate([r[_chunk_rows(c), c * HG_HEAD_DIM:(c + 1) * HG_HEAD_DIM] for c in range(HG_GROUP)], axis=0)


def _lane_block(a, c):
    return a[:, c * HG_HEAD_DIM:(c + 1) * HG_HEAD_DIM]


def _chunk_cumsum(a, reverse=False):
    n = a.shape[0]
    pos = lax.broadcasted_iota(jnp.int32, a.shape, 0) % HG_CHUNK
    shift = 1
    while shift < HG_CHUNK:
        if reverse:
            a = a + jnp.where(pos < HG_CHUNK - shift, pltpu.roll(a, n - shift, 0), 0.0)
        else:
            a = a + jnp.where(pos >= shift, pltpu.roll(a, shift, 0), 0.0)
        shift *= 2
    return a


def _chunk_bcast(rows_1x128):
    return jnp.concatenate([jnp.broadcast_to(r, (HG_CHUNK, HG_HEAD_DIM)) for r in rows_1x128], axis=0)


def _hgrn_gates(hq, hf, lb):
    sq = _sigmoid(hq)
    q = hq * sq
    sg = _sigmoid(hf)
    f = lb + (1.0 - lb) * sg
    k = 1.0 - f
    logf = jnp.log(f)
    b = _chunk_cumsum(logf)
    bl = [_sum_rows(logf[_chunk_rows(c), :]) for c in range(HG_GROUP)]
    eb, enb, e2 = jnp.exp(b), jnp.exp(-b), jnp.exp(_chunk_bcast(bl) - b)
    ebl = [jnp.exp(r) for r in bl]
    return dict(sq=sq, sg=sg, f=f, eb=eb, enb=enb, e2=e2, ebl=ebl, qd=q * eb, kd=k * enb, k2=k * e2)


def _chunk_rows(c):
    return slice(c * HG_CHUNK, (c + 1) * HG_CHUNK)


def _head_lanes(h):
    return slice(h * HG_HEAD_DIM, (h + 1) * HG_HEAD_DIM)


def _hgrn_fwd(proj_h, lb, hg_w, ride_srcs, ride_modes):
    B, T, _ = proj_h.shape
    ng = T // HG_ROWS
    nr = len(ride_srcs)

    def body(*refs):
        hq_ref, hf_ref, hi_ref, hg_ref, lb_ref, gw_ref = refs[:6]
        ride_in = refs[6:6 + nr]
        o_ref, rg_ref, sp_ref = refs[6 + nr:9 + nr]
        ride_out = refs[9 + nr:9 + 2 * nr]
        st = refs[9 + 2 * nr]
        sems = refs[10 + 2 * nr:]
        gi = pl.program_id(1)
        step = pl.program_id(0) * ng + gi
        _ride_start(ride_modes, step, B * ng, ride_in, ride_out, sems)

        @pl.when(gi == 0)
        def _():
            st[...] = jnp.zeros(st.shape, F32)

        lo = _group_mask()
        for h in range(HG_HEADS):
            lanes = _head_lanes(h)
            gt = _hgrn_gates(hq_ref[:, lanes], hf_ref[:, lanes], lb_ref[:, lanes])
            v, qd, kd = _bf(hi_ref[:, lanes]), _bf(gt["qd"]), _bf(gt["kd"])
            a = jnp.where(lo, _dot_nt(qd, kd), 0.0)
            kv = _dot_tn(v, _spread(_bf(gt["k2"])))
            s = st[h]
            before = []
            for c in range(HG_GROUP):
                before.append(s)
                s = s * gt["ebl"][c] + _lane_block(kv, c)
            st[h] = s
            sp = jnp.concatenate(before, axis=1)
            sp_ref[h] = sp
            o = _dot(_bf(a), v) + _dot_nt(_spread(qd), _bf(sp))
            o_ref[:, lanes] = o
            hg = hg_ref[:, lanes]
            rn = o * lax.rsqrt(_mean_last(o * o) + EPS) * gw_ref[...]
            rg_ref[:, lanes] = _bf(rn * (hg * _sigmoid(hg)))
        _ride_wait(ride_modes, step, B * ng, ride_in, ride_out, sems)

    part = lambda j: pl.BlockSpec((None, HG_ROWS, HG_WIDTH), lambda b, g: (b, g, j))
    return pl.pallas_call(
        body, name="hgrn_fwd", grid=(B, ng),
        in_specs=[part(0), part(1), part(2), part(3),
                  pl.BlockSpec((1, HG_WIDTH), lambda b, g: (0, 0)),
                  pl.BlockSpec((1, LANES), lambda b, g: (0, 0))] + [ANY_SPEC] * nr,
        out_specs=[part(0), part(0),
                   pl.BlockSpec((None, HG_HEADS, None, HG_HEAD_DIM, HG_STACK), lambda b, g: (b, 0, g, 0, 0))]
        + [ANY_SPEC] * nr,
        out_shape=[SDS((B, T, HG_WIDTH), F32), SDS((B, T, HG_WIDTH), BF16),
                   SDS((B, HG_HEADS, ng, HG_HEAD_DIM, HG_STACK), F32)] + _exchange_shapes(ride_srcs, ride_modes),
        scratch_shapes=[pltpu.VMEM((HG_HEADS, HG_HEAD_DIM, HG_HEAD_DIM), F32)] + _exchange_sems(nr),
        compiler_params=_params(("arbitrary", "arbitrary"), VMEM_LIMIT_BIG),
    )(proj_h, proj_h, proj_h, proj_h, lb, hg_w, *ride_srcs)


def _mix_out(x2, attn_n, rec_g, mod8, post_w, w_out_bf, T, ride_srcs, ride_modes):
    N = x2.shape[0]
    TM = _tile_rows(T, big=True)
    tps = T // TM
    nr = len(ride_srcs)

    def body(*refs):
        x_ref, an_ref, rg_ref, mod_ref, pw_ref, w_ref = refs[:6]
        ride_in = refs[6:6 + nr]
        mix_ref, x1_ref, cat_ref = refs[6 + nr:9 + nr]
        ride_out = refs[9 + nr:9 + 2 * nr]
        sems = refs[9 + 2 * nr:]
        _ride_start(ride_modes, pl.program_id(0), N // TM, ride_in, ride_out, sems)
        cat = jnp.concatenate([an_ref[...], rg_ref[...]], axis=1)
        cat_ref[...] = cat
        mix = _dot(cat, w_ref[...])
        mix_ref[...] = mix
        r = lax.rsqrt(_mean_last(mix * mix) + EPS)
        x1_ref[...] = x_ref[...] + mod_ref[2:3, :] * (mix * r * pw_ref[...])
        _ride_wait(ride_modes, pl.program_id(0), N // TM, ride_in, ride_out, sems)

    row = lambda w: pl.BlockSpec((TM, w), lambda i: (i, 0))
    return pl.pallas_call(
        body, name="mix_out", grid=(N // TM,),
        in_specs=[row(D_MODEL), row(ATT_WIDTH), row(HG_WIDTH), _mod_spec(tps),
                  pl.BlockSpec((1, D_MODEL), lambda i: (0, 0)),
                  pl.BlockSpec((D_MODEL, D_MODEL), lambda i: (0, 0))] + [ANY_SPEC] * nr,
        out_specs=[row(D_MODEL), row(D_MODEL), row(D_MODEL)] + [ANY_SPEC] * nr,
        out_shape=[SDS((N, D_MODEL), F32), SDS((N, D_MODEL), F32), SDS((N, D_MODEL), BF16)]
        + _exchange_shapes(ride_srcs, ride_modes),
        scratch_shapes=_exchange_sems(nr),
        compiler_params=_params(("arbitrary",), VMEM_LIMIT_BIG),
    )(x2, attn_n, rec_g, mod8, post_w, w_out_bf, *ride_srcs)


def _load_weights_once(pairs, sem):
    @pl.when(pl.program_id(0) == 0)
    def _():
        cps = [pltpu.make_async_copy(src, dst, sem.at[i]) for i, (src, dst) in enumerate(pairs)]
        for cp in cps:
            cp.start()
        for cp in cps:
            cp.wait()


MLP_HALF = D_MODEL // 2
MLP_PIECES = 2 * N_DEV + 2


def _mlp_weight_pieces(wu_a, wu_b, wd_a, wd_b, wu, wd):
    cols = D_FF // N_DEV
    pairs = []
    for h, half in enumerate((wu_a, wu_b)):
        for j in range(N_DEV):
            pairs.append((half.at[j], wu.at[pl.ds(h * MLP_HALF, MLP_HALF), pl.ds(j * cols, cols)]))
    for h, half in enumerate((wd_a, wd_b)):
        pairs.append((half, wd.at[:, pl.ds(h * MLP_HALF, MLP_HALF)]))
    return pairs


def _mlp_fwd(x1, mod8, pre_w, w_up_halves, w_down_halves, T):
    N = x1.shape[0]
    TM = _tile_rows(T)
    tps = T // TM

    def body(x_ref, mod_ref, pw_ref, wua, wub, wda, wdb, up_ref, u_ref, d_ref, h2_ref, wu, wd, sem):
        _load_weights_once(_mlp_weight_pieces(wua, wub, wda, wdb, wu, wd), sem)
        x = x_ref[...]
        r = lax.rsqrt(_mean_last(x * x) + EPS)
        h = (x * r * pw_ref[...]) * (1.0 + mod_ref[4:5, :]) + mod_ref[3:4, :]
        hb = _bf(h)
        h2_ref[...] = hb
        up = _dot(hb, wu[...])
        up_ref[...] = up
        ru = jnp.maximum(up, 0.0)
        u = _bf(ru * ru)
        u_ref[...] = u
        d_ref[...] = _dot(u, wd[...])

    row = lambda w: pl.BlockSpec((TM, w), lambda i: (i, 0))
    return pl.pallas_call(
        body, name="mlp_fwd", grid=(N // TM,),
        in_specs=[row(D_MODEL), _mod_spec(tps), pl.BlockSpec((1, D_MODEL), lambda i: (0, 0))] + [ANY_SPEC] * 4,
        out_specs=[row(D_FF), row(D_FF), row(D_MODEL), row(D_MODEL)],
        out_shape=[SDS((N, D_FF), F32), SDS((N, D_FF), BF16), SDS((N, D_MODEL), F32), SDS((N, D_MODEL), BF16)],
        scratch_shapes=[pltpu.VMEM((D_MODEL, D_FF), BF16), pltpu.VMEM((D_FF, D_MODEL), BF16),
                        pltpu.SemaphoreType.DMA((MLP_PIECES,))],
        compiler_params=_params(("arbitrary",), VMEM_LIMIT_BIG),
    )(x1, mod8, pre_w, *w_up_halves, *w_down_halves)


def _acc_rows(acc_ref, first, rows):
    @pl.when(first)
    def _():
        acc_ref[...] = jnp.zeros(acc_ref.shape, F32)
    for i, r in enumerate(rows):
        acc_ref[i:i + 1, :] += r


def _mlp_bwd(x1, d, up, tgt, mod8, pre_w, post_w, w_up_halves, w_down_halves, T):
    N = x1.shape[0]
    TM = _tile_rows(T)
    tps = T // TM

    def body(x_ref, d_ref, up_ref, t_ref, mod_ref, pw_ref, qw_ref, wua, wub, wda, wdb,
             dx_ref, dup_ref, dd_ref, acc_ref, wd, wu, sem):
        _load_weights_once(_mlp_weight_pieces(wua, wub, wda, wdb, wu, wd), sem)
        sh2, sc2, g2 = mod_ref[3:4, :], mod_ref[4:5, :], mod_ref[5:6, :]
        x = x_ref[...]
        r1 = lax.rsqrt(_mean_last(x * x) + EPS)
        xh = x * r1
        n2 = xh * pw_ref[...]
        dv = d_ref[...]
        rd = lax.rsqrt(_mean_last(dv * dv) + EPS)
        dh = dv * rd
        rr = dh * qw_ref[...]
        e = x + g2 * rr - t_ref[...]
        loss = 0.5 * jnp.sum(_sum_rows(e * e), axis=1, keepdims=True) / D_MODEL
        dy = e * (1.0 / D_MODEL)
        dg2 = _sum_rows(dy * rr)
        drr = dy * g2
        dw_post = _sum_rows(drr * dh)
        ddh = drr * qw_ref[...]
        dd = _bf(rd * (ddh - dh * _mean_last(ddh * dh)))
        dd_ref[...] = dd
        ru = jnp.maximum(up_ref[...], 0.0)
        dup = _bf(_dot_nt(dd, wd[...]) * (2.0 * ru))
        dup_ref[...] = dup
        dh2 = _dot_nt(dup, wu[...])
        dsh2 = _sum_rows(dh2)
        dsc2 = _sum_rows(dh2 * n2)
        dn2 = dh2 * (1.0 + sc2)
        dw_pre = _sum_rows(dn2 * xh)
        dxh = dn2 * pw_ref[...]
        dx_ref[...] = dy + r1 * (dxh - xh * _mean_last(dxh * xh))
        _acc_rows(acc_ref, pl.program_id(0) % tps == 0,
                  [dsh2, dsc2, dg2, dw_pre, dw_post, jnp.broadcast_to(loss, (1, D_MODEL))])

    row = lambda w: pl.BlockSpec((TM, w), lambda i: (i, 0))
    vec = pl.BlockSpec((1, D_MODEL), lambda i: (0, 0))
    B = N // T
    return pl.pallas_call(
        body, name="mlp_bwd", grid=(N // TM,),
        in_specs=[row(D_MODEL), row(D_MODEL), row(D_FF), row(D_MODEL), _mod_spec(tps), vec, vec] + [ANY_SPEC] * 4,
        out_specs=[row(D_MODEL), row(D_FF), row(D_MODEL), _mod_spec(tps)],
        out_shape=[SDS((N, D_MODEL), F32), SDS((N, D_FF), BF16), SDS((N, D_MODEL), BF16),
                   SDS((B, 8, D_MODEL), F32)],
        scratch_shapes=[pltpu.VMEM((D_FF, D_MODEL), BF16), pltpu.VMEM((D_MODEL, D_FF), BF16),
                        pltpu.SemaphoreType.DMA((MLP_PIECES,))],
        compiler_params=_params(("arbitrary",), VMEM_LIMIT_BIG),
    )(x1, d, up, tgt, mod8, pre_w, post_w, *w_up_halves, *w_down_halves)


def _mix_bwd(mix, dx1, mod8, post_w, w_out_bf, T, ride_srcs, ride_modes):
    N = mix.shape[0]
    TM = _tile_rows(T, big=True)
    tps = T // TM
    nr = len(ride_srcs)

    def body(*refs):
        mix_ref, dx_ref, mod_ref, pw_ref, w_ref = refs[:5]
        ride_in = refs[5:5 + nr]
        dan_ref, drg_ref, dmix_ref, acc_ref = refs[5 + nr:9 + nr]
        ride_out = refs[9 + nr:9 + 2 * nr]
        sems = refs[9 + 2 * nr:]
        _ride_start(ride_modes, pl.program_id(0), N // TM, ride_in, ride_out, sems)
        g1 = mod_ref[2:3, :]
        mix = mix_ref[...]
        dx1 = dx_ref[...]
        rm = lax.rsqrt(_mean_last(mix * mix) + EPS)
        mh = mix * rm
        dg1 = _sum_rows(dx1 * (mh * pw_ref[...]))
        dr = dx1 * g1
        dw_post = _sum_rows(dr * mh)
        dmh = dr * pw_ref[...]
        dmix = _bf(rm * (dmh - mh * _mean_last(dmh * mh)))
        dmix_ref[...] = dmix
        dcat = _dot_nt(dmix, w_ref[...])
        dan_ref[...] = dcat[:, :ATT_WIDTH]
        drg_ref[...] = dcat[:, ATT_WIDTH:]
        _acc_rows(acc_ref, pl.program_id(0) % tps == 0, [dg1, dw_post])
        _ride_wait(ride_modes, pl.program_id(0), N // TM, ride_in, ride_out, sems)

    row = lambda w: pl.BlockSpec((TM, w), lambda i: (i, 0))
    B = N // T
    return pl.pallas_call(
        body, name="mix_bwd", grid=(N // TM,),
        in_specs=[row(D_MODEL), row(D_MODEL), _mod_spec(tps), pl.BlockSpec((1, D_MODEL), lambda i: (0, 0)),
                  pl.BlockSpec((D_MODEL, D_MODEL), lambda i: (0, 0))] + [ANY_SPEC] * nr,
        out_specs=[row(ATT_WIDTH), row(HG_WIDTH), row(D_MODEL), _mod_spec(tps)] + [ANY_SPEC] * nr,
        out_shape=[SDS((N, ATT_WIDTH), F32), SDS((N, HG_WIDTH), F32), SDS((N, D_MODEL), BF16),
                   SDS((B, 8, D_MODEL), F32)] + _exchange_shapes(ride_srcs, ride_modes),
        scratch_shapes=_exchange_sems(nr),
        compiler_params=_params(("arbitrary",), VMEM_LIMIT_BIG),
    )(mix, dx1, mod8, post_w, w_out_bf, *ride_srcs)


def _hgrn_bwd(proj_h, lb, hg_w, o, s_prev, drg, ride_srcs, ride_modes):
    B, T, _ = proj_h.shape
    ng = T // HG_ROWS
    nr = len(ride_srcs)

    def body(*refs):
        hq_ref, hf_ref, hi_ref, hg_ref, lb_ref, gw_ref, o_ref, sp_ref, drg_ref = refs[:9]
        ride_in = refs[9:9 + nr]
        dhq_ref, dhf_ref, dhi_ref, dhg_ref, dlb_ref, dgw_ref = refs[9 + nr:15 + nr]
        ride_out = refs[15 + nr:15 + 2 * nr]
        dst = refs[15 + 2 * nr]
        sems = refs[16 + 2 * nr:]
        step = pl.program_id(0) * ng + pl.program_id(1)
        _ride_start(ride_modes, step, B * ng, ride_in, ride_out, sems)

        @pl.when(pl.program_id(1) == 0)
        def _():
            dst[...] = jnp.zeros(dst.shape, F32)
            dlb_ref[...] = jnp.zeros(dlb_ref.shape, F32)
            dgw_ref[...] = jnp.zeros(dgw_ref.shape, F32)

        lo = _group_mask()
        gw = gw_ref[...]

        for h in range(HG_HEADS):
            lanes = _head_lanes(h)
            lbv = lb_ref[:, lanes]
            hq = hq_ref[:, lanes]
            gt = _hgrn_gates(hq, hf_ref[:, lanes], lbv)
            sq, sg, qdf, kdf, k2f, ebl = gt["sq"], gt["sg"], gt["qd"], gt["kd"], gt["k2"], gt["ebl"]
            v, qd, kd = _bf(hi_ref[:, lanes]), _bf(qdf), _bf(kdf)
            ov = o_ref[:, lanes]
            hg = hg_ref[:, lanes]
            shg = _sigmoid(hg)
            dr = drg_ref[:, lanes]
            ro = lax.rsqrt(_mean_last(ov * ov) + EPS)
            oh = ov * ro
            dhg_ref[:, lanes] = _bf(dr * (oh * gw) * (shg + hg * shg * (1.0 - shg)))
            drn = dr * (hg * shg)
            dgw_ref[...] += jnp.broadcast_to(_sum_rows(drn * oh), (8, LANES))
            doh = drn * gw
            do = _bf(ro * (doh - oh * _mean_last(doh * oh)))
            a = jnp.where(lo, _dot_nt(qd, kd), 0.0)
            da = _bf(jnp.where(lo, _dot_nt(do, v), 0.0))
            dv = _dot_tn(_bf(a), do)
            dqd = _dot(da, kd)
            dkd = _dot_tn(da, qd)
            sp = sp_ref[h]
            incr = _dot_tn(do, _spread(qd))
            ds = dst[h]
            after = [None] * HG_GROUP
            for c in reversed(range(HG_GROUP)):
                after[c] = ds
                ds = ds * ebl[c] + _lane_block(incr, c)
            dst[h] = ds
            dss = jnp.concatenate(after, axis=1)
            dssb = _bf(dss)
            dk2 = _pick(_dot(v, dssb))
            dhi_ref[:, lanes] = _bf(dv + _dot_nt(_spread(_bf(k2f)), dssb))
            dqd = dqd + _pick(_dot(do, _bf(sp)))
            debl = _sum_rows(dss * sp)
            k2g = dk2 * k2f
            db = dqd * qdf - dkd * kdf - k2g
            dk = dkd * gt["enb"] + dk2 * gt["e2"]
            dbl = _chunk_bcast([_lane_block(debl, c) * ebl[c] + _sum_rows(k2g[_chunk_rows(c), :])
                                for c in range(HG_GROUP)])
            dg = _chunk_cumsum(db, reverse=True) + dbl
            df = dg / gt["f"] - dk
            dhf_ref[:, lanes] = _bf(df * (1.0 - lbv) * sg * (1.0 - sg))
            dlb_ref[:, lanes] += jnp.broadcast_to(_sum_rows(df * (1.0 - sg)), (8, LANES))
            dhq_ref[:, lanes] = _bf((dqd * gt["eb"]) * (sq + hq * sq * (1.0 - sq)))
        _ride_wait(ride_modes, step, B * ng, ride_in, ride_out, sems)

    part = lambda j: pl.BlockSpec((None, HG_ROWS, HG_WIDTH), lambda b, g: (b, ng - 1 - g, j))
    return pl.pallas_call(
        body, name="hgrn_bwd", grid=(B, ng),
        in_specs=[part(0), part(1), part(2), part(3),
                  pl.BlockSpec((1, HG_WIDTH), lambda b, g: (0, 0)),
                  pl.BlockSpec((1, LANES), lambda b, g: (0, 0)),
                  part(0),
                  pl.BlockSpec((None, HG_HEADS, None, HG_HEAD_DIM, HG_STACK), lambda b, g: (b, 0, ng - 1 - g, 0, 0)),
                  part(0)] + [ANY_SPEC] * nr,
        out_specs=[part(0), part(0), part(0), part(0),
                   pl.BlockSpec((None, 8, HG_WIDTH), lambda b, g: (b, 0, 0)),
                   pl.BlockSpec((None, 8, LANES), lambda b, g: (b, 0, 0))] + [ANY_SPEC] * nr,
        out_shape=[SDS((B, T, HG_WIDTH), BF16)] * 4 + [SDS((B, 8, HG_WIDTH), F32), SDS((B, 8, LANES), F32)]
        + _exchange_shapes(ride_srcs, ride_modes),
        scratch_shapes=[pltpu.VMEM((HG_HEADS, HG_HEAD_DIM, HG_HEAD_DIM), F32)] + _exchange_sems(nr),
        compiler_params=_params(("arbitrary", "arbitrary"), VMEM_LIMIT_BIG),
    )(proj_h, proj_h, proj_h, proj_h, lb, hg_w, o, s_prev, drg, *ride_srcs)


def _attn_bwd(qr, kr, proj3, attn_o, dan, tables, sinks, attn_w, ride_srcs, ride_modes):
    B, T, _ = proj3.shape
    nb = T // WINDOW
    splits = min(ATT_SPLITS, nb)
    per = nb // splits
    nr = len(ride_srcs)
    cos, sinl, sinr = tables
    QKV = ATT_WIDTH + 2 * LANES

    def body(*refs):
        qr_ref, kr_ref, v_ref, o_ref, dan_ref, cos_ref, sl_ref, sr_ref, sink_ref, aw_ref = refs[:10]
        ride_in = refs[10:10 + nr]
        dqkv_ref, dsink_ref, daw_ref = refs[10 + nr:13 + nr]
        ride_out = refs[13 + nr:13 + 2 * nr]
        kpad, vpad, dkpad, dvpad, dqb, dsk = refs[13 + 2 * nr:19 + 2 * nr]
        sems = refs[19 + 2 * nr:]
        part = pl.program_id(1)
        step = pl.program_id(0) * splits + part
        _ride_start(ride_modes, step, B * splits, ride_in, ride_out, sems)

        @pl.when(part == 0)
        def _():
            kpad[0:WINDOW, :] = jnp.zeros((WINDOW, LANES), BF16)
            vpad[0:WINDOW, :] = jnp.zeros((WINDOW, LANES), BF16)
            kpad[WINDOW:, :] = kr_ref[...]
            vpad[WINDOW:, :] = _bf(v_ref[...])
            dkpad[...] = jnp.zeros(dkpad.shape, F32)
            dvpad[...] = jnp.zeros(dvpad.shape, F32)
            dsk[...] = jnp.zeros(dsk.shape, F32)
            daw_ref[...] = jnp.zeros(daw_ref.shape, F32)

        lower = _lower_mask()
        aw = aw_ref[...]

        def block(n, daw):
            r0 = pl.multiple_of(n * WINDOW, WINDOW)
            rows = pl.ds(r0, WINDOW)
            nxt = pl.ds(r0 + WINDOW, WINDOW)
            ob = o_ref[rows, :]
            dn = dan_ref[rows, :]
            ro = lax.rsqrt(_mean_last(ob * ob) + EPS)
            oh = ob * ro
            daw = daw + _sum_rows(dn * oh)
            doh = dn * aw
            do = _bf(ro * (doh - oh * _mean_last(doh * oh)))
            doparts = [do[:, j * LANES:(j + 1) * LANES] for j in range(ATT_WIDTH // LANES)]
            qparts = [qr_ref[rows, j * LANES:(j + 1) * LANES] for j in range(ATT_WIDTH // LANES)]
            for hk in range(ATT_KV_HEADS):
                lanes = slice(hk * ATT_HEAD_DIM, (hk + 1) * ATT_HEAD_DIM)
                qs = _stack_heads(qparts, hk)
                dos = _stack_heads(doparts, hk)
                k_cur, k_prev = kpad[nxt, lanes], kpad[rows, lanes]
                v_cur, v_prev = vpad[nxt, lanes], vpad[rows, lanes]
                p, inv, es = _softmax_window(qs, k_cur, k_prev, lower, n > 0, _sink_row(sink_ref, hk))
                p = p * inv
                dp = jnp.where(lower, _dot_nt(v_cur, dos), _dot_nt(v_prev, dos))
                delta = jnp.sum(p * dp, axis=0, keepdims=True)
                ds = p * (dp - delta)
                sk = (es * inv) * delta
                ds_cur = jnp.where(lower, ds, 0.0)
                p_cur = jnp.where(lower, p, 0.0)
                ds_cur, ds_prev = _bf(ds_cur), _bf(ds - ds_cur)
                p_cur, p_prev = _bf(p_cur), _bf(p - p_cur)
                dqt = (_dot_tn(k_cur, ds_cur) + _dot_tn(k_prev, ds_prev)) * ATT_SCALE
                dkpad[nxt, lanes] += _dot(ds_cur, qs)
                dkpad[rows, lanes] += _dot(ds_prev, qs)
                dvpad[nxt, lanes] += _dot(p_cur, dos)
                dvpad[rows, lanes] += _dot(p_prev, dos)
                for g in range(ATT_GROUP):
                    h = ATT_GROUP * hk + g
                    cols = slice(g * WINDOW, (g + 1) * WINDOW)
                    dqb[:, h * ATT_HEAD_DIM:(h + 1) * ATT_HEAD_DIM] = dqt[:, cols].T
                    head_lane = lax.broadcasted_iota(jnp.int32, dsk.shape, 1) == h
                    dsk[...] += jnp.where(head_lane, -jnp.sum(sk[:, cols], axis=1, keepdims=True), 0.0)
            cs, sl, sr = cos_ref[rows, :], sl_ref[rows, :], sr_ref[rows, :]
            for j in range(ATT_WIDTH // LANES):
                dqkv_ref[rows, j * LANES:(j + 1) * LANES] = _bf(_rope_t(dqb[:, j * LANES:(j + 1) * LANES], cs, sl, sr))
            return daw

        daw = _loop_pairs(part * per, per, block, jnp.zeros((1, ATT_WIDTH), F32))
        daw_ref[...] += jnp.broadcast_to(daw, (8, ATT_WIDTH))
        dsink_ref[...] = dsk[...]

        def finish(n, carry):
            r0 = pl.multiple_of(n * WINDOW, WINDOW)
            rows = pl.ds(r0, WINDOW)
            nxt = pl.ds(r0 + WINDOW, WINDOW)
            cs, sl, sr = cos_ref[rows, :], sl_ref[rows, :], sr_ref[rows, :]
            dqkv_ref[rows, ATT_WIDTH:ATT_WIDTH + LANES] = _bf(_rope_t(dkpad[nxt, :], cs, sl, sr))
            dqkv_ref[rows, ATT_WIDTH + LANES:QKV] = _bf(dvpad[nxt, :])
            return carry

        @pl.when(part == splits - 1)
        def _():
            lax.fori_loop(0, nb, finish, 0)

        _ride_wait(ride_modes, step, B * splits, ride_in, ride_out, sems)

    seq = lambda w, j: pl.BlockSpec((None, T, w), lambda b, s: (b, 0, j))
    full = lambda r, w: pl.BlockSpec((r, w), lambda b, s: (0, 0))
    return pl.pallas_call(
        body, name="attn_bwd", grid=(B, splits),
        in_specs=[seq(ATT_WIDTH, 0), seq(LANES, 0), seq(LANES, 5), seq(ATT_WIDTH, 0), seq(ATT_WIDTH, 0),
                  full(T, LANES), full(T, LANES), full(T, LANES),
                  pl.BlockSpec(memory_space=pltpu.SMEM), full(1, ATT_WIDTH)] + [ANY_SPEC] * nr,
        out_specs=[seq(QKV, 0), pl.BlockSpec((None, 8, LANES), lambda b, s: (b, 0, 0)),
                   pl.BlockSpec((None, 8, ATT_WIDTH), lambda b, s: (b, 0, 0))] + [ANY_SPEC] * nr,
        out_shape=[SDS((B, T, QKV), BF16), SDS((B, 8, LANES), F32), SDS((B, 8, ATT_WIDTH), F32)]
        + _exchange_shapes(ride_srcs, ride_modes),
        scratch_shapes=[pltpu.VMEM((T + WINDOW, LANES), BF16), pltpu.VMEM((T + WINDOW, LANES), BF16),
                        pltpu.VMEM((T + WINDOW, LANES), F32), pltpu.VMEM((T + WINDOW, LANES), F32),
                        pltpu.VMEM((WINDOW, ATT_WIDTH), F32), pltpu.VMEM((8, LANES), F32)] + _exchange_sems(nr),
        compiler_params=_params(("arbitrary", "arbitrary"), VMEM_LIMIT_BIG),
    )(qr, kr, proj3, attn_o, dan, cos, sinl, sinr, sinks, attn_w, *ride_srcs)


def _in_bwd(x2, dx1, dqkv, dhq, dhf, dhi, dhg, mod8, pre_w, w_in_bf, T, ride_srcs, ride_modes):
    N = x2.shape[0]
    TM = _tile_rows(T, big=True)
    tps = T // TM
    nr = len(ride_srcs)
    pieces = [(0, ATT_WIDTH + 2 * LANES), (768, HG_WIDTH), (1280, HG_WIDTH), (1792, HG_WIDTH), (2304, HG_WIDTH)]

    def body(*refs):
        x_ref, dx_ref, p0, p1, p2, p3, p4, mod_ref, pw_ref, w_ref = refs[:10]
        ride_in = refs[10:10 + nr]
        gx_ref, dproj_ref, acc_ref = refs[10 + nr:13 + nr]
        ride_out = refs[13 + nr:13 + 2 * nr]
        sems = refs[13 + 2 * nr:]
        _ride_start(ride_modes, pl.program_id(0), N // TM, ride_in, ride_out, sems)
        sc1 = mod_ref[1:2, :]
        dh = jnp.zeros((TM, D_MODEL), F32)
        for ref, (off, width) in zip((p0, p1, p2, p3, p4), pieces):
            pb = ref[...]
            dproj_ref[:, off:off + width] = pb
            dh = dh + _dot(pb, w_ref[off:off + width, :])
        x = x_ref[...]
        r = lax.rsqrt(_mean_last(x * x) + EPS)
        xh = x * r
        n1 = xh * pw_ref[...]
        dsh1 = _sum_rows(dh)
        dsc1 = _sum_rows(dh * n1)
        dn1 = dh * (1.0 + sc1)
        dw_pre = _sum_rows(dn1 * xh)
        dxh = dn1 * pw_ref[...]
        gx_ref[...] = dx_ref[...] + r * (dxh - xh * _mean_last(dxh * xh))
        _acc_rows(acc_ref, pl.program_id(0) % tps == 0, [dsh1, dsc1, dw_pre])
        _ride_wait(ride_modes, pl.program_id(0), N // TM, ride_in, ride_out, sems)

    row = lambda w: pl.BlockSpec((TM, w), lambda i: (i, 0))
    B = N // T
    return pl.pallas_call(
        body, name="in_bwd", grid=(N // TM,),
        in_specs=[row(D_MODEL), row(D_MODEL), row(768), row(HG_WIDTH), row(HG_WIDTH), row(HG_WIDTH),
                  row(HG_WIDTH), _mod_spec(tps), pl.BlockSpec((1, D_MODEL), lambda i: (0, 0)),
                  pl.BlockSpec((IN_COLS, D_MODEL), lambda i: (0, 0))] + [ANY_SPEC] * nr,
        out_specs=[row(D_MODEL), row(IN_COLS), _mod_spec(tps)] + [ANY_SPEC] * nr,
        out_shape=[SDS((N, D_MODEL), F32), SDS((N, IN_COLS), BF16), SDS((B, 8, D_MODEL), F32)]
        + _exchange_shapes(ride_srcs, ride_modes),
        scratch_shapes=_exchange_sems(nr),
        compiler_params=_params(("arbitrary",), VMEM_LIMIT_BIG),
    )(x2, dx1, dqkv, dhq, dhf, dhi, dhg, mod8, pre_w, w_in_bf, *ride_srcs)


def _matmul_tn(name, a, b, tn, tm=512, by_owner_cols=False):
    K, M = a.shape
    Nc = b.shape[1]
    tm = min(tm, M)

    def body(a_ref, b_ref, o_ref):
        o_ref[...] = _bf(_dot_tn(a_ref[...], b_ref[...]))

    if by_owner_cols:
        assert tn * N_DEV == Nc
        out_shape = SDS((N_DEV, M, tn), BF16)
        out_spec = pl.BlockSpec((None, tm, tn), lambda i, j: (j, i, 0))
    else:
        out_shape = SDS((M, Nc), BF16)
        out_spec = pl.BlockSpec((tm, tn), lambda i, j: (i, j))
    return pl.pallas_call(
        body, name=name, grid=(M // tm, Nc // tn),
        in_specs=[pl.BlockSpec((K, tm), lambda i, j: (0, i)),
                  pl.BlockSpec((K, tn), lambda i, j: (0, j))],
        out_specs=out_spec, out_shape=out_shape,
        compiler_params=_params(("arbitrary", "arbitrary"), VMEM_LIMIT_BIG),
    )(a, b)


def _adamw_math(w, g, m, v):
    m2 = ADAM_B1 * m + (1.0 - ADAM_B1) * g
    v2 = ADAM_B2 * v + (1.0 - ADAM_B2) * (g * g)
    m_hat = m2 / (1.0 - ADAM_B1 ** ADAM_STEP)
    v_hat = v2 / (1.0 - ADAM_B2 ** ADAM_STEP)
    delta = -ADAM_LR * (m_hat / (jnp.sqrt(v_hat) + ADAM_EPS) + ADAM_WD * w)
    return delta, m2, v2


def _pair_add(name, gw, theirs):
    chips, _, r, c = gw.shape
    tr = r
    core = lax.axis_index("c").astype(jnp.int32).reshape(1)

    def body(core_ref, mine_ref, theirs_ref, o_ref):
        o_ref[...] = _bf(mine_ref[...].astype(F32) + theirs_ref[...].astype(F32))

    block = pl.BlockSpec((None, tr, c), lambda s, i, core_ref: (s, i, 0))
    grid_spec = pltpu.PrefetchScalarGridSpec(
        num_scalar_prefetch=1, grid=(chips, r // tr),
        in_specs=[pl.BlockSpec((None, None, tr, c), lambda s, i, core_ref: (s, core_ref[0], i, 0)), block],
        out_specs=block)
    return pl.pallas_call(
        body, name=name, grid_spec=grid_spec, out_shape=SDS((chips, r, c), BF16),
        compiler_params=_params(("arbitrary", "arbitrary")),
    )(core, gw, theirs)


def _reduce_adamw(name, parts, w, m, v):
    r, c = w.shape
    tr = r if r % 256 else 256
    slots = parts.shape[0]

    def body(p_ref, w_ref, m_ref, v_ref, g_ref, d_ref, m2_ref, v2_ref):
        g = p_ref[0].astype(F32)
        for s in range(1, slots):
            g = g + p_ref[s].astype(F32)
        g_ref[...] = g
        d_ref[...], m2_ref[...], v2_ref[...] = _adamw_math(w_ref[...], g, m_ref[...], v_ref[...])

    blk = pl.BlockSpec((tr, c), lambda i: (i, 0))
    return pl.pallas_call(
        body, name=name, grid=(r // tr,),
        in_specs=[pl.BlockSpec((slots, tr, c), lambda i: (0, i, 0)), blk, blk, blk],
        out_specs=[blk] * 4, out_shape=[SDS((r, c), F32)] * 4,
        compiler_params=_params(("arbitrary",), VMEM_LIMIT_BIG),
    )(parts, w, m, v)


def _ada_grad_adamw(c_all, dmod_all, w, m, v):
    r, c = w.shape
    tr = 256
    nb = c_all.shape[0]

    def body(c_ref, dm_ref, w_ref, m_ref, v_ref, g_ref, d_ref, m2_ref, v2_ref):
        cv = c_ref[...]
        g = _dot_tn(cv * _sigmoid(cv), dm_ref[...])
        g_ref[...] = g
        d_ref[...], m2_ref[...], v2_ref[...] = _adamw_math(w_ref[...], g, m_ref[...], v_ref[...])

    blk = pl.BlockSpec((tr, c), lambda i: (i, 0))
    return pl.pallas_call(
        body, name="ada_grad_adamw", grid=(r // tr,),
        in_specs=[pl.BlockSpec((nb, tr), lambda i: (0, i)), pl.BlockSpec((nb, c), lambda i: (0, 0)),
                  blk, blk, blk],
        out_specs=[blk] * 4, out_shape=[SDS((r, c), F32)] * 4,
        compiler_params=_params(("arbitrary",)),
    )(c_all, dmod_all, w, m, v)


_SMALL = [("b_ada", 6144), ("pre_w_mix", 1024), ("attn_sinks", 128), ("attn_out_w", 512), ("lb_table", 1024),
          ("hg_norm_w", 128), ("post_w_mix", 1024), ("pre_w_mlp", 1024), ("post_w_mlp", 1024)]


def _pack_small(acc_in, acc_mix, acc_mlp, dsink, daw, dlb, dgw, lb_p, ada_cols):
    B = acc_in.shape[0]
    width = sum(w for _, w in _SMALL) + LANES

    def body(ain, amix, amlp, dsk_ref, daw_ref, dlb_ref, dgw_ref, lbp_ref, packed_ref, dmod_ref):
        def total(ref, r, w=None):
            out = ref[0, r:r + 1, :] if w is None else ref[0, r:r + 1, :w]
            for b in range(1, B):
                out = out + (ref[b, r:r + 1, :] if w is None else ref[b, r:r + 1, :w])
            return out

        d_b_ada = None
        for b in range(B):
            mods = [ain[b, 0:1, :], ain[b, 1:2, :], amix[b, 0:1, :], amlp[b, 0:1, :], amlp[b, 1:2, :], amlp[b, 2:3, :]]
            full = jnp.concatenate(mods, axis=1)
            for j in range(N_DEV):
                dmod_ref[j, b:b + 1, :] = full[:, j * ada_cols:(j + 1) * ada_cols]
            d_b_ada = full if d_b_ada is None else d_b_ada + full
        d_lb = total(dlb_ref, 0)
        pp = lbp_ref[0:1, :] * lbp_ref[1:2, :]
        pieces = [d_b_ada, total(ain, 2), total(dsk_ref, 0), total(daw_ref, 0), -d_lb * pp, d_lb * pp,
                  total(dgw_ref, 0), total(amix, 1), total(amlp, 3), total(amlp, 4), total(amlp, 5, LANES)]
        off = 0
        for piece in pieces:
            packed_ref[:, off:off + piece.shape[1]] = piece
            off += piece.shape[1]

    return pl.pallas_call(
        body, name="pack_small",
        out_shape=[SDS((1, width), F32), SDS((N_DEV, B, ada_cols), F32)],
    )(acc_in, acc_mix, acc_mlp, dsink, daw, dlb, dgw, lb_p)


def _adamw_small(parts, given):
    names = [n for n, _ in _SMALL]
    flat_in = [a for n in names for a in given[n]]

    def body(*refs):
        p_ref = refs[0]
        in_refs = refs[1:1 + 3 * len(names)]
        out_refs = refs[1 + 3 * len(names):-1]
        loss_ref = refs[-1]
        g = p_ref[0]
        for s in range(1, N_DEV):
            g = g + p_ref[s]
        off = 0
        for i, (name, width) in enumerate(_SMALL):
            w_ref, m_ref, v_ref = in_refs[3 * i:3 * i + 3]
            rows, cols = w_ref.shape
            for r in range(rows):
                gr = g[:, off + r * cols:off + (r + 1) * cols]
                res = (gr,) + _adamw_math(w_ref[r:r + 1, :], gr, m_ref[r:r + 1, :], v_ref[r:r + 1, :])
                for o_ref, val in zip(out_refs[4 * i:4 * i + 4], res):
                    o_ref[r:r + 1, :] = val
            off += width
        loss_ref[...] = g[:, off:off + LANES]

    out_shape = [SDS(given[n][0].shape, F32) for n in names for _ in range(4)] + [SDS((1, LANES), F32)]
    outs = pl.pallas_call(body, name="adamw_small", out_shape=out_shape)(parts, *flat_in)
    return {n: tuple(outs[4 * i:4 * i + 4]) for i, n in enumerate(names)}, outs[-1][0, 0]


def kernel(x, c, w_ada, b_ada, pre_w_mix, w_in, attn_sinks, attn_out_w, lb_table, hg_norm_w, w_out, post_w_mix, pre_w_mlp, w_up, w_down, post_w_mlp, loss_target, m_w_ada, m_b_ada, m_pre_w_mix, m_w_in, m_attn_sinks, m_attn_out_w, m_lb_table, m_hg_norm_w, m_w_out, m_post_w_mix, m_pre_w_mlp, m_w_up, m_w_down, m_post_w_mlp, v_w_ada, v_b_ada, v_pre_w_mix, v_w_in, v_attn_sinks, v_attn_out_w, v_lb_table, v_hg_norm_w, v_w_out, v_post_w_mix, v_pre_w_mlp, v_w_up, v_w_down, v_post_w_mlp):
    B, T, _ = x.shape
    N = B * T
    me = 4 * lax.axis_index("x") + 2 * lax.axis_index("y") + lax.axis_index("c")
    x2 = x.reshape(N, D_MODEL)
    tgt2 = loss_target.reshape(N, D_MODEL)

    w_in_t, m_w_in_t, v_w_in_t = w_in[0].T, m_w_in[0].T, v_w_in[0].T
    w_in_g, c_g = _exchange("gather_w_in", [_bf(w_in_t), c], ["gather"] * 2)
    w_in_f = w_in_g.reshape(IN_COLS, D_MODEL)
    c_all = c_g.reshape(N_DEV * B, D_MODEL)

    ada_cols = w_ada.shape[2]
    b_mine = lax.dynamic_slice(b_ada, (0, me * ada_cols), (1, ada_cols))
    mod_cols = _ada_mod(c_all, w_ada[0], b_mine)
    (mod_g,) = _exchange("scatter_mod", [mod_cols.reshape(N_DEV, B, ada_cols)], ["a2a"])
    mod = mod_g.transpose(1, 0, 2).reshape(B, 6, D_MODEL)
    mod8 = jnp.pad(mod, ((0, 0), (0, 2), (0, 0)))

    lb_p = jax.nn.softmax(lb_table, axis=0)
    lb = lb_p[1:2]
    tables = _rope_tables(T)

    w_up_b, w_down_b = _bf(w_up[0]), _bf(w_down[0])
    proj_a, proj_h, h1, w_out_g, w_up_g0 = _in_proj(x2, mod8, pre_w_mix, w_in_f, T,
                                                    [_bf(w_out[0]), w_up_b[:MLP_HALF]], ["gather"] * 2)
    proj3 = proj_a.reshape(B, T, ATT_COLS)
    proj_h = proj_h.reshape(B, T, IN_COLS - ATT_COLS)
    rec_o, rec_g, s_prev, w_up_g1 = _hgrn_fwd(proj_h, lb, hg_norm_w, [w_up_b[MLP_HALF:]], ["gather"])
    attn_o, attn_n, qr, kr, w_down_g0 = _attn_fwd(proj3, tables, attn_sinks, attn_out_w,
                                                  [w_down_b[:, :MLP_HALF]], ["gather"])
    w_out_f = w_out_g.reshape(D_MODEL, D_MODEL)
    mix, x1, cat, w_down_g1 = _mix_out(x2, attn_n.reshape(N, ATT_WIDTH), rec_g.reshape(N, HG_WIDTH), mod8,
                                       post_w_mix, w_out_f, T, [w_down_b[:, MLP_HALF:]], ["gather"])
    w_up_halves = [w_up_g0, w_up_g1]
    w_down_halves = [w_down_g0.reshape(D_FF, MLP_HALF), w_down_g1.reshape(D_FF, MLP_HALF)]
    up, u, d, h2 = _mlp_fwd(x1, mod8, pre_w_mlp, w_up_halves, w_down_halves, T)

    dx1, dup, dd, acc_mlp = _mlp_bwd(x1, d, up, tgt2, mod8, pre_w_mlp, post_w_mlp, w_up_halves, w_down_halves, T)
    chips = N_DEV // 2
    by_chip = lambda a: a.reshape((chips, 2, a.shape[0] // N_DEV) + a.shape[1:])
    gw_up = _matmul_tn("grad_w_up", h2, dup, D_FF // N_DEV, tm=D_MODEL, by_owner_cols=True)
    gw_up = gw_up.reshape(chips, 2, D_MODEL, D_FF // N_DEV)
    gw_down = by_chip(_matmul_tn("grad_w_down", u, dd, D_MODEL))
    dan, drg, dmix, acc_mix, q_down, q_up = _mix_bwd(mix, dx1, mod8, post_w_mix, w_out_f, T,
                                                     [gw_down, gw_up], ["pair"] * 2)
    p_down, p_up = _pair_add("pair_add_w_down", gw_down, q_down), _pair_add("pair_add_w_up", gw_up, q_up)
    gw_out = _matmul_tn("grad_w_out", cat, dmix, 512, tm=D_MODEL).reshape(N_DEV, D_MODEL // N_DEV, D_MODEL)
    dhq, dhf, dhi, dhg, dlb_p, dgw_p, r_down, r_up = _hgrn_bwd(
        proj_h, lb, hg_norm_w, rec_o, s_prev, drg.reshape(B, T, HG_WIDTH), [p_down, p_up], ["chips"] * 2)
    dqkv, dsink_p, daw_p, r_out = _attn_bwd(qr, kr, proj3, attn_o, dan.reshape(B, T, ATT_WIDTH), tables,
                                            attn_sinks, attn_out_w, [gw_out], ["a2a"])
    flat = lambda a: a.reshape(N, a.shape[-1])
    grad_x, dproj, acc_in = _in_bwd(x2, dx1, flat(dqkv), flat(dhq), flat(dhf), flat(dhi), flat(dhg),
                                    mod8, pre_w_mix, w_in_f, T, [], [])

    gw_in = by_chip(_matmul_tn("grad_w_in", dproj, h1, 512, tm=IN_COLS // 2))
    (q_in,) = _exchange("pair_w_in", [gw_in], ["pair"])
    p_in = _pair_add("pair_add_w_in", gw_in, q_in)

    packed, dmod_blocks = _pack_small(acc_in, acc_mix, acc_mlp, dsink_p, daw_p, dlb_p, dgw_p, lb_p, ada_cols)
    r_in, r_dmod, r_small = _exchange("reduce_grads", [p_in, dmod_blocks, packed], ["chips", "a2a", "gather"])

    res = {}
    res["w_in"] = tuple(a.T for a in _reduce_adamw("adamw_w_in", r_in, w_in_t, m_w_in_t, v_w_in_t))
    res["w_out"] = _reduce_adamw("adamw_w_out", r_out, w_out[0], m_w_out[0], v_w_out[0])
    res["w_up"] = _reduce_adamw("adamw_w_up", r_up, w_up[0], m_w_up[0], v_w_up[0])
    res["w_down"] = _reduce_adamw("adamw_w_down", r_down, w_down[0], m_w_down[0], v_w_down[0])
    res["w_ada"] = _ada_grad_adamw(c_all, r_dmod.reshape(N_DEV * B, ada_cols), w_ada[0], m_w_ada[0], v_w_ada[0])

    given = dict(b_ada=(b_ada, m_b_ada, v_b_ada), pre_w_mix=(pre_w_mix, m_pre_w_mix, v_pre_w_mix),
                 attn_sinks=(attn_sinks, m_attn_sinks, v_attn_sinks),
                 attn_out_w=(attn_out_w, m_attn_out_w, v_attn_out_w), lb_table=(lb_table, m_lb_table, v_lb_table),
                 hg_norm_w=(hg_norm_w, m_hg_norm_w, v_hg_norm_w), post_w_mix=(post_w_mix, m_post_w_mix, v_post_w_mix),
                 pre_w_mlp=(pre_w_mlp, m_pre_w_mlp, v_pre_w_mlp), post_w_mlp=(post_w_mlp, m_post_w_mlp, v_post_w_mlp))
    small_res, loss = _adamw_small(r_small, given)
    res.update(small_res)

    order = ["w_ada", "b_ada", "pre_w_mix", "w_in", "attn_sinks", "attn_out_w", "lb_table", "hg_norm_w", "w_out",
             "post_w_mix", "pre_w_mlp", "w_up", "w_down", "post_w_mlp"]
    big = {"w_ada", "w_in", "w_out", "w_up", "w_down"}
    outs = [loss, grad_x.reshape(B, T, D_MODEL)]
    for i in range(4):
        for k in order:
            a = res[k][i]
            outs.append(a[None] if k in big else a)
    return tuple(outs)
```

```python
import jax
import jax.numpy as jnp
import numpy as np
from jax import lax
from jax.experimental import pallas as pl
from jax.experimental.pallas import tpu as pltpu

F32 = jnp.float32
BF16 = jnp.bfloat16
SDS = jax.ShapeDtypeStruct

D_MODEL = 1024
ATT_WIDTH = 512
ATT_HEAD_DIM = 64
ATT_KV_HEADS = 2
ATT_GROUP = 4
WINDOW = 128
ROPE_DIM = 16
ROPE_THETA = 500000.0
HG_WIDTH = 512
HG_HEAD_DIM = 128
HG_HEADS = 4
HG_CHUNK = 32
IN_COLS = 2816
ATT_COLS = 768
D_FF = 4096
EPS = 1e-6
N_DEV = 8

ADAM_LR = 0.001
ADAM_B1 = 0.9
ADAM_B2 = 0.999
ADAM_EPS = 1e-08
ADAM_WD = 0.01
ADAM_STEP = 10

VMEM_LIMIT_BIG = 56 << 20
LANES = 128

MESH = pl.DeviceIdType.MESH
NT_DIMS = (((1,), (1,)), ((), ()))
TN_DIMS = (((0,), (0,)), ((), ()))


def _dot(a, b):
    return jnp.dot(a, b, preferred_element_type=F32)


def _dot_nt(a, b):
    return lax.dot_general(a, b, NT_DIMS, preferred_element_type=F32)


def _dot_tn(a, b):
    return lax.dot_general(a, b, TN_DIMS, preferred_element_type=F32)


def _bf(a):
    return a.astype(BF16)


def _sigmoid(a):
    return 0.5 * jnp.tanh(0.5 * a) + 0.5


def _mean_last(a):
    return jnp.mean(a, axis=-1, keepdims=True)


def _sum_rows(a):
    return jnp.sum(a, axis=0, keepdims=True)


def _loop_pairs(first, count, body, init, per_trip=2):
    if count % per_trip:
        return lax.fori_loop(first, first + count, body, init)

    def trip(i, c):
        for k in range(per_trip):
            c = body(first + per_trip * i + k, c)
        return c

    return lax.fori_loop(0, count // per_trip, trip, init)


def _params(sem=None, vmem=None):
    kw = {}
    if sem is not None:
        kw["dimension_semantics"] = sem
    if vmem is not None:
        kw["vmem_limit_bytes"] = vmem
    return pltpu.CompilerParams(**kw)


ANY_SPEC = pl.BlockSpec(memory_space=pl.ANY)


def _exchange_shapes(srcs, modes):
    out_shape = []
    for s, m in zip(srcs, modes):
        shp = {"gather": (N_DEV,) + tuple(s.shape), "pair": (s.shape[0],) + tuple(s.shape[2:])}.get(m, tuple(s.shape))
        out_shape.append(SDS(shp, s.dtype))
    return out_shape


def _exchange_sems(n):
    if n == 0:
        return []
    return [pltpu.SemaphoreType.DMA((n, N_DEV - 1)), pltpu.SemaphoreType.DMA((n, N_DEV - 1)),
            pltpu.SemaphoreType.DMA((n,))]


SIBLING = 1
OTHER_CHIPS = (2, 4, 6)


def _related(k):
    x, y, c = lax.axis_index("x"), lax.axis_index("y"), lax.axis_index("c")
    px, py, pc = x ^ ((k >> 2) & 1), y ^ ((k >> 1) & 1), c ^ (k & 1)
    return (px, py, pc), 4 * px + 2 * py + pc


def _exchange_phases(modes, src_refs, out_refs, send_sems, recv_sems, own_sems):
    _, me = _related(0)
    sib_dev, sib = _related(SIBLING)
    start, middle, end = [], [], []

    def remote(a, i, src, dst, dev):
        return pltpu.make_async_remote_copy(src_ref=src, dst_ref=dst, send_sem=send_sems.at[a, i],
                                            recv_sem=recv_sems.at[a, i], device_id=dev, device_id_type=MESH)

    for a, mode in enumerate(modes):
        out = out_refs[a]
        if mode == "gather":
            src = src_refs[a]
            own = pltpu.make_async_copy(src, out.at[me], own_sems.at[a])
            to_sib = remote(a, 0, src, out.at[me], sib_dev)
            start += [own.start, to_sib.start]
            end += [remote(a, 0, src, out.at[sib], sib_dev).wait_recv, to_sib.wait_send, own.wait]
            for j, k in enumerate(OTHER_CHIPS, start=1):
                dev, peer = _related(k)
                _, peer_sib = _related(k ^ SIBLING)
                send = remote(a, j, src, out.at[me], dev)
                passed = remote(a, 3 + j, out.at[peer], out.at[peer], sib_dev)
                start.append(send.start)
                middle += [remote(a, j, src, out.at[peer], dev).wait_recv, passed.start]
                end += [remote(a, 3 + j, out.at[peer_sib], out.at[peer_sib], sib_dev).wait_recv,
                        send.wait_send, passed.wait_send]
        elif mode == "pair":
            core = lax.axis_index("c")
            for s in range(N_DEV // 2):
                send = remote(a, s, src_refs[a].at[s, 1 - core], out.at[s], sib_dev)
                start.append(send.start)
                end += [remote(a, s, src_refs[a].at[s, 1 - core], out.at[s], sib_dev).wait_recv, send.wait_send]
        elif mode == "chips":
            chip = me // 2
            own = pltpu.make_async_copy(src_refs[a].at[chip], out.at[chip], own_sems.at[a])
            start.append(own.start)
            end.append(own.wait)
            for j, k in enumerate(OTHER_CHIPS, start=1):
                dev, peer = _related(k)
                send = remote(a, j, src_refs[a].at[peer // 2], out.at[chip], dev)
                start.append(send.start)
                end += [remote(a, j, src_refs[a].at[peer // 2], out.at[peer // 2], dev).wait_recv, send.wait_send]
        else:
            own = pltpu.make_async_copy(src_refs[a].at[me], out.at[me], own_sems.at[a])
            start.append(own.start)
            end.append(own.wait)
            for k in range(1, N_DEV):
                dev, peer = _related(k)
                send = remote(a, k - 1, src_refs[a].at[peer], out.at[me], dev)
                start.append(send.start)
                end += [remote(a, k - 1, src_refs[a].at[peer], out.at[peer], dev).wait_recv, send.wait_send]
    return start, middle, end


def _run(actions):
    for act in actions:
        act()


def _exchange(name, srcs, modes):
    n = len(srcs)

    def body(*refs):
        start, middle, end = _exchange_phases(modes, refs[:n], refs[n:2 * n], *refs[2 * n:])
        _run(start)
        _run(middle)
        _run(end)

    return pl.pallas_call(
        body, name=name, out_shape=_exchange_shapes(srcs, modes),
        in_specs=[ANY_SPEC] * n, out_specs=[ANY_SPEC] * n,
        scratch_shapes=_exchange_sems(n),
    )(*srcs)


def _ride_start(modes, step, steps, src_refs, out_refs, sems):
    if not modes:
        return
    middle_step = steps - 1

    @pl.when(step == 0)
    def _():
        _run(_exchange_phases(modes, src_refs, out_refs, *sems)[0])

    if "gather" in modes:
        @pl.when(step == middle_step)
        def _():
            _run(_exchange_phases(modes, src_refs, out_refs, *sems)[1])


def _ride_wait(modes, step, steps, src_refs, out_refs, sems):
    if not modes:
        return

    @pl.when(step == steps - 1)
    def _():
        _run(_exchange_phases(modes, src_refs, out_refs, *sems)[2])


def _ada_mod(c_all, w_ada, b_ada_mine):
    nb, cols = c_all.shape[0], w_ada.shape[1]

    def body(c_ref, w_ref, b_ref, o_ref):
        cv = c_ref[...]
        ca = cv * _sigmoid(cv)
        o_ref[...] = _dot(ca, w_ref[...]) + b_ref[...]

    return pl.pallas_call(body, name="ada_mod", out_shape=SDS((nb, cols), F32))(c_all, w_ada, b_ada_mine)


def _tile_rows(T, big=False):
    return min(512 if big else 256, T)


def _mod_spec(tps):
    return pl.BlockSpec((None, 8, D_MODEL), lambda i: (i // tps, 0, 0))


def _in_proj(x2, mod8, pre_w, w_in_bf, T, ride_srcs, ride_modes):
    N = x2.shape[0]
    TM = _tile_rows(T, big=True)
    tps = T // TM
    nr = len(ride_srcs)

    def body(*refs):
        x_ref, mod_ref, pw_ref, w_ref = refs[:4]
        ride_in = refs[4:4 + nr]
        pa_ref, ph_ref, h1_ref = refs[4 + nr:7 + nr]
        ride_out = refs[7 + nr:7 + 2 * nr]
        sems = refs[7 + 2 * nr:]
        _ride_start(ride_modes, pl.program_id(0), N // TM, ride_in, ride_out, sems)
        x = x_ref[...]
        r = lax.rsqrt(_mean_last(x * x) + EPS)
        h = (x * r * pw_ref[...]) * (1.0 + mod_ref[1:2, :]) + mod_ref[0:1, :]
        hb = _bf(h)
        h1_ref[...] = hb
        pa_ref[...] = _dot_nt(hb, w_ref[:ATT_COLS, :])
        ph_ref[...] = _dot_nt(hb, w_ref[ATT_COLS:, :])
        _ride_wait(ride_modes, pl.program_id(0), N // TM, ride_in, ride_out, sems)

    return pl.pallas_call(
        body, name="in_proj", grid=(N // TM,),
        in_specs=[pl.BlockSpec((TM, D_MODEL), lambda i: (i, 0)), _mod_spec(tps),
                  pl.BlockSpec((1, D_MODEL), lambda i: (0, 0)),
                  pl.BlockSpec((IN_COLS, D_MODEL), lambda i: (0, 0))] + [ANY_SPEC] * nr,
        out_specs=[pl.BlockSpec((TM, ATT_COLS), lambda i: (i, 0)),
                   pl.BlockSpec((TM, IN_COLS - ATT_COLS), lambda i: (i, 0)),
                   pl.BlockSpec((TM, D_MODEL), lambda i: (i, 0))] + [ANY_SPEC] * nr,
        out_shape=[SDS((N, ATT_COLS), F32), SDS((N, IN_COLS - ATT_COLS), F32), SDS((N, D_MODEL), BF16)]
        + _exchange_shapes(ride_srcs, ride_modes),
        scratch_shapes=_exchange_sems(nr),
        compiler_params=_params(("arbitrary",), VMEM_LIMIT_BIG),
    )(x2, mod8, pre_w, w_in_bf, *ride_srcs)


def _rope_tables(T):
    half = ROPE_DIM // 2
    f32 = np.float32
    inv_freq = (f32(ROPE_THETA) ** (-np.arange(0, ROPE_DIM, 2, dtype=f32) / f32(ROPE_DIM))).astype(f32)
    ang = np.arange(T, dtype=f32)[:, None] * inv_freq[None, :]
    cos, sin = np.cos(ang).astype(f32), np.sin(ang).astype(f32)
    ones = np.ones((T, ATT_HEAD_DIM - ROPE_DIM), f32)
    zeros = np.zeros((T, ATT_HEAD_DIM - ROPE_DIM), f32)
    zh = np.zeros((T, half), f32)
    cos64 = np.concatenate([cos, cos, ones], axis=1)
    sin_left = np.concatenate([-sin, zh, zeros], axis=1)
    sin_right = np.concatenate([zh, sin, zeros], axis=1)
    rep = LANES // ATT_HEAD_DIM
    return tuple(jnp.asarray(np.tile(t, (1, rep))) for t in (cos64, sin_left, sin_right))


def _rope(xc, cs, sl, sr):
    return xc * cs + pltpu.roll(xc, LANES - 8, 1) * sl + pltpu.roll(xc, 8, 1) * sr


def _rope_t(dy, cs, sl, sr):
    return dy * cs + pltpu.roll(dy * sl, 8, 1) + pltpu.roll(dy * sr, LANES - 8, 1)


ATT_SCALE = ATT_HEAD_DIM ** -0.5
ATT_SPLITS = 4


def _lower_mask():
    j = lax.broadcasted_iota(jnp.int32, (WINDOW, ATT_GROUP * WINDOW), 0)
    i = lax.broadcasted_iota(jnp.int32, (WINDOW, ATT_GROUP * WINDOW), 1) & (WINDOW - 1)
    return j <= i


def _sink_row(sink_ref, hk):
    return jnp.concatenate(
        [jnp.full((1, WINDOW), sink_ref[0, ATT_GROUP * hk + g], F32) for g in range(ATT_GROUP)], axis=1)


def _softmax_window(qs, k_cur, k_prev, lower, has_prev, sink):
    s_prev = jnp.where(has_prev, _dot_nt(k_prev, qs), jnp.finfo(F32).min)
    s = jnp.where(lower, _dot_nt(k_cur, qs), s_prev)
    m = jnp.maximum(jnp.max(s, axis=0, keepdims=True), sink)
    p = jnp.exp(s - m)
    es = jnp.exp(sink - m)
    inv = 1.0 / (jnp.sum(p, axis=0, keepdims=True) + es)
    return p, inv, es


def _stack_heads(parts, hk):
    hs = []
    for g in range(ATT_GROUP):
        h = ATT_GROUP * hk + g
        hs.append(parts[h // 2][:, (h % 2) * ATT_HEAD_DIM:(h % 2 + 1) * ATT_HEAD_DIM])
    return jnp.concatenate(hs, axis=0)


def _attn_fwd(proj3, tables, sinks, attn_w, ride_srcs, ride_modes):
    B, T, _ = proj3.shape
    nb = T // WINDOW
    splits = min(ATT_SPLITS, nb)
    per = nb // splits
    nr = len(ride_srcs)
    cos, sinl, sinr = tables

    def body(*refs):
        q_ref, k_ref, v_ref, cos_ref, sl_ref, sr_ref, sink_ref, aw_ref = refs[:8]
        ride_in = refs[8:8 + nr]
        o_ref, an_ref, qr_ref, kr_ref = refs[8 + nr:12 + nr]
        ride_out = refs[12 + nr:12 + 2 * nr]
        kpad, vpad = refs[12 + 2 * nr:14 + 2 * nr]
        sems = refs[14 + 2 * nr:]
        part = pl.program_id(1)
        step = pl.program_id(0) * splits + part
        _ride_start(ride_modes, step, B * splits, ride_in, ride_out, sems)

        @pl.when(part == 0)
        def _():
            kpad[0:WINDOW, :] = jnp.zeros((WINDOW, LANES), BF16)
            vpad[0:WINDOW, :] = jnp.zeros((WINDOW, LANES), BF16)

        lower = _lower_mask()

        def block(n, carry):
            r0 = pl.multiple_of(n * WINDOW, WINDOW)
            rows = pl.ds(r0, WINDOW)
            nxt = pl.ds(r0 + WINDOW, WINDOW)
            cs, sl, sr = cos_ref[rows, :], sl_ref[rows, :], sr_ref[rows, :]
            kb = _bf(_rope(k_ref[rows, :], cs, sl, sr))
            vb = _bf(v_ref[rows, :])
            kpad[nxt, :] = kb
            kr_ref[rows, :] = kb
            vpad[nxt, :] = vb
            qparts = []
            for j in range(ATT_WIDTH // LANES):
                qp = _bf(_rope(q_ref[rows, j * LANES:(j + 1) * LANES], cs, sl, sr) * ATT_SCALE)
                qr_ref[rows, j * LANES:(j + 1) * LANES] = qp
                qparts.append(qp)
            for hk in range(ATT_KV_HEADS):
                lanes = slice(hk * ATT_HEAD_DIM, (hk + 1) * ATT_HEAD_DIM)
                qs = _stack_heads(qparts, hk)
                p, inv, _ = _softmax_window(qs, kb[:, lanes], kpad[rows, lanes], lower, n > 0,
                                            _sink_row(sink_ref, hk))
                p_cur = jnp.where(lower, p, 0.0)
                ot = (_dot_tn(vb[:, lanes], _bf(p_cur)) + _dot_tn(vpad[rows, lanes], _bf(p - p_cur))) * inv
                for g in range(ATT_GROUP):
                    h = ATT_GROUP * hk + g
                    o_ref[rows, h * ATT_HEAD_DIM:(h + 1) * ATT_HEAD_DIM] = ot[:, g * WINDOW:(g + 1) * WINDOW].T
            ob = o_ref[rows, :]
            an_ref[rows, :] = _bf(ob * lax.rsqrt(_mean_last(ob * ob) + EPS) * aw_ref[...])
            return carry

        _loop_pairs(part * per, per, block, 0)
        _ride_wait(ride_modes, step, B * splits, ride_in, ride_out, sems)

    seq = lambda w, j: pl.BlockSpec((None, T, w), lambda b, s: (b, 0, j))
    full = lambda r, w: pl.BlockSpec((r, w), lambda b, s: (0, 0))
    return pl.pallas_call(
        body, name="attn_fwd", grid=(B, splits),
        in_specs=[seq(ATT_WIDTH, 0), seq(LANES, 4), seq(LANES, 5),
                  full(T, LANES), full(T, LANES), full(T, LANES),
                  pl.BlockSpec(memory_space=pltpu.SMEM), full(1, ATT_WIDTH)] + [ANY_SPEC] * nr,
        out_specs=[seq(ATT_WIDTH, 0), seq(ATT_WIDTH, 0), seq(ATT_WIDTH, 0), seq(LANES, 0)] + [ANY_SPEC] * nr,
        out_shape=[SDS((B, T, ATT_WIDTH), F32), SDS((B, T, ATT_WIDTH), BF16),
                   SDS((B, T, ATT_WIDTH), BF16), SDS((B, T, LANES), BF16)] + _exchange_shapes(ride_srcs, ride_modes),
        scratch_shapes=[pltpu.VMEM((T + WINDOW, LANES), BF16), pltpu.VMEM((T + WINDOW, LANES), BF16)]
        + _exchange_sems(nr),
        compiler_params=_params(("arbitrary", "arbitrary"), VMEM_LIMIT_BIG),
    )(proj3, proj3, proj3, cos, sinl, sinr, sinks, attn_w, *ride_srcs)


HG_GROUP = 8
HG_ROWS = HG_GROUP * HG_CHUNK


HG_STACK = HG_GROUP * HG_HEAD_DIM


def _group_mask():
    r = lax.broadcasted_iota(jnp.int32, (HG_ROWS, HG_ROWS), 0)
    c = lax.broadcasted_iota(jnp.int32, (HG_ROWS, HG_ROWS), 1)
    return ((r // HG_CHUNK) == (c // HG_CHUNK)) & (r >= c)


def _spread(a):
    blocks = []
    for c in range(HG_GROUP):
        above = jnp.zeros((c * HG_CHUNK, HG_HEAD_DIM), a.dtype)
        below = jnp.zeros(((HG_GROUP - 1 - c) * HG_CHUNK, HG_HEAD_DIM), a.dtype)
        blocks.append(jnp.concatenate([p for p in (above, a[_chunk_rows(c), :], below) if p.shape[0]], axis=0))
    return jnp.concatenate(blocks, axis=1)


def _pick(r):
    return jnp.concatenate([r[_chunk_rows(c), c * HG_HEAD_DIM:(c + 1) * HG_HEAD_DIM] for c in range(HG_GROUP)], axis=0)


def _lane_block(a, c):
    return a[:, c * HG_HEAD_DIM:(c + 1) * HG_HEAD_DIM]


def _chunk_cumsum(a, reverse=False):
    n = a.shape[0]
    pos = lax.broadcasted_iota(jnp.int32, a.shape, 0) % HG_CHUNK
    shift = 1
    while shift < HG_CHUNK:
        if reverse:
            a = a + jnp.where(pos < HG_CHUNK - shift, pltpu.roll(a, n - shift, 0), 0.0)
        else:
            a = a + jnp.where(pos >= shift, pltpu.roll(a, shift, 0), 0.0)
        shift *= 2
    return a


def _chunk_bcast(rows_1x128):
    return jnp.concatenate([jnp.broadcast_to(r, (HG_CHUNK, HG_HEAD_DIM)) for r in rows_1x128], axis=0)


def _hgrn_gates(hq, hf, lb):
    sq = _sigmoid(hq)
    q = hq * sq
    sg = _sigmoid(hf)
    f = lb + (1.0 - lb) * sg
    k = 1.0 - f
    logf = jnp.log(f)
    b = _chunk_cumsum(logf)
    bl = [_sum_rows(logf[_chunk_rows(c), :]) for c in range(HG_GROUP)]
    eb, enb, e2 = jnp.exp(b), jnp.exp(-b), jnp.exp(_chunk_bcast(bl) - b)
    ebl = [jnp.exp(r) for r in bl]
    return dict(sq=sq, sg=sg, f=f, eb=eb, enb=enb, e2=e2, ebl=ebl, qd=q * eb, kd=k * enb, k2=k * e2)


def _chunk_rows(c):
    return slice(c * HG_CHUNK, (c + 1) * HG_CHUNK)


def _head_lanes(h):
    return slice(h * HG_HEAD_DIM, (h + 1) * HG_HEAD_DIM)


def _hgrn_fwd(proj_h, lb, hg_w, ride_srcs, ride_modes):
    B, T, _ = proj_h.shape
    ng = T // HG_ROWS
    nr = len(ride_srcs)

    def body(*refs):
        hq_ref, hf_ref, hi_ref, hg_ref, lb_ref, gw_ref = refs[:6]
        ride_in = refs[6:6 + nr]
        o_ref, rg_ref, sp_ref = refs[6 + nr:9 + nr]
        ride_out = refs[9 + nr:9 + 2 * nr]
        st = refs[9 + 2 * nr]
        sems = refs[10 + 2 * nr:]
        gi = pl.program_id(1)
        step = pl.program_id(0) * ng + gi
        _ride_start(ride_modes, step, B * ng, ride_in, ride_out, sems)

        @pl.when(gi == 0)
        def _():
            st[...] = jnp.zeros(st.shape, F32)

        lo = _group_mask()
        for h in range(HG_HEADS):
            lanes = _head_lanes(h)
            gt = _hgrn_gates(hq_ref[:, lanes], hf_ref[:, lanes], lb_ref[:, lanes])
            v, qd, kd = _bf(hi_ref[:, lanes]), _bf(gt["qd"]), _bf(gt["kd"])
            a = jnp.where(lo, _dot_nt(qd, kd), 0.0)
            kv = _dot_tn(v, _spread(_bf(gt["k2"])))
            s = st[h]
            before = []
            for c in range(HG_GROUP):
                before.append(s)
                s = s * gt["ebl"][c] + _lane_block(kv, c)
            st[h] = s
            sp = jnp.concatenate(before, axis=1)
            sp_ref[h] = sp
            o = _dot(_bf(a), v) + _dot_nt(_spread(qd), _bf(sp))
            o_ref[:, lanes] = o
            hg = hg_ref[:, lanes]
            rn = o * lax.rsqrt(_mean_last(o * o) + EPS) * gw_ref[...]
            rg_ref[:, lanes] = _bf(rn * (hg * _sigmoid(hg)))
        _ride_wait(ride_modes, step, B * ng, ride_in, ride_out, sems)

    part = lambda j: pl.BlockSpec((None, HG_ROWS, HG_WIDTH), lambda b, g: (b, g, j))
    return pl.pallas_call(
        body, name="hgrn_fwd", grid=(B, ng),
        in_specs=[part(0), part(1), part(2), part(3),
                  pl.BlockSpec((1, HG_WIDTH), lambda b, g: (0, 0)),
                  pl.BlockSpec((1, LANES), lambda b, g: (0, 0))] + [ANY_SPEC] * nr,
        out_specs=[part(0), part(0),
                   pl.BlockSpec((None, HG_HEADS, None, HG_HEAD_DIM, HG_STACK), lambda b, g: (b, 0, g, 0, 0))]
        + [ANY_SPEC] * nr,
        out_shape=[SDS((B, T, HG_WIDTH), F32), SDS((B, T, HG_WIDTH), BF16),
                   SDS((B, HG_HEADS, ng, HG_HEAD_DIM, HG_STACK), F32)] + _exchange_shapes(ride_srcs, ride_modes),
        scratch_shapes=[pltpu.VMEM((HG_HEADS, HG_HEAD_DIM, HG_HEAD_DIM), F32)] + _exchange_sems(nr),
        compiler_params=_params(("arbitrary", "arbitrary"), VMEM_LIMIT_BIG),
    )(proj_h, proj_h, proj_h, proj_h, lb, hg_w, *ride_srcs)


def _mix_out(x2, attn_n, rec_g, mod8, post_w, w_out_bf, T, ride_srcs, ride_modes):
    N = x2.shape[0]
    TM = _tile_rows(T, big=True)
    tps = T // TM
    nr = len(ride_srcs)

    def body(*refs):
        x_ref, an_ref, rg_ref, mod_ref, pw_ref, w_ref = refs[:6]
        ride_in = refs[6:6 + nr]
        mix_ref, x1_ref, cat_ref = refs[6 + nr:9 + nr]
        ride_out = refs[9 + nr:9 + 2 * nr]
        sems = refs[9 + 2 * nr:]
        _ride_start(ride_modes, pl.program_id(0), N // TM, ride_in, ride_out, sems)
        cat = jnp.concatenate([an_ref[...], rg_ref[...]], axis=1)
        cat_ref[...] = cat
        mix = _dot(cat, w_ref[...])
        mix_ref[...] = mix
        r = lax.rsqrt(_mean_last(mix * mix) + EPS)
        x1_ref[...] = x_ref[...] + mod_ref[2:3, :] * (mix * r * pw_ref[...])
        _ride_wait(ride_modes, pl.program_id(0), N // TM, ride_in, ride_out, sems)

    row = lambda w: pl.BlockSpec((TM, w), lambda i: (i, 0))
    return pl.pallas_call(
        body, name="mix_out", grid=(N // TM,),
        in_specs=[row(D_MODEL), row(ATT_WIDTH), row(HG_WIDTH), _mod_spec(tps),
                  pl.BlockSpec((1, D_MODEL), lambda i: (0, 0)),
                  pl.BlockSpec((D_MODEL, D_MODEL), lambda i: (0, 0))] + [ANY_SPEC] * nr,
        out_specs=[row(D_MODEL), row(D_MODEL), row(D_MODEL)] + [ANY_SPEC] * nr,
        out_shape=[SDS((N, D_MODEL), F32), SDS((N, D_MODEL), F32), SDS((N, D_MODEL), BF16)]
        + _exchange_shapes(ride_srcs, ride_modes),
        scratch_shapes=_exchange_sems(nr),
        compiler_params=_params(("arbitrary",), VMEM_LIMIT_BIG),
    )(x2, attn_n, rec_g, mod8, post_w, w_out_bf, *ride_srcs)


def _load_weights_once(pairs, sem):
    @pl.when(pl.program_id(0) == 0)
    def _():
        cps = [pltpu.make_async_copy(src, dst, sem.at[i]) for i, (src, dst) in enumerate(pairs)]
        for cp in cps:
            cp.start()
        for cp in cps:
            cp.wait()


MLP_HALF = D_MODEL // 2
MLP_PIECES = 2 * N_DEV + 2


def _mlp_weight_pieces(wu_a, wu_b, wd_a, wd_b, wu, wd):
    cols = D_FF // N_DEV
    pairs = []
    for h, half in enumerate((wu_a, wu_b)):
        for j in range(N_DEV):
            pairs.append((half.at[j], wu.at[pl.ds(h * MLP_HALF, MLP_HALF), pl.ds(j * cols, cols)]))
    for h, half in enumerate((wd_a, wd_b)):
        pairs.append((half, wd.at[:, pl.ds(h * MLP_HALF, MLP_HALF)]))
    return pairs


def _mlp_fwd(x1, mod8, pre_w, w_up_halves, w_down_halves, T):
    N = x1.shape[0]
    TM = _tile_rows(T)
    tps = T // TM

    def body(x_ref, mod_ref, pw_ref, wua, wub, wda, wdb, up_ref, u_ref, d_ref, h2_ref, wu, wd, sem):
        _load_weights_once(_mlp_weight_pieces(wua, wub, wda, wdb, wu, wd), sem)
        x = x_ref[...]
        r = lax.rsqrt(_mean_last(x * x) + EPS)
        h = (x * r * pw_ref[...]) * (1.0 + mod_ref[4:5, :]) + mod_ref[3:4, :]
        hb = _bf(h)
        h2_ref[...] = hb
        up = _dot(hb, wu[...])
        up_ref[...] = up
        ru = jnp.maximum(up, 0.0)
        u = _bf(ru * ru)
        u_ref[...] = u
        d_ref[...] = _dot(u, wd[...])

    row = lambda w: pl.BlockSpec((TM, w), lambda i: (i, 0))
    return pl.pallas_call(
        body, name="mlp_fwd", grid=(N // TM,),
        in_specs=[row(D_MODEL), _mod_spec(tps), pl.BlockSpec((1, D_MODEL), lambda i: (0, 0))] + [ANY_SPEC] * 4,
        out_specs=[row(D_FF), row(D_FF), row(D_MODEL), row(D_MODEL)],
        out_shape=[SDS((N, D_FF), F32), SDS((N, D_FF), BF16), SDS((N, D_MODEL), F32), SDS((N, D_MODEL), BF16)],
        scratch_shapes=[pltpu.VMEM((D_MODEL, D_FF), BF16), pltpu.VMEM((D_FF, D_MODEL), BF16),
                        pltpu.SemaphoreType.DMA((MLP_PIECES,))],
        compiler_params=_params(("arbitrary",), VMEM_LIMIT_BIG),
    )(x1, mod8, pre_w, *w_up_halves, *w_down_halves)


def _acc_rows(acc_ref, first, rows):
    @pl.when(first)
    def _():
        acc_ref[...] = jnp.zeros(acc_ref.shape, F32)
    for i, r in enumerate(rows):
        acc_ref[i:i + 1, :] += r


def _mlp_bwd(x1, d, up, tgt, mod8, pre_w, post_w, w_up_halves, w_down_halves, T):
    N = x1.shape[0]
    TM = _tile_rows(T)
    tps = T // TM

    def body(x_ref, d_ref, up_ref, t_ref, mod_ref, pw_ref, qw_ref, wua, wub, wda, wdb,
             dx_ref, dup_ref, dd_ref, acc_ref, wd, wu, sem):
        _load_weights_once(_mlp_weight_pieces(wua, wub, wda, wdb, wu, wd), sem)
        sh2, sc2, g2 = mod_ref[3:4, :], mod_ref[4:5, :], mod_ref[5:6, :]
        x = x_ref[...]
        r1 = lax.rsqrt(_mean_last(x * x) + EPS)
        xh = x * r1
        n2 = xh * pw_ref[...]
        dv = d_ref[...]
        rd = lax.rsqrt(_mean_last(dv * dv) + EPS)
        dh = dv * rd
        rr = dh * qw_ref[...]
        e = x + g2 * rr - t_ref[...]
        loss = 0.5 * jnp.sum(_sum_rows(e * e), axis=1, keepdims=True) / D_MODEL
        dy = e * (1.0 / D_MODEL)
        dg2 = _sum_rows(dy * rr)
        drr = dy * g2
        dw_post = _sum_rows(drr * dh)
        ddh = drr * qw_ref[...]
        dd = _bf(rd * (ddh - dh * _mean_last(ddh * dh)))
        dd_ref[...] = dd
        ru = jnp.maximum(up_ref[...], 0.0)
        dup = _bf(_dot_nt(dd, wd[...]) * (2.0 * ru))
        dup_ref[...] = dup
        dh2 = _dot_nt(dup, wu[...])
        dsh2 = _sum_rows(dh2)
        dsc2 = _sum_rows(dh2 * n2)
        dn2 = dh2 * (1.0 + sc2)
        dw_pre = _sum_rows(dn2 * xh)
        dxh = dn2 * pw_ref[...]
        dx_ref[...] = dy + r1 * (dxh - xh * _mean_last(dxh * xh))
        _acc_rows(acc_ref, pl.program_id(0) % tps == 0,
                  [dsh2, dsc2, dg2, dw_pre, dw_post, jnp.broadcast_to(loss, (1, D_MODEL))])

    row = lambda w: pl.BlockSpec((TM, w), lambda i: (i, 0))
    vec = pl.BlockSpec((1, D_MODEL), lambda i: (0, 0))
    B = N // T
    return pl.pallas_call(
        body, name="mlp_bwd", grid=(N // TM,),
        in_specs=[row(D_MODEL), row(D_MODEL), row(D_FF), row(D_MODEL), _mod_spec(tps), vec, vec] + [ANY_SPEC] * 4,
        out_specs=[row(D_MODEL), row(D_FF), row(D_MODEL), _mod_spec(tps)],
        out_shape=[SDS((N, D_MODEL), F32), SDS((N, D_FF), BF16), SDS((N, D_MODEL), BF16),
                   SDS((B, 8, D_MODEL), F32)],
        scratch_shapes=[pltpu.VMEM((D_FF, D_MODEL), BF16), pltpu.VMEM((D_MODEL, D_FF), BF16),
                        pltpu.SemaphoreType.DMA((MLP_PIECES,))],
        compiler_params=_params(("arbitrary",), VMEM_LIMIT_BIG),
    )(x1, d, up, tgt, mod8, pre_w, post_w, *w_up_halves, *w_down_halves)


def _mix_bwd(mix, dx1, mod8, post_w, w_out_bf, T, ride_srcs, ride_modes):
    N = mix.shape[0]
    TM = _tile_rows(T, big=True)
    tps = T // TM
    nr = len(ride_srcs)

    def body(*refs):
        mix_ref, dx_ref, mod_ref, pw_ref, w_ref = refs[:5]
        ride_in = refs[5:5 + nr]
        dan_ref, drg_ref, dmix_ref, acc_ref = refs[5 + nr:9 + nr]
        ride_out = refs[9 + nr:9 + 2 * nr]
        sems = refs[9 + 2 * nr:]
        _ride_start(ride_modes, pl.program_id(0), N // TM, ride_in, ride_out, sems)
        g1 = mod_ref[2:3, :]
        mix = mix_ref[...]
        dx1 = dx_ref[...]
        rm = lax.rsqrt(_mean_last(mix * mix) + EPS)
        mh = mix * rm
        dg1 = _sum_rows(dx1 * (mh * pw_ref[...]))
        dr = dx1 * g1
        dw_post = _sum_rows(dr * mh)
        dmh = dr * pw_ref[...]
        dmix = _bf(rm * (dmh - mh * _mean_last(dmh * mh)))
        dmix_ref[...] = dmix
        dcat = _dot_nt(dmix, w_ref[...])
        dan_ref[...] = dcat[:, :ATT_WIDTH]
        drg_ref[...] = dcat[:, ATT_WIDTH:]
        _acc_rows(acc_ref, pl.program_id(0) % tps == 0, [dg1, dw_post])
        _ride_wait(ride_modes, pl.program_id(0), N // TM, ride_in, ride_out, sems)

    row = lambda w: pl.BlockSpec((TM, w), lambda i: (i, 0))
    B = N // T
    return pl.pallas_call(
        body, name="mix_bwd", grid=(N // TM,),
        in_specs=[row(D_MODEL), row(D_MODEL), _mod_spec(tps), pl.BlockSpec((1, D_MODEL), lambda i: (0, 0)),
                  pl.BlockSpec((D_MODEL, D_MODEL), lambda i: (0, 0))] + [ANY_SPEC] * nr,
        out_specs=[row(ATT_WIDTH), row(HG_WIDTH), row(D_MODEL), _mod_spec(tps)] + [ANY_SPEC] * nr,
        out_shape=[SDS((N, ATT_WIDTH), F32), SDS((N, HG_WIDTH), F32), SDS((N, D_MODEL), BF16),
                   SDS((B, 8, D_MODEL), F32)] + _exchange_shapes(ride_srcs, ride_modes),
        scratch_shapes=_exchange_sems(nr),
        compiler_params=_params(("arbitrary",), VMEM_LIMIT_BIG),
    )(mix, dx1, mod8, post_w, w_out_bf, *ride_srcs)


def _hgrn_bwd(proj_h, lb, hg_w, o, s_prev, drg, ride_srcs, ride_modes):
    B, T, _ = proj_h.shape
    ng = T // HG_ROWS
    nr = len(ride_srcs)

    def body(*refs):
        hq_ref, hf_ref, hi_ref, hg_ref, lb_ref, gw_ref, o_ref, sp_ref, drg_ref = refs[:9]
        ride_in = refs[9:9 + nr]
        dhq_ref, dhf_ref, dhi_ref, dhg_ref, dlb_ref, dgw_ref = refs[9 + nr:15 + nr]
        ride_out = refs[15 + nr:15 + 2 * nr]
        dst = refs[15 + 2 * nr]
        sems = refs[16 + 2 * nr:]
        step = pl.program_id(0) * ng + pl.program_id(1)
        _ride_start(ride_modes, step, B * ng, ride_in, ride_out, sems)

        @pl.when(pl.program_id(1) == 0)
        def _():
            dst[...] = jnp.zeros(dst.shape, F32)
            dlb_ref[...] = jnp.zeros(dlb_ref.shape, F32)
            dgw_ref[...] = jnp.zeros(dgw_ref.shape, F32)

        lo = _group_mask()
        gw = gw_ref[...]

        for h in range(HG_HEADS):
            lanes = _head_lanes(h)
            lbv = lb_ref[:, lanes]
            hq = hq_ref[:, lanes]
            gt = _hgrn_gates(hq, hf_ref[:, lanes], lbv)
            sq, sg, qdf, kdf, k2f, ebl = gt["sq"], gt["sg"], gt["qd"], gt["kd"], gt["k2"], gt["ebl"]
            v, qd, kd = _bf(hi_ref[:, lanes]), _bf(qdf), _bf(kdf)
            ov = o_ref[:, lanes]
            hg = hg_ref[:, lanes]
            shg = _sigmoid(hg)
            dr = drg_ref[:, lanes]
            ro = lax.rsqrt(_mean_last(ov * ov) + EPS)
            oh = ov * ro
            dhg_ref[:, lanes] = _bf(dr * (oh * gw) * (shg + hg * shg * (1.0 - shg)))
            drn = dr * (hg * shg)
            dgw_ref[...] += jnp.broadcast_to(_sum_rows(drn * oh), (8, LANES))
            doh = drn * gw
            do = _bf(ro * (doh - oh * _mean_last(doh * oh)))
            a = jnp.where(lo, _dot_nt(qd, kd), 0.0)
            da = _bf(jnp.where(lo, _dot_nt(do, v), 0.0))
            dv = _dot_tn(_bf(a), do)
            dqd = _dot(da, kd)
            dkd = _dot_tn(da, qd)
            sp = sp_ref[h]
            incr = _dot_tn(do, _spread(qd))
            ds = dst[h]
            after = [None] * HG_GROUP
            for c in reversed(range(HG_GROUP)):
                after[c] = ds
                ds = ds * ebl[c] + _lane_block(incr, c)
            dst[h] = ds
            dss = jnp.concatenate(after, axis=1)
            dssb = _bf(dss)
            dk2 = _pick(_dot(v, dssb))
            dhi_ref[:, lanes] = _bf(dv + _dot_nt(_spread(_bf(k2f)), dssb))
            dqd = dqd + _pick(_dot(do, _bf(sp)))
            debl = _sum_rows(dss * sp)
            k2g = dk2 * k2f
            db = dqd * qdf - dkd * kdf - k2g
            dk = dkd * gt["enb"] + dk2 * gt["e2"]
            dbl = _chunk_bcast([_lane_block(debl, c) * ebl[c] + _sum_rows(k2g[_chunk_rows(c), :])
                                for c in range(HG_GROUP)])
            dg = _chunk_cumsum(db, reverse=True) + dbl
            df = dg / gt["f"] - dk
            dhf_ref[:, lanes] = _bf(df * (1.0 - lbv) * sg * (1.0 - sg))
            dlb_ref[:, lanes] += jnp.broadcast_to(_sum_rows(df * (1.0 - sg)), (8, LANES))
            dhq_ref[:, lanes] = _bf((dqd * gt["eb"]) * (sq + hq * sq * (1.0 - sq)))
        _ride_wait(ride_modes, step, B * ng, ride_in, ride_out, sems)

    part = lambda j: pl.BlockSpec((None, HG_ROWS, HG_WIDTH), lambda b, g: (b, ng - 1 - g, j))
    return pl.pallas_call(
        body, name="hgrn_bwd", grid=(B, ng),
        in_specs=[part(0), part(1), part(2), part(3),
                  pl.BlockSpec((1, HG_WIDTH), lambda b, g: (0, 0)),
                  pl.BlockSpec((1, LANES), lambda b, g: (0, 0)),
                  part(0),
                  pl.BlockSpec((None, HG_HEADS, None, HG_HEAD_DIM, HG_STACK), lambda b, g: (b, 0, ng - 1 - g, 0, 0)),
                  part(0)] + [ANY_SPEC] * nr,
        out_specs=[part(0), part(0), part(0), part(0),
                   pl.BlockSpec((None, 8, HG_WIDTH), lambda b, g: (b, 0, 0)),
                   pl.BlockSpec((None, 8, LANES), lambda b, g: (b, 0, 0))] + [ANY_SPEC] * nr,
        out_shape=[SDS((B, T, HG_WIDTH), BF16)] * 4 + [SDS((B, 8, HG_WIDTH), F32), SDS((B, 8, LANES), F32)]
        + _exchange_shapes(ride_srcs, ride_modes),
        scratch_shapes=[pltpu.VMEM((HG_HEADS, HG_HEAD_DIM, HG_HEAD_DIM), F32)] + _exchange_sems(nr),
        compiler_params=_params(("arbitrary", "arbitrary"), VMEM_LIMIT_BIG),
    )(proj_h, proj_h, proj_h, proj_h, lb, hg_w, o, s_prev, drg, *ride_srcs)


def _attn_bwd(qr, kr, proj3, attn_o, dan, tables, sinks, attn_w, ride_srcs, ride_modes):
    B, T, _ = proj3.shape
    nb = T // WINDOW
    splits = min(ATT_SPLITS, nb)
    per = nb // splits
    nr = len(ride_srcs)
    cos, sinl, sinr = tables
    QKV = ATT_WIDTH + 2 * LANES

    def body(*refs):
        qr_ref, kr_ref, v_ref, o_ref, dan_ref, cos_ref, sl_ref, sr_ref, sink_ref, aw_ref = refs[:10]
        ride_in = refs[10:10 + nr]
        dqkv_ref, dsink_ref, daw_ref = refs[10 + nr:13 + nr]
        ride_out = refs[13 + nr:13 + 2 * nr]
        kpad, vpad, dkpad, dvpad, dqb, dsk = refs[13 + 2 * nr:19 + 2 * nr]
        sems = refs[19 + 2 * nr:]
        part = pl.program_id(1)
        step = pl.program_id(0) * splits + part
        _ride_start(ride_modes, step, B * splits, ride_in, ride_out, sems)

        @pl.when(part == 0)
        def _():
            kpad[0:WINDOW, :] = jnp.zeros((WINDOW, LANES), BF16)
            vpad[0:WINDOW, :] = jnp.zeros((WINDOW, LANES), BF16)
            kpad[WINDOW:, :] = kr_ref[...]
            vpad[WINDOW:, :] = _bf(v_ref[...])
            dkpad[...] = jnp.zeros(dkpad.shape, F32)
            dvpad[...] = jnp.zeros(dvpad.shape, F32)
            dsk[...] = jnp.zeros(dsk.shape, F32)
            daw_ref[...] = jnp.zeros(daw_ref.shape, F32)

        lower = _lower_mask()
        aw = aw_ref[...]

        def block(n, daw):
            r0 = pl.multiple_of(n * WINDOW, WINDOW)
            rows = pl.ds(r0, WINDOW)
            nxt = pl.ds(r0 + WINDOW, WINDOW)
            ob = o_ref[rows, :]
            dn = dan_ref[rows, :]
            ro = lax.rsqrt(_mean_last(ob * ob) + EPS)
            oh = ob * ro
            daw = daw + _sum_rows(dn * oh)
            doh = dn * aw
            do = _bf(ro * (doh - oh * _mean_last(doh * oh)))
            doparts = [do[:, j * LANES:(j + 1) * LANES] for j in range(ATT_WIDTH // LANES)]
            qparts = [qr_ref[rows, j * LANES:(j + 1) * LANES] for j in range(ATT_WIDTH // LANES)]
            for hk in range(ATT_KV_HEADS):
                lanes = slice(hk * ATT_HEAD_DIM, (hk + 1) * ATT_HEAD_DIM)
                qs = _stack_heads(qparts, hk)
                dos = _stack_heads(doparts, hk)
                k_cur, k_prev = kpad[nxt, lanes], kpad[rows, lanes]
                v_cur, v_prev = vpad[nxt, lanes], vpad[rows, lanes]
                p, inv, es = _softmax_window(qs, k_cur, k_prev, lower, n > 0, _sink_row(sink_ref, hk))
                p = p * inv
                dp = jnp.where(lower, _dot_nt(v_cur, dos), _dot_nt(v_prev, dos))
                delta = jnp.sum(p * dp, axis=0, keepdims=True)
                ds = p * (dp - delta)
                sk = (es * inv) * delta
                ds_cur = jnp.where(lower, ds, 0.0)
                p_cur = jnp.where(lower, p, 0.0)
                ds_cur, ds_prev = _bf(ds_cur), _bf(ds - ds_cur)
                p_cur, p_prev = _bf(p_cur), _bf(p - p_cur)
                dqt = (_dot_tn(k_cur, ds_cur) + _dot_tn(k_prev, ds_prev)) * ATT_SCALE
                dkpad[nxt, lanes] += _dot(ds_cur, qs)
                dkpad[rows, lanes] += _dot(ds_prev, qs)
                dvpad[nxt, lanes] += _dot(p_cur, dos)
                dvpad[rows, lanes] += _dot(p_prev, dos)
                for g in range(ATT_GROUP):
                    h = ATT_GROUP * hk + g
                    cols = slice(g * WINDOW, (g + 1) * WINDOW)
                    dqb[:, h * ATT_HEAD_DIM:(h + 1) * ATT_HEAD_DIM] = dqt[:, cols].T
                    head_lane = lax.broadcasted_iota(jnp.int32, dsk.shape, 1) == h
                    dsk[...] += jnp.where(head_lane, -jnp.sum(sk[:, cols], axis=1, keepdims=True), 0.0)
            cs, sl, sr = cos_ref[rows, :], sl_ref[rows, :], sr_ref[rows, :]
            for j in range(ATT_WIDTH // LANES):
                dqkv_ref[rows, j * LANES:(j + 1) * LANES] = _bf(_rope_t(dqb[:, j * LANES:(j + 1) * LANES], cs, sl, sr))
            return daw

        daw = _loop_pairs(part * per, per, block, jnp.zeros((1, ATT_WIDTH), F32))
        daw_ref[...] += jnp.broadcast_to(daw, (8, ATT_WIDTH))
        dsink_ref[...] = dsk[...]

        def finish(n, carry):
            r0 = pl.multiple_of(n * WINDOW, WINDOW)
            rows = pl.ds(r0, WINDOW)
            nxt = pl.ds(r0 + WINDOW, WINDOW)
            cs, sl, sr = cos_ref[rows, :], sl_ref[rows, :], sr_ref[rows, :]
            dqkv_ref[rows, ATT_WIDTH:ATT_WIDTH + LANES] = _bf(_rope_t(dkpad[nxt, :], cs, sl, sr))
            dqkv_ref[rows, ATT_WIDTH + LANES:QKV] = _bf(dvpad[nxt, :])
            return carry

        @pl.when(part == splits - 1)
        def _():
            lax.fori_loop(0, nb, finish, 0)

        _ride_wait(ride_modes, step, B * splits, ride_in, ride_out, sems)

    seq = lambda w, j: pl.BlockSpec((None, T, w), lambda b, s: (b, 0, j))
    full = lambda r, w: pl.BlockSpec((r, w), lambda b, s: (0, 0))
    return pl.pallas_call(
        body, name="attn_bwd", grid=(B, splits),
        in_specs=[seq(ATT_WIDTH, 0), seq(LANES, 0), seq(LANES, 5), seq(ATT_WIDTH, 0), seq(ATT_WIDTH, 0),
                  full(T, LANES), full(T, LANES), full(T, LANES),
                  pl.BlockSpec(memory_space=pltpu.SMEM), full(1, ATT_WIDTH)] + [ANY_SPEC] * nr,
        out_specs=[seq(QKV, 0), pl.BlockSpec((None, 8, LANES), lambda b, s: (b, 0, 0)),
                   pl.BlockSpec((None, 8, ATT_WIDTH), lambda b, s: (b, 0, 0))] + [ANY_SPEC] * nr,
        out_shape=[SDS((B, T, QKV), BF16), SDS((B, 8, LANES), F32), SDS((B, 8, ATT_WIDTH), F32)]
        + _exchange_shapes(ride_srcs, ride_modes),
        scratch_shapes=[pltpu.VMEM((T + WINDOW, LANES), BF16), pltpu.VMEM((T + WINDOW, LANES), BF16),
                        pltpu.VMEM((T + WINDOW, LANES), F32), pltpu.VMEM((T + WINDOW, LANES), F32),
                        pltpu.VMEM((WINDOW, ATT_WIDTH), F32), pltpu.VMEM((8, LANES), F32)] + _exchange_sems(nr),
        compiler_params=_params(("arbitrary", "arbitrary"), VMEM_LIMIT_BIG),
    )(qr, kr, proj3, attn_o, dan, cos, sinl, sinr, sinks, attn_w, *ride_srcs)


def _in_bwd(x2, dx1, dqkv, dhq, dhf, dhi, dhg, mod8, pre_w, w_in_bf, T, ride_srcs, ride_modes):
    N = x2.shape[0]
    TM = _tile_rows(T, big=True)
    tps = T // TM
    nr = len(ride_srcs)

    def body(*refs):
        x_ref, dx_ref, p0, p1, p2, p3, p4, mod_ref, pw_ref, w_ref = refs[:10]
        ride_in = refs[10:10 + nr]
        gx_ref, dproj_ref, acc_ref = refs[10 + nr:13 + nr]
        ride_out = refs[13 + nr:13 + 2 * nr]
        sems = refs[13 + 2 * nr:]
        _ride_start(ride_modes, pl.program_id(0), N // TM, ride_in, ride_out, sems)
        sc1 = mod_ref[1:2, :]
        dproj = jnp.concatenate([ref[...] for ref in (p0, p1, p2, p3, p4)], axis=1)
        dproj_ref[...] = dproj
        dh = _dot(dproj, w_ref[...])
        x = x_ref[...]
        r = lax.rsqrt(_mean_last(x * x) + EPS)
        xh = x * r
        n1 = xh * pw_ref[...]
        dsh1 = _sum_rows(dh)
        dsc1 = _sum_rows(dh * n1)
        dn1 = dh * (1.0 + sc1)
        dw_pre = _sum_rows(dn1 * xh)
        dxh = dn1 * pw_ref[...]
        gx_ref[...] = dx_ref[...] + r * (dxh - xh * _mean_last(dxh * xh))
        _acc_rows(acc_ref, pl.program_id(0) % tps == 0, [dsh1, dsc1, dw_pre])
        _ride_wait(ride_modes, pl.program_id(0), N // TM, ride_in, ride_out, sems)

    row = lambda w: pl.BlockSpec((TM, w), lambda i: (i, 0))
    B = N // T
    return pl.pallas_call(
        body, name="in_bwd", grid=(N // TM,),
        in_specs=[row(D_MODEL), row(D_MODEL), row(768), row(HG_WIDTH), row(HG_WIDTH), row(HG_WIDTH),
                  row(HG_WIDTH), _mod_spec(tps), pl.BlockSpec((1, D_MODEL), lambda i: (0, 0)),
                  pl.BlockSpec((IN_COLS, D_MODEL), lambda i: (0, 0))] + [ANY_SPEC] * nr,
        out_specs=[row(D_MODEL), row(IN_COLS), _mod_spec(tps)] + [ANY_SPEC] * nr,
        out_shape=[SDS((N, D_MODEL), F32), SDS((N, IN_COLS), BF16), SDS((B, 8, D_MODEL), F32)]
        + _exchange_shapes(ride_srcs, ride_modes),
        scratch_shapes=_exchange_sems(nr),
        compiler_params=_params(("arbitrary",), VMEM_LIMIT_BIG),
    )(x2, dx1, dqkv, dhq, dhf, dhi, dhg, mod8, pre_w, w_in_bf, *ride_srcs)


def _matmul_tn(name, a, b, tn, tm=512, by_owner_cols=False):
    K, M = a.shape
    Nc = b.shape[1]
    tm = min(tm, M)

    def body(a_ref, b_ref, o_ref):
        o_ref[...] = _bf(_dot_tn(a_ref[...], b_ref[...]))

    if by_owner_cols:
        assert tn * N_DEV == Nc
        out_shape = SDS((N_DEV, M, tn), BF16)
        out_spec = pl.BlockSpec((None, tm, tn), lambda i, j: (j, i, 0))
    else:
        out_shape = SDS((M, Nc), BF16)
        out_spec = pl.BlockSpec((tm, tn), lambda i, j: (i, j))
    return pl.pallas_call(
        body, name=name, grid=(M // tm, Nc // tn),
        in_specs=[pl.BlockSpec((K, tm), lambda i, j: (0, i)),
                  pl.BlockSpec((K, tn), lambda i, j: (0, j))],
        out_specs=out_spec, out_shape=out_shape,
        compiler_params=_params(("arbitrary", "arbitrary"), VMEM_LIMIT_BIG),
    )(a, b)


def _adamw_math(w, g, m, v):
    m2 = ADAM_B1 * m + (1.0 - ADAM_B1) * g
    v2 = ADAM_B2 * v + (1.0 - ADAM_B2) * (g * g)
    m_hat = m2 / (1.0 - ADAM_B1 ** ADAM_STEP)
    v_hat = v2 / (1.0 - ADAM_B2 ** ADAM_STEP)
    delta = -ADAM_LR * (m_hat / (jnp.sqrt(v_hat) + ADAM_EPS) + ADAM_WD * w)
    return delta, m2, v2


def _pair_add(name, gw, theirs):
    chips, _, r, c = gw.shape
    tr = r
    core = lax.axis_index("c").astype(jnp.int32).reshape(1)

    def body(core_ref, mine_ref, theirs_ref, o_ref):
        o_ref[...] = _bf(mine_ref[...].astype(F32) + theirs_ref[...].astype(F32))

    block = pl.BlockSpec((None, tr, c), lambda s, i, core_ref: (s, i, 0))
    grid_spec = pltpu.PrefetchScalarGridSpec(
        num_scalar_prefetch=1, grid=(chips, r // tr),
        in_specs=[pl.BlockSpec((None, None, tr, c), lambda s, i, core_ref: (s, core_ref[0], i, 0)), block],
        out_specs=block)
    return pl.pallas_call(
        body, name=name, grid_spec=grid_spec, out_shape=SDS((chips, r, c), BF16),
        compiler_params=_params(("arbitrary", "arbitrary")),
    )(core, gw, theirs)


def _reduce_adamw(name, parts, w, m, v):
    r, c = w.shape
    tr = r if r % 256 else 256
    slots = parts.shape[0]

    def body(p_ref, w_ref, m_ref, v_ref, g_ref, d_ref, m2_ref, v2_ref):
        g = p_ref[0].astype(F32)
        for s in range(1, slots):
            g = g + p_ref[s].astype(F32)
        g_ref[...] = g
        d_ref[...], m2_ref[...], v2_ref[...] = _adamw_math(w_ref[...], g, m_ref[...], v_ref[...])

    blk = pl.BlockSpec((tr, c), lambda i: (i, 0))
    return pl.pallas_call(
        body, name=name, grid=(r // tr,),
        in_specs=[pl.BlockSpec((slots, tr, c), lambda i: (0, i, 0)), blk, blk, blk],
        out_specs=[blk] * 4, out_shape=[SDS((r, c), F32)] * 4,
        compiler_params=_params(("arbitrary",), VMEM_LIMIT_BIG),
    )(parts, w, m, v)


def _ada_grad_adamw(c_all, dmod_all, w, m, v):
    r, c = w.shape
    tr = 256
    nb = c_all.shape[0]

    def body(c_ref, dm_ref, w_ref, m_ref, v_ref, g_ref, d_ref, m2_ref, v2_ref):
        cv = c_ref[...]
        g = _dot_tn(cv * _sigmoid(cv), dm_ref[...])
        g_ref[...] = g
        d_ref[...], m2_ref[...], v2_ref[...] = _adamw_math(w_ref[...], g, m_ref[...], v_ref[...])

    blk = pl.BlockSpec((tr, c), lambda i: (i, 0))
    return pl.pallas_call(
        body, name="ada_grad_adamw", grid=(r // tr,),
        in_specs=[pl.BlockSpec((nb, tr), lambda i: (0, i)), pl.BlockSpec((nb, c), lambda i: (0, 0)),
                  blk, blk, blk],
        out_specs=[blk] * 4, out_shape=[SDS((r, c), F32)] * 4,
        compiler_params=_params(("arbitrary",)),
    )(c_all, dmod_all, w, m, v)


_SMALL = [("b_ada", 6144), ("pre_w_mix", 1024), ("attn_sinks", 128), ("attn_out_w", 512), ("lb_table", 1024),
          ("hg_norm_w", 128), ("post_w_mix", 1024), ("pre_w_mlp", 1024), ("post_w_mlp", 1024)]


def _pack_small(acc_in, acc_mix, acc_mlp, dsink, daw, dlb, dgw, lb_p, ada_cols):
    B = acc_in.shape[0]
    width = sum(w for _, w in _SMALL) + LANES

    def body(ain, amix, amlp, dsk_ref, daw_ref, dlb_ref, dgw_ref, lbp_ref, packed_ref, dmod_ref):
        def total(ref, r, w=None):
            out = ref[0, r:r + 1, :] if w is None else ref[0, r:r + 1, :w]
            for b in range(1, B):
                out = out + (ref[b, r:r + 1, :] if w is None else ref[b, r:r + 1, :w])
            return out

        d_b_ada = None
        for b in range(B):
            mods = [ain[b, 0:1, :], ain[b, 1:2, :], amix[b, 0:1, :], amlp[b, 0:1, :], amlp[b, 1:2, :], amlp[b, 2:3, :]]
            full = jnp.concatenate(mods, axis=1)
            for j in range(N_DEV):
                dmod_ref[j, b:b + 1, :] = full[:, j * ada_cols:(j + 1) * ada_cols]
            d_b_ada = full if d_b_ada is None else d_b_ada + full
        d_lb = total(dlb_ref, 0)
        pp = lbp_ref[0:1, :] * lbp_ref[1:2, :]
        pieces = [d_b_ada, total(ain, 2), total(dsk_ref, 0), total(daw_ref, 0), -d_lb * pp, d_lb * pp,
                  total(dgw_ref, 0), total(amix, 1), total(amlp, 3), total(amlp, 4), total(amlp, 5, LANES)]
        off = 0
        for piece in pieces:
            packed_ref[:, off:off + piece.shape[1]] = piece
            off += piece.shape[1]

    return pl.pallas_call(
        body, name="pack_small",
        out_shape=[SDS((1, width), F32), SDS((N_DEV, B, ada_cols), F32)],
    )(acc_in, acc_mix, acc_mlp, dsink, daw, dlb, dgw, lb_p)


def _adamw_small(parts, given):
    names = [n for n, _ in _SMALL]
    flat_in = [a for n in names for a in given[n]]

    def body(*refs):
        p_ref = refs[0]
        in_refs = refs[1:1 + 3 * len(names)]
        out_refs = refs[1 + 3 * len(names):-1]
        loss_ref = refs[-1]
        g = p_ref[0]
        for s in range(1, N_DEV):
            g = g + p_ref[s]
        off = 0
        for i, (name, width) in enumerate(_SMALL):
            w_ref, m_ref, v_ref = in_refs[3 * i:3 * i + 3]
            rows, cols = w_ref.shape
            for r in range(rows):
                gr = g[:, off + r * cols:off + (r + 1) * cols]
                res = (gr,) + _adamw_math(w_ref[r:r + 1, :], gr, m_ref[r:r + 1, :], v_ref[r:r + 1, :])
                for o_ref, val in zip(out_refs[4 * i:4 * i + 4], res):
                    o_ref[r:r + 1, :] = val
            off += width
        loss_ref[...] = g[:, off:off + LANES]

    out_shape = [SDS(given[n][0].shape, F32) for n in names for _ in range(4)] + [SDS((1, LANES), F32)]
    outs = pl.pallas_call(body, name="adamw_small", out_shape=out_shape)(parts, *flat_in)
    return {n: tuple(outs[4 * i:4 * i + 4]) for i, n in enumerate(names)}, outs[-1][0, 0]


def kernel(x, c, w_ada, b_ada, pre_w_mix, w_in, attn_sinks, attn_out_w, lb_table, hg_norm_w, w_out, post_w_mix, pre_w_mlp, w_up, w_down, post_w_mlp, loss_target, m_w_ada, m_b_ada, m_pre_w_mix, m_w_in, m_attn_sinks, m_attn_out_w, m_lb_table, m_hg_norm_w, m_w_out, m_post_w_mix, m_pre_w_mlp, m_w_up, m_w_down, m_post_w_mlp, v_w_ada, v_b_ada, v_pre_w_mix, v_w_in, v_attn_sinks, v_attn_out_w, v_lb_table, v_hg_norm_w, v_w_out, v_post_w_mix, v_pre_w_mlp, v_w_up, v_w_down, v_post_w_mlp):
    B, T, _ = x.shape
    N = B * T
    me = 4 * lax.axis_index("x") + 2 * lax.axis_index("y") + lax.axis_index("c")
    x2 = x.reshape(N, D_MODEL)
    tgt2 = loss_target.reshape(N, D_MODEL)

    w_in_t, m_w_in_t, v_w_in_t = w_in[0].T, m_w_in[0].T, v_w_in[0].T
    w_in_g, c_g = _exchange("gather_w_in", [_bf(w_in_t), c], ["gather"] * 2)
    w_in_f = w_in_g.reshape(IN_COLS, D_MODEL)
    c_all = c_g.reshape(N_DEV * B, D_MODEL)

    ada_cols = w_ada.shape[2]
    b_mine = lax.dynamic_slice(b_ada, (0, me * ada_cols), (1, ada_cols))
    mod_cols = _ada_mod(c_all, w_ada[0], b_mine)
    (mod_g,) = _exchange("scatter_mod", [mod_cols.reshape(N_DEV, B, ada_cols)], ["a2a"])
    mod = mod_g.transpose(1, 0, 2).reshape(B, 6, D_MODEL)
    mod8 = jnp.pad(mod, ((0, 0), (0, 2), (0, 0)))

    lb_p = jax.nn.softmax(lb_table, axis=0)
    lb = lb_p[1:2]
    tables = _rope_tables(T)

    w_up_b, w_down_b = _bf(w_up[0]), _bf(w_down[0])
    proj_a, proj_h, h1, w_out_g, w_up_g0 = _in_proj(x2, mod8, pre_w_mix, w_in_f, T,
                                                    [_bf(w_out[0]), w_up_b[:MLP_HALF]], ["gather"] * 2)
    proj3 = proj_a.reshape(B, T, ATT_COLS)
    proj_h = proj_h.reshape(B, T, IN_COLS - ATT_COLS)
    rec_o, rec_g, s_prev, w_up_g1 = _hgrn_fwd(proj_h, lb, hg_norm_w, [w_up_b[MLP_HALF:]], ["gather"])
    attn_o, attn_n, qr, kr, w_down_g0 = _attn_fwd(proj3, tables, attn_sinks, attn_out_w,
                                                  [w_down_b[:, :MLP_HALF]], ["gather"])
    w_out_f = w_out_g.reshape(D_MODEL, D_MODEL)
    mix, x1, cat, w_down_g1 = _mix_out(x2, attn_n.reshape(N, ATT_WIDTH), rec_g.reshape(N, HG_WIDTH), mod8,
                                       post_w_mix, w_out_f, T, [w_down_b[:, MLP_HALF:]], ["gather"])
    w_up_halves = [w_up_g0, w_up_g1]
    w_down_halves = [w_down_g0.reshape(D_FF, MLP_HALF), w_down_g1.reshape(D_FF, MLP_HALF)]
    up, u, d, h2 = _mlp_fwd(x1, mod8, pre_w_mlp, w_up_halves, w_down_halves, T)

    dx1, dup, dd, acc_mlp = _mlp_bwd(x1, d, up, tgt2, mod8, pre_w_mlp, post_w_mlp, w_up_halves, w_down_halves, T)
    chips = N_DEV // 2
    by_chip = lambda a: a.reshape((chips, 2, a.shape[0] // N_DEV) + a.shape[1:])
    gw_up = _matmul_tn("grad_w_up", h2, dup, D_FF // N_DEV, tm=D_MODEL, by_owner_cols=True)
    gw_up = gw_up.reshape(chips, 2, D_MODEL, D_FF // N_DEV)
    gw_down = by_chip(_matmul_tn("grad_w_down", u, dd, D_MODEL))
    dan, drg, dmix, acc_mix, q_down, q_up = _mix_bwd(mix, dx1, mod8, post_w_mix, w_out_f, T,
                                                     [gw_down, gw_up], ["pair"] * 2)
    p_down, p_up = _pair_add("pair_add_w_down", gw_down, q_down), _pair_add("pair_add_w_up", gw_up, q_up)
    gw_out = _matmul_tn("grad_w_out", cat, dmix, 512, tm=D_MODEL).reshape(N_DEV, D_MODEL // N_DEV, D_MODEL)
    dhq, dhf, dhi, dhg, dlb_p, dgw_p, r_down, r_up = _hgrn_bwd(
        proj_h, lb, hg_norm_w, rec_o, s_prev, drg.reshape(B, T, HG_WIDTH), [p_down, p_up], ["chips"] * 2)
    dqkv, dsink_p, daw_p, r_out = _attn_bwd(qr, kr, proj3, attn_o, dan.reshape(B, T, ATT_WIDTH), tables,
                                            attn_sinks, attn_out_w, [gw_out], ["a2a"])
    flat = lambda a: a.reshape(N, a.shape[-1])
    grad_x, dproj, acc_in = _in_bwd(x2, dx1, flat(dqkv), flat(dhq), flat(dhf), flat(dhi), flat(dhg),
                                    mod8, pre_w_mix, w_in_f, T, [], [])

    gw_in = by_chip(_matmul_tn("grad_w_in", dproj, h1, 512, tm=IN_COLS // 2))
    (q_in,) = _exchange("pair_w_in", [gw_in], ["pair"])
    p_in = _pair_add("pair_add_w_in", gw_in, q_in)

    packed, dmod_blocks = _pack_small(acc_in, acc_mix, acc_mlp, dsink_p, daw_p, dlb_p, dgw_p, lb_p, ada_cols)
    r_in, r_dmod, r_small = _exchange("reduce_grads", [p_in, dmod_blocks, packed], ["chips", "a2a", "gather"])

    res = {}
    res["w_in"] = tuple(a.T for a in _reduce_adamw("adamw_w_in", r_in, w_in_t, m_w_in_t, v_w_in_t))
    res["w_out"] = _reduce_adamw("adamw_w_out", r_out, w_out[0], m_w_out[0], v_w_out[0])
    res["w_up"] = _reduce_adamw("adamw_w_up", r_up, w_up[0], m_w_up[0], v_w_up[0])
    res["w_down"] = _reduce_adamw("adamw_w_down", r_down, w_down[0], m_w_down[0], v_w_down[0])
    res["w_ada"] = _ada_grad_adamw(c_all, r_dmod.reshape(N_DEV * B, ada_cols), w_ada[0], m_w_ada[0], v_w_ada[0])

    given = dict(b_ada=(b_ada, m_b_ada, v_b_ada), pre_w_mix=(pre_w_mix, m_pre_w_mix, v_pre_w_mix),
                 attn_sinks=(attn_sinks, m_attn_sinks, v_attn_sinks),
                 attn_out_w=(attn_out_w, m_attn_out_w, v_attn_out_w), lb_table=(lb_table, m_lb_table, v_lb_table),
                 hg_norm_w=(hg_norm_w, m_hg_norm_w, v_hg_norm_w), post_w_mix=(post_w_mix, m_post_w_mix, v_post_w_mix),
                 pre_w_mlp=(pre_w_mlp, m_pre_w_mlp, v_pre_w_mlp), post_w_mlp=(post_w_mlp, m_post_w_mlp, v_post_w_mlp))
    small_res, loss = _adamw_small(r_small, given)
    res.update(small_res)

    order = ["w_ada", "b_ada", "pre_w_mix", "w_in", "attn_sinks", "attn_out_w", "lb_table", "hg_norm_w", "w_out",
             "post_w_mix", "pre_w_mlp", "w_up", "w_down", "post_w_mlp"]
    big = {"w_ada", "w_in", "w_out", "w_up", "w_down"}
    outs = [loss, grad_x.reshape(B, T, D_MODEL)]
    for i in range(4):
        for k in order:
            a = res[k][i]
            outs.append(a[None] if k in big else a)
    return tuple(outs)
```

```python
import jax
import jax.numpy as jnp
import numpy as np
from jax import lax
from jax.experimental import pallas as pl
from jax.experimental.pallas import tpu as pltpu

F32 = jnp.float32
BF16 = jnp.bfloat16
SDS = jax.ShapeDtypeStruct

D_MODEL = 1024
ATT_WIDTH = 512
ATT_HEAD_DIM = 64
ATT_KV_HEADS = 2
ATT_GROUP = 4
WINDOW = 128
ROPE_DIM = 16
ROPE_THETA = 500000.0
HG_WIDTH = 512
HG_HEAD_DIM = 128
HG_HEADS = 4
HG_CHUNK = 32
IN_COLS = 2816
ATT_COLS = 768
D_FF = 4096
EPS = 1e-6
N_DEV = 8

ADAM_LR = 0.001
ADAM_B1 = 0.9
ADAM_B2 = 0.999
ADAM_EPS = 1e-08
ADAM_WD = 0.01
ADAM_STEP = 10

VMEM_LIMIT_BIG = 56 << 20
LANES = 128

MESH = pl.DeviceIdType.MESH
NT_DIMS = (((1,), (1,)), ((), ()))
TN_DIMS = (((0,), (0,)), ((), ()))


def _dot(a, b):
    return jnp.dot(a, b, preferred_element_type=F32)


def _dot_nt(a, b):
    return lax.dot_general(a, b, NT_DIMS, preferred_element_type=F32)


def _dot_tn(a, b):
    return lax.dot_general(a, b, TN_DIMS, preferred_element_type=F32)


def _bf(a):
    return a.astype(BF16)


def _sigmoid(a):
    return 0.5 * jnp.tanh(0.5 * a) + 0.5


def _mean_last(a):
    return jnp.mean(a, axis=-1, keepdims=True)


def _sum_rows(a):
    return jnp.sum(a, axis=0, keepdims=True)


def _loop_pairs(first, count, body, init, per_trip=2):
    if count % per_trip:
        return lax.fori_loop(first, first + count, body, init)

    def trip(i, c):
        for k in range(per_trip):
            c = body(first + per_trip * i + k, c)
        return c

    return lax.fori_loop(0, count // per_trip, trip, init)


def _params(sem=None, vmem=None):
    kw = {}
    if sem is not None:
        kw["dimension_semantics"] = sem
    if vmem is not None:
        kw["vmem_limit_bytes"] = vmem
    return pltpu.CompilerParams(**kw)


ANY_SPEC = pl.BlockSpec(memory_space=pl.ANY)


def _exchange_shapes(srcs, modes):
    out_shape = []
    for s, m in zip(srcs, modes):
        shp = {"gather": (N_DEV,) + tuple(s.shape), "pair": (s.shape[0],) + tuple(s.shape[2:])}.get(m, tuple(s.shape))
        out_shape.append(SDS(shp, s.dtype))
    return out_shape


def _exchange_sems(n):
    if n == 0:
        return []
    return [pltpu.SemaphoreType.DMA((n, N_DEV - 1)), pltpu.SemaphoreType.DMA((n, N_DEV - 1)),
            pltpu.SemaphoreType.DMA((n,))]


SIBLING = 1
OTHER_CHIPS = (2, 4, 6)


def _related(k):
    x, y, c = lax.axis_index("x"), lax.axis_index("y"), lax.axis_index("c")
    px, py, pc = x ^ ((k >> 2) & 1), y ^ ((k >> 1) & 1), c ^ (k & 1)
    return (px, py, pc), 4 * px + 2 * py + pc


def _exchange_phases(modes, src_refs, out_refs, send_sems, recv_sems, own_sems):
    _, me = _related(0)
    sib_dev, sib = _related(SIBLING)
    start, middle, end = [], [], []

    def remote(a, i, src, dst, dev):
        return pltpu.make_async_remote_copy(src_ref=src, dst_ref=dst, send_sem=send_sems.at[a, i],
                                            recv_sem=recv_sems.at[a, i], device_id=dev, device_id_type=MESH)

    for a, mode in enumerate(modes):
        out = out_refs[a]
        if mode == "gather":
            src = src_refs[a]
            own = pltpu.make_async_copy(src, out.at[me], own_sems.at[a])
            to_sib = remote(a, 0, src, out.at[me], sib_dev)
            start += [own.start, to_sib.start]
            end += [remote(a, 0, src, out.at[sib], sib_dev).wait_recv, to_sib.wait_send, own.wait]
            for j, k in enumerate(OTHER_CHIPS, start=1):
                dev, peer = _related(k)
                _, peer_sib = _related(k ^ SIBLING)
                send = remote(a, j, src, out.at[me], dev)
                passed = remote(a, 3 + j, out.at[peer], out.at[peer], sib_dev)
                start.append(send.start)
                middle += [remote(a, j, src, out.at[peer], dev).wait_recv, passed.start]
                end += [remote(a, 3 + j, out.at[peer_sib], out.at[peer_sib], sib_dev).wait_recv,
                        send.wait_send, passed.wait_send]
        elif mode == "pair":
            core = lax.axis_index("c")
            for s in range(N_DEV // 2):
                send = remote(a, s, src_refs[a].at[s, 1 - core], out.at[s], sib_dev)
                start.append(send.start)
                end += [remote(a, s, src_refs[a].at[s, 1 - core], out.at[s], sib_dev).wait_recv, send.wait_send]
        elif mode == "chips":
            chip = me // 2
            own = pltpu.make_async_copy(src_refs[a].at[chip], out.at[chip], own_sems.at[a])
            start.append(own.start)
            end.append(own.wait)
            for j, k in enumerate(OTHER_CHIPS, start=1):
                dev, peer = _related(k)
                send = remote(a, j, src_refs[a].at[peer // 2], out.at[chip], dev)
                start.append(send.start)
                end += [remote(a, j, src_refs[a].at[peer // 2], out.at[peer // 2], dev).wait_recv, send.wait_send]
        else:
            own = pltpu.make_async_copy(src_refs[a].at[me], out.at[me], own_sems.at[a])
            start.append(own.start)
            end.append(own.wait)
            for k in range(1, N_DEV):
                dev, peer = _related(k)
                send = remote(a, k - 1, src_refs[a].at[peer], out.at[me], dev)
                start.append(send.start)
                end += [remote(a, k - 1, src_refs[a].at[peer], out.at[peer], dev).wait_recv, send.wait_send]
    return start, middle, end


def _run(actions):
    for act in actions:
        act()


def _exchange(name, srcs, modes):
    n = len(srcs)

    def body(*refs):
        start, middle, end = _exchange_phases(modes, refs[:n], refs[n:2 * n], *refs[2 * n:])
        _run(start)
        _run(middle)
        _run(end)

    return pl.pallas_call(
        body, name=name, out_shape=_exchange_shapes(srcs, modes),
        in_specs=[ANY_SPEC] * n, out_specs=[ANY_SPEC] * n,
        scratch_shapes=_exchange_sems(n),
    )(*srcs)


def _ride_start(modes, step, steps, src_refs, out_refs, sems):
    if not modes:
        return
    middle_step = steps - 1

    @pl.when(step == 0)
    def _():
        _run(_exchange_phases(modes, src_refs, out_refs, *sems)[0])

    if "gather" in modes:
        @pl.when(step == middle_step)
        def _():
            _run(_exchange_phases(modes, src_refs, out_refs, *sems)[1])


def _ride_wait(modes, step, steps, src_refs, out_refs, sems):
    if not modes:
        return

    @pl.when(step == steps - 1)
    def _():
        _run(_exchange_phases(modes, src_refs, out_refs, *sems)[2])


def _ada_mod(c_all, w_ada, b_ada_mine):
    nb, cols = c_all.shape[0], w_ada.shape[1]

    def body(c_ref, w_ref, b_ref, o_ref):
        cv = c_ref[...]
        ca = cv * _sigmoid(cv)
        o_ref[...] = _dot(ca, w_ref[...]) + b_ref[...]

    return pl.pallas_call(body, name="ada_mod", out_shape=SDS((nb, cols), F32))(c_all, w_ada, b_ada_mine)


def _tile_rows(T, big=False):
    return min(512 if big else 256, T)


def _mod_spec(tps):
    return pl.BlockSpec((None, 8, D_MODEL), lambda i: (i // tps, 0, 0))


def _in_proj(x2, mod8, pre_w, w_in_bf, T, ride_srcs, ride_modes):
    N = x2.shape[0]
    TM = _tile_rows(T, big=True)
    tps = T // TM
    nr = len(ride_srcs)

    def body(*refs):
        x_ref, mod_ref, pw_ref, w_ref = refs[:4]
        ride_in = refs[4:4 + nr]
        pa_ref, ph_ref, h1_ref = refs[4 + nr:7 + nr]
        ride_out = refs[7 + nr:7 + 2 * nr]
        sems = refs[7 + 2 * nr:]
        _ride_start(ride_modes, pl.program_id(0), N // TM, ride_in, ride_out, sems)
        x = x_ref[...]
        r = lax.rsqrt(_mean_last(x * x) + EPS)
        h = (x * r * pw_ref[...]) * (1.0 + mod_ref[1:2, :]) + mod_ref[0:1, :]
        hb = _bf(h)
        h1_ref[...] = hb
        pa_ref[...] = _dot_nt(hb, w_ref[:ATT_COLS, :])
        ph_ref[...] = _dot_nt(hb, w_ref[ATT_COLS:, :])
        _ride_wait(ride_modes, pl.program_id(0), N // TM, ride_in, ride_out, sems)

    return pl.pallas_call(
        body, name="in_proj", grid=(N // TM,),
        in_specs=[pl.BlockSpec((TM, D_MODEL), lambda i: (i, 0)), _mod_spec(tps),
                  pl.BlockSpec((1, D_MODEL), lambda i: (0, 0)),
                  pl.BlockSpec((IN_COLS, D_MODEL), lambda i: (0, 0))] + [ANY_SPEC] * nr,
        out_specs=[pl.BlockSpec((TM, ATT_COLS), lambda i: (i, 0)),
                   pl.BlockSpec((TM, IN_COLS - ATT_COLS), lambda i: (i, 0)),
                   pl.BlockSpec((TM, D_MODEL), lambda i: (i, 0))] + [ANY_SPEC] * nr,
        out_shape=[SDS((N, ATT_COLS), F32), SDS((N, IN_COLS - ATT_COLS), F32), SDS((N, D_MODEL), BF16)]
        + _exchange_shapes(ride_srcs, ride_modes),
        scratch_shapes=_exchange_sems(nr),
        compiler_params=_params(("arbitrary",), VMEM_LIMIT_BIG),
    )(x2, mod8, pre_w, w_in_bf, *ride_srcs)


def _rope_tables(T):
    half = ROPE_DIM // 2
    f32 = np.float32
    inv_freq = (f32(ROPE_THETA) ** (-np.arange(0, ROPE_DIM, 2, dtype=f32) / f32(ROPE_DIM))).astype(f32)
    ang = np.arange(T, dtype=f32)[:, None] * inv_freq[None, :]
    cos, sin = np.cos(ang).astype(f32), np.sin(ang).astype(f32)
    ones = np.ones((T, ATT_HEAD_DIM - ROPE_DIM), f32)
    zeros = np.zeros((T, ATT_HEAD_DIM - ROPE_DIM), f32)
    zh = np.zeros((T, half), f32)
    cos64 = np.concatenate([cos, cos, ones], axis=1)
    sin_left = np.concatenate([-sin, zh, zeros], axis=1)
    sin_right = np.concatenate([zh, sin, zeros], axis=1)
    rep = LANES // ATT_HEAD_DIM
    return tuple(jnp.asarray(np.tile(t, (1, rep))) for t in (cos64, sin_left, sin_right))


def _rope(xc, cs, sl, sr):
    return xc * cs + pltpu.roll(xc, LANES - 8, 1) * sl + pltpu.roll(xc, 8, 1) * sr


def _rope_t(dy, cs, sl, sr):
    return dy * cs + pltpu.roll(dy * sl, 8, 1) + pltpu.roll(dy * sr, LANES - 8, 1)


ATT_SCALE = ATT_HEAD_DIM ** -0.5
ATT_SPLITS = 4


def _lower_mask():
    j = lax.broadcasted_iota(jnp.int32, (WINDOW, ATT_GROUP * WINDOW), 0)
    i = lax.broadcasted_iota(jnp.int32, (WINDOW, ATT_GROUP * WINDOW), 1) & (WINDOW - 1)
    return j <= i


def _sink_row(sink_ref, hk):
    return jnp.concatenate(
        [jnp.full((1, WINDOW), sink_ref[0, ATT_GROUP * hk + g], F32) for g in range(ATT_GROUP)], axis=1)


def _softmax_window(qs, k_cur, k_prev, lower, has_prev, sink):
    s_prev = jnp.where(has_prev, _dot_nt(k_prev, qs), jnp.finfo(F32).min)
    s = jnp.where(lower, _dot_nt(k_cur, qs), s_prev)
    m = jnp.maximum(jnp.max(s, axis=0, keepdims=True), sink)
    p = jnp.exp(s - m)
    es = jnp.exp(sink - m)
    inv = 1.0 / (jnp.sum(p, axis=0, keepdims=True) + es)
    return p, inv, es


def _stack_heads(parts, hk):
    hs = []
    for g in range(ATT_GROUP):
        h = ATT_GROUP * hk + g
        hs.append(parts[h // 2][:, (h % 2) * ATT_HEAD_DIM:(h % 2 + 1) * ATT_HEAD_DIM])
    return jnp.concatenate(hs, axis=0)


def _attn_fwd(proj3, tables, sinks, attn_w, ride_srcs, ride_modes):
    B, T, _ = proj3.shape
    nb = T // WINDOW
    splits = min(ATT_SPLITS, nb)
    per = nb // splits
    nr = len(ride_srcs)
    cos, sinl, sinr = tables

    def body(*refs):
        q_ref, k_ref, v_ref, cos_ref, sl_ref, sr_ref, sink_ref, aw_ref = refs[:8]
        ride_in = refs[8:8 + nr]
        o_ref, an_ref, qr_ref, kr_ref = refs[8 + nr:12 + nr]
        ride_out = refs[12 + nr:12 + 2 * nr]
        kpad, vpad = refs[12 + 2 * nr:14 + 2 * nr]
        sems = refs[14 + 2 * nr:]
        part = pl.program_id(1)
        step = pl.program_id(0) * splits + part
        _ride_start(ride_modes, step, B * splits, ride_in, ride_out, sems)

        @pl.when(part == 0)
        def _():
            kpad[0:WINDOW, :] = jnp.zeros((WINDOW, LANES), BF16)
            vpad[0:WINDOW, :] = jnp.zeros((WINDOW, LANES), BF16)

        lower = _lower_mask()

        def block(n, carry):
            r0 = pl.multiple_of(n * WINDOW, WINDOW)
            rows = pl.ds(r0, WINDOW)
            nxt = pl.ds(r0 + WINDOW, WINDOW)
            cs, sl, sr = cos_ref[rows, :], sl_ref[rows, :], sr_ref[rows, :]
            kb = _bf(_rope(k_ref[rows, :], cs, sl, sr))
            vb = _bf(v_ref[rows, :])
            kpad[nxt, :] = kb
            kr_ref[rows, :] = kb
            vpad[nxt, :] = vb
            qparts = []
            for j in range(ATT_WIDTH // LANES):
                qp = _bf(_rope(q_ref[rows, j * LANES:(j + 1) * LANES], cs, sl, sr) * ATT_SCALE)
                qr_ref[rows, j * LANES:(j + 1) * LANES] = qp
                qparts.append(qp)
            for hk in range(ATT_KV_HEADS):
                lanes = slice(hk * ATT_HEAD_DIM, (hk + 1) * ATT_HEAD_DIM)
                qs = _stack_heads(qparts, hk)
                p, inv, _ = _softmax_window(qs, kb[:, lanes], kpad[rows, lanes], lower, n > 0,
                                            _sink_row(sink_ref, hk))
                p_cur = jnp.where(lower, p, 0.0)
                ot = (_dot_tn(vb[:, lanes], _bf(p_cur)) + _dot_tn(vpad[rows, lanes], _bf(p - p_cur))) * inv
                for g in range(ATT_GROUP):
                    h = ATT_GROUP * hk + g
                    o_ref[rows, h * ATT_HEAD_DIM:(h + 1) * ATT_HEAD_DIM] = ot[:, g * WINDOW:(g + 1) * WINDOW].T
            ob = o_ref[rows, :]
            an_ref[rows, :] = _bf(ob * lax.rsqrt(_mean_last(ob * ob) + EPS) * aw_ref[...])
            return carry

        _loop_pairs(part * per, per, block, 0)
        _ride_wait(ride_modes, step, B * splits, ride_in, ride_out, sems)

    seq = lambda w, j: pl.BlockSpec((None, T, w), lambda b, s: (b, 0, j))
    full = lambda r, w: pl.BlockSpec((r, w), lambda b, s: (0, 0))
    return pl.pallas_call(
        body, name="attn_fwd", grid=(B, splits),
        in_specs=[seq(ATT_WIDTH, 0), seq(LANES, 4), seq(LANES, 5),
                  full(T, LANES), full(T, LANES), full(T, LANES),
                  pl.BlockSpec(memory_space=pltpu.SMEM), full(1, ATT_WIDTH)] + [ANY_SPEC] * nr,
        out_specs=[seq(ATT_WIDTH, 0), seq(ATT_WIDTH, 0), seq(ATT_WIDTH, 0), seq(LANES, 0)] + [ANY_SPEC] * nr,
        out_shape=[SDS((B, T, ATT_WIDTH), F32), SDS((B, T, ATT_WIDTH), BF16),
                   SDS((B, T, ATT_WIDTH), BF16), SDS((B, T, LANES), BF16)] + _exchange_shapes(ride_srcs, ride_modes),
        scratch_shapes=[pltpu.VMEM((T + WINDOW, LANES), BF16), pltpu.VMEM((T + WINDOW, LANES), BF16)]
        + _exchange_sems(nr),
        compiler_params=_params(("arbitrary", "arbitrary"), VMEM_LIMIT_BIG),
    )(proj3, proj3, proj3, cos, sinl, sinr, sinks, attn_w, *ride_srcs)


HG_GROUP = 8
HG_ROWS = HG_GROUP * HG_CHUNK


HG_STACK = HG_GROUP * HG_HEAD_DIM


def _group_mask():
    r = lax.broadcasted_iota(jnp.int32, (HG_ROWS, HG_ROWS), 0)
    c = lax.broadcasted_iota(jnp.int32, (HG_ROWS, HG_ROWS), 1)
    return ((r // HG_CHUNK) == (c // HG_CHUNK)) & (r >= c)


def _spread(a):
    blocks = []
    for c in range(HG_GROUP):
        above = jnp.zeros((c * HG_CHUNK, HG_HEAD_DIM), a.dtype)
        below = jnp.zeros(((HG_GROUP - 1 - c) * HG_CHUNK, HG_HEAD_DIM), a.dtype)
        blocks.append(jnp.concatenate([p for p in (above, a[_chunk_rows(c), :], below) if p.shape[0]], axis=0))
    return jnp.concatenate(blocks, axis=1)


def _pick(r):
    return jnp.concatenate([r[_chunk_rows(c), c * HG_HEAD_DIM:(c + 1) * HG_HEAD_DIM] for c in range(HG_GROUP)], axis=0)


def _lane_block(a, c):
    return a[:, c * HG_HEAD_DIM:(c + 1) * HG_HEAD_DIM]


def _chunk_cumsum(a, reverse=False):
    n = a.shape[0]
    pos = lax.broadcasted_iota(jnp.int32, a.shape, 0) % HG_CHUNK
    shift = 1
    while shift < HG_CHUNK:
        if reverse:
            a = a + jnp.where(pos < HG_CHUNK - shift, pltpu.roll(a, n - shift, 0), 0.0)
        else:
            a = a + jnp.where(pos >= shift, pltpu.roll(a, shift, 0), 0.0)
        shift *= 2
    return a


def _chunk_bcast(rows_1x128):
    return jnp.concatenate([jnp.broadcast_to(r, (HG_CHUNK, HG_HEAD_DIM)) for r in rows_1x128], axis=0)


def _hgrn_gates(hq, hf, lb):
    sq = _sigmoid(hq)
    q = hq * sq
    sg = _sigmoid(hf)
    f = lb + (1.0 - lb) * sg
    k = 1.0 - f
    logf = jnp.log(f)
    b = _chunk_cumsum(logf)
    bl = [_sum_rows(logf[_chunk_rows(c), :]) for c in range(HG_GROUP)]
    eb, enb, e2 = jnp.exp(b), jnp.exp(-b), jnp.exp(_chunk_bcast(bl) - b)
    ebl = [jnp.exp(r) for r in bl]
    return dict(sq=sq, sg=sg, f=f, eb=eb, enb=enb, e2=e2, ebl=ebl, qd=q * eb, kd=k * enb, k2=k * e2)


def _chunk_rows(c):
    return slice(c * HG_CHUNK, (c + 1) * HG_CHUNK)


def _head_lanes(h):
    return slice(h * HG_HEAD_DIM, (h + 1) * HG_HEAD_DIM)


def _hgrn_fwd(proj_h, lb, hg_w, ride_srcs, ride_modes):
    B, T, _ = proj_h.shape
    ng = T // HG_ROWS
    nr = len(ride_srcs)

    def body(*refs):
        hq_ref, hf_ref, hi_ref, hg_ref, lb_ref, gw_ref = refs[:6]
        ride_in = refs[6:6 + nr]
        o_ref, rg_ref, sp_ref = refs[6 + nr:9 + nr]
        ride_out = refs[9 + nr:9 + 2 * nr]
        st = refs[9 + 2 * nr]
        sems = refs[10 + 2 * nr:]
        gi = pl.program_id(1)
        step = pl.program_id(0) * ng + gi
        _ride_start(ride_modes, step, B * ng, ride_in, ride_out, sems)

        @pl.when(gi == 0)
        def _():
            st[...] = jnp.zeros(st.shape, F32)

        lo = _group_mask()
        for h in range(HG_HEADS):
            lanes = _head_lanes(h)
            gt = _hgrn_gates(hq_ref[:, lanes], hf_ref[:, lanes], lb_ref[:, lanes])
            v, qd, kd = _bf(hi_ref[:, lanes]), _bf(gt["qd"]), _bf(gt["kd"])
            a = jnp.where(lo, _dot_nt(qd, kd), 0.0)
            kv = _dot_tn(v, _spread(_bf(gt["k2"])))
            s = st[h]
            before = []
            for c in range(HG_GROUP):
                before.append(s)
                s = s * gt["ebl"][c] + _lane_block(kv, c)
            st[h] = s
            sp = jnp.concatenate(before, axis=1)
            sp_ref[h] = sp
            o = _dot(_bf(a), v) + _dot_nt(_spread(qd), _bf(sp))
            o_ref[:, lanes] = o
            hg = hg_ref[:, lanes]
            rn = o * lax.rsqrt(_mean_last(o * o) + EPS) * gw_ref[...]
            rg_ref[:, lanes] = _bf(rn * (hg * _sigmoid(hg)))
        _ride_wait(ride_modes, step, B * ng, ride_in, ride_out, sems)

    part = lambda j: pl.BlockSpec((None, HG_ROWS, HG_WIDTH), lambda b, g: (b, g, j))
    return pl.pallas_call(
        body, name="hgrn_fwd", grid=(B, ng),
        in_specs=[part(0), part(1), part(2), part(3),
                  pl.BlockSpec((1, HG_WIDTH), lambda b, g: (0, 0)),
                  pl.BlockSpec((1, LANES), lambda b, g: (0, 0))] + [ANY_SPEC] * nr,
        out_specs=[part(0), part(0),
                   pl.BlockSpec((None, HG_HEADS, None, HG_HEAD_DIM, HG_STACK), lambda b, g: (b, 0, g, 0, 0))]
        + [ANY_SPEC] * nr,
        out_shape=[SDS((B, T, HG_WIDTH), F32), SDS((B, T, HG_WIDTH), BF16),
                   SDS((B, HG_HEADS, ng, HG_HEAD_DIM, HG_STACK), F32)] + _exchange_shapes(ride_srcs, ride_modes),
        scratch_shapes=[pltpu.VMEM((HG_HEADS, HG_HEAD_DIM, HG_HEAD_DIM), F32)] + _exchange_sems(nr),
        compiler_params=_params(("arbitrary", "arbitrary"), VMEM_LIMIT_BIG),
    )(proj_h, proj_h, proj_h, proj_h, lb, hg_w, *ride_srcs)


def _mix_out(x2, attn_n, rec_g, mod8, post_w, w_out_bf, T, ride_srcs, ride_modes):
    N = x2.shape[0]
    TM = _tile_rows(T, big=True)
    tps = T // TM
    nr = len(ride_srcs)

    def body(*refs):
        x_ref, an_ref, rg_ref, mod_ref, pw_ref, w_ref = refs[:6]
        ride_in = refs[6:6 + nr]
        mix_ref, x1_ref, cat_ref = refs[6 + nr:9 + nr]
        ride_out = refs[9 + nr:9 + 2 * nr]
        sems = refs[9 + 2 * nr:]
        _ride_start(ride_modes, pl.program_id(0), N // TM, ride_in, ride_out, sems)
        cat = jnp.concatenate([an_ref[...], rg_ref[...]], axis=1)
        cat_ref[...] = cat
        mix = _dot(cat, w_ref[...])
        mix_ref[...] = mix
        r = lax.rsqrt(_mean_last(mix * mix) + EPS)
        x1_ref[...] = x_ref[...] + mod_ref[2:3, :] * (mix * r * pw_ref[...])
        _ride_wait(ride_modes, pl.program_id(0), N // TM, ride_in, ride_out, sems)

    row = lambda w: pl.BlockSpec((TM, w), lambda i: (i, 0))
    return pl.pallas_call(
        body, name="mix_out", grid=(N // TM,),
        in_specs=[row(D_MODEL), row(ATT_WIDTH), row(HG_WIDTH), _mod_spec(tps),
                  pl.BlockSpec((1, D_MODEL), lambda i: (0, 0)),
                  pl.BlockSpec((D_MODEL, D_MODEL), lambda i: (0, 0))] + [ANY_SPEC] * nr,
        out_specs=[row(D_MODEL), row(D_MODEL), row(D_MODEL)] + [ANY_SPEC] * nr,
        out_shape=[SDS((N, D_MODEL), F32), SDS((N, D_MODEL), F32), SDS((N, D_MODEL), BF16)]
        + _exchange_shapes(ride_srcs, ride_modes),
        scratch_shapes=_exchange_sems(nr),
        compiler_params=_params(("arbitrary",), VMEM_LIMIT_BIG),
    )(x2, attn_n, rec_g, mod8, post_w, w_out_bf, *ride_srcs)


def _load_weights_once(pairs, sem):
    @pl.when(pl.program_id(0) == 0)
    def _():
        cps = [pltpu.make_async_copy(src, dst, sem.at[i]) for i, (src, dst) in enumerate(pairs)]
        for cp in cps:
            cp.start()
        for cp in cps:
            cp.wait()


MLP_HALF = D_MODEL // 2
MLP_PIECES = 2 * N_DEV + 2


def _mlp_weight_pieces(wu_a, wu_b, wd_a, wd_b, wu, wd):
    cols = D_FF // N_DEV
    pairs = []
    for h, half in enumerate((wu_a, wu_b)):
        for j in range(N_DEV):
            pairs.append((half.at[j], wu.at[pl.ds(h * MLP_HALF, MLP_HALF), pl.ds(j * cols, cols)]))
    for h, half in enumerate((wd_a, wd_b)):
        pairs.append((half, wd.at[:, pl.ds(h * MLP_HALF, MLP_HALF)]))
    return pairs


def _mlp_fwd(x1, mod8, pre_w, w_up_halves, w_down_halves, T):
    N = x1.shape[0]
    TM = _tile_rows(T)
    tps = T // TM

    def body(x_ref, mod_ref, pw_ref, wua, wub, wda, wdb, up_ref, u_ref, d_ref, h2_ref, wu, wd, sem):
        _load_weights_once(_mlp_weight_pieces(wua, wub, wda, wdb, wu, wd), sem)
        x = x_ref[...]
        r = lax.rsqrt(_mean_last(x * x) + EPS)
        h = (x * r * pw_ref[...]) * (1.0 + mod_ref[4:5, :]) + mod_ref[3:4, :]
        hb = _bf(h)
        h2_ref[...] = hb
        up = _dot(hb, wu[...])
        up_ref[...] = up
        ru = jnp.maximum(up, 0.0)
        u = _bf(ru * ru)
        u_ref[...] = u
        d_ref[...] = _dot(u, wd[...])

    row = lambda w: pl.BlockSpec((TM, w), lambda i: (i, 0))
    return pl.pallas_call(
        body, name="mlp_fwd", grid=(N // TM,),
        in_specs=[row(D_MODEL), _mod_spec(tps), pl.BlockSpec((1, D_MODEL), lambda i: (0, 0))] + [ANY_SPEC] * 4,
        out_specs=[row(D_FF), row(D_FF), row(D_MODEL), row(D_MODEL)],
        out_shape=[SDS((N, D_FF), F32), SDS((N, D_FF), BF16), SDS((N, D_MODEL), F32), SDS((N, D_MODEL), BF16)],
        scratch_shapes=[pltpu.VMEM((D_MODEL, D_FF), BF16), pltpu.VMEM((D_FF, D_MODEL), BF16),
                        pltpu.SemaphoreType.DMA((MLP_PIECES,))],
        compiler_params=_params(("arbitrary",), VMEM_LIMIT_BIG),
    )(x1, mod8, pre_w, *w_up_halves, *w_down_halves)


def _acc_rows(acc_ref, first, rows):
    @pl.when(first)
    def _():
        acc_ref[...] = jnp.zeros(acc_ref.shape, F32)
    for i, r in enumerate(rows):
        acc_ref[i:i + 1, :] += r


def _mlp_bwd(x1, d, up, tgt, mod8, pre_w, post_w, w_up_halves, w_down_halves, T):
    N = x1.shape[0]
    TM = _tile_rows(T)
    tps = T // TM

    def body(x_ref, d_ref, up_ref, t_ref, mod_ref, pw_ref, qw_ref, wua, wub, wda, wdb,
             dx_ref, dup_ref, dd_ref, acc_ref, wd, wu, sem):
        _load_weights_once(_mlp_weight_pieces(wua, wub, wda, wdb, wu, wd), sem)
        sh2, sc2, g2 = mod_ref[3:4, :], mod_ref[4:5, :], mod_ref[5:6, :]
        x = x_ref[...]
        r1 = lax.rsqrt(_mean_last(x * x) + EPS)
        xh = x * r1
        n2 = xh * pw_ref[...]
        dv = d_ref[...]
        rd = lax.rsqrt(_mean_last(dv * dv) + EPS)
        dh = dv * rd
        rr = dh * qw_ref[...]
        e = x + g2 * rr - t_ref[...]
        loss = 0.5 * jnp.sum(_sum_rows(e * e), axis=1, keepdims=True) / D_MODEL
        dy = e * (1.0 / D_MODEL)
        dg2 = _sum_rows(dy * rr)
        drr = dy * g2
        dw_post = _sum_rows(drr * dh)
        ddh = drr * qw_ref[...]
        dd = _bf(rd * (ddh - dh * _mean_last(ddh * dh)))
        dd_ref[...] = dd
        ru = jnp.maximum(up_ref[...], 0.0)
        dup = _bf(_dot_nt(dd, wd[...]) * (2.0 * ru))
        dup_ref[...] = dup
        dh2 = _dot_nt(dup, wu[...])
        dsh2 = _sum_rows(dh2)
        dsc2 = _sum_rows(dh2 * n2)
        dn2 = dh2 * (1.0 + sc2)
        dw_pre = _sum_rows(dn2 * xh)
        dxh = dn2 * pw_ref[...]
        dx_ref[...] = dy + r1 * (dxh - xh * _mean_last(dxh * xh))
        _acc_rows(acc_ref, pl.program_id(0) % tps == 0,
                  [dsh2, dsc2, dg2, dw_pre, dw_post, jnp.broadcast_to(loss, (1, D_MODEL))])

    row = lambda w: pl.BlockSpec((TM, w), lambda i: (i, 0))
    vec = pl.BlockSpec((1, D_MODEL), lambda i: (0, 0))
    B = N // T
    return pl.pallas_call(
        body, name="mlp_bwd", grid=(N // TM,),
        in_specs=[row(D_MODEL), row(D_MODEL), row(D_FF), row(D_MODEL), _mod_spec(tps), vec, vec] + [ANY_SPEC] * 4,
        out_specs=[row(D_MODEL), row(D_FF), row(D_MODEL), _mod_spec(tps)],
        out_shape=[SDS((N, D_MODEL), F32), SDS((N, D_FF), BF16), SDS((N, D_MODEL), BF16),
                   SDS((B, 8, D_MODEL), F32)],
        scratch_shapes=[pltpu.VMEM((D_FF, D_MODEL), BF16), pltpu.VMEM((D_MODEL, D_FF), BF16),
                        pltpu.SemaphoreType.DMA((MLP_PIECES,))],
        compiler_params=_params(("arbitrary",), VMEM_LIMIT_BIG),
    )(x1, d, up, tgt, mod8, pre_w, post_w, *w_up_halves, *w_down_halves)


def _mix_bwd(mix, dx1, mod8, post_w, w_out_bf, T, ride_srcs, ride_modes):
    N = mix.shape[0]
    TM = _tile_rows(T, big=True)
    tps = T // TM
    nr = len(ride_srcs)

    def body(*refs):
        mix_ref, dx_ref, mod_ref, pw_ref, w_ref = refs[:5]
        ride_in = refs[5:5 + nr]
        dan_ref, drg_ref, dmix_ref, acc_ref = refs[5 + nr:9 + nr]
        ride_out = refs[9 + nr:9 + 2 * nr]
        sems = refs[9 + 2 * nr:]
        _ride_start(ride_modes, pl.program_id(0), N // TM, ride_in, ride_out, sems)
        g1 = mod_ref[2:3, :]
        mix = mix_ref[...]
        dx1 = dx_ref[...]
        rm = lax.rsqrt(_mean_last(mix * mix) + EPS)
        mh = mix * rm
        dg1 = _sum_rows(dx1 * (mh * pw_ref[...]))
        dr = dx1 * g1
        dw_post = _sum_rows(dr * mh)
        dmh = dr * pw_ref[...]
        dmix = _bf(rm * (dmh - mh * _mean_last(dmh * mh)))
        dmix_ref[...] = dmix
        dcat = _dot_nt(dmix, w_ref[...])
        dan_ref[...] = dcat[:, :ATT_WIDTH]
        drg_ref[...] = dcat[:, ATT_WIDTH:]
        _acc_rows(acc_ref, pl.program_id(0) % tps == 0, [dg1, dw_post])
        _ride_wait(ride_modes, pl.program_id(0), N // TM, ride_in, ride_out, sems)

    row = lambda w: pl.BlockSpec((TM, w), lambda i: (i, 0))
    B = N // T
    return pl.pallas_call(
        body, name="mix_bwd", grid=(N // TM,),
        in_specs=[row(D_MODEL), row(D_MODEL), _mod_spec(tps), pl.BlockSpec((1, D_MODEL), lambda i: (0, 0)),
                  pl.BlockSpec((D_MODEL, D_MODEL), lambda i: (0, 0))] + [ANY_SPEC] * nr,
        out_specs=[row(ATT_WIDTH), row(HG_WIDTH), row(D_MODEL), _mod_spec(tps)] + [ANY_SPEC] * nr,
        out_shape=[SDS((N, ATT_WIDTH), F32), SDS((N, HG_WIDTH), F32), SDS((N, D_MODEL), BF16),
                   SDS((B, 8, D_MODEL), F32)] + _exchange_shapes(ride_srcs, ride_modes),
        scratch_shapes=_exchange_sems(nr),
        compiler_params=_params(("arbitrary",), VMEM_LIMIT_BIG),
    )(mix, dx1, mod8, post_w, w_out_bf, *ride_srcs)


def _hgrn_bwd(proj_h, lb, hg_w, o, s_prev, drg, ride_srcs, ride_modes):
    B, T, _ = proj_h.shape
    ng = T // HG_ROWS
    nr = len(ride_srcs)

    def body(*refs):
        hq_ref, hf_ref, hi_ref, hg_ref, lb_ref, gw_ref, o_ref, sp_ref, drg_ref = refs[:9]
        ride_in = refs[9:9 + nr]
        dhq_ref, dhf_ref, dhi_ref, dhg_ref, dlb_ref, dgw_ref = refs[9 + nr:15 + nr]
        ride_out = refs[15 + nr:15 + 2 * nr]
        dst = refs[15 + 2 * nr]
        sems = refs[16 + 2 * nr:]
        step = pl.program_id(0) * ng + pl.program_id(1)
        _ride_start(ride_modes, step, B * ng, ride_in, ride_out, sems)

        @pl.when(pl.program_id(1) == 0)
        def _():
            dst[...] = jnp.zeros(dst.shape, F32)
            dlb_ref[...] = jnp.zeros(dlb_ref.shape, F32)
            dgw_ref[...] = jnp.zeros(dgw_ref.shape, F32)

        lo = _group_mask()
        gw = gw_ref[...]

        for h in range(HG_HEADS):
            lanes = _head_lanes(h)
            lbv = lb_ref[:, lanes]
            hq = hq_ref[:, lanes]
            gt = _hgrn_gates(hq, hf_ref[:, lanes], lbv)
            sq, sg, qdf, kdf, k2f, ebl = gt["sq"], gt["sg"], gt["qd"], gt["kd"], gt["k2"], gt["ebl"]
            v, qd, kd = _bf(hi_ref[:, lanes]), _bf(qdf), _bf(kdf)
            ov = o_ref[:, lanes]
            hg = hg_ref[:, lanes]
            shg = _sigmoid(hg)
            dr = drg_ref[:, lanes]
            ro = lax.rsqrt(_mean_last(ov * ov) + EPS)
            oh = ov * ro
            dhg_ref[:, lanes] = _bf(dr * (oh * gw) * (shg + hg * shg * (1.0 - shg)))
            drn = dr * (hg * shg)
            dgw_ref[...] += jnp.broadcast_to(_sum_rows(drn * oh), (8, LANES))
            doh = drn * gw
            do = _bf(ro * (doh - oh * _mean_last(doh * oh)))
            a = jnp.where(lo, _dot_nt(qd, kd), 0.0)
            da = _bf(jnp.where(lo, _dot_nt(do, v), 0.0))
            dv = _dot_tn(_bf(a), do)
            dqd = _dot(da, kd)
            dkd = _dot_tn(da, qd)
            sp = sp_ref[h]
            incr = _dot_tn(do, _spread(qd))
            ds = dst[h]
            after = [None] * HG_GROUP
            for c in reversed(range(HG_GROUP)):
                after[c] = ds
                ds = ds * ebl[c] + _lane_block(incr, c)
            dst[h] = ds
            dss = jnp.concatenate(after, axis=1)
            dssb = _bf(dss)
            dk2 = _pick(_dot(v, dssb))
            dhi_ref[:, lanes] = _bf(dv + _dot_nt(_spread(_bf(k2f)), dssb))
            dqd = dqd + _pick(_dot(do, _bf(sp)))
            debl = _sum_rows(dss * sp)
            k2g = dk2 * k2f
            db = dqd * qdf - dkd * kdf - k2g
            dk = dkd * gt["enb"] + dk2 * gt["e2"]
            dbl = _chunk_bcast([_lane_block(debl, c) * ebl[c] + _sum_rows(k2g[_chunk_rows(c), :])
                                for c in range(HG_GROUP)])
            dg = _chunk_cumsum(db, reverse=True) + dbl
            df = dg / gt["f"] - dk
            dhf_ref[:, lanes] = _bf(df * (1.0 - lbv) * sg * (1.0 - sg))
            dlb_ref[:, lanes] += jnp.broadcast_to(_sum_rows(df * (1.0 - sg)), (8, LANES))
            dhq_ref[:, lanes] = _bf((dqd * gt["eb"]) * (sq + hq * sq * (1.0 - sq)))
        _ride_wait(ride_modes, step, B * ng, ride_in, ride_out, sems)

    part = lambda j: pl.BlockSpec((None, HG_ROWS, HG_WIDTH), lambda b, g: (b, ng - 1 - g, j))
    return pl.pallas_call(
        body, name="hgrn_bwd", grid=(B, ng),
        in_specs=[part(0), part(1), part(2), part(3),
                  pl.BlockSpec((1, HG_WIDTH), lambda b, g: (0, 0)),
                  pl.BlockSpec((1, LANES), lambda b, g: (0, 0)),
                  part(0),
                  pl.BlockSpec((None, HG_HEADS, None, HG_HEAD_DIM, HG_STACK), lambda b, g: (b, 0, ng - 1 - g, 0, 0)),
                  part(0)] + [ANY_SPEC] * nr,
        out_specs=[part(0), part(0), part(0), part(0),
                   pl.BlockSpec((None, 8, HG_WIDTH), lambda b, g: (b, 0, 0)),
                   pl.BlockSpec((None, 8, LANES), lambda b, g: (b, 0, 0))] + [ANY_SPEC] * nr,
        out_shape=[SDS((B, T, HG_WIDTH), BF16)] * 4 + [SDS((B, 8, HG_WIDTH), F32), SDS((B, 8, LANES), F32)]
        + _exchange_shapes(ride_srcs, ride_modes),
        scratch_shapes=[pltpu.VMEM((HG_HEADS, HG_HEAD_DIM, HG_HEAD_DIM), F32)] + _exchange_sems(nr),
        compiler_params=_params(("arbitrary", "arbitrary"), VMEM_LIMIT_BIG),
    )(proj_h, proj_h, proj_h, proj_h, lb, hg_w, o, s_prev, drg, *ride_srcs)


def _attn_bwd(qr, kr, proj3, attn_o, dan, tables, sinks, attn_w, ride_srcs, ride_modes):
    B, T, _ = proj3.shape
    nb = T // WINDOW
    splits = min(ATT_SPLITS, nb)
    per = nb // splits
    nr = len(ride_srcs)
    cos, sinl, sinr = tables
    QKV = ATT_WIDTH + 2 * LANES

    def body(*refs):
        qr_ref, kr_ref, v_ref, o_ref, dan_ref, cos_ref, sl_ref, sr_ref, sink_ref, aw_ref = refs[:10]
        ride_in = refs[10:10 + nr]
        dqkv_ref, dsink_ref, daw_ref = refs[10 + nr:13 + nr]
        ride_out = refs[13 + nr:13 + 2 * nr]
        kpad, vpad, dkpad, dvpad, dqb, dsk = refs[13 + 2 * nr:19 + 2 * nr]
        sems = refs[19 + 2 * nr:]
        part = pl.program_id(1)
        step = pl.program_id(0) * splits + part
        _ride_start(ride_modes, step, B * splits, ride_in, ride_out, sems)

        @pl.when(part == 0)
        def _():
            kpad[0:WINDOW, :] = jnp.zeros((WINDOW, LANES), BF16)
            vpad[0:WINDOW, :] = jnp.zeros((WINDOW, LANES), BF16)
            kpad[WINDOW:, :] = kr_ref[...]
            vpad[WINDOW:, :] = _bf(v_ref[...])
            dkpad[...] = jnp.zeros(dkpad.shape, F32)
            dvpad[...] = jnp.zeros(dvpad.shape, F32)
            dsk[...] = jnp.zeros(dsk.shape, F32)
            daw_ref[...] = jnp.zeros(daw_ref.shape, F32)

        lower = _lower_mask()
        aw = aw_ref[...]

        def block(n, daw):
            r0 = pl.multiple_of(n * WINDOW, WINDOW)
            rows = pl.ds(r0, WINDOW)
            nxt = pl.ds(r0 + WINDOW, WINDOW)
            ob = o_ref[rows, :]
            dn = dan_ref[rows, :]
            ro = lax.rsqrt(_mean_last(ob * ob) + EPS)
            oh = ob * ro
            daw = daw + _sum_rows(dn * oh)
            doh = dn * aw
            do = _bf(ro * (doh - oh * _mean_last(doh * oh)))
            doparts = [do[:, j * LANES:(j + 1) * LANES] for j in range(ATT_WIDTH // LANES)]
            qparts = [qr_ref[rows, j * LANES:(j + 1) * LANES] for j in range(ATT_WIDTH // LANES)]
            for hk in range(ATT_KV_HEADS):
                lanes = slice(hk * ATT_HEAD_DIM, (hk + 1) * ATT_HEAD_DIM)
                qs = _stack_heads(qparts, hk)
                dos = _stack_heads(doparts, hk)
                k_cur, k_prev = kpad[nxt, lanes], kpad[rows, lanes]
                v_cur, v_prev = vpad[nxt, lanes], vpad[rows, lanes]
                p, inv, es = _softmax_window(qs, k_cur, k_prev, lower, n > 0, _sink_row(sink_ref, hk))
                p = p * inv
                dp = jnp.where(lower, _dot_nt(v_cur, dos), _dot_nt(v_prev, dos))
                delta = jnp.sum(p * dp, axis=0, keepdims=True)
                ds = p * (dp - delta)
                sk = (es * inv) * delta
                ds_cur = jnp.where(lower, ds, 0.0)
                p_cur = jnp.where(lower, p, 0.0)
                ds_cur, ds_prev = _bf(ds_cur), _bf(ds - ds_cur)
                p_cur, p_prev = _bf(p_cur), _bf(p - p_cur)
                dqt = (_dot_tn(k_cur, ds_cur) + _dot_tn(k_prev, ds_prev)) * ATT_SCALE
                dkpad[nxt, lanes] += _dot(ds_cur, qs)
                dkpad[rows, lanes] += _dot(ds_prev, qs)
                dvpad[nxt, lanes] += _dot(p_cur, dos)
                dvpad[rows, lanes] += _dot(p_prev, dos)
                for g in range(ATT_GROUP):
                    h = ATT_GROUP * hk + g
                    cols = slice(g * WINDOW, (g + 1) * WINDOW)
                    dqb[:, h * ATT_HEAD_DIM:(h + 1) * ATT_HEAD_DIM] = dqt[:, cols].T
                    head_lane = lax.broadcasted_iota(jnp.int32, dsk.shape, 1) == h
                    dsk[...] += jnp.where(head_lane, -jnp.sum(sk[:, cols], axis=1, keepdims=True), 0.0)
            cs, sl, sr = cos_ref[rows, :], sl_ref[rows, :], sr_ref[rows, :]
            for j in range(ATT_WIDTH // LANES):
                dqkv_ref[rows, j * LANES:(j + 1) * LANES] = _bf(_rope_t(dqb[:, j * LANES:(j + 1) * LANES], cs, sl, sr))
            return daw

        daw = _loop_pairs(part * per, per, block, jnp.zeros((1, ATT_WIDTH), F32))
        daw_ref[...] += jnp.broadcast_to(daw, (8, ATT_WIDTH))
        dsink_ref[...] = dsk[...]

        def finish(n, carry):
            r0 = pl.multiple_of(n * WINDOW, WINDOW)
            rows = pl.ds(r0, WINDOW)
            nxt = pl.ds(r0 + WINDOW, WINDOW)
            cs, sl, sr = cos_ref[rows, :], sl_ref[rows, :], sr_ref[rows, :]
            dqkv_ref[rows, ATT_WIDTH:ATT_WIDTH + LANES] = _bf(_rope_t(dkpad[nxt, :], cs, sl, sr))
            dqkv_ref[rows, ATT_WIDTH + LANES:QKV] = _bf(dvpad[nxt, :])
            return carry

        @pl.when(part == splits - 1)
        def _():
            lax.fori_loop(0, nb, finish, 0)

        _ride_wait(ride_modes, step, B * splits, ride_in, ride_out, sems)

    seq = lambda w, j: pl.BlockSpec((None, T, w), lambda b, s: (b, 0, j))
    full = lambda r, w: pl.BlockSpec((r, w), lambda b, s: (0, 0))
    return pl.pallas_call(
        body, name="attn_bwd", grid=(B, splits),
        in_specs=[seq(ATT_WIDTH, 0), seq(LANES, 0), seq(LANES, 5), seq(ATT_WIDTH, 0), seq(ATT_WIDTH, 0),
                  full(T, LANES), full(T, LANES), full(T, LANES),
                  pl.BlockSpec(memory_space=pltpu.SMEM), full(1, ATT_WIDTH)] + [ANY_SPEC] * nr,
        out_specs=[seq(QKV, 0), pl.BlockSpec((None, 8, LANES), lambda b, s: (b, 0, 0)),
                   pl.BlockSpec((None, 8, ATT_WIDTH), lambda b, s: (b, 0, 0))] + [ANY_SPEC] * nr,
        out_shape=[SDS((B, T, QKV), BF16), SDS((B, 8, LANES), F32), SDS((B, 8, ATT_WIDTH), F32)]
        + _exchange_shapes(ride_srcs, ride_modes),
        scratch_shapes=[pltpu.VMEM((T + WINDOW, LANES), BF16), pltpu.VMEM((T + WINDOW, LANES), BF16),
                        pltpu.VMEM((T + WINDOW, LANES), F32), pltpu.VMEM((T + WINDOW, LANES), F32),
                        pltpu.VMEM((WINDOW, ATT_WIDTH), F32), pltpu.VMEM((8, LANES), F32)] + _exchange_sems(nr),
        compiler_params=_params(("arbitrary", "arbitrary"), VMEM_LIMIT_BIG),
    )(qr, kr, proj3, attn_o, dan, cos, sinl, sinr, sinks, attn_w, *ride_srcs)


def _in_bwd(x2, dx1, dqkv, dhq, dhf, dhi, dhg, mod8, pre_w, w_in_bf, T, ride_srcs, ride_modes):
    N = x2.shape[0]
    TM = _tile_rows(T, big=True)
    tps = T // TM
    nr = len(ride_srcs)
    pieces = [(0, ATT_WIDTH + 2 * LANES), (768, HG_WIDTH), (1280, HG_WIDTH), (1792, HG_WIDTH), (2304, HG_WIDTH)]

    def body(*refs):
        x_ref, dx_ref, p0, p1, p2, p3, p4, mod_ref, pw_ref, w_ref = refs[:10]
        ride_in = refs[10:10 + nr]
        gx_ref, dproj_ref, acc_ref = refs[10 + nr:13 + nr]
        ride_out = refs[13 + nr:13 + 2 * nr]
        sems = refs[13 + 2 * nr:]
        _ride_start(ride_modes, pl.program_id(0), N // TM, ride_in, ride_out, sems)
        sc1 = mod_ref[1:2, :]
        dh = jnp.zeros((TM, D_MODEL), F32)
        for ref, (off, width) in zip((p0, p1, p2, p3, p4), pieces):
            pb = ref[...]
            dproj_ref[:, off:off + width] = pb
            dh = dh + _dot(pb, w_ref[off:off + width, :])
        x = x_ref[...]
        r = lax.rsqrt(_mean_last(x * x) + EPS)
        xh = x * r
        n1 = xh * pw_ref[...]
        dsh1 = _sum_rows(dh)
        dsc1 = _sum_rows(dh * n1)
        dn1 = dh * (1.0 + sc1)
        dw_pre = _sum_rows(dn1 * xh)
        dxh = dn1 * pw_ref[...]
        gx_ref[...] = dx_ref[...] + r * (dxh - xh * _mean_last(dxh * xh))
        _acc_rows(acc_ref, pl.program_id(0) % tps == 0, [dsh1, dsc1, dw_pre])
        _ride_wait(ride_modes, pl.program_id(0), N // TM, ride_in, ride_out, sems)

    row = lambda w: pl.BlockSpec((TM, w), lambda i: (i, 0))
    B = N // T
    return pl.pallas_call(
        body, name="in_bwd", grid=(N // TM,),
        in_specs=[row(D_MODEL), row(D_MODEL), row(768), row(HG_WIDTH), row(HG_WIDTH), row(HG_WIDTH),
                  row(HG_WIDTH), _mod_spec(tps), pl.BlockSpec((1, D_MODEL), lambda i: (0, 0)),
                  pl.BlockSpec((IN_COLS, D_MODEL), lambda i: (0, 0))] + [ANY_SPEC] * nr,
        out_specs=[row(D_MODEL), row(IN_COLS), _mod_spec(tps)] + [ANY_SPEC] * nr,
        out_shape=[SDS((N, D_MODEL), F32), SDS((N, IN_COLS), BF16), SDS((B, 8, D_MODEL), F32)]
        + _exchange_shapes(ride_srcs, ride_modes),
        scratch_shapes=_exchange_sems(nr),
        compiler_params=_params(("arbitrary",), VMEM_LIMIT_BIG),
    )(x2, dx1, dqkv, dhq, dhf, dhi, dhg, mod8, pre_w, w_in_bf, *ride_srcs)


def _matmul_tn(name, a, b, tn, tm=512, by_owner_cols=False):
    K, M = a.shape
    Nc = b.shape[1]
    tm = min(tm, M)

    def body(a_ref, b_ref, o_ref):
        o_ref[...] = _bf(_dot_tn(a_ref[...], b_ref[...]))

    if by_owner_cols:
        assert tn * N_DEV == Nc
        out_shape = SDS((N_DEV, M, tn), BF16)
        out_spec = pl.BlockSpec((None, tm, tn), lambda i, j: (j, i, 0))
    else:
        out_shape = SDS((M, Nc), BF16)
        out_spec = pl.BlockSpec((tm, tn), lambda i, j: (i, j))
    return pl.pallas_call(
        body, name=name, grid=(M // tm, Nc // tn),
        in_specs=[pl.BlockSpec((K, tm), lambda i, j: (0, i)),
                  pl.BlockSpec((K, tn), lambda i, j: (0, j))],
        out_specs=out_spec, out_shape=out_shape,
        compiler_params=_params(("arbitrary", "arbitrary"), VMEM_LIMIT_BIG),
    )(a, b)


GW_BLOCK = IN_COLS // N_DEV
GW_HALF = IN_COLS // 2


def _grad_w_in_reduced(dproj, h1, ride_srcs, ride_modes):
    K = dproj.shape[0]
    nr = len(ride_srcs)
    chips = N_DEV // 2
    tn = 512

    def body(*refs):
        a_hbm, b_hbm = refs[:2]
        ride_in, out, ride_out = refs[2:2 + nr], refs[2 + nr], refs[3 + nr:3 + 2 * nr]
        a_buf, b_buf, g_buf, theirs, p_buf, in_sems, pair_send, pair_recv, chip_send, chip_recv, own_sem = \
            refs[3 + 2 * nr:14 + 2 * nr]
        ride = _exchange_phases(ride_modes, ride_in, ride_out, *refs[14 + 2 * nr:]) if nr else ([], [], [])
        x, y, core = lax.axis_index("x"), lax.axis_index("y"), lax.axis_index("c")
        chip = 2 * x + y
        sib_dev, _ = _related(SIBLING)

        def remote(src, dst, send_sem, recv_sem, dev):
            return pltpu.make_async_remote_copy(src_ref=src, dst_ref=dst, send_sem=send_sem, recv_sem=recv_sem,
                                                device_id=dev, device_id_type=MESH)

        halves = [1 - x, x]
        loads = [pltpu.make_async_copy(b_hbm, b_buf, in_sems.at[0])]
        for t in range(2):
            col = pl.multiple_of(halves[t] * GW_HALF, LANES)
            loads.append(pltpu.make_async_copy(a_hbm.at[:, pl.ds(col, GW_HALF)], a_buf.at[t], in_sems.at[1 + t]))
        for cp in loads:
            cp.start()
        _run(ride[0])
        loads[0].wait()
        end = []
        for t in range(2):
            loads[1 + t].wait()
            if t == 1:
                _run(ride[1])
            for j in range(D_MODEL // tn):
                res = _dot_tn(a_buf[t], b_buf[:, j * tn:(j + 1) * tn])
                for q in range(2):
                    for cc in range(2):
                        r0 = (2 * q + cc) * GW_BLOCK
                        g_buf[t, q, cc, :, j * tn:(j + 1) * tn] = _bf(res[r0:r0 + GW_BLOCK])
            swaps = [remote(g_buf.at[t, q, 1 - core], theirs.at[t, q], pair_send.at[t, q], pair_recv.at[t, q], sib_dev)
                     for q in range(2)]
            for cp in swaps:
                cp.start()
            for cp in swaps:
                cp.wait_recv()
            end += [cp.wait_send for cp in swaps]
            for q in range(2):
                p_buf[t, q] = _bf(g_buf[t, q, core].astype(F32) + theirs[t, q].astype(F32))
            for dy in range(2):
                k = 4 * (1 - t) + 2 * dy
                if k == 0:
                    own = pltpu.make_async_copy(p_buf.at[t, y], out.at[chip], own_sem.at[0])
                    own.start()
                    end.append(own.wait)
                    continue
                dev, peer = _related(k)
                send = remote(p_buf.at[t, y ^ dy], out.at[chip], chip_send.at[k // 2], chip_recv.at[k // 2], dev)
                send.start()
                end += [remote(p_buf.at[t, y ^ dy], out.at[peer // 2], chip_send.at[k // 2], chip_recv.at[k // 2],
                               dev).wait_recv, send.wait_send]
        _run(ride[2])
        _run(end)

    return pl.pallas_call(
        body, name="grad_w_in",
        in_specs=[ANY_SPEC] * (2 + nr), out_specs=[ANY_SPEC] * (1 + nr),
        out_shape=[SDS((chips, GW_BLOCK, D_MODEL), BF16)] + _exchange_shapes(ride_srcs, ride_modes),
        scratch_shapes=[pltpu.VMEM((2, K, GW_HALF), BF16), pltpu.VMEM((K, D_MODEL), BF16),
                        pltpu.VMEM((2, 2, 2, GW_BLOCK, D_MODEL), BF16), pltpu.VMEM((2, 2, GW_BLOCK, D_MODEL), BF16),
                        pltpu.VMEM((2, 2, GW_BLOCK, D_MODEL), BF16), pltpu.SemaphoreType.DMA((3,)),
                        pltpu.SemaphoreType.DMA((2, 2)), pltpu.SemaphoreType.DMA((2, 2)),
                        pltpu.SemaphoreType.DMA((chips,)), pltpu.SemaphoreType.DMA((chips,)),
                        pltpu.SemaphoreType.DMA((1,))] + _exchange_sems(nr),
        compiler_params=_params(None, VMEM_LIMIT_BIG),
    )(dproj, h1, *ride_srcs)


def _adamw_math(w, g, m, v):
    m2 = ADAM_B1 * m + (1.0 - ADAM_B1) * g
    v2 = ADAM_B2 * v + (1.0 - ADAM_B2) * (g * g)
    m_hat = m2 / (1.0 - ADAM_B1 ** ADAM_STEP)
    v_hat = v2 / (1.0 - ADAM_B2 ** ADAM_STEP)
    delta = -ADAM_LR * (m_hat / (jnp.sqrt(v_hat) + ADAM_EPS) + ADAM_WD * w)
    return delta, m2, v2


def _pair_add(name, gw, theirs):
    chips, _, r, c = gw.shape
    tr = r
    core = lax.axis_index("c").astype(jnp.int32).reshape(1)

    def body(core_ref, mine_ref, theirs_ref, o_ref):
        o_ref[...] = _bf(mine_ref[...].astype(F32) + theirs_ref[...].astype(F32))

    block = pl.BlockSpec((None, tr, c), lambda s, i, core_ref: (s, i, 0))
    grid_spec = pltpu.PrefetchScalarGridSpec(
        num_scalar_prefetch=1, grid=(chips, r // tr),
        in_specs=[pl.BlockSpec((None, None, tr, c), lambda s, i, core_ref: (s, core_ref[0], i, 0)), block],
        out_specs=block)
    return pl.pallas_call(
        body, name=name, grid_spec=grid_spec, out_shape=SDS((chips, r, c), BF16),
        compiler_params=_params(("arbitrary", "arbitrary")),
    )(core, gw, theirs)


def _reduce_adamw(name, parts, w, m, v):
    r, c = w.shape
    tr = r if r % 256 else 256
    slots = parts.shape[0]

    def body(p_ref, w_ref, m_ref, v_ref, g_ref, d_ref, m2_ref, v2_ref):
        g = p_ref[0].astype(F32)
        for s in range(1, slots):
            g = g + p_ref[s].astype(F32)
        g_ref[...] = g
        d_ref[...], m2_ref[...], v2_ref[...] = _adamw_math(w_ref[...], g, m_ref[...], v_ref[...])

    blk = pl.BlockSpec((tr, c), lambda i: (i, 0))
    return pl.pallas_call(
        body, name=name, grid=(r // tr,),
        in_specs=[pl.BlockSpec((slots, tr, c), lambda i: (0, i, 0)), blk, blk, blk],
        out_specs=[blk] * 4, out_shape=[SDS((r, c), F32)] * 4,
        compiler_params=_params(("arbitrary",), VMEM_LIMIT_BIG),
    )(parts, w, m, v)


def _ada_grad_adamw(c_all, dmod_all, w, m, v):
    r, c = w.shape
    tr = 256
    nb = c_all.shape[0]

    def body(c_ref, dm_ref, w_ref, m_ref, v_ref, g_ref, d_ref, m2_ref, v2_ref):
        cv = c_ref[...]
        g = _dot_tn(cv * _sigmoid(cv), dm_ref[...])
        g_ref[...] = g
        d_ref[...], m2_ref[...], v2_ref[...] = _adamw_math(w_ref[...], g, m_ref[...], v_ref[...])

    blk = pl.BlockSpec((tr, c), lambda i: (i, 0))
    return pl.pallas_call(
        body, name="ada_grad_adamw", grid=(r // tr,),
        in_specs=[pl.BlockSpec((nb, tr), lambda i: (0, i)), pl.BlockSpec((nb, c), lambda i: (0, 0)),
                  blk, blk, blk],
        out_specs=[blk] * 4, out_shape=[SDS((r, c), F32)] * 4,
        compiler_params=_params(("arbitrary",)),
    )(c_all, dmod_all, w, m, v)


_SMALL = [("b_ada", 6144), ("pre_w_mix", 1024), ("attn_sinks", 128), ("attn_out_w", 512), ("lb_table", 1024),
          ("hg_norm_w", 128), ("post_w_mix", 1024), ("pre_w_mlp", 1024), ("post_w_mlp", 1024)]


def _pack_small(acc_in, acc_mix, acc_mlp, dsink, daw, dlb, dgw, lb_p, ada_cols):
    B = acc_in.shape[0]
    width = sum(w for _, w in _SMALL) + LANES

    def body(ain, amix, amlp, dsk_ref, daw_ref, dlb_ref, dgw_ref, lbp_ref, packed_ref, dmod_ref):
        def total(ref, r, w=None):
            out = ref[0, r:r + 1, :] if w is None else ref[0, r:r + 1, :w]
            for b in range(1, B):
                out = out + (ref[b, r:r + 1, :] if w is None else ref[b, r:r + 1, :w])
            return out

        d_b_ada = None
        for b in range(B):
            mods = [ain[b, 0:1, :], ain[b, 1:2, :], amix[b, 0:1, :], amlp[b, 0:1, :], amlp[b, 1:2, :], amlp[b, 2:3, :]]
            full = jnp.concatenate(mods, axis=1)
            for j in range(N_DEV):
                dmod_ref[j, b:b + 1, :] = full[:, j * ada_cols:(j + 1) * ada_cols]
            d_b_ada = full if d_b_ada is None else d_b_ada + full
        d_lb = total(dlb_ref, 0)
        pp = lbp_ref[0:1, :] * lbp_ref[1:2, :]
        pieces = [d_b_ada, total(ain, 2), total(dsk_ref, 0), total(daw_ref, 0), -d_lb * pp, d_lb * pp,
                  total(dgw_ref, 0), total(amix, 1), total(amlp, 3), total(amlp, 4), total(amlp, 5, LANES)]
        off = 0
        for piece in pieces:
            packed_ref[:, off:off + piece.shape[1]] = piece
            off += piece.shape[1]

    return pl.pallas_call(
        body, name="pack_small",
        out_shape=[SDS((1, width), F32), SDS((N_DEV, B, ada_cols), F32)],
    )(acc_in, acc_mix, acc_mlp, dsink, daw, dlb, dgw, lb_p)


def _adamw_small(parts, given):
    names = [n for n, _ in _SMALL]
    flat_in = [a for n in names for a in given[n]]

    def body(*refs):
        p_ref = refs[0]
        in_refs = refs[1:1 + 3 * len(names)]
        out_refs = refs[1 + 3 * len(names):-1]
        loss_ref = refs[-1]
        g = p_ref[0]
        for s in range(1, N_DEV):
            g = g + p_ref[s]
        off = 0
        for i, (name, width) in enumerate(_SMALL):
            w_ref, m_ref, v_ref = in_refs[3 * i:3 * i + 3]
            rows, cols = w_ref.shape
            for r in range(rows):
                gr = g[:, off + r * cols:off + (r + 1) * cols]
                res = (gr,) + _adamw_math(w_ref[r:r + 1, :], gr, m_ref[r:r + 1, :], v_ref[r:r + 1, :])
                for o_ref, val in zip(out_refs[4 * i:4 * i + 4], res):
                    o_ref[r:r + 1, :] = val
            off += width
        loss_ref[...] = g[:, off:off + LANES]

    out_shape = [SDS(given[n][0].shape, F32) for n in names for _ in range(4)] + [SDS((1, LANES), F32)]
    outs = pl.pallas_call(body, name="adamw_small", out_shape=out_shape)(parts, *flat_in)
    return {n: tuple(outs[4 * i:4 * i + 4]) for i, n in enumerate(names)}, outs[-1][0, 0]


def kernel(x, c, w_ada, b_ada, pre_w_mix, w_in, attn_sinks, attn_out_w, lb_table, hg_norm_w, w_out, post_w_mix, pre_w_mlp, w_up, w_down, post_w_mlp, loss_target, m_w_ada, m_b_ada, m_pre_w_mix, m_w_in, m_attn_sinks, m_attn_out_w, m_lb_table, m_hg_norm_w, m_w_out, m_post_w_mix, m_pre_w_mlp, m_w_up, m_w_down, m_post_w_mlp, v_w_ada, v_b_ada, v_pre_w_mix, v_w_in, v_attn_sinks, v_attn_out_w, v_lb_table, v_hg_norm_w, v_w_out, v_post_w_mix, v_pre_w_mlp, v_w_up, v_w_down, v_post_w_mlp):
    B, T, _ = x.shape
    N = B * T
    me = 4 * lax.axis_index("x") + 2 * lax.axis_index("y") + lax.axis_index("c")
    x2 = x.reshape(N, D_MODEL)
    tgt2 = loss_target.reshape(N, D_MODEL)

    w_in_t, m_w_in_t, v_w_in_t = w_in[0].T, m_w_in[0].T, v_w_in[0].T
    w_in_g, c_g = _exchange("gather_w_in", [_bf(w_in_t), c], ["gather"] * 2)
    w_in_f = w_in_g.reshape(IN_COLS, D_MODEL)
    c_all = c_g.reshape(N_DEV * B, D_MODEL)

    ada_cols = w_ada.shape[2]
    b_mine = lax.dynamic_slice(b_ada, (0, me * ada_cols), (1, ada_cols))
    mod_cols = _ada_mod(c_all, w_ada[0], b_mine)
    (mod_g,) = _exchange("scatter_mod", [mod_cols.reshape(N_DEV, B, ada_cols)], ["a2a"])
    mod = mod_g.transpose(1, 0, 2).reshape(B, 6, D_MODEL)
    mod8 = jnp.pad(mod, ((0, 0), (0, 2), (0, 0)))

    lb_p = jax.nn.softmax(lb_table, axis=0)
    lb = lb_p[1:2]
    tables = _rope_tables(T)

    w_up_b, w_down_b = _bf(w_up[0]), _bf(w_down[0])
    proj_a, proj_h, h1, w_out_g, w_up_g0 = _in_proj(x2, mod8, pre_w_mix, w_in_f, T,
                                                    [_bf(w_out[0]), w_up_b[:MLP_HALF]], ["gather"] * 2)
    proj3 = proj_a.reshape(B, T, ATT_COLS)
    proj_h = proj_h.reshape(B, T, IN_COLS - ATT_COLS)
    rec_o, rec_g, s_prev, w_up_g1 = _hgrn_fwd(proj_h, lb, hg_norm_w, [w_up_b[MLP_HALF:]], ["gather"])
    attn_o, attn_n, qr, kr, w_down_g0 = _attn_fwd(proj3, tables, attn_sinks, attn_out_w,
                                                  [w_down_b[:, :MLP_HALF]], ["gather"])
    w_out_f = w_out_g.reshape(D_MODEL, D_MODEL)
    mix, x1, cat, w_down_g1 = _mix_out(x2, attn_n.reshape(N, ATT_WIDTH), rec_g.reshape(N, HG_WIDTH), mod8,
                                       post_w_mix, w_out_f, T, [w_down_b[:, MLP_HALF:]], ["gather"])
    w_up_halves = [w_up_g0, w_up_g1]
    w_down_halves = [w_down_g0.reshape(D_FF, MLP_HALF), w_down_g1.reshape(D_FF, MLP_HALF)]
    up, u, d, h2 = _mlp_fwd(x1, mod8, pre_w_mlp, w_up_halves, w_down_halves, T)

    dx1, dup, dd, acc_mlp = _mlp_bwd(x1, d, up, tgt2, mod8, pre_w_mlp, post_w_mlp, w_up_halves, w_down_halves, T)
    chips = N_DEV // 2
    by_chip = lambda a: a.reshape((chips, 2, a.shape[0] // N_DEV) + a.shape[1:])
    gw_up = _matmul_tn("grad_w_up", h2, dup, D_FF // N_DEV, tm=D_MODEL, by_owner_cols=True)
    gw_up = gw_up.reshape(chips, 2, D_MODEL, D_FF // N_DEV)
    gw_down = by_chip(_matmul_tn("grad_w_down", u, dd, D_MODEL))
    dan, drg, dmix, acc_mix, q_down, q_up = _mix_bwd(mix, dx1, mod8, post_w_mix, w_out_f, T,
                                                     [gw_down, gw_up], ["pair"] * 2)
    p_down, p_up = _pair_add("pair_add_w_down", gw_down, q_down), _pair_add("pair_add_w_up", gw_up, q_up)
    gw_out = _matmul_tn("grad_w_out", cat, dmix, 512, tm=D_MODEL).reshape(N_DEV, D_MODEL // N_DEV, D_MODEL)
    dhq, dhf, dhi, dhg, dlb_p, dgw_p, r_down, r_up = _hgrn_bwd(
        proj_h, lb, hg_norm_w, rec_o, s_prev, drg.reshape(B, T, HG_WIDTH), [p_down, p_up], ["chips"] * 2)
    dqkv, dsink_p, daw_p, r_out = _attn_bwd(qr, kr, proj3, attn_o, dan.reshape(B, T, ATT_WIDTH), tables,
                                            attn_sinks, attn_out_w, [gw_out], ["a2a"])
    flat = lambda a: a.reshape(N, a.shape[-1])
    grad_x, dproj, acc_in = _in_bwd(x2, dx1, flat(dqkv), flat(dhq), flat(dhf), flat(dhi), flat(dhg),
                                    mod8, pre_w_mix, w_in_f, T, [], [])

    packed, dmod_blocks = _pack_small(acc_in, acc_mix, acc_mlp, dsink_p, daw_p, dlb_p, dgw_p, lb_p, ada_cols)
    r_in, r_dmod, r_small = _grad_w_in_reduced(dproj, h1, [dmod_blocks, packed], ["a2a", "gather"])

    res = {}
    res["w_in"] = tuple(a.T for a in _reduce_adamw("adamw_w_in", r_in, w_in_t, m_w_in_t, v_w_in_t))
    res["w_out"] = _reduce_adamw("adamw_w_out", r_out, w_out[0], m_w_out[0], v_w_out[0])
    res["w_up"] = _reduce_adamw("adamw_w_up", r_up, w_up[0], m_w_up[0], v_w_up[0])
    res["w_down"] = _reduce_adamw("adamw_w_down", r_down, w_down[0], m_w_down[0], v_w_down[0])
    res["w_ada"] = _ada_grad_adamw(c_all, r_dmod.reshape(N_DEV * B, ada_cols), w_ada[0], m_w_ada[0], v_w_ada[0])

    given = dict(b_ada=(b_ada, m_b_ada, v_b_ada), pre_w_mix=(pre_w_mix, m_pre_w_mix, v_pre_w_mix),
                 attn_sinks=(attn_sinks, m_attn_sinks, v_attn_sinks),
                 attn_out_w=(attn_out_w, m_attn_out_w, v_attn_out_w), lb_table=(lb_table, m_lb_table, v_lb_table),
                 hg_norm_w=(hg_norm_w, m_hg_norm_w, v_hg_norm_w), post_w_mix=(post_w_mix, m_post_w_mix, v_post_w_mix),
                 pre_w_mlp=(pre_w_mlp, m_pre_w_mlp, v_pre_w_mlp), post_w_mlp=(post_w_mlp, m_post_w_mlp, v_post_w_mlp))
    small_res, loss = _adamw_small(r_small, given)
    res.update(small_res)

    order = ["w_ada", "b_ada", "pre_w_mix", "w_in", "attn_sinks", "attn_out_w", "lb_table", "hg_norm_w", "w_out",
             "post_w_mix", "pre_w_mlp", "w_up", "w_down", "post_w_mlp"]
    big = {"w_ada", "w_in", "w_out", "w_up", "w_down"}
    outs = [loss, grad_x.reshape(B, T, D_MODEL)]
    for i in range(4):
        for k in order:
            a = res[k][i]
            outs.append(a[None] if k in big else a)
    return tuple(outs)
```

```python
import jax
import jax.numpy as jnp
import numpy as np
from jax import lax
from jax.experimental import pallas as pl
from jax.experimental.pallas import tpu as pltpu

F32 = jnp.float32
BF16 = jnp.bfloat16
SDS = jax.ShapeDtypeStruct

D_MODEL = 1024
ATT_WIDTH = 512
ATT_HEAD_DIM = 64
ATT_KV_HEADS = 2
ATT_GROUP = 4
WINDOW = 128
ROPE_DIM = 16
ROPE_THETA = 500000.0
HG_WIDTH = 512
HG_HEAD_DIM = 128
HG_HEADS = 4
HG_CHUNK = 32
IN_COLS = 2816
ATT_COLS = 768
D_FF = 4096
EPS = 1e-6
N_DEV = 8

ADAM_LR = 0.001
ADAM_B1 = 0.9
ADAM_B2 = 0.999
ADAM_EPS = 1e-08
ADAM_WD = 0.01
ADAM_STEP = 10

VMEM_LIMIT_BIG = 56 << 20
LANES = 128

MESH = pl.DeviceIdType.MESH
NT_DIMS = (((1,), (1,)), ((), ()))
TN_DIMS = (((0,), (0,)), ((), ()))


def _dot(a, b):
    return jnp.dot(a, b, preferred_element_type=F32)


def _dot_nt(a, b):
    return lax.dot_general(a, b, NT_DIMS, preferred_element_type=F32)


def _dot_tn(a, b):
    return lax.dot_general(a, b, TN_DIMS, preferred_element_type=F32)


def _bf(a):
    return a.astype(BF16)


def _sigmoid(a):
    return 0.5 * jnp.tanh(0.5 * a) + 0.5


def _mean_last(a):
    return jnp.mean(a, axis=-1, keepdims=True)


def _sum_rows(a):
    return jnp.sum(a, axis=0, keepdims=True)


def _loop_pairs(first, count, body, init, per_trip=2):
    if count % per_trip:
        return lax.fori_loop(first, first + count, body, init)

    def trip(i, c):
        for k in range(per_trip):
            c = body(first + per_trip * i + k, c)
        return c

    return lax.fori_loop(0, count // per_trip, trip, init)


def _params(sem=None, vmem=None):
    kw = {}
    if sem is not None:
        kw["dimension_semantics"] = sem
    if vmem is not None:
        kw["vmem_limit_bytes"] = vmem
    return pltpu.CompilerParams(**kw)


ANY_SPEC = pl.BlockSpec(memory_space=pl.ANY)


def _exchange_shapes(srcs, modes):
    out_shape = []
    for s, m in zip(srcs, modes):
        shp = {"gather": (N_DEV,) + tuple(s.shape), "pair": (s.shape[0],) + tuple(s.shape[2:])}.get(m, tuple(s.shape))
        out_shape.append(SDS(shp, s.dtype))
    return out_shape


def _exchange_sems(n):
    if n == 0:
        return []
    return [pltpu.SemaphoreType.DMA((n, N_DEV - 1)), pltpu.SemaphoreType.DMA((n, N_DEV - 1)),
            pltpu.SemaphoreType.DMA((n,))]


SIBLING = 1
OTHER_CHIPS = (2, 4, 6)


def _related(k):
    x, y, c = lax.axis_index("x"), lax.axis_index("y"), lax.axis_index("c")
    px, py, pc = x ^ ((k >> 2) & 1), y ^ ((k >> 1) & 1), c ^ (k & 1)
    return (px, py, pc), 4 * px + 2 * py + pc


def _exchange_phases(modes, src_refs, out_refs, send_sems, recv_sems, own_sems):
    _, me = _related(0)
    sib_dev, sib = _related(SIBLING)
    start, middle, end = [], [], []

    def remote(a, i, src, dst, dev):
        return pltpu.make_async_remote_copy(src_ref=src, dst_ref=dst, send_sem=send_sems.at[a, i],
                                            recv_sem=recv_sems.at[a, i], device_id=dev, device_id_type=MESH)

    for a, mode in enumerate(modes):
        out = out_refs[a]
        if mode == "gather":
            src = src_refs[a]
            own = pltpu.make_async_copy(src, out.at[me], own_sems.at[a])
            to_sib = remote(a, 0, src, out.at[me], sib_dev)
            start += [own.start, to_sib.start]
            end += [remote(a, 0, src, out.at[sib], sib_dev).wait_recv, to_sib.wait_send, own.wait]
            for j, k in enumerate(OTHER_CHIPS, start=1):
                dev, peer = _related(k)
                _, peer_sib = _related(k ^ SIBLING)
                send = remote(a, j, src, out.at[me], dev)
                passed = remote(a, 3 + j, out.at[peer], out.at[peer], sib_dev)
                start.append(send.start)
                middle += [remote(a, j, src, out.at[peer], dev).wait_recv, passed.start]
                end += [remote(a, 3 + j, out.at[peer_sib], out.at[peer_sib], sib_dev).wait_recv,
                        send.wait_send, passed.wait_send]
        elif mode == "pair":
            core = lax.axis_index("c")
            for s in range(N_DEV // 2):
                send = remote(a, s, src_refs[a].at[s, 1 - core], out.at[s], sib_dev)
                start.append(send.start)
                end += [remote(a, s, src_refs[a].at[s, 1 - core], out.at[s], sib_dev).wait_recv, send.wait_send]
        elif mode == "chips":
            chip = me // 2
            own = pltpu.make_async_copy(src_refs[a].at[chip], out.at[chip], own_sems.at[a])
            start.append(own.start)
            end.append(own.wait)
            for j, k in enumerate(OTHER_CHIPS, start=1):
                dev, peer = _related(k)
                send = remote(a, j, src_refs[a].at[peer // 2], out.at[chip], dev)
                start.append(send.start)
                end += [remote(a, j, src_refs[a].at[peer // 2], out.at[peer // 2], dev).wait_recv, send.wait_send]
        else:
            own = pltpu.make_async_copy(src_refs[a].at[me], out.at[me], own_sems.at[a])
            start.append(own.start)
            end.append(own.wait)
            for k in range(1, N_DEV):
                dev, peer = _related(k)
                send = remote(a, k - 1, src_refs[a].at[peer], out.at[me], dev)
                start.append(send.start)
                end += [remote(a, k - 1, src_refs[a].at[peer], out.at[peer], dev).wait_recv, send.wait_send]
    return start, middle, end


def _run(actions):
    for act in actions:
        act()


def _exchange(name, srcs, modes):
    n = len(srcs)

    def body(*refs):
        start, middle, end = _exchange_phases(modes, refs[:n], refs[n:2 * n], *refs[2 * n:])
        _run(start)
        _run(middle)
        _run(end)

    return pl.pallas_call(
        body, name=name, out_shape=_exchange_shapes(srcs, modes),
        in_specs=[ANY_SPEC] * n, out_specs=[ANY_SPEC] * n,
        scratch_shapes=_exchange_sems(n),
    )(*srcs)


def _ride_start(modes, step, steps, src_refs, out_refs, sems):
    if not modes:
        return
    middle_step = steps - 1

    @pl.when(step == 0)
    def _():
        _run(_exchange_phases(modes, src_refs, out_refs, *sems)[0])

    if "gather" in modes:
        @pl.when(step == middle_step)
        def _():
            _run(_exchange_phases(modes, src_refs, out_refs, *sems)[1])


def _ride_wait(modes, step, steps, src_refs, out_refs, sems):
    if not modes:
        return

    @pl.when(step == steps - 1)
    def _():
        _run(_exchange_phases(modes, src_refs, out_refs, *sems)[2])


def _ada_mod(c_all, w_ada, b_ada_mine):
    nb, cols = c_all.shape[0], w_ada.shape[1]

    def body(c_ref, w_ref, b_ref, o_ref):
        cv = c_ref[...]
        ca = cv * _sigmoid(cv)
        o_ref[...] = _dot(ca, w_ref[...]) + b_ref[...]

    return pl.pallas_call(body, name="ada_mod", out_shape=SDS((nb, cols), F32))(c_all, w_ada, b_ada_mine)


def _tile_rows(T, big=False):
    return min(512 if big else 256, T)


def _mod_spec(tps):
    return pl.BlockSpec((None, 8, D_MODEL), lambda i: (i // tps, 0, 0))


def _in_proj(x2, mod8, pre_w, w_in_bf, T, ride_srcs, ride_modes):
    N = x2.shape[0]
    TM = _tile_rows(T, big=True)
    tps = T // TM
    nr = len(ride_srcs)

    def body(*refs):
        x_ref, mod_ref, pw_ref, w_ref = refs[:4]
        ride_in = refs[4:4 + nr]
        pa_ref, ph_ref, h1_ref = refs[4 + nr:7 + nr]
        ride_out = refs[7 + nr:7 + 2 * nr]
        sems = refs[7 + 2 * nr:]
        _ride_start(ride_modes, pl.program_id(0), N // TM, ride_in, ride_out, sems)
        x = x_ref[...]
        r = lax.rsqrt(_mean_last(x * x) + EPS)
        h = (x * r * pw_ref[...]) * (1.0 + mod_ref[1:2, :]) + mod_ref[0:1, :]
        hb = _bf(h)
        h1_ref[...] = hb
        pa_ref[...] = _dot_nt(hb, w_ref[:ATT_COLS, :])
        ph_ref[...] = _dot_nt(hb, w_ref[ATT_COLS:, :])
        _ride_wait(ride_modes, pl.program_id(0), N // TM, ride_in, ride_out, sems)

    return pl.pallas_call(
        body, name="in_proj", grid=(N // TM,),
        in_specs=[pl.BlockSpec((TM, D_MODEL), lambda i: (i, 0)), _mod_spec(tps),
                  pl.BlockSpec((1, D_MODEL), lambda i: (0, 0)),
                  pl.BlockSpec((IN_COLS, D_MODEL), lambda i: (0, 0))] + [ANY_SPEC] * nr,
        out_specs=[pl.BlockSpec((TM, ATT_COLS), lambda i: (i, 0)),
                   pl.BlockSpec((TM, IN_COLS - ATT_COLS), lambda i: (i, 0)),
                   pl.BlockSpec((TM, D_MODEL), lambda i: (i, 0))] + [ANY_SPEC] * nr,
        out_shape=[SDS((N, ATT_COLS), F32), SDS((N, IN_COLS - ATT_COLS), F32), SDS((N, D_MODEL), BF16)]
        + _exchange_shapes(ride_srcs, ride_modes),
        scratch_shapes=_exchange_sems(nr),
        compiler_params=_params(("arbitrary",), VMEM_LIMIT_BIG),
    )(x2, mod8, pre_w, w_in_bf, *ride_srcs)


def _rope_tables(T):
    half = ROPE_DIM // 2
    f32 = np.float32
    inv_freq = (f32(ROPE_THETA) ** (-np.arange(0, ROPE_DIM, 2, dtype=f32) / f32(ROPE_DIM))).astype(f32)
    ang = np.arange(T, dtype=f32)[:, None] * inv_freq[None, :]
    cos, sin = np.cos(ang).astype(f32), np.sin(ang).astype(f32)
    ones = np.ones((T, ATT_HEAD_DIM - ROPE_DIM), f32)
    zeros = np.zeros((T, ATT_HEAD_DIM - ROPE_DIM), f32)
    zh = np.zeros((T, half), f32)
    cos64 = np.concatenate([cos, cos, ones], axis=1)
    sin_left = np.concatenate([-sin, zh, zeros], axis=1)
    sin_right = np.concatenate([zh, sin, zeros], axis=1)
    rep = LANES // ATT_HEAD_DIM
    return tuple(jnp.asarray(np.tile(t, (1, rep))) for t in (cos64, sin_left, sin_right))


def _rope(xc, cs, sl, sr):
    return xc * cs + pltpu.roll(xc, LANES - 8, 1) * sl + pltpu.roll(xc, 8, 1) * sr


def _rope_t(dy, cs, sl, sr):
    return dy * cs + pltpu.roll(dy * sl, 8, 1) + pltpu.roll(dy * sr, LANES - 8, 1)


ATT_SCALE = ATT_HEAD_DIM ** -0.5
ATT_SPLITS = 4


def _lower_mask():
    j = lax.broadcasted_iota(jnp.int32, (WINDOW, ATT_GROUP * WINDOW), 0)
    i = lax.broadcasted_iota(jnp.int32, (WINDOW, ATT_GROUP * WINDOW), 1) & (WINDOW - 1)
    return j <= i


def _sink_row(sink_ref, hk):
    return jnp.concatenate(
        [jnp.full((1, WINDOW), sink_ref[0, ATT_GROUP * hk + g], F32) for g in range(ATT_GROUP)], axis=1)


def _softmax_window(qs, k_cur, k_prev, lower, has_prev, sink):
    s_prev = jnp.where(has_prev, _dot_nt(k_prev, qs), jnp.finfo(F32).min)
    s = jnp.where(lower, _dot_nt(k_cur, qs), s_prev)
    m = jnp.maximum(jnp.max(s, axis=0, keepdims=True), sink)
    p = jnp.exp(s - m)
    es = jnp.exp(sink - m)
    inv = 1.0 / (jnp.sum(p, axis=0, keepdims=True) + es)
    return p, inv, es


def _stack_heads(parts, hk):
    hs = []
    for g in range(ATT_GROUP):
        h = ATT_GROUP * hk + g
        hs.append(parts[h // 2][:, (h % 2) * ATT_HEAD_DIM:(h % 2 + 1) * ATT_HEAD_DIM])
    return jnp.concatenate(hs, axis=0)


def _attn_fwd(proj3, tables, sinks, attn_w, ride_srcs, ride_modes):
    B, T, _ = proj3.shape
    nb = T // WINDOW
    splits = min(ATT_SPLITS, nb)
    per = nb // splits
    nr = len(ride_srcs)
    cos, sinl, sinr = tables

    def body(*refs):
        q_ref, k_ref, v_ref, cos_ref, sl_ref, sr_ref, sink_ref, aw_ref = refs[:8]
        ride_in = refs[8:8 + nr]
        o_ref, an_ref, qr_ref, kr_ref = refs[8 + nr:12 + nr]
        ride_out = refs[12 + nr:12 + 2 * nr]
        kpad, vpad = refs[12 + 2 * nr:14 + 2 * nr]
        sems = refs[14 + 2 * nr:]
        part = pl.program_id(1)
        step = pl.program_id(0) * splits + part
        _ride_start(ride_modes, step, B * splits, ride_in, ride_out, sems)

        @pl.when(part == 0)
        def _():
            kpad[0:WINDOW, :] = jnp.zeros((WINDOW, LANES), BF16)
            vpad[0:WINDOW, :] = jnp.zeros((WINDOW, LANES), BF16)

        lower = _lower_mask()

        def block(n, carry):
            r0 = pl.multiple_of(n * WINDOW, WINDOW)
            rows = pl.ds(r0, WINDOW)
            nxt = pl.ds(r0 + WINDOW, WINDOW)
            cs, sl, sr = cos_ref[rows, :], sl_ref[rows, :], sr_ref[rows, :]
            kb = _bf(_rope(k_ref[rows, :], cs, sl, sr))
            vb = _bf(v_ref[rows, :])
            kpad[nxt, :] = kb
            kr_ref[rows, :] = kb
            vpad[nxt, :] = vb
            qparts = []
            for j in range(ATT_WIDTH // LANES):
                qp = _bf(_rope(q_ref[rows, j * LANES:(j + 1) * LANES], cs, sl, sr) * ATT_SCALE)
                qr_ref[rows, j * LANES:(j + 1) * LANES] = qp
                qparts.append(qp)
            for hk in range(ATT_KV_HEADS):
                lanes = slice(hk * ATT_HEAD_DIM, (hk + 1) * ATT_HEAD_DIM)
                qs = _stack_heads(qparts, hk)
                p, inv, _ = _softmax_window(qs, kb[:, lanes], kpad[rows, lanes], lower, n > 0,
                                            _sink_row(sink_ref, hk))
                p_cur = jnp.where(lower, p, 0.0)
                ot = (_dot_tn(vb[:, lanes], _bf(p_cur)) + _dot_tn(vpad[rows, lanes], _bf(p - p_cur))) * inv
                for g in range(ATT_GROUP):
                    h = ATT_GROUP * hk + g
                    o_ref[rows, h * ATT_HEAD_DIM:(h + 1) * ATT_HEAD_DIM] = ot[:, g * WINDOW:(g + 1) * WINDOW].T
            ob = o_ref[rows, :]
            an_ref[rows, :] = _bf(ob * lax.rsqrt(_mean_last(ob * ob) + EPS) * aw_ref[...])
            return carry

        _loop_pairs(part * per, per, block, 0)
        _ride_wait(ride_modes, step, B * splits, ride_in, ride_out, sems)

    seq = lambda w, j: pl.BlockSpec((None, T, w), lambda b, s: (b, 0, j))
    full = lambda r, w: pl.BlockSpec((r, w), lambda b, s: (0, 0))
    return pl.pallas_call(
        body, name="attn_fwd", grid=(B, splits),
        in_specs=[seq(ATT_WIDTH, 0), seq(LANES, 4), seq(LANES, 5),
                  full(T, LANES), full(T, LANES), full(T, LANES),
                  pl.BlockSpec(memory_space=pltpu.SMEM), full(1, ATT_WIDTH)] + [ANY_SPEC] * nr,
        out_specs=[seq(ATT_WIDTH, 0), seq(ATT_WIDTH, 0), seq(ATT_WIDTH, 0), seq(LANES, 0)] + [ANY_SPEC] * nr,
        out_shape=[SDS((B, T, ATT_WIDTH), F32), SDS((B, T, ATT_WIDTH), BF16),
                   SDS((B, T, ATT_WIDTH), BF16), SDS((B, T, LANES), BF16)] + _exchange_shapes(ride_srcs, ride_modes),
        scratch_shapes=[pltpu.VMEM((T + WINDOW, LANES), BF16), pltpu.VMEM((T + WINDOW, LANES), BF16)]
        + _exchange_sems(nr),
        compiler_params=_params(("arbitrary", "arbitrary"), VMEM_LIMIT_BIG),
    )(proj3, proj3, proj3, cos, sinl, sinr, sinks, attn_w, *ride_srcs)


HG_GROUP = 8
HG_ROWS = HG_GROUP * HG_CHUNK


HG_STACK = HG_GROUP * HG_HEAD_DIM


def _group_mask():
    r = lax.broadcasted_iota(jnp.int32, (HG_ROWS, HG_ROWS), 0)
    c = lax.broadcasted_iota(jnp.int32, (HG_ROWS, HG_ROWS), 1)
    return ((r // HG_CHUNK) == (c // HG_CHUNK)) & (r >= c)


def _spread(a):
    blocks = []
    for c in range(HG_GROUP):
        above = jnp.zeros((c * HG_CHUNK, HG_HEAD_DIM), a.dtype)
        below = jnp.zeros(((HG_GROUP - 1 - c) * HG_CHUNK, HG_HEAD_DIM), a.dtype)
        blocks.append(jnp.concatenate([p for p in (above, a[_chunk_rows(c), :], below) if p.shape[0]], axis=0))
    return jnp.concatenate(blocks, axis=1)


def _pick(r):
    return jnp.concatenate([r[_chunk_rows(c), c * HG_HEAD_DIM:(c + 1) * HG_HEAD_DIM] for c in range(HG_GROUP)], axis=0)


def _lane_block(a, c):
    return a[:, c * HG_HEAD_DIM:(c + 1) * HG_HEAD_DIM]


def _chunk_cumsum(a, reverse=False):
    n = a.shape[0]
    pos = lax.broadcasted_iota(jnp.int32, a.shape, 0) % HG_CHUNK
    shift = 1
    while shift < HG_CHUNK:
        if reverse:
            a = a + jnp.where(pos < HG_CHUNK - shift, pltpu.roll(a, n - shift, 0), 0.0)
        else:
            a = a + jnp.where(pos >= shift, pltpu.roll(a, shift, 0), 0.0)
        shift *= 2
    return a


def _chunk_bcast(rows_1x128):
    return jnp.concatenate([jnp.broadcast_to(r, (HG_CHUNK, HG_HEAD_DIM)) for r in rows_1x128], axis=0)


def _hgrn_gates(hq, hf, lb):
    sq = _sigmoid(hq)
    q = hq * sq
    sg = _sigmoid(hf)
    f = lb + (1.0 - lb) * sg
    k = 1.0 - f
    logf = jnp.log(f)
    b = _chunk_cumsum(logf)
    bl = [_sum_rows(logf[_chunk_rows(c), :]) for c in range(HG_GROUP)]
    eb, enb, e2 = jnp.exp(b), jnp.exp(-b), jnp.exp(_chunk_bcast(bl) - b)
    ebl = [jnp.exp(r) for r in bl]
    return dict(sq=sq, sg=sg, f=f, eb=eb, enb=enb, e2=e2, ebl=ebl, qd=q * eb, kd=k * enb, k2=k * e2)


def _chunk_rows(c):
    return slice(c * HG_CHUNK, (c + 1) * HG_CHUNK)


def _head_lanes(h):
    return slice(h * HG_HEAD_DIM, (h + 1) * HG_HEAD_DIM)


def _hgrn_fwd(proj_h, lb, hg_w, ride_srcs, ride_modes):
    B, T, _ = proj_h.shape
    ng = T // HG_ROWS
    nr = len(ride_srcs)

    def body(*refs):
        hq_ref, hf_ref, hi_ref, hg_ref, lb_ref, gw_ref = refs[:6]
        ride_in = refs[6:6 + nr]
        o_ref, rg_ref, sp_ref = refs[6 + nr:9 + nr]
        ride_out = refs[9 + nr:9 + 2 * nr]
        st = refs[9 + 2 * nr]
        sems = refs[10 + 2 * nr:]
        gi = pl.program_id(1)
        step = pl.program_id(0) * ng + gi
        _ride_start(ride_modes, step, B * ng, ride_in, ride_out, sems)

        @pl.when(gi == 0)
        def _():
            st[...] = jnp.zeros(st.shape, F32)

        lo = _group_mask()
        for h in range(HG_HEADS):
            lanes = _head_lanes(h)
            gt = _hgrn_gates(hq_ref[:, lanes], hf_ref[:, lanes], lb_ref[:, lanes])
            v, qd, kd = _bf(hi_ref[:, lanes]), _bf(gt["qd"]), _bf(gt["kd"])
            a = jnp.where(lo, _dot_nt(qd, kd), 0.0)
            kv = _dot_tn(v, _spread(_bf(gt["k2"])))
            s = st[h]
            before = []
            for c in range(HG_GROUP):
                before.append(s)
                s = s * gt["ebl"][c] + _lane_block(kv, c)
            st[h] = s
            sp = jnp.concatenate(before, axis=1)
            sp_ref[h] = sp
            o = _dot(_bf(a), v) + _dot_nt(_spread(qd), _bf(sp))
            o_ref[:, lanes] = o
            hg = hg_ref[:, lanes]
            rn = o * lax.rsqrt(_mean_last(o * o) + EPS) * gw_ref[...]
            rg_ref[:, lanes] = _bf(rn * (hg * _sigmoid(hg)))
        _ride_wait(ride_modes, step, B * ng, ride_in, ride_out, sems)

    part = lambda j: pl.BlockSpec((None, HG_ROWS, HG_WIDTH), lambda b, g: (b, g, j))
    return pl.pallas_call(
        body, name="hgrn_fwd", grid=(B, ng),
        in_specs=[part(0), part(1), part(2), part(3),
                  pl.BlockSpec((1, HG_WIDTH), lambda b, g: (0, 0)),
                  pl.BlockSpec((1, LANES), lambda b, g: (0, 0))] + [ANY_SPEC] * nr,
        out_specs=[part(0), part(0),
                   pl.BlockSpec((None, HG_HEADS, None, HG_HEAD_DIM, HG_STACK), lambda b, g: (b, 0, g, 0, 0))]
        + [ANY_SPEC] * nr,
        out_shape=[SDS((B, T, HG_WIDTH), F32), SDS((B, T, HG_WIDTH), BF16),
                   SDS((B, HG_HEADS, ng, HG_HEAD_DIM, HG_STACK), F32)] + _exchange_shapes(ride_srcs, ride_modes),
        scratch_shapes=[pltpu.VMEM((HG_HEADS, HG_HEAD_DIM, HG_HEAD_DIM), F32)] + _exchange_sems(nr),
        compiler_params=_params(("arbitrary", "arbitrary"), VMEM_LIMIT_BIG),
    )(proj_h, proj_h, proj_h, proj_h, lb, hg_w, *ride_srcs)


def _mix_out(x2, attn_n, rec_g, mod8, post_w, w_out_bf, T, ride_srcs, ride_modes):
    N = x2.shape[0]
    TM = _tile_rows(T, big=True)
    tps = T // TM
    nr = len(ride_srcs)

    def body(*refs):
        x_ref, an_ref, rg_ref, mod_ref, pw_ref, w_ref = refs[:6]
        ride_in = refs[6:6 + nr]
        mix_ref, x1_ref, cat_ref = refs[6 + nr:9 + nr]
        ride_out = refs[9 + nr:9 + 2 * nr]
        sems = refs[9 + 2 * nr:]
        _ride_start(ride_modes, pl.program_id(0), N // TM, ride_in, ride_out, sems)
        cat = jnp.concatenate([an_ref[...], rg_ref[...]], axis=1)
        cat_ref[...] = cat
        mix = _dot(cat, w_ref[...])
        mix_ref[...] = mix
        r = lax.rsqrt(_mean_last(mix * mix) + EPS)
        x1_ref[...] = x_ref[...] + mod_ref[2:3, :] * (mix * r * pw_ref[...])
        _ride_wait(ride_modes, pl.program_id(0), N // TM, ride_in, ride_out, sems)

    row = lambda w: pl.BlockSpec((TM, w), lambda i: (i, 0))
    return pl.pallas_call(
        body, name="mix_out", grid=(N // TM,),
        in_specs=[row(D_MODEL), row(ATT_WIDTH), row(HG_WIDTH), _mod_spec(tps),
                  pl.BlockSpec((1, D_MODEL), lambda i: (0, 0)),
                  pl.BlockSpec((D_MODEL, D_MODEL), lambda i: (0, 0))] + [ANY_SPEC] * nr,
        out_specs=[row(D_MODEL), row(D_MODEL), row(D_MODEL)] + [ANY_SPEC] * nr,
        out_shape=[SDS((N, D_MODEL), F32), SDS((N, D_MODEL), F32), SDS((N, D_MODEL), BF16)]
        + _exchange_shapes(ride_srcs, ride_modes),
        scratch_shapes=_exchange_sems(nr),
        compiler_params=_params(("arbitrary",), VMEM_LIMIT_BIG),
    )(x2, attn_n, rec_g, mod8, post_w, w_out_bf, *ride_srcs)


def _load_weights_once(pairs, sem):
    @pl.when(pl.program_id(0) == 0)
    def _():
        cps = [pltpu.make_async_copy(src, dst, sem.at[i]) for i, (src, dst) in enumerate(pairs)]
        for cp in cps:
            cp.start()
        for cp in cps:
            cp.wait()


MLP_HALF = D_MODEL // 2
MLP_PIECES = 2 * N_DEV + 2


def _mlp_weight_pieces(wu_a, wu_b, wd_a, wd_b, wu, wd):
    cols = D_FF // N_DEV
    pairs = []
    for h, half in enumerate((wu_a, wu_b)):
        for j in range(N_DEV):
            pairs.append((half.at[j], wu.at[pl.ds(h * MLP_HALF, MLP_HALF), pl.ds(j * cols, cols)]))
    for h, half in enumerate((wd_a, wd_b)):
        pairs.append((half, wd.at[:, pl.ds(h * MLP_HALF, MLP_HALF)]))
    return pairs


def _mlp_fwd(x1, mod8, pre_w, w_up_halves, w_down_halves, T):
    N = x1.shape[0]
    TM = _tile_rows(T)
    tps = T // TM

    def body(x_ref, mod_ref, pw_ref, wua, wub, wda, wdb, up_ref, u_ref, d_ref, h2_ref, wu, wd, sem):
        _load_weights_once(_mlp_weight_pieces(wua, wub, wda, wdb, wu, wd), sem)
        x = x_ref[...]
        r = lax.rsqrt(_mean_last(x * x) + EPS)
        h = (x * r * pw_ref[...]) * (1.0 + mod_ref[4:5, :]) + mod_ref[3:4, :]
        hb = _bf(h)
        h2_ref[...] = hb
        up = _dot(hb, wu[...])
        up_ref[...] = up
        ru = jnp.maximum(up, 0.0)
        u = _bf(ru * ru)
        u_ref[...] = u
        d_ref[...] = _dot(u, wd[...])

    row = lambda w: pl.BlockSpec((TM, w), lambda i: (i, 0))
    return pl.pallas_call(
        body, name="mlp_fwd", grid=(N // TM,),
        in_specs=[row(D_MODEL), _mod_spec(tps), pl.BlockSpec((1, D_MODEL), lambda i: (0, 0))] + [ANY_SPEC] * 4,
        out_specs=[row(D_FF), row(D_FF), row(D_MODEL), row(D_MODEL)],
        out_shape=[SDS((N, D_FF), F32), SDS((N, D_FF), BF16), SDS((N, D_MODEL), F32), SDS((N, D_MODEL), BF16)],
        scratch_shapes=[pltpu.VMEM((D_MODEL, D_FF), BF16), pltpu.VMEM((D_FF, D_MODEL), BF16),
                        pltpu.SemaphoreType.DMA((MLP_PIECES,))],
        compiler_params=_params(("arbitrary",), VMEM_LIMIT_BIG),
    )(x1, mod8, pre_w, *w_up_halves, *w_down_halves)


def _acc_rows(acc_ref, first, rows):
    @pl.when(first)
    def _():
        acc_ref[...] = jnp.zeros(acc_ref.shape, F32)
    for i, r in enumerate(rows):
        acc_ref[i:i + 1, :] += r


def _mlp_bwd(x1, d, up, tgt, mod8, pre_w, post_w, w_up_halves, w_down_halves, T):
    N = x1.shape[0]
    TM = _tile_rows(T)
    tps = T // TM

    def body(x_ref, d_ref, up_ref, t_ref, mod_ref, pw_ref, qw_ref, wua, wub, wda, wdb,
             dx_ref, dup_ref, dd_ref, acc_ref, wd, wu, sem):
        _load_weights_once(_mlp_weight_pieces(wua, wub, wda, wdb, wu, wd), sem)
        sh2, sc2, g2 = mod_ref[3:4, :], mod_ref[4:5, :], mod_ref[5:6, :]
        x = x_ref[...]
        r1 = lax.rsqrt(_mean_last(x * x) + EPS)
        xh = x * r1
        n2 = xh * pw_ref[...]
        dv = d_ref[...]
        rd = lax.rsqrt(_mean_last(dv * dv) + EPS)
        dh = dv * rd
        rr = dh * qw_ref[...]
        e = x + g2 * rr - t_ref[...]
        loss = 0.5 * jnp.sum(_sum_rows(e * e), axis=1, keepdims=True) / D_MODEL
        dy = e * (1.0 / D_MODEL)
        dg2 = _sum_rows(dy * rr)
        drr = dy * g2
        dw_post = _sum_rows(drr * dh)
        ddh = drr * qw_ref[...]
        dd = _bf(rd * (ddh - dh * _mean_last(ddh * dh)))
        dd_ref[...] = dd
        ru = jnp.maximum(up_ref[...], 0.0)
        dup = _bf(_dot_nt(dd, wd[...]) * (2.0 * ru))
        dup_ref[...] = dup
        dh2 = _dot_nt(dup, wu[...])
        dsh2 = _sum_rows(dh2)
        dsc2 = _sum_rows(dh2 * n2)
        dn2 = dh2 * (1.0 + sc2)
        dw_pre = _sum_rows(dn2 * xh)
        dxh = dn2 * pw_ref[...]
        dx_ref[...] = dy + r1 * (dxh - xh * _mean_last(dxh * xh))
        _acc_rows(acc_ref, pl.program_id(0) % tps == 0,
                  [dsh2, dsc2, dg2, dw_pre, dw_post, jnp.broadcast_to(loss, (1, D_MODEL))])

    row = lambda w: pl.BlockSpec((TM, w), lambda i: (i, 0))
    vec = pl.BlockSpec((1, D_MODEL), lambda i: (0, 0))
    B = N // T
    return pl.pallas_call(
        body, name="mlp_bwd", grid=(N // TM,),
        in_specs=[row(D_MODEL), row(D_MODEL), row(D_FF), row(D_MODEL), _mod_spec(tps), vec, vec] + [ANY_SPEC] * 4,
        out_specs=[row(D_MODEL), row(D_FF), row(D_MODEL), _mod_spec(tps)],
        out_shape=[SDS((N, D_MODEL), F32), SDS((N, D_FF), BF16), SDS((N, D_MODEL), BF16),
                   SDS((B, 8, D_MODEL), F32)],
        scratch_shapes=[pltpu.VMEM((D_FF, D_MODEL), BF16), pltpu.VMEM((D_MODEL, D_FF), BF16),
                        pltpu.SemaphoreType.DMA((MLP_PIECES,))],
        compiler_params=_params(("arbitrary",), VMEM_LIMIT_BIG),
    )(x1, d, up, tgt, mod8, pre_w, post_w, *w_up_halves, *w_down_halves)


def _mix_bwd(mix, dx1, mod8, post_w, w_out_bf, T, ride_srcs, ride_modes):
    N = mix.shape[0]
    TM = _tile_rows(T, big=True)
    tps = T // TM
    nr = len(ride_srcs)

    def body(*refs):
        mix_ref, dx_ref, mod_ref, pw_ref, w_ref = refs[:5]
        ride_in = refs[5:5 + nr]
        dan_ref, drg_ref, dmix_ref, acc_ref = refs[5 + nr:9 + nr]
        ride_out = refs[9 + nr:9 + 2 * nr]
        sems = refs[9 + 2 * nr:]
        _ride_start(ride_modes, pl.program_id(0), N // TM, ride_in, ride_out, sems)
        g1 = mod_ref[2:3, :]
        mix = mix_ref[...]
        dx1 = dx_ref[...]
        rm = lax.rsqrt(_mean_last(mix * mix) + EPS)
        mh = mix * rm
        dg1 = _sum_rows(dx1 * (mh * pw_ref[...]))
        dr = dx1 * g1
        dw_post = _sum_rows(dr * mh)
        dmh = dr * pw_ref[...]
        dmix = _bf(rm * (dmh - mh * _mean_last(dmh * mh)))
        dmix_ref[...] = dmix
        dcat = _dot_nt(dmix, w_ref[...])
        dan_ref[...] = dcat[:, :ATT_WIDTH]
        drg_ref[...] = dcat[:, ATT_WIDTH:]
        _acc_rows(acc_ref, pl.program_id(0) % tps == 0, [dg1, dw_post])
        _ride_wait(ride_modes, pl.program_id(0), N // TM, ride_in, ride_out, sems)

    row = lambda w: pl.BlockSpec((TM, w), lambda i: (i, 0))
    B = N // T
    return pl.pallas_call(
        body, name="mix_bwd", grid=(N // TM,),
        in_specs=[row(D_MODEL), row(D_MODEL), _mod_spec(tps), pl.BlockSpec((1, D_MODEL), lambda i: (0, 0)),
                  pl.BlockSpec((D_MODEL, D_MODEL), lambda i: (0, 0))] + [ANY_SPEC] * nr,
        out_specs=[row(ATT_WIDTH), row(HG_WIDTH), row(D_MODEL), _mod_spec(tps)] + [ANY_SPEC] * nr,
        out_shape=[SDS((N, ATT_WIDTH), F32), SDS((N, HG_WIDTH), F32), SDS((N, D_MODEL), BF16),
                   SDS((B, 8, D_MODEL), F32)] + _exchange_shapes(ride_srcs, ride_modes),
        scratch_shapes=_exchange_sems(nr),
        compiler_params=_params(("arbitrary",), VMEM_LIMIT_BIG),
    )(mix, dx1, mod8, post_w, w_out_bf, *ride_srcs)


def _hgrn_bwd(proj_h, lb, hg_w, o, s_prev, drg, ride_srcs, ride_modes):
    B, T, _ = proj_h.shape
    ng = T // HG_ROWS
    nr = len(ride_srcs)

    def body(*refs):
        hq_ref, hf_ref, hi_ref, hg_ref, lb_ref, gw_ref, o_ref, sp_ref, drg_ref = refs[:9]
        ride_in = refs[9:9 + nr]
        dhq_ref, dhf_ref, dhi_ref, dhg_ref, dlb_ref, dgw_ref = refs[9 + nr:15 + nr]
        ride_out = refs[15 + nr:15 + 2 * nr]
        dst = refs[15 + 2 * nr]
        sems = refs[16 + 2 * nr:]
        step = pl.program_id(0) * ng + pl.program_id(1)
        _ride_start(ride_modes, step, B * ng, ride_in, ride_out, sems)

        @pl.when(pl.program_id(1) == 0)
        def _():
            dst[...] = jnp.zeros(dst.shape, F32)
            dlb_ref[...] = jnp.zeros(dlb_ref.shape, F32)
            dgw_ref[...] = jnp.zeros(dgw_ref.shape, F32)

        lo = _group_mask()
        gw = gw_ref[...]

        for h in range(HG_HEADS):
            lanes = _head_lanes(h)
            lbv = lb_ref[:, lanes]
            hq = hq_ref[:, lanes]
            gt = _hgrn_gates(hq, hf_ref[:, lanes], lbv)
            sq, sg, qdf, kdf, k2f, ebl = gt["sq"], gt["sg"], gt["qd"], gt["kd"], gt["k2"], gt["ebl"]
            v, qd, kd = _bf(hi_ref[:, lanes]), _bf(qdf), _bf(kdf)
            ov = o_ref[:, lanes]
            hg = hg_ref[:, lanes]
            shg = _sigmoid(hg)
            dr = drg_ref[:, lanes]
            ro = lax.rsqrt(_mean_last(ov * ov) + EPS)
            oh = ov * ro
            dhg_ref[:, lanes] = _bf(dr * (oh * gw) * (shg + hg * shg * (1.0 - shg)))
            drn = dr * (hg * shg)
            dgw_ref[...] += jnp.broadcast_to(_sum_rows(drn * oh), (8, LANES))
            doh = drn * gw
            do = _bf(ro * (doh - oh * _mean_last(doh * oh)))
            a = jnp.where(lo, _dot_nt(qd, kd), 0.0)
            da = _bf(jnp.where(lo, _dot_nt(do, v), 0.0))
            dv = _dot_tn(_bf(a), do)
            dqd = _dot(da, kd)
            dkd = _dot_tn(da, qd)
            sp = sp_ref[h]
            incr = _dot_tn(do, _spread(qd))
            ds = dst[h]
            after = [None] * HG_GROUP
            for c in reversed(range(HG_GROUP)):
                after[c] = ds
                ds = ds * ebl[c] + _lane_block(incr, c)
            dst[h] = ds
            dss = jnp.concatenate(after, axis=1)
            dssb = _bf(dss)
            dk2 = _pick(_dot(v, dssb))
            dhi_ref[:, lanes] = _bf(dv + _dot_nt(_spread(_bf(k2f)), dssb))
            dqd = dqd + _pick(_dot(do, _bf(sp)))
            debl = _sum_rows(dss * sp)
            k2g = dk2 * k2f
            db = dqd * qdf - dkd * kdf - k2g
            dk = dkd * gt["enb"] + dk2 * gt["e2"]
            dbl = _chunk_bcast([_lane_block(debl, c) * ebl[c] + _sum_rows(k2g[_chunk_rows(c), :])
                                for c in range(HG_GROUP)])
            dg = _chunk_cumsum(db, reverse=True) + dbl
            df = dg / gt["f"] - dk
            dhf_ref[:, lanes] = _bf(df * (1.0 - lbv) * sg * (1.0 - sg))
            dlb_ref[:, lanes] += jnp.broadcast_to(_sum_rows(df * (1.0 - sg)), (8, LANES))
            dhq_ref[:, lanes] = _bf((dqd * gt["eb"]) * (sq + hq * sq * (1.0 - sq)))
        _ride_wait(ride_modes, step, B * ng, ride_in, ride_out, sems)

    part = lambda j: pl.BlockSpec((None, HG_ROWS, HG_WIDTH), lambda b, g: (b, ng - 1 - g, j))
    return pl.pallas_call(
        body, name="hgrn_bwd", grid=(B, ng),
        in_specs=[part(0), part(1), part(2), part(3),
                  pl.BlockSpec((1, HG_WIDTH), lambda b, g: (0, 0)),
                  pl.BlockSpec((1, LANES), lambda b, g: (0, 0)),
                  part(0),
                  pl.BlockSpec((None, HG_HEADS, None, HG_HEAD_DIM, HG_STACK), lambda b, g: (b, 0, ng - 1 - g, 0, 0)),
                  part(0)] + [ANY_SPEC] * nr,
        out_specs=[part(0), part(0), part(0), part(0),
                   pl.BlockSpec((None, 8, HG_WIDTH), lambda b, g: (b, 0, 0)),
                   pl.BlockSpec((None, 8, LANES), lambda b, g: (b, 0, 0))] + [ANY_SPEC] * nr,
        out_shape=[SDS((B, T, HG_WIDTH), BF16)] * 4 + [SDS((B, 8, HG_WIDTH), F32), SDS((B, 8, LANES), F32)]
        + _exchange_shapes(ride_srcs, ride_modes),
        scratch_shapes=[pltpu.VMEM((HG_HEADS, HG_HEAD_DIM, HG_HEAD_DIM), F32)] + _exchange_sems(nr),
        compiler_params=_params(("arbitrary", "arbitrary"), VMEM_LIMIT_BIG),
    )(proj_h, proj_h, proj_h, proj_h, lb, hg_w, o, s_prev, drg, *ride_srcs)


def _attn_bwd(qr, kr, proj3, attn_o, dan, tables, sinks, attn_w, ride_srcs, ride_modes):
    B, T, _ = proj3.shape
    nb = T // WINDOW
    splits = min(ATT_SPLITS, nb)
    per = nb // splits
    nr = len(ride_srcs)
    cos, sinl, sinr = tables
    QKV = ATT_WIDTH + 2 * LANES

    def body(*refs):
        qr_ref, kr_ref, v_ref, o_ref, dan_ref, cos_ref, sl_ref, sr_ref, sink_ref, aw_ref = refs[:10]
        ride_in = refs[10:10 + nr]
        dqkv_ref, dsink_ref, daw_ref = refs[10 + nr:13 + nr]
        ride_out = refs[13 + nr:13 + 2 * nr]
        kpad, vpad, dkpad, dvpad, dqb, dsk = refs[13 + 2 * nr:19 + 2 * nr]
        sems = refs[19 + 2 * nr:]
        part = pl.program_id(1)
        step = pl.program_id(0) * splits + part
        _ride_start(ride_modes, step, B * splits, ride_in, ride_out, sems)

        @pl.when(part == 0)
        def _():
            kpad[0:WINDOW, :] = jnp.zeros((WINDOW, LANES), BF16)
            vpad[0:WINDOW, :] = jnp.zeros((WINDOW, LANES), BF16)
            kpad[WINDOW:, :] = kr_ref[...]
            vpad[WINDOW:, :] = _bf(v_ref[...])
            dkpad[...] = jnp.zeros(dkpad.shape, F32)
            dvpad[...] = jnp.zeros(dvpad.shape, F32)
            dsk[...] = jnp.zeros(dsk.shape, F32)
            daw_ref[...] = jnp.zeros(daw_ref.shape, F32)

        lower = _lower_mask()
        aw = aw_ref[...]

        def block(n, daw):
            r0 = pl.multiple_of(n * WINDOW, WINDOW)
            rows = pl.ds(r0, WINDOW)
            nxt = pl.ds(r0 + WINDOW, WINDOW)
            ob = o_ref[rows, :]
            dn = dan_ref[rows, :]
            ro = lax.rsqrt(_mean_last(ob * ob) + EPS)
            oh = ob * ro
            daw = daw + _sum_rows(dn * oh)
            doh = dn * aw
            do = _bf(ro * (doh - oh * _mean_last(doh * oh)))
            doparts = [do[:, j * LANES:(j + 1) * LANES] for j in range(ATT_WIDTH // LANES)]
            qparts = [qr_ref[rows, j * LANES:(j + 1) * LANES] for j in range(ATT_WIDTH // LANES)]
            for hk in range(ATT_KV_HEADS):
                lanes = slice(hk * ATT_HEAD_DIM, (hk + 1) * ATT_HEAD_DIM)
                qs = _stack_heads(qparts, hk)
                dos = _stack_heads(doparts, hk)
                k_cur, k_prev = kpad[nxt, lanes], kpad[rows, lanes]
                v_cur, v_prev = vpad[nxt, lanes], vpad[rows, lanes]
                p, inv, es = _softmax_window(qs, k_cur, k_prev, lower, n > 0, _sink_row(sink_ref, hk))
                p = p * inv
                dp = jnp.where(lower, _dot_nt(v_cur, dos), _dot_nt(v_prev, dos))
                delta = jnp.sum(p * dp, axis=0, keepdims=True)
                ds = p * (dp - delta)
                sk = (es * inv) * delta
                ds_cur = jnp.where(lower, ds, 0.0)
                p_cur = jnp.where(lower, p, 0.0)
                ds_cur, ds_prev = _bf(ds_cur), _bf(ds - ds_cur)
                p_cur, p_prev = _bf(p_cur), _bf(p - p_cur)
                dqt = (_dot_tn(k_cur, ds_cur) + _dot_tn(k_prev, ds_prev)) * ATT_SCALE
                dkpad[nxt, lanes] += _dot(ds_cur, qs)
                dkpad[rows, lanes] += _dot(ds_prev, qs)
                dvpad[nxt, lanes] += _dot(p_cur, dos)
                dvpad[rows, lanes] += _dot(p_prev, dos)
                for g in range(ATT_GROUP):
                    h = ATT_GROUP * hk + g
                    cols = slice(g * WINDOW, (g + 1) * WINDOW)
                    dqb[:, h * ATT_HEAD_DIM:(h + 1) * ATT_HEAD_DIM] = dqt[:, cols].T
                    head_lane = lax.broadcasted_iota(jnp.int32, dsk.shape, 1) == h
                    dsk[...] += jnp.where(head_lane, -jnp.sum(sk[:, cols], axis=1, keepdims=True), 0.0)
            cs, sl, sr = cos_ref[rows, :], sl_ref[rows, :], sr_ref[rows, :]
            for j in range(ATT_WIDTH // LANES):
                dqkv_ref[rows, j * LANES:(j + 1) * LANES] = _bf(_rope_t(dqb[:, j * LANES:(j + 1) * LANES], cs, sl, sr))
            return daw

        daw = _loop_pairs(part * per, per, block, jnp.zeros((1, ATT_WIDTH), F32))
        daw_ref[...] += jnp.broadcast_to(daw, (8, ATT_WIDTH))
        dsink_ref[...] = dsk[...]

        def finish(n, carry):
            r0 = pl.multiple_of(n * WINDOW, WINDOW)
            rows = pl.ds(r0, WINDOW)
            nxt = pl.ds(r0 + WINDOW, WINDOW)
            cs, sl, sr = cos_ref[rows, :], sl_ref[rows, :], sr_ref[rows, :]
            dqkv_ref[rows, ATT_WIDTH:ATT_WIDTH + LANES] = _bf(_rope_t(dkpad[nxt, :], cs, sl, sr))
            dqkv_ref[rows, ATT_WIDTH + LANES:QKV] = _bf(dvpad[nxt, :])
            return carry

        @pl.when(part == splits - 1)
        def _():
            lax.fori_loop(0, nb, finish, 0)

        _ride_wait(ride_modes, step, B * splits, ride_in, ride_out, sems)

    seq = lambda w, j: pl.BlockSpec((None, T, w), lambda b, s: (b, 0, j))
    full = lambda r, w: pl.BlockSpec((r, w), lambda b, s: (0, 0))
    return pl.pallas_call(
        body, name="attn_bwd", grid=(B, splits),
        in_specs=[seq(ATT_WIDTH, 0), seq(LANES, 0), seq(LANES, 5), seq(ATT_WIDTH, 0), seq(ATT_WIDTH, 0),
                  full(T, LANES), full(T, LANES), full(T, LANES),
                  pl.BlockSpec(memory_space=pltpu.SMEM), full(1, ATT_WIDTH)] + [ANY_SPEC] * nr,
        out_specs=[seq(QKV, 0), pl.BlockSpec((None, 8, LANES), lambda b, s: (b, 0, 0)),
                   pl.BlockSpec((None, 8, ATT_WIDTH), lambda b, s: (b, 0, 0))] + [ANY_SPEC] * nr,
        out_shape=[SDS((B, T, QKV), BF16), SDS((B, 8, LANES), F32), SDS((B, 8, ATT_WIDTH), F32)]
        + _exchange_shapes(ride_srcs, ride_modes),
        scratch_shapes=[pltpu.VMEM((T + WINDOW, LANES), BF16), pltpu.VMEM((T + WINDOW, LANES), BF16),
                        pltpu.VMEM((T + WINDOW, LANES), F32), pltpu.VMEM((T + WINDOW, LANES), F32),
                        pltpu.VMEM((WINDOW, ATT_WIDTH), F32), pltpu.VMEM((8, LANES), F32)] + _exchange_sems(nr),
        compiler_params=_params(("arbitrary", "arbitrary"), VMEM_LIMIT_BIG),
    )(qr, kr, proj3, attn_o, dan, cos, sinl, sinr, sinks, attn_w, *ride_srcs)


def _in_bwd(x2, dx1, dqkv, dhq, dhf, dhi, dhg, mod8, pre_w, w_in_bf, T, ride_srcs, ride_modes):
    N = x2.shape[0]
    TM = _tile_rows(T, big=True)
    tps = T // TM
    nr = len(ride_srcs)
    pieces = [(0, ATT_WIDTH + 2 * LANES), (768, HG_WIDTH), (1280, HG_WIDTH), (1792, HG_WIDTH), (2304, HG_WIDTH)]

    def body(*refs):
        x_ref, dx_ref, p0, p1, p2, p3, p4, mod_ref, pw_ref, w_ref = refs[:10]
        ride_in = refs[10:10 + nr]
        gx_ref, dproj_ref, acc_ref = refs[10 + nr:13 + nr]
        ride_out = refs[13 + nr:13 + 2 * nr]
        sems = refs[13 + 2 * nr:]
        _ride_start(ride_modes, pl.program_id(0), N // TM, ride_in, ride_out, sems)
        sc1 = mod_ref[1:2, :]
        dh = jnp.zeros((TM, D_MODEL), F32)
        for ref, (off, width) in zip((p0, p1, p2, p3, p4), pieces):
            pb = ref[...]
            dproj_ref[:, off:off + width] = pb
            dh = dh + _dot(pb, w_ref[off:off + width, :])
        x = x_ref[...]
        r = lax.rsqrt(_mean_last(x * x) + EPS)
        xh = x * r
        n1 = xh * pw_ref[...]
        dsh1 = _sum_rows(dh)
        dsc1 = _sum_rows(dh * n1)
        dn1 = dh * (1.0 + sc1)
        dw_pre = _sum_rows(dn1 * xh)
        dxh = dn1 * pw_ref[...]
        gx_ref[...] = dx_ref[...] + r * (dxh - xh * _mean_last(dxh * xh))
        _acc_rows(acc_ref, pl.program_id(0) % tps == 0, [dsh1, dsc1, dw_pre])
        _ride_wait(ride_modes, pl.program_id(0), N // TM, ride_in, ride_out, sems)

    row = lambda w: pl.BlockSpec((TM, w), lambda i: (i, 0))
    B = N // T
    return pl.pallas_call(
        body, name="in_bwd", grid=(N // TM,),
        in_specs=[row(D_MODEL), row(D_MODEL), row(768), row(HG_WIDTH), row(HG_WIDTH), row(HG_WIDTH),
                  row(HG_WIDTH), _mod_spec(tps), pl.BlockSpec((1, D_MODEL), lambda i: (0, 0)),
                  pl.BlockSpec((IN_COLS, D_MODEL), lambda i: (0, 0))] + [ANY_SPEC] * nr,
        out_specs=[row(D_MODEL), row(IN_COLS), _mod_spec(tps)] + [ANY_SPEC] * nr,
        out_shape=[SDS((N, D_MODEL), F32), SDS((N, IN_COLS), BF16), SDS((B, 8, D_MODEL), F32)]
        + _exchange_shapes(ride_srcs, ride_modes),
        scratch_shapes=_exchange_sems(nr),
        compiler_params=_params(("arbitrary",), VMEM_LIMIT_BIG),
    )(x2, dx1, dqkv, dhq, dhf, dhi, dhg, mod8, pre_w, w_in_bf, *ride_srcs)


def _matmul_tn(name, a, b, tn, tm=512, by_owner_cols=False):
    K, M = a.shape
    Nc = b.shape[1]
    tm = min(tm, M)

    def body(a_ref, b_ref, o_ref):
        o_ref[...] = _bf(_dot_tn(a_ref[...], b_ref[...]))

    if by_owner_cols:
        assert tn * N_DEV == Nc
        out_shape = SDS((N_DEV, M, tn), BF16)
        out_spec = pl.BlockSpec((None, tm, tn), lambda i, j: (j, i, 0))
    else:
        out_shape = SDS((M, Nc), BF16)
        out_spec = pl.BlockSpec((tm, tn), lambda i, j: (i, j))
    return pl.pallas_call(
        body, name=name, grid=(M // tm, Nc // tn),
        in_specs=[pl.BlockSpec((K, tm), lambda i, j: (0, i)),
                  pl.BlockSpec((K, tn), lambda i, j: (0, j))],
        out_specs=out_spec, out_shape=out_shape,
        compiler_params=_params(("arbitrary", "arbitrary"), VMEM_LIMIT_BIG),
    )(a, b)


GW_BLOCK = IN_COLS // N_DEV
GW_HALF = IN_COLS // 2


def _grad_w_in_reduced(dproj, h1, ride_srcs, ride_modes):
    K = dproj.shape[0]
    nr = len(ride_srcs)
    chips = N_DEV // 2
    tn = 512

    def body(*refs):
        a_hbm, b_hbm = refs[:2]
        ride_in, out, ride_out = refs[2:2 + nr], refs[2 + nr], refs[3 + nr:3 + 2 * nr]
        a_buf, b_buf, g_buf, theirs, p_buf, in_sems, pair_send, pair_recv, chip_send, chip_recv, own_sem = \
            refs[3 + 2 * nr:14 + 2 * nr]
        ride = _exchange_phases(ride_modes, ride_in, ride_out, *refs[14 + 2 * nr:]) if nr else ([], [], [])
        x, y, core = lax.axis_index("x"), lax.axis_index("y"), lax.axis_index("c")
        chip = 2 * x + y
        sib_dev, _ = _related(SIBLING)

        def remote(src, dst, send_sem, recv_sem, dev):
            return pltpu.make_async_remote_copy(src_ref=src, dst_ref=dst, send_sem=send_sem, recv_sem=recv_sem,
                                                device_id=dev, device_id_type=MESH)

        halves = [1 - x, x]
        loads = [pltpu.make_async_copy(b_hbm, b_buf, in_sems.at[0])]
        for t in range(2):
            col = pl.multiple_of(halves[t] * GW_HALF, LANES)
            loads.append(pltpu.make_async_copy(a_hbm.at[:, pl.ds(col, GW_HALF)], a_buf.at[t], in_sems.at[1 + t]))
        for cp in loads:
            cp.start()
        _run(ride[0])
        loads[0].wait()
        end = []
        for t in range(2):
            loads[1 + t].wait()
            if t == 1:
                _run(ride[1])
            for j in range(D_MODEL // tn):
                cols = pl.ds(j * tn, tn)
                res = _dot_tn(a_buf[t], b_buf[:, cols])
                for q in range(2):
                    for cc in range(2):
                        r0 = (2 * q + cc) * GW_BLOCK
                        g_buf[t, q, cc, :, cols] = _bf(res[r0:r0 + GW_BLOCK])
                swaps = [remote(g_buf.at[t, q, 1 - core, :, cols], theirs.at[t, q, :, cols],
                                pair_send.at[t, 2 * j + q], pair_recv.at[t, 2 * j + q], sib_dev) for q in range(2)]
                for cp in swaps:
                    cp.start()
                for cp in swaps:
                    cp.wait_recv()
                end += [cp.wait_send for cp in swaps]
                for q in range(2):
                    p_buf[t, q, :, cols] = _bf(g_buf[t, q, core, :, cols].astype(F32)
                                               + theirs[t, q, :, cols].astype(F32))
                for dy in range(2):
                    k = 4 * (1 - t) + 2 * dy
                    if k == 0:
                        own = pltpu.make_async_copy(p_buf.at[t, y, :, cols], out.at[chip, :, cols], own_sem.at[j])
                        own.start()
                        end.append(own.wait)
                        continue
                    dev, peer = _related(k)
                    sems = chip_send.at[k // 2, j], chip_recv.at[k // 2, j]
                    send = remote(p_buf.at[t, y ^ dy, :, cols], out.at[chip, :, cols], *sems, dev)
                    send.start()
                    end += [remote(p_buf.at[t, y ^ dy, :, cols], out.at[peer // 2, :, cols], *sems, dev).wait_recv,
                            send.wait_send]
        _run(ride[2])
        _run(end)

    return pl.pallas_call(
        body, name="grad_w_in",
        in_specs=[ANY_SPEC] * (2 + nr), out_specs=[ANY_SPEC] * (1 + nr),
        out_shape=[SDS((chips, GW_BLOCK, D_MODEL), BF16)] + _exchange_shapes(ride_srcs, ride_modes),
        scratch_shapes=[pltpu.VMEM((2, K, GW_HALF), BF16), pltpu.VMEM((K, D_MODEL), BF16),
                        pltpu.VMEM((2, 2, 2, GW_BLOCK, D_MODEL), BF16), pltpu.VMEM((2, 2, GW_BLOCK, D_MODEL), BF16),
                        pltpu.VMEM((2, 2, GW_BLOCK, D_MODEL), BF16), pltpu.SemaphoreType.DMA((3,)),
                        pltpu.SemaphoreType.DMA((2, 4)), pltpu.SemaphoreType.DMA((2, 4)),
                        pltpu.SemaphoreType.DMA((chips, 2)), pltpu.SemaphoreType.DMA((chips, 2)),
                        pltpu.SemaphoreType.DMA((2,))] + _exchange_sems(nr),
        compiler_params=_params(None, VMEM_LIMIT_BIG),
    )(dproj, h1, *ride_srcs)


def _adamw_math(w, g, m, v):
    m2 = ADAM_B1 * m + (1.0 - ADAM_B1) * g
    v2 = ADAM_B2 * v + (1.0 - ADAM_B2) * (g * g)
    m_hat = m2 / (1.0 - ADAM_B1 ** ADAM_STEP)
    v_hat = v2 / (1.0 - ADAM_B2 ** ADAM_STEP)
    delta = -ADAM_LR * (m_hat / (jnp.sqrt(v_hat) + ADAM_EPS) + ADAM_WD * w)
    return delta, m2, v2


def _pair_add(name, gw, theirs):
    chips, _, r, c = gw.shape
    tr = r
    core = lax.axis_index("c").astype(jnp.int32).reshape(1)

    def body(core_ref, mine_ref, theirs_ref, o_ref):
        o_ref[...] = _bf(mine_ref[...].astype(F32) + theirs_ref[...].astype(F32))

    block = pl.BlockSpec((None, tr, c), lambda s, i, core_ref: (s, i, 0))
    grid_spec = pltpu.PrefetchScalarGridSpec(
        num_scalar_prefetch=1, grid=(chips, r // tr),
        in_specs=[pl.BlockSpec((None, None, tr, c), lambda s, i, core_ref: (s, core_ref[0], i, 0)), block],
        out_specs=block)
    return pl.pallas_call(
        body, name=name, grid_spec=grid_spec, out_shape=SDS((chips, r, c), BF16),
        compiler_params=_params(("arbitrary", "arbitrary")),
    )(core, gw, theirs)


def _reduce_adamw(name, parts, w, m, v):
    r, c = w.shape
    tr = r if r % 256 else 256
    slots = parts.shape[0]

    def body(p_ref, w_ref, m_ref, v_ref, g_ref, d_ref, m2_ref, v2_ref):
        g = p_ref[0].astype(F32)
        for s in range(1, slots):
            g = g + p_ref[s].astype(F32)
        g_ref[...] = g
        d_ref[...], m2_ref[...], v2_ref[...] = _adamw_math(w_ref[...], g, m_ref[...], v_ref[...])

    blk = pl.BlockSpec((tr, c), lambda i: (i, 0))
    return pl.pallas_call(
        body, name=name, grid=(r // tr,),
        in_specs=[pl.BlockSpec((slots, tr, c), lambda i: (0, i, 0)), blk, blk, blk],
        out_specs=[blk] * 4, out_shape=[SDS((r, c), F32)] * 4,
        compiler_params=_params(("arbitrary",), VMEM_LIMIT_BIG),
    )(parts, w, m, v)


def _ada_grad_adamw(c_all, dmod_all, w, m, v):
    r, c = w.shape
    tr = 256
    nb = c_all.shape[0]

    def body(c_ref, dm_ref, w_ref, m_ref, v_ref, g_ref, d_ref, m2_ref, v2_ref):
        cv = c_ref[...]
        g = _dot_tn(cv * _sigmoid(cv), dm_ref[...])
        g_ref[...] = g
        d_ref[...], m2_ref[...], v2_ref[...] = _adamw_math(w_ref[...], g, m_ref[...], v_ref[...])

    blk = pl.BlockSpec((tr, c), lambda i: (i, 0))
    return pl.pallas_call(
        body, name="ada_grad_adamw", grid=(r // tr,),
        in_specs=[pl.BlockSpec((nb, tr), lambda i: (0, i)), pl.BlockSpec((nb, c), lambda i: (0, 0)),
                  blk, blk, blk],
        out_specs=[blk] * 4, out_shape=[SDS((r, c), F32)] * 4,
        compiler_params=_params(("arbitrary",)),
    )(c_all, dmod_all, w, m, v)


_SMALL = [("b_ada", 6144), ("pre_w_mix", 1024), ("attn_sinks", 128), ("attn_out_w", 512), ("lb_table", 1024),
          ("hg_norm_w", 128), ("post_w_mix", 1024), ("pre_w_mlp", 1024), ("post_w_mlp", 1024)]


def _pack_small(acc_in, acc_mix, acc_mlp, dsink, daw, dlb, dgw, lb_p, ada_cols):
    B = acc_in.shape[0]
    width = sum(w for _, w in _SMALL) + LANES

    def body(ain, amix, amlp, dsk_ref, daw_ref, dlb_ref, dgw_ref, lbp_ref, packed_ref, dmod_ref):
        def total(ref, r, w=None):
            out = ref[0, r:r + 1, :] if w is None else ref[0, r:r + 1, :w]
            for b in range(1, B):
                out = out + (ref[b, r:r + 1, :] if w is None else ref[b, r:r + 1, :w])
            return out

        d_b_ada = None
        for b in range(B):
            mods = [ain[b, 0:1, :], ain[b, 1:2, :], amix[b, 0:1, :], amlp[b, 0:1, :], amlp[b, 1:2, :], amlp[b, 2:3, :]]
            full = jnp.concatenate(mods, axis=1)
            for j in range(N_DEV):
                dmod_ref[j, b:b + 1, :] = full[:, j * ada_cols:(j + 1) * ada_cols]
            d_b_ada = full if d_b_ada is None else d_b_ada + full
        d_lb = total(dlb_ref, 0)
        pp = lbp_ref[0:1, :] * lbp_ref[1:2, :]
        pieces = [d_b_ada, total(ain, 2), total(dsk_ref, 0), total(daw_ref, 0), -d_lb * pp, d_lb * pp,
                  total(dgw_ref, 0), total(amix, 1), total(amlp, 3), total(amlp, 4), total(amlp, 5, LANES)]
        off = 0
        for piece in pieces:
            packed_ref[:, off:off + piece.shape[1]] = piece
            off += piece.shape[1]

    return pl.pallas_call(
        body, name="pack_small",
        out_shape=[SDS((1, width), F32), SDS((N_DEV, B, ada_cols), F32)],
    )(acc_in, acc_mix, acc_mlp, dsink, daw, dlb, dgw, lb_p)


def _adamw_small(parts, given):
    names = [n for n, _ in _SMALL]
    flat_in = [a for n in names for a in given[n]]

    def body(*refs):
        p_ref = refs[0]
        in_refs = refs[1:1 + 3 * len(names)]
        out_refs = refs[1 + 3 * len(names):-1]
        loss_ref = refs[-1]
        g = p_ref[0]
        for s in range(1, N_DEV):
            g = g + p_ref[s]
        off = 0
        for i, (name, width) in enumerate(_SMALL):
            w_ref, m_ref, v_ref = in_refs[3 * i:3 * i + 3]
            rows, cols = w_ref.shape
            for r in range(rows):
                gr = g[:, off + r * cols:off + (r + 1) * cols]
                res = (gr,) + _adamw_math(w_ref[r:r + 1, :], gr, m_ref[r:r + 1, :], v_ref[r:r + 1, :])
                for o_ref, val in zip(out_refs[4 * i:4 * i + 4], res):
                    o_ref[r:r + 1, :] = val
            off += width
        loss_ref[...] = g[:, off:off + LANES]

    out_shape = [SDS(given[n][0].shape, F32) for n in names for _ in range(4)] + [SDS((1, LANES), F32)]
    outs = pl.pallas_call(body, name="adamw_small", out_shape=out_shape)(parts, *flat_in)
    return {n: tuple(outs[4 * i:4 * i + 4]) for i, n in enumerate(names)}, outs[-1][0, 0]


def kernel(x, c, w_ada, b_ada, pre_w_mix, w_in, attn_sinks, attn_out_w, lb_table, hg_norm_w, w_out, post_w_mix, pre_w_mlp, w_up, w_down, post_w_mlp, loss_target, m_w_ada, m_b_ada, m_pre_w_mix, m_w_in, m_attn_sinks, m_attn_out_w, m_lb_table, m_hg_norm_w, m_w_out, m_post_w_mix, m_pre_w_mlp, m_w_up, m_w_down, m_post_w_mlp, v_w_ada, v_b_ada, v_pre_w_mix, v_w_in, v_attn_sinks, v_attn_out_w, v_lb_table, v_hg_norm_w, v_w_out, v_post_w_mix, v_pre_w_mlp, v_w_up, v_w_down, v_post_w_mlp):
    B, T, _ = x.shape
    N = B * T
    me = 4 * lax.axis_index("x") + 2 * lax.axis_index("y") + lax.axis_index("c")
    x2 = x.reshape(N, D_MODEL)
    tgt2 = loss_target.reshape(N, D_MODEL)

    w_in_t, m_w_in_t, v_w_in_t = w_in[0].T, m_w_in[0].T, v_w_in[0].T
    w_in_g, c_g = _exchange("gather_w_in", [_bf(w_in_t), c], ["gather"] * 2)
    w_in_f = w_in_g.reshape(IN_COLS, D_MODEL)
    c_all = c_g.reshape(N_DEV * B, D_MODEL)

    ada_cols = w_ada.shape[2]
    b_mine = lax.dynamic_slice(b_ada, (0, me * ada_cols), (1, ada_cols))
    mod_cols = _ada_mod(c_all, w_ada[0], b_mine)
    (mod_g,) = _exchange("scatter_mod", [mod_cols.reshape(N_DEV, B, ada_cols)], ["a2a"])
    mod = mod_g.transpose(1, 0, 2).reshape(B, 6, D_MODEL)
    mod8 = jnp.pad(mod, ((0, 0), (0, 2), (0, 0)))

    lb_p = jax.nn.softmax(lb_table, axis=0)
    lb = lb_p[1:2]
    tables = _rope_tables(T)

    w_up_b, w_down_b = _bf(w_up[0]), _bf(w_down[0])
    proj_a, proj_h, h1, w_out_g, w_up_g0 = _in_proj(x2, mod8, pre_w_mix, w_in_f, T,
                                                    [_bf(w_out[0]), w_up_b[:MLP_HALF]], ["gather"] * 2)
    proj3 = proj_a.reshape(B, T, ATT_COLS)
    proj_h = proj_h.reshape(B, T, IN_COLS - ATT_COLS)
    rec_o, rec_g, s_prev, w_up_g1 = _hgrn_fwd(proj_h, lb, hg_norm_w, [w_up_b[MLP_HALF:]], ["gather"])
    attn_o, attn_n, qr, kr, w_down_g0 = _attn_fwd(proj3, tables, attn_sinks, attn_out_w,
                                                  [w_down_b[:, :MLP_HALF]], ["gather"])
    w_out_f = w_out_g.reshape(D_MODEL, D_MODEL)
    mix, x1, cat, w_down_g1 = _mix_out(x2, attn_n.reshape(N, ATT_WIDTH), rec_g.reshape(N, HG_WIDTH), mod8,
                                       post_w_mix, w_out_f, T, [w_down_b[:, MLP_HALF:]], ["gather"])
    w_up_halves = [w_up_g0, w_up_g1]
    w_down_halves = [w_down_g0.reshape(D_FF, MLP_HALF), w_down_g1.reshape(D_FF, MLP_HALF)]
    up, u, d, h2 = _mlp_fwd(x1, mod8, pre_w_mlp, w_up_halves, w_down_halves, T)

    dx1, dup, dd, acc_mlp = _mlp_bwd(x1, d, up, tgt2, mod8, pre_w_mlp, post_w_mlp, w_up_halves, w_down_halves, T)
    chips = N_DEV // 2
    by_chip = lambda a: a.reshape((chips, 2, a.shape[0] // N_DEV) + a.shape[1:])
    gw_up = _matmul_tn("grad_w_up", h2, dup, D_FF // N_DEV, tm=D_MODEL, by_owner_cols=True)
    gw_up = gw_up.reshape(chips, 2, D_MODEL, D_FF // N_DEV)
    gw_down = by_chip(_matmul_tn("grad_w_down", u, dd, D_MODEL))
    dan, drg, dmix, acc_mix, q_down, q_up = _mix_bwd(mix, dx1, mod8, post_w_mix, w_out_f, T,
                                                     [gw_down, gw_up], ["pair"] * 2)
    p_down, p_up = _pair_add("pair_add_w_down", gw_down, q_down), _pair_add("pair_add_w_up", gw_up, q_up)
    gw_out = _matmul_tn("grad_w_out", cat, dmix, 512, tm=D_MODEL).reshape(N_DEV, D_MODEL // N_DEV, D_MODEL)
    dhq, dhf, dhi, dhg, dlb_p, dgw_p, r_down, r_up = _hgrn_bwd(
        proj_h, lb, hg_norm_w, rec_o, s_prev, drg.reshape(B, T, HG_WIDTH), [p_down, p_up], ["chips"] * 2)
    dqkv, dsink_p, daw_p, r_out = _attn_bwd(qr, kr, proj3, attn_o, dan.reshape(B, T, ATT_WIDTH), tables,
                                            attn_sinks, attn_out_w, [gw_out], ["a2a"])
    flat = lambda a: a.reshape(N, a.shape[-1])
    grad_x, dproj, acc_in = _in_bwd(x2, dx1, flat(dqkv), flat(dhq), flat(dhf), flat(dhi), flat(dhg),
                                    mod8, pre_w_mix, w_in_f, T, [], [])

    packed, dmod_blocks = _pack_small(acc_in, acc_mix, acc_mlp, dsink_p, daw_p, dlb_p, dgw_p, lb_p, ada_cols)
    r_in, r_dmod, r_small = _grad_w_in_reduced(dproj, h1, [dmod_blocks, packed], ["a2a", "gather"])

    res = {}
    res["w_in"] = tuple(a.T for a in _reduce_adamw("adamw_w_in", r_in, w_in_t, m_w_in_t, v_w_in_t))
    res["w_out"] = _reduce_adamw("adamw_w_out", r_out, w_out[0], m_w_out[0], v_w_out[0])
    res["w_up"] = _reduce_adamw("adamw_w_up", r_up, w_up[0], m_w_up[0], v_w_up[0])
    res["w_down"] = _reduce_adamw("adamw_w_down", r_down, w_down[0], m_w_down[0], v_w_down[0])
    res["w_ada"] = _ada_grad_adamw(c_all, r_dmod.reshape(N_DEV * B, ada_cols), w_ada[0], m_w_ada[0], v_w_ada[0])

    given = dict(b_ada=(b_ada, m_b_ada, v_b_ada), pre_w_mix=(pre_w_mix, m_pre_w_mix, v_pre_w_mix),
                 attn_sinks=(attn_sinks, m_attn_sinks, v_attn_sinks),
                 attn_out_w=(attn_out_w, m_attn_out_w, v_attn_out_w), lb_table=(lb_table, m_lb_table, v_lb_table),
                 hg_norm_w=(hg_norm_w, m_hg_norm_w, v_hg_norm_w), post_w_mix=(post_w_mix, m_post_w_mix, v_post_w_mix),
                 pre_w_mlp=(pre_w_mlp, m_pre_w_mlp, v_pre_w_mlp), post_w_mlp=(post_w_mlp, m_post_w_mlp, v_post_w_mlp))
    small_res, loss = _adamw_small(r_small, given)
    res.update(small_res)

    order = ["w_ada", "b_ada", "pre_w_mix", "w_in", "attn_sinks", "attn_out_w", "lb_table", "hg_norm_w", "w_out",
             "post_w_mix", "pre_w_mlp", "w_up", "w_down", "post_w_mlp"]
    big = {"w_ada", "w_in", "w_out", "w_up", "w_down"}
    outs = [loss, grad_x.reshape(B, T, D_MODEL)]
    for i in range(4):
        for k in order:
            a = res[k][i]
            outs.append(a[None] if k in big else a)
    return tuple(outs)
```

```python
import jax
import jax.numpy as jnp
import numpy as np
from jax import lax
from jax.experimental import pallas as pl
from jax.experimental.pallas import tpu as pltpu

F32 = jnp.float32
BF16 = jnp.bfloat16
SDS = jax.ShapeDtypeStruct

D_MODEL = 1024
ATT_WIDTH = 512
ATT_HEAD_DIM = 64
ATT_KV_HEADS = 2
ATT_GROUP = 4
WINDOW = 128
ROPE_DIM = 16
ROPE_THETA = 500000.0
HG_WIDTH = 512
HG_HEAD_DIM = 128
HG_HEADS = 4
HG_CHUNK = 32
IN_COLS = 2816
ATT_COLS = 768
D_FF = 4096
EPS = 1e-6
N_DEV = 8

ADAM_LR = 0.001
ADAM_B1 = 0.9
ADAM_B2 = 0.999
ADAM_EPS = 1e-08
ADAM_WD = 0.01
ADAM_STEP = 10

VMEM_LIMIT_BIG = 56 << 20
LANES = 128

MESH = pl.DeviceIdType.MESH
NT_DIMS = (((1,), (1,)), ((), ()))
TN_DIMS = (((0,), (0,)), ((), ()))


def _dot(a, b):
    return jnp.dot(a, b, preferred_element_type=F32)


def _dot_nt(a, b):
    return lax.dot_general(a, b, NT_DIMS, preferred_element_type=F32)


def _dot_tn(a, b):
    return lax.dot_general(a, b, TN_DIMS, preferred_element_type=F32)


def _bf(a):
    return a.astype(BF16)


def _sigmoid(a):
    return 0.5 * jnp.tanh(0.5 * a) + 0.5


def _mean_last(a):
    return jnp.mean(a, axis=-1, keepdims=True)


def _sum_rows(a):
    return jnp.sum(a, axis=0, keepdims=True)


def _loop_pairs(first, count, body, init, per_trip=2):
    if count % per_trip:
        return lax.fori_loop(first, first + count, body, init)

    def trip(i, c):
        for k in range(per_trip):
            c = body(first + per_trip * i + k, c)
        return c

    return lax.fori_loop(0, count // per_trip, trip, init)


def _params(sem=None, vmem=None):
    kw = {}
    if sem is not None:
        kw["dimension_semantics"] = sem
    if vmem is not None:
        kw["vmem_limit_bytes"] = vmem
    return pltpu.CompilerParams(**kw)


ANY_SPEC = pl.BlockSpec(memory_space=pl.ANY)


def _exchange_shapes(srcs, modes):
    out_shape = []
    for s, m in zip(srcs, modes):
        shp = {"gather": (N_DEV,) + tuple(s.shape), "pair": (s.shape[0],) + tuple(s.shape[2:])}.get(m, tuple(s.shape))
        out_shape.append(SDS(shp, s.dtype))
    return out_shape


def _exchange_sems(n):
    if n == 0:
        return []
    return [pltpu.SemaphoreType.DMA((n, N_DEV - 1)), pltpu.SemaphoreType.DMA((n, N_DEV - 1)),
            pltpu.SemaphoreType.DMA((n,))]


SIBLING = 1
OTHER_CHIPS = (2, 4, 6)


def _related(k):
    x, y, c = lax.axis_index("x"), lax.axis_index("y"), lax.axis_index("c")
    px, py, pc = x ^ ((k >> 2) & 1), y ^ ((k >> 1) & 1), c ^ (k & 1)
    return (px, py, pc), 4 * px + 2 * py + pc


def _exchange_phases(modes, src_refs, out_refs, send_sems, recv_sems, own_sems):
    _, me = _related(0)
    sib_dev, sib = _related(SIBLING)
    start, middle, end = [], [], []

    def remote(a, i, src, dst, dev):
        return pltpu.make_async_remote_copy(src_ref=src, dst_ref=dst, send_sem=send_sems.at[a, i],
                                            recv_sem=recv_sems.at[a, i], device_id=dev, device_id_type=MESH)

    for a, mode in enumerate(modes):
        out = out_refs[a]
        if mode == "gather":
            src = src_refs[a]
            own = pltpu.make_async_copy(src, out.at[me], own_sems.at[a])
            to_sib = remote(a, 0, src, out.at[me], sib_dev)
            start += [own.start, to_sib.start]
            end += [remote(a, 0, src, out.at[sib], sib_dev).wait_recv, to_sib.wait_send, own.wait]
            for j, k in enumerate(OTHER_CHIPS, start=1):
                dev, peer = _related(k)
                _, peer_sib = _related(k ^ SIBLING)
                send = remote(a, j, src, out.at[me], dev)
                passed = remote(a, 3 + j, out.at[peer], out.at[peer], sib_dev)
                start.append(send.start)
                middle += [remote(a, j, src, out.at[peer], dev).wait_recv, passed.start]
                end += [remote(a, 3 + j, out.at[peer_sib], out.at[peer_sib], sib_dev).wait_recv,
                        send.wait_send, passed.wait_send]
        elif mode == "pair":
            core = lax.axis_index("c")
            for s in range(N_DEV // 2):
                send = remote(a, s, src_refs[a].at[s, 1 - core], out.at[s], sib_dev)
                start.append(send.start)
                end += [remote(a, s, src_refs[a].at[s, 1 - core], out.at[s], sib_dev).wait_recv, send.wait_send]
        elif mode == "chips":
            chip = me // 2
            own = pltpu.make_async_copy(src_refs[a].at[chip], out.at[chip], own_sems.at[a])
            start.append(own.start)
            end.append(own.wait)
            for j, k in enumerate(OTHER_CHIPS, start=1):
                dev, peer = _related(k)
                send = remote(a, j, src_refs[a].at[peer // 2], out.at[chip], dev)
                start.append(send.start)
                end += [remote(a, j, src_refs[a].at[peer // 2], out.at[peer // 2], dev).wait_recv, send.wait_send]
        else:
            own = pltpu.make_async_copy(src_refs[a].at[me], out.at[me], own_sems.at[a])
            start.append(own.start)
            end.append(own.wait)
            for k in range(1, N_DEV):
                dev, peer = _related(k)
                send = remote(a, k - 1, src_refs[a].at[peer], out.at[me], dev)
                start.append(send.start)
                end += [remote(a, k - 1, src_refs[a].at[peer], out.at[peer], dev).wait_recv, send.wait_send]
    return start, middle, end


def _run(actions):
    for act in actions:
        act()


def _exchange(name, srcs, modes):
    n = len(srcs)

    def body(*refs):
        start, middle, end = _exchange_phases(modes, refs[:n], refs[n:2 * n], *refs[2 * n:])
        _run(start)
        _run(middle)
        _run(end)

    return pl.pallas_call(
        body, name=name, out_shape=_exchange_shapes(srcs, modes),
        in_specs=[ANY_SPEC] * n, out_specs=[ANY_SPEC] * n,
        scratch_shapes=_exchange_sems(n),
    )(*srcs)


def _ride_start(modes, step, steps, src_refs, out_refs, sems):
    if not modes:
        return
    middle_step = steps - 1

    @pl.when(step == 0)
    def _():
        _run(_exchange_phases(modes, src_refs, out_refs, *sems)[0])

    if "gather" in modes:
        @pl.when(step == middle_step)
        def _():
            _run(_exchange_phases(modes, src_refs, out_refs, *sems)[1])


def _ride_wait(modes, step, steps, src_refs, out_refs, sems):
    if not modes:
        return

    @pl.when(step == steps - 1)
    def _():
        _run(_exchange_phases(modes, src_refs, out_refs, *sems)[2])


def _ada_mod(c_all, w_ada, b_ada_mine):
    nb, cols = c_all.shape[0], w_ada.shape[1]

    def body(c_ref, w_ref, b_ref, o_ref):
        cv = c_ref[...]
        ca = cv * _sigmoid(cv)
        o_ref[...] = _dot(ca, w_ref[...]) + b_ref[...]

    return pl.pallas_call(body, name="ada_mod", out_shape=SDS((nb, cols), F32))(c_all, w_ada, b_ada_mine)


def _tile_rows(T, big=False):
    return min(512 if big else 256, T)


def _mod_spec(tps):
    return pl.BlockSpec((None, 8, D_MODEL), lambda i: (i // tps, 0, 0))


def _in_proj(x2, mod8, pre_w, w_in_bf, T, ride_srcs, ride_modes):
    N = x2.shape[0]
    TM = _tile_rows(T, big=True)
    tps = T // TM
    nr = len(ride_srcs)

    def body(*refs):
        x_ref, mod_ref, pw_ref, w_ref = refs[:4]
        ride_in = refs[4:4 + nr]
        pa_ref, ph_ref, h1_ref = refs[4 + nr:7 + nr]
        ride_out = refs[7 + nr:7 + 2 * nr]
        sems = refs[7 + 2 * nr:]
        _ride_start(ride_modes, pl.program_id(0), N // TM, ride_in, ride_out, sems)
        x = x_ref[...]
        r = lax.rsqrt(_mean_last(x * x) + EPS)
        h = (x * r * pw_ref[...]) * (1.0 + mod_ref[1:2, :]) + mod_ref[0:1, :]
        hb = _bf(h)
        h1_ref[...] = hb
        pa_ref[...] = _dot_nt(hb, w_ref[:ATT_COLS, :])
        ph_ref[...] = _dot_nt(hb, w_ref[ATT_COLS:, :])
        _ride_wait(ride_modes, pl.program_id(0), N // TM, ride_in, ride_out, sems)

    return pl.pallas_call(
        body, name="in_proj", grid=(N // TM,),
        in_specs=[pl.BlockSpec((TM, D_MODEL), lambda i: (i, 0)), _mod_spec(tps),
                  pl.BlockSpec((1, D_MODEL), lambda i: (0, 0)),
                  pl.BlockSpec((IN_COLS, D_MODEL), lambda i: (0, 0))] + [ANY_SPEC] * nr,
        out_specs=[pl.BlockSpec((TM, ATT_COLS), lambda i: (i, 0)),
                   pl.BlockSpec((TM, IN_COLS - ATT_COLS), lambda i: (i, 0)),
                   pl.BlockSpec((TM, D_MODEL), lambda i: (i, 0))] + [ANY_SPEC] * nr,
        out_shape=[SDS((N, ATT_COLS), F32), SDS((N, IN_COLS - ATT_COLS), F32), SDS((N, D_MODEL), BF16)]
        + _exchange_shapes(ride_srcs, ride_modes),
        scratch_shapes=_exchange_sems(nr),
        compiler_params=_params(("arbitrary",), VMEM_LIMIT_BIG),
    )(x2, mod8, pre_w, w_in_bf, *ride_srcs)


def _rope_tables(T):
    half = ROPE_DIM // 2
    f32 = np.float32
    inv_freq = (f32(ROPE_THETA) ** (-np.arange(0, ROPE_DIM, 2, dtype=f32) / f32(ROPE_DIM))).astype(f32)
    ang = np.arange(T, dtype=f32)[:, None] * inv_freq[None, :]
    cos, sin = np.cos(ang).astype(f32), np.sin(ang).astype(f32)
    ones = np.ones((T, ATT_HEAD_DIM - ROPE_DIM), f32)
    zeros = np.zeros((T, ATT_HEAD_DIM - ROPE_DIM), f32)
    zh = np.zeros((T, half), f32)
    cos64 = np.concatenate([cos, cos, ones], axis=1)
    sin_left = np.concatenate([-sin, zh, zeros], axis=1)
    sin_right = np.concatenate([zh, sin, zeros], axis=1)
    rep = LANES // ATT_HEAD_DIM
    return tuple(jnp.asarray(np.tile(t, (1, rep))) for t in (cos64, sin_left, sin_right))


def _rope(xc, cs, sl, sr):
    return xc * cs + pltpu.roll(xc, LANES - 8, 1) * sl + pltpu.roll(xc, 8, 1) * sr


def _rope_t(dy, cs, sl, sr):
    return dy * cs + pltpu.roll(dy * sl, 8, 1) + pltpu.roll(dy * sr, LANES - 8, 1)


ATT_SCALE = ATT_HEAD_DIM ** -0.5
ATT_SPLITS = 4


def _lower_mask():
    j = lax.broadcasted_iota(jnp.int32, (WINDOW, ATT_GROUP * WINDOW), 0)
    i = lax.broadcasted_iota(jnp.int32, (WINDOW, ATT_GROUP * WINDOW), 1) & (WINDOW - 1)
    return j <= i


def _sink_row(sink_ref, hk):
    return jnp.concatenate(
        [jnp.full((1, WINDOW), sink_ref[0, ATT_GROUP * hk + g], F32) for g in range(ATT_GROUP)], axis=1)


def _softmax_window(qs, k_cur, k_prev, lower, has_prev, sink):
    s_prev = jnp.where(has_prev, _dot_nt(k_prev, qs), jnp.finfo(F32).min)
    s = jnp.where(lower, _dot_nt(k_cur, qs), s_prev)
    m = jnp.maximum(jnp.max(s, axis=0, keepdims=True), sink)
    p = jnp.exp(s - m)
    es = jnp.exp(sink - m)
    inv = 1.0 / (jnp.sum(p, axis=0, keepdims=True) + es)
    return p, inv, es


def _stack_heads(parts, hk):
    hs = []
    for g in range(ATT_GROUP):
        h = ATT_GROUP * hk + g
        hs.append(parts[h // 2][:, (h % 2) * ATT_HEAD_DIM:(h % 2 + 1) * ATT_HEAD_DIM])
    return jnp.concatenate(hs, axis=0)


def _attn_fwd(proj3, tables, sinks, attn_w, ride_srcs, ride_modes):
    B, T, _ = proj3.shape
    nb = T // WINDOW
    splits = min(ATT_SPLITS, nb)
    per = nb // splits
    nr = len(ride_srcs)
    cos, sinl, sinr = tables

    def body(*refs):
        q_ref, k_ref, v_ref, cos_ref, sl_ref, sr_ref, sink_ref, aw_ref = refs[:8]
        ride_in = refs[8:8 + nr]
        o_ref, an_ref, qr_ref, kr_ref = refs[8 + nr:12 + nr]
        ride_out = refs[12 + nr:12 + 2 * nr]
        kpad, vpad = refs[12 + 2 * nr:14 + 2 * nr]
        sems = refs[14 + 2 * nr:]
        part = pl.program_id(1)
        step = pl.program_id(0) * splits + part
        _ride_start(ride_modes, step, B * splits, ride_in, ride_out, sems)

        @pl.when(part == 0)
        def _():
            kpad[0:WINDOW, :] = jnp.zeros((WINDOW, LANES), BF16)
            vpad[0:WINDOW, :] = jnp.zeros((WINDOW, LANES), BF16)

        lower = _lower_mask()

        def block(n, carry):
            r0 = pl.multiple_of(n * WINDOW, WINDOW)
            rows = pl.ds(r0, WINDOW)
            nxt = pl.ds(r0 + WINDOW, WINDOW)
            cs, sl, sr = cos_ref[rows, :], sl_ref[rows, :], sr_ref[rows, :]
            kb = _bf(_rope(k_ref[rows, :], cs, sl, sr))
            vb = _bf(v_ref[rows, :])
            kpad[nxt, :] = kb
            kr_ref[rows, :] = kb
            vpad[nxt, :] = vb
            qparts = []
            for j in range(ATT_WIDTH // LANES):
                qp = _bf(_rope(q_ref[rows, j * LANES:(j + 1) * LANES], cs, sl, sr) * ATT_SCALE)
                qr_ref[rows, j * LANES:(j + 1) * LANES] = qp
                qparts.append(qp)
            for hk in range(ATT_KV_HEADS):
                lanes = slice(hk * ATT_HEAD_DIM, (hk + 1) * ATT_HEAD_DIM)
                qs = _stack_heads(qparts, hk)
                p, inv, _ = _softmax_window(qs, kb[:, lanes], kpad[rows, lanes], lower, n > 0,
                                            _sink_row(sink_ref, hk))
                p_cur = jnp.where(lower, p, 0.0)
                ot = (_dot_tn(vb[:, lanes], _bf(p_cur)) + _dot_tn(vpad[rows, lanes], _bf(p - p_cur))) * inv
                for g in range(ATT_GROUP):
                    h = ATT_GROUP * hk + g
                    o_ref[rows, h * ATT_HEAD_DIM:(h + 1) * ATT_HEAD_DIM] = ot[:, g * WINDOW:(g + 1) * WINDOW].T
            ob = o_ref[rows, :]
            an_ref[rows, :] = _bf(ob * lax.rsqrt(_mean_last(ob * ob) + EPS) * aw_ref[...])
            return carry

        _loop_pairs(part * per, per, block, 0)
        _ride_wait(ride_modes, step, B * splits, ride_in, ride_out, sems)

    seq = lambda w, j: pl.BlockSpec((None, T, w), lambda b, s: (b, 0, j))
    full = lambda r, w: pl.BlockSpec((r, w), lambda b, s: (0, 0))
    return pl.pallas_call(
        body, name="attn_fwd", grid=(B, splits),
        in_specs=[seq(ATT_WIDTH, 0), seq(LANES, 4), seq(LANES, 5),
                  full(T, LANES), full(T, LANES), full(T, LANES),
                  pl.BlockSpec(memory_space=pltpu.SMEM), full(1, ATT_WIDTH)] + [ANY_SPEC] * nr,
        out_specs=[seq(ATT_WIDTH, 0), seq(ATT_WIDTH, 0), seq(ATT_WIDTH, 0), seq(LANES, 0)] + [ANY_SPEC] * nr,
        out_shape=[SDS((B, T, ATT_WIDTH), F32), SDS((B, T, ATT_WIDTH), BF16),
                   SDS((B, T, ATT_WIDTH), BF16), SDS((B, T, LANES), BF16)] + _exchange_shapes(ride_srcs, ride_modes),
        scratch_shapes=[pltpu.VMEM((T + WINDOW, LANES), BF16), pltpu.VMEM((T + WINDOW, LANES), BF16)]
        + _exchange_sems(nr),
        compiler_params=_params(("arbitrary", "arbitrary"), VMEM_LIMIT_BIG),
    )(proj3, proj3, proj3, cos, sinl, sinr, sinks, attn_w, *ride_srcs)


HG_GROUP = 8
HG_ROWS = HG_GROUP * HG_CHUNK


HG_STACK = HG_GROUP * HG_HEAD_DIM


def _group_mask():
    r = lax.broadcasted_iota(jnp.int32, (HG_ROWS, HG_ROWS), 0)
    c = lax.broadcasted_iota(jnp.int32, (HG_ROWS, HG_ROWS), 1)
    return ((r // HG_CHUNK) == (c // HG_CHUNK)) & (r >= c)


def _spread(a):
    blocks = []
    for c in range(HG_GROUP):
        above = jnp.zeros((c * HG_CHUNK, HG_HEAD_DIM), a.dtype)
        below = jnp.zeros(((HG_GROUP - 1 - c) * HG_CHUNK, HG_HEAD_DIM), a.dtype)
        blocks.append(jnp.concatenate([p for p in (above, a[_chunk_rows(c), :], below) if p.shape[0]], axis=0))
    return jnp.concatenate(blocks, axis=1)


def _pick(r):
    return jnp.concatenate([r[_chunk_rows(c), c * HG_HEAD_DIM:(c + 1) * HG_HEAD_DIM] for c in range(HG_GROUP)], axis=0)


def _lane_block(a, c):
    return a[:, c * HG_HEAD_DIM:(c + 1) * HG_HEAD_DIM]


def _chunk_cumsum(a, reverse=False):
    n = a.shape[0]
    pos = lax.broadcasted_iota(jnp.int32, a.shape, 0) % HG_CHUNK
    shift = 1
    while shift < HG_CHUNK:
        if reverse:
            a = a + jnp.where(pos < HG_CHUNK - shift, pltpu.roll(a, n - shift, 0), 0.0)
        else:
            a = a + jnp.where(pos >= shift, pltpu.roll(a, shift, 0), 0.0)
        shift *= 2
    return a


def _chunk_bcast(rows_1x128):
    return jnp.concatenate([jnp.broadcast_to(r, (HG_CHUNK, HG_HEAD_DIM)) for r in rows_1x128], axis=0)


def _hgrn_gates(hq, hf, lb):
    sq = _sigmoid(hq)
    q = hq * sq
    sg = _sigmoid(hf)
    f = lb + (1.0 - lb) * sg
    k = 1.0 - f
    logf = jnp.log(f)
    b = _chunk_cumsum(logf)
    bl = [_sum_rows(logf[_chunk_rows(c), :]) for c in range(HG_GROUP)]
    eb, enb, e2 = jnp.exp(b), jnp.exp(-b), jnp.exp(_chunk_bcast(bl) - b)
    ebl = [jnp.exp(r) for r in bl]
    return dict(sq=sq, sg=sg, f=f, eb=eb, enb=enb, e2=e2, ebl=ebl, qd=q * eb, kd=k * enb, k2=k * e2)


def _chunk_rows(c):
    return slice(c * HG_CHUNK, (c + 1) * HG_CHUNK)


def _head_lanes(h):
    return slice(h * HG_HEAD_DIM, (h + 1) * HG_HEAD_DIM)


def _hgrn_fwd(proj_h, lb, hg_w, ride_srcs, ride_modes):
    B, T, _ = proj_h.shape
    ng = T // HG_ROWS
    nr = len(ride_srcs)

    def body(*refs):
        hq_ref, hf_ref, hi_ref, hg_ref, lb_ref, gw_ref = refs[:6]
        ride_in = refs[6:6 + nr]
        o_ref, rg_ref, sp_ref = refs[6 + nr:9 + nr]
        ride_out = refs[9 + nr:9 + 2 * nr]
        st = refs[9 + 2 * nr]
        sems = refs[10 + 2 * nr:]
        gi = pl.program_id(1)
        step = pl.program_id(0) * ng + gi
        _ride_start(ride_modes, step, B * ng, ride_in, ride_out, sems)

        @pl.when(gi == 0)
        def _():
            st[...] = jnp.zeros(st.shape, F32)

        lo = _group_mask()
        for h in range(HG_HEADS):
            lanes = _head_lanes(h)
            gt = _hgrn_gates(hq_ref[:, lanes], hf_ref[:, lanes], lb_ref[:, lanes])
            v, qd, kd = _bf(hi_ref[:, lanes]), _bf(gt["qd"]), _bf(gt["kd"])
            a = jnp.where(lo, _dot_nt(qd, kd), 0.0)
            kv = _dot_tn(v, _spread(_bf(gt["k2"])))
            s = st[h]
            before = []
            for c in range(HG_GROUP):
                before.append(s)
                s = s * gt["ebl"][c] + _lane_block(kv, c)
            st[h] = s
            sp = jnp.concatenate(before, axis=1)
            sp_ref[h] = sp
            o = _dot(_bf(a), v) + _dot_nt(_spread(qd), _bf(sp))
            o_ref[:, lanes] = o
            hg = hg_ref[:, lanes]
            rn = o * lax.rsqrt(_mean_last(o * o) + EPS) * gw_ref[...]
            rg_ref[:, lanes] = _bf(rn * (hg * _sigmoid(hg)))
        _ride_wait(ride_modes, step, B * ng, ride_in, ride_out, sems)

    part = lambda j: pl.BlockSpec((None, HG_ROWS, HG_WIDTH), lambda b, g: (b, g, j))
    return pl.pallas_call(
        body, name="hgrn_fwd", grid=(B, ng),
        in_specs=[part(0), part(1), part(2), part(3),
                  pl.BlockSpec((1, HG_WIDTH), lambda b, g: (0, 0)),
                  pl.BlockSpec((1, LANES), lambda b, g: (0, 0))] + [ANY_SPEC] * nr,
        out_specs=[part(0), part(0),
                   pl.BlockSpec((None, HG_HEADS, None, HG_HEAD_DIM, HG_STACK), lambda b, g: (b, 0, g, 0, 0))]
        + [ANY_SPEC] * nr,
        out_shape=[SDS((B, T, HG_WIDTH), F32), SDS((B, T, HG_WIDTH), BF16),
                   SDS((B, HG_HEADS, ng, HG_HEAD_DIM, HG_STACK), F32)] + _exchange_shapes(ride_srcs, ride_modes),
        scratch_shapes=[pltpu.VMEM((HG_HEADS, HG_HEAD_DIM, HG_HEAD_DIM), F32)] + _exchange_sems(nr),
        compiler_params=_params(("arbitrary", "arbitrary"), VMEM_LIMIT_BIG),
    )(proj_h, proj_h, proj_h, proj_h, lb, hg_w, *ride_srcs)


def _mix_out(x2, attn_n, rec_g, mod8, post_w, w_out_bf, T, ride_srcs, ride_modes):
    N = x2.shape[0]
    TM = _tile_rows(T, big=True)
    tps = T // TM
    nr = len(ride_srcs)

    def body(*refs):
        x_ref, an_ref, rg_ref, mod_ref, pw_ref, w_ref = refs[:6]
        ride_in = refs[6:6 + nr]
        mix_ref, x1_ref, cat_ref = refs[6 + nr:9 + nr]
        ride_out = refs[9 + nr:9 + 2 * nr]
        sems = refs[9 + 2 * nr:]
        _ride_start(ride_modes, pl.program_id(0), N // TM, ride_in, ride_out, sems)
        cat = jnp.concatenate([an_ref[...], rg_ref[...]], axis=1)
        cat_ref[...] = cat
        mix = _dot(cat, w_ref[...])
        mix_ref[...] = mix
        r = lax.rsqrt(_mean_last(mix * mix) + EPS)
        x1_ref[...] = x_ref[...] + mod_ref[2:3, :] * (mix * r * pw_ref[...])
        _ride_wait(ride_modes, pl.program_id(0), N // TM, ride_in, ride_out, sems)

    row = lambda w: pl.BlockSpec((TM, w), lambda i: (i, 0))
    return pl.pallas_call(
        body, name="mix_out", grid=(N // TM,),
        in_specs=[row(D_MODEL), row(ATT_WIDTH), row(HG_WIDTH), _mod_spec(tps),
                  pl.BlockSpec((1, D_MODEL), lambda i: (0, 0)),
                  pl.BlockSpec((D_MODEL, D_MODEL), lambda i: (0, 0))] + [ANY_SPEC] * nr,
        out_specs=[row(D_MODEL), row(D_MODEL), row(D_MODEL)] + [ANY_SPEC] * nr,
        out_shape=[SDS((N, D_MODEL), F32), SDS((N, D_MODEL), F32), SDS((N, D_MODEL), BF16)]
        + _exchange_shapes(ride_srcs, ride_modes),
        scratch_shapes=_exchange_sems(nr),
        compiler_params=_params(("arbitrary",), VMEM_LIMIT_BIG),
    )(x2, attn_n, rec_g, mod8, post_w, w_out_bf, *ride_srcs)


def _load_weights_once(pairs, sem):
    @pl.when(pl.program_id(0) == 0)
    def _():
        cps = [pltpu.make_async_copy(src, dst, sem.at[i]) for i, (src, dst) in enumerate(pairs)]
        for cp in cps:
            cp.start()
        for cp in cps:
            cp.wait()


MLP_HALF = D_MODEL // 2
MLP_PIECES = 2 * N_DEV + 2


def _mlp_weight_pieces(wu_a, wu_b, wd_a, wd_b, wu, wd):
    cols = D_FF // N_DEV
    pairs = []
    for h, half in enumerate((wu_a, wu_b)):
        for j in range(N_DEV):
            pairs.append((half.at[j], wu.at[pl.ds(h * MLP_HALF, MLP_HALF), pl.ds(j * cols, cols)]))
    for h, half in enumerate((wd_a, wd_b)):
        pairs.append((half, wd.at[:, pl.ds(h * MLP_HALF, MLP_HALF)]))
    return pairs


def _mlp_fwd(x1, mod8, pre_w, w_up_halves, w_down_halves, T):
    N = x1.shape[0]
    TM = _tile_rows(T)
    tps = T // TM

    def body(x_ref, mod_ref, pw_ref, wua, wub, wda, wdb, up_ref, u_ref, d_ref, h2_ref, wu, wd, sem):
        _load_weights_once(_mlp_weight_pieces(wua, wub, wda, wdb, wu, wd), sem)
        x = x_ref[...]
        r = lax.rsqrt(_mean_last(x * x) + EPS)
        h = (x * r * pw_ref[...]) * (1.0 + mod_ref[4:5, :]) + mod_ref[3:4, :]
        hb = _bf(h)
        h2_ref[...] = hb
        up = _dot(hb, wu[...])
        up_ref[...] = up
        ru = jnp.maximum(up, 0.0)
        u = _bf(ru * ru)
        u_ref[...] = u
        d_ref[...] = _dot(u, wd[...])

    row = lambda w: pl.BlockSpec((TM, w), lambda i: (i, 0))
    return pl.pallas_call(
        body, name="mlp_fwd", grid=(N // TM,),
        in_specs=[row(D_MODEL), _mod_spec(tps), pl.BlockSpec((1, D_MODEL), lambda i: (0, 0))] + [ANY_SPEC] * 4,
        out_specs=[row(D_FF), row(D_FF), row(D_MODEL), row(D_MODEL)],
        out_shape=[SDS((N, D_FF), F32), SDS((N, D_FF), BF16), SDS((N, D_MODEL), F32), SDS((N, D_MODEL), BF16)],
        scratch_shapes=[pltpu.VMEM((D_MODEL, D_FF), BF16), pltpu.VMEM((D_FF, D_MODEL), BF16),
                        pltpu.SemaphoreType.DMA((MLP_PIECES,))],
        compiler_params=_params(("arbitrary",), VMEM_LIMIT_BIG),
    )(x1, mod8, pre_w, *w_up_halves, *w_down_halves)


def _acc_rows(acc_ref, first, rows):
    @pl.when(first)
    def _():
        acc_ref[...] = jnp.zeros(acc_ref.shape, F32)
    for i, r in enumerate(rows):
        acc_ref[i:i + 1, :] += r


def _mlp_bwd(x1, d, up, tgt, mod8, pre_w, post_w, w_up_halves, w_down_halves, T):
    N = x1.shape[0]
    TM = _tile_rows(T)
    tps = T // TM

    def body(x_ref, d_ref, up_ref, t_ref, mod_ref, pw_ref, qw_ref, wua, wub, wda, wdb,
             dx_ref, dup_ref, dd_ref, acc_ref, wd, wu, sem):
        _load_weights_once(_mlp_weight_pieces(wua, wub, wda, wdb, wu, wd), sem)
        sh2, sc2, g2 = mod_ref[3:4, :], mod_ref[4:5, :], mod_ref[5:6, :]
        x = x_ref[...]
        r1 = lax.rsqrt(_mean_last(x * x) + EPS)
        xh = x * r1
        n2 = xh * pw_ref[...]
        dv = d_ref[...]
        rd = lax.rsqrt(_mean_last(dv * dv) + EPS)
        dh = dv * rd
        rr = dh * qw_ref[...]
        e = x + g2 * rr - t_ref[...]
        loss = 0.5 * jnp.sum(_sum_rows(e * e), axis=1, keepdims=True) / D_MODEL
        dy = e * (1.0 / D_MODEL)
        dg2 = _sum_rows(dy * rr)
        drr = dy * g2
        dw_post = _sum_rows(drr * dh)
        ddh = drr * qw_ref[...]
        dd = _bf(rd * (ddh - dh * _mean_last(ddh * dh)))
        dd_ref[...] = dd
        ru = jnp.maximum(up_ref[...], 0.0)
        dup = _bf(_dot_nt(dd, wd[...]) * (2.0 * ru))
        dup_ref[...] = dup
        dh2 = _dot_nt(dup, wu[...])
        dsh2 = _sum_rows(dh2)
        dsc2 = _sum_rows(dh2 * n2)
        dn2 = dh2 * (1.0 + sc2)
        dw_pre = _sum_rows(dn2 * xh)
        dxh = dn2 * pw_ref[...]
        dx_ref[...] = dy + r1 * (dxh - xh * _mean_last(dxh * xh))
        _acc_rows(acc_ref, pl.program_id(0) % tps == 0,
                  [dsh2, dsc2, dg2, dw_pre, dw_post, jnp.broadcast_to(loss, (1, D_MODEL))])

    row = lambda w: pl.BlockSpec((TM, w), lambda i: (i, 0))
    vec = pl.BlockSpec((1, D_MODEL), lambda i: (0, 0))
    B = N // T
    return pl.pallas_call(
        body, name="mlp_bwd", grid=(N // TM,),
        in_specs=[row(D_MODEL), row(D_MODEL), row(D_FF), row(D_MODEL), _mod_spec(tps), vec, vec] + [ANY_SPEC] * 4,
        out_specs=[row(D_MODEL), row(D_FF), row(D_MODEL), _mod_spec(tps)],
        out_shape=[SDS((N, D_MODEL), F32), SDS((N, D_FF), BF16), SDS((N, D_MODEL), BF16),
                   SDS((B, 8, D_MODEL), F32)],
        scratch_shapes=[pltpu.VMEM((D_FF, D_MODEL), BF16), pltpu.VMEM((D_MODEL, D_FF), BF16),
                        pltpu.SemaphoreType.DMA((MLP_PIECES,))],
        compiler_params=_params(("arbitrary",), VMEM_LIMIT_BIG),
    )(x1, d, up, tgt, mod8, pre_w, post_w, *w_up_halves, *w_down_halves)


def _mix_bwd(mix, dx1, mod8, post_w, w_out_bf, T, ride_srcs, ride_modes):
    N = mix.shape[0]
    TM = _tile_rows(T, big=True)
    tps = T // TM
    nr = len(ride_srcs)

    def body(*refs):
        mix_ref, dx_ref, mod_ref, pw_ref, w_ref = refs[:5]
        ride_in = refs[5:5 + nr]
        dan_ref, drg_ref, dmix_ref, acc_ref = refs[5 + nr:9 + nr]
        ride_out = refs[9 + nr:9 + 2 * nr]
        sems = refs[9 + 2 * nr:]
        _ride_start(ride_modes, pl.program_id(0), N // TM, ride_in, ride_out, sems)
        g1 = mod_ref[2:3, :]
        mix = mix_ref[...]
        dx1 = dx_ref[...]
        rm = lax.rsqrt(_mean_last(mix * mix) + EPS)
        mh = mix * rm
        dg1 = _sum_rows(dx1 * (mh * pw_ref[...]))
        dr = dx1 * g1
        dw_post = _sum_rows(dr * mh)
        dmh = dr * pw_ref[...]
        dmix = _bf(rm * (dmh - mh * _mean_last(dmh * mh)))
        dmix_ref[...] = dmix
        dcat = _dot_nt(dmix, w_ref[...])
        dan_ref[...] = dcat[:, :ATT_WIDTH]
        drg_ref[...] = dcat[:, ATT_WIDTH:]
        _acc_rows(acc_ref, pl.program_id(0) % tps == 0, [dg1, dw_post])
        _ride_wait(ride_modes, pl.program_id(0), N // TM, ride_in, ride_out, sems)

    row = lambda w: pl.BlockSpec((TM, w), lambda i: (i, 0))
    B = N // T
    return pl.pallas_call(
        body, name="mix_bwd", grid=(N // TM,),
        in_specs=[row(D_MODEL), row(D_MODEL), _mod_spec(tps), pl.BlockSpec((1, D_MODEL), lambda i: (0, 0)),
                  pl.BlockSpec((D_MODEL, D_MODEL), lambda i: (0, 0))] + [ANY_SPEC] * nr,
        out_specs=[row(ATT_WIDTH), row(HG_WIDTH), row(D_MODEL), _mod_spec(tps)] + [ANY_SPEC] * nr,
        out_shape=[SDS((N, ATT_WIDTH), F32), SDS((N, HG_WIDTH), F32), SDS((N, D_MODEL), BF16),
                   SDS((B, 8, D_MODEL), F32)] + _exchange_shapes(ride_srcs, ride_modes),
        scratch_shapes=_exchange_sems(nr),
        compiler_params=_params(("arbitrary",), VMEM_LIMIT_BIG),
    )(mix, dx1, mod8, post_w, w_out_bf, *ride_srcs)


def _hgrn_bwd(proj_h, lb, hg_w, o, s_prev, drg, ride_srcs, ride_modes):
    B, T, _ = proj_h.shape
    ng = T // HG_ROWS
    nr = len(ride_srcs)

    def body(*refs):
        hq_ref, hf_ref, hi_ref, hg_ref, lb_ref, gw_ref, o_ref, sp_ref, drg_ref = refs[:9]
        ride_in = refs[9:9 + nr]
        dhq_ref, dhf_ref, dhi_ref, dhg_ref, dlb_ref, dgw_ref = refs[9 + nr:15 + nr]
        ride_out = refs[15 + nr:15 + 2 * nr]
        dst = refs[15 + 2 * nr]
        sems = refs[16 + 2 * nr:]
        step = pl.program_id(0) * ng + pl.program_id(1)
        _ride_start(ride_modes, step, B * ng, ride_in, ride_out, sems)

        @pl.when(pl.program_id(1) == 0)
        def _():
            dst[...] = jnp.zeros(dst.shape, F32)
            dlb_ref[...] = jnp.zeros(dlb_ref.shape, F32)
            dgw_ref[...] = jnp.zeros(dgw_ref.shape, F32)

        lo = _group_mask()
        gw = gw_ref[...]

        for h in range(HG_HEADS):
            lanes = _head_lanes(h)
            lbv = lb_ref[:, lanes]
            hq = hq_ref[:, lanes]
            gt = _hgrn_gates(hq, hf_ref[:, lanes], lbv)
            sq, sg, qdf, kdf, k2f, ebl = gt["sq"], gt["sg"], gt["qd"], gt["kd"], gt["k2"], gt["ebl"]
            v, qd, kd = _bf(hi_ref[:, lanes]), _bf(qdf), _bf(kdf)
            ov = o_ref[:, lanes]
            hg = hg_ref[:, lanes]
            shg = _sigmoid(hg)
            dr = drg_ref[:, lanes]
            ro = lax.rsqrt(_mean_last(ov * ov) + EPS)
            oh = ov * ro
            dhg_ref[:, lanes] = _bf(dr * (oh * gw) * (shg + hg * shg * (1.0 - shg)))
            drn = dr * (hg * shg)
            dgw_ref[...] += jnp.broadcast_to(_sum_rows(drn * oh), (8, LANES))
            doh = drn * gw
            do = _bf(ro * (doh - oh * _mean_last(doh * oh)))
            a = jnp.where(lo, _dot_nt(qd, kd), 0.0)
            da = _bf(jnp.where(lo, _dot_nt(do, v), 0.0))
            dv = _dot_tn(_bf(a), do)
            dqd = _dot(da, kd)
            dkd = _dot_tn(da, qd)
            sp = sp_ref[h]
            incr = _dot_tn(do, _spread(qd))
            ds = dst[h]
            after = [None] * HG_GROUP
            for c in reversed(range(HG_GROUP)):
                after[c] = ds
                ds = ds * ebl[c] + _lane_block(incr, c)
            dst[h] = ds
            dss = jnp.concatenate(after, axis=1)
            dssb = _bf(dss)
            dk2 = _pick(_dot(v, dssb))
            dhi_ref[:, lanes] = _bf(dv + _dot_nt(_spread(_bf(k2f)), dssb))
            dqd = dqd + _pick(_dot(do, _bf(sp)))
            debl = _sum_rows(dss * sp)
            k2g = dk2 * k2f
            db = dqd * qdf - dkd * kdf - k2g
            dk = dkd * gt["enb"] + dk2 * gt["e2"]
            dbl = _chunk_bcast([_lane_block(debl, c) * ebl[c] + _sum_rows(k2g[_chunk_rows(c), :])
                                for c in range(HG_GROUP)])
            dg = _chunk_cumsum(db, reverse=True) + dbl
            df = dg / gt["f"] - dk
            dhf_ref[:, lanes] = _bf(df * (1.0 - lbv) * sg * (1.0 - sg))
            dlb_ref[:, lanes] += jnp.broadcast_to(_sum_rows(df * (1.0 - sg)), (8, LANES))
            dhq_ref[:, lanes] = _bf((dqd * gt["eb"]) * (sq + hq * sq * (1.0 - sq)))
        _ride_wait(ride_modes, step, B * ng, ride_in, ride_out, sems)

    part = lambda j: pl.BlockSpec((None, HG_ROWS, HG_WIDTH), lambda b, g: (b, ng - 1 - g, j))
    return pl.pallas_call(
        body, name="hgrn_bwd", grid=(B, ng),
        in_specs=[part(0), part(1), part(2), part(3),
                  pl.BlockSpec((1, HG_WIDTH), lambda b, g: (0, 0)),
                  pl.BlockSpec((1, LANES), lambda b, g: (0, 0)),
                  part(0),
                  pl.BlockSpec((None, HG_HEADS, None, HG_HEAD_DIM, HG_STACK), lambda b, g: (b, 0, ng - 1 - g, 0, 0)),
                  part(0)] + [ANY_SPEC] * nr,
        out_specs=[part(0), part(0), part(0), part(0),
                   pl.BlockSpec((None, 8, HG_WIDTH), lambda b, g: (b, 0, 0)),
                   pl.BlockSpec((None, 8, LANES), lambda b, g: (b, 0, 0))] + [ANY_SPEC] * nr,
        out_shape=[SDS((B, T, HG_WIDTH), BF16)] * 4 + [SDS((B, 8, HG_WIDTH), F32), SDS((B, 8, LANES), F32)]
        + _exchange_shapes(ride_srcs, ride_modes),
        scratch_shapes=[pltpu.VMEM((HG_HEADS, HG_HEAD_DIM, HG_HEAD_DIM), F32)] + _exchange_sems(nr),
        compiler_params=_params(("arbitrary", "arbitrary"), VMEM_LIMIT_BIG),
    )(proj_h, proj_h, proj_h, proj_h, lb, hg_w, o, s_prev, drg, *ride_srcs)


def _attn_bwd(qr, kr, proj3, attn_o, dan, tables, sinks, attn_w, ride_srcs, ride_modes):
    B, T, _ = proj3.shape
    nb = T // WINDOW
    splits = min(ATT_SPLITS, nb)
    per = nb // splits
    nr = len(ride_srcs)
    cos, sinl, sinr = tables
    QKV = ATT_WIDTH + 2 * LANES

    def body(*refs):
        qr_ref, kr_ref, v_ref, o_ref, dan_ref, cos_ref, sl_ref, sr_ref, sink_ref, aw_ref = refs[:10]
        ride_in = refs[10:10 + nr]
        dqkv_ref, dsink_ref, daw_ref = refs[10 + nr:13 + nr]
        ride_out = refs[13 + nr:13 + 2 * nr]
        kpad, vpad, dkpad, dvpad, dqb, dsk = refs[13 + 2 * nr:19 + 2 * nr]
        sems = refs[19 + 2 * nr:]
        part = pl.program_id(1)
        step = pl.program_id(0) * splits + part
        _ride_start(ride_modes, step, B * splits, ride_in, ride_out, sems)

        @pl.when(part == 0)
        def _():
            kpad[0:WINDOW, :] = jnp.zeros((WINDOW, LANES), BF16)
            vpad[0:WINDOW, :] = jnp.zeros((WINDOW, LANES), BF16)
            kpad[WINDOW:, :] = kr_ref[...]
            vpad[WINDOW:, :] = _bf(v_ref[...])
            dkpad[...] = jnp.zeros(dkpad.shape, F32)
            dvpad[...] = jnp.zeros(dvpad.shape, F32)
            dsk[...] = jnp.zeros(dsk.shape, F32)
            daw_ref[...] = jnp.zeros(daw_ref.shape, F32)

        lower = _lower_mask()
        aw = aw_ref[...]

        def block(n, daw):
            r0 = pl.multiple_of(n * WINDOW, WINDOW)
            rows = pl.ds(r0, WINDOW)
            nxt = pl.ds(r0 + WINDOW, WINDOW)
            ob = o_ref[rows, :]
            dn = dan_ref[rows, :]
            ro = lax.rsqrt(_mean_last(ob * ob) + EPS)
            oh = ob * ro
            daw = daw + _sum_rows(dn * oh)
            doh = dn * aw
            do = _bf(ro * (doh - oh * _mean_last(doh * oh)))
            doparts = [do[:, j * LANES:(j + 1) * LANES] for j in range(ATT_WIDTH // LANES)]
            qparts = [qr_ref[rows, j * LANES:(j + 1) * LANES] for j in range(ATT_WIDTH // LANES)]
            for hk in range(ATT_KV_HEADS):
                lanes = slice(hk * ATT_HEAD_DIM, (hk + 1) * ATT_HEAD_DIM)
                qs = _stack_heads(qparts, hk)
                dos = _stack_heads(doparts, hk)
                k_cur, k_prev = kpad[nxt, lanes], kpad[rows, lanes]
                v_cur, v_prev = vpad[nxt, lanes], vpad[rows, lanes]
                p, inv, es = _softmax_window(qs, k_cur, k_prev, lower, n > 0, _sink_row(sink_ref, hk))
                p = p * inv
                dp = jnp.where(lower, _dot_nt(v_cur, dos), _dot_nt(v_prev, dos))
                delta = jnp.sum(p * dp, axis=0, keepdims=True)
                ds = p * (dp - delta)
                sk = (es * inv) * delta
                ds_cur = jnp.where(lower, ds, 0.0)
                p_cur = jnp.where(lower, p, 0.0)
                ds_cur, ds_prev = _bf(ds_cur), _bf(ds - ds_cur)
                p_cur, p_prev = _bf(p_cur), _bf(p - p_cur)
                dqt = (_dot_tn(k_cur, ds_cur) + _dot_tn(k_prev, ds_prev)) * ATT_SCALE
                dkpad[nxt, lanes] += _dot(ds_cur, qs)
                dkpad[rows, lanes] += _dot(ds_prev, qs)
                dvpad[nxt, lanes] += _dot(p_cur, dos)
                dvpad[rows, lanes] += _dot(p_prev, dos)
                for g in range(ATT_GROUP):
                    h = ATT_GROUP * hk + g
                    cols = slice(g * WINDOW, (g + 1) * WINDOW)
                    dqb[:, h * ATT_HEAD_DIM:(h + 1) * ATT_HEAD_DIM] = dqt[:, cols].T
                    head_lane = lax.broadcasted_iota(jnp.int32, dsk.shape, 1) == h
                    dsk[...] += jnp.where(head_lane, -jnp.sum(sk[:, cols], axis=1, keepdims=True), 0.0)
            cs, sl, sr = cos_ref[rows, :], sl_ref[rows, :], sr_ref[rows, :]
            for j in range(ATT_WIDTH // LANES):
                dqkv_ref[rows, j * LANES:(j + 1) * LANES] = _bf(_rope_t(dqb[:, j * LANES:(j + 1) * LANES], cs, sl, sr))
            return daw

        daw = _loop_pairs(part * per, per, block, jnp.zeros((1, ATT_WIDTH), F32))
        daw_ref[...] += jnp.broadcast_to(daw, (8, ATT_WIDTH))
        dsink_ref[...] = dsk[...]

        def finish(n, carry):
            r0 = pl.multiple_of(n * WINDOW, WINDOW)
            rows = pl.ds(r0, WINDOW)
            nxt = pl.ds(r0 + WINDOW, WINDOW)
            cs, sl, sr = cos_ref[rows, :], sl_ref[rows, :], sr_ref[rows, :]
            dqkv_ref[rows, ATT_WIDTH:ATT_WIDTH + LANES] = _bf(_rope_t(dkpad[nxt, :], cs, sl, sr))
            dqkv_ref[rows, ATT_WIDTH + LANES:QKV] = _bf(dvpad[nxt, :])
            return carry

        @pl.when(part == splits - 1)
        def _():
            lax.fori_loop(0, nb, finish, 0)

        _ride_wait(ride_modes, step, B * splits, ride_in, ride_out, sems)

    seq = lambda w, j: pl.BlockSpec((None, T, w), lambda b, s: (b, 0, j))
    full = lambda r, w: pl.BlockSpec((r, w), lambda b, s: (0, 0))
    return pl.pallas_call(
        body, name="attn_bwd", grid=(B, splits),
        in_specs=[seq(ATT_WIDTH, 0), seq(LANES, 0), seq(LANES, 5), seq(ATT_WIDTH, 0), seq(ATT_WIDTH, 0),
                  full(T, LANES), full(T, LANES), full(T, LANES),
                  pl.BlockSpec(memory_space=pltpu.SMEM), full(1, ATT_WIDTH)] + [ANY_SPEC] * nr,
        out_specs=[seq(QKV, 0), pl.BlockSpec((None, 8, LANES), lambda b, s: (b, 0, 0)),
                   pl.BlockSpec((None, 8, ATT_WIDTH), lambda b, s: (b, 0, 0))] + [ANY_SPEC] * nr,
        out_shape=[SDS((B, T, QKV), BF16), SDS((B, 8, LANES), F32), SDS((B, 8, ATT_WIDTH), F32)]
        + _exchange_shapes(ride_srcs, ride_modes),
        scratch_shapes=[pltpu.VMEM((T + WINDOW, LANES), BF16), pltpu.VMEM((T + WINDOW, LANES), BF16),
                        pltpu.VMEM((T + WINDOW, LANES), F32), pltpu.VMEM((T + WINDOW, LANES), F32),
                        pltpu.VMEM((WINDOW, ATT_WIDTH), F32), pltpu.VMEM((8, LANES), F32)] + _exchange_sems(nr),
        compiler_params=_params(("arbitrary", "arbitrary"), VMEM_LIMIT_BIG),
    )(qr, kr, proj3, attn_o, dan, cos, sinl, sinr, sinks, attn_w, *ride_srcs)


def _in_bwd(x2, dx1, dqkv, dhq, dhf, dhi, dhg, mod8, pre_w, w_in_bf, T, ride_srcs, ride_modes):
    N = x2.shape[0]
    TM = _tile_rows(T, big=True)
    tps = T // TM
    nr = len(ride_srcs)
    pieces = [(0, ATT_WIDTH + 2 * LANES), (768, HG_WIDTH), (1280, HG_WIDTH), (1792, HG_WIDTH), (2304, HG_WIDTH)]

    def body(*refs):
        x_ref, dx_ref, p0, p1, p2, p3, p4, mod_ref, pw_ref, w_ref = refs[:10]
        ride_in = refs[10:10 + nr]
        gx_ref, dproj_ref, acc_ref = refs[10 + nr:13 + nr]
        ride_out = refs[13 + nr:13 + 2 * nr]
        sems = refs[13 + 2 * nr:]
        _ride_start(ride_modes, pl.program_id(0), N // TM, ride_in, ride_out, sems)
        sc1 = mod_ref[1:2, :]
        dh = jnp.zeros((TM, D_MODEL), F32)
        for ref, (off, width) in zip((p0, p1, p2, p3, p4), pieces):
            pb = ref[...]
            dproj_ref[:, off:off + width] = pb
            dh = dh + _dot(pb, w_ref[off:off + width, :])
        x = x_ref[...]
        r = lax.rsqrt(_mean_last(x * x) + EPS)
        xh = x * r
        n1 = xh * pw_ref[...]
        dsh1 = _sum_rows(dh)
        dsc1 = _sum_rows(dh * n1)
        dn1 = dh * (1.0 + sc1)
        dw_pre = _sum_rows(dn1 * xh)
        dxh = dn1 * pw_ref[...]
        gx_ref[...] = dx_ref[...] + r * (dxh - xh * _mean_last(dxh * xh))
        _acc_rows(acc_ref, pl.program_id(0) % tps == 0, [dsh1, dsc1, dw_pre])
        _ride_wait(ride_modes, pl.program_id(0), N // TM, ride_in, ride_out, sems)

    row = lambda w: pl.BlockSpec((TM, w), lambda i: (i, 0))
    B = N // T
    return pl.pallas_call(
        body, name="in_bwd", grid=(N // TM,),
        in_specs=[row(D_MODEL), row(D_MODEL), row(768), row(HG_WIDTH), row(HG_WIDTH), row(HG_WIDTH),
                  row(HG_WIDTH), _mod_spec(tps), pl.BlockSpec((1, D_MODEL), lambda i: (0, 0)),
                  pl.BlockSpec((IN_COLS, D_MODEL), lambda i: (0, 0))] + [ANY_SPEC] * nr,
        out_specs=[row(D_MODEL), row(IN_COLS), _mod_spec(tps)] + [ANY_SPEC] * nr,
        out_shape=[SDS((N, D_MODEL), F32), SDS((N, IN_COLS), BF16), SDS((B, 8, D_MODEL), F32)]
        + _exchange_shapes(ride_srcs, ride_modes),
        scratch_shapes=_exchange_sems(nr),
        compiler_params=_params(("arbitrary",), VMEM_LIMIT_BIG),
    )(x2, dx1, dqkv, dhq, dhf, dhi, dhg, mod8, pre_w, w_in_bf, *ride_srcs)


def _matmul_tn(name, a, b, tn, tm=512, by_owner_cols=False):
    K, M = a.shape
    Nc = b.shape[1]
    tm = min(tm, M)

    def body(a_ref, b_ref, o_ref):
        o_ref[...] = _bf(_dot_tn(a_ref[...], b_ref[...]))

    if by_owner_cols:
        assert tn * N_DEV == Nc
        out_shape = SDS((N_DEV, M, tn), BF16)
        out_spec = pl.BlockSpec((None, tm, tn), lambda i, j: (j, i, 0))
    else:
        out_shape = SDS((M, Nc), BF16)
        out_spec = pl.BlockSpec((tm, tn), lambda i, j: (i, j))
    return pl.pallas_call(
        body, name=name, grid=(M // tm, Nc // tn),
        in_specs=[pl.BlockSpec((K, tm), lambda i, j: (0, i)),
                  pl.BlockSpec((K, tn), lambda i, j: (0, j))],
        out_specs=out_spec, out_shape=out_shape,
        compiler_params=_params(("arbitrary", "arbitrary"), VMEM_LIMIT_BIG),
    )(a, b)


GW_BLOCK = IN_COLS // N_DEV
GW_HALF = IN_COLS // 2


def _grad_w_in_reduced(dproj, h1, ride_srcs, ride_modes):
    K = dproj.shape[0]
    nr = len(ride_srcs)
    chips = N_DEV // 2
    quarter = D_MODEL // 4
    pieces = [[(0, quarter), (quarter, quarter), (2 * quarter, 2 * quarter)],
              [(0, 2 * quarter), (2 * quarter, quarter), (3 * quarter, quarter)]]
    npieces = len(pieces[0])

    def body(*refs):
        a_hbm, b_hbm = refs[:2]
        ride_in, out, ride_out = refs[2:2 + nr], refs[2 + nr], refs[3 + nr:3 + 2 * nr]
        a_buf, b_buf, g_buf, theirs, p_buf, in_sems, pair_send, pair_recv, chip_send, chip_recv, own_sem = \
            refs[3 + 2 * nr:14 + 2 * nr]
        ride = _exchange_phases(ride_modes, ride_in, ride_out, *refs[14 + 2 * nr:]) if nr else ([], [], [])
        x, y, core = lax.axis_index("x"), lax.axis_index("y"), lax.axis_index("c")
        chip = 2 * x + y
        sib_dev, _ = _related(SIBLING)

        def remote(src, dst, send_sem, recv_sem, dev):
            return pltpu.make_async_remote_copy(src_ref=src, dst_ref=dst, send_sem=send_sem, recv_sem=recv_sem,
                                                device_id=dev, device_id_type=MESH)

        halves = [1 - x, x]
        first = pl.ds(0, pieces[0][0][1])
        rest = pl.ds(pieces[0][0][1], D_MODEL - pieces[0][0][1])
        a_loads = []
        for t in range(2):
            col = pl.multiple_of(halves[t] * GW_HALF, LANES)
            a_loads.append(pltpu.make_async_copy(a_hbm.at[:, pl.ds(col, GW_HALF)], a_buf.at[t], in_sems.at[t]))
        b_loads = [pltpu.make_async_copy(b_hbm.at[:, c], b_buf.at[:, c], in_sems.at[2 + i])
                   for i, c in enumerate((first, rest))]
        for cp in (b_loads[0], a_loads[0], b_loads[1], a_loads[1]):
            cp.start()
        _run(ride[0])
        b_loads[0].wait()
        end = []
        for t in range(2):
            a_loads[t].wait()
            if t == 1:
                _run(ride[1])
            for j, (c0, width) in enumerate(pieces[t]):
                if (t, j) == (0, 1):
                    b_loads[1].wait()
                cols = pl.ds(c0, width)
                res = _dot_tn(a_buf[t], b_buf[:, cols])
                for q in range(2):
                    for cc in range(2):
                        r0 = (2 * q + cc) * GW_BLOCK
                        g_buf[t, q, cc, :, cols] = _bf(res[r0:r0 + GW_BLOCK])
                swaps = [remote(g_buf.at[t, q, 1 - core, :, cols], theirs.at[t, q, :, cols],
                                pair_send.at[t, 2 * j + q], pair_recv.at[t, 2 * j + q], sib_dev) for q in range(2)]
                for cp in swaps:
                    cp.start()
                for cp in swaps:
                    cp.wait_recv()
                end += [cp.wait_send for cp in swaps]
                for q in range(2):
                    p_buf[t, q, :, cols] = _bf(g_buf[t, q, core, :, cols].astype(F32)
                                               + theirs[t, q, :, cols].astype(F32))
                for dy in range(2):
                    k = 4 * (1 - t) + 2 * dy
                    if k == 0:
                        own = pltpu.make_async_copy(p_buf.at[t, y, :, cols], out.at[chip, :, cols], own_sem.at[j])
                        own.start()
                        end.append(own.wait)
                        continue
                    dev, peer = _related(k)
                    sems = chip_send.at[k // 2, j], chip_recv.at[k // 2, j]
                    send = remote(p_buf.at[t, y ^ dy, :, cols], out.at[chip, :, cols], *sems, dev)
                    send.start()
                    end += [remote(p_buf.at[t, y ^ dy, :, cols], out.at[peer // 2, :, cols], *sems, dev).wait_recv,
                            send.wait_send]
        _run(ride[2])
        _run(end)

    return pl.pallas_call(
        body, name="grad_w_in",
        in_specs=[ANY_SPEC] * (2 + nr), out_specs=[ANY_SPEC] * (1 + nr),
        out_shape=[SDS((chips, GW_BLOCK, D_MODEL), BF16)] + _exchange_shapes(ride_srcs, ride_modes),
        scratch_shapes=[pltpu.VMEM((2, K, GW_HALF), BF16), pltpu.VMEM((K, D_MODEL), BF16),
                        pltpu.VMEM((2, 2, 2, GW_BLOCK, D_MODEL), BF16), pltpu.VMEM((2, 2, GW_BLOCK, D_MODEL), BF16),
                        pltpu.VMEM((2, 2, GW_BLOCK, D_MODEL), BF16), pltpu.SemaphoreType.DMA((4,)),
                        pltpu.SemaphoreType.DMA((2, 2 * npieces)), pltpu.SemaphoreType.DMA((2, 2 * npieces)),
                        pltpu.SemaphoreType.DMA((chips, npieces)), pltpu.SemaphoreType.DMA((chips, npieces)),
                        pltpu.SemaphoreType.DMA((npieces,))] + _exchange_sems(nr),
        compiler_params=_params(None, VMEM_LIMIT_BIG),
    )(dproj, h1, *ride_srcs)


def _adamw_math(w, g, m, v):
    m2 = ADAM_B1 * m + (1.0 - ADAM_B1) * g
    v2 = ADAM_B2 * v + (1.0 - ADAM_B2) * (g * g)
    m_hat = m2 / (1.0 - ADAM_B1 ** ADAM_STEP)
    v_hat = v2 / (1.0 - ADAM_B2 ** ADAM_STEP)
    delta = -ADAM_LR * (m_hat / (jnp.sqrt(v_hat) + ADAM_EPS) + ADAM_WD * w)
    return delta, m2, v2


def _pair_add(name, gw, theirs):
    chips, _, r, c = gw.shape
    tr = r
    core = lax.axis_index("c").astype(jnp.int32).reshape(1)

    def body(core_ref, mine_ref, theirs_ref, o_ref):
        o_ref[...] = _bf(mine_ref[...].astype(F32) + theirs_ref[...].astype(F32))

    block = pl.BlockSpec((None, tr, c), lambda s, i, core_ref: (s, i, 0))
    grid_spec = pltpu.PrefetchScalarGridSpec(
        num_scalar_prefetch=1, grid=(chips, r // tr),
        in_specs=[pl.BlockSpec((None, None, tr, c), lambda s, i, core_ref: (s, core_ref[0], i, 0)), block],
        out_specs=block)
    return pl.pallas_call(
        body, name=name, grid_spec=grid_spec, out_shape=SDS((chips, r, c), BF16),
        compiler_params=_params(("arbitrary", "arbitrary")),
    )(core, gw, theirs)


def _reduce_adamw(name, parts, w, m, v):
    r, c = w.shape
    tr = r if r % 256 else 256
    slots = parts.shape[0]

    def body(p_ref, w_ref, m_ref, v_ref, g_ref, d_ref, m2_ref, v2_ref):
        g = p_ref[0].astype(F32)
        for s in range(1, slots):
            g = g + p_ref[s].astype(F32)
        g_ref[...] = g
        d_ref[...], m2_ref[...], v2_ref[...] = _adamw_math(w_ref[...], g, m_ref[...], v_ref[...])

    blk = pl.BlockSpec((tr, c), lambda i: (i, 0))
    return pl.pallas_call(
        body, name=name, grid=(r // tr,),
        in_specs=[pl.BlockSpec((slots, tr, c), lambda i: (0, i, 0)), blk, blk, blk],
        out_specs=[blk] * 4, out_shape=[SDS((r, c), F32)] * 4,
        compiler_params=_params(("arbitrary",), VMEM_LIMIT_BIG),
    )(parts, w, m, v)


def _ada_grad_adamw(c_all, dmod_all, w, m, v):
    r, c = w.shape
    tr = 256
    nb = c_all.shape[0]

    def body(c_ref, dm_ref, w_ref, m_ref, v_ref, g_ref, d_ref, m2_ref, v2_ref):
        cv = c_ref[...]
        g = _dot_tn(cv * _sigmoid(cv), dm_ref[...])
        g_ref[...] = g
        d_ref[...], m2_ref[...], v2_ref[...] = _adamw_math(w_ref[...], g, m_ref[...], v_ref[...])

    blk = pl.BlockSpec((tr, c), lambda i: (i, 0))
    return pl.pallas_call(
        body, name="ada_grad_adamw", grid=(r // tr,),
        in_specs=[pl.BlockSpec((nb, tr), lambda i: (0, i)), pl.BlockSpec((nb, c), lambda i: (0, 0)),
                  blk, blk, blk],
        out_specs=[blk] * 4, out_shape=[SDS((r, c), F32)] * 4,
        compiler_params=_params(("arbitrary",)),
    )(c_all, dmod_all, w, m, v)


_SMALL = [("b_ada", 6144), ("pre_w_mix", 1024), ("attn_sinks", 128), ("attn_out_w", 512), ("lb_table", 1024),
          ("hg_norm_w", 128), ("post_w_mix", 1024), ("pre_w_mlp", 1024), ("post_w_mlp", 1024)]


def _pack_small(acc_in, acc_mix, acc_mlp, dsink, daw, dlb, dgw, lb_p, ada_cols):
    B = acc_in.shape[0]
    width = sum(w for _, w in _SMALL) + LANES

    def body(ain, amix, amlp, dsk_ref, daw_ref, dlb_ref, dgw_ref, lbp_ref, packed_ref, dmod_ref):
        def total(ref, r, w=None):
            out = ref[0, r:r + 1, :] if w is None else ref[0, r:r + 1, :w]
            for b in range(1, B):
                out = out + (ref[b, r:r + 1, :] if w is None else ref[b, r:r + 1, :w])
            return out

        d_b_ada = None
        for b in range(B):
            mods = [ain[b, 0:1, :], ain[b, 1:2, :], amix[b, 0:1, :], amlp[b, 0:1, :], amlp[b, 1:2, :], amlp[b, 2:3, :]]
            full = jnp.concatenate(mods, axis=1)
            for j in range(N_DEV):
                dmod_ref[j, b:b + 1, :] = full[:, j * ada_cols:(j + 1) * ada_cols]
            d_b_ada = full if d_b_ada is None else d_b_ada + full
        d_lb = total(dlb_ref, 0)
        pp = lbp_ref[0:1, :] * lbp_ref[1:2, :]
        pieces = [d_b_ada, total(ain, 2), total(dsk_ref, 0), total(daw_ref, 0), -d_lb * pp, d_lb * pp,
                  total(dgw_ref, 0), total(amix, 1), total(amlp, 3), total(amlp, 4), total(amlp, 5, LANES)]
        off = 0
        for piece in pieces:
            packed_ref[:, off:off + piece.shape[1]] = piece
            off += piece.shape[1]

    return pl.pallas_call(
        body, name="pack_small",
        out_shape=[SDS((1, width), F32), SDS((N_DEV, B, ada_cols), F32)],
    )(acc_in, acc_mix, acc_mlp, dsink, daw, dlb, dgw, lb_p)


def _adamw_small(parts, given):
    names = [n for n, _ in _SMALL]
    flat_in = [a for n in names for a in given[n]]

    def body(*refs):
        p_ref = refs[0]
        in_refs = refs[1:1 + 3 * len(names)]
        out_refs = refs[1 + 3 * len(names):-1]
        loss_ref = refs[-1]
        g = p_ref[0]
        for s in range(1, N_DEV):
            g = g + p_ref[s]
        off = 0
        for i, (name, width) in enumerate(_SMALL):
            w_ref, m_ref, v_ref = in_refs[3 * i:3 * i + 3]
            rows, cols = w_ref.shape
            for r in range(rows):
                gr = g[:, off + r * cols:off + (r + 1) * cols]
                res = (gr,) + _adamw_math(w_ref[r:r + 1, :], gr, m_ref[r:r + 1, :], v_ref[r:r + 1, :])
                for o_ref, val in zip(out_refs[4 * i:4 * i + 4], res):
                    o_ref[r:r + 1, :] = val
            off += width
        loss_ref[...] = g[:, off:off + LANES]

    out_shape = [SDS(given[n][0].shape, F32) for n in names for _ in range(4)] + [SDS((1, LANES), F32)]
    outs = pl.pallas_call(body, name="adamw_small", out_shape=out_shape)(parts, *flat_in)
    return {n: tuple(outs[4 * i:4 * i + 4]) for i, n in enumerate(names)}, outs[-1][0, 0]


def kernel(x, c, w_ada, b_ada, pre_w_mix, w_in, attn_sinks, attn_out_w, lb_table, hg_norm_w, w_out, post_w_mix, pre_w_mlp, w_up, w_down, post_w_mlp, loss_target, m_w_ada, m_b_ada, m_pre_w_mix, m_w_in, m_attn_sinks, m_attn_out_w, m_lb_table, m_hg_norm_w, m_w_out, m_post_w_mix, m_pre_w_mlp, m_w_up, m_w_down, m_post_w_mlp, v_w_ada, v_b_ada, v_pre_w_mix, v_w_in, v_attn_sinks, v_attn_out_w, v_lb_table, v_hg_norm_w, v_w_out, v_post_w_mix, v_pre_w_mlp, v_w_up, v_w_down, v_post_w_mlp):
    B, T, _ = x.shape
    N = B * T
    me = 4 * lax.axis_index("x") + 2 * lax.axis_index("y") + lax.axis_index("c")
    x2 = x.reshape(N, D_MODEL)
    tgt2 = loss_target.reshape(N, D_MODEL)

    w_in_t, m_w_in_t, v_w_in_t = w_in[0].T, m_w_in[0].T, v_w_in[0].T
    w_in_g, c_g = _exchange("gather_w_in", [_bf(w_in_t), c], ["gather"] * 2)
    w_in_f = w_in_g.reshape(IN_COLS, D_MODEL)
    c_all = c_g.reshape(N_DEV * B, D_MODEL)

    ada_cols = w_ada.shape[2]
    b_mine = lax.dynamic_slice(b_ada, (0, me * ada_cols), (1, ada_cols))
    mod_cols = _ada_mod(c_all, w_ada[0], b_mine)
    (mod_g,) = _exchange("scatter_mod", [mod_cols.reshape(N_DEV, B, ada_cols)], ["a2a"])
    mod = mod_g.transpose(1, 0, 2).reshape(B, 6, D_MODEL)
    mod8 = jnp.pad(mod, ((0, 0), (0, 2), (0, 0)))

    lb_p = jax.nn.softmax(lb_table, axis=0)
    lb = lb_p[1:2]
    tables = _rope_tables(T)

    w_up_b, w_down_b = _bf(w_up[0]), _bf(w_down[0])
    proj_a, proj_h, h1, w_out_g, w_up_g0 = _in_proj(x2, mod8, pre_w_mix, w_in_f, T,
                                                    [_bf(w_out[0]), w_up_b[:MLP_HALF]], ["gather"] * 2)
    proj3 = proj_a.reshape(B, T, ATT_COLS)
    proj_h = proj_h.reshape(B, T, IN_COLS - ATT_COLS)
    rec_o, rec_g, s_prev, w_up_g1 = _hgrn_fwd(proj_h, lb, hg_norm_w, [w_up_b[MLP_HALF:]], ["gather"])
    attn_o, attn_n, qr, kr, w_down_g0 = _attn_fwd(proj3, tables, attn_sinks, attn_out_w,
                                                  [w_down_b[:, :MLP_HALF]], ["gather"])
    w_out_f = w_out_g.reshape(D_MODEL, D_MODEL)
    mix, x1, cat, w_down_g1 = _mix_out(x2, attn_n.reshape(N, ATT_WIDTH), rec_g.reshape(N, HG_WIDTH), mod8,
                                       post_w_mix, w_out_f, T, [w_down_b[:, MLP_HALF:]], ["gather"])
    w_up_halves = [w_up_g0, w_up_g1]
    w_down_halves = [w_down_g0.reshape(D_FF, MLP_HALF), w_down_g1.reshape(D_FF, MLP_HALF)]
    up, u, d, h2 = _mlp_fwd(x1, mod8, pre_w_mlp, w_up_halves, w_down_halves, T)

    dx1, dup, dd, acc_mlp = _mlp_bwd(x1, d, up, tgt2, mod8, pre_w_mlp, post_w_mlp, w_up_halves, w_down_halves, T)
    chips = N_DEV // 2
    by_chip = lambda a: a.reshape((chips, 2, a.shape[0] // N_DEV) + a.shape[1:])
    gw_up = _matmul_tn("grad_w_up", h2, dup, D_FF // N_DEV, tm=D_MODEL, by_owner_cols=True)
    gw_up = gw_up.reshape(chips, 2, D_MODEL, D_FF // N_DEV)
    gw_down = by_chip(_matmul_tn("grad_w_down", u, dd, D_MODEL))
    dan, drg, dmix, acc_mix, q_down, q_up = _mix_bwd(mix, dx1, mod8, post_w_mix, w_out_f, T,
                                                     [gw_down, gw_up], ["pair"] * 2)
    p_down, p_up = _pair_add("pair_add_w_down", gw_down, q_down), _pair_add("pair_add_w_up", gw_up, q_up)
    gw_out = _matmul_tn("grad_w_out", cat, dmix, 512, tm=D_MODEL).reshape(N_DEV, D_MODEL // N_DEV, D_MODEL)
    dhq, dhf, dhi, dhg, dlb_p, dgw_p, r_down, r_up = _hgrn_bwd(
        proj_h, lb, hg_norm_w, rec_o, s_prev, drg.reshape(B, T, HG_WIDTH), [p_down, p_up], ["chips"] * 2)
    dqkv, dsink_p, daw_p, r_out = _attn_bwd(qr, kr, proj3, attn_o, dan.reshape(B, T, ATT_WIDTH), tables,
                                            attn_sinks, attn_out_w, [gw_out], ["a2a"])
    flat = lambda a: a.reshape(N, a.shape[-1])
    grad_x, dproj, acc_in = _in_bwd(x2, dx1, flat(dqkv), flat(dhq), flat(dhf), flat(dhi), flat(dhg),
                                    mod8, pre_w_mix, w_in_f, T, [], [])

    packed, dmod_blocks = _pack_small(acc_in, acc_mix, acc_mlp, dsink_p, daw_p, dlb_p, dgw_p, lb_p, ada_cols)
    r_in, r_dmod, r_small = _grad_w_in_reduced(dproj, h1, [dmod_blocks, packed], ["a2a", "gather"])

    res = {}
    res["w_in"] = tuple(a.T for a in _reduce_adamw("adamw_w_in", r_in, w_in_t, m_w_in_t, v_w_in_t))
    res["w_out"] = _reduce_adamw("adamw_w_out", r_out, w_out[0], m_w_out[0], v_w_out[0])
    res["w_up"] = _reduce_adamw("adamw_w_up", r_up, w_up[0], m_w_up[0], v_w_up[0])
    res["w_down"] = _reduce_adamw("adamw_w_down", r_down, w_down[0], m_w_down[0], v_w_down[0])
    res["w_ada"] = _ada_grad_adamw(c_all, r_dmod.reshape(N_DEV * B, ada_cols), w_ada[0], m_w_ada[0], v_w_ada[0])

    given = dict(b_ada=(b_ada, m_b_ada, v_b_ada), pre_w_mix=(pre_w_mix, m_pre_w_mix, v_pre_w_mix),
                 attn_sinks=(attn_sinks, m_attn_sinks, v_attn_sinks),
                 attn_out_w=(attn_out_w, m_attn_out_w, v_attn_out_w), lb_table=(lb_table, m_lb_table, v_lb_table),
                 hg_norm_w=(hg_norm_w, m_hg_norm_w, v_hg_norm_w), post_w_mix=(post_w_mix, m_post_w_mix, v_post_w_mix),
                 pre_w_mlp=(pre_w_mlp, m_pre_w_mlp, v_pre_w_mlp), post_w_mlp=(post_w_mlp, m_post_w_mlp, v_post_w_mlp))
    small_res, loss = _adamw_small(r_small, given)
    res.update(small_res)

    order = ["w_ada", "b_ada", "pre_w_mix", "w_in", "attn_sinks", "attn_out_w", "lb_table", "hg_norm_w", "w_out",
             "post_w_mix", "pre_w_mlp", "w_up", "w_down", "post_w_mlp"]
    big = {"w_ada", "w_in", "w_out", "w_up", "w_down"}
    outs = [loss, grad_x.reshape(B, T, D_MODEL)]
    for i in range(4):
        for k in order:
            a = res[k][i]
            outs.append(a[None] if k in big else a)
    return tuple(outs)
```

```python
import jax
import jax.numpy as jnp
import numpy as np
from jax import lax
from jax.experimental import pallas as pl
from jax.experimental.pallas import tpu as pltpu

F32 = jnp.float32
BF16 = jnp.bfloat16
SDS = jax.ShapeDtypeStruct

D_MODEL = 1024
ATT_WIDTH = 512
ATT_HEAD_DIM = 64
ATT_KV_HEADS = 2
ATT_GROUP = 4
WINDOW = 128
ROPE_DIM = 16
ROPE_THETA = 500000.0
HG_WIDTH = 512
HG_HEAD_DIM = 128
HG_HEADS = 4
HG_CHUNK = 32
IN_COLS = 2816
ATT_COLS = 768
D_FF = 4096
EPS = 1e-6
N_DEV = 8

ADAM_LR = 0.001
ADAM_B1 = 0.9
ADAM_B2 = 0.999
ADAM_EPS = 1e-08
ADAM_WD = 0.01
ADAM_STEP = 10

VMEM_LIMIT_BIG = 56 << 20
LANES = 128

MESH = pl.DeviceIdType.MESH
NT_DIMS = (((1,), (1,)), ((), ()))
TN_DIMS = (((0,), (0,)), ((), ()))


def _dot(a, b):
    return jnp.dot(a, b, preferred_element_type=F32)


def _dot_nt(a, b):
    return lax.dot_general(a, b, NT_DIMS, preferred_element_type=F32)


def _dot_tn(a, b):
    return lax.dot_general(a, b, TN_DIMS, preferred_element_type=F32)


def _bf(a):
    return a.astype(BF16)


def _sigmoid(a):
    return 0.5 * jnp.tanh(0.5 * a) + 0.5


def _mean_last(a):
    return jnp.mean(a, axis=-1, keepdims=True)


def _sum_rows(a):
    return jnp.sum(a, axis=0, keepdims=True)


def _loop_pairs(first, count, body, init, per_trip=2):
    if count % per_trip:
        return lax.fori_loop(first, first + count, body, init)

    def trip(i, c):
        for k in range(per_trip):
            c = body(first + per_trip * i + k, c)
        return c

    return lax.fori_loop(0, count // per_trip, trip, init)


def _params(sem=None, vmem=None):
    kw = {}
    if sem is not None:
        kw["dimension_semantics"] = sem
    if vmem is not None:
        kw["vmem_limit_bytes"] = vmem
    return pltpu.CompilerParams(**kw)


ANY_SPEC = pl.BlockSpec(memory_space=pl.ANY)


def _exchange_shapes(srcs, modes):
    out_shape = []
    for s, m in zip(srcs, modes):
        shp = (N_DEV,) + tuple(s.shape) if m == "gather" else tuple(s.shape)
        out_shape.append(SDS(shp, s.dtype))
    return out_shape


def _exchange_sems(n):
    if n == 0:
        return []
    return [pltpu.SemaphoreType.DMA((n, N_DEV - 1)), pltpu.SemaphoreType.DMA((n, N_DEV - 1)),
            pltpu.SemaphoreType.DMA((n,))]


SIBLING = 1
OTHER_CHIPS = (2, 4, 6)


def _related(k):
    x, y, c = lax.axis_index("x"), lax.axis_index("y"), lax.axis_index("c")
    px, py, pc = x ^ ((k >> 2) & 1), y ^ ((k >> 1) & 1), c ^ (k & 1)
    return (px, py, pc), 4 * px + 2 * py + pc


def _exchange_phases(modes, src_refs, out_refs, send_sems, recv_sems, own_sems):
    _, me = _related(0)
    sib_dev, sib = _related(SIBLING)
    start, middle, end = [], [], []

    def remote(a, i, src, dst, dev):
        return pltpu.make_async_remote_copy(src_ref=src, dst_ref=dst, send_sem=send_sems.at[a, i],
                                            recv_sem=recv_sems.at[a, i], device_id=dev, device_id_type=MESH)

    for a, mode in enumerate(modes):
        out = out_refs[a]
        if mode == "gather":
            src = src_refs[a]
            own = pltpu.make_async_copy(src, out.at[me], own_sems.at[a])
            to_sib = remote(a, 0, src, out.at[me], sib_dev)
            start += [own.start, to_sib.start]
            end += [remote(a, 0, src, out.at[sib], sib_dev).wait_recv, to_sib.wait_send, own.wait]
            for j, k in enumerate(OTHER_CHIPS, start=1):
                dev, peer = _related(k)
                _, peer_sib = _related(k ^ SIBLING)
                send = remote(a, j, src, out.at[me], dev)
                passed = remote(a, 3 + j, out.at[peer], out.at[peer], sib_dev)
                start.append(send.start)
                middle += [remote(a, j, src, out.at[peer], dev).wait_recv, passed.start]
                end += [remote(a, 3 + j, out.at[peer_sib], out.at[peer_sib], sib_dev).wait_recv,
                        send.wait_send, passed.wait_send]
        elif mode == "chips":
            chip = me // 2
            own = pltpu.make_async_copy(src_refs[a].at[chip], out.at[chip], own_sems.at[a])
            start.append(own.start)
            end.append(own.wait)
            for j, k in enumerate(OTHER_CHIPS, start=1):
                dev, peer = _related(k)
                send = remote(a, j, src_refs[a].at[peer // 2], out.at[chip], dev)
                start.append(send.start)
                end += [remote(a, j, src_refs[a].at[peer // 2], out.at[peer // 2], dev).wait_recv, send.wait_send]
        else:
            own = pltpu.make_async_copy(src_refs[a].at[me], out.at[me], own_sems.at[a])
            start.append(own.start)
            end.append(own.wait)
            for k in range(1, N_DEV):
                dev, peer = _related(k)
                send = remote(a, k - 1, src_refs[a].at[peer], out.at[me], dev)
                start.append(send.start)
                end += [remote(a, k - 1, src_refs[a].at[peer], out.at[peer], dev).wait_recv, send.wait_send]
    return start, middle, end


def _run(actions):
    for act in actions:
        act()


def _exchange(name, srcs, modes):
    n = len(srcs)

    def body(*refs):
        start, middle, end = _exchange_phases(modes, refs[:n], refs[n:2 * n], *refs[2 * n:])
        _run(start)
        _run(middle)
        _run(end)

    return pl.pallas_call(
        body, name=name, out_shape=_exchange_shapes(srcs, modes),
        in_specs=[ANY_SPEC] * n, out_specs=[ANY_SPEC] * n,
        scratch_shapes=_exchange_sems(n),
    )(*srcs)


def _ride_start(modes, step, steps, src_refs, out_refs, sems):
    if not modes:
        return
    middle_step = steps - 1

    @pl.when(step == 0)
    def _():
        _run(_exchange_phases(modes, src_refs, out_refs, *sems)[0])

    if "gather" in modes:
        @pl.when(step == middle_step)
        def _():
            _run(_exchange_phases(modes, src_refs, out_refs, *sems)[1])


def _ride_wait(modes, step, steps, src_refs, out_refs, sems):
    if not modes:
        return

    @pl.when(step == steps - 1)
    def _():
        _run(_exchange_phases(modes, src_refs, out_refs, *sems)[2])


def _ada_mod(c_all, w_ada, b_ada_mine):
    nb, cols = c_all.shape[0], w_ada.shape[1]

    def body(c_ref, w_ref, b_ref, o_ref):
        cv = c_ref[...]
        ca = cv * _sigmoid(cv)
        o_ref[...] = _dot(ca, w_ref[...]) + b_ref[...]

    return pl.pallas_call(body, name="ada_mod", out_shape=SDS((nb, cols), F32))(c_all, w_ada, b_ada_mine)


def _tile_rows(T, big=False):
    return min(512 if big else 256, T)


def _mod_spec(tps):
    return pl.BlockSpec((None, 8, D_MODEL), lambda i: (i // tps, 0, 0))


def _in_proj(x2, mod8, pre_w, w_in_bf, T, ride_srcs, ride_modes):
    N = x2.shape[0]
    TM = _tile_rows(T, big=True)
    tps = T // TM
    nr = len(ride_srcs)

    def body(*refs):
        x_ref, mod_ref, pw_ref, w_ref = refs[:4]
        ride_in = refs[4:4 + nr]
        pa_ref, ph_ref, h1_ref = refs[4 + nr:7 + nr]
        ride_out = refs[7 + nr:7 + 2 * nr]
        sems = refs[7 + 2 * nr:]
        _ride_start(ride_modes, pl.program_id(0), N // TM, ride_in, ride_out, sems)
        x = x_ref[...]
        r = lax.rsqrt(_mean_last(x * x) + EPS)
        h = (x * r * pw_ref[...]) * (1.0 + mod_ref[1:2, :]) + mod_ref[0:1, :]
        hb = _bf(h)
        h1_ref[...] = hb
        pa_ref[...] = _dot_nt(hb, w_ref[:ATT_COLS, :])
        ph_ref[...] = _dot_nt(hb, w_ref[ATT_COLS:, :])
        _ride_wait(ride_modes, pl.program_id(0), N // TM, ride_in, ride_out, sems)

    return pl.pallas_call(
        body, name="in_proj", grid=(N // TM,),
        in_specs=[pl.BlockSpec((TM, D_MODEL), lambda i: (i, 0)), _mod_spec(tps),
                  pl.BlockSpec((1, D_MODEL), lambda i: (0, 0)),
                  pl.BlockSpec((IN_COLS, D_MODEL), lambda i: (0, 0))] + [ANY_SPEC] * nr,
        out_specs=[pl.BlockSpec((TM, ATT_COLS), lambda i: (i, 0)),
                   pl.BlockSpec((TM, IN_COLS - ATT_COLS), lambda i: (i, 0)),
                   pl.BlockSpec((TM, D_MODEL), lambda i: (i, 0))] + [ANY_SPEC] * nr,
        out_shape=[SDS((N, ATT_COLS), F32), SDS((N, IN_COLS - ATT_COLS), F32), SDS((N, D_MODEL), BF16)]
        + _exchange_shapes(ride_srcs, ride_modes),
        scratch_shapes=_exchange_sems(nr),
        compiler_params=_params(("arbitrary",), VMEM_LIMIT_BIG),
    )(x2, mod8, pre_w, w_in_bf, *ride_srcs)


def _rope_tables(T):
    half = ROPE_DIM // 2
    f32 = np.float32
    inv_freq = (f32(ROPE_THETA) ** (-np.arange(0, ROPE_DIM, 2, dtype=f32) / f32(ROPE_DIM))).astype(f32)
    ang = np.arange(T, dtype=f32)[:, None] * inv_freq[None, :]
    cos, sin = np.cos(ang).astype(f32), np.sin(ang).astype(f32)
    ones = np.ones((T, ATT_HEAD_DIM - ROPE_DIM), f32)
    zeros = np.zeros((T, ATT_HEAD_DIM - ROPE_DIM), f32)
    zh = np.zeros((T, half), f32)
    cos64 = np.concatenate([cos, cos, ones], axis=1)
    sin_left = np.concatenate([-sin, zh, zeros], axis=1)
    sin_right = np.concatenate([zh, sin, zeros], axis=1)
    rep = LANES // ATT_HEAD_DIM
    return tuple(jnp.asarray(np.tile(t, (1, rep))) for t in (cos64, sin_left, sin_right))


def _rope(xc, cs, sl, sr):
    return xc * cs + pltpu.roll(xc, LANES - 8, 1) * sl + pltpu.roll(xc, 8, 1) * sr


def _rope_t(dy, cs, sl, sr):
    return dy * cs + pltpu.roll(dy * sl, 8, 1) + pltpu.roll(dy * sr, LANES - 8, 1)


ATT_SCALE = ATT_HEAD_DIM ** -0.5
ATT_SPLITS = 4


def _lower_mask():
    j = lax.broadcasted_iota(jnp.int32, (WINDOW, ATT_GROUP * WINDOW), 0)
    i = lax.broadcasted_iota(jnp.int32, (WINDOW, ATT_GROUP * WINDOW), 1) & (WINDOW - 1)
    return j <= i


def _sink_row(sink_ref, hk):
    return jnp.concatenate(
        [jnp.full((1, WINDOW), sink_ref[0, ATT_GROUP * hk + g], F32) for g in range(ATT_GROUP)], axis=1)


def _softmax_window(qs, k_cur, k_prev, lower, has_prev, sink):
    s_prev = jnp.where(has_prev, _dot_nt(k_prev, qs), jnp.finfo(F32).min)
    s = jnp.where(lower, _dot_nt(k_cur, qs), s_prev)
    m = jnp.maximum(jnp.max(s, axis=0, keepdims=True), sink)
    p = jnp.exp(s - m)
    es = jnp.exp(sink - m)
    inv = 1.0 / (jnp.sum(p, axis=0, keepdims=True) + es)
    return p, inv, es


def _stack_heads(parts, hk):
    hs = []
    for g in range(ATT_GROUP):
        h = ATT_GROUP * hk + g
        hs.append(parts[h // 2][:, (h % 2) * ATT_HEAD_DIM:(h % 2 + 1) * ATT_HEAD_DIM])
    return jnp.concatenate(hs, axis=0)


def _attn_fwd(proj3, tables, sinks, attn_w, ride_srcs, ride_modes):
    B, T, _ = proj3.shape
    nb = T // WINDOW
    splits = min(ATT_SPLITS, nb)
    per = nb // splits
    nr = len(ride_srcs)
    cos, sinl, sinr = tables

    def body(*refs):
        q_ref, k_ref, v_ref, cos_ref, sl_ref, sr_ref, sink_ref, aw_ref = refs[:8]
        ride_in = refs[8:8 + nr]
        o_ref, an_ref, qr_ref, kr_ref = refs[8 + nr:12 + nr]
        ride_out = refs[12 + nr:12 + 2 * nr]
        kpad, vpad = refs[12 + 2 * nr:14 + 2 * nr]
        sems = refs[14 + 2 * nr:]
        part = pl.program_id(1)
        step = pl.program_id(0) * splits + part
        _ride_start(ride_modes, step, B * splits, ride_in, ride_out, sems)

        @pl.when(part == 0)
        def _():
            kpad[0:WINDOW, :] = jnp.zeros((WINDOW, LANES), BF16)
            vpad[0:WINDOW, :] = jnp.zeros((WINDOW, LANES), BF16)

        lower = _lower_mask()

        def block(n, carry):
            r0 = pl.multiple_of(n * WINDOW, WINDOW)
            rows = pl.ds(r0, WINDOW)
            nxt = pl.ds(r0 + WINDOW, WINDOW)
            cs, sl, sr = cos_ref[rows, :], sl_ref[rows, :], sr_ref[rows, :]
            kb = _bf(_rope(k_ref[rows, :], cs, sl, sr))
            vb = _bf(v_ref[rows, :])
            kpad[nxt, :] = kb
            kr_ref[rows, :] = kb
            vpad[nxt, :] = vb
            qparts = []
            for j in range(ATT_WIDTH // LANES):
                qp = _bf(_rope(q_ref[rows, j * LANES:(j + 1) * LANES], cs, sl, sr) * ATT_SCALE)
                qr_ref[rows, j * LANES:(j + 1) * LANES] = qp
                qparts.append(qp)
            for hk in range(ATT_KV_HEADS):
                lanes = slice(hk * ATT_HEAD_DIM, (hk + 1) * ATT_HEAD_DIM)
                qs = _stack_heads(qparts, hk)
                p, inv, _ = _softmax_window(qs, kb[:, lanes], kpad[rows, lanes], lower, n > 0,
                                            _sink_row(sink_ref, hk))
                p_cur = jnp.where(lower, p, 0.0)
                ot = (_dot_tn(vb[:, lanes], _bf(p_cur)) + _dot_tn(vpad[rows, lanes], _bf(p - p_cur))) * inv
                for g in range(ATT_GROUP):
                    h = ATT_GROUP * hk + g
                    o_ref[rows, h * ATT_HEAD_DIM:(h + 1) * ATT_HEAD_DIM] = ot[:, g * WINDOW:(g + 1) * WINDOW].T
            ob = o_ref[rows, :]
            an_ref[rows, :] = _bf(ob * lax.rsqrt(_mean_last(ob * ob) + EPS) * aw_ref[...])
            return carry

        _loop_pairs(part * per, per, block, 0)
        _ride_wait(ride_modes, step, B * splits, ride_in, ride_out, sems)

    seq = lambda w, j: pl.BlockSpec((None, T, w), lambda b, s: (b, 0, j))
    full = lambda r, w: pl.BlockSpec((r, w), lambda b, s: (0, 0))
    return pl.pallas_call(
        body, name="attn_fwd", grid=(B, splits),
        in_specs=[seq(ATT_WIDTH, 0), seq(LANES, 4), seq(LANES, 5),
                  full(T, LANES), full(T, LANES), full(T, LANES),
                  pl.BlockSpec(memory_space=pltpu.SMEM), full(1, ATT_WIDTH)] + [ANY_SPEC] * nr,
        out_specs=[seq(ATT_WIDTH, 0), seq(ATT_WIDTH, 0), seq(ATT_WIDTH, 0), seq(LANES, 0)] + [ANY_SPEC] * nr,
        out_shape=[SDS((B, T, ATT_WIDTH), F32), SDS((B, T, ATT_WIDTH), BF16),
                   SDS((B, T, ATT_WIDTH), BF16), SDS((B, T, LANES), BF16)] + _exchange_shapes(ride_srcs, ride_modes),
        scratch_shapes=[pltpu.VMEM((T + WINDOW, LANES), BF16), pltpu.VMEM((T + WINDOW, LANES), BF16)]
        + _exchange_sems(nr),
        compiler_params=_params(("arbitrary", "arbitrary"), VMEM_LIMIT_BIG),
    )(proj3, proj3, proj3, cos, sinl, sinr, sinks, attn_w, *ride_srcs)


HG_GROUP = 8
HG_ROWS = HG_GROUP * HG_CHUNK


HG_STACK = HG_GROUP * HG_HEAD_DIM


def _group_mask():
    r = lax.broadcasted_iota(jnp.int32, (HG_ROWS, HG_ROWS), 0)
    c = lax.broadcasted_iota(jnp.int32, (HG_ROWS, HG_ROWS), 1)
    return ((r // HG_CHUNK) == (c // HG_CHUNK)) & (r >= c)


def _spread(a):
    blocks = []
    for c in range(HG_GROUP):
        above = jnp.zeros((c * HG_CHUNK, HG_HEAD_DIM), a.dtype)
        below = jnp.zeros(((HG_GROUP - 1 - c) * HG_CHUNK, HG_HEAD_DIM), a.dtype)
        blocks.append(jnp.concatenate([p for p in (above, a[_chunk_rows(c), :], below) if p.shape[0]], axis=0))
    return jnp.concatenate(blocks, axis=1)


def _pick(r):
    return jnp.concatenate([r[_chunk_rows(c), c * HG_HEAD_DIM:(c + 1) * HG_HEAD_DIM] for c in range(HG_GROUP)], axis=0)


def _lane_block(a, c):
    return a[:, c * HG_HEAD_DIM:(c + 1) * HG_HEAD_DIM]


def _chunk_cumsum(a, reverse=False):
    n = a.shape[0]
    pos = lax.broadcasted_iota(jnp.int32, a.shape, 0) % HG_CHUNK
    shift = 1
    while shift < HG_CHUNK:
        if reverse:
            a = a + jnp.where(pos < HG_CHUNK - shift, pltpu.roll(a, n - shift, 0), 0.0)
        else:
            a = a + jnp.where(pos >= shift, pltpu.roll(a, shift, 0), 0.0)
        shift *= 2
    return a


def _chunk_bcast(rows_1x128):
    return jnp.concatenate([jnp.broadcast_to(r, (HG_CHUNK, HG_HEAD_DIM)) for r in rows_1x128], axis=0)


def _hgrn_gates(hq, hf, lb):
    sq = _sigmoid(hq)
    q = hq * sq
    sg = _sigmoid(hf)
    f = lb + (1.0 - lb) * sg
    k = 1.0 - f
    logf = jnp.log(f)
    b = _chunk_cumsum(logf)
    bl = [_sum_rows(logf[_chunk_rows(c), :]) for c in range(HG_GROUP)]
    eb, enb, e2 = jnp.exp(b), jnp.exp(-b), jnp.exp(_chunk_bcast(bl) - b)
    ebl = [jnp.exp(r) for r in bl]
    return dict(sq=sq, sg=sg, f=f, eb=eb, enb=enb, e2=e2, ebl=ebl, qd=q * eb, kd=k * enb, k2=k * e2)


def _chunk_rows(c):
    return slice(c * HG_CHUNK, (c + 1) * HG_CHUNK)


def _head_lanes(h):
    return slice(h * HG_HEAD_DIM, (h + 1) * HG_HEAD_DIM)


def _hgrn_fwd(proj_h, lb, hg_w, ride_srcs, ride_modes):
    B, T, _ = proj_h.shape
    ng = T // HG_ROWS
    nr = len(ride_srcs)

    def body(*refs):
        hq_ref, hf_ref, hi_ref, hg_ref, lb_ref, gw_ref = refs[:6]
        ride_in = refs[6:6 + nr]
        o_ref, rg_ref, sp_ref = refs[6 + nr:9 + nr]
        ride_out = refs[9 + nr:9 + 2 * nr]
        st = refs[9 + 2 * nr]
        sems = refs[10 + 2 * nr:]
        gi = pl.program_id(1)
        step = pl.program_id(0) * ng + gi
        _ride_start(ride_modes, step, B * ng, ride_in, ride_out, sems)

        @pl.when(gi == 0)
        def _():
            st[...] = jnp.zeros(st.shape, F32)

        lo = _group_mask()
        for h in range(HG_HEADS):
            lanes = _head_lanes(h)
            gt = _hgrn_gates(hq_ref[:, lanes], hf_ref[:, lanes], lb_ref[:, lanes])
            v, qd, kd = _bf(hi_ref[:, lanes]), _bf(gt["qd"]), _bf(gt["kd"])
            a = jnp.where(lo, _dot_nt(qd, kd), 0.0)
            kv = _dot_tn(v, _spread(_bf(gt["k2"])))
            s = st[h]
            before = []
            for c in range(HG_GROUP):
                before.append(s)
                s = s * gt["ebl"][c] + _lane_block(kv, c)
            st[h] = s
            sp = jnp.concatenate(before, axis=1)
            sp_ref[h] = sp
            o = _dot(_bf(a), v) + _dot_nt(_spread(qd), _bf(sp))
            o_ref[:, lanes] = o
            hg = hg_ref[:, lanes]
            rn = o * lax.rsqrt(_mean_last(o * o) + EPS) * gw_ref[...]
            rg_ref[:, lanes] = _bf(rn * (hg * _sigmoid(hg)))
        _ride_wait(ride_modes, step, B * ng, ride_in, ride_out, sems)

    part = lambda j: pl.BlockSpec((None, HG_ROWS, HG_WIDTH), lambda b, g: (b, g, j))
    return pl.pallas_call(
        body, name="hgrn_fwd", grid=(B, ng),
        in_specs=[part(0), part(1), part(2), part(3),
                  pl.BlockSpec((1, HG_WIDTH), lambda b, g: (0, 0)),
                  pl.BlockSpec((1, LANES), lambda b, g: (0, 0))] + [ANY_SPEC] * nr,
        out_specs=[part(0), part(0),
                   pl.BlockSpec((None, HG_HEADS, None, HG_HEAD_DIM, HG_STACK), lambda b, g: (b, 0, g, 0, 0))]
        + [ANY_SPEC] * nr,
        out_shape=[SDS((B, T, HG_WIDTH), F32), SDS((B, T, HG_WIDTH), BF16),
                   SDS((B, HG_HEADS, ng, HG_HEAD_DIM, HG_STACK), F32)] + _exchange_shapes(ride_srcs, ride_modes),
        scratch_shapes=[pltpu.VMEM((HG_HEADS, HG_HEAD_DIM, HG_HEAD_DIM), F32)] + _exchange_sems(nr),
        compiler_params=_params(("arbitrary", "arbitrary"), VMEM_LIMIT_BIG),
    )(proj_h, proj_h, proj_h, proj_h, lb, hg_w, *ride_srcs)


def _mix_out(x2, attn_n, rec_g, mod8, post_w, w_out_bf, T, ride_srcs, ride_modes):
    N = x2.shape[0]
    TM = _tile_rows(T, big=True)
    tps = T // TM
    nr = len(ride_srcs)

    def body(*refs):
        x_ref, an_ref, rg_ref, mod_ref, pw_ref, w_ref = refs[:6]
        ride_in = refs[6:6 + nr]
        mix_ref, x1_ref, cat_ref = refs[6 + nr:9 + nr]
        ride_out = refs[9 + nr:9 + 2 * nr]
        sems = refs[9 + 2 * nr:]
        _ride_start(ride_modes, pl.program_id(0), N // TM, ride_in, ride_out, sems)
        cat = jnp.concatenate([an_ref[...], rg_ref[...]], axis=1)
        cat_ref[...] = cat
        mix = _dot(cat, w_ref[...])
        mix_ref[...] = mix
        r = lax.rsqrt(_mean_last(mix * mix) + EPS)
        x1_ref[...] = x_ref[...] + mod_ref[2:3, :] * (mix * r * pw_ref[...])
        _ride_wait(ride_modes, pl.program_id(0), N // TM, ride_in, ride_out, sems)

    row = lambda w: pl.BlockSpec((TM, w), lambda i: (i, 0))
    return pl.pallas_call(
        body, name="mix_out", grid=(N // TM,),
        in_specs=[row(D_MODEL), row(ATT_WIDTH), row(HG_WIDTH), _mod_spec(tps),
                  pl.BlockSpec((1, D_MODEL), lambda i: (0, 0)),
                  pl.BlockSpec((D_MODEL, D_MODEL), lambda i: (0, 0))] + [ANY_SPEC] * nr,
        out_specs=[row(D_MODEL), row(D_MODEL), row(D_MODEL)] + [ANY_SPEC] * nr,
        out_shape=[SDS((N, D_MODEL), F32), SDS((N, D_MODEL), F32), SDS((N, D_MODEL), BF16)]
        + _exchange_shapes(ride_srcs, ride_modes),
        scratch_shapes=_exchange_sems(nr),
        compiler_params=_params(("arbitrary",), VMEM_LIMIT_BIG),
    )(x2, attn_n, rec_g, mod8, post_w, w_out_bf, *ride_srcs)


def _load_weights_once(pairs, sem):
    @pl.when(pl.program_id(0) == 0)
    def _():
        cps = [pltpu.make_async_copy(src, dst, sem.at[i]) for i, (src, dst) in enumerate(pairs)]
        for cp in cps:
            cp.start()
        for cp in cps:
            cp.wait()


MLP_HALF = D_MODEL // 2
MLP_PIECES = 2 * N_DEV + 2


def _mlp_weight_pieces(wu_a, wu_b, wd_a, wd_b, wu, wd):
    cols = D_FF // N_DEV
    pairs = []
    for h, half in enumerate((wu_a, wu_b)):
        for j in range(N_DEV):
            pairs.append((half.at[j], wu.at[pl.ds(h * MLP_HALF, MLP_HALF), pl.ds(j * cols, cols)]))
    for h, half in enumerate((wd_a, wd_b)):
        pairs.append((half, wd.at[:, pl.ds(h * MLP_HALF, MLP_HALF)]))
    return pairs


def _mlp_fwd(x1, mod8, pre_w, w_up_halves, w_down_halves, T):
    N = x1.shape[0]
    TM = _tile_rows(T)
    tps = T // TM

    def body(x_ref, mod_ref, pw_ref, wua, wub, wda, wdb, up_ref, u_ref, d_ref, h2_ref, wu, wd, sem):
        _load_weights_once(_mlp_weight_pieces(wua, wub, wda, wdb, wu, wd), sem)
        x = x_ref[...]
        r = lax.rsqrt(_mean_last(x * x) + EPS)
        h = (x * r * pw_ref[...]) * (1.0 + mod_ref[4:5, :]) + mod_ref[3:4, :]
        hb = _bf(h)
        h2_ref[...] = hb
        up = _dot(hb, wu[...])
        up_ref[...] = up
        ru = jnp.maximum(up, 0.0)
        u = _bf(ru * ru)
        u_ref[...] = u
        d_ref[...] = _dot(u, wd[...])

    row = lambda w: pl.BlockSpec((TM, w), lambda i: (i, 0))
    return pl.pallas_call(
        body, name="mlp_fwd", grid=(N // TM,),
        in_specs=[row(D_MODEL), _mod_spec(tps), pl.BlockSpec((1, D_MODEL), lambda i: (0, 0))] + [ANY_SPEC] * 4,
        out_specs=[row(D_FF), row(D_FF), row(D_MODEL), row(D_MODEL)],
        out_shape=[SDS((N, D_FF), F32), SDS((N, D_FF), BF16), SDS((N, D_MODEL), F32), SDS((N, D_MODEL), BF16)],
        scratch_shapes=[pltpu.VMEM((D_MODEL, D_FF), BF16), pltpu.VMEM((D_FF, D_MODEL), BF16),
                        pltpu.SemaphoreType.DMA((MLP_PIECES,))],
        compiler_params=_params(("arbitrary",), VMEM_LIMIT_BIG),
    )(x1, mod8, pre_w, *w_up_halves, *w_down_halves)


def _acc_rows(acc_ref, first, rows):
    @pl.when(first)
    def _():
        acc_ref[...] = jnp.zeros(acc_ref.shape, F32)
    for i, r in enumerate(rows):
        acc_ref[i:i + 1, :] += r


def _mlp_bwd(x1, d, up, tgt, mod8, pre_w, post_w, w_up_halves, w_down_halves, T):
    N = x1.shape[0]
    TM = _tile_rows(T)
    tps = T // TM

    def body(x_ref, d_ref, up_ref, t_ref, mod_ref, pw_ref, qw_ref, wua, wub, wda, wdb,
             dx_ref, dup_ref, dd_ref, acc_ref, wd, wu, sem):
        _load_weights_once(_mlp_weight_pieces(wua, wub, wda, wdb, wu, wd), sem)
        sh2, sc2, g2 = mod_ref[3:4, :], mod_ref[4:5, :], mod_ref[5:6, :]
        x = x_ref[...]
        r1 = lax.rsqrt(_mean_last(x * x) + EPS)
        xh = x * r1
        n2 = xh * pw_ref[...]
        dv = d_ref[...]
        rd = lax.rsqrt(_mean_last(dv * dv) + EPS)
        dh = dv * rd
        rr = dh * qw_ref[...]
        e = x + g2 * rr - t_ref[...]
        loss = 0.5 * jnp.sum(_sum_rows(e * e), axis=1, keepdims=True) / D_MODEL
        dy = e * (1.0 / D_MODEL)
        dg2 = _sum_rows(dy * rr)
        drr = dy * g2
        dw_post = _sum_rows(drr * dh)
        ddh = drr * qw_ref[...]
        dd = _bf(rd * (ddh - dh * _mean_last(ddh * dh)))
        dd_ref[...] = dd
        ru = jnp.maximum(up_ref[...], 0.0)
        dup = _bf(_dot_nt(dd, wd[...]) * (2.0 * ru))
        dup_ref[...] = dup
        dh2 = _dot_nt(dup, wu[...])
        dsh2 = _sum_rows(dh2)
        dsc2 = _sum_rows(dh2 * n2)
        dn2 = dh2 * (1.0 + sc2)
        dw_pre = _sum_rows(dn2 * xh)
        dxh = dn2 * pw_ref[...]
        dx_ref[...] = dy + r1 * (dxh - xh * _mean_last(dxh * xh))
        _acc_rows(acc_ref, pl.program_id(0) % tps == 0,
                  [dsh2, dsc2, dg2, dw_pre, dw_post, jnp.broadcast_to(loss, (1, D_MODEL))])

    row = lambda w: pl.BlockSpec((TM, w), lambda i: (i, 0))
    vec = pl.BlockSpec((1, D_MODEL), lambda i: (0, 0))
    B = N // T
    return pl.pallas_call(
        body, name="mlp_bwd", grid=(N // TM,),
        in_specs=[row(D_MODEL), row(D_MODEL), row(D_FF), row(D_MODEL), _mod_spec(tps), vec, vec] + [ANY_SPEC] * 4,
        out_specs=[row(D_MODEL), row(D_FF), row(D_MODEL), _mod_spec(tps)],
        out_shape=[SDS((N, D_MODEL), F32), SDS((N, D_FF), BF16), SDS((N, D_MODEL), BF16),
                   SDS((B, 8, D_MODEL), F32)],
        scratch_shapes=[pltpu.VMEM((D_FF, D_MODEL), BF16), pltpu.VMEM((D_MODEL, D_FF), BF16),
                        pltpu.SemaphoreType.DMA((MLP_PIECES,))],
        compiler_params=_params(("arbitrary",), VMEM_LIMIT_BIG),
    )(x1, d, up, tgt, mod8, pre_w, post_w, *w_up_halves, *w_down_halves)


def _mix_bwd(mix, dx1, mod8, post_w, w_out_bf, T, ride_srcs, ride_modes):
    N = mix.shape[0]
    TM = _tile_rows(T, big=True)
    tps = T // TM
    nr = len(ride_srcs)

    def body(*refs):
        mix_ref, dx_ref, mod_ref, pw_ref, w_ref = refs[:5]
        ride_in = refs[5:5 + nr]
        dan_ref, drg_ref, dmix_ref, acc_ref = refs[5 + nr:9 + nr]
        ride_out = refs[9 + nr:9 + 2 * nr]
        sems = refs[9 + 2 * nr:]
        _ride_start(ride_modes, pl.program_id(0), N // TM, ride_in, ride_out, sems)
        g1 = mod_ref[2:3, :]
        mix = mix_ref[...]
        dx1 = dx_ref[...]
        rm = lax.rsqrt(_mean_last(mix * mix) + EPS)
        mh = mix * rm
        dg1 = _sum_rows(dx1 * (mh * pw_ref[...]))
        dr = dx1 * g1
        dw_post = _sum_rows(dr * mh)
        dmh = dr * pw_ref[...]
        dmix = _bf(rm * (dmh - mh * _mean_last(dmh * mh)))
        dmix_ref[...] = dmix
        dcat = _dot_nt(dmix, w_ref[...])
        dan_ref[...] = dcat[:, :ATT_WIDTH]
        drg_ref[...] = dcat[:, ATT_WIDTH:]
        _acc_rows(acc_ref, pl.program_id(0) % tps == 0, [dg1, dw_post])
        _ride_wait(ride_modes, pl.program_id(0), N // TM, ride_in, ride_out, sems)

    row = lambda w: pl.BlockSpec((TM, w), lambda i: (i, 0))
    B = N // T
    return pl.pallas_call(
        body, name="mix_bwd", grid=(N // TM,),
        in_specs=[row(D_MODEL), row(D_MODEL), _mod_spec(tps), pl.BlockSpec((1, D_MODEL), lambda i: (0, 0)),
                  pl.BlockSpec((D_MODEL, D_MODEL), lambda i: (0, 0))] + [ANY_SPEC] * nr,
        out_specs=[row(ATT_WIDTH), row(HG_WIDTH), row(D_MODEL), _mod_spec(tps)] + [ANY_SPEC] * nr,
        out_shape=[SDS((N, ATT_WIDTH), F32), SDS((N, HG_WIDTH), F32), SDS((N, D_MODEL), BF16),
                   SDS((B, 8, D_MODEL), F32)] + _exchange_shapes(ride_srcs, ride_modes),
        scratch_shapes=_exchange_sems(nr),
        compiler_params=_params(("arbitrary",), VMEM_LIMIT_BIG),
    )(mix, dx1, mod8, post_w, w_out_bf, *ride_srcs)


def _hgrn_bwd(proj_h, lb, hg_w, o, s_prev, drg, ride_srcs, ride_modes):
    B, T, _ = proj_h.shape
    ng = T // HG_ROWS
    nr = len(ride_srcs)

    def body(*refs):
        hq_ref, hf_ref, hi_ref, hg_ref, lb_ref, gw_ref, o_ref, sp_ref, drg_ref = refs[:9]
        ride_in = refs[9:9 + nr]
        dhq_ref, dhf_ref, dhi_ref, dhg_ref, dlb_ref, dgw_ref = refs[9 + nr:15 + nr]
        ride_out = refs[15 + nr:15 + 2 * nr]
        dst = refs[15 + 2 * nr]
        sems = refs[16 + 2 * nr:]
        step = pl.program_id(0) * ng + pl.program_id(1)
        _ride_start(ride_modes, step, B * ng, ride_in, ride_out, sems)

        @pl.when(pl.program_id(1) == 0)
        def _():
            dst[...] = jnp.zeros(dst.shape, F32)
            dlb_ref[...] = jnp.zeros(dlb_ref.shape, F32)
            dgw_ref[...] = jnp.zeros(dgw_ref.shape, F32)

        lo = _group_mask()
        gw = gw_ref[...]

        for h in range(HG_HEADS):
            lanes = _head_lanes(h)
            lbv = lb_ref[:, lanes]
            hq = hq_ref[:, lanes]
            gt = _hgrn_gates(hq, hf_ref[:, lanes], lbv)
            sq, sg, qdf, kdf, k2f, ebl = gt["sq"], gt["sg"], gt["qd"], gt["kd"], gt["k2"], gt["ebl"]
            v, qd, kd = _bf(hi_ref[:, lanes]), _bf(qdf), _bf(kdf)
            ov = o_ref[:, lanes]
            hg = hg_ref[:, lanes]
            shg = _sigmoid(hg)
            dr = drg_ref[:, lanes]
            ro = lax.rsqrt(_mean_last(ov * ov) + EPS)
            oh = ov * ro
            dhg_ref[:, lanes] = _bf(dr * (oh * gw) * (shg + hg * shg * (1.0 - shg)))
            drn = dr * (hg * shg)
            dgw_ref[...] += jnp.broadcast_to(_sum_rows(drn * oh), (8, LANES))
            doh = drn * gw
            do = _bf(ro * (doh - oh * _mean_last(doh * oh)))
            a = jnp.where(lo, _dot_nt(qd, kd), 0.0)
            da = _bf(jnp.where(lo, _dot_nt(do, v), 0.0))
            dv = _dot_tn(_bf(a), do)
            dqd = _dot(da, kd)
            dkd = _dot_tn(da, qd)
            sp = sp_ref[h]
            incr = _dot_tn(do, _spread(qd))
            ds = dst[h]
            after = [None] * HG_GROUP
            for c in reversed(range(HG_GROUP)):
                after[c] = ds
                ds = ds * ebl[c] + _lane_block(incr, c)
            dst[h] = ds
            dss = jnp.concatenate(after, axis=1)
            dssb = _bf(dss)
            dk2 = _pick(_dot(v, dssb))
            dhi_ref[:, lanes] = _bf(dv + _dot_nt(_spread(_bf(k2f)), dssb))
            dqd = dqd + _pick(_dot(do, _bf(sp)))
            debl = _sum_rows(dss * sp)
            k2g = dk2 * k2f
            db = dqd * qdf - dkd * kdf - k2g
            dk = dkd * gt["enb"] + dk2 * gt["e2"]
            dbl = _chunk_bcast([_lane_block(debl, c) * ebl[c] + _sum_rows(k2g[_chunk_rows(c), :])
                                for c in range(HG_GROUP)])
            dg = _chunk_cumsum(db, reverse=True) + dbl
            df = dg / gt["f"] - dk
            dhf_ref[:, lanes] = _bf(df * (1.0 - lbv) * sg * (1.0 - sg))
            dlb_ref[:, lanes] += jnp.broadcast_to(_sum_rows(df * (1.0 - sg)), (8, LANES))
            dhq_ref[:, lanes] = _bf((dqd * gt["eb"]) * (sq + hq * sq * (1.0 - sq)))
        _ride_wait(ride_modes, step, B * ng, ride_in, ride_out, sems)

    part = lambda j: pl.BlockSpec((None, HG_ROWS, HG_WIDTH), lambda b, g: (b, ng - 1 - g, j))
    return pl.pallas_call(
        body, name="hgrn_bwd", grid=(B, ng),
        in_specs=[part(0), part(1), part(2), part(3),
                  pl.BlockSpec((1, HG_WIDTH), lambda b, g: (0, 0)),
                  pl.BlockSpec((1, LANES), lambda b, g: (0, 0)),
                  part(0),
                  pl.BlockSpec((None, HG_HEADS, None, HG_HEAD_DIM, HG_STACK), lambda b, g: (b, 0, ng - 1 - g, 0, 0)),
                  part(0)] + [ANY_SPEC] * nr,
        out_specs=[part(0), part(0), part(0), part(0),
                   pl.BlockSpec((None, 8, HG_WIDTH), lambda b, g: (b, 0, 0)),
                   pl.BlockSpec((None, 8, LANES), lambda b, g: (b, 0, 0))] + [ANY_SPEC] * nr,
        out_shape=[SDS((B, T, HG_WIDTH), BF16)] * 4 + [SDS((B, 8, HG_WIDTH), F32), SDS((B, 8, LANES), F32)]
        + _exchange_shapes(ride_srcs, ride_modes),
        scratch_shapes=[pltpu.VMEM((HG_HEADS, HG_HEAD_DIM, HG_HEAD_DIM), F32)] + _exchange_sems(nr),
        compiler_params=_params(("arbitrary", "arbitrary"), VMEM_LIMIT_BIG),
    )(proj_h, proj_h, proj_h, proj_h, lb, hg_w, o, s_prev, drg, *ride_srcs)


def _attn_bwd(qr, kr, proj3, attn_o, dan, tables, sinks, attn_w, ride_srcs, ride_modes):
    B, T, _ = proj3.shape
    nb = T // WINDOW
    splits = min(ATT_SPLITS, nb)
    per = nb // splits
    nr = len(ride_srcs)
    cos, sinl, sinr = tables
    QKV = ATT_WIDTH + 2 * LANES

    def body(*refs):
        qr_ref, kr_ref, v_ref, o_ref, dan_ref, cos_ref, sl_ref, sr_ref, sink_ref, aw_ref = refs[:10]
        ride_in = refs[10:10 + nr]
        dqkv_ref, dsink_ref, daw_ref = refs[10 + nr:13 + nr]
        ride_out = refs[13 + nr:13 + 2 * nr]
        kpad, vpad, dkpad, dvpad, dqb, dsk = refs[13 + 2 * nr:19 + 2 * nr]
        sems = refs[19 + 2 * nr:]
        part = pl.program_id(1)
        step = pl.program_id(0) * splits + part
        _ride_start(ride_modes, step, B * splits, ride_in, ride_out, sems)

        @pl.when(part == 0)
        def _():
            kpad[0:WINDOW, :] = jnp.zeros((WINDOW, LANES), BF16)
            vpad[0:WINDOW, :] = jnp.zeros((WINDOW, LANES), BF16)
            kpad[WINDOW:, :] = kr_ref[...]
            vpad[WINDOW:, :] = _bf(v_ref[...])
            dkpad[...] = jnp.zeros(dkpad.shape, F32)
            dvpad[...] = jnp.zeros(dvpad.shape, F32)
            dsk[...] = jnp.zeros(dsk.shape, F32)
            daw_ref[...] = jnp.zeros(daw_ref.shape, F32)

        lower = _lower_mask()
        aw = aw_ref[...]

        def block(n, daw):
            r0 = pl.multiple_of(n * WINDOW, WINDOW)
            rows = pl.ds(r0, WINDOW)
            nxt = pl.ds(r0 + WINDOW, WINDOW)
            ob = o_ref[rows, :]
            dn = dan_ref[rows, :]
            ro = lax.rsqrt(_mean_last(ob * ob) + EPS)
            oh = ob * ro
            daw = daw + _sum_rows(dn * oh)
            doh = dn * aw
            do = _bf(ro * (doh - oh * _mean_last(doh * oh)))
            doparts = [do[:, j * LANES:(j + 1) * LANES] for j in range(ATT_WIDTH // LANES)]
            qparts = [qr_ref[rows, j * LANES:(j + 1) * LANES] for j in range(ATT_WIDTH // LANES)]
            for hk in range(ATT_KV_HEADS):
                lanes = slice(hk * ATT_HEAD_DIM, (hk + 1) * ATT_HEAD_DIM)
                qs = _stack_heads(qparts, hk)
                dos = _stack_heads(doparts, hk)
                k_cur, k_prev = kpad[nxt, lanes], kpad[rows, lanes]
                v_cur, v_prev = vpad[nxt, lanes], vpad[rows, lanes]
                p, inv, es = _softmax_window(qs, k_cur, k_prev, lower, n > 0, _sink_row(sink_ref, hk))
                p = p * inv
                dp = jnp.where(lower, _dot_nt(v_cur, dos), _dot_nt(v_prev, dos))
                delta = jnp.sum(p * dp, axis=0, keepdims=True)
                ds = p * (dp - delta)
                sk = (es * inv) * delta
                ds_cur = jnp.where(lower, ds, 0.0)
                p_cur = jnp.where(lower, p, 0.0)
                ds_cur, ds_prev = _bf(ds_cur), _bf(ds - ds_cur)
                p_cur, p_prev = _bf(p_cur), _bf(p - p_cur)
                dqt = (_dot_tn(k_cur, ds_cur) + _dot_tn(k_prev, ds_prev)) * ATT_SCALE
                dkpad[nxt, lanes] += _dot(ds_cur, qs)
                dkpad[rows, lanes] += _dot(ds_prev, qs)
                dvpad[nxt, lanes] += _dot(p_cur, dos)
                dvpad[rows, lanes] += _dot(p_prev, dos)
                for g in range(ATT_GROUP):
                    h = ATT_GROUP * hk + g
                    cols = slice(g * WINDOW, (g + 1) * WINDOW)
                    dqb[:, h * ATT_HEAD_DIM:(h + 1) * ATT_HEAD_DIM] = dqt[:, cols].T
                    head_lane = lax.broadcasted_iota(jnp.int32, dsk.shape, 1) == h
                    dsk[...] += jnp.where(head_lane, -jnp.sum(sk[:, cols], axis=1, keepdims=True), 0.0)
            cs, sl, sr = cos_ref[rows, :], sl_ref[rows, :], sr_ref[rows, :]
            for j in range(ATT_WIDTH // LANES):
                dqkv_ref[rows, j * LANES:(j + 1) * LANES] = _bf(_rope_t(dqb[:, j * LANES:(j + 1) * LANES], cs, sl, sr))
            return daw

        daw = _loop_pairs(part * per, per, block, jnp.zeros((1, ATT_WIDTH), F32))
        daw_ref[...] += jnp.broadcast_to(daw, (8, ATT_WIDTH))
        dsink_ref[...] = dsk[...]

        def finish(n, carry):
            r0 = pl.multiple_of(n * WINDOW, WINDOW)
            rows = pl.ds(r0, WINDOW)
            nxt = pl.ds(r0 + WINDOW, WINDOW)
            cs, sl, sr = cos_ref[rows, :], sl_ref[rows, :], sr_ref[rows, :]
            dqkv_ref[rows, ATT_WIDTH:ATT_WIDTH + LANES] = _bf(_rope_t(dkpad[nxt, :], cs, sl, sr))
            dqkv_ref[rows, ATT_WIDTH + LANES:QKV] = _bf(dvpad[nxt, :])
            return carry

        @pl.when(part == splits - 1)
        def _():
            lax.fori_loop(0, nb, finish, 0)

        _ride_wait(ride_modes, step, B * splits, ride_in, ride_out, sems)

    seq = lambda w, j: pl.BlockSpec((None, T, w), lambda b, s: (b, 0, j))
    full = lambda r, w: pl.BlockSpec((r, w), lambda b, s: (0, 0))
    return pl.pallas_call(
        body, name="attn_bwd", grid=(B, splits),
        in_specs=[seq(ATT_WIDTH, 0), seq(LANES, 0), seq(LANES, 5), seq(ATT_WIDTH, 0), seq(ATT_WIDTH, 0),
                  full(T, LANES), full(T, LANES), full(T, LANES),
                  pl.BlockSpec(memory_space=pltpu.SMEM), full(1, ATT_WIDTH)] + [ANY_SPEC] * nr,
        out_specs=[seq(QKV, 0), pl.BlockSpec((None, 8, LANES), lambda b, s: (b, 0, 0)),
                   pl.BlockSpec((None, 8, ATT_WIDTH), lambda b, s: (b, 0, 0))] + [ANY_SPEC] * nr,
        out_shape=[SDS((B, T, QKV), BF16), SDS((B, 8, LANES), F32), SDS((B, 8, ATT_WIDTH), F32)]
        + _exchange_shapes(ride_srcs, ride_modes),
        scratch_shapes=[pltpu.VMEM((T + WINDOW, LANES), BF16), pltpu.VMEM((T + WINDOW, LANES), BF16),
                        pltpu.VMEM((T + WINDOW, LANES), F32), pltpu.VMEM((T + WINDOW, LANES), F32),
                        pltpu.VMEM((WINDOW, ATT_WIDTH), F32), pltpu.VMEM((8, LANES), F32)] + _exchange_sems(nr),
        compiler_params=_params(("arbitrary", "arbitrary"), VMEM_LIMIT_BIG),
    )(qr, kr, proj3, attn_o, dan, cos, sinl, sinr, sinks, attn_w, *ride_srcs)


def _in_bwd(x2, dx1, dqkv, dhq, dhf, dhi, dhg, mod8, pre_w, w_in_bf, T, ride_srcs, ride_modes):
    N = x2.shape[0]
    TM = _tile_rows(T, big=True)
    tps = T // TM
    nr = len(ride_srcs)
    pieces = [(0, ATT_WIDTH + 2 * LANES), (768, HG_WIDTH), (1280, HG_WIDTH), (1792, HG_WIDTH), (2304, HG_WIDTH)]

    def body(*refs):
        x_ref, dx_ref, p0, p1, p2, p3, p4, mod_ref, pw_ref, w_ref = refs[:10]
        ride_in = refs[10:10 + nr]
        gx_ref, dproj_ref, acc_ref = refs[10 + nr:13 + nr]
        ride_out = refs[13 + nr:13 + 2 * nr]
        sems = refs[13 + 2 * nr:]
        _ride_start(ride_modes, pl.program_id(0), N // TM, ride_in, ride_out, sems)
        sc1 = mod_ref[1:2, :]
        dh = jnp.zeros((TM, D_MODEL), F32)
        for ref, (off, width) in zip((p0, p1, p2, p3, p4), pieces):
            pb = ref[...]
            dproj_ref[:, off:off + width] = pb
            dh = dh + _dot(pb, w_ref[off:off + width, :])
        x = x_ref[...]
        r = lax.rsqrt(_mean_last(x * x) + EPS)
        xh = x * r
        n1 = xh * pw_ref[...]
        dsh1 = _sum_rows(dh)
        dsc1 = _sum_rows(dh * n1)
        dn1 = dh * (1.0 + sc1)
        dw_pre = _sum_rows(dn1 * xh)
        dxh = dn1 * pw_ref[...]
        gx_ref[...] = dx_ref[...] + r * (dxh - xh * _mean_last(dxh * xh))
        _acc_rows(acc_ref, pl.program_id(0) % tps == 0, [dsh1, dsc1, dw_pre])
        _ride_wait(ride_modes, pl.program_id(0), N // TM, ride_in, ride_out, sems)

    row = lambda w: pl.BlockSpec((TM, w), lambda i: (i, 0))
    B = N // T
    return pl.pallas_call(
        body, name="in_bwd", grid=(N // TM,),
        in_specs=[row(D_MODEL), row(D_MODEL), row(768), row(HG_WIDTH), row(HG_WIDTH), row(HG_WIDTH),
                  row(HG_WIDTH), _mod_spec(tps), pl.BlockSpec((1, D_MODEL), lambda i: (0, 0)),
                  pl.BlockSpec((IN_COLS, D_MODEL), lambda i: (0, 0))] + [ANY_SPEC] * nr,
        out_specs=[row(D_MODEL), row(IN_COLS), _mod_spec(tps)] + [ANY_SPEC] * nr,
        out_shape=[SDS((N, D_MODEL), F32), SDS((N, IN_COLS), BF16), SDS((B, 8, D_MODEL), F32)]
        + _exchange_shapes(ride_srcs, ride_modes),
        scratch_shapes=_exchange_sems(nr),
        compiler_params=_params(("arbitrary",), VMEM_LIMIT_BIG),
    )(x2, dx1, dqkv, dhq, dhf, dhi, dhg, mod8, pre_w, w_in_bf, *ride_srcs)


def _matmul_tn(name, a, b, tn, tm):
    K, M = a.shape
    Nc = b.shape[1]

    def body(a_ref, b_ref, o_ref):
        o_ref[...] = _bf(_dot_tn(a_ref[...], b_ref[...]))

    return pl.pallas_call(
        body, name=name, grid=(M // tm, Nc // tn),
        in_specs=[pl.BlockSpec((K, tm), lambda i, j: (0, i)),
                  pl.BlockSpec((K, tn), lambda i, j: (0, j))],
        out_specs=pl.BlockSpec((tm, tn), lambda i, j: (i, j)), out_shape=SDS((M, Nc), BF16),
        compiler_params=_params(("arbitrary", "arbitrary"), VMEM_LIMIT_BIG),
    )(a, b)


def _matmul_tn_paired(name, a, b, stream_a):
    K = a.shape[0]
    stream, fixed = (a, b) if stream_a else (b, a)
    w = stream.shape[1] // N_DEV
    blk = (w, fixed.shape[1]) if stream_a else (fixed.shape[1], w)
    chips = N_DEV // 2

    def body(s_hbm, f_hbm, out_ref, s_buf, f_buf, g_buf, theirs, in_sems, send_sems, recv_sems):
        core = lax.axis_index("c")
        sib_dev, _ = _related(SIBLING)
        fixed_load = pltpu.make_async_copy(f_hbm, f_buf, in_sems.at[2])

        def load(j):
            return pltpu.make_async_copy(s_hbm.at[:, pl.ds(j * w, w)], s_buf.at[j % 2], in_sems.at[j % 2])

        def swap(s):
            return pltpu.make_async_remote_copy(
                src_ref=g_buf.at[s, 1 - core], dst_ref=theirs.at[s], send_sem=send_sems.at[s],
                recv_sem=recv_sems.at[s], device_id=sib_dev, device_id_type=MESH)

        def finish(s):
            swap(s).wait_recv()
            out_ref[s] = _bf(g_buf[s, core].astype(F32) + theirs[s].astype(F32))

        fixed_load.start()
        load(0).start()
        fixed_load.wait()
        for j in range(N_DEV):
            s, cc = divmod(j, 2)
            load(j).wait()
            if j + 1 < N_DEV:
                load(j + 1).start()
            if stream_a:
                g_buf[s, cc] = _bf(_dot_tn(s_buf[j % 2], f_buf[...]))
            else:
                g_buf[s, cc] = _bf(_dot_tn(f_buf[...], s_buf[j % 2]))
            if cc == 1:
                swap(s).start()
                if s > 0:
                    finish(s - 1)
        finish(chips - 1)
        for s in range(chips):
            swap(s).wait_send()

    return pl.pallas_call(
        body, name=name, in_specs=[ANY_SPEC] * 2, out_shape=SDS((chips,) + blk, BF16),
        scratch_shapes=[pltpu.VMEM((2, K, w), BF16), pltpu.VMEM(fixed.shape, BF16),
                        pltpu.VMEM((chips, 2) + blk, BF16), pltpu.VMEM((chips,) + blk, BF16),
                        pltpu.SemaphoreType.DMA((3,)), pltpu.SemaphoreType.DMA((chips,)),
                        pltpu.SemaphoreType.DMA((chips,))],
        compiler_params=_params(None, VMEM_LIMIT_BIG),
    )(stream, fixed)


GW_BLOCK = IN_COLS // N_DEV
GW_HALF = IN_COLS // 2


def _grad_w_in_reduced(dproj, h1, ride_srcs, ride_modes):
    K = dproj.shape[0]
    nr = len(ride_srcs)
    chips = N_DEV // 2
    tn = 512

    def body(*refs):
        a_hbm, b_hbm = refs[:2]
        ride_in, out, ride_out = refs[2:2 + nr], refs[2 + nr], refs[3 + nr:3 + 2 * nr]
        a_buf, b_buf, g_buf, theirs, p_buf, in_sems, pair_send, pair_recv, chip_send, chip_recv, own_sem = \
            refs[3 + 2 * nr:14 + 2 * nr]
        ride = _exchange_phases(ride_modes, ride_in, ride_out, *refs[14 + 2 * nr:]) if nr else ([], [], [])
        x, y, core = lax.axis_index("x"), lax.axis_index("y"), lax.axis_index("c")
        chip = 2 * x + y
        sib_dev, _ = _related(SIBLING)

        def remote(src, dst, send_sem, recv_sem, dev):
            return pltpu.make_async_remote_copy(src_ref=src, dst_ref=dst, send_sem=send_sem, recv_sem=recv_sem,
                                                device_id=dev, device_id_type=MESH)

        halves = [1 - x, x]
        loads = [pltpu.make_async_copy(b_hbm, b_buf, in_sems.at[0])]
        for t in range(2):
            col = pl.multiple_of(halves[t] * GW_HALF, LANES)
            loads.append(pltpu.make_async_copy(a_hbm.at[:, pl.ds(col, GW_HALF)], a_buf.at[t], in_sems.at[1 + t]))
        for cp in loads:
            cp.start()
        _run(ride[0])
        loads[0].wait()
        end = []
        for t in range(2):
            loads[1 + t].wait()
            if t == 1:
                _run(ride[1])
            for j in range(D_MODEL // tn):
                cols = pl.ds(j * tn, tn)
                res = _dot_tn(a_buf[t], b_buf[:, cols])
                for q in range(2):
                    for cc in range(2):
                        r0 = (2 * q + cc) * GW_BLOCK
                        g_buf[t, q, cc, :, cols] = _bf(res[r0:r0 + GW_BLOCK])
                swaps = [remote(g_buf.at[t, q, 1 - core, :, cols], theirs.at[t, q, :, cols],
                                pair_send.at[t, 2 * j + q], pair_recv.at[t, 2 * j + q], sib_dev) for q in range(2)]
                for cp in swaps:
                    cp.start()
                for cp in swaps:
                    cp.wait_recv()
                end += [cp.wait_send for cp in swaps]
                for q in range(2):
                    p_buf[t, q, :, cols] = _bf(g_buf[t, q, core, :, cols].astype(F32)
                                               + theirs[t, q, :, cols].astype(F32))
                for dy in range(2):
                    k = 4 * (1 - t) + 2 * dy
                    if k == 0:
                        own = pltpu.make_async_copy(p_buf.at[t, y, :, cols], out.at[chip, :, cols], own_sem.at[j])
                        own.start()
                        end.append(own.wait)
                        continue
                    dev, peer = _related(k)
                    sems = chip_send.at[k // 2, j], chip_recv.at[k // 2, j]
                    send = remote(p_buf.at[t, y ^ dy, :, cols], out.at[chip, :, cols], *sems, dev)
                    send.start()
                    end += [remote(p_buf.at[t, y ^ dy, :, cols], out.at[peer // 2, :, cols], *sems, dev).wait_recv,
                            send.wait_send]
        _run(ride[2])
        _run(end)

    return pl.pallas_call(
        body, name="grad_w_in",
        in_specs=[ANY_SPEC] * (2 + nr), out_specs=[ANY_SPEC] * (1 + nr),
        out_shape=[SDS((chips, GW_BLOCK, D_MODEL), BF16)] + _exchange_shapes(ride_srcs, ride_modes),
        scratch_shapes=[pltpu.VMEM((2, K, GW_HALF), BF16), pltpu.VMEM((K, D_MODEL), BF16),
                        pltpu.VMEM((2, 2, 2, GW_BLOCK, D_MODEL), BF16), pltpu.VMEM((2, 2, GW_BLOCK, D_MODEL), BF16),
                        pltpu.VMEM((2, 2, GW_BLOCK, D_MODEL), BF16), pltpu.SemaphoreType.DMA((3,)),
                        pltpu.SemaphoreType.DMA((2, 4)), pltpu.SemaphoreType.DMA((2, 4)),
                        pltpu.SemaphoreType.DMA((chips, 2)), pltpu.SemaphoreType.DMA((chips, 2)),
                        pltpu.SemaphoreType.DMA((2,))] + _exchange_sems(nr),
        compiler_params=_params(None, VMEM_LIMIT_BIG),
    )(dproj, h1, *ride_srcs)


def _adamw_math(w, g, m, v):
    m2 = ADAM_B1 * m + (1.0 - ADAM_B1) * g
    v2 = ADAM_B2 * v + (1.0 - ADAM_B2) * (g * g)
    m_hat = m2 / (1.0 - ADAM_B1 ** ADAM_STEP)
    v_hat = v2 / (1.0 - ADAM_B2 ** ADAM_STEP)
    delta = -ADAM_LR * (m_hat / (jnp.sqrt(v_hat) + ADAM_EPS) + ADAM_WD * w)
    return delta, m2, v2


def _reduce_adamw(name, parts, w, m, v):
    r, c = w.shape
    tr = r if r % 256 else 256
    slots = parts.shape[0]

    def body(p_ref, w_ref, m_ref, v_ref, g_ref, d_ref, m2_ref, v2_ref):
        g = p_ref[0].astype(F32)
        for s in range(1, slots):
            g = g + p_ref[s].astype(F32)
        g_ref[...] = g
        d_ref[...], m2_ref[...], v2_ref[...] = _adamw_math(w_ref[...], g, m_ref[...], v_ref[...])

    blk = pl.BlockSpec((tr, c), lambda i: (i, 0))
    return pl.pallas_call(
        body, name=name, grid=(r // tr,),
        in_specs=[pl.BlockSpec((slots, tr, c), lambda i: (0, i, 0)), blk, blk, blk],
        out_specs=[blk] * 4, out_shape=[SDS((r, c), F32)] * 4,
        compiler_params=_params(("arbitrary",), VMEM_LIMIT_BIG),
    )(parts, w, m, v)


def _ada_grad_adamw(c_all, dmod_all, w, m, v):
    r, c = w.shape
    tr = 256
    nb = c_all.shape[0]

    def body(c_ref, dm_ref, w_ref, m_ref, v_ref, g_ref, d_ref, m2_ref, v2_ref):
        cv = c_ref[...]
        g = _dot_tn(cv * _sigmoid(cv), dm_ref[...])
        g_ref[...] = g
        d_ref[...], m2_ref[...], v2_ref[...] = _adamw_math(w_ref[...], g, m_ref[...], v_ref[...])

    blk = pl.BlockSpec((tr, c), lambda i: (i, 0))
    return pl.pallas_call(
        body, name="ada_grad_adamw", grid=(r // tr,),
        in_specs=[pl.BlockSpec((nb, tr), lambda i: (0, i)), pl.BlockSpec((nb, c), lambda i: (0, 0)),
                  blk, blk, blk],
        out_specs=[blk] * 4, out_shape=[SDS((r, c), F32)] * 4,
        compiler_params=_params(("arbitrary",)),
    )(c_all, dmod_all, w, m, v)


_SMALL = [("b_ada", 6144), ("pre_w_mix", 1024), ("attn_sinks", 128), ("attn_out_w", 512), ("lb_table", 1024),
          ("hg_norm_w", 128), ("post_w_mix", 1024), ("pre_w_mlp", 1024), ("post_w_mlp", 1024)]


def _pack_small(acc_in, acc_mix, acc_mlp, dsink, daw, dlb, dgw, lb_p, ada_cols):
    B = acc_in.shape[0]
    width = sum(w for _, w in _SMALL) + LANES

    def body(ain, amix, amlp, dsk_ref, daw_ref, dlb_ref, dgw_ref, lbp_ref, packed_ref, dmod_ref):
        def total(ref, r, w=None):
            out = ref[0, r:r + 1, :] if w is None else ref[0, r:r + 1, :w]
            for b in range(1, B):
                out = out + (ref[b, r:r + 1, :] if w is None else ref[b, r:r + 1, :w])
            return out

        d_b_ada = None
        for b in range(B):
            mods = [ain[b, 0:1, :], ain[b, 1:2, :], amix[b, 0:1, :], amlp[b, 0:1, :], amlp[b, 1:2, :], amlp[b, 2:3, :]]
            full = jnp.concatenate(mods, axis=1)
            for j in range(N_DEV):
                dmod_ref[j, b:b + 1, :] = full[:, j * ada_cols:(j + 1) * ada_cols]
            d_b_ada = full if d_b_ada is None else d_b_ada + full
        d_lb = total(dlb_ref, 0)
        pp = lbp_ref[0:1, :] * lbp_ref[1:2, :]
        pieces = [d_b_ada, total(ain, 2), total(dsk_ref, 0), total(daw_ref, 0), -d_lb * pp, d_lb * pp,
                  total(dgw_ref, 0), total(amix, 1), total(amlp, 3), total(amlp, 4), total(amlp, 5, LANES)]
        off = 0
        for piece in pieces:
            packed_ref[:, off:off + piece.shape[1]] = piece
            off += piece.shape[1]

    return pl.pallas_call(
        body, name="pack_small",
        out_shape=[SDS((1, width), F32), SDS((N_DEV, B, ada_cols), F32)],
    )(acc_in, acc_mix, acc_mlp, dsink, daw, dlb, dgw, lb_p)


def _adamw_small(parts, given):
    names = [n for n, _ in _SMALL]
    flat_in = [a for n in names for a in given[n]]

    def body(*refs):
        p_ref = refs[0]
        in_refs = refs[1:1 + 3 * len(names)]
        out_refs = refs[1 + 3 * len(names):-1]
        loss_ref = refs[-1]
        g = p_ref[0]
        for s in range(1, N_DEV):
            g = g + p_ref[s]
        off = 0
        for i, (name, width) in enumerate(_SMALL):
            w_ref, m_ref, v_ref = in_refs[3 * i:3 * i + 3]
            rows, cols = w_ref.shape
            for r in range(rows):
                gr = g[:, off + r * cols:off + (r + 1) * cols]
                res = (gr,) + _adamw_math(w_ref[r:r + 1, :], gr, m_ref[r:r + 1, :], v_ref[r:r + 1, :])
                for o_ref, val in zip(out_refs[4 * i:4 * i + 4], res):
                    o_ref[r:r + 1, :] = val
            off += width
        loss_ref[...] = g[:, off:off + LANES]

    out_shape = [SDS(given[n][0].shape, F32) for n in names for _ in range(4)] + [SDS((1, LANES), F32)]
    outs = pl.pallas_call(body, name="adamw_small", out_shape=out_shape)(parts, *flat_in)
    return {n: tuple(outs[4 * i:4 * i + 4]) for i, n in enumerate(names)}, outs[-1][0, 0]


def kernel(x, c, w_ada, b_ada, pre_w_mix, w_in, attn_sinks, attn_out_w, lb_table, hg_norm_w, w_out, post_w_mix, pre_w_mlp, w_up, w_down, post_w_mlp, loss_target, m_w_ada, m_b_ada, m_pre_w_mix, m_w_in, m_attn_sinks, m_attn_out_w, m_lb_table, m_hg_norm_w, m_w_out, m_post_w_mix, m_pre_w_mlp, m_w_up, m_w_down, m_post_w_mlp, v_w_ada, v_b_ada, v_pre_w_mix, v_w_in, v_attn_sinks, v_attn_out_w, v_lb_table, v_hg_norm_w, v_w_out, v_post_w_mix, v_pre_w_mlp, v_w_up, v_w_down, v_post_w_mlp):
    B, T, _ = x.shape
    N = B * T
    me = 4 * lax.axis_index("x") + 2 * lax.axis_index("y") + lax.axis_index("c")
    x2 = x.reshape(N, D_MODEL)
    tgt2 = loss_target.reshape(N, D_MODEL)

    w_in_t, m_w_in_t, v_w_in_t = w_in[0].T, m_w_in[0].T, v_w_in[0].T
    w_in_g, c_g = _exchange("gather_w_in", [_bf(w_in_t), c], ["gather"] * 2)
    w_in_f = w_in_g.reshape(IN_COLS, D_MODEL)
    c_all = c_g.reshape(N_DEV * B, D_MODEL)

    ada_cols = w_ada.shape[2]
    b_mine = lax.dynamic_slice(b_ada, (0, me * ada_cols), (1, ada_cols))
    mod_cols = _ada_mod(c_all, w_ada[0], b_mine)
    (mod_g,) = _exchange("scatter_mod", [mod_cols.reshape(N_DEV, B, ada_cols)], ["a2a"])
    mod = mod_g.transpose(1, 0, 2).reshape(B, 6, D_MODEL)
    mod8 = jnp.pad(mod, ((0, 0), (0, 2), (0, 0)))

    lb_p = jax.nn.softmax(lb_table, axis=0)
    lb = lb_p[1:2]
    tables = _rope_tables(T)

    w_up_b, w_down_b = _bf(w_up[0]), _bf(w_down[0])
    proj_a, proj_h, h1, w_out_g, w_up_g0 = _in_proj(x2, mod8, pre_w_mix, w_in_f, T,
                                                    [_bf(w_out[0]), w_up_b[:MLP_HALF]], ["gather"] * 2)
    proj3 = proj_a.reshape(B, T, ATT_COLS)
    proj_h = proj_h.reshape(B, T, IN_COLS - ATT_COLS)
    rec_o, rec_g, s_prev, w_up_g1 = _hgrn_fwd(proj_h, lb, hg_norm_w, [w_up_b[MLP_HALF:]], ["gather"])
    attn_o, attn_n, qr, kr, w_down_g0 = _attn_fwd(proj3, tables, attn_sinks, attn_out_w,
                                                  [w_down_b[:, :MLP_HALF]], ["gather"])
    w_out_f = w_out_g.reshape(D_MODEL, D_MODEL)
    mix, x1, cat, w_down_g1 = _mix_out(x2, attn_n.reshape(N, ATT_WIDTH), rec_g.reshape(N, HG_WIDTH), mod8,
                                       post_w_mix, w_out_f, T, [w_down_b[:, MLP_HALF:]], ["gather"])
    w_up_halves = [w_up_g0, w_up_g1]
    w_down_halves = [w_down_g0.reshape(D_FF, MLP_HALF), w_down_g1.reshape(D_FF, MLP_HALF)]
    up, u, d, h2 = _mlp_fwd(x1, mod8, pre_w_mlp, w_up_halves, w_down_halves, T)

    dx1, dup, dd, acc_mlp = _mlp_bwd(x1, d, up, tgt2, mod8, pre_w_mlp, post_w_mlp, w_up_halves, w_down_halves, T)
    p_up = _matmul_tn_paired("grad_w_up", h2, dup, stream_a=False)
    p_down = _matmul_tn_paired("grad_w_down", u, dd, stream_a=True)
    dan, drg, dmix, acc_mix = _mix_bwd(mix, dx1, mod8, post_w_mix, w_out_f, T, [], [])
    gw_out = _matmul_tn("grad_w_out", cat, dmix, 512, tm=D_MODEL).reshape(N_DEV, D_MODEL // N_DEV, D_MODEL)
    dhq, dhf, dhi, dhg, dlb_p, dgw_p, r_down, r_up = _hgrn_bwd(
        proj_h, lb, hg_norm_w, rec_o, s_prev, drg.reshape(B, T, HG_WIDTH), [p_down, p_up], ["chips"] * 2)
    dqkv, dsink_p, daw_p, r_out = _attn_bwd(qr, kr, proj3, attn_o, dan.reshape(B, T, ATT_WIDTH), tables,
                                            attn_sinks, attn_out_w, [gw_out], ["a2a"])
    flat = lambda a: a.reshape(N, a.shape[-1])
    grad_x, dproj, acc_in = _in_bwd(x2, dx1, flat(dqkv), flat(dhq), flat(dhf), flat(dhi), flat(dhg),
                                    mod8, pre_w_mix, w_in_f, T, [], [])

    packed, dmod_blocks = _pack_small(acc_in, acc_mix, acc_mlp, dsink_p, daw_p, dlb_p, dgw_p, lb_p, ada_cols)
    r_in, r_dmod, r_small = _grad_w_in_reduced(dproj, h1, [dmod_blocks, packed], ["a2a", "gather"])

    res = {}
    res["w_in"] = tuple(a.T for a in _reduce_adamw("adamw_w_in", r_in, w_in_t, m_w_in_t, v_w_in_t))
    res["w_out"] = _reduce_adamw("adamw_w_out", r_out, w_out[0], m_w_out[0], v_w_out[0])
    res["w_up"] = _reduce_adamw("adamw_w_up", r_up, w_up[0], m_w_up[0], v_w_up[0])
    res["w_down"] = _reduce_adamw("adamw_w_down", r_down, w_down[0], m_w_down[0], v_w_down[0])
    res["w_ada"] = _ada_grad_adamw(c_all, r_dmod.reshape(N_DEV * B, ada_cols), w_ada[0], m_w_ada[0], v_w_ada[0])

    given = dict(b_ada=(b_ada, m_b_ada, v_b_ada), pre_w_mix=(pre_w_mix, m_pre_w_mix, v_pre_w_mix),
                 attn_sinks=(attn_sinks, m_attn_sinks, v_attn_sinks),
                 attn_out_w=(attn_out_w, m_attn_out_w, v_attn_out_w), lb_table=(lb_table, m_lb_table, v_lb_table),
                 hg_norm_w=(hg_norm_w, m_hg_norm_w, v_hg_norm_w), post_w_mix=(post_w_mix, m_post_w_mix, v_post_w_mix),
                 pre_w_mlp=(pre_w_mlp, m_pre_w_mlp, v_pre_w_mlp), post_w_mlp=(post_w_mlp, m_post_w_mlp, v_post_w_mlp))
    small_res, loss = _adamw_small(r_small, given)
    res.update(small_res)

    order = ["w_ada", "b_ada", "pre_w_mix", "w_in", "attn_sinks", "attn_out_w", "lb_table", "hg_norm_w", "w_out",
             "post_w_mix", "pre_w_mlp", "w_up", "w_down", "post_w_mlp"]
    big = {"w_ada", "w_in", "w_out", "w_up", "w_down"}
    outs = [loss, grad_x.reshape(B, T, D_MODEL)]
    for i in range(4):
        for k in order:
            a = res[k][i]
            outs.append(a[None] if k in big else a)
    return tuple(outs)
```

```python
import jax
import jax.numpy as jnp
import numpy as np
from jax import lax
from jax.experimental import pallas as pl
from jax.experimental.pallas import tpu as pltpu

F32 = jnp.float32
BF16 = jnp.bfloat16
SDS = jax.ShapeDtypeStruct

D_MODEL = 1024
ATT_WIDTH = 512
ATT_HEAD_DIM = 64
ATT_KV_HEADS = 2
ATT_GROUP = 4
WINDOW = 128
ROPE_DIM = 16
ROPE_THETA = 500000.0
HG_WIDTH = 512
HG_HEAD_DIM = 128
HG_HEADS = 4
HG_CHUNK = 32
IN_COLS = 2816
ATT_COLS = 768
D_FF = 4096
EPS = 1e-6
N_DEV = 8

ADAM_LR = 0.001
ADAM_B1 = 0.9
ADAM_B2 = 0.999
ADAM_EPS = 1e-08
ADAM_WD = 0.01
ADAM_STEP = 10

VMEM_LIMIT_BIG = 56 << 20
LANES = 128

MESH = pl.DeviceIdType.MESH
NT_DIMS = (((1,), (1,)), ((), ()))
TN_DIMS = (((0,), (0,)), ((), ()))


def _dot(a, b):
    return jnp.dot(a, b, preferred_element_type=F32)


def _dot_nt(a, b):
    return lax.dot_general(a, b, NT_DIMS, preferred_element_type=F32)


def _dot_tn(a, b):
    return lax.dot_general(a, b, TN_DIMS, preferred_element_type=F32)


def _bf(a):
    return a.astype(BF16)


def _sigmoid(a):
    return 0.5 * jnp.tanh(0.5 * a) + 0.5


def _mean_last(a):
    return jnp.mean(a, axis=-1, keepdims=True)


def _sum_rows(a):
    return jnp.sum(a, axis=0, keepdims=True)


def _loop_pairs(first, count, body, init, per_trip=2):
    if count % per_trip:
        return lax.fori_loop(first, first + count, body, init)

    def trip(i, c):
        for k in range(per_trip):
            c = body(first + per_trip * i + k, c)
        return c

    return lax.fori_loop(0, count // per_trip, trip, init)


def _params(sem=None, vmem=None):
    kw = {}
    if sem is not None:
        kw["dimension_semantics"] = sem
    if vmem is not None:
        kw["vmem_limit_bytes"] = vmem
    return pltpu.CompilerParams(**kw)


ANY_SPEC = pl.BlockSpec(memory_space=pl.ANY)


def _exchange_shapes(srcs, modes):
    out_shape = []
    for s, m in zip(srcs, modes):
        shp = (N_DEV,) + tuple(s.shape) if m == "gather" else tuple(s.shape)
        out_shape.append(SDS(shp, s.dtype))
    return out_shape


def _exchange_sems(n):
    if n == 0:
        return []
    return [pltpu.SemaphoreType.DMA((n, N_DEV - 1)), pltpu.SemaphoreType.DMA((n, N_DEV - 1)),
            pltpu.SemaphoreType.DMA((n,))]


SIBLING = 1
OTHER_CHIPS = (2, 4, 6)


def _related(k):
    x, y, c = lax.axis_index("x"), lax.axis_index("y"), lax.axis_index("c")
    px, py, pc = x ^ ((k >> 2) & 1), y ^ ((k >> 1) & 1), c ^ (k & 1)
    return (px, py, pc), 4 * px + 2 * py + pc


def _exchange_phases(modes, src_refs, out_refs, send_sems, recv_sems, own_sems):
    _, me = _related(0)
    sib_dev, sib = _related(SIBLING)
    start, middle, end = [], [], []

    def remote(a, i, src, dst, dev):
        return pltpu.make_async_remote_copy(src_ref=src, dst_ref=dst, send_sem=send_sems.at[a, i],
                                            recv_sem=recv_sems.at[a, i], device_id=dev, device_id_type=MESH)

    for a, mode in enumerate(modes):
        out = out_refs[a]
        if mode == "gather":
            src = src_refs[a]
            own = pltpu.make_async_copy(src, out.at[me], own_sems.at[a])
            to_sib = remote(a, 0, src, out.at[me], sib_dev)
            start += [own.start, to_sib.start]
            end += [remote(a, 0, src, out.at[sib], sib_dev).wait_recv, to_sib.wait_send, own.wait]
            for j, k in enumerate(OTHER_CHIPS, start=1):
                dev, peer = _related(k)
                _, peer_sib = _related(k ^ SIBLING)
                send = remote(a, j, src, out.at[me], dev)
                passed = remote(a, 3 + j, out.at[peer], out.at[peer], sib_dev)
                start.append(send.start)
                middle += [remote(a, j, src, out.at[peer], dev).wait_recv, passed.start]
                end += [remote(a, 3 + j, out.at[peer_sib], out.at[peer_sib], sib_dev).wait_recv,
                        send.wait_send, passed.wait_send]
        elif mode == "chips":
            chip = me // 2
            own = pltpu.make_async_copy(src_refs[a].at[chip], out.at[chip], own_sems.at[a])
            start.append(own.start)
            end.append(own.wait)
            for j, k in enumerate(OTHER_CHIPS, start=1):
                dev, peer = _related(k)
                send = remote(a, j, src_refs[a].at[peer // 2], out.at[chip], dev)
                start.append(send.start)
                end += [remote(a, j, src_refs[a].at[peer // 2], out.at[peer // 2], dev).wait_recv, send.wait_send]
        else:
            own = pltpu.make_async_copy(src_refs[a].at[me], out.at[me], own_sems.at[a])
            start.append(own.start)
            end.append(own.wait)
            for k in range(1, N_DEV):
                dev, peer = _related(k)
                send = remote(a, k - 1, src_refs[a].at[peer], out.at[me], dev)
                start.append(send.start)
                end += [remote(a, k - 1, src_refs[a].at[peer], out.at[peer], dev).wait_recv, send.wait_send]
    return start, middle, end


def _run(actions):
    for act in actions:
        act()


def _exchange(name, srcs, modes):
    n = len(srcs)

    def body(*refs):
        start, middle, end = _exchange_phases(modes, refs[:n], refs[n:2 * n], *refs[2 * n:])
        _run(start)
        _run(middle)
        _run(end)

    return pl.pallas_call(
        body, name=name, out_shape=_exchange_shapes(srcs, modes),
        in_specs=[ANY_SPEC] * n, out_specs=[ANY_SPEC] * n,
        scratch_shapes=_exchange_sems(n),
    )(*srcs)


def _ride_start(modes, step, steps, src_refs, out_refs, sems):
    if not modes:
        return
    middle_step = steps - 1

    @pl.when(step == 0)
    def _():
        _run(_exchange_phases(modes, src_refs, out_refs, *sems)[0])

    if "gather" in modes:
        @pl.when(step == middle_step)
        def _():
            _run(_exchange_phases(modes, src_refs, out_refs, *sems)[1])


def _ride_wait(modes, step, steps, src_refs, out_refs, sems):
    if not modes:
        return

    @pl.when(step == steps - 1)
    def _():
        _run(_exchange_phases(modes, src_refs, out_refs, *sems)[2])


def _ada_mod(c_all, w_ada, b_ada_mine):
    nb, cols = c_all.shape[0], w_ada.shape[1]

    def body(c_ref, w_ref, b_ref, o_ref):
        cv = c_ref[...]
        ca = cv * _sigmoid(cv)
        o_ref[...] = _dot(ca, w_ref[...]) + b_ref[...]

    return pl.pallas_call(body, name="ada_mod", out_shape=SDS((nb, cols), F32))(c_all, w_ada, b_ada_mine)


def _tile_rows(T, big=False):
    return min(512 if big else 256, T)


def _mod_spec(tps):
    return pl.BlockSpec((None, 8, D_MODEL), lambda i: (i // tps, 0, 0))


def _in_proj(x2, mod8, pre_w, w_in_bf, T, ride_srcs, ride_modes):
    N = x2.shape[0]
    TM = _tile_rows(T, big=True)
    tps = T // TM
    nr = len(ride_srcs)

    def body(*refs):
        x_ref, mod_ref, pw_ref, w_ref = refs[:4]
        ride_in = refs[4:4 + nr]
        pa_ref, ph_ref, h1_ref = refs[4 + nr:7 + nr]
        ride_out = refs[7 + nr:7 + 2 * nr]
        sems = refs[7 + 2 * nr:]
        _ride_start(ride_modes, pl.program_id(0), N // TM, ride_in, ride_out, sems)
        x = x_ref[...]
        r = lax.rsqrt(_mean_last(x * x) + EPS)
        h = (x * r * pw_ref[...]) * (1.0 + mod_ref[1:2, :]) + mod_ref[0:1, :]
        hb = _bf(h)
        h1_ref[...] = hb
        pa_ref[...] = _dot_nt(hb, w_ref[:ATT_COLS, :])
        ph_ref[...] = _dot_nt(hb, w_ref[ATT_COLS:, :])
        _ride_wait(ride_modes, pl.program_id(0), N // TM, ride_in, ride_out, sems)

    return pl.pallas_call(
        body, name="in_proj", grid=(N // TM,),
        in_specs=[pl.BlockSpec((TM, D_MODEL), lambda i: (i, 0)), _mod_spec(tps),
                  pl.BlockSpec((1, D_MODEL), lambda i: (0, 0)),
                  pl.BlockSpec((IN_COLS, D_MODEL), lambda i: (0, 0))] + [ANY_SPEC] * nr,
        out_specs=[pl.BlockSpec((TM, ATT_COLS), lambda i: (i, 0)),
                   pl.BlockSpec((TM, IN_COLS - ATT_COLS), lambda i: (i, 0)),
                   pl.BlockSpec((TM, D_MODEL), lambda i: (i, 0))] + [ANY_SPEC] * nr,
        out_shape=[SDS((N, ATT_COLS), F32), SDS((N, IN_COLS - ATT_COLS), F32), SDS((N, D_MODEL), BF16)]
        + _exchange_shapes(ride_srcs, ride_modes),
        scratch_shapes=_exchange_sems(nr),
        compiler_params=_params(("arbitrary",), VMEM_LIMIT_BIG),
    )(x2, mod8, pre_w, w_in_bf, *ride_srcs)


def _rope_tables(T):
    half = ROPE_DIM // 2
    f32 = np.float32
    inv_freq = (f32(ROPE_THETA) ** (-np.arange(0, ROPE_DIM, 2, dtype=f32) / f32(ROPE_DIM))).astype(f32)
    ang = np.arange(T, dtype=f32)[:, None] * inv_freq[None, :]
    cos, sin = np.cos(ang).astype(f32), np.sin(ang).astype(f32)
    ones = np.ones((T, ATT_HEAD_DIM - ROPE_DIM), f32)
    zeros = np.zeros((T, ATT_HEAD_DIM - ROPE_DIM), f32)
    zh = np.zeros((T, half), f32)
    cos64 = np.concatenate([cos, cos, ones], axis=1)
    sin_left = np.concatenate([-sin, zh, zeros], axis=1)
    sin_right = np.concatenate([zh, sin, zeros], axis=1)
    rep = LANES // ATT_HEAD_DIM
    return tuple(jnp.asarray(np.tile(t, (1, rep))) for t in (cos64, sin_left, sin_right))


def _rope(xc, cs, sl, sr):
    return xc * cs + pltpu.roll(xc, LANES - 8, 1) * sl + pltpu.roll(xc, 8, 1) * sr


def _rope_t(dy, cs, sl, sr):
    return dy * cs + pltpu.roll(dy * sl, 8, 1) + pltpu.roll(dy * sr, LANES - 8, 1)


ATT_SCALE = ATT_HEAD_DIM ** -0.5
ATT_SPLITS = 4


def _lower_mask():
    j = lax.broadcasted_iota(jnp.int32, (WINDOW, ATT_GROUP * WINDOW), 0)
    i = lax.broadcasted_iota(jnp.int32, (WINDOW, ATT_GROUP * WINDOW), 1) & (WINDOW - 1)
    return j <= i


def _sink_row(sink_ref, hk):
    return jnp.concatenate(
        [jnp.full((1, WINDOW), sink_ref[0, ATT_GROUP * hk + g], F32) for g in range(ATT_GROUP)], axis=1)


def _softmax_window(qs, k_cur, k_prev, lower, has_prev, sink):
    s_prev = jnp.where(has_prev, _dot_nt(k_prev, qs), jnp.finfo(F32).min)
    s = jnp.where(lower, _dot_nt(k_cur, qs), s_prev)
    m = jnp.maximum(jnp.max(s, axis=0, keepdims=True), sink)
    p = jnp.exp(s - m)
    es = jnp.exp(sink - m)
    inv = 1.0 / (jnp.sum(p, axis=0, keepdims=True) + es)
    return p, inv, es


def _stack_heads(parts, hk):
    hs = []
    for g in range(ATT_GROUP):
        h = ATT_GROUP * hk + g
        hs.append(parts[h // 2][:, (h % 2) * ATT_HEAD_DIM:(h % 2 + 1) * ATT_HEAD_DIM])
    return jnp.concatenate(hs, axis=0)


def _attn_fwd(proj3, tables, sinks, attn_w, ride_srcs, ride_modes):
    B, T, _ = proj3.shape
    nb = T // WINDOW
    splits = min(ATT_SPLITS, nb)
    per = nb // splits
    nr = len(ride_srcs)
    cos, sinl, sinr = tables

    def body(*refs):
        q_ref, k_ref, v_ref, cos_ref, sl_ref, sr_ref, sink_ref, aw_ref = refs[:8]
        ride_in = refs[8:8 + nr]
        o_ref, an_ref, qr_ref, kr_ref = refs[8 + nr:12 + nr]
        ride_out = refs[12 + nr:12 + 2 * nr]
        kpad, vpad = refs[12 + 2 * nr:14 + 2 * nr]
        sems = refs[14 + 2 * nr:]
        part = pl.program_id(1)
        step = pl.program_id(0) * splits + part
        _ride_start(ride_modes, step, B * splits, ride_in, ride_out, sems)

        @pl.when(part == 0)
        def _():
            kpad[0:WINDOW, :] = jnp.zeros((WINDOW, LANES), BF16)
            vpad[0:WINDOW, :] = jnp.zeros((WINDOW, LANES), BF16)

        lower = _lower_mask()

        def block(n, carry):
            r0 = pl.multiple_of(n * WINDOW, WINDOW)
            rows = pl.ds(r0, WINDOW)
            nxt = pl.ds(r0 + WINDOW, WINDOW)
            cs, sl, sr = cos_ref[rows, :], sl_ref[rows, :], sr_ref[rows, :]
            kb = _bf(_rope(k_ref[rows, :], cs, sl, sr))
            vb = _bf(v_ref[rows, :])
            kpad[nxt, :] = kb
            kr_ref[rows, :] = kb
            vpad[nxt, :] = vb
            qparts = []
            for j in range(ATT_WIDTH // LANES):
                qp = _bf(_rope(q_ref[rows, j * LANES:(j + 1) * LANES], cs, sl, sr) * ATT_SCALE)
                qr_ref[rows, j * LANES:(j + 1) * LANES] = qp
                qparts.append(qp)
            for hk in range(ATT_KV_HEADS):
                lanes = slice(hk * ATT_HEAD_DIM, (hk + 1) * ATT_HEAD_DIM)
                qs = _stack_heads(qparts, hk)
                p, inv, _ = _softmax_window(qs, kb[:, lanes], kpad[rows, lanes], lower, n > 0,
                                            _sink_row(sink_ref, hk))
                p_cur = jnp.where(lower, p, 0.0)
                ot = (_dot_tn(vb[:, lanes], _bf(p_cur)) + _dot_tn(vpad[rows, lanes], _bf(p - p_cur))) * inv
                for g in range(ATT_GROUP):
                    h = ATT_GROUP * hk + g
                    o_ref[rows, h * ATT_HEAD_DIM:(h + 1) * ATT_HEAD_DIM] = ot[:, g * WINDOW:(g + 1) * WINDOW].T
            ob = o_ref[rows, :]
            an_ref[rows, :] = _bf(ob * lax.rsqrt(_mean_last(ob * ob) + EPS) * aw_ref[...])
            return carry

        _loop_pairs(part * per, per, block, 0)
        _ride_wait(ride_modes, step, B * splits, ride_in, ride_out, sems)

    seq = lambda w, j: pl.BlockSpec((None, T, w), lambda b, s: (b, 0, j))
    full = lambda r, w: pl.BlockSpec((r, w), lambda b, s: (0, 0))
    return pl.pallas_call(
        body, name="attn_fwd", grid=(B, splits),
        in_specs=[seq(ATT_WIDTH, 0), seq(LANES, 4), seq(LANES, 5),
                  full(T, LANES), full(T, LANES), full(T, LANES),
                  pl.BlockSpec(memory_space=pltpu.SMEM), full(1, ATT_WIDTH)] + [ANY_SPEC] * nr,
        out_specs=[seq(ATT_WIDTH, 0), seq(ATT_WIDTH, 0), seq(ATT_WIDTH, 0), seq(LANES, 0)] + [ANY_SPEC] * nr,
        out_shape=[SDS((B, T, ATT_WIDTH), F32), SDS((B, T, ATT_WIDTH), BF16),
                   SDS((B, T, ATT_WIDTH), BF16), SDS((B, T, LANES), BF16)] + _exchange_shapes(ride_srcs, ride_modes),
        scratch_shapes=[pltpu.VMEM((T + WINDOW, LANES), BF16), pltpu.VMEM((T + WINDOW, LANES), BF16)]
        + _exchange_sems(nr),
        compiler_params=_params(("arbitrary", "arbitrary"), VMEM_LIMIT_BIG),
    )(proj3, proj3, proj3, cos, sinl, sinr, sinks, attn_w, *ride_srcs)


HG_GROUP = 8
HG_ROWS = HG_GROUP * HG_CHUNK


HG_STACK = HG_GROUP * HG_HEAD_DIM


def _group_mask():
    r = lax.broadcasted_iota(jnp.int32, (HG_ROWS, HG_ROWS), 0)
    c = lax.broadcasted_iota(jnp.int32, (HG_ROWS, HG_ROWS), 1)
    return ((r // HG_CHUNK) == (c // HG_CHUNK)) & (r >= c)


def _spread(a):
    blocks = []
    for c in range(HG_GROUP):
        above = jnp.zeros((c * HG_CHUNK, HG_HEAD_DIM), a.dtype)
        below = jnp.zeros(((HG_GROUP - 1 - c) * HG_CHUNK, HG_HEAD_DIM), a.dtype)
        blocks.append(jnp.concatenate([p for p in (above, a[_chunk_rows(c), :], below) if p.shape[0]], axis=0))
    return jnp.concatenate(blocks, axis=1)


def _pick(r):
    return jnp.concatenate([r[_chunk_rows(c), c * HG_HEAD_DIM:(c + 1) * HG_HEAD_DIM] for c in range(HG_GROUP)], axis=0)


def _lane_block(a, c):
    return a[:, c * HG_HEAD_DIM:(c + 1) * HG_HEAD_DIM]


def _chunk_cumsum(a, reverse=False):
    n = a.shape[0]
    pos = lax.broadcasted_iota(jnp.int32, a.shape, 0) % HG_CHUNK
    shift = 1
    while shift < HG_CHUNK:
        if reverse:
            a = a + jnp.where(pos < HG_CHUNK - shift, pltpu.roll(a, n - shift, 0), 0.0)
        else:
            a = a + jnp.where(pos >= shift, pltpu.roll(a, shift, 0), 0.0)
        shift *= 2
    return a


def _chunk_bcast(rows_1x128):
    return jnp.concatenate([jnp.broadcast_to(r, (HG_CHUNK, HG_HEAD_DIM)) for r in rows_1x128], axis=0)


def _hgrn_gates(hq, hf, lb):
    sq = _sigmoid(hq)
    q = hq * sq
    sg = _sigmoid(hf)
    f = lb + (1.0 - lb) * sg
    k = 1.0 - f
    logf = jnp.log(f)
    b = _chunk_cumsum(logf)
    bl = [_sum_rows(logf[_chunk_rows(c), :]) for c in range(HG_GROUP)]
    eb, enb, e2 = jnp.exp(b), jnp.exp(-b), jnp.exp(_chunk_bcast(bl) - b)
    ebl = [jnp.exp(r) for r in bl]
    return dict(sq=sq, sg=sg, f=f, eb=eb, enb=enb, e2=e2, ebl=ebl, qd=q * eb, kd=k * enb, k2=k * e2)


def _chunk_rows(c):
    return slice(c * HG_CHUNK, (c + 1) * HG_CHUNK)


def _head_lanes(h):
    return slice(h * HG_HEAD_DIM, (h + 1) * HG_HEAD_DIM)


def _hgrn_fwd(proj_h, lb, hg_w, ride_srcs, ride_modes):
    B, T, _ = proj_h.shape
    ng = T // HG_ROWS
    nr = len(ride_srcs)

    def body(*refs):
        hq_ref, hf_ref, hi_ref, hg_ref, lb_ref, gw_ref = refs[:6]
        ride_in = refs[6:6 + nr]
        o_ref, rg_ref, sp_ref = refs[6 + nr:9 + nr]
        ride_out = refs[9 + nr:9 + 2 * nr]
        st = refs[9 + 2 * nr]
        sems = refs[10 + 2 * nr:]
        gi = pl.program_id(1)
        step = pl.program_id(0) * ng + gi
        _ride_start(ride_modes, step, B * ng, ride_in, ride_out, sems)

        @pl.when(gi == 0)
        def _():
            st[...] = jnp.zeros(st.shape, F32)

        lo = _group_mask()
        for h in range(HG_HEADS):
            lanes = _head_lanes(h)
            gt = _hgrn_gates(hq_ref[:, lanes], hf_ref[:, lanes], lb_ref[:, lanes])
            v, qd, kd = _bf(hi_ref[:, lanes]), _bf(gt["qd"]), _bf(gt["kd"])
            a = jnp.where(lo, _dot_nt(qd, kd), 0.0)
            kv = _dot_tn(v, _spread(_bf(gt["k2"])))
            s = st[h]
            before = []
            for c in range(HG_GROUP):
                before.append(s)
                s = s * gt["ebl"][c] + _lane_block(kv, c)
            st[h] = s
            sp = jnp.concatenate(before, axis=1)
            sp_ref[h] = sp
            o = _dot(_bf(a), v) + _dot_nt(_spread(qd), _bf(sp))
            o_ref[:, lanes] = o
            hg = hg_ref[:, lanes]
            rn = o * lax.rsqrt(_mean_last(o * o) + EPS) * gw_ref[...]
            rg_ref[:, lanes] = _bf(rn * (hg * _sigmoid(hg)))
        _ride_wait(ride_modes, step, B * ng, ride_in, ride_out, sems)

    part = lambda j: pl.BlockSpec((None, HG_ROWS, HG_WIDTH), lambda b, g: (b, g, j))
    return pl.pallas_call(
        body, name="hgrn_fwd", grid=(B, ng),
        in_specs=[part(0), part(1), part(2), part(3),
                  pl.BlockSpec((1, HG_WIDTH), lambda b, g: (0, 0)),
                  pl.BlockSpec((1, LANES), lambda b, g: (0, 0))] + [ANY_SPEC] * nr,
        out_specs=[part(0), part(0),
                   pl.BlockSpec((None, HG_HEADS, None, HG_HEAD_DIM, HG_STACK), lambda b, g: (b, 0, g, 0, 0))]
        + [ANY_SPEC] * nr,
        out_shape=[SDS((B, T, HG_WIDTH), F32), SDS((B, T, HG_WIDTH), BF16),
                   SDS((B, HG_HEADS, ng, HG_HEAD_DIM, HG_STACK), F32)] + _exchange_shapes(ride_srcs, ride_modes),
        scratch_shapes=[pltpu.VMEM((HG_HEADS, HG_HEAD_DIM, HG_HEAD_DIM), F32)] + _exchange_sems(nr),
        compiler_params=_params(("arbitrary", "arbitrary"), VMEM_LIMIT_BIG),
    )(proj_h, proj_h, proj_h, proj_h, lb, hg_w, *ride_srcs)


def _mix_out(x2, attn_n, rec_g, mod8, post_w, w_out_bf, T, ride_srcs, ride_modes):
    N = x2.shape[0]
    TM = _tile_rows(T, big=True)
    tps = T // TM
    nr = len(ride_srcs)

    def body(*refs):
        x_ref, an_ref, rg_ref, mod_ref, pw_ref, w_ref = refs[:6]
        ride_in = refs[6:6 + nr]
        mix_ref, x1_ref, cat_ref = refs[6 + nr:9 + nr]
        ride_out = refs[9 + nr:9 + 2 * nr]
        sems = refs[9 + 2 * nr:]
        _ride_start(ride_modes, pl.program_id(0), N // TM, ride_in, ride_out, sems)
        cat = jnp.concatenate([an_ref[...], rg_ref[...]], axis=1)
        cat_ref[...] = cat
        mix = _dot(cat, w_ref[...])
        mix_ref[...] = mix
        r = lax.rsqrt(_mean_last(mix * mix) + EPS)
        x1_ref[...] = x_ref[...] + mod_ref[2:3, :] * (mix * r * pw_ref[...])
        _ride_wait(ride_modes, pl.program_id(0), N // TM, ride_in, ride_out, sems)

    row = lambda w: pl.BlockSpec((TM, w), lambda i: (i, 0))
    return pl.pallas_call(
        body, name="mix_out", grid=(N // TM,),
        in_specs=[row(D_MODEL), row(ATT_WIDTH), row(HG_WIDTH), _mod_spec(tps),
                  pl.BlockSpec((1, D_MODEL), lambda i: (0, 0)),
                  pl.BlockSpec((D_MODEL, D_MODEL), lambda i: (0, 0))] + [ANY_SPEC] * nr,
        out_specs=[row(D_MODEL), row(D_MODEL), row(D_MODEL)] + [ANY_SPEC] * nr,
        out_shape=[SDS((N, D_MODEL), F32), SDS((N, D_MODEL), F32), SDS((N, D_MODEL), BF16)]
        + _exchange_shapes(ride_srcs, ride_modes),
        scratch_shapes=_exchange_sems(nr),
        compiler_params=_params(("arbitrary",), VMEM_LIMIT_BIG),
    )(x2, attn_n, rec_g, mod8, post_w, w_out_bf, *ride_srcs)


def _load_weights_once(pairs, sem):
    @pl.when(pl.program_id(0) == 0)
    def _():
        cps = [pltpu.make_async_copy(src, dst, sem.at[i]) for i, (src, dst) in enumerate(pairs)]
        for cp in cps:
            cp.start()
        for cp in cps:
            cp.wait()


MLP_HALF = D_MODEL // 2
MLP_PIECES = 2 * N_DEV + 2


def _mlp_weight_pieces(wu_a, wu_b, wd_a, wd_b, wu, wd):
    cols = D_FF // N_DEV
    pairs = []
    for h, half in enumerate((wu_a, wu_b)):
        for j in range(N_DEV):
            pairs.append((half.at[j], wu.at[pl.ds(h * MLP_HALF, MLP_HALF), pl.ds(j * cols, cols)]))
    for h, half in enumerate((wd_a, wd_b)):
        pairs.append((half, wd.at[:, pl.ds(h * MLP_HALF, MLP_HALF)]))
    return pairs


def _mlp_fwd(x1, mod8, pre_w, w_up_halves, w_down_halves, T):
    N = x1.shape[0]
    TM = _tile_rows(T)
    tps = T // TM

    def body(x_ref, mod_ref, pw_ref, wua, wub, wda, wdb, up_ref, u_ref, d_ref, h2_ref, wu, wd, sem):
        _load_weights_once(_mlp_weight_pieces(wua, wub, wda, wdb, wu, wd), sem)
        x = x_ref[...]
        r = lax.rsqrt(_mean_last(x * x) + EPS)
        h = (x * r * pw_ref[...]) * (1.0 + mod_ref[4:5, :]) + mod_ref[3:4, :]
        hb = _bf(h)
        h2_ref[...] = hb
        up = _dot(hb, wu[...])
        up_ref[...] = up
        ru = jnp.maximum(up, 0.0)
        u = _bf(ru * ru)
        u_ref[...] = u
        d_ref[...] = _dot(u, wd[...])

    row = lambda w: pl.BlockSpec((TM, w), lambda i: (i, 0))
    return pl.pallas_call(
        body, name="mlp_fwd", grid=(N // TM,),
        in_specs=[row(D_MODEL), _mod_spec(tps), pl.BlockSpec((1, D_MODEL), lambda i: (0, 0))] + [ANY_SPEC] * 4,
        out_specs=[row(D_FF), row(D_FF), row(D_MODEL), row(D_MODEL)],
        out_shape=[SDS((N, D_FF), F32), SDS((N, D_FF), BF16), SDS((N, D_MODEL), F32), SDS((N, D_MODEL), BF16)],
        scratch_shapes=[pltpu.VMEM((D_MODEL, D_FF), BF16), pltpu.VMEM((D_FF, D_MODEL), BF16),
                        pltpu.SemaphoreType.DMA((MLP_PIECES,))],
        compiler_params=_params(("arbitrary",), VMEM_LIMIT_BIG),
    )(x1, mod8, pre_w, *w_up_halves, *w_down_halves)


def _acc_rows(acc_ref, first, rows):
    @pl.when(first)
    def _():
        acc_ref[...] = jnp.zeros(acc_ref.shape, F32)
    for i, r in enumerate(rows):
        acc_ref[i:i + 1, :] += r


def _mlp_bwd(x1, d, up, tgt, mod8, pre_w, post_w, w_up_halves, w_down_halves, T):
    N = x1.shape[0]
    TM = _tile_rows(T)
    tps = T // TM

    def body(x_ref, d_ref, up_ref, t_ref, mod_ref, pw_ref, qw_ref, wua, wub, wda, wdb,
             dx_ref, dup_ref, dd_ref, acc_ref, wd, wu, sem):
        _load_weights_once(_mlp_weight_pieces(wua, wub, wda, wdb, wu, wd), sem)
        sh2, sc2, g2 = mod_ref[3:4, :], mod_ref[4:5, :], mod_ref[5:6, :]
        x = x_ref[...]
        r1 = lax.rsqrt(_mean_last(x * x) + EPS)
        xh = x * r1
        n2 = xh * pw_ref[...]
        dv = d_ref[...]
        rd = lax.rsqrt(_mean_last(dv * dv) + EPS)
        dh = dv * rd
        rr = dh * qw_ref[...]
        e = x + g2 * rr - t_ref[...]
        loss = 0.5 * jnp.sum(_sum_rows(e * e), axis=1, keepdims=True) / D_MODEL
        dy = e * (1.0 / D_MODEL)
        dg2 = _sum_rows(dy * rr)
        drr = dy * g2
        dw_post = _sum_rows(drr * dh)
        ddh = drr * qw_ref[...]
        dd = _bf(rd * (ddh - dh * _mean_last(ddh * dh)))
        dd_ref[...] = dd
        ru = jnp.maximum(up_ref[...], 0.0)
        dup = _bf(_dot_nt(dd, wd[...]) * (2.0 * ru))
        dup_ref[...] = dup
        dh2 = _dot_nt(dup, wu[...])
        dsh2 = _sum_rows(dh2)
        dsc2 = _sum_rows(dh2 * n2)
        dn2 = dh2 * (1.0 + sc2)
        dw_pre = _sum_rows(dn2 * xh)
        dxh = dn2 * pw_ref[...]
        dx_ref[...] = dy + r1 * (dxh - xh * _mean_last(dxh * xh))
        _acc_rows(acc_ref, pl.program_id(0) % tps == 0,
                  [dsh2, dsc2, dg2, dw_pre, dw_post, jnp.broadcast_to(loss, (1, D_MODEL))])

    row = lambda w: pl.BlockSpec((TM, w), lambda i: (i, 0))
    vec = pl.BlockSpec((1, D_MODEL), lambda i: (0, 0))
    B = N // T
    return pl.pallas_call(
        body, name="mlp_bwd", grid=(N // TM,),
        in_specs=[row(D_MODEL), row(D_MODEL), row(D_FF), row(D_MODEL), _mod_spec(tps), vec, vec] + [ANY_SPEC] * 4,
        out_specs=[row(D_MODEL), row(D_FF), row(D_MODEL), _mod_spec(tps)],
        out_shape=[SDS((N, D_MODEL), F32), SDS((N, D_FF), BF16), SDS((N, D_MODEL), BF16),
                   SDS((B, 8, D_MODEL), F32)],
        scratch_shapes=[pltpu.VMEM((D_FF, D_MODEL), BF16), pltpu.VMEM((D_MODEL, D_FF), BF16),
                        pltpu.SemaphoreType.DMA((MLP_PIECES,))],
        compiler_params=_params(("arbitrary",), VMEM_LIMIT_BIG),
    )(x1, d, up, tgt, mod8, pre_w, post_w, *w_up_halves, *w_down_halves)


def _mix_bwd(mix, dx1, mod8, post_w, w_out_bf, T, ride_srcs, ride_modes):
    N = mix.shape[0]
    TM = _tile_rows(T, big=True)
    tps = T // TM
    nr = len(ride_srcs)

    def body(*refs):
        mix_ref, dx_ref, mod_ref, pw_ref, w_ref = refs[:5]
        ride_in = refs[5:5 + nr]
        dan_ref, drg_ref, dmix_ref, acc_ref = refs[5 + nr:9 + nr]
        ride_out = refs[9 + nr:9 + 2 * nr]
        sems = refs[9 + 2 * nr:]
        _ride_start(ride_modes, pl.program_id(0), N // TM, ride_in, ride_out, sems)
        g1 = mod_ref[2:3, :]
        mix = mix_ref[...]
        dx1 = dx_ref[...]
        rm = lax.rsqrt(_mean_last(mix * mix) + EPS)
        mh = mix * rm
        dg1 = _sum_rows(dx1 * (mh * pw_ref[...]))
        dr = dx1 * g1
        dw_post = _sum_rows(dr * mh)
        dmh = dr * pw_ref[...]
        dmix = _bf(rm * (dmh - mh * _mean_last(dmh * mh)))
        dmix_ref[...] = dmix
        dcat = _dot_nt(dmix, w_ref[...])
        dan_ref[...] = dcat[:, :ATT_WIDTH]
        drg_ref[...] = dcat[:, ATT_WIDTH:]
        _acc_rows(acc_ref, pl.program_id(0) % tps == 0, [dg1, dw_post])
        _ride_wait(ride_modes, pl.program_id(0), N // TM, ride_in, ride_out, sems)

    row = lambda w: pl.BlockSpec((TM, w), lambda i: (i, 0))
    B = N // T
    return pl.pallas_call(
        body, name="mix_bwd", grid=(N // TM,),
        in_specs=[row(D_MODEL), row(D_MODEL), _mod_spec(tps), pl.BlockSpec((1, D_MODEL), lambda i: (0, 0)),
                  pl.BlockSpec((D_MODEL, D_MODEL), lambda i: (0, 0))] + [ANY_SPEC] * nr,
        out_specs=[row(ATT_WIDTH), row(HG_WIDTH), row(D_MODEL), _mod_spec(tps)] + [ANY_SPEC] * nr,
        out_shape=[SDS((N, ATT_WIDTH), F32), SDS((N, HG_WIDTH), F32), SDS((N, D_MODEL), BF16),
                   SDS((B, 8, D_MODEL), F32)] + _exchange_shapes(ride_srcs, ride_modes),
        scratch_shapes=_exchange_sems(nr),
        compiler_params=_params(("arbitrary",), VMEM_LIMIT_BIG),
    )(mix, dx1, mod8, post_w, w_out_bf, *ride_srcs)


def _hgrn_bwd(proj_h, lb, hg_w, o, s_prev, drg, ride_srcs, ride_modes):
    B, T, _ = proj_h.shape
    ng = T // HG_ROWS
    nr = len(ride_srcs)

    def body(*refs):
        hq_ref, hf_ref, hi_ref, hg_ref, lb_ref, gw_ref, o_ref, sp_ref, drg_ref = refs[:9]
        ride_in = refs[9:9 + nr]
        dhq_ref, dhf_ref, dhi_ref, dhg_ref, dlb_ref, dgw_ref = refs[9 + nr:15 + nr]
        ride_out = refs[15 + nr:15 + 2 * nr]
        dst = refs[15 + 2 * nr]
        sems = refs[16 + 2 * nr:]
        step = pl.program_id(0) * ng + pl.program_id(1)
        _ride_start(ride_modes, step, B * ng, ride_in, ride_out, sems)

        @pl.when(pl.program_id(1) == 0)
        def _():
            dst[...] = jnp.zeros(dst.shape, F32)
            dlb_ref[...] = jnp.zeros(dlb_ref.shape, F32)
            dgw_ref[...] = jnp.zeros(dgw_ref.shape, F32)

        lo = _group_mask()
        gw = gw_ref[...]

        for h in range(HG_HEADS):
            lanes = _head_lanes(h)
            lbv = lb_ref[:, lanes]
            hq = hq_ref[:, lanes]
            gt = _hgrn_gates(hq, hf_ref[:, lanes], lbv)
            sq, sg, qdf, kdf, k2f, ebl = gt["sq"], gt["sg"], gt["qd"], gt["kd"], gt["k2"], gt["ebl"]
            v, qd, kd = _bf(hi_ref[:, lanes]), _bf(qdf), _bf(kdf)
            ov = o_ref[:, lanes]
            hg = hg_ref[:, lanes]
            shg = _sigmoid(hg)
            dr = drg_ref[:, lanes]
            ro = lax.rsqrt(_mean_last(ov * ov) + EPS)
            oh = ov * ro
            dhg_ref[:, lanes] = _bf(dr * (oh * gw) * (shg + hg * shg * (1.0 - shg)))
            drn = dr * (hg * shg)
            dgw_ref[...] += jnp.broadcast_to(_sum_rows(drn * oh), (8, LANES))
            doh = drn * gw
            do = _bf(ro * (doh - oh * _mean_last(doh * oh)))
            a = jnp.where(lo, _dot_nt(qd, kd), 0.0)
            da = _bf(jnp.where(lo, _dot_nt(do, v), 0.0))
            dv = _dot_tn(_bf(a), do)
            dqd = _dot(da, kd)
            dkd = _dot_tn(da, qd)
            sp = sp_ref[h]
            incr = _dot_tn(do, _spread(qd))
            ds = dst[h]
            after = [None] * HG_GROUP
            for c in reversed(range(HG_GROUP)):
                after[c] = ds
                ds = ds * ebl[c] + _lane_block(incr, c)
            dst[h] = ds
            dss = jnp.concatenate(after, axis=1)
            dssb = _bf(dss)
            dk2 = _pick(_dot(v, dssb))
            dhi_ref[:, lanes] = _bf(dv + _dot_nt(_spread(_bf(k2f)), dssb))
            dqd = dqd + _pick(_dot(do, _bf(sp)))
            debl = _sum_rows(dss * sp)
            k2g = dk2 * k2f
            db = dqd * qdf - dkd * kdf - k2g
            dk = dkd * gt["enb"] + dk2 * gt["e2"]
            dbl = _chunk_bcast([_lane_block(debl, c) * ebl[c] + _sum_rows(k2g[_chunk_rows(c), :])
                                for c in range(HG_GROUP)])
            dg = _chunk_cumsum(db, reverse=True) + dbl
            df = dg / gt["f"] - dk
            dhf_ref[:, lanes] = _bf(df * (1.0 - lbv) * sg * (1.0 - sg))
            dlb_ref[:, lanes] += jnp.broadcast_to(_sum_rows(df * (1.0 - sg)), (8, LANES))
            dhq_ref[:, lanes] = _bf((dqd * gt["eb"]) * (sq + hq * sq * (1.0 - sq)))
        _ride_wait(ride_modes, step, B * ng, ride_in, ride_out, sems)

    part = lambda j: pl.BlockSpec((None, HG_ROWS, HG_WIDTH), lambda b, g: (b, ng - 1 - g, j))
    return pl.pallas_call(
        body, name="hgrn_bwd", grid=(B, ng),
        in_specs=[part(0), part(1), part(2), part(3),
                  pl.BlockSpec((1, HG_WIDTH), lambda b, g: (0, 0)),
                  pl.BlockSpec((1, LANES), lambda b, g: (0, 0)),
                  part(0),
                  pl.BlockSpec((None, HG_HEADS, None, HG_HEAD_DIM, HG_STACK), lambda b, g: (b, 0, ng - 1 - g, 0, 0)),
                  part(0)] + [ANY_SPEC] * nr,
        out_specs=[part(0), part(0), part(0), part(0),
                   pl.BlockSpec((None, 8, HG_WIDTH), lambda b, g: (b, 0, 0)),
                   pl.BlockSpec((None, 8, LANES), lambda b, g: (b, 0, 0))] + [ANY_SPEC] * nr,
        out_shape=[SDS((B, T, HG_WIDTH), BF16)] * 4 + [SDS((B, 8, HG_WIDTH), F32), SDS((B, 8, LANES), F32)]
        + _exchange_shapes(ride_srcs, ride_modes),
        scratch_shapes=[pltpu.VMEM((HG_HEADS, HG_HEAD_DIM, HG_HEAD_DIM), F32)] + _exchange_sems(nr),
        compiler_params=_params(("arbitrary", "arbitrary"), VMEM_LIMIT_BIG),
    )(proj_h, proj_h, proj_h, proj_h, lb, hg_w, o, s_prev, drg, *ride_srcs)


def _attn_bwd(qr, kr, proj3, attn_o, dan, tables, sinks, attn_w, ride_srcs, ride_modes):
    B, T, _ = proj3.shape
    nb = T // WINDOW
    splits = min(ATT_SPLITS, nb)
    per = nb // splits
    nr = len(ride_srcs)
    cos, sinl, sinr = tables
    QKV = ATT_WIDTH + 2 * LANES

    def body(*refs):
        qr_ref, kr_ref, v_ref, o_ref, dan_ref, cos_ref, sl_ref, sr_ref, sink_ref, aw_ref = refs[:10]
        ride_in = refs[10:10 + nr]
        dqkv_ref, dsink_ref, daw_ref = refs[10 + nr:13 + nr]
        ride_out = refs[13 + nr:13 + 2 * nr]
        kpad, vpad, dkpad, dvpad, dqb, dsk = refs[13 + 2 * nr:19 + 2 * nr]
        sems = refs[19 + 2 * nr:]
        part = pl.program_id(1)
        step = pl.program_id(0) * splits + part
        _ride_start(ride_modes, step, B * splits, ride_in, ride_out, sems)

        @pl.when(part == 0)
        def _():
            kpad[0:WINDOW, :] = jnp.zeros((WINDOW, LANES), BF16)
            vpad[0:WINDOW, :] = jnp.zeros((WINDOW, LANES), BF16)
            kpad[WINDOW:, :] = kr_ref[...]
            vpad[WINDOW:, :] = _bf(v_ref[...])
            dkpad[...] = jnp.zeros(dkpad.shape, F32)
            dvpad[...] = jnp.zeros(dvpad.shape, F32)
            dsk[...] = jnp.zeros(dsk.shape, F32)
            daw_ref[...] = jnp.zeros(daw_ref.shape, F32)

        lower = _lower_mask()
        aw = aw_ref[...]

        def block(n, daw):
            r0 = pl.multiple_of(n * WINDOW, WINDOW)
            rows = pl.ds(r0, WINDOW)
            nxt = pl.ds(r0 + WINDOW, WINDOW)
            ob = o_ref[rows, :]
            dn = dan_ref[rows, :]
            ro = lax.rsqrt(_mean_last(ob * ob) + EPS)
            oh = ob * ro
            daw = daw + _sum_rows(dn * oh)
            doh = dn * aw
            do = _bf(ro * (doh - oh * _mean_last(doh * oh)))
            doparts = [do[:, j * LANES:(j + 1) * LANES] for j in range(ATT_WIDTH // LANES)]
            qparts = [qr_ref[rows, j * LANES:(j + 1) * LANES] for j in range(ATT_WIDTH // LANES)]
            for hk in range(ATT_KV_HEADS):
                lanes = slice(hk * ATT_HEAD_DIM, (hk + 1) * ATT_HEAD_DIM)
                qs = _stack_heads(qparts, hk)
                dos = _stack_heads(doparts, hk)
                k_cur, k_prev = kpad[nxt, lanes], kpad[rows, lanes]
                v_cur, v_prev = vpad[nxt, lanes], vpad[rows, lanes]
                p, inv, es = _softmax_window(qs, k_cur, k_prev, lower, n > 0, _sink_row(sink_ref, hk))
                p = p * inv
                dp = jnp.where(lower, _dot_nt(v_cur, dos), _dot_nt(v_prev, dos))
                delta = jnp.sum(p * dp, axis=0, keepdims=True)
                ds = p * (dp - delta)
                sk = (es * inv) * delta
                ds_cur = jnp.where(lower, ds, 0.0)
                p_cur = jnp.where(lower, p, 0.0)
                ds_cur, ds_prev = _bf(ds_cur), _bf(ds - ds_cur)
                p_cur, p_prev = _bf(p_cur), _bf(p - p_cur)
                dqt = (_dot_tn(k_cur, ds_cur) + _dot_tn(k_prev, ds_prev)) * ATT_SCALE
                dkpad[nxt, lanes] += _dot(ds_cur, qs)
                dkpad[rows, lanes] += _dot(ds_prev, qs)
                dvpad[nxt, lanes] += _dot(p_cur, dos)
                dvpad[rows, lanes] += _dot(p_prev, dos)
                for g in range(ATT_GROUP):
                    h = ATT_GROUP * hk + g
                    cols = slice(g * WINDOW, (g + 1) * WINDOW)
                    dqb[:, h * ATT_HEAD_DIM:(h + 1) * ATT_HEAD_DIM] = dqt[:, cols].T
                    head_lane = lax.broadcasted_iota(jnp.int32, dsk.shape, 1) == h
                    dsk[...] += jnp.where(head_lane, -jnp.sum(sk[:, cols], axis=1, keepdims=True), 0.0)
            cs, sl, sr = cos_ref[rows, :], sl_ref[rows, :], sr_ref[rows, :]
            for j in range(ATT_WIDTH // LANES):
                dqkv_ref[rows, j * LANES:(j + 1) * LANES] = _bf(_rope_t(dqb[:, j * LANES:(j + 1) * LANES], cs, sl, sr))
            return daw

        daw = _loop_pairs(part * per, per, block, jnp.zeros((1, ATT_WIDTH), F32))
        daw_ref[...] += jnp.broadcast_to(daw, (8, ATT_WIDTH))
        dsink_ref[...] = dsk[...]

        def finish(n, carry):
            r0 = pl.multiple_of(n * WINDOW, WINDOW)
            rows = pl.ds(r0, WINDOW)
            nxt = pl.ds(r0 + WINDOW, WINDOW)
            cs, sl, sr = cos_ref[rows, :], sl_ref[rows, :], sr_ref[rows, :]
            dqkv_ref[rows, ATT_WIDTH:ATT_WIDTH + LANES] = _bf(_rope_t(dkpad[nxt, :], cs, sl, sr))
            dqkv_ref[rows, ATT_WIDTH + LANES:QKV] = _bf(dvpad[nxt, :])
            return carry

        @pl.when(part == splits - 1)
        def _():
            lax.fori_loop(0, nb, finish, 0)

        _ride_wait(ride_modes, step, B * splits, ride_in, ride_out, sems)

    seq = lambda w, j: pl.BlockSpec((None, T, w), lambda b, s: (b, 0, j))
    full = lambda r, w: pl.BlockSpec((r, w), lambda b, s: (0, 0))
    return pl.pallas_call(
        body, name="attn_bwd", grid=(B, splits),
        in_specs=[seq(ATT_WIDTH, 0), seq(LANES, 0), seq(LANES, 5), seq(ATT_WIDTH, 0), seq(ATT_WIDTH, 0),
                  full(T, LANES), full(T, LANES), full(T, LANES),
                  pl.BlockSpec(memory_space=pltpu.SMEM), full(1, ATT_WIDTH)] + [ANY_SPEC] * nr,
        out_specs=[seq(QKV, 0), pl.BlockSpec((None, 8, LANES), lambda b, s: (b, 0, 0)),
                   pl.BlockSpec((None, 8, ATT_WIDTH), lambda b, s: (b, 0, 0))] + [ANY_SPEC] * nr,
        out_shape=[SDS((B, T, QKV), BF16), SDS((B, 8, LANES), F32), SDS((B, 8, ATT_WIDTH), F32)]
        + _exchange_shapes(ride_srcs, ride_modes),
        scratch_shapes=[pltpu.VMEM((T + WINDOW, LANES), BF16), pltpu.VMEM((T + WINDOW, LANES), BF16),
                        pltpu.VMEM((T + WINDOW, LANES), F32), pltpu.VMEM((T + WINDOW, LANES), F32),
                        pltpu.VMEM((WINDOW, ATT_WIDTH), F32), pltpu.VMEM((8, LANES), F32)] + _exchange_sems(nr),
        compiler_params=_params(("arbitrary", "arbitrary"), VMEM_LIMIT_BIG),
    )(qr, kr, proj3, attn_o, dan, cos, sinl, sinr, sinks, attn_w, *ride_srcs)


def _in_bwd(x2, dx1, dqkv, dhq, dhf, dhi, dhg, mod8, pre_w, w_in_bf, T, ride_srcs, ride_modes):
    N = x2.shape[0]
    TM = _tile_rows(T, big=True)
    tps = T // TM
    nr = len(ride_srcs)
    pieces = [(0, ATT_WIDTH + 2 * LANES), (768, HG_WIDTH), (1280, HG_WIDTH), (1792, HG_WIDTH), (2304, HG_WIDTH)]

    def body(*refs):
        x_ref, dx_ref, p0, p1, p2, p3, p4, mod_ref, pw_ref, w_ref = refs[:10]
        ride_in = refs[10:10 + nr]
        gx_ref, dproj_ref, acc_ref = refs[10 + nr:13 + nr]
        ride_out = refs[13 + nr:13 + 2 * nr]
        sems = refs[13 + 2 * nr:]
        _ride_start(ride_modes, pl.program_id(0), N // TM, ride_in, ride_out, sems)
        sc1 = mod_ref[1:2, :]
        dh = jnp.zeros((TM, D_MODEL), F32)
        for ref, (off, width) in zip((p0, p1, p2, p3, p4), pieces):
            pb = ref[...]
            dproj_ref[:, off:off + width] = pb
            dh = dh + _dot(pb, w_ref[off:off + width, :])
        x = x_ref[...]
        r = lax.rsqrt(_mean_last(x * x) + EPS)
        xh = x * r
        n1 = xh * pw_ref[...]
        dsh1 = _sum_rows(dh)
        dsc1 = _sum_rows(dh * n1)
        dn1 = dh * (1.0 + sc1)
        dw_pre = _sum_rows(dn1 * xh)
        dxh = dn1 * pw_ref[...]
        gx_ref[...] = dx_ref[...] + r * (dxh - xh * _mean_last(dxh * xh))
        _acc_rows(acc_ref, pl.program_id(0) % tps == 0, [dsh1, dsc1, dw_pre])
        _ride_wait(ride_modes, pl.program_id(0), N // TM, ride_in, ride_out, sems)

    row = lambda w: pl.BlockSpec((TM, w), lambda i: (i, 0))
    B = N // T
    return pl.pallas_call(
        body, name="in_bwd", grid=(N // TM,),
        in_specs=[row(D_MODEL), row(D_MODEL), row(768), row(HG_WIDTH), row(HG_WIDTH), row(HG_WIDTH),
                  row(HG_WIDTH), _mod_spec(tps), pl.BlockSpec((1, D_MODEL), lambda i: (0, 0)),
                  pl.BlockSpec((IN_COLS, D_MODEL), lambda i: (0, 0))] + [ANY_SPEC] * nr,
        out_specs=[row(D_MODEL), row(IN_COLS), _mod_spec(tps)] + [ANY_SPEC] * nr,
        out_shape=[SDS((N, D_MODEL), F32), SDS((N, IN_COLS), BF16), SDS((B, 8, D_MODEL), F32)]
        + _exchange_shapes(ride_srcs, ride_modes),
        scratch_shapes=_exchange_sems(nr),
        compiler_params=_params(("arbitrary",), VMEM_LIMIT_BIG),
    )(x2, dx1, dqkv, dhq, dhf, dhi, dhg, mod8, pre_w, w_in_bf, *ride_srcs)


def _matmul_tn(name, a, b, tn, tm):
    K, M = a.shape
    Nc = b.shape[1]

    def body(a_ref, b_ref, o_ref):
        o_ref[...] = _bf(_dot_tn(a_ref[...], b_ref[...]))

    return pl.pallas_call(
        body, name=name, grid=(M // tm, Nc // tn),
        in_specs=[pl.BlockSpec((K, tm), lambda i, j: (0, i)),
                  pl.BlockSpec((K, tn), lambda i, j: (0, j))],
        out_specs=pl.BlockSpec((tm, tn), lambda i, j: (i, j)), out_shape=SDS((M, Nc), BF16),
        compiler_params=_params(("arbitrary", "arbitrary"), VMEM_LIMIT_BIG),
    )(a, b)


def _matmul_tn_paired(name, a, b, stream_a):
    K = a.shape[0]
    stream, fixed = (a, b) if stream_a else (b, a)
    w = stream.shape[1] // N_DEV
    blk = (w, fixed.shape[1]) if stream_a else (fixed.shape[1], w)
    chips = N_DEV // 2

    def body(s_hbm, f_hbm, out_ref, s_buf, f_buf, g_buf, theirs, in_sems, send_sems, recv_sems):
        core = lax.axis_index("c")
        sib_dev, _ = _related(SIBLING)
        fixed_load = pltpu.make_async_copy(f_hbm, f_buf, in_sems.at[2])

        def load(j):
            owner = 2 * (j // 2) + (core if j % 2 else 1 - core)
            return pltpu.make_async_copy(s_hbm.at[:, pl.ds(pl.multiple_of(owner * w, LANES), w)], s_buf.at[j % 2],
                                         in_sems.at[j % 2])

        def swap(s):
            return pltpu.make_async_remote_copy(
                src_ref=g_buf.at[s, 0], dst_ref=theirs.at[s], send_sem=send_sems.at[s],
                recv_sem=recv_sems.at[s], device_id=sib_dev, device_id_type=MESH)

        fixed_load.start()
        load(0).start()
        fixed_load.wait()
        for j in range(N_DEV):
            s, mine = divmod(j, 2)
            load(j).wait()
            if j + 1 < N_DEV:
                load(j + 1).start()
            if stream_a:
                g_buf[s, mine] = _bf(_dot_tn(s_buf[j % 2], f_buf[...]))
            else:
                g_buf[s, mine] = _bf(_dot_tn(f_buf[...], s_buf[j % 2]))
            if mine:
                swap(s).wait_recv()
                out_ref[s] = _bf(g_buf[s, 1].astype(F32) + theirs[s].astype(F32))
            else:
                swap(s).start()
        for s in range(chips):
            swap(s).wait_send()

    return pl.pallas_call(
        body, name=name, in_specs=[ANY_SPEC] * 2, out_shape=SDS((chips,) + blk, BF16),
        scratch_shapes=[pltpu.VMEM((2, K, w), BF16), pltpu.VMEM(fixed.shape, BF16),
                        pltpu.VMEM((chips, 2) + blk, BF16), pltpu.VMEM((chips,) + blk, BF16),
                        pltpu.SemaphoreType.DMA((3,)), pltpu.SemaphoreType.DMA((chips,)),
                        pltpu.SemaphoreType.DMA((chips,))],
        compiler_params=_params(None, VMEM_LIMIT_BIG),
    )(stream, fixed)


GW_BLOCK = IN_COLS // N_DEV
GW_HALF = IN_COLS // 2


def _grad_w_in_reduced(dproj, h1, ride_srcs, ride_modes):
    K = dproj.shape[0]
    nr = len(ride_srcs)
    chips = N_DEV // 2
    tn = 512

    def body(*refs):
        a_hbm, b_hbm = refs[:2]
        ride_in, out, ride_out = refs[2:2 + nr], refs[2 + nr], refs[3 + nr:3 + 2 * nr]
        a_buf, b_buf, g_buf, theirs, p_buf, in_sems, pair_send, pair_recv, chip_send, chip_recv, own_sem = \
            refs[3 + 2 * nr:14 + 2 * nr]
        ride = _exchange_phases(ride_modes, ride_in, ride_out, *refs[14 + 2 * nr:]) if nr else ([], [], [])
        x, y, core = lax.axis_index("x"), lax.axis_index("y"), lax.axis_index("c")
        chip = 2 * x + y
        sib_dev, _ = _related(SIBLING)

        def remote(src, dst, send_sem, recv_sem, dev):
            return pltpu.make_async_remote_copy(src_ref=src, dst_ref=dst, send_sem=send_sem, recv_sem=recv_sem,
                                                device_id=dev, device_id_type=MESH)

        halves = [1 - x, x]
        loads = [pltpu.make_async_copy(b_hbm, b_buf, in_sems.at[0])]
        for t in range(2):
            col = pl.multiple_of(halves[t] * GW_HALF, LANES)
            loads.append(pltpu.make_async_copy(a_hbm.at[:, pl.ds(col, GW_HALF)], a_buf.at[t], in_sems.at[1 + t]))
        for cp in loads:
            cp.start()
        _run(ride[0])
        loads[0].wait()
        end = []
        for t in range(2):
            loads[1 + t].wait()
            if t == 1:
                _run(ride[1])
            for j in range(D_MODEL // tn):
                cols = pl.ds(j * tn, tn)
                res = _dot_tn(a_buf[t], b_buf[:, cols])
                for q in range(2):
                    for cc in range(2):
                        r0 = (2 * q + cc) * GW_BLOCK
                        g_buf[t, q, cc, :, cols] = _bf(res[r0:r0 + GW_BLOCK])
                swaps = [remote(g_buf.at[t, q, 1 - core, :, cols], theirs.at[t, q, :, cols],
                                pair_send.at[t, 2 * j + q], pair_recv.at[t, 2 * j + q], sib_dev) for q in range(2)]
                for cp in swaps:
                    cp.start()
                for cp in swaps:
                    cp.wait_recv()
                end += [cp.wait_send for cp in swaps]
                for q in range(2):
                    p_buf[t, q, :, cols] = _bf(g_buf[t, q, core, :, cols].astype(F32)
                                               + theirs[t, q, :, cols].astype(F32))
                for dy in range(2):
                    k = 4 * (1 - t) + 2 * dy
                    if k == 0:
                        own = pltpu.make_async_copy(p_buf.at[t, y, :, cols], out.at[chip, :, cols], own_sem.at[j])
                        own.start()
                        end.append(own.wait)
                        continue
                    dev, peer = _related(k)
                    sems = chip_send.at[k // 2, j], chip_recv.at[k // 2, j]
                    send = remote(p_buf.at[t, y ^ dy, :, cols], out.at[chip, :, cols], *sems, dev)
                    send.start()
                    end += [remote(p_buf.at[t, y ^ dy, :, cols], out.at[peer // 2, :, cols], *sems, dev).wait_recv,
                            send.wait_send]
        _run(ride[2])
        _run(end)

    return pl.pallas_call(
        body, name="grad_w_in",
        in_specs=[ANY_SPEC] * (2 + nr), out_specs=[ANY_SPEC] * (1 + nr),
        out_shape=[SDS((chips, GW_BLOCK, D_MODEL), BF16)] + _exchange_shapes(ride_srcs, ride_modes),
        scratch_shapes=[pltpu.VMEM((2, K, GW_HALF), BF16), pltpu.VMEM((K, D_MODEL), BF16),
                        pltpu.VMEM((2, 2, 2, GW_BLOCK, D_MODEL), BF16), pltpu.VMEM((2, 2, GW_BLOCK, D_MODEL), BF16),
                        pltpu.VMEM((2, 2, GW_BLOCK, D_MODEL), BF16), pltpu.SemaphoreType.DMA((3,)),
                        pltpu.SemaphoreType.DMA((2, 4)), pltpu.SemaphoreType.DMA((2, 4)),
                        pltpu.SemaphoreType.DMA((chips, 2)), pltpu.SemaphoreType.DMA((chips, 2)),
                        pltpu.SemaphoreType.DMA((2,))] + _exchange_sems(nr),
        compiler_params=_params(None, VMEM_LIMIT_BIG),
    )(dproj, h1, *ride_srcs)


def _adamw_math(w, g, m, v):
    m2 = ADAM_B1 * m + (1.0 - ADAM_B1) * g
    v2 = ADAM_B2 * v + (1.0 - ADAM_B2) * (g * g)
    m_hat = m2 / (1.0 - ADAM_B1 ** ADAM_STEP)
    v_hat = v2 / (1.0 - ADAM_B2 ** ADAM_STEP)
    delta = -ADAM_LR * (m_hat / (jnp.sqrt(v_hat) + ADAM_EPS) + ADAM_WD * w)
    return delta, m2, v2


def _reduce_adamw(name, parts, w, m, v):
    r, c = w.shape
    tr = r if r % 256 else 256
    slots = parts.shape[0]

    def body(p_ref, w_ref, m_ref, v_ref, g_ref, d_ref, m2_ref, v2_ref):
        g = p_ref[0].astype(F32)
        for s in range(1, slots):
            g = g + p_ref[s].astype(F32)
        g_ref[...] = g
        d_ref[...], m2_ref[...], v2_ref[...] = _adamw_math(w_ref[...], g, m_ref[...], v_ref[...])

    blk = pl.BlockSpec((tr, c), lambda i: (i, 0))
    return pl.pallas_call(
        body, name=name, grid=(r // tr,),
        in_specs=[pl.BlockSpec((slots, tr, c), lambda i: (0, i, 0)), blk, blk, blk],
        out_specs=[blk] * 4, out_shape=[SDS((r, c), F32)] * 4,
        compiler_params=_params(("arbitrary",), VMEM_LIMIT_BIG),
    )(parts, w, m, v)


def _ada_grad_adamw(c_all, dmod_all, w, m, v):
    r, c = w.shape
    tr = 256
    nb = c_all.shape[0]

    def body(c_ref, dm_ref, w_ref, m_ref, v_ref, g_ref, d_ref, m2_ref, v2_ref):
        cv = c_ref[...]
        g = _dot_tn(cv * _sigmoid(cv), dm_ref[...])
        g_ref[...] = g
        d_ref[...], m2_ref[...], v2_ref[...] = _adamw_math(w_ref[...], g, m_ref[...], v_ref[...])

    blk = pl.BlockSpec((tr, c), lambda i: (i, 0))
    return pl.pallas_call(
        body, name="ada_grad_adamw", grid=(r // tr,),
        in_specs=[pl.BlockSpec((nb, tr), lambda i: (0, i)), pl.BlockSpec((nb, c), lambda i: (0, 0)),
                  blk, blk, blk],
        out_specs=[blk] * 4, out_shape=[SDS((r, c), F32)] * 4,
        compiler_params=_params(("arbitrary",)),
    )(c_all, dmod_all, w, m, v)


_SMALL = [("b_ada", 6144), ("pre_w_mix", 1024), ("attn_sinks", 128), ("attn_out_w", 512), ("lb_table", 1024),
          ("hg_norm_w", 128), ("post_w_mix", 1024), ("pre_w_mlp", 1024), ("post_w_mlp", 1024)]


def _pack_small(acc_in, acc_mix, acc_mlp, dsink, daw, dlb, dgw, lb_p, ada_cols):
    B = acc_in.shape[0]
    width = sum(w for _, w in _SMALL) + LANES

    def body(ain, amix, amlp, dsk_ref, daw_ref, dlb_ref, dgw_ref, lbp_ref, packed_ref, dmod_ref):
        def total(ref, r, w=None):
            out = ref[0, r:r + 1, :] if w is None else ref[0, r:r + 1, :w]
            for b in range(1, B):
                out = out + (ref[b, r:r + 1, :] if w is None else ref[b, r:r + 1, :w])
            return out

        d_b_ada = None
        for b in range(B):
            mods = [ain[b, 0:1, :], ain[b, 1:2, :], amix[b, 0:1, :], amlp[b, 0:1, :], amlp[b, 1:2, :], amlp[b, 2:3, :]]
            full = jnp.concatenate(mods, axis=1)
            for j in range(N_DEV):
                dmod_ref[j, b:b + 1, :] = full[:, j * ada_cols:(j + 1) * ada_cols]
            d_b_ada = full if d_b_ada is None else d_b_ada + full
        d_lb = total(dlb_ref, 0)
        pp = lbp_ref[0:1, :] * lbp_ref[1:2, :]
        pieces = [d_b_ada, total(ain, 2), total(dsk_ref, 0), total(daw_ref, 0), -d_lb * pp, d_lb * pp,
                  total(dgw_ref, 0), total(amix, 1), total(amlp, 3), total(amlp, 4), total(amlp, 5, LANES)]
        off = 0
        for piece in pieces:
            packed_ref[:, off:off + piece.shape[1]] = piece
            off += piece.shape[1]

    return pl.pallas_call(
        body, name="pack_small",
        out_shape=[SDS((1, width), F32), SDS((N_DEV, B, ada_cols), F32)],
    )(acc_in, acc_mix, acc_mlp, dsink, daw, dlb, dgw, lb_p)


def _adamw_small(parts, given):
    names = [n for n, _ in _SMALL]
    flat_in = [a for n in names for a in given[n]]

    def body(*refs):
        p_ref = refs[0]
        in_refs = refs[1:1 + 3 * len(names)]
        out_refs = refs[1 + 3 * len(names):-1]
        loss_ref = refs[-1]
        g = p_ref[0]
        for s in range(1, N_DEV):
            g = g + p_ref[s]
        off = 0
        for i, (name, width) in enumerate(_SMALL):
            w_ref, m_ref, v_ref = in_refs[3 * i:3 * i + 3]
            rows, cols = w_ref.shape
            for r in range(rows):
                gr = g[:, off + r * cols:off + (r + 1) * cols]
                res = (gr,) + _adamw_math(w_ref[r:r + 1, :], gr, m_ref[r:r + 1, :], v_ref[r:r + 1, :])
                for o_ref, val in zip(out_refs[4 * i:4 * i + 4], res):
                    o_ref[r:r + 1, :] = val
            off += width
        loss_ref[...] = g[:, off:off + LANES]

    out_shape = [SDS(given[n][0].shape, F32) for n in names for _ in range(4)] + [SDS((1, LANES), F32)]
    outs = pl.pallas_call(body, name="adamw_small", out_shape=out_shape)(parts, *flat_in)
    return {n: tuple(outs[4 * i:4 * i + 4]) for i, n in enumerate(names)}, outs[-1][0, 0]


def kernel(x, c, w_ada, b_ada, pre_w_mix, w_in, attn_sinks, attn_out_w, lb_table, hg_norm_w, w_out, post_w_mix, pre_w_mlp, w_up, w_down, post_w_mlp, loss_target, m_w_ada, m_b_ada, m_pre_w_mix, m_w_in, m_attn_sinks, m_attn_out_w, m_lb_table, m_hg_norm_w, m_w_out, m_post_w_mix, m_pre_w_mlp, m_w_up, m_w_down, m_post_w_mlp, v_w_ada, v_b_ada, v_pre_w_mix, v_w_in, v_attn_sinks, v_attn_out_w, v_lb_table, v_hg_norm_w, v_w_out, v_post_w_mix, v_pre_w_mlp, v_w_up, v_w_down, v_post_w_mlp):
    B, T, _ = x.shape
    N = B * T
    me = 4 * lax.axis_index("x") + 2 * lax.axis_index("y") + lax.axis_index("c")
    x2 = x.reshape(N, D_MODEL)
    tgt2 = loss_target.reshape(N, D_MODEL)

    w_in_t, m_w_in_t, v_w_in_t = w_in[0].T, m_w_in[0].T, v_w_in[0].T
    w_in_g, c_g = _exchange("gather_w_in", [_bf(w_in_t), c], ["gather"] * 2)
    w_in_f = w_in_g.reshape(IN_COLS, D_MODEL)
    c_all = c_g.reshape(N_DEV * B, D_MODEL)

    ada_cols = w_ada.shape[2]
    b_mine = lax.dynamic_slice(b_ada, (0, me * ada_cols), (1, ada_cols))
    mod_cols = _ada_mod(c_all, w_ada[0], b_mine)
    (mod_g,) = _exchange("scatter_mod", [mod_cols.reshape(N_DEV, B, ada_cols)], ["a2a"])
    mod = mod_g.transpose(1, 0, 2).reshape(B, 6, D_MODEL)
    mod8 = jnp.pad(mod, ((0, 0), (0, 2), (0, 0)))

    lb_p = jax.nn.softmax(lb_table, axis=0)
    lb = lb_p[1:2]
    tables = _rope_tables(T)

    w_up_b, w_down_b = _bf(w_up[0]), _bf(w_down[0])
    proj_a, proj_h, h1, w_out_g, w_up_g0 = _in_proj(x2, mod8, pre_w_mix, w_in_f, T,
                                                    [_bf(w_out[0]), w_up_b[:MLP_HALF]], ["gather"] * 2)
    proj3 = proj_a.reshape(B, T, ATT_COLS)
    proj_h = proj_h.reshape(B, T, IN_COLS - ATT_COLS)
    rec_o, rec_g, s_prev, w_up_g1 = _hgrn_fwd(proj_h, lb, hg_norm_w, [w_up_b[MLP_HALF:]], ["gather"])
    attn_o, attn_n, qr, kr, w_down_g0 = _attn_fwd(proj3, tables, attn_sinks, attn_out_w,
                                                  [w_down_b[:, :MLP_HALF]], ["gather"])
    w_out_f = w_out_g.reshape(D_MODEL, D_MODEL)
    mix, x1, cat, w_down_g1 = _mix_out(x2, attn_n.reshape(N, ATT_WIDTH), rec_g.reshape(N, HG_WIDTH), mod8,
                                       post_w_mix, w_out_f, T, [w_down_b[:, MLP_HALF:]], ["gather"])
    w_up_halves = [w_up_g0, w_up_g1]
    w_down_halves = [w_down_g0.reshape(D_FF, MLP_HALF), w_down_g1.reshape(D_FF, MLP_HALF)]
    up, u, d, h2 = _mlp_fwd(x1, mod8, pre_w_mlp, w_up_halves, w_down_halves, T)

    dx1, dup, dd, acc_mlp = _mlp_bwd(x1, d, up, tgt2, mod8, pre_w_mlp, post_w_mlp, w_up_halves, w_down_halves, T)
    p_up = _matmul_tn_paired("grad_w_up", h2, dup, stream_a=False)
    p_down = _matmul_tn_paired("grad_w_down", u, dd, stream_a=True)
    dan, drg, dmix, acc_mix = _mix_bwd(mix, dx1, mod8, post_w_mix, w_out_f, T, [], [])
    gw_out = _matmul_tn("grad_w_out", cat, dmix, 512, tm=D_MODEL).reshape(N_DEV, D_MODEL // N_DEV, D_MODEL)
    dhq, dhf, dhi, dhg, dlb_p, dgw_p, r_down, r_up = _hgrn_bwd(
        proj_h, lb, hg_norm_w, rec_o, s_prev, drg.reshape(B, T, HG_WIDTH), [p_down, p_up], ["chips"] * 2)
    dqkv, dsink_p, daw_p, r_out = _attn_bwd(qr, kr, proj3, attn_o, dan.reshape(B, T, ATT_WIDTH), tables,
                                            attn_sinks, attn_out_w, [gw_out], ["a2a"])
    flat = lambda a: a.reshape(N, a.shape[-1])
    grad_x, dproj, acc_in = _in_bwd(x2, dx1, flat(dqkv), flat(dhq), flat(dhf), flat(dhi), flat(dhg),
                                    mod8, pre_w_mix, w_in_f, T, [], [])

    packed, dmod_blocks = _pack_small(acc_in, acc_mix, acc_mlp, dsink_p, daw_p, dlb_p, dgw_p, lb_p, ada_cols)
    r_in, r_dmod, r_small = _grad_w_in_reduced(dproj, h1, [dmod_blocks, packed], ["a2a", "gather"])

    res = {}
    res["w_in"] = tuple(a.T for a in _reduce_adamw("adamw_w_in", r_in, w_in_t, m_w_in_t, v_w_in_t))
    res["w_out"] = _reduce_adamw("adamw_w_out", r_out, w_out[0], m_w_out[0], v_w_out[0])
    res["w_up"] = _reduce_adamw("adamw_w_up", r_up, w_up[0], m_w_up[0], v_w_up[0])
    res["w_down"] = _reduce_adamw("adamw_w_down", r_down, w_down[0], m_w_down[0], v_w_down[0])
    res["w_ada"] = _ada_grad_adamw(c_all, r_dmod.reshape(N_DEV * B, ada_cols), w_ada[0], m_w_ada[0], v_w_ada[0])

    given = dict(b_ada=(b_ada, m_b_ada, v_b_ada), pre_w_mix=(pre_w_mix, m_pre_w_mix, v_pre_w_mix),
                 attn_sinks=(attn_sinks, m_attn_sinks, v_attn_sinks),
                 attn_out_w=(attn_out_w, m_attn_out_w, v_attn_out_w), lb_table=(lb_table, m_lb_table, v_lb_table),
                 hg_norm_w=(hg_norm_w, m_hg_norm_w, v_hg_norm_w), post_w_mix=(post_w_mix, m_post_w_mix, v_post_w_mix),
                 pre_w_mlp=(pre_w_mlp, m_pre_w_mlp, v_pre_w_mlp), post_w_mlp=(post_w_mlp, m_post_w_mlp, v_post_w_mlp))
    small_res, loss = _adamw_small(r_small, given)
    res.update(small_res)

    order = ["w_ada", "b_ada", "pre_w_mix", "w_in", "attn_sinks", "attn_out_w", "lb_table", "hg_norm_w", "w_out",
             "post_w_mix", "pre_w_mlp", "w_up", "w_down", "post_w_mlp"]
    big = {"w_ada", "w_in", "w_out", "w_up", "w_down"}
    outs = [loss, grad_x.reshape(B, T, D_MODEL)]
    for i in range(4):
        for k in order:
            a = res[k][i]
            outs.append(a[None] if k in big else a)
    return tuple(outs)
```

```python
import jax
import jax.numpy as jnp
import numpy as np
from jax import lax
from jax.experimental import pallas as pl
from jax.experimental.pallas import tpu as pltpu

F32 = jnp.float32
BF16 = jnp.bfloat16
SDS = jax.ShapeDtypeStruct

D_MODEL = 1024
ATT_WIDTH = 512
ATT_HEAD_DIM = 64
ATT_KV_HEADS = 2
ATT_GROUP = 4
WINDOW = 128
ROPE_DIM = 16
ROPE_THETA = 500000.0
HG_WIDTH = 512
HG_HEAD_DIM = 128
HG_HEADS = 4
HG_CHUNK = 32
IN_COLS = 2816
ATT_COLS = 768
D_FF = 4096
EPS = 1e-6
N_DEV = 8

ADAM_LR = 0.001
ADAM_B1 = 0.9
ADAM_B2 = 0.999
ADAM_EPS = 1e-08
ADAM_WD = 0.01
ADAM_STEP = 10

VMEM_LIMIT_BIG = 56 << 20
LANES = 128

MESH = pl.DeviceIdType.MESH
NT_DIMS = (((1,), (1,)), ((), ()))
TN_DIMS = (((0,), (0,)), ((), ()))


def _dot(a, b):
    return jnp.dot(a, b, preferred_element_type=F32)


def _dot_nt(a, b):
    return lax.dot_general(a, b, NT_DIMS, preferred_element_type=F32)


def _dot_tn(a, b):
    return lax.dot_general(a, b, TN_DIMS, preferred_element_type=F32)


def _bf(a):
    return a.astype(BF16)


def _sigmoid(a):
    return 0.5 * jnp.tanh(0.5 * a) + 0.5


def _mean_last(a):
    return jnp.mean(a, axis=-1, keepdims=True)


def _sum_rows(a):
    return jnp.sum(a, axis=0, keepdims=True)


def _loop_pairs(first, count, body, init, per_trip=2):
    if count % per_trip:
        return lax.fori_loop(first, first + count, body, init)

    def trip(i, c):
        for k in range(per_trip):
            c = body(first + per_trip * i + k, c)
        return c

    return lax.fori_loop(0, count // per_trip, trip, init)


def _params(sem=None, vmem=None):
    kw = {}
    if sem is not None:
        kw["dimension_semantics"] = sem
    if vmem is not None:
        kw["vmem_limit_bytes"] = vmem
    return pltpu.CompilerParams(**kw)


ANY_SPEC = pl.BlockSpec(memory_space=pl.ANY)


def _exchange_shapes(srcs, modes):
    out_shape = []
    for s, m in zip(srcs, modes):
        shp = (N_DEV,) + tuple(s.shape) if m == "gather" else tuple(s.shape)
        out_shape.append(SDS(shp, s.dtype))
    return out_shape


def _exchange_sems(n):
    if n == 0:
        return []
    return [pltpu.SemaphoreType.DMA((n, N_DEV - 1)), pltpu.SemaphoreType.DMA((n, N_DEV - 1)),
            pltpu.SemaphoreType.DMA((n,))]


SIBLING = 1
OTHER_CHIPS = (2, 4, 6)


def _related(k):
    x, y, c = lax.axis_index("x"), lax.axis_index("y"), lax.axis_index("c")
    px, py, pc = x ^ ((k >> 2) & 1), y ^ ((k >> 1) & 1), c ^ (k & 1)
    return (px, py, pc), 4 * px + 2 * py + pc


def _exchange_phases(modes, src_refs, out_refs, send_sems, recv_sems, own_sems):
    _, me = _related(0)
    sib_dev, sib = _related(SIBLING)
    start, middle, end = [], [], []

    def remote(a, i, src, dst, dev):
        return pltpu.make_async_remote_copy(src_ref=src, dst_ref=dst, send_sem=send_sems.at[a, i],
                                            recv_sem=recv_sems.at[a, i], device_id=dev, device_id_type=MESH)

    for a, mode in enumerate(modes):
        out = out_refs[a]
        if mode == "gather":
            src = src_refs[a]
            own = pltpu.make_async_copy(src, out.at[me], own_sems.at[a])
            to_sib = remote(a, 0, src, out.at[me], sib_dev)
            start += [own.start, to_sib.start]
            end += [remote(a, 0, src, out.at[sib], sib_dev).wait_recv, to_sib.wait_send, own.wait]
            for j, k in enumerate(OTHER_CHIPS, start=1):
                dev, peer = _related(k)
                _, peer_sib = _related(k ^ SIBLING)
                send = remote(a, j, src, out.at[me], dev)
                passed = remote(a, 3 + j, out.at[peer], out.at[peer], sib_dev)
                start.append(send.start)
                middle += [remote(a, j, src, out.at[peer], dev).wait_recv, passed.start]
                end += [remote(a, 3 + j, out.at[peer_sib], out.at[peer_sib], sib_dev).wait_recv,
                        send.wait_send, passed.wait_send]
        elif mode == "chips":
            chip = me // 2
            own = pltpu.make_async_copy(src_refs[a].at[chip], out.at[chip], own_sems.at[a])
            start.append(own.start)
            end.append(own.wait)
            for j, k in enumerate(OTHER_CHIPS, start=1):
                dev, peer = _related(k)
                send = remote(a, j, src_refs[a].at[peer // 2], out.at[chip], dev)
                start.append(send.start)
                end += [remote(a, j, src_refs[a].at[peer // 2], out.at[peer // 2], dev).wait_recv, send.wait_send]
        else:
            own = pltpu.make_async_copy(src_refs[a].at[me], out.at[me], own_sems.at[a])
            start.append(own.start)
            end.append(own.wait)
            for k in range(1, N_DEV):
                dev, peer = _related(k)
                send = remote(a, k - 1, src_refs[a].at[peer], out.at[me], dev)
                start.append(send.start)
                end += [remote(a, k - 1, src_refs[a].at[peer], out.at[peer], dev).wait_recv, send.wait_send]
    return start, middle, end


def _run(actions):
    for act in actions:
        act()


def _exchange(name, srcs, modes):
    n = len(srcs)

    def body(*refs):
        start, middle, end = _exchange_phases(modes, refs[:n], refs[n:2 * n], *refs[2 * n:])
        _run(start)
        _run(middle)
        _run(end)

    return pl.pallas_call(
        body, name=name, out_shape=_exchange_shapes(srcs, modes),
        in_specs=[ANY_SPEC] * n, out_specs=[ANY_SPEC] * n,
        scratch_shapes=_exchange_sems(n),
    )(*srcs)


def _ride_start(modes, step, steps, src_refs, out_refs, sems):
    if not modes:
        return
    middle_step = steps - 1

    @pl.when(step == 0)
    def _():
        _run(_exchange_phases(modes, src_refs, out_refs, *sems)[0])

    if "gather" in modes:
        @pl.when(step == middle_step)
        def _():
            _run(_exchange_phases(modes, src_refs, out_refs, *sems)[1])


def _ride_wait(modes, step, steps, src_refs, out_refs, sems):
    if not modes:
        return

    @pl.when(step == steps - 1)
    def _():
        _run(_exchange_phases(modes, src_refs, out_refs, *sems)[2])


def _ada_mod(c_all, w_ada, b_ada_mine):
    nb, cols = c_all.shape[0], w_ada.shape[1]

    def body(c_ref, w_ref, b_ref, o_ref):
        cv = c_ref[...]
        ca = cv * _sigmoid(cv)
        o_ref[...] = _dot(ca, w_ref[...]) + b_ref[...]

    return pl.pallas_call(body, name="ada_mod", out_shape=SDS((nb, cols), F32))(c_all, w_ada, b_ada_mine)


def _tile_rows(T, big=False):
    return min(512 if big else 256, T)


def _mod_spec(tps):
    return pl.BlockSpec((None, 8, D_MODEL), lambda i: (i // tps, 0, 0))


def _in_proj(x2, mod8, pre_w, w_in_bf, T, ride_srcs, ride_modes):
    N = x2.shape[0]
    TM = _tile_rows(T, big=True)
    tps = T // TM
    nr = len(ride_srcs)

    def body(*refs):
        x_ref, mod_ref, pw_ref, w_ref = refs[:4]
        ride_in = refs[4:4 + nr]
        pa_ref, ph_ref, h1_ref = refs[4 + nr:7 + nr]
        ride_out = refs[7 + nr:7 + 2 * nr]
        sems = refs[7 + 2 * nr:]
        _ride_start(ride_modes, pl.program_id(0), N // TM, ride_in, ride_out, sems)
        x = x_ref[...]
        r = lax.rsqrt(_mean_last(x * x) + EPS)
        h = (x * r * pw_ref[...]) * (1.0 + mod_ref[1:2, :]) + mod_ref[0:1, :]
        hb = _bf(h)
        h1_ref[...] = hb
        pa_ref[...] = _dot_nt(hb, w_ref[:ATT_COLS, :])
        ph_ref[...] = _dot_nt(hb, w_ref[ATT_COLS:, :])
        _ride_wait(ride_modes, pl.program_id(0), N // TM, ride_in, ride_out, sems)

    return pl.pallas_call(
        body, name="in_proj", grid=(N // TM,),
        in_specs=[pl.BlockSpec((TM, D_MODEL), lambda i: (i, 0)), _mod_spec(tps),
                  pl.BlockSpec((1, D_MODEL), lambda i: (0, 0)),
                  pl.BlockSpec((IN_COLS, D_MODEL), lambda i: (0, 0))] + [ANY_SPEC] * nr,
        out_specs=[pl.BlockSpec((TM, ATT_COLS), lambda i: (i, 0)),
                   pl.BlockSpec((TM, IN_COLS - ATT_COLS), lambda i: (i, 0)),
                   pl.BlockSpec((TM, D_MODEL), lambda i: (i, 0))] + [ANY_SPEC] * nr,
        out_shape=[SDS((N, ATT_COLS), F32), SDS((N, IN_COLS - ATT_COLS), F32), SDS((N, D_MODEL), BF16)]
        + _exchange_shapes(ride_srcs, ride_modes),
        scratch_shapes=_exchange_sems(nr),
        compiler_params=_params(("arbitrary",), VMEM_LIMIT_BIG),
    )(x2, mod8, pre_w, w_in_bf, *ride_srcs)


def _rope_tables(T):
    half = ROPE_DIM // 2
    f32 = np.float32
    inv_freq = (f32(ROPE_THETA) ** (-np.arange(0, ROPE_DIM, 2, dtype=f32) / f32(ROPE_DIM))).astype(f32)
    ang = np.arange(T, dtype=f32)[:, None] * inv_freq[None, :]
    cos, sin = np.cos(ang).astype(f32), np.sin(ang).astype(f32)
    ones = np.ones((T, ATT_HEAD_DIM - ROPE_DIM), f32)
    zeros = np.zeros((T, ATT_HEAD_DIM - ROPE_DIM), f32)
    zh = np.zeros((T, half), f32)
    cos64 = np.concatenate([cos, cos, ones], axis=1)
    sin_left = np.concatenate([-sin, zh, zeros], axis=1)
    sin_right = np.concatenate([zh, sin, zeros], axis=1)
    rep = LANES // ATT_HEAD_DIM
    return tuple(jnp.asarray(np.tile(t, (1, rep))) for t in (cos64, sin_left, sin_right))


def _rope(xc, cs, sl, sr):
    return xc * cs + pltpu.roll(xc, LANES - 8, 1) * sl + pltpu.roll(xc, 8, 1) * sr


def _rope_t(dy, cs, sl, sr):
    return dy * cs + pltpu.roll(dy * sl, 8, 1) + pltpu.roll(dy * sr, LANES - 8, 1)


ATT_SCALE = ATT_HEAD_DIM ** -0.5
ATT_SPLITS = 4


def _lower_mask():
    j = lax.broadcasted_iota(jnp.int32, (WINDOW, ATT_GROUP * WINDOW), 0)
    i = lax.broadcasted_iota(jnp.int32, (WINDOW, ATT_GROUP * WINDOW), 1) & (WINDOW - 1)
    return j <= i


def _sink_row(sink_ref, hk):
    return jnp.concatenate(
        [jnp.full((1, WINDOW), sink_ref[0, ATT_GROUP * hk + g], F32) for g in range(ATT_GROUP)], axis=1)


def _softmax_window(qs, k_cur, k_prev, lower, has_prev, sink):
    s_prev = jnp.where(has_prev, _dot_nt(k_prev, qs), jnp.finfo(F32).min)
    s = jnp.where(lower, _dot_nt(k_cur, qs), s_prev)
    m = jnp.maximum(jnp.max(s, axis=0, keepdims=True), sink)
    p = jnp.exp(s - m)
    es = jnp.exp(sink - m)
    inv = 1.0 / (jnp.sum(p, axis=0, keepdims=True) + es)
    return p, inv, es


def _stack_heads(parts, hk):
    hs = []
    for g in range(ATT_GROUP):
        h = ATT_GROUP * hk + g
        hs.append(parts[h // 2][:, (h % 2) * ATT_HEAD_DIM:(h % 2 + 1) * ATT_HEAD_DIM])
    return jnp.concatenate(hs, axis=0)


def _attn_fwd(proj3, tables, sinks, attn_w, ride_srcs, ride_modes):
    B, T, _ = proj3.shape
    nb = T // WINDOW
    splits = min(ATT_SPLITS, nb)
    per = nb // splits
    nr = len(ride_srcs)
    cos, sinl, sinr = tables

    def body(*refs):
        q_ref, k_ref, v_ref, cos_ref, sl_ref, sr_ref, sink_ref, aw_ref = refs[:8]
        ride_in = refs[8:8 + nr]
        o_ref, an_ref, qr_ref, kr_ref = refs[8 + nr:12 + nr]
        ride_out = refs[12 + nr:12 + 2 * nr]
        kpad, vpad = refs[12 + 2 * nr:14 + 2 * nr]
        sems = refs[14 + 2 * nr:]
        part = pl.program_id(1)
        step = pl.program_id(0) * splits + part
        _ride_start(ride_modes, step, B * splits, ride_in, ride_out, sems)

        @pl.when(part == 0)
        def _():
            kpad[0:WINDOW, :] = jnp.zeros((WINDOW, LANES), BF16)
            vpad[0:WINDOW, :] = jnp.zeros((WINDOW, LANES), BF16)

        lower = _lower_mask()

        def block(n, carry):
            r0 = pl.multiple_of(n * WINDOW, WINDOW)
            rows = pl.ds(r0, WINDOW)
            nxt = pl.ds(r0 + WINDOW, WINDOW)
            cs, sl, sr = cos_ref[rows, :], sl_ref[rows, :], sr_ref[rows, :]
            kb = _bf(_rope(k_ref[rows, :], cs, sl, sr))
            vb = _bf(v_ref[rows, :])
            kpad[nxt, :] = kb
            kr_ref[rows, :] = kb
            vpad[nxt, :] = vb
            qparts = []
            for j in range(ATT_WIDTH // LANES):
                qp = _bf(_rope(q_ref[rows, j * LANES:(j + 1) * LANES], cs, sl, sr) * ATT_SCALE)
                qr_ref[rows, j * LANES:(j + 1) * LANES] = qp
                qparts.append(qp)
            for hk in range(ATT_KV_HEADS):
                lanes = slice(hk * ATT_HEAD_DIM, (hk + 1) * ATT_HEAD_DIM)
                qs = _stack_heads(qparts, hk)
                p, inv, _ = _softmax_window(qs, kb[:, lanes], kpad[rows, lanes], lower, n > 0,
                                            _sink_row(sink_ref, hk))
                p_cur = jnp.where(lower, p, 0.0)
                ot = (_dot_tn(vb[:, lanes], _bf(p_cur)) + _dot_tn(vpad[rows, lanes], _bf(p - p_cur))) * inv
                for g in range(ATT_GROUP):
                    h = ATT_GROUP * hk + g
                    o_ref[rows, h * ATT_HEAD_DIM:(h + 1) * ATT_HEAD_DIM] = ot[:, g * WINDOW:(g + 1) * WINDOW].T
            ob = o_ref[rows, :]
            an_ref[rows, :] = _bf(ob * lax.rsqrt(_mean_last(ob * ob) + EPS) * aw_ref[...])
            return carry

        _loop_pairs(part * per, per, block, 0)
        _ride_wait(ride_modes, step, B * splits, ride_in, ride_out, sems)

    seq = lambda w, j: pl.BlockSpec((None, T, w), lambda b, s: (b, 0, j))
    full = lambda r, w: pl.BlockSpec((r, w), lambda b, s: (0, 0))
    return pl.pallas_call(
        body, name="attn_fwd", grid=(B, splits),
        in_specs=[seq(ATT_WIDTH, 0), seq(LANES, 4), seq(LANES, 5),
                  full(T, LANES), full(T, LANES), full(T, LANES),
                  pl.BlockSpec(memory_space=pltpu.SMEM), full(1, ATT_WIDTH)] + [ANY_SPEC] * nr,
        out_specs=[seq(ATT_WIDTH, 0), seq(ATT_WIDTH, 0), seq(ATT_WIDTH, 0), seq(LANES, 0)] + [ANY_SPEC] * nr,
        out_shape=[SDS((B, T, ATT_WIDTH), F32), SDS((B, T, ATT_WIDTH), BF16),
                   SDS((B, T, ATT_WIDTH), BF16), SDS((B, T, LANES), BF16)] + _exchange_shapes(ride_srcs, ride_modes),
        scratch_shapes=[pltpu.VMEM((T + WINDOW, LANES), BF16), pltpu.VMEM((T + WINDOW, LANES), BF16)]
        + _exchange_sems(nr),
        compiler_params=_params(("arbitrary", "arbitrary"), VMEM_LIMIT_BIG),
    )(proj3, proj3, proj3, cos, sinl, sinr, sinks, attn_w, *ride_srcs)


HG_GROUP = 8
HG_ROWS = HG_GROUP * HG_CHUNK


HG_STACK = HG_GROUP * HG_HEAD_DIM


def _group_mask():
    r = lax.broadcasted_iota(jnp.int32, (HG_ROWS, HG_ROWS), 0)
    c = lax.broadcasted_iota(jnp.int32, (HG_ROWS, HG_ROWS), 1)
    return ((r // HG_CHUNK) == (c // HG_CHUNK)) & (r >= c)


def _spread(a):
    blocks = []
    for c in range(HG_GROUP):
        above = jnp.zeros((c * HG_CHUNK, HG_HEAD_DIM), a.dtype)
        below = jnp.zeros(((HG_GROUP - 1 - c) * HG_CHUNK, HG_HEAD_DIM), a.dtype)
        blocks.append(jnp.concatenate([p for p in (above, a[_chunk_rows(c), :], below) if p.shape[0]], axis=0))
    return jnp.concatenate(blocks, axis=1)


def _pick(r):
    return jnp.concatenate([r[_chunk_rows(c), c * HG_HEAD_DIM:(c + 1) * HG_HEAD_DIM] for c in range(HG_GROUP)], axis=0)


def _lane_block(a, c):
    return a[:, c * HG_HEAD_DIM:(c + 1) * HG_HEAD_DIM]


def _chunk_cumsum(a, reverse=False):
    n = a.shape[0]
    pos = lax.broadcasted_iota(jnp.int32, a.shape, 0) % HG_CHUNK
    shift = 1
    while shift < HG_CHUNK:
        if reverse:
            a = a + jnp.where(pos < HG_CHUNK - shift, pltpu.roll(a, n - shift, 0), 0.0)
        else:
            a = a + jnp.where(pos >= shift, pltpu.roll(a, shift, 0), 0.0)
        shift *= 2
    return a


def _chunk_bcast(rows_1x128):
    return jnp.concatenate([jnp.broadcast_to(r, (HG_CHUNK, HG_HEAD_DIM)) for r in rows_1x128], axis=0)


def _hgrn_gates(hq, hf, lb):
    sq = _sigmoid(hq)
    q = hq * sq
    sg = _sigmoid(hf)
    f = lb + (1.0 - lb) * sg
    k = 1.0 - f
    logf = jnp.log(f)
    b = _chunk_cumsum(logf)
    bl = [_sum_rows(logf[_chunk_rows(c), :]) for c in range(HG_GROUP)]
    eb, enb, e2 = jnp.exp(b), jnp.exp(-b), jnp.exp(_chunk_bcast(bl) - b)
    ebl = [jnp.exp(r) for r in bl]
    return dict(sq=sq, sg=sg, f=f, eb=eb, enb=enb, e2=e2, ebl=ebl, qd=q * eb, kd=k * enb, k2=k * e2)


def _chunk_rows(c):
    return slice(c * HG_CHUNK, (c + 1) * HG_CHUNK)


def _head_lanes(h):
    return slice(h * HG_HEAD_DIM, (h + 1) * HG_HEAD_DIM)


def _hgrn_fwd(proj_h, lb, hg_w, ride_srcs, ride_modes):
    B, T, _ = proj_h.shape
    ng = T // HG_ROWS
    nr = len(ride_srcs)

    def body(*refs):
        hq_ref, hf_ref, hi_ref, hg_ref, lb_ref, gw_ref = refs[:6]
        ride_in = refs[6:6 + nr]
        o_ref, rg_ref, sp_ref = refs[6 + nr:9 + nr]
        ride_out = refs[9 + nr:9 + 2 * nr]
        st = refs[9 + 2 * nr]
        sems = refs[10 + 2 * nr:]
        gi = pl.program_id(1)
        step = pl.program_id(0) * ng + gi
        _ride_start(ride_modes, step, B * ng, ride_in, ride_out, sems)

        @pl.when(gi == 0)
        def _():
            st[...] = jnp.zeros(st.shape, F32)

        lo = _group_mask()
        for h in range(HG_HEADS):
            lanes = _head_lanes(h)
            gt = _hgrn_gates(hq_ref[:, lanes], hf_ref[:, lanes], lb_ref[:, lanes])
            v, qd, kd = _bf(hi_ref[:, lanes]), _bf(gt["qd"]), _bf(gt["kd"])
            a = jnp.where(lo, _dot_nt(qd, kd), 0.0)
            kv = _dot_tn(v, _spread(_bf(gt["k2"])))
            s = st[h]
            before = []
            for c in range(HG_GROUP):
                before.append(s)
                s = s * gt["ebl"][c] + _lane_block(kv, c)
            st[h] = s
            sp = jnp.concatenate(before, axis=1)
            sp_ref[h] = sp
            o = _dot(_bf(a), v) + _dot_nt(_spread(qd), _bf(sp))
            o_ref[:, lanes] = o
            hg = hg_ref[:, lanes]
            rn = o * lax.rsqrt(_mean_last(o * o) + EPS) * gw_ref[...]
            rg_ref[:, lanes] = _bf(rn * (hg * _sigmoid(hg)))
        _ride_wait(ride_modes, step, B * ng, ride_in, ride_out, sems)

    part = lambda j: pl.BlockSpec((None, HG_ROWS, HG_WIDTH), lambda b, g: (b, g, j))
    return pl.pallas_call(
        body, name="hgrn_fwd", grid=(B, ng),
        in_specs=[part(0), part(1), part(2), part(3),
                  pl.BlockSpec((1, HG_WIDTH), lambda b, g: (0, 0)),
                  pl.BlockSpec((1, LANES), lambda b, g: (0, 0))] + [ANY_SPEC] * nr,
        out_specs=[part(0), part(0),
                   pl.BlockSpec((None, HG_HEADS, None, HG_HEAD_DIM, HG_STACK), lambda b, g: (b, 0, g, 0, 0))]
        + [ANY_SPEC] * nr,
        out_shape=[SDS((B, T, HG_WIDTH), F32), SDS((B, T, HG_WIDTH), BF16),
                   SDS((B, HG_HEADS, ng, HG_HEAD_DIM, HG_STACK), F32)] + _exchange_shapes(ride_srcs, ride_modes),
        scratch_shapes=[pltpu.VMEM((HG_HEADS, HG_HEAD_DIM, HG_HEAD_DIM), F32)] + _exchange_sems(nr),
        compiler_params=_params(("arbitrary", "arbitrary"), VMEM_LIMIT_BIG),
    )(proj_h, proj_h, proj_h, proj_h, lb, hg_w, *ride_srcs)


def _mix_out(x2, attn_n, rec_g, mod8, post_w, w_out_bf, T, ride_srcs, ride_modes):
    N = x2.shape[0]
    TM = _tile_rows(T, big=True)
    tps = T // TM
    nr = len(ride_srcs)

    def body(*refs):
        x_ref, an_ref, rg_ref, mod_ref, pw_ref, w_ref = refs[:6]
        ride_in = refs[6:6 + nr]
        mix_ref, x1_ref, cat_ref = refs[6 + nr:9 + nr]
        ride_out = refs[9 + nr:9 + 2 * nr]
        sems = refs[9 + 2 * nr:]
        _ride_start(ride_modes, pl.program_id(0), N // TM, ride_in, ride_out, sems)
        cat = jnp.concatenate([an_ref[...], rg_ref[...]], axis=1)
        cat_ref[...] = cat
        mix = _dot(cat, w_ref[...])
        mix_ref[...] = mix
        r = lax.rsqrt(_mean_last(mix * mix) + EPS)
        x1_ref[...] = x_ref[...] + mod_ref[2:3, :] * (mix * r * pw_ref[...])
        _ride_wait(ride_modes, pl.program_id(0), N // TM, ride_in, ride_out, sems)

    row = lambda w: pl.BlockSpec((TM, w), lambda i: (i, 0))
    return pl.pallas_call(
        body, name="mix_out", grid=(N // TM,),
        in_specs=[row(D_MODEL), row(ATT_WIDTH), row(HG_WIDTH), _mod_spec(tps),
                  pl.BlockSpec((1, D_MODEL), lambda i: (0, 0)),
                  pl.BlockSpec((D_MODEL, D_MODEL), lambda i: (0, 0))] + [ANY_SPEC] * nr,
        out_specs=[row(D_MODEL), row(D_MODEL), row(D_MODEL)] + [ANY_SPEC] * nr,
        out_shape=[SDS((N, D_MODEL), F32), SDS((N, D_MODEL), F32), SDS((N, D_MODEL), BF16)]
        + _exchange_shapes(ride_srcs, ride_modes),
        scratch_shapes=_exchange_sems(nr),
        compiler_params=_params(("arbitrary",), VMEM_LIMIT_BIG),
    )(x2, attn_n, rec_g, mod8, post_w, w_out_bf, *ride_srcs)


def _load_weights_once(pairs, sem):
    @pl.when(pl.program_id(0) == 0)
    def _():
        cps = [pltpu.make_async_copy(src, dst, sem.at[i]) for i, (src, dst) in enumerate(pairs)]
        for cp in cps:
            cp.start()
        for cp in cps:
            cp.wait()


MLP_HALF = D_MODEL // 2
MLP_PIECES = 2 * N_DEV + 2


def _mlp_weight_pieces(wu_a, wu_b, wd_a, wd_b, wu, wd):
    cols = D_FF // N_DEV
    pairs = []
    for h, half in enumerate((wu_a, wu_b)):
        for j in range(N_DEV):
            pairs.append((half.at[j], wu.at[pl.ds(h * MLP_HALF, MLP_HALF), pl.ds(j * cols, cols)]))
    for h, half in enumerate((wd_a, wd_b)):
        pairs.append((half, wd.at[:, pl.ds(h * MLP_HALF, MLP_HALF)]))
    return pairs


def _mlp_fwd(x1, mod8, pre_w, w_up_halves, w_down_halves, T):
    N = x1.shape[0]
    TM = _tile_rows(T)
    tps = T // TM

    def body(x_ref, mod_ref, pw_ref, wua, wub, wda, wdb, up_ref, u_ref, d_ref, h2_ref, wu, wd, sem):
        _load_weights_once(_mlp_weight_pieces(wua, wub, wda, wdb, wu, wd), sem)
        x = x_ref[...]
        r = lax.rsqrt(_mean_last(x * x) + EPS)
        h = (x * r * pw_ref[...]) * (1.0 + mod_ref[4:5, :]) + mod_ref[3:4, :]
        hb = _bf(h)
        h2_ref[...] = hb
        up = _dot(hb, wu[...])
        up_ref[...] = up
        ru = jnp.maximum(up, 0.0)
        u = _bf(ru * ru)
        u_ref[...] = u
        d_ref[...] = _dot(u, wd[...])

    row = lambda w: pl.BlockSpec((TM, w), lambda i: (i, 0))
    return pl.pallas_call(
        body, name="mlp_fwd", grid=(N // TM,),
        in_specs=[row(D_MODEL), _mod_spec(tps), pl.BlockSpec((1, D_MODEL), lambda i: (0, 0))] + [ANY_SPEC] * 4,
        out_specs=[row(D_FF), row(D_FF), row(D_MODEL), row(D_MODEL)],
        out_shape=[SDS((N, D_FF), F32), SDS((N, D_FF), BF16), SDS((N, D_MODEL), F32), SDS((N, D_MODEL), BF16)],
        scratch_shapes=[pltpu.VMEM((D_MODEL, D_FF), BF16), pltpu.VMEM((D_FF, D_MODEL), BF16),
                        pltpu.SemaphoreType.DMA((MLP_PIECES,))],
        compiler_params=_params(("arbitrary",), VMEM_LIMIT_BIG),
    )(x1, mod8, pre_w, *w_up_halves, *w_down_halves)


def _acc_rows(acc_ref, first, rows):
    @pl.when(first)
    def _():
        acc_ref[...] = jnp.zeros(acc_ref.shape, F32)
    for i, r in enumerate(rows):
        acc_ref[i:i + 1, :] += r


def _mlp_bwd(x1, d, up, tgt, mod8, pre_w, post_w, w_up_halves, w_down_halves, T):
    N = x1.shape[0]
    TM = _tile_rows(T)
    tps = T // TM

    def body(x_ref, d_ref, up_ref, t_ref, mod_ref, pw_ref, qw_ref, wua, wub, wda, wdb,
             dx_ref, dup_ref, dd_ref, acc_ref, wd, wu, sem):
        _load_weights_once(_mlp_weight_pieces(wua, wub, wda, wdb, wu, wd), sem)
        sh2, sc2, g2 = mod_ref[3:4, :], mod_ref[4:5, :], mod_ref[5:6, :]
        x = x_ref[...]
        r1 = lax.rsqrt(_mean_last(x * x) + EPS)
        xh = x * r1
        n2 = xh * pw_ref[...]
        dv = d_ref[...]
        rd = lax.rsqrt(_mean_last(dv * dv) + EPS)
        dh = dv * rd
        rr = dh * qw_ref[...]
        e = x + g2 * rr - t_ref[...]
        loss = 0.5 * jnp.sum(_sum_rows(e * e), axis=1, keepdims=True) / D_MODEL
        dy = e * (1.0 / D_MODEL)
        dg2 = _sum_rows(dy * rr)
        drr = dy * g2
        dw_post = _sum_rows(drr * dh)
        ddh = drr * qw_ref[...]
        dd = _bf(rd * (ddh - dh * _mean_last(ddh * dh)))
        dd_ref[...] = dd
        ru = jnp.maximum(up_ref[...], 0.0)
        dup = _bf(_dot_nt(dd, wd[...]) * (2.0 * ru))
        dup_ref[...] = dup
        dh2 = _dot_nt(dup, wu[...])
        dsh2 = _sum_rows(dh2)
        dsc2 = _sum_rows(dh2 * n2)
        dn2 = dh2 * (1.0 + sc2)
        dw_pre = _sum_rows(dn2 * xh)
        dxh = dn2 * pw_ref[...]
        dx_ref[...] = dy + r1 * (dxh - xh * _mean_last(dxh * xh))
        _acc_rows(acc_ref, pl.program_id(0) % tps == 0,
                  [dsh2, dsc2, dg2, dw_pre, dw_post, jnp.broadcast_to(loss, (1, D_MODEL))])

    row = lambda w: pl.BlockSpec((TM, w), lambda i: (i, 0))
    vec = pl.BlockSpec((1, D_MODEL), lambda i: (0, 0))
    B = N // T
    return pl.pallas_call(
        body, name="mlp_bwd", grid=(N // TM,),
        in_specs=[row(D_MODEL), row(D_MODEL), row(D_FF), row(D_MODEL), _mod_spec(tps), vec, vec] + [ANY_SPEC] * 4,
        out_specs=[row(D_MODEL), row(D_FF), row(D_MODEL), _mod_spec(tps)],
        out_shape=[SDS((N, D_MODEL), F32), SDS((N, D_FF), BF16), SDS((N, D_MODEL), BF16),
                   SDS((B, 8, D_MODEL), F32)],
        scratch_shapes=[pltpu.VMEM((D_FF, D_MODEL), BF16), pltpu.VMEM((D_MODEL, D_FF), BF16),
                        pltpu.SemaphoreType.DMA((MLP_PIECES,))],
        compiler_params=_params(("arbitrary",), VMEM_LIMIT_BIG),
    )(x1, d, up, tgt, mod8, pre_w, post_w, *w_up_halves, *w_down_halves)


def _mix_bwd(mix, dx1, mod8, post_w, w_out_bf, T, ride_srcs, ride_modes):
    N = mix.shape[0]
    TM = _tile_rows(T, big=True)
    tps = T // TM
    nr = len(ride_srcs)

    def body(*refs):
        mix_ref, dx_ref, mod_ref, pw_ref, w_ref = refs[:5]
        ride_in = refs[5:5 + nr]
        dan_ref, drg_ref, dmix_ref, acc_ref = refs[5 + nr:9 + nr]
        ride_out = refs[9 + nr:9 + 2 * nr]
        sems = refs[9 + 2 * nr:]
        _ride_start(ride_modes, pl.program_id(0), N // TM, ride_in, ride_out, sems)
        g1 = mod_ref[2:3, :]
        mix = mix_ref[...]
        dx1 = dx_ref[...]
        rm = lax.rsqrt(_mean_last(mix * mix) + EPS)
        mh = mix * rm
        dg1 = _sum_rows(dx1 * (mh * pw_ref[...]))
        dr = dx1 * g1
        dw_post = _sum_rows(dr * mh)
        dmh = dr * pw_ref[...]
        dmix = _bf(rm * (dmh - mh * _mean_last(dmh * mh)))
        dmix_ref[...] = dmix
        dcat = _dot_nt(dmix, w_ref[...])
        dan_ref[...] = dcat[:, :ATT_WIDTH]
        drg_ref[...] = dcat[:, ATT_WIDTH:]
        _acc_rows(acc_ref, pl.program_id(0) % tps == 0, [dg1, dw_post])
        _ride_wait(ride_modes, pl.program_id(0), N // TM, ride_in, ride_out, sems)

    row = lambda w: pl.BlockSpec((TM, w), lambda i: (i, 0))
    B = N // T
    return pl.pallas_call(
        body, name="mix_bwd", grid=(N // TM,),
        in_specs=[row(D_MODEL), row(D_MODEL), _mod_spec(tps), pl.BlockSpec((1, D_MODEL), lambda i: (0, 0)),
                  pl.BlockSpec((D_MODEL, D_MODEL), lambda i: (0, 0))] + [ANY_SPEC] * nr,
        out_specs=[row(ATT_WIDTH), row(HG_WIDTH), row(D_MODEL), _mod_spec(tps)] + [ANY_SPEC] * nr,
        out_shape=[SDS((N, ATT_WIDTH), F32), SDS((N, HG_WIDTH), F32), SDS((N, D_MODEL), BF16),
                   SDS((B, 8, D_MODEL), F32)] + _exchange_shapes(ride_srcs, ride_modes),
        scratch_shapes=_exchange_sems(nr),
        compiler_params=_params(("arbitrary",), VMEM_LIMIT_BIG),
    )(mix, dx1, mod8, post_w, w_out_bf, *ride_srcs)


def _hgrn_bwd(proj_h, lb, hg_w, o, s_prev, drg, ride_srcs, ride_modes):
    B, T, _ = proj_h.shape
    ng = T // HG_ROWS
    nr = len(ride_srcs)

    def body(*refs):
        hq_ref, hf_ref, hi_ref, hg_ref, lb_ref, gw_ref, o_ref, sp_ref, drg_ref = refs[:9]
        ride_in = refs[9:9 + nr]
        dhq_ref, dhf_ref, dhi_ref, dhg_ref, dlb_ref, dgw_ref = refs[9 + nr:15 + nr]
        ride_out = refs[15 + nr:15 + 2 * nr]
        dst = refs[15 + 2 * nr]
        sems = refs[16 + 2 * nr:]
        step = pl.program_id(0) * ng + pl.program_id(1)
        _ride_start(ride_modes, step, B * ng, ride_in, ride_out, sems)

        @pl.when(pl.program_id(1) == 0)
        def _():
            dst[...] = jnp.zeros(dst.shape, F32)
            dlb_ref[...] = jnp.zeros(dlb_ref.shape, F32)
            dgw_ref[...] = jnp.zeros(dgw_ref.shape, F32)

        lo = _group_mask()
        gw = gw_ref[...]

        for h in range(HG_HEADS):
            lanes = _head_lanes(h)
            lbv = lb_ref[:, lanes]
            hq = hq_ref[:, lanes]
            gt = _hgrn_gates(hq, hf_ref[:, lanes], lbv)
            sq, sg, qdf, kdf, k2f, ebl = gt["sq"], gt["sg"], gt["qd"], gt["kd"], gt["k2"], gt["ebl"]
            v, qd, kd = _bf(hi_ref[:, lanes]), _bf(qdf), _bf(kdf)
            ov = o_ref[:, lanes]
            hg = hg_ref[:, lanes]
            shg = _sigmoid(hg)
            dr = drg_ref[:, lanes]
            ro = lax.rsqrt(_mean_last(ov * ov) + EPS)
            oh = ov * ro
            dhg_ref[:, lanes] = _bf(dr * (oh * gw) * (shg + hg * shg * (1.0 - shg)))
            drn = dr * (hg * shg)
            dgw_ref[...] += jnp.broadcast_to(_sum_rows(drn * oh), (8, LANES))
            doh = drn * gw
            do = _bf(ro * (doh - oh * _mean_last(doh * oh)))
            a = jnp.where(lo, _dot_nt(qd, kd), 0.0)
            da = _bf(jnp.where(lo, _dot_nt(do, v), 0.0))
            dv = _dot_tn(_bf(a), do)
            dqd = _dot(da, kd)
            dkd = _dot_tn(da, qd)
            sp = sp_ref[h]
            incr = _dot_tn(do, _spread(qd))
            ds = dst[h]
            after = [None] * HG_GROUP
            for c in reversed(range(HG_GROUP)):
                after[c] = ds
                ds = ds * ebl[c] + _lane_block(incr, c)
            dst[h] = ds
            dss = jnp.concatenate(after, axis=1)
            dssb = _bf(dss)
            dk2 = _pick(_dot(v, dssb))
            dhi_ref[:, lanes] = _bf(dv + _dot_nt(_spread(_bf(k2f)), dssb))
            dqd = dqd + _pick(_dot(do, _bf(sp)))
            debl = _sum_rows(dss * sp)
            k2g = dk2 * k2f
            db = dqd * qdf - dkd * kdf - k2g
            dk = dkd * gt["enb"] + dk2 * gt["e2"]
            dbl = _chunk_bcast([_lane_block(debl, c) * ebl[c] + _sum_rows(k2g[_chunk_rows(c), :])
                                for c in range(HG_GROUP)])
            dg = _chunk_cumsum(db, reverse=True) + dbl
            df = dg / gt["f"] - dk
            dhf_ref[:, lanes] = _bf(df * (1.0 - lbv) * sg * (1.0 - sg))
            dlb_ref[:, lanes] += jnp.broadcast_to(_sum_rows(df * (1.0 - sg)), (8, LANES))
            dhq_ref[:, lanes] = _bf((dqd * gt["eb"]) * (sq + hq * sq * (1.0 - sq)))
        _ride_wait(ride_modes, step, B * ng, ride_in, ride_out, sems)

    part = lambda j: pl.BlockSpec((None, HG_ROWS, HG_WIDTH), lambda b, g: (b, ng - 1 - g, j))
    return pl.pallas_call(
        body, name="hgrn_bwd", grid=(B, ng),
        in_specs=[part(0), part(1), part(2), part(3),
                  pl.BlockSpec((1, HG_WIDTH), lambda b, g: (0, 0)),
                  pl.BlockSpec((1, LANES), lambda b, g: (0, 0)),
                  part(0),
                  pl.BlockSpec((None, HG_HEADS, None, HG_HEAD_DIM, HG_STACK), lambda b, g: (b, 0, ng - 1 - g, 0, 0)),
                  part(0)] + [ANY_SPEC] * nr,
        out_specs=[part(0), part(0), part(0), part(0),
                   pl.BlockSpec((None, 8, HG_WIDTH), lambda b, g: (b, 0, 0)),
                   pl.BlockSpec((None, 8, LANES), lambda b, g: (b, 0, 0))] + [ANY_SPEC] * nr,
        out_shape=[SDS((B, T, HG_WIDTH), BF16)] * 4 + [SDS((B, 8, HG_WIDTH), F32), SDS((B, 8, LANES), F32)]
        + _exchange_shapes(ride_srcs, ride_modes),
        scratch_shapes=[pltpu.VMEM((HG_HEADS, HG_HEAD_DIM, HG_HEAD_DIM), F32)] + _exchange_sems(nr),
        compiler_params=_params(("arbitrary", "arbitrary"), VMEM_LIMIT_BIG),
    )(proj_h, proj_h, proj_h, proj_h, lb, hg_w, o, s_prev, drg, *ride_srcs)


def _attn_bwd(qr, kr, proj3, attn_o, dan, tables, sinks, attn_w, ride_srcs, ride_modes):
    B, T, _ = proj3.shape
    nb = T // WINDOW
    splits = min(ATT_SPLITS, nb)
    per = nb // splits
    nr = len(ride_srcs)
    cos, sinl, sinr = tables
    QKV = ATT_WIDTH + 2 * LANES

    def body(*refs):
        qr_ref, kr_ref, v_ref, o_ref, dan_ref, cos_ref, sl_ref, sr_ref, sink_ref, aw_ref = refs[:10]
        ride_in = refs[10:10 + nr]
        dqkv_ref, dsink_ref, daw_ref = refs[10 + nr:13 + nr]
        ride_out = refs[13 + nr:13 + 2 * nr]
        kpad, vpad, dkpad, dvpad, dqb, dsk = refs[13 + 2 * nr:19 + 2 * nr]
        sems = refs[19 + 2 * nr:]
        part = pl.program_id(1)
        step = pl.program_id(0) * splits + part
        _ride_start(ride_modes, step, B * splits, ride_in, ride_out, sems)

        @pl.when(part == 0)
        def _():
            kpad[0:WINDOW, :] = jnp.zeros((WINDOW, LANES), BF16)
            vpad[0:WINDOW, :] = jnp.zeros((WINDOW, LANES), BF16)
            kpad[WINDOW:, :] = kr_ref[...]
            vpad[WINDOW:, :] = _bf(v_ref[...])
            dkpad[...] = jnp.zeros(dkpad.shape, F32)
            dvpad[...] = jnp.zeros(dvpad.shape, F32)
            dsk[...] = jnp.zeros(dsk.shape, F32)
            daw_ref[...] = jnp.zeros(daw_ref.shape, F32)

        lower = _lower_mask()
        aw = aw_ref[...]

        def block(n, daw):
            r0 = pl.multiple_of(n * WINDOW, WINDOW)
            rows = pl.ds(r0, WINDOW)
            nxt = pl.ds(r0 + WINDOW, WINDOW)
            ob = o_ref[rows, :]
            dn = dan_ref[rows, :]
            ro = lax.rsqrt(_mean_last(ob * ob) + EPS)
            oh = ob * ro
            daw = daw + _sum_rows(dn * oh)
            doh = dn * aw
            do = _bf(ro * (doh - oh * _mean_last(doh * oh)))
            doparts = [do[:, j * LANES:(j + 1) * LANES] for j in range(ATT_WIDTH // LANES)]
            qparts = [qr_ref[rows, j * LANES:(j + 1) * LANES] for j in range(ATT_WIDTH // LANES)]
            for hk in range(ATT_KV_HEADS):
                lanes = slice(hk * ATT_HEAD_DIM, (hk + 1) * ATT_HEAD_DIM)
                qs = _stack_heads(qparts, hk)
                dos = _stack_heads(doparts, hk)
                k_cur, k_prev = kpad[nxt, lanes], kpad[rows, lanes]
                v_cur, v_prev = vpad[nxt, lanes], vpad[rows, lanes]
                p, inv, es = _softmax_window(qs, k_cur, k_prev, lower, n > 0, _sink_row(sink_ref, hk))
                p = p * inv
                dp = jnp.where(lower, _dot_nt(v_cur, dos), _dot_nt(v_prev, dos))
                delta = jnp.sum(p * dp, axis=0, keepdims=True)
                ds = p * (dp - delta)
                sk = (es * inv) * delta
                ds_cur = jnp.where(lower, ds, 0.0)
                p_cur = jnp.where(lower, p, 0.0)
                ds_cur, ds_prev = _bf(ds_cur), _bf(ds - ds_cur)
                p_cur, p_prev = _bf(p_cur), _bf(p - p_cur)
                dqt = (_dot_tn(k_cur, ds_cur) + _dot_tn(k_prev, ds_prev)) * ATT_SCALE
                dkpad[nxt, lanes] += _dot(ds_cur, qs)
                dkpad[rows, lanes] += _dot(ds_prev, qs)
                dvpad[nxt, lanes] += _dot(p_cur, dos)
                dvpad[rows, lanes] += _dot(p_prev, dos)
                for g in range(ATT_GROUP):
                    h = ATT_GROUP * hk + g
                    cols = slice(g * WINDOW, (g + 1) * WINDOW)
                    dqb[:, h * ATT_HEAD_DIM:(h + 1) * ATT_HEAD_DIM] = dqt[:, cols].T
                    head_lane = lax.broadcasted_iota(jnp.int32, dsk.shape, 1) == h
                    dsk[...] += jnp.where(head_lane, -jnp.sum(sk[:, cols], axis=1, keepdims=True), 0.0)
            cs, sl, sr = cos_ref[rows, :], sl_ref[rows, :], sr_ref[rows, :]
            for j in range(ATT_WIDTH // LANES):
                dqkv_ref[rows, j * LANES:(j + 1) * LANES] = _bf(_rope_t(dqb[:, j * LANES:(j + 1) * LANES], cs, sl, sr))
            return daw

        daw = _loop_pairs(part * per, per, block, jnp.zeros((1, ATT_WIDTH), F32))
        daw_ref[...] += jnp.broadcast_to(daw, (8, ATT_WIDTH))
        dsink_ref[...] = dsk[...]

        def finish(n, carry):
            r0 = pl.multiple_of(n * WINDOW, WINDOW)
            rows = pl.ds(r0, WINDOW)
            nxt = pl.ds(r0 + WINDOW, WINDOW)
            cs, sl, sr = cos_ref[rows, :], sl_ref[rows, :], sr_ref[rows, :]
            dqkv_ref[rows, ATT_WIDTH:ATT_WIDTH + LANES] = _bf(_rope_t(dkpad[nxt, :], cs, sl, sr))
            dqkv_ref[rows, ATT_WIDTH + LANES:QKV] = _bf(dvpad[nxt, :])
            return carry

        @pl.when(part == splits - 1)
        def _():
            lax.fori_loop(0, nb, finish, 0)

        _ride_wait(ride_modes, step, B * splits, ride_in, ride_out, sems)

    seq = lambda w, j: pl.BlockSpec((None, T, w), lambda b, s: (b, 0, j))
    full = lambda r, w: pl.BlockSpec((r, w), lambda b, s: (0, 0))
    return pl.pallas_call(
        body, name="attn_bwd", grid=(B, splits),
        in_specs=[seq(ATT_WIDTH, 0), seq(LANES, 0), seq(LANES, 5), seq(ATT_WIDTH, 0), seq(ATT_WIDTH, 0),
                  full(T, LANES), full(T, LANES), full(T, LANES),
                  pl.BlockSpec(memory_space=pltpu.SMEM), full(1, ATT_WIDTH)] + [ANY_SPEC] * nr,
        out_specs=[seq(QKV, 0), pl.BlockSpec((None, 8, LANES), lambda b, s: (b, 0, 0)),
                   pl.BlockSpec((None, 8, ATT_WIDTH), lambda b, s: (b, 0, 0))] + [ANY_SPEC] * nr,
        out_shape=[SDS((B, T, QKV), BF16), SDS((B, 8, LANES), F32), SDS((B, 8, ATT_WIDTH), F32)]
        + _exchange_shapes(ride_srcs, ride_modes),
        scratch_shapes=[pltpu.VMEM((T + WINDOW, LANES), BF16), pltpu.VMEM((T + WINDOW, LANES), BF16),
                        pltpu.VMEM((T + WINDOW, LANES), F32), pltpu.VMEM((T + WINDOW, LANES), F32),
                        pltpu.VMEM((WINDOW, ATT_WIDTH), F32), pltpu.VMEM((8, LANES), F32)] + _exchange_sems(nr),
        compiler_params=_params(("arbitrary", "arbitrary"), VMEM_LIMIT_BIG),
    )(qr, kr, proj3, attn_o, dan, cos, sinl, sinr, sinks, attn_w, *ride_srcs)


def _in_bwd(x2, dx1, dqkv, dhq, dhf, dhi, dhg, mod8, pre_w, w_in_bf, T, ride_srcs, ride_modes):
    N = x2.shape[0]
    TM = _tile_rows(T, big=True)
    tps = T // TM
    nr = len(ride_srcs)
    pieces = [(0, ATT_WIDTH + 2 * LANES), (768, HG_WIDTH), (1280, HG_WIDTH), (1792, HG_WIDTH), (2304, HG_WIDTH)]

    def body(*refs):
        x_ref, dx_ref, p0, p1, p2, p3, p4, mod_ref, pw_ref, w_ref = refs[:10]
        ride_in = refs[10:10 + nr]
        gx_ref, dproj_ref, acc_ref = refs[10 + nr:13 + nr]
        ride_out = refs[13 + nr:13 + 2 * nr]
        sems = refs[13 + 2 * nr:]
        _ride_start(ride_modes, pl.program_id(0), N // TM, ride_in, ride_out, sems)
        sc1 = mod_ref[1:2, :]
        dh = jnp.zeros((TM, D_MODEL), F32)
        for ref, (off, width) in zip((p0, p1, p2, p3, p4), pieces):
            pb = ref[...]
            dproj_ref[:, off:off + width] = pb
            dh = dh + _dot(pb, w_ref[off:off + width, :])
        x = x_ref[...]
        r = lax.rsqrt(_mean_last(x * x) + EPS)
        xh = x * r
        n1 = xh * pw_ref[...]
        dsh1 = _sum_rows(dh)
        dsc1 = _sum_rows(dh * n1)
        dn1 = dh * (1.0 + sc1)
        dw_pre = _sum_rows(dn1 * xh)
        dxh = dn1 * pw_ref[...]
        gx_ref[...] = dx_ref[...] + r * (dxh - xh * _mean_last(dxh * xh))
        _acc_rows(acc_ref, pl.program_id(0) % tps == 0, [dsh1, dsc1, dw_pre])
        _ride_wait(ride_modes, pl.program_id(0), N // TM, ride_in, ride_out, sems)

    row = lambda w: pl.BlockSpec((TM, w), lambda i: (i, 0))
    B = N // T
    return pl.pallas_call(
        body, name="in_bwd", grid=(N // TM,),
        in_specs=[row(D_MODEL), row(D_MODEL), row(768), row(HG_WIDTH), row(HG_WIDTH), row(HG_WIDTH),
                  row(HG_WIDTH), _mod_spec(tps), pl.BlockSpec((1, D_MODEL), lambda i: (0, 0)),
                  pl.BlockSpec((IN_COLS, D_MODEL), lambda i: (0, 0))] + [ANY_SPEC] * nr,
        out_specs=[row(D_MODEL), row(IN_COLS), _mod_spec(tps)] + [ANY_SPEC] * nr,
        out_shape=[SDS((N, D_MODEL), F32), SDS((N, IN_COLS), BF16), SDS((B, 8, D_MODEL), F32)]
        + _exchange_shapes(ride_srcs, ride_modes),
        scratch_shapes=_exchange_sems(nr),
        compiler_params=_params(("arbitrary",), VMEM_LIMIT_BIG),
    )(x2, dx1, dqkv, dhq, dhf, dhi, dhg, mod8, pre_w, w_in_bf, *ride_srcs)


def _matmul_tn(name, a, b, tn, tm):
    K, M = a.shape
    Nc = b.shape[1]

    def body(a_ref, b_ref, o_ref):
        o_ref[...] = _bf(_dot_tn(a_ref[...], b_ref[...]))

    return pl.pallas_call(
        body, name=name, grid=(M // tm, Nc // tn),
        in_specs=[pl.BlockSpec((K, tm), lambda i, j: (0, i)),
                  pl.BlockSpec((K, tn), lambda i, j: (0, j))],
        out_specs=pl.BlockSpec((tm, tn), lambda i, j: (i, j)), out_shape=SDS((M, Nc), BF16),
        compiler_params=_params(("arbitrary", "arbitrary"), VMEM_LIMIT_BIG),
    )(a, b)


def _matmul_tn_paired(name, a, b, stream_a):
    K = a.shape[0]
    stream, fixed = (a, b) if stream_a else (b, a)
    w = stream.shape[1] // N_DEV
    blk = (w, fixed.shape[1]) if stream_a else (fixed.shape[1], w)
    chips = N_DEV // 2

    def body(s_hbm, f_hbm, out_ref, s_buf, f_buf, g_buf, theirs, in_sems, send_sems, recv_sems):
        core = lax.axis_index("c")
        sib_dev, _ = _related(SIBLING)
        fixed_load = pltpu.make_async_copy(f_hbm, f_buf, in_sems.at[2])

        def load(j):
            owner = 2 * (j // 2) + (core if j % 2 else 1 - core)
            return pltpu.make_async_copy(s_hbm.at[:, pl.ds(pl.multiple_of(owner * w, LANES), w)], s_buf.at[j % 2],
                                         in_sems.at[j % 2])

        def swap(s):
            return pltpu.make_async_remote_copy(
                src_ref=g_buf.at[s, 0], dst_ref=theirs.at[s], send_sem=send_sems.at[s],
                recv_sem=recv_sems.at[s], device_id=sib_dev, device_id_type=MESH)

        fixed_load.start()
        load(0).start()
        fixed_load.wait()
        for j in range(N_DEV):
            s, mine = divmod(j, 2)
            load(j).wait()
            if j + 1 < N_DEV:
                load(j + 1).start()
            if stream_a:
                g_buf[s, mine] = _bf(_dot_tn(s_buf[j % 2], f_buf[...]))
            else:
                g_buf[s, mine] = _bf(_dot_tn(f_buf[...], s_buf[j % 2]))
            if mine:
                swap(s).wait_recv()
                out_ref[s] = _bf(g_buf[s, 1].astype(F32) + theirs[s].astype(F32))
            else:
                swap(s).start()
        for s in range(chips):
            swap(s).wait_send()

    return pl.pallas_call(
        body, name=name, in_specs=[ANY_SPEC] * 2, out_shape=SDS((chips,) + blk, BF16),
        scratch_shapes=[pltpu.VMEM((2, K, w), BF16), pltpu.VMEM(fixed.shape, BF16),
                        pltpu.VMEM((chips, 2) + blk, BF16), pltpu.VMEM((chips,) + blk, BF16),
                        pltpu.SemaphoreType.DMA((3,)), pltpu.SemaphoreType.DMA((chips,)),
                        pltpu.SemaphoreType.DMA((chips,))],
        compiler_params=_params(None, VMEM_LIMIT_BIG),
    )(stream, fixed)


GW_BLOCK = IN_COLS // N_DEV
GW_HALF = IN_COLS // 2


def _grad_w_in_reduced(dproj, h1, ride_srcs, ride_modes):
    K = dproj.shape[0]
    nr = len(ride_srcs)
    chips = N_DEV // 2
    tn = 512

    def body(*refs):
        a_hbm, b_hbm = refs[:2]
        ride_in, out, ride_out = refs[2:2 + nr], refs[2 + nr], refs[3 + nr:3 + 2 * nr]
        a_buf, b_buf, g_buf, theirs, p_buf, in_sems, pair_send, pair_recv, chip_send, chip_recv, own_sem = \
            refs[3 + 2 * nr:14 + 2 * nr]
        ride = _exchange_phases(ride_modes, ride_in, ride_out, *refs[14 + 2 * nr:]) if nr else ([], [], [])
        x, y, core = lax.axis_index("x"), lax.axis_index("y"), lax.axis_index("c")
        chip = 2 * x + y
        sib_dev, _ = _related(SIBLING)

        def remote(src, dst, send_sem, recv_sem, dev):
            return pltpu.make_async_remote_copy(src_ref=src, dst_ref=dst, send_sem=send_sem, recv_sem=recv_sem,
                                                device_id=dev, device_id_type=MESH)

        halves = [1 - x, x]
        loads = [pltpu.make_async_copy(b_hbm, b_buf, in_sems.at[0])]
        for t in range(2):
            col = pl.multiple_of(halves[t] * GW_HALF, LANES)
            loads.append(pltpu.make_async_copy(a_hbm.at[:, pl.ds(col, GW_HALF)], a_buf.at[t], in_sems.at[1 + t]))
        for cp in loads:
            cp.start()
        _run(ride[0])
        loads[0].wait()
        end = []
        for t in range(2):
            loads[1 + t].wait()
            if t == 1:
                _run(ride[1])
            for j in range(D_MODEL // tn):
                cols = pl.ds(j * tn, tn)
                res = _dot_tn(a_buf[t], b_buf[:, cols])
                for q in range(2):
                    for cc in range(2):
                        r0 = (2 * q + cc) * GW_BLOCK
                        g_buf[t, q, cc, :, cols] = _bf(res[r0:r0 + GW_BLOCK])
                swaps = [remote(g_buf.at[t, q, 1 - core, :, cols], theirs.at[t, q, :, cols],
                                pair_send.at[t, 2 * j + q], pair_recv.at[t, 2 * j + q], sib_dev) for q in range(2)]
                for cp in swaps:
                    cp.start()
                for cp in swaps:
                    cp.wait_recv()
                end += [cp.wait_send for cp in swaps]
                for q in range(2):
                    p_buf[t, q, :, cols] = _bf(g_buf[t, q, core, :, cols].astype(F32)
                                               + theirs[t, q, :, cols].astype(F32))
                for dy in range(2):
                    k = 4 * (1 - t) + 2 * dy
                    if k == 0:
                        own = pltpu.make_async_copy(p_buf.at[t, y, :, cols], out.at[chip, :, cols], own_sem.at[j])
                        own.start()
                        end.append(own.wait)
                        continue
                    dev, peer = _related(k)
                    sems = chip_send.at[k // 2, j], chip_recv.at[k // 2, j]
                    send = remote(p_buf.at[t, y ^ dy, :, cols], out.at[chip, :, cols], *sems, dev)
                    send.start()
                    end += [remote(p_buf.at[t, y ^ dy, :, cols], out.at[peer // 2, :, cols], *sems, dev).wait_recv,
                            send.wait_send]
        _run(ride[2])
        _run(end)

    return pl.pallas_call(
        body, name="grad_w_in",
        in_specs=[ANY_SPEC] * (2 + nr), out_specs=[ANY_SPEC] * (1 + nr),
        out_shape=[SDS((chips, GW_BLOCK, D_MODEL), BF16)] + _exchange_shapes(ride_srcs, ride_modes),
        scratch_shapes=[pltpu.VMEM((2, K, GW_HALF), BF16), pltpu.VMEM((K, D_MODEL), BF16),
                        pltpu.VMEM((2, 2, 2, GW_BLOCK, D_MODEL), BF16), pltpu.VMEM((2, 2, GW_BLOCK, D_MODEL), BF16),
                        pltpu.VMEM((2, 2, GW_BLOCK, D_MODEL), BF16), pltpu.SemaphoreType.DMA((3,)),
                        pltpu.SemaphoreType.DMA((2, 4)), pltpu.SemaphoreType.DMA((2, 4)),
                        pltpu.SemaphoreType.DMA((chips, 2)), pltpu.SemaphoreType.DMA((chips, 2)),
                        pltpu.SemaphoreType.DMA((2,))] + _exchange_sems(nr),
        compiler_params=_params(None, VMEM_LIMIT_BIG),
    )(dproj, h1, *ride_srcs)


def _adamw_math(w, g, m, v):
    m2 = ADAM_B1 * m + (1.0 - ADAM_B1) * g
    v2 = ADAM_B2 * v + (1.0 - ADAM_B2) * (g * g)
    m_hat = m2 / (1.0 - ADAM_B1 ** ADAM_STEP)
    v_hat = v2 / (1.0 - ADAM_B2 ** ADAM_STEP)
    delta = -ADAM_LR * (m_hat / (jnp.sqrt(v_hat) + ADAM_EPS) + ADAM_WD * w)
    return delta, m2, v2


def _reduce_adamw(name, parts, w, m, v):
    r, c = w.shape
    tr = r if r % 256 else 256
    slots = parts.shape[0]

    def body(p_ref, w_ref, m_ref, v_ref, g_ref, d_ref, m2_ref, v2_ref):
        g = p_ref[0].astype(F32)
        for s in range(1, slots):
            g = g + p_ref[s].astype(F32)
        g_ref[...] = g
        d_ref[...], m2_ref[...], v2_ref[...] = _adamw_math(w_ref[...], g, m_ref[...], v_ref[...])

    blk = pl.BlockSpec((tr, c), lambda i: (i, 0))
    return pl.pallas_call(
        body, name=name, grid=(r // tr,),
        in_specs=[pl.BlockSpec((slots, tr, c), lambda i: (0, i, 0)), blk, blk, blk],
        out_specs=[blk] * 4, out_shape=[SDS((r, c), F32)] * 4,
        compiler_params=_params(("arbitrary",), VMEM_LIMIT_BIG),
    )(parts, w, m, v)


def _ada_grad_adamw(c_all, dmod_all, w, m, v):
    r, c = w.shape
    tr = 256
    nb = c_all.shape[0]

    def body(c_ref, dm_ref, w_ref, m_ref, v_ref, g_ref, d_ref, m2_ref, v2_ref):
        cv = c_ref[...]
        g = _dot_tn(cv * _sigmoid(cv), dm_ref[...])
        g_ref[...] = g
        d_ref[...], m2_ref[...], v2_ref[...] = _adamw_math(w_ref[...], g, m_ref[...], v_ref[...])

    blk = pl.BlockSpec((tr, c), lambda i: (i, 0))
    return pl.pallas_call(
        body, name="ada_grad_adamw", grid=(r // tr,),
        in_specs=[pl.BlockSpec((nb, tr), lambda i: (0, i)), pl.BlockSpec((nb, c), lambda i: (0, 0)),
                  blk, blk, blk],
        out_specs=[blk] * 4, out_shape=[SDS((r, c), F32)] * 4,
        compiler_params=_params(("arbitrary",)),
    )(c_all, dmod_all, w, m, v)


_SMALL = [("b_ada", 6144), ("pre_w_mix", 1024), ("attn_sinks", 128), ("attn_out_w", 512), ("lb_table", 1024),
          ("hg_norm_w", 128), ("post_w_mix", 1024), ("pre_w_mlp", 1024), ("post_w_mlp", 1024)]


def _pack_small(acc_in, acc_mix, acc_mlp, dsink, daw, dlb, dgw, lb_p, ada_cols):
    B = acc_in.shape[0]
    width = sum(w for _, w in _SMALL) + LANES

    def body(ain, amix, amlp, dsk_ref, daw_ref, dlb_ref, dgw_ref, lbp_ref, packed_ref, dmod_ref):
        def total(ref, r, w=None):
            out = ref[0, r:r + 1, :] if w is None else ref[0, r:r + 1, :w]
            for b in range(1, B):
                out = out + (ref[b, r:r + 1, :] if w is None else ref[b, r:r + 1, :w])
            return out

        d_b_ada = None
        for b in range(B):
            mods = [ain[b, 0:1, :], ain[b, 1:2, :], amix[b, 0:1, :], amlp[b, 0:1, :], amlp[b, 1:2, :], amlp[b, 2:3, :]]
            full = jnp.concatenate(mods, axis=1)
            for j in range(N_DEV):
                dmod_ref[j, b:b + 1, :] = full[:, j * ada_cols:(j + 1) * ada_cols]
            d_b_ada = full if d_b_ada is None else d_b_ada + full
        d_lb = total(dlb_ref, 0)
        pp = lbp_ref[0:1, :] * lbp_ref[1:2, :]
        pieces = [d_b_ada, total(ain, 2), total(dsk_ref, 0), total(daw_ref, 0), -d_lb * pp, d_lb * pp,
                  total(dgw_ref, 0), total(amix, 1), total(amlp, 3), total(amlp, 4), total(amlp, 5, LANES)]
        off = 0
        for piece in pieces:
            packed_ref[:, off:off + piece.shape[1]] = piece
            off += piece.shape[1]

    return pl.pallas_call(
        body, name="pack_small",
        out_shape=[SDS((1, width), F32), SDS((N_DEV, B, ada_cols), F32)],
    )(acc_in, acc_mix, acc_mlp, dsink, daw, dlb, dgw, lb_p)


def _adamw_small(parts, given):
    names = [n for n, _ in _SMALL]
    flat_in = [a for n in names for a in given[n]]

    def body(*refs):
        p_ref = refs[0]
        in_refs = refs[1:1 + 3 * len(names)]
        out_refs = refs[1 + 3 * len(names):-1]
        loss_ref = refs[-1]
        g = p_ref[0]
        for s in range(1, N_DEV):
            g = g + p_ref[s]
        off = 0
        for i, (name, width) in enumerate(_SMALL):
            w_ref, m_ref, v_ref = in_refs[3 * i:3 * i + 3]
            rows, cols = w_ref.shape
            for r in range(rows):
                gr = g[:, off + r * cols:off + (r + 1) * cols]
                res = (gr,) + _adamw_math(w_ref[r:r + 1, :], gr, m_ref[r:r + 1, :], v_ref[r:r + 1, :])
                for o_ref, val in zip(out_refs[4 * i:4 * i + 4], res):
                    o_ref[r:r + 1, :] = val
            off += width
        loss_ref[...] = g[:, off:off + LANES]

    out_shape = [SDS(given[n][0].shape, F32) for n in names for _ in range(4)] + [SDS((1, LANES), F32)]
    outs = pl.pallas_call(body, name="adamw_small", out_shape=out_shape)(parts, *flat_in)
    return {n: tuple(outs[4 * i:4 * i + 4]) for i, n in enumerate(names)}, outs[-1][0, 0]


def kernel(x, c, w_ada, b_ada, pre_w_mix, w_in, attn_sinks, attn_out_w, lb_table, hg_norm_w, w_out, post_w_mix, pre_w_mlp, w_up, w_down, post_w_mlp, loss_target, m_w_ada, m_b_ada, m_pre_w_mix, m_w_in, m_attn_sinks, m_attn_out_w, m_lb_table, m_hg_norm_w, m_w_out, m_post_w_mix, m_pre_w_mlp, m_w_up, m_w_down, m_post_w_mlp, v_w_ada, v_b_ada, v_pre_w_mix, v_w_in, v_attn_sinks, v_attn_out_w, v_lb_table, v_hg_norm_w, v_w_out, v_post_w_mix, v_pre_w_mlp, v_w_up, v_w_down, v_post_w_mlp):
    B, T, _ = x.shape
    N = B * T
    me = 4 * lax.axis_index("x") + 2 * lax.axis_index("y") + lax.axis_index("c")
    x2 = x.reshape(N, D_MODEL)
    tgt2 = loss_target.reshape(N, D_MODEL)

    w_in_t, m_w_in_t, v_w_in_t = w_in[0].T, m_w_in[0].T, v_w_in[0].T
    w_in_g, c_g = _exchange("gather_w_in", [_bf(w_in_t), c], ["gather"] * 2)
    w_in_f = w_in_g.reshape(IN_COLS, D_MODEL)
    c_all = c_g.reshape(N_DEV * B, D_MODEL)

    ada_cols = w_ada.shape[2]
    b_mine = lax.dynamic_slice(b_ada, (0, me * ada_cols), (1, ada_cols))
    mod_cols = _ada_mod(c_all, w_ada[0], b_mine)
    (mod_g,) = _exchange("scatter_mod", [mod_cols.reshape(N_DEV, B, ada_cols)], ["a2a"])
    mod = mod_g.transpose(1, 0, 2).reshape(B, 6, D_MODEL)
    mod8 = jnp.pad(mod, ((0, 0), (0, 2), (0, 0)))

    lb_p = jax.nn.softmax(lb_table, axis=0)
    lb = lb_p[1:2]
    tables = _rope_tables(T)

    w_up_b, w_down_b = _bf(w_up[0]), _bf(w_down[0])
    proj_a, proj_h, h1, w_out_g, w_up_g0 = _in_proj(x2, mod8, pre_w_mix, w_in_f, T,
                                                    [_bf(w_out[0]), w_up_b[:MLP_HALF]], ["gather"] * 2)
    proj3 = proj_a.reshape(B, T, ATT_COLS)
    proj_h = proj_h.reshape(B, T, IN_COLS - ATT_COLS)
    rec_o, rec_g, s_prev, w_up_g1 = _hgrn_fwd(proj_h, lb, hg_norm_w, [w_up_b[MLP_HALF:]], ["gather"])
    attn_o, attn_n, qr, kr, w_down_g0 = _attn_fwd(proj3, tables, attn_sinks, attn_out_w,
                                                  [w_down_b[:, :MLP_HALF]], ["gather"])
    w_out_f = w_out_g.reshape(D_MODEL, D_MODEL)
    mix, x1, cat, w_down_g1 = _mix_out(x2, attn_n.reshape(N, ATT_WIDTH), rec_g.reshape(N, HG_WIDTH), mod8,
                                       post_w_mix, w_out_f, T, [w_down_b[:, MLP_HALF:]], ["gather"])
    w_up_halves = [w_up_g0, w_up_g1]
    w_down_halves = [w_down_g0.reshape(D_FF, MLP_HALF), w_down_g1.reshape(D_FF, MLP_HALF)]
    up, u, d, h2 = _mlp_fwd(x1, mod8, pre_w_mlp, w_up_halves, w_down_halves, T)

    dx1, dup, dd, acc_mlp = _mlp_bwd(x1, d, up, tgt2, mod8, pre_w_mlp, post_w_mlp, w_up_halves, w_down_halves, T)
    p_up = _matmul_tn_paired("grad_w_up", h2, dup, stream_a=False)
    p_down = _matmul_tn_paired("grad_w_down", u, dd, stream_a=True)
    dan, drg, dmix, acc_mix = _mix_bwd(mix, dx1, mod8, post_w_mix, w_out_f, T, [], [])
    gw_out = _matmul_tn("grad_w_out", cat, dmix, 512, tm=D_MODEL).reshape(N_DEV, D_MODEL // N_DEV, D_MODEL)
    dhq, dhf, dhi, dhg, dlb_p, dgw_p, r_down, r_out = _hgrn_bwd(
        proj_h, lb, hg_norm_w, rec_o, s_prev, drg.reshape(B, T, HG_WIDTH), [p_down, gw_out], ["chips", "a2a"])
    dqkv, dsink_p, daw_p, r_up = _attn_bwd(qr, kr, proj3, attn_o, dan.reshape(B, T, ATT_WIDTH), tables,
                                           attn_sinks, attn_out_w, [p_up], ["chips"])
    flat = lambda a: a.reshape(N, a.shape[-1])
    grad_x, dproj, acc_in = _in_bwd(x2, dx1, flat(dqkv), flat(dhq), flat(dhf), flat(dhi), flat(dhg),
                                    mod8, pre_w_mix, w_in_f, T, [], [])

    packed, dmod_blocks = _pack_small(acc_in, acc_mix, acc_mlp, dsink_p, daw_p, dlb_p, dgw_p, lb_p, ada_cols)
    r_in, r_dmod, r_small = _grad_w_in_reduced(dproj, h1, [dmod_blocks, packed], ["a2a", "gather"])

    res = {}
    res["w_in"] = tuple(a.T for a in _reduce_adamw("adamw_w_in", r_in, w_in_t, m_w_in_t, v_w_in_t))
    res["w_out"] = _reduce_adamw("adamw_w_out", r_out, w_out[0], m_w_out[0], v_w_out[0])
    res["w_up"] = _reduce_adamw("adamw_w_up", r_up, w_up[0], m_w_up[0], v_w_up[0])
    res["w_down"] = _reduce_adamw("adamw_w_down", r_down, w_down[0], m_w_down[0], v_w_down[0])
    res["w_ada"] = _ada_grad_adamw(c_all, r_dmod.reshape(N_DEV * B, ada_cols), w_ada[0], m_w_ada[0], v_w_ada[0])

    given = dict(b_ada=(b_ada, m_b_ada, v_b_ada), pre_w_mix=(pre_w_mix, m_pre_w_mix, v_pre_w_mix),
                 attn_sinks=(attn_sinks, m_attn_sinks, v_attn_sinks),
                 attn_out_w=(attn_out_w, m_attn_out_w, v_attn_out_w), lb_table=(lb_table, m_lb_table, v_lb_table),
                 hg_norm_w=(hg_norm_w, m_hg_norm_w, v_hg_norm_w), post_w_mix=(post_w_mix, m_post_w_mix, v_post_w_mix),
                 pre_w_mlp=(pre_w_mlp, m_pre_w_mlp, v_pre_w_mlp), post_w_mlp=(post_w_mlp, m_post_w_mlp, v_post_w_mlp))
    small_res, loss = _adamw_small(r_small, given)
    res.update(small_res)

    order = ["w_ada", "b_ada", "pre_w_mix", "w_in", "attn_sinks", "attn_out_w", "lb_table", "hg_norm_w", "w_out",
             "post_w_mix", "pre_w_mlp", "w_up", "w_down", "post_w_mlp"]
    big = {"w_ada", "w_in", "w_out", "w_up", "w_down"}
    outs = [loss, grad_x.reshape(B, T, D_MODEL)]
    for i in range(4):
        for k in order:
            a = res[k][i]
            outs.append(a[None] if k in big else a)
    return tuple(outs)
```

```python
import jax
import jax.numpy as jnp
import numpy as np
from jax import lax
from jax.experimental import pallas as pl
from jax.experimental.pallas import tpu as pltpu

F32 = jnp.float32
BF16 = jnp.bfloat16
SDS = jax.ShapeDtypeStruct

D_MODEL = 1024
ATT_WIDTH = 512
ATT_HEAD_DIM = 64
ATT_KV_HEADS = 2
ATT_GROUP = 4
WINDOW = 128
ROPE_DIM = 16
ROPE_THETA = 500000.0
HG_WIDTH = 512
HG_HEAD_DIM = 128
HG_HEADS = 4
HG_CHUNK = 32
IN_COLS = 2816
ATT_COLS = 768
D_FF = 4096
EPS = 1e-6
N_DEV = 8

ADAM_LR = 0.001
ADAM_B1 = 0.9
ADAM_B2 = 0.999
ADAM_EPS = 1e-08
ADAM_WD = 0.01
ADAM_STEP = 10

VMEM_LIMIT_BIG = 56 << 20
LANES = 128

MESH = pl.DeviceIdType.MESH
NT_DIMS = (((1,), (1,)), ((), ()))
TN_DIMS = (((0,), (0,)), ((), ()))


def _dot(a, b):
    return jnp.dot(a, b, preferred_element_type=F32)


def _dot_nt(a, b):
    return lax.dot_general(a, b, NT_DIMS, preferred_element_type=F32)


def _dot_tn(a, b):
    return lax.dot_general(a, b, TN_DIMS, preferred_element_type=F32)


def _bf(a):
    return a.astype(BF16)


def _sigmoid(a):
    return 0.5 * jnp.tanh(0.5 * a) + 0.5


def _mean_last(a):
    return jnp.mean(a, axis=-1, keepdims=True)


def _sum_rows(a):
    return jnp.sum(a, axis=0, keepdims=True)


def _loop_pairs(first, count, body, init, per_trip=2):
    if count % per_trip:
        return lax.fori_loop(first, first + count, body, init)

    def trip(i, c):
        for k in range(per_trip):
            c = body(first + per_trip * i + k, c)
        return c

    return lax.fori_loop(0, count // per_trip, trip, init)


def _params(sem=None, vmem=None):
    kw = {}
    if sem is not None:
        kw["dimension_semantics"] = sem
    if vmem is not None:
        kw["vmem_limit_bytes"] = vmem
    return pltpu.CompilerParams(**kw)


ANY_SPEC = pl.BlockSpec(memory_space=pl.ANY)


def _exchange_shapes(srcs, modes):
    out_shape = []
    for s, m in zip(srcs, modes):
        shp = (N_DEV,) + tuple(s.shape) if m == "gather" else tuple(s.shape)
        out_shape.append(SDS(shp, s.dtype))
    return out_shape


def _exchange_sems(n):
    if n == 0:
        return []
    return [pltpu.SemaphoreType.DMA((n, N_DEV - 1)), pltpu.SemaphoreType.DMA((n, N_DEV - 1)),
            pltpu.SemaphoreType.DMA((n,))]


SIBLING = 1
OTHER_CHIPS = (2, 4, 6)


def _related(k):
    x, y, c = lax.axis_index("x"), lax.axis_index("y"), lax.axis_index("c")
    px, py, pc = x ^ ((k >> 2) & 1), y ^ ((k >> 1) & 1), c ^ (k & 1)
    return (px, py, pc), 4 * px + 2 * py + pc


def _exchange_phases(modes, src_refs, out_refs, send_sems, recv_sems, own_sems):
    _, me = _related(0)
    sib_dev, sib = _related(SIBLING)
    start, middle, end = [], [], []

    def remote(a, i, src, dst, dev):
        return pltpu.make_async_remote_copy(src_ref=src, dst_ref=dst, send_sem=send_sems.at[a, i],
                                            recv_sem=recv_sems.at[a, i], device_id=dev, device_id_type=MESH)

    for a, mode in enumerate(modes):
        out = out_refs[a]
        if mode == "gather":
            src = src_refs[a]
            own = pltpu.make_async_copy(src, out.at[me], own_sems.at[a])
            to_sib = remote(a, 0, src, out.at[me], sib_dev)
            start += [own.start, to_sib.start]
            end += [remote(a, 0, src, out.at[sib], sib_dev).wait_recv, to_sib.wait_send, own.wait]
            for j, k in enumerate(OTHER_CHIPS, start=1):
                dev, peer = _related(k)
                _, peer_sib = _related(k ^ SIBLING)
                send = remote(a, j, src, out.at[me], dev)
                passed = remote(a, 3 + j, out.at[peer], out.at[peer], sib_dev)
                start.append(send.start)
                middle += [remote(a, j, src, out.at[peer], dev).wait_recv, passed.start]
                end += [remote(a, 3 + j, out.at[peer_sib], out.at[peer_sib], sib_dev).wait_recv,
                        send.wait_send, passed.wait_send]
        elif mode == "chips":
            chip = me // 2
            own = pltpu.make_async_copy(src_refs[a].at[chip], out.at[chip], own_sems.at[a])
            start.append(own.start)
            end.append(own.wait)
            for j, k in enumerate(OTHER_CHIPS, start=1):
                dev, peer = _related(k)
                send = remote(a, j, src_refs[a].at[peer // 2], out.at[chip], dev)
                start.append(send.start)
                end += [remote(a, j, src_refs[a].at[peer // 2], out.at[peer // 2], dev).wait_recv, send.wait_send]
        else:
            own = pltpu.make_async_copy(src_refs[a].at[me], out.at[me], own_sems.at[a])
            start.append(own.start)
            end.append(own.wait)
            for k in range(1, N_DEV):
                dev, peer = _related(k)
                send = remote(a, k - 1, src_refs[a].at[peer], out.at[me], dev)
                start.append(send.start)
                end += [remote(a, k - 1, src_refs[a].at[peer], out.at[peer], dev).wait_recv, send.wait_send]
    return start, middle, end


def _run(actions):
    for act in actions:
        act()


def _exchange(name, srcs, modes):
    n = len(srcs)

    def body(*refs):
        start, middle, end = _exchange_phases(modes, refs[:n], refs[n:2 * n], *refs[2 * n:])
        _run(start)
        _run(middle)
        _run(end)

    return pl.pallas_call(
        body, name=name, out_shape=_exchange_shapes(srcs, modes),
        in_specs=[ANY_SPEC] * n, out_specs=[ANY_SPEC] * n,
        scratch_shapes=_exchange_sems(n),
    )(*srcs)


def _ride_start(modes, step, steps, src_refs, out_refs, sems):
    if not modes:
        return
    middle_step = steps - 1

    @pl.when(step == 0)
    def _():
        _run(_exchange_phases(modes, src_refs, out_refs, *sems)[0])

    if "gather" in modes:
        @pl.when(step == middle_step)
        def _():
            _run(_exchange_phases(modes, src_refs, out_refs, *sems)[1])


def _ride_wait(modes, step, steps, src_refs, out_refs, sems):
    if not modes:
        return

    @pl.when(step == steps - 1)
    def _():
        _run(_exchange_phases(modes, src_refs, out_refs, *sems)[2])


def _ada_mod(c_all, w_ada, b_ada_mine):
    nb, cols = c_all.shape[0], w_ada.shape[1]

    def body(c_ref, w_ref, b_ref, o_ref):
        cv = c_ref[...]
        ca = cv * _sigmoid(cv)
        o_ref[...] = _dot(ca, w_ref[...]) + b_ref[...]

    return pl.pallas_call(body, name="ada_mod", out_shape=SDS((nb, cols), F32))(c_all, w_ada, b_ada_mine)


def _tile_rows(T, big=False):
    return min(512 if big else 256, T)


def _mod_spec(tps):
    return pl.BlockSpec((None, 8, D_MODEL), lambda i: (i // tps, 0, 0))


def _in_proj(x2, mod8, pre_w, w_in_bf, T, ride_srcs, ride_modes):
    N = x2.shape[0]
    TM = _tile_rows(T, big=True)
    tps = T // TM
    nr = len(ride_srcs)

    def body(*refs):
        x_ref, mod_ref, pw_ref, w_ref = refs[:4]
        ride_in = refs[4:4 + nr]
        pa_ref, ph_ref, h1_ref = refs[4 + nr:7 + nr]
        ride_out = refs[7 + nr:7 + 2 * nr]
        sems = refs[7 + 2 * nr:]
        _ride_start(ride_modes, pl.program_id(0), N // TM, ride_in, ride_out, sems)
        x = x_ref[...]
        r = lax.rsqrt(_mean_last(x * x) + EPS)
        h = (x * r * pw_ref[...]) * (1.0 + mod_ref[1:2, :]) + mod_ref[0:1, :]
        hb = _bf(h)
        h1_ref[...] = hb
        pa_ref[...] = _dot_nt(hb, w_ref[:ATT_COLS, :])
        ph_ref[...] = _dot_nt(hb, w_ref[ATT_COLS:, :])
        _ride_wait(ride_modes, pl.program_id(0), N // TM, ride_in, ride_out, sems)

    return pl.pallas_call(
        body, name="in_proj", grid=(N // TM,),
        in_specs=[pl.BlockSpec((TM, D_MODEL), lambda i: (i, 0)), _mod_spec(tps),
                  pl.BlockSpec((1, D_MODEL), lambda i: (0, 0)),
                  pl.BlockSpec((IN_COLS, D_MODEL), lambda i: (0, 0))] + [ANY_SPEC] * nr,
        out_specs=[pl.BlockSpec((TM, ATT_COLS), lambda i: (i, 0)),
                   pl.BlockSpec((TM, IN_COLS - ATT_COLS), lambda i: (i, 0)),
                   pl.BlockSpec((TM, D_MODEL), lambda i: (i, 0))] + [ANY_SPEC] * nr,
        out_shape=[SDS((N, ATT_COLS), F32), SDS((N, IN_COLS - ATT_COLS), F32), SDS((N, D_MODEL), BF16)]
        + _exchange_shapes(ride_srcs, ride_modes),
        scratch_shapes=_exchange_sems(nr),
        compiler_params=_params(("arbitrary",), VMEM_LIMIT_BIG),
    )(x2, mod8, pre_w, w_in_bf, *ride_srcs)


def _rope_tables(T):
    half = ROPE_DIM // 2
    f32 = np.float32
    inv_freq = (f32(ROPE_THETA) ** (-np.arange(0, ROPE_DIM, 2, dtype=f32) / f32(ROPE_DIM))).astype(f32)
    ang = np.arange(T, dtype=f32)[:, None] * inv_freq[None, :]
    cos, sin = np.cos(ang).astype(f32), np.sin(ang).astype(f32)
    ones = np.ones((T, ATT_HEAD_DIM - ROPE_DIM), f32)
    zeros = np.zeros((T, ATT_HEAD_DIM - ROPE_DIM), f32)
    zh = np.zeros((T, half), f32)
    cos64 = np.concatenate([cos, cos, ones], axis=1)
    sin_left = np.concatenate([-sin, zh, zeros], axis=1)
    sin_right = np.concatenate([zh, sin, zeros], axis=1)
    rep = LANES // ATT_HEAD_DIM
    return tuple(jnp.asarray(np.tile(t, (1, rep))) for t in (cos64, sin_left, sin_right))


def _rope(xc, cs, sl, sr):
    return xc * cs + pltpu.roll(xc, LANES - 8, 1) * sl + pltpu.roll(xc, 8, 1) * sr


def _rope_t(dy, cs, sl, sr):
    return dy * cs + pltpu.roll(dy * sl, 8, 1) + pltpu.roll(dy * sr, LANES - 8, 1)


ATT_SCALE = ATT_HEAD_DIM ** -0.5
ATT_SPLITS = 4


def _lower_mask():
    j = lax.broadcasted_iota(jnp.int32, (WINDOW, ATT_GROUP * WINDOW), 0)
    i = lax.broadcasted_iota(jnp.int32, (WINDOW, ATT_GROUP * WINDOW), 1) & (WINDOW - 1)
    return j <= i


def _sink_row(sink_ref, hk):
    return jnp.concatenate(
        [jnp.full((1, WINDOW), sink_ref[0, ATT_GROUP * hk + g], F32) for g in range(ATT_GROUP)], axis=1)


def _softmax_window(qs, k_cur, k_prev, lower, has_prev, sink):
    s_prev = jnp.where(has_prev, _dot_nt(k_prev, qs), jnp.finfo(F32).min)
    s = jnp.where(lower, _dot_nt(k_cur, qs), s_prev)
    m = jnp.maximum(jnp.max(s, axis=0, keepdims=True), sink)
    p = jnp.exp(s - m)
    es = jnp.exp(sink - m)
    inv = 1.0 / (jnp.sum(p, axis=0, keepdims=True) + es)
    return p, inv, es


def _stack_heads(parts, hk):
    hs = []
    for g in range(ATT_GROUP):
        h = ATT_GROUP * hk + g
        hs.append(parts[h // 2][:, (h % 2) * ATT_HEAD_DIM:(h % 2 + 1) * ATT_HEAD_DIM])
    return jnp.concatenate(hs, axis=0)


def _attn_fwd(proj3, tables, sinks, attn_w, ride_srcs, ride_modes):
    B, T, _ = proj3.shape
    nb = T // WINDOW
    splits = min(ATT_SPLITS, nb)
    per = nb // splits
    nr = len(ride_srcs)
    cos, sinl, sinr = tables

    def body(*refs):
        q_ref, k_ref, v_ref, cos_ref, sl_ref, sr_ref, sink_ref, aw_ref = refs[:8]
        ride_in = refs[8:8 + nr]
        o_ref, an_ref, qr_ref, kr_ref = refs[8 + nr:12 + nr]
        ride_out = refs[12 + nr:12 + 2 * nr]
        kpad, vpad = refs[12 + 2 * nr:14 + 2 * nr]
        sems = refs[14 + 2 * nr:]
        part = pl.program_id(1)
        step = pl.program_id(0) * splits + part
        _ride_start(ride_modes, step, B * splits, ride_in, ride_out, sems)

        @pl.when(part == 0)
        def _():
            kpad[0:WINDOW, :] = jnp.zeros((WINDOW, LANES), BF16)
            vpad[0:WINDOW, :] = jnp.zeros((WINDOW, LANES), BF16)

        lower = _lower_mask()

        def block(n, carry):
            r0 = pl.multiple_of(n * WINDOW, WINDOW)
            rows = pl.ds(r0, WINDOW)
            nxt = pl.ds(r0 + WINDOW, WINDOW)
            cs, sl, sr = cos_ref[rows, :], sl_ref[rows, :], sr_ref[rows, :]
            kb = _bf(_rope(k_ref[rows, :], cs, sl, sr))
            vb = _bf(v_ref[rows, :])
            kpad[nxt, :] = kb
            kr_ref[rows, :] = kb
            vpad[nxt, :] = vb
            qparts = []
            for j in range(ATT_WIDTH // LANES):
                qp = _bf(_rope(q_ref[rows, j * LANES:(j + 1) * LANES], cs, sl, sr) * ATT_SCALE)
                qr_ref[rows, j * LANES:(j + 1) * LANES] = qp
                qparts.append(qp)
            for hk in range(ATT_KV_HEADS):
                lanes = slice(hk * ATT_HEAD_DIM, (hk + 1) * ATT_HEAD_DIM)
                qs = _stack_heads(qparts, hk)
                p, inv, _ = _softmax_window(qs, kb[:, lanes], kpad[rows, lanes], lower, n > 0,
                                            _sink_row(sink_ref, hk))
                p_cur = jnp.where(lower, p, 0.0)
                ot = (_dot_tn(vb[:, lanes], _bf(p_cur)) + _dot_tn(vpad[rows, lanes], _bf(p - p_cur))) * inv
                for g in range(ATT_GROUP):
                    h = ATT_GROUP * hk + g
                    o_ref[rows, h * ATT_HEAD_DIM:(h + 1) * ATT_HEAD_DIM] = ot[:, g * WINDOW:(g + 1) * WINDOW].T
            ob = o_ref[rows, :]
            an_ref[rows, :] = _bf(ob * lax.rsqrt(_mean_last(ob * ob) + EPS) * aw_ref[...])
            return carry

        _loop_pairs(part * per, per, block, 0)
        _ride_wait(ride_modes, step, B * splits, ride_in, ride_out, sems)

    seq = lambda w, j: pl.BlockSpec((None, T, w), lambda b, s: (b, 0, j))
    full = lambda r, w: pl.BlockSpec((r, w), lambda b, s: (0, 0))
    return pl.pallas_call(
        body, name="attn_fwd", grid=(B, splits),
        in_specs=[seq(ATT_WIDTH, 0), seq(LANES, 4), seq(LANES, 5),
                  full(T, LANES), full(T, LANES), full(T, LANES),
                  pl.BlockSpec(memory_space=pltpu.SMEM), full(1, ATT_WIDTH)] + [ANY_SPEC] * nr,
        out_specs=[seq(ATT_WIDTH, 0), seq(ATT_WIDTH, 0), seq(ATT_WIDTH, 0), seq(LANES, 0)] + [ANY_SPEC] * nr,
        out_shape=[SDS((B, T, ATT_WIDTH), F32), SDS((B, T, ATT_WIDTH), BF16),
                   SDS((B, T, ATT_WIDTH), BF16), SDS((B, T, LANES), BF16)] + _exchange_shapes(ride_srcs, ride_modes),
        scratch_shapes=[pltpu.VMEM((T + WINDOW, LANES), BF16), pltpu.VMEM((T + WINDOW, LANES), BF16)]
        + _exchange_sems(nr),
        compiler_params=_params(("arbitrary", "arbitrary"), VMEM_LIMIT_BIG),
    )(proj3, proj3, proj3, cos, sinl, sinr, sinks, attn_w, *ride_srcs)


HG_GROUP = 8
HG_ROWS = HG_GROUP * HG_CHUNK


HG_STACK = HG_GROUP * HG_HEAD_DIM


def _group_mask():
    r = lax.broadcasted_iota(jnp.int32, (HG_ROWS, HG_ROWS), 0)
    c = lax.broadcasted_iota(jnp.int32, (HG_ROWS, HG_ROWS), 1)
    return ((r // HG_CHUNK) == (c // HG_CHUNK)) & (r >= c)


def _spread(a):
    blocks = []
    for c in range(HG_GROUP):
        above = jnp.zeros((c * HG_CHUNK, HG_HEAD_DIM), a.dtype)
        below = jnp.zeros(((HG_GROUP - 1 - c) * HG_CHUNK, HG_HEAD_DIM), a.dtype)
        blocks.append(jnp.concatenate([p for p in (above, a[_chunk_rows(c), :], below) if p.shape[0]], axis=0))
    return jnp.concatenate(blocks, axis=1)


def _pick(r):
    return jnp.concatenate([r[_chunk_rows(c), c * HG_HEAD_DIM:(c + 1) * HG_HEAD_DIM] for c in range(HG_GROUP)], axis=0)


def _lane_block(a, c):
    return a[:, c * HG_HEAD_DIM:(c + 1) * HG_HEAD_DIM]


def _chunk_cumsum(a, reverse=False):
    n = a.shape[0]
    pos = lax.broadcasted_iota(jnp.int32, a.shape, 0) % HG_CHUNK
    shift = 1
    while shift < HG_CHUNK:
        if reverse:
            a = a + jnp.where(pos < HG_CHUNK - shift, pltpu.roll(a, n - shift, 0), 0.0)
        else:
            a = a + jnp.where(pos >= shift, pltpu.roll(a, shift, 0), 0.0)
        shift *= 2
    return a


def _chunk_bcast(rows_1x128):
    return jnp.concatenate([jnp.broadcast_to(r, (HG_CHUNK, HG_HEAD_DIM)) for r in rows_1x128], axis=0)


def _hgrn_gates(hq, hf, lb):
    sq = _sigmoid(hq)
    q = hq * sq
    sg = _sigmoid(hf)
    f = lb + (1.0 - lb) * sg
    k = 1.0 - f
    logf = jnp.log(f)
    b = _chunk_cumsum(logf)
    bl = [_sum_rows(logf[_chunk_rows(c), :]) for c in range(HG_GROUP)]
    eb, enb, e2 = jnp.exp(b), jnp.exp(-b), jnp.exp(_chunk_bcast(bl) - b)
    ebl = [jnp.exp(r) for r in bl]
    return dict(sq=sq, sg=sg, f=f, eb=eb, enb=enb, e2=e2, ebl=ebl, qd=q * eb, kd=k * enb, k2=k * e2)


def _chunk_rows(c):
    return slice(c * HG_CHUNK, (c + 1) * HG_CHUNK)


def _head_lanes(h):
    return slice(h * HG_HEAD_DIM, (h + 1) * HG_HEAD_DIM)


def _hgrn_fwd(proj_h, lb, hg_w, ride_srcs, ride_modes):
    B, T, _ = proj_h.shape
    ng = T // HG_ROWS
    nr = len(ride_srcs)

    def body(*refs):
        hq_ref, hf_ref, hi_ref, hg_ref, lb_ref, gw_ref = refs[:6]
        ride_in = refs[6:6 + nr]
        o_ref, rg_ref, sp_ref = refs[6 + nr:9 + nr]
        ride_out = refs[9 + nr:9 + 2 * nr]
        st = refs[9 + 2 * nr]
        sems = refs[10 + 2 * nr:]
        gi = pl.program_id(1)
        step = pl.program_id(0) * ng + gi
        _ride_start(ride_modes, step, B * ng, ride_in, ride_out, sems)

        @pl.when(gi == 0)
        def _():
            st[...] = jnp.zeros(st.shape, F32)

        lo = _group_mask()
        for h in range(HG_HEADS):
            lanes = _head_lanes(h)
            gt = _hgrn_gates(hq_ref[:, lanes], hf_ref[:, lanes], lb_ref[:, lanes])
            v, qd, kd = _bf(hi_ref[:, lanes]), _bf(gt["qd"]), _bf(gt["kd"])
            a = jnp.where(lo, _dot_nt(qd, kd), 0.0)
            kv = _dot_tn(v, _spread(_bf(gt["k2"])))
            s = st[h]
            before = []
            for c in range(HG_GROUP):
                before.append(s)
                s = s * gt["ebl"][c] + _lane_block(kv, c)
            st[h] = s
            sp = jnp.concatenate(before, axis=1)
            sp_ref[h] = sp
            o = _dot(_bf(a), v) + _dot_nt(_spread(qd), _bf(sp))
            o_ref[:, lanes] = o
            hg = hg_ref[:, lanes]
            rn = o * lax.rsqrt(_mean_last(o * o) + EPS) * gw_ref[...]
            rg_ref[:, lanes] = _bf(rn * (hg * _sigmoid(hg)))
        _ride_wait(ride_modes, step, B * ng, ride_in, ride_out, sems)

    part = lambda j: pl.BlockSpec((None, HG_ROWS, HG_WIDTH), lambda b, g: (b, g, j))
    return pl.pallas_call(
        body, name="hgrn_fwd", grid=(B, ng),
        in_specs=[part(0), part(1), part(2), part(3),
                  pl.BlockSpec((1, HG_WIDTH), lambda b, g: (0, 0)),
                  pl.BlockSpec((1, LANES), lambda b, g: (0, 0))] + [ANY_SPEC] * nr,
        out_specs=[part(0), part(0),
                   pl.BlockSpec((None, HG_HEADS, None, HG_HEAD_DIM, HG_STACK), lambda b, g: (b, 0, g, 0, 0))]
        + [ANY_SPEC] * nr,
        out_shape=[SDS((B, T, HG_WIDTH), F32), SDS((B, T, HG_WIDTH), BF16),
                   SDS((B, HG_HEADS, ng, HG_HEAD_DIM, HG_STACK), F32)] + _exchange_shapes(ride_srcs, ride_modes),
        scratch_shapes=[pltpu.VMEM((HG_HEADS, HG_HEAD_DIM, HG_HEAD_DIM), F32)] + _exchange_sems(nr),
        compiler_params=_params(("arbitrary", "arbitrary"), VMEM_LIMIT_BIG),
    )(proj_h, proj_h, proj_h, proj_h, lb, hg_w, *ride_srcs)


def _mix_out(x2, attn_n, rec_g, mod8, post_w, w_out_bf, T, ride_srcs, ride_modes):
    N = x2.shape[0]
    TM = _tile_rows(T, big=True)
    tps = T // TM
    nr = len(ride_srcs)

    def body(*refs):
        x_ref, an_ref, rg_ref, mod_ref, pw_ref, w_ref = refs[:6]
        ride_in = refs[6:6 + nr]
        mix_ref, x1_ref, cat_ref = refs[6 + nr:9 + nr]
        ride_out = refs[9 + nr:9 + 2 * nr]
        sems = refs[9 + 2 * nr:]
        _ride_start(ride_modes, pl.program_id(0), N // TM, ride_in, ride_out, sems)
        cat = jnp.concatenate([an_ref[...], rg_ref[...]], axis=1)
        cat_ref[...] = cat
        mix = _dot(cat, w_ref[...])
        mix_ref[...] = mix
        r = lax.rsqrt(_mean_last(mix * mix) + EPS)
        x1_ref[...] = x_ref[...] + mod_ref[2:3, :] * (mix * r * pw_ref[...])
        _ride_wait(ride_modes, pl.program_id(0), N // TM, ride_in, ride_out, sems)

    row = lambda w: pl.BlockSpec((TM, w), lambda i: (i, 0))
    return pl.pallas_call(
        body, name="mix_out", grid=(N // TM,),
        in_specs=[row(D_MODEL), row(ATT_WIDTH), row(HG_WIDTH), _mod_spec(tps),
                  pl.BlockSpec((1, D_MODEL), lambda i: (0, 0)),
                  pl.BlockSpec((D_MODEL, D_MODEL), lambda i: (0, 0))] + [ANY_SPEC] * nr,
        out_specs=[row(D_MODEL), row(D_MODEL), row(D_MODEL)] + [ANY_SPEC] * nr,
        out_shape=[SDS((N, D_MODEL), F32), SDS((N, D_MODEL), F32), SDS((N, D_MODEL), BF16)]
        + _exchange_shapes(ride_srcs, ride_modes),
        scratch_shapes=_exchange_sems(nr),
        compiler_params=_params(("arbitrary",), VMEM_LIMIT_BIG),
    )(x2, attn_n, rec_g, mod8, post_w, w_out_bf, *ride_srcs)


def _load_weights_once(pairs, sem):
    @pl.when(pl.program_id(0) == 0)
    def _():
        cps = [pltpu.make_async_copy(src, dst, sem.at[i]) for i, (src, dst) in enumerate(pairs)]
        for cp in cps:
            cp.start()
        for cp in cps:
            cp.wait()


MLP_HALF = D_MODEL // 2
MLP_PIECES = 2 * N_DEV + 2


def _mlp_weight_pieces(wu_a, wu_b, wd_a, wd_b, wu, wd):
    cols = D_FF // N_DEV
    pairs = []
    for h, half in enumerate((wu_a, wu_b)):
        for j in range(N_DEV):
            pairs.append((half.at[j], wu.at[pl.ds(h * MLP_HALF, MLP_HALF), pl.ds(j * cols, cols)]))
    for h, half in enumerate((wd_a, wd_b)):
        pairs.append((half, wd.at[:, pl.ds(h * MLP_HALF, MLP_HALF)]))
    return pairs


def _mlp_fwd(x1, mod8, pre_w, w_up_halves, w_down_halves, T):
    N = x1.shape[0]
    TM = _tile_rows(T)
    tps = T // TM

    def body(x_ref, mod_ref, pw_ref, wua, wub, wda, wdb, up_ref, u_ref, d_ref, h2_ref, wu, wd, sem):
        _load_weights_once(_mlp_weight_pieces(wua, wub, wda, wdb, wu, wd), sem)
        x = x_ref[...]
        r = lax.rsqrt(_mean_last(x * x) + EPS)
        h = (x * r * pw_ref[...]) * (1.0 + mod_ref[4:5, :]) + mod_ref[3:4, :]
        hb = _bf(h)
        h2_ref[...] = hb
        up = _dot(hb, wu[...])
        up_ref[...] = up
        ru = jnp.maximum(up, 0.0)
        u = _bf(ru * ru)
        u_ref[...] = u
        d_ref[...] = _dot(u, wd[...])

    row = lambda w: pl.BlockSpec((TM, w), lambda i: (i, 0))
    return pl.pallas_call(
        body, name="mlp_fwd", grid=(N // TM,),
        in_specs=[row(D_MODEL), _mod_spec(tps), pl.BlockSpec((1, D_MODEL), lambda i: (0, 0))] + [ANY_SPEC] * 4,
        out_specs=[row(D_FF), row(D_FF), row(D_MODEL), row(D_MODEL)],
        out_shape=[SDS((N, D_FF), F32), SDS((N, D_FF), BF16), SDS((N, D_MODEL), F32), SDS((N, D_MODEL), BF16)],
        scratch_shapes=[pltpu.VMEM((D_MODEL, D_FF), BF16), pltpu.VMEM((D_FF, D_MODEL), BF16),
                        pltpu.SemaphoreType.DMA((MLP_PIECES,))],
        compiler_params=_params(("arbitrary",), VMEM_LIMIT_BIG),
    )(x1, mod8, pre_w, *w_up_halves, *w_down_halves)


def _acc_rows(acc_ref, first, rows):
    @pl.when(first)
    def _():
        acc_ref[...] = jnp.zeros(acc_ref.shape, F32)
    for i, r in enumerate(rows):
        acc_ref[i:i + 1, :] += r


def _mlp_bwd(x1, d, up, tgt, mod8, pre_w, post_w, w_up_halves, w_down_halves, T):
    N = x1.shape[0]
    TM = _tile_rows(T)
    tps = T // TM

    def body(x_ref, d_ref, up_ref, t_ref, mod_ref, pw_ref, qw_ref, wua, wub, wda, wdb,
             dx_ref, dup_ref, dd_ref, acc_ref, wd, wu, sem):
        _load_weights_once(_mlp_weight_pieces(wua, wub, wda, wdb, wu, wd), sem)
        sh2, sc2, g2 = mod_ref[3:4, :], mod_ref[4:5, :], mod_ref[5:6, :]
        x = x_ref[...]
        r1 = lax.rsqrt(_mean_last(x * x) + EPS)
        xh = x * r1
        n2 = xh * pw_ref[...]
        dv = d_ref[...]
        rd = lax.rsqrt(_mean_last(dv * dv) + EPS)
        dh = dv * rd
        rr = dh * qw_ref[...]
        e = x + g2 * rr - t_ref[...]
        loss = 0.5 * jnp.sum(_sum_rows(e * e), axis=1, keepdims=True) / D_MODEL
        dy = e * (1.0 / D_MODEL)
        dg2 = _sum_rows(dy * rr)
        drr = dy * g2
        dw_post = _sum_rows(drr * dh)
        ddh = drr * qw_ref[...]
        dd = _bf(rd * (ddh - dh * _mean_last(ddh * dh)))
        dd_ref[...] = dd
        ru = jnp.maximum(up_ref[...], 0.0)
        dup = _bf(_dot_nt(dd, wd[...]) * (2.0 * ru))
        dup_ref[...] = dup
        dh2 = _dot_nt(dup, wu[...])
        dsh2 = _sum_rows(dh2)
        dsc2 = _sum_rows(dh2 * n2)
        dn2 = dh2 * (1.0 + sc2)
        dw_pre = _sum_rows(dn2 * xh)
        dxh = dn2 * pw_ref[...]
        dx_ref[...] = dy + r1 * (dxh - xh * _mean_last(dxh * xh))
        _acc_rows(acc_ref, pl.program_id(0) % tps == 0,
                  [dsh2, dsc2, dg2, dw_pre, dw_post, jnp.broadcast_to(loss, (1, D_MODEL))])

    row = lambda w: pl.BlockSpec((TM, w), lambda i: (i, 0))
    vec = pl.BlockSpec((1, D_MODEL), lambda i: (0, 0))
    B = N // T
    return pl.pallas_call(
        body, name="mlp_bwd", grid=(N // TM,),
        in_specs=[row(D_MODEL), row(D_MODEL), row(D_FF), row(D_MODEL), _mod_spec(tps), vec, vec] + [ANY_SPEC] * 4,
        out_specs=[row(D_MODEL), row(D_FF), row(D_MODEL), _mod_spec(tps)],
        out_shape=[SDS((N, D_MODEL), F32), SDS((N, D_FF), BF16), SDS((N, D_MODEL), BF16),
                   SDS((B, 8, D_MODEL), F32)],
        scratch_shapes=[pltpu.VMEM((D_FF, D_MODEL), BF16), pltpu.VMEM((D_MODEL, D_FF), BF16),
                        pltpu.SemaphoreType.DMA((MLP_PIECES,))],
        compiler_params=_params(("arbitrary",), VMEM_LIMIT_BIG),
    )(x1, d, up, tgt, mod8, pre_w, post_w, *w_up_halves, *w_down_halves)


def _mix_bwd(mix, dx1, mod8, post_w, w_out_bf, T, ride_srcs, ride_modes):
    N = mix.shape[0]
    TM = _tile_rows(T, big=True)
    tps = T // TM
    nr = len(ride_srcs)

    def body(*refs):
        mix_ref, dx_ref, mod_ref, pw_ref, w_ref = refs[:5]
        ride_in = refs[5:5 + nr]
        dan_ref, drg_ref, dmix_ref, acc_ref = refs[5 + nr:9 + nr]
        ride_out = refs[9 + nr:9 + 2 * nr]
        sems = refs[9 + 2 * nr:]
        _ride_start(ride_modes, pl.program_id(0), N // TM, ride_in, ride_out, sems)
        g1 = mod_ref[2:3, :]
        mix = mix_ref[...]
        dx1 = dx_ref[...]
        rm = lax.rsqrt(_mean_last(mix * mix) + EPS)
        mh = mix * rm
        dg1 = _sum_rows(dx1 * (mh * pw_ref[...]))
        dr = dx1 * g1
        dw_post = _sum_rows(dr * mh)
        dmh = dr * pw_ref[...]
        dmix = _bf(rm * (dmh - mh * _mean_last(dmh * mh)))
        dmix_ref[...] = dmix
        dcat = _dot_nt(dmix, w_ref[...])
        dan_ref[...] = dcat[:, :ATT_WIDTH]
        drg_ref[...] = dcat[:, ATT_WIDTH:]
        _acc_rows(acc_ref, pl.program_id(0) % tps == 0, [dg1, dw_post])
        _ride_wait(ride_modes, pl.program_id(0), N // TM, ride_in, ride_out, sems)

    row = lambda w: pl.BlockSpec((TM, w), lambda i: (i, 0))
    B = N // T
    return pl.pallas_call(
        body, name="mix_bwd", grid=(N // TM,),
        in_specs=[row(D_MODEL), row(D_MODEL), _mod_spec(tps), pl.BlockSpec((1, D_MODEL), lambda i: (0, 0)),
                  pl.BlockSpec((D_MODEL, D_MODEL), lambda i: (0, 0))] + [ANY_SPEC] * nr,
        out_specs=[row(ATT_WIDTH), row(HG_WIDTH), row(D_MODEL), _mod_spec(tps)] + [ANY_SPEC] * nr,
        out_shape=[SDS((N, ATT_WIDTH), F32), SDS((N, HG_WIDTH), F32), SDS((N, D_MODEL), BF16),
                   SDS((B, 8, D_MODEL), F32)] + _exchange_shapes(ride_srcs, ride_modes),
        scratch_shapes=_exchange_sems(nr),
        compiler_params=_params(("arbitrary",), VMEM_LIMIT_BIG),
    )(mix, dx1, mod8, post_w, w_out_bf, *ride_srcs)


def _hgrn_bwd(proj_h, lb, hg_w, o, s_prev, drg, ride_srcs, ride_modes):
    B, T, _ = proj_h.shape
    ng = T // HG_ROWS
    nr = len(ride_srcs)

    def body(*refs):
        hq_ref, hf_ref, hi_ref, hg_ref, lb_ref, gw_ref, o_ref, sp_ref, drg_ref = refs[:9]
        ride_in = refs[9:9 + nr]
        dhq_ref, dhf_ref, dhi_ref, dhg_ref, dlb_ref, dgw_ref = refs[9 + nr:15 + nr]
        ride_out = refs[15 + nr:15 + 2 * nr]
        dst = refs[15 + 2 * nr]
        sems = refs[16 + 2 * nr:]
        step = pl.program_id(0) * ng + pl.program_id(1)
        _ride_start(ride_modes, step, B * ng, ride_in, ride_out, sems)

        @pl.when(pl.program_id(1) == 0)
        def _():
            dst[...] = jnp.zeros(dst.shape, F32)
            dlb_ref[...] = jnp.zeros(dlb_ref.shape, F32)
            dgw_ref[...] = jnp.zeros(dgw_ref.shape, F32)

        lo = _group_mask()
        gw = gw_ref[...]

        for h in range(HG_HEADS):
            lanes = _head_lanes(h)
            lbv = lb_ref[:, lanes]
            hq = hq_ref[:, lanes]
            gt = _hgrn_gates(hq, hf_ref[:, lanes], lbv)
            sq, sg, qdf, kdf, k2f, ebl = gt["sq"], gt["sg"], gt["qd"], gt["kd"], gt["k2"], gt["ebl"]
            v, qd, kd = _bf(hi_ref[:, lanes]), _bf(qdf), _bf(kdf)
            ov = o_ref[:, lanes]
            hg = hg_ref[:, lanes]
            shg = _sigmoid(hg)
            dr = drg_ref[:, lanes]
            ro = lax.rsqrt(_mean_last(ov * ov) + EPS)
            oh = ov * ro
            dhg_ref[:, lanes] = _bf(dr * (oh * gw) * (shg + hg * shg * (1.0 - shg)))
            drn = dr * (hg * shg)
            dgw_ref[...] += jnp.broadcast_to(_sum_rows(drn * oh), (8, LANES))
            doh = drn * gw
            do = _bf(ro * (doh - oh * _mean_last(doh * oh)))
            a = jnp.where(lo, _dot_nt(qd, kd), 0.0)
            da = _bf(jnp.where(lo, _dot_nt(do, v), 0.0))
            dv = _dot_tn(_bf(a), do)
            dqd = _dot(da, kd)
            dkd = _dot_tn(da, qd)
            sp = sp_ref[h]
            incr = _dot_tn(do, _spread(qd))
            ds = dst[h]
            after = [None] * HG_GROUP
            for c in reversed(range(HG_GROUP)):
                after[c] = ds
                ds = ds * ebl[c] + _lane_block(incr, c)
            dst[h] = ds
            dss = jnp.concatenate(after, axis=1)
            dssb = _bf(dss)
            dk2 = _pick(_dot(v, dssb))
            dhi_ref[:, lanes] = _bf(dv + _dot_nt(_spread(_bf(k2f)), dssb))
            dqd = dqd + _pick(_dot(do, _bf(sp)))
            debl = _sum_rows(dss * sp)
            k2g = dk2 * k2f
            db = dqd * qdf - dkd * kdf - k2g
            dk = dkd * gt["enb"] + dk2 * gt["e2"]
            dbl = _chunk_bcast([_lane_block(debl, c) * ebl[c] + _sum_rows(k2g[_chunk_rows(c), :])
                                for c in range(HG_GROUP)])
            dg = _chunk_cumsum(db, reverse=True) + dbl
            df = dg / gt["f"] - dk
            dhf_ref[:, lanes] = _bf(df * (1.0 - lbv) * sg * (1.0 - sg))
            dlb_ref[:, lanes] += jnp.broadcast_to(_sum_rows(df * (1.0 - sg)), (8, LANES))
            dhq_ref[:, lanes] = _bf((dqd * gt["eb"]) * (sq + hq * sq * (1.0 - sq)))
        _ride_wait(ride_modes, step, B * ng, ride_in, ride_out, sems)

    part = lambda j: pl.BlockSpec((None, HG_ROWS, HG_WIDTH), lambda b, g: (b, ng - 1 - g, j))
    return pl.pallas_call(
        body, name="hgrn_bwd", grid=(B, ng),
        in_specs=[part(0), part(1), part(2), part(3),
                  pl.BlockSpec((1, HG_WIDTH), lambda b, g: (0, 0)),
                  pl.BlockSpec((1, LANES), lambda b, g: (0, 0)),
                  part(0),
                  pl.BlockSpec((None, HG_HEADS, None, HG_HEAD_DIM, HG_STACK), lambda b, g: (b, 0, ng - 1 - g, 0, 0)),
                  part(0)] + [ANY_SPEC] * nr,
        out_specs=[part(0), part(0), part(0), part(0),
                   pl.BlockSpec((None, 8, HG_WIDTH), lambda b, g: (b, 0, 0)),
                   pl.BlockSpec((None, 8, LANES), lambda b, g: (b, 0, 0))] + [ANY_SPEC] * nr,
        out_shape=[SDS((B, T, HG_WIDTH), BF16)] * 4 + [SDS((B, 8, HG_WIDTH), F32), SDS((B, 8, LANES), F32)]
        + _exchange_shapes(ride_srcs, ride_modes),
        scratch_shapes=[pltpu.VMEM((HG_HEADS, HG_HEAD_DIM, HG_HEAD_DIM), F32)] + _exchange_sems(nr),
        compiler_params=_params(("arbitrary", "arbitrary"), VMEM_LIMIT_BIG),
    )(proj_h, proj_h, proj_h, proj_h, lb, hg_w, o, s_prev, drg, *ride_srcs)


def _attn_bwd(qr, kr, proj3, attn_o, dan, tables, sinks, attn_w, ride_srcs, ride_modes):
    B, T, _ = proj3.shape
    nb = T // WINDOW
    splits = min(ATT_SPLITS, nb)
    per = nb // splits
    nr = len(ride_srcs)
    cos, sinl, sinr = tables
    QKV = ATT_WIDTH + 2 * LANES

    def body(*refs):
        qr_ref, kr_ref, v_ref, o_ref, dan_ref, cos_ref, sl_ref, sr_ref, sink_ref, aw_ref = refs[:10]
        ride_in = refs[10:10 + nr]
        dqkv_ref, dsink_ref, daw_ref = refs[10 + nr:13 + nr]
        ride_out = refs[13 + nr:13 + 2 * nr]
        kpad, vpad, dkpad, dvpad, dqb, dsk = refs[13 + 2 * nr:19 + 2 * nr]
        sems = refs[19 + 2 * nr:]
        part = pl.program_id(1)
        step = pl.program_id(0) * splits + part
        _ride_start(ride_modes, step, B * splits, ride_in, ride_out, sems)

        @pl.when(part == 0)
        def _():
            kpad[0:WINDOW, :] = jnp.zeros((WINDOW, LANES), BF16)
            vpad[0:WINDOW, :] = jnp.zeros((WINDOW, LANES), BF16)
            kpad[WINDOW:, :] = kr_ref[...]
            vpad[WINDOW:, :] = _bf(v_ref[...])
            dkpad[...] = jnp.zeros(dkpad.shape, F32)
            dvpad[...] = jnp.zeros(dvpad.shape, F32)
            dsk[...] = jnp.zeros(dsk.shape, F32)
            daw_ref[...] = jnp.zeros(daw_ref.shape, F32)

        lower = _lower_mask()
        aw = aw_ref[...]

        def block(n, daw):
            r0 = pl.multiple_of(n * WINDOW, WINDOW)
            rows = pl.ds(r0, WINDOW)
            nxt = pl.ds(r0 + WINDOW, WINDOW)
            ob = o_ref[rows, :]
            dn = dan_ref[rows, :]
            ro = lax.rsqrt(_mean_last(ob * ob) + EPS)
            oh = ob * ro
            daw = daw + _sum_rows(dn * oh)
            doh = dn * aw
            do = _bf(ro * (doh - oh * _mean_last(doh * oh)))
            doparts = [do[:, j * LANES:(j + 1) * LANES] for j in range(ATT_WIDTH // LANES)]
            qparts = [qr_ref[rows, j * LANES:(j + 1) * LANES] for j in range(ATT_WIDTH // LANES)]
            for hk in range(ATT_KV_HEADS):
                lanes = slice(hk * ATT_HEAD_DIM, (hk + 1) * ATT_HEAD_DIM)
                qs = _stack_heads(qparts, hk)
                dos = _stack_heads(doparts, hk)
                k_cur, k_prev = kpad[nxt, lanes], kpad[rows, lanes]
                v_cur, v_prev = vpad[nxt, lanes], vpad[rows, lanes]
                p, inv, es = _softmax_window(qs, k_cur, k_prev, lower, n > 0, _sink_row(sink_ref, hk))
                p = p * inv
                dp = jnp.where(lower, _dot_nt(v_cur, dos), _dot_nt(v_prev, dos))
                delta = jnp.sum(p * dp, axis=0, keepdims=True)
                ds = p * (dp - delta)
                sk = (es * inv) * delta
                ds_cur = jnp.where(lower, ds, 0.0)
                p_cur = jnp.where(lower, p, 0.0)
                ds_cur, ds_prev = _bf(ds_cur), _bf(ds - ds_cur)
                p_cur, p_prev = _bf(p_cur), _bf(p - p_cur)
                dqt = (_dot_tn(k_cur, ds_cur) + _dot_tn(k_prev, ds_prev)) * ATT_SCALE
                dkpad[nxt, lanes] += _dot(ds_cur, qs)
                dkpad[rows, lanes] += _dot(ds_prev, qs)
                dvpad[nxt, lanes] += _dot(p_cur, dos)
                dvpad[rows, lanes] += _dot(p_prev, dos)
                for g in range(ATT_GROUP):
                    h = ATT_GROUP * hk + g
                    cols = slice(g * WINDOW, (g + 1) * WINDOW)
                    dqb[:, h * ATT_HEAD_DIM:(h + 1) * ATT_HEAD_DIM] = dqt[:, cols].T
                    head_lane = lax.broadcasted_iota(jnp.int32, dsk.shape, 1) == h
                    dsk[...] += jnp.where(head_lane, -jnp.sum(sk[:, cols], axis=1, keepdims=True), 0.0)
            cs, sl, sr = cos_ref[rows, :], sl_ref[rows, :], sr_ref[rows, :]
            for j in range(ATT_WIDTH // LANES):
                dqkv_ref[rows, j * LANES:(j + 1) * LANES] = _bf(_rope_t(dqb[:, j * LANES:(j + 1) * LANES], cs, sl, sr))
            return daw

        daw = _loop_pairs(part * per, per, block, jnp.zeros((1, ATT_WIDTH), F32))
        daw_ref[...] += jnp.broadcast_to(daw, (8, ATT_WIDTH))
        dsink_ref[...] = dsk[...]

        def finish(n, carry):
            r0 = pl.multiple_of(n * WINDOW, WINDOW)
            rows = pl.ds(r0, WINDOW)
            nxt = pl.ds(r0 + WINDOW, WINDOW)
            cs, sl, sr = cos_ref[rows, :], sl_ref[rows, :], sr_ref[rows, :]
            dqkv_ref[rows, ATT_WIDTH:ATT_WIDTH + LANES] = _bf(_rope_t(dkpad[nxt, :], cs, sl, sr))
            dqkv_ref[rows, ATT_WIDTH + LANES:QKV] = _bf(dvpad[nxt, :])
            return carry

        @pl.when(part == splits - 1)
        def _():
            lax.fori_loop(0, nb, finish, 0)

        _ride_wait(ride_modes, step, B * splits, ride_in, ride_out, sems)

    seq = lambda w, j: pl.BlockSpec((None, T, w), lambda b, s: (b, 0, j))
    full = lambda r, w: pl.BlockSpec((r, w), lambda b, s: (0, 0))
    return pl.pallas_call(
        body, name="attn_bwd", grid=(B, splits),
        in_specs=[seq(ATT_WIDTH, 0), seq(LANES, 0), seq(LANES, 5), seq(ATT_WIDTH, 0), seq(ATT_WIDTH, 0),
                  full(T, LANES), full(T, LANES), full(T, LANES),
                  pl.BlockSpec(memory_space=pltpu.SMEM), full(1, ATT_WIDTH)] + [ANY_SPEC] * nr,
        out_specs=[seq(QKV, 0), pl.BlockSpec((None, 8, LANES), lambda b, s: (b, 0, 0)),
                   pl.BlockSpec((None, 8, ATT_WIDTH), lambda b, s: (b, 0, 0))] + [ANY_SPEC] * nr,
        out_shape=[SDS((B, T, QKV), BF16), SDS((B, 8, LANES), F32), SDS((B, 8, ATT_WIDTH), F32)]
        + _exchange_shapes(ride_srcs, ride_modes),
        scratch_shapes=[pltpu.VMEM((T + WINDOW, LANES), BF16), pltpu.VMEM((T + WINDOW, LANES), BF16),
                        pltpu.VMEM((T + WINDOW, LANES), F32), pltpu.VMEM((T + WINDOW, LANES), F32),
                        pltpu.VMEM((WINDOW, ATT_WIDTH), F32), pltpu.VMEM((8, LANES), F32)] + _exchange_sems(nr),
        compiler_params=_params(("arbitrary", "arbitrary"), VMEM_LIMIT_BIG),
    )(qr, kr, proj3, attn_o, dan, cos, sinl, sinr, sinks, attn_w, *ride_srcs)


def _in_bwd(x2, dx1, dqkv, dhq, dhf, dhi, dhg, mod8, pre_w, w_in_bf, T, ride_srcs, ride_modes):
    N = x2.shape[0]
    TM = _tile_rows(T, big=True)
    tps = T // TM
    nr = len(ride_srcs)
    pieces = [(0, ATT_WIDTH + 2 * LANES), (768, HG_WIDTH), (1280, HG_WIDTH), (1792, HG_WIDTH), (2304, HG_WIDTH)]

    def body(*refs):
        x_ref, dx_ref, p0, p1, p2, p3, p4, mod_ref, pw_ref, w_ref = refs[:10]
        ride_in = refs[10:10 + nr]
        gx_ref, dproj_ref, acc_ref = refs[10 + nr:13 + nr]
        ride_out = refs[13 + nr:13 + 2 * nr]
        sems = refs[13 + 2 * nr:]
        _ride_start(ride_modes, pl.program_id(0), N // TM, ride_in, ride_out, sems)
        sc1 = mod_ref[1:2, :]
        dh = jnp.zeros((TM, D_MODEL), F32)
        for ref, (off, width) in zip((p0, p1, p2, p3, p4), pieces):
            pb = ref[...]
            dproj_ref[:, off:off + width] = pb
            dh = dh + _dot(pb, w_ref[off:off + width, :])
        x = x_ref[...]
        r = lax.rsqrt(_mean_last(x * x) + EPS)
        xh = x * r
        n1 = xh * pw_ref[...]
        dsh1 = _sum_rows(dh)
        dsc1 = _sum_rows(dh * n1)
        dn1 = dh * (1.0 + sc1)
        dw_pre = _sum_rows(dn1 * xh)
        dxh = dn1 * pw_ref[...]
        gx_ref[...] = dx_ref[...] + r * (dxh - xh * _mean_last(dxh * xh))
        _acc_rows(acc_ref, pl.program_id(0) % tps == 0, [dsh1, dsc1, dw_pre])
        _ride_wait(ride_modes, pl.program_id(0), N // TM, ride_in, ride_out, sems)

    row = lambda w: pl.BlockSpec((TM, w), lambda i: (i, 0))
    B = N // T
    return pl.pallas_call(
        body, name="in_bwd", grid=(N // TM,),
        in_specs=[row(D_MODEL), row(D_MODEL), row(768), row(HG_WIDTH), row(HG_WIDTH), row(HG_WIDTH),
                  row(HG_WIDTH), _mod_spec(tps), pl.BlockSpec((1, D_MODEL), lambda i: (0, 0)),
                  pl.BlockSpec((IN_COLS, D_MODEL), lambda i: (0, 0))] + [ANY_SPEC] * nr,
        out_specs=[row(D_MODEL), row(IN_COLS), _mod_spec(tps)] + [ANY_SPEC] * nr,
        out_shape=[SDS((N, D_MODEL), F32), SDS((N, IN_COLS), BF16), SDS((B, 8, D_MODEL), F32)]
        + _exchange_shapes(ride_srcs, ride_modes),
        scratch_shapes=_exchange_sems(nr),
        compiler_params=_params(("arbitrary",), VMEM_LIMIT_BIG),
    )(x2, dx1, dqkv, dhq, dhf, dhi, dhg, mod8, pre_w, w_in_bf, *ride_srcs)


def _matmul_tn(name, a, b, tn, tm):
    K, M = a.shape
    Nc = b.shape[1]

    def body(a_ref, b_ref, o_ref):
        o_ref[...] = _bf(_dot_tn(a_ref[...], b_ref[...]))

    return pl.pallas_call(
        body, name=name, grid=(M // tm, Nc // tn),
        in_specs=[pl.BlockSpec((K, tm), lambda i, j: (0, i)),
                  pl.BlockSpec((K, tn), lambda i, j: (0, j))],
        out_specs=pl.BlockSpec((tm, tn), lambda i, j: (i, j)), out_shape=SDS((M, Nc), BF16),
        compiler_params=_params(("arbitrary", "arbitrary"), VMEM_LIMIT_BIG),
    )(a, b)


def _matmul_tn_paired(name, a, b, stream_a):
    K = a.shape[0]
    stream, fixed = (a, b) if stream_a else (b, a)
    w = stream.shape[1] // N_DEV
    blk = (w, fixed.shape[1]) if stream_a else (fixed.shape[1], w)
    chips = N_DEV // 2

    def body(s_hbm, f_hbm, out_ref, s_buf, f_buf, g_buf, theirs, in_sems, send_sems, recv_sems):
        core = lax.axis_index("c")
        sib_dev, _ = _related(SIBLING)
        fixed_load = pltpu.make_async_copy(f_hbm, f_buf, in_sems.at[2])

        def load(j):
            owner = 2 * (j // 2) + (core if j % 2 else 1 - core)
            return pltpu.make_async_copy(s_hbm.at[:, pl.ds(pl.multiple_of(owner * w, LANES), w)], s_buf.at[j % 2],
                                         in_sems.at[j % 2])

        def swap(s):
            return pltpu.make_async_remote_copy(
                src_ref=g_buf.at[s, 0], dst_ref=theirs.at[s], send_sem=send_sems.at[s],
                recv_sem=recv_sems.at[s], device_id=sib_dev, device_id_type=MESH)

        fixed_load.start()
        load(0).start()
        fixed_load.wait()
        for j in range(N_DEV):
            s, mine = divmod(j, 2)
            load(j).wait()
            if j + 1 < N_DEV:
                load(j + 1).start()
            if stream_a:
                g_buf[s, mine] = _bf(_dot_tn(s_buf[j % 2], f_buf[...]))
            else:
                g_buf[s, mine] = _bf(_dot_tn(f_buf[...], s_buf[j % 2]))
            if mine:
                swap(s).wait_recv()
                out_ref[s] = _bf(g_buf[s, 1].astype(F32) + theirs[s].astype(F32))
            else:
                swap(s).start()
        for s in range(chips):
            swap(s).wait_send()

    return pl.pallas_call(
        body, name=name, in_specs=[ANY_SPEC] * 2, out_shape=SDS((chips,) + blk, BF16),
        scratch_shapes=[pltpu.VMEM((2, K, w), BF16), pltpu.VMEM(fixed.shape, BF16),
                        pltpu.VMEM((chips, 2) + blk, BF16), pltpu.VMEM((chips,) + blk, BF16),
                        pltpu.SemaphoreType.DMA((3,)), pltpu.SemaphoreType.DMA((chips,)),
                        pltpu.SemaphoreType.DMA((chips,))],
        compiler_params=_params(None, VMEM_LIMIT_BIG),
    )(stream, fixed)


GW_BLOCK = IN_COLS // N_DEV
GW_HALF = IN_COLS // 2


def _grad_w_in_reduced(dproj, h1, ride_srcs, ride_modes):
    K = dproj.shape[0]
    nr = len(ride_srcs)
    chips = N_DEV // 2
    tn = 512

    def body(*refs):
        a_hbm, b_hbm = refs[:2]
        ride_in, out, ride_out = refs[2:2 + nr], refs[2 + nr], refs[3 + nr:3 + 2 * nr]
        a_buf, b_buf, g_buf, theirs, p_buf, in_sems, pair_send, pair_recv, chip_send, chip_recv, own_sem = \
            refs[3 + 2 * nr:14 + 2 * nr]
        ride = _exchange_phases(ride_modes, ride_in, ride_out, *refs[14 + 2 * nr:]) if nr else ([], [], [])
        x, y, core = lax.axis_index("x"), lax.axis_index("y"), lax.axis_index("c")
        chip = 2 * x + y
        sib_dev, _ = _related(SIBLING)

        def remote(src, dst, send_sem, recv_sem, dev):
            return pltpu.make_async_remote_copy(src_ref=src, dst_ref=dst, send_sem=send_sem, recv_sem=recv_sem,
                                                device_id=dev, device_id_type=MESH)

        halves = [1 - x, x]
        loads = [pltpu.make_async_copy(b_hbm, b_buf, in_sems.at[0])]
        for t in range(2):
            col = pl.multiple_of(halves[t] * GW_HALF, LANES)
            loads.append(pltpu.make_async_copy(a_hbm.at[:, pl.ds(col, GW_HALF)], a_buf.at[t], in_sems.at[1 + t]))
        for cp in loads:
            cp.start()
        _run(ride[0])
        loads[0].wait()
        end = []
        for t in range(2):
            loads[1 + t].wait()
            if t == 1:
                _run(ride[1])
            for j in range(D_MODEL // tn):
                cols = pl.ds(j * tn, tn)
                res = _dot_tn(a_buf[t], b_buf[:, cols])
                for q in range(2):
                    for cc in range(2):
                        r0 = (2 * q + cc) * GW_BLOCK
                        g_buf[t, q, cc, :, cols] = _bf(res[r0:r0 + GW_BLOCK])
                swaps = [remote(g_buf.at[t, q, 1 - core, :, cols], theirs.at[t, q, :, cols],
                                pair_send.at[t, 2 * j + q], pair_recv.at[t, 2 * j + q], sib_dev) for q in range(2)]
                for cp in swaps:
                    cp.start()
                for cp in swaps:
                    cp.wait_recv()
                end += [cp.wait_send for cp in swaps]
                for q in range(2):
                    p_buf[t, q, :, cols] = _bf(g_buf[t, q, core, :, cols].astype(F32)
                                               + theirs[t, q, :, cols].astype(F32))
                for dy in range(2):
                    k = 4 * (1 - t) + 2 * dy
                    if k == 0:
                        own = pltpu.make_async_copy(p_buf.at[t, y, :, cols], out.at[chip, :, cols], own_sem.at[j])
                        own.start()
                        end.append(own.wait)
                        continue
                    dev, peer = _related(k)
                    sems = chip_send.at[k // 2, j], chip_recv.at[k // 2, j]
                    send = remote(p_buf.at[t, y ^ dy, :, cols], out.at[chip, :, cols], *sems, dev)
                    send.start()
                    end += [remote(p_buf.at[t, y ^ dy, :, cols], out.at[peer // 2, :, cols], *sems, dev).wait_recv,
                            send.wait_send]
        _run(ride[2])
        _run(end)

    return pl.pallas_call(
        body, name="grad_w_in",
        in_specs=[ANY_SPEC] * (2 + nr), out_specs=[ANY_SPEC] * (1 + nr),
        out_shape=[SDS((chips, GW_BLOCK, D_MODEL), BF16)] + _exchange_shapes(ride_srcs, ride_modes),
        scratch_shapes=[pltpu.VMEM((2, K, GW_HALF), BF16), pltpu.VMEM((K, D_MODEL), BF16),
                        pltpu.VMEM((2, 2, 2, GW_BLOCK, D_MODEL), BF16), pltpu.VMEM((2, 2, GW_BLOCK, D_MODEL), BF16),
                        pltpu.VMEM((2, 2, GW_BLOCK, D_MODEL), BF16), pltpu.SemaphoreType.DMA((3,)),
                        pltpu.SemaphoreType.DMA((2, 4)), pltpu.SemaphoreType.DMA((2, 4)),
                        pltpu.SemaphoreType.DMA((chips, 2)), pltpu.SemaphoreType.DMA((chips, 2)),
                        pltpu.SemaphoreType.DMA((2,))] + _exchange_sems(nr),
        compiler_params=_params(None, VMEM_LIMIT_BIG),
    )(dproj, h1, *ride_srcs)


def _adamw_math(w, g, m, v):
    m2 = ADAM_B1 * m + (1.0 - ADAM_B1) * g
    v2 = ADAM_B2 * v + (1.0 - ADAM_B2) * (g * g)
    m_hat = m2 / (1.0 - ADAM_B1 ** ADAM_STEP)
    v_hat = v2 / (1.0 - ADAM_B2 ** ADAM_STEP)
    delta = -ADAM_LR * (m_hat / (jnp.sqrt(v_hat) + ADAM_EPS) + ADAM_WD * w)
    return delta, m2, v2


def _reduce_adamw(name, parts, w, m, v):
    r, c = w.shape
    tr = r if r % 256 else 256
    slots = parts.shape[0]

    def body(p_ref, w_ref, m_ref, v_ref, g_ref, d_ref, m2_ref, v2_ref):
        g = p_ref[0].astype(F32)
        for s in range(1, slots):
            g = g + p_ref[s].astype(F32)
        g_ref[...] = g
        d_ref[...], m2_ref[...], v2_ref[...] = _adamw_math(w_ref[...], g, m_ref[...], v_ref[...])

    blk = pl.BlockSpec((tr, c), lambda i: (i, 0))
    return pl.pallas_call(
        body, name=name, grid=(r // tr,),
        in_specs=[pl.BlockSpec((slots, tr, c), lambda i: (0, i, 0)), blk, blk, blk],
        out_specs=[blk] * 4, out_shape=[SDS((r, c), F32)] * 4,
        compiler_params=_params(("arbitrary",), VMEM_LIMIT_BIG),
    )(parts, w, m, v)


def _ada_grad_adamw(c_all, dmod_all, w, m, v):
    r, c = w.shape
    tr = 256
    nb = c_all.shape[0]

    def body(c_ref, dm_ref, w_ref, m_ref, v_ref, g_ref, d_ref, m2_ref, v2_ref):
        cv = c_ref[...]
        g = _dot_tn(cv * _sigmoid(cv), dm_ref[...])
        g_ref[...] = g
        d_ref[...], m2_ref[...], v2_ref[...] = _adamw_math(w_ref[...], g, m_ref[...], v_ref[...])

    blk = pl.BlockSpec((tr, c), lambda i: (i, 0))
    return pl.pallas_call(
        body, name="ada_grad_adamw", grid=(r // tr,),
        in_specs=[pl.BlockSpec((nb, tr), lambda i: (0, i)), pl.BlockSpec((nb, c), lambda i: (0, 0)),
                  blk, blk, blk],
        out_specs=[blk] * 4, out_shape=[SDS((r, c), F32)] * 4,
        compiler_params=_params(("arbitrary",)),
    )(c_all, dmod_all, w, m, v)


_SMALL = [("b_ada", 6144), ("pre_w_mix", 1024), ("attn_sinks", 128), ("attn_out_w", 512), ("lb_table", 1024),
          ("hg_norm_w", 128), ("post_w_mix", 1024), ("pre_w_mlp", 1024), ("post_w_mlp", 1024)]


def _pack_small(acc_in, acc_mix, acc_mlp, dsink, daw, dlb, dgw, lb_p, ada_cols):
    B = acc_in.shape[0]
    width = sum(w for _, w in _SMALL) + LANES

    def body(ain, amix, amlp, dsk_ref, daw_ref, dlb_ref, dgw_ref, lbp_ref, packed_ref, dmod_ref):
        def total(ref, r, w=None):
            out = ref[0, r:r + 1, :] if w is None else ref[0, r:r + 1, :w]
            for b in range(1, B):
                out = out + (ref[b, r:r + 1, :] if w is None else ref[b, r:r + 1, :w])
            return out

        d_b_ada = None
        for b in range(B):
            mods = [ain[b, 0:1, :], ain[b, 1:2, :], amix[b, 0:1, :], amlp[b, 0:1, :], amlp[b, 1:2, :], amlp[b, 2:3, :]]
            full = jnp.concatenate(mods, axis=1)
            for j in range(N_DEV):
                dmod_ref[j, b:b + 1, :] = full[:, j * ada_cols:(j + 1) * ada_cols]
            d_b_ada = full if d_b_ada is None else d_b_ada + full
        d_lb = total(dlb_ref, 0)
        pp = lbp_ref[0:1, :] * lbp_ref[1:2, :]
        pieces = [d_b_ada, total(ain, 2), total(dsk_ref, 0), total(daw_ref, 0), -d_lb * pp, d_lb * pp,
                  total(dgw_ref, 0), total(amix, 1), total(amlp, 3), total(amlp, 4), total(amlp, 5, LANES)]
        off = 0
        for piece in pieces:
            packed_ref[:, off:off + piece.shape[1]] = piece
            off += piece.shape[1]

    return pl.pallas_call(
        body, name="pack_small",
        out_shape=[SDS((1, width), F32), SDS((N_DEV, B, ada_cols), F32)],
    )(acc_in, acc_mix, acc_mlp, dsink, daw, dlb, dgw, lb_p)


def _adamw_small(parts, given):
    names = [n for n, _ in _SMALL]
    flat_in = [a for n in names for a in given[n]]

    def body(*refs):
        p_ref = refs[0]
        in_refs = refs[1:1 + 3 * len(names)]
        out_refs = refs[1 + 3 * len(names):-1]
        loss_ref = refs[-1]
        g = p_ref[0]
        for s in range(1, N_DEV):
            g = g + p_ref[s]
        off = 0
        for i, (name, width) in enumerate(_SMALL):
            w_ref, m_ref, v_ref = in_refs[3 * i:3 * i + 3]
            rows, cols = w_ref.shape
            for r in range(rows):
                gr = g[:, off + r * cols:off + (r + 1) * cols]
                res = (gr,) + _adamw_math(w_ref[r:r + 1, :], gr, m_ref[r:r + 1, :], v_ref[r:r + 1, :])
                for o_ref, val in zip(out_refs[4 * i:4 * i + 4], res):
                    o_ref[r:r + 1, :] = val
            off += width
        loss_ref[...] = g[:, off:off + LANES]

    out_shape = [SDS(given[n][0].shape, F32) for n in names for _ in range(4)] + [SDS((1, LANES), F32)]
    outs = pl.pallas_call(body, name="adamw_small", out_shape=out_shape)(parts, *flat_in)
    return {n: tuple(outs[4 * i:4 * i + 4]) for i, n in enumerate(names)}, outs[-1][0, 0]


def kernel(x, c, w_ada, b_ada, pre_w_mix, w_in, attn_sinks, attn_out_w, lb_table, hg_norm_w, w_out, post_w_mix, pre_w_mlp, w_up, w_down, post_w_mlp, loss_target, m_w_ada, m_b_ada, m_pre_w_mix, m_w_in, m_attn_sinks, m_attn_out_w, m_lb_table, m_hg_norm_w, m_w_out, m_post_w_mix, m_pre_w_mlp, m_w_up, m_w_down, m_post_w_mlp, v_w_ada, v_b_ada, v_pre_w_mix, v_w_in, v_attn_sinks, v_attn_out_w, v_lb_table, v_hg_norm_w, v_w_out, v_post_w_mix, v_pre_w_mlp, v_w_up, v_w_down, v_post_w_mlp):
    B, T, _ = x.shape
    N = B * T
    me = 4 * lax.axis_index("x") + 2 * lax.axis_index("y") + lax.axis_index("c")
    x2 = x.reshape(N, D_MODEL)
    tgt2 = loss_target.reshape(N, D_MODEL)

    w_in_t, m_w_in_t, v_w_in_t = w_in[0].T, m_w_in[0].T, v_w_in[0].T
    w_in_g, c_g = _exchange("gather_w_in", [_bf(w_in_t), c], ["gather"] * 2)
    w_in_f = w_in_g.reshape(IN_COLS, D_MODEL)
    c_all = c_g.reshape(N_DEV * B, D_MODEL)

    ada_cols = w_ada.shape[2]
    b_mine = lax.dynamic_slice(b_ada, (0, me * ada_cols), (1, ada_cols))
    mod_cols = _ada_mod(c_all, w_ada[0], b_mine)
    (mod_g,) = _exchange("scatter_mod", [mod_cols.reshape(N_DEV, B, ada_cols)], ["a2a"])
    mod = mod_g.transpose(1, 0, 2).reshape(B, 6, D_MODEL)
    mod8 = jnp.pad(mod, ((0, 0), (0, 2), (0, 0)))

    lb_p = jax.nn.softmax(lb_table, axis=0)
    lb = lb_p[1:2]
    tables = _rope_tables(T)

    w_up_b, w_down_b = _bf(w_up[0]), _bf(w_down[0])
    proj_a, proj_h, h1, w_up_g0 = _in_proj(x2, mod8, pre_w_mix, w_in_f, T, [w_up_b[:MLP_HALF]], ["gather"])
    proj3 = proj_a.reshape(B, T, ATT_COLS)
    proj_h = proj_h.reshape(B, T, IN_COLS - ATT_COLS)
    rec_o, rec_g, s_prev, w_up_g1, w_out_g = _hgrn_fwd(proj_h, lb, hg_norm_w, [w_up_b[MLP_HALF:], _bf(w_out[0])],
                                                       ["gather"] * 2)
    attn_o, attn_n, qr, kr, w_down_g0 = _attn_fwd(proj3, tables, attn_sinks, attn_out_w,
                                                  [w_down_b[:, :MLP_HALF]], ["gather"])
    w_out_f = w_out_g.reshape(D_MODEL, D_MODEL)
    mix, x1, cat, w_down_g1 = _mix_out(x2, attn_n.reshape(N, ATT_WIDTH), rec_g.reshape(N, HG_WIDTH), mod8,
                                       post_w_mix, w_out_f, T, [w_down_b[:, MLP_HALF:]], ["gather"])
    w_up_halves = [w_up_g0, w_up_g1]
    w_down_halves = [w_down_g0.reshape(D_FF, MLP_HALF), w_down_g1.reshape(D_FF, MLP_HALF)]
    up, u, d, h2 = _mlp_fwd(x1, mod8, pre_w_mlp, w_up_halves, w_down_halves, T)

    dx1, dup, dd, acc_mlp = _mlp_bwd(x1, d, up, tgt2, mod8, pre_w_mlp, post_w_mlp, w_up_halves, w_down_halves, T)
    p_up = _matmul_tn_paired("grad_w_up", h2, dup, stream_a=False)
    p_down = _matmul_tn_paired("grad_w_down", u, dd, stream_a=True)
    dan, drg, dmix, acc_mix = _mix_bwd(mix, dx1, mod8, post_w_mix, w_out_f, T, [], [])
    gw_out = _matmul_tn("grad_w_out", cat, dmix, 512, tm=D_MODEL).reshape(N_DEV, D_MODEL // N_DEV, D_MODEL)
    dhq, dhf, dhi, dhg, dlb_p, dgw_p, r_down, r_out = _hgrn_bwd(
        proj_h, lb, hg_norm_w, rec_o, s_prev, drg.reshape(B, T, HG_WIDTH), [p_down, gw_out], ["chips", "a2a"])
    dqkv, dsink_p, daw_p, r_up = _attn_bwd(qr, kr, proj3, attn_o, dan.reshape(B, T, ATT_WIDTH), tables,
                                           attn_sinks, attn_out_w, [p_up], ["chips"])
    flat = lambda a: a.reshape(N, a.shape[-1])
    grad_x, dproj, acc_in = _in_bwd(x2, dx1, flat(dqkv), flat(dhq), flat(dhf), flat(dhi), flat(dhg),
                                    mod8, pre_w_mix, w_in_f, T, [], [])

    packed, dmod_blocks = _pack_small(acc_in, acc_mix, acc_mlp, dsink_p, daw_p, dlb_p, dgw_p, lb_p, ada_cols)
    r_in, r_dmod, r_small = _grad_w_in_reduced(dproj, h1, [dmod_blocks, packed], ["a2a", "gather"])

    res = {}
    res["w_in"] = tuple(a.T for a in _reduce_adamw("adamw_w_in", r_in, w_in_t, m_w_in_t, v_w_in_t))
    res["w_out"] = _reduce_adamw("adamw_w_out", r_out, w_out[0], m_w_out[0], v_w_out[0])
    res["w_up"] = _reduce_adamw("adamw_w_up", r_up, w_up[0], m_w_up[0], v_w_up[0])
    res["w_down"] = _reduce_adamw("adamw_w_down", r_down, w_down[0], m_w_down[0], v_w_down[0])
    res["w_ada"] = _ada_grad_adamw(c_all, r_dmod.reshape(N_DEV * B, ada_cols), w_ada[0], m_w_ada[0], v_w_ada[0])

    given = dict(b_ada=(b_ada, m_b_ada, v_b_ada), pre_w_mix=(pre_w_mix, m_pre_w_mix, v_pre_w_mix),
                 attn_sinks=(attn_sinks, m_attn_sinks, v_attn_sinks),
                 attn_out_w=(attn_out_w, m_attn_out_w, v_attn_out_w), lb_table=(lb_table, m_lb_table, v_lb_table),
                 hg_norm_w=(hg_norm_w, m_hg_norm_w, v_hg_norm_w), post_w_mix=(post_w_mix, m_post_w_mix, v_post_w_mix),
                 pre_w_mlp=(pre_w_mlp, m_pre_w_mlp, v_pre_w_mlp), post_w_mlp=(post_w_mlp, m_post_w_mlp, v_post_w_mlp))
    small_res, loss = _adamw_small(r_small, given)
    res.update(small_res)

    order = ["w_ada", "b_ada", "pre_w_mix", "w_in", "attn_sinks", "attn_out_w", "lb_table", "hg_norm_w", "w_out",
             "post_w_mix", "pre_w_mlp", "w_up", "w_down", "post_w_mlp"]
    big = {"w_ada", "w_in", "w_out", "w_up", "w_down"}
    outs = [loss, grad_x.reshape(B, T, D_MODEL)]
    for i in range(4):
        for k in order:
            a = res[k][i]
            outs.append(a[None] if k in big else a)
    return tuple(outs)
```

```python
import jax
import jax.numpy as jnp
import numpy as np
from jax import lax
from jax.experimental import pallas as pl
from jax.experimental.pallas import tpu as pltpu

F32 = jnp.float32
BF16 = jnp.bfloat16
SDS = jax.ShapeDtypeStruct

D_MODEL = 1024
ATT_WIDTH = 512
ATT_HEAD_DIM = 64
ATT_KV_HEADS = 2
ATT_GROUP = 4
WINDOW = 128
ROPE_DIM = 16
ROPE_THETA = 500000.0
HG_WIDTH = 512
HG_HEAD_DIM = 128
HG_HEADS = 4
HG_CHUNK = 32
IN_COLS = 2816
ATT_COLS = 768
D_FF = 4096
EPS = 1e-6
N_DEV = 8

ADAM_LR = 0.001
ADAM_B1 = 0.9
ADAM_B2 = 0.999
ADAM_EPS = 1e-08
ADAM_WD = 0.01
ADAM_STEP = 10

VMEM_LIMIT_BIG = 56 << 20
LANES = 128

MESH = pl.DeviceIdType.MESH
NT_DIMS = (((1,), (1,)), ((), ()))
TN_DIMS = (((0,), (0,)), ((), ()))


def _dot(a, b):
    return jnp.dot(a, b, preferred_element_type=F32)


def _dot_nt(a, b):
    return lax.dot_general(a, b, NT_DIMS, preferred_element_type=F32)


def _dot_tn(a, b):
    return lax.dot_general(a, b, TN_DIMS, preferred_element_type=F32)


def _bf(a):
    return a.astype(BF16)


def _sigmoid(a):
    return 0.5 * jnp.tanh(0.5 * a) + 0.5


def _mean_last(a):
    return jnp.mean(a, axis=-1, keepdims=True)


def _sum_rows(a):
    return jnp.sum(a, axis=0, keepdims=True)


def _loop_pairs(first, count, body, init, per_trip=2):
    if count % per_trip:
        return lax.fori_loop(first, first + count, body, init)

    def trip(i, c):
        for k in range(per_trip):
            c = body(first + per_trip * i + k, c)
        return c

    return lax.fori_loop(0, count // per_trip, trip, init)


def _params(sem=None, vmem=None):
    kw = {}
    if sem is not None:
        kw["dimension_semantics"] = sem
    if vmem is not None:
        kw["vmem_limit_bytes"] = vmem
    return pltpu.CompilerParams(**kw)


ANY_SPEC = pl.BlockSpec(memory_space=pl.ANY)


def _exchange_shapes(srcs, modes):
    out_shape = []
    for s, m in zip(srcs, modes):
        shp = (N_DEV,) + tuple(s.shape) if m == "gather" else tuple(s.shape)
        out_shape.append(SDS(shp, s.dtype))
    return out_shape


def _exchange_sems(n):
    if n == 0:
        return []
    return [pltpu.SemaphoreType.DMA((n, N_DEV - 1)), pltpu.SemaphoreType.DMA((n, N_DEV - 1)),
            pltpu.SemaphoreType.DMA((n,))]


SIBLING = 1
OTHER_CHIPS = (2, 4, 6)


def _related(k):
    x, y, c = lax.axis_index("x"), lax.axis_index("y"), lax.axis_index("c")
    px, py, pc = x ^ ((k >> 2) & 1), y ^ ((k >> 1) & 1), c ^ (k & 1)
    return (px, py, pc), 4 * px + 2 * py + pc


def _exchange_phases(modes, src_refs, out_refs, send_sems, recv_sems, own_sems):
    _, me = _related(0)
    sib_dev, sib = _related(SIBLING)
    start, middle, end = [], [], []

    def remote(a, i, src, dst, dev):
        return pltpu.make_async_remote_copy(src_ref=src, dst_ref=dst, send_sem=send_sems.at[a, i],
                                            recv_sem=recv_sems.at[a, i], device_id=dev, device_id_type=MESH)

    for a, mode in enumerate(modes):
        out = out_refs[a]
        if mode == "gather":
            src = src_refs[a]
            own = pltpu.make_async_copy(src, out.at[me], own_sems.at[a])
            to_sib = remote(a, 0, src, out.at[me], sib_dev)
            start += [own.start, to_sib.start]
            end += [remote(a, 0, src, out.at[sib], sib_dev).wait_recv, to_sib.wait_send, own.wait]
            for j, k in enumerate(OTHER_CHIPS, start=1):
                dev, peer = _related(k)
                _, peer_sib = _related(k ^ SIBLING)
                send = remote(a, j, src, out.at[me], dev)
                passed = remote(a, 3 + j, out.at[peer], out.at[peer], sib_dev)
                start.append(send.start)
                middle += [remote(a, j, src, out.at[peer], dev).wait_recv, passed.start]
                end += [remote(a, 3 + j, out.at[peer_sib], out.at[peer_sib], sib_dev).wait_recv,
                        send.wait_send, passed.wait_send]
        elif mode == "chips":
            chip = me // 2
            own = pltpu.make_async_copy(src_refs[a].at[chip], out.at[chip], own_sems.at[a])
            start.append(own.start)
            end.append(own.wait)
            for j, k in enumerate(OTHER_CHIPS, start=1):
                dev, peer = _related(k)
                send = remote(a, j, src_refs[a].at[peer // 2], out.at[chip], dev)
                start.append(send.start)
                end += [remote(a, j, src_refs[a].at[peer // 2], out.at[peer // 2], dev).wait_recv, send.wait_send]
        else:
            own = pltpu.make_async_copy(src_refs[a].at[me], out.at[me], own_sems.at[a])
            start.append(own.start)
            end.append(own.wait)
            for k in range(1, N_DEV):
                dev, peer = _related(k)
                send = remote(a, k - 1, src_refs[a].at[peer], out.at[me], dev)
                start.append(send.start)
                end += [remote(a, k - 1, src_refs[a].at[peer], out.at[peer], dev).wait_recv, send.wait_send]
    return start, middle, end


def _run(actions):
    for act in actions:
        act()


def _exchange(name, srcs, modes):
    n = len(srcs)

    def body(*refs):
        start, middle, end = _exchange_phases(modes, refs[:n], refs[n:2 * n], *refs[2 * n:])
        _run(start)
        _run(middle)
        _run(end)

    return pl.pallas_call(
        body, name=name, out_shape=_exchange_shapes(srcs, modes),
        in_specs=[ANY_SPEC] * n, out_specs=[ANY_SPEC] * n,
        scratch_shapes=_exchange_sems(n),
    )(*srcs)


def _ride_start(modes, step, steps, src_refs, out_refs, sems):
    if not modes:
        return
    middle_step = steps - 1

    @pl.when(step == 0)
    def _():
        _run(_exchange_phases(modes, src_refs, out_refs, *sems)[0])

    if "gather" in modes:
        @pl.when(step == middle_step)
        def _():
            _run(_exchange_phases(modes, src_refs, out_refs, *sems)[1])


def _ride_wait(modes, step, steps, src_refs, out_refs, sems):
    if not modes:
        return

    @pl.when(step == steps - 1)
    def _():
        _run(_exchange_phases(modes, src_refs, out_refs, *sems)[2])


def _ada_mod(c_all, w_ada, b_ada_mine):
    nb, cols = c_all.shape[0], w_ada.shape[1]

    def body(c_ref, w_ref, b_ref, o_ref):
        cv = c_ref[...]
        ca = cv * _sigmoid(cv)
        o_ref[...] = _dot(ca, w_ref[...]) + b_ref[...]

    return pl.pallas_call(body, name="ada_mod", out_shape=SDS((nb, cols), F32))(c_all, w_ada, b_ada_mine)


def _tile_rows(T, big=False):
    return min(512 if big else 256, T)


def _mod_spec(tps):
    return pl.BlockSpec((None, 8, D_MODEL), lambda i: (i // tps, 0, 0))


def _in_proj(x2, mod8, pre_w, w_in_bf, T, ride_srcs, ride_modes):
    N = x2.shape[0]
    TM = _tile_rows(T, big=True)
    tps = T // TM
    nr = len(ride_srcs)

    def body(*refs):
        x_ref, mod_ref, pw_ref, w_ref = refs[:4]
        ride_in = refs[4:4 + nr]
        pa_ref, ph_ref, h1_ref = refs[4 + nr:7 + nr]
        ride_out = refs[7 + nr:7 + 2 * nr]
        sems = refs[7 + 2 * nr:]
        _ride_start(ride_modes, pl.program_id(0), N // TM, ride_in, ride_out, sems)
        x = x_ref[...]
        r = lax.rsqrt(_mean_last(x * x) + EPS)
        h = (x * r * pw_ref[...]) * (1.0 + mod_ref[1:2, :]) + mod_ref[0:1, :]
        hb = _bf(h)
        h1_ref[...] = hb
        pa_ref[...] = _dot_nt(hb, w_ref[:ATT_COLS, :])
        ph_ref[...] = _dot_nt(hb, w_ref[ATT_COLS:, :])
        _ride_wait(ride_modes, pl.program_id(0), N // TM, ride_in, ride_out, sems)

    return pl.pallas_call(
        body, name="in_proj", grid=(N // TM,),
        in_specs=[pl.BlockSpec((TM, D_MODEL), lambda i: (i, 0)), _mod_spec(tps),
                  pl.BlockSpec((1, D_MODEL), lambda i: (0, 0)),
                  pl.BlockSpec((IN_COLS, D_MODEL), lambda i: (0, 0))] + [ANY_SPEC] * nr,
        out_specs=[pl.BlockSpec((TM, ATT_COLS), lambda i: (i, 0)),
                   pl.BlockSpec((TM, IN_COLS - ATT_COLS), lambda i: (i, 0)),
                   pl.BlockSpec((TM, D_MODEL), lambda i: (i, 0))] + [ANY_SPEC] * nr,
        out_shape=[SDS((N, ATT_COLS), F32), SDS((N, IN_COLS - ATT_COLS), F32), SDS((N, D_MODEL), BF16)]
        + _exchange_shapes(ride_srcs, ride_modes),
        scratch_shapes=_exchange_sems(nr),
        compiler_params=_params(("arbitrary",), VMEM_LIMIT_BIG),
    )(x2, mod8, pre_w, w_in_bf, *ride_srcs)


def _rope_tables(T):
    half = ROPE_DIM // 2
    f32 = np.float32
    inv_freq = (f32(ROPE_THETA) ** (-np.arange(0, ROPE_DIM, 2, dtype=f32) / f32(ROPE_DIM))).astype(f32)
    ang = np.arange(T, dtype=f32)[:, None] * inv_freq[None, :]
    cos, sin = np.cos(ang).astype(f32), np.sin(ang).astype(f32)
    ones = np.ones((T, ATT_HEAD_DIM - ROPE_DIM), f32)
    zeros = np.zeros((T, ATT_HEAD_DIM - ROPE_DIM), f32)
    zh = np.zeros((T, half), f32)
    cos64 = np.concatenate([cos, cos, ones], axis=1)
    sin_left = np.concatenate([-sin, zh, zeros], axis=1)
    sin_right = np.concatenate([zh, sin, zeros], axis=1)
    rep = LANES // ATT_HEAD_DIM
    return tuple(jnp.asarray(np.tile(t, (1, rep))) for t in (cos64, sin_left, sin_right))


def _rope(xc, cs, sl, sr):
    return xc * cs + pltpu.roll(xc, LANES - 8, 1) * sl + pltpu.roll(xc, 8, 1) * sr


def _rope_t(dy, cs, sl, sr):
    return dy * cs + pltpu.roll(dy * sl, 8, 1) + pltpu.roll(dy * sr, LANES - 8, 1)


ATT_SCALE = ATT_HEAD_DIM ** -0.5
ATT_SPLITS = 4


def _lower_mask():
    j = lax.broadcasted_iota(jnp.int32, (WINDOW, ATT_GROUP * WINDOW), 0)
    i = lax.broadcasted_iota(jnp.int32, (WINDOW, ATT_GROUP * WINDOW), 1) & (WINDOW - 1)
    return j <= i


def _sink_row(sink_ref, hk):
    return jnp.concatenate(
        [jnp.full((1, WINDOW), sink_ref[0, ATT_GROUP * hk + g], F32) for g in range(ATT_GROUP)], axis=1)


def _softmax_window(qs, k_cur, k_prev, lower, has_prev, sink):
    s_prev = jnp.where(has_prev, _dot_nt(k_prev, qs), jnp.finfo(F32).min)
    s = jnp.where(lower, _dot_nt(k_cur, qs), s_prev)
    m = jnp.maximum(jnp.max(s, axis=0, keepdims=True), sink)
    p = jnp.exp(s - m)
    es = jnp.exp(sink - m)
    inv = 1.0 / (jnp.sum(p, axis=0, keepdims=True) + es)
    return p, inv, es


def _stack_heads(parts, hk):
    hs = []
    for g in range(ATT_GROUP):
        h = ATT_GROUP * hk + g
        hs.append(parts[h // 2][:, (h % 2) * ATT_HEAD_DIM:(h % 2 + 1) * ATT_HEAD_DIM])
    return jnp.concatenate(hs, axis=0)


def _attn_fwd(proj3, tables, sinks, attn_w, ride_srcs, ride_modes):
    B, T, _ = proj3.shape
    nb = T // WINDOW
    splits = min(ATT_SPLITS, nb)
    per = nb // splits
    nr = len(ride_srcs)
    cos, sinl, sinr = tables

    def body(*refs):
        q_ref, k_ref, v_ref, cos_ref, sl_ref, sr_ref, sink_ref, aw_ref = refs[:8]
        ride_in = refs[8:8 + nr]
        o_ref, an_ref, qr_ref, kr_ref = refs[8 + nr:12 + nr]
        ride_out = refs[12 + nr:12 + 2 * nr]
        kpad, vpad = refs[12 + 2 * nr:14 + 2 * nr]
        sems = refs[14 + 2 * nr:]
        part = pl.program_id(1)
        step = pl.program_id(0) * splits + part
        _ride_start(ride_modes, step, B * splits, ride_in, ride_out, sems)

        @pl.when(part == 0)
        def _():
            kpad[0:WINDOW, :] = jnp.zeros((WINDOW, LANES), BF16)
            vpad[0:WINDOW, :] = jnp.zeros((WINDOW, LANES), BF16)

        lower = _lower_mask()

        def block(n, carry):
            r0 = pl.multiple_of(n * WINDOW, WINDOW)
            rows = pl.ds(r0, WINDOW)
            nxt = pl.ds(r0 + WINDOW, WINDOW)
            cs, sl, sr = cos_ref[rows, :], sl_ref[rows, :], sr_ref[rows, :]
            kb = _bf(_rope(k_ref[rows, :], cs, sl, sr))
            vb = _bf(v_ref[rows, :])
            kpad[nxt, :] = kb
            kr_ref[rows, :] = kb
            vpad[nxt, :] = vb
            qparts = []
            for j in range(ATT_WIDTH // LANES):
                qp = _bf(_rope(q_ref[rows, j * LANES:(j + 1) * LANES], cs, sl, sr) * ATT_SCALE)
                qr_ref[rows, j * LANES:(j + 1) * LANES] = qp
                qparts.append(qp)
            for hk in range(ATT_KV_HEADS):
                lanes = slice(hk * ATT_HEAD_DIM, (hk + 1) * ATT_HEAD_DIM)
                qs = _stack_heads(qparts, hk)
                p, inv, _ = _softmax_window(qs, kb[:, lanes], kpad[rows, lanes], lower, n > 0,
                                            _sink_row(sink_ref, hk))
                p_cur = jnp.where(lower, p, 0.0)
                ot = (_dot_tn(vb[:, lanes], _bf(p_cur)) + _dot_tn(vpad[rows, lanes], _bf(p - p_cur))) * inv
                for g in range(ATT_GROUP):
                    h = ATT_GROUP * hk + g
                    o_ref[rows, h * ATT_HEAD_DIM:(h + 1) * ATT_HEAD_DIM] = ot[:, g * WINDOW:(g + 1) * WINDOW].T
            ob = o_ref[rows, :]
            an_ref[rows, :] = _bf(ob * lax.rsqrt(_mean_last(ob * ob) + EPS) * aw_ref[...])
            return carry

        _loop_pairs(part * per, per, block, 0)
        _ride_wait(ride_modes, step, B * splits, ride_in, ride_out, sems)

    seq = lambda w, j: pl.BlockSpec((None, T, w), lambda b, s: (b, 0, j))
    full = lambda r, w: pl.BlockSpec((r, w), lambda b, s: (0, 0))
    return pl.pallas_call(
        body, name="attn_fwd", grid=(B, splits),
        in_specs=[seq(ATT_WIDTH, 0), seq(LANES, 4), seq(LANES, 5),
                  full(T, LANES), full(T, LANES), full(T, LANES),
                  pl.BlockSpec(memory_space=pltpu.SMEM), full(1, ATT_WIDTH)] + [ANY_SPEC] * nr,
        out_specs=[seq(ATT_WIDTH, 0), seq(ATT_WIDTH, 0), seq(ATT_WIDTH, 0), seq(LANES, 0)] + [ANY_SPEC] * nr,
        out_shape=[SDS((B, T, ATT_WIDTH), F32), SDS((B, T, ATT_WIDTH), BF16),
                   SDS((B, T, ATT_WIDTH), BF16), SDS((B, T, LANES), BF16)] + _exchange_shapes(ride_srcs, ride_modes),
        scratch_shapes=[pltpu.VMEM((T + WINDOW, LANES), BF16), pltpu.VMEM((T + WINDOW, LANES), BF16)]
        + _exchange_sems(nr),
        compiler_params=_params(("arbitrary", "arbitrary"), VMEM_LIMIT_BIG),
    )(proj3, proj3, proj3, cos, sinl, sinr, sinks, attn_w, *ride_srcs)


HG_GROUP = 8
HG_ROWS = HG_GROUP * HG_CHUNK


HG_STACK = HG_GROUP * HG_HEAD_DIM


def _group_mask():
    r = lax.broadcasted_iota(jnp.int32, (HG_ROWS, HG_ROWS), 0)
    c = lax.broadcasted_iota(jnp.int32, (HG_ROWS, HG_ROWS), 1)
    return ((r // HG_CHUNK) == (c // HG_CHUNK)) & (r >= c)


def _spread(a):
    blocks = []
    for c in range(HG_GROUP):
        above = jnp.zeros((c * HG_CHUNK, HG_HEAD_DIM), a.dtype)
        below = jnp.zeros(((HG_GROUP - 1 - c) * HG_CHUNK, HG_HEAD_DIM), a.dtype)
        blocks.append(jnp.concatenate([p for p in (above, a[_chunk_rows(c), :], below) if p.shape[0]], axis=0))
    return jnp.concatenate(blocks, axis=1)


def _pick(r):
    return jnp.concatenate([r[_chunk_rows(c), c * HG_HEAD_DIM:(c + 1) * HG_HEAD_DIM] for c in range(HG_GROUP)], axis=0)


def _lane_block(a, c):
    return a[:, c * HG_HEAD_DIM:(c + 1) * HG_HEAD_DIM]


def _chunk_cumsum(a, reverse=False):
    n = a.shape[0]
    pos = lax.broadcasted_iota(jnp.int32, a.shape, 0) % HG_CHUNK
    shift = 1
    while shift < HG_CHUNK:
        if reverse:
            a = a + jnp.where(pos < HG_CHUNK - shift, pltpu.roll(a, n - shift, 0), 0.0)
        else:
            a = a + jnp.where(pos >= shift, pltpu.roll(a, shift, 0), 0.0)
        shift *= 2
    return a


def _chunk_bcast(rows_1x128):
    return jnp.concatenate([jnp.broadcast_to(r, (HG_CHUNK, HG_HEAD_DIM)) for r in rows_1x128], axis=0)


def _hgrn_gates(hq, hf, lb):
    sq = _sigmoid(hq)
    q = hq * sq
    sg = _sigmoid(hf)
    f = lb + (1.0 - lb) * sg
    k = 1.0 - f
    logf = jnp.log(f)
    b = _chunk_cumsum(logf)
    bl = [_sum_rows(logf[_chunk_rows(c), :]) for c in range(HG_GROUP)]
    eb, enb, e2 = jnp.exp(b), jnp.exp(-b), jnp.exp(_chunk_bcast(bl) - b)
    ebl = [jnp.exp(r) for r in bl]
    return dict(sq=sq, sg=sg, f=f, eb=eb, enb=enb, e2=e2, ebl=ebl, qd=q * eb, kd=k * enb, k2=k * e2)


def _chunk_rows(c):
    return slice(c * HG_CHUNK, (c + 1) * HG_CHUNK)


def _head_lanes(h):
    return slice(h * HG_HEAD_DIM, (h + 1) * HG_HEAD_DIM)


def _hgrn_fwd(proj_h, lb, hg_w, ride_srcs, ride_modes):
    B, T, _ = proj_h.shape
    ng = T // HG_ROWS
    nr = len(ride_srcs)

    def body(*refs):
        hq_ref, hf_ref, hi_ref, hg_ref, lb_ref, gw_ref = refs[:6]
        ride_in = refs[6:6 + nr]
        o_ref, rg_ref, sp_ref = refs[6 + nr:9 + nr]
        ride_out = refs[9 + nr:9 + 2 * nr]
        st = refs[9 + 2 * nr]
        sems = refs[10 + 2 * nr:]
        gi = pl.program_id(1)
        step = pl.program_id(0) * ng + gi
        _ride_start(ride_modes, step, B * ng, ride_in, ride_out, sems)

        @pl.when(gi == 0)
        def _():
            st[...] = jnp.zeros(st.shape, F32)

        lo = _group_mask()
        for h in range(HG_HEADS):
            lanes = _head_lanes(h)
            gt = _hgrn_gates(hq_ref[:, lanes], hf_ref[:, lanes], lb_ref[:, lanes])
            v, qd, kd = _bf(hi_ref[:, lanes]), _bf(gt["qd"]), _bf(gt["kd"])
            a = jnp.where(lo, _dot_nt(qd, kd), 0.0)
            kv = _dot_tn(v, _spread(_bf(gt["k2"])))
            s = st[h]
            before = []
            for c in range(HG_GROUP):
                before.append(s)
                s = s * gt["ebl"][c] + _lane_block(kv, c)
            st[h] = s
            sp = jnp.concatenate(before, axis=1)
            sp_ref[h] = sp
            o = _dot(_bf(a), v) + _dot_nt(_spread(qd), _bf(sp))
            o_ref[:, lanes] = o
            hg = hg_ref[:, lanes]
            rn = o * lax.rsqrt(_mean_last(o * o) + EPS) * gw_ref[...]
            rg_ref[:, lanes] = _bf(rn * (hg * _sigmoid(hg)))
        _ride_wait(ride_modes, step, B * ng, ride_in, ride_out, sems)

    part = lambda j: pl.BlockSpec((None, HG_ROWS, HG_WIDTH), lambda b, g: (b, g, j))
    return pl.pallas_call(
        body, name="hgrn_fwd", grid=(B, ng),
        in_specs=[part(0), part(1), part(2), part(3),
                  pl.BlockSpec((1, HG_WIDTH), lambda b, g: (0, 0)),
                  pl.BlockSpec((1, LANES), lambda b, g: (0, 0))] + [ANY_SPEC] * nr,
        out_specs=[part(0), part(0),
                   pl.BlockSpec((None, HG_HEADS, None, HG_HEAD_DIM, HG_STACK), lambda b, g: (b, 0, g, 0, 0))]
        + [ANY_SPEC] * nr,
        out_shape=[SDS((B, T, HG_WIDTH), F32), SDS((B, T, HG_WIDTH), BF16),
                   SDS((B, HG_HEADS, ng, HG_HEAD_DIM, HG_STACK), F32)] + _exchange_shapes(ride_srcs, ride_modes),
        scratch_shapes=[pltpu.VMEM((HG_HEADS, HG_HEAD_DIM, HG_HEAD_DIM), F32)] + _exchange_sems(nr),
        compiler_params=_params(("arbitrary", "arbitrary"), VMEM_LIMIT_BIG),
    )(proj_h, proj_h, proj_h, proj_h, lb, hg_w, *ride_srcs)


def _mix_out(x2, attn_n, rec_g, mod8, post_w, w_out_bf, T, ride_srcs, ride_modes):
    N = x2.shape[0]
    TM = _tile_rows(T, big=True)
    tps = T // TM
    nr = len(ride_srcs)

    def body(*refs):
        x_ref, an_ref, rg_ref, mod_ref, pw_ref, w_ref = refs[:6]
        ride_in = refs[6:6 + nr]
        mix_ref, x1_ref, cat_ref = refs[6 + nr:9 + nr]
        ride_out = refs[9 + nr:9 + 2 * nr]
        sems = refs[9 + 2 * nr:]
        _ride_start(ride_modes, pl.program_id(0), N // TM, ride_in, ride_out, sems)
        cat = jnp.concatenate([an_ref[...], rg_ref[...]], axis=1)
        cat_ref[...] = cat
        mix = _dot(cat, w_ref[...])
        mix_ref[...] = mix
        r = lax.rsqrt(_mean_last(mix * mix) + EPS)
        x1_ref[...] = x_ref[...] + mod_ref[2:3, :] * (mix * r * pw_ref[...])
        _ride_wait(ride_modes, pl.program_id(0), N // TM, ride_in, ride_out, sems)

    row = lambda w: pl.BlockSpec((TM, w), lambda i: (i, 0))
    return pl.pallas_call(
        body, name="mix_out", grid=(N // TM,),
        in_specs=[row(D_MODEL), row(ATT_WIDTH), row(HG_WIDTH), _mod_spec(tps),
                  pl.BlockSpec((1, D_MODEL), lambda i: (0, 0)),
                  pl.BlockSpec((D_MODEL, D_MODEL), lambda i: (0, 0))] + [ANY_SPEC] * nr,
        out_specs=[row(D_MODEL), row(D_MODEL), row(D_MODEL)] + [ANY_SPEC] * nr,
        out_shape=[SDS((N, D_MODEL), F32), SDS((N, D_MODEL), F32), SDS((N, D_MODEL), BF16)]
        + _exchange_shapes(ride_srcs, ride_modes),
        scratch_shapes=_exchange_sems(nr),
        compiler_params=_params(("arbitrary",), VMEM_LIMIT_BIG),
    )(x2, attn_n, rec_g, mod8, post_w, w_out_bf, *ride_srcs)


def _load_weights_once(pairs, sem):
    @pl.when(pl.program_id(0) == 0)
    def _():
        cps = [pltpu.make_async_copy(src, dst, sem.at[i]) for i, (src, dst) in enumerate(pairs)]
        for cp in cps:
            cp.start()
        for cp in cps:
            cp.wait()


MLP_HALF = D_MODEL // 2
MLP_PIECES = 2 * N_DEV + 2


def _mlp_weight_pieces(wu_a, wu_b, wd_a, wd_b, wu, wd):
    cols = D_FF // N_DEV
    pairs = []
    for h, half in enumerate((wu_a, wu_b)):
        for j in range(N_DEV):
            pairs.append((half.at[j], wu.at[pl.ds(h * MLP_HALF, MLP_HALF), pl.ds(j * cols, cols)]))
    col = 0
    for part in (wd_a, wd_b):
        pairs.append((part, wd.at[:, pl.ds(col, part.shape[1])]))
        col += part.shape[1]
    return pairs


def _mlp_fwd(x1, mod8, pre_w, w_up_halves, w_down_halves, T):
    N = x1.shape[0]
    TM = _tile_rows(T)
    tps = T // TM

    def body(x_ref, mod_ref, pw_ref, wua, wub, wda, wdb, up_ref, u_ref, d_ref, h2_ref, wu, wd, sem):
        _load_weights_once(_mlp_weight_pieces(wua, wub, wda, wdb, wu, wd), sem)
        x = x_ref[...]
        r = lax.rsqrt(_mean_last(x * x) + EPS)
        h = (x * r * pw_ref[...]) * (1.0 + mod_ref[4:5, :]) + mod_ref[3:4, :]
        hb = _bf(h)
        h2_ref[...] = hb
        up = _dot(hb, wu[...])
        up_ref[...] = up
        ru = jnp.maximum(up, 0.0)
        u = _bf(ru * ru)
        u_ref[...] = u
        d_ref[...] = _dot(u, wd[...])

    row = lambda w: pl.BlockSpec((TM, w), lambda i: (i, 0))
    return pl.pallas_call(
        body, name="mlp_fwd", grid=(N // TM,),
        in_specs=[row(D_MODEL), _mod_spec(tps), pl.BlockSpec((1, D_MODEL), lambda i: (0, 0))] + [ANY_SPEC] * 4,
        out_specs=[row(D_FF), row(D_FF), row(D_MODEL), row(D_MODEL)],
        out_shape=[SDS((N, D_FF), F32), SDS((N, D_FF), BF16), SDS((N, D_MODEL), F32), SDS((N, D_MODEL), BF16)],
        scratch_shapes=[pltpu.VMEM((D_MODEL, D_FF), BF16), pltpu.VMEM((D_FF, D_MODEL), BF16),
                        pltpu.SemaphoreType.DMA((MLP_PIECES,))],
        compiler_params=_params(("arbitrary",), VMEM_LIMIT_BIG),
    )(x1, mod8, pre_w, *w_up_halves, *w_down_halves)


def _acc_rows(acc_ref, first, rows):
    @pl.when(first)
    def _():
        acc_ref[...] = jnp.zeros(acc_ref.shape, F32)
    for i, r in enumerate(rows):
        acc_ref[i:i + 1, :] += r


def _mlp_bwd(x1, d, up, tgt, mod8, pre_w, post_w, w_up_halves, w_down_halves, T):
    N = x1.shape[0]
    TM = _tile_rows(T)
    tps = T // TM

    def body(x_ref, d_ref, up_ref, t_ref, mod_ref, pw_ref, qw_ref, wua, wub, wda, wdb,
             dx_ref, dup_ref, dd_ref, acc_ref, wd, wu, sem):
        _load_weights_once(_mlp_weight_pieces(wua, wub, wda, wdb, wu, wd), sem)
        sh2, sc2, g2 = mod_ref[3:4, :], mod_ref[4:5, :], mod_ref[5:6, :]
        x = x_ref[...]
        r1 = lax.rsqrt(_mean_last(x * x) + EPS)
        xh = x * r1
        n2 = xh * pw_ref[...]
        dv = d_ref[...]
        rd = lax.rsqrt(_mean_last(dv * dv) + EPS)
        dh = dv * rd
        rr = dh * qw_ref[...]
        e = x + g2 * rr - t_ref[...]
        loss = 0.5 * jnp.sum(_sum_rows(e * e), axis=1, keepdims=True) / D_MODEL
        dy = e * (1.0 / D_MODEL)
        dg2 = _sum_rows(dy * rr)
        drr = dy * g2
        dw_post = _sum_rows(drr * dh)
        ddh = drr * qw_ref[...]
        dd = _bf(rd * (ddh - dh * _mean_last(ddh * dh)))
        dd_ref[...] = dd
        ru = jnp.maximum(up_ref[...], 0.0)
        dup = _bf(_dot_nt(dd, wd[...]) * (2.0 * ru))
        dup_ref[...] = dup
        dh2 = _dot_nt(dup, wu[...])
        dsh2 = _sum_rows(dh2)
        dsc2 = _sum_rows(dh2 * n2)
        dn2 = dh2 * (1.0 + sc2)
        dw_pre = _sum_rows(dn2 * xh)
        dxh = dn2 * pw_ref[...]
        dx_ref[...] = dy + r1 * (dxh - xh * _mean_last(dxh * xh))
        _acc_rows(acc_ref, pl.program_id(0) % tps == 0,
                  [dsh2, dsc2, dg2, dw_pre, dw_post, jnp.broadcast_to(loss, (1, D_MODEL))])

    row = lambda w: pl.BlockSpec((TM, w), lambda i: (i, 0))
    vec = pl.BlockSpec((1, D_MODEL), lambda i: (0, 0))
    B = N // T
    return pl.pallas_call(
        body, name="mlp_bwd", grid=(N // TM,),
        in_specs=[row(D_MODEL), row(D_MODEL), row(D_FF), row(D_MODEL), _mod_spec(tps), vec, vec] + [ANY_SPEC] * 4,
        out_specs=[row(D_MODEL), row(D_FF), row(D_MODEL), _mod_spec(tps)],
        out_shape=[SDS((N, D_MODEL), F32), SDS((N, D_FF), BF16), SDS((N, D_MODEL), BF16),
                   SDS((B, 8, D_MODEL), F32)],
        scratch_shapes=[pltpu.VMEM((D_FF, D_MODEL), BF16), pltpu.VMEM((D_MODEL, D_FF), BF16),
                        pltpu.SemaphoreType.DMA((MLP_PIECES,))],
        compiler_params=_params(("arbitrary",), VMEM_LIMIT_BIG),
    )(x1, d, up, tgt, mod8, pre_w, post_w, *w_up_halves, *w_down_halves)


def _mix_bwd(mix, dx1, mod8, post_w, w_out_bf, T, ride_srcs, ride_modes):
    N = mix.shape[0]
    TM = _tile_rows(T, big=True)
    tps = T // TM
    nr = len(ride_srcs)

    def body(*refs):
        mix_ref, dx_ref, mod_ref, pw_ref, w_ref = refs[:5]
        ride_in = refs[5:5 + nr]
        dan_ref, drg_ref, dmix_ref, acc_ref = refs[5 + nr:9 + nr]
        ride_out = refs[9 + nr:9 + 2 * nr]
        sems = refs[9 + 2 * nr:]
        _ride_start(ride_modes, pl.program_id(0), N // TM, ride_in, ride_out, sems)
        g1 = mod_ref[2:3, :]
        mix = mix_ref[...]
        dx1 = dx_ref[...]
        rm = lax.rsqrt(_mean_last(mix * mix) + EPS)
        mh = mix * rm
        dg1 = _sum_rows(dx1 * (mh * pw_ref[...]))
        dr = dx1 * g1
        dw_post = _sum_rows(dr * mh)
        dmh = dr * pw_ref[...]
        dmix = _bf(rm * (dmh - mh * _mean_last(dmh * mh)))
        dmix_ref[...] = dmix
        dcat = _dot_nt(dmix, w_ref[...])
        dan_ref[...] = dcat[:, :ATT_WIDTH]
        drg_ref[...] = dcat[:, ATT_WIDTH:]
        _acc_rows(acc_ref, pl.program_id(0) % tps == 0, [dg1, dw_post])
        _ride_wait(ride_modes, pl.program_id(0), N // TM, ride_in, ride_out, sems)

    row = lambda w: pl.BlockSpec((TM, w), lambda i: (i, 0))
    B = N // T
    return pl.pallas_call(
        body, name="mix_bwd", grid=(N // TM,),
        in_specs=[row(D_MODEL), row(D_MODEL), _mod_spec(tps), pl.BlockSpec((1, D_MODEL), lambda i: (0, 0)),
                  pl.BlockSpec((D_MODEL, D_MODEL), lambda i: (0, 0))] + [ANY_SPEC] * nr,
        out_specs=[row(ATT_WIDTH), row(HG_WIDTH), row(D_MODEL), _mod_spec(tps)] + [ANY_SPEC] * nr,
        out_shape=[SDS((N, ATT_WIDTH), F32), SDS((N, HG_WIDTH), F32), SDS((N, D_MODEL), BF16),
                   SDS((B, 8, D_MODEL), F32)] + _exchange_shapes(ride_srcs, ride_modes),
        scratch_shapes=_exchange_sems(nr),
        compiler_params=_params(("arbitrary",), VMEM_LIMIT_BIG),
    )(mix, dx1, mod8, post_w, w_out_bf, *ride_srcs)


def _hgrn_bwd(proj_h, lb, hg_w, o, s_prev, drg, ride_srcs, ride_modes):
    B, T, _ = proj_h.shape
    ng = T // HG_ROWS
    nr = len(ride_srcs)

    def body(*refs):
        hq_ref, hf_ref, hi_ref, hg_ref, lb_ref, gw_ref, o_ref, sp_ref, drg_ref = refs[:9]
        ride_in = refs[9:9 + nr]
        dhq_ref, dhf_ref, dhi_ref, dhg_ref, dlb_ref, dgw_ref = refs[9 + nr:15 + nr]
        ride_out = refs[15 + nr:15 + 2 * nr]
        dst = refs[15 + 2 * nr]
        sems = refs[16 + 2 * nr:]
        step = pl.program_id(0) * ng + pl.program_id(1)
        _ride_start(ride_modes, step, B * ng, ride_in, ride_out, sems)

        @pl.when(pl.program_id(1) == 0)
        def _():
            dst[...] = jnp.zeros(dst.shape, F32)
            dlb_ref[...] = jnp.zeros(dlb_ref.shape, F32)
            dgw_ref[...] = jnp.zeros(dgw_ref.shape, F32)

        lo = _group_mask()
        gw = gw_ref[...]

        for h in range(HG_HEADS):
            lanes = _head_lanes(h)
            lbv = lb_ref[:, lanes]
            hq = hq_ref[:, lanes]
            gt = _hgrn_gates(hq, hf_ref[:, lanes], lbv)
            sq, sg, qdf, kdf, k2f, ebl = gt["sq"], gt["sg"], gt["qd"], gt["kd"], gt["k2"], gt["ebl"]
            v, qd, kd = _bf(hi_ref[:, lanes]), _bf(qdf), _bf(kdf)
            ov = o_ref[:, lanes]
            hg = hg_ref[:, lanes]
            shg = _sigmoid(hg)
            dr = drg_ref[:, lanes]
            ro = lax.rsqrt(_mean_last(ov * ov) + EPS)
            oh = ov * ro
            dhg_ref[:, lanes] = _bf(dr * (oh * gw) * (shg + hg * shg * (1.0 - shg)))
            drn = dr * (hg * shg)
            dgw_ref[...] += jnp.broadcast_to(_sum_rows(drn * oh), (8, LANES))
            doh = drn * gw
            do = _bf(ro * (doh - oh * _mean_last(doh * oh)))
            a = jnp.where(lo, _dot_nt(qd, kd), 0.0)
            da = _bf(jnp.where(lo, _dot_nt(do, v), 0.0))
            dv = _dot_tn(_bf(a), do)
            dqd = _dot(da, kd)
            dkd = _dot_tn(da, qd)
            sp = sp_ref[h]
            incr = _dot_tn(do, _spread(qd))
            ds = dst[h]
            after = [None] * HG_GROUP
            for c in reversed(range(HG_GROUP)):
                after[c] = ds
                ds = ds * ebl[c] + _lane_block(incr, c)
            dst[h] = ds
            dss = jnp.concatenate(after, axis=1)
            dssb = _bf(dss)
            dk2 = _pick(_dot(v, dssb))
            dhi_ref[:, lanes] = _bf(dv + _dot_nt(_spread(_bf(k2f)), dssb))
            dqd = dqd + _pick(_dot(do, _bf(sp)))
            debl = _sum_rows(dss * sp)
            k2g = dk2 * k2f
            db = dqd * qdf - dkd * kdf - k2g
            dk = dkd * gt["enb"] + dk2 * gt["e2"]
            dbl = _chunk_bcast([_lane_block(debl, c) * ebl[c] + _sum_rows(k2g[_chunk_rows(c), :])
                                for c in range(HG_GROUP)])
            dg = _chunk_cumsum(db, reverse=True) + dbl
            df = dg / gt["f"] - dk
            dhf_ref[:, lanes] = _bf(df * (1.0 - lbv) * sg * (1.0 - sg))
            dlb_ref[:, lanes] += jnp.broadcast_to(_sum_rows(df * (1.0 - sg)), (8, LANES))
            dhq_ref[:, lanes] = _bf((dqd * gt["eb"]) * (sq + hq * sq * (1.0 - sq)))
        _ride_wait(ride_modes, step, B * ng, ride_in, ride_out, sems)

    part = lambda j: pl.BlockSpec((None, HG_ROWS, HG_WIDTH), lambda b, g: (b, ng - 1 - g, j))
    return pl.pallas_call(
        body, name="hgrn_bwd", grid=(B, ng),
        in_specs=[part(0), part(1), part(2), part(3),
                  pl.BlockSpec((1, HG_WIDTH), lambda b, g: (0, 0)),
                  pl.BlockSpec((1, LANES), lambda b, g: (0, 0)),
                  part(0),
                  pl.BlockSpec((None, HG_HEADS, None, HG_HEAD_DIM, HG_STACK), lambda b, g: (b, 0, ng - 1 - g, 0, 0)),
                  part(0)] + [ANY_SPEC] * nr,
        out_specs=[part(0), part(0), part(0), part(0),
                   pl.BlockSpec((None, 8, HG_WIDTH), lambda b, g: (b, 0, 0)),
                   pl.BlockSpec((None, 8, LANES), lambda b, g: (b, 0, 0))] + [ANY_SPEC] * nr,
        out_shape=[SDS((B, T, HG_WIDTH), BF16)] * 4 + [SDS((B, 8, HG_WIDTH), F32), SDS((B, 8, LANES), F32)]
        + _exchange_shapes(ride_srcs, ride_modes),
        scratch_shapes=[pltpu.VMEM((HG_HEADS, HG_HEAD_DIM, HG_HEAD_DIM), F32)] + _exchange_sems(nr),
        compiler_params=_params(("arbitrary", "arbitrary"), VMEM_LIMIT_BIG),
    )(proj_h, proj_h, proj_h, proj_h, lb, hg_w, o, s_prev, drg, *ride_srcs)


def _attn_bwd(qr, kr, proj3, attn_o, dan, tables, sinks, attn_w, ride_srcs, ride_modes):
    B, T, _ = proj3.shape
    nb = T // WINDOW
    splits = min(ATT_SPLITS, nb)
    per = nb // splits
    nr = len(ride_srcs)
    cos, sinl, sinr = tables
    QKV = ATT_WIDTH + 2 * LANES

    def body(*refs):
        qr_ref, kr_ref, v_ref, o_ref, dan_ref, cos_ref, sl_ref, sr_ref, sink_ref, aw_ref = refs[:10]
        ride_in = refs[10:10 + nr]
        dqkv_ref, dsink_ref, daw_ref = refs[10 + nr:13 + nr]
        ride_out = refs[13 + nr:13 + 2 * nr]
        kpad, vpad, dkpad, dvpad, dqb, dsk = refs[13 + 2 * nr:19 + 2 * nr]
        sems = refs[19 + 2 * nr:]
        part = pl.program_id(1)
        step = pl.program_id(0) * splits + part
        _ride_start(ride_modes, step, B * splits, ride_in, ride_out, sems)

        @pl.when(part == 0)
        def _():
            kpad[0:WINDOW, :] = jnp.zeros((WINDOW, LANES), BF16)
            vpad[0:WINDOW, :] = jnp.zeros((WINDOW, LANES), BF16)
            kpad[WINDOW:, :] = kr_ref[...]
            vpad[WINDOW:, :] = _bf(v_ref[...])
            dkpad[...] = jnp.zeros(dkpad.shape, F32)
            dvpad[...] = jnp.zeros(dvpad.shape, F32)
            dsk[...] = jnp.zeros(dsk.shape, F32)
            daw_ref[...] = jnp.zeros(daw_ref.shape, F32)

        lower = _lower_mask()
        aw = aw_ref[...]

        def block(n, daw):
            r0 = pl.multiple_of(n * WINDOW, WINDOW)
            rows = pl.ds(r0, WINDOW)
            nxt = pl.ds(r0 + WINDOW, WINDOW)
            ob = o_ref[rows, :]
            dn = dan_ref[rows, :]
            ro = lax.rsqrt(_mean_last(ob * ob) + EPS)
            oh = ob * ro
            daw = daw + _sum_rows(dn * oh)
            doh = dn * aw
            do = _bf(ro * (doh - oh * _mean_last(doh * oh)))
            doparts = [do[:, j * LANES:(j + 1) * LANES] for j in range(ATT_WIDTH // LANES)]
            qparts = [qr_ref[rows, j * LANES:(j + 1) * LANES] for j in range(ATT_WIDTH // LANES)]
            for hk in range(ATT_KV_HEADS):
                lanes = slice(hk * ATT_HEAD_DIM, (hk + 1) * ATT_HEAD_DIM)
                qs = _stack_heads(qparts, hk)
                dos = _stack_heads(doparts, hk)
                k_cur, k_prev = kpad[nxt, lanes], kpad[rows, lanes]
                v_cur, v_prev = vpad[nxt, lanes], vpad[rows, lanes]
                p, inv, es = _softmax_window(qs, k_cur, k_prev, lower, n > 0, _sink_row(sink_ref, hk))
                p = p * inv
                dp = jnp.where(lower, _dot_nt(v_cur, dos), _dot_nt(v_prev, dos))
                delta = jnp.sum(p * dp, axis=0, keepdims=True)
                ds = p * (dp - delta)
                sk = (es * inv) * delta
                ds_cur = jnp.where(lower, ds, 0.0)
                p_cur = jnp.where(lower, p, 0.0)
                ds_cur, ds_prev = _bf(ds_cur), _bf(ds - ds_cur)
                p_cur, p_prev = _bf(p_cur), _bf(p - p_cur)
                dqt = (_dot_tn(k_cur, ds_cur) + _dot_tn(k_prev, ds_prev)) * ATT_SCALE
                dkpad[nxt, lanes] += _dot(ds_cur, qs)
                dkpad[rows, lanes] += _dot(ds_prev, qs)
                dvpad[nxt, lanes] += _dot(p_cur, dos)
                dvpad[rows, lanes] += _dot(p_prev, dos)
                for g in range(ATT_GROUP):
                    h = ATT_GROUP * hk + g
                    cols = slice(g * WINDOW, (g + 1) * WINDOW)
                    dqb[:, h * ATT_HEAD_DIM:(h + 1) * ATT_HEAD_DIM] = dqt[:, cols].T
                    head_lane = lax.broadcasted_iota(jnp.int32, dsk.shape, 1) == h
                    dsk[...] += jnp.where(head_lane, -jnp.sum(sk[:, cols], axis=1, keepdims=True), 0.0)
            cs, sl, sr = cos_ref[rows, :], sl_ref[rows, :], sr_ref[rows, :]
            for j in range(ATT_WIDTH // LANES):
                dqkv_ref[rows, j * LANES:(j + 1) * LANES] = _bf(_rope_t(dqb[:, j * LANES:(j + 1) * LANES], cs, sl, sr))
            return daw

        daw = _loop_pairs(part * per, per, block, jnp.zeros((1, ATT_WIDTH), F32))
        daw_ref[...] += jnp.broadcast_to(daw, (8, ATT_WIDTH))
        dsink_ref[...] = dsk[...]

        def finish(n, carry):
            r0 = pl.multiple_of(n * WINDOW, WINDOW)
            rows = pl.ds(r0, WINDOW)
            nxt = pl.ds(r0 + WINDOW, WINDOW)
            cs, sl, sr = cos_ref[rows, :], sl_ref[rows, :], sr_ref[rows, :]
            dqkv_ref[rows, ATT_WIDTH:ATT_WIDTH + LANES] = _bf(_rope_t(dkpad[nxt, :], cs, sl, sr))
            dqkv_ref[rows, ATT_WIDTH + LANES:QKV] = _bf(dvpad[nxt, :])
            return carry

        @pl.when(part == splits - 1)
        def _():
            lax.fori_loop(0, nb, finish, 0)

        _ride_wait(ride_modes, step, B * splits, ride_in, ride_out, sems)

    seq = lambda w, j: pl.BlockSpec((None, T, w), lambda b, s: (b, 0, j))
    full = lambda r, w: pl.BlockSpec((r, w), lambda b, s: (0, 0))
    return pl.pallas_call(
        body, name="attn_bwd", grid=(B, splits),
        in_specs=[seq(ATT_WIDTH, 0), seq(LANES, 0), seq(LANES, 5), seq(ATT_WIDTH, 0), seq(ATT_WIDTH, 0),
                  full(T, LANES), full(T, LANES), full(T, LANES),
                  pl.BlockSpec(memory_space=pltpu.SMEM), full(1, ATT_WIDTH)] + [ANY_SPEC] * nr,
        out_specs=[seq(QKV, 0), pl.BlockSpec((None, 8, LANES), lambda b, s: (b, 0, 0)),
                   pl.BlockSpec((None, 8, ATT_WIDTH), lambda b, s: (b, 0, 0))] + [ANY_SPEC] * nr,
        out_shape=[SDS((B, T, QKV), BF16), SDS((B, 8, LANES), F32), SDS((B, 8, ATT_WIDTH), F32)]
        + _exchange_shapes(ride_srcs, ride_modes),
        scratch_shapes=[pltpu.VMEM((T + WINDOW, LANES), BF16), pltpu.VMEM((T + WINDOW, LANES), BF16),
                        pltpu.VMEM((T + WINDOW, LANES), F32), pltpu.VMEM((T + WINDOW, LANES), F32),
                        pltpu.VMEM((WINDOW, ATT_WIDTH), F32), pltpu.VMEM((8, LANES), F32)] + _exchange_sems(nr),
        compiler_params=_params(("arbitrary", "arbitrary"), VMEM_LIMIT_BIG),
    )(qr, kr, proj3, attn_o, dan, cos, sinl, sinr, sinks, attn_w, *ride_srcs)


def _in_bwd(x2, dx1, dqkv, dhq, dhf, dhi, dhg, mod8, pre_w, w_in_bf, T, ride_srcs, ride_modes):
    N = x2.shape[0]
    TM = _tile_rows(T, big=True)
    tps = T // TM
    nr = len(ride_srcs)
    pieces = [(0, ATT_WIDTH + 2 * LANES), (768, HG_WIDTH), (1280, HG_WIDTH), (1792, HG_WIDTH), (2304, HG_WIDTH)]

    def body(*refs):
        x_ref, dx_ref, p0, p1, p2, p3, p4, mod_ref, pw_ref, w_ref = refs[:10]
        ride_in = refs[10:10 + nr]
        gx_ref, dproj_ref, acc_ref = refs[10 + nr:13 + nr]
        ride_out = refs[13 + nr:13 + 2 * nr]
        sems = refs[13 + 2 * nr:]
        _ride_start(ride_modes, pl.program_id(0), N // TM, ride_in, ride_out, sems)
        sc1 = mod_ref[1:2, :]
        dh = jnp.zeros((TM, D_MODEL), F32)
        for ref, (off, width) in zip((p0, p1, p2, p3, p4), pieces):
            pb = ref[...]
            dproj_ref[:, off:off + width] = pb
            dh = dh + _dot(pb, w_ref[off:off + width, :])
        x = x_ref[...]
        r = lax.rsqrt(_mean_last(x * x) + EPS)
        xh = x * r
        n1 = xh * pw_ref[...]
        dsh1 = _sum_rows(dh)
        dsc1 = _sum_rows(dh * n1)
        dn1 = dh * (1.0 + sc1)
        dw_pre = _sum_rows(dn1 * xh)
        dxh = dn1 * pw_ref[...]
        gx_ref[...] = dx_ref[...] + r * (dxh - xh * _mean_last(dxh * xh))
        _acc_rows(acc_ref, pl.program_id(0) % tps == 0, [dsh1, dsc1, dw_pre])
        _ride_wait(ride_modes, pl.program_id(0), N // TM, ride_in, ride_out, sems)

    row = lambda w: pl.BlockSpec((TM, w), lambda i: (i, 0))
    B = N // T
    return pl.pallas_call(
        body, name="in_bwd", grid=(N // TM,),
        in_specs=[row(D_MODEL), row(D_MODEL), row(768), row(HG_WIDTH), row(HG_WIDTH), row(HG_WIDTH),
                  row(HG_WIDTH), _mod_spec(tps), pl.BlockSpec((1, D_MODEL), lambda i: (0, 0)),
                  pl.BlockSpec((IN_COLS, D_MODEL), lambda i: (0, 0))] + [ANY_SPEC] * nr,
        out_specs=[row(D_MODEL), row(IN_COLS), _mod_spec(tps)] + [ANY_SPEC] * nr,
        out_shape=[SDS((N, D_MODEL), F32), SDS((N, IN_COLS), BF16), SDS((B, 8, D_MODEL), F32)]
        + _exchange_shapes(ride_srcs, ride_modes),
        scratch_shapes=_exchange_sems(nr),
        compiler_params=_params(("arbitrary",), VMEM_LIMIT_BIG),
    )(x2, dx1, dqkv, dhq, dhf, dhi, dhg, mod8, pre_w, w_in_bf, *ride_srcs)


def _matmul_tn(name, a, b, tn, tm):
    K, M = a.shape
    Nc = b.shape[1]

    def body(a_ref, b_ref, o_ref):
        o_ref[...] = _bf(_dot_tn(a_ref[...], b_ref[...]))

    return pl.pallas_call(
        body, name=name, grid=(M // tm, Nc // tn),
        in_specs=[pl.BlockSpec((K, tm), lambda i, j: (0, i)),
                  pl.BlockSpec((K, tn), lambda i, j: (0, j))],
        out_specs=pl.BlockSpec((tm, tn), lambda i, j: (i, j)), out_shape=SDS((M, Nc), BF16),
        compiler_params=_params(("arbitrary", "arbitrary"), VMEM_LIMIT_BIG),
    )(a, b)


def _matmul_tn_paired(name, a, b, stream_a):
    K = a.shape[0]
    stream, fixed = (a, b) if stream_a else (b, a)
    w = stream.shape[1] // N_DEV
    blk = (w, fixed.shape[1]) if stream_a else (fixed.shape[1], w)
    chips = N_DEV // 2

    def body(s_hbm, f_hbm, out_ref, s_buf, f_buf, g_buf, theirs, in_sems, send_sems, recv_sems):
        core = lax.axis_index("c")
        sib_dev, _ = _related(SIBLING)
        fixed_load = pltpu.make_async_copy(f_hbm, f_buf, in_sems.at[2])

        def load(j):
            owner = 2 * (j // 2) + (core if j % 2 else 1 - core)
            return pltpu.make_async_copy(s_hbm.at[:, pl.ds(pl.multiple_of(owner * w, LANES), w)], s_buf.at[j % 2],
                                         in_sems.at[j % 2])

        def swap(s):
            return pltpu.make_async_remote_copy(
                src_ref=g_buf.at[s, 0], dst_ref=theirs.at[s], send_sem=send_sems.at[s],
                recv_sem=recv_sems.at[s], device_id=sib_dev, device_id_type=MESH)

        fixed_load.start()
        load(0).start()
        fixed_load.wait()
        for j in range(N_DEV):
            s, mine = divmod(j, 2)
            load(j).wait()
            if j + 1 < N_DEV:
                load(j + 1).start()
            if stream_a:
                g_buf[s, mine] = _bf(_dot_tn(s_buf[j % 2], f_buf[...]))
            else:
                g_buf[s, mine] = _bf(_dot_tn(f_buf[...], s_buf[j % 2]))
            if mine:
                swap(s).wait_recv()
                out_ref[s] = _bf(g_buf[s, 1].astype(F32) + theirs[s].astype(F32))
            else:
                swap(s).start()
        for s in range(chips):
            swap(s).wait_send()

    return pl.pallas_call(
        body, name=name, in_specs=[ANY_SPEC] * 2, out_shape=SDS((chips,) + blk, BF16),
        scratch_shapes=[pltpu.VMEM((2, K, w), BF16), pltpu.VMEM(fixed.shape, BF16),
                        pltpu.VMEM((chips, 2) + blk, BF16), pltpu.VMEM((chips,) + blk, BF16),
                        pltpu.SemaphoreType.DMA((3,)), pltpu.SemaphoreType.DMA((chips,)),
                        pltpu.SemaphoreType.DMA((chips,))],
        compiler_params=_params(None, VMEM_LIMIT_BIG),
    )(stream, fixed)


GW_BLOCK = IN_COLS // N_DEV
GW_HALF = IN_COLS // 2


def _grad_w_in_reduced(dproj, h1, ride_srcs, ride_modes):
    K = dproj.shape[0]
    nr = len(ride_srcs)
    chips = N_DEV // 2
    tn = 512

    def body(*refs):
        a_hbm, b_hbm = refs[:2]
        ride_in, out, ride_out = refs[2:2 + nr], refs[2 + nr], refs[3 + nr:3 + 2 * nr]
        a_buf, b_buf, g_buf, theirs, p_buf, in_sems, pair_send, pair_recv, chip_send, chip_recv, own_sem = \
            refs[3 + 2 * nr:14 + 2 * nr]
        ride = _exchange_phases(ride_modes, ride_in, ride_out, *refs[14 + 2 * nr:]) if nr else ([], [], [])
        x, y, core = lax.axis_index("x"), lax.axis_index("y"), lax.axis_index("c")
        chip = 2 * x + y
        sib_dev, _ = _related(SIBLING)

        def remote(src, dst, send_sem, recv_sem, dev):
            return pltpu.make_async_remote_copy(src_ref=src, dst_ref=dst, send_sem=send_sem, recv_sem=recv_sem,
                                                device_id=dev, device_id_type=MESH)

        halves = [1 - x, x]
        loads = [pltpu.make_async_copy(b_hbm, b_buf, in_sems.at[0])]
        for t in range(2):
            col = pl.multiple_of(halves[t] * GW_HALF, LANES)
            loads.append(pltpu.make_async_copy(a_hbm.at[:, pl.ds(col, GW_HALF)], a_buf.at[t], in_sems.at[1 + t]))
        for cp in loads:
            cp.start()
        _run(ride[0])
        loads[0].wait()
        end = []
        for t in range(2):
            loads[1 + t].wait()
            if t == 1:
                _run(ride[1])
            for j in range(D_MODEL // tn):
                cols = pl.ds(j * tn, tn)
                res = _dot_tn(a_buf[t], b_buf[:, cols])
                for q in range(2):
                    for cc in range(2):
                        r0 = (2 * q + cc) * GW_BLOCK
                        g_buf[t, q, cc, :, cols] = _bf(res[r0:r0 + GW_BLOCK])
                swaps = [remote(g_buf.at[t, q, 1 - core, :, cols], theirs.at[t, q, :, cols],
                                pair_send.at[t, 2 * j + q], pair_recv.at[t, 2 * j + q], sib_dev) for q in range(2)]
                for cp in swaps:
                    cp.start()
                for cp in swaps:
                    cp.wait_recv()
                end += [cp.wait_send for cp in swaps]
                for q in range(2):
                    p_buf[t, q, :, cols] = _bf(g_buf[t, q, core, :, cols].astype(F32)
                                               + theirs[t, q, :, cols].astype(F32))
                for dy in range(2):
                    k = 4 * (1 - t) + 2 * dy
                    if k == 0:
                        own = pltpu.make_async_copy(p_buf.at[t, y, :, cols], out.at[chip, :, cols], own_sem.at[j])
                        own.start()
                        end.append(own.wait)
                        continue
                    dev, peer = _related(k)
                    sems = chip_send.at[k // 2, j], chip_recv.at[k // 2, j]
                    send = remote(p_buf.at[t, y ^ dy, :, cols], out.at[chip, :, cols], *sems, dev)
                    send.start()
                    end += [remote(p_buf.at[t, y ^ dy, :, cols], out.at[peer // 2, :, cols], *sems, dev).wait_recv,
                            send.wait_send]
        _run(ride[2])
        _run(end)

    return pl.pallas_call(
        body, name="grad_w_in",
        in_specs=[ANY_SPEC] * (2 + nr), out_specs=[ANY_SPEC] * (1 + nr),
        out_shape=[SDS((chips, GW_BLOCK, D_MODEL), BF16)] + _exchange_shapes(ride_srcs, ride_modes),
        scratch_shapes=[pltpu.VMEM((2, K, GW_HALF), BF16), pltpu.VMEM((K, D_MODEL), BF16),
                        pltpu.VMEM((2, 2, 2, GW_BLOCK, D_MODEL), BF16), pltpu.VMEM((2, 2, GW_BLOCK, D_MODEL), BF16),
                        pltpu.VMEM((2, 2, GW_BLOCK, D_MODEL), BF16), pltpu.SemaphoreType.DMA((3,)),
                        pltpu.SemaphoreType.DMA((2, 4)), pltpu.SemaphoreType.DMA((2, 4)),
                        pltpu.SemaphoreType.DMA((chips, 2)), pltpu.SemaphoreType.DMA((chips, 2)),
                        pltpu.SemaphoreType.DMA((2,))] + _exchange_sems(nr),
        compiler_params=_params(None, VMEM_LIMIT_BIG),
    )(dproj, h1, *ride_srcs)


def _adamw_math(w, g, m, v):
    m2 = ADAM_B1 * m + (1.0 - ADAM_B1) * g
    v2 = ADAM_B2 * v + (1.0 - ADAM_B2) * (g * g)
    m_hat = m2 / (1.0 - ADAM_B1 ** ADAM_STEP)
    v_hat = v2 / (1.0 - ADAM_B2 ** ADAM_STEP)
    delta = -ADAM_LR * (m_hat / (jnp.sqrt(v_hat) + ADAM_EPS) + ADAM_WD * w)
    return delta, m2, v2


def _reduce_adamw(name, parts, w, m, v):
    r, c = w.shape
    tr = r if r % 256 else 256
    slots = parts.shape[0]

    def body(p_ref, w_ref, m_ref, v_ref, g_ref, d_ref, m2_ref, v2_ref):
        g = p_ref[0].astype(F32)
        for s in range(1, slots):
            g = g + p_ref[s].astype(F32)
        g_ref[...] = g
        d_ref[...], m2_ref[...], v2_ref[...] = _adamw_math(w_ref[...], g, m_ref[...], v_ref[...])

    blk = pl.BlockSpec((tr, c), lambda i: (i, 0))
    return pl.pallas_call(
        body, name=name, grid=(r // tr,),
        in_specs=[pl.BlockSpec((slots, tr, c), lambda i: (0, i, 0)), blk, blk, blk],
        out_specs=[blk] * 4, out_shape=[SDS((r, c), F32)] * 4,
        compiler_params=_params(("arbitrary",), VMEM_LIMIT_BIG),
    )(parts, w, m, v)


def _ada_grad_adamw(c_all, dmod_all, w, m, v):
    r, c = w.shape
    tr = 256
    nb = c_all.shape[0]

    def body(c_ref, dm_ref, w_ref, m_ref, v_ref, g_ref, d_ref, m2_ref, v2_ref):
        cv = c_ref[...]
        g = _dot_tn(cv * _sigmoid(cv), dm_ref[...])
        g_ref[...] = g
        d_ref[...], m2_ref[...], v2_ref[...] = _adamw_math(w_ref[...], g, m_ref[...], v_ref[...])

    blk = pl.BlockSpec((tr, c), lambda i: (i, 0))
    return pl.pallas_call(
        body, name="ada_grad_adamw", grid=(r // tr,),
        in_specs=[pl.BlockSpec((nb, tr), lambda i: (0, i)), pl.BlockSpec((nb, c), lambda i: (0, 0)),
                  blk, blk, blk],
        out_specs=[blk] * 4, out_shape=[SDS((r, c), F32)] * 4,
        compiler_params=_params(("arbitrary",)),
    )(c_all, dmod_all, w, m, v)


_SMALL = [("b_ada", 6144), ("pre_w_mix", 1024), ("attn_sinks", 128), ("attn_out_w", 512), ("lb_table", 1024),
          ("hg_norm_w", 128), ("post_w_mix", 1024), ("pre_w_mlp", 1024), ("post_w_mlp", 1024)]


def _pack_small(acc_in, acc_mix, acc_mlp, dsink, daw, dlb, dgw, lb_p, ada_cols):
    B = acc_in.shape[0]
    width = sum(w for _, w in _SMALL) + LANES

    def body(ain, amix, amlp, dsk_ref, daw_ref, dlb_ref, dgw_ref, lbp_ref, packed_ref, dmod_ref):
        def total(ref, r, w=None):
            out = ref[0, r:r + 1, :] if w is None else ref[0, r:r + 1, :w]
            for b in range(1, B):
                out = out + (ref[b, r:r + 1, :] if w is None else ref[b, r:r + 1, :w])
            return out

        d_b_ada = None
        for b in range(B):
            mods = [ain[b, 0:1, :], ain[b, 1:2, :], amix[b, 0:1, :], amlp[b, 0:1, :], amlp[b, 1:2, :], amlp[b, 2:3, :]]
            full = jnp.concatenate(mods, axis=1)
            for j in range(N_DEV):
                dmod_ref[j, b:b + 1, :] = full[:, j * ada_cols:(j + 1) * ada_cols]
            d_b_ada = full if d_b_ada is None else d_b_ada + full
        d_lb = total(dlb_ref, 0)
        pp = lbp_ref[0:1, :] * lbp_ref[1:2, :]
        pieces = [d_b_ada, total(ain, 2), total(dsk_ref, 0), total(daw_ref, 0), -d_lb * pp, d_lb * pp,
                  total(dgw_ref, 0), total(amix, 1), total(amlp, 3), total(amlp, 4), total(amlp, 5, LANES)]
        off = 0
        for piece in pieces:
            packed_ref[:, off:off + piece.shape[1]] = piece
            off += piece.shape[1]

    return pl.pallas_call(
        body, name="pack_small",
        out_shape=[SDS((1, width), F32), SDS((N_DEV, B, ada_cols), F32)],
    )(acc_in, acc_mix, acc_mlp, dsink, daw, dlb, dgw, lb_p)


def _adamw_small(parts, given):
    names = [n for n, _ in _SMALL]
    flat_in = [a for n in names for a in given[n]]

    def body(*refs):
        p_ref = refs[0]
        in_refs = refs[1:1 + 3 * len(names)]
        out_refs = refs[1 + 3 * len(names):-1]
        loss_ref = refs[-1]
        g = p_ref[0]
        for s in range(1, N_DEV):
            g = g + p_ref[s]
        off = 0
        for i, (name, width) in enumerate(_SMALL):
            w_ref, m_ref, v_ref = in_refs[3 * i:3 * i + 3]
            rows, cols = w_ref.shape
            for r in range(rows):
                gr = g[:, off + r * cols:off + (r + 1) * cols]
                res = (gr,) + _adamw_math(w_ref[r:r + 1, :], gr, m_ref[r:r + 1, :], v_ref[r:r + 1, :])
                for o_ref, val in zip(out_refs[4 * i:4 * i + 4], res):
                    o_ref[r:r + 1, :] = val
            off += width
        loss_ref[...] = g[:, off:off + LANES]

    out_shape = [SDS(given[n][0].shape, F32) for n in names for _ in range(4)] + [SDS((1, LANES), F32)]
    outs = pl.pallas_call(body, name="adamw_small", out_shape=out_shape)(parts, *flat_in)
    return {n: tuple(outs[4 * i:4 * i + 4]) for i, n in enumerate(names)}, outs[-1][0, 0]


def kernel(x, c, w_ada, b_ada, pre_w_mix, w_in, attn_sinks, attn_out_w, lb_table, hg_norm_w, w_out, post_w_mix, pre_w_mlp, w_up, w_down, post_w_mlp, loss_target, m_w_ada, m_b_ada, m_pre_w_mix, m_w_in, m_attn_sinks, m_attn_out_w, m_lb_table, m_hg_norm_w, m_w_out, m_post_w_mix, m_pre_w_mlp, m_w_up, m_w_down, m_post_w_mlp, v_w_ada, v_b_ada, v_pre_w_mix, v_w_in, v_attn_sinks, v_attn_out_w, v_lb_table, v_hg_norm_w, v_w_out, v_post_w_mix, v_pre_w_mlp, v_w_up, v_w_down, v_post_w_mlp):
    B, T, _ = x.shape
    N = B * T
    me = 4 * lax.axis_index("x") + 2 * lax.axis_index("y") + lax.axis_index("c")
    x2 = x.reshape(N, D_MODEL)
    tgt2 = loss_target.reshape(N, D_MODEL)

    w_in_t, m_w_in_t, v_w_in_t = w_in[0].T, m_w_in[0].T, v_w_in[0].T
    w_in_g, c_g = _exchange("gather_w_in", [_bf(w_in_t), c], ["gather"] * 2)
    w_in_f = w_in_g.reshape(IN_COLS, D_MODEL)
    c_all = c_g.reshape(N_DEV * B, D_MODEL)

    ada_cols = w_ada.shape[2]
    b_mine = lax.dynamic_slice(b_ada, (0, me * ada_cols), (1, ada_cols))
    mod_cols = _ada_mod(c_all, w_ada[0], b_mine)
    (mod_g,) = _exchange("scatter_mod", [mod_cols.reshape(N_DEV, B, ada_cols)], ["a2a"])
    mod = mod_g.transpose(1, 0, 2).reshape(B, 6, D_MODEL)
    mod8 = jnp.pad(mod, ((0, 0), (0, 2), (0, 0)))

    lb_p = jax.nn.softmax(lb_table, axis=0)
    lb = lb_p[1:2]
    tables = _rope_tables(T)

    w_up_b, w_down_b = _bf(w_up[0]), _bf(w_down[0])
    wd_split = 5 * LANES
    proj_a, proj_h, h1, w_up_g0 = _in_proj(x2, mod8, pre_w_mix, w_in_f, T, [w_up_b[:MLP_HALF]], ["gather"])
    proj3 = proj_a.reshape(B, T, ATT_COLS)
    proj_h = proj_h.reshape(B, T, IN_COLS - ATT_COLS)
    rec_o, rec_g, s_prev, w_up_g1, w_out_g = _hgrn_fwd(proj_h, lb, hg_norm_w, [w_up_b[MLP_HALF:], _bf(w_out[0])],
                                                       ["gather"] * 2)
    attn_o, attn_n, qr, kr, w_down_g0 = _attn_fwd(proj3, tables, attn_sinks, attn_out_w,
                                                  [w_down_b[:, :wd_split]], ["gather"])
    w_out_f = w_out_g.reshape(D_MODEL, D_MODEL)
    mix, x1, cat, w_down_g1 = _mix_out(x2, attn_n.reshape(N, ATT_WIDTH), rec_g.reshape(N, HG_WIDTH), mod8,
                                       post_w_mix, w_out_f, T, [w_down_b[:, wd_split:]], ["gather"])
    w_up_halves = [w_up_g0, w_up_g1]
    w_down_halves = [w_down_g0.reshape(D_FF, wd_split), w_down_g1.reshape(D_FF, D_MODEL - wd_split)]
    up, u, d, h2 = _mlp_fwd(x1, mod8, pre_w_mlp, w_up_halves, w_down_halves, T)

    dx1, dup, dd, acc_mlp = _mlp_bwd(x1, d, up, tgt2, mod8, pre_w_mlp, post_w_mlp, w_up_halves, w_down_halves, T)
    p_up = _matmul_tn_paired("grad_w_up", h2, dup, stream_a=False)
    p_down = _matmul_tn_paired("grad_w_down", u, dd, stream_a=True)
    dan, drg, dmix, acc_mix = _mix_bwd(mix, dx1, mod8, post_w_mix, w_out_f, T, [], [])
    gw_out = _matmul_tn("grad_w_out", cat, dmix, 512, tm=D_MODEL).reshape(N_DEV, D_MODEL // N_DEV, D_MODEL)
    dhq, dhf, dhi, dhg, dlb_p, dgw_p, r_down, r_out = _hgrn_bwd(
        proj_h, lb, hg_norm_w, rec_o, s_prev, drg.reshape(B, T, HG_WIDTH), [p_down, gw_out], ["chips", "a2a"])
    dqkv, dsink_p, daw_p, r_up = _attn_bwd(qr, kr, proj3, attn_o, dan.reshape(B, T, ATT_WIDTH), tables,
                                           attn_sinks, attn_out_w, [p_up], ["chips"])
    flat = lambda a: a.reshape(N, a.shape[-1])
    grad_x, dproj, acc_in = _in_bwd(x2, dx1, flat(dqkv), flat(dhq), flat(dhf), flat(dhi), flat(dhg),
                                    mod8, pre_w_mix, w_in_f, T, [], [])

    packed, dmod_blocks = _pack_small(acc_in, acc_mix, acc_mlp, dsink_p, daw_p, dlb_p, dgw_p, lb_p, ada_cols)
    r_in, r_dmod, r_small = _grad_w_in_reduced(dproj, h1, [dmod_blocks, packed], ["a2a", "gather"])

    res = {}
    res["w_in"] = tuple(a.T for a in _reduce_adamw("adamw_w_in", r_in, w_in_t, m_w_in_t, v_w_in_t))
    res["w_out"] = _reduce_adamw("adamw_w_out", r_out, w_out[0], m_w_out[0], v_w_out[0])
    res["w_up"] = _reduce_adamw("adamw_w_up", r_up, w_up[0], m_w_up[0], v_w_up[0])
    res["w_down"] = _reduce_adamw("adamw_w_down", r_down, w_down[0], m_w_down[0], v_w_down[0])
    res["w_ada"] = _ada_grad_adamw(c_all, r_dmod.reshape(N_DEV * B, ada_cols), w_ada[0], m_w_ada[0], v_w_ada[0])

    given = dict(b_ada=(b_ada, m_b_ada, v_b_ada), pre_w_mix=(pre_w_mix, m_pre_w_mix, v_pre_w_mix),
                 attn_sinks=(attn_sinks, m_attn_sinks, v_attn_sinks),
                 attn_out_w=(attn_out_w, m_attn_out_w, v_attn_out_w), lb_table=(lb_table, m_lb_table, v_lb_table),
                 hg_norm_w=(hg_norm_w, m_hg_norm_w, v_hg_norm_w), post_w_mix=(post_w_mix, m_post_w_mix, v_post_w_mix),
                 pre_w_mlp=(pre_w_mlp, m_pre_w_mlp, v_pre_w_mlp), post_w_mlp=(post_w_mlp, m_post_w_mlp, v_post_w_mlp))
    small_res, loss = _adamw_small(r_small, given)
    res.update(small_res)

    order = ["w_ada", "b_ada", "pre_w_mix", "w_in", "attn_sinks", "attn_out_w", "lb_table", "hg_norm_w", "w_out",
             "post_w_mix", "pre_w_mlp", "w_up", "w_down", "post_w_mlp"]
    big = {"w_ada", "w_in", "w_out", "w_up", "w_down"}
    outs = [loss, grad_x.reshape(B, T, D_MODEL)]
    for i in range(4):
        for k in order:
            a = res[k][i]
            outs.append(a[None] if k in big else a)
    return tuple(outs)
```

```python
import jax
import jax.numpy as jnp
import numpy as np
from jax import lax
from jax.experimental import pallas as pl
from jax.experimental.pallas import tpu as pltpu

F32 = jnp.float32
BF16 = jnp.bfloat16
SDS = jax.ShapeDtypeStruct

D_MODEL = 1024
ATT_WIDTH = 512
ATT_HEAD_DIM = 64
ATT_KV_HEADS = 2
ATT_GROUP = 4
WINDOW = 128
ROPE_DIM = 16
ROPE_THETA = 500000.0
HG_WIDTH = 512
HG_HEAD_DIM = 128
HG_HEADS = 4
HG_CHUNK = 32
IN_COLS = 2816
ATT_COLS = 768
D_FF = 4096
EPS = 1e-6
N_DEV = 8

ADAM_LR = 0.001
ADAM_B1 = 0.9
ADAM_B2 = 0.999
ADAM_EPS = 1e-08
ADAM_WD = 0.01
ADAM_STEP = 10

VMEM_LIMIT_BIG = 56 << 20
LANES = 128

MESH = pl.DeviceIdType.MESH
NT_DIMS = (((1,), (1,)), ((), ()))
TN_DIMS = (((0,), (0,)), ((), ()))


def _dot(a, b):
    return jnp.dot(a, b, preferred_element_type=F32)


def _dot_nt(a, b):
    return lax.dot_general(a, b, NT_DIMS, preferred_element_type=F32)


def _dot_tn(a, b):
    return lax.dot_general(a, b, TN_DIMS, preferred_element_type=F32)


def _bf(a):
    return a.astype(BF16)


def _sigmoid(a):
    return 0.5 * jnp.tanh(0.5 * a) + 0.5


def _mean_last(a):
    return jnp.mean(a, axis=-1, keepdims=True)


def _sum_rows(a):
    return jnp.sum(a, axis=0, keepdims=True)


def _loop_pairs(first, count, body, init, per_trip=2):
    if count % per_trip:
        return lax.fori_loop(first, first + count, body, init)

    def trip(i, c):
        for k in range(per_trip):
            c = body(first + per_trip * i + k, c)
        return c

    return lax.fori_loop(0, count // per_trip, trip, init)


def _params(sem=None, vmem=None):
    kw = {}
    if sem is not None:
        kw["dimension_semantics"] = sem
    if vmem is not None:
        kw["vmem_limit_bytes"] = vmem
    return pltpu.CompilerParams(**kw)


ANY_SPEC = pl.BlockSpec(memory_space=pl.ANY)


def _exchange_shapes(srcs, modes):
    out_shape = []
    for s, m in zip(srcs, modes):
        shp = (N_DEV,) + tuple(s.shape) if m == "gather" else tuple(s.shape)
        out_shape.append(SDS(shp, s.dtype))
    return out_shape


def _exchange_sems(n):
    if n == 0:
        return []
    return [pltpu.SemaphoreType.DMA((n, N_DEV - 1)), pltpu.SemaphoreType.DMA((n, N_DEV - 1)),
            pltpu.SemaphoreType.DMA((n,))]


SIBLING = 1
OTHER_CHIPS = (2, 4, 6)


def _related(k):
    x, y, c = lax.axis_index("x"), lax.axis_index("y"), lax.axis_index("c")
    px, py, pc = x ^ ((k >> 2) & 1), y ^ ((k >> 1) & 1), c ^ (k & 1)
    return (px, py, pc), 4 * px + 2 * py + pc


def _exchange_phases(modes, src_refs, out_refs, send_sems, recv_sems, own_sems):
    _, me = _related(0)
    sib_dev, sib = _related(SIBLING)
    start, middle, end = [], [], []

    def remote(a, i, src, dst, dev):
        return pltpu.make_async_remote_copy(src_ref=src, dst_ref=dst, send_sem=send_sems.at[a, i],
                                            recv_sem=recv_sems.at[a, i], device_id=dev, device_id_type=MESH)

    for a, mode in enumerate(modes):
        out = out_refs[a]
        if mode == "gather":
            src = src_refs[a]
            own = pltpu.make_async_copy(src, out.at[me], own_sems.at[a])
            to_sib = remote(a, 0, src, out.at[me], sib_dev)
            start += [own.start, to_sib.start]
            end += [remote(a, 0, src, out.at[sib], sib_dev).wait_recv, to_sib.wait_send, own.wait]
            for j, k in enumerate(OTHER_CHIPS, start=1):
                dev, peer = _related(k)
                _, peer_sib = _related(k ^ SIBLING)
                send = remote(a, j, src, out.at[me], dev)
                passed = remote(a, 3 + j, out.at[peer], out.at[peer], sib_dev)
                start.append(send.start)
                middle += [remote(a, j, src, out.at[peer], dev).wait_recv, passed.start]
                end += [remote(a, 3 + j, out.at[peer_sib], out.at[peer_sib], sib_dev).wait_recv,
                        send.wait_send, passed.wait_send]
        elif mode == "chips":
            chip = me // 2
            own = pltpu.make_async_copy(src_refs[a].at[chip], out.at[chip], own_sems.at[a])
            start.append(own.start)
            end.append(own.wait)
            for j, k in enumerate(OTHER_CHIPS, start=1):
                dev, peer = _related(k)
                send = remote(a, j, src_refs[a].at[peer // 2], out.at[chip], dev)
                start.append(send.start)
                end += [remote(a, j, src_refs[a].at[peer // 2], out.at[peer // 2], dev).wait_recv, send.wait_send]
        else:
            own = pltpu.make_async_copy(src_refs[a].at[me], out.at[me], own_sems.at[a])
            start.append(own.start)
            end.append(own.wait)
            for k in range(1, N_DEV):
                dev, peer = _related(k)
                send = remote(a, k - 1, src_refs[a].at[peer], out.at[me], dev)
                start.append(send.start)
                end += [remote(a, k - 1, src_refs[a].at[peer], out.at[peer], dev).wait_recv, send.wait_send]
    return start, middle, end


def _run(actions):
    for act in actions:
        act()


def _exchange(name, srcs, modes):
    n = len(srcs)

    def body(*refs):
        start, middle, end = _exchange_phases(modes, refs[:n], refs[n:2 * n], *refs[2 * n:])
        _run(start)
        _run(middle)
        _run(end)

    return pl.pallas_call(
        body, name=name, out_shape=_exchange_shapes(srcs, modes),
        in_specs=[ANY_SPEC] * n, out_specs=[ANY_SPEC] * n,
        scratch_shapes=_exchange_sems(n),
    )(*srcs)


def _ride_start(modes, step, steps, src_refs, out_refs, sems):
    if not modes:
        return
    middle_step = steps - 1

    @pl.when(step == 0)
    def _():
        _run(_exchange_phases(modes, src_refs, out_refs, *sems)[0])

    if "gather" in modes:
        @pl.when(step == middle_step)
        def _():
            _run(_exchange_phases(modes, src_refs, out_refs, *sems)[1])


def _ride_wait(modes, step, steps, src_refs, out_refs, sems):
    if not modes:
        return

    @pl.when(step == steps - 1)
    def _():
        _run(_exchange_phases(modes, src_refs, out_refs, *sems)[2])


def _ada_mod(c_all, w_ada, b_ada_mine):
    nb, cols = c_all.shape[0], w_ada.shape[1]

    def body(c_ref, w_ref, b_ref, o_ref):
        cv = c_ref[...]
        ca = cv * _sigmoid(cv)
        o_ref[...] = _dot(ca, w_ref[...]) + b_ref[...]

    return pl.pallas_call(body, name="ada_mod", out_shape=SDS((nb, cols), F32))(c_all, w_ada, b_ada_mine)


def _tile_rows(T, big=False):
    return min(512 if big else 256, T)


def _mod_spec(tps):
    return pl.BlockSpec((None, 8, D_MODEL), lambda i: (i // tps, 0, 0))


def _in_proj(x2, mod8, pre_w, w_in_bf, T, ride_srcs, ride_modes):
    N = x2.shape[0]
    TM = _tile_rows(T, big=True)
    tps = T // TM
    nr = len(ride_srcs)

    def body(*refs):
        x_ref, mod_ref, pw_ref, w_ref = refs[:4]
        ride_in = refs[4:4 + nr]
        pa_ref, ph_ref, h1_ref = refs[4 + nr:7 + nr]
        ride_out = refs[7 + nr:7 + 2 * nr]
        sems = refs[7 + 2 * nr:]
        _ride_start(ride_modes, pl.program_id(0), N // TM, ride_in, ride_out, sems)
        x = x_ref[...]
        r = lax.rsqrt(_mean_last(x * x) + EPS)
        h = (x * r * pw_ref[...]) * (1.0 + mod_ref[1:2, :]) + mod_ref[0:1, :]
        hb = _bf(h)
        h1_ref[...] = hb
        pa_ref[...] = _dot_nt(hb, w_ref[:ATT_COLS, :])
        ph_ref[...] = _dot_nt(hb, w_ref[ATT_COLS:, :])
        _ride_wait(ride_modes, pl.program_id(0), N // TM, ride_in, ride_out, sems)

    return pl.pallas_call(
        body, name="in_proj", grid=(N // TM,),
        in_specs=[pl.BlockSpec((TM, D_MODEL), lambda i: (i, 0)), _mod_spec(tps),
                  pl.BlockSpec((1, D_MODEL), lambda i: (0, 0)),
                  pl.BlockSpec((IN_COLS, D_MODEL), lambda i: (0, 0))] + [ANY_SPEC] * nr,
        out_specs=[pl.BlockSpec((TM, ATT_COLS), lambda i: (i, 0)),
                   pl.BlockSpec((TM, IN_COLS - ATT_COLS), lambda i: (i, 0)),
                   pl.BlockSpec((TM, D_MODEL), lambda i: (i, 0))] + [ANY_SPEC] * nr,
        out_shape=[SDS((N, ATT_COLS), F32), SDS((N, IN_COLS - ATT_COLS), F32), SDS((N, D_MODEL), BF16)]
        + _exchange_shapes(ride_srcs, ride_modes),
        scratch_shapes=_exchange_sems(nr),
        compiler_params=_params(("arbitrary",), VMEM_LIMIT_BIG),
    )(x2, mod8, pre_w, w_in_bf, *ride_srcs)


def _rope_tables(T):
    half = ROPE_DIM // 2
    f32 = np.float32
    inv_freq = (f32(ROPE_THETA) ** (-np.arange(0, ROPE_DIM, 2, dtype=f32) / f32(ROPE_DIM))).astype(f32)
    ang = np.arange(T, dtype=f32)[:, None] * inv_freq[None, :]
    cos, sin = np.cos(ang).astype(f32), np.sin(ang).astype(f32)
    ones = np.ones((T, ATT_HEAD_DIM - ROPE_DIM), f32)
    zeros = np.zeros((T, ATT_HEAD_DIM - ROPE_DIM), f32)
    zh = np.zeros((T, half), f32)
    cos64 = np.concatenate([cos, cos, ones], axis=1)
    sin_left = np.concatenate([-sin, zh, zeros], axis=1)
    sin_right = np.concatenate([zh, sin, zeros], axis=1)
    rep = LANES // ATT_HEAD_DIM
    return tuple(jnp.asarray(np.tile(t, (1, rep))) for t in (cos64, sin_left, sin_right))


def _rope(xc, cs, sl, sr):
    return xc * cs + pltpu.roll(xc, LANES - 8, 1) * sl + pltpu.roll(xc, 8, 1) * sr


def _rope_t(dy, cs, sl, sr):
    return dy * cs + pltpu.roll(dy * sl, 8, 1) + pltpu.roll(dy * sr, LANES - 8, 1)


ATT_SCALE = ATT_HEAD_DIM ** -0.5
ATT_SPLITS = 4


def _lower_mask():
    j = lax.broadcasted_iota(jnp.int32, (WINDOW, ATT_GROUP * WINDOW), 0)
    i = lax.broadcasted_iota(jnp.int32, (WINDOW, ATT_GROUP * WINDOW), 1) & (WINDOW - 1)
    return j <= i


def _sink_row(sink_ref, hk):
    return jnp.concatenate(
        [jnp.full((1, WINDOW), sink_ref[0, ATT_GROUP * hk + g], F32) for g in range(ATT_GROUP)], axis=1)


def _softmax_window(qs, k_cur, k_prev, lower, has_prev, sink):
    s_prev = jnp.where(has_prev, _dot_nt(k_prev, qs), jnp.finfo(F32).min)
    s = jnp.where(lower, _dot_nt(k_cur, qs), s_prev)
    m = jnp.maximum(jnp.max(s, axis=0, keepdims=True), sink)
    p = jnp.exp(s - m)
    es = jnp.exp(sink - m)
    inv = 1.0 / (jnp.sum(p, axis=0, keepdims=True) + es)
    return p, inv, es


def _stack_heads(parts, hk):
    hs = []
    for g in range(ATT_GROUP):
        h = ATT_GROUP * hk + g
        hs.append(parts[h // 2][:, (h % 2) * ATT_HEAD_DIM:(h % 2 + 1) * ATT_HEAD_DIM])
    return jnp.concatenate(hs, axis=0)


def _attn_fwd(proj3, tables, sinks, attn_w, ride_srcs, ride_modes):
    B, T, _ = proj3.shape
    nb = T // WINDOW
    splits = min(ATT_SPLITS, nb)
    per = nb // splits
    nr = len(ride_srcs)
    cos, sinl, sinr = tables

    def body(*refs):
        q_ref, k_ref, v_ref, cos_ref, sl_ref, sr_ref, sink_ref, aw_ref = refs[:8]
        ride_in = refs[8:8 + nr]
        o_ref, an_ref, qr_ref, kr_ref = refs[8 + nr:12 + nr]
        ride_out = refs[12 + nr:12 + 2 * nr]
        kpad, vpad = refs[12 + 2 * nr:14 + 2 * nr]
        sems = refs[14 + 2 * nr:]
        part = pl.program_id(1)
        step = pl.program_id(0) * splits + part
        _ride_start(ride_modes, step, B * splits, ride_in, ride_out, sems)

        @pl.when(part == 0)
        def _():
            kpad[0:WINDOW, :] = jnp.zeros((WINDOW, LANES), BF16)
            vpad[0:WINDOW, :] = jnp.zeros((WINDOW, LANES), BF16)

        lower = _lower_mask()

        def block(n, carry):
            r0 = pl.multiple_of(n * WINDOW, WINDOW)
            rows = pl.ds(r0, WINDOW)
            nxt = pl.ds(r0 + WINDOW, WINDOW)
            cs, sl, sr = cos_ref[rows, :], sl_ref[rows, :], sr_ref[rows, :]
            kb = _bf(_rope(k_ref[rows, :], cs, sl, sr))
            vb = _bf(v_ref[rows, :])
            kpad[nxt, :] = kb
            kr_ref[rows, :] = kb
            vpad[nxt, :] = vb
            qparts = []
            for j in range(ATT_WIDTH // LANES):
                qp = _bf(_rope(q_ref[rows, j * LANES:(j + 1) * LANES], cs, sl, sr) * ATT_SCALE)
                qr_ref[rows, j * LANES:(j + 1) * LANES] = qp
                qparts.append(qp)
            for hk in range(ATT_KV_HEADS):
                lanes = slice(hk * ATT_HEAD_DIM, (hk + 1) * ATT_HEAD_DIM)
                qs = _stack_heads(qparts, hk)
                p, inv, _ = _softmax_window(qs, kb[:, lanes], kpad[rows, lanes], lower, n > 0,
                                            _sink_row(sink_ref, hk))
                p_cur = jnp.where(lower, p, 0.0)
                ot = (_dot_tn(vb[:, lanes], _bf(p_cur)) + _dot_tn(vpad[rows, lanes], _bf(p - p_cur))) * inv
                for g in range(ATT_GROUP):
                    h = ATT_GROUP * hk + g
                    o_ref[rows, h * ATT_HEAD_DIM:(h + 1) * ATT_HEAD_DIM] = ot[:, g * WINDOW:(g + 1) * WINDOW].T
            ob = o_ref[rows, :]
            an_ref[rows, :] = _bf(ob * lax.rsqrt(_mean_last(ob * ob) + EPS) * aw_ref[...])
            return carry

        _loop_pairs(part * per, per, block, 0)
        _ride_wait(ride_modes, step, B * splits, ride_in, ride_out, sems)

    seq = lambda w, j: pl.BlockSpec((None, T, w), lambda b, s: (b, 0, j))
    full = lambda r, w: pl.BlockSpec((r, w), lambda b, s: (0, 0))
    return pl.pallas_call(
        body, name="attn_fwd", grid=(B, splits),
        in_specs=[seq(ATT_WIDTH, 0), seq(LANES, 4), seq(LANES, 5),
                  full(T, LANES), full(T, LANES), full(T, LANES),
                  pl.BlockSpec(memory_space=pltpu.SMEM), full(1, ATT_WIDTH)] + [ANY_SPEC] * nr,
        out_specs=[seq(ATT_WIDTH, 0), seq(ATT_WIDTH, 0), seq(ATT_WIDTH, 0), seq(LANES, 0)] + [ANY_SPEC] * nr,
        out_shape=[SDS((B, T, ATT_WIDTH), F32), SDS((B, T, ATT_WIDTH), BF16),
                   SDS((B, T, ATT_WIDTH), BF16), SDS((B, T, LANES), BF16)] + _exchange_shapes(ride_srcs, ride_modes),
        scratch_shapes=[pltpu.VMEM((T + WINDOW, LANES), BF16), pltpu.VMEM((T + WINDOW, LANES), BF16)]
        + _exchange_sems(nr),
        compiler_params=_params(("arbitrary", "arbitrary"), VMEM_LIMIT_BIG),
    )(proj3, proj3, proj3, cos, sinl, sinr, sinks, attn_w, *ride_srcs)


HG_GROUP = 8
HG_ROWS = HG_GROUP * HG_CHUNK


HG_STACK = HG_GROUP * HG_HEAD_DIM


def _group_mask():
    r = lax.broadcasted_iota(jnp.int32, (HG_ROWS, HG_ROWS), 0)
    c = lax.broadcasted_iota(jnp.int32, (HG_ROWS, HG_ROWS), 1)
    return ((r // HG_CHUNK) == (c // HG_CHUNK)) & (r >= c)


def _spread(a):
    blocks = []
    for c in range(HG_GROUP):
        above = jnp.zeros((c * HG_CHUNK, HG_HEAD_DIM), a.dtype)
        below = jnp.zeros(((HG_GROUP - 1 - c) * HG_CHUNK, HG_HEAD_DIM), a.dtype)
        blocks.append(jnp.concatenate([p for p in (above, a[_chunk_rows(c), :], below) if p.shape[0]], axis=0))
    return jnp.concatenate(blocks, axis=1)


def _pick(r):
    return jnp.concatenate([r[_chunk_rows(c), c * HG_HEAD_DIM:(c + 1) * HG_HEAD_DIM] for c in range(HG_GROUP)], axis=0)


def _lane_block(a, c):
    return a[:, c * HG_HEAD_DIM:(c + 1) * HG_HEAD_DIM]


def _chunk_cumsum(a, reverse=False):
    n = a.shape[0]
    pos = lax.broadcasted_iota(jnp.int32, a.shape, 0) % HG_CHUNK
    shift = 1
    while shift < HG_CHUNK:
        if reverse:
            a = a + jnp.where(pos < HG_CHUNK - shift, pltpu.roll(a, n - shift, 0), 0.0)
        else:
            a = a + jnp.where(pos >= shift, pltpu.roll(a, shift, 0), 0.0)
        shift *= 2
    return a


def _chunk_bcast(rows_1x128):
    return jnp.concatenate([jnp.broadcast_to(r, (HG_CHUNK, HG_HEAD_DIM)) for r in rows_1x128], axis=0)


def _hgrn_gates(hq, hf, lb):
    sq = _sigmoid(hq)
    q = hq * sq
    sg = _sigmoid(hf)
    f = lb + (1.0 - lb) * sg
    k = 1.0 - f
    logf = jnp.log(f)
    b = _chunk_cumsum(logf)
    bl = [_sum_rows(logf[_chunk_rows(c), :]) for c in range(HG_GROUP)]
    eb, enb, e2 = jnp.exp(b), jnp.exp(-b), jnp.exp(_chunk_bcast(bl) - b)
    ebl = [jnp.exp(r) for r in bl]
    return dict(sq=sq, sg=sg, f=f, eb=eb, enb=enb, e2=e2, ebl=ebl, qd=q * eb, kd=k * enb, k2=k * e2)


def _chunk_rows(c):
    return slice(c * HG_CHUNK, (c + 1) * HG_CHUNK)


def _head_lanes(h):
    return slice(h * HG_HEAD_DIM, (h + 1) * HG_HEAD_DIM)


def _hgrn_fwd(proj_h, lb, hg_w, ride_srcs, ride_modes):
    B, T, _ = proj_h.shape
    ng = T // HG_ROWS
    nr = len(ride_srcs)

    def body(*refs):
        hq_ref, hf_ref, hi_ref, hg_ref, lb_ref, gw_ref = refs[:6]
        ride_in = refs[6:6 + nr]
        o_ref, rg_ref, sp_ref = refs[6 + nr:9 + nr]
        ride_out = refs[9 + nr:9 + 2 * nr]
        st = refs[9 + 2 * nr]
        sems = refs[10 + 2 * nr:]
        gi = pl.program_id(1)
        step = pl.program_id(0) * ng + gi
        _ride_start(ride_modes, step, B * ng, ride_in, ride_out, sems)

        @pl.when(gi == 0)
        def _():
            st[...] = jnp.zeros(st.shape, F32)

        lo = _group_mask()
        for h in range(HG_HEADS):
            lanes = _head_lanes(h)
            gt = _hgrn_gates(hq_ref[:, lanes], hf_ref[:, lanes], lb_ref[:, lanes])
            v, qd, kd = _bf(hi_ref[:, lanes]), _bf(gt["qd"]), _bf(gt["kd"])
            a = jnp.where(lo, _dot_nt(qd, kd), 0.0)
            kv = _dot_tn(v, _spread(_bf(gt["k2"])))
            s = st[h]
            before = []
            for c in range(HG_GROUP):
                before.append(s)
                s = s * gt["ebl"][c] + _lane_block(kv, c)
            st[h] = s
            sp = jnp.concatenate(before, axis=1)
            sp_ref[h] = sp
            o = _dot(_bf(a), v) + _dot_nt(_spread(qd), _bf(sp))
            o_ref[:, lanes] = o
            hg = hg_ref[:, lanes]
            rn = o * lax.rsqrt(_mean_last(o * o) + EPS) * gw_ref[...]
            rg_ref[:, lanes] = _bf(rn * (hg * _sigmoid(hg)))
        _ride_wait(ride_modes, step, B * ng, ride_in, ride_out, sems)

    part = lambda j: pl.BlockSpec((None, HG_ROWS, HG_WIDTH), lambda b, g: (b, g, j))
    return pl.pallas_call(
        body, name="hgrn_fwd", grid=(B, ng),
        in_specs=[part(0), part(1), part(2), part(3),
                  pl.BlockSpec((1, HG_WIDTH), lambda b, g: (0, 0)),
                  pl.BlockSpec((1, LANES), lambda b, g: (0, 0))] + [ANY_SPEC] * nr,
        out_specs=[part(0), part(0),
                   pl.BlockSpec((None, HG_HEADS, None, HG_HEAD_DIM, HG_STACK), lambda b, g: (b, 0, g, 0, 0))]
        + [ANY_SPEC] * nr,
        out_shape=[SDS((B, T, HG_WIDTH), F32), SDS((B, T, HG_WIDTH), BF16),
                   SDS((B, HG_HEADS, ng, HG_HEAD_DIM, HG_STACK), F32)] + _exchange_shapes(ride_srcs, ride_modes),
        scratch_shapes=[pltpu.VMEM((HG_HEADS, HG_HEAD_DIM, HG_HEAD_DIM), F32)] + _exchange_sems(nr),
        compiler_params=_params(("arbitrary", "arbitrary"), VMEM_LIMIT_BIG),
    )(proj_h, proj_h, proj_h, proj_h, lb, hg_w, *ride_srcs)


def _mix_out(x2, attn_n, rec_g, mod8, post_w, w_out_bf, T, ride_srcs, ride_modes):
    N = x2.shape[0]
    TM = _tile_rows(T, big=True)
    tps = T // TM
    nr = len(ride_srcs)

    def body(*refs):
        x_ref, an_ref, rg_ref, mod_ref, pw_ref, w_ref = refs[:6]
        ride_in = refs[6:6 + nr]
        mix_ref, x1_ref, cat_ref = refs[6 + nr:9 + nr]
        ride_out = refs[9 + nr:9 + 2 * nr]
        sems = refs[9 + 2 * nr:]
        _ride_start(ride_modes, pl.program_id(0), N // TM, ride_in, ride_out, sems)
        cat = jnp.concatenate([an_ref[...], rg_ref[...]], axis=1)
        cat_ref[...] = cat
        mix = _dot(cat, w_ref[...])
        mix_ref[...] = mix
        r = lax.rsqrt(_mean_last(mix * mix) + EPS)
        x1_ref[...] = x_ref[...] + mod_ref[2:3, :] * (mix * r * pw_ref[...])
        _ride_wait(ride_modes, pl.program_id(0), N // TM, ride_in, ride_out, sems)

    row = lambda w: pl.BlockSpec((TM, w), lambda i: (i, 0))
    return pl.pallas_call(
        body, name="mix_out", grid=(N // TM,),
        in_specs=[row(D_MODEL), row(ATT_WIDTH), row(HG_WIDTH), _mod_spec(tps),
                  pl.BlockSpec((1, D_MODEL), lambda i: (0, 0)),
                  pl.BlockSpec((D_MODEL, D_MODEL), lambda i: (0, 0))] + [ANY_SPEC] * nr,
        out_specs=[row(D_MODEL), row(D_MODEL), row(D_MODEL)] + [ANY_SPEC] * nr,
        out_shape=[SDS((N, D_MODEL), F32), SDS((N, D_MODEL), F32), SDS((N, D_MODEL), BF16)]
        + _exchange_shapes(ride_srcs, ride_modes),
        scratch_shapes=_exchange_sems(nr),
        compiler_params=_params(("arbitrary",), VMEM_LIMIT_BIG),
    )(x2, attn_n, rec_g, mod8, post_w, w_out_bf, *ride_srcs)


def _load_weights_once(pairs, sem):
    @pl.when(pl.program_id(0) == 0)
    def _():
        cps = [pltpu.make_async_copy(src, dst, sem.at[i]) for i, (src, dst) in enumerate(pairs)]
        for cp in cps:
            cp.start()
        for cp in cps:
            cp.wait()


MLP_HALF = D_MODEL // 2
MLP_PIECES = 2 * N_DEV + 2


def _mlp_weight_pieces(wu_a, wu_b, wd_a, wd_b, wu, wd):
    cols = D_FF // N_DEV
    pairs = []
    for h, half in enumerate((wu_a, wu_b)):
        for j in range(N_DEV):
            pairs.append((half.at[j], wu.at[pl.ds(h * MLP_HALF, MLP_HALF), pl.ds(j * cols, cols)]))
    col = 0
    for part in (wd_a, wd_b):
        pairs.append((part, wd.at[:, pl.ds(col, part.shape[1])]))
        col += part.shape[1]
    return pairs


def _mlp_fwd(x1, mod8, pre_w, w_up_halves, w_down_halves, T):
    N = x1.shape[0]
    TM = _tile_rows(T)
    tps = T // TM

    def body(x_ref, mod_ref, pw_ref, wua, wub, wda, wdb, up_ref, u_ref, d_ref, h2_ref, wu, wd, sem):
        _load_weights_once(_mlp_weight_pieces(wua, wub, wda, wdb, wu, wd), sem)
        x = x_ref[...]
        r = lax.rsqrt(_mean_last(x * x) + EPS)
        h = (x * r * pw_ref[...]) * (1.0 + mod_ref[4:5, :]) + mod_ref[3:4, :]
        hb = _bf(h)
        h2_ref[...] = hb
        up = _dot(hb, wu[...])
        up_ref[...] = up
        ru = jnp.maximum(up, 0.0)
        u = _bf(ru * ru)
        u_ref[...] = u
        d_ref[...] = _dot(u, wd[...])

    row = lambda w: pl.BlockSpec((TM, w), lambda i: (i, 0))
    return pl.pallas_call(
        body, name="mlp_fwd", grid=(N // TM,),
        in_specs=[row(D_MODEL), _mod_spec(tps), pl.BlockSpec((1, D_MODEL), lambda i: (0, 0))] + [ANY_SPEC] * 4,
        out_specs=[row(D_FF), row(D_FF), row(D_MODEL), row(D_MODEL)],
        out_shape=[SDS((N, D_FF), F32), SDS((N, D_FF), BF16), SDS((N, D_MODEL), F32), SDS((N, D_MODEL), BF16)],
        scratch_shapes=[pltpu.VMEM((D_MODEL, D_FF), BF16), pltpu.VMEM((D_FF, D_MODEL), BF16),
                        pltpu.SemaphoreType.DMA((MLP_PIECES,))],
        compiler_params=_params(("arbitrary",), VMEM_LIMIT_BIG),
    )(x1, mod8, pre_w, *w_up_halves, *w_down_halves)


def _acc_rows(acc_ref, first, rows):
    @pl.when(first)
    def _():
        acc_ref[...] = jnp.zeros(acc_ref.shape, F32)
    for i, r in enumerate(rows):
        acc_ref[i:i + 1, :] += r


def _mlp_bwd(x1, d, up, tgt, mod8, pre_w, post_w, w_up_halves, w_down_halves, T):
    N = x1.shape[0]
    TM = _tile_rows(T)
    tps = T // TM

    def body(x_ref, d_ref, up_ref, t_ref, mod_ref, pw_ref, qw_ref, wua, wub, wda, wdb,
             dx_ref, dup_ref, dd_ref, acc_ref, wd, wu, sem):
        _load_weights_once(_mlp_weight_pieces(wua, wub, wda, wdb, wu, wd), sem)
        sh2, sc2, g2 = mod_ref[3:4, :], mod_ref[4:5, :], mod_ref[5:6, :]
        x = x_ref[...]
        r1 = lax.rsqrt(_mean_last(x * x) + EPS)
        xh = x * r1
        n2 = xh * pw_ref[...]
        dv = d_ref[...]
        rd = lax.rsqrt(_mean_last(dv * dv) + EPS)
        dh = dv * rd
        rr = dh * qw_ref[...]
        e = x + g2 * rr - t_ref[...]
        loss = 0.5 * jnp.sum(_sum_rows(e * e), axis=1, keepdims=True) / D_MODEL
        dy = e * (1.0 / D_MODEL)
        dg2 = _sum_rows(dy * rr)
        drr = dy * g2
        dw_post = _sum_rows(drr * dh)
        ddh = drr * qw_ref[...]
        dd = _bf(rd * (ddh - dh * _mean_last(ddh * dh)))
        dd_ref[...] = dd
        ru = jnp.maximum(up_ref[...], 0.0)
        dup = _bf(_dot_nt(dd, wd[...]) * (2.0 * ru))
        dup_ref[...] = dup
        dh2 = _dot_nt(dup, wu[...])
        dsh2 = _sum_rows(dh2)
        dsc2 = _sum_rows(dh2 * n2)
        dn2 = dh2 * (1.0 + sc2)
        dw_pre = _sum_rows(dn2 * xh)
        dxh = dn2 * pw_ref[...]
        dx_ref[...] = dy + r1 * (dxh - xh * _mean_last(dxh * xh))
        _acc_rows(acc_ref, pl.program_id(0) % tps == 0,
                  [dsh2, dsc2, dg2, dw_pre, dw_post, jnp.broadcast_to(loss, (1, D_MODEL))])

    row = lambda w: pl.BlockSpec((TM, w), lambda i: (i, 0))
    vec = pl.BlockSpec((1, D_MODEL), lambda i: (0, 0))
    B = N // T
    return pl.pallas_call(
        body, name="mlp_bwd", grid=(N // TM,),
        in_specs=[row(D_MODEL), row(D_MODEL), row(D_FF), row(D_MODEL), _mod_spec(tps), vec, vec] + [ANY_SPEC] * 4,
        out_specs=[row(D_MODEL), row(D_FF), row(D_MODEL), _mod_spec(tps)],
        out_shape=[SDS((N, D_MODEL), F32), SDS((N, D_FF), BF16), SDS((N, D_MODEL), BF16),
                   SDS((B, 8, D_MODEL), F32)],
        scratch_shapes=[pltpu.VMEM((D_FF, D_MODEL), BF16), pltpu.VMEM((D_MODEL, D_FF), BF16),
                        pltpu.SemaphoreType.DMA((MLP_PIECES,))],
        compiler_params=_params(("arbitrary",), VMEM_LIMIT_BIG),
    )(x1, d, up, tgt, mod8, pre_w, post_w, *w_up_halves, *w_down_halves)


def _mix_bwd(mix, dx1, mod8, post_w, w_out_bf, T, ride_srcs, ride_modes):
    N = mix.shape[0]
    TM = _tile_rows(T, big=True)
    tps = T // TM
    nr = len(ride_srcs)

    def body(*refs):
        mix_ref, dx_ref, mod_ref, pw_ref, w_ref = refs[:5]
        ride_in = refs[5:5 + nr]
        dan_ref, drg_ref, dmix_ref, acc_ref = refs[5 + nr:9 + nr]
        ride_out = refs[9 + nr:9 + 2 * nr]
        sems = refs[9 + 2 * nr:]
        _ride_start(ride_modes, pl.program_id(0), N // TM, ride_in, ride_out, sems)
        g1 = mod_ref[2:3, :]
        mix = mix_ref[...]
        dx1 = dx_ref[...]
        rm = lax.rsqrt(_mean_last(mix * mix) + EPS)
        mh = mix * rm
        dg1 = _sum_rows(dx1 * (mh * pw_ref[...]))
        dr = dx1 * g1
        dw_post = _sum_rows(dr * mh)
        dmh = dr * pw_ref[...]
        dmix = _bf(rm * (dmh - mh * _mean_last(dmh * mh)))
        dmix_ref[...] = dmix
        dcat = _dot_nt(dmix, w_ref[...])
        dan_ref[...] = dcat[:, :ATT_WIDTH]
        drg_ref[...] = dcat[:, ATT_WIDTH:]
        _acc_rows(acc_ref, pl.program_id(0) % tps == 0, [dg1, dw_post])
        _ride_wait(ride_modes, pl.program_id(0), N // TM, ride_in, ride_out, sems)

    row = lambda w: pl.BlockSpec((TM, w), lambda i: (i, 0))
    B = N // T
    return pl.pallas_call(
        body, name="mix_bwd", grid=(N // TM,),
        in_specs=[row(D_MODEL), row(D_MODEL), _mod_spec(tps), pl.BlockSpec((1, D_MODEL), lambda i: (0, 0)),
                  pl.BlockSpec((D_MODEL, D_MODEL), lambda i: (0, 0))] + [ANY_SPEC] * nr,
        out_specs=[row(ATT_WIDTH), row(HG_WIDTH), row(D_MODEL), _mod_spec(tps)] + [ANY_SPEC] * nr,
        out_shape=[SDS((N, ATT_WIDTH), F32), SDS((N, HG_WIDTH), F32), SDS((N, D_MODEL), BF16),
                   SDS((B, 8, D_MODEL), F32)] + _exchange_shapes(ride_srcs, ride_modes),
        scratch_shapes=_exchange_sems(nr),
        compiler_params=_params(("arbitrary",), VMEM_LIMIT_BIG),
    )(mix, dx1, mod8, post_w, w_out_bf, *ride_srcs)


def _hgrn_bwd(proj_h, lb, hg_w, o, s_prev, drg, ride_srcs, ride_modes):
    B, T, _ = proj_h.shape
    ng = T // HG_ROWS
    nr = len(ride_srcs)

    def body(*refs):
        hq_ref, hf_ref, hi_ref, hg_ref, lb_ref, gw_ref, o_ref, sp_ref, drg_ref = refs[:9]
        ride_in = refs[9:9 + nr]
        dhq_ref, dhf_ref, dhi_ref, dhg_ref, dlb_ref, dgw_ref = refs[9 + nr:15 + nr]
        ride_out = refs[15 + nr:15 + 2 * nr]
        dst = refs[15 + 2 * nr]
        sems = refs[16 + 2 * nr:]
        step = pl.program_id(0) * ng + pl.program_id(1)
        _ride_start(ride_modes, step, B * ng, ride_in, ride_out, sems)

        @pl.when(pl.program_id(1) == 0)
        def _():
            dst[...] = jnp.zeros(dst.shape, F32)
            dlb_ref[...] = jnp.zeros(dlb_ref.shape, F32)
            dgw_ref[...] = jnp.zeros(dgw_ref.shape, F32)

        lo = _group_mask()
        gw = gw_ref[...]

        for h in range(HG_HEADS):
            lanes = _head_lanes(h)
            lbv = lb_ref[:, lanes]
            hq = hq_ref[:, lanes]
            gt = _hgrn_gates(hq, hf_ref[:, lanes], lbv)
            sq, sg, qdf, kdf, k2f, ebl = gt["sq"], gt["sg"], gt["qd"], gt["kd"], gt["k2"], gt["ebl"]
            v, qd, kd = _bf(hi_ref[:, lanes]), _bf(qdf), _bf(kdf)
            ov = o_ref[:, lanes]
            hg = hg_ref[:, lanes]
            shg = _sigmoid(hg)
            dr = drg_ref[:, lanes]
            ro = lax.rsqrt(_mean_last(ov * ov) + EPS)
            oh = ov * ro
            dhg_ref[:, lanes] = _bf(dr * (oh * gw) * (shg + hg * shg * (1.0 - shg)))
            drn = dr * (hg * shg)
            dgw_ref[...] += jnp.broadcast_to(_sum_rows(drn * oh), (8, LANES))
            doh = drn * gw
            do = _bf(ro * (doh - oh * _mean_last(doh * oh)))
            a = jnp.where(lo, _dot_nt(qd, kd), 0.0)
            da = _bf(jnp.where(lo, _dot_nt(do, v), 0.0))
            dv = _dot_tn(_bf(a), do)
            dqd = _dot(da, kd)
            dkd = _dot_tn(da, qd)
            sp = sp_ref[h]
            incr = _dot_tn(do, _spread(qd))
            ds = dst[h]
            after = [None] * HG_GROUP
            for c in reversed(range(HG_GROUP)):
                after[c] = ds
                ds = ds * ebl[c] + _lane_block(incr, c)
            dst[h] = ds
            dss = jnp.concatenate(after, axis=1)
            dssb = _bf(dss)
            dk2 = _pick(_dot(v, dssb))
            dhi_ref[:, lanes] = _bf(dv + _dot_nt(_spread(_bf(k2f)), dssb))
            dqd = dqd + _pick(_dot(do, _bf(sp)))
            debl = _sum_rows(dss * sp)
            k2g = dk2 * k2f
            db = dqd * qdf - dkd * kdf - k2g
            dk = dkd * gt["enb"] + dk2 * gt["e2"]
            dbl = _chunk_bcast([_lane_block(debl, c) * ebl[c] + _sum_rows(k2g[_chunk_rows(c), :])
                                for c in range(HG_GROUP)])
            dg = _chunk_cumsum(db, reverse=True) + dbl
            df = dg / gt["f"] - dk
            dhf_ref[:, lanes] = _bf(df * (1.0 - lbv) * sg * (1.0 - sg))
            dlb_ref[:, lanes] += jnp.broadcast_to(_sum_rows(df * (1.0 - sg)), (8, LANES))
            dhq_ref[:, lanes] = _bf((dqd * gt["eb"]) * (sq + hq * sq * (1.0 - sq)))
        _ride_wait(ride_modes, step, B * ng, ride_in, ride_out, sems)

    part = lambda j: pl.BlockSpec((None, HG_ROWS, HG_WIDTH), lambda b, g: (b, ng - 1 - g, j))
    return pl.pallas_call(
        body, name="hgrn_bwd", grid=(B, ng),
        in_specs=[part(0), part(1), part(2), part(3),
                  pl.BlockSpec((1, HG_WIDTH), lambda b, g: (0, 0)),
                  pl.BlockSpec((1, LANES), lambda b, g: (0, 0)),
                  part(0),
                  pl.BlockSpec((None, HG_HEADS, None, HG_HEAD_DIM, HG_STACK), lambda b, g: (b, 0, ng - 1 - g, 0, 0)),
                  part(0)] + [ANY_SPEC] * nr,
        out_specs=[part(0), part(0), part(0), part(0),
                   pl.BlockSpec((None, 8, HG_WIDTH), lambda b, g: (b, 0, 0)),
                   pl.BlockSpec((None, 8, LANES), lambda b, g: (b, 0, 0))] + [ANY_SPEC] * nr,
        out_shape=[SDS((B, T, HG_WIDTH), BF16)] * 4 + [SDS((B, 8, HG_WIDTH), F32), SDS((B, 8, LANES), F32)]
        + _exchange_shapes(ride_srcs, ride_modes),
        scratch_shapes=[pltpu.VMEM((HG_HEADS, HG_HEAD_DIM, HG_HEAD_DIM), F32)] + _exchange_sems(nr),
        compiler_params=_params(("arbitrary", "arbitrary"), VMEM_LIMIT_BIG),
    )(proj_h, proj_h, proj_h, proj_h, lb, hg_w, o, s_prev, drg, *ride_srcs)


def _attn_bwd(qr, kr, proj3, attn_o, dan, tables, sinks, attn_w, ride_srcs, ride_modes):
    B, T, _ = proj3.shape
    nb = T // WINDOW
    splits = min(ATT_SPLITS, nb)
    per = nb // splits
    nr = len(ride_srcs)
    cos, sinl, sinr = tables
    QKV = ATT_WIDTH + 2 * LANES

    def body(*refs):
        qr_ref, kr_ref, v_ref, o_ref, dan_ref, cos_ref, sl_ref, sr_ref, sink_ref, aw_ref = refs[:10]
        ride_in = refs[10:10 + nr]
        dqkv_ref, dsink_ref, daw_ref = refs[10 + nr:13 + nr]
        ride_out = refs[13 + nr:13 + 2 * nr]
        kpad, vpad, dkpad, dvpad, dqb, dsk = refs[13 + 2 * nr:19 + 2 * nr]
        sems = refs[19 + 2 * nr:]
        part = pl.program_id(1)
        step = pl.program_id(0) * splits + part
        _ride_start(ride_modes, step, B * splits, ride_in, ride_out, sems)

        @pl.when(part == 0)
        def _():
            kpad[0:WINDOW, :] = jnp.zeros((WINDOW, LANES), BF16)
            vpad[0:WINDOW, :] = jnp.zeros((WINDOW, LANES), BF16)
            kpad[WINDOW:, :] = kr_ref[...]
            vpad[WINDOW:, :] = _bf(v_ref[...])
            dkpad[...] = jnp.zeros(dkpad.shape, F32)
            dvpad[...] = jnp.zeros(dvpad.shape, F32)
            dsk[...] = jnp.zeros(dsk.shape, F32)
            daw_ref[...] = jnp.zeros(daw_ref.shape, F32)

        lower = _lower_mask()
        aw = aw_ref[...]

        def block(n, daw):
            r0 = pl.multiple_of(n * WINDOW, WINDOW)
            rows = pl.ds(r0, WINDOW)
            nxt = pl.ds(r0 + WINDOW, WINDOW)
            ob = o_ref[rows, :]
            dn = dan_ref[rows, :]
            ro = lax.rsqrt(_mean_last(ob * ob) + EPS)
            oh = ob * ro
            daw = daw + _sum_rows(dn * oh)
            doh = dn * aw
            do = _bf(ro * (doh - oh * _mean_last(doh * oh)))
            doparts = [do[:, j * LANES:(j + 1) * LANES] for j in range(ATT_WIDTH // LANES)]
            qparts = [qr_ref[rows, j * LANES:(j + 1) * LANES] for j in range(ATT_WIDTH // LANES)]
            for hk in range(ATT_KV_HEADS):
                lanes = slice(hk * ATT_HEAD_DIM, (hk + 1) * ATT_HEAD_DIM)
                qs = _stack_heads(qparts, hk)
                dos = _stack_heads(doparts, hk)
                k_cur, k_prev = kpad[nxt, lanes], kpad[rows, lanes]
                v_cur, v_prev = vpad[nxt, lanes], vpad[rows, lanes]
                p, inv, es = _softmax_window(qs, k_cur, k_prev, lower, n > 0, _sink_row(sink_ref, hk))
                p = p * inv
                dp = jnp.where(lower, _dot_nt(v_cur, dos), _dot_nt(v_prev, dos))
                delta = jnp.sum(p * dp, axis=0, keepdims=True)
                ds = p * (dp - delta)
                sk = (es * inv) * delta
                ds_cur = jnp.where(lower, ds, 0.0)
                p_cur = jnp.where(lower, p, 0.0)
                ds_cur, ds_prev = _bf(ds_cur), _bf(ds - ds_cur)
                p_cur, p_prev = _bf(p_cur), _bf(p - p_cur)
                dqt = (_dot_tn(k_cur, ds_cur) + _dot_tn(k_prev, ds_prev)) * ATT_SCALE
                dkpad[nxt, lanes] += _dot(ds_cur, qs)
                dkpad[rows, lanes] += _dot(ds_prev, qs)
                dvpad[nxt, lanes] += _dot(p_cur, dos)
                dvpad[rows, lanes] += _dot(p_prev, dos)
                for g in range(ATT_GROUP):
                    h = ATT_GROUP * hk + g
                    cols = slice(g * WINDOW, (g + 1) * WINDOW)
                    dqb[:, h * ATT_HEAD_DIM:(h + 1) * ATT_HEAD_DIM] = dqt[:, cols].T
                    head_lane = lax.broadcasted_iota(jnp.int32, dsk.shape, 1) == h
                    dsk[...] += jnp.where(head_lane, -jnp.sum(sk[:, cols], axis=1, keepdims=True), 0.0)
            cs, sl, sr = cos_ref[rows, :], sl_ref[rows, :], sr_ref[rows, :]
            for j in range(ATT_WIDTH // LANES):
                dqkv_ref[rows, j * LANES:(j + 1) * LANES] = _bf(_rope_t(dqb[:, j * LANES:(j + 1) * LANES], cs, sl, sr))
            return daw

        daw = _loop_pairs(part * per, per, block, jnp.zeros((1, ATT_WIDTH), F32))
        daw_ref[...] += jnp.broadcast_to(daw, (8, ATT_WIDTH))
        dsink_ref[...] = dsk[...]

        def finish(n, carry):
            r0 = pl.multiple_of(n * WINDOW, WINDOW)
            rows = pl.ds(r0, WINDOW)
            nxt = pl.ds(r0 + WINDOW, WINDOW)
            cs, sl, sr = cos_ref[rows, :], sl_ref[rows, :], sr_ref[rows, :]
            dqkv_ref[rows, ATT_WIDTH:ATT_WIDTH + LANES] = _bf(_rope_t(dkpad[nxt, :], cs, sl, sr))
            dqkv_ref[rows, ATT_WIDTH + LANES:QKV] = _bf(dvpad[nxt, :])
            return carry

        @pl.when(part == splits - 1)
        def _():
            lax.fori_loop(0, nb, finish, 0)

        _ride_wait(ride_modes, step, B * splits, ride_in, ride_out, sems)

    seq = lambda w, j: pl.BlockSpec((None, T, w), lambda b, s: (b, 0, j))
    full = lambda r, w: pl.BlockSpec((r, w), lambda b, s: (0, 0))
    return pl.pallas_call(
        body, name="attn_bwd", grid=(B, splits),
        in_specs=[seq(ATT_WIDTH, 0), seq(LANES, 0), seq(LANES, 5), seq(ATT_WIDTH, 0), seq(ATT_WIDTH, 0),
                  full(T, LANES), full(T, LANES), full(T, LANES),
                  pl.BlockSpec(memory_space=pltpu.SMEM), full(1, ATT_WIDTH)] + [ANY_SPEC] * nr,
        out_specs=[seq(QKV, 0), pl.BlockSpec((None, 8, LANES), lambda b, s: (b, 0, 0)),
                   pl.BlockSpec((None, 8, ATT_WIDTH), lambda b, s: (b, 0, 0))] + [ANY_SPEC] * nr,
        out_shape=[SDS((B, T, QKV), BF16), SDS((B, 8, LANES), F32), SDS((B, 8, ATT_WIDTH), F32)]
        + _exchange_shapes(ride_srcs, ride_modes),
        scratch_shapes=[pltpu.VMEM((T + WINDOW, LANES), BF16), pltpu.VMEM((T + WINDOW, LANES), BF16),
                        pltpu.VMEM((T + WINDOW, LANES), F32), pltpu.VMEM((T + WINDOW, LANES), F32),
                        pltpu.VMEM((WINDOW, ATT_WIDTH), F32), pltpu.VMEM((8, LANES), F32)] + _exchange_sems(nr),
        compiler_params=_params(("arbitrary", "arbitrary"), VMEM_LIMIT_BIG),
    )(qr, kr, proj3, attn_o, dan, cos, sinl, sinr, sinks, attn_w, *ride_srcs)


def _in_bwd(x2, dx1, dqkv, dhq, dhf, dhi, dhg, mod8, pre_w, w_in_bf, T, ride_srcs, ride_modes):
    N = x2.shape[0]
    TM = _tile_rows(T, big=True)
    tps = T // TM
    nr = len(ride_srcs)
    pieces = [(0, ATT_WIDTH + 2 * LANES), (768, HG_WIDTH), (1280, HG_WIDTH), (1792, HG_WIDTH), (2304, HG_WIDTH)]

    def body(*refs):
        x_ref, dx_ref, p0, p1, p2, p3, p4, mod_ref, pw_ref, w_ref = refs[:10]
        ride_in = refs[10:10 + nr]
        gx_ref, dproj_ref, acc_ref = refs[10 + nr:13 + nr]
        ride_out = refs[13 + nr:13 + 2 * nr]
        sems = refs[13 + 2 * nr:]
        _ride_start(ride_modes, pl.program_id(0), N // TM, ride_in, ride_out, sems)
        sc1 = mod_ref[1:2, :]
        dh = jnp.zeros((TM, D_MODEL), F32)
        for ref, (off, width) in zip((p0, p1, p2, p3, p4), pieces):
            pb = ref[...]
            dproj_ref[:, off:off + width] = pb
            dh = dh + _dot(pb, w_ref[off:off + width, :])
        x = x_ref[...]
        r = lax.rsqrt(_mean_last(x * x) + EPS)
        xh = x * r
        n1 = xh * pw_ref[...]
        dsh1 = _sum_rows(dh)
        dsc1 = _sum_rows(dh * n1)
        dn1 = dh * (1.0 + sc1)
        dw_pre = _sum_rows(dn1 * xh)
        dxh = dn1 * pw_ref[...]
        gx_ref[...] = dx_ref[...] + r * (dxh - xh * _mean_last(dxh * xh))
        _acc_rows(acc_ref, pl.program_id(0) % tps == 0, [dsh1, dsc1, dw_pre])
        _ride_wait(ride_modes, pl.program_id(0), N // TM, ride_in, ride_out, sems)

    row = lambda w: pl.BlockSpec((TM, w), lambda i: (i, 0))
    B = N // T
    return pl.pallas_call(
        body, name="in_bwd", grid=(N // TM,),
        in_specs=[row(D_MODEL), row(D_MODEL), row(768), row(HG_WIDTH), row(HG_WIDTH), row(HG_WIDTH),
                  row(HG_WIDTH), _mod_spec(tps), pl.BlockSpec((1, D_MODEL), lambda i: (0, 0)),
                  pl.BlockSpec((IN_COLS, D_MODEL), lambda i: (0, 0))] + [ANY_SPEC] * nr,
        out_specs=[row(D_MODEL), row(IN_COLS), _mod_spec(tps)] + [ANY_SPEC] * nr,
        out_shape=[SDS((N, D_MODEL), F32), SDS((N, IN_COLS), BF16), SDS((B, 8, D_MODEL), F32)]
        + _exchange_shapes(ride_srcs, ride_modes),
        scratch_shapes=_exchange_sems(nr),
        compiler_params=_params(("arbitrary",), VMEM_LIMIT_BIG),
    )(x2, dx1, dqkv, dhq, dhf, dhi, dhg, mod8, pre_w, w_in_bf, *ride_srcs)


def _matmul_tn(name, a, b, tn, tm):
    K, M = a.shape
    Nc = b.shape[1]

    def body(a_ref, b_ref, o_ref):
        o_ref[...] = _bf(_dot_tn(a_ref[...], b_ref[...]))

    return pl.pallas_call(
        body, name=name, grid=(M // tm, Nc // tn),
        in_specs=[pl.BlockSpec((K, tm), lambda i, j: (0, i)),
                  pl.BlockSpec((K, tn), lambda i, j: (0, j))],
        out_specs=pl.BlockSpec((tm, tn), lambda i, j: (i, j)), out_shape=SDS((M, Nc), BF16),
        compiler_params=_params(("arbitrary", "arbitrary"), VMEM_LIMIT_BIG),
    )(a, b)


def _matmul_tn_paired(name, a, b, stream_a):
    K = a.shape[0]
    stream, fixed = (a, b) if stream_a else (b, a)
    w = stream.shape[1] // N_DEV
    blk = (w, fixed.shape[1]) if stream_a else (fixed.shape[1], w)
    chips = N_DEV // 2

    def body(s_hbm, f_hbm, out_ref, s_buf, f_buf, g_buf, theirs, in_sems, send_sems, recv_sems):
        core = lax.axis_index("c")
        sib_dev, _ = _related(SIBLING)
        fixed_load = pltpu.make_async_copy(f_hbm, f_buf, in_sems.at[2])

        def load(j):
            owner = 2 * (j // 2) + (core if j % 2 else 1 - core)
            return pltpu.make_async_copy(s_hbm.at[:, pl.ds(pl.multiple_of(owner * w, LANES), w)], s_buf.at[j % 2],
                                         in_sems.at[j % 2])

        def swap(s):
            return pltpu.make_async_remote_copy(
                src_ref=g_buf.at[s, 0], dst_ref=theirs.at[s], send_sem=send_sems.at[s],
                recv_sem=recv_sems.at[s], device_id=sib_dev, device_id_type=MESH)

        fixed_load.start()
        load(0).start()
        fixed_load.wait()
        for j in range(N_DEV):
            s, mine = divmod(j, 2)
            load(j).wait()
            if j + 1 < N_DEV:
                load(j + 1).start()
            if stream_a:
                g_buf[s, mine] = _bf(_dot_tn(s_buf[j % 2], f_buf[...]))
            else:
                g_buf[s, mine] = _bf(_dot_tn(f_buf[...], s_buf[j % 2]))
            if mine:
                swap(s).wait_recv()
                out_ref[s] = _bf(g_buf[s, 1].astype(F32) + theirs[s].astype(F32))
            else:
                swap(s).start()
        for s in range(chips):
            swap(s).wait_send()

    return pl.pallas_call(
        body, name=name, in_specs=[ANY_SPEC] * 2, out_shape=SDS((chips,) + blk, BF16),
        scratch_shapes=[pltpu.VMEM((2, K, w), BF16), pltpu.VMEM(fixed.shape, BF16),
                        pltpu.VMEM((chips, 2) + blk, BF16), pltpu.VMEM((chips,) + blk, BF16),
                        pltpu.SemaphoreType.DMA((3,)), pltpu.SemaphoreType.DMA((chips,)),
                        pltpu.SemaphoreType.DMA((chips,))],
        compiler_params=_params(None, VMEM_LIMIT_BIG),
    )(stream, fixed)


GW_BLOCK = IN_COLS // N_DEV
GW_HALF = IN_COLS // 2


def _grad_w_in_reduced(dproj, h1, ride_srcs, ride_modes):
    K = dproj.shape[0]
    nr = len(ride_srcs)
    chips = N_DEV // 2
    tn = 512

    def body(*refs):
        a_hbm, b_hbm = refs[:2]
        ride_in, out, ride_out = refs[2:2 + nr], refs[2 + nr], refs[3 + nr:3 + 2 * nr]
        a_buf, b_buf, g_buf, theirs, p_buf, in_sems, pair_send, pair_recv, chip_send, chip_recv, own_sem = \
            refs[3 + 2 * nr:14 + 2 * nr]
        ride = _exchange_phases(ride_modes, ride_in, ride_out, *refs[14 + 2 * nr:]) if nr else ([], [], [])
        x, y, core = lax.axis_index("x"), lax.axis_index("y"), lax.axis_index("c")
        chip = 2 * x + y
        sib_dev, _ = _related(SIBLING)

        def remote(src, dst, send_sem, recv_sem, dev):
            return pltpu.make_async_remote_copy(src_ref=src, dst_ref=dst, send_sem=send_sem, recv_sem=recv_sem,
                                                device_id=dev, device_id_type=MESH)

        halves = [1 - x, x]
        loads = [pltpu.make_async_copy(b_hbm, b_buf, in_sems.at[0])]
        for t in range(2):
            col = pl.multiple_of(halves[t] * GW_HALF, LANES)
            loads.append(pltpu.make_async_copy(a_hbm.at[:, pl.ds(col, GW_HALF)], a_buf.at[t], in_sems.at[1 + t]))
        for cp in loads:
            cp.start()
        _run(ride[0])
        loads[0].wait()
        end = []
        for t in range(2):
            loads[1 + t].wait()
            if t == 1:
                _run(ride[1])
            for j in range(D_MODEL // tn):
                cols = pl.ds(j * tn, tn)
                res = _dot_tn(a_buf[t], b_buf[:, cols])
                for q in range(2):
                    for cc in range(2):
                        r0 = (2 * q + cc) * GW_BLOCK
                        g_buf[t, q, cc, :, cols] = _bf(res[r0:r0 + GW_BLOCK])
                swaps = [remote(g_buf.at[t, q, 1 - core, :, cols], theirs.at[t, q, :, cols],
                                pair_send.at[t, 2 * j + q], pair_recv.at[t, 2 * j + q], sib_dev) for q in range(2)]
                for cp in swaps:
                    cp.start()
                for cp in swaps:
                    cp.wait_recv()
                end += [cp.wait_send for cp in swaps]
                for q in range(2):
                    p_buf[t, q, :, cols] = _bf(g_buf[t, q, core, :, cols].astype(F32)
                                               + theirs[t, q, :, cols].astype(F32))
                for dy in range(2):
                    k = 4 * (1 - t) + 2 * dy
                    if k == 0:
                        own = pltpu.make_async_copy(p_buf.at[t, y, :, cols], out.at[chip, :, cols], own_sem.at[j])
                        own.start()
                        end.append(own.wait)
                        continue
                    dev, peer = _related(k)
                    sems = chip_send.at[k // 2, j], chip_recv.at[k // 2, j]
                    send = remote(p_buf.at[t, y ^ dy, :, cols], out.at[chip, :, cols], *sems, dev)
                    send.start()
                    end += [remote(p_buf.at[t, y ^ dy, :, cols], out.at[peer // 2, :, cols], *sems, dev).wait_recv,
                            send.wait_send]
        _run(ride[2])
        _run(end)

    return pl.pallas_call(
        body, name="grad_w_in",
        in_specs=[ANY_SPEC] * (2 + nr), out_specs=[ANY_SPEC] * (1 + nr),
        out_shape=[SDS((chips, GW_BLOCK, D_MODEL), BF16)] + _exchange_shapes(ride_srcs, ride_modes),
        scratch_shapes=[pltpu.VMEM((2, K, GW_HALF), BF16), pltpu.VMEM((K, D_MODEL), BF16),
                        pltpu.VMEM((2, 2, 2, GW_BLOCK, D_MODEL), BF16), pltpu.VMEM((2, 2, GW_BLOCK, D_MODEL), BF16),
                        pltpu.VMEM((2, 2, GW_BLOCK, D_MODEL), BF16), pltpu.SemaphoreType.DMA((3,)),
                        pltpu.SemaphoreType.DMA((2, 4)), pltpu.SemaphoreType.DMA((2, 4)),
                        pltpu.SemaphoreType.DMA((chips, 2)), pltpu.SemaphoreType.DMA((chips, 2)),
                        pltpu.SemaphoreType.DMA((2,))] + _exchange_sems(nr),
        compiler_params=_params(None, VMEM_LIMIT_BIG),
    )(dproj, h1, *ride_srcs)


def _adamw_math(w, g, m, v):
    m2 = ADAM_B1 * m + (1.0 - ADAM_B1) * g
    v2 = ADAM_B2 * v + (1.0 - ADAM_B2) * (g * g)
    m_hat = m2 / (1.0 - ADAM_B1 ** ADAM_STEP)
    v_hat = v2 / (1.0 - ADAM_B2 ** ADAM_STEP)
    delta = -ADAM_LR * (m_hat / (jnp.sqrt(v_hat) + ADAM_EPS) + ADAM_WD * w)
    return delta, m2, v2


def _reduce_adamw(name, sets, steps):
    n = len(sets)

    def body(*refs):
        for k in range(n):
            p_ref, w_ref, m_ref, v_ref = refs[4 * k:4 * k + 4]
            g_ref, d_ref, m2_ref, v2_ref = refs[4 * (n + k):4 * (n + k) + 4]
            g = p_ref[0].astype(F32)
            for s in range(1, p_ref.shape[0]):
                g = g + p_ref[s].astype(F32)
            g_ref[...] = g
            d_ref[...], m2_ref[...], v2_ref[...] = _adamw_math(w_ref[...], g, m_ref[...], v_ref[...])

    in_specs, out_specs, out_shape = [], [], []
    for parts, w, _, _ in sets:
        r, c = w.shape
        blk = pl.BlockSpec((r // steps, c), lambda i: (i, 0))
        in_specs += [pl.BlockSpec((parts.shape[0], r // steps, c), lambda i: (0, i, 0)), blk, blk, blk]
        out_specs += [blk] * 4
        out_shape += [SDS((r, c), F32)] * 4
    outs = pl.pallas_call(
        body, name=name, grid=(steps,), in_specs=in_specs, out_specs=out_specs, out_shape=out_shape,
        compiler_params=_params(("arbitrary",), VMEM_LIMIT_BIG),
    )(*[a for st in sets for a in st])
    return [tuple(outs[4 * k:4 * k + 4]) for k in range(n)]


def _ada_grad_adamw(c_all, dmod_all, w, m, v):
    r, c = w.shape
    tr = 256
    nb = c_all.shape[0]

    def body(c_ref, dm_ref, w_ref, m_ref, v_ref, g_ref, d_ref, m2_ref, v2_ref):
        cv = c_ref[...]
        g = _dot_tn(cv * _sigmoid(cv), dm_ref[...])
        g_ref[...] = g
        d_ref[...], m2_ref[...], v2_ref[...] = _adamw_math(w_ref[...], g, m_ref[...], v_ref[...])

    blk = pl.BlockSpec((tr, c), lambda i: (i, 0))
    return pl.pallas_call(
        body, name="ada_grad_adamw", grid=(r // tr,),
        in_specs=[pl.BlockSpec((nb, tr), lambda i: (0, i)), pl.BlockSpec((nb, c), lambda i: (0, 0)),
                  blk, blk, blk],
        out_specs=[blk] * 4, out_shape=[SDS((r, c), F32)] * 4,
        compiler_params=_params(("arbitrary",)),
    )(c_all, dmod_all, w, m, v)


_SMALL = [("b_ada", 6144), ("pre_w_mix", 1024), ("attn_sinks", 128), ("attn_out_w", 512), ("lb_table", 1024),
          ("hg_norm_w", 128), ("post_w_mix", 1024), ("pre_w_mlp", 1024), ("post_w_mlp", 1024)]


def _pack_small(acc_in, acc_mix, acc_mlp, dsink, daw, dlb, dgw, lb_p, ada_cols):
    B = acc_in.shape[0]
    width = sum(w for _, w in _SMALL) + LANES

    def body(ain, amix, amlp, dsk_ref, daw_ref, dlb_ref, dgw_ref, lbp_ref, packed_ref, dmod_ref):
        def total(ref, r, w=None):
            out = ref[0, r:r + 1, :] if w is None else ref[0, r:r + 1, :w]
            for b in range(1, B):
                out = out + (ref[b, r:r + 1, :] if w is None else ref[b, r:r + 1, :w])
            return out

        d_b_ada = None
        for b in range(B):
            mods = [ain[b, 0:1, :], ain[b, 1:2, :], amix[b, 0:1, :], amlp[b, 0:1, :], amlp[b, 1:2, :], amlp[b, 2:3, :]]
            full = jnp.concatenate(mods, axis=1)
            for j in range(N_DEV):
                dmod_ref[j, b:b + 1, :] = full[:, j * ada_cols:(j + 1) * ada_cols]
            d_b_ada = full if d_b_ada is None else d_b_ada + full
        d_lb = total(dlb_ref, 0)
        pp = lbp_ref[0:1, :] * lbp_ref[1:2, :]
        pieces = [d_b_ada, total(ain, 2), total(dsk_ref, 0), total(daw_ref, 0), -d_lb * pp, d_lb * pp,
                  total(dgw_ref, 0), total(amix, 1), total(amlp, 3), total(amlp, 4), total(amlp, 5, LANES)]
        off = 0
        for piece in pieces:
            packed_ref[:, off:off + piece.shape[1]] = piece
            off += piece.shape[1]

    return pl.pallas_call(
        body, name="pack_small",
        out_shape=[SDS((1, width), F32), SDS((N_DEV, B, ada_cols), F32)],
    )(acc_in, acc_mix, acc_mlp, dsink, daw, dlb, dgw, lb_p)


def _adamw_small(parts, given):
    names = [n for n, _ in _SMALL]
    flat_in = [a for n in names for a in given[n]]

    def body(*refs):
        p_ref = refs[0]
        in_refs = refs[1:1 + 3 * len(names)]
        out_refs = refs[1 + 3 * len(names):-1]
        loss_ref = refs[-1]
        g = p_ref[0]
        for s in range(1, N_DEV):
            g = g + p_ref[s]
        off = 0
        for i, (name, width) in enumerate(_SMALL):
            w_ref, m_ref, v_ref = in_refs[3 * i:3 * i + 3]
            rows, cols = w_ref.shape
            for r in range(rows):
                gr = g[:, off + r * cols:off + (r + 1) * cols]
                res = (gr,) + _adamw_math(w_ref[r:r + 1, :], gr, m_ref[r:r + 1, :], v_ref[r:r + 1, :])
                for o_ref, val in zip(out_refs[4 * i:4 * i + 4], res):
                    o_ref[r:r + 1, :] = val
            off += width
        loss_ref[...] = g[:, off:off + LANES]

    out_shape = [SDS(given[n][0].shape, F32) for n in names for _ in range(4)] + [SDS((1, LANES), F32)]
    outs = pl.pallas_call(body, name="adamw_small", out_shape=out_shape)(parts, *flat_in)
    return {n: tuple(outs[4 * i:4 * i + 4]) for i, n in enumerate(names)}, outs[-1][0, 0]


def kernel(x, c, w_ada, b_ada, pre_w_mix, w_in, attn_sinks, attn_out_w, lb_table, hg_norm_w, w_out, post_w_mix, pre_w_mlp, w_up, w_down, post_w_mlp, loss_target, m_w_ada, m_b_ada, m_pre_w_mix, m_w_in, m_attn_sinks, m_attn_out_w, m_lb_table, m_hg_norm_w, m_w_out, m_post_w_mix, m_pre_w_mlp, m_w_up, m_w_down, m_post_w_mlp, v_w_ada, v_b_ada, v_pre_w_mix, v_w_in, v_attn_sinks, v_attn_out_w, v_lb_table, v_hg_norm_w, v_w_out, v_post_w_mix, v_pre_w_mlp, v_w_up, v_w_down, v_post_w_mlp):
    B, T, _ = x.shape
    N = B * T
    me = 4 * lax.axis_index("x") + 2 * lax.axis_index("y") + lax.axis_index("c")
    x2 = x.reshape(N, D_MODEL)
    tgt2 = loss_target.reshape(N, D_MODEL)

    w_in_t, m_w_in_t, v_w_in_t = w_in[0].T, m_w_in[0].T, v_w_in[0].T
    w_in_g, c_g = _exchange("gather_w_in", [_bf(w_in_t), c], ["gather"] * 2)
    w_in_f = w_in_g.reshape(IN_COLS, D_MODEL)
    c_all = c_g.reshape(N_DEV * B, D_MODEL)

    ada_cols = w_ada.shape[2]
    b_mine = lax.dynamic_slice(b_ada, (0, me * ada_cols), (1, ada_cols))
    mod_cols = _ada_mod(c_all, w_ada[0], b_mine)
    (mod_g,) = _exchange("scatter_mod", [mod_cols.reshape(N_DEV, B, ada_cols)], ["a2a"])
    mod = mod_g.transpose(1, 0, 2).reshape(B, 6, D_MODEL)
    mod8 = jnp.pad(mod, ((0, 0), (0, 2), (0, 0)))

    lb_p = jax.nn.softmax(lb_table, axis=0)
    lb = lb_p[1:2]
    tables = _rope_tables(T)

    w_up_b, w_down_b = _bf(w_up[0]), _bf(w_down[0])
    wd_split = 5 * LANES
    proj_a, proj_h, h1, w_up_g0 = _in_proj(x2, mod8, pre_w_mix, w_in_f, T, [w_up_b[:MLP_HALF]], ["gather"])
    proj3 = proj_a.reshape(B, T, ATT_COLS)
    proj_h = proj_h.reshape(B, T, IN_COLS - ATT_COLS)
    rec_o, rec_g, s_prev, w_up_g1, w_out_g = _hgrn_fwd(proj_h, lb, hg_norm_w, [w_up_b[MLP_HALF:], _bf(w_out[0])],
                                                       ["gather"] * 2)
    attn_o, attn_n, qr, kr, w_down_g0 = _attn_fwd(proj3, tables, attn_sinks, attn_out_w,
                                                  [w_down_b[:, :wd_split]], ["gather"])
    w_out_f = w_out_g.reshape(D_MODEL, D_MODEL)
    mix, x1, cat, w_down_g1 = _mix_out(x2, attn_n.reshape(N, ATT_WIDTH), rec_g.reshape(N, HG_WIDTH), mod8,
                                       post_w_mix, w_out_f, T, [w_down_b[:, wd_split:]], ["gather"])
    w_up_halves = [w_up_g0, w_up_g1]
    w_down_halves = [w_down_g0.reshape(D_FF, wd_split), w_down_g1.reshape(D_FF, D_MODEL - wd_split)]
    up, u, d, h2 = _mlp_fwd(x1, mod8, pre_w_mlp, w_up_halves, w_down_halves, T)

    dx1, dup, dd, acc_mlp = _mlp_bwd(x1, d, up, tgt2, mod8, pre_w_mlp, post_w_mlp, w_up_halves, w_down_halves, T)
    p_up = _matmul_tn_paired("grad_w_up", h2, dup, stream_a=False)
    p_down = _matmul_tn_paired("grad_w_down", u, dd, stream_a=True)
    dan, drg, dmix, acc_mix = _mix_bwd(mix, dx1, mod8, post_w_mix, w_out_f, T, [], [])
    gw_out = _matmul_tn("grad_w_out", cat, dmix, 512, tm=D_MODEL).reshape(N_DEV, D_MODEL // N_DEV, D_MODEL)
    dhq, dhf, dhi, dhg, dlb_p, dgw_p, r_down, r_out = _hgrn_bwd(
        proj_h, lb, hg_norm_w, rec_o, s_prev, drg.reshape(B, T, HG_WIDTH), [p_down, gw_out], ["chips", "a2a"])
    dqkv, dsink_p, daw_p, r_up = _attn_bwd(qr, kr, proj3, attn_o, dan.reshape(B, T, ATT_WIDTH), tables,
                                           attn_sinks, attn_out_w, [p_up], ["chips"])
    flat = lambda a: a.reshape(N, a.shape[-1])
    grad_x, dproj, acc_in = _in_bwd(x2, dx1, flat(dqkv), flat(dhq), flat(dhf), flat(dhi), flat(dhg),
                                    mod8, pre_w_mix, w_in_f, T, [], [])

    packed, dmod_blocks = _pack_small(acc_in, acc_mix, acc_mlp, dsink_p, daw_p, dlb_p, dgw_p, lb_p, ada_cols)
    r_in, r_dmod, r_small = _grad_w_in_reduced(dproj, h1, [dmod_blocks, packed], ["a2a", "gather"])

    res = {}
    (got,) = _reduce_adamw("adamw_w_in", [(r_in, w_in_t, m_w_in_t, v_w_in_t)], 1)
    res["w_in"] = tuple(a.T for a in got)
    res["w_out"], res["w_up"], res["w_down"] = _reduce_adamw(
        "adamw_matrices", [(r_out, w_out[0], m_w_out[0], v_w_out[0]), (r_up, w_up[0], m_w_up[0], v_w_up[0]),
                           (r_down, w_down[0], m_w_down[0], v_w_down[0])], 4)
    res["w_ada"] = _ada_grad_adamw(c_all, r_dmod.reshape(N_DEV * B, ada_cols), w_ada[0], m_w_ada[0], v_w_ada[0])

    given = dict(b_ada=(b_ada, m_b_ada, v_b_ada), pre_w_mix=(pre_w_mix, m_pre_w_mix, v_pre_w_mix),
                 attn_sinks=(attn_sinks, m_attn_sinks, v_attn_sinks),
                 attn_out_w=(attn_out_w, m_attn_out_w, v_attn_out_w), lb_table=(lb_table, m_lb_table, v_lb_table),
                 hg_norm_w=(hg_norm_w, m_hg_norm_w, v_hg_norm_w), post_w_mix=(post_w_mix, m_post_w_mix, v_post_w_mix),
                 pre_w_mlp=(pre_w_mlp, m_pre_w_mlp, v_pre_w_mlp), post_w_mlp=(post_w_mlp, m_post_w_mlp, v_post_w_mlp))
    small_res, loss = _adamw_small(r_small, given)
    res.update(small_res)

    order = ["w_ada", "b_ada", "pre_w_mix", "w_in", "attn_sinks", "attn_out_w", "lb_table", "hg_norm_w", "w_out",
             "post_w_mix", "pre_w_mlp", "w_up", "w_down", "post_w_mlp"]
    big = {"w_ada", "w_in", "w_out", "w_up", "w_down"}
    outs = [loss, grad_x.reshape(B, T, D_MODEL)]
    for i in range(4):
        for k in order:
            a = res[k][i]
            outs.append(a[None] if k in big else a)
    return tuple(outs)
```

```python
import jax
import jax.numpy as jnp
import numpy as np
from jax import lax
from jax.experimental import pallas as pl
from jax.experimental.pallas import tpu as pltpu

F32 = jnp.float32
BF16 = jnp.bfloat16
SDS = jax.ShapeDtypeStruct

D_MODEL = 1024
ATT_WIDTH = 512
ATT_HEAD_DIM = 64
ATT_KV_HEADS = 2
ATT_GROUP = 4
WINDOW = 128
ROPE_DIM = 16
ROPE_THETA = 500000.0
HG_WIDTH = 512
HG_HEAD_DIM = 128
HG_HEADS = 4
HG_CHUNK = 32
IN_COLS = 2816
ATT_COLS = 768
D_FF = 4096
EPS = 1e-6
N_DEV = 8

ADAM_LR = 0.001
ADAM_B1 = 0.9
ADAM_B2 = 0.999
ADAM_EPS = 1e-08
ADAM_WD = 0.01
ADAM_STEP = 10

VMEM_LIMIT_BIG = 56 << 20
LANES = 128

MESH = pl.DeviceIdType.MESH
NT_DIMS = (((1,), (1,)), ((), ()))
TN_DIMS = (((0,), (0,)), ((), ()))


def _dot(a, b):
    return jnp.dot(a, b, preferred_element_type=F32)


def _dot_nt(a, b):
    return lax.dot_general(a, b, NT_DIMS, preferred_element_type=F32)


def _dot_tn(a, b):
    return lax.dot_general(a, b, TN_DIMS, preferred_element_type=F32)


def _bf(a):
    return a.astype(BF16)


def _sigmoid(a):
    return 0.5 * jnp.tanh(0.5 * a) + 0.5


def _mean_last(a):
    return jnp.mean(a, axis=-1, keepdims=True)


def _sum_rows(a):
    return jnp.sum(a, axis=0, keepdims=True)


def _loop_pairs(first, count, body, init, per_trip=2):
    if count % per_trip:
        return lax.fori_loop(first, first + count, body, init)

    def trip(i, c):
        for k in range(per_trip):
            c = body(first + per_trip * i + k, c)
        return c

    return lax.fori_loop(0, count // per_trip, trip, init)


def _params(sem=None, vmem=None):
    kw = {}
    if sem is not None:
        kw["dimension_semantics"] = sem
    if vmem is not None:
        kw["vmem_limit_bytes"] = vmem
    return pltpu.CompilerParams(**kw)


ANY_SPEC = pl.BlockSpec(memory_space=pl.ANY)


def _exchange_shapes(srcs, modes):
    out_shape = []
    for s, m in zip(srcs, modes):
        shp = (N_DEV,) + tuple(s.shape) if m == "gather" else tuple(s.shape)
        out_shape.append(SDS(shp, s.dtype))
    return out_shape


def _exchange_sems(n):
    if n == 0:
        return []
    return [pltpu.SemaphoreType.DMA((n, N_DEV - 1)), pltpu.SemaphoreType.DMA((n, N_DEV - 1)),
            pltpu.SemaphoreType.DMA((n,))]


SIBLING = 1
OTHER_CHIPS = (2, 4, 6)


def _related(k):
    x, y, c = lax.axis_index("x"), lax.axis_index("y"), lax.axis_index("c")
    px, py, pc = x ^ ((k >> 2) & 1), y ^ ((k >> 1) & 1), c ^ (k & 1)
    return (px, py, pc), 4 * px + 2 * py + pc


def _exchange_phases(modes, src_refs, out_refs, send_sems, recv_sems, own_sems):
    _, me = _related(0)
    sib_dev, sib = _related(SIBLING)
    start, middle, end = [], [], []

    def remote(a, i, src, dst, dev):
        return pltpu.make_async_remote_copy(src_ref=src, dst_ref=dst, send_sem=send_sems.at[a, i],
                                            recv_sem=recv_sems.at[a, i], device_id=dev, device_id_type=MESH)

    for a, mode in enumerate(modes):
        out = out_refs[a]
        if mode == "gather":
            src = src_refs[a]
            own = pltpu.make_async_copy(src, out.at[me], own_sems.at[a])
            to_sib = remote(a, 0, src, out.at[me], sib_dev)
            start += [own.start, to_sib.start]
            end += [remote(a, 0, src, out.at[sib], sib_dev).wait_recv, to_sib.wait_send, own.wait]
            for j, k in enumerate(OTHER_CHIPS, start=1):
                dev, peer = _related(k)
                _, peer_sib = _related(k ^ SIBLING)
                send = remote(a, j, src, out.at[me], dev)
                passed = remote(a, 3 + j, out.at[peer], out.at[peer], sib_dev)
                start.append(send.start)
                middle += [remote(a, j, src, out.at[peer], dev).wait_recv, passed.start]
                end += [remote(a, 3 + j, out.at[peer_sib], out.at[peer_sib], sib_dev).wait_recv,
                        send.wait_send, passed.wait_send]
        elif mode == "chips":
            chip = me // 2
            own = pltpu.make_async_copy(src_refs[a].at[chip], out.at[chip], own_sems.at[a])
            start.append(own.start)
            end.append(own.wait)
            for j, k in enumerate(OTHER_CHIPS, start=1):
                dev, peer = _related(k)
                send = remote(a, j, src_refs[a].at[peer // 2], out.at[chip], dev)
                start.append(send.start)
                end += [remote(a, j, src_refs[a].at[peer // 2], out.at[peer // 2], dev).wait_recv, send.wait_send]
        else:
            own = pltpu.make_async_copy(src_refs[a].at[me], out.at[me], own_sems.at[a])
            start.append(own.start)
            end.append(own.wait)
            for k in range(1, N_DEV):
                dev, peer = _related(k)
                send = remote(a, k - 1, src_refs[a].at[peer], out.at[me], dev)
                start.append(send.start)
                end += [remote(a, k - 1, src_refs[a].at[peer], out.at[peer], dev).wait_recv, send.wait_send]
    return start, middle, end


def _run(actions):
    for act in actions:
        act()


def _exchange(name, srcs, modes):
    n = len(srcs)

    def body(*refs):
        start, middle, end = _exchange_phases(modes, refs[:n], refs[n:2 * n], *refs[2 * n:])
        _run(start)
        _run(middle)
        _run(end)

    return pl.pallas_call(
        body, name=name, out_shape=_exchange_shapes(srcs, modes),
        in_specs=[ANY_SPEC] * n, out_specs=[ANY_SPEC] * n,
        scratch_shapes=_exchange_sems(n),
    )(*srcs)


def _ride_start(modes, step, steps, src_refs, out_refs, sems):
    if not modes:
        return
    middle_step = steps - 1

    @pl.when(step == 0)
    def _():
        _run(_exchange_phases(modes, src_refs, out_refs, *sems)[0])

    if "gather" in modes:
        @pl.when(step == middle_step)
        def _():
            _run(_exchange_phases(modes, src_refs, out_refs, *sems)[1])


def _ride_wait(modes, step, steps, src_refs, out_refs, sems):
    if not modes:
        return

    @pl.when(step == steps - 1)
    def _():
        _run(_exchange_phases(modes, src_refs, out_refs, *sems)[2])


def _ada_mod(c_all, w_ada, b_ada_mine):
    nb, cols = c_all.shape[0], w_ada.shape[1]

    def body(c_ref, w_ref, b_ref, o_ref):
        cv = c_ref[...]
        ca = cv * _sigmoid(cv)
        o_ref[...] = _dot(ca, w_ref[...]) + b_ref[...]

    return pl.pallas_call(body, name="ada_mod", out_shape=SDS((nb, cols), F32))(c_all, w_ada, b_ada_mine)


def _tile_rows(T, big=False):
    return min(512 if big else 256, T)


def _mod_spec(tps):
    return pl.BlockSpec((None, 8, D_MODEL), lambda i: (i // tps, 0, 0))


def _in_proj(x2, mod8, pre_w, w_in_bf, T, ride_srcs, ride_modes):
    N = x2.shape[0]
    TM = _tile_rows(T, big=True)
    tps = T // TM
    nr = len(ride_srcs)

    def body(*refs):
        x_ref, mod_ref, pw_ref, w_ref = refs[:4]
        ride_in = refs[4:4 + nr]
        pa_ref, ph_ref, h1_ref = refs[4 + nr:7 + nr]
        ride_out = refs[7 + nr:7 + 2 * nr]
        sems = refs[7 + 2 * nr:]
        _ride_start(ride_modes, pl.program_id(0), N // TM, ride_in, ride_out, sems)
        x = x_ref[...]
        r = lax.rsqrt(_mean_last(x * x) + EPS)
        h = (x * r * pw_ref[...]) * (1.0 + mod_ref[1:2, :]) + mod_ref[0:1, :]
        hb = _bf(h)
        h1_ref[...] = hb
        pa_ref[...] = _dot_nt(hb, w_ref[:ATT_COLS, :])
        ph_ref[...] = _dot_nt(hb, w_ref[ATT_COLS:, :])
        _ride_wait(ride_modes, pl.program_id(0), N // TM, ride_in, ride_out, sems)

    return pl.pallas_call(
        body, name="in_proj", grid=(N // TM,),
        in_specs=[pl.BlockSpec((TM, D_MODEL), lambda i: (i, 0)), _mod_spec(tps),
                  pl.BlockSpec((1, D_MODEL), lambda i: (0, 0)),
                  pl.BlockSpec((IN_COLS, D_MODEL), lambda i: (0, 0))] + [ANY_SPEC] * nr,
        out_specs=[pl.BlockSpec((TM, ATT_COLS), lambda i: (i, 0)),
                   pl.BlockSpec((TM, IN_COLS - ATT_COLS), lambda i: (i, 0)),
                   pl.BlockSpec((TM, D_MODEL), lambda i: (i, 0))] + [ANY_SPEC] * nr,
        out_shape=[SDS((N, ATT_COLS), F32), SDS((N, IN_COLS - ATT_COLS), F32), SDS((N, D_MODEL), BF16)]
        + _exchange_shapes(ride_srcs, ride_modes),
        scratch_shapes=_exchange_sems(nr),
        compiler_params=_params(("arbitrary",), VMEM_LIMIT_BIG),
    )(x2, mod8, pre_w, w_in_bf, *ride_srcs)


def _rope_tables(T):
    half = ROPE_DIM // 2
    f32 = np.float32
    inv_freq = (f32(ROPE_THETA) ** (-np.arange(0, ROPE_DIM, 2, dtype=f32) / f32(ROPE_DIM))).astype(f32)
    ang = np.arange(T, dtype=f32)[:, None] * inv_freq[None, :]
    cos, sin = np.cos(ang).astype(f32), np.sin(ang).astype(f32)
    ones = np.ones((T, ATT_HEAD_DIM - ROPE_DIM), f32)
    zeros = np.zeros((T, ATT_HEAD_DIM - ROPE_DIM), f32)
    zh = np.zeros((T, half), f32)
    cos64 = np.concatenate([cos, cos, ones], axis=1)
    sin_left = np.concatenate([-sin, zh, zeros], axis=1)
    sin_right = np.concatenate([zh, sin, zeros], axis=1)
    rep = LANES // ATT_HEAD_DIM
    return tuple(jnp.asarray(np.tile(t, (1, rep))) for t in (cos64, sin_left, sin_right))


def _rope(xc, cs, sl, sr):
    return xc * cs + pltpu.roll(xc, LANES - 8, 1) * sl + pltpu.roll(xc, 8, 1) * sr


def _rope_t(dy, cs, sl, sr):
    return dy * cs + pltpu.roll(dy * sl, 8, 1) + pltpu.roll(dy * sr, LANES - 8, 1)


ATT_SCALE = ATT_HEAD_DIM ** -0.5
ATT_SPLITS = 4


def _lower_mask():
    j = lax.broadcasted_iota(jnp.int32, (WINDOW, ATT_GROUP * WINDOW), 0)
    i = lax.broadcasted_iota(jnp.int32, (WINDOW, ATT_GROUP * WINDOW), 1) & (WINDOW - 1)
    return j <= i


def _sink_row(sink_ref, hk):
    return jnp.concatenate(
        [jnp.full((1, WINDOW), sink_ref[0, ATT_GROUP * hk + g], F32) for g in range(ATT_GROUP)], axis=1)


def _softmax_window(qs, k_cur, k_prev, lower, has_prev, sink):
    s_prev = jnp.where(has_prev, _dot_nt(k_prev, qs), jnp.finfo(F32).min)
    s = jnp.where(lower, _dot_nt(k_cur, qs), s_prev)
    m = jnp.maximum(jnp.max(s, axis=0, keepdims=True), sink)
    p = jnp.exp(s - m)
    es = jnp.exp(sink - m)
    inv = 1.0 / (jnp.sum(p, axis=0, keepdims=True) + es)
    return p, inv, es


def _stack_heads(parts, hk):
    hs = []
    for g in range(ATT_GROUP):
        h = ATT_GROUP * hk + g
        hs.append(parts[h // 2][:, (h % 2) * ATT_HEAD_DIM:(h % 2 + 1) * ATT_HEAD_DIM])
    return jnp.concatenate(hs, axis=0)


def _attn_fwd(proj3, tables, sinks, attn_w, ride_srcs, ride_modes):
    B, T, _ = proj3.shape
    nb = T // WINDOW
    splits = min(ATT_SPLITS, nb)
    per = nb // splits
    nr = len(ride_srcs)
    cos, sinl, sinr = tables

    def body(*refs):
        q_ref, k_ref, v_ref, cos_ref, sl_ref, sr_ref, sink_ref, aw_ref = refs[:8]
        ride_in = refs[8:8 + nr]
        o_ref, an_ref, qr_ref, kr_ref = refs[8 + nr:12 + nr]
        ride_out = refs[12 + nr:12 + 2 * nr]
        kpad, vpad = refs[12 + 2 * nr:14 + 2 * nr]
        sems = refs[14 + 2 * nr:]
        part = pl.program_id(1)
        step = pl.program_id(0) * splits + part
        _ride_start(ride_modes, step, B * splits, ride_in, ride_out, sems)

        @pl.when(part == 0)
        def _():
            kpad[0:WINDOW, :] = jnp.zeros((WINDOW, LANES), BF16)
            vpad[0:WINDOW, :] = jnp.zeros((WINDOW, LANES), BF16)

        lower = _lower_mask()

        def block(n, carry):
            r0 = pl.multiple_of(n * WINDOW, WINDOW)
            rows = pl.ds(r0, WINDOW)
            nxt = pl.ds(r0 + WINDOW, WINDOW)
            cs, sl, sr = cos_ref[rows, :], sl_ref[rows, :], sr_ref[rows, :]
            kb = _bf(_rope(k_ref[rows, :], cs, sl, sr))
            vb = _bf(v_ref[rows, :])
            kpad[nxt, :] = kb
            kr_ref[rows, :] = kb
            vpad[nxt, :] = vb
            qparts = []
            for j in range(ATT_WIDTH // LANES):
                qp = _bf(_rope(q_ref[rows, j * LANES:(j + 1) * LANES], cs, sl, sr) * ATT_SCALE)
                qr_ref[rows, j * LANES:(j + 1) * LANES] = qp
                qparts.append(qp)
            for hk in range(ATT_KV_HEADS):
                lanes = slice(hk * ATT_HEAD_DIM, (hk + 1) * ATT_HEAD_DIM)
                qs = _stack_heads(qparts, hk)
                p, inv, _ = _softmax_window(qs, kb[:, lanes], kpad[rows, lanes], lower, n > 0,
                                            _sink_row(sink_ref, hk))
                p_cur = jnp.where(lower, p, 0.0)
                ot = (_dot_tn(vb[:, lanes], _bf(p_cur)) + _dot_tn(vpad[rows, lanes], _bf(p - p_cur))) * inv
                for g in range(ATT_GROUP):
                    h = ATT_GROUP * hk + g
                    o_ref[rows, h * ATT_HEAD_DIM:(h + 1) * ATT_HEAD_DIM] = ot[:, g * WINDOW:(g + 1) * WINDOW].T
            ob = o_ref[rows, :]
            an_ref[rows, :] = _bf(ob * lax.rsqrt(_mean_last(ob * ob) + EPS) * aw_ref[...])
            return carry

        _loop_pairs(part * per, per, block, 0)
        _ride_wait(ride_modes, step, B * splits, ride_in, ride_out, sems)

    seq = lambda w, j: pl.BlockSpec((None, T, w), lambda b, s: (b, 0, j))
    full = lambda r, w: pl.BlockSpec((r, w), lambda b, s: (0, 0))
    return pl.pallas_call(
        body, name="attn_fwd", grid=(B, splits),
        in_specs=[seq(ATT_WIDTH, 0), seq(LANES, 4), seq(LANES, 5),
                  full(T, LANES), full(T, LANES), full(T, LANES),
                  pl.BlockSpec(memory_space=pltpu.SMEM), full(1, ATT_WIDTH)] + [ANY_SPEC] * nr,
        out_specs=[seq(ATT_WIDTH, 0), seq(ATT_WIDTH, 0), seq(ATT_WIDTH, 0), seq(LANES, 0)] + [ANY_SPEC] * nr,
        out_shape=[SDS((B, T, ATT_WIDTH), F32), SDS((B, T, ATT_WIDTH), BF16),
                   SDS((B, T, ATT_WIDTH), BF16), SDS((B, T, LANES), BF16)] + _exchange_shapes(ride_srcs, ride_modes),
        scratch_shapes=[pltpu.VMEM((T + WINDOW, LANES), BF16), pltpu.VMEM((T + WINDOW, LANES), BF16)]
        + _exchange_sems(nr),
        compiler_params=_params(("arbitrary", "arbitrary"), VMEM_LIMIT_BIG),
    )(proj3, proj3, proj3, cos, sinl, sinr, sinks, attn_w, *ride_srcs)


HG_GROUP = 8
HG_ROWS = HG_GROUP * HG_CHUNK


HG_STACK = HG_GROUP * HG_HEAD_DIM


def _group_mask():
    r = lax.broadcasted_iota(jnp.int32, (HG_ROWS, HG_ROWS), 0)
    c = lax.broadcasted_iota(jnp.int32, (HG_ROWS, HG_ROWS), 1)
    return ((r // HG_CHUNK) == (c // HG_CHUNK)) & (r >= c)


def _spread(a):
    blocks = []
    for c in range(HG_GROUP):
        above = jnp.zeros((c * HG_CHUNK, HG_HEAD_DIM), a.dtype)
        below = jnp.zeros(((HG_GROUP - 1 - c) * HG_CHUNK, HG_HEAD_DIM), a.dtype)
        blocks.append(jnp.concatenate([p for p in (above, a[_chunk_rows(c), :], below) if p.shape[0]], axis=0))
    return jnp.concatenate(blocks, axis=1)


def _pick(r):
    return jnp.concatenate([r[_chunk_rows(c), c * HG_HEAD_DIM:(c + 1) * HG_HEAD_DIM] for c in range(HG_GROUP)], axis=0)


def _lane_block(a, c):
    return a[:, c * HG_HEAD_DIM:(c + 1) * HG_HEAD_DIM]


def _chunk_cumsum(a, reverse=False):
    n = a.shape[0]
    pos = lax.broadcasted_iota(jnp.int32, a.shape, 0) % HG_CHUNK
    shift = 1
    while shift < HG_CHUNK:
        if reverse:
            a = a + jnp.where(pos < HG_CHUNK - shift, pltpu.roll(a, n - shift, 0), 0.0)
        else:
            a = a + jnp.where(pos >= shift, pltpu.roll(a, shift, 0), 0.0)
        shift *= 2
    return a


def _chunk_bcast(rows_1x128):
    return jnp.concatenate([jnp.broadcast_to(r, (HG_CHUNK, HG_HEAD_DIM)) for r in rows_1x128], axis=0)


def _hgrn_gates(hq, hf, lb):
    sq = _sigmoid(hq)
    q = hq * sq
    sg = _sigmoid(hf)
    f = lb + (1.0 - lb) * sg
    k = 1.0 - f
    logf = jnp.log(f)
    b = _chunk_cumsum(logf)
    bl = [_sum_rows(logf[_chunk_rows(c), :]) for c in range(HG_GROUP)]
    eb, enb, e2 = jnp.exp(b), jnp.exp(-b), jnp.exp(_chunk_bcast(bl) - b)
    ebl = [jnp.exp(r) for r in bl]
    return dict(sq=sq, sg=sg, f=f, eb=eb, enb=enb, e2=e2, ebl=ebl, qd=q * eb, kd=k * enb, k2=k * e2)


def _chunk_rows(c):
    return slice(c * HG_CHUNK, (c + 1) * HG_CHUNK)


def _head_lanes(h):
    return slice(h * HG_HEAD_DIM, (h + 1) * HG_HEAD_DIM)


def _hgrn_fwd(proj_h, lb, hg_w, ride_srcs, ride_modes):
    B, T, _ = proj_h.shape
    ng = T // HG_ROWS
    nr = len(ride_srcs)

    def body(*refs):
        hq_ref, hf_ref, hi_ref, hg_ref, lb_ref, gw_ref = refs[:6]
        ride_in = refs[6:6 + nr]
        o_ref, rg_ref, sp_ref = refs[6 + nr:9 + nr]
        ride_out = refs[9 + nr:9 + 2 * nr]
        st = refs[9 + 2 * nr]
        sems = refs[10 + 2 * nr:]
        gi = pl.program_id(1)
        step = pl.program_id(0) * ng + gi
        _ride_start(ride_modes, step, B * ng, ride_in, ride_out, sems)

        @pl.when(gi == 0)
        def _():
            st[...] = jnp.zeros(st.shape, F32)

        lo = _group_mask()
        for h in range(HG_HEADS):
            lanes = _head_lanes(h)
            gt = _hgrn_gates(hq_ref[:, lanes], hf_ref[:, lanes], lb_ref[:, lanes])
            v, qd, kd = _bf(hi_ref[:, lanes]), _bf(gt["qd"]), _bf(gt["kd"])
            a = jnp.where(lo, _dot_nt(qd, kd), 0.0)
            kv = _dot_tn(v, _spread(_bf(gt["k2"])))
            s = st[h]
            before = []
            for c in range(HG_GROUP):
                before.append(s)
                s = s * gt["ebl"][c] + _lane_block(kv, c)
            st[h] = s
            sp = jnp.concatenate(before, axis=1)
            sp_ref[h] = sp
            o = _dot(_bf(a), v) + _dot_nt(_spread(qd), _bf(sp))
            o_ref[:, lanes] = o
            hg = hg_ref[:, lanes]
            rn = o * lax.rsqrt(_mean_last(o * o) + EPS) * gw_ref[...]
            rg_ref[:, lanes] = _bf(rn * (hg * _sigmoid(hg)))
        _ride_wait(ride_modes, step, B * ng, ride_in, ride_out, sems)

    part = lambda j: pl.BlockSpec((None, HG_ROWS, HG_WIDTH), lambda b, g: (b, g, j))
    return pl.pallas_call(
        body, name="hgrn_fwd", grid=(B, ng),
        in_specs=[part(0), part(1), part(2), part(3),
                  pl.BlockSpec((1, HG_WIDTH), lambda b, g: (0, 0)),
                  pl.BlockSpec((1, LANES), lambda b, g: (0, 0))] + [ANY_SPEC] * nr,
        out_specs=[part(0), part(0),
                   pl.BlockSpec((None, HG_HEADS, None, HG_HEAD_DIM, HG_STACK), lambda b, g: (b, 0, g, 0, 0))]
        + [ANY_SPEC] * nr,
        out_shape=[SDS((B, T, HG_WIDTH), F32), SDS((B, T, HG_WIDTH), BF16),
                   SDS((B, HG_HEADS, ng, HG_HEAD_DIM, HG_STACK), F32)] + _exchange_shapes(ride_srcs, ride_modes),
        scratch_shapes=[pltpu.VMEM((HG_HEADS, HG_HEAD_DIM, HG_HEAD_DIM), F32)] + _exchange_sems(nr),
        compiler_params=_params(("arbitrary", "arbitrary"), VMEM_LIMIT_BIG),
    )(proj_h, proj_h, proj_h, proj_h, lb, hg_w, *ride_srcs)


def _mix_out(x2, attn_n, rec_g, mod8, post_w, w_out_bf, T, ride_srcs, ride_modes):
    N = x2.shape[0]
    TM = _tile_rows(T, big=True)
    tps = T // TM
    nr = len(ride_srcs)

    def body(*refs):
        x_ref, an_ref, rg_ref, mod_ref, pw_ref, w_ref = refs[:6]
        ride_in = refs[6:6 + nr]
        mix_ref, x1_ref, cat_ref = refs[6 + nr:9 + nr]
        ride_out = refs[9 + nr:9 + 2 * nr]
        sems = refs[9 + 2 * nr:]
        _ride_start(ride_modes, pl.program_id(0), N // TM, ride_in, ride_out, sems)
        cat = jnp.concatenate([an_ref[...], rg_ref[...]], axis=1)
        cat_ref[...] = cat
        mix = _dot(cat, w_ref[...])
        mix_ref[...] = mix
        r = lax.rsqrt(_mean_last(mix * mix) + EPS)
        x1_ref[...] = x_ref[...] + mod_ref[2:3, :] * (mix * r * pw_ref[...])
        _ride_wait(ride_modes, pl.program_id(0), N // TM, ride_in, ride_out, sems)

    row = lambda w: pl.BlockSpec((TM, w), lambda i: (i, 0))
    return pl.pallas_call(
        body, name="mix_out", grid=(N // TM,),
        in_specs=[row(D_MODEL), row(ATT_WIDTH), row(HG_WIDTH), _mod_spec(tps),
                  pl.BlockSpec((1, D_MODEL), lambda i: (0, 0)),
                  pl.BlockSpec((D_MODEL, D_MODEL), lambda i: (0, 0))] + [ANY_SPEC] * nr,
        out_specs=[row(D_MODEL), row(D_MODEL), row(D_MODEL)] + [ANY_SPEC] * nr,
        out_shape=[SDS((N, D_MODEL), F32), SDS((N, D_MODEL), F32), SDS((N, D_MODEL), BF16)]
        + _exchange_shapes(ride_srcs, ride_modes),
        scratch_shapes=_exchange_sems(nr),
        compiler_params=_params(("arbitrary",), VMEM_LIMIT_BIG),
    )(x2, attn_n, rec_g, mod8, post_w, w_out_bf, *ride_srcs)


def _load_weights_once(pairs, sem):
    @pl.when(pl.program_id(0) == 0)
    def _():
        cps = [pltpu.make_async_copy(src, dst, sem.at[i]) for i, (src, dst) in enumerate(pairs)]
        for cp in cps:
            cp.start()
        for cp in cps:
            cp.wait()


MLP_HALF = D_MODEL // 2
MLP_PIECES = 2 * N_DEV + 2


def _mlp_weight_pieces(wu_a, wu_b, wd_a, wd_b, wu, wd):
    cols = D_FF // N_DEV
    pairs = []
    for h, half in enumerate((wu_a, wu_b)):
        for j in range(N_DEV):
            pairs.append((half.at[j], wu.at[pl.ds(h * MLP_HALF, MLP_HALF), pl.ds(j * cols, cols)]))
    col = 0
    for part in (wd_a, wd_b):
        pairs.append((part, wd.at[:, pl.ds(col, part.shape[1])]))
        col += part.shape[1]
    return pairs


def _mlp_fwd(x1, mod8, pre_w, w_up_halves, w_down_halves, T):
    N = x1.shape[0]
    TM = _tile_rows(T)
    tps = T // TM

    def body(x_ref, mod_ref, pw_ref, wua, wub, wda, wdb, up_ref, u_ref, d_ref, h2_ref, wu, wd, sem):
        _load_weights_once(_mlp_weight_pieces(wua, wub, wda, wdb, wu, wd), sem)
        x = x_ref[...]
        r = lax.rsqrt(_mean_last(x * x) + EPS)
        h = (x * r * pw_ref[...]) * (1.0 + mod_ref[4:5, :]) + mod_ref[3:4, :]
        hb = _bf(h)
        h2_ref[...] = hb
        up = _dot(hb, wu[...])
        up_ref[...] = up
        ru = jnp.maximum(up, 0.0)
        u = _bf(ru * ru)
        u_ref[...] = u
        d_ref[...] = _dot(u, wd[...])

    row = lambda w: pl.BlockSpec((TM, w), lambda i: (i, 0))
    return pl.pallas_call(
        body, name="mlp_fwd", grid=(N // TM,),
        in_specs=[row(D_MODEL), _mod_spec(tps), pl.BlockSpec((1, D_MODEL), lambda i: (0, 0))] + [ANY_SPEC] * 4,
        out_specs=[row(D_FF), row(D_FF), row(D_MODEL), row(D_MODEL)],
        out_shape=[SDS((N, D_FF), F32), SDS((N, D_FF), BF16), SDS((N, D_MODEL), F32), SDS((N, D_MODEL), BF16)],
        scratch_shapes=[pltpu.VMEM((D_MODEL, D_FF), BF16), pltpu.VMEM((D_FF, D_MODEL), BF16),
                        pltpu.SemaphoreType.DMA((MLP_PIECES,))],
        compiler_params=_params(("arbitrary",), VMEM_LIMIT_BIG),
    )(x1, mod8, pre_w, *w_up_halves, *w_down_halves)


def _acc_rows(acc_ref, first, rows):
    @pl.when(first)
    def _():
        acc_ref[...] = jnp.zeros(acc_ref.shape, F32)
    for i, r in enumerate(rows):
        acc_ref[i:i + 1, :] += r


def _mlp_bwd(x1, d, up, tgt, mod8, pre_w, post_w, w_up_halves, w_down_halves, T):
    N = x1.shape[0]
    TM = _tile_rows(T)
    tps = T // TM

    def body(x_ref, d_ref, up_ref, t_ref, mod_ref, pw_ref, qw_ref, wua, wub, wda, wdb,
             dx_ref, dup_ref, dd_ref, acc_ref, wd, wu, sem):
        _load_weights_once(_mlp_weight_pieces(wua, wub, wda, wdb, wu, wd), sem)
        sh2, sc2, g2 = mod_ref[3:4, :], mod_ref[4:5, :], mod_ref[5:6, :]
        x = x_ref[...]
        r1 = lax.rsqrt(_mean_last(x * x) + EPS)
        xh = x * r1
        n2 = xh * pw_ref[...]
        dv = d_ref[...]
        rd = lax.rsqrt(_mean_last(dv * dv) + EPS)
        dh = dv * rd
        rr = dh * qw_ref[...]
        e = x + g2 * rr - t_ref[...]
        loss = 0.5 * jnp.sum(_sum_rows(e * e), axis=1, keepdims=True) / D_MODEL
        dy = e * (1.0 / D_MODEL)
        dg2 = _sum_rows(dy * rr)
        drr = dy * g2
        dw_post = _sum_rows(drr * dh)
        ddh = drr * qw_ref[...]
        dd = _bf(rd * (ddh - dh * _mean_last(ddh * dh)))
        dd_ref[...] = dd
        ru = jnp.maximum(up_ref[...], 0.0)
        dup = _bf(_dot_nt(dd, wd[...]) * (2.0 * ru))
        dup_ref[...] = dup
        dh2 = _dot_nt(dup, wu[...])
        dsh2 = _sum_rows(dh2)
        dsc2 = _sum_rows(dh2 * n2)
        dn2 = dh2 * (1.0 + sc2)
        dw_pre = _sum_rows(dn2 * xh)
        dxh = dn2 * pw_ref[...]
        dx_ref[...] = dy + r1 * (dxh - xh * _mean_last(dxh * xh))
        _acc_rows(acc_ref, pl.program_id(0) % tps == 0,
                  [dsh2, dsc2, dg2, dw_pre, dw_post, jnp.broadcast_to(loss, (1, D_MODEL))])

    row = lambda w: pl.BlockSpec((TM, w), lambda i: (i, 0))
    vec = pl.BlockSpec((1, D_MODEL), lambda i: (0, 0))
    B = N // T
    return pl.pallas_call(
        body, name="mlp_bwd", grid=(N // TM,),
        in_specs=[row(D_MODEL), row(D_MODEL), row(D_FF), row(D_MODEL), _mod_spec(tps), vec, vec] + [ANY_SPEC] * 4,
        out_specs=[row(D_MODEL), row(D_FF), row(D_MODEL), _mod_spec(tps)],
        out_shape=[SDS((N, D_MODEL), F32), SDS((N, D_FF), BF16), SDS((N, D_MODEL), BF16),
                   SDS((B, 8, D_MODEL), F32)],
        scratch_shapes=[pltpu.VMEM((D_FF, D_MODEL), BF16), pltpu.VMEM((D_MODEL, D_FF), BF16),
                        pltpu.SemaphoreType.DMA((MLP_PIECES,))],
        compiler_params=_params(("arbitrary",), VMEM_LIMIT_BIG),
    )(x1, d, up, tgt, mod8, pre_w, post_w, *w_up_halves, *w_down_halves)


def _mix_bwd(mix, dx1, mod8, post_w, w_out_bf, T, ride_srcs, ride_modes):
    N = mix.shape[0]
    TM = _tile_rows(T, big=True)
    tps = T // TM
    nr = len(ride_srcs)

    def body(*refs):
        mix_ref, dx_ref, mod_ref, pw_ref, w_ref = refs[:5]
        ride_in = refs[5:5 + nr]
        dan_ref, drg_ref, dmix_ref, acc_ref = refs[5 + nr:9 + nr]
        ride_out = refs[9 + nr:9 + 2 * nr]
        sems = refs[9 + 2 * nr:]
        _ride_start(ride_modes, pl.program_id(0), N // TM, ride_in, ride_out, sems)
        g1 = mod_ref[2:3, :]
        mix = mix_ref[...]
        dx1 = dx_ref[...]
        rm = lax.rsqrt(_mean_last(mix * mix) + EPS)
        mh = mix * rm
        dg1 = _sum_rows(dx1 * (mh * pw_ref[...]))
        dr = dx1 * g1
        dw_post = _sum_rows(dr * mh)
        dmh = dr * pw_ref[...]
        dmix = _bf(rm * (dmh - mh * _mean_last(dmh * mh)))
        dmix_ref[...] = dmix
        dcat = _dot_nt(dmix, w_ref[...])
        dan_ref[...] = dcat[:, :ATT_WIDTH]
        drg_ref[...] = dcat[:, ATT_WIDTH:]
        _acc_rows(acc_ref, pl.program_id(0) % tps == 0, [dg1, dw_post])
        _ride_wait(ride_modes, pl.program_id(0), N // TM, ride_in, ride_out, sems)

    row = lambda w: pl.BlockSpec((TM, w), lambda i: (i, 0))
    B = N // T
    return pl.pallas_call(
        body, name="mix_bwd", grid=(N // TM,),
        in_specs=[row(D_MODEL), row(D_MODEL), _mod_spec(tps), pl.BlockSpec((1, D_MODEL), lambda i: (0, 0)),
                  pl.BlockSpec((D_MODEL, D_MODEL), lambda i: (0, 0))] + [ANY_SPEC] * nr,
        out_specs=[row(ATT_WIDTH), row(HG_WIDTH), row(D_MODEL), _mod_spec(tps)] + [ANY_SPEC] * nr,
        out_shape=[SDS((N, ATT_WIDTH), F32), SDS((N, HG_WIDTH), F32), SDS((N, D_MODEL), BF16),
                   SDS((B, 8, D_MODEL), F32)] + _exchange_shapes(ride_srcs, ride_modes),
        scratch_shapes=_exchange_sems(nr),
        compiler_params=_params(("arbitrary",), VMEM_LIMIT_BIG),
    )(mix, dx1, mod8, post_w, w_out_bf, *ride_srcs)


def _hgrn_bwd(proj_h, lb, hg_w, o, s_prev, drg, ride_srcs, ride_modes):
    B, T, _ = proj_h.shape
    ng = T // HG_ROWS
    nr = len(ride_srcs)

    def body(*refs):
        hq_ref, hf_ref, hi_ref, hg_ref, lb_ref, gw_ref, o_ref, sp_ref, drg_ref = refs[:9]
        ride_in = refs[9:9 + nr]
        dhq_ref, dhf_ref, dhi_ref, dhg_ref, dlb_ref, dgw_ref = refs[9 + nr:15 + nr]
        ride_out = refs[15 + nr:15 + 2 * nr]
        dst = refs[15 + 2 * nr]
        sems = refs[16 + 2 * nr:]
        step = pl.program_id(0) * ng + pl.program_id(1)
        _ride_start(ride_modes, step, B * ng, ride_in, ride_out, sems)

        @pl.when(pl.program_id(1) == 0)
        def _():
            dst[...] = jnp.zeros(dst.shape, F32)
            dlb_ref[...] = jnp.zeros(dlb_ref.shape, F32)
            dgw_ref[...] = jnp.zeros(dgw_ref.shape, F32)

        lo = _group_mask()
        gw = gw_ref[...]

        for h in range(HG_HEADS):
            lanes = _head_lanes(h)
            lbv = lb_ref[:, lanes]
            hq = hq_ref[:, lanes]
            gt = _hgrn_gates(hq, hf_ref[:, lanes], lbv)
            sq, sg, qdf, kdf, k2f, ebl = gt["sq"], gt["sg"], gt["qd"], gt["kd"], gt["k2"], gt["ebl"]
            v, qd, kd = _bf(hi_ref[:, lanes]), _bf(qdf), _bf(kdf)
            ov = o_ref[:, lanes]
            hg = hg_ref[:, lanes]
            shg = _sigmoid(hg)
            dr = drg_ref[:, lanes]
            ro = lax.rsqrt(_mean_last(ov * ov) + EPS)
            oh = ov * ro
            dhg_ref[:, lanes] = _bf(dr * (oh * gw) * (shg + hg * shg * (1.0 - shg)))
            drn = dr * (hg * shg)
            dgw_ref[...] += jnp.broadcast_to(_sum_rows(drn * oh), (8, LANES))
            doh = drn * gw
            do = _bf(ro * (doh - oh * _mean_last(doh * oh)))
            a = jnp.where(lo, _dot_nt(qd, kd), 0.0)
            da = _bf(jnp.where(lo, _dot_nt(do, v), 0.0))
            dv = _dot_tn(_bf(a), do)
            dqd = _dot(da, kd)
            dkd = _dot_tn(da, qd)
            sp = sp_ref[h]
            incr = _dot_tn(do, _spread(qd))
            ds = dst[h]
            after = [None] * HG_GROUP
            for c in reversed(range(HG_GROUP)):
                after[c] = ds
                ds = ds * ebl[c] + _lane_block(incr, c)
            dst[h] = ds
            dss = jnp.concatenate(after, axis=1)
            dssb = _bf(dss)
            dk2 = _pick(_dot(v, dssb))
            dhi_ref[:, lanes] = _bf(dv + _dot_nt(_spread(_bf(k2f)), dssb))
            dqd = dqd + _pick(_dot(do, _bf(sp)))
            debl = _sum_rows(dss * sp)
            k2g = dk2 * k2f
            db = dqd * qdf - dkd * kdf - k2g
            dk = dkd * gt["enb"] + dk2 * gt["e2"]
            dbl = _chunk_bcast([_lane_block(debl, c) * ebl[c] + _sum_rows(k2g[_chunk_rows(c), :])
                                for c in range(HG_GROUP)])
            dg = _chunk_cumsum(db, reverse=True) + dbl
            df = dg / gt["f"] - dk
            dhf_ref[:, lanes] = _bf(df * (1.0 - lbv) * sg * (1.0 - sg))
            dlb_ref[:, lanes] += jnp.broadcast_to(_sum_rows(df * (1.0 - sg)), (8, LANES))
            dhq_ref[:, lanes] = _bf((dqd * gt["eb"]) * (sq + hq * sq * (1.0 - sq)))
        _ride_wait(ride_modes, step, B * ng, ride_in, ride_out, sems)

    part = lambda j: pl.BlockSpec((None, HG_ROWS, HG_WIDTH), lambda b, g: (b, ng - 1 - g, j))
    return pl.pallas_call(
        body, name="hgrn_bwd", grid=(B, ng),
        in_specs=[part(0), part(1), part(2), part(3),
                  pl.BlockSpec((1, HG_WIDTH), lambda b, g: (0, 0)),
                  pl.BlockSpec((1, LANES), lambda b, g: (0, 0)),
                  part(0),
                  pl.BlockSpec((None, HG_HEADS, None, HG_HEAD_DIM, HG_STACK), lambda b, g: (b, 0, ng - 1 - g, 0, 0)),
                  part(0)] + [ANY_SPEC] * nr,
        out_specs=[part(0), part(0), part(0), part(0),
                   pl.BlockSpec((None, 8, HG_WIDTH), lambda b, g: (b, 0, 0)),
                   pl.BlockSpec((None, 8, LANES), lambda b, g: (b, 0, 0))] + [ANY_SPEC] * nr,
        out_shape=[SDS((B, T, HG_WIDTH), BF16)] * 4 + [SDS((B, 8, HG_WIDTH), F32), SDS((B, 8, LANES), F32)]
        + _exchange_shapes(ride_srcs, ride_modes),
        scratch_shapes=[pltpu.VMEM((HG_HEADS, HG_HEAD_DIM, HG_HEAD_DIM), F32)] + _exchange_sems(nr),
        compiler_params=_params(("arbitrary", "arbitrary"), VMEM_LIMIT_BIG),
    )(proj_h, proj_h, proj_h, proj_h, lb, hg_w, o, s_prev, drg, *ride_srcs)


def _attn_bwd(qr, kr, proj3, attn_o, dan, tables, sinks, attn_w, ride_srcs, ride_modes):
    B, T, _ = proj3.shape
    nb = T // WINDOW
    splits = min(ATT_SPLITS, nb)
    per = nb // splits
    nr = len(ride_srcs)
    cos, sinl, sinr = tables
    QKV = ATT_WIDTH + 2 * LANES

    def body(*refs):
        qr_ref, kr_ref, v_ref, o_ref, dan_ref, cos_ref, sl_ref, sr_ref, sink_ref, aw_ref = refs[:10]
        ride_in = refs[10:10 + nr]
        dqkv_ref, dsink_ref, daw_ref = refs[10 + nr:13 + nr]
        ride_out = refs[13 + nr:13 + 2 * nr]
        kpad, vpad, dkpad, dvpad, dqb, dsk = refs[13 + 2 * nr:19 + 2 * nr]
        sems = refs[19 + 2 * nr:]
        part = pl.program_id(1)
        step = pl.program_id(0) * splits + part
        _ride_start(ride_modes, step, B * splits, ride_in, ride_out, sems)

        @pl.when(part == 0)
        def _():
            kpad[0:WINDOW, :] = jnp.zeros((WINDOW, LANES), BF16)
            vpad[0:WINDOW, :] = jnp.zeros((WINDOW, LANES), BF16)
            kpad[WINDOW:, :] = kr_ref[...]
            vpad[WINDOW:, :] = _bf(v_ref[...])
            dkpad[...] = jnp.zeros(dkpad.shape, F32)
            dvpad[...] = jnp.zeros(dvpad.shape, F32)
            dsk[...] = jnp.zeros(dsk.shape, F32)
            daw_ref[...] = jnp.zeros(daw_ref.shape, F32)

        lower = _lower_mask()
        aw = aw_ref[...]

        def block(n, daw):
            r0 = pl.multiple_of(n * WINDOW, WINDOW)
            rows = pl.ds(r0, WINDOW)
            nxt = pl.ds(r0 + WINDOW, WINDOW)
            ob = o_ref[rows, :]
            dn = dan_ref[rows, :]
            ro = lax.rsqrt(_mean_last(ob * ob) + EPS)
            oh = ob * ro
            daw = daw + _sum_rows(dn * oh)
            doh = dn * aw
            do = _bf(ro * (doh - oh * _mean_last(doh * oh)))
            doparts = [do[:, j * LANES:(j + 1) * LANES] for j in range(ATT_WIDTH // LANES)]
            qparts = [qr_ref[rows, j * LANES:(j + 1) * LANES] for j in range(ATT_WIDTH // LANES)]
            for hk in range(ATT_KV_HEADS):
                lanes = slice(hk * ATT_HEAD_DIM, (hk + 1) * ATT_HEAD_DIM)
                qs = _stack_heads(qparts, hk)
                dos = _stack_heads(doparts, hk)
                k_cur, k_prev = kpad[nxt, lanes], kpad[rows, lanes]
                v_cur, v_prev = vpad[nxt, lanes], vpad[rows, lanes]
                p, inv, es = _softmax_window(qs, k_cur, k_prev, lower, n > 0, _sink_row(sink_ref, hk))
                p = p * inv
                dp = jnp.where(lower, _dot_nt(v_cur, dos), _dot_nt(v_prev, dos))
                delta = jnp.sum(p * dp, axis=0, keepdims=True)
                ds = p * (dp - delta)
                sk = (es * inv) * delta
                ds_cur = jnp.where(lower, ds, 0.0)
                p_cur = jnp.where(lower, p, 0.0)
                ds_cur, ds_prev = _bf(ds_cur), _bf(ds - ds_cur)
                p_cur, p_prev = _bf(p_cur), _bf(p - p_cur)
                dqt = (_dot_tn(k_cur, ds_cur) + _dot_tn(k_prev, ds_prev)) * ATT_SCALE
                dkpad[nxt, lanes] += _dot(ds_cur, qs)
                dkpad[rows, lanes] += _dot(ds_prev, qs)
                dvpad[nxt, lanes] += _dot(p_cur, dos)
                dvpad[rows, lanes] += _dot(p_prev, dos)
                for g in range(ATT_GROUP):
                    h = ATT_GROUP * hk + g
                    cols = slice(g * WINDOW, (g + 1) * WINDOW)
                    dqb[:, h * ATT_HEAD_DIM:(h + 1) * ATT_HEAD_DIM] = dqt[:, cols].T
                    head_lane = lax.broadcasted_iota(jnp.int32, dsk.shape, 1) == h
                    dsk[...] += jnp.where(head_lane, -jnp.sum(sk[:, cols], axis=1, keepdims=True), 0.0)
            cs, sl, sr = cos_ref[rows, :], sl_ref[rows, :], sr_ref[rows, :]
            for j in range(ATT_WIDTH // LANES):
                dqkv_ref[rows, j * LANES:(j + 1) * LANES] = _bf(_rope_t(dqb[:, j * LANES:(j + 1) * LANES], cs, sl, sr))
            return daw

        daw = _loop_pairs(part * per, per, block, jnp.zeros((1, ATT_WIDTH), F32))
        daw_ref[...] += jnp.broadcast_to(daw, (8, ATT_WIDTH))
        dsink_ref[...] = dsk[...]

        def finish(n, carry):
            r0 = pl.multiple_of(n * WINDOW, WINDOW)
            rows = pl.ds(r0, WINDOW)
            nxt = pl.ds(r0 + WINDOW, WINDOW)
            cs, sl, sr = cos_ref[rows, :], sl_ref[rows, :], sr_ref[rows, :]
            dqkv_ref[rows, ATT_WIDTH:ATT_WIDTH + LANES] = _bf(_rope_t(dkpad[nxt, :], cs, sl, sr))
            dqkv_ref[rows, ATT_WIDTH + LANES:QKV] = _bf(dvpad[nxt, :])
            return carry

        @pl.when(part == splits - 1)
        def _():
            lax.fori_loop(0, nb, finish, 0)

        _ride_wait(ride_modes, step, B * splits, ride_in, ride_out, sems)

    seq = lambda w, j: pl.BlockSpec((None, T, w), lambda b, s: (b, 0, j))
    full = lambda r, w: pl.BlockSpec((r, w), lambda b, s: (0, 0))
    return pl.pallas_call(
        body, name="attn_bwd", grid=(B, splits),
        in_specs=[seq(ATT_WIDTH, 0), seq(LANES, 0), seq(LANES, 5), seq(ATT_WIDTH, 0), seq(ATT_WIDTH, 0),
                  full(T, LANES), full(T, LANES), full(T, LANES),
                  pl.BlockSpec(memory_space=pltpu.SMEM), full(1, ATT_WIDTH)] + [ANY_SPEC] * nr,
        out_specs=[seq(QKV, 0), pl.BlockSpec((None, 8, LANES), lambda b, s: (b, 0, 0)),
                   pl.BlockSpec((None, 8, ATT_WIDTH), lambda b, s: (b, 0, 0))] + [ANY_SPEC] * nr,
        out_shape=[SDS((B, T, QKV), BF16), SDS((B, 8, LANES), F32), SDS((B, 8, ATT_WIDTH), F32)]
        + _exchange_shapes(ride_srcs, ride_modes),
        scratch_shapes=[pltpu.VMEM((T + WINDOW, LANES), BF16), pltpu.VMEM((T + WINDOW, LANES), BF16),
                        pltpu.VMEM((T + WINDOW, LANES), F32), pltpu.VMEM((T + WINDOW, LANES), F32),
                        pltpu.VMEM((WINDOW, ATT_WIDTH), F32), pltpu.VMEM((8, LANES), F32)] + _exchange_sems(nr),
        compiler_params=_params(("arbitrary", "arbitrary"), VMEM_LIMIT_BIG),
    )(qr, kr, proj3, attn_o, dan, cos, sinl, sinr, sinks, attn_w, *ride_srcs)


def _in_bwd(x2, dx1, dqkv, dhq, dhf, dhi, dhg, mod8, pre_w, w_in_bf, T, ride_srcs, ride_modes):
    N = x2.shape[0]
    TM = _tile_rows(T, big=True)
    tps = T // TM
    nr = len(ride_srcs)
    pieces = [(0, ATT_WIDTH + 2 * LANES), (768, HG_WIDTH), (1280, HG_WIDTH), (1792, HG_WIDTH), (2304, HG_WIDTH)]

    def body(*refs):
        x_ref, dx_ref, p0, p1, p2, p3, p4, mod_ref, pw_ref, w_ref = refs[:10]
        ride_in = refs[10:10 + nr]
        gx_ref, dproj_ref, acc_ref = refs[10 + nr:13 + nr]
        ride_out = refs[13 + nr:13 + 2 * nr]
        sems = refs[13 + 2 * nr:]
        _ride_start(ride_modes, pl.program_id(0), N // TM, ride_in, ride_out, sems)
        sc1 = mod_ref[1:2, :]
        dh = jnp.zeros((TM, D_MODEL), F32)
        for ref, (off, width) in zip((p0, p1, p2, p3, p4), pieces):
            pb = ref[...]
            dproj_ref[:, off:off + width] = pb
            dh = dh + _dot(pb, w_ref[off:off + width, :])
        x = x_ref[...]
        r = lax.rsqrt(_mean_last(x * x) + EPS)
        xh = x * r
        n1 = xh * pw_ref[...]
        dsh1 = _sum_rows(dh)
        dsc1 = _sum_rows(dh * n1)
        dn1 = dh * (1.0 + sc1)
        dw_pre = _sum_rows(dn1 * xh)
        dxh = dn1 * pw_ref[...]
        gx_ref[...] = dx_ref[...] + r * (dxh - xh * _mean_last(dxh * xh))
        _acc_rows(acc_ref, pl.program_id(0) % tps == 0, [dsh1, dsc1, dw_pre])
        _ride_wait(ride_modes, pl.program_id(0), N // TM, ride_in, ride_out, sems)

    row = lambda w: pl.BlockSpec((TM, w), lambda i: (i, 0))
    B = N // T
    return pl.pallas_call(
        body, name="in_bwd", grid=(N // TM,),
        in_specs=[row(D_MODEL), row(D_MODEL), row(768), row(HG_WIDTH), row(HG_WIDTH), row(HG_WIDTH),
                  row(HG_WIDTH), _mod_spec(tps), pl.BlockSpec((1, D_MODEL), lambda i: (0, 0)),
                  pl.BlockSpec((IN_COLS, D_MODEL), lambda i: (0, 0))] + [ANY_SPEC] * nr,
        out_specs=[row(D_MODEL), row(IN_COLS), _mod_spec(tps)] + [ANY_SPEC] * nr,
        out_shape=[SDS((N, D_MODEL), F32), SDS((N, IN_COLS), BF16), SDS((B, 8, D_MODEL), F32)]
        + _exchange_shapes(ride_srcs, ride_modes),
        scratch_shapes=_exchange_sems(nr),
        compiler_params=_params(("arbitrary",), VMEM_LIMIT_BIG),
    )(x2, dx1, dqkv, dhq, dhf, dhi, dhg, mod8, pre_w, w_in_bf, *ride_srcs)


def _matmul_tn(name, a, b, tn, tm):
    K, M = a.shape
    Nc = b.shape[1]

    def body(a_ref, b_ref, o_ref):
        o_ref[...] = _bf(_dot_tn(a_ref[...], b_ref[...]))

    return pl.pallas_call(
        body, name=name, grid=(M // tm, Nc // tn),
        in_specs=[pl.BlockSpec((K, tm), lambda i, j: (0, i)),
                  pl.BlockSpec((K, tn), lambda i, j: (0, j))],
        out_specs=pl.BlockSpec((tm, tn), lambda i, j: (i, j)), out_shape=SDS((M, Nc), BF16),
        compiler_params=_params(("arbitrary", "arbitrary"), VMEM_LIMIT_BIG),
    )(a, b)


def _matmul_tn_paired(name, a, b, stream_a):
    K = a.shape[0]
    stream, fixed = (a, b) if stream_a else (b, a)
    w = stream.shape[1] // N_DEV
    blk = (w, fixed.shape[1]) if stream_a else (fixed.shape[1], w)
    chips = N_DEV // 2

    def body(s_hbm, f_hbm, out_ref, s_buf, f_buf, g_buf, theirs, in_sems, send_sems, recv_sems):
        core = lax.axis_index("c")
        sib_dev, _ = _related(SIBLING)
        fixed_load = pltpu.make_async_copy(f_hbm, f_buf, in_sems.at[2])

        def load(j):
            owner = 2 * (j // 2) + (core if j % 2 else 1 - core)
            return pltpu.make_async_copy(s_hbm.at[:, pl.ds(pl.multiple_of(owner * w, LANES), w)], s_buf.at[j % 2],
                                         in_sems.at[j % 2])

        def swap(s):
            return pltpu.make_async_remote_copy(
                src_ref=g_buf.at[s, 0], dst_ref=theirs.at[s], send_sem=send_sems.at[s],
                recv_sem=recv_sems.at[s], device_id=sib_dev, device_id_type=MESH)

        fixed_load.start()
        load(0).start()
        fixed_load.wait()
        for j in range(N_DEV):
            s, mine = divmod(j, 2)
            load(j).wait()
            if j + 1 < N_DEV:
                load(j + 1).start()
            if stream_a:
                g_buf[s, mine] = _bf(_dot_tn(s_buf[j % 2], f_buf[...]))
            else:
                g_buf[s, mine] = _bf(_dot_tn(f_buf[...], s_buf[j % 2]))
            if mine:
                swap(s).wait_recv()
                out_ref[s] = _bf(g_buf[s, 1].astype(F32) + theirs[s].astype(F32))
            else:
                swap(s).start()
        for s in range(chips):
            swap(s).wait_send()

    return pl.pallas_call(
        body, name=name, in_specs=[ANY_SPEC] * 2, out_shape=SDS((chips,) + blk, BF16),
        scratch_shapes=[pltpu.VMEM((2, K, w), BF16), pltpu.VMEM(fixed.shape, BF16),
                        pltpu.VMEM((chips, 2) + blk, BF16), pltpu.VMEM((chips,) + blk, BF16),
                        pltpu.SemaphoreType.DMA((3,)), pltpu.SemaphoreType.DMA((chips,)),
                        pltpu.SemaphoreType.DMA((chips,))],
        compiler_params=_params(None, VMEM_LIMIT_BIG),
    )(stream, fixed)


GW_BLOCK = IN_COLS // N_DEV
GW_HALF = IN_COLS // 2


def _grad_w_in_reduced(dproj, h1, ride_srcs, ride_modes):
    K = dproj.shape[0]
    nr = len(ride_srcs)
    chips = N_DEV // 2
    tn = 512

    def body(*refs):
        a_hbm, b_hbm = refs[:2]
        ride_in, out, ride_out = refs[2:2 + nr], refs[2 + nr], refs[3 + nr:3 + 2 * nr]
        a_buf, b_buf, g_buf, theirs, p_buf, in_sems, pair_send, pair_recv, chip_send, chip_recv, own_sem = \
            refs[3 + 2 * nr:14 + 2 * nr]
        ride = _exchange_phases(ride_modes, ride_in, ride_out, *refs[14 + 2 * nr:]) if nr else ([], [], [])
        x, y, core = lax.axis_index("x"), lax.axis_index("y"), lax.axis_index("c")
        chip = 2 * x + y
        sib_dev, _ = _related(SIBLING)

        def remote(src, dst, send_sem, recv_sem, dev):
            return pltpu.make_async_remote_copy(src_ref=src, dst_ref=dst, send_sem=send_sem, recv_sem=recv_sem,
                                                device_id=dev, device_id_type=MESH)

        halves = [1 - x, x]
        loads = [pltpu.make_async_copy(b_hbm, b_buf, in_sems.at[0])]
        for t in range(2):
            col = pl.multiple_of(halves[t] * GW_HALF, LANES)
            loads.append(pltpu.make_async_copy(a_hbm.at[:, pl.ds(col, GW_HALF)], a_buf.at[t], in_sems.at[1 + t]))
        for cp in loads:
            cp.start()
        _run(ride[0])
        loads[0].wait()
        end = []
        for t in range(2):
            loads[1 + t].wait()
            if t == 1:
                _run(ride[1])
            for j in range(D_MODEL // tn):
                cols = pl.ds(j * tn, tn)
                res = _dot_tn(a_buf[t], b_buf[:, cols])
                for q in range(2):
                    for cc in range(2):
                        r0 = (2 * q + cc) * GW_BLOCK
                        g_buf[t, q, cc, :, cols] = _bf(res[r0:r0 + GW_BLOCK])
                swaps = [remote(g_buf.at[t, q, 1 - core, :, cols], theirs.at[t, q, :, cols],
                                pair_send.at[t, 2 * j + q], pair_recv.at[t, 2 * j + q], sib_dev) for q in range(2)]
                for cp in swaps:
                    cp.start()
                for cp in swaps:
                    cp.wait_recv()
                end += [cp.wait_send for cp in swaps]
                for q in range(2):
                    p_buf[t, q, :, cols] = _bf(g_buf[t, q, core, :, cols].astype(F32)
                                               + theirs[t, q, :, cols].astype(F32))
                for dy in range(2):
                    k = 4 * (1 - t) + 2 * dy
                    if k == 0:
                        own = pltpu.make_async_copy(p_buf.at[t, y, :, cols], out.at[chip, :, cols], own_sem.at[j])
                        own.start()
                        end.append(own.wait)
                        continue
                    dev, peer = _related(k)
                    sems = chip_send.at[k // 2, j], chip_recv.at[k // 2, j]
                    send = remote(p_buf.at[t, y ^ dy, :, cols], out.at[chip, :, cols], *sems, dev)
                    send.start()
                    end += [remote(p_buf.at[t, y ^ dy, :, cols], out.at[peer // 2, :, cols], *sems, dev).wait_recv,
                            send.wait_send]
        _run(ride[2])
        _run(end)

    return pl.pallas_call(
        body, name="grad_w_in",
        in_specs=[ANY_SPEC] * (2 + nr), out_specs=[ANY_SPEC] * (1 + nr),
        out_shape=[SDS((chips, GW_BLOCK, D_MODEL), BF16)] + _exchange_shapes(ride_srcs, ride_modes),
        scratch_shapes=[pltpu.VMEM((2, K, GW_HALF), BF16), pltpu.VMEM((K, D_MODEL), BF16),
                        pltpu.VMEM((2, 2, 2, GW_BLOCK, D_MODEL), BF16), pltpu.VMEM((2, 2, GW_BLOCK, D_MODEL), BF16),
                        pltpu.VMEM((2, 2, GW_BLOCK, D_MODEL), BF16), pltpu.SemaphoreType.DMA((3,)),
                        pltpu.SemaphoreType.DMA((2, 4)), pltpu.SemaphoreType.DMA((2, 4)),
                        pltpu.SemaphoreType.DMA((chips, 2)), pltpu.SemaphoreType.DMA((chips, 2)),
                        pltpu.SemaphoreType.DMA((2,))] + _exchange_sems(nr),
        compiler_params=_params(None, VMEM_LIMIT_BIG),
    )(dproj, h1, *ride_srcs)


def _adamw_math(w, g, m, v):
    m2 = ADAM_B1 * m + (1.0 - ADAM_B1) * g
    v2 = ADAM_B2 * v + (1.0 - ADAM_B2) * (g * g)
    m_hat = m2 / (1.0 - ADAM_B1 ** ADAM_STEP)
    v_hat = v2 / (1.0 - ADAM_B2 ** ADAM_STEP)
    delta = -ADAM_LR * (m_hat / (jnp.sqrt(v_hat) + ADAM_EPS) + ADAM_WD * w)
    return delta, m2, v2


def _reduce_adamw(name, sets, steps, ada=None):
    n = len(sets)
    n_in = 4 * n + (5 if ada else 0)

    def body(*refs):
        for k in range(n):
            p_ref, w_ref, m_ref, v_ref = refs[4 * k:4 * k + 4]
            g_ref, d_ref, m2_ref, v2_ref = refs[n_in + 4 * k:n_in + 4 * k + 4]
            g = p_ref[0].astype(F32)
            for s in range(1, p_ref.shape[0]):
                g = g + p_ref[s].astype(F32)
            g_ref[...] = g
            d_ref[...], m2_ref[...], v2_ref[...] = _adamw_math(w_ref[...], g, m_ref[...], v_ref[...])
        if ada:
            c_ref, dm_ref, w_ref, m_ref, v_ref = refs[4 * n:n_in]
            g_ref, d_ref, m2_ref, v2_ref = refs[n_in + 4 * n:n_in + 4 * n + 4]
            cv = c_ref[...]
            g = _dot_tn(cv * _sigmoid(cv), dm_ref[...])
            g_ref[...] = g
            d_ref[...], m2_ref[...], v2_ref[...] = _adamw_math(w_ref[...], g, m_ref[...], v_ref[...])

    in_specs, out_specs, out_shape, args = [], [], [], [a for st in sets for a in st]
    for parts, w, _, _ in sets:
        r, c = w.shape
        blk = pl.BlockSpec((r // steps, c), lambda i: (i, 0))
        in_specs += [pl.BlockSpec((parts.shape[0], r // steps, c), lambda i: (0, i, 0)), blk, blk, blk]
        out_specs += [blk] * 4
        out_shape += [SDS((r, c), F32)] * 4
    if ada:
        c_all, dmod_all, w = ada[:3]
        r, c = w.shape
        blk = pl.BlockSpec((r // steps, c), lambda i: (i, 0))
        in_specs += [pl.BlockSpec((c_all.shape[0], r // steps), lambda i: (0, i)),
                     pl.BlockSpec(dmod_all.shape, lambda i: (0, 0)), blk, blk, blk]
        out_specs += [blk] * 4
        out_shape += [SDS((r, c), F32)] * 4
        args += list(ada)
    outs = pl.pallas_call(
        body, name=name, grid=(steps,), in_specs=in_specs, out_specs=out_specs, out_shape=out_shape,
        compiler_params=_params(("arbitrary",), VMEM_LIMIT_BIG),
    )(*args)
    return [tuple(outs[4 * k:4 * k + 4]) for k in range(len(outs) // 4)]


_SMALL = [("b_ada", 6144), ("pre_w_mix", 1024), ("attn_sinks", 128), ("attn_out_w", 512), ("lb_table", 1024),
          ("hg_norm_w", 128), ("post_w_mix", 1024), ("pre_w_mlp", 1024), ("post_w_mlp", 1024)]


def _pack_small(acc_in, acc_mix, acc_mlp, dsink, daw, dlb, dgw, lb_p, ada_cols):
    B = acc_in.shape[0]
    width = sum(w for _, w in _SMALL) + LANES

    def body(ain, amix, amlp, dsk_ref, daw_ref, dlb_ref, dgw_ref, lbp_ref, packed_ref, dmod_ref):
        def total(ref, r, w=None):
            out = ref[0, r:r + 1, :] if w is None else ref[0, r:r + 1, :w]
            for b in range(1, B):
                out = out + (ref[b, r:r + 1, :] if w is None else ref[b, r:r + 1, :w])
            return out

        d_b_ada = None
        for b in range(B):
            mods = [ain[b, 0:1, :], ain[b, 1:2, :], amix[b, 0:1, :], amlp[b, 0:1, :], amlp[b, 1:2, :], amlp[b, 2:3, :]]
            full = jnp.concatenate(mods, axis=1)
            for j in range(N_DEV):
                dmod_ref[j, b:b + 1, :] = full[:, j * ada_cols:(j + 1) * ada_cols]
            d_b_ada = full if d_b_ada is None else d_b_ada + full
        d_lb = total(dlb_ref, 0)
        pp = lbp_ref[0:1, :] * lbp_ref[1:2, :]
        pieces = [d_b_ada, total(ain, 2), total(dsk_ref, 0), total(daw_ref, 0), -d_lb * pp, d_lb * pp,
                  total(dgw_ref, 0), total(amix, 1), total(amlp, 3), total(amlp, 4), total(amlp, 5, LANES)]
        off = 0
        for piece in pieces:
            packed_ref[:, off:off + piece.shape[1]] = piece
            off += piece.shape[1]

    return pl.pallas_call(
        body, name="pack_small",
        out_shape=[SDS((1, width), F32), SDS((N_DEV, B, ada_cols), F32)],
    )(acc_in, acc_mix, acc_mlp, dsink, daw, dlb, dgw, lb_p)


def _adamw_small(parts, given):
    names = [n for n, _ in _SMALL]
    flat_in = [a for n in names for a in given[n]]

    def body(*refs):
        p_ref = refs[0]
        in_refs = refs[1:1 + 3 * len(names)]
        out_refs = refs[1 + 3 * len(names):-1]
        loss_ref = refs[-1]
        g = p_ref[0]
        for s in range(1, N_DEV):
            g = g + p_ref[s]
        off = 0
        for i, (name, width) in enumerate(_SMALL):
            w_ref, m_ref, v_ref = in_refs[3 * i:3 * i + 3]
            rows, cols = w_ref.shape
            for r in range(rows):
                gr = g[:, off + r * cols:off + (r + 1) * cols]
                res = (gr,) + _adamw_math(w_ref[r:r + 1, :], gr, m_ref[r:r + 1, :], v_ref[r:r + 1, :])
                for o_ref, val in zip(out_refs[4 * i:4 * i + 4], res):
                    o_ref[r:r + 1, :] = val
            off += width
        loss_ref[...] = g[:, off:off + LANES]

    out_shape = [SDS(given[n][0].shape, F32) for n in names for _ in range(4)] + [SDS((1, LANES), F32)]
    outs = pl.pallas_call(body, name="adamw_small", out_shape=out_shape)(parts, *flat_in)
    return {n: tuple(outs[4 * i:4 * i + 4]) for i, n in enumerate(names)}, outs[-1][0, 0]


def kernel(x, c, w_ada, b_ada, pre_w_mix, w_in, attn_sinks, attn_out_w, lb_table, hg_norm_w, w_out, post_w_mix, pre_w_mlp, w_up, w_down, post_w_mlp, loss_target, m_w_ada, m_b_ada, m_pre_w_mix, m_w_in, m_attn_sinks, m_attn_out_w, m_lb_table, m_hg_norm_w, m_w_out, m_post_w_mix, m_pre_w_mlp, m_w_up, m_w_down, m_post_w_mlp, v_w_ada, v_b_ada, v_pre_w_mix, v_w_in, v_attn_sinks, v_attn_out_w, v_lb_table, v_hg_norm_w, v_w_out, v_post_w_mix, v_pre_w_mlp, v_w_up, v_w_down, v_post_w_mlp):
    B, T, _ = x.shape
    N = B * T
    me = 4 * lax.axis_index("x") + 2 * lax.axis_index("y") + lax.axis_index("c")
    x2 = x.reshape(N, D_MODEL)
    tgt2 = loss_target.reshape(N, D_MODEL)

    w_in_t, m_w_in_t, v_w_in_t = w_in[0].T, m_w_in[0].T, v_w_in[0].T
    w_in_g, c_g = _exchange("gather_w_in", [_bf(w_in_t), c], ["gather"] * 2)
    w_in_f = w_in_g.reshape(IN_COLS, D_MODEL)
    c_all = c_g.reshape(N_DEV * B, D_MODEL)

    ada_cols = w_ada.shape[2]
    b_mine = lax.dynamic_slice(b_ada, (0, me * ada_cols), (1, ada_cols))
    mod_cols = _ada_mod(c_all, w_ada[0], b_mine)
    (mod_g,) = _exchange("scatter_mod", [mod_cols.reshape(N_DEV, B, ada_cols)], ["a2a"])
    mod = mod_g.transpose(1, 0, 2).reshape(B, 6, D_MODEL)
    mod8 = jnp.pad(mod, ((0, 0), (0, 2), (0, 0)))

    lb_p = jax.nn.softmax(lb_table, axis=0)
    lb = lb_p[1:2]
    tables = _rope_tables(T)

    w_up_b, w_down_b = _bf(w_up[0]), _bf(w_down[0])
    wd_split = 5 * LANES
    proj_a, proj_h, h1, w_up_g0 = _in_proj(x2, mod8, pre_w_mix, w_in_f, T, [w_up_b[:MLP_HALF]], ["gather"])
    proj3 = proj_a.reshape(B, T, ATT_COLS)
    proj_h = proj_h.reshape(B, T, IN_COLS - ATT_COLS)
    rec_o, rec_g, s_prev, w_up_g1, w_out_g = _hgrn_fwd(proj_h, lb, hg_norm_w, [w_up_b[MLP_HALF:], _bf(w_out[0])],
                                                       ["gather"] * 2)
    attn_o, attn_n, qr, kr, w_down_g0 = _attn_fwd(proj3, tables, attn_sinks, attn_out_w,
                                                  [w_down_b[:, :wd_split]], ["gather"])
    w_out_f = w_out_g.reshape(D_MODEL, D_MODEL)
    mix, x1, cat, w_down_g1 = _mix_out(x2, attn_n.reshape(N, ATT_WIDTH), rec_g.reshape(N, HG_WIDTH), mod8,
                                       post_w_mix, w_out_f, T, [w_down_b[:, wd_split:]], ["gather"])
    w_up_halves = [w_up_g0, w_up_g1]
    w_down_halves = [w_down_g0.reshape(D_FF, wd_split), w_down_g1.reshape(D_FF, D_MODEL - wd_split)]
    up, u, d, h2 = _mlp_fwd(x1, mod8, pre_w_mlp, w_up_halves, w_down_halves, T)

    dx1, dup, dd, acc_mlp = _mlp_bwd(x1, d, up, tgt2, mod8, pre_w_mlp, post_w_mlp, w_up_halves, w_down_halves, T)
    p_up = _matmul_tn_paired("grad_w_up", h2, dup, stream_a=False)
    p_down = _matmul_tn_paired("grad_w_down", u, dd, stream_a=True)
    dan, drg, dmix, acc_mix = _mix_bwd(mix, dx1, mod8, post_w_mix, w_out_f, T, [], [])
    gw_out = _matmul_tn("grad_w_out", cat, dmix, 512, tm=D_MODEL).reshape(N_DEV, D_MODEL // N_DEV, D_MODEL)
    dhq, dhf, dhi, dhg, dlb_p, dgw_p, r_down, r_out = _hgrn_bwd(
        proj_h, lb, hg_norm_w, rec_o, s_prev, drg.reshape(B, T, HG_WIDTH), [p_down, gw_out], ["chips", "a2a"])
    dqkv, dsink_p, daw_p, r_up = _attn_bwd(qr, kr, proj3, attn_o, dan.reshape(B, T, ATT_WIDTH), tables,
                                           attn_sinks, attn_out_w, [p_up], ["chips"])
    flat = lambda a: a.reshape(N, a.shape[-1])
    grad_x, dproj, acc_in = _in_bwd(x2, dx1, flat(dqkv), flat(dhq), flat(dhf), flat(dhi), flat(dhg),
                                    mod8, pre_w_mix, w_in_f, T, [], [])

    packed, dmod_blocks = _pack_small(acc_in, acc_mix, acc_mlp, dsink_p, daw_p, dlb_p, dgw_p, lb_p, ada_cols)
    r_in, r_dmod, r_small = _grad_w_in_reduced(dproj, h1, [dmod_blocks, packed], ["a2a", "gather"])

    res = {}
    (got,) = _reduce_adamw("adamw_w_in", [(r_in, w_in_t, m_w_in_t, v_w_in_t)], 1)
    res["w_in"] = tuple(a.T for a in got)
    res["w_out"], res["w_up"], res["w_down"], res["w_ada"] = _reduce_adamw(
        "adamw_matrices", [(r_out, w_out[0], m_w_out[0], v_w_out[0]), (r_up, w_up[0], m_w_up[0], v_w_up[0]),
                           (r_down, w_down[0], m_w_down[0], v_w_down[0])], 4,
        ada=(c_all, r_dmod.reshape(N_DEV * B, ada_cols), w_ada[0], m_w_ada[0], v_w_ada[0]))

    given = dict(b_ada=(b_ada, m_b_ada, v_b_ada), pre_w_mix=(pre_w_mix, m_pre_w_mix, v_pre_w_mix),
                 attn_sinks=(attn_sinks, m_attn_sinks, v_attn_sinks),
                 attn_out_w=(attn_out_w, m_attn_out_w, v_attn_out_w), lb_table=(lb_table, m_lb_table, v_lb_table),
                 hg_norm_w=(hg_norm_w, m_hg_norm_w, v_hg_norm_w), post_w_mix=(post_w_mix, m_post_w_mix, v_post_w_mix),
                 pre_w_mlp=(pre_w_mlp, m_pre_w_mlp, v_pre_w_mlp), post_w_mlp=(post_w_mlp, m_post_w_mlp, v_post_w_mlp))
    small_res, loss = _adamw_small(r_small, given)
    res.update(small_res)

    order = ["w_ada", "b_ada", "pre_w_mix", "w_in", "attn_sinks", "attn_out_w", "lb_table", "hg_norm_w", "w_out",
             "post_w_mix", "pre_w_mlp", "w_up", "w_down", "post_w_mlp"]
    big = {"w_ada", "w_in", "w_out", "w_up", "w_down"}
    outs = [loss, grad_x.reshape(B, T, D_MODEL)]
    for i in range(4):
        for k in order:
            a = res[k][i]
            outs.append(a[None] if k in big else a)
    return tuple(outs)
```
